```python
import math
import jax, jax.numpy as jnp
from jax import lax
import numpy as np

D_MODEL = 1024
BATCH = 8
SEQ = 8192
DEPTH = 1

MEM_LEN = 256
HEAD_DIM = 64
MIX_W = D_MODEL
ATTN_W = MIX_W // 2
N_Q_HEADS = ATTN_W // HEAD_DIM
N_KV_HEADS = N_Q_HEADS // 4
GQA_GROUP = N_Q_HEADS // N_KV_HEADS
KV_W = N_KV_HEADS * HEAD_DIM
GM_W = MIX_W - ATTN_W
GM_HEADS = GM_W // HEAD_DIM
GM_DH = GM_W // GM_HEADS
IN_COLS = ATTN_W + 2 * KV_W + 2 * GM_W
WINDOW = 128
BLK = 128
CHUNK = 128
ROPE_THETA = 10000.0
XA_HEADS = 4
XA_DH = D_MODEL // XA_HEADS
D_FF = ((8 * D_MODEL // 3 + 127) // 128) * 128
CONV_W = 3
MAX_POS_OFFSET = 1024
EPS = 1e-6

kernel_name = "hybrid_swa_gmlp_xattn_convffn"


def rms_norm(x, g):
    xf = x.astype(jnp.float32)
    y = xf * lax.rsqrt(jnp.mean(xf * xf, axis=-1, keepdims=True) + EPS)
    return (y * g.astype(jnp.float32)).astype(x.dtype)


def rope(x, positions):
    dh = x.shape[-1]
    half = dh // 2
    inv_freq = 1.0 / (ROPE_THETA ** (jnp.arange(half, dtype=jnp.float32) * (2.0 / dh)))
    ang = positions.astype(jnp.float32)[..., None] * inv_freq
    cos = jnp.cos(ang)[:, :, None, :]
    sin = jnp.sin(ang)[:, :, None, :]
    xf = x.astype(jnp.float32)
    x1, x2 = xf[..., :half], xf[..., half:]
    out = jnp.concatenate([x1 * cos - x2 * sin, x2 * cos + x1 * sin], axis=-1)
    return out.astype(x.dtype)


def sliding_window_attn(q, k, v, sinks):
    B, S = q.shape[0], q.shape[1]
    nb = S // BLK
    qb = q.reshape(B, nb, BLK, N_KV_HEADS, GQA_GROUP, HEAD_DIM)
    kb = k.reshape(B, nb, BLK, N_KV_HEADS, HEAD_DIM)
    vb = v.reshape(B, nb, BLK, N_KV_HEADS, HEAD_DIM)
    pad = ((0, 0), (1, 0), (0, 0), (0, 0), (0, 0))
    kk = jnp.concatenate([jnp.pad(kb[:, :-1], pad), kb], axis=2)
    vv = jnp.concatenate([jnp.pad(vb[:, :-1], pad), vb], axis=2)
    scores = jnp.einsum('bnqhgd,bnkhd->bnhgqk', qb, kk).astype(jnp.float32)
    scores = scores * (1.0 / math.sqrt(HEAD_DIM))
    qi = jnp.arange(BLK)[:, None]
    kj = jnp.arange(2 * BLK)[None, :]
    diff = qi + BLK - kj
    band = (diff >= 0) & (diff < WINDOW)
    valid = (jnp.arange(nb)[:, None, None] > 0) | (kj >= BLK)[None]
    mask = (band[None] & valid)[None, :, None, None]
    scores = jnp.where(mask, scores, jnp.finfo(jnp.float32).min)
    sink = sinks.astype(jnp.float32).reshape(N_KV_HEADS, GQA_GROUP)[None, None, :, :, None, None]
    sink = jnp.broadcast_to(sink, scores.shape[:-1] + (1,))
    probs = jax.nn.softmax(jnp.concatenate([scores, sink], axis=-1), axis=-1)[..., :-1]
    out = jnp.einsum('bnhgqk,bnkhd->bnqhgd', probs.astype(v.dtype), vv)
    return out.reshape(B, S, ATTN_W)


def chunked_spatial_gating(u, v, ws, bs):
    B, S = v.shape[0], v.shape[1]
    nc = S // CHUNK
    vb = v.reshape(B, nc, CHUNK, GM_HEADS, GM_DH)
    causal = jnp.tril(jnp.ones((CHUNK, CHUNK), dtype=ws.dtype))
    mixed = jnp.einsum('hts,bnshd->bnthd', ws * causal[None], vb)
    mixed = mixed + bs.T[None, None, :, :, None]
    return u * mixed.reshape(B, S, GM_W)


def parallel_mixer(x, positions, mix_norm, w_in, q_norm, k_norm, attn_sinks,
                   gmlp_v_norm, gmlp_ws, gmlp_bs, attn_out_norm, gmlp_out_norm, w_out):
    B, S, _ = x.shape
    h = rms_norm(x, mix_norm)
    proj = h @ w_in
    q, k, v, gz = jnp.split(proj, [ATTN_W, ATTN_W + KV_W, ATTN_W + 2 * KV_W], axis=-1)
    q = rope(rms_norm(q.reshape(B, S, N_Q_HEADS, HEAD_DIM), q_norm), positions)
    k = rope(rms_norm(k.reshape(B, S, N_KV_HEADS, HEAD_DIM), k_norm), positions)
    v = v.reshape(B, S, N_KV_HEADS, HEAD_DIM)
    attn = sliding_window_attn(q, k, v, attn_sinks)
    gz = jax.nn.gelu(gz)
    gu, gv = jnp.split(gz, 2, axis=-1)
    gm = chunked_spatial_gating(gu, rms_norm(gv, gmlp_v_norm), gmlp_ws, gmlp_bs)
    y = jnp.concatenate([rms_norm(attn, attn_out_norm), rms_norm(gm, gmlp_out_norm)], axis=-1)
    return y @ w_out


def memory_cross_attn(x, mem, xa_norm, mem_norm, xa_wq, xa_wkv, xa_q_norm, xa_k_norm, xa_wo):
    B, S, _ = x.shape
    M = mem.shape[1]
    h = rms_norm(x, xa_norm)
    m = rms_norm(mem, mem_norm)
    q = rms_norm((h @ xa_wq).reshape(B, S, XA_HEADS, XA_DH), xa_q_norm)
    k, v = jnp.split(m @ xa_wkv, 2, axis=-1)
    k = rms_norm(k.reshape(B, M, XA_HEADS, XA_DH), xa_k_norm)
    v = v.reshape(B, M, XA_HEADS, XA_DH)
    scores = jnp.einsum('bshd,bmhd->bhsm', q, k).astype(jnp.float32) * (1.0 / math.sqrt(XA_DH))
    probs = jax.nn.softmax(scores, axis=-1).astype(v.dtype)
    out = jnp.einsum('bhsm,bmhd->bshd', probs, v).reshape(B, S, XA_HEADS * XA_DH)
    return out @ xa_wo


def conv_gated_ffn(x, ffn_norm, ffn_up, ffn_conv, ffn_conv_b, ffn_down):
    h = rms_norm(x, ffn_norm)
    a = h @ ffn_up
    c = lax.conv_general_dilated(
        a, ffn_conv.reshape(CONV_W, 1, 2 * D_FF).astype(a.dtype),
        window_strides=(1,), padding=[(CONV_W - 1, 0)],
        dimension_numbers=('NWC', 'WIO', 'NWC'),
        feature_group_count=2 * D_FF) + ffn_conv_b
    gate, up = jnp.split(c, 2, axis=-1)
    return (jax.nn.gelu(gate) * up) @ ffn_down


def _fwd_setup_inputs(seed: int = 0) -> dict:
    key = jax.random.key(seed)
    ks = iter(jax.random.split(key, 40))
    L = DEPTH

    def nrm(shape, scale):
        return jax.random.normal(next(ks), shape, jnp.float32) * scale

    def gain(shape):
        return 1.0 + 0.02 * jax.random.normal(next(ks), shape, jnp.float32)

    x = nrm((BATCH, SEQ, D_MODEL), 1.0)
    mem = nrm((BATCH, MEM_LEN, D_MODEL), 1.0)
    offs = jax.random.randint(next(ks), (BATCH, 1), 0, MAX_POS_OFFSET, dtype=jnp.int32)
    positions = (offs + jnp.arange(SEQ, dtype=jnp.int32)[None, :]).astype(jnp.int32)
    return {
        "x": x,
        "mem": mem,
        "positions": positions,
        "mix_norm": gain((L, D_MODEL)),
        "w_in": nrm((L, D_MODEL, IN_COLS), D_MODEL ** -0.5),
        "q_norm": gain((L, HEAD_DIM)),
        "k_norm": gain((L, HEAD_DIM)),
        "attn_sinks": nrm((L, N_Q_HEADS), 0.5),
        "gmlp_v_norm": gain((L, GM_W)),
        "gmlp_ws": nrm((L, GM_HEADS, CHUNK, CHUNK), 0.5 * CHUNK ** -0.5),
        "gmlp_bs": 1.0 + nrm((L, GM_HEADS, CHUNK), 0.02),
        "attn_out_norm": gain((L, ATTN_W)),
        "gmlp_out_norm": gain((L, GM_W)),
        "w_out": nrm((L, MIX_W, D_MODEL), MIX_W ** -0.5),
        "xa_norm": gain((L, D_MODEL)),
        "mem_norm": gain((L, D_MODEL)),
        "xa_wq": nrm((L, D_MODEL, XA_HEADS * XA_DH), D_MODEL ** -0.5),
        "xa_wkv": nrm((L, D_MODEL, 2 * XA_HEADS * XA_DH), D_MODEL ** -0.5),
        "xa_q_norm": gain((L, XA_DH)),
        "xa_k_norm": gain((L, XA_DH)),
        "xa_wo": nrm((L, XA_HEADS * XA_DH, D_MODEL), (XA_HEADS * XA_DH) ** -0.5),
        "ffn_norm": gain((L, D_MODEL)),
        "ffn_up": nrm((L, D_MODEL, 2 * D_FF), D_MODEL ** -0.5),
        "ffn_conv": nrm((L, CONV_W, 2 * D_FF), CONV_W ** -0.5),
        "ffn_conv_b": nrm((L, 2 * D_FF), 0.02),
        "ffn_down": nrm((L, D_FF, D_MODEL), D_FF ** -0.5),
    }


def _fwd_reference(x, mem, positions, mix_norm, w_in, q_norm, k_norm, attn_sinks,
              gmlp_v_norm, gmlp_ws, gmlp_bs, attn_out_norm, gmlp_out_norm, w_out,
              xa_norm, mem_norm, xa_wq, xa_wkv, xa_q_norm, xa_k_norm, xa_wo,
              ffn_norm, ffn_up, ffn_conv, ffn_conv_b, ffn_down):
    for l in range(DEPTH):
        x = x + parallel_mixer(x, positions, mix_norm[l], w_in[l], q_norm[l], k_norm[l],
                               attn_sinks[l], gmlp_v_norm[l], gmlp_ws[l], gmlp_bs[l],
                               attn_out_norm[l], gmlp_out_norm[l], w_out[l])
        x = x + memory_cross_attn(x, mem, xa_norm[l], mem_norm[l], xa_wq[l], xa_wkv[l],
                                  xa_q_norm[l], xa_k_norm[l], xa_wo[l])
        x = x + conv_gated_ffn(x, ffn_norm[l], ffn_up[l], ffn_conv[l], ffn_conv_b[l],
                               ffn_down[l])
    return x


import jax as _jax
import jax.numpy as _jnp

TWIN_FORMAT = 'train_step'
FWD_PARAMS = ['x', 'mem', 'positions', 'mix_norm', 'w_in', 'q_norm', 'k_norm', 'attn_sinks', 'gmlp_v_norm', 'gmlp_ws', 'gmlp_bs', 'attn_out_norm', 'gmlp_out_norm', 'w_out', 'xa_norm', 'mem_norm', 'xa_wq', 'xa_wkv', 'xa_q_norm', 'xa_k_norm', 'xa_wo', 'ffn_norm', 'ffn_up', 'ffn_conv', 'ffn_conv_b', 'ffn_down']
TWIN_WEIGHTS = ['mix_norm', 'w_in', 'q_norm', 'k_norm', 'attn_sinks', 'gmlp_v_norm', 'gmlp_ws', 'gmlp_bs', 'attn_out_norm', 'gmlp_out_norm', 'w_out', 'xa_norm', 'mem_norm', 'xa_wq', 'xa_wkv', 'xa_q_norm', 'xa_k_norm', 'xa_wo', 'ffn_norm', 'ffn_up', 'ffn_conv', 'ffn_conv_b', 'ffn_down']
TWIN_DIFF_INPUT = 'x'
TWIN_INPUTS = ['x', 'mem', 'positions', 'mix_norm', 'w_in', 'q_norm', 'k_norm', 'attn_sinks', 'gmlp_v_norm', 'gmlp_ws', 'gmlp_bs', 'attn_out_norm', 'gmlp_out_norm', 'w_out', 'xa_norm', 'mem_norm', 'xa_wq', 'xa_wkv', 'xa_q_norm', 'xa_k_norm', 'xa_wo', 'ffn_norm', 'ffn_up', 'ffn_conv', 'ffn_conv_b', 'ffn_down', 'loss_target', 'm_mix_norm', 'm_w_in', 'm_q_norm', 'm_k_norm', 'm_attn_sinks', 'm_gmlp_v_norm', 'm_gmlp_ws', 'm_gmlp_bs', 'm_attn_out_norm', 'm_gmlp_out_norm', 'm_w_out', 'm_xa_norm', 'm_mem_norm', 'm_xa_wq', 'm_xa_wkv', 'm_xa_q_norm', 'm_xa_k_norm', 'm_xa_wo', 'm_ffn_norm', 'm_ffn_up', 'm_ffn_conv', 'm_ffn_conv_b', 'm_ffn_down', 'v_mix_norm', 'v_w_in', 'v_q_norm', 'v_k_norm', 'v_attn_sinks', 'v_gmlp_v_norm', 'v_gmlp_ws', 'v_gmlp_bs', 'v_attn_out_norm', 'v_gmlp_out_norm', 'v_w_out', 'v_xa_norm', 'v_mem_norm', 'v_xa_wq', 'v_xa_wkv', 'v_xa_q_norm', 'v_xa_k_norm', 'v_xa_wo', 'v_ffn_norm', 'v_ffn_up', 'v_ffn_conv', 'v_ffn_conv_b', 'v_ffn_down']
TWIN_OUTPUTS = ['loss', 'grad_x', 'grad_mix_norm', 'grad_w_in', 'grad_q_norm', 'grad_k_norm', 'grad_attn_sinks', 'grad_gmlp_v_norm', 'grad_gmlp_ws', 'grad_gmlp_bs', 'grad_attn_out_norm', 'grad_gmlp_out_norm', 'grad_w_out', 'grad_xa_norm', 'grad_mem_norm', 'grad_xa_wq', 'grad_xa_wkv', 'grad_xa_q_norm', 'grad_xa_k_norm', 'grad_xa_wo', 'grad_ffn_norm', 'grad_ffn_up', 'grad_ffn_conv', 'grad_ffn_conv_b', 'grad_ffn_down', 'delta_mix_norm', 'delta_w_in', 'delta_q_norm', 'delta_k_norm', 'delta_attn_sinks', 'delta_gmlp_v_norm', 'delta_gmlp_ws', 'delta_gmlp_bs', 'delta_attn_out_norm', 'delta_gmlp_out_norm', 'delta_w_out', 'delta_xa_norm', 'delta_mem_norm', 'delta_xa_wq', 'delta_xa_wkv', 'delta_xa_q_norm', 'delta_xa_k_norm', 'delta_xa_wo', 'delta_ffn_norm', 'delta_ffn_up', 'delta_ffn_conv', 'delta_ffn_conv_b', 'delta_ffn_down', 'new_m_mix_norm', 'new_m_w_in', 'new_m_q_norm', 'new_m_k_norm', 'new_m_attn_sinks', 'new_m_gmlp_v_norm', 'new_m_gmlp_ws', 'new_m_gmlp_bs', 'new_m_attn_out_norm', 'new_m_gmlp_out_norm', 'new_m_w_out', 'new_m_xa_norm', 'new_m_mem_norm', 'new_m_xa_wq', 'new_m_xa_wkv', 'new_m_xa_q_norm', 'new_m_xa_k_norm', 'new_m_xa_wo', 'new_m_ffn_norm', 'new_m_ffn_up', 'new_m_ffn_conv', 'new_m_ffn_conv_b', 'new_m_ffn_down', 'new_v_mix_norm', 'new_v_w_in', 'new_v_q_norm', 'new_v_k_norm', 'new_v_attn_sinks', 'new_v_gmlp_v_norm', 'new_v_gmlp_ws', 'new_v_gmlp_bs', 'new_v_attn_out_norm', 'new_v_gmlp_out_norm', 'new_v_w_out', 'new_v_xa_norm', 'new_v_mem_norm', 'new_v_xa_wq', 'new_v_xa_wkv', 'new_v_xa_q_norm', 'new_v_xa_k_norm', 'new_v_xa_wo', 'new_v_ffn_norm', 'new_v_ffn_up', 'new_v_ffn_conv', 'new_v_ffn_conv_b', 'new_v_ffn_down']
TWIN_LEAF_KINDS = {'loss': 'loss', 'grad_x': 'grad_x', 'grad_mix_norm': 'grad_w', 'grad_w_in': 'grad_w', 'grad_q_norm': 'grad_w', 'grad_k_norm': 'grad_w', 'grad_attn_sinks': 'grad_w', 'grad_gmlp_v_norm': 'grad_w', 'grad_gmlp_ws': 'grad_w', 'grad_gmlp_bs': 'grad_w', 'grad_attn_out_norm': 'grad_w', 'grad_gmlp_out_norm': 'grad_w', 'grad_w_out': 'grad_w', 'grad_xa_norm': 'grad_w', 'grad_mem_norm': 'grad_w', 'grad_xa_wq': 'grad_w', 'grad_xa_wkv': 'grad_w', 'grad_xa_q_norm': 'grad_w', 'grad_xa_k_norm': 'grad_w', 'grad_xa_wo': 'grad_w', 'grad_ffn_norm': 'grad_w', 'grad_ffn_up': 'grad_w', 'grad_ffn_conv': 'grad_w', 'grad_ffn_conv_b': 'grad_w', 'grad_ffn_down': 'grad_w', 'delta_mix_norm': 'delta_w', 'delta_w_in': 'delta_w', 'delta_q_norm': 'delta_w', 'delta_k_norm': 'delta_w', 'delta_attn_sinks': 'delta_w', 'delta_gmlp_v_norm': 'delta_w', 'delta_gmlp_ws': 'delta_w', 'delta_gmlp_bs': 'delta_w', 'delta_attn_out_norm': 'delta_w', 'delta_gmlp_out_norm': 'delta_w', 'delta_w_out': 'delta_w', 'delta_xa_norm': 'delta_w', 'delta_mem_norm': 'delta_w', 'delta_xa_wq': 'delta_w', 'delta_xa_wkv': 'delta_w', 'delta_xa_q_norm': 'delta_w', 'delta_xa_k_norm': 'delta_w', 'delta_xa_wo': 'delta_w', 'delta_ffn_norm': 'delta_w', 'delta_ffn_up': 'delta_w', 'delta_ffn_conv': 'delta_w', 'delta_ffn_conv_b': 'delta_w', 'delta_ffn_down': 'delta_w', 'new_m_mix_norm': 'new_m', 'new_m_w_in': 'new_m', 'new_m_q_norm': 'new_m', 'new_m_k_norm': 'new_m', 'new_m_attn_sinks': 'new_m', 'new_m_gmlp_v_norm': 'new_m', 'new_m_gmlp_ws': 'new_m', 'new_m_gmlp_bs': 'new_m', 'new_m_attn_out_norm': 'new_m', 'new_m_gmlp_out_norm': 'new_m', 'new_m_w_out': 'new_m', 'new_m_xa_norm': 'new_m', 'new_m_mem_norm': 'new_m', 'new_m_xa_wq': 'new_m', 'new_m_xa_wkv': 'new_m', 'new_m_xa_q_norm': 'new_m', 'new_m_xa_k_norm': 'new_m', 'new_m_xa_wo': 'new_m', 'new_m_ffn_norm': 'new_m', 'new_m_ffn_up': 'new_m', 'new_m_ffn_conv': 'new_m', 'new_m_ffn_conv_b': 'new_m', 'new_m_ffn_down': 'new_m', 'new_v_mix_norm': 'new_v', 'new_v_w_in': 'new_v', 'new_v_q_norm': 'new_v', 'new_v_k_norm': 'new_v', 'new_v_attn_sinks': 'new_v', 'new_v_gmlp_v_norm': 'new_v', 'new_v_gmlp_ws': 'new_v', 'new_v_gmlp_bs': 'new_v', 'new_v_attn_out_norm': 'new_v', 'new_v_gmlp_out_norm': 'new_v', 'new_v_w_out': 'new_v', 'new_v_xa_norm': 'new_v', 'new_v_mem_norm': 'new_v', 'new_v_xa_wq': 'new_v', 'new_v_xa_wkv': 'new_v', 'new_v_xa_q_norm': 'new_v', 'new_v_xa_k_norm': 'new_v', 'new_v_xa_wo': 'new_v', 'new_v_ffn_norm': 'new_v', 'new_v_ffn_up': 'new_v', 'new_v_ffn_conv': 'new_v', 'new_v_ffn_conv_b': 'new_v', 'new_v_ffn_down': 'new_v'}


def _forward(args):
    return _fwd_reference(*[args[k] for k in FWD_PARAMS])


def _output_shape():
    def fwd():
        inp = _fwd_setup_inputs(0)
        return _fwd_reference(*[inp[k] for k in FWD_PARAMS])
    out = _jax.eval_shape(fwd)
    return out.shape, out.dtype

N_MICROBATCH = 1
ADAM_LR = 0.001
ADAM_B1 = 0.9
ADAM_B2 = 0.999
ADAM_EPS = 1e-08
ADAM_WD = 0.01
ADAM_STEP = 10
PER_EXAMPLE_BATCH_AXIS = {'x': 0, 'mem': 0, 'positions': 0, 'loss_target': 0}
SHARED_INPUTS = []
_WEIGHT_DTYPES = {'mix_norm': _jnp.float32, 'w_in': _jnp.float32, 'q_norm': _jnp.float32, 'k_norm': _jnp.float32, 'attn_sinks': _jnp.float32, 'gmlp_v_norm': _jnp.float32, 'gmlp_ws': _jnp.float32, 'gmlp_bs': _jnp.float32, 'attn_out_norm': _jnp.float32, 'gmlp_out_norm': _jnp.float32, 'w_out': _jnp.float32, 'xa_norm': _jnp.float32, 'mem_norm': _jnp.float32, 'xa_wq': _jnp.float32, 'xa_wkv': _jnp.float32, 'xa_q_norm': _jnp.float32, 'xa_k_norm': _jnp.float32, 'xa_wo': _jnp.float32, 'ffn_norm': _jnp.float32, 'ffn_up': _jnp.float32, 'ffn_conv': _jnp.float32, 'ffn_conv_b': _jnp.float32, 'ffn_down': _jnp.float32}
MOMENT_SCALE = {'mix_norm': 1.678877e+00, 'w_in': 1.158848e+00, 'q_norm': 3.138747e+00, 'k_norm': 2.275568e+00, 'attn_sinks': 1.619829e-01, 'gmlp_v_norm': 3.180885e-01, 'gmlp_ws': 3.453446e-01, 'gmlp_bs': 7.047881e-01, 'attn_out_norm': 6.355536e+01, 'gmlp_out_norm': 6.576577e+01, 'w_out': 7.993167e+00, 'xa_norm': 2.149836e-01, 'mem_norm': 1.023248e+00, 'xa_wq': 2.141200e-01, 'xa_wkv': 4.448371e-01, 'xa_q_norm': 2.458637e+00, 'xa_k_norm': 2.454371e+00, 'xa_wo': 5.886054e-01, 'ffn_norm': 6.361336e+01, 'ffn_up': 2.570609e+00, 'ffn_conv': 9.814037e+00, 'ffn_conv_b': 9.036268e+00, 'ffn_down': 1.475815e+00}


def _to_microbatches(a, axis):
    t = _jnp.moveaxis(a, axis, 0)
    t = t.reshape((N_MICROBATCH, t.shape[0] // N_MICROBATCH) + t.shape[1:])
    return _jnp.moveaxis(t, 1, axis + 1)


def setup_inputs(seed: int = 0) -> dict:
    inp = _fwd_setup_inputs(seed)
    key = _jax.random.fold_in(_jax.random.key(seed), 7919)
    shape, _ = _output_shape()
    out = dict(inp)
    out["loss_target"] = _jax.random.normal(_jax.random.fold_in(key, 0), shape, _jnp.float32)
    for i, name in enumerate(TWIN_WEIGHTS):
        w = inp[name].astype(_jnp.float32)
        if MOMENT_SCALE is None:
            s = _jnp.sqrt(_jnp.mean(_jnp.square(w)) + 1e-30)
        else:
            s = MOMENT_SCALE[name]
        km, kv = _jax.random.split(_jax.random.fold_in(key, i + 1))
        out[name] = w
        out["m_" + name] = s * _jax.random.normal(km, w.shape, _jnp.float32)
        out["v_" + name] = (s * s) * _jax.random.uniform(kv, w.shape, _jnp.float32, 0.5, 1.5)
    if N_MICROBATCH > 1:
        for name, axis in PER_EXAMPLE_BATCH_AXIS.items():
            out[name] = _to_microbatches(out[name], axis)
    return {'x': out['x'], 'mem': out['mem'], 'positions': out['positions'], 'mix_norm': out['mix_norm'], 'w_in': out['w_in'], 'q_norm': out['q_norm'], 'k_norm': out['k_norm'], 'attn_sinks': out['attn_sinks'], 'gmlp_v_norm': out['gmlp_v_norm'], 'gmlp_ws': out['gmlp_ws'], 'gmlp_bs': out['gmlp_bs'], 'attn_out_norm': out['attn_out_norm'], 'gmlp_out_norm': out['gmlp_out_norm'], 'w_out': out['w_out'], 'xa_norm': out['xa_norm'], 'mem_norm': out['mem_norm'], 'xa_wq': out['xa_wq'], 'xa_wkv': out['xa_wkv'], 'xa_q_norm': out['xa_q_norm'], 'xa_k_norm': out['xa_k_norm'], 'xa_wo': out['xa_wo'], 'ffn_norm': out['ffn_norm'], 'ffn_up': out['ffn_up'], 'ffn_conv': out['ffn_conv'], 'ffn_conv_b': out['ffn_conv_b'], 'ffn_down': out['ffn_down'], 'loss_target': out['loss_target'], 'm_mix_norm': out['m_mix_norm'], 'm_w_in': out['m_w_in'], 'm_q_norm': out['m_q_norm'], 'm_k_norm': out['m_k_norm'], 'm_attn_sinks': out['m_attn_sinks'], 'm_gmlp_v_norm': out['m_gmlp_v_norm'], 'm_gmlp_ws': out['m_gmlp_ws'], 'm_gmlp_bs': out['m_gmlp_bs'], 'm_attn_out_norm': out['m_attn_out_norm'], 'm_gmlp_out_norm': out['m_gmlp_out_norm'], 'm_w_out': out['m_w_out'], 'm_xa_norm': out['m_xa_norm'], 'm_mem_norm': out['m_mem_norm'], 'm_xa_wq': out['m_xa_wq'], 'm_xa_wkv': out['m_xa_wkv'], 'm_xa_q_norm': out['m_xa_q_norm'], 'm_xa_k_norm': out['m_xa_k_norm'], 'm_xa_wo': out['m_xa_wo'], 'm_ffn_norm': out['m_ffn_norm'], 'm_ffn_up': out['m_ffn_up'], 'm_ffn_conv': out['m_ffn_conv'], 'm_ffn_conv_b': out['m_ffn_conv_b'], 'm_ffn_down': out['m_ffn_down'], 'v_mix_norm': out['v_mix_norm'], 'v_w_in': out['v_w_in'], 'v_q_norm': out['v_q_norm'], 'v_k_norm': out['v_k_norm'], 'v_attn_sinks': out['v_attn_sinks'], 'v_gmlp_v_norm': out['v_gmlp_v_norm'], 'v_gmlp_ws': out['v_gmlp_ws'], 'v_gmlp_bs': out['v_gmlp_bs'], 'v_attn_out_norm': out['v_attn_out_norm'], 'v_gmlp_out_norm': out['v_gmlp_out_norm'], 'v_w_out': out['v_w_out'], 'v_xa_norm': out['v_xa_norm'], 'v_mem_norm': out['v_mem_norm'], 'v_xa_wq': out['v_xa_wq'], 'v_xa_wkv': out['v_xa_wkv'], 'v_xa_q_norm': out['v_xa_q_norm'], 'v_xa_k_norm': out['v_xa_k_norm'], 'v_xa_wo': out['v_xa_wo'], 'v_ffn_norm': out['v_ffn_norm'], 'v_ffn_up': out['v_ffn_up'], 'v_ffn_conv': out['v_ffn_conv'], 'v_ffn_conv_b': out['v_ffn_conv_b'], 'v_ffn_down': out['v_ffn_down']}


def _loss(weights, diff, rest, loss_target):
    with _jax.named_scope("forward"):
        args = {**rest, TWIN_DIFF_INPUT: diff, **{k: w.astype(_WEIGHT_DTYPES[k]) for k, w in weights.items()}}
        y = _forward(args)
    with _jax.named_scope("loss_head"):
        err = _jnp.square(y.astype(_jnp.float32) - loss_target)
        return 0.5 * _jnp.sum(_jnp.mean(err, axis=-1)) if err.ndim else 0.5 * err


def _adamw(w, g, m, v):
    m = ADAM_B1 * m + (1.0 - ADAM_B1) * g
    v = ADAM_B2 * v + (1.0 - ADAM_B2) * _jnp.square(g)
    m_hat = m / (1.0 - ADAM_B1 ** ADAM_STEP)
    v_hat = v / (1.0 - ADAM_B2 ** ADAM_STEP)
    delta = -ADAM_LR * (m_hat / (_jnp.sqrt(v_hat) + ADAM_EPS) + ADAM_WD * w)
    return delta, m, v


def reference(x, mem, positions, mix_norm, w_in, q_norm, k_norm, attn_sinks, gmlp_v_norm, gmlp_ws, gmlp_bs, attn_out_norm, gmlp_out_norm, w_out, xa_norm, mem_norm, xa_wq, xa_wkv, xa_q_norm, xa_k_norm, xa_wo, ffn_norm, ffn_up, ffn_conv, ffn_conv_b, ffn_down, loss_target, m_mix_norm, m_w_in, m_q_norm, m_k_norm, m_attn_sinks, m_gmlp_v_norm, m_gmlp_ws, m_gmlp_bs, m_attn_out_norm, m_gmlp_out_norm, m_w_out, m_xa_norm, m_mem_norm, m_xa_wq, m_xa_wkv, m_xa_q_norm, m_xa_k_norm, m_xa_wo, m_ffn_norm, m_ffn_up, m_ffn_conv, m_ffn_conv_b, m_ffn_down, v_mix_norm, v_w_in, v_q_norm, v_k_norm, v_attn_sinks, v_gmlp_v_norm, v_gmlp_ws, v_gmlp_bs, v_attn_out_norm, v_gmlp_out_norm, v_w_out, v_xa_norm, v_mem_norm, v_xa_wq, v_xa_wkv, v_xa_q_norm, v_xa_k_norm, v_xa_wo, v_ffn_norm, v_ffn_up, v_ffn_conv, v_ffn_conv_b, v_ffn_down):
    given = dict(x=x, mem=mem, positions=positions, mix_norm=mix_norm, w_in=w_in, q_norm=q_norm, k_norm=k_norm, attn_sinks=attn_sinks, gmlp_v_norm=gmlp_v_norm, gmlp_ws=gmlp_ws, gmlp_bs=gmlp_bs, attn_out_norm=attn_out_norm, gmlp_out_norm=gmlp_out_norm, w_out=w_out, xa_norm=xa_norm, mem_norm=mem_norm, xa_wq=xa_wq, xa_wkv=xa_wkv, xa_q_norm=xa_q_norm, xa_k_norm=xa_k_norm, xa_wo=xa_wo, ffn_norm=ffn_norm, ffn_up=ffn_up, ffn_conv=ffn_conv, ffn_conv_b=ffn_conv_b, ffn_down=ffn_down, loss_target=loss_target, m_mix_norm=m_mix_norm, m_w_in=m_w_in, m_q_norm=m_q_norm, m_k_norm=m_k_norm, m_attn_sinks=m_attn_sinks, m_gmlp_v_norm=m_gmlp_v_norm, m_gmlp_ws=m_gmlp_ws, m_gmlp_bs=m_gmlp_bs, m_attn_out_norm=m_attn_out_norm, m_gmlp_out_norm=m_gmlp_out_norm, m_w_out=m_w_out, m_xa_norm=m_xa_norm, m_mem_norm=m_mem_norm, m_xa_wq=m_xa_wq, m_xa_wkv=m_xa_wkv, m_xa_q_norm=m_xa_q_norm, m_xa_k_norm=m_xa_k_norm, m_xa_wo=m_xa_wo, m_ffn_norm=m_ffn_norm, m_ffn_up=m_ffn_up, m_ffn_conv=m_ffn_conv, m_ffn_conv_b=m_ffn_conv_b, m_ffn_down=m_ffn_down, v_mix_norm=v_mix_norm, v_w_in=v_w_in, v_q_norm=v_q_norm, v_k_norm=v_k_norm, v_attn_sinks=v_attn_sinks, v_gmlp_v_norm=v_gmlp_v_norm, v_gmlp_ws=v_gmlp_ws, v_gmlp_bs=v_gmlp_bs, v_attn_out_norm=v_attn_out_norm, v_gmlp_out_norm=v_gmlp_out_norm, v_w_out=v_w_out, v_xa_norm=v_xa_norm, v_mem_norm=v_mem_norm, v_xa_wq=v_xa_wq, v_xa_wkv=v_xa_wkv, v_xa_q_norm=v_xa_q_norm, v_xa_k_norm=v_xa_k_norm, v_xa_wo=v_xa_wo, v_ffn_norm=v_ffn_norm, v_ffn_up=v_ffn_up, v_ffn_conv=v_ffn_conv, v_ffn_conv_b=v_ffn_conv_b, v_ffn_down=v_ffn_down)
    weights = {n: given[n] for n in TWIN_WEIGHTS}
    shared = {n: given[n] for n in SHARED_INPUTS}
    per_example = {n: given[n] for n in ['x', 'mem', 'positions']}
    grad_fn = _jax.value_and_grad(_loss, argnums=(0, 1))

    def one_microbatch(ex, loss_target):
        ex = dict(ex)
        diff = ex.pop(TWIN_DIFF_INPUT)
        return grad_fn(weights, diff, {**shared, **ex}, loss_target)

    if N_MICROBATCH == 1:
        loss, (grad_w, grad_x) = one_microbatch(per_example, given["loss_target"])
    else:
        def body(carry, xs):
            loss_sum, grad_sum = carry
            l_k, (gw_k, gx_k) = one_microbatch(xs[0], xs[1])
            with _jax.named_scope("update"):
                return (loss_sum + l_k, _jax.tree.map(_jnp.add, grad_sum, gw_k)), gx_k

        init = (_jnp.zeros((), _jnp.float32), _jax.tree.map(_jnp.zeros_like, weights))
        (loss, grad_w), grad_x = _jax.lax.scan(body, init, (per_example, given["loss_target"]))
    with _jax.named_scope("update"):
        delta_w, new_m, new_v = {}, {}, {}
        for n in TWIN_WEIGHTS:
            delta_w[n], new_m[n], new_v[n] = _adamw(weights[n], grad_w[n], given["m_" + n], given["v_" + n])
    return (loss, grad_x, *[grad_w[n] for n in TWIN_WEIGHTS], *[delta_w[n] for n in TWIN_WEIGHTS],
            *[new_m[n] for n in TWIN_WEIGHTS], *[new_v[n] for n in TWIN_WEIGHTS])
```

```python
import functools
import math

import jax
import jax.numpy as jnp
from jax import lax
from jax.experimental import pallas as pl
from jax.experimental.pallas import tpu as pltpu

F32 = jnp.float32
BF16 = jnp.bfloat16
MXU_DTYPE = jnp.bfloat16
WIRE_DTYPE = jnp.bfloat16
EPS = 1e-6
VMEM_LIMIT_V7X = 56 * 1024 * 1024

D_MODEL = 1024
HEAD_DIM = 64
ATTN_W = 512
GM_W = 512
BLK = 128
XA_HEADS = 4
XA_DH = 256
MEM_LEN = 256
D_FF = 2816
IN_COLS = 1792
IN_COLS_DUP = 2048
N_CHIPS = 4
N_DEV = 8
PACK_W = 1024
PACK_ROWS = 3840
HALF_ROWS = PACK_ROWS // 2

ADAM_LR = 0.001
ADAM_B1 = 0.9
ADAM_B2 = 0.999
ADAM_EPS = 1e-08
ADAM_WD = 0.01
ADAM_STEP = 10

NT = (((1,), (1,)), ((), ()))
TN = (((0,), (0,)), ((), ()))
NN = (((1,), (0,)), ((), ()))
MINF = float(jnp.finfo(jnp.float32).min)
GELU_K0 = math.sqrt(2.0 / math.pi)
GELU_K1 = 0.044715

BS = pl.BlockSpec
SDS = jax.ShapeDtypeStruct
ANY = pl.BlockSpec(memory_space=pl.ANY)
MESH = pl.DeviceIdType.MESH


def _dot(a, b, dims=NN):
    return lax.dot_general(a.astype(MXU_DTYPE), b.astype(MXU_DTYPE), dims, preferred_element_type=F32)


def _segsum(x, bmat):
    hi = x.astype(BF16)
    lo = (x - hi.astype(F32)).astype(BF16)
    return (jnp.dot(hi, bmat, preferred_element_type=F32) + jnp.dot(lo, bmat, preferred_element_type=F32))


def _gelu(x):
    return 0.5 * x * (1.0 + jnp.tanh(GELU_K0 * (x + GELU_K1 * x * x * x)))


def _gelu_grad(x):
    t = jnp.tanh(GELU_K0 * (x + GELU_K1 * x * x * x))
    return 0.5 * (1.0 + t) + 0.5 * x * (1.0 - t * t) * GELU_K0 * (1.0 + 3.0 * GELU_K1 * x * x)


def _rms(x):
    return lax.rsqrt(jnp.mean(x * x, axis=-1, keepdims=True) + EPS)


def _rms_bwd(dy, x, g, r):
    dyg = dy * g
    dx = r * dyg - x * (r * r * r) * jnp.mean(dyg * x, axis=-1, keepdims=True)
    return dx, dy * x * r


def _pcall(body, *, name, grid, in_specs, out_specs, out_shape, scratch=(), prefetch=0):
    params = pltpu.CompilerParams(dimension_semantics=("arbitrary",) * len(grid), vmem_limit_bytes=VMEM_LIMIT_V7X)
    if prefetch:
        spec = pltpu.PrefetchScalarGridSpec(num_scalar_prefetch=prefetch, grid=grid, in_specs=in_specs,
                                            out_specs=out_specs, scratch_shapes=list(scratch))
        return pl.pallas_call(body, name=name, grid_spec=spec, out_shape=out_shape, compiler_params=params)
    return pl.pallas_call(body, name=name, grid=grid, in_specs=in_specs, out_specs=out_specs, out_shape=out_shape,
                          scratch_shapes=list(scratch), compiler_params=params)


def _tile(n, prefs):
    for p in prefs:
        if p <= n and n % p == 0:
            return p
    return n


def _acc_rows(ref, row, val):
    ref[row:row + 1, :] += jnp.sum(val, axis=0, keepdims=True)


def rms_mm(x, g, w, *, name):
    M, K = x.shape
    N = w.shape[1]
    tm, tn = _tile(M, (512, 256)), _tile(N, (512, 256))

    def body(x_ref, g_ref, w_ref, h_ref, o_ref):
        @pl.when(pl.program_id(1) == 0)
        def _():
            xv = x_ref[...]
            h_ref[...] = (xv * _rms(xv) * g_ref[...]).astype(h_ref.dtype)

        o_ref[...] = _dot(h_ref[...], w_ref[...])

    return _pcall(body, name=name, grid=(M // tm, N // tn),
                  in_specs=[BS((tm, K), lambda i, j: (i, 0)), BS((1, K), lambda i, j: (0, 0)),
                            BS((K, tn), lambda i, j: (0, j))],
                  out_specs=[BS((tm, K), lambda i, j: (i, 0)), BS((tm, tn), lambda i, j: (i, j))],
                  out_shape=[SDS((M, K), MXU_DTYPE), SDS((M, N), F32)])(x, g, w)


def mm(a, w, *, name, res=None, out_dtype=F32):
    M, K = a.shape
    N = w.shape[1]
    tm, tn = _tile(M, (512, 256)), _tile(N, (1024, 512, 256))
    tk = K if K <= 2816 else _tile(K, (2816, 1024, 512))
    nk = K // tk

    def body(*refs):
        if res is None:
            a_ref, w_ref, o_ref, acc = refs
        else:
            a_ref, w_ref, r_ref, o_ref, acc = refs
        k = pl.program_id(2)

        @pl.when(k == 0)
        def _():
            acc[...] = jnp.zeros_like(acc)

        acc[...] += _dot(a_ref[...], w_ref[...])

        @pl.when(k == nk - 1)
        def _():
            o = acc[...]
            if res is not None:
                o = o + r_ref[...]
            o_ref[...] = o.astype(o_ref.dtype)

    in_specs = [BS((tm, tk), lambda i, j, k: (i, k)), BS((tk, tn), lambda i, j, k: (k, j))]
    args = [a, w]
    if res is not None:
        in_specs.append(BS((tm, tn), lambda i, j, k: (i, j)))
        args.append(res)
    return _pcall(body, name=name, grid=(M // tm, N // tn, nk), in_specs=in_specs,
                  out_specs=BS((tm, tn), lambda i, j, k: (i, j)), out_shape=SDS((M, N), out_dtype),
                  scratch=[pltpu.VMEM((tm, tn), F32)])(*args)


def mm_tn(a, b, *, name, out_dtype):
    M, K = a.shape
    N = b.shape[1]
    tm = _tile(M, (512, 256))
    tk = _tile(K, (1408, 1024, 512))
    tn = _tile(N, (1408, 1024, 512))
    nm = M // tm

    def body(a_ref, b_ref, o_ref, acc):
        m = pl.program_id(2)

        @pl.when(m == 0)
        def _():
            acc[...] = jnp.zeros_like(acc)

        acc[...] += _dot(a_ref[...], b_ref[...], TN)

        @pl.when(m == nm - 1)
        def _():
            o_ref[...] = acc[...].astype(o_ref.dtype)

    return _pcall(body, name=name, grid=(K // tk, N // tn, nm),
                  in_specs=[BS((tm, tk), lambda k, n, m: (m, k)), BS((tm, tn), lambda k, n, m: (m, n))],
                  out_specs=BS((tk, tn), lambda k, n, m: (k, n)), out_shape=SDS((K, N), out_dtype),
                  scratch=[pltpu.VMEM((tk, tn), F32)])(a, b)


def _lane(shape):
    return lax.broadcasted_iota(jnp.int32, shape, 1)


def _norm_rope(slab, g, bmat, cos, sin, first):
    r = lax.rsqrt(_segsum(slab * slab, bmat) * (1.0 / HEAD_DIM) + EPS)
    qn = slab * r * g
    swapped = jnp.where(first, pltpu.roll(qn, 96, 1), pltpu.roll(qn, 32, 1))
    return qn * cos + swapped * sin


def mixer_pre(proj, cos, sin, gq, gk, gvn, bmat):
    S = proj.shape[0]
    tm = _tile(S, (256,))

    def body(p_ref, c_ref, s_ref, gq_ref, gk_ref, gvn_ref, b_ref, qr_ref, kr_ref, vb_ref, gu_ref, gvo_ref):
        cos_v, sin_v, bm = c_ref[...], s_ref[...], b_ref[...]
        first = (_lane((tm, 128)) & 63) < 32
        for s in range(4):
            sl = slice(s * 128, (s + 1) * 128)
            qr_ref[:, sl] = _norm_rope(p_ref[:, sl], gq_ref[...], bm, cos_v, sin_v, first).astype(qr_ref.dtype)
        for s in range(2):
            kr_ref[:, s * 128:(s + 1) * 128] = _norm_rope(p_ref[:, 512 + s * 128:640 + s * 128], gk_ref[...], bm,
                                                          cos_v, sin_v, first).astype(kr_ref.dtype)
        vb_ref[...] = p_ref[:, 768:1024].astype(vb_ref.dtype)
        gu_ref[...] = _gelu(p_ref[:, 1024:1536])
        gv = _gelu(p_ref[:, 1536:2048])
        gvo_ref[...] = (gv * _rms(gv) * gvn_ref[...]).astype(gvo_ref.dtype)

    row = lambda w: BS((tm, w), lambda i: (i, 0))
    const = lambda r, w: BS((r, w), lambda i: (0, 0))
    return _pcall(body, name="mixer_pre", grid=(S // tm,),
                  in_specs=[row(IN_COLS_DUP), row(128), row(128), const(1, 128), const(1, 128), const(1, 512),
                            const(128, 128)],
                  out_specs=[row(512), row(256), row(256), row(512), row(512)],
                  out_shape=[SDS((S, 512), MXU_DTYPE), SDS((S, 256), MXU_DTYPE), SDS((S, 256), MXU_DTYPE),
                             SDS((S, 512), F32), SDS((S, 512), MXU_DTYPE)])(proj, cos, sin, gq, gk, gvn, bmat)


def _swa_probs(qs, kd, sink, n, lo):
    z = jnp.zeros_like(qs)
    qp = jnp.concatenate([jnp.where(lo, qs, z), jnp.where(lo, z, qs)], axis=0)
    sc = _dot(qp, kd, NT) * (1.0 / math.sqrt(HEAD_DIM))
    r_i = lax.broadcasted_iota(jnp.int32, (2 * BLK, 2 * BLK), 0)
    k_j = lax.broadcasted_iota(jnp.int32, (2 * BLK, 2 * BLK), 1)
    diff = (r_i & (BLK - 1)) + BLK - k_j
    mask = (diff >= 0) & (diff < BLK) & ((k_j >= BLK) | (n > 0))
    sc = jnp.where(mask, sc, MINF)
    m = jnp.maximum(jnp.max(sc, axis=1, keepdims=True), sink)
    p = jnp.exp(sc - m)
    es = jnp.exp(sink - m)
    l = jnp.sum(p, axis=1, keepdims=True) + es
    return qp, p / l, es / l


def swa_fwd(qr, kr, vb, sinkcol, gao):
    S = qr.shape[0]
    nb = S // BLK

    def body(q_ref, kc_ref, kp_ref, vc_ref, vp_ref, sk_ref, g_ref, o_ref, ya_ref):
        n = pl.program_id(0)
        lo = _lane((BLK, 128)) < 64
        for s in range(4):
            h = s // 2
            hs = slice(h * 128, (h + 1) * 128)
            kd = jnp.concatenate([kp_ref[:, hs], kc_ref[:, hs]], axis=0)
            vd = jnp.concatenate([vp_ref[:, hs], vc_ref[:, hs]], axis=0)
            _, p, _ = _swa_probs(q_ref[:, s * 128:(s + 1) * 128], kd, sk_ref[s], n, lo)
            o2 = _dot(p, vd)
            o_ref[:, s * 128:(s + 1) * 128] = jnp.where(lo, o2[:BLK], o2[BLK:])
        a = o_ref[...]
        ya_ref[...] = (a * _rms(a) * g_ref[...]).astype(ya_ref.dtype)

    cur = lambda w: BS((BLK, w), lambda n: (n, 0))
    prev = lambda w: BS((BLK, w), lambda n: (jnp.maximum(n - 1, 0), 0))
    return _pcall(body, name="swa_fwd", grid=(nb,),
                  in_specs=[cur(512), cur(256), prev(256), cur(256), prev(256),
                            BS((4, 2 * BLK, 1), lambda n: (0, 0, 0)), BS((1, 512), lambda n: (0, 0))],
                  out_specs=[cur(512), cur(512)],
                  out_shape=[SDS((S, 512), F32), SDS((S, 512), MXU_DTYPE)])(qr, kr, kr, vb, vb, sinkcol, gao)


def gmlp_fwd(gvn, gu, ya, w2, bsl, ggo):
    S = gvn.shape[0]

    def body(gvn_ref, gu_ref, ya_ref, w2_ref, bsl_ref, g_ref, gm_ref, y_ref):
        lo = _lane((BLK, 128)) < 64
        for j in range(4):
            sl = slice(j * 128, (j + 1) * 128)
            m2 = _dot(w2_ref[j], gvn_ref[:, sl])
            mixed = jnp.where(lo, m2[:BLK], m2[BLK:]) + bsl_ref[j]
            gm_ref[:, sl] = gu_ref[:, sl] * mixed
        gm = gm_ref[...]
        y_ref[:, :512] = ya_ref[...]
        y_ref[:, 512:] = (gm * _rms(gm) * g_ref[...]).astype(y_ref.dtype)

    row = lambda w: BS((BLK, w), lambda n: (n, 0))
    return _pcall(body, name="gmlp_fwd", grid=(S // BLK,),
                  in_specs=[row(512), row(512), row(512), BS((4, 2 * BLK, BLK), lambda n: (0, 0, 0)),
                            BS((4, BLK, 128), lambda n: (0, 0, 0)), BS((1, 512), lambda n: (0, 0))],
                  out_specs=[row(512), row(1024)],
                  out_shape=[SDS((S, 512), F32), SDS((S, 1024), MXU_DTYPE)])(gvn, gu, ya, w2, bsl, ggo)


def mem_pre(kv, gxk):
    def body(kv_ref, g_ref, kn_ref, vb_ref):
        for h in range(XA_HEADS):
            sl = slice(h * XA_DH, (h + 1) * XA_DH)
            k = kv_ref[:, sl]
            kn_ref[:, sl] = (k * _rms(k) * g_ref[...]).astype(kn_ref.dtype)
        vb_ref[...] = kv_ref[:, 1024:2048].astype(vb_ref.dtype)

    full = lambda r, w: BS((r, w), lambda i: (0, 0))
    return _pcall(body, name="mem_pre", grid=(1,), in_specs=[full(MEM_LEN, 2048), full(1, XA_DH)],
                  out_specs=[full(MEM_LEN, 1024), full(MEM_LEN, 1024)],
                  out_shape=[SDS((MEM_LEN, 1024), MXU_DTYPE), SDS((MEM_LEN, 1024), MXU_DTYPE)])(kv, gxk)


def _xa_probs(qh, g, kn_h):
    r = _rms(qh)
    qn = qh * r * g
    s = _dot(qn, kn_h, NT) * (1.0 / math.sqrt(XA_DH))
    p = jnp.exp(s - jnp.max(s, axis=1, keepdims=True))
    return r, qn, p / jnp.sum(p, axis=1, keepdims=True)


def xattn_fwd(qx, kn, vb, gxq):
    S = qx.shape[0]
    tm = _tile(S, (256,))

    def body(q_ref, kn_ref, vb_ref, g_ref, o_ref):
        for h in range(XA_HEADS):
            sl = slice(h * XA_DH, (h + 1) * XA_DH)
            _, _, p = _xa_probs(q_ref[:, sl], g_ref[...], kn_ref[:, sl])
            o_ref[:, sl] = _dot(p, vb_ref[:, sl]).astype(o_ref.dtype)

    full = lambda r, w: BS((r, w), lambda i: (0, 0))
    return _pcall(body, name="xattn_fwd", grid=(S // tm,),
                  in_specs=[BS((tm, 1024), lambda i: (i, 0)), full(MEM_LEN, 1024), full(MEM_LEN, 1024), full(1, XA_DH)],
                  out_specs=BS((tm, 1024), lambda i: (i, 0)), out_shape=SDS((S, 1024), MXU_DTYPE))(qx, kn, vb, gxq)


def _causal_taps(a, halo_ref, first_tile, row):
    h6 = jnp.where(first_tile, 0.0, halo_ref[6:7, :])
    h7 = jnp.where(first_tile, 0.0, halo_ref[7:8, :])
    a1 = jnp.where(row == 0, h7, pltpu.roll(a, 1, 0))
    a2 = jnp.where(row == 0, h6, jnp.where(row == 1, h7, pltpu.roll(a, 2, 0)))
    return a1, a2


def _conv(a, a1, a2, w_ref, b_ref):
    return w_ref[2:3, :] * a + w_ref[1:2, :] * a1 + w_ref[0:1, :] * a2 + b_ref[...]


def _conv_specs(tm, S):
    half = lambda r: BS((r, D_FF), lambda i: (0, 0))
    halo_blocks = tm // 8
    return [BS((tm, D_FF), lambda i: (i, 0)), BS((tm, D_FF), lambda i: (i, 1)),
            BS((8, D_FF), lambda i: (jnp.maximum(i * halo_blocks - 1, 0), 0)),
            BS((8, D_FF), lambda i: (jnp.maximum(i * halo_blocks - 1, 0), 1)),
            BS((3, D_FF), lambda i: (0, 0)), BS((3, D_FF), lambda i: (0, 1)),
            BS((1, D_FF), lambda i: (0, 0)), BS((1, D_FF), lambda i: (0, 1))]


def convgate_fwd(a, cw, cb):
    S = a.shape[0]
    tm = _tile(S, (256,))

    def body(ag_ref, au_ref, hg_ref, hu_ref, wg_ref, wu_ref, bg_ref, bu_ref, f_ref):
        first_tile = pl.program_id(0) == 0
        row = lax.broadcasted_iota(jnp.int32, (tm, D_FF), 0)
        ag, au = ag_ref[...], au_ref[...]
        cg = _conv(ag, *_causal_taps(ag, hg_ref, first_tile, row), wg_ref, bg_ref)
        cu = _conv(au, *_causal_taps(au, hu_ref, first_tile, row), wu_ref, bu_ref)
        f_ref[...] = (_gelu(cg) * cu).astype(f_ref.dtype)

    return _pcall(body, name="convgate_fwd", grid=(S // tm,), in_specs=_conv_specs(tm, S),
                  out_specs=BS((tm, D_FF), lambda i: (i, 0)),
                  out_shape=SDS((S, D_FF), MXU_DTYPE))(a, a, a, a, cw, cw, cb, cb)


def loss_head(x3, target):
    S = x3.shape[0]
    tm = _tile(S, (512, 256))

    def body(x_ref, t_ref, d_ref, l_ref):
        @pl.when(pl.program_id(0) == 0)
        def _():
            l_ref[...] = jnp.zeros_like(l_ref)

        e = x_ref[...] - t_ref[...]
        d_ref[...] = e * (1.0 / D_MODEL)
        l_ref[...] += jnp.sum(e * e) * (0.5 / D_MODEL)

    row = BS((tm, D_MODEL), lambda i: (i, 0))
    return _pcall(body, name="loss_head", grid=(S // tm,), in_specs=[row, row],
                  out_specs=[row, BS((8, 128), lambda i: (0, 0))],
                  out_shape=[SDS((S, D_MODEL), F32), SDS((8, 128), F32)])(x3, target)


def convgate_bwd(a, df, cw, cb):
    S = a.shape[0]
    tm = _tile(S, (128,))

    def body(ag_ref, au_ref, hg_ref, hu_ref, wg_ref, wu_ref, bg_ref, bu_ref, df_ref, dc_ref, gw_ref):
        first_tile = pl.program_id(0) == 0

        @pl.when(first_tile)
        def _():
            gw_ref[...] = jnp.zeros_like(gw_ref)

        row = lax.broadcasted_iota(jnp.int32, (tm, D_FF), 0)
        ag, au, df_v = ag_ref[...], au_ref[...], df_ref[...]
        ag1, ag2 = _causal_taps(ag, hg_ref, first_tile, row)
        au1, au2 = _causal_taps(au, hu_ref, first_tile, row)
        cg = _conv(ag, ag1, ag2, wg_ref, bg_ref)
        cu = _conv(au, au1, au2, wu_ref, bu_ref)
        dcg = df_v * cu * _gelu_grad(cg)
        dcu = df_v * _gelu(cg)
        dc_ref[:, :D_FF] = dcg
        dc_ref[:, D_FF:] = dcu
        for col, dcv, taps in ((slice(0, D_FF), dcg, (ag2, ag1, ag)), (slice(D_FF, 2 * D_FF), dcu, (au2, au1, au))):
            for j in range(3):
                gw_ref[j:j + 1, col] += jnp.sum(dcv * taps[j], axis=0, keepdims=True)
            gw_ref[3:4, col] += jnp.sum(dcv, axis=0, keepdims=True)

    return _pcall(body, name="convgate_bwd", grid=(S // tm,),
                  in_specs=_conv_specs(tm, S) + [BS((tm, D_FF), lambda i: (i, 0))],
                  out_specs=[BS((tm, 2 * D_FF), lambda i: (i, 0)), BS((8, 2 * D_FF), lambda i: (0, 0))],
                  out_shape=[SDS((S, 2 * D_FF), F32), SDS((8, 2 * D_FF), F32)])(a, a, a, a, cw, cw, cb, cb, df)


def conv_transpose(dc, cw):
    S, C = dc.shape
    tm = _tile(S, (128,))
    nt = S // tm
    halo_blocks = tm // 8

    def body(dc_ref, halo_ref, w_ref, da_ref):
        last_tile = pl.program_id(0) == nt - 1
        row = lax.broadcasted_iota(jnp.int32, (tm, C), 0)
        h0 = jnp.where(last_tile, 0.0, halo_ref[0:1, :])
        h1 = jnp.where(last_tile, 0.0, halo_ref[1:2, :])
        dc_v = dc_ref[...]
        n1 = jnp.where(row == tm - 1, h0, pltpu.roll(dc_v, tm - 1, 0))
        n2 = jnp.where(row == tm - 1, h1, jnp.where(row == tm - 2, h0, pltpu.roll(dc_v, tm - 2, 0)))
        da_ref[...] = (w_ref[2:3, :] * dc_v + w_ref[1:2, :] * n1 + w_ref[0:1, :] * n2).astype(da_ref.dtype)

    return _pcall(body, name="conv_transpose", grid=(nt,),
                  in_specs=[BS((tm, C), lambda i: (i, 0)),
                            BS((8, C), lambda i: (jnp.minimum((i + 1) * halo_blocks, S // 8 - 1), 0)),
                            BS((3, C), lambda i: (0, 0))],
                  out_specs=BS((tm, C), lambda i: (i, 0)), out_shape=SDS((S, C), MXU_DTYPE))(dc, dc, cw)


def rms_bwd(dh, x, g, dres, *, name):
    S, W = x.shape
    tm = _tile(S, (512, 256))

    def body(dh_ref, x_ref, g_ref, dr_ref, dx_ref, dg_ref):
        @pl.when(pl.program_id(0) == 0)
        def _():
            dg_ref[...] = jnp.zeros_like(dg_ref)

        xv = x_ref[...]
        dx, dgc = _rms_bwd(dh_ref[...], xv, g_ref[...], _rms(xv))
        dx_ref[...] = dr_ref[...] + dx
        _acc_rows(dg_ref, 0, dgc)

    row = BS((tm, W), lambda i: (i, 0))
    return _pcall(body, name=name, grid=(S // tm,), in_specs=[row, row, BS((1, W), lambda i: (0, 0)), row],
                  out_specs=[row, BS((8, W), lambda i: (0, 0))],
                  out_shape=[SDS((S, W), F32), SDS((8, W), F32)])(dh, x, g, dres)


def xattn_bwd(qx, dxo, kn, vb, gxq):
    S = qx.shape[0]
    tm = _tile(S, (256,))

    def body(q_ref, do_ref, kn_ref, vb_ref, g_ref, dq_ref, dkn_ref, dv_ref, dg_ref):
        @pl.when(pl.program_id(0) == 0)
        def _():
            dkn_ref[...] = jnp.zeros_like(dkn_ref)
            dv_ref[...] = jnp.zeros_like(dv_ref)
            dg_ref[...] = jnp.zeros_like(dg_ref)

        g = g_ref[...]
        for h in range(XA_HEADS):
            sl = slice(h * XA_DH, (h + 1) * XA_DH)
            qh, do = q_ref[:, sl], do_ref[:, sl]
            r, qn, p = _xa_probs(qh, g, kn_ref[:, sl])
            dp = _dot(do, vb_ref[:, sl], NT)
            ds = p * (dp - jnp.sum(dp * p, axis=1, keepdims=True)) * (1.0 / math.sqrt(XA_DH))
            dqn = _dot(ds, kn_ref[:, sl])
            dkn_ref[:, sl] += _dot(ds, qn, TN)
            dv_ref[:, sl] += _dot(p, do, TN)
            dqh, dgc = _rms_bwd(dqn, qh, g, r)
            dq_ref[:, sl] = dqh.astype(dq_ref.dtype)
            _acc_rows(dg_ref, 0, dgc)

    row = BS((tm, 1024), lambda i: (i, 0))
    full = lambda r, w: BS((r, w), lambda i: (0, 0))
    return _pcall(body, name="xattn_bwd", grid=(S // tm,),
                  in_specs=[row, row, full(MEM_LEN, 1024), full(MEM_LEN, 1024), full(1, XA_DH)],
                  out_specs=[row, full(MEM_LEN, 1024), full(MEM_LEN, 1024), full(8, XA_DH)],
                  out_shape=[SDS((S, 1024), MXU_DTYPE), SDS((MEM_LEN, 1024), F32), SDS((MEM_LEN, 1024), F32),
                             SDS((8, XA_DH), F32)])(qx, dxo, kn, vb, gxq)


def mem_bwd(kv, dkn, dvb, gxk):
    def body(kv_ref, dkn_ref, dv_ref, g_ref, dkv_ref, dg_ref):
        dg_ref[...] = jnp.zeros_like(dg_ref)
        for h in range(XA_HEADS):
            sl = slice(h * XA_DH, (h + 1) * XA_DH)
            k = kv_ref[:, sl]
            dk, dgc = _rms_bwd(dkn_ref[:, sl], k, g_ref[...], _rms(k))
            dkv_ref[:, sl] = dk.astype(dkv_ref.dtype)
            _acc_rows(dg_ref, 0, dgc)
        dkv_ref[:, 1024:2048] = dv_ref[...].astype(dkv_ref.dtype)

    full = lambda r, w: BS((r, w), lambda i: (0, 0))
    return _pcall(body, name="mem_bwd", grid=(1,),
                  in_specs=[full(MEM_LEN, 2048), full(MEM_LEN, 1024), full(MEM_LEN, 1024), full(1, XA_DH)],
                  out_specs=[full(MEM_LEN, 2048), full(8, XA_DH)],
                  out_shape=[SDS((MEM_LEN, 2048), MXU_DTYPE), SDS((8, XA_DH), F32)])(kv, dkn, dvb, gxk)


def mixer_post_bwd(dy, attn, gm, gao, ggo):
    S = dy.shape[0]
    tm = _tile(S, (256,))

    def body(dy_ref, a_ref, gm_ref, gao_ref, ggo_ref, da_ref, dgm_ref, dg_ref):
        @pl.when(pl.program_id(0) == 0)
        def _():
            dg_ref[...] = jnp.zeros_like(dg_ref)

        a, gmv = a_ref[...], gm_ref[...]
        da, dga = _rms_bwd(dy_ref[:, :512], a, gao_ref[...], _rms(a))
        dgm, dgg = _rms_bwd(dy_ref[:, 512:], gmv, ggo_ref[...], _rms(gmv))
        da_ref[...] = da
        dgm_ref[...] = dgm
        dg_ref[0:1, :512] += jnp.sum(dga, axis=0, keepdims=True)
        dg_ref[0:1, 512:] += jnp.sum(dgg, axis=0, keepdims=True)

    row = lambda w: BS((tm, w), lambda i: (i, 0))
    const = lambda r, w: BS((r, w), lambda i: (0, 0))
    return _pcall(body, name="mixer_post_bwd", grid=(S // tm,),
                  in_specs=[row(1024), row(512), row(512), const(1, 512), const(1, 512)],
                  out_specs=[row(512), row(512), const(8, 1024)],
                  out_shape=[SDS((S, 512), F32), SDS((S, 512), F32), SDS((8, 1024), F32)])(dy, attn, gm, gao, ggo)


def gmlp_bwd(dgm, gvn, gu, w2, w2t, bsl):
    S = dgm.shape[0]

    def body(dgm_ref, gvn_ref, gu_ref, w2_ref, w2t_ref, bsl_ref, dgu_ref, dgvn_ref, dws_ref, dbl_ref):
        @pl.when(pl.program_id(0) == 0)
        def _():
            dws_ref[...] = jnp.zeros_like(dws_ref)
            dbl_ref[...] = jnp.zeros_like(dbl_ref)

        lo = _lane((BLK, 128)) < 64
        for j in range(4):
            sl = slice(j * 128, (j + 1) * 128)
            gvn_s = gvn_ref[:, sl]
            m2 = _dot(w2_ref[j], gvn_s)
            mixed = jnp.where(lo, m2[:BLK], m2[BLK:]) + bsl_ref[j]
            dgm_s = dgm_ref[:, sl]
            dgu_ref[:, sl] = dgm_s * mixed
            dmx = dgm_s * gu_ref[:, sl]
            d2 = _dot(w2t_ref[j], dmx)
            dgvn_ref[:, sl] = jnp.where(lo, d2[:BLK], d2[BLK:])
            z = jnp.zeros_like(dmx)
            dws_ref[2 * j] += _dot(jnp.where(lo, dmx, z), gvn_s, NT)
            dws_ref[2 * j + 1] += _dot(jnp.where(lo, z, dmx), gvn_s, NT)
            dbl_ref[j] += dmx

    row = lambda w: BS((BLK, w), lambda n: (n, 0))
    const3 = lambda a, b, c: BS((a, b, c), lambda n: (0, 0, 0))
    return _pcall(body, name="gmlp_bwd", grid=(S // BLK,),
                  in_specs=[row(512), row(512), row(512), const3(4, 2 * BLK, BLK), const3(4, 2 * BLK, BLK),
                            const3(4, BLK, 128)],
                  out_specs=[row(512), row(512), const3(8, BLK, BLK), const3(4, BLK, 128)],
                  out_shape=[SDS((S, 512), F32), SDS((S, 512), F32), SDS((8, BLK, BLK), F32),
                             SDS((4, BLK, 128), F32)])(dgm, gvn, gu, w2, w2t, bsl)


def swa_bwd(qr, kr, vb, sinkcol, dattn):
    S = qr.shape[0]
    nb = S // BLK

    def body(q_ref, kc_ref, kp_ref, vc_ref, vp_ref, sk_ref, do_ref, dq_ref, dk_ref, dv_ref, dsk_ref,
             carry_k, carry_v, prev_k, prev_v):
        n = pl.program_id(0)

        @pl.when(n == 0)
        def _():
            dsk_ref[...] = jnp.zeros_like(dsk_ref)
            carry_k[...] = jnp.zeros_like(carry_k)
            carry_v[...] = jnp.zeros_like(carry_v)

        @pl.when(n < nb)
        def _():
            lo = _lane((BLK, 128)) < 64
            for h in range(2):
                hs = slice(h * 128, (h + 1) * 128)
                kd = jnp.concatenate([kp_ref[:, hs], kc_ref[:, hs]], axis=0)
                vd = jnp.concatenate([vp_ref[:, hs], vc_ref[:, hs]], axis=0)
                dkd = jnp.zeros((2 * BLK, 128), F32)
                dvd = jnp.zeros((2 * BLK, 128), F32)
                for s in (2 * h, 2 * h + 1):
                    sl = slice(s * 128, (s + 1) * 128)
                    qp, p, psink = _swa_probs(q_ref[:, sl], kd, sk_ref[s], n, lo)
                    do = do_ref[:, sl]
                    z = jnp.zeros_like(do)
                    dop = jnp.concatenate([jnp.where(lo, do, z), jnp.where(lo, z, do)], axis=0)
                    dp = _dot(dop, vd, NT)
                    delta = jnp.sum(dp * p, axis=1, keepdims=True)
                    ds = p * (dp - delta) * (1.0 / math.sqrt(HEAD_DIM))
                    dsk_ref[s] += -psink * delta
                    dq2 = _dot(ds, kd)
                    dq_ref[:, sl] = jnp.where(lo, dq2[:BLK], dq2[BLK:])
                    dkd = dkd + _dot(ds, qp, TN)
                    dvd = dvd + _dot(p, dop, TN)
                prev_k[:, hs] = carry_k[:, hs] + dkd[:BLK]
                prev_v[:, hs] = carry_v[:, hs] + dvd[:BLK]
                carry_k[:, hs] = dkd[BLK:]
                carry_v[:, hs] = dvd[BLK:]

        @pl.when(n == nb)
        def _():
            prev_k[...] = carry_k[...]
            prev_v[...] = carry_v[...]

        dk_ref[...] = prev_k[...]
        dv_ref[...] = prev_v[...]

    last = nb - 1
    cur = lambda w: BS((BLK, w), lambda n: (jnp.minimum(n, last), 0))
    prev = lambda w: BS((BLK, w), lambda n: (jnp.clip(n - 1, 0, last), 0))
    done = lambda w: BS((BLK, w), lambda n: (jnp.maximum(n - 1, 0), 0))
    return _pcall(body, name="swa_bwd", grid=(nb + 1,),
                  in_specs=[cur(512), cur(256), prev(256), cur(256), prev(256),
                            BS((4, 2 * BLK, 1), lambda n: (0, 0, 0)), cur(512)],
                  out_specs=[cur(512), done(256), done(256), BS((4, 2 * BLK, 1), lambda n: (0, 0, 0))],
                  out_shape=[SDS((S, 512), F32), SDS((S, 256), F32), SDS((S, 256), F32), SDS((4, 2 * BLK, 1), F32)],
                  scratch=[pltpu.VMEM((BLK, 256), F32)] * 4)(qr, kr, kr, vb, vb, sinkcol, dattn)


def mixer_pre_bwd(proj, cos, sin, gq, gk, gvn, bmat, dqr, dkr, dvb, dgu, dgvn):
    S = proj.shape[0]
    tm = _tile(S, (256,))

    def body(p_ref, c_ref, s_ref, gq_ref, gk_ref, gvn_ref, b_ref, dqr_ref, dkr_ref, dvb_ref, dgu_ref, dgvn_ref,
             dp_ref, dgq_ref, dgk_ref, dgv_ref):
        @pl.when(pl.program_id(0) == 0)
        def _():
            dgq_ref[...] = jnp.zeros_like(dgq_ref)
            dgk_ref[...] = jnp.zeros_like(dgk_ref)
            dgv_ref[...] = jnp.zeros_like(dgv_ref)

        cos_v, sin_v, bm = c_ref[...], s_ref[...], b_ref[...]
        first = (_lane((tm, 128)) & 63) < 32

        def slab_bwd(slab, dout, g, dg_ref):
            r = lax.rsqrt(_segsum(slab * slab, bm) * (1.0 / HEAD_DIM) + EPS)
            ds = dout * sin_v
            dqn = dout * cos_v + jnp.where(first, pltpu.roll(ds, 96, 1), pltpu.roll(ds, 32, 1))
            dyg = dqn * g
            dx = r * dyg - slab * (r * r * r) * (_segsum(dyg * slab, bm) * (1.0 / HEAD_DIM))
            _acc_rows(dg_ref, 0, dqn * slab * r)
            return dx

        for s in range(4):
            sl = slice(s * 128, (s + 1) * 128)
            dp_ref[:, sl] = slab_bwd(p_ref[:, sl], dqr_ref[:, sl], gq_ref[...], dgq_ref).astype(dp_ref.dtype)
        for s in range(2):
            sl = slice(512 + s * 128, 640 + s * 128)
            dp_ref[:, sl] = slab_bwd(p_ref[:, sl], dkr_ref[:, s * 128:(s + 1) * 128], gk_ref[...],
                                     dgk_ref).astype(dp_ref.dtype)
        dp_ref[:, 768:1024] = dvb_ref[...].astype(dp_ref.dtype)
        dp_ref[:, 1024:1536] = (dgu_ref[...] * _gelu_grad(p_ref[:, 1024:1536])).astype(dp_ref.dtype)
        gvp = p_ref[:, 1536:2048]
        gv = _gelu(gvp)
        dgv, dgc = _rms_bwd(dgvn_ref[...], gv, gvn_ref[...], _rms(gv))
        dp_ref[:, 1536:2048] = (dgv * _gelu_grad(gvp)).astype(dp_ref.dtype)
        _acc_rows(dgv_ref, 0, dgc)

    row = lambda w: BS((tm, w), lambda i: (i, 0))
    const = lambda r, w: BS((r, w), lambda i: (0, 0))
    return _pcall(body, name="mixer_pre_bwd", grid=(S // tm,),
                  in_specs=[row(IN_COLS_DUP), row(128), row(128), const(1, 128), const(1, 128), const(1, 512),
                            const(128, 128), row(512), row(256), row(256), row(512), row(512)],
                  out_specs=[row(IN_COLS_DUP), const(8, 128), const(8, 128), const(8, 512)],
                  out_shape=[SDS((S, IN_COLS_DUP), MXU_DTYPE), SDS((8, 128), F32), SDS((8, 128), F32),
                             SDS((8, 512), F32)])(proj, cos, sin, gq, gk, gvn, bmat, dqr, dkr, dvb, dgu, dgvn)


def adamw(w, g, m, v, *, name):
    R, W = w.shape
    tr = _tile(R, (240, 168, 8))

    def body(w_ref, g_ref, m_ref, v_ref, d_ref, mo_ref, vo_ref):
        gv = g_ref[...]
        mn = ADAM_B1 * m_ref[...] + (1.0 - ADAM_B1) * gv
        vn = ADAM_B2 * v_ref[...] + (1.0 - ADAM_B2) * (gv * gv)
        m_hat = mn / (1.0 - ADAM_B1 ** ADAM_STEP)
        v_hat = vn / (1.0 - ADAM_B2 ** ADAM_STEP)
        d_ref[...] = -ADAM_LR * (m_hat / (jnp.sqrt(v_hat) + ADAM_EPS) + ADAM_WD * w_ref[...])
        mo_ref[...] = mn
        vo_ref[...] = vn

    row = BS((tr, W), lambda i: (i, 0))
    return _pcall(body, name=name, grid=(R // tr,), in_specs=[row] * 4, out_specs=[row] * 3,
                  out_shape=[SDS((R, W), F32)] * 3)(w, g, m, v)


def _place():
    return lax.axis_index("x"), lax.axis_index("y"), lax.axis_index("c")


def _other_chips(x, y):
    return [(1 - x, y), (x, 1 - y), (1 - x, 1 - y)]


def _rows_of_core(c, half):
    return pl.ds(pl.multiple_of(c * half, 16), half)


def _comm_call(body, *, name, out_shape, scratch, n_in):
    return pl.pallas_call(body, name=name, out_shape=out_shape, in_specs=[ANY] * n_in,
                          out_specs=ANY if not isinstance(out_shape, (list, tuple)) else [ANY] * len(out_shape),
                          scratch_shapes=scratch, compiler_params=pltpu.CompilerParams(has_side_effects=True))


def gather_weights(shard):
    rows, width = shard.shape
    half = rows // 2

    def body(sh_ref, out_ref, send_sems, recv_sems, local_sem):
        x, y, c = _place()
        p = 2 * x + y
        sibling = (x, y, 1 - c)
        chips = _other_chips(x, y)
        mine_rows, sib_rows = _rows_of_core(c, half), _rows_of_core(1 - c, half)

        def rcopy(k, src, dst, to):
            return pltpu.make_async_remote_copy(src_ref=src, dst_ref=dst, send_sem=send_sems.at[k],
                                                recv_sem=recv_sems.at[k], device_id=to, device_id_type=MESH)

        mine = pltpu.make_async_copy(sh_ref, out_ref.at[p], local_sem)
        mine.start()
        first = [rcopy(j, sh_ref.at[mine_rows], out_ref.at[p, mine_rows], (cx, cy, c))
                 for j, (cx, cy) in enumerate(chips)]
        for cp in first:
            cp.start()
        passed = []
        for j, (cx, cy) in enumerate(chips):
            slab = out_ref.at[2 * cx + cy, mine_rows]
            rcopy(j, slab, slab, (cx, cy, c)).wait_recv()
            fw = rcopy(3 + j, slab, slab, sibling)
            fw.start()
            passed.append(fw)
        for j, (cx, cy) in enumerate(chips):
            slab = out_ref.at[2 * cx + cy, sib_rows]
            rcopy(3 + j, slab, slab, sibling).wait_recv()
        for cp in first + passed:
            cp.wait_send()
        mine.wait()

    return _comm_call(body, name="gather_weights", out_shape=SDS((N_CHIPS, rows, width), shard.dtype), n_in=1,
                      scratch=[pltpu.SemaphoreType.DMA((6,)), pltpu.SemaphoreType.DMA((6,)),
                               pltpu.SemaphoreType.DMA])(shard)


def pair_exchange(g):
    n, rows, width = g.shape
    half = rows // 2

    def body(g_ref, r_ref, send_sem, recv_sem):
        x, y, c = _place()
        cp = pltpu.make_async_remote_copy(src_ref=g_ref.at[:, _rows_of_core(1 - c, half)], dst_ref=r_ref,
                                          send_sem=send_sem, recv_sem=recv_sem, device_id=(x, y, 1 - c),
                                          device_id_type=MESH)
        cp.start()
        cp.wait()

    return _comm_call(body, name="pair_exchange", out_shape=SDS((n, half, width), g.dtype), n_in=1,
                      scratch=[pltpu.SemaphoreType.DMA, pltpu.SemaphoreType.DMA])(g)


def pair_add(g, r):
    n, rows, width = g.shape
    half = rows // 2
    tr = _tile(half, (480, 240, 16))
    g4 = g.reshape(n, 2, half, width)
    cvec = lax.axis_index("c").astype(jnp.int32).reshape(1)

    def body(c_ref, g_ref, r_ref, o_ref):
        o_ref[...] = (g_ref[...].astype(F32) + r_ref[...].astype(F32)).astype(o_ref.dtype)

    return _pcall(body, name="pair_add", grid=(n, half // tr), prefetch=1,
                  in_specs=[BS((None, None, tr, width), lambda q, i, c_ref: (q, c_ref[0], i, 0)),
                            BS((None, tr, width), lambda q, i, c_ref: (q, i, 0))],
                  out_specs=BS((None, tr, width), lambda q, i, c_ref: (q, i, 0)),
                  out_shape=SDS((n, half, width), g.dtype))(cvec, g4, r)


def scatter_partials(pp):
    n, rows, width = pp.shape

    def body(p_ref, r_ref, send_sems, recv_sems, local_sem):
        x, y, c = _place()
        p = 2 * x + y
        chips = _other_chips(x, y)
        mine = pltpu.make_async_copy(p_ref.at[p], r_ref.at[p], local_sem)
        mine.start()

        def rcopy(j, q, slot, to):
            return pltpu.make_async_remote_copy(src_ref=p_ref.at[q], dst_ref=r_ref.at[slot], send_sem=send_sems.at[j],
                                                recv_sem=recv_sems.at[j], device_id=to, device_id_type=MESH)

        sends = [rcopy(j, 2 * cx + cy, p, (cx, cy, c)) for j, (cx, cy) in enumerate(chips)]
        for cp in sends:
            cp.start()
        for j, (cx, cy) in enumerate(chips):
            rcopy(j, p, 2 * cx + cy, (cx, cy, c)).wait_recv()
        for cp in sends:
            cp.wait_send()
        mine.wait()

    return _comm_call(body, name="scatter_partials", out_shape=SDS((n, rows, width), pp.dtype), n_in=1,
                      scratch=[pltpu.SemaphoreType.DMA((3,)), pltpu.SemaphoreType.DMA((3,)),
                               pltpu.SemaphoreType.DMA])(pp)


def sum_slots(r, *, name):
    n, rows, width = r.shape
    tr = _tile(rows, (480, 240, 168, 8))

    def body(r_ref, o_ref):
        acc = r_ref[0].astype(F32)
        for s in range(1, n):
            acc = acc + r_ref[s].astype(F32)
        o_ref[...] = acc

    return _pcall(body, name=name, grid=(rows // tr,), in_specs=[BS((n, tr, width), lambda i: (0, i, 0))],
                  out_specs=BS((tr, width), lambda i: (i, 0)), out_shape=SDS((rows, width), F32))(r)


def pair_share(f):
    rows, width = f.shape

    def body(f_ref, o_ref, send_sem, recv_sem, local_sem):
        x, y, c = _place()
        mine = pltpu.make_async_copy(f_ref, o_ref.at[c], local_sem)
        mine.start()
        cp = pltpu.make_async_remote_copy(src_ref=f_ref, dst_ref=o_ref.at[c], send_sem=send_sem, recv_sem=recv_sem,
                                          device_id=(x, y, 1 - c), device_id_type=MESH)
        cp.start()
        pltpu.make_async_remote_copy(src_ref=f_ref, dst_ref=o_ref.at[1 - c], send_sem=send_sem, recv_sem=recv_sem,
                                     device_id=(x, y, 1 - c), device_id_type=MESH).wait_recv()
        cp.wait_send()
        mine.wait()

    return _comm_call(body, name="pair_share", out_shape=SDS((2, rows, width), f.dtype), n_in=1,
                      scratch=[pltpu.SemaphoreType.DMA, pltpu.SemaphoreType.DMA, pltpu.SemaphoreType.DMA])(f)


def gather_all(sm):
    rows, width = sm.shape

    def body(s_ref, o_ref, send_sems, recv_sems, local_sem):
        x, y, c = _place()
        me = 4 * x + 2 * y + c
        mine = pltpu.make_async_copy(s_ref, o_ref.at[me], local_sem)
        mine.start()
        peers = []
        for k in range(1, N_DEV):
            px = 1 - x if k & 4 else x
            py = 1 - y if k & 2 else y
            pc = 1 - c if k & 1 else c
            peers.append((px, py, pc))

        def rcopy(k, slot, to):
            return pltpu.make_async_remote_copy(src_ref=s_ref, dst_ref=o_ref.at[slot], send_sem=send_sems.at[k],
                                                recv_sem=recv_sems.at[k], device_id=to, device_id_type=MESH)

        sends = [rcopy(k, me, peer) for k, peer in enumerate(peers)]
        for cp in sends:
            cp.start()
        for k, (px, py, pc) in enumerate(peers):
            rcopy(k, 4 * px + 2 * py + pc, (px, py, pc)).wait_recv()
        for cp in sends:
            cp.wait_send()
        mine.wait()

    return _comm_call(body, name="gather_all", out_shape=SDS((N_DEV, rows, width), sm.dtype), n_in=1,
                      scratch=[pltpu.SemaphoreType.DMA((7,)), pltpu.SemaphoreType.DMA((7,)),
                               pltpu.SemaphoreType.DMA])(sm)


BIG = (("w_in", (1024, 448), True), ("w_out", (256, 1024), False), ("xa_wq", (256, 1024), False),
       ("xa_wkv", (1024, 512), True), ("xa_wo", (256, 1024), False), ("ffn_up", (1024, 1408), True),
       ("ffn_down", (704, 1024), False))
SMALL = ("mix_norm", "q_norm", "k_norm", "attn_sinks", "gmlp_v_norm", "gmlp_ws", "gmlp_bs", "attn_out_norm",
         "gmlp_out_norm", "xa_norm", "mem_norm", "xa_q_norm", "xa_k_norm", "ffn_norm", "ffn_conv_b", "ffn_conv")
WEIGHTS = ("mix_norm", "w_in", "q_norm", "k_norm", "attn_sinks", "gmlp_v_norm", "gmlp_ws", "gmlp_bs",
           "attn_out_norm", "gmlp_out_norm", "w_out", "xa_norm", "mem_norm", "xa_wq", "xa_wkv", "xa_q_norm",
           "xa_k_norm", "xa_wo", "ffn_norm", "ffn_up", "ffn_conv", "ffn_conv_b", "ffn_down")
CONV_SHARD = (3, 1408)
CONV_WIRE_ROWS = 32


def _rows(shape):
    return -(-math.prod(shape) // PACK_W)


def _pack_rows(arrs, pad_to=8):
    parts = []
    for a in arrs:
        flat = a.reshape(-1)
        parts.append(jnp.pad(flat, (0, _rows(a.shape) * PACK_W - flat.shape[0])).reshape(-1, PACK_W))
    buf = jnp.concatenate(parts, axis=0)
    return jnp.pad(buf, ((0, -buf.shape[0] % pad_to), (0, 0)))


def _unpack_rows(buf, shapes):
    out, off = [], 0
    for shp in shapes:
        r = _rows(shp)
        out.append(buf[off:off + r].reshape(-1)[:math.prod(shp)].reshape(shp))
        off += r
    return out


def _to_full(blk, col):
    n, r, c = blk.shape
    return blk.transpose(1, 0, 2).reshape(r, n * c) if col else blk.reshape(n * r, c)


def _to_chip_major(full, shard_shape, col):
    r, c = shard_shape
    blk = full.reshape(r, N_CHIPS, c).transpose(1, 0, 2) if col else full.reshape(N_CHIPS, r, c)
    return blk.reshape(N_CHIPS, -1, PACK_W)


def _dup_cols(w):
    dup = lambda t: jnp.concatenate([t[:, :64], t[:, :64], t[:, 64:], t[:, 64:]], axis=1)
    return jnp.concatenate([w[:, :512], dup(w[:, 512:640]), dup(w[:, 640:768]), w[:, 768:]], axis=1)


def _fold_cols(d):
    fold = lambda t: jnp.concatenate([t[:, 0:64] + t[:, 64:128], t[:, 128:192] + t[:, 192:256]], axis=1)
    return jnp.concatenate([d[:, :512], fold(d[:, 512:768]), fold(d[:, 768:1024]), d[:, 1024:]], axis=1)


def _local_step(x, mem, positions, target, wf, sp):
    gain = lambda n: sp[n].reshape(1, -1)
    half = HEAD_DIM // 2
    inv_freq = 1.0 / (10000.0 ** (jnp.arange(half, dtype=F32) * (2.0 / HEAD_DIM)))
    ang = positions.astype(F32)[:, None] * inv_freq
    cos, sin = jnp.cos(ang), jnp.sin(ang)
    cos128 = jnp.tile(cos, (1, 4))
    sin128 = jnp.concatenate([-sin, sin, -sin, sin], axis=1)
    seg = jnp.arange(128) // HEAD_DIM
    bmat = (seg[:, None] == seg[None, :]).astype(BF16)
    gq128, gk128 = jnp.tile(gain("q_norm"), (1, 2)), jnp.tile(gain("k_norm"), (1, 2))
    sinkcol = jnp.repeat(sp["attn_sinks"].reshape(4, 2), BLK, axis=1).reshape(4, 2 * BLK, 1)
    causal = jnp.tril(jnp.ones((BLK, BLK), F32))
    wsc = sp["gmlp_ws"] * causal[None]
    w2 = wsc.reshape(4, 2 * BLK, BLK).astype(MXU_DTYPE)
    w2t = wsc.swapaxes(1, 2).reshape(4, 2 * BLK, BLK).astype(MXU_DTYPE)
    bsl = jnp.repeat(sp["gmlp_bs"].reshape(4, 2, BLK).transpose(0, 2, 1), HEAD_DIM, axis=2)
    cw, cb = sp["ffn_conv"], sp["ffn_conv_b"].reshape(1, -1)
    w_in_d = _dup_cols(wf["w_in"])

    h1, proj = rms_mm(x, gain("mix_norm"), w_in_d, name="mix_in")
    qr, kr, vb, gu, gvn = mixer_pre(proj, cos128, sin128, gq128, gk128, gain("gmlp_v_norm"), bmat)
    attn, ya = swa_fwd(qr, kr, vb, sinkcol, gain("attn_out_norm"))
    gm, y = gmlp_fwd(gvn, gu, ya, w2, bsl, gain("gmlp_out_norm"))
    x1 = mm(y, wf["w_out"], res=x, name="mix_out")
    h2, qx = rms_mm(x1, gain("xa_norm"), wf["xa_wq"], name="xa_q")
    mn, kv = rms_mm(mem, gain("mem_norm"), wf["xa_wkv"], name="xa_kv")
    kn, vbx = mem_pre(kv, gain("xa_k_norm"))
    xo = xattn_fwd(qx, kn, vbx, gain("xa_q_norm"))
    x2 = mm(xo, wf["xa_wo"], res=x1, name="xa_out")
    h3, a = rms_mm(x2, gain("ffn_norm"), wf["ffn_up"], name="ffn_up")
    f = convgate_fwd(a, cw, cb)
    x3 = mm(f, wf["ffn_down"], res=x2, name="ffn_down")
    dx3, loss_acc = loss_head(x3, target)

    gbig = {}
    df = mm(dx3, wf["ffn_down"].T, name="d_f")
    gbig["ffn_down"] = mm_tn(f, dx3, name="g_ffn_down", out_dtype=WIRE_DTYPE)
    dc, gcw = convgate_bwd(a, df, cw, cb)
    da = conv_transpose(dc, cw)
    dh3 = mm(da, wf["ffn_up"].T, name="d_h3")
    gbig["ffn_up"] = mm_tn(h3, da, name="g_ffn_up", out_dtype=WIRE_DTYPE)
    dx2, dg_ffn = rms_bwd(dh3, x2, gain("ffn_norm"), dx3, name="ffn_norm_bwd")
    dxo = mm(dx2, wf["xa_wo"].T, name="d_xo")
    gbig["xa_wo"] = mm_tn(xo, dx2, name="g_xa_wo", out_dtype=WIRE_DTYPE)
    dqx, dkn, dvx, dg_xq = xattn_bwd(qx, dxo, kn, vbx, gain("xa_q_norm"))
    dh2 = mm(dqx, wf["xa_wq"].T, name="d_h2")
    gbig["xa_wq"] = mm_tn(h2, dqx, name="g_xa_wq", out_dtype=WIRE_DTYPE)
    dx1, dg_xa = rms_bwd(dh2, x1, gain("xa_norm"), dx2, name="xa_norm_bwd")
    dkv, dg_xk = mem_bwd(kv, dkn, dvx, gain("xa_k_norm"))
    dmn = mm(dkv, wf["xa_wkv"].T, name="d_mn")
    gbig["xa_wkv"] = mm_tn(mn, dkv, name="g_xa_wkv", out_dtype=WIRE_DTYPE)
    _, dg_mem = rms_bwd(dmn, mem, gain("mem_norm"), jnp.zeros_like(mem), name="mem_norm_bwd")
    dy = mm(dx1, wf["w_out"].T, name="d_y")
    gbig["w_out"] = mm_tn(y, dx1, name="g_w_out", out_dtype=WIRE_DTYPE)
    dattn, dgm, dg_y = mixer_post_bwd(dy, attn, gm, gain("attn_out_norm"), gain("gmlp_out_norm"))
    dgu, dgvn, dws, dbl = gmlp_bwd(dgm, gvn, gu, w2, w2t, bsl)
    dqr, dkr, dvb, dsk = swa_bwd(qr, kr, vb, sinkcol, dattn)
    dproj, dgq, dgk, dg_gvn = mixer_pre_bwd(proj, cos128, sin128, gq128, gk128, gain("gmlp_v_norm"), bmat,
                                            dqr, dkr, dvb, dgu, dgvn)
    dh1 = mm(dproj, w_in_d.T, name="d_h1")
    gbig["w_in"] = _fold_cols(mm_tn(h1, dproj, name="g_w_in", out_dtype=F32)).astype(WIRE_DTYPE)
    grad_x, dg_mix = rms_bwd(dh1, x, gain("mix_norm"), dx1, name="mix_norm_bwd")

    gsmall = {
        "mix_norm": dg_mix[0], "q_norm": dgq[0, :64] + dgq[0, 64:], "k_norm": dgk[0, :64] + dgk[0, 64:],
        "attn_sinks": dsk.reshape(4, 2, BLK).sum(-1).reshape(8), "gmlp_v_norm": dg_gvn[0],
        "gmlp_ws": dws * causal[None],
        "gmlp_bs": dbl.reshape(4, BLK, 2, HEAD_DIM).sum(-1).transpose(0, 2, 1).reshape(8, BLK),
        "attn_out_norm": dg_y[0, :512], "gmlp_out_norm": dg_y[0, 512:], "xa_norm": dg_xa[0], "mem_norm": dg_mem[0],
        "xa_q_norm": dg_xq[0], "xa_k_norm": dg_xk[0], "ffn_norm": dg_ffn[0], "ffn_conv_b": gcw[3],
        "ffn_conv": gcw[0:3],
    }
    return loss_acc, grad_x, gbig, gsmall


def kernel(x, mem, positions, mix_norm, w_in, q_norm, k_norm, attn_sinks, gmlp_v_norm, gmlp_ws, gmlp_bs, attn_out_norm, gmlp_out_norm, w_out, xa_norm, mem_norm, xa_wq, xa_wkv, xa_q_norm, xa_k_norm, xa_wo, ffn_norm, ffn_up, ffn_conv, ffn_conv_b, ffn_down, loss_target, m_mix_norm, m_w_in, m_q_norm, m_k_norm, m_attn_sinks, m_gmlp_v_norm, m_gmlp_ws, m_gmlp_bs, m_attn_out_norm, m_gmlp_out_norm, m_w_out, m_xa_norm, m_mem_norm, m_xa_wq, m_xa_wkv, m_xa_q_norm, m_xa_k_norm, m_xa_wo, m_ffn_norm, m_ffn_up, m_ffn_conv, m_ffn_conv_b, m_ffn_down, v_mix_norm, v_w_in, v_q_norm, v_k_norm, v_attn_sinks, v_gmlp_v_norm, v_gmlp_ws, v_gmlp_bs, v_attn_out_norm, v_gmlp_out_norm, v_w_out, v_xa_norm, v_mem_norm, v_xa_wq, v_xa_wkv, v_xa_q_norm, v_xa_k_norm, v_xa_wo, v_ffn_norm, v_ffn_up, v_ffn_conv, v_ffn_conv_b, v_ffn_down):
    w = dict(mix_norm=mix_norm, w_in=w_in, q_norm=q_norm, k_norm=k_norm, attn_sinks=attn_sinks, gmlp_v_norm=gmlp_v_norm, gmlp_ws=gmlp_ws, gmlp_bs=gmlp_bs, attn_out_norm=attn_out_norm, gmlp_out_norm=gmlp_out_norm, w_out=w_out, xa_norm=xa_norm, mem_norm=mem_norm, xa_wq=xa_wq, xa_wkv=xa_wkv, xa_q_norm=xa_q_norm, xa_k_norm=xa_k_norm, xa_wo=xa_wo, ffn_norm=ffn_norm, ffn_up=ffn_up, ffn_conv=ffn_conv, ffn_conv_b=ffn_conv_b, ffn_down=ffn_down)
    m = dict(mix_norm=m_mix_norm, w_in=m_w_in, q_norm=m_q_norm, k_norm=m_k_norm, attn_sinks=m_attn_sinks, gmlp_v_norm=m_gmlp_v_norm, gmlp_ws=m_gmlp_ws, gmlp_bs=m_gmlp_bs, attn_out_norm=m_attn_out_norm, gmlp_out_norm=m_gmlp_out_norm, w_out=m_w_out, xa_norm=m_xa_norm, mem_norm=m_mem_norm, xa_wq=m_xa_wq, xa_wkv=m_xa_wkv, xa_q_norm=m_xa_q_norm, xa_k_norm=m_xa_k_norm, xa_wo=m_xa_wo, ffn_norm=m_ffn_norm, ffn_up=m_ffn_up, ffn_conv=m_ffn_conv, ffn_conv_b=m_ffn_conv_b, ffn_down=m_ffn_down)
    v = dict(mix_norm=v_mix_norm, w_in=v_w_in, q_norm=v_q_norm, k_norm=v_k_norm, attn_sinks=v_attn_sinks, gmlp_v_norm=v_gmlp_v_norm, gmlp_ws=v_gmlp_ws, gmlp_bs=v_gmlp_bs, attn_out_norm=v_attn_out_norm, gmlp_out_norm=v_gmlp_out_norm, w_out=v_w_out, xa_norm=v_xa_norm, mem_norm=v_mem_norm, xa_wq=v_xa_wq, xa_wkv=v_xa_wkv, xa_q_norm=v_xa_q_norm, xa_k_norm=v_xa_k_norm, xa_wo=v_xa_wo, ffn_norm=v_ffn_norm, ffn_up=v_ffn_up, ffn_conv=v_ffn_conv, ffn_conv_b=v_ffn_conv_b, ffn_down=v_ffn_down)
    chip = 2 * lax.axis_index("x") + lax.axis_index("y")

    conv_bits = lax.bitcast_convert_type(w["ffn_conv"][0], BF16).reshape(-1)
    conv_rows = jnp.pad(conv_bits, (0, CONV_WIRE_ROWS * PACK_W - conv_bits.shape[0])).reshape(CONV_WIRE_ROWS, PACK_W)
    shard = jnp.concatenate([w[n][0].astype(BF16).reshape(-1, PACK_W) for n, _, _ in BIG] + [conv_rows], axis=0)
    gathered = gather_weights(shard)
    wf, off = {}, 0
    for n, shp, col in BIG:
        r = _rows(shp)
        wf[n] = _to_full(gathered[:, off:off + r].reshape(N_CHIPS, *shp), col)
        off += r
    conv_all = gathered[:, off:].reshape(N_CHIPS, -1)[:, :2 * math.prod(CONV_SHARD)].reshape(N_CHIPS, *CONV_SHARD, 2)
    sp = {n: w[n][0] for n in SMALL}
    sp["ffn_conv"] = _to_full(lax.bitcast_convert_type(conv_all, F32), True)

    loss_acc, grad_x, gbig, gsmall = _local_step(x[0], mem[0], positions[0], loss_target[0], wf, sp)

    parts = jnp.concatenate([_to_chip_major(gbig[n], shp, col) for n, shp, col in BIG], axis=1)
    pair = pair_add(parts, pair_exchange(parts))
    reduced = pair_share(sum_slots(scatter_partials(pair), name="sum_chips")).reshape(PACK_ROWS, PACK_W)

    small_shapes = [gsmall[n].shape for n in SMALL]
    small_sum = sum_slots(gather_all(_pack_rows([gsmall[n] for n in SMALL])), name="sum_small")
    gs = dict(zip(SMALL, _unpack_rows(small_sum, small_shapes)))
    gs["ffn_conv"] = lax.dynamic_slice(gs["ffn_conv"], (0, chip * CONV_SHARD[1]), CONV_SHARD)

    grads = dict(zip([n for n, _, _ in BIG], _unpack_rows(reduced, [shp for _, shp, _ in BIG])))
    grads.update(gs)
    big_names = [n for n, _, _ in BIG]
    pack_big = lambda d: jnp.concatenate([d[n][0].reshape(-1, PACK_W) for n in big_names], axis=0)
    big_out = adamw(pack_big(w), reduced, pack_big(m), pack_big(v), name="adamw_matrices")
    pack_small = lambda d: _pack_rows([d[n][0] for n in SMALL])
    small_out = adamw(pack_small(w), _pack_rows([gs[n] for n in SMALL]), pack_small(m), pack_small(v),
                      name="adamw_small")
    upd = []
    for big_buf, small_buf in zip(big_out, small_out):
        d = dict(zip(big_names, _unpack_rows(big_buf, [shp for _, shp, _ in BIG])))
        d.update(zip(SMALL, _unpack_rows(small_buf, [w[n][0].shape for n in SMALL])))
        upd.append(d)

    loss = lax.psum(loss_acc[0, 0], ("x", "y", "c"))
    shaped = lambda d: [d[n].reshape(w[n].shape) for n in WEIGHTS]
    return (loss, grad_x[None], *shaped(grads), *shaped(upd[0]), *shaped(upd[1]), *shaped(upd[2]))
```

```python
import math

import jax
import jax.numpy as jnp
from jax import lax
from jax.experimental import pallas as pl
from jax.experimental.pallas import tpu as pltpu

F32 = jnp.float32
BF16 = jnp.bfloat16
MXU_DTYPE = jnp.bfloat16
WIRE_DTYPE = jnp.bfloat16
EPS = 1e-6
VMEM_LIMIT_V7X = 56 * 1024 * 1024

D_MODEL = 1024
HEAD_DIM = 64
BLK = 128
XA_HEADS = 4
XA_DH = 256
MEM_LEN = 256
D_FF = 2816
IN_COLS_DUP = 2048
N_CHIPS = 4
N_DEV = 8

ADAM_LR = 0.001
ADAM_B1 = 0.9
ADAM_B2 = 0.999
ADAM_EPS = 1e-08
ADAM_WD = 0.01
ADAM_STEP = 10

NT = (((1,), (1,)), ((), ()))
TN = (((0,), (0,)), ((), ()))
NN = (((1,), (0,)), ((), ()))
MINF = float(jnp.finfo(jnp.float32).min)
GELU_K0 = math.sqrt(2.0 / math.pi)
GELU_K1 = 0.044715

BS = pl.BlockSpec
SDS = jax.ShapeDtypeStruct
ANY = pl.BlockSpec(memory_space=pl.ANY)
MESH = pl.DeviceIdType.MESH


def _dot(a, b, dims=NN):
    return lax.dot_general(a.astype(MXU_DTYPE), b.astype(MXU_DTYPE), dims, preferred_element_type=F32)


def _segsum(x, bmat):
    hi = x.astype(BF16)
    lo = (x - hi.astype(F32)).astype(BF16)
    return (jnp.dot(hi, bmat, preferred_element_type=F32) + jnp.dot(lo, bmat, preferred_element_type=F32))


def _gelu(x):
    return 0.5 * x * (1.0 + jnp.tanh(GELU_K0 * (x + GELU_K1 * x * x * x)))


def _gelu_grad(x):
    t = jnp.tanh(GELU_K0 * (x + GELU_K1 * x * x * x))
    return 0.5 * (1.0 + t) + 0.5 * x * (1.0 - t * t) * GELU_K0 * (1.0 + 3.0 * GELU_K1 * x * x)


def _rms(x):
    return lax.rsqrt(jnp.mean(x * x, axis=-1, keepdims=True) + EPS)


def _rms_bwd(dy, x, g, r):
    dyg = dy * g
    dx = r * dyg - x * (r * r * r) * jnp.mean(dyg * x, axis=-1, keepdims=True)
    return dx, dy * x * r


def _pcall(body, *, name, grid, in_specs, out_specs, out_shape, scratch=(), prefetch=0):
    params = pltpu.CompilerParams(dimension_semantics=("arbitrary",) * len(grid), vmem_limit_bytes=VMEM_LIMIT_V7X)
    if prefetch:
        spec = pltpu.PrefetchScalarGridSpec(num_scalar_prefetch=prefetch, grid=grid, in_specs=in_specs,
                                            out_specs=out_specs, scratch_shapes=list(scratch))
        return pl.pallas_call(body, name=name, grid_spec=spec, out_shape=out_shape, compiler_params=params)
    return pl.pallas_call(body, name=name, grid=grid, in_specs=in_specs, out_specs=out_specs, out_shape=out_shape,
                          scratch_shapes=list(scratch), compiler_params=params)


def _tile(n, prefs):
    for p in prefs:
        if p <= n and n % p == 0:
            return p
    return n


def _acc_rows(ref, row, val):
    ref[row:row + 1, :] += jnp.sum(val, axis=0, keepdims=True)


def rms_mm(x, g, w3, *, name, tm=512):
    M, K = x.shape
    Q, _, C = w3.shape
    tm = _tile(M, (tm, 256))

    def body(x_ref, g_ref, w_ref, h_ref, o_ref):
        @pl.when(pl.program_id(1) == 0)
        def _():
            xv = x_ref[...]
            h_ref[...] = (xv * _rms(xv) * g_ref[...]).astype(h_ref.dtype)

        o_ref[...] = _dot(h_ref[...], w_ref[...])

    return _pcall(body, name=name, grid=(M // tm, Q),
                  in_specs=[BS((tm, K), lambda i, j: (i, 0)), BS((1, K), lambda i, j: (0, 0)),
                            BS((None, K, C), lambda i, j: (j, 0, 0))],
                  out_specs=[BS((tm, K), lambda i, j: (i, 0)), BS((tm, C), lambda i, j: (i, j))],
                  out_shape=[SDS((M, K), MXU_DTYPE), SDS((M, Q * C), F32)])(x, g, w3)


def mm(a, w, *, name, res):
    M, K = a.shape
    N = w.shape[1]
    tm = _tile(M, (512, 256))

    def body(a_ref, w_ref, r_ref, o_ref):
        o_ref[...] = _dot(a_ref[...], w_ref[...]) + r_ref[...]

    return _pcall(body, name=name, grid=(M // tm,),
                  in_specs=[BS((tm, K), lambda i: (i, 0)), BS((K, N), lambda i: (0, 0)), BS((tm, N), lambda i: (i, 0))],
                  out_specs=BS((tm, N), lambda i: (i, 0)), out_shape=SDS((M, N), F32))(a, w, res)


def mm_nt(a, w3, *, name):
    M = a.shape[0]
    Q, N, Kc = w3.shape
    tm, tn = _tile(M, (512, 256)), _tile(N, (1408, 1024, 512))

    def body(a_ref, w_ref, o_ref, acc):
        q = pl.program_id(2)

        @pl.when(q == 0)
        def _():
            acc[...] = jnp.zeros_like(acc)

        acc[...] += _dot(a_ref[...], w_ref[...], NT)

        @pl.when(q == Q - 1)
        def _():
            o_ref[...] = acc[...]

    return _pcall(body, name=name, grid=(M // tm, N // tn, Q),
                  in_specs=[BS((tm, Kc), lambda i, n, q: (i, q)), BS((None, tn, Kc), lambda i, n, q: (q, n, 0))],
                  out_specs=BS((tm, tn), lambda i, n, q: (i, n)), out_shape=SDS((M, N), F32),
                  scratch=[pltpu.VMEM((tm, tn), F32)])(a, w3)


def mm_tn(a, b, *, name, out_dtype, chunks=1):
    M, K = a.shape
    N = b.shape[1]
    C = N // chunks
    tm = _tile(M, (512, 256))
    tk = _tile(K, (1408, 1024, 512))
    tn = _tile(C, (1408, 1024, 512))
    per = C // tn
    nm = M // tm

    def body(a_ref, b_ref, o_ref, acc):
        m = pl.program_id(2)

        @pl.when(m == 0)
        def _():
            acc[...] = jnp.zeros_like(acc)

        acc[...] += _dot(a_ref[...], b_ref[...], TN)

        @pl.when(m == nm - 1)
        def _():
            o_ref[...] = acc[...].astype(o_ref.dtype)

    return _pcall(body, name=name, grid=(K // tk, N // tn, nm),
                  in_specs=[BS((tm, tk), lambda k, n, m: (m, k)), BS((tm, tn), lambda k, n, m: (m, n))],
                  out_specs=BS((None, tk, tn), lambda k, n, m: (n // per, k, n % per)),
                  out_shape=SDS((chunks, K, C), out_dtype), scratch=[pltpu.VMEM((tk, tn), F32)])(a, b)


def _lane(shape):
    return lax.broadcasted_iota(jnp.int32, shape, 1)


def _norm_rope(slab, g, bmat, cos, sin, first):
    r = lax.rsqrt(_segsum(slab * slab, bmat) * (1.0 / HEAD_DIM) + EPS)
    qn = slab * r * g
    swapped = jnp.where(first, pltpu.roll(qn, 96, 1), pltpu.roll(qn, 32, 1))
    return qn * cos + swapped * sin


def mixer_pre(proj, cos, sin, gq, gk, gvn, bmat):
    S = proj.shape[0]
    tm = _tile(S, (256,))

    def body(p_ref, c_ref, s_ref, gq_ref, gk_ref, gvn_ref, b_ref, qr_ref, kr_ref, vb_ref, gu_ref, gvo_ref):
        cos_v, sin_v, bm = c_ref[...], s_ref[...], b_ref[...]
        first = (_lane((tm, 128)) & 63) < 32
        for s in range(4):
            sl = slice(s * 128, (s + 1) * 128)
            qr_ref[:, sl] = _norm_rope(p_ref[:, sl], gq_ref[...], bm, cos_v, sin_v, first).astype(qr_ref.dtype)
        for s in range(2):
            kr_ref[:, s * 128:(s + 1) * 128] = _norm_rope(p_ref[:, 512 + s * 128:640 + s * 128], gk_ref[...], bm,
                                                          cos_v, sin_v, first).astype(kr_ref.dtype)
        vb_ref[...] = p_ref[:, 768:1024].astype(vb_ref.dtype)
        gu_ref[...] = _gelu(p_ref[:, 1024:1536])
        gv = _gelu(p_ref[:, 1536:2048])
        gvo_ref[...] = (gv * _rms(gv) * gvn_ref[...]).astype(gvo_ref.dtype)

    row = lambda w: BS((tm, w), lambda i: (i, 0))
    const = lambda r, w: BS((r, w), lambda i: (0, 0))
    return _pcall(body, name="mixer_pre", grid=(S // tm,),
                  in_specs=[row(IN_COLS_DUP), row(128), row(128), const(1, 128), const(1, 128), const(1, 512),
                            const(128, 128)],
                  out_specs=[row(512), row(256), row(256), row(512), row(512)],
                  out_shape=[SDS((S, 512), MXU_DTYPE), SDS((S, 256), MXU_DTYPE), SDS((S, 256), MXU_DTYPE),
                             SDS((S, 512), F32), SDS((S, 512), MXU_DTYPE)])(proj, cos, sin, gq, gk, gvn, bmat)


def _swa_probs(qs, kd, sink, n, lo):
    z = jnp.zeros_like(qs)
    qp = jnp.concatenate([jnp.where(lo, qs, z), jnp.where(lo, z, qs)], axis=0)
    sc = _dot(qp, kd, NT) * (1.0 / math.sqrt(HEAD_DIM))
    r_i = lax.broadcasted_iota(jnp.int32, (2 * BLK, 2 * BLK), 0)
    k_j = lax.broadcasted_iota(jnp.int32, (2 * BLK, 2 * BLK), 1)
    diff = (r_i & (BLK - 1)) + BLK - k_j
    mask = (diff >= 0) & (diff < BLK) & ((k_j >= BLK) | (n > 0))
    sc = jnp.where(mask, sc, MINF)
    m = jnp.maximum(jnp.max(sc, axis=1, keepdims=True), sink)
    p = jnp.exp(sc - m)
    es = jnp.exp(sink - m)
    l = jnp.sum(p, axis=1, keepdims=True) + es
    return qp, p / l, es / l


def swa_fwd(qr, kr, vb, sinkcol, gao):
    S = qr.shape[0]
    nb = S // BLK

    def body(q_ref, kc_ref, kp_ref, vc_ref, vp_ref, sk_ref, g_ref, o_ref, ya_ref):
        n = pl.program_id(0)
        lo = _lane((BLK, 128)) < 64
        for s in range(4):
            h = s // 2
            hs = slice(h * 128, (h + 1) * 128)
            kd = jnp.concatenate([kp_ref[:, hs], kc_ref[:, hs]], axis=0)
            vd = jnp.concatenate([vp_ref[:, hs], vc_ref[:, hs]], axis=0)
            _, p, _ = _swa_probs(q_ref[:, s * 128:(s + 1) * 128], kd, sk_ref[s], n, lo)
            o2 = _dot(p, vd)
            o_ref[:, s * 128:(s + 1) * 128] = jnp.where(lo, o2[:BLK], o2[BLK:])
        a = o_ref[...]
        ya_ref[...] = (a * _rms(a) * g_ref[...]).astype(ya_ref.dtype)

    cur = lambda w: BS((BLK, w), lambda n: (n, 0))
    prev = lambda w: BS((BLK, w), lambda n: (jnp.maximum(n - 1, 0), 0))
    return _pcall(body, name="swa_fwd", grid=(nb,),
                  in_specs=[cur(512), cur(256), prev(256), cur(256), prev(256),
                            BS((4, 2 * BLK, 1), lambda n: (0, 0, 0)), BS((1, 512), lambda n: (0, 0))],
                  out_specs=[cur(512), cur(512)],
                  out_shape=[SDS((S, 512), F32), SDS((S, 512), MXU_DTYPE)])(qr, kr, kr, vb, vb, sinkcol, gao)


def gmlp_fwd(gvn, gu, ya, w2, bsl, ggo):
    S = gvn.shape[0]

    def body(gvn_ref, gu_ref, ya_ref, w2_ref, bsl_ref, g_ref, gm_ref, y_ref):
        lo = _lane((BLK, 128)) < 64
        for j in range(4):
            sl = slice(j * 128, (j + 1) * 128)
            m2 = _dot(w2_ref[j], gvn_ref[:, sl])
            mixed = jnp.where(lo, m2[:BLK], m2[BLK:]) + bsl_ref[j]
            gm_ref[:, sl] = gu_ref[:, sl] * mixed
        gm = gm_ref[...]
        y_ref[:, :512] = ya_ref[...]
        y_ref[:, 512:] = (gm * _rms(gm) * g_ref[...]).astype(y_ref.dtype)

    row = lambda w: BS((BLK, w), lambda n: (n, 0))
    return _pcall(body, name="gmlp_fwd", grid=(S // BLK,),
                  in_specs=[row(512), row(512), row(512), BS((4, 2 * BLK, BLK), lambda n: (0, 0, 0)),
                            BS((4, BLK, 128), lambda n: (0, 0, 0)), BS((1, 512), lambda n: (0, 0))],
                  out_specs=[row(512), row(1024)],
                  out_shape=[SDS((S, 512), F32), SDS((S, 1024), MXU_DTYPE)])(gvn, gu, ya, w2, bsl, ggo)


def mem_pre(kv, gxk):
    def body(kv_ref, g_ref, kn_ref, vb_ref):
        for h in range(XA_HEADS):
            sl = slice(h * XA_DH, (h + 1) * XA_DH)
            k = kv_ref[:, sl]
            kn_ref[:, sl] = (k * _rms(k) * g_ref[...]).astype(kn_ref.dtype)
        vb_ref[...] = kv_ref[:, 1024:2048].astype(vb_ref.dtype)

    full = lambda r, w: BS((r, w), lambda i: (0, 0))
    return _pcall(body, name="mem_pre", grid=(1,), in_specs=[full(MEM_LEN, 2048), full(1, XA_DH)],
                  out_specs=[full(MEM_LEN, 1024), full(MEM_LEN, 1024)],
                  out_shape=[SDS((MEM_LEN, 1024), MXU_DTYPE), SDS((MEM_LEN, 1024), MXU_DTYPE)])(kv, gxk)


def _xa_probs(qh, g, kn_h):
    r = _rms(qh)
    qn = qh * r * g
    s = _dot(qn, kn_h, NT) * (1.0 / math.sqrt(XA_DH))
    p = jnp.exp(s - jnp.max(s, axis=1, keepdims=True))
    return r, qn, p / jnp.sum(p, axis=1, keepdims=True)


def xattn_fwd(qx, kn, vb, gxq):
    S = qx.shape[0]
    tm = _tile(S, (256,))

    def body(q_ref, kn_ref, vb_ref, g_ref, o_ref):
        for h in range(XA_HEADS):
            sl = slice(h * XA_DH, (h + 1) * XA_DH)
            _, _, p = _xa_probs(q_ref[:, sl], g_ref[...], kn_ref[:, sl])
            o_ref[:, sl] = _dot(p, vb_ref[:, sl]).astype(o_ref.dtype)

    full = lambda r, w: BS((r, w), lambda i: (0, 0))
    return _pcall(body, name="xattn_fwd", grid=(S // tm,),
                  in_specs=[BS((tm, 1024), lambda i: (i, 0)), full(MEM_LEN, 1024), full(MEM_LEN, 1024), full(1, XA_DH)],
                  out_specs=BS((tm, 1024), lambda i: (i, 0)), out_shape=SDS((S, 1024), MXU_DTYPE))(qx, kn, vb, gxq)


def _causal_taps(a, halo_ref, first_tile, row):
    h6 = jnp.where(first_tile, 0.0, halo_ref[6:7, :])
    h7 = jnp.where(first_tile, 0.0, halo_ref[7:8, :])
    a1 = jnp.where(row == 0, h7, pltpu.roll(a, 1, 0))
    a2 = jnp.where(row == 0, h6, jnp.where(row == 1, h7, pltpu.roll(a, 2, 0)))
    return a1, a2


def _conv(a, a1, a2, w_ref, b_ref):
    return w_ref[2:3, :] * a + w_ref[1:2, :] * a1 + w_ref[0:1, :] * a2 + b_ref[...]


def _conv_specs(tm):
    halo_blocks = tm // 8
    return [BS((tm, D_FF), lambda i: (i, 0)), BS((tm, D_FF), lambda i: (i, 1)),
            BS((8, D_FF), lambda i: (jnp.maximum(i * halo_blocks - 1, 0), 0)),
            BS((8, D_FF), lambda i: (jnp.maximum(i * halo_blocks - 1, 0), 1)),
            BS((3, D_FF), lambda i: (0, 0)), BS((3, D_FF), lambda i: (0, 1)),
            BS((1, D_FF), lambda i: (0, 0)), BS((1, D_FF), lambda i: (0, 1))]


def convgate_fwd(a, cw, cb):
    S = a.shape[0]
    tm = _tile(S, (256,))

    def body(ag_ref, au_ref, hg_ref, hu_ref, wg_ref, wu_ref, bg_ref, bu_ref, f_ref):
        first_tile = pl.program_id(0) == 0
        row = lax.broadcasted_iota(jnp.int32, (tm, D_FF), 0)
        ag, au = ag_ref[...], au_ref[...]
        cg = _conv(ag, *_causal_taps(ag, hg_ref, first_tile, row), wg_ref, bg_ref)
        cu = _conv(au, *_causal_taps(au, hu_ref, first_tile, row), wu_ref, bu_ref)
        f_ref[...] = (_gelu(cg) * cu).astype(f_ref.dtype)

    return _pcall(body, name="convgate_fwd", grid=(S // tm,), in_specs=_conv_specs(tm),
                  out_specs=BS((tm, D_FF), lambda i: (i, 0)),
                  out_shape=SDS((S, D_FF), MXU_DTYPE))(a, a, a, a, cw, cw, cb, cb)


def loss_head(x3, target):
    S = x3.shape[0]
    tm = _tile(S, (512, 256))

    def body(x_ref, t_ref, d_ref, l_ref):
        @pl.when(pl.program_id(0) == 0)
        def _():
            l_ref[...] = jnp.zeros_like(l_ref)

        e = x_ref[...] - t_ref[...]
        d_ref[...] = e * (1.0 / D_MODEL)
        l_ref[...] += jnp.sum(e * e) * (0.5 / D_MODEL)

    row = BS((tm, D_MODEL), lambda i: (i, 0))
    return _pcall(body, name="loss_head", grid=(S // tm,), in_specs=[row, row],
                  out_specs=[row, BS((8, 128), lambda i: (0, 0))],
                  out_shape=[SDS((S, D_MODEL), F32), SDS((8, 128), F32)])(x3, target)


def convgate_bwd(a, df, cw, cb):
    S = a.shape[0]
    tm = _tile(S, (128,))

    def body(ag_ref, au_ref, hg_ref, hu_ref, wg_ref, wu_ref, bg_ref, bu_ref, df_ref, dc_ref, gw_ref):
        first_tile = pl.program_id(0) == 0

        @pl.when(first_tile)
        def _():
            gw_ref[...] = jnp.zeros_like(gw_ref)

        row = lax.broadcasted_iota(jnp.int32, (tm, D_FF), 0)
        ag, au, df_v = ag_ref[...], au_ref[...], df_ref[...]
        ag1, ag2 = _causal_taps(ag, hg_ref, first_tile, row)
        au1, au2 = _causal_taps(au, hu_ref, first_tile, row)
        cg = _conv(ag, ag1, ag2, wg_ref, bg_ref)
        cu = _conv(au, au1, au2, wu_ref, bu_ref)
        dcg = df_v * cu * _gelu_grad(cg)
        dcu = df_v * _gelu(cg)
        dc_ref[:, :D_FF] = dcg
        dc_ref[:, D_FF:] = dcu
        for col, dcv, taps in ((slice(0, D_FF), dcg, (ag2, ag1, ag)), (slice(D_FF, 2 * D_FF), dcu, (au2, au1, au))):
            for j in range(3):
                gw_ref[j:j + 1, col] += jnp.sum(dcv * taps[j], axis=0, keepdims=True)
            gw_ref[3:4, col] += jnp.sum(dcv, axis=0, keepdims=True)

    return _pcall(body, name="convgate_bwd", grid=(S // tm,),
                  in_specs=_conv_specs(tm) + [BS((tm, D_FF), lambda i: (i, 0))],
                  out_specs=[BS((tm, 2 * D_FF), lambda i: (i, 0)), BS((8, 2 * D_FF), lambda i: (0, 0))],
                  out_shape=[SDS((S, 2 * D_FF), F32), SDS((8, 2 * D_FF), F32)])(a, a, a, a, cw, cw, cb, cb, df)


def conv_transpose(dc, cw):
    S, C = dc.shape
    tm = _tile(S, (128,))
    nt = S // tm
    halo_blocks = tm // 8

    def body(dc_ref, halo_ref, w_ref, da_ref):
        last_tile = pl.program_id(0) == nt - 1
        row = lax.broadcasted_iota(jnp.int32, (tm, C), 0)
        h0 = jnp.where(last_tile, 0.0, halo_ref[0:1, :])
        h1 = jnp.where(last_tile, 0.0, halo_ref[1:2, :])
        dc_v = dc_ref[...]
        n1 = jnp.where(row == tm - 1, h0, pltpu.roll(dc_v, tm - 1, 0))
        n2 = jnp.where(row == tm - 1, h1, jnp.where(row == tm - 2, h0, pltpu.roll(dc_v, tm - 2, 0)))
        da_ref[...] = (w_ref[2:3, :] * dc_v + w_ref[1:2, :] * n1 + w_ref[0:1, :] * n2).astype(da_ref.dtype)

    return _pcall(body, name="conv_transpose", grid=(nt,),
                  in_specs=[BS((tm, C), lambda i: (i, 0)),
                            BS((8, C), lambda i: (jnp.minimum((i + 1) * halo_blocks, S // 8 - 1), 0)),
                            BS((3, C), lambda i: (0, 0))],
                  out_specs=BS((tm, C), lambda i: (i, 0)), out_shape=SDS((S, C), MXU_DTYPE))(dc, dc, cw)


def rms_bwd(dh, x, g, dres, *, name):
    S, W = x.shape
    tm = _tile(S, (512, 256))

    def body(dh_ref, x_ref, g_ref, dr_ref, dx_ref, dg_ref):
        @pl.when(pl.program_id(0) == 0)
        def _():
            dg_ref[...] = jnp.zeros_like(dg_ref)

        xv = x_ref[...]
        dx, dgc = _rms_bwd(dh_ref[...], xv, g_ref[...], _rms(xv))
        dx_ref[...] = dr_ref[...] + dx
        _acc_rows(dg_ref, 0, dgc)

    row = BS((tm, W), lambda i: (i, 0))
    return _pcall(body, name=name, grid=(S // tm,), in_specs=[row, row, BS((1, W), lambda i: (0, 0)), row],
                  out_specs=[row, BS((8, W), lambda i: (0, 0))],
                  out_shape=[SDS((S, W), F32), SDS((8, W), F32)])(dh, x, g, dres)


def xattn_bwd(qx, dxo, kn, vb, gxq):
    S = qx.shape[0]
    tm = _tile(S, (256,))

    def body(q_ref, do_ref, kn_ref, vb_ref, g_ref, dq_ref, dkn_ref, dv_ref, dg_ref):
        @pl.when(pl.program_id(0) == 0)
        def _():
            dkn_ref[...] = jnp.zeros_like(dkn_ref)
            dv_ref[...] = jnp.zeros_like(dv_ref)
            dg_ref[...] = jnp.zeros_like(dg_ref)

        g = g_ref[...]
        for h in range(XA_HEADS):
            sl = slice(h * XA_DH, (h + 1) * XA_DH)
            qh, do = q_ref[:, sl], do_ref[:, sl]
            r, qn, p = _xa_probs(qh, g, kn_ref[:, sl])
            dp = _dot(do, vb_ref[:, sl], NT)
            ds = p * (dp - jnp.sum(dp * p, axis=1, keepdims=True)) * (1.0 / math.sqrt(XA_DH))
            dqn = _dot(ds, kn_ref[:, sl])
            dkn_ref[:, sl] += _dot(ds, qn, TN)
            dv_ref[:, sl] += _dot(p, do, TN)
            dqh, dgc = _rms_bwd(dqn, qh, g, r)
            dq_ref[:, sl] = dqh.astype(dq_ref.dtype)
            _acc_rows(dg_ref, 0, dgc)

    row = BS((tm, 1024), lambda i: (i, 0))
    full = lambda r, w: BS((r, w), lambda i: (0, 0))
    return _pcall(body, name="xattn_bwd", grid=(S // tm,),
                  in_specs=[row, row, full(MEM_LEN, 1024), full(MEM_LEN, 1024), full(1, XA_DH)],
                  out_specs=[row, full(MEM_LEN, 1024), full(MEM_LEN, 1024), full(8, XA_DH)],
                  out_shape=[SDS((S, 1024), MXU_DTYPE), SDS((MEM_LEN, 1024), F32), SDS((MEM_LEN, 1024), F32),
                             SDS((8, XA_DH), F32)])(qx, dxo, kn, vb, gxq)


def mem_bwd(kv, dkn, dvb, gxk):
    def body(kv_ref, dkn_ref, dv_ref, g_ref, dkv_ref, dg_ref):
        dg_ref[...] = jnp.zeros_like(dg_ref)
        for h in range(XA_HEADS):
            sl = slice(h * XA_DH, (h + 1) * XA_DH)
            k = kv_ref[:, sl]
            dk, dgc = _rms_bwd(dkn_ref[:, sl], k, g_ref[...], _rms(k))
            dkv_ref[:, sl] = dk.astype(dkv_ref.dtype)
            _acc_rows(dg_ref, 0, dgc)
        dkv_ref[:, 1024:2048] = dv_ref[...].astype(dkv_ref.dtype)

    full = lambda r, w: BS((r, w), lambda i: (0, 0))
    return _pcall(body, name="mem_bwd", grid=(1,),
                  in_specs=[full(MEM_LEN, 2048), full(MEM_LEN, 1024), full(MEM_LEN, 1024), full(1, XA_DH)],
                  out_specs=[full(MEM_LEN, 2048), full(8, XA_DH)],
                  out_shape=[SDS((MEM_LEN, 2048), MXU_DTYPE), SDS((8, XA_DH), F32)])(kv, dkn, dvb, gxk)


def mixer_post_bwd(dy, attn, gm, gao, ggo):
    S = dy.shape[0]
    tm = _tile(S, (256,))

    def body(dy_ref, a_ref, gm_ref, gao_ref, ggo_ref, da_ref, dgm_ref, dg_ref):
        @pl.when(pl.program_id(0) == 0)
        def _():
            dg_ref[...] = jnp.zeros_like(dg_ref)

        a, gmv = a_ref[...], gm_ref[...]
        da, dga = _rms_bwd(dy_ref[:, :512], a, gao_ref[...], _rms(a))
        dgm, dgg = _rms_bwd(dy_ref[:, 512:], gmv, ggo_ref[...], _rms(gmv))
        da_ref[...] = da
        dgm_ref[...] = dgm
        dg_ref[0:1, :512] += jnp.sum(dga, axis=0, keepdims=True)
        dg_ref[0:1, 512:] += jnp.sum(dgg, axis=0, keepdims=True)

    row = lambda w: BS((tm, w), lambda i: (i, 0))
    const = lambda r, w: BS((r, w), lambda i: (0, 0))
    return _pcall(body, name="mixer_post_bwd", grid=(S // tm,),
                  in_specs=[row(1024), row(512), row(512), const(1, 512), const(1, 512)],
                  out_specs=[row(512), row(512), const(8, 1024)],
                  out_shape=[SDS((S, 512), F32), SDS((S, 512), F32), SDS((8, 1024), F32)])(dy, attn, gm, gao, ggo)


def gmlp_bwd(dgm, gvn, gu, w2, w2t, bsl):
    S = dgm.shape[0]

    def body(dgm_ref, gvn_ref, gu_ref, w2_ref, w2t_ref, bsl_ref, dgu_ref, dgvn_ref, dws_ref, dbl_ref):
        @pl.when(pl.program_id(0) == 0)
        def _():
            dws_ref[...] = jnp.zeros_like(dws_ref)
            dbl_ref[...] = jnp.zeros_like(dbl_ref)

        lo = _lane((BLK, 128)) < 64
        for j in range(4):
            sl = slice(j * 128, (j + 1) * 128)
            gvn_s = gvn_ref[:, sl]
            m2 = _dot(w2_ref[j], gvn_s)
            mixed = jnp.where(lo, m2[:BLK], m2[BLK:]) + bsl_ref[j]
            dgm_s = dgm_ref[:, sl]
            dgu_ref[:, sl] = dgm_s * mixed
            dmx = dgm_s * gu_ref[:, sl]
            d2 = _dot(w2t_ref[j], dmx)
            dgvn_ref[:, sl] = jnp.where(lo, d2[:BLK], d2[BLK:])
            z = jnp.zeros_like(dmx)
            dws_ref[2 * j] += _dot(jnp.where(lo, dmx, z), gvn_s, NT)
            dws_ref[2 * j + 1] += _dot(jnp.where(lo, z, dmx), gvn_s, NT)
            dbl_ref[j] += dmx

    row = lambda w: BS((BLK, w), lambda n: (n, 0))
    const3 = lambda a, b, c: BS((a, b, c), lambda n: (0, 0, 0))
    return _pcall(body, name="gmlp_bwd", grid=(S // BLK,),
                  in_specs=[row(512), row(512), row(512), const3(4, 2 * BLK, BLK), const3(4, 2 * BLK, BLK),
                            const3(4, BLK, 128)],
                  out_specs=[row(512), row(512), const3(8, BLK, BLK), const3(4, BLK, 128)],
                  out_shape=[SDS((S, 512), F32), SDS((S, 512), F32), SDS((8, BLK, BLK), F32),
                             SDS((4, BLK, 128), F32)])(dgm, gvn, gu, w2, w2t, bsl)


def swa_bwd(qr, kr, vb, sinkcol, dattn):
    S = qr.shape[0]
    nb = S // BLK

    def body(q_ref, kc_ref, kp_ref, vc_ref, vp_ref, sk_ref, do_ref, dq_ref, dk_ref, dv_ref, dsk_ref,
             carry_k, carry_v, prev_k, prev_v):
        n = pl.program_id(0)

        @pl.when(n == 0)
        def _():
            dsk_ref[...] = jnp.zeros_like(dsk_ref)
            carry_k[...] = jnp.zeros_like(carry_k)
            carry_v[...] = jnp.zeros_like(carry_v)

        @pl.when(n < nb)
        def _():
            lo = _lane((BLK, 128)) < 64
            for h in range(2):
                hs = slice(h * 128, (h + 1) * 128)
                kd = jnp.concatenate([kp_ref[:, hs], kc_ref[:, hs]], axis=0)
                vd = jnp.concatenate([vp_ref[:, hs], vc_ref[:, hs]], axis=0)
                dkd = jnp.zeros((2 * BLK, 128), F32)
                dvd = jnp.zeros((2 * BLK, 128), F32)
                for s in (2 * h, 2 * h + 1):
                    sl = slice(s * 128, (s + 1) * 128)
                    qp, p, psink = _swa_probs(q_ref[:, sl], kd, sk_ref[s], n, lo)
                    do = do_ref[:, sl]
                    z = jnp.zeros_like(do)
                    dop = jnp.concatenate([jnp.where(lo, do, z), jnp.where(lo, z, do)], axis=0)
                    dp = _dot(dop, vd, NT)
                    delta = jnp.sum(dp * p, axis=1, keepdims=True)
                    ds = p * (dp - delta) * (1.0 / math.sqrt(HEAD_DIM))
                    dsk_ref[s] += -psink * delta
                    dq2 = _dot(ds, kd)
                    dq_ref[:, sl] = jnp.where(lo, dq2[:BLK], dq2[BLK:])
                    dkd = dkd + _dot(ds, qp, TN)
                    dvd = dvd + _dot(p, dop, TN)
                prev_k[:, hs] = carry_k[:, hs] + dkd[:BLK]
                prev_v[:, hs] = carry_v[:, hs] + dvd[:BLK]
                carry_k[:, hs] = dkd[BLK:]
                carry_v[:, hs] = dvd[BLK:]

        @pl.when(n == nb)
        def _():
            prev_k[...] = carry_k[...]
            prev_v[...] = carry_v[...]

        dk_ref[...] = prev_k[...]
        dv_ref[...] = prev_v[...]

    last = nb - 1
    cur = lambda w: BS((BLK, w), lambda n: (jnp.minimum(n, last), 0))
    prev = lambda w: BS((BLK, w), lambda n: (jnp.clip(n - 1, 0, last), 0))
    done = lambda w: BS((BLK, w), lambda n: (jnp.maximum(n - 1, 0), 0))
    return _pcall(body, name="swa_bwd", grid=(nb + 1,),
                  in_specs=[cur(512), cur(256), prev(256), cur(256), prev(256),
                            BS((4, 2 * BLK, 1), lambda n: (0, 0, 0)), cur(512)],
                  out_specs=[cur(512), done(256), done(256), BS((4, 2 * BLK, 1), lambda n: (0, 0, 0))],
                  out_shape=[SDS((S, 512), F32), SDS((S, 256), F32), SDS((S, 256), F32), SDS((4, 2 * BLK, 1), F32)],
                  scratch=[pltpu.VMEM((BLK, 256), F32)] * 4)(qr, kr, kr, vb, vb, sinkcol, dattn)


def mixer_pre_bwd(proj, cos, sin, gq, gk, gvn, bmat, dqr, dkr, dvb, dgu, dgvn):
    S = proj.shape[0]
    tm = _tile(S, (256,))

    def body(p_ref, c_ref, s_ref, gq_ref, gk_ref, gvn_ref, b_ref, dqr_ref, dkr_ref, dvb_ref, dgu_ref, dgvn_ref,
             dp_ref, dgq_ref, dgk_ref, dgv_ref):
        @pl.when(pl.program_id(0) == 0)
        def _():
            dgq_ref[...] = jnp.zeros_like(dgq_ref)
            dgk_ref[...] = jnp.zeros_like(dgk_ref)
            dgv_ref[...] = jnp.zeros_like(dgv_ref)

        cos_v, sin_v, bm = c_ref[...], s_ref[...], b_ref[...]
        first = (_lane((tm, 128)) & 63) < 32

        def slab_bwd(slab, dout, g, dg_ref):
            r = lax.rsqrt(_segsum(slab * slab, bm) * (1.0 / HEAD_DIM) + EPS)
            ds = dout * sin_v
            dqn = dout * cos_v + jnp.where(first, pltpu.roll(ds, 96, 1), pltpu.roll(ds, 32, 1))
            dyg = dqn * g
            dx = r * dyg - slab * (r * r * r) * (_segsum(dyg * slab, bm) * (1.0 / HEAD_DIM))
            _acc_rows(dg_ref, 0, dqn * slab * r)
            return dx

        for s in range(4):
            sl = slice(s * 128, (s + 1) * 128)
            dp_ref[:, sl] = slab_bwd(p_ref[:, sl], dqr_ref[:, sl], gq_ref[...], dgq_ref).astype(dp_ref.dtype)
        for s in range(2):
            sl = slice(512 + s * 128, 640 + s * 128)
            dp_ref[:, sl] = slab_bwd(p_ref[:, sl], dkr_ref[:, s * 128:(s + 1) * 128], gk_ref[...],
                                     dgk_ref).astype(dp_ref.dtype)
        dp_ref[:, 768:1024] = dvb_ref[...].astype(dp_ref.dtype)
        dp_ref[:, 1024:1536] = (dgu_ref[...] * _gelu_grad(p_ref[:, 1024:1536])).astype(dp_ref.dtype)
        gvp = p_ref[:, 1536:2048]
        gv = _gelu(gvp)
        dgv, dgc = _rms_bwd(dgvn_ref[...], gv, gvn_ref[...], _rms(gv))
        dp_ref[:, 1536:2048] = (dgv * _gelu_grad(gvp)).astype(dp_ref.dtype)
        _acc_rows(dgv_ref, 0, dgc)

    row = lambda w: BS((tm, w), lambda i: (i, 0))
    const = lambda r, w: BS((r, w), lambda i: (0, 0))
    return _pcall(body, name="mixer_pre_bwd", grid=(S // tm,),
                  in_specs=[row(IN_COLS_DUP), row(128), row(128), const(1, 128), const(1, 128), const(1, 512),
                            const(128, 128), row(512), row(256), row(256), row(512), row(512)],
                  out_specs=[row(IN_COLS_DUP), const(8, 128), const(8, 128), const(8, 512)],
                  out_shape=[SDS((S, IN_COLS_DUP), MXU_DTYPE), SDS((8, 128), F32), SDS((8, 128), F32),
                             SDS((8, 512), F32)])(proj, cos, sin, gq, gk, gvn, bmat, dqr, dkr, dvb, dgu, dgvn)


BIG = (("w_in", (1024, 448), True), ("w_out", (256, 1024), False), ("xa_wq", (256, 1024), False),
       ("xa_wkv", (1024, 512), True), ("xa_wo", (256, 1024), False), ("ffn_up", (1024, 1408), True),
       ("ffn_down", (704, 1024), False))
BIG_NAMES = tuple(n for n, _, _ in BIG)
SMALL_VECS = (("mix_norm", 1024), ("q_norm", 64), ("k_norm", 64), ("attn_sinks", 8), ("gmlp_v_norm", 512),
              ("attn_out_norm", 512), ("gmlp_out_norm", 512), ("xa_norm", 1024), ("mem_norm", 1024),
              ("xa_q_norm", 256), ("xa_k_norm", 256), ("ffn_norm", 1024), ("ffn_conv_b", 5632))
SMALL = tuple(n for n, _ in SMALL_VECS) + ("gmlp_bs", "gmlp_ws", "ffn_conv")
WEIGHTS = ("mix_norm", "w_in", "q_norm", "k_norm", "attn_sinks", "gmlp_v_norm", "gmlp_ws", "gmlp_bs",
           "attn_out_norm", "gmlp_out_norm", "w_out", "xa_norm", "mem_norm", "xa_wq", "xa_wkv", "xa_q_norm",
           "xa_k_norm", "xa_wo", "ffn_norm", "ffn_up", "ffn_conv", "ffn_conv_b", "ffn_down")
CONV_SHARD = (3, 1408)
CONV_LANE_ROWS = CONV_SHARD[1] // 128
CONV_CHIP_ROWS = 40


def _small_rows():
    rows, r = {}, 0
    for n, length in SMALL_VECS:
        rows[n] = r
        r += -(-length // 128)
    r += -r % 8
    rows["gmlp_bs"] = r
    r += 8
    rows["gmlp_ws"] = r
    r += 8 * BLK
    rows["ffn_conv"] = r
    r += N_CHIPS * CONV_CHIP_ROWS
    return rows, r


SMALL_ROW, SMALL_ROWS = _small_rows()


def pack_small(dg_mix, dgq, dgk, dsk, dg_gvn, dg_y, dg_xa, dg_mem, dg_xq, dg_xk, dg_ffn, gcw, dbl, dws):
    def body(mix_ref, q_ref, k_ref, sk_ref, gvn_ref, y_ref, xa_ref, mem_ref, xq_ref, xk_ref, ffn_ref, cw_ref,
             dbl_ref, dws_ref, o_ref):
        o_ref[...] = jnp.zeros_like(o_ref)
        lane = _lane((1, 128))

        def put(name, src_ref, row, lane0, length):
            for k in range(length // 128):
                o_ref[SMALL_ROW[name] + k:SMALL_ROW[name] + k + 1, :] = src_ref[row:row + 1, lane0 + k * 128:lane0 + (k + 1) * 128]

        put("mix_norm", mix_ref, 0, 0, 1024)
        for name, ref in (("q_norm", q_ref), ("k_norm", k_ref)):
            v = ref[0:1, :]
            o_ref[SMALL_ROW[name]:SMALL_ROW[name] + 1, :] = jnp.where(lane < HEAD_DIM, v + pltpu.roll(v, 64, 1), 0.0)
        sinks = jnp.zeros((1, 128), F32)
        for s in range(4):
            col = sk_ref[s]
            sinks = sinks + jnp.where(lane == 2 * s, jnp.sum(col[:BLK]), 0.0) + jnp.where(lane == 2 * s + 1, jnp.sum(col[BLK:]), 0.0)
        o_ref[SMALL_ROW["attn_sinks"]:SMALL_ROW["attn_sinks"] + 1, :] = sinks
        put("gmlp_v_norm", gvn_ref, 0, 0, 512)
        put("attn_out_norm", y_ref, 0, 0, 512)
        put("gmlp_out_norm", y_ref, 0, 512, 512)
        put("xa_norm", xa_ref, 0, 0, 1024)
        put("mem_norm", mem_ref, 0, 0, 1024)
        put("xa_q_norm", xq_ref, 0, 0, 256)
        put("xa_k_norm", xk_ref, 0, 0, 256)
        put("ffn_norm", ffn_ref, 0, 0, 1024)
        put("ffn_conv_b", cw_ref, 3, 0, 2 * D_FF)
        r8 = lax.broadcasted_iota(jnp.int32, (8, 128), 0)
        l8 = _lane((8, 128))
        bs = jnp.zeros((8, BLK), F32)
        for j in range(4):
            sel = (((r8 == 2 * j) & (l8 < 64)) | ((r8 == 2 * j + 1) & (l8 >= 64))).astype(F32).astype(BF16)
            xj = dbl_ref[j]
            hi = xj.astype(BF16)
            lo = (xj - hi.astype(F32)).astype(BF16)
            bs = bs + lax.dot_general(sel, hi, NT, preferred_element_type=F32) + lax.dot_general(sel, lo, NT, preferred_element_type=F32)
        o_ref[SMALL_ROW["gmlp_bs"]:SMALL_ROW["gmlp_bs"] + 8, :] = bs
        causal = lax.broadcasted_iota(jnp.int32, (BLK, BLK), 0) >= lax.broadcasted_iota(jnp.int32, (BLK, BLK), 1)
        for h in range(8):
            r0 = SMALL_ROW["gmlp_ws"] + h * BLK
            o_ref[r0:r0 + BLK, :] = jnp.where(causal, dws_ref[h], 0.0)
        for q in range(N_CHIPS):
            for j in range(3):
                for k in range(CONV_LANE_ROWS):
                    r0 = SMALL_ROW["ffn_conv"] + q * CONV_CHIP_ROWS + j * CONV_LANE_ROWS + k
                    l0 = (q * CONV_LANE_ROWS + k) * 128
                    o_ref[r0:r0 + 1, :] = cw_ref[j:j + 1, l0:l0 + 128]

    args = (dg_mix, dgq, dgk, dsk, dg_gvn, dg_y, dg_xa, dg_mem, dg_xq, dg_xk, dg_ffn, gcw, dbl, dws)
    full = lambda a: BS(a.shape, lambda i, nd=a.ndim: (0,) * nd)
    return _pcall(body, name="pack_small", grid=(1,), in_specs=[full(a) for a in args],
                  out_specs=BS((SMALL_ROWS, 128), lambda i: (0, 0)), out_shape=SDS((SMALL_ROWS, 128), F32))(*args)


def _adam(w, g, m, v):
    mn = ADAM_B1 * m + (1.0 - ADAM_B1) * g
    vn = ADAM_B2 * v + (1.0 - ADAM_B2) * (g * g)
    m_hat = mn / (1.0 - ADAM_B1 ** ADAM_STEP)
    v_hat = vn / (1.0 - ADAM_B2 ** ADAM_STEP)
    return -ADAM_LR * (m_hat / (jnp.sqrt(v_hat) + ADAM_EPS) + ADAM_WD * w), mn, vn


def adamw_small(gsum, w, m, v, chipvec):
    n = len(SMALL)

    def body(chip_ref, g_ref, *refs):
        w_refs, m_refs, v_refs = refs[:n], refs[n:2 * n], refs[2 * n:3 * n]
        outs = refs[3 * n:]
        go, do, mo, vo = outs[:n], outs[n:2 * n], outs[2 * n:3 * n], outs[3 * n:]

        def update(i, idx, g):
            d, mn, vn = _adam(w_refs[i][idx], g, m_refs[i][idx], v_refs[i][idx])
            go[i][idx] = g
            do[i][idx] = d
            mo[i][idx] = mn
            vo[i][idx] = vn

        for i, (name, length) in enumerate(SMALL_VECS):
            for k in range(-(-length // 128)):
                wd = min(128, length - k * 128)
                r = SMALL_ROW[name] + k
                update(i, (slice(0, 1), slice(k * 128, k * 128 + wd)), g_ref[r:r + 1, 0:wd])
        i_bs, i_ws, i_cv = len(SMALL_VECS), len(SMALL_VECS) + 1, len(SMALL_VECS) + 2
        update(i_bs, (0,), g_ref[SMALL_ROW["gmlp_bs"]:SMALL_ROW["gmlp_bs"] + 8, :])
        for h in range(8):
            r0 = SMALL_ROW["gmlp_ws"] + h * BLK
            update(i_ws, (0, h), g_ref[r0:r0 + BLK, :])
        mine = g_ref[pl.ds(pl.multiple_of(SMALL_ROW["ffn_conv"] + chip_ref[0] * CONV_CHIP_ROWS, 8), CONV_CHIP_ROWS), :]
        for j in range(3):
            for k in range(CONV_LANE_ROWS):
                r = j * CONV_LANE_ROWS + k
                update(i_cv, (0, slice(j, j + 1), slice(k * 128, (k + 1) * 128)), mine[r:r + 1, :])

    nat = [w[nm] for nm in SMALL]
    full = lambda a: BS(a.shape, lambda i, c, nd=a.ndim: (0,) * nd)
    outs = _pcall(body, name="adamw_small", grid=(1,), prefetch=1,
                  in_specs=[BS((SMALL_ROWS, 128), lambda i, c: (0, 0))] + [full(a) for a in nat] * 3,
                  out_specs=[full(a) for a in nat] * 4, out_shape=[SDS(a.shape, F32) for a in nat] * 4)(
        chipvec, gsum, *nat, *[m[nm] for nm in SMALL], *[v[nm] for nm in SMALL])
    return outs[:n], outs[n:2 * n], outs[2 * n:3 * n], outs[3 * n:]


def adamw_matrix(w, m, v, g_own, g_other, cvec, *, name):
    _, r, c = w.shape
    half = r // 2
    tr = _tile(half, (128, 176))
    T = half // tr

    def body(c_ref, w_ref, m_ref, v_ref, own_ref, oth_ref, g_ref, d_ref, mo_ref, vo_ref):
        g = jnp.where(pl.program_id(0) == c_ref[0], own_ref[...], oth_ref[...])
        d, mn, vn = _adam(w_ref[...], g, m_ref[...], v_ref[...])
        g_ref[...] = g
        d_ref[...] = d
        mo_ref[...] = mn
        vo_ref[...] = vn

    nat = BS((None, tr, c), lambda hf, t, cr: (0, hf * T + t, 0))
    hlf = BS((tr, c), lambda hf, t, cr: (t, 0))
    return _pcall(body, name=name, grid=(2, T), prefetch=1, in_specs=[nat, nat, nat, hlf, hlf], out_specs=[nat] * 4,
                  out_shape=[SDS(w.shape, F32)] * 4)(cvec, w, m, v, g_own, g_other)


def _place():
    return lax.axis_index("x"), lax.axis_index("y"), lax.axis_index("c")


def _other_chips(x, y):
    return [(1 - x, y), (x, 1 - y), (1 - x, 1 - y)]


def _rows_of_core(c, half):
    return pl.ds(pl.multiple_of(c * half, 16), half)


def _rcopy(src, dst, sems, k, to):
    return pltpu.make_async_remote_copy(src_ref=src, dst_ref=dst, send_sem=sems[0].at[k], recv_sem=sems[1].at[k],
                                        device_id=to, device_id_type=MESH)


def _comm_call(body, *, name, out_shape, n_in, n_sems, aliases=None):
    return pl.pallas_call(body, name=name, out_shape=out_shape, in_specs=[ANY] * n_in, out_specs=[ANY] * len(out_shape),
                          scratch_shapes=[pltpu.SemaphoreType.DMA((n_sems,)), pltpu.SemaphoreType.DMA((n_sems,))],
                          input_output_aliases=aliases or {},
                          compiler_params=pltpu.CompilerParams(has_side_effects=True))


def cast_shards(shards, conv, chipvec):
    n = len(shards)

    def body(chip_ref, *refs):
        for i_ref, o_ref in zip(refs[:n + 1], refs[n + 1:]):
            o_ref[...] = i_ref[...].astype(o_ref.dtype)

    in_specs = [BS((s.shape[0] // 4, s.shape[1]), lambda i, p: (i, 0)) for s in shards]
    in_specs.append(BS(conv.shape, lambda i, p: (0, 0)))
    out_specs = [BS((None, s.shape[0] // 4, s.shape[1]), lambda i, p: (p[0], i, 0)) for s in shards]
    out_specs.append(BS((None,) + conv.shape, lambda i, p: (p[0], 0, 0)))
    out_shape = [SDS((N_CHIPS,) + s.shape, MXU_DTYPE) for s in shards] + [SDS((N_CHIPS,) + conv.shape, F32)]
    return _pcall(body, name="cast_shards", grid=(4,), prefetch=1, in_specs=in_specs, out_specs=out_specs,
                  out_shape=out_shape)(chipvec, *shards, conv)


def gather_weights(slots):
    n = len(slots) - 1
    first_base, pass_base = 0, 3 * (n + 1)

    def body(*refs):
        outs, sems = refs[n + 1:2 * n + 2], refs[2 * n + 2:]
        x, y, c = _place()
        p = 2 * x + y
        sibling = (x, y, 1 - c)
        chips = _other_chips(x, y)
        first, passed = [], []
        for i, o in enumerate(outs):
            rows = _rows_of_core(c, o.shape[1] // 2) if i < n else slice(None)
            for j, (cx, cy) in enumerate(chips):
                first.append(_rcopy(o.at[p, rows], o.at[p, rows], sems, first_base + 3 * i + j, (cx, cy, c)))
        for cp in first:
            cp.start()
        for i, o in enumerate(outs):
            rows = _rows_of_core(c, o.shape[1] // 2) if i < n else slice(None)
            for j, (cx, cy) in enumerate(chips):
                slab = o.at[2 * cx + cy, rows]
                _rcopy(slab, slab, sems, first_base + 3 * i + j, (cx, cy, c)).wait_recv()
                if i < n:
                    fw = _rcopy(slab, slab, sems, pass_base + 3 * i + j, sibling)
                    fw.start()
                    passed.append(fw)
        for i, o in enumerate(outs[:n]):
            rows = _rows_of_core(1 - c, o.shape[1] // 2)
            for j, (cx, cy) in enumerate(chips):
                slab = o.at[2 * cx + cy, rows]
                _rcopy(slab, slab, sems, pass_base + 3 * i + j, sibling).wait_recv()
        for cp in first + passed:
            cp.wait_send()

    return _comm_call(body, name="gather_weights", out_shape=[SDS(s.shape, s.dtype) for s in slots], n_in=n + 1,
                      n_sems=6 * n + 3, aliases={i: i for i in range(n + 1)})(*slots)


def pair_exchange(gs):
    n = len(gs)

    def body(*refs):
        g_refs, r_refs, sems = refs[:n], refs[n:2 * n], refs[2 * n:]
        x, y, c = _place()
        cps = [_rcopy(g.at[:, _rows_of_core(1 - c, g.shape[1] // 2)], r, sems, i, (x, y, 1 - c))
               for i, (g, r) in enumerate(zip(g_refs, r_refs))]
        for cp in cps:
            cp.start()
        for cp in cps:
            cp.wait()

    return _comm_call(body, name="pair_exchange", n_in=n, n_sems=n,
                      out_shape=[SDS((g.shape[0], g.shape[1] // 2, g.shape[2]), g.dtype) for g in gs])(*gs)


def pair_add(gs, rs, cvec):
    n = len(gs)

    def body(c_ref, *refs):
        for g_ref, r_ref, o_ref in zip(refs[:n], refs[n:2 * n], refs[2 * n:]):
            o_ref[...] = (g_ref[...].astype(F32) + r_ref[...].astype(F32)).astype(o_ref.dtype)

    g4 = [g.reshape(g.shape[0], 2, g.shape[1] // 2, g.shape[2]) for g in gs]
    return _pcall(body, name="pair_add", grid=(N_CHIPS,), prefetch=1,
                  in_specs=[BS((None, None) + g.shape[2:], lambda q, cr: (q, cr[0], 0, 0)) for g in g4]
                  + [BS((None,) + r.shape[1:], lambda q, cr: (q, 0, 0)) for r in rs],
                  out_specs=[BS((None,) + r.shape[1:], lambda q, cr: (q, 0, 0)) for r in rs],
                  out_shape=[SDS(r.shape, r.dtype) for r in rs])(cvec, *g4, *rs)


def scatter_partials(ps):
    n = len(ps)

    def body(*refs):
        p_refs, r_refs, sems = refs[:n], refs[n:2 * n], refs[2 * n:]
        x, y, c = _place()
        p = 2 * x + y
        chips = _other_chips(x, y)
        sends = [_rcopy(pr.at[2 * cx + cy], rr.at[p], sems, 3 * i + j, (cx, cy, c))
                 for i, (pr, rr) in enumerate(zip(p_refs, r_refs)) for j, (cx, cy) in enumerate(chips)]
        for cp in sends:
            cp.start()
        for i, (pr, rr) in enumerate(zip(p_refs, r_refs)):
            for j, (cx, cy) in enumerate(chips):
                _rcopy(pr.at[p], rr.at[2 * cx + cy], sems, 3 * i + j, (cx, cy, c)).wait_recv()
        for cp in sends:
            cp.wait_send()

    return _comm_call(body, name="scatter_partials", n_in=n, n_sems=3 * n,
                      out_shape=[SDS(p.shape, p.dtype) for p in ps])(*ps)


def sum_chips(ps, rs, order):
    n = len(ps)

    def body(o_ref, *refs):
        j = pl.program_id(0)
        for p_ref, r_ref, f_ref in zip(refs[:n], refs[n:2 * n], refs[2 * n:]):
            @pl.when(j == 0)
            def _():
                f_ref[...] = p_ref[...].astype(F32) + r_ref[...].astype(F32)

            @pl.when(j > 0)
            def _():
                f_ref[...] += r_ref[...].astype(F32)

    return _pcall(body, name="sum_chips", grid=(N_CHIPS - 1,), prefetch=1,
                  in_specs=[BS((None,) + p.shape[1:], lambda j, o: (o[0], 0, 0)) for p in ps]
                  + [BS((None,) + r.shape[1:], lambda j, o: (o[j + 1], 0, 0)) for r in rs],
                  out_specs=[BS(p.shape[1:], lambda j, o: (0, 0)) for p in ps],
                  out_shape=[SDS(p.shape[1:], F32) for p in ps])(order, *ps, *rs)


def pair_share(fs):
    n = len(fs)

    def body(*refs):
        f_refs, o_refs, sems = refs[:n], refs[n:2 * n], refs[2 * n:]
        x, y, c = _place()
        cps = [_rcopy(f, o, sems, i, (x, y, 1 - c)) for i, (f, o) in enumerate(zip(f_refs, o_refs))]
        for cp in cps:
            cp.start()
        for cp in cps:
            cp.wait()

    return _comm_call(body, name="pair_share", n_in=n, n_sems=n, out_shape=[SDS(f.shape, f.dtype) for f in fs])(*fs)


def gather_all(sm):
    rows, width = sm.shape

    def body(s_ref, o_ref, send_sems, recv_sems, local_sem):
        x, y, c = _place()
        me = 4 * x + 2 * y + c
        sems = (send_sems, recv_sems)
        mine = pltpu.make_async_copy(s_ref, o_ref.at[me], local_sem)
        mine.start()
        peers = [(1 - x if k & 4 else x, 1 - y if k & 2 else y, 1 - c if k & 1 else c) for k in range(1, N_DEV)]
        sends = [_rcopy(s_ref, o_ref.at[me], sems, k, peer) for k, peer in enumerate(peers)]
        for cp in sends:
            cp.start()
        for k, (px, py, pc) in enumerate(peers):
            _rcopy(s_ref, o_ref.at[4 * px + 2 * py + pc], sems, k, (px, py, pc)).wait_recv()
        for cp in sends:
            cp.wait_send()
        mine.wait()

    return pl.pallas_call(body, name="gather_all", out_shape=SDS((N_DEV, rows, width), sm.dtype), in_specs=[ANY],
                          out_specs=ANY,
                          scratch_shapes=[pltpu.SemaphoreType.DMA((N_DEV - 1,)), pltpu.SemaphoreType.DMA((N_DEV - 1,)),
                                          pltpu.SemaphoreType.DMA],
                          compiler_params=pltpu.CompilerParams(has_side_effects=True))(sm)


def sum_slots(r, *, name):
    n, rows, width = r.shape
    tr = _tile(rows, (184, 8))

    def body(r_ref, o_ref):
        acc = r_ref[0]
        for s in range(1, n):
            acc = acc + r_ref[s]
        o_ref[...] = acc

    return _pcall(body, name=name, grid=(rows // tr,), in_specs=[BS((n, tr, width), lambda i: (0, i, 0))],
                  out_specs=BS((tr, width), lambda i: (i, 0)), out_shape=SDS((rows, width), F32))(r)


def _to_full(blk, col):
    n, r, c = blk.shape
    return blk.transpose(1, 0, 2).reshape(r, n * c) if col else blk.reshape(n * r, c)


def _dup_cols(w):
    dup = lambda t: jnp.concatenate([t[:, :64], t[:, :64], t[:, 64:], t[:, 64:]], axis=1)
    return jnp.concatenate([w[:, :512], dup(w[:, 512:640]), dup(w[:, 640:768]), w[:, 768:]], axis=1)


def _fold_cols(d):
    fold = lambda t: jnp.concatenate([t[:, 0:64] + t[:, 64:128], t[:, 128:192] + t[:, 192:256]], axis=1)
    return jnp.concatenate([d[:, :512], fold(d[:, 512:768]), fold(d[:, 768:1024]), d[:, 1024:]], axis=1)


def _local_step(x, mem, positions, target, wf, sp):
    gain = lambda n: sp[n].reshape(1, -1)
    half = HEAD_DIM // 2
    inv_freq = 1.0 / (10000.0 ** (jnp.arange(half, dtype=F32) * (2.0 / HEAD_DIM)))
    ang = positions.astype(F32)[:, None] * inv_freq
    cos, sin = jnp.cos(ang), jnp.sin(ang)
    cos128 = jnp.tile(cos, (1, 4))
    sin128 = jnp.concatenate([-sin, sin, -sin, sin], axis=1)
    seg = jnp.arange(128) // HEAD_DIM
    bmat = (seg[:, None] == seg[None, :]).astype(BF16)
    gq128, gk128 = jnp.tile(gain("q_norm"), (1, 2)), jnp.tile(gain("k_norm"), (1, 2))
    sinkcol = jnp.repeat(sp["attn_sinks"].reshape(4, 2), BLK, axis=1).reshape(4, 2 * BLK, 1)
    wsc = sp["gmlp_ws"] * jnp.tril(jnp.ones((BLK, BLK), F32))[None]
    w2 = wsc.reshape(4, 2 * BLK, BLK).astype(MXU_DTYPE)
    w2t = wsc.swapaxes(1, 2).reshape(4, 2 * BLK, BLK).astype(MXU_DTYPE)
    bsl = jnp.repeat(sp["gmlp_bs"].reshape(4, 2, BLK).transpose(0, 2, 1), HEAD_DIM, axis=2)
    cw, cb = sp["ffn_conv"], sp["ffn_conv_b"].reshape(1, -1)
    w_in_d = _dup_cols(_to_full(wf["w_in"], True))[None]
    w_out, xa_wq, xa_wo, ffn_down = (_to_full(wf[n], False) for n in ("w_out", "xa_wq", "xa_wo", "ffn_down"))

    h1, proj = rms_mm(x, gain("mix_norm"), w_in_d, name="mix_in")
    qr, kr, vb, gu, gvn = mixer_pre(proj, cos128, sin128, gq128, gk128, gain("gmlp_v_norm"), bmat)
    attn, ya = swa_fwd(qr, kr, vb, sinkcol, gain("attn_out_norm"))
    gm, y = gmlp_fwd(gvn, gu, ya, w2, bsl, gain("gmlp_out_norm"))
    x1 = mm(y, w_out, res=x, name="mix_out")
    h2, qx = rms_mm(x1, gain("xa_norm"), xa_wq[None], name="xa_q")
    mn, kv = rms_mm(mem, gain("mem_norm"), wf["xa_wkv"], name="xa_kv")
    kn, vbx = mem_pre(kv, gain("xa_k_norm"))
    xo = xattn_fwd(qx, kn, vbx, gain("xa_q_norm"))
    x2 = mm(xo, xa_wo, res=x1, name="xa_out")
    h3, a = rms_mm(x2, gain("ffn_norm"), wf["ffn_up"], name="ffn_up", tm=1024)
    f = convgate_fwd(a, cw, cb)
    x3 = mm(f, ffn_down, res=x2, name="ffn_down")
    dx3, loss_acc = loss_head(x3, target)

    gbig = {}
    by_rows = lambda g: g.reshape(N_CHIPS, g.shape[1] // N_CHIPS, g.shape[2])
    df = mm_nt(dx3, ffn_down[None], name="d_f")
    gbig["ffn_down"] = by_rows(mm_tn(f, dx3, name="g_ffn_down", out_dtype=WIRE_DTYPE))
    dc, gcw = convgate_bwd(a, df, cw, cb)
    da = conv_transpose(dc, cw)
    dh3 = mm_nt(da, wf["ffn_up"], name="d_h3")
    gbig["ffn_up"] = mm_tn(h3, da, name="g_ffn_up", out_dtype=WIRE_DTYPE, chunks=N_CHIPS)
    dx2, dg_ffn = rms_bwd(dh3, x2, gain("ffn_norm"), dx3, name="ffn_norm_bwd")
    dxo = mm_nt(dx2, xa_wo[None], name="d_xo")
    gbig["xa_wo"] = by_rows(mm_tn(xo, dx2, name="g_xa_wo", out_dtype=WIRE_DTYPE))
    dqx, dkn, dvx, dg_xq = xattn_bwd(qx, dxo, kn, vbx, gain("xa_q_norm"))
    dh2 = mm_nt(dqx, xa_wq[None], name="d_h2")
    gbig["xa_wq"] = by_rows(mm_tn(h2, dqx, name="g_xa_wq", out_dtype=WIRE_DTYPE))
    dx1, dg_xa = rms_bwd(dh2, x1, gain("xa_norm"), dx2, name="xa_norm_bwd")
    dkv, dg_xk = mem_bwd(kv, dkn, dvx, gain("xa_k_norm"))
    dmn = mm_nt(dkv, wf["xa_wkv"], name="d_mn")
    gbig["xa_wkv"] = mm_tn(mn, dkv, name="g_xa_wkv", out_dtype=WIRE_DTYPE, chunks=N_CHIPS)
    _, dg_mem = rms_bwd(dmn, mem, gain("mem_norm"), jnp.zeros_like(mem), name="mem_norm_bwd")
    dy = mm_nt(dx1, w_out[None], name="d_y")
    gbig["w_out"] = by_rows(mm_tn(y, dx1, name="g_w_out", out_dtype=WIRE_DTYPE))
    dattn, dgm, dg_y = mixer_post_bwd(dy, attn, gm, gain("attn_out_norm"), gain("gmlp_out_norm"))
    dgu, dgvn, dws, dbl = gmlp_bwd(dgm, gvn, gu, w2, w2t, bsl)
    dqr, dkr, dvb, dsk = swa_bwd(qr, kr, vb, sinkcol, dattn)
    dproj, dgq, dgk, dg_gvn = mixer_pre_bwd(proj, cos128, sin128, gq128, gk128, gain("gmlp_v_norm"), bmat,
                                            dqr, dkr, dvb, dgu, dgvn)
    dh1 = mm_nt(dproj, w_in_d, name="d_h1")
    g_in = _fold_cols(mm_tn(h1, dproj, name="g_w_in", out_dtype=F32)[0])
    gbig["w_in"] = g_in.reshape(1024, N_CHIPS, 448).transpose(1, 0, 2).astype(WIRE_DTYPE)
    grad_x, dg_mix = rms_bwd(dh1, x, gain("mix_norm"), dx1, name="mix_norm_bwd")
    packed = pack_small(dg_mix, dgq, dgk, dsk, dg_gvn, dg_y, dg_xa, dg_mem, dg_xq, dg_xk, dg_ffn, gcw, dbl, dws)
    return loss_acc, grad_x, gbig, packed


def _gather_step(w, chipvec):
    slots = cast_shards([w[n][0] for n in BIG_NAMES], w["ffn_conv"][0], chipvec)
    slots = gather_weights(slots)
    return dict(zip(BIG_NAMES, slots[:-1])), _to_full(slots[-1], True)


def _reduce_update(gbig, packed, w, m, v, chipvec, cvec, order):
    parts = [gbig[n] for n in BIG_NAMES]
    pair = pair_add(parts, pair_exchange(parts), cvec)
    own = sum_chips(pair, scatter_partials(pair), order)
    other = pair_share(own)
    res = [{}, {}, {}, {}]
    for n, g_own, g_other in zip(BIG_NAMES, own, other):
        for d, o in zip(res, adamw_matrix(w[n], m[n], v[n], g_own, g_other, cvec, name="adamw_" + n)):
            d[n] = o
    small_sum = sum_slots(gather_all(packed), name="sum_small")
    for d, outs in zip(res, adamw_small(small_sum, w, m, v, chipvec)):
        d.update(zip(SMALL, outs))
    return res


def kernel(x, mem, positions, mix_norm, w_in, q_norm, k_norm, attn_sinks, gmlp_v_norm, gmlp_ws, gmlp_bs, attn_out_norm, gmlp_out_norm, w_out, xa_norm, mem_norm, xa_wq, xa_wkv, xa_q_norm, xa_k_norm, xa_wo, ffn_norm, ffn_up, ffn_conv, ffn_conv_b, ffn_down, loss_target, m_mix_norm, m_w_in, m_q_norm, m_k_norm, m_attn_sinks, m_gmlp_v_norm, m_gmlp_ws, m_gmlp_bs, m_attn_out_norm, m_gmlp_out_norm, m_w_out, m_xa_norm, m_mem_norm, m_xa_wq, m_xa_wkv, m_xa_q_norm, m_xa_k_norm, m_xa_wo, m_ffn_norm, m_ffn_up, m_ffn_conv, m_ffn_conv_b, m_ffn_down, v_mix_norm, v_w_in, v_q_norm, v_k_norm, v_attn_sinks, v_gmlp_v_norm, v_gmlp_ws, v_gmlp_bs, v_attn_out_norm, v_gmlp_out_norm, v_w_out, v_xa_norm, v_mem_norm, v_xa_wq, v_xa_wkv, v_xa_q_norm, v_xa_k_norm, v_xa_wo, v_ffn_norm, v_ffn_up, v_ffn_conv, v_ffn_conv_b, v_ffn_down):
    w = dict(mix_norm=mix_norm, w_in=w_in, q_norm=q_norm, k_norm=k_norm, attn_sinks=attn_sinks, gmlp_v_norm=gmlp_v_norm, gmlp_ws=gmlp_ws, gmlp_bs=gmlp_bs, attn_out_norm=attn_out_norm, gmlp_out_norm=gmlp_out_norm, w_out=w_out, xa_norm=xa_norm, mem_norm=mem_norm, xa_wq=xa_wq, xa_wkv=xa_wkv, xa_q_norm=xa_q_norm, xa_k_norm=xa_k_norm, xa_wo=xa_wo, ffn_norm=ffn_norm, ffn_up=ffn_up, ffn_conv=ffn_conv, ffn_conv_b=ffn_conv_b, ffn_down=ffn_down)
    m = dict(mix_norm=m_mix_norm, w_in=m_w_in, q_norm=m_q_norm, k_norm=m_k_norm, attn_sinks=m_attn_sinks, gmlp_v_norm=m_gmlp_v_norm, gmlp_ws=m_gmlp_ws, gmlp_bs=m_gmlp_bs, attn_out_norm=m_attn_out_norm, gmlp_out_norm=m_gmlp_out_norm, w_out=m_w_out, xa_norm=m_xa_norm, mem_norm=m_mem_norm, xa_wq=m_xa_wq, xa_wkv=m_xa_wkv, xa_q_norm=m_xa_q_norm, xa_k_norm=m_xa_k_norm, xa_wo=m_xa_wo, ffn_norm=m_ffn_norm, ffn_up=m_ffn_up, ffn_conv=m_ffn_conv, ffn_conv_b=m_ffn_conv_b, ffn_down=m_ffn_down)
    v = dict(mix_norm=v_mix_norm, w_in=v_w_in, q_norm=v_q_norm, k_norm=v_k_norm, attn_sinks=v_attn_sinks, gmlp_v_norm=v_gmlp_v_norm, gmlp_ws=v_gmlp_ws, gmlp_bs=v_gmlp_bs, attn_out_norm=v_attn_out_norm, gmlp_out_norm=v_gmlp_out_norm, w_out=v_w_out, xa_norm=v_xa_norm, mem_norm=v_mem_norm, xa_wq=v_xa_wq, xa_wkv=v_xa_wkv, xa_q_norm=v_xa_q_norm, xa_k_norm=v_xa_k_norm, xa_wo=v_xa_wo, ffn_norm=v_ffn_norm, ffn_up=v_ffn_up, ffn_conv=v_ffn_conv, ffn_conv_b=v_ffn_conv_b, ffn_down=v_ffn_down)
    ix, iy, ic = lax.axis_index("x"), lax.axis_index("y"), lax.axis_index("c")
    chip = 2 * ix + iy
    chipvec = chip.astype(jnp.int32).reshape(1)
    cvec = ic.astype(jnp.int32).reshape(1)
    order = jnp.stack([chip, 2 * (1 - ix) + iy, 2 * ix + (1 - iy), 2 * (1 - ix) + (1 - iy)]).astype(jnp.int32)

    wf, conv_full = _gather_step(w, chipvec)
    sp = {n: w[n][0] for n in SMALL}
    sp["ffn_conv"] = conv_full
    loss_acc, grad_x, gbig, packed = _local_step(x[0], mem[0], positions[0], loss_target[0], wf, sp)
    grads, delta, new_m, new_v = _reduce_update(gbig, packed, w, m, v, chipvec, cvec, order)
    loss = lax.psum(loss_acc[0, 0], ("x", "y", "c"))
    ordered = lambda d: [d[n] for n in WEIGHTS]
    return (loss, grad_x[None], *ordered(grads), *ordered(delta), *ordered(new_m), *ordered(new_v))
```

```python
import math

import jax
import jax.numpy as jnp
from jax import lax
from jax.experimental import pallas as pl
from jax.experimental.pallas import tpu as pltpu

F32 = jnp.float32
BF16 = jnp.bfloat16
MXU_DTYPE = jnp.bfloat16
WIRE_DTYPE = jnp.bfloat16
EPS = 1e-6
VMEM_LIMIT_V7X = 56 * 1024 * 1024

D_MODEL = 1024
HEAD_DIM = 64
BLK = 128
XA_HEADS = 4
XA_DH = 256
MEM_LEN = 256
D_FF = 2816
IN_COLS_DUP = 2048
N_CHIPS = 4
N_DEV = 8

ADAM_LR = 0.001
ADAM_B1 = 0.9
ADAM_B2 = 0.999
ADAM_EPS = 1e-08
ADAM_WD = 0.01
ADAM_STEP = 10

NT = (((1,), (1,)), ((), ()))
TN = (((0,), (0,)), ((), ()))
NN = (((1,), (0,)), ((), ()))
MINF = float(jnp.finfo(jnp.float32).min)
GELU_K0 = math.sqrt(2.0 / math.pi)
GELU_K1 = 0.044715

BS = pl.BlockSpec
SDS = jax.ShapeDtypeStruct
ANY = pl.BlockSpec(memory_space=pl.ANY)
MESH = pl.DeviceIdType.MESH


def _dot(a, b, dims=NN):
    return lax.dot_general(a.astype(MXU_DTYPE), b.astype(MXU_DTYPE), dims, preferred_element_type=F32)


def _segsum(x, bmat):
    hi = x.astype(BF16)
    lo = (x - hi.astype(F32)).astype(BF16)
    return (jnp.dot(hi, bmat, preferred_element_type=F32) + jnp.dot(lo, bmat, preferred_element_type=F32))


def _gelu(x):
    return 0.5 * x * (1.0 + jnp.tanh(GELU_K0 * (x + GELU_K1 * x * x * x)))


def _gelu_grad(x):
    t = jnp.tanh(GELU_K0 * (x + GELU_K1 * x * x * x))
    return 0.5 * (1.0 + t) + 0.5 * x * (1.0 - t * t) * GELU_K0 * (1.0 + 3.0 * GELU_K1 * x * x)


def _rms(x):
    return lax.rsqrt(jnp.mean(x * x, axis=-1, keepdims=True) + EPS)


def _rms_bwd(dy, x, g, r):
    dyg = dy * g
    dx = r * dyg - x * (r * r * r) * jnp.mean(dyg * x, axis=-1, keepdims=True)
    return dx, dy * x * r


def _pcall(body, *, name, grid, in_specs, out_specs, out_shape, scratch=(), prefetch=0):
    params = pltpu.CompilerParams(dimension_semantics=("arbitrary",) * len(grid), vmem_limit_bytes=VMEM_LIMIT_V7X)
    if prefetch:
        spec = pltpu.PrefetchScalarGridSpec(num_scalar_prefetch=prefetch, grid=grid, in_specs=in_specs,
                                            out_specs=out_specs, scratch_shapes=list(scratch))
        return pl.pallas_call(body, name=name, grid_spec=spec, out_shape=out_shape, compiler_params=params)
    return pl.pallas_call(body, name=name, grid=grid, in_specs=in_specs, out_specs=out_specs, out_shape=out_shape,
                          scratch_shapes=list(scratch), compiler_params=params)


def _tile(n, prefs):
    for p in prefs:
        if p <= n and n % p == 0:
            return p
    return n


def _acc_rows(ref, row, val):
    ref[row:row + 1, :] += jnp.sum(val, axis=0, keepdims=True)


def rms_mm(x, g, w3, *, name, tm=512):
    M, K = x.shape
    Q, _, C = w3.shape
    tm = _tile(M, (tm, 256))

    def body(x_ref, g_ref, w_ref, h_ref, o_ref):
        @pl.when(pl.program_id(1) == 0)
        def _():
            xv = x_ref[...]
            h_ref[...] = (xv * _rms(xv) * g_ref[...]).astype(h_ref.dtype)

        o_ref[...] = _dot(h_ref[...], w_ref[pl.program_id(1)])

    return _pcall(body, name=name, grid=(M // tm, Q),
                  in_specs=[BS((tm, K), lambda i, j: (i, 0)), BS((1, K), lambda i, j: (0, 0)),
                            BS((Q, K, C), lambda i, j: (0, 0, 0))],
                  out_specs=[BS((tm, K), lambda i, j: (i, 0)), BS((tm, C), lambda i, j: (i, j))],
                  out_shape=[SDS((M, K), MXU_DTYPE), SDS((M, Q * C), F32)])(x, g, w3)


def mm(a, w, *, name, res):
    M, K = a.shape
    N = w.shape[1]
    tm = _tile(M, (512, 256))

    def body(a_ref, w_ref, r_ref, o_ref):
        o_ref[...] = _dot(a_ref[...], w_ref[...]) + r_ref[...]

    return _pcall(body, name=name, grid=(M // tm,),
                  in_specs=[BS((tm, K), lambda i: (i, 0)), BS((K, N), lambda i: (0, 0)), BS((tm, N), lambda i: (i, 0))],
                  out_specs=BS((tm, N), lambda i: (i, 0)), out_shape=SDS((M, N), F32))(a, w, res)


def mm_nt(a, w3, *, name):
    M = a.shape[0]
    Q, N, Kc = w3.shape
    tm = _tile(M, (512, 256))

    def body(a_ref, w_ref, o_ref):
        acc = _dot(a_ref[:, 0:Kc], w_ref[0], NT)
        for q in range(1, Q):
            acc = acc + _dot(a_ref[:, q * Kc:(q + 1) * Kc], w_ref[q], NT)
        o_ref[...] = acc

    return _pcall(body, name=name, grid=(M // tm,),
                  in_specs=[BS((tm, Q * Kc), lambda i: (i, 0)), BS((Q, N, Kc), lambda i: (0, 0, 0))],
                  out_specs=BS((tm, N), lambda i: (i, 0)), out_shape=SDS((M, N), F32))(a, w3)


def mm_tn(a, b, *, name, out_dtype, chunks=1):
    M, K = a.shape
    N = b.shape[1]
    C = N // chunks
    tm = _tile(M, (512, 256))
    tk = _tile(K, (1408, 1024, 512))
    tn = _tile(C, (1408, 1024, 512))
    per = C // tn
    nm = M // tm

    def body(a_ref, b_ref, o_ref, acc):
        m = pl.program_id(2)

        @pl.when(m == 0)
        def _():
            acc[...] = jnp.zeros_like(acc)

        acc[...] += _dot(a_ref[...], b_ref[...], TN)

        @pl.when(m == nm - 1)
        def _():
            o_ref[...] = acc[...].astype(o_ref.dtype)

    return _pcall(body, name=name, grid=(K // tk, N // tn, nm),
                  in_specs=[BS((tm, tk), lambda k, n, m: (m, k)), BS((tm, tn), lambda k, n, m: (m, n))],
                  out_specs=BS((None, tk, tn), lambda k, n, m: (n // per, k, n % per)),
                  out_shape=SDS((chunks, K, C), out_dtype), scratch=[pltpu.VMEM((tk, tn), F32)])(a, b)


def _lane(shape):
    return lax.broadcasted_iota(jnp.int32, shape, 1)


def _norm_rope(slab, g, bmat, cos, sin, first):
    r = lax.rsqrt(_segsum(slab * slab, bmat) * (1.0 / HEAD_DIM) + EPS)
    qn = slab * r * g
    swapped = jnp.where(first, pltpu.roll(qn, 96, 1), pltpu.roll(qn, 32, 1))
    return qn * cos + swapped * sin


def mixer_pre(proj, cos, sin, gq, gk, gvn, bmat):
    S = proj.shape[0]
    tm = _tile(S, (256,))

    def body(p_ref, c_ref, s_ref, gq_ref, gk_ref, gvn_ref, b_ref, qr_ref, kr_ref, vb_ref, gu_ref, gvo_ref):
        cos_v, sin_v, bm = c_ref[...], s_ref[...], b_ref[...]
        first = (_lane((tm, 128)) & 63) < 32
        for s in range(4):
            sl = slice(s * 128, (s + 1) * 128)
            qr_ref[:, sl] = _norm_rope(p_ref[:, sl], gq_ref[...], bm, cos_v, sin_v, first).astype(qr_ref.dtype)
        for s in range(2):
            kr_ref[:, s * 128:(s + 1) * 128] = _norm_rope(p_ref[:, 512 + s * 128:640 + s * 128], gk_ref[...], bm,
                                                          cos_v, sin_v, first).astype(kr_ref.dtype)
        vb_ref[...] = p_ref[:, 768:1024].astype(vb_ref.dtype)
        gu_ref[...] = _gelu(p_ref[:, 1024:1536])
        gv = _gelu(p_ref[:, 1536:2048])
        gvo_ref[...] = (gv * _rms(gv) * gvn_ref[...]).astype(gvo_ref.dtype)

    row = lambda w: BS((tm, w), lambda i: (i, 0))
    const = lambda r, w: BS((r, w), lambda i: (0, 0))
    return _pcall(body, name="mixer_pre", grid=(S // tm,),
                  in_specs=[row(IN_COLS_DUP), row(128), row(128), const(1, 128), const(1, 128), const(1, 512),
                            const(128, 128)],
                  out_specs=[row(512), row(256), row(256), row(512), row(512)],
                  out_shape=[SDS((S, 512), MXU_DTYPE), SDS((S, 256), MXU_DTYPE), SDS((S, 256), MXU_DTYPE),
                             SDS((S, 512), F32), SDS((S, 512), MXU_DTYPE)])(proj, cos, sin, gq, gk, gvn, bmat)


def _swa_probs(qs, kd, sink, n, lo):
    z = jnp.zeros_like(qs)
    qp = jnp.concatenate([jnp.where(lo, qs, z), jnp.where(lo, z, qs)], axis=0)
    sc = _dot(qp, kd, NT) * (1.0 / math.sqrt(HEAD_DIM))
    r_i = lax.broadcasted_iota(jnp.int32, (2 * BLK, 2 * BLK), 0)
    k_j = lax.broadcasted_iota(jnp.int32, (2 * BLK, 2 * BLK), 1)
    diff = (r_i & (BLK - 1)) + BLK - k_j
    mask = (diff >= 0) & (diff < BLK) & ((k_j >= BLK) | (n > 0))
    sc = jnp.where(mask, sc, MINF)
    m = jnp.maximum(jnp.max(sc, axis=1, keepdims=True), sink)
    p = jnp.exp(sc - m)
    es = jnp.exp(sink - m)
    l = jnp.sum(p, axis=1, keepdims=True) + es
    return qp, p / l, es / l


def swa_fwd(qr, kr, vb, sinkcol, gao):
    S = qr.shape[0]
    nb = S // BLK

    def body(q_ref, kc_ref, kp_ref, vc_ref, vp_ref, sk_ref, g_ref, o_ref, ya_ref):
        n = pl.program_id(0)
        lo = _lane((BLK, 128)) < 64
        for s in range(4):
            h = s // 2
            hs = slice(h * 128, (h + 1) * 128)
            kd = jnp.concatenate([kp_ref[:, hs], kc_ref[:, hs]], axis=0)
            vd = jnp.concatenate([vp_ref[:, hs], vc_ref[:, hs]], axis=0)
            _, p, _ = _swa_probs(q_ref[:, s * 128:(s + 1) * 128], kd, sk_ref[s], n, lo)
            o2 = _dot(p, vd)
            o_ref[:, s * 128:(s + 1) * 128] = jnp.where(lo, o2[:BLK], o2[BLK:])
        a = o_ref[...]
        ya_ref[...] = (a * _rms(a) * g_ref[...]).astype(ya_ref.dtype)

    cur = lambda w: BS((BLK, w), lambda n: (n, 0))
    prev = lambda w: BS((BLK, w), lambda n: (jnp.maximum(n - 1, 0), 0))
    return _pcall(body, name="swa_fwd", grid=(nb,),
                  in_specs=[cur(512), cur(256), prev(256), cur(256), prev(256),
                            BS((4, 2 * BLK, 1), lambda n: (0, 0, 0)), BS((1, 512), lambda n: (0, 0))],
                  out_specs=[cur(512), cur(512)],
                  out_shape=[SDS((S, 512), F32), SDS((S, 512), MXU_DTYPE)])(qr, kr, kr, vb, vb, sinkcol, gao)


def gmlp_fwd(gvn, gu, ya, w2, bsl, ggo):
    S = gvn.shape[0]

    def body(gvn_ref, gu_ref, ya_ref, w2_ref, bsl_ref, g_ref, gm_ref, y_ref):
        lo = _lane((BLK, 128)) < 64
        for j in range(4):
            sl = slice(j * 128, (j + 1) * 128)
            m2 = _dot(w2_ref[j], gvn_ref[:, sl])
            mixed = jnp.where(lo, m2[:BLK], m2[BLK:]) + bsl_ref[j]
            gm_ref[:, sl] = gu_ref[:, sl] * mixed
        gm = gm_ref[...]
        y_ref[:, :512] = ya_ref[...]
        y_ref[:, 512:] = (gm * _rms(gm) * g_ref[...]).astype(y_ref.dtype)

    row = lambda w: BS((BLK, w), lambda n: (n, 0))
    return _pcall(body, name="gmlp_fwd", grid=(S // BLK,),
                  in_specs=[row(512), row(512), row(512), BS((4, 2 * BLK, BLK), lambda n: (0, 0, 0)),
                            BS((4, BLK, 128), lambda n: (0, 0, 0)), BS((1, 512), lambda n: (0, 0))],
                  out_specs=[row(512), row(1024)],
                  out_shape=[SDS((S, 512), F32), SDS((S, 1024), MXU_DTYPE)])(gvn, gu, ya, w2, bsl, ggo)


def mem_pre(kv, gxk):
    def body(kv_ref, g_ref, kn_ref, vb_ref):
        for h in range(XA_HEADS):
            sl = slice(h * XA_DH, (h + 1) * XA_DH)
            k = kv_ref[:, sl]
            kn_ref[:, sl] = (k * _rms(k) * g_ref[...]).astype(kn_ref.dtype)
        vb_ref[...] = kv_ref[:, 1024:2048].astype(vb_ref.dtype)

    full = lambda r, w: BS((r, w), lambda i: (0, 0))
    return _pcall(body, name="mem_pre", grid=(1,), in_specs=[full(MEM_LEN, 2048), full(1, XA_DH)],
                  out_specs=[full(MEM_LEN, 1024), full(MEM_LEN, 1024)],
                  out_shape=[SDS((MEM_LEN, 1024), MXU_DTYPE), SDS((MEM_LEN, 1024), MXU_DTYPE)])(kv, gxk)


def _xa_probs(qh, g, kn_h):
    r = _rms(qh)
    qn = qh * r * g
    s = _dot(qn, kn_h, NT) * (1.0 / math.sqrt(XA_DH))
    p = jnp.exp(s - jnp.max(s, axis=1, keepdims=True))
    return r, qn, p / jnp.sum(p, axis=1, keepdims=True)


def xattn_fwd(qx, kn, vb, gxq):
    S = qx.shape[0]
    tm = _tile(S, (256,))

    def body(q_ref, kn_ref, vb_ref, g_ref, o_ref):
        for h in range(XA_HEADS):
            sl = slice(h * XA_DH, (h + 1) * XA_DH)
            _, _, p = _xa_probs(q_ref[:, sl], g_ref[...], kn_ref[:, sl])
            o_ref[:, sl] = _dot(p, vb_ref[:, sl]).astype(o_ref.dtype)

    full = lambda r, w: BS((r, w), lambda i: (0, 0))
    return _pcall(body, name="xattn_fwd", grid=(S // tm,),
                  in_specs=[BS((tm, 1024), lambda i: (i, 0)), full(MEM_LEN, 1024), full(MEM_LEN, 1024), full(1, XA_DH)],
                  out_specs=BS((tm, 1024), lambda i: (i, 0)), out_shape=SDS((S, 1024), MXU_DTYPE))(qx, kn, vb, gxq)


def _causal_taps(a, halo_ref, first_tile, row):
    h6 = jnp.where(first_tile, 0.0, halo_ref[6:7, :])
    h7 = jnp.where(first_tile, 0.0, halo_ref[7:8, :])
    a1 = jnp.where(row == 0, h7, pltpu.roll(a, 1, 0))
    a2 = jnp.where(row == 0, h6, jnp.where(row == 1, h7, pltpu.roll(a, 2, 0)))
    return a1, a2


def _conv(a, a1, a2, w_ref, b_ref):
    return w_ref[2:3, :] * a + w_ref[1:2, :] * a1 + w_ref[0:1, :] * a2 + b_ref[...]


def _conv_specs(tm):
    halo_blocks = tm // 8
    return [BS((tm, D_FF), lambda i: (i, 0)), BS((tm, D_FF), lambda i: (i, 1)),
            BS((8, D_FF), lambda i: (jnp.maximum(i * halo_blocks - 1, 0), 0)),
            BS((8, D_FF), lambda i: (jnp.maximum(i * halo_blocks - 1, 0), 1)),
            BS((3, D_FF), lambda i: (0, 0)), BS((3, D_FF), lambda i: (0, 1)),
            BS((1, D_FF), lambda i: (0, 0)), BS((1, D_FF), lambda i: (0, 1))]


def convgate_fwd(a, cw, cb):
    S = a.shape[0]
    tm = _tile(S, (256,))

    def body(ag_ref, au_ref, hg_ref, hu_ref, wg_ref, wu_ref, bg_ref, bu_ref, f_ref):
        first_tile = pl.program_id(0) == 0
        row = lax.broadcasted_iota(jnp.int32, (tm, D_FF), 0)
        ag, au = ag_ref[...], au_ref[...]
        cg = _conv(ag, *_causal_taps(ag, hg_ref, first_tile, row), wg_ref, bg_ref)
        cu = _conv(au, *_causal_taps(au, hu_ref, first_tile, row), wu_ref, bu_ref)
        f_ref[...] = (_gelu(cg) * cu).astype(f_ref.dtype)

    return _pcall(body, name="convgate_fwd", grid=(S // tm,), in_specs=_conv_specs(tm),
                  out_specs=BS((tm, D_FF), lambda i: (i, 0)),
                  out_shape=SDS((S, D_FF), MXU_DTYPE))(a, a, a, a, cw, cw, cb, cb)


def loss_head(x3, target):
    S = x3.shape[0]
    tm = _tile(S, (512, 256))

    def body(x_ref, t_ref, d_ref, l_ref):
        @pl.when(pl.program_id(0) == 0)
        def _():
            l_ref[...] = jnp.zeros_like(l_ref)

        e = x_ref[...] - t_ref[...]
        d_ref[...] = e * (1.0 / D_MODEL)
        l_ref[...] += jnp.sum(e * e) * (0.5 / D_MODEL)

    row = BS((tm, D_MODEL), lambda i: (i, 0))
    return _pcall(body, name="loss_head", grid=(S // tm,), in_specs=[row, row],
                  out_specs=[row, BS((8, 128), lambda i: (0, 0))],
                  out_shape=[SDS((S, D_MODEL), F32), SDS((8, 128), F32)])(x3, target)


def convgate_bwd(a, df, cw, cb):
    S = a.shape[0]
    tm = _tile(S, (128,))

    def body(ag_ref, au_ref, hg_ref, hu_ref, wg_ref, wu_ref, bg_ref, bu_ref, df_ref, dc_ref, gw_ref):
        first_tile = pl.program_id(0) == 0

        @pl.when(first_tile)
        def _():
            gw_ref[...] = jnp.zeros_like(gw_ref)

        row = lax.broadcasted_iota(jnp.int32, (tm, D_FF), 0)
        ag, au, df_v = ag_ref[...], au_ref[...], df_ref[...]
        ag1, ag2 = _causal_taps(ag, hg_ref, first_tile, row)
        au1, au2 = _causal_taps(au, hu_ref, first_tile, row)
        cg = _conv(ag, ag1, ag2, wg_ref, bg_ref)
        cu = _conv(au, au1, au2, wu_ref, bu_ref)
        dcg = df_v * cu * _gelu_grad(cg)
        dcu = df_v * _gelu(cg)
        dc_ref[:, :D_FF] = dcg
        dc_ref[:, D_FF:] = dcu
        for col, dcv, taps in ((slice(0, D_FF), dcg, (ag2, ag1, ag)), (slice(D_FF, 2 * D_FF), dcu, (au2, au1, au))):
            for j in range(3):
                gw_ref[j:j + 1, col] += jnp.sum(dcv * taps[j], axis=0, keepdims=True)
            gw_ref[3:4, col] += jnp.sum(dcv, axis=0, keepdims=True)

    return _pcall(body, name="convgate_bwd", grid=(S // tm,),
                  in_specs=_conv_specs(tm) + [BS((tm, D_FF), lambda i: (i, 0))],
                  out_specs=[BS((tm, 2 * D_FF), lambda i: (i, 0)), BS((8, 2 * D_FF), lambda i: (0, 0))],
                  out_shape=[SDS((S, 2 * D_FF), F32), SDS((8, 2 * D_FF), F32)])(a, a, a, a, cw, cw, cb, cb, df)


def conv_transpose(dc, cw):
    S, C = dc.shape
    tm = _tile(S, (128,))
    nt = S // tm
    halo_blocks = tm // 8

    def body(dc_ref, halo_ref, w_ref, da_ref):
        last_tile = pl.program_id(0) == nt - 1
        row = lax.broadcasted_iota(jnp.int32, (tm, C), 0)
        h0 = jnp.where(last_tile, 0.0, halo_ref[0:1, :])
        h1 = jnp.where(last_tile, 0.0, halo_ref[1:2, :])
        dc_v = dc_ref[...]
        n1 = jnp.where(row == tm - 1, h0, pltpu.roll(dc_v, tm - 1, 0))
        n2 = jnp.where(row == tm - 1, h1, jnp.where(row == tm - 2, h0, pltpu.roll(dc_v, tm - 2, 0)))
        da_ref[...] = (w_ref[2:3, :] * dc_v + w_ref[1:2, :] * n1 + w_ref[0:1, :] * n2).astype(da_ref.dtype)

    return _pcall(body, name="conv_transpose", grid=(nt,),
                  in_specs=[BS((tm, C), lambda i: (i, 0)),
                            BS((8, C), lambda i: (jnp.minimum((i + 1) * halo_blocks, S // 8 - 1), 0)),
                            BS((3, C), lambda i: (0, 0))],
                  out_specs=BS((tm, C), lambda i: (i, 0)), out_shape=SDS((S, C), MXU_DTYPE))(dc, dc, cw)


def rms_bwd(dh, x, g, dres, *, name):
    S, W = x.shape
    tm = _tile(S, (512, 256))

    def body(dh_ref, x_ref, g_ref, dr_ref, dx_ref, dg_ref):
        @pl.when(pl.program_id(0) == 0)
        def _():
            dg_ref[...] = jnp.zeros_like(dg_ref)

        xv = x_ref[...]
        dx, dgc = _rms_bwd(dh_ref[...], xv, g_ref[...], _rms(xv))
        dx_ref[...] = dr_ref[...] + dx
        _acc_rows(dg_ref, 0, dgc)

    row = BS((tm, W), lambda i: (i, 0))
    return _pcall(body, name=name, grid=(S // tm,), in_specs=[row, row, BS((1, W), lambda i: (0, 0)), row],
                  out_specs=[row, BS((8, W), lambda i: (0, 0))],
                  out_shape=[SDS((S, W), F32), SDS((8, W), F32)])(dh, x, g, dres)


def xattn_bwd(qx, dxo, kn, vb, gxq):
    S = qx.shape[0]
    tm = _tile(S, (256,))

    def body(q_ref, do_ref, kn_ref, vb_ref, g_ref, dq_ref, dkn_ref, dv_ref, dg_ref):
        @pl.when(pl.program_id(0) == 0)
        def _():
            dkn_ref[...] = jnp.zeros_like(dkn_ref)
            dv_ref[...] = jnp.zeros_like(dv_ref)
            dg_ref[...] = jnp.zeros_like(dg_ref)

        g = g_ref[...]
        for h in range(XA_HEADS):
            sl = slice(h * XA_DH, (h + 1) * XA_DH)
            qh, do = q_ref[:, sl], do_ref[:, sl]
            r, qn, p = _xa_probs(qh, g, kn_ref[:, sl])
            dp = _dot(do, vb_ref[:, sl], NT)
            ds = p * (dp - jnp.sum(dp * p, axis=1, keepdims=True)) * (1.0 / math.sqrt(XA_DH))
            dqn = _dot(ds, kn_ref[:, sl])
            dkn_ref[:, sl] += _dot(ds, qn, TN)
            dv_ref[:, sl] += _dot(p, do, TN)
            dqh, dgc = _rms_bwd(dqn, qh, g, r)
            dq_ref[:, sl] = dqh.astype(dq_ref.dtype)
            _acc_rows(dg_ref, 0, dgc)

    row = BS((tm, 1024), lambda i: (i, 0))
    full = lambda r, w: BS((r, w), lambda i: (0, 0))
    return _pcall(body, name="xattn_bwd", grid=(S // tm,),
                  in_specs=[row, row, full(MEM_LEN, 1024), full(MEM_LEN, 1024), full(1, XA_DH)],
                  out_specs=[row, full(MEM_LEN, 1024), full(MEM_LEN, 1024), full(8, XA_DH)],
                  out_shape=[SDS((S, 1024), MXU_DTYPE), SDS((MEM_LEN, 1024), F32), SDS((MEM_LEN, 1024), F32),
                             SDS((8, XA_DH), F32)])(qx, dxo, kn, vb, gxq)


def mem_bwd(kv, dkn, dvb, gxk):
    def body(kv_ref, dkn_ref, dv_ref, g_ref, dkv_ref, dg_ref):
        dg_ref[...] = jnp.zeros_like(dg_ref)
        for h in range(XA_HEADS):
            sl = slice(h * XA_DH, (h + 1) * XA_DH)
            k = kv_ref[:, sl]
            dk, dgc = _rms_bwd(dkn_ref[:, sl], k, g_ref[...], _rms(k))
            dkv_ref[:, sl] = dk.astype(dkv_ref.dtype)
            _acc_rows(dg_ref, 0, dgc)
        dkv_ref[:, 1024:2048] = dv_ref[...].astype(dkv_ref.dtype)

    full = lambda r, w: BS((r, w), lambda i: (0, 0))
    return _pcall(body, name="mem_bwd", grid=(1,),
                  in_specs=[full(MEM_LEN, 2048), full(MEM_LEN, 1024), full(MEM_LEN, 1024), full(1, XA_DH)],
                  out_specs=[full(MEM_LEN, 2048), full(8, XA_DH)],
                  out_shape=[SDS((MEM_LEN, 2048), MXU_DTYPE), SDS((8, XA_DH), F32)])(kv, dkn, dvb, gxk)


def mixer_post_bwd(dy, attn, gm, gao, ggo):
    S = dy.shape[0]
    tm = _tile(S, (256,))

    def body(dy_ref, a_ref, gm_ref, gao_ref, ggo_ref, da_ref, dgm_ref, dg_ref):
        @pl.when(pl.program_id(0) == 0)
        def _():
            dg_ref[...] = jnp.zeros_like(dg_ref)

        a, gmv = a_ref[...], gm_ref[...]
        da, dga = _rms_bwd(dy_ref[:, :512], a, gao_ref[...], _rms(a))
        dgm, dgg = _rms_bwd(dy_ref[:, 512:], gmv, ggo_ref[...], _rms(gmv))
        da_ref[...] = da
        dgm_ref[...] = dgm
        dg_ref[0:1, :512] += jnp.sum(dga, axis=0, keepdims=True)
        dg_ref[0:1, 512:] += jnp.sum(dgg, axis=0, keepdims=True)

    row = lambda w: BS((tm, w), lambda i: (i, 0))
    const = lambda r, w: BS((r, w), lambda i: (0, 0))
    return _pcall(body, name="mixer_post_bwd", grid=(S // tm,),
                  in_specs=[row(1024), row(512), row(512), const(1, 512), const(1, 512)],
                  out_specs=[row(512), row(512), const(8, 1024)],
                  out_shape=[SDS((S, 512), F32), SDS((S, 512), F32), SDS((8, 1024), F32)])(dy, attn, gm, gao, ggo)


def gmlp_bwd(dgm, gvn, gu, w2, w2t, bsl):
    S = dgm.shape[0]

    def body(dgm_ref, gvn_ref, gu_ref, w2_ref, w2t_ref, bsl_ref, dgu_ref, dgvn_ref, dws_ref, dbl_ref):
        @pl.when(pl.program_id(0) == 0)
        def _():
            dws_ref[...] = jnp.zeros_like(dws_ref)
            dbl_ref[...] = jnp.zeros_like(dbl_ref)

        lo = _lane((BLK, 128)) < 64
        for j in range(4):
            sl = slice(j * 128, (j + 1) * 128)
            gvn_s = gvn_ref[:, sl]
            m2 = _dot(w2_ref[j], gvn_s)
            mixed = jnp.where(lo, m2[:BLK], m2[BLK:]) + bsl_ref[j]
            dgm_s = dgm_ref[:, sl]
            dgu_ref[:, sl] = dgm_s * mixed
            dmx = dgm_s * gu_ref[:, sl]
            d2 = _dot(w2t_ref[j], dmx)
            dgvn_ref[:, sl] = jnp.where(lo, d2[:BLK], d2[BLK:])
            z = jnp.zeros_like(dmx)
            dws_ref[2 * j] += _dot(jnp.where(lo, dmx, z), gvn_s, NT)
            dws_ref[2 * j + 1] += _dot(jnp.where(lo, z, dmx), gvn_s, NT)
            dbl_ref[j] += dmx

    row = lambda w: BS((BLK, w), lambda n: (n, 0))
    const3 = lambda a, b, c: BS((a, b, c), lambda n: (0, 0, 0))
    return _pcall(body, name="gmlp_bwd", grid=(S // BLK,),
                  in_specs=[row(512), row(512), row(512), const3(4, 2 * BLK, BLK), const3(4, 2 * BLK, BLK),
                            const3(4, BLK, 128)],
                  out_specs=[row(512), row(512), const3(8, BLK, BLK), const3(4, BLK, 128)],
                  out_shape=[SDS((S, 512), F32), SDS((S, 512), F32), SDS((8, BLK, BLK), F32),
                             SDS((4, BLK, 128), F32)])(dgm, gvn, gu, w2, w2t, bsl)


def swa_bwd(qr, kr, vb, sinkcol, dattn):
    S = qr.shape[0]
    nb = S // BLK

    def body(q_ref, kc_ref, kp_ref, vc_ref, vp_ref, sk_ref, do_ref, dq_ref, dk_ref, dv_ref, dsk_ref,
             carry_k, carry_v, prev_k, prev_v):
        n = pl.program_id(0)

        @pl.when(n == 0)
        def _():
            dsk_ref[...] = jnp.zeros_like(dsk_ref)
            carry_k[...] = jnp.zeros_like(carry_k)
            carry_v[...] = jnp.zeros_like(carry_v)

        @pl.when(n < nb)
        def _():
            lo = _lane((BLK, 128)) < 64
            for h in range(2):
                hs = slice(h * 128, (h + 1) * 128)
                kd = jnp.concatenate([kp_ref[:, hs], kc_ref[:, hs]], axis=0)
                vd = jnp.concatenate([vp_ref[:, hs], vc_ref[:, hs]], axis=0)
                dkd = jnp.zeros((2 * BLK, 128), F32)
                dvd = jnp.zeros((2 * BLK, 128), F32)
                for s in (2 * h, 2 * h + 1):
                    sl = slice(s * 128, (s + 1) * 128)
                    qp, p, psink = _swa_probs(q_ref[:, sl], kd, sk_ref[s], n, lo)
                    do = do_ref[:, sl]
                    z = jnp.zeros_like(do)
                    dop = jnp.concatenate([jnp.where(lo, do, z), jnp.where(lo, z, do)], axis=0)
                    dp = _dot(dop, vd, NT)
                    delta = jnp.sum(dp * p, axis=1, keepdims=True)
                    ds = p * (dp - delta) * (1.0 / math.sqrt(HEAD_DIM))
                    dsk_ref[s] += -psink * delta
                    dq2 = _dot(ds, kd)
                    dq_ref[:, sl] = jnp.where(lo, dq2[:BLK], dq2[BLK:])
                    dkd = dkd + _dot(ds, qp, TN)
                    dvd = dvd + _dot(p, dop, TN)
                prev_k[:, hs] = carry_k[:, hs] + dkd[:BLK]
                prev_v[:, hs] = carry_v[:, hs] + dvd[:BLK]
                carry_k[:, hs] = dkd[BLK:]
                carry_v[:, hs] = dvd[BLK:]

        @pl.when(n == nb)
        def _():
            prev_k[...] = carry_k[...]
            prev_v[...] = carry_v[...]

        dk_ref[...] = prev_k[...]
        dv_ref[...] = prev_v[...]

    last = nb - 1
    cur = lambda w: BS((BLK, w), lambda n: (jnp.minimum(n, last), 0))
    prev = lambda w: BS((BLK, w), lambda n: (jnp.clip(n - 1, 0, last), 0))
    done = lambda w: BS((BLK, w), lambda n: (jnp.maximum(n - 1, 0), 0))
    return _pcall(body, name="swa_bwd", grid=(nb + 1,),
                  in_specs=[cur(512), cur(256), prev(256), cur(256), prev(256),
                            BS((4, 2 * BLK, 1), lambda n: (0, 0, 0)), cur(512)],
                  out_specs=[cur(512), done(256), done(256), BS((4, 2 * BLK, 1), lambda n: (0, 0, 0))],
                  out_shape=[SDS((S, 512), F32), SDS((S, 256), F32), SDS((S, 256), F32), SDS((4, 2 * BLK, 1), F32)],
                  scratch=[pltpu.VMEM((BLK, 256), F32)] * 4)(qr, kr, kr, vb, vb, sinkcol, dattn)


def mixer_pre_bwd(proj, cos, sin, gq, gk, gvn, bmat, dqr, dkr, dvb, dgu, dgvn):
    S = proj.shape[0]
    tm = _tile(S, (256,))

    def body(p_ref, c_ref, s_ref, gq_ref, gk_ref, gvn_ref, b_ref, dqr_ref, dkr_ref, dvb_ref, dgu_ref, dgvn_ref,
             dp_ref, dgq_ref, dgk_ref, dgv_ref):
        @pl.when(pl.program_id(0) == 0)
        def _():
            dgq_ref[...] = jnp.zeros_like(dgq_ref)
            dgk_ref[...] = jnp.zeros_like(dgk_ref)
            dgv_ref[...] = jnp.zeros_like(dgv_ref)

        cos_v, sin_v, bm = c_ref[...], s_ref[...], b_ref[...]
        first = (_lane((tm, 128)) & 63) < 32

        def slab_bwd(slab, dout, g, dg_ref):
            r = lax.rsqrt(_segsum(slab * slab, bm) * (1.0 / HEAD_DIM) + EPS)
            ds = dout * sin_v
            dqn = dout * cos_v + jnp.where(first, pltpu.roll(ds, 96, 1), pltpu.roll(ds, 32, 1))
            dyg = dqn * g
            dx = r * dyg - slab * (r * r * r) * (_segsum(dyg * slab, bm) * (1.0 / HEAD_DIM))
            _acc_rows(dg_ref, 0, dqn * slab * r)
            return dx

        for s in range(4):
            sl = slice(s * 128, (s + 1) * 128)
            dp_ref[:, sl] = slab_bwd(p_ref[:, sl], dqr_ref[:, sl], gq_ref[...], dgq_ref).astype(dp_ref.dtype)
        for s in range(2):
            sl = slice(512 + s * 128, 640 + s * 128)
            dp_ref[:, sl] = slab_bwd(p_ref[:, sl], dkr_ref[:, s * 128:(s + 1) * 128], gk_ref[...],
                                     dgk_ref).astype(dp_ref.dtype)
        dp_ref[:, 768:1024] = dvb_ref[...].astype(dp_ref.dtype)
        dp_ref[:, 1024:1536] = (dgu_ref[...] * _gelu_grad(p_ref[:, 1024:1536])).astype(dp_ref.dtype)
        gvp = p_ref[:, 1536:2048]
        gv = _gelu(gvp)
        dgv, dgc = _rms_bwd(dgvn_ref[...], gv, gvn_ref[...], _rms(gv))
        dp_ref[:, 1536:2048] = (dgv * _gelu_grad(gvp)).astype(dp_ref.dtype)
        _acc_rows(dgv_ref, 0, dgc)

    row = lambda w: BS((tm, w), lambda i: (i, 0))
    const = lambda r, w: BS((r, w), lambda i: (0, 0))
    return _pcall(body, name="mixer_pre_bwd", grid=(S // tm,),
                  in_specs=[row(IN_COLS_DUP), row(128), row(128), const(1, 128), const(1, 128), const(1, 512),
                            const(128, 128), row(512), row(256), row(256), row(512), row(512)],
                  out_specs=[row(IN_COLS_DUP), const(8, 128), const(8, 128), const(8, 512)],
                  out_shape=[SDS((S, IN_COLS_DUP), MXU_DTYPE), SDS((8, 128), F32), SDS((8, 128), F32),
                             SDS((8, 512), F32)])(proj, cos, sin, gq, gk, gvn, bmat, dqr, dkr, dvb, dgu, dgvn)


BIG = (("w_in", (1024, 448), True), ("w_out", (256, 1024), False), ("xa_wq", (256, 1024), False),
       ("xa_wkv", (1024, 512), True), ("xa_wo", (256, 1024), False), ("ffn_up", (1024, 1408), True),
       ("ffn_down", (704, 1024), False))
BIG_NAMES = tuple(n for n, _, _ in BIG)
SMALL_VECS = (("mix_norm", 1024), ("q_norm", 64), ("k_norm", 64), ("attn_sinks", 8), ("gmlp_v_norm", 512),
              ("attn_out_norm", 512), ("gmlp_out_norm", 512), ("xa_norm", 1024), ("mem_norm", 1024),
              ("xa_q_norm", 256), ("xa_k_norm", 256), ("ffn_norm", 1024), ("ffn_conv_b", 5632))
SMALL = tuple(n for n, _ in SMALL_VECS) + ("gmlp_bs", "gmlp_ws", "ffn_conv")
WEIGHTS = ("mix_norm", "w_in", "q_norm", "k_norm", "attn_sinks", "gmlp_v_norm", "gmlp_ws", "gmlp_bs",
           "attn_out_norm", "gmlp_out_norm", "w_out", "xa_norm", "mem_norm", "xa_wq", "xa_wkv", "xa_q_norm",
           "xa_k_norm", "xa_wo", "ffn_norm", "ffn_up", "ffn_conv", "ffn_conv_b", "ffn_down")
CONV_SHARD = (3, 1408)
CONV_LANE_ROWS = CONV_SHARD[1] // 128
CONV_CHIP_ROWS = 40


def _small_rows():
    rows, r = {}, 0
    for n, length in SMALL_VECS:
        rows[n] = r
        r += -(-length // 128)
    r += -r % 8
    rows["gmlp_bs"] = r
    r += 8
    rows["gmlp_ws"] = r
    r += 8 * BLK
    rows["ffn_conv"] = r
    r += N_CHIPS * CONV_CHIP_ROWS
    return rows, r


SMALL_ROW, SMALL_ROWS = _small_rows()


def pack_small(dg_mix, dgq, dgk, dsk, dg_gvn, dg_y, dg_xa, dg_mem, dg_xq, dg_xk, dg_ffn, gcw, dbl, dws):
    def body(mix_ref, q_ref, k_ref, sk_ref, gvn_ref, y_ref, xa_ref, mem_ref, xq_ref, xk_ref, ffn_ref, cw_ref,
             dbl_ref, dws_ref, o_ref):
        o_ref[...] = jnp.zeros_like(o_ref)
        lane = _lane((1, 128))

        def put(name, src_ref, row, lane0, length):
            for k in range(length // 128):
                o_ref[SMALL_ROW[name] + k:SMALL_ROW[name] + k + 1, :] = src_ref[row:row + 1, lane0 + k * 128:lane0 + (k + 1) * 128]

        put("mix_norm", mix_ref, 0, 0, 1024)
        for name, ref in (("q_norm", q_ref), ("k_norm", k_ref)):
            v = ref[0:1, :]
            o_ref[SMALL_ROW[name]:SMALL_ROW[name] + 1, :] = jnp.where(lane < HEAD_DIM, v + pltpu.roll(v, 64, 1), 0.0)
        sinks = jnp.zeros((1, 128), F32)
        for s in range(4):
            col = sk_ref[s]
            sinks = sinks + jnp.where(lane == 2 * s, jnp.sum(col[:BLK]), 0.0) + jnp.where(lane == 2 * s + 1, jnp.sum(col[BLK:]), 0.0)
        o_ref[SMALL_ROW["attn_sinks"]:SMALL_ROW["attn_sinks"] + 1, :] = sinks
        put("gmlp_v_norm", gvn_ref, 0, 0, 512)
        put("attn_out_norm", y_ref, 0, 0, 512)
        put("gmlp_out_norm", y_ref, 0, 512, 512)
        put("xa_norm", xa_ref, 0, 0, 1024)
        put("mem_norm", mem_ref, 0, 0, 1024)
        put("xa_q_norm", xq_ref, 0, 0, 256)
        put("xa_k_norm", xk_ref, 0, 0, 256)
        put("ffn_norm", ffn_ref, 0, 0, 1024)
        put("ffn_conv_b", cw_ref, 3, 0, 2 * D_FF)
        r8 = lax.broadcasted_iota(jnp.int32, (8, 128), 0)
        l8 = _lane((8, 128))
        bs = jnp.zeros((8, BLK), F32)
        for j in range(4):
            sel = (((r8 == 2 * j) & (l8 < 64)) | ((r8 == 2 * j + 1) & (l8 >= 64))).astype(F32).astype(BF16)
            xj = dbl_ref[j]
            hi = xj.astype(BF16)
            lo = (xj - hi.astype(F32)).astype(BF16)
            bs = bs + lax.dot_general(sel, hi, NT, preferred_element_type=F32) + lax.dot_general(sel, lo, NT, preferred_element_type=F32)
        o_ref[SMALL_ROW["gmlp_bs"]:SMALL_ROW["gmlp_bs"] + 8, :] = bs
        causal = lax.broadcasted_iota(jnp.int32, (BLK, BLK), 0) >= lax.broadcasted_iota(jnp.int32, (BLK, BLK), 1)
        for h in range(8):
            r0 = SMALL_ROW["gmlp_ws"] + h * BLK
            o_ref[r0:r0 + BLK, :] = jnp.where(causal, dws_ref[h], 0.0)
        for q in range(N_CHIPS):
            for j in range(3):
                for k in range(CONV_LANE_ROWS):
                    r0 = SMALL_ROW["ffn_conv"] + q * CONV_CHIP_ROWS + j * CONV_LANE_ROWS + k
                    l0 = (q * CONV_LANE_ROWS + k) * 128
                    o_ref[r0:r0 + 1, :] = cw_ref[j:j + 1, l0:l0 + 128]

    args = (dg_mix, dgq, dgk, dsk, dg_gvn, dg_y, dg_xa, dg_mem, dg_xq, dg_xk, dg_ffn, gcw, dbl, dws)
    full = lambda a: BS(a.shape, lambda i, nd=a.ndim: (0,) * nd)
    return _pcall(body, name="pack_small", grid=(1,), in_specs=[full(a) for a in args],
                  out_specs=BS((SMALL_ROWS, 128), lambda i: (0, 0)), out_shape=SDS((SMALL_ROWS, 128), F32))(*args)


def _adam(w, g, m, v):
    mn = ADAM_B1 * m + (1.0 - ADAM_B1) * g
    vn = ADAM_B2 * v + (1.0 - ADAM_B2) * (g * g)
    m_hat = mn / (1.0 - ADAM_B1 ** ADAM_STEP)
    v_hat = vn / (1.0 - ADAM_B2 ** ADAM_STEP)
    return -ADAM_LR * (m_hat / (jnp.sqrt(v_hat) + ADAM_EPS) + ADAM_WD * w), mn, vn


def adamw_small(gsum, w, m, v, chipvec):
    n = len(SMALL)

    def body(chip_ref, g_ref, *refs):
        w_refs, m_refs, v_refs = refs[:n], refs[n:2 * n], refs[2 * n:3 * n]
        outs = refs[3 * n:]
        go, do, mo, vo = outs[:n], outs[n:2 * n], outs[2 * n:3 * n], outs[3 * n:]

        def update(i, idx, g):
            d, mn, vn = _adam(w_refs[i][idx], g, m_refs[i][idx], v_refs[i][idx])
            go[i][idx] = g
            do[i][idx] = d
            mo[i][idx] = mn
            vo[i][idx] = vn

        for i, (name, length) in enumerate(SMALL_VECS):
            for k in range(-(-length // 128)):
                wd = min(128, length - k * 128)
                r = SMALL_ROW[name] + k
                update(i, (slice(0, 1), slice(k * 128, k * 128 + wd)), g_ref[r:r + 1, 0:wd])
        i_bs, i_ws, i_cv = len(SMALL_VECS), len(SMALL_VECS) + 1, len(SMALL_VECS) + 2
        update(i_bs, (0,), g_ref[SMALL_ROW["gmlp_bs"]:SMALL_ROW["gmlp_bs"] + 8, :])
        for h in range(8):
            r0 = SMALL_ROW["gmlp_ws"] + h * BLK
            update(i_ws, (0, h), g_ref[r0:r0 + BLK, :])
        mine = g_ref[pl.ds(pl.multiple_of(SMALL_ROW["ffn_conv"] + chip_ref[0] * CONV_CHIP_ROWS, 8), CONV_CHIP_ROWS), :]
        for j in range(3):
            for k in range(CONV_LANE_ROWS):
                r = j * CONV_LANE_ROWS + k
                update(i_cv, (0, slice(j, j + 1), slice(k * 128, (k + 1) * 128)), mine[r:r + 1, :])

    nat = [w[nm] for nm in SMALL]
    full = lambda a: BS(a.shape, lambda i, c, nd=a.ndim: (0,) * nd)
    outs = _pcall(body, name="adamw_small", grid=(1,), prefetch=1,
                  in_specs=[BS((SMALL_ROWS, 128), lambda i, c: (0, 0))] + [full(a) for a in nat] * 3,
                  out_specs=[full(a) for a in nat] * 4, out_shape=[SDS(a.shape, F32) for a in nat] * 4)(
        chipvec, gsum, *nat, *[m[nm] for nm in SMALL], *[v[nm] for nm in SMALL])
    return outs[:n], outs[n:2 * n], outs[2 * n:3 * n], outs[3 * n:]


def adamw_matrix(w, m, v, g_own, g_other, cvec, *, name):
    _, r, c = w.shape
    half = r // 2
    tr = _tile(half, (128, 176))
    T = half // tr

    def body(c_ref, w_ref, m_ref, v_ref, own_ref, oth_ref, g_ref, d_ref, mo_ref, vo_ref):
        g = jnp.where(pl.program_id(0) == c_ref[0], own_ref[...], oth_ref[...])
        d, mn, vn = _adam(w_ref[...], g, m_ref[...], v_ref[...])
        g_ref[...] = g
        d_ref[...] = d
        mo_ref[...] = mn
        vo_ref[...] = vn

    nat = BS((None, tr, c), lambda hf, t, cr: (0, hf * T + t, 0))
    hlf = BS((tr, c), lambda hf, t, cr: (t, 0))
    return _pcall(body, name=name, grid=(2, T), prefetch=1, in_specs=[nat, nat, nat, hlf, hlf], out_specs=[nat] * 4,
                  out_shape=[SDS(w.shape, F32)] * 4)(cvec, w, m, v, g_own, g_other)


def _place():
    return lax.axis_index("x"), lax.axis_index("y"), lax.axis_index("c")


def _other_chips(x, y):
    return [(1 - x, y), (x, 1 - y), (1 - x, 1 - y)]


def _rows_of_core(c, half):
    return pl.ds(pl.multiple_of(c * half, 16), half)


def _rcopy(src, dst, sems, k, to):
    return pltpu.make_async_remote_copy(src_ref=src, dst_ref=dst, send_sem=sems[0].at[k], recv_sem=sems[1].at[k],
                                        device_id=to, device_id_type=MESH)


def _comm_call(body, *, name, out_shape, n_in, n_sems, aliases=None):
    return pl.pallas_call(body, name=name, out_shape=out_shape, in_specs=[ANY] * n_in, out_specs=[ANY] * len(out_shape),
                          scratch_shapes=[pltpu.SemaphoreType.DMA((n_sems,)), pltpu.SemaphoreType.DMA((n_sems,))],
                          input_output_aliases=aliases or {},
                          compiler_params=pltpu.CompilerParams(has_side_effects=True))


def cast_shards(shards, conv, chipvec):
    n = len(shards)

    def body(chip_ref, *refs):
        for i_ref, o_ref in zip(refs[:n + 1], refs[n + 1:]):
            o_ref[...] = i_ref[...].astype(o_ref.dtype)

    in_specs = [BS((s.shape[0] // 4, s.shape[1]), lambda i, p: (i, 0)) for s in shards]
    in_specs.append(BS(conv.shape, lambda i, p: (0, 0)))
    out_specs = [BS((None, s.shape[0] // 4, s.shape[1]), lambda i, p: (p[0], i, 0)) for s in shards]
    out_specs.append(BS((None,) + conv.shape, lambda i, p: (p[0], 0, 0)))
    out_shape = [SDS((N_CHIPS,) + s.shape, MXU_DTYPE) for s in shards] + [SDS((N_CHIPS,) + conv.shape, F32)]
    return _pcall(body, name="cast_shards", grid=(4,), prefetch=1, in_specs=in_specs, out_specs=out_specs,
                  out_shape=out_shape)(chipvec, *shards, conv)


HBM = pl.BlockSpec(memory_space=pltpu.HBM)
SEM = pl.BlockSpec(memory_space=pltpu.SEMAPHORE)
DATAFLOW = pltpu.SideEffectType.DATAFLOW_SIDE_EFFECTING


def _gather_copies(bufs, send_sems, recv_sems, outgoing):
    x, y, c = _place()
    p = 2 * x + y
    cps = []
    for i, o in enumerate(bufs):
        for j, (cx, cy) in enumerate(_other_chips(x, y)):
            slot = o.at[p] if outgoing else o.at[2 * cx + cy]
            cps.append(_rcopy(slot, slot, (send_sems, recv_sems), 3 * i + j, (cx, cy, c)))
    return cps


def gather_start(slots):
    n = len(slots)

    def body(*refs):
        send_sems, recv_sems, thru = refs[n], refs[n + 1], refs[n + 2:]
        for cp in _gather_copies(thru, send_sems, recv_sems, True):
            cp.start()

    hbm = [pltpu.with_memory_space_constraint(s, pltpu.HBM) for s in slots]
    outs = pl.pallas_call(
        body, name="gather_start_%d" % n,
        out_shape=[pltpu.SemaphoreType.DMA((3 * n,)), pltpu.SemaphoreType.DMA((3 * n,))]
        + [pltpu.HBM(s.shape, s.dtype) for s in slots],
        in_specs=[HBM] * n, out_specs=[SEM, SEM] + [HBM] * n, input_output_aliases={i: 2 + i for i in range(n)},
        compiler_params=pltpu.CompilerParams(has_side_effects=DATAFLOW))(*hbm)
    return outs[0], outs[1], outs[2:]


def gather_wait(send_sems, recv_sems, bufs, after):
    n = len(bufs)

    def body(*refs):
        ins, send_ref, recv_ref = refs[:n], refs[n], refs[n + 1]
        for cp in _gather_copies(ins, send_ref, recv_ref, False):
            cp.wait_send()
            cp.wait_recv()

    return pl.pallas_call(
        body, name="gather_wait_%d" % n, out_shape=[pltpu.HBM(s.shape, s.dtype) for s in bufs],
        in_specs=[HBM] * n + [SEM, SEM, ANY], out_specs=[HBM] * n, input_output_aliases={i: i for i in range(n)},
        compiler_params=pltpu.CompilerParams(has_side_effects=DATAFLOW))(*bufs, send_sems, recv_sems, after)


def pair_exchange(gs):
    n = len(gs)

    def body(*refs):
        g_refs, r_refs, sems = refs[:n], refs[n:2 * n], refs[2 * n:]
        x, y, c = _place()
        cps = [_rcopy(g.at[:, _rows_of_core(1 - c, g.shape[1] // 2)], r, sems, i, (x, y, 1 - c))
               for i, (g, r) in enumerate(zip(g_refs, r_refs))]
        for cp in cps:
            cp.start()
        for cp in cps:
            cp.wait()

    return _comm_call(body, name="pair_exchange", n_in=n, n_sems=n,
                      out_shape=[SDS((g.shape[0], g.shape[1] // 2, g.shape[2]), g.dtype) for g in gs])(*gs)


def pair_add(gs, rs, cvec):
    n = len(gs)

    def body(c_ref, *refs):
        for g_ref, r_ref, o_ref in zip(refs[:n], refs[n:2 * n], refs[2 * n:]):
            o_ref[...] = (g_ref[...].astype(F32) + r_ref[...].astype(F32)).astype(o_ref.dtype)

    g4 = [g.reshape(g.shape[0], 2, g.shape[1] // 2, g.shape[2]) for g in gs]
    return _pcall(body, name="pair_add", grid=(N_CHIPS,), prefetch=1,
                  in_specs=[BS((None, None) + g.shape[2:], lambda q, cr: (q, cr[0], 0, 0)) for g in g4]
                  + [BS((None,) + r.shape[1:], lambda q, cr: (q, 0, 0)) for r in rs],
                  out_specs=[BS((None,) + r.shape[1:], lambda q, cr: (q, 0, 0)) for r in rs],
                  out_shape=[SDS(r.shape, r.dtype) for r in rs])(cvec, *g4, *rs)


def scatter_partials(ps):
    n = len(ps)

    def body(*refs):
        p_refs, r_refs, sems = refs[:n], refs[n:2 * n], refs[2 * n:]
        x, y, c = _place()
        p = 2 * x + y
        chips = _other_chips(x, y)
        sends = [_rcopy(pr.at[2 * cx + cy], rr.at[p], sems, 3 * i + j, (cx, cy, c))
                 for i, (pr, rr) in enumerate(zip(p_refs, r_refs)) for j, (cx, cy) in enumerate(chips)]
        for cp in sends:
            cp.start()
        for i, (pr, rr) in enumerate(zip(p_refs, r_refs)):
            for j, (cx, cy) in enumerate(chips):
                _rcopy(pr.at[p], rr.at[2 * cx + cy], sems, 3 * i + j, (cx, cy, c)).wait_recv()
        for cp in sends:
            cp.wait_send()

    return _comm_call(body, name="scatter_partials", n_in=n, n_sems=3 * n,
                      out_shape=[SDS(p.shape, p.dtype) for p in ps])(*ps)


def sum_chips(ps, rs, order):
    n = len(ps)

    def body(o_ref, *refs):
        j = pl.program_id(0)
        for p_ref, r_ref, f_ref in zip(refs[:n], refs[n:2 * n], refs[2 * n:]):
            @pl.when(j == 0)
            def _():
                f_ref[...] = p_ref[...].astype(F32) + r_ref[...].astype(F32)

            @pl.when(j > 0)
            def _():
                f_ref[...] += r_ref[...].astype(F32)

    return _pcall(body, name="sum_chips", grid=(N_CHIPS - 1,), prefetch=1,
                  in_specs=[BS((None,) + p.shape[1:], lambda j, o: (o[0], 0, 0)) for p in ps]
                  + [BS((None,) + r.shape[1:], lambda j, o: (o[j + 1], 0, 0)) for r in rs],
                  out_specs=[BS(p.shape[1:], lambda j, o: (0, 0)) for p in ps],
                  out_shape=[SDS(p.shape[1:], F32) for p in ps])(order, *ps, *rs)


def pair_share(fs):
    n = len(fs)

    def body(*refs):
        f_refs, o_refs, sems = refs[:n], refs[n:2 * n], refs[2 * n:]
        x, y, c = _place()
        cps = [_rcopy(f, o, sems, i, (x, y, 1 - c)) for i, (f, o) in enumerate(zip(f_refs, o_refs))]
        for cp in cps:
            cp.start()
        for cp in cps:
            cp.wait()

    return _comm_call(body, name="pair_share", n_in=n, n_sems=n, out_shape=[SDS(f.shape, f.dtype) for f in fs])(*fs)


def gather_all(sm):
    rows, width = sm.shape

    def body(s_ref, o_ref, send_sems, recv_sems, local_sem):
        x, y, c = _place()
        me = 4 * x + 2 * y + c
        sems = (send_sems, recv_sems)
        mine = pltpu.make_async_copy(s_ref, o_ref.at[me], local_sem)
        mine.start()
        peers = [(1 - x if k & 4 else x, 1 - y if k & 2 else y, 1 - c if k & 1 else c) for k in range(1, N_DEV)]
        sends = [_rcopy(s_ref, o_ref.at[me], sems, k, peer) for k, peer in enumerate(peers)]
        for cp in sends:
            cp.start()
        for k, (px, py, pc) in enumerate(peers):
            _rcopy(s_ref, o_ref.at[4 * px + 2 * py + pc], sems, k, (px, py, pc)).wait_recv()
        for cp in sends:
            cp.wait_send()
        mine.wait()

    return pl.pallas_call(body, name="gather_all", out_shape=SDS((N_DEV, rows, width), sm.dtype), in_specs=[ANY],
                          out_specs=ANY,
                          scratch_shapes=[pltpu.SemaphoreType.DMA((N_DEV - 1,)), pltpu.SemaphoreType.DMA((N_DEV - 1,)),
                                          pltpu.SemaphoreType.DMA],
                          compiler_params=pltpu.CompilerParams(has_side_effects=True))(sm)


def sum_slots(r, *, name):
    n, rows, width = r.shape
    tr = _tile(rows, (184, 8))

    def body(r_ref, o_ref):
        acc = r_ref[0]
        for s in range(1, n):
            acc = acc + r_ref[s]
        o_ref[...] = acc

    return _pcall(body, name=name, grid=(rows // tr,), in_specs=[BS((n, tr, width), lambda i: (0, i, 0))],
                  out_specs=BS((tr, width), lambda i: (i, 0)), out_shape=SDS((rows, width), F32))(r)


def _to_full(blk, col):
    n, r, c = blk.shape
    return blk.transpose(1, 0, 2).reshape(r, n * c) if col else blk.reshape(n * r, c)


def _dup_cols(w):
    dup = lambda t: jnp.concatenate([t[:, :64], t[:, :64], t[:, 64:], t[:, 64:]], axis=1)
    return jnp.concatenate([w[:, :512], dup(w[:, 512:640]), dup(w[:, 640:768]), w[:, 768:]], axis=1)


def _fold_cols(d):
    fold = lambda t: jnp.concatenate([t[:, 0:64] + t[:, 64:128], t[:, 128:192] + t[:, 192:256]], axis=1)
    return jnp.concatenate([d[:, :512], fold(d[:, 512:768]), fold(d[:, 768:1024]), d[:, 1024:]], axis=1)


def _local_step(x, mem, positions, target, w_in, later, sp):
    gain = lambda n: sp[n].reshape(1, -1)
    half = HEAD_DIM // 2
    inv_freq = 1.0 / (10000.0 ** (jnp.arange(half, dtype=F32) * (2.0 / HEAD_DIM)))
    ang = positions.astype(F32)[:, None] * inv_freq
    cos, sin = jnp.cos(ang), jnp.sin(ang)
    cos128 = jnp.tile(cos, (1, 4))
    sin128 = jnp.concatenate([-sin, sin, -sin, sin], axis=1)
    seg = jnp.arange(128) // HEAD_DIM
    bmat = (seg[:, None] == seg[None, :]).astype(BF16)
    gq128, gk128 = jnp.tile(gain("q_norm"), (1, 2)), jnp.tile(gain("k_norm"), (1, 2))
    sinkcol = jnp.repeat(sp["attn_sinks"].reshape(4, 2), BLK, axis=1).reshape(4, 2 * BLK, 1)
    wsc = sp["gmlp_ws"] * jnp.tril(jnp.ones((BLK, BLK), F32))[None]
    w2 = wsc.reshape(4, 2 * BLK, BLK).astype(MXU_DTYPE)
    w2t = wsc.swapaxes(1, 2).reshape(4, 2 * BLK, BLK).astype(MXU_DTYPE)
    bsl = jnp.repeat(sp["gmlp_bs"].reshape(4, 2, BLK).transpose(0, 2, 1), HEAD_DIM, axis=2)
    cb = sp["ffn_conv_b"].reshape(1, -1)
    w_in_d = _dup_cols(_to_full(w_in, True))[None]

    h1, proj = rms_mm(x, gain("mix_norm"), w_in_d, name="mix_in")
    qr, kr, vb, gu, gvn = mixer_pre(proj, cos128, sin128, gq128, gk128, gain("gmlp_v_norm"), bmat)
    attn, ya = swa_fwd(qr, kr, vb, sinkcol, gain("attn_out_norm"))
    gm, y = gmlp_fwd(gvn, gu, ya, w2, bsl, gain("gmlp_out_norm"))
    wf, cw = later(y)
    w_out, xa_wq, xa_wo, ffn_down = (_to_full(wf[n], False) for n in ("w_out", "xa_wq", "xa_wo", "ffn_down"))
    x1 = mm(y, w_out, res=x, name="mix_out")
    h2, qx = rms_mm(x1, gain("xa_norm"), xa_wq[None], name="xa_q")
    mn, kv = rms_mm(mem, gain("mem_norm"), wf["xa_wkv"], name="xa_kv")
    kn, vbx = mem_pre(kv, gain("xa_k_norm"))
    xo = xattn_fwd(qx, kn, vbx, gain("xa_q_norm"))
    x2 = mm(xo, xa_wo, res=x1, name="xa_out")
    h3, a = rms_mm(x2, gain("ffn_norm"), wf["ffn_up"], name="ffn_up", tm=1024)
    f = convgate_fwd(a, cw, cb)
    x3 = mm(f, ffn_down, res=x2, name="ffn_down")
    dx3, loss_acc = loss_head(x3, target)

    gbig = {}
    by_rows = lambda g: g.reshape(N_CHIPS, g.shape[1] // N_CHIPS, g.shape[2])
    df = mm_nt(dx3, ffn_down[None], name="d_f")
    gbig["ffn_down"] = by_rows(mm_tn(f, dx3, name="g_ffn_down", out_dtype=WIRE_DTYPE))
    dc, gcw = convgate_bwd(a, df, cw, cb)
    da = conv_transpose(dc, cw)
    dh3 = mm_nt(da, wf["ffn_up"], name="d_h3")
    gbig["ffn_up"] = mm_tn(h3, da, name="g_ffn_up", out_dtype=WIRE_DTYPE, chunks=N_CHIPS)
    dx2, dg_ffn = rms_bwd(dh3, x2, gain("ffn_norm"), dx3, name="ffn_norm_bwd")
    dxo = mm_nt(dx2, xa_wo[None], name="d_xo")
    gbig["xa_wo"] = by_rows(mm_tn(xo, dx2, name="g_xa_wo", out_dtype=WIRE_DTYPE))
    dqx, dkn, dvx, dg_xq = xattn_bwd(qx, dxo, kn, vbx, gain("xa_q_norm"))
    dh2 = mm_nt(dqx, xa_wq[None], name="d_h2")
    gbig["xa_wq"] = by_rows(mm_tn(h2, dqx, name="g_xa_wq", out_dtype=WIRE_DTYPE))
    dx1, dg_xa = rms_bwd(dh2, x1, gain("xa_norm"), dx2, name="xa_norm_bwd")
    dkv, dg_xk = mem_bwd(kv, dkn, dvx, gain("xa_k_norm"))
    dmn = mm_nt(dkv, wf["xa_wkv"], name="d_mn")
    gbig["xa_wkv"] = mm_tn(mn, dkv, name="g_xa_wkv", out_dtype=WIRE_DTYPE, chunks=N_CHIPS)
    _, dg_mem = rms_bwd(dmn, mem, gain("mem_norm"), jnp.zeros_like(mem), name="mem_norm_bwd")
    dy = mm_nt(dx1, w_out[None], name="d_y")
    gbig["w_out"] = by_rows(mm_tn(y, dx1, name="g_w_out", out_dtype=WIRE_DTYPE))
    dattn, dgm, dg_y = mixer_post_bwd(dy, attn, gm, gain("attn_out_norm"), gain("gmlp_out_norm"))
    dgu, dgvn, dws, dbl = gmlp_bwd(dgm, gvn, gu, w2, w2t, bsl)
    dqr, dkr, dvb, dsk = swa_bwd(qr, kr, vb, sinkcol, dattn)
    dproj, dgq, dgk, dg_gvn = mixer_pre_bwd(proj, cos128, sin128, gq128, gk128, gain("gmlp_v_norm"), bmat,
                                            dqr, dkr, dvb, dgu, dgvn)
    dh1 = mm_nt(dproj, w_in_d, name="d_h1")
    g_in = _fold_cols(mm_tn(h1, dproj, name="g_w_in", out_dtype=F32)[0])
    gbig["w_in"] = g_in.reshape(1024, N_CHIPS, 448).transpose(1, 0, 2).astype(WIRE_DTYPE)
    grad_x, dg_mix = rms_bwd(dh1, x, gain("mix_norm"), dx1, name="mix_norm_bwd")
    packed = pack_small(dg_mix, dgq, dgk, dsk, dg_gvn, dg_y, dg_xa, dg_mem, dg_xq, dg_xk, dg_ffn, gcw, dbl, dws)
    return loss_acc, grad_x, gbig, packed


def _gather_step(w, chipvec):
    slots = cast_shards([w[n][0] for n in BIG_NAMES], w["ffn_conv"][0], chipvec)
    send_a, recv_a, first = gather_start(slots[:1])
    send_b, recv_b, rest = gather_start(slots[1:])
    w_in, = gather_wait(send_a, recv_a, first, chipvec)

    def later(after):
        got = gather_wait(send_b, recv_b, rest, after)
        return dict(zip(BIG_NAMES[1:], got[:-1])), _to_full(got[-1], True)

    return w_in, later


def _reduce_update(gbig, packed, w, m, v, chipvec, cvec, order):
    parts = [gbig[n] for n in BIG_NAMES]
    pair = pair_add(parts, pair_exchange(parts), cvec)
    own = sum_chips(pair, scatter_partials(pair), order)
    other = pair_share(own)
    res = [{}, {}, {}, {}]
    for n, g_own, g_other in zip(BIG_NAMES, own, other):
        for d, o in zip(res, adamw_matrix(w[n], m[n], v[n], g_own, g_other, cvec, name="adamw_" + n)):
            d[n] = o
    small_sum = sum_slots(gather_all(packed), name="sum_small")
    for d, outs in zip(res, adamw_small(small_sum, w, m, v, chipvec)):
        d.update(zip(SMALL, outs))
    return res


def kernel(x, mem, positions, mix_norm, w_in, q_norm, k_norm, attn_sinks, gmlp_v_norm, gmlp_ws, gmlp_bs, attn_out_norm, gmlp_out_norm, w_out, xa_norm, mem_norm, xa_wq, xa_wkv, xa_q_norm, xa_k_norm, xa_wo, ffn_norm, ffn_up, ffn_conv, ffn_conv_b, ffn_down, loss_target, m_mix_norm, m_w_in, m_q_norm, m_k_norm, m_attn_sinks, m_gmlp_v_norm, m_gmlp_ws, m_gmlp_bs, m_attn_out_norm, m_gmlp_out_norm, m_w_out, m_xa_norm, m_mem_norm, m_xa_wq, m_xa_wkv, m_xa_q_norm, m_xa_k_norm, m_xa_wo, m_ffn_norm, m_ffn_up, m_ffn_conv, m_ffn_conv_b, m_ffn_down, v_mix_norm, v_w_in, v_q_norm, v_k_norm, v_attn_sinks, v_gmlp_v_norm, v_gmlp_ws, v_gmlp_bs, v_attn_out_norm, v_gmlp_out_norm, v_w_out, v_xa_norm, v_mem_norm, v_xa_wq, v_xa_wkv, v_xa_q_norm, v_xa_k_norm, v_xa_wo, v_ffn_norm, v_ffn_up, v_ffn_conv, v_ffn_conv_b, v_ffn_down):
    w = dict(mix_norm=mix_norm, w_in=w_in, q_norm=q_norm, k_norm=k_norm, attn_sinks=attn_sinks, gmlp_v_norm=gmlp_v_norm, gmlp_ws=gmlp_ws, gmlp_bs=gmlp_bs, attn_out_norm=attn_out_norm, gmlp_out_norm=gmlp_out_norm, w_out=w_out, xa_norm=xa_norm, mem_norm=mem_norm, xa_wq=xa_wq, xa_wkv=xa_wkv, xa_q_norm=xa_q_norm, xa_k_norm=xa_k_norm, xa_wo=xa_wo, ffn_norm=ffn_norm, ffn_up=ffn_up, ffn_conv=ffn_conv, ffn_conv_b=ffn_conv_b, ffn_down=ffn_down)
    m = dict(mix_norm=m_mix_norm, w_in=m_w_in, q_norm=m_q_norm, k_norm=m_k_norm, attn_sinks=m_attn_sinks, gmlp_v_norm=m_gmlp_v_norm, gmlp_ws=m_gmlp_ws, gmlp_bs=m_gmlp_bs, attn_out_norm=m_attn_out_norm, gmlp_out_norm=m_gmlp_out_norm, w_out=m_w_out, xa_norm=m_xa_norm, mem_norm=m_mem_norm, xa_wq=m_xa_wq, xa_wkv=m_xa_wkv, xa_q_norm=m_xa_q_norm, xa_k_norm=m_xa_k_norm, xa_wo=m_xa_wo, ffn_norm=m_ffn_norm, ffn_up=m_ffn_up, ffn_conv=m_ffn_conv, ffn_conv_b=m_ffn_conv_b, ffn_down=m_ffn_down)
    v = dict(mix_norm=v_mix_norm, w_in=v_w_in, q_norm=v_q_norm, k_norm=v_k_norm, attn_sinks=v_attn_sinks, gmlp_v_norm=v_gmlp_v_norm, gmlp_ws=v_gmlp_ws, gmlp_bs=v_gmlp_bs, attn_out_norm=v_attn_out_norm, gmlp_out_norm=v_gmlp_out_norm, w_out=v_w_out, xa_norm=v_xa_norm, mem_norm=v_mem_norm, xa_wq=v_xa_wq, xa_wkv=v_xa_wkv, xa_q_norm=v_xa_q_norm, xa_k_norm=v_xa_k_norm, xa_wo=v_xa_wo, ffn_norm=v_ffn_norm, ffn_up=v_ffn_up, ffn_conv=v_ffn_conv, ffn_conv_b=v_ffn_conv_b, ffn_down=v_ffn_down)
    ix, iy, ic = lax.axis_index("x"), lax.axis_index("y"), lax.axis_index("c")
    chip = 2 * ix + iy
    chipvec = chip.astype(jnp.int32).reshape(1)
    cvec = ic.astype(jnp.int32).reshape(1)
    order = jnp.stack([chip, 2 * (1 - ix) + iy, 2 * ix + (1 - iy), 2 * (1 - ix) + (1 - iy)]).astype(jnp.int32)

    w_in_all, later = _gather_step(w, chipvec)
    sp = {n: w[n][0] for n in SMALL if n != "ffn_conv"}
    loss_acc, grad_x, gbig, packed = _local_step(x[0], mem[0], positions[0], loss_target[0], w_in_all, later, sp)
    grads, delta, new_m, new_v = _reduce_update(gbig, packed, w, m, v, chipvec, cvec, order)
    loss = lax.psum(loss_acc[0, 0], ("x", "y", "c"))
    ordered = lambda d: [d[n] for n in WEIGHTS]
    return (loss, grad_x[None], *ordered(grads), *ordered(delta), *ordered(new_m), *ordered(new_v))
```

```python
import math

import jax
import jax.numpy as jnp
from jax import lax
from jax.experimental import pallas as pl
from jax.experimental.pallas import tpu as pltpu

F32 = jnp.float32
BF16 = jnp.bfloat16
MXU_DTYPE = jnp.bfloat16
WIRE_DTYPE = jnp.bfloat16
EPS = 1e-6
VMEM_LIMIT_V7X = 56 * 1024 * 1024

D_MODEL = 1024
HEAD_DIM = 64
BLK = 128
XA_HEADS = 4
XA_DH = 256
MEM_LEN = 256
D_FF = 2816
IN_COLS_DUP = 2048
N_CHIPS = 4
N_DEV = 8

ADAM_LR = 0.001
ADAM_B1 = 0.9
ADAM_B2 = 0.999
ADAM_EPS = 1e-08
ADAM_WD = 0.01
ADAM_STEP = 10

NT = (((1,), (1,)), ((), ()))
TN = (((0,), (0,)), ((), ()))
NN = (((1,), (0,)), ((), ()))
MINF = float(jnp.finfo(jnp.float32).min)
GELU_K0 = math.sqrt(2.0 / math.pi)
GELU_K1 = 0.044715

BS = pl.BlockSpec
SDS = jax.ShapeDtypeStruct
ANY = pl.BlockSpec(memory_space=pl.ANY)
MESH = pl.DeviceIdType.MESH


def _dot(a, b, dims=NN):
    return lax.dot_general(a.astype(MXU_DTYPE), b.astype(MXU_DTYPE), dims, preferred_element_type=F32)


def _segsum(x, bmat):
    hi = x.astype(BF16)
    lo = (x - hi.astype(F32)).astype(BF16)
    return (jnp.dot(hi, bmat, preferred_element_type=F32) + jnp.dot(lo, bmat, preferred_element_type=F32))


def _gelu(x):
    return 0.5 * x * (1.0 + jnp.tanh(GELU_K0 * (x + GELU_K1 * x * x * x)))


def _gelu_grad(x):
    t = jnp.tanh(GELU_K0 * (x + GELU_K1 * x * x * x))
    return 0.5 * (1.0 + t) + 0.5 * x * (1.0 - t * t) * GELU_K0 * (1.0 + 3.0 * GELU_K1 * x * x)


def _rms(x):
    return lax.rsqrt(jnp.mean(x * x, axis=-1, keepdims=True) + EPS)


def _rms_bwd(dy, x, g, r):
    dyg = dy * g
    dx = r * dyg - x * (r * r * r) * jnp.mean(dyg * x, axis=-1, keepdims=True)
    return dx, dy * x * r


def _pcall(body, *, name, grid, in_specs, out_specs, out_shape, scratch=(), prefetch=0):
    params = pltpu.CompilerParams(dimension_semantics=("arbitrary",) * len(grid), vmem_limit_bytes=VMEM_LIMIT_V7X)
    if prefetch:
        spec = pltpu.PrefetchScalarGridSpec(num_scalar_prefetch=prefetch, grid=grid, in_specs=in_specs,
                                            out_specs=out_specs, scratch_shapes=list(scratch))
        return pl.pallas_call(body, name=name, grid_spec=spec, out_shape=out_shape, compiler_params=params)
    return pl.pallas_call(body, name=name, grid=grid, in_specs=in_specs, out_specs=out_specs, out_shape=out_shape,
                          scratch_shapes=list(scratch), compiler_params=params)


def _tile(n, prefs):
    for p in prefs:
        if p <= n and n % p == 0:
            return p
    return n


def _acc_rows(ref, row, val):
    ref[row:row + 1, :] += jnp.sum(val, axis=0, keepdims=True)


def rms_mm(x, g, w3, *, name, tm=512):
    M, K = x.shape
    Q, _, C = w3.shape
    tm = _tile(M, (tm, 256))

    def body(x_ref, g_ref, w_ref, h_ref, o_ref):
        @pl.when(pl.program_id(1) == 0)
        def _():
            xv = x_ref[...]
            h_ref[...] = (xv * _rms(xv) * g_ref[...]).astype(h_ref.dtype)

        o_ref[...] = _dot(h_ref[...], w_ref[pl.program_id(1)])

    return _pcall(body, name=name, grid=(M // tm, Q),
                  in_specs=[BS((tm, K), lambda i, j: (i, 0)), BS((1, K), lambda i, j: (0, 0)),
                            BS((Q, K, C), lambda i, j: (0, 0, 0))],
                  out_specs=[BS((tm, K), lambda i, j: (i, 0)), BS((tm, C), lambda i, j: (i, j))],
                  out_shape=[SDS((M, K), MXU_DTYPE), SDS((M, Q * C), F32)])(x, g, w3)


def mm(a, w, *, name, res):
    M, K = a.shape
    N = w.shape[1]
    tm = _tile(M, (512, 256))

    def body(a_ref, w_ref, r_ref, o_ref):
        o_ref[...] = _dot(a_ref[...], w_ref[...]) + r_ref[...]

    return _pcall(body, name=name, grid=(M // tm,),
                  in_specs=[BS((tm, K), lambda i: (i, 0)), BS((K, N), lambda i: (0, 0)), BS((tm, N), lambda i: (i, 0))],
                  out_specs=BS((tm, N), lambda i: (i, 0)), out_shape=SDS((M, N), F32))(a, w, res)


def mm_nt(a, w3, *, name):
    M = a.shape[0]
    Q, N, Kc = w3.shape
    tm = _tile(M, (512, 256))

    def body(a_ref, w_ref, o_ref):
        acc = _dot(a_ref[:, 0:Kc], w_ref[0], NT)
        for q in range(1, Q):
            acc = acc + _dot(a_ref[:, q * Kc:(q + 1) * Kc], w_ref[q], NT)
        o_ref[...] = acc

    return _pcall(body, name=name, grid=(M // tm,),
                  in_specs=[BS((tm, Q * Kc), lambda i: (i, 0)), BS((Q, N, Kc), lambda i: (0, 0, 0))],
                  out_specs=BS((tm, N), lambda i: (i, 0)), out_shape=SDS((M, N), F32))(a, w3)


def mm_tn(a, b, *, name, out_dtype, chunks=1):
    M, K = a.shape
    N = b.shape[1]
    C = N // chunks
    tm = _tile(M, (512, 256))
    tk = _tile(K, (1408, 1024, 512))
    tn = _tile(C, (1408, 1024, 512))
    per = C // tn
    nm = M // tm

    def body(a_ref, b_ref, o_ref, acc):
        m = pl.program_id(2)

        @pl.when(m == 0)
        def _():
            acc[...] = jnp.zeros_like(acc)

        acc[...] += _dot(a_ref[...], b_ref[...], TN)

        @pl.when(m == nm - 1)
        def _():
            o_ref[...] = acc[...].astype(o_ref.dtype)

    return _pcall(body, name=name, grid=(K // tk, N // tn, nm),
                  in_specs=[BS((tm, tk), lambda k, n, m: (m, k)), BS((tm, tn), lambda k, n, m: (m, n))],
                  out_specs=BS((None, tk, tn), lambda k, n, m: (n // per, k, n % per)),
                  out_shape=SDS((chunks, K, C), out_dtype), scratch=[pltpu.VMEM((tk, tn), F32)])(a, b)


def _lane(shape):
    return lax.broadcasted_iota(jnp.int32, shape, 1)


def _norm_rope(slab, g, bmat, cos, sin, first):
    r = lax.rsqrt(_segsum(slab * slab, bmat) * (1.0 / HEAD_DIM) + EPS)
    qn = slab * r * g
    swapped = jnp.where(first, pltpu.roll(qn, 96, 1), pltpu.roll(qn, 32, 1))
    return qn * cos + swapped * sin


def mixer_pre(proj, cos, sin, gq, gk, gvn, bmat):
    S = proj.shape[0]
    tm = _tile(S, (256,))

    def body(p_ref, c_ref, s_ref, gq_ref, gk_ref, gvn_ref, b_ref, qr_ref, kr_ref, vb_ref, gu_ref, gvo_ref):
        cos_v, sin_v, bm = c_ref[...], s_ref[...], b_ref[...]
        first = (_lane((tm, 128)) & 63) < 32
        for s in range(4):
            sl = slice(s * 128, (s + 1) * 128)
            qr_ref[:, sl] = _norm_rope(p_ref[:, sl], gq_ref[...], bm, cos_v, sin_v, first).astype(qr_ref.dtype)
        for s in range(2):
            kr_ref[:, s * 128:(s + 1) * 128] = _norm_rope(p_ref[:, 512 + s * 128:640 + s * 128], gk_ref[...], bm,
                                                          cos_v, sin_v, first).astype(kr_ref.dtype)
        vb_ref[...] = p_ref[:, 768:1024].astype(vb_ref.dtype)
        gu_ref[...] = _gelu(p_ref[:, 1024:1536])
        gv = _gelu(p_ref[:, 1536:2048])
        gvo_ref[...] = (gv * _rms(gv) * gvn_ref[...]).astype(gvo_ref.dtype)

    row = lambda w: BS((tm, w), lambda i: (i, 0))
    const = lambda r, w: BS((r, w), lambda i: (0, 0))
    return _pcall(body, name="mixer_pre", grid=(S // tm,),
                  in_specs=[row(IN_COLS_DUP), row(128), row(128), const(1, 128), const(1, 128), const(1, 512),
                            const(128, 128)],
                  out_specs=[row(512), row(256), row(256), row(512), row(512)],
                  out_shape=[SDS((S, 512), MXU_DTYPE), SDS((S, 256), MXU_DTYPE), SDS((S, 256), MXU_DTYPE),
                             SDS((S, 512), F32), SDS((S, 512), MXU_DTYPE)])(proj, cos, sin, gq, gk, gvn, bmat)


def _swa_probs(qs, kd, sink, n, lo):
    z = jnp.zeros_like(qs)
    qp = jnp.concatenate([jnp.where(lo, qs, z), jnp.where(lo, z, qs)], axis=0)
    sc = _dot(qp, kd, NT) * (1.0 / math.sqrt(HEAD_DIM))
    r_i = lax.broadcasted_iota(jnp.int32, (2 * BLK, 2 * BLK), 0)
    k_j = lax.broadcasted_iota(jnp.int32, (2 * BLK, 2 * BLK), 1)
    diff = (r_i & (BLK - 1)) + BLK - k_j
    mask = (diff >= 0) & (diff < BLK) & ((k_j >= BLK) | (n > 0))
    sc = jnp.where(mask, sc, MINF)
    m = jnp.maximum(jnp.max(sc, axis=1, keepdims=True), sink)
    p = jnp.exp(sc - m)
    es = jnp.exp(sink - m)
    l = jnp.sum(p, axis=1, keepdims=True) + es
    return qp, p / l, es / l


def swa_fwd(qr, kr, vb, sinkcol, gao):
    S = qr.shape[0]
    nb = S // BLK

    def body(q_ref, kc_ref, kp_ref, vc_ref, vp_ref, sk_ref, g_ref, o_ref, ya_ref):
        n = pl.program_id(0)
        lo = _lane((BLK, 128)) < 64
        for s in range(4):
            h = s // 2
            hs = slice(h * 128, (h + 1) * 128)
            kd = jnp.concatenate([kp_ref[:, hs], kc_ref[:, hs]], axis=0)
            vd = jnp.concatenate([vp_ref[:, hs], vc_ref[:, hs]], axis=0)
            _, p, _ = _swa_probs(q_ref[:, s * 128:(s + 1) * 128], kd, sk_ref[s], n, lo)
            o2 = _dot(p, vd)
            o_ref[:, s * 128:(s + 1) * 128] = jnp.where(lo, o2[:BLK], o2[BLK:])
        a = o_ref[...]
        ya_ref[...] = (a * _rms(a) * g_ref[...]).astype(ya_ref.dtype)

    cur = lambda w: BS((BLK, w), lambda n: (n, 0))
    prev = lambda w: BS((BLK, w), lambda n: (jnp.maximum(n - 1, 0), 0))
    return _pcall(body, name="swa_fwd", grid=(nb,),
                  in_specs=[cur(512), cur(256), prev(256), cur(256), prev(256),
                            BS((4, 2 * BLK, 1), lambda n: (0, 0, 0)), BS((1, 512), lambda n: (0, 0))],
                  out_specs=[cur(512), cur(512)],
                  out_shape=[SDS((S, 512), F32), SDS((S, 512), MXU_DTYPE)])(qr, kr, kr, vb, vb, sinkcol, gao)


def gmlp_fwd(gvn, gu, ya, w2, bsl, ggo):
    S = gvn.shape[0]

    def body(gvn_ref, gu_ref, ya_ref, w2_ref, bsl_ref, g_ref, gm_ref, y_ref):
        lo = _lane((BLK, 128)) < 64
        for j in range(4):
            sl = slice(j * 128, (j + 1) * 128)
            m2 = _dot(w2_ref[j], gvn_ref[:, sl])
            mixed = jnp.where(lo, m2[:BLK], m2[BLK:]) + bsl_ref[j]
            gm_ref[:, sl] = gu_ref[:, sl] * mixed
        gm = gm_ref[...]
        y_ref[:, :512] = ya_ref[...]
        y_ref[:, 512:] = (gm * _rms(gm) * g_ref[...]).astype(y_ref.dtype)

    row = lambda w: BS((BLK, w), lambda n: (n, 0))
    return _pcall(body, name="gmlp_fwd", grid=(S // BLK,),
                  in_specs=[row(512), row(512), row(512), BS((4, 2 * BLK, BLK), lambda n: (0, 0, 0)),
                            BS((4, BLK, 128), lambda n: (0, 0, 0)), BS((1, 512), lambda n: (0, 0))],
                  out_specs=[row(512), row(1024)],
                  out_shape=[SDS((S, 512), F32), SDS((S, 1024), MXU_DTYPE)])(gvn, gu, ya, w2, bsl, ggo)


def mem_pre(kv, gxk):
    def body(kv_ref, g_ref, kn_ref, vb_ref):
        for h in range(XA_HEADS):
            sl = slice(h * XA_DH, (h + 1) * XA_DH)
            k = kv_ref[:, sl]
            kn_ref[:, sl] = (k * _rms(k) * g_ref[...]).astype(kn_ref.dtype)
        vb_ref[...] = kv_ref[:, 1024:2048].astype(vb_ref.dtype)

    full = lambda r, w: BS((r, w), lambda i: (0, 0))
    return _pcall(body, name="mem_pre", grid=(1,), in_specs=[full(MEM_LEN, 2048), full(1, XA_DH)],
                  out_specs=[full(MEM_LEN, 1024), full(MEM_LEN, 1024)],
                  out_shape=[SDS((MEM_LEN, 1024), MXU_DTYPE), SDS((MEM_LEN, 1024), MXU_DTYPE)])(kv, gxk)


def _xa_probs(qh, g, kn_h):
    r = _rms(qh)
    qn = qh * r * g
    s = _dot(qn, kn_h, NT) * (1.0 / math.sqrt(XA_DH))
    p = jnp.exp(s - jnp.max(s, axis=1, keepdims=True))
    return r, qn, p / jnp.sum(p, axis=1, keepdims=True)


def xattn_fwd(qx, kn, vb, gxq):
    S = qx.shape[0]
    tm = _tile(S, (256,))

    def body(q_ref, kn_ref, vb_ref, g_ref, o_ref):
        for h in range(XA_HEADS):
            sl = slice(h * XA_DH, (h + 1) * XA_DH)
            _, _, p = _xa_probs(q_ref[:, sl], g_ref[...], kn_ref[:, sl])
            o_ref[:, sl] = _dot(p, vb_ref[:, sl]).astype(o_ref.dtype)

    full = lambda r, w: BS((r, w), lambda i: (0, 0))
    return _pcall(body, name="xattn_fwd", grid=(S // tm,),
                  in_specs=[BS((tm, 1024), lambda i: (i, 0)), full(MEM_LEN, 1024), full(MEM_LEN, 1024), full(1, XA_DH)],
                  out_specs=BS((tm, 1024), lambda i: (i, 0)), out_shape=SDS((S, 1024), MXU_DTYPE))(qx, kn, vb, gxq)


def _causal_taps(a, halo_ref, first_tile, row):
    h6 = jnp.where(first_tile, 0.0, halo_ref[6:7, :])
    h7 = jnp.where(first_tile, 0.0, halo_ref[7:8, :])
    a1 = jnp.where(row == 0, h7, pltpu.roll(a, 1, 0))
    a2 = jnp.where(row == 0, h6, jnp.where(row == 1, h7, pltpu.roll(a, 2, 0)))
    return a1, a2


def _conv(a, a1, a2, w_ref, b_ref):
    return w_ref[2:3, :] * a + w_ref[1:2, :] * a1 + w_ref[0:1, :] * a2 + b_ref[...]


def _conv_specs(tm):
    halo_blocks = tm // 8
    return [BS((tm, D_FF), lambda i: (i, 0)), BS((tm, D_FF), lambda i: (i, 1)),
            BS((8, D_FF), lambda i: (jnp.maximum(i * halo_blocks - 1, 0), 0)),
            BS((8, D_FF), lambda i: (jnp.maximum(i * halo_blocks - 1, 0), 1)),
            BS((3, D_FF), lambda i: (0, 0)), BS((3, D_FF), lambda i: (0, 1)),
            BS((1, D_FF), lambda i: (0, 0)), BS((1, D_FF), lambda i: (0, 1))]


def convgate_fwd(a, cw, cb):
    S = a.shape[0]
    tm = _tile(S, (256,))

    def body(ag_ref, au_ref, hg_ref, hu_ref, wg_ref, wu_ref, bg_ref, bu_ref, f_ref):
        first_tile = pl.program_id(0) == 0
        row = lax.broadcasted_iota(jnp.int32, (tm, D_FF), 0)
        ag, au = ag_ref[...], au_ref[...]
        cg = _conv(ag, *_causal_taps(ag, hg_ref, first_tile, row), wg_ref, bg_ref)
        cu = _conv(au, *_causal_taps(au, hu_ref, first_tile, row), wu_ref, bu_ref)
        f_ref[...] = (_gelu(cg) * cu).astype(f_ref.dtype)

    return _pcall(body, name="convgate_fwd", grid=(S // tm,), in_specs=_conv_specs(tm),
                  out_specs=BS((tm, D_FF), lambda i: (i, 0)),
                  out_shape=SDS((S, D_FF), MXU_DTYPE))(a, a, a, a, cw, cw, cb, cb)


def loss_head(x3, target):
    S = x3.shape[0]
    tm = _tile(S, (512, 256))

    def body(x_ref, t_ref, d_ref, l_ref):
        @pl.when(pl.program_id(0) == 0)
        def _():
            l_ref[...] = jnp.zeros_like(l_ref)

        e = x_ref[...] - t_ref[...]
        d_ref[...] = e * (1.0 / D_MODEL)
        l_ref[...] += jnp.sum(e * e) * (0.5 / D_MODEL)

    row = BS((tm, D_MODEL), lambda i: (i, 0))
    return _pcall(body, name="loss_head", grid=(S // tm,), in_specs=[row, row],
                  out_specs=[row, BS((8, 128), lambda i: (0, 0))],
                  out_shape=[SDS((S, D_MODEL), F32), SDS((8, 128), F32)])(x3, target)


def convgate_bwd(a, df, cw, cb):
    S = a.shape[0]
    tm = _tile(S, (128,))

    def body(ag_ref, au_ref, hg_ref, hu_ref, wg_ref, wu_ref, bg_ref, bu_ref, df_ref, dc_ref, gw_ref):
        first_tile = pl.program_id(0) == 0

        @pl.when(first_tile)
        def _():
            gw_ref[...] = jnp.zeros_like(gw_ref)

        row = lax.broadcasted_iota(jnp.int32, (tm, D_FF), 0)
        ag, au, df_v = ag_ref[...], au_ref[...], df_ref[...]
        ag1, ag2 = _causal_taps(ag, hg_ref, first_tile, row)
        au1, au2 = _causal_taps(au, hu_ref, first_tile, row)
        cg = _conv(ag, ag1, ag2, wg_ref, bg_ref)
        cu = _conv(au, au1, au2, wu_ref, bu_ref)
        dcg = df_v * cu * _gelu_grad(cg)
        dcu = df_v * _gelu(cg)
        dc_ref[:, :D_FF] = dcg
        dc_ref[:, D_FF:] = dcu
        for col, dcv, taps in ((slice(0, D_FF), dcg, (ag2, ag1, ag)), (slice(D_FF, 2 * D_FF), dcu, (au2, au1, au))):
            for j in range(3):
                gw_ref[j:j + 1, col] += jnp.sum(dcv * taps[j], axis=0, keepdims=True)
            gw_ref[3:4, col] += jnp.sum(dcv, axis=0, keepdims=True)

    return _pcall(body, name="convgate_bwd", grid=(S // tm,),
                  in_specs=_conv_specs(tm) + [BS((tm, D_FF), lambda i: (i, 0))],
                  out_specs=[BS((tm, 2 * D_FF), lambda i: (i, 0)), BS((8, 2 * D_FF), lambda i: (0, 0))],
                  out_shape=[SDS((S, 2 * D_FF), F32), SDS((8, 2 * D_FF), F32)])(a, a, a, a, cw, cw, cb, cb, df)


def conv_transpose(dc, cw):
    S, C = dc.shape
    tm = _tile(S, (128,))
    nt = S // tm
    halo_blocks = tm // 8

    def body(dc_ref, halo_ref, w_ref, da_ref):
        last_tile = pl.program_id(0) == nt - 1
        row = lax.broadcasted_iota(jnp.int32, (tm, C), 0)
        h0 = jnp.where(last_tile, 0.0, halo_ref[0:1, :])
        h1 = jnp.where(last_tile, 0.0, halo_ref[1:2, :])
        dc_v = dc_ref[...]
        n1 = jnp.where(row == tm - 1, h0, pltpu.roll(dc_v, tm - 1, 0))
        n2 = jnp.where(row == tm - 1, h1, jnp.where(row == tm - 2, h0, pltpu.roll(dc_v, tm - 2, 0)))
        da_ref[...] = (w_ref[2:3, :] * dc_v + w_ref[1:2, :] * n1 + w_ref[0:1, :] * n2).astype(da_ref.dtype)

    return _pcall(body, name="conv_transpose", grid=(nt,),
                  in_specs=[BS((tm, C), lambda i: (i, 0)),
                            BS((8, C), lambda i: (jnp.minimum((i + 1) * halo_blocks, S // 8 - 1), 0)),
                            BS((3, C), lambda i: (0, 0))],
                  out_specs=BS((tm, C), lambda i: (i, 0)), out_shape=SDS((S, C), MXU_DTYPE))(dc, dc, cw)


def rms_bwd(dh, x, g, dres, *, name):
    S, W = x.shape
    tm = _tile(S, (512, 256))

    def body(dh_ref, x_ref, g_ref, dr_ref, dx_ref, dg_ref):
        @pl.when(pl.program_id(0) == 0)
        def _():
            dg_ref[...] = jnp.zeros_like(dg_ref)

        xv = x_ref[...]
        dx, dgc = _rms_bwd(dh_ref[...], xv, g_ref[...], _rms(xv))
        dx_ref[...] = dr_ref[...] + dx
        _acc_rows(dg_ref, 0, dgc)

    row = BS((tm, W), lambda i: (i, 0))
    return _pcall(body, name=name, grid=(S // tm,), in_specs=[row, row, BS((1, W), lambda i: (0, 0)), row],
                  out_specs=[row, BS((8, W), lambda i: (0, 0))],
                  out_shape=[SDS((S, W), F32), SDS((8, W), F32)])(dh, x, g, dres)


def xattn_bwd(qx, dxo, kn, vb, gxq):
    S = qx.shape[0]
    tm = _tile(S, (256,))

    def body(q_ref, do_ref, kn_ref, vb_ref, g_ref, dq_ref, dkn_ref, dv_ref, dg_ref):
        @pl.when(pl.program_id(0) == 0)
        def _():
            dkn_ref[...] = jnp.zeros_like(dkn_ref)
            dv_ref[...] = jnp.zeros_like(dv_ref)
            dg_ref[...] = jnp.zeros_like(dg_ref)

        g = g_ref[...]
        for h in range(XA_HEADS):
            sl = slice(h * XA_DH, (h + 1) * XA_DH)
            qh, do = q_ref[:, sl], do_ref[:, sl]
            r, qn, p = _xa_probs(qh, g, kn_ref[:, sl])
            dp = _dot(do, vb_ref[:, sl], NT)
            ds = p * (dp - jnp.sum(dp * p, axis=1, keepdims=True)) * (1.0 / math.sqrt(XA_DH))
            dqn = _dot(ds, kn_ref[:, sl])
            dkn_ref[:, sl] += _dot(ds, qn, TN)
            dv_ref[:, sl] += _dot(p, do, TN)
            dqh, dgc = _rms_bwd(dqn, qh, g, r)
            dq_ref[:, sl] = dqh.astype(dq_ref.dtype)
            _acc_rows(dg_ref, 0, dgc)

    row = BS((tm, 1024), lambda i: (i, 0))
    full = lambda r, w: BS((r, w), lambda i: (0, 0))
    return _pcall(body, name="xattn_bwd", grid=(S // tm,),
                  in_specs=[row, row, full(MEM_LEN, 1024), full(MEM_LEN, 1024), full(1, XA_DH)],
                  out_specs=[row, full(MEM_LEN, 1024), full(MEM_LEN, 1024), full(8, XA_DH)],
                  out_shape=[SDS((S, 1024), MXU_DTYPE), SDS((MEM_LEN, 1024), F32), SDS((MEM_LEN, 1024), F32),
                             SDS((8, XA_DH), F32)])(qx, dxo, kn, vb, gxq)


def mem_bwd(kv, dkn, dvb, gxk):
    def body(kv_ref, dkn_ref, dv_ref, g_ref, dkv_ref, dg_ref):
        dg_ref[...] = jnp.zeros_like(dg_ref)
        for h in range(XA_HEADS):
            sl = slice(h * XA_DH, (h + 1) * XA_DH)
            k = kv_ref[:, sl]
            dk, dgc = _rms_bwd(dkn_ref[:, sl], k, g_ref[...], _rms(k))
            dkv_ref[:, sl] = dk.astype(dkv_ref.dtype)
            _acc_rows(dg_ref, 0, dgc)
        dkv_ref[:, 1024:2048] = dv_ref[...].astype(dkv_ref.dtype)

    full = lambda r, w: BS((r, w), lambda i: (0, 0))
    return _pcall(body, name="mem_bwd", grid=(1,),
                  in_specs=[full(MEM_LEN, 2048), full(MEM_LEN, 1024), full(MEM_LEN, 1024), full(1, XA_DH)],
                  out_specs=[full(MEM_LEN, 2048), full(8, XA_DH)],
                  out_shape=[SDS((MEM_LEN, 2048), MXU_DTYPE), SDS((8, XA_DH), F32)])(kv, dkn, dvb, gxk)


def mixer_post_bwd(dy, attn, gm, gao, ggo):
    S = dy.shape[0]
    tm = _tile(S, (256,))

    def body(dy_ref, a_ref, gm_ref, gao_ref, ggo_ref, da_ref, dgm_ref, dg_ref):
        @pl.when(pl.program_id(0) == 0)
        def _():
            dg_ref[...] = jnp.zeros_like(dg_ref)

        a, gmv = a_ref[...], gm_ref[...]
        da, dga = _rms_bwd(dy_ref[:, :512], a, gao_ref[...], _rms(a))
        dgm, dgg = _rms_bwd(dy_ref[:, 512:], gmv, ggo_ref[...], _rms(gmv))
        da_ref[...] = da
        dgm_ref[...] = dgm
        dg_ref[0:1, :512] += jnp.sum(dga, axis=0, keepdims=True)
        dg_ref[0:1, 512:] += jnp.sum(dgg, axis=0, keepdims=True)

    row = lambda w: BS((tm, w), lambda i: (i, 0))
    const = lambda r, w: BS((r, w), lambda i: (0, 0))
    return _pcall(body, name="mixer_post_bwd", grid=(S // tm,),
                  in_specs=[row(1024), row(512), row(512), const(1, 512), const(1, 512)],
                  out_specs=[row(512), row(512), const(8, 1024)],
                  out_shape=[SDS((S, 512), F32), SDS((S, 512), F32), SDS((8, 1024), F32)])(dy, attn, gm, gao, ggo)


def gmlp_bwd(dgm, gvn, gu, w2, w2t, bsl):
    S = dgm.shape[0]

    def body(dgm_ref, gvn_ref, gu_ref, w2_ref, w2t_ref, bsl_ref, dgu_ref, dgvn_ref, dws_ref, dbl_ref):
        @pl.when(pl.program_id(0) == 0)
        def _():
            dws_ref[...] = jnp.zeros_like(dws_ref)
            dbl_ref[...] = jnp.zeros_like(dbl_ref)

        lo = _lane((BLK, 128)) < 64
        for j in range(4):
            sl = slice(j * 128, (j + 1) * 128)
            gvn_s = gvn_ref[:, sl]
            m2 = _dot(w2_ref[j], gvn_s)
            mixed = jnp.where(lo, m2[:BLK], m2[BLK:]) + bsl_ref[j]
            dgm_s = dgm_ref[:, sl]
            dgu_ref[:, sl] = dgm_s * mixed
            dmx = dgm_s * gu_ref[:, sl]
            d2 = _dot(w2t_ref[j], dmx)
            dgvn_ref[:, sl] = jnp.where(lo, d2[:BLK], d2[BLK:])
            z = jnp.zeros_like(dmx)
            dws_ref[2 * j] += _dot(jnp.where(lo, dmx, z), gvn_s, NT)
            dws_ref[2 * j + 1] += _dot(jnp.where(lo, z, dmx), gvn_s, NT)
            dbl_ref[j] += dmx

    row = lambda w: BS((BLK, w), lambda n: (n, 0))
    const3 = lambda a, b, c: BS((a, b, c), lambda n: (0, 0, 0))
    return _pcall(body, name="gmlp_bwd", grid=(S // BLK,),
                  in_specs=[row(512), row(512), row(512), const3(4, 2 * BLK, BLK), const3(4, 2 * BLK, BLK),
                            const3(4, BLK, 128)],
                  out_specs=[row(512), row(512), const3(8, BLK, BLK), const3(4, BLK, 128)],
                  out_shape=[SDS((S, 512), F32), SDS((S, 512), F32), SDS((8, BLK, BLK), F32),
                             SDS((4, BLK, 128), F32)])(dgm, gvn, gu, w2, w2t, bsl)


def swa_bwd(qr, kr, vb, sinkcol, dattn):
    S = qr.shape[0]
    nb = S // BLK

    def body(q_ref, kc_ref, kp_ref, vc_ref, vp_ref, sk_ref, do_ref, dq_ref, dk_ref, dv_ref, dsk_ref,
             carry_k, carry_v, prev_k, prev_v):
        n = pl.program_id(0)

        @pl.when(n == 0)
        def _():
            dsk_ref[...] = jnp.zeros_like(dsk_ref)
            carry_k[...] = jnp.zeros_like(carry_k)
            carry_v[...] = jnp.zeros_like(carry_v)

        @pl.when(n < nb)
        def _():
            lo = _lane((BLK, 128)) < 64
            for h in range(2):
                hs = slice(h * 128, (h + 1) * 128)
                kd = jnp.concatenate([kp_ref[:, hs], kc_ref[:, hs]], axis=0)
                vd = jnp.concatenate([vp_ref[:, hs], vc_ref[:, hs]], axis=0)
                dkd = jnp.zeros((2 * BLK, 128), F32)
                dvd = jnp.zeros((2 * BLK, 128), F32)
                for s in (2 * h, 2 * h + 1):
                    sl = slice(s * 128, (s + 1) * 128)
                    qp, p, psink = _swa_probs(q_ref[:, sl], kd, sk_ref[s], n, lo)
                    do = do_ref[:, sl]
                    z = jnp.zeros_like(do)
                    dop = jnp.concatenate([jnp.where(lo, do, z), jnp.where(lo, z, do)], axis=0)
                    dp = _dot(dop, vd, NT)
                    delta = jnp.sum(dp * p, axis=1, keepdims=True)
                    ds = p * (dp - delta) * (1.0 / math.sqrt(HEAD_DIM))
                    dsk_ref[s] += -psink * delta
                    dq2 = _dot(ds, kd)
                    dq_ref[:, sl] = jnp.where(lo, dq2[:BLK], dq2[BLK:])
                    dkd = dkd + _dot(ds, qp, TN)
                    dvd = dvd + _dot(p, dop, TN)
                prev_k[:, hs] = carry_k[:, hs] + dkd[:BLK]
                prev_v[:, hs] = carry_v[:, hs] + dvd[:BLK]
                carry_k[:, hs] = dkd[BLK:]
                carry_v[:, hs] = dvd[BLK:]

        @pl.when(n == nb)
        def _():
            prev_k[...] = carry_k[...]
            prev_v[...] = carry_v[...]

        dk_ref[...] = prev_k[...]
        dv_ref[...] = prev_v[...]

    last = nb - 1
    cur = lambda w: BS((BLK, w), lambda n: (jnp.minimum(n, last), 0))
    prev = lambda w: BS((BLK, w), lambda n: (jnp.clip(n - 1, 0, last), 0))
    done = lambda w: BS((BLK, w), lambda n: (jnp.maximum(n - 1, 0), 0))
    return _pcall(body, name="swa_bwd", grid=(nb + 1,),
                  in_specs=[cur(512), cur(256), prev(256), cur(256), prev(256),
                            BS((4, 2 * BLK, 1), lambda n: (0, 0, 0)), cur(512)],
                  out_specs=[cur(512), done(256), done(256), BS((4, 2 * BLK, 1), lambda n: (0, 0, 0))],
                  out_shape=[SDS((S, 512), F32), SDS((S, 256), F32), SDS((S, 256), F32), SDS((4, 2 * BLK, 1), F32)],
                  scratch=[pltpu.VMEM((BLK, 256), F32)] * 4)(qr, kr, kr, vb, vb, sinkcol, dattn)


def mixer_pre_bwd(proj, cos, sin, gq, gk, gvn, bmat, dqr, dkr, dvb, dgu, dgvn):
    S = proj.shape[0]
    tm = _tile(S, (256,))

    def body(p_ref, c_ref, s_ref, gq_ref, gk_ref, gvn_ref, b_ref, dqr_ref, dkr_ref, dvb_ref, dgu_ref, dgvn_ref,
             dp_ref, dgq_ref, dgk_ref, dgv_ref):
        @pl.when(pl.program_id(0) == 0)
        def _():
            dgq_ref[...] = jnp.zeros_like(dgq_ref)
            dgk_ref[...] = jnp.zeros_like(dgk_ref)
            dgv_ref[...] = jnp.zeros_like(dgv_ref)

        cos_v, sin_v, bm = c_ref[...], s_ref[...], b_ref[...]
        first = (_lane((tm, 128)) & 63) < 32

        def slab_bwd(slab, dout, g, dg_ref):
            r = lax.rsqrt(_segsum(slab * slab, bm) * (1.0 / HEAD_DIM) + EPS)
            ds = dout * sin_v
            dqn = dout * cos_v + jnp.where(first, pltpu.roll(ds, 96, 1), pltpu.roll(ds, 32, 1))
            dyg = dqn * g
            dx = r * dyg - slab * (r * r * r) * (_segsum(dyg * slab, bm) * (1.0 / HEAD_DIM))
            _acc_rows(dg_ref, 0, dqn * slab * r)
            return dx

        for s in range(4):
            sl = slice(s * 128, (s + 1) * 128)
            dp_ref[:, sl] = slab_bwd(p_ref[:, sl], dqr_ref[:, sl], gq_ref[...], dgq_ref).astype(dp_ref.dtype)
        for s in range(2):
            sl = slice(512 + s * 128, 640 + s * 128)
            dp_ref[:, sl] = slab_bwd(p_ref[:, sl], dkr_ref[:, s * 128:(s + 1) * 128], gk_ref[...],
                                     dgk_ref).astype(dp_ref.dtype)
        dp_ref[:, 768:1024] = dvb_ref[...].astype(dp_ref.dtype)
        dp_ref[:, 1024:1536] = (dgu_ref[...] * _gelu_grad(p_ref[:, 1024:1536])).astype(dp_ref.dtype)
        gvp = p_ref[:, 1536:2048]
        gv = _gelu(gvp)
        dgv, dgc = _rms_bwd(dgvn_ref[...], gv, gvn_ref[...], _rms(gv))
        dp_ref[:, 1536:2048] = (dgv * _gelu_grad(gvp)).astype(dp_ref.dtype)
        _acc_rows(dgv_ref, 0, dgc)

    row = lambda w: BS((tm, w), lambda i: (i, 0))
    const = lambda r, w: BS((r, w), lambda i: (0, 0))
    return _pcall(body, name="mixer_pre_bwd", grid=(S // tm,),
                  in_specs=[row(IN_COLS_DUP), row(128), row(128), const(1, 128), const(1, 128), const(1, 512),
                            const(128, 128), row(512), row(256), row(256), row(512), row(512)],
                  out_specs=[row(IN_COLS_DUP), const(8, 128), const(8, 128), const(8, 512)],
                  out_shape=[SDS((S, IN_COLS_DUP), MXU_DTYPE), SDS((8, 128), F32), SDS((8, 128), F32),
                             SDS((8, 512), F32)])(proj, cos, sin, gq, gk, gvn, bmat, dqr, dkr, dvb, dgu, dgvn)


BIG = (("w_in", (1024, 448), True), ("w_out", (256, 1024), False), ("xa_wq", (256, 1024), False),
       ("xa_wkv", (1024, 512), True), ("xa_wo", (256, 1024), False), ("ffn_up", (1024, 1408), True),
       ("ffn_down", (704, 1024), False))
BIG_NAMES = tuple(n for n, _, _ in BIG)
SMALL_VECS = (("mix_norm", 1024), ("q_norm", 64), ("k_norm", 64), ("attn_sinks", 8), ("gmlp_v_norm", 512),
              ("attn_out_norm", 512), ("gmlp_out_norm", 512), ("xa_norm", 1024), ("mem_norm", 1024),
              ("xa_q_norm", 256), ("xa_k_norm", 256), ("ffn_norm", 1024), ("ffn_conv_b", 5632))
SMALL = tuple(n for n, _ in SMALL_VECS) + ("gmlp_bs", "gmlp_ws", "ffn_conv")
WEIGHTS = ("mix_norm", "w_in", "q_norm", "k_norm", "attn_sinks", "gmlp_v_norm", "gmlp_ws", "gmlp_bs",
           "attn_out_norm", "gmlp_out_norm", "w_out", "xa_norm", "mem_norm", "xa_wq", "xa_wkv", "xa_q_norm",
           "xa_k_norm", "xa_wo", "ffn_norm", "ffn_up", "ffn_conv", "ffn_conv_b", "ffn_down")
CONV_SHARD = (3, 1408)
CONV_LANE_ROWS = CONV_SHARD[1] // 128
CONV_CHIP_ROWS = 40


def _small_rows():
    rows, r = {}, 0
    for n, length in SMALL_VECS:
        rows[n] = r
        r += -(-length // 128)
    r += -r % 8
    rows["gmlp_bs"] = r
    r += 8
    rows["gmlp_ws"] = r
    r += 8 * BLK
    rows["ffn_conv"] = r
    r += N_CHIPS * CONV_CHIP_ROWS
    return rows, r


SMALL_ROW, SMALL_ROWS = _small_rows()


def pack_small(dg_mix, dgq, dgk, dsk, dg_gvn, dg_y, dg_xa, dg_mem, dg_xq, dg_xk, dg_ffn, gcw, dbl, dws):
    def body(mix_ref, q_ref, k_ref, sk_ref, gvn_ref, y_ref, xa_ref, mem_ref, xq_ref, xk_ref, ffn_ref, cw_ref,
             dbl_ref, dws_ref, o_ref):
        o_ref[...] = jnp.zeros_like(o_ref)
        lane = _lane((1, 128))

        def put(name, src_ref, row, lane0, length):
            for k in range(length // 128):
                o_ref[SMALL_ROW[name] + k:SMALL_ROW[name] + k + 1, :] = src_ref[row:row + 1, lane0 + k * 128:lane0 + (k + 1) * 128]

        put("mix_norm", mix_ref, 0, 0, 1024)
        for name, ref in (("q_norm", q_ref), ("k_norm", k_ref)):
            v = ref[0:1, :]
            o_ref[SMALL_ROW[name]:SMALL_ROW[name] + 1, :] = jnp.where(lane < HEAD_DIM, v + pltpu.roll(v, 64, 1), 0.0)
        sinks = jnp.zeros((1, 128), F32)
        for s in range(4):
            col = sk_ref[s]
            sinks = sinks + jnp.where(lane == 2 * s, jnp.sum(col[:BLK]), 0.0) + jnp.where(lane == 2 * s + 1, jnp.sum(col[BLK:]), 0.0)
        o_ref[SMALL_ROW["attn_sinks"]:SMALL_ROW["attn_sinks"] + 1, :] = sinks
        put("gmlp_v_norm", gvn_ref, 0, 0, 512)
        put("attn_out_norm", y_ref, 0, 0, 512)
        put("gmlp_out_norm", y_ref, 0, 512, 512)
        put("xa_norm", xa_ref, 0, 0, 1024)
        put("mem_norm", mem_ref, 0, 0, 1024)
        put("xa_q_norm", xq_ref, 0, 0, 256)
        put("xa_k_norm", xk_ref, 0, 0, 256)
        put("ffn_norm", ffn_ref, 0, 0, 1024)
        put("ffn_conv_b", cw_ref, 3, 0, 2 * D_FF)
        r8 = lax.broadcasted_iota(jnp.int32, (8, 128), 0)
        l8 = _lane((8, 128))
        bs = jnp.zeros((8, BLK), F32)
        for j in range(4):
            sel = (((r8 == 2 * j) & (l8 < 64)) | ((r8 == 2 * j + 1) & (l8 >= 64))).astype(F32).astype(BF16)
            xj = dbl_ref[j]
            hi = xj.astype(BF16)
            lo = (xj - hi.astype(F32)).astype(BF16)
            bs = bs + lax.dot_general(sel, hi, NT, preferred_element_type=F32) + lax.dot_general(sel, lo, NT, preferred_element_type=F32)
        o_ref[SMALL_ROW["gmlp_bs"]:SMALL_ROW["gmlp_bs"] + 8, :] = bs
        causal = lax.broadcasted_iota(jnp.int32, (BLK, BLK), 0) >= lax.broadcasted_iota(jnp.int32, (BLK, BLK), 1)
        for h in range(8):
            r0 = SMALL_ROW["gmlp_ws"] + h * BLK
            o_ref[r0:r0 + BLK, :] = jnp.where(causal, dws_ref[h], 0.0)
        for q in range(N_CHIPS):
            for j in range(3):
                for k in range(CONV_LANE_ROWS):
                    r0 = SMALL_ROW["ffn_conv"] + q * CONV_CHIP_ROWS + j * CONV_LANE_ROWS + k
                    l0 = (q * CONV_LANE_ROWS + k) * 128
                    o_ref[r0:r0 + 1, :] = cw_ref[j:j + 1, l0:l0 + 128]

    args = (dg_mix, dgq, dgk, dsk, dg_gvn, dg_y, dg_xa, dg_mem, dg_xq, dg_xk, dg_ffn, gcw, dbl, dws)
    full = lambda a: BS(a.shape, lambda i, nd=a.ndim: (0,) * nd)
    return _pcall(body, name="pack_small", grid=(1,), in_specs=[full(a) for a in args],
                  out_specs=BS((SMALL_ROWS, 128), lambda i: (0, 0)), out_shape=SDS((SMALL_ROWS, 128), F32))(*args)


def _adam(w, g, m, v):
    mn = ADAM_B1 * m + (1.0 - ADAM_B1) * g
    vn = ADAM_B2 * v + (1.0 - ADAM_B2) * (g * g)
    m_hat = mn / (1.0 - ADAM_B1 ** ADAM_STEP)
    v_hat = vn / (1.0 - ADAM_B2 ** ADAM_STEP)
    return -ADAM_LR * (m_hat / (jnp.sqrt(v_hat) + ADAM_EPS) + ADAM_WD * w), mn, vn


def adamw_small(gsum, w, m, v, chipvec):
    n = len(SMALL)

    def body(chip_ref, g_ref, *refs):
        w_refs, m_refs, v_refs = refs[:n], refs[n:2 * n], refs[2 * n:3 * n]
        outs = refs[3 * n:]
        go, do, mo, vo = outs[:n], outs[n:2 * n], outs[2 * n:3 * n], outs[3 * n:]

        def update(i, idx, g):
            d, mn, vn = _adam(w_refs[i][idx], g, m_refs[i][idx], v_refs[i][idx])
            go[i][idx] = g
            do[i][idx] = d
            mo[i][idx] = mn
            vo[i][idx] = vn

        for i, (name, length) in enumerate(SMALL_VECS):
            for k in range(-(-length // 128)):
                wd = min(128, length - k * 128)
                r = SMALL_ROW[name] + k
                update(i, (slice(0, 1), slice(k * 128, k * 128 + wd)), g_ref[r:r + 1, 0:wd])
        i_bs, i_ws, i_cv = len(SMALL_VECS), len(SMALL_VECS) + 1, len(SMALL_VECS) + 2
        update(i_bs, (0,), g_ref[SMALL_ROW["gmlp_bs"]:SMALL_ROW["gmlp_bs"] + 8, :])
        for h in range(8):
            r0 = SMALL_ROW["gmlp_ws"] + h * BLK
            update(i_ws, (0, h), g_ref[r0:r0 + BLK, :])
        mine = g_ref[pl.ds(pl.multiple_of(SMALL_ROW["ffn_conv"] + chip_ref[0] * CONV_CHIP_ROWS, 8), CONV_CHIP_ROWS), :]
        for j in range(3):
            for k in range(CONV_LANE_ROWS):
                r = j * CONV_LANE_ROWS + k
                update(i_cv, (0, slice(j, j + 1), slice(k * 128, (k + 1) * 128)), mine[r:r + 1, :])

    nat = [w[nm] for nm in SMALL]
    full = lambda a: BS(a.shape, lambda i, c, nd=a.ndim: (0,) * nd)
    outs = _pcall(body, name="adamw_small", grid=(1,), prefetch=1,
                  in_specs=[BS((SMALL_ROWS, 128), lambda i, c: (0, 0))] + [full(a) for a in nat] * 3,
                  out_specs=[full(a) for a in nat] * 4, out_shape=[SDS(a.shape, F32) for a in nat] * 4)(
        chipvec, gsum, *nat, *[m[nm] for nm in SMALL], *[v[nm] for nm in SMALL])
    return outs[:n], outs[n:2 * n], outs[2 * n:3 * n], outs[3 * n:]


def adamw_matrix(w, m, v, g_own, g_other, cvec, *, name):
    _, r, c = w.shape
    half = r // 2
    tr = _tile(half, (128, 176))
    T = half // tr

    def body(c_ref, w_ref, m_ref, v_ref, own_ref, oth_ref, g_ref, d_ref, mo_ref, vo_ref):
        g = jnp.where(pl.program_id(0) == c_ref[0], own_ref[...], oth_ref[...])
        d, mn, vn = _adam(w_ref[...], g, m_ref[...], v_ref[...])
        g_ref[...] = g
        d_ref[...] = d
        mo_ref[...] = mn
        vo_ref[...] = vn

    nat = BS((None, tr, c), lambda hf, t, cr: (0, hf * T + t, 0))
    hlf = BS((tr, c), lambda hf, t, cr: (t, 0))
    return _pcall(body, name=name, grid=(2, T), prefetch=1, in_specs=[nat, nat, nat, hlf, hlf], out_specs=[nat] * 4,
                  out_shape=[SDS(w.shape, F32)] * 4)(cvec, w, m, v, g_own, g_other)


def _place():
    return lax.axis_index("x"), lax.axis_index("y"), lax.axis_index("c")


def _other_chips(x, y):
    return [(1 - x, y), (x, 1 - y), (1 - x, 1 - y)]


def _rows_of_core(c, half):
    return pl.ds(pl.multiple_of(c * half, 16), half)


def _rcopy(src, dst, sems, k, to):
    return pltpu.make_async_remote_copy(src_ref=src, dst_ref=dst, send_sem=sems[0].at[k], recv_sem=sems[1].at[k],
                                        device_id=to, device_id_type=MESH)


def _comm_call(body, *, name, out_shape, n_in, n_sems, aliases=None):
    return pl.pallas_call(body, name=name, out_shape=out_shape, in_specs=[ANY] * n_in, out_specs=[ANY] * len(out_shape),
                          scratch_shapes=[pltpu.SemaphoreType.DMA((n_sems,)), pltpu.SemaphoreType.DMA((n_sems,))],
                          input_output_aliases=aliases or {},
                          compiler_params=pltpu.CompilerParams(has_side_effects=True))


def cast_shards(shards, conv, chipvec):
    n = len(shards)

    def body(chip_ref, *refs):
        for i_ref, o_ref in zip(refs[:n + 1], refs[n + 1:]):
            o_ref[...] = i_ref[...].astype(o_ref.dtype)

    in_specs = [BS((s.shape[0] // 4, s.shape[1]), lambda i, p: (i, 0)) for s in shards]
    in_specs.append(BS(conv.shape, lambda i, p: (0, 0)))
    out_specs = [BS((None, s.shape[0] // 4, s.shape[1]), lambda i, p: (p[0], i, 0)) for s in shards]
    out_specs.append(BS((None,) + conv.shape, lambda i, p: (p[0], 0, 0)))
    out_shape = [SDS((N_CHIPS,) + s.shape, MXU_DTYPE) for s in shards] + [SDS((N_CHIPS,) + conv.shape, F32)]
    return _pcall(body, name="cast_shards", grid=(4,), prefetch=1, in_specs=in_specs, out_specs=out_specs,
                  out_shape=out_shape)(chipvec, *shards, conv)


HBM = pl.BlockSpec(memory_space=pltpu.HBM)
SEM = pl.BlockSpec(memory_space=pltpu.SEMAPHORE)
DATAFLOW = pltpu.SideEffectType.DATAFLOW_SIDE_EFFECTING


def _gather_copies(bufs, send_sems, recv_sems, outgoing):
    x, y, c = _place()
    p = 2 * x + y
    cps = []
    for i, o in enumerate(bufs):
        for j, (cx, cy) in enumerate(_other_chips(x, y)):
            slot = o.at[p] if outgoing else o.at[2 * cx + cy]
            cps.append(_rcopy(slot, slot, (send_sems, recv_sems), 3 * i + j, (cx, cy, c)))
    return cps


def gather_start(slots):
    n = len(slots)

    def body(*refs):
        send_sems, recv_sems, thru = refs[n], refs[n + 1], refs[n + 2:]
        for cp in _gather_copies(thru, send_sems, recv_sems, True):
            cp.start()

    hbm = [pltpu.with_memory_space_constraint(s, pltpu.HBM) for s in slots]
    outs = pl.pallas_call(
        body, name="gather_start_%d" % n,
        out_shape=[pltpu.SemaphoreType.DMA((3 * n,)), pltpu.SemaphoreType.DMA((3 * n,))]
        + [pltpu.HBM(s.shape, s.dtype) for s in slots],
        in_specs=[HBM] * n, out_specs=[SEM, SEM] + [HBM] * n, input_output_aliases={i: 2 + i for i in range(n)},
        compiler_params=pltpu.CompilerParams(has_side_effects=DATAFLOW))(*hbm)
    return outs[0], outs[1], outs[2:]


def gather_wait(send_sems, recv_sems, bufs, after):
    n = len(bufs)

    def body(*refs):
        ins, send_ref, recv_ref = refs[:n], refs[n], refs[n + 1]
        for cp in _gather_copies(ins, send_ref, recv_ref, False):
            cp.wait_send()
            cp.wait_recv()

    return pl.pallas_call(
        body, name="gather_wait_%d" % n, out_shape=[pltpu.HBM(s.shape, s.dtype) for s in bufs],
        in_specs=[HBM] * n + [SEM, SEM, ANY], out_specs=[HBM] * n, input_output_aliases={i: i for i in range(n)},
        compiler_params=pltpu.CompilerParams(has_side_effects=DATAFLOW))(*bufs, send_sems, recv_sems, after)


def _peers(x, y, c):
    return [(1 - x if k & 4 else x, 1 - y if k & 2 else y, 1 - c if k & 1 else c) for k in range(1, N_DEV)]


def _partial_copies(g_ref, land_ref, send_sems, recv_sems, outgoing):
    x, y, c = _place()
    half = g_ref.shape[1] // 2
    cps = []
    for k, (px, py, pc) in enumerate(_peers(x, y, c)):
        src = g_ref.at[2 * px + py, _rows_of_core(pc, half)]
        dst = land_ref.at[4 * x + 2 * y + c] if outgoing else land_ref.at[4 * px + 2 * py + pc]
        cps.append(_rcopy(src, dst, (send_sems, recv_sems), k, (px, py, pc)))
    return cps


def partials_start(g, *, name):
    land = lax.empty((N_DEV, g.shape[1] // 2, g.shape[2]), g.dtype)

    def body(g_ref, land_ref, send_sems, recv_sems, g_thru, land_thru):
        for cp in _partial_copies(g_thru, land_thru, send_sems, recv_sems, True):
            cp.start()

    return pl.pallas_call(
        body, name=name,
        out_shape=[pltpu.SemaphoreType.DMA((N_DEV - 1,)), pltpu.SemaphoreType.DMA((N_DEV - 1,)),
                   pltpu.HBM(g.shape, g.dtype), pltpu.HBM(land.shape, land.dtype)],
        in_specs=[HBM, HBM], out_specs=[SEM, SEM, HBM, HBM], input_output_aliases={0: 2, 1: 3},
        compiler_params=pltpu.CompilerParams(has_side_effects=DATAFLOW))(
        pltpu.with_memory_space_constraint(g, pltpu.HBM), pltpu.with_memory_space_constraint(land, pltpu.HBM))


def partials_wait(started, after):
    n = len(started)

    def body(*refs):
        for i in range(n):
            send_ref, recv_ref, g_ref, land_ref = refs[4 * i:4 * i + 4]
            for cp in _partial_copies(g_ref, land_ref, send_ref, recv_ref, False):
                cp.wait_send()
                cp.wait_recv()

    flat = [a for s in started for a in s]
    bufs = [a for s in started for a in s[2:]]
    outs = pl.pallas_call(
        body, name="partials_wait", out_shape=[pltpu.HBM(b.shape, b.dtype) for b in bufs],
        in_specs=[SEM, SEM, HBM, HBM] * n + [ANY], out_specs=[HBM] * (2 * n),
        input_output_aliases={4 * i + 2 + j: 2 * i + j for i in range(n) for j in range(2)},
        compiler_params=pltpu.CompilerParams(has_side_effects=DATAFLOW))(*flat, after)
    return [(outs[2 * i], outs[2 * i + 1]) for i in range(n)]


def sum_partials(pairs, order):
    n = len(pairs)

    def body(o_ref, *refs):
        j = pl.program_id(0)
        for g_ref, l_ref, f_ref in zip(refs[:n], refs[n:2 * n], refs[2 * n:]):
            @pl.when(j == 0)
            def _():
                f_ref[...] = g_ref[...].astype(F32)

            @pl.when(j > 0)
            def _():
                f_ref[...] += l_ref[...].astype(F32)

    g4 = [g.reshape(g.shape[0], 2, g.shape[1] // 2, g.shape[2]) for g, _ in pairs]
    lands = [l for _, l in pairs]
    return _pcall(body, name="sum_partials", grid=(N_DEV,), prefetch=1,
                  in_specs=[BS((None, None) + g.shape[2:], lambda j, o: (o[0], o[1], 0, 0)) for g in g4]
                  + [BS((None,) + l.shape[1:], lambda j, o: (o[jnp.maximum(j, 1) + 1], 0, 0)) for l in lands],
                  out_specs=[BS(l.shape[1:], lambda j, o: (0, 0)) for l in lands],
                  out_shape=[SDS(l.shape[1:], F32) for l in lands])(order, *g4, *lands)


def pair_share(fs):
    n = len(fs)

    def body(*refs):
        f_refs, o_refs, sems = refs[:n], refs[n:2 * n], refs[2 * n:]
        x, y, c = _place()
        cps = [_rcopy(f, o, sems, i, (x, y, 1 - c)) for i, (f, o) in enumerate(zip(f_refs, o_refs))]
        for cp in cps:
            cp.start()
        for cp in cps:
            cp.wait()

    return _comm_call(body, name="pair_share", n_in=n, n_sems=n, out_shape=[SDS(f.shape, f.dtype) for f in fs])(*fs)


def gather_all(sm):
    rows, width = sm.shape

    def body(s_ref, o_ref, send_sems, recv_sems, local_sem):
        x, y, c = _place()
        me = 4 * x + 2 * y + c
        sems = (send_sems, recv_sems)
        mine = pltpu.make_async_copy(s_ref, o_ref.at[me], local_sem)
        mine.start()
        peers = [(1 - x if k & 4 else x, 1 - y if k & 2 else y, 1 - c if k & 1 else c) for k in range(1, N_DEV)]
        sends = [_rcopy(s_ref, o_ref.at[me], sems, k, peer) for k, peer in enumerate(peers)]
        for cp in sends:
            cp.start()
        for k, (px, py, pc) in enumerate(peers):
            _rcopy(s_ref, o_ref.at[4 * px + 2 * py + pc], sems, k, (px, py, pc)).wait_recv()
        for cp in sends:
            cp.wait_send()
        mine.wait()

    return pl.pallas_call(body, name="gather_all", out_shape=SDS((N_DEV, rows, width), sm.dtype), in_specs=[ANY],
                          out_specs=ANY,
                          scratch_shapes=[pltpu.SemaphoreType.DMA((N_DEV - 1,)), pltpu.SemaphoreType.DMA((N_DEV - 1,)),
                                          pltpu.SemaphoreType.DMA],
                          compiler_params=pltpu.CompilerParams(has_side_effects=True))(sm)


def sum_slots(r, *, name):
    n, rows, width = r.shape
    tr = _tile(rows, (184, 8))

    def body(r_ref, o_ref):
        acc = r_ref[0]
        for s in range(1, n):
            acc = acc + r_ref[s]
        o_ref[...] = acc

    return _pcall(body, name=name, grid=(rows // tr,), in_specs=[BS((n, tr, width), lambda i: (0, i, 0))],
                  out_specs=BS((tr, width), lambda i: (i, 0)), out_shape=SDS((rows, width), F32))(r)


def _to_full(blk, col):
    n, r, c = blk.shape
    return blk.transpose(1, 0, 2).reshape(r, n * c) if col else blk.reshape(n * r, c)


def _dup_cols(w):
    dup = lambda t: jnp.concatenate([t[:, :64], t[:, :64], t[:, 64:], t[:, 64:]], axis=1)
    return jnp.concatenate([w[:, :512], dup(w[:, 512:640]), dup(w[:, 640:768]), w[:, 768:]], axis=1)


def _fold_cols(d):
    fold = lambda t: jnp.concatenate([t[:, 0:64] + t[:, 64:128], t[:, 128:192] + t[:, 192:256]], axis=1)
    return jnp.concatenate([d[:, :512], fold(d[:, 512:768]), fold(d[:, 768:1024]), d[:, 1024:]], axis=1)


def _local_step(x, mem, positions, target, w_in, later, sp, emit):
    gain = lambda n: sp[n].reshape(1, -1)
    half = HEAD_DIM // 2
    inv_freq = 1.0 / (10000.0 ** (jnp.arange(half, dtype=F32) * (2.0 / HEAD_DIM)))
    ang = positions.astype(F32)[:, None] * inv_freq
    cos, sin = jnp.cos(ang), jnp.sin(ang)
    cos128 = jnp.tile(cos, (1, 4))
    sin128 = jnp.concatenate([-sin, sin, -sin, sin], axis=1)
    seg = jnp.arange(128) // HEAD_DIM
    bmat = (seg[:, None] == seg[None, :]).astype(BF16)
    gq128, gk128 = jnp.tile(gain("q_norm"), (1, 2)), jnp.tile(gain("k_norm"), (1, 2))
    sinkcol = jnp.repeat(sp["attn_sinks"].reshape(4, 2), BLK, axis=1).reshape(4, 2 * BLK, 1)
    wsc = sp["gmlp_ws"] * jnp.tril(jnp.ones((BLK, BLK), F32))[None]
    w2 = wsc.reshape(4, 2 * BLK, BLK).astype(MXU_DTYPE)
    w2t = wsc.swapaxes(1, 2).reshape(4, 2 * BLK, BLK).astype(MXU_DTYPE)
    bsl = jnp.repeat(sp["gmlp_bs"].reshape(4, 2, BLK).transpose(0, 2, 1), HEAD_DIM, axis=2)
    cb = sp["ffn_conv_b"].reshape(1, -1)
    w_in_d = _dup_cols(_to_full(w_in, True))[None]

    h1, proj = rms_mm(x, gain("mix_norm"), w_in_d, name="mix_in")
    qr, kr, vb, gu, gvn = mixer_pre(proj, cos128, sin128, gq128, gk128, gain("gmlp_v_norm"), bmat)
    attn, ya = swa_fwd(qr, kr, vb, sinkcol, gain("attn_out_norm"))
    gm, y = gmlp_fwd(gvn, gu, ya, w2, bsl, gain("gmlp_out_norm"))
    wf, cw = later(y)
    w_out, xa_wq, xa_wo, ffn_down = (_to_full(wf[n], False) for n in ("w_out", "xa_wq", "xa_wo", "ffn_down"))
    x1 = mm(y, w_out, res=x, name="mix_out")
    h2, qx = rms_mm(x1, gain("xa_norm"), xa_wq[None], name="xa_q")
    mn, kv = rms_mm(mem, gain("mem_norm"), wf["xa_wkv"], name="xa_kv")
    kn, vbx = mem_pre(kv, gain("xa_k_norm"))
    xo = xattn_fwd(qx, kn, vbx, gain("xa_q_norm"))
    x2 = mm(xo, xa_wo, res=x1, name="xa_out")
    h3, a = rms_mm(x2, gain("ffn_norm"), wf["ffn_up"], name="ffn_up", tm=1024)
    f = convgate_fwd(a, cw, cb)
    x3 = mm(f, ffn_down, res=x2, name="ffn_down")
    dx3, loss_acc = loss_head(x3, target)

    by_rows = lambda g: g.reshape(N_CHIPS, g.shape[1] // N_CHIPS, g.shape[2])
    df = mm_nt(dx3, ffn_down[None], name="d_f")
    emit("ffn_down", by_rows(mm_tn(f, dx3, name="g_ffn_down", out_dtype=WIRE_DTYPE)))
    dc, gcw = convgate_bwd(a, df, cw, cb)
    da = conv_transpose(dc, cw)
    dh3 = mm_nt(da, wf["ffn_up"], name="d_h3")
    emit("ffn_up", mm_tn(h3, da, name="g_ffn_up", out_dtype=WIRE_DTYPE, chunks=N_CHIPS))
    dx2, dg_ffn = rms_bwd(dh3, x2, gain("ffn_norm"), dx3, name="ffn_norm_bwd")
    dxo = mm_nt(dx2, xa_wo[None], name="d_xo")
    emit("xa_wo", by_rows(mm_tn(xo, dx2, name="g_xa_wo", out_dtype=WIRE_DTYPE)))
    dqx, dkn, dvx, dg_xq = xattn_bwd(qx, dxo, kn, vbx, gain("xa_q_norm"))
    dh2 = mm_nt(dqx, xa_wq[None], name="d_h2")
    emit("xa_wq", by_rows(mm_tn(h2, dqx, name="g_xa_wq", out_dtype=WIRE_DTYPE)))
    dx1, dg_xa = rms_bwd(dh2, x1, gain("xa_norm"), dx2, name="xa_norm_bwd")
    dkv, dg_xk = mem_bwd(kv, dkn, dvx, gain("xa_k_norm"))
    dmn = mm_nt(dkv, wf["xa_wkv"], name="d_mn")
    emit("xa_wkv", mm_tn(mn, dkv, name="g_xa_wkv", out_dtype=WIRE_DTYPE, chunks=N_CHIPS))
    _, dg_mem = rms_bwd(dmn, mem, gain("mem_norm"), jnp.zeros_like(mem), name="mem_norm_bwd")
    dy = mm_nt(dx1, w_out[None], name="d_y")
    emit("w_out", by_rows(mm_tn(y, dx1, name="g_w_out", out_dtype=WIRE_DTYPE)))
    dattn, dgm, dg_y = mixer_post_bwd(dy, attn, gm, gain("attn_out_norm"), gain("gmlp_out_norm"))
    dgu, dgvn, dws, dbl = gmlp_bwd(dgm, gvn, gu, w2, w2t, bsl)
    dqr, dkr, dvb, dsk = swa_bwd(qr, kr, vb, sinkcol, dattn)
    dproj, dgq, dgk, dg_gvn = mixer_pre_bwd(proj, cos128, sin128, gq128, gk128, gain("gmlp_v_norm"), bmat,
                                            dqr, dkr, dvb, dgu, dgvn)
    dh1 = mm_nt(dproj, w_in_d, name="d_h1")
    g_in = _fold_cols(mm_tn(h1, dproj, name="g_w_in", out_dtype=F32)[0])
    emit("w_in", g_in.reshape(1024, N_CHIPS, 448).transpose(1, 0, 2).astype(WIRE_DTYPE))
    grad_x, dg_mix = rms_bwd(dh1, x, gain("mix_norm"), dx1, name="mix_norm_bwd")
    packed = pack_small(dg_mix, dgq, dgk, dsk, dg_gvn, dg_y, dg_xa, dg_mem, dg_xq, dg_xk, dg_ffn, gcw, dbl, dws)
    return loss_acc, grad_x, packed


def _gather_step(w, chipvec):
    slots = cast_shards([w[n][0] for n in BIG_NAMES], w["ffn_conv"][0], chipvec)
    send_a, recv_a, first = gather_start(slots[:1])
    send_b, recv_b, rest = gather_start(slots[1:])
    w_in, = gather_wait(send_a, recv_a, first, chipvec)

    def later(after):
        got = gather_wait(send_b, recv_b, rest, after)
        return dict(zip(BIG_NAMES[1:], got[:-1])), _to_full(got[-1], True)

    return w_in, later


def _reduce_update(started, packed, w, m, v, chipvec, cvec, order):
    own = sum_partials(partials_wait([started[n] for n in BIG_NAMES], packed), order)
    other = pair_share(own)
    res = [{}, {}, {}, {}]
    for n, g_own, g_other in zip(BIG_NAMES, own, other):
        for d, o in zip(res, adamw_matrix(w[n], m[n], v[n], g_own, g_other, cvec, name="adamw_" + n)):
            d[n] = o
    small_sum = sum_slots(gather_all(packed), name="sum_small")
    for d, outs in zip(res, adamw_small(small_sum, w, m, v, chipvec)):
        d.update(zip(SMALL, outs))
    return res


def kernel(x, mem, positions, mix_norm, w_in, q_norm, k_norm, attn_sinks, gmlp_v_norm, gmlp_ws, gmlp_bs, attn_out_norm, gmlp_out_norm, w_out, xa_norm, mem_norm, xa_wq, xa_wkv, xa_q_norm, xa_k_norm, xa_wo, ffn_norm, ffn_up, ffn_conv, ffn_conv_b, ffn_down, loss_target, m_mix_norm, m_w_in, m_q_norm, m_k_norm, m_attn_sinks, m_gmlp_v_norm, m_gmlp_ws, m_gmlp_bs, m_attn_out_norm, m_gmlp_out_norm, m_w_out, m_xa_norm, m_mem_norm, m_xa_wq, m_xa_wkv, m_xa_q_norm, m_xa_k_norm, m_xa_wo, m_ffn_norm, m_ffn_up, m_ffn_conv, m_ffn_conv_b, m_ffn_down, v_mix_norm, v_w_in, v_q_norm, v_k_norm, v_attn_sinks, v_gmlp_v_norm, v_gmlp_ws, v_gmlp_bs, v_attn_out_norm, v_gmlp_out_norm, v_w_out, v_xa_norm, v_mem_norm, v_xa_wq, v_xa_wkv, v_xa_q_norm, v_xa_k_norm, v_xa_wo, v_ffn_norm, v_ffn_up, v_ffn_conv, v_ffn_conv_b, v_ffn_down):
    w = dict(mix_norm=mix_norm, w_in=w_in, q_norm=q_norm, k_norm=k_norm, attn_sinks=attn_sinks, gmlp_v_norm=gmlp_v_norm, gmlp_ws=gmlp_ws, gmlp_bs=gmlp_bs, attn_out_norm=attn_out_norm, gmlp_out_norm=gmlp_out_norm, w_out=w_out, xa_norm=xa_norm, mem_norm=mem_norm, xa_wq=xa_wq, xa_wkv=xa_wkv, xa_q_norm=xa_q_norm, xa_k_norm=xa_k_norm, xa_wo=xa_wo, ffn_norm=ffn_norm, ffn_up=ffn_up, ffn_conv=ffn_conv, ffn_conv_b=ffn_conv_b, ffn_down=ffn_down)
    m = dict(mix_norm=m_mix_norm, w_in=m_w_in, q_norm=m_q_norm, k_norm=m_k_norm, attn_sinks=m_attn_sinks, gmlp_v_norm=m_gmlp_v_norm, gmlp_ws=m_gmlp_ws, gmlp_bs=m_gmlp_bs, attn_out_norm=m_attn_out_norm, gmlp_out_norm=m_gmlp_out_norm, w_out=m_w_out, xa_norm=m_xa_norm, mem_norm=m_mem_norm, xa_wq=m_xa_wq, xa_wkv=m_xa_wkv, xa_q_norm=m_xa_q_norm, xa_k_norm=m_xa_k_norm, xa_wo=m_xa_wo, ffn_norm=m_ffn_norm, ffn_up=m_ffn_up, ffn_conv=m_ffn_conv, ffn_conv_b=m_ffn_conv_b, ffn_down=m_ffn_down)
    v = dict(mix_norm=v_mix_norm, w_in=v_w_in, q_norm=v_q_norm, k_norm=v_k_norm, attn_sinks=v_attn_sinks, gmlp_v_norm=v_gmlp_v_norm, gmlp_ws=v_gmlp_ws, gmlp_bs=v_gmlp_bs, attn_out_norm=v_attn_out_norm, gmlp_out_norm=v_gmlp_out_norm, w_out=v_w_out, xa_norm=v_xa_norm, mem_norm=v_mem_norm, xa_wq=v_xa_wq, xa_wkv=v_xa_wkv, xa_q_norm=v_xa_q_norm, xa_k_norm=v_xa_k_norm, xa_wo=v_xa_wo, ffn_norm=v_ffn_norm, ffn_up=v_ffn_up, ffn_conv=v_ffn_conv, ffn_conv_b=v_ffn_conv_b, ffn_down=v_ffn_down)
    ix, iy, ic = lax.axis_index("x"), lax.axis_index("y"), lax.axis_index("c")
    chip = 2 * ix + iy
    chipvec = chip.astype(jnp.int32).reshape(1)
    cvec = ic.astype(jnp.int32).reshape(1)
    order = jnp.stack([chip, ic] + [4 * px + 2 * py + pc for px, py, pc in _peers(ix, iy, ic)]).astype(jnp.int32)

    w_in_all, later = _gather_step(w, chipvec)
    sp = {n: w[n][0] for n in SMALL if n != "ffn_conv"}
    started = {}

    def emit(name, g):
        started[name] = partials_start(g, name="partials_start_" + name)

    loss_acc, grad_x, packed = _local_step(x[0], mem[0], positions[0], loss_target[0], w_in_all, later, sp, emit)
    grads, delta, new_m, new_v = _reduce_update(started, packed, w, m, v, chipvec, cvec, order)
    loss = lax.psum(loss_acc[0, 0], ("x", "y", "c"))
    ordered = lambda d: [d[n] for n in WEIGHTS]
    return (loss, grad_x[None], *ordered(grads), *ordered(delta), *ordered(new_m), *ordered(new_v))
```

```python
import math

import jax
import jax.numpy as jnp
from jax import lax
from jax.experimental import pallas as pl
from jax.experimental.pallas import tpu as pltpu

F32 = jnp.float32
BF16 = jnp.bfloat16
MXU_DTYPE = jnp.bfloat16
WIRE_DTYPE = jnp.bfloat16
EPS = 1e-6
VMEM_LIMIT_V7X = 56 * 1024 * 1024

D_MODEL = 1024
HEAD_DIM = 64
BLK = 128
XA_HEADS = 4
XA_DH = 256
MEM_LEN = 256
D_FF = 2816
IN_COLS_DUP = 2048
N_CHIPS = 4
N_DEV = 8

ADAM_LR = 0.001
ADAM_B1 = 0.9
ADAM_B2 = 0.999
ADAM_EPS = 1e-08
ADAM_WD = 0.01
ADAM_STEP = 10

NT = (((1,), (1,)), ((), ()))
TN = (((0,), (0,)), ((), ()))
NN = (((1,), (0,)), ((), ()))
MINF = float(jnp.finfo(jnp.float32).min)
GELU_K0 = math.sqrt(2.0 / math.pi)
GELU_K1 = 0.044715

BS = pl.BlockSpec
SDS = jax.ShapeDtypeStruct
ANY = pl.BlockSpec(memory_space=pl.ANY)
MESH = pl.DeviceIdType.MESH


def _dot(a, b, dims=NN):
    return lax.dot_general(a.astype(MXU_DTYPE), b.astype(MXU_DTYPE), dims, preferred_element_type=F32)


def _segsum(x, bmat):
    hi = x.astype(BF16)
    lo = (x - hi.astype(F32)).astype(BF16)
    return (jnp.dot(hi, bmat, preferred_element_type=F32) + jnp.dot(lo, bmat, preferred_element_type=F32))


def _gelu(x):
    return 0.5 * x * (1.0 + jnp.tanh(GELU_K0 * (x + GELU_K1 * x * x * x)))


def _gelu_grad(x):
    t = jnp.tanh(GELU_K0 * (x + GELU_K1 * x * x * x))
    return 0.5 * (1.0 + t) + 0.5 * x * (1.0 - t * t) * GELU_K0 * (1.0 + 3.0 * GELU_K1 * x * x)


def _rms(x):
    return lax.rsqrt(jnp.mean(x * x, axis=-1, keepdims=True) + EPS)


def _rms_bwd(dy, x, g, r):
    dyg = dy * g
    dx = r * dyg - x * (r * r * r) * jnp.mean(dyg * x, axis=-1, keepdims=True)
    return dx, dy * x * r


def _pcall(body, *, name, grid, in_specs, out_specs, out_shape, scratch=(), prefetch=0):
    params = pltpu.CompilerParams(dimension_semantics=("arbitrary",) * len(grid), vmem_limit_bytes=VMEM_LIMIT_V7X)
    if prefetch:
        spec = pltpu.PrefetchScalarGridSpec(num_scalar_prefetch=prefetch, grid=grid, in_specs=in_specs,
                                            out_specs=out_specs, scratch_shapes=list(scratch))
        return pl.pallas_call(body, name=name, grid_spec=spec, out_shape=out_shape, compiler_params=params)
    return pl.pallas_call(body, name=name, grid=grid, in_specs=in_specs, out_specs=out_specs, out_shape=out_shape,
                          scratch_shapes=list(scratch), compiler_params=params)


def _tile(n, prefs):
    for p in prefs:
        if p <= n and n % p == 0:
            return p
    return n


def _acc_rows(ref, row, val):
    ref[row:row + 1, :] += jnp.sum(val, axis=0, keepdims=True)


def rms_mm(x, g, w3, *, name, tm=512):
    M, K = x.shape
    Q, _, C = w3.shape
    tm = _tile(M, (tm, 256))

    def body(x_ref, g_ref, w_ref, h_ref, o_ref):
        @pl.when(pl.program_id(1) == 0)
        def _():
            xv = x_ref[...]
            h_ref[...] = (xv * _rms(xv) * g_ref[...]).astype(h_ref.dtype)

        o_ref[...] = _dot(h_ref[...], w_ref[pl.program_id(1)])

    return _pcall(body, name=name, grid=(M // tm, Q),
                  in_specs=[BS((tm, K), lambda i, j: (i, 0)), BS((1, K), lambda i, j: (0, 0)),
                            BS((Q, K, C), lambda i, j: (0, 0, 0))],
                  out_specs=[BS((tm, K), lambda i, j: (i, 0)), BS((tm, C), lambda i, j: (i, j))],
                  out_shape=[SDS((M, K), MXU_DTYPE), SDS((M, Q * C), F32)])(x, g, w3)


def mm(a, w, *, name, res):
    M, K = a.shape
    N = w.shape[1]
    tm = _tile(M, (512, 256))

    def body(a_ref, w_ref, r_ref, o_ref):
        o_ref[...] = _dot(a_ref[...], w_ref[...]) + r_ref[...]

    return _pcall(body, name=name, grid=(M // tm,),
                  in_specs=[BS((tm, K), lambda i: (i, 0)), BS((K, N), lambda i: (0, 0)), BS((tm, N), lambda i: (i, 0))],
                  out_specs=BS((tm, N), lambda i: (i, 0)), out_shape=SDS((M, N), F32))(a, w, res)


def mm_nt(a, w3, *, name):
    M = a.shape[0]
    Q, N, Kc = w3.shape
    tm = _tile(M, (512, 256))

    def body(a_ref, w_ref, o_ref):
        acc = _dot(a_ref[:, 0:Kc], w_ref[0], NT)
        for q in range(1, Q):
            acc = acc + _dot(a_ref[:, q * Kc:(q + 1) * Kc], w_ref[q], NT)
        o_ref[...] = acc

    return _pcall(body, name=name, grid=(M // tm,),
                  in_specs=[BS((tm, Q * Kc), lambda i: (i, 0)), BS((Q, N, Kc), lambda i: (0, 0, 0))],
                  out_specs=BS((tm, N), lambda i: (i, 0)), out_shape=SDS((M, N), F32))(a, w3)


def _nt_chunks(a_ref, w_ref):
    q_n, _, kc = w_ref.shape
    acc = _dot(a_ref[:, 0:kc], w_ref[0], NT)
    for q in range(1, q_n):
        acc = acc + _dot(a_ref[:, q * kc:(q + 1) * kc], w_ref[q], NT)
    return acc


def mm_nt_rms_bwd(a, w3, x, g, dres, *, name, tm=512):
    M = a.shape[0]
    Q, N, Kc = w3.shape
    tm = _tile(M, (tm, 256))

    def body(a_ref, w_ref, x_ref, g_ref, dr_ref, dx_ref, dg_ref):
        @pl.when(pl.program_id(0) == 0)
        def _():
            dg_ref[...] = jnp.zeros_like(dg_ref)

        xv = x_ref[...]
        dx, dgc = _rms_bwd(_nt_chunks(a_ref, w_ref), xv, g_ref[...], _rms(xv))
        dx_ref[...] = dr_ref[...] + dx
        _acc_rows(dg_ref, 0, dgc)

    row = BS((tm, N), lambda i: (i, 0))
    return _pcall(body, name=name, grid=(M // tm,),
                  in_specs=[BS((tm, Q * Kc), lambda i: (i, 0)), BS((Q, N, Kc), lambda i: (0, 0, 0)), row,
                            BS((1, N), lambda i: (0, 0)), row],
                  out_specs=[row, BS((8, N), lambda i: (0, 0))],
                  out_shape=[SDS((M, N), F32), SDS((8, N), F32)])(a, w3, x, g, dres)


def mm_nt_post_bwd(a, w3, attn, gm, gao, ggo, *, name):
    M = a.shape[0]
    Q, N, Kc = w3.shape
    tm = _tile(M, (512, 256))
    hw = N // 2

    def body(a_ref, w_ref, at_ref, gm_ref, gao_ref, ggo_ref, da_ref, dgm_ref, dg_ref):
        @pl.when(pl.program_id(0) == 0)
        def _():
            dg_ref[...] = jnp.zeros_like(dg_ref)

        dy = _nt_chunks(a_ref, w_ref)
        av, gmv = at_ref[...], gm_ref[...]
        da, dga = _rms_bwd(dy[:, :hw], av, gao_ref[...], _rms(av))
        dgm, dgg = _rms_bwd(dy[:, hw:], gmv, ggo_ref[...], _rms(gmv))
        da_ref[...] = da
        dgm_ref[...] = dgm
        dg_ref[0:1, :hw] += jnp.sum(dga, axis=0, keepdims=True)
        dg_ref[0:1, hw:] += jnp.sum(dgg, axis=0, keepdims=True)

    half = BS((tm, hw), lambda i: (i, 0))
    const = lambda r, w: BS((r, w), lambda i: (0, 0))
    return _pcall(body, name=name, grid=(M // tm,),
                  in_specs=[BS((tm, Q * Kc), lambda i: (i, 0)), BS((Q, N, Kc), lambda i: (0, 0, 0)), half, half,
                            const(1, hw), const(1, hw)],
                  out_specs=[half, half, const(8, N)],
                  out_shape=[SDS((M, hw), F32), SDS((M, hw), F32), SDS((8, N), F32)])(a, w3, attn, gm, gao, ggo)


def mm_loss(a, w, res, target, *, name):
    M, K = a.shape
    N = w.shape[1]
    tm = _tile(M, (512, 256))

    def body(a_ref, w_ref, r_ref, t_ref, d_ref, l_ref):
        @pl.when(pl.program_id(0) == 0)
        def _():
            l_ref[...] = jnp.zeros_like(l_ref)

        e = _dot(a_ref[...], w_ref[...]) + r_ref[...] - t_ref[...]
        d_ref[...] = e * (1.0 / N)
        l_ref[...] += jnp.sum(e * e) * (0.5 / N)

    row = BS((tm, N), lambda i: (i, 0))
    return _pcall(body, name=name, grid=(M // tm,),
                  in_specs=[BS((tm, K), lambda i: (i, 0)), BS((K, N), lambda i: (0, 0)), row, row],
                  out_specs=[row, BS((8, 128), lambda i: (0, 0))],
                  out_shape=[SDS((M, N), F32), SDS((8, 128), F32)])(a, w, res, target)


def mm_tn(a, b, *, name, out_dtype, chunks=1):
    M, K = a.shape
    N = b.shape[1]
    C = N // chunks
    tm = _tile(M, (512, 256))
    tk = _tile(K, (1408, 1024, 512))
    tn = _tile(C, (1408, 1024, 512))
    per = C // tn
    nm = M // tm

    def body(a_ref, b_ref, o_ref, acc):
        m = pl.program_id(2)

        @pl.when(m == 0)
        def _():
            acc[...] = jnp.zeros_like(acc)

        acc[...] += _dot(a_ref[...], b_ref[...], TN)

        @pl.when(m == nm - 1)
        def _():
            o_ref[...] = acc[...].astype(o_ref.dtype)

    return _pcall(body, name=name, grid=(K // tk, N // tn, nm),
                  in_specs=[BS((tm, tk), lambda k, n, m: (m, k)), BS((tm, tn), lambda k, n, m: (m, n))],
                  out_specs=BS((None, tk, tn), lambda k, n, m: (n // per, k, n % per)),
                  out_shape=SDS((chunks, K, C), out_dtype), scratch=[pltpu.VMEM((tk, tn), F32)])(a, b)


def _lane(shape):
    return lax.broadcasted_iota(jnp.int32, shape, 1)


def _norm_rope(slab, g, bmat, cos, sin, first):
    r = lax.rsqrt(_segsum(slab * slab, bmat) * (1.0 / HEAD_DIM) + EPS)
    qn = slab * r * g
    swapped = jnp.where(first, pltpu.roll(qn, 96, 1), pltpu.roll(qn, 32, 1))
    return qn * cos + swapped * sin


def mixer_pre(proj, cos, sin, gq, gk, gvn, bmat):
    S = proj.shape[0]
    tm = _tile(S, (256,))

    def body(p_ref, c_ref, s_ref, gq_ref, gk_ref, gvn_ref, b_ref, qr_ref, kr_ref, vb_ref, gu_ref, gvo_ref):
        cos_v, sin_v, bm = c_ref[...], s_ref[...], b_ref[...]
        first = (_lane((tm, 128)) & 63) < 32
        for s in range(4):
            sl = slice(s * 128, (s + 1) * 128)
            qr_ref[:, sl] = _norm_rope(p_ref[:, sl], gq_ref[...], bm, cos_v, sin_v, first).astype(qr_ref.dtype)
        for s in range(2):
            kr_ref[:, s * 128:(s + 1) * 128] = _norm_rope(p_ref[:, 512 + s * 128:640 + s * 128], gk_ref[...], bm,
                                                          cos_v, sin_v, first).astype(kr_ref.dtype)
        vb_ref[...] = p_ref[:, 768:1024].astype(vb_ref.dtype)
        gu_ref[...] = _gelu(p_ref[:, 1024:1536])
        gv = _gelu(p_ref[:, 1536:2048])
        gvo_ref[...] = (gv * _rms(gv) * gvn_ref[...]).astype(gvo_ref.dtype)

    row = lambda w: BS((tm, w), lambda i: (i, 0))
    const = lambda r, w: BS((r, w), lambda i: (0, 0))
    return _pcall(body, name="mixer_pre", grid=(S // tm,),
                  in_specs=[row(IN_COLS_DUP), row(128), row(128), const(1, 128), const(1, 128), const(1, 512),
                            const(128, 128)],
                  out_specs=[row(512), row(256), row(256), row(512), row(512)],
                  out_shape=[SDS((S, 512), MXU_DTYPE), SDS((S, 256), MXU_DTYPE), SDS((S, 256), MXU_DTYPE),
                             SDS((S, 512), F32), SDS((S, 512), MXU_DTYPE)])(proj, cos, sin, gq, gk, gvn, bmat)


def _swa_probs(qs, kd, sink, n, lo):
    z = jnp.zeros_like(qs)
    qp = jnp.concatenate([jnp.where(lo, qs, z), jnp.where(lo, z, qs)], axis=0)
    sc = _dot(qp, kd, NT) * (1.0 / math.sqrt(HEAD_DIM))
    r_i = lax.broadcasted_iota(jnp.int32, (2 * BLK, 2 * BLK), 0)
    k_j = lax.broadcasted_iota(jnp.int32, (2 * BLK, 2 * BLK), 1)
    diff = (r_i & (BLK - 1)) + BLK - k_j
    mask = (diff >= 0) & (diff < BLK) & ((k_j >= BLK) | (n > 0))
    sc = jnp.where(mask, sc, MINF)
    m = jnp.maximum(jnp.max(sc, axis=1, keepdims=True), sink)
    p = jnp.exp(sc - m)
    es = jnp.exp(sink - m)
    l = jnp.sum(p, axis=1, keepdims=True) + es
    return qp, p / l, es / l


def swa_fwd(qr, kr, vb, sinkcol, gao):
    S = qr.shape[0]
    nb = S // BLK

    def body(q_ref, kc_ref, kp_ref, vc_ref, vp_ref, sk_ref, g_ref, o_ref, ya_ref):
        n = pl.program_id(0)
        lo = _lane((BLK, 128)) < 64
        for s in range(4):
            h = s // 2
            hs = slice(h * 128, (h + 1) * 128)
            kd = jnp.concatenate([kp_ref[:, hs], kc_ref[:, hs]], axis=0)
            vd = jnp.concatenate([vp_ref[:, hs], vc_ref[:, hs]], axis=0)
            _, p, _ = _swa_probs(q_ref[:, s * 128:(s + 1) * 128], kd, sk_ref[s], n, lo)
            o2 = _dot(p, vd)
            o_ref[:, s * 128:(s + 1) * 128] = jnp.where(lo, o2[:BLK], o2[BLK:])
        a = o_ref[...]
        ya_ref[...] = (a * _rms(a) * g_ref[...]).astype(ya_ref.dtype)

    cur = lambda w: BS((BLK, w), lambda n: (n, 0))
    prev = lambda w: BS((BLK, w), lambda n: (jnp.maximum(n - 1, 0), 0))
    return _pcall(body, name="swa_fwd", grid=(nb,),
                  in_specs=[cur(512), cur(256), prev(256), cur(256), prev(256),
                            BS((4, 2 * BLK, 1), lambda n: (0, 0, 0)), BS((1, 512), lambda n: (0, 0))],
                  out_specs=[cur(512), cur(512)],
                  out_shape=[SDS((S, 512), F32), SDS((S, 512), MXU_DTYPE)])(qr, kr, kr, vb, vb, sinkcol, gao)


def gmlp_fwd(gvn, gu, ya, w2, bsl, ggo):
    S = gvn.shape[0]

    def body(gvn_ref, gu_ref, ya_ref, w2_ref, bsl_ref, g_ref, gm_ref, y_ref):
        lo = _lane((BLK, 128)) < 64
        for j in range(4):
            sl = slice(j * 128, (j + 1) * 128)
            m2 = _dot(w2_ref[j], gvn_ref[:, sl])
            mixed = jnp.where(lo, m2[:BLK], m2[BLK:]) + bsl_ref[j]
            gm_ref[:, sl] = gu_ref[:, sl] * mixed
        gm = gm_ref[...]
        y_ref[:, :512] = ya_ref[...]
        y_ref[:, 512:] = (gm * _rms(gm) * g_ref[...]).astype(y_ref.dtype)

    row = lambda w: BS((BLK, w), lambda n: (n, 0))
    return _pcall(body, name="gmlp_fwd", grid=(S // BLK,),
                  in_specs=[row(512), row(512), row(512), BS((4, 2 * BLK, BLK), lambda n: (0, 0, 0)),
                            BS((4, BLK, 128), lambda n: (0, 0, 0)), BS((1, 512), lambda n: (0, 0))],
                  out_specs=[row(512), row(1024)],
                  out_shape=[SDS((S, 512), F32), SDS((S, 1024), MXU_DTYPE)])(gvn, gu, ya, w2, bsl, ggo)


def mem_pre(kv, gxk):
    def body(kv_ref, g_ref, kn_ref, vb_ref):
        for h in range(XA_HEADS):
            sl = slice(h * XA_DH, (h + 1) * XA_DH)
            k = kv_ref[:, sl]
            kn_ref[:, sl] = (k * _rms(k) * g_ref[...]).astype(kn_ref.dtype)
        vb_ref[...] = kv_ref[:, 1024:2048].astype(vb_ref.dtype)

    full = lambda r, w: BS((r, w), lambda i: (0, 0))
    return _pcall(body, name="mem_pre", grid=(1,), in_specs=[full(MEM_LEN, 2048), full(1, XA_DH)],
                  out_specs=[full(MEM_LEN, 1024), full(MEM_LEN, 1024)],
                  out_shape=[SDS((MEM_LEN, 1024), MXU_DTYPE), SDS((MEM_LEN, 1024), MXU_DTYPE)])(kv, gxk)


def _xa_probs(qh, g, kn_h):
    r = _rms(qh)
    qn = qh * r * g
    s = _dot(qn, kn_h, NT) * (1.0 / math.sqrt(XA_DH))
    p = jnp.exp(s - jnp.max(s, axis=1, keepdims=True))
    return r, qn, p / jnp.sum(p, axis=1, keepdims=True)


def xattn_fwd(qx, kn, vb, gxq):
    S = qx.shape[0]
    tm = _tile(S, (256,))

    def body(q_ref, kn_ref, vb_ref, g_ref, o_ref):
        for h in range(XA_HEADS):
            sl = slice(h * XA_DH, (h + 1) * XA_DH)
            _, _, p = _xa_probs(q_ref[:, sl], g_ref[...], kn_ref[:, sl])
            o_ref[:, sl] = _dot(p, vb_ref[:, sl]).astype(o_ref.dtype)

    full = lambda r, w: BS((r, w), lambda i: (0, 0))
    return _pcall(body, name="xattn_fwd", grid=(S // tm,),
                  in_specs=[BS((tm, 1024), lambda i: (i, 0)), full(MEM_LEN, 1024), full(MEM_LEN, 1024), full(1, XA_DH)],
                  out_specs=BS((tm, 1024), lambda i: (i, 0)), out_shape=SDS((S, 1024), MXU_DTYPE))(qx, kn, vb, gxq)


def _causal_taps(a, halo_ref, first_tile, row):
    h6 = jnp.where(first_tile, 0.0, halo_ref[6:7, :])
    h7 = jnp.where(first_tile, 0.0, halo_ref[7:8, :])
    a1 = jnp.where(row == 0, h7, pltpu.roll(a, 1, 0))
    a2 = jnp.where(row == 0, h6, jnp.where(row == 1, h7, pltpu.roll(a, 2, 0)))
    return a1, a2


def _conv(a, a1, a2, w_ref, b_ref):
    return w_ref[2:3, :] * a + w_ref[1:2, :] * a1 + w_ref[0:1, :] * a2 + b_ref[...]


def _conv_specs(tm):
    halo_blocks = tm // 8
    return [BS((tm, D_FF), lambda i: (i, 0)), BS((tm, D_FF), lambda i: (i, 1)),
            BS((8, D_FF), lambda i: (jnp.maximum(i * halo_blocks - 1, 0), 0)),
            BS((8, D_FF), lambda i: (jnp.maximum(i * halo_blocks - 1, 0), 1)),
            BS((3, D_FF), lambda i: (0, 0)), BS((3, D_FF), lambda i: (0, 1)),
            BS((1, D_FF), lambda i: (0, 0)), BS((1, D_FF), lambda i: (0, 1))]


def convgate_fwd(a, cw, cb):
    S = a.shape[0]
    tm = _tile(S, (256,))

    def body(ag_ref, au_ref, hg_ref, hu_ref, wg_ref, wu_ref, bg_ref, bu_ref, f_ref):
        first_tile = pl.program_id(0) == 0
        row = lax.broadcasted_iota(jnp.int32, (tm, D_FF), 0)
        ag, au = ag_ref[...], au_ref[...]
        cg = _conv(ag, *_causal_taps(ag, hg_ref, first_tile, row), wg_ref, bg_ref)
        cu = _conv(au, *_causal_taps(au, hu_ref, first_tile, row), wu_ref, bu_ref)
        f_ref[...] = (_gelu(cg) * cu).astype(f_ref.dtype)

    return _pcall(body, name="convgate_fwd", grid=(S // tm,), in_specs=_conv_specs(tm),
                  out_specs=BS((tm, D_FF), lambda i: (i, 0)),
                  out_shape=SDS((S, D_FF), MXU_DTYPE))(a, a, a, a, cw, cw, cb, cb)


def convgate_bwd(a, df, cw, cb):
    S = a.shape[0]
    tm = _tile(S, (128,))

    def body(ag_ref, au_ref, hg_ref, hu_ref, wg_ref, wu_ref, bg_ref, bu_ref, df_ref, dc_ref, gw_ref):
        first_tile = pl.program_id(0) == 0

        @pl.when(first_tile)
        def _():
            gw_ref[...] = jnp.zeros_like(gw_ref)

        row = lax.broadcasted_iota(jnp.int32, (tm, D_FF), 0)
        ag, au, df_v = ag_ref[...], au_ref[...], df_ref[...]
        ag1, ag2 = _causal_taps(ag, hg_ref, first_tile, row)
        au1, au2 = _causal_taps(au, hu_ref, first_tile, row)
        cg = _conv(ag, ag1, ag2, wg_ref, bg_ref)
        cu = _conv(au, au1, au2, wu_ref, bu_ref)
        dcg = df_v * cu * _gelu_grad(cg)
        dcu = df_v * _gelu(cg)
        dc_ref[:, :D_FF] = dcg
        dc_ref[:, D_FF:] = dcu
        for col, dcv, taps in ((slice(0, D_FF), dcg, (ag2, ag1, ag)), (slice(D_FF, 2 * D_FF), dcu, (au2, au1, au))):
            for j in range(3):
                gw_ref[j:j + 1, col] += jnp.sum(dcv * taps[j], axis=0, keepdims=True)
            gw_ref[3:4, col] += jnp.sum(dcv, axis=0, keepdims=True)

    return _pcall(body, name="convgate_bwd", grid=(S // tm,),
                  in_specs=_conv_specs(tm) + [BS((tm, D_FF), lambda i: (i, 0))],
                  out_specs=[BS((tm, 2 * D_FF), lambda i: (i, 0)), BS((8, 2 * D_FF), lambda i: (0, 0))],
                  out_shape=[SDS((S, 2 * D_FF), F32), SDS((8, 2 * D_FF), F32)])(a, a, a, a, cw, cw, cb, cb, df)


def conv_transpose(dc, cw):
    S, C = dc.shape
    tm = _tile(S, (128,))
    nt = S // tm
    halo_blocks = tm // 8

    def body(dc_ref, halo_ref, w_ref, da_ref):
        last_tile = pl.program_id(0) == nt - 1
        row = lax.broadcasted_iota(jnp.int32, (tm, C), 0)
        h0 = jnp.where(last_tile, 0.0, halo_ref[0:1, :])
        h1 = jnp.where(last_tile, 0.0, halo_ref[1:2, :])
        dc_v = dc_ref[...]
        n1 = jnp.where(row == tm - 1, h0, pltpu.roll(dc_v, tm - 1, 0))
        n2 = jnp.where(row == tm - 1, h1, jnp.where(row == tm - 2, h0, pltpu.roll(dc_v, tm - 2, 0)))
        da_ref[...] = (w_ref[2:3, :] * dc_v + w_ref[1:2, :] * n1 + w_ref[0:1, :] * n2).astype(da_ref.dtype)

    return _pcall(body, name="conv_transpose", grid=(nt,),
                  in_specs=[BS((tm, C), lambda i: (i, 0)),
                            BS((8, C), lambda i: (jnp.minimum((i + 1) * halo_blocks, S // 8 - 1), 0)),
                            BS((3, C), lambda i: (0, 0))],
                  out_specs=BS((tm, C), lambda i: (i, 0)), out_shape=SDS((S, C), MXU_DTYPE))(dc, dc, cw)


def xattn_bwd(qx, dxo, kn, vb, gxq):
    S = qx.shape[0]
    tm = _tile(S, (256,))

    def body(q_ref, do_ref, kn_ref, vb_ref, g_ref, dq_ref, dkn_ref, dv_ref, dg_ref):
        @pl.when(pl.program_id(0) == 0)
        def _():
            dkn_ref[...] = jnp.zeros_like(dkn_ref)
            dv_ref[...] = jnp.zeros_like(dv_ref)
            dg_ref[...] = jnp.zeros_like(dg_ref)

        g = g_ref[...]
        for h in range(XA_HEADS):
            sl = slice(h * XA_DH, (h + 1) * XA_DH)
            qh, do = q_ref[:, sl], do_ref[:, sl]
            r, qn, p = _xa_probs(qh, g, kn_ref[:, sl])
            dp = _dot(do, vb_ref[:, sl], NT)
            ds = p * (dp - jnp.sum(dp * p, axis=1, keepdims=True)) * (1.0 / math.sqrt(XA_DH))
            dqn = _dot(ds, kn_ref[:, sl])
            dkn_ref[:, sl] += _dot(ds, qn, TN)
            dv_ref[:, sl] += _dot(p, do, TN)
            dqh, dgc = _rms_bwd(dqn, qh, g, r)
            dq_ref[:, sl] = dqh.astype(dq_ref.dtype)
            _acc_rows(dg_ref, 0, dgc)

    row = BS((tm, 1024), lambda i: (i, 0))
    full = lambda r, w: BS((r, w), lambda i: (0, 0))
    return _pcall(body, name="xattn_bwd", grid=(S // tm,),
                  in_specs=[row, row, full(MEM_LEN, 1024), full(MEM_LEN, 1024), full(1, XA_DH)],
                  out_specs=[row, full(MEM_LEN, 1024), full(MEM_LEN, 1024), full(8, XA_DH)],
                  out_shape=[SDS((S, 1024), MXU_DTYPE), SDS((MEM_LEN, 1024), F32), SDS((MEM_LEN, 1024), F32),
                             SDS((8, XA_DH), F32)])(qx, dxo, kn, vb, gxq)


def mem_bwd(kv, dkn, dvb, gxk):
    def body(kv_ref, dkn_ref, dv_ref, g_ref, dkv_ref, dg_ref):
        dg_ref[...] = jnp.zeros_like(dg_ref)
        for h in range(XA_HEADS):
            sl = slice(h * XA_DH, (h + 1) * XA_DH)
            k = kv_ref[:, sl]
            dk, dgc = _rms_bwd(dkn_ref[:, sl], k, g_ref[...], _rms(k))
            dkv_ref[:, sl] = dk.astype(dkv_ref.dtype)
            _acc_rows(dg_ref, 0, dgc)
        dkv_ref[:, 1024:2048] = dv_ref[...].astype(dkv_ref.dtype)

    full = lambda r, w: BS((r, w), lambda i: (0, 0))
    return _pcall(body, name="mem_bwd", grid=(1,),
                  in_specs=[full(MEM_LEN, 2048), full(MEM_LEN, 1024), full(MEM_LEN, 1024), full(1, XA_DH)],
                  out_specs=[full(MEM_LEN, 2048), full(8, XA_DH)],
                  out_shape=[SDS((MEM_LEN, 2048), MXU_DTYPE), SDS((8, XA_DH), F32)])(kv, dkn, dvb, gxk)


def gmlp_bwd(dgm, gvn, gu, w2, w2t, bsl):
    S = dgm.shape[0]

    def body(dgm_ref, gvn_ref, gu_ref, w2_ref, w2t_ref, bsl_ref, dgu_ref, dgvn_ref, dws_ref, dbl_ref):
        @pl.when(pl.program_id(0) == 0)
        def _():
            dws_ref[...] = jnp.zeros_like(dws_ref)
            dbl_ref[...] = jnp.zeros_like(dbl_ref)

        lo = _lane((BLK, 128)) < 64
        for j in range(4):
            sl = slice(j * 128, (j + 1) * 128)
            gvn_s = gvn_ref[:, sl]
            m2 = _dot(w2_ref[j], gvn_s)
            mixed = jnp.where(lo, m2[:BLK], m2[BLK:]) + bsl_ref[j]
            dgm_s = dgm_ref[:, sl]
            dgu_ref[:, sl] = dgm_s * mixed
            dmx = dgm_s * gu_ref[:, sl]
            d2 = _dot(w2t_ref[j], dmx)
            dgvn_ref[:, sl] = jnp.where(lo, d2[:BLK], d2[BLK:])
            z = jnp.zeros_like(dmx)
            dws_ref[2 * j] += _dot(jnp.where(lo, dmx, z), gvn_s, NT)
            dws_ref[2 * j + 1] += _dot(jnp.where(lo, z, dmx), gvn_s, NT)
            dbl_ref[j] += dmx

    row = lambda w: BS((BLK, w), lambda n: (n, 0))
    const3 = lambda a, b, c: BS((a, b, c), lambda n: (0, 0, 0))
    return _pcall(body, name="gmlp_bwd", grid=(S // BLK,),
                  in_specs=[row(512), row(512), row(512), const3(4, 2 * BLK, BLK), const3(4, 2 * BLK, BLK),
                            const3(4, BLK, 128)],
                  out_specs=[row(512), row(512), const3(8, BLK, BLK), const3(4, BLK, 128)],
                  out_shape=[SDS((S, 512), F32), SDS((S, 512), F32), SDS((8, BLK, BLK), F32),
                             SDS((4, BLK, 128), F32)])(dgm, gvn, gu, w2, w2t, bsl)


def swa_bwd(qr, kr, vb, sinkcol, dattn):
    S = qr.shape[0]
    nb = S // BLK

    def body(q_ref, kc_ref, kp_ref, vc_ref, vp_ref, sk_ref, do_ref, dq_ref, dk_ref, dv_ref, dsk_ref,
             carry_k, carry_v, prev_k, prev_v):
        n = pl.program_id(0)

        @pl.when(n == 0)
        def _():
            dsk_ref[...] = jnp.zeros_like(dsk_ref)
            carry_k[...] = jnp.zeros_like(carry_k)
            carry_v[...] = jnp.zeros_like(carry_v)

        @pl.when(n < nb)
        def _():
            lo = _lane((BLK, 128)) < 64
            for h in range(2):
                hs = slice(h * 128, (h + 1) * 128)
                kd = jnp.concatenate([kp_ref[:, hs], kc_ref[:, hs]], axis=0)
                vd = jnp.concatenate([vp_ref[:, hs], vc_ref[:, hs]], axis=0)
                dkd = jnp.zeros((2 * BLK, 128), F32)
                dvd = jnp.zeros((2 * BLK, 128), F32)
                for s in (2 * h, 2 * h + 1):
                    sl = slice(s * 128, (s + 1) * 128)
                    qp, p, psink = _swa_probs(q_ref[:, sl], kd, sk_ref[s], n, lo)
                    do = do_ref[:, sl]
                    z = jnp.zeros_like(do)
                    dop = jnp.concatenate([jnp.where(lo, do, z), jnp.where(lo, z, do)], axis=0)
                    dp = _dot(dop, vd, NT)
                    delta = jnp.sum(dp * p, axis=1, keepdims=True)
                    ds = p * (dp - delta) * (1.0 / math.sqrt(HEAD_DIM))
                    dsk_ref[s] += -psink * delta
                    dq2 = _dot(ds, kd)
                    dq_ref[:, sl] = jnp.where(lo, dq2[:BLK], dq2[BLK:])
                    dkd = dkd + _dot(ds, qp, TN)
                    dvd = dvd + _dot(p, dop, TN)
                prev_k[:, hs] = carry_k[:, hs] + dkd[:BLK]
                prev_v[:, hs] = carry_v[:, hs] + dvd[:BLK]
                carry_k[:, hs] = dkd[BLK:]
                carry_v[:, hs] = dvd[BLK:]

        @pl.when(n == nb)
        def _():
            prev_k[...] = carry_k[...]
            prev_v[...] = carry_v[...]

        dk_ref[...] = prev_k[...]
        dv_ref[...] = prev_v[...]

    last = nb - 1
    cur = lambda w: BS((BLK, w), lambda n: (jnp.minimum(n, last), 0))
    prev = lambda w: BS((BLK, w), lambda n: (jnp.clip(n - 1, 0, last), 0))
    done = lambda w: BS((BLK, w), lambda n: (jnp.maximum(n - 1, 0), 0))
    return _pcall(body, name="swa_bwd", grid=(nb + 1,),
                  in_specs=[cur(512), cur(256), prev(256), cur(256), prev(256),
                            BS((4, 2 * BLK, 1), lambda n: (0, 0, 0)), cur(512)],
                  out_specs=[cur(512), done(256), done(256), BS((4, 2 * BLK, 1), lambda n: (0, 0, 0))],
                  out_shape=[SDS((S, 512), F32), SDS((S, 256), F32), SDS((S, 256), F32), SDS((4, 2 * BLK, 1), F32)],
                  scratch=[pltpu.VMEM((BLK, 256), F32)] * 4)(qr, kr, kr, vb, vb, sinkcol, dattn)


def mixer_pre_bwd(proj, cos, sin, gq, gk, gvn, bmat, dqr, dkr, dvb, dgu, dgvn):
    S = proj.shape[0]
    tm = _tile(S, (256,))

    def body(p_ref, c_ref, s_ref, gq_ref, gk_ref, gvn_ref, b_ref, dqr_ref, dkr_ref, dvb_ref, dgu_ref, dgvn_ref,
             dp_ref, dgq_ref, dgk_ref, dgv_ref):
        @pl.when(pl.program_id(0) == 0)
        def _():
            dgq_ref[...] = jnp.zeros_like(dgq_ref)
            dgk_ref[...] = jnp.zeros_like(dgk_ref)
            dgv_ref[...] = jnp.zeros_like(dgv_ref)

        cos_v, sin_v, bm = c_ref[...], s_ref[...], b_ref[...]
        first = (_lane((tm, 128)) & 63) < 32

        def slab_bwd(slab, dout, g, dg_ref):
            r = lax.rsqrt(_segsum(slab * slab, bm) * (1.0 / HEAD_DIM) + EPS)
            ds = dout * sin_v
            dqn = dout * cos_v + jnp.where(first, pltpu.roll(ds, 96, 1), pltpu.roll(ds, 32, 1))
            dyg = dqn * g
            dx = r * dyg - slab * (r * r * r) * (_segsum(dyg * slab, bm) * (1.0 / HEAD_DIM))
            _acc_rows(dg_ref, 0, dqn * slab * r)
            return dx

        for s in range(4):
            sl = slice(s * 128, (s + 1) * 128)
            dp_ref[:, sl] = slab_bwd(p_ref[:, sl], dqr_ref[:, sl], gq_ref[...], dgq_ref).astype(dp_ref.dtype)
        for s in range(2):
            sl = slice(512 + s * 128, 640 + s * 128)
            dp_ref[:, sl] = slab_bwd(p_ref[:, sl], dkr_ref[:, s * 128:(s + 1) * 128], gk_ref[...],
                                     dgk_ref).astype(dp_ref.dtype)
        dp_ref[:, 768:1024] = dvb_ref[...].astype(dp_ref.dtype)
        dp_ref[:, 1024:1536] = (dgu_ref[...] * _gelu_grad(p_ref[:, 1024:1536])).astype(dp_ref.dtype)
        gvp = p_ref[:, 1536:2048]
        gv = _gelu(gvp)
        dgv, dgc = _rms_bwd(dgvn_ref[...], gv, gvn_ref[...], _rms(gv))
        dp_ref[:, 1536:2048] = (dgv * _gelu_grad(gvp)).astype(dp_ref.dtype)
        _acc_rows(dgv_ref, 0, dgc)

    row = lambda w: BS((tm, w), lambda i: (i, 0))
    const = lambda r, w: BS((r, w), lambda i: (0, 0))
    return _pcall(body, name="mixer_pre_bwd", grid=(S // tm,),
                  in_specs=[row(IN_COLS_DUP), row(128), row(128), const(1, 128), const(1, 128), const(1, 512),
                            const(128, 128), row(512), row(256), row(256), row(512), row(512)],
                  out_specs=[row(IN_COLS_DUP), const(8, 128), const(8, 128), const(8, 512)],
                  out_shape=[SDS((S, IN_COLS_DUP), MXU_DTYPE), SDS((8, 128), F32), SDS((8, 128), F32),
                             SDS((8, 512), F32)])(proj, cos, sin, gq, gk, gvn, bmat, dqr, dkr, dvb, dgu, dgvn)


BIG = (("w_in", (1024, 448), True), ("w_out", (256, 1024), False), ("xa_wq", (256, 1024), False),
       ("xa_wkv", (1024, 512), True), ("xa_wo", (256, 1024), False), ("ffn_up", (1024, 1408), True),
       ("ffn_down", (704, 1024), False))
BIG_NAMES = tuple(n for n, _, _ in BIG)
SMALL_VECS = (("mix_norm", 1024), ("q_norm", 64), ("k_norm", 64), ("attn_sinks", 8), ("gmlp_v_norm", 512),
              ("attn_out_norm", 512), ("gmlp_out_norm", 512), ("xa_norm", 1024), ("mem_norm", 1024),
              ("xa_q_norm", 256), ("xa_k_norm", 256), ("ffn_norm", 1024), ("ffn_conv_b", 5632))
SMALL = tuple(n for n, _ in SMALL_VECS) + ("gmlp_bs", "gmlp_ws", "ffn_conv")
WEIGHTS = ("mix_norm", "w_in", "q_norm", "k_norm", "attn_sinks", "gmlp_v_norm", "gmlp_ws", "gmlp_bs",
           "attn_out_norm", "gmlp_out_norm", "w_out", "xa_norm", "mem_norm", "xa_wq", "xa_wkv", "xa_q_norm",
           "xa_k_norm", "xa_wo", "ffn_norm", "ffn_up", "ffn_conv", "ffn_conv_b", "ffn_down")
CONV_SHARD = (3, 1408)
CONV_LANE_ROWS = CONV_SHARD[1] // 128
CONV_CHIP_ROWS = 40


def _small_rows():
    rows, r = {}, 0
    for n, length in SMALL_VECS:
        rows[n] = r
        r += -(-length // 128)
    r += -r % 8
    rows["gmlp_bs"] = r
    r += 8
    rows["gmlp_ws"] = r
    r += 8 * BLK
    rows["ffn_conv"] = r
    r += N_CHIPS * CONV_CHIP_ROWS
    return rows, r


SMALL_ROW, SMALL_ROWS = _small_rows()


def pack_small(dg_mix, dgq, dgk, dsk, dg_gvn, dg_y, dg_xa, dg_mem, dg_xq, dg_xk, dg_ffn, gcw, dbl, dws):
    def body(mix_ref, q_ref, k_ref, sk_ref, gvn_ref, y_ref, xa_ref, mem_ref, xq_ref, xk_ref, ffn_ref, cw_ref,
             dbl_ref, dws_ref, o_ref):
        o_ref[...] = jnp.zeros_like(o_ref)
        lane = _lane((1, 128))

        def put(name, src_ref, row, lane0, length):
            for k in range(length // 128):
                o_ref[SMALL_ROW[name] + k:SMALL_ROW[name] + k + 1, :] = src_ref[row:row + 1, lane0 + k * 128:lane0 + (k + 1) * 128]

        put("mix_norm", mix_ref, 0, 0, 1024)
        for name, ref in (("q_norm", q_ref), ("k_norm", k_ref)):
            v = ref[0:1, :]
            o_ref[SMALL_ROW[name]:SMALL_ROW[name] + 1, :] = jnp.where(lane < HEAD_DIM, v + pltpu.roll(v, 64, 1), 0.0)
        sinks = jnp.zeros((1, 128), F32)
        for s in range(4):
            col = sk_ref[s]
            sinks = sinks + jnp.where(lane == 2 * s, jnp.sum(col[:BLK]), 0.0) + jnp.where(lane == 2 * s + 1, jnp.sum(col[BLK:]), 0.0)
        o_ref[SMALL_ROW["attn_sinks"]:SMALL_ROW["attn_sinks"] + 1, :] = sinks
        put("gmlp_v_norm", gvn_ref, 0, 0, 512)
        put("attn_out_norm", y_ref, 0, 0, 512)
        put("gmlp_out_norm", y_ref, 0, 512, 512)
        put("xa_norm", xa_ref, 0, 0, 1024)
        put("mem_norm", mem_ref, 0, 0, 1024)
        put("xa_q_norm", xq_ref, 0, 0, 256)
        put("xa_k_norm", xk_ref, 0, 0, 256)
        put("ffn_norm", ffn_ref, 0, 0, 1024)
        put("ffn_conv_b", cw_ref, 3, 0, 2 * D_FF)
        r8 = lax.broadcasted_iota(jnp.int32, (8, 128), 0)
        l8 = _lane((8, 128))
        bs = jnp.zeros((8, BLK), F32)
        for j in range(4):
            sel = (((r8 == 2 * j) & (l8 < 64)) | ((r8 == 2 * j + 1) & (l8 >= 64))).astype(F32).astype(BF16)
            xj = dbl_ref[j]
            hi = xj.astype(BF16)
            lo = (xj - hi.astype(F32)).astype(BF16)
            bs = bs + lax.dot_general(sel, hi, NT, preferred_element_type=F32) + lax.dot_general(sel, lo, NT, preferred_element_type=F32)
        o_ref[SMALL_ROW["gmlp_bs"]:SMALL_ROW["gmlp_bs"] + 8, :] = bs
        causal = lax.broadcasted_iota(jnp.int32, (BLK, BLK), 0) >= lax.broadcasted_iota(jnp.int32, (BLK, BLK), 1)
        for h in range(8):
            r0 = SMALL_ROW["gmlp_ws"] + h * BLK
            o_ref[r0:r0 + BLK, :] = jnp.where(causal, dws_ref[h], 0.0)
        for q in range(N_CHIPS):
            for j in range(3):
                for k in range(CONV_LANE_ROWS):
                    r0 = SMALL_ROW["ffn_conv"] + q * CONV_CHIP_ROWS + j * CONV_LANE_ROWS + k
                    l0 = (q * CONV_LANE_ROWS + k) * 128
                    o_ref[r0:r0 + 1, :] = cw_ref[j:j + 1, l0:l0 + 128]

    args = (dg_mix, dgq, dgk, dsk, dg_gvn, dg_y, dg_xa, dg_mem, dg_xq, dg_xk, dg_ffn, gcw, dbl, dws)
    full = lambda a: BS(a.shape, lambda i, nd=a.ndim: (0,) * nd)
    return _pcall(body, name="pack_small", grid=(1,), in_specs=[full(a) for a in args],
                  out_specs=BS((SMALL_ROWS, 128), lambda i: (0, 0)), out_shape=SDS((SMALL_ROWS, 128), F32))(*args)


def _adam(w, g, m, v):
    mn = ADAM_B1 * m + (1.0 - ADAM_B1) * g
    vn = ADAM_B2 * v + (1.0 - ADAM_B2) * (g * g)
    m_hat = mn / (1.0 - ADAM_B1 ** ADAM_STEP)
    v_hat = vn / (1.0 - ADAM_B2 ** ADAM_STEP)
    return -ADAM_LR * (m_hat / (jnp.sqrt(v_hat) + ADAM_EPS) + ADAM_WD * w), mn, vn


def adamw_small(gsum, w, m, v, chipvec):
    n = len(SMALL)

    def body(chip_ref, g_ref, *refs):
        w_refs, m_refs, v_refs = refs[:n], refs[n:2 * n], refs[2 * n:3 * n]
        outs = refs[3 * n:]
        go, do, mo, vo = outs[:n], outs[n:2 * n], outs[2 * n:3 * n], outs[3 * n:]

        def update(i, idx, g):
            d, mn, vn = _adam(w_refs[i][idx], g, m_refs[i][idx], v_refs[i][idx])
            go[i][idx] = g
            do[i][idx] = d
            mo[i][idx] = mn
            vo[i][idx] = vn

        for i, (name, length) in enumerate(SMALL_VECS):
            for k in range(-(-length // 128)):
                wd = min(128, length - k * 128)
                r = SMALL_ROW[name] + k
                update(i, (slice(0, 1), slice(k * 128, k * 128 + wd)), g_ref[r:r + 1, 0:wd])
        i_bs, i_ws, i_cv = len(SMALL_VECS), len(SMALL_VECS) + 1, len(SMALL_VECS) + 2
        update(i_bs, (0,), g_ref[SMALL_ROW["gmlp_bs"]:SMALL_ROW["gmlp_bs"] + 8, :])
        for h in range(8):
            r0 = SMALL_ROW["gmlp_ws"] + h * BLK
            update(i_ws, (0, h), g_ref[r0:r0 + BLK, :])
        mine = g_ref[pl.ds(pl.multiple_of(SMALL_ROW["ffn_conv"] + chip_ref[0] * CONV_CHIP_ROWS, 8), CONV_CHIP_ROWS), :]
        for j in range(3):
            for k in range(CONV_LANE_ROWS):
                r = j * CONV_LANE_ROWS + k
                update(i_cv, (0, slice(j, j + 1), slice(k * 128, (k + 1) * 128)), mine[r:r + 1, :])

    nat = [w[nm] for nm in SMALL]
    full = lambda a: BS(a.shape, lambda i, c, nd=a.ndim: (0,) * nd)
    outs = _pcall(body, name="adamw_small", grid=(1,), prefetch=1,
                  in_specs=[BS((SMALL_ROWS, 128), lambda i, c: (0, 0))] + [full(a) for a in nat] * 3,
                  out_specs=[full(a) for a in nat] * 4, out_shape=[SDS(a.shape, F32) for a in nat] * 4)(
        chipvec, gsum, *nat, *[m[nm] for nm in SMALL], *[v[nm] for nm in SMALL])
    return outs[:n], outs[n:2 * n], outs[2 * n:3 * n], outs[3 * n:]


def adamw_matrix(w, m, v, g_own, g_other, cvec, *, name):
    _, r, c = w.shape
    half = r // 2
    tr = _tile(half, (128, 176))
    T = half // tr

    def body(c_ref, w_ref, m_ref, v_ref, own_ref, oth_ref, g_ref, d_ref, mo_ref, vo_ref):
        g = jnp.where(pl.program_id(0) == c_ref[0], own_ref[...], oth_ref[...])
        d, mn, vn = _adam(w_ref[...], g, m_ref[...], v_ref[...])
        g_ref[...] = g
        d_ref[...] = d
        mo_ref[...] = mn
        vo_ref[...] = vn

    nat = BS((None, tr, c), lambda hf, t, cr: (0, hf * T + t, 0))
    hlf = BS((tr, c), lambda hf, t, cr: (t, 0))
    return _pcall(body, name=name, grid=(2, T), prefetch=1, in_specs=[nat, nat, nat, hlf, hlf], out_specs=[nat] * 4,
                  out_shape=[SDS(w.shape, F32)] * 4)(cvec, w, m, v, g_own, g_other)


def _place():
    return lax.axis_index("x"), lax.axis_index("y"), lax.axis_index("c")


def _other_chips(x, y):
    return [(1 - x, y), (x, 1 - y), (1 - x, 1 - y)]


def _rows_of_core(c, half):
    return pl.ds(pl.multiple_of(c * half, 16), half)


def _rcopy(src, dst, sems, k, to):
    return pltpu.make_async_remote_copy(src_ref=src, dst_ref=dst, send_sem=sems[0].at[k], recv_sem=sems[1].at[k],
                                        device_id=to, device_id_type=MESH)


def _comm_call(body, *, name, out_shape, n_in, n_sems, aliases=None):
    return pl.pallas_call(body, name=name, out_shape=out_shape, in_specs=[ANY] * n_in, out_specs=[ANY] * len(out_shape),
                          scratch_shapes=[pltpu.SemaphoreType.DMA((n_sems,)), pltpu.SemaphoreType.DMA((n_sems,))],
                          input_output_aliases=aliases or {},
                          compiler_params=pltpu.CompilerParams(has_side_effects=True))


def cast_shards(shards, conv, chipvec):
    n = len(shards)

    def body(chip_ref, *refs):
        for i_ref, o_ref in zip(refs[:n + 1], refs[n + 1:]):
            o_ref[...] = i_ref[...].astype(o_ref.dtype)

    in_specs = [BS((s.shape[0] // 4, s.shape[1]), lambda i, p: (i, 0)) for s in shards]
    in_specs.append(BS(conv.shape, lambda i, p: (0, 0)))
    out_specs = [BS((None, s.shape[0] // 4, s.shape[1]), lambda i, p: (p[0], i, 0)) for s in shards]
    out_specs.append(BS((None,) + conv.shape, lambda i, p: (p[0], 0, 0)))
    out_shape = [SDS((N_CHIPS,) + s.shape, MXU_DTYPE) for s in shards] + [SDS((N_CHIPS,) + conv.shape, F32)]
    return _pcall(body, name="cast_shards", grid=(4,), prefetch=1, in_specs=in_specs, out_specs=out_specs,
                  out_shape=out_shape)(chipvec, *shards, conv)


HBM = pl.BlockSpec(memory_space=pltpu.HBM)
SEM = pl.BlockSpec(memory_space=pltpu.SEMAPHORE)
DATAFLOW = pltpu.SideEffectType.DATAFLOW_SIDE_EFFECTING


def _gather_copies(bufs, send_sems, recv_sems, outgoing):
    x, y, c = _place()
    p = 2 * x + y
    cps = []
    for i, o in enumerate(bufs):
        for j, (cx, cy) in enumerate(_other_chips(x, y)):
            slot = o.at[p] if outgoing else o.at[2 * cx + cy]
            cps.append(_rcopy(slot, slot, (send_sems, recv_sems), 3 * i + j, (cx, cy, c)))
    return cps


def gather_start(slots):
    n = len(slots)

    def body(*refs):
        send_sems, recv_sems, thru = refs[n], refs[n + 1], refs[n + 2:]
        for cp in _gather_copies(thru, send_sems, recv_sems, True):
            cp.start()

    hbm = [pltpu.with_memory_space_constraint(s, pltpu.HBM) for s in slots]
    outs = pl.pallas_call(
        body, name="gather_start_%d" % n,
        out_shape=[pltpu.SemaphoreType.DMA((3 * n,)), pltpu.SemaphoreType.DMA((3 * n,))]
        + [pltpu.HBM(s.shape, s.dtype) for s in slots],
        in_specs=[HBM] * n, out_specs=[SEM, SEM] + [HBM] * n, input_output_aliases={i: 2 + i for i in range(n)},
        compiler_params=pltpu.CompilerParams(has_side_effects=DATAFLOW))(*hbm)
    return outs[0], outs[1], outs[2:]


def gather_wait(send_sems, recv_sems, bufs, after):
    n = len(bufs)

    def body(*refs):
        ins, send_ref, recv_ref = refs[:n], refs[n], refs[n + 1]
        for cp in _gather_copies(ins, send_ref, recv_ref, False):
            cp.wait_send()
            cp.wait_recv()

    return pl.pallas_call(
        body, name="gather_wait_%d" % n, out_shape=[pltpu.HBM(s.shape, s.dtype) for s in bufs],
        in_specs=[HBM] * n + [SEM, SEM, ANY], out_specs=[HBM] * n, input_output_aliases={i: i for i in range(n)},
        compiler_params=pltpu.CompilerParams(has_side_effects=DATAFLOW))(*bufs, send_sems, recv_sems, after)


def _peers(x, y, c):
    return [(1 - x if k & 4 else x, 1 - y if k & 2 else y, 1 - c if k & 1 else c) for k in range(1, N_DEV)]


def _partial_copies(g_ref, land_ref, send_sems, recv_sems, outgoing):
    x, y, c = _place()
    half = g_ref.shape[1] // 2
    cps = []
    for k, (px, py, pc) in enumerate(_peers(x, y, c)):
        src = g_ref.at[2 * px + py, _rows_of_core(pc, half)]
        dst = land_ref.at[4 * x + 2 * y + c] if outgoing else land_ref.at[4 * px + 2 * py + pc]
        cps.append(_rcopy(src, dst, (send_sems, recv_sems), k, (px, py, pc)))
    return cps


def partials_start(g, *, name):
    land = lax.empty((N_DEV, g.shape[1] // 2, g.shape[2]), g.dtype)

    def body(g_ref, land_ref, send_sems, recv_sems, g_thru, land_thru):
        for cp in _partial_copies(g_thru, land_thru, send_sems, recv_sems, True):
            cp.start()

    return pl.pallas_call(
        body, name=name,
        out_shape=[pltpu.SemaphoreType.DMA((N_DEV - 1,)), pltpu.SemaphoreType.DMA((N_DEV - 1,)),
                   pltpu.HBM(g.shape, g.dtype), pltpu.HBM(land.shape, land.dtype)],
        in_specs=[HBM, HBM], out_specs=[SEM, SEM, HBM, HBM], input_output_aliases={0: 2, 1: 3},
        compiler_params=pltpu.CompilerParams(has_side_effects=DATAFLOW))(
        pltpu.with_memory_space_constraint(g, pltpu.HBM), pltpu.with_memory_space_constraint(land, pltpu.HBM))


def partials_wait(started, after):
    n = len(started)

    def body(*refs):
        for i in range(n):
            send_ref, recv_ref, g_ref, land_ref = refs[4 * i:4 * i + 4]
            for cp in _partial_copies(g_ref, land_ref, send_ref, recv_ref, False):
                cp.wait_send()
                cp.wait_recv()

    flat = [a for s in started for a in s]
    bufs = [a for s in started for a in s[2:]]
    outs = pl.pallas_call(
        body, name="partials_wait", out_shape=[pltpu.HBM(b.shape, b.dtype) for b in bufs],
        in_specs=[SEM, SEM, HBM, HBM] * n + [ANY], out_specs=[HBM] * (2 * n),
        input_output_aliases={4 * i + 2 + j: 2 * i + j for i in range(n) for j in range(2)},
        compiler_params=pltpu.CompilerParams(has_side_effects=DATAFLOW))(*flat, after)
    return [(outs[2 * i], outs[2 * i + 1]) for i in range(n)]


def sum_partials(pairs, order):
    n = len(pairs)

    def body(o_ref, *refs):
        j = pl.program_id(0)
        for g_ref, l_ref, f_ref in zip(refs[:n], refs[n:2 * n], refs[2 * n:]):
            @pl.when(j == 0)
            def _():
                f_ref[...] = g_ref[...].astype(F32)

            @pl.when(j > 0)
            def _():
                f_ref[...] += l_ref[...].astype(F32)

    g4 = [g.reshape(g.shape[0], 2, g.shape[1] // 2, g.shape[2]) for g, _ in pairs]
    lands = [l for _, l in pairs]
    return _pcall(body, name="sum_partials", grid=(N_DEV,), prefetch=1,
                  in_specs=[BS((None, None) + g.shape[2:], lambda j, o: (o[0], o[1], 0, 0)) for g in g4]
                  + [BS((None,) + l.shape[1:], lambda j, o: (o[jnp.maximum(j, 1) + 1], 0, 0)) for l in lands],
                  out_specs=[BS(l.shape[1:], lambda j, o: (0, 0)) for l in lands],
                  out_shape=[SDS(l.shape[1:], F32) for l in lands])(order, *g4, *lands)


def pair_share(fs):
    n = len(fs)

    def body(*refs):
        f_refs, o_refs, sems = refs[:n], refs[n:2 * n], refs[2 * n:]
        x, y, c = _place()
        cps = [_rcopy(f, o, sems, i, (x, y, 1 - c)) for i, (f, o) in enumerate(zip(f_refs, o_refs))]
        for cp in cps:
            cp.start()
        for cp in cps:
            cp.wait()

    return _comm_call(body, name="pair_share", n_in=n, n_sems=n, out_shape=[SDS(f.shape, f.dtype) for f in fs])(*fs)


def _small_copies(s_ref, land_ref, send_sems, recv_sems, outgoing):
    x, y, c = _place()
    cps = []
    for k, (px, py, pc) in enumerate(_peers(x, y, c)):
        dst = land_ref.at[4 * x + 2 * y + c] if outgoing else land_ref.at[4 * px + 2 * py + pc]
        cps.append(_rcopy(s_ref, dst, (send_sems, recv_sems), k, (px, py, pc)))
    return cps


def small_start(sm):
    land = lax.empty((N_DEV,) + sm.shape, sm.dtype)

    def body(s_ref, land_ref, send_sems, recv_sems, s_thru, land_thru):
        for cp in _small_copies(s_thru, land_thru, send_sems, recv_sems, True):
            cp.start()

    return pl.pallas_call(
        body, name="small_start",
        out_shape=[pltpu.SemaphoreType.DMA((N_DEV - 1,)), pltpu.SemaphoreType.DMA((N_DEV - 1,)),
                   pltpu.HBM(sm.shape, sm.dtype), pltpu.HBM(land.shape, land.dtype)],
        in_specs=[HBM, HBM], out_specs=[SEM, SEM, HBM, HBM], input_output_aliases={0: 2, 1: 3},
        compiler_params=pltpu.CompilerParams(has_side_effects=DATAFLOW))(
        pltpu.with_memory_space_constraint(sm, pltpu.HBM), pltpu.with_memory_space_constraint(land, pltpu.HBM))


def small_wait(send_sems, recv_sems, sm, land, after):
    def body(send_ref, recv_ref, s_ref, land_ref, after_ref, s_out, land_out):
        for cp in _small_copies(s_ref, land_ref, send_ref, recv_ref, False):
            cp.wait_send()
            cp.wait_recv()

    return pl.pallas_call(
        body, name="small_wait", out_shape=[pltpu.HBM(sm.shape, sm.dtype), pltpu.HBM(land.shape, land.dtype)],
        in_specs=[SEM, SEM, HBM, HBM, ANY], out_specs=[HBM, HBM], input_output_aliases={2: 0, 3: 1},
        compiler_params=pltpu.CompilerParams(has_side_effects=DATAFLOW))(send_sems, recv_sems, sm, land, after)


def sum_small(own, land, mevec):
    n, rows, width = land.shape
    tr = _tile(rows, (184, 8))

    def body(me_ref, own_ref, land_ref, o_ref):
        acc = jnp.zeros((tr, width), F32)
        for s in range(n):
            acc = acc + jnp.where(me_ref[0] == s, own_ref[...], land_ref[s])
        o_ref[...] = acc

    return _pcall(body, name="sum_small", grid=(rows // tr,), prefetch=1,
                  in_specs=[BS((tr, width), lambda i, me: (i, 0)), BS((n, tr, width), lambda i, me: (0, i, 0))],
                  out_specs=BS((tr, width), lambda i, me: (i, 0)), out_shape=SDS((rows, width), F32))(mevec, own, land)


def _to_full(blk, col):
    n, r, c = blk.shape
    return blk.transpose(1, 0, 2).reshape(r, n * c) if col else blk.reshape(n * r, c)


def _dup_cols(w):
    dup = lambda t: jnp.concatenate([t[:, :64], t[:, :64], t[:, 64:], t[:, 64:]], axis=1)
    return jnp.concatenate([w[:, :512], dup(w[:, 512:640]), dup(w[:, 640:768]), w[:, 768:]], axis=1)


def _fold_cols(d):
    fold = lambda t: jnp.concatenate([t[:, 0:64] + t[:, 64:128], t[:, 128:192] + t[:, 192:256]], axis=1)
    return jnp.concatenate([d[:, :512], fold(d[:, 512:768]), fold(d[:, 768:1024]), d[:, 1024:]], axis=1)


def _local_step(x, mem, positions, target, w_in, later, sp, emit):
    gain = lambda n: sp[n].reshape(1, -1)
    half = HEAD_DIM // 2
    inv_freq = 1.0 / (10000.0 ** (jnp.arange(half, dtype=F32) * (2.0 / HEAD_DIM)))
    ang = positions.astype(F32)[:, None] * inv_freq
    cos, sin = jnp.cos(ang), jnp.sin(ang)
    cos128 = jnp.tile(cos, (1, 4))
    sin128 = jnp.concatenate([-sin, sin, -sin, sin], axis=1)
    seg = jnp.arange(128) // HEAD_DIM
    bmat = (seg[:, None] == seg[None, :]).astype(BF16)
    gq128, gk128 = jnp.tile(gain("q_norm"), (1, 2)), jnp.tile(gain("k_norm"), (1, 2))
    sinkcol = jnp.repeat(sp["attn_sinks"].reshape(4, 2), BLK, axis=1).reshape(4, 2 * BLK, 1)
    wsc = sp["gmlp_ws"] * jnp.tril(jnp.ones((BLK, BLK), F32))[None]
    w2 = wsc.reshape(4, 2 * BLK, BLK).astype(MXU_DTYPE)
    w2t = wsc.swapaxes(1, 2).reshape(4, 2 * BLK, BLK).astype(MXU_DTYPE)
    bsl = jnp.repeat(sp["gmlp_bs"].reshape(4, 2, BLK).transpose(0, 2, 1), HEAD_DIM, axis=2)
    cb = sp["ffn_conv_b"].reshape(1, -1)
    w_in_d = _dup_cols(_to_full(w_in, True))[None]

    h1, proj = rms_mm(x, gain("mix_norm"), w_in_d, name="mix_in")
    qr, kr, vb, gu, gvn = mixer_pre(proj, cos128, sin128, gq128, gk128, gain("gmlp_v_norm"), bmat)
    attn, ya = swa_fwd(qr, kr, vb, sinkcol, gain("attn_out_norm"))
    gm, y = gmlp_fwd(gvn, gu, ya, w2, bsl, gain("gmlp_out_norm"))
    wf, cw = later(y)
    w_out, xa_wq, xa_wo, ffn_down = (_to_full(wf[n], False) for n in ("w_out", "xa_wq", "xa_wo", "ffn_down"))
    x1 = mm(y, w_out, res=x, name="mix_out")
    h2, qx = rms_mm(x1, gain("xa_norm"), xa_wq[None], name="xa_q")
    mn, kv = rms_mm(mem, gain("mem_norm"), wf["xa_wkv"], name="xa_kv")
    kn, vbx = mem_pre(kv, gain("xa_k_norm"))
    xo = xattn_fwd(qx, kn, vbx, gain("xa_q_norm"))
    x2 = mm(xo, xa_wo, res=x1, name="xa_out")
    h3, a = rms_mm(x2, gain("ffn_norm"), wf["ffn_up"], name="ffn_up", tm=1024)
    f = convgate_fwd(a, cw, cb)
    dx3, loss_acc = mm_loss(f, ffn_down, x2, target, name="ffn_down_loss")

    by_rows = lambda g: g.reshape(N_CHIPS, g.shape[1] // N_CHIPS, g.shape[2])
    df = mm_nt(dx3, ffn_down[None], name="d_f")
    emit("ffn_down", by_rows(mm_tn(f, dx3, name="g_ffn_down", out_dtype=WIRE_DTYPE)))
    dc, gcw = convgate_bwd(a, df, cw, cb)
    da = conv_transpose(dc, cw)
    dx2, dg_ffn = mm_nt_rms_bwd(da, wf["ffn_up"], x2, gain("ffn_norm"), dx3, name="d_x2", tm=256)
    emit("ffn_up", mm_tn(h3, da, name="g_ffn_up", out_dtype=WIRE_DTYPE, chunks=N_CHIPS))
    dxo = mm_nt(dx2, xa_wo[None], name="d_xo")
    emit("xa_wo", by_rows(mm_tn(xo, dx2, name="g_xa_wo", out_dtype=WIRE_DTYPE)))
    dqx, dkn, dvx, dg_xq = xattn_bwd(qx, dxo, kn, vbx, gain("xa_q_norm"))
    dx1, dg_xa = mm_nt_rms_bwd(dqx, xa_wq[None], x1, gain("xa_norm"), dx2, name="d_x1")
    emit("xa_wq", by_rows(mm_tn(h2, dqx, name="g_xa_wq", out_dtype=WIRE_DTYPE)))
    dkv, dg_xk = mem_bwd(kv, dkn, dvx, gain("xa_k_norm"))
    _, dg_mem = mm_nt_rms_bwd(dkv, wf["xa_wkv"], mem, gain("mem_norm"), jnp.zeros_like(mem), name="d_mem")
    emit("xa_wkv", mm_tn(mn, dkv, name="g_xa_wkv", out_dtype=WIRE_DTYPE, chunks=N_CHIPS))
    dattn, dgm, dg_y = mm_nt_post_bwd(dx1, w_out[None], attn, gm, gain("attn_out_norm"), gain("gmlp_out_norm"),
                                      name="d_mix_out")
    emit("w_out", by_rows(mm_tn(y, dx1, name="g_w_out", out_dtype=WIRE_DTYPE)))
    dgu, dgvn, dws, dbl = gmlp_bwd(dgm, gvn, gu, w2, w2t, bsl)
    dqr, dkr, dvb, dsk = swa_bwd(qr, kr, vb, sinkcol, dattn)
    dproj, dgq, dgk, dg_gvn = mixer_pre_bwd(proj, cos128, sin128, gq128, gk128, gain("gmlp_v_norm"), bmat,
                                            dqr, dkr, dvb, dgu, dgvn)
    g_in = _fold_cols(mm_tn(h1, dproj, name="g_w_in", out_dtype=F32)[0])
    emit("w_in", g_in.reshape(1024, N_CHIPS, 448).transpose(1, 0, 2).astype(WIRE_DTYPE))
    grad_x, dg_mix = mm_nt_rms_bwd(dproj, w_in_d, x, gain("mix_norm"), dx1, name="d_x")
    packed = pack_small(dg_mix, dgq, dgk, dsk, dg_gvn, dg_y, dg_xa, dg_mem, dg_xq, dg_xk, dg_ffn, gcw, dbl, dws)
    return loss_acc, grad_x, packed


def _gather_step(w, chipvec):
    slots = cast_shards([w[n][0] for n in BIG_NAMES], w["ffn_conv"][0], chipvec)
    send_a, recv_a, first = gather_start(slots[:1])
    send_b, recv_b, rest = gather_start(slots[1:])
    w_in, = gather_wait(send_a, recv_a, first, chipvec)

    def later(after):
        got = gather_wait(send_b, recv_b, rest, after)
        return dict(zip(BIG_NAMES[1:], got[:-1])), _to_full(got[-1], True)

    return w_in, later


def _reduce_update(started, packed, w, m, v, chipvec, cvec, order):
    small_sent = small_start(packed)
    own = sum_partials(partials_wait([started[n] for n in BIG_NAMES], small_sent[2]), order)
    other = pair_share(own)
    res = [{}, {}, {}, {}]
    for n, g_own, g_other in zip(BIG_NAMES, own, other):
        for d, o in zip(res, adamw_matrix(w[n], m[n], v[n], g_own, g_other, cvec, name="adamw_" + n)):
            d[n] = o
    mevec = (2 * order[0:1] + order[1:2]).astype(jnp.int32)
    small_sum = sum_small(*small_wait(*small_sent, res[3][BIG_NAMES[-1]]), mevec)
    for d, outs in zip(res, adamw_small(small_sum, w, m, v, chipvec)):
        d.update(zip(SMALL, outs))
    return res


def kernel(x, mem, positions, mix_norm, w_in, q_norm, k_norm, attn_sinks, gmlp_v_norm, gmlp_ws, gmlp_bs, attn_out_norm, gmlp_out_norm, w_out, xa_norm, mem_norm, xa_wq, xa_wkv, xa_q_norm, xa_k_norm, xa_wo, ffn_norm, ffn_up, ffn_conv, ffn_conv_b, ffn_down, loss_target, m_mix_norm, m_w_in, m_q_norm, m_k_norm, m_attn_sinks, m_gmlp_v_norm, m_gmlp_ws, m_gmlp_bs, m_attn_out_norm, m_gmlp_out_norm, m_w_out, m_xa_norm, m_mem_norm, m_xa_wq, m_xa_wkv, m_xa_q_norm, m_xa_k_norm, m_xa_wo, m_ffn_norm, m_ffn_up, m_ffn_conv, m_ffn_conv_b, m_ffn_down, v_mix_norm, v_w_in, v_q_norm, v_k_norm, v_attn_sinks, v_gmlp_v_norm, v_gmlp_ws, v_gmlp_bs, v_attn_out_norm, v_gmlp_out_norm, v_w_out, v_xa_norm, v_mem_norm, v_xa_wq, v_xa_wkv, v_xa_q_norm, v_xa_k_norm, v_xa_wo, v_ffn_norm, v_ffn_up, v_ffn_conv, v_ffn_conv_b, v_ffn_down):
    w = dict(mix_norm=mix_norm, w_in=w_in, q_norm=q_norm, k_norm=k_norm, attn_sinks=attn_sinks, gmlp_v_norm=gmlp_v_norm, gmlp_ws=gmlp_ws, gmlp_bs=gmlp_bs, attn_out_norm=attn_out_norm, gmlp_out_norm=gmlp_out_norm, w_out=w_out, xa_norm=xa_norm, mem_norm=mem_norm, xa_wq=xa_wq, xa_wkv=xa_wkv, xa_q_norm=xa_q_norm, xa_k_norm=xa_k_norm, xa_wo=xa_wo, ffn_norm=ffn_norm, ffn_up=ffn_up, ffn_conv=ffn_conv, ffn_conv_b=ffn_conv_b, ffn_down=ffn_down)
    m = dict(mix_norm=m_mix_norm, w_in=m_w_in, q_norm=m_q_norm, k_norm=m_k_norm, attn_sinks=m_attn_sinks, gmlp_v_norm=m_gmlp_v_norm, gmlp_ws=m_gmlp_ws, gmlp_bs=m_gmlp_bs, attn_out_norm=m_attn_out_norm, gmlp_out_norm=m_gmlp_out_norm, w_out=m_w_out, xa_norm=m_xa_norm, mem_norm=m_mem_norm, xa_wq=m_xa_wq, xa_wkv=m_xa_wkv, xa_q_norm=m_xa_q_norm, xa_k_norm=m_xa_k_norm, xa_wo=m_xa_wo, ffn_norm=m_ffn_norm, ffn_up=m_ffn_up, ffn_conv=m_ffn_conv, ffn_conv_b=m_ffn_conv_b, ffn_down=m_ffn_down)
    v = dict(mix_norm=v_mix_norm, w_in=v_w_in, q_norm=v_q_norm, k_norm=v_k_norm, attn_sinks=v_attn_sinks, gmlp_v_norm=v_gmlp_v_norm, gmlp_ws=v_gmlp_ws, gmlp_bs=v_gmlp_bs, attn_out_norm=v_attn_out_norm, gmlp_out_norm=v_gmlp_out_norm, w_out=v_w_out, xa_norm=v_xa_norm, mem_norm=v_mem_norm, xa_wq=v_xa_wq, xa_wkv=v_xa_wkv, xa_q_norm=v_xa_q_norm, xa_k_norm=v_xa_k_norm, xa_wo=v_xa_wo, ffn_norm=v_ffn_norm, ffn_up=v_ffn_up, ffn_conv=v_ffn_conv, ffn_conv_b=v_ffn_conv_b, ffn_down=v_ffn_down)
    ix, iy, ic = lax.axis_index("x"), lax.axis_index("y"), lax.axis_index("c")
    chip = 2 * ix + iy
    chipvec = chip.astype(jnp.int32).reshape(1)
    cvec = ic.astype(jnp.int32).reshape(1)
    order = jnp.stack([chip, ic] + [4 * px + 2 * py + pc for px, py, pc in _peers(ix, iy, ic)]).astype(jnp.int32)

    w_in_all, later = _gather_step(w, chipvec)
    sp = {n: w[n][0] for n in SMALL if n != "ffn_conv"}
    started = {}

    def emit(name, g):
        started[name] = partials_start(g, name="partials_start_" + name)

    loss_acc, grad_x, packed = _local_step(x[0], mem[0], positions[0], loss_target[0], w_in_all, later, sp, emit)
    grads, delta, new_m, new_v = _reduce_update(started, packed, w, m, v, chipvec, cvec, order)
    loss = lax.psum(loss_acc[0, 0], ("x", "y", "c"))
    ordered = lambda d: [d[n] for n in WEIGHTS]
    return (loss, grad_x[None], *ordered(grads), *ordered(delta), *ordered(new_m), *ordered(new_v))
```

```python
import math

import jax
import jax.numpy as jnp
from jax import lax
from jax.experimental import pallas as pl
from jax.experimental.pallas import tpu as pltpu

F32 = jnp.float32
BF16 = jnp.bfloat16
MXU_DTYPE = jnp.bfloat16
WIRE_DTYPE = jnp.bfloat16
EPS = 1e-6
VMEM_LIMIT_V7X = 56 * 1024 * 1024

D_MODEL = 1024
HEAD_DIM = 64
BLK = 128
XA_HEADS = 4
XA_DH = 256
MEM_LEN = 256
D_FF = 2816
IN_COLS_DUP = 2048
N_CHIPS = 4
N_DEV = 8

ADAM_LR = 0.001
ADAM_B1 = 0.9
ADAM_B2 = 0.999
ADAM_EPS = 1e-08
ADAM_WD = 0.01
ADAM_STEP = 10

NT = (((1,), (1,)), ((), ()))
TN = (((0,), (0,)), ((), ()))
NN = (((1,), (0,)), ((), ()))
MINF = float(jnp.finfo(jnp.float32).min)
GELU_K0 = math.sqrt(2.0 / math.pi)
GELU_K1 = 0.044715

BS = pl.BlockSpec
SDS = jax.ShapeDtypeStruct
ANY = pl.BlockSpec(memory_space=pl.ANY)
MESH = pl.DeviceIdType.MESH


def _dot(a, b, dims=NN):
    return lax.dot_general(a.astype(MXU_DTYPE), b.astype(MXU_DTYPE), dims, preferred_element_type=F32)


def _segsum(x, bmat):
    hi = x.astype(BF16)
    lo = (x - hi.astype(F32)).astype(BF16)
    return (jnp.dot(hi, bmat, preferred_element_type=F32) + jnp.dot(lo, bmat, preferred_element_type=F32))


def _gelu(x):
    return 0.5 * x * (1.0 + jnp.tanh(GELU_K0 * (x + GELU_K1 * x * x * x)))


def _gelu_grad(x):
    t = jnp.tanh(GELU_K0 * (x + GELU_K1 * x * x * x))
    return 0.5 * (1.0 + t) + 0.5 * x * (1.0 - t * t) * GELU_K0 * (1.0 + 3.0 * GELU_K1 * x * x)


def _rms(x):
    return lax.rsqrt(jnp.mean(x * x, axis=-1, keepdims=True) + EPS)


def _rms_bwd(dy, x, g, r):
    dyg = dy * g
    dx = r * dyg - x * (r * r * r) * jnp.mean(dyg * x, axis=-1, keepdims=True)
    return dx, dy * x * r


def _pcall(body, *, name, grid, in_specs, out_specs, out_shape, scratch=(), prefetch=0, after=None):
    params = pltpu.CompilerParams(dimension_semantics=("arbitrary",) * len(grid), vmem_limit_bytes=VMEM_LIMIT_V7X)
    in_specs = list(in_specs)
    kernel_fn = body
    if after is not None:
        n_in = prefetch + len(in_specs)
        in_specs.append(ANY)

        def kernel_fn(*refs):
            return body(*refs[:n_in], *refs[n_in + 1:])

    if prefetch:
        spec = pltpu.PrefetchScalarGridSpec(num_scalar_prefetch=prefetch, grid=grid, in_specs=in_specs,
                                            out_specs=out_specs, scratch_shapes=list(scratch))
        call = pl.pallas_call(kernel_fn, name=name, grid_spec=spec, out_shape=out_shape, compiler_params=params)
    else:
        call = pl.pallas_call(kernel_fn, name=name, grid=grid, in_specs=in_specs, out_specs=out_specs,
                              out_shape=out_shape, scratch_shapes=list(scratch), compiler_params=params)
    return call if after is None else (lambda *args: call(*args, after))


def _tile(n, prefs):
    for p in prefs:
        if p <= n and n % p == 0:
            return p
    return n


def _acc_rows(ref, row, val):
    ref[row:row + 1, :] += jnp.sum(val, axis=0, keepdims=True)


def rms_mm(x, g, w3, *, name, tm=512):
    M, K = x.shape
    Q, _, C = w3.shape
    tm = _tile(M, (tm, 256))

    def body(x_ref, g_ref, w_ref, h_ref, o_ref):
        @pl.when(pl.program_id(1) == 0)
        def _():
            xv = x_ref[...]
            h_ref[...] = (xv * _rms(xv) * g_ref[...]).astype(h_ref.dtype)

        o_ref[...] = _dot(h_ref[...], w_ref[pl.program_id(1)])

    return _pcall(body, name=name, grid=(M // tm, Q),
                  in_specs=[BS((tm, K), lambda i, j: (i, 0)), BS((1, K), lambda i, j: (0, 0)),
                            BS((Q, K, C), lambda i, j: (0, 0, 0))],
                  out_specs=[BS((tm, K), lambda i, j: (i, 0)), BS((tm, C), lambda i, j: (i, j))],
                  out_shape=[SDS((M, K), MXU_DTYPE), SDS((M, Q * C), F32)])(x, g, w3)


def mm(a, w, *, name, res):
    M, K = a.shape
    N = w.shape[1]
    tm = _tile(M, (512, 256))

    def body(a_ref, w_ref, r_ref, o_ref):
        o_ref[...] = _dot(a_ref[...], w_ref[...]) + r_ref[...]

    return _pcall(body, name=name, grid=(M // tm,),
                  in_specs=[BS((tm, K), lambda i: (i, 0)), BS((K, N), lambda i: (0, 0)), BS((tm, N), lambda i: (i, 0))],
                  out_specs=BS((tm, N), lambda i: (i, 0)), out_shape=SDS((M, N), F32))(a, w, res)


def _nt_chunks(a_ref, w_ref):
    q_n, _, kc = w_ref.shape
    acc = _dot(a_ref[:, 0:kc], w_ref[0], NT)
    for q in range(1, q_n):
        acc = acc + _dot(a_ref[:, q * kc:(q + 1) * kc], w_ref[q], NT)
    return acc


def mm_nt(a, w3, *, name, after=None):
    M = a.shape[0]
    Q, N, Kc = w3.shape
    tm = _tile(M, (512, 256))

    def body(a_ref, w_ref, o_ref):
        o_ref[...] = _nt_chunks(a_ref, w_ref)

    return _pcall(body, name=name, grid=(M // tm,), after=after,
                  in_specs=[BS((tm, Q * Kc), lambda i: (i, 0)), BS((Q, N, Kc), lambda i: (0, 0, 0))],
                  out_specs=BS((tm, N), lambda i: (i, 0)), out_shape=SDS((M, N), F32))(a, w3)


def mm_nt_rms_bwd(a, w3, x, g, dres, *, name, tm=512, after=None):
    M = a.shape[0]
    Q, N, Kc = w3.shape
    tm = _tile(M, (tm, 256))

    def body(a_ref, w_ref, x_ref, g_ref, dr_ref, dx_ref, dg_ref):
        @pl.when(pl.program_id(0) == 0)
        def _():
            dg_ref[...] = jnp.zeros_like(dg_ref)

        xv = x_ref[...]
        dx, dgc = _rms_bwd(_nt_chunks(a_ref, w_ref), xv, g_ref[...], _rms(xv))
        dx_ref[...] = dr_ref[...] + dx
        _acc_rows(dg_ref, 0, dgc)

    row = BS((tm, N), lambda i: (i, 0))
    return _pcall(body, name=name, grid=(M // tm,), after=after,
                  in_specs=[BS((tm, Q * Kc), lambda i: (i, 0)), BS((Q, N, Kc), lambda i: (0, 0, 0)), row,
                            BS((1, N), lambda i: (0, 0)), row],
                  out_specs=[row, BS((8, N), lambda i: (0, 0))],
                  out_shape=[SDS((M, N), F32), SDS((8, N), F32)])(a, w3, x, g, dres)


def mm_nt_post_bwd(a, w3, attn, gm, gao, ggo, *, name, after=None):
    M = a.shape[0]
    Q, N, Kc = w3.shape
    tm = _tile(M, (512, 256))
    hw = N // 2

    def body(a_ref, w_ref, at_ref, gm_ref, gao_ref, ggo_ref, da_ref, dgm_ref, dg_ref):
        @pl.when(pl.program_id(0) == 0)
        def _():
            dg_ref[...] = jnp.zeros_like(dg_ref)

        dy = _nt_chunks(a_ref, w_ref)
        av, gmv = at_ref[...], gm_ref[...]
        da, dga = _rms_bwd(dy[:, :hw], av, gao_ref[...], _rms(av))
        dgm, dgg = _rms_bwd(dy[:, hw:], gmv, ggo_ref[...], _rms(gmv))
        da_ref[...] = da
        dgm_ref[...] = dgm
        dg_ref[0:1, :hw] += jnp.sum(dga, axis=0, keepdims=True)
        dg_ref[0:1, hw:] += jnp.sum(dgg, axis=0, keepdims=True)

    half = BS((tm, hw), lambda i: (i, 0))
    const = lambda r, w: BS((r, w), lambda i: (0, 0))
    return _pcall(body, name=name, grid=(M // tm,), after=after,
                  in_specs=[BS((tm, Q * Kc), lambda i: (i, 0)), BS((Q, N, Kc), lambda i: (0, 0, 0)), half, half,
                            const(1, hw), const(1, hw)],
                  out_specs=[half, half, const(8, N)],
                  out_shape=[SDS((M, hw), F32), SDS((M, hw), F32), SDS((8, N), F32)])(a, w3, attn, gm, gao, ggo)


def mm_loss(a, w, res, target, *, name):
    M, K = a.shape
    N = w.shape[1]
    tm = _tile(M, (512, 256))

    def body(a_ref, w_ref, r_ref, t_ref, d_ref, l_ref):
        @pl.when(pl.program_id(0) == 0)
        def _():
            l_ref[...] = jnp.zeros_like(l_ref)

        e = _dot(a_ref[...], w_ref[...]) + r_ref[...] - t_ref[...]
        d_ref[...] = e * (1.0 / N)
        l_ref[...] += jnp.sum(e * e) * (0.5 / N)

    row = BS((tm, N), lambda i: (i, 0))
    return _pcall(body, name=name, grid=(M // tm,),
                  in_specs=[BS((tm, K), lambda i: (i, 0)), BS((K, N), lambda i: (0, 0)), row, row],
                  out_specs=[row, BS((8, 128), lambda i: (0, 0))],
                  out_shape=[SDS((M, N), F32), SDS((8, 128), F32)])(a, w, res, target)


def mm_tn(a, b, *, name, out_dtype, chunks=1):
    M, K = a.shape
    N = b.shape[1]
    C = N // chunks
    tm = _tile(M, (512, 256))
    tk = _tile(K, (1408, 1024, 512))
    tn = _tile(C, (1408, 1024, 512))
    per = C // tn
    nm = M // tm

    def body(a_ref, b_ref, o_ref, acc):
        m = pl.program_id(2)

        @pl.when(m == 0)
        def _():
            acc[...] = jnp.zeros_like(acc)

        acc[...] += _dot(a_ref[...], b_ref[...], TN)

        @pl.when(m == nm - 1)
        def _():
            o_ref[...] = acc[...].astype(o_ref.dtype)

    return _pcall(body, name=name, grid=(K // tk, N // tn, nm),
                  in_specs=[BS((tm, tk), lambda k, n, m: (m, k)), BS((tm, tn), lambda k, n, m: (m, n))],
                  out_specs=BS((None, tk, tn), lambda k, n, m: (n // per, k, n % per)),
                  out_shape=SDS((chunks, K, C), out_dtype), scratch=[pltpu.VMEM((tk, tn), F32)])(a, b)


def _lane(shape):
    return lax.broadcasted_iota(jnp.int32, shape, 1)


def _norm_rope(slab, g, bmat, cos, sin, first):
    r = lax.rsqrt(_segsum(slab * slab, bmat) * (1.0 / HEAD_DIM) + EPS)
    qn = slab * r * g
    swapped = jnp.where(first, pltpu.roll(qn, 96, 1), pltpu.roll(qn, 32, 1))
    return qn * cos + swapped * sin


def mixer_pre(proj, cos, sin, gq, gk, gvn, bmat):
    S = proj.shape[0]
    tm = _tile(S, (256,))

    def body(p_ref, c_ref, s_ref, gq_ref, gk_ref, gvn_ref, b_ref, qr_ref, kr_ref, vb_ref, gu_ref, gvo_ref):
        cos_v, sin_v, bm = c_ref[...], s_ref[...], b_ref[...]
        first = (_lane((tm, 128)) & 63) < 32
        for s in range(4):
            sl = slice(s * 128, (s + 1) * 128)
            qr_ref[:, sl] = _norm_rope(p_ref[:, sl], gq_ref[...], bm, cos_v, sin_v, first).astype(qr_ref.dtype)
        for s in range(2):
            kr_ref[:, s * 128:(s + 1) * 128] = _norm_rope(p_ref[:, 512 + s * 128:640 + s * 128], gk_ref[...], bm,
                                                          cos_v, sin_v, first).astype(kr_ref.dtype)
        vb_ref[...] = p_ref[:, 768:1024].astype(vb_ref.dtype)
        gu_ref[...] = _gelu(p_ref[:, 1024:1536])
        gv = _gelu(p_ref[:, 1536:2048])
        gvo_ref[...] = (gv * _rms(gv) * gvn_ref[...]).astype(gvo_ref.dtype)

    row = lambda w: BS((tm, w), lambda i: (i, 0))
    const = lambda r, w: BS((r, w), lambda i: (0, 0))
    return _pcall(body, name="mixer_pre", grid=(S // tm,),
                  in_specs=[row(IN_COLS_DUP), row(128), row(128), const(1, 128), const(1, 128), const(1, 512),
                            const(128, 128)],
                  out_specs=[row(512), row(256), row(256), row(512), row(512)],
                  out_shape=[SDS((S, 512), MXU_DTYPE), SDS((S, 256), MXU_DTYPE), SDS((S, 256), MXU_DTYPE),
                             SDS((S, 512), F32), SDS((S, 512), MXU_DTYPE)])(proj, cos, sin, gq, gk, gvn, bmat)


def _swa_probs(qs, kd, sink, n, lo):
    z = jnp.zeros_like(qs)
    qp = jnp.concatenate([jnp.where(lo, qs, z), jnp.where(lo, z, qs)], axis=0)
    sc = _dot(qp, kd, NT) * (1.0 / math.sqrt(HEAD_DIM))
    r_i = lax.broadcasted_iota(jnp.int32, (2 * BLK, 2 * BLK), 0)
    k_j = lax.broadcasted_iota(jnp.int32, (2 * BLK, 2 * BLK), 1)
    diff = (r_i & (BLK - 1)) + BLK - k_j
    mask = (diff >= 0) & (diff < BLK) & ((k_j >= BLK) | (n > 0))
    sc = jnp.where(mask, sc, MINF)
    m = jnp.maximum(jnp.max(sc, axis=1, keepdims=True), sink)
    p = jnp.exp(sc - m)
    es = jnp.exp(sink - m)
    l = jnp.sum(p, axis=1, keepdims=True) + es
    return qp, p / l, es / l


def swa_fwd(qr, kr, vb, sinkcol, gao):
    S = qr.shape[0]
    nb = S // BLK

    def body(q_ref, kc_ref, kp_ref, vc_ref, vp_ref, sk_ref, g_ref, o_ref, ya_ref):
        n = pl.program_id(0)
        lo = _lane((BLK, 128)) < 64
        for s in range(4):
            h = s // 2
            hs = slice(h * 128, (h + 1) * 128)
            kd = jnp.concatenate([kp_ref[:, hs], kc_ref[:, hs]], axis=0)
            vd = jnp.concatenate([vp_ref[:, hs], vc_ref[:, hs]], axis=0)
            _, p, _ = _swa_probs(q_ref[:, s * 128:(s + 1) * 128], kd, sk_ref[s], n, lo)
            o2 = _dot(p, vd)
            o_ref[:, s * 128:(s + 1) * 128] = jnp.where(lo, o2[:BLK], o2[BLK:])
        a = o_ref[...]
        ya_ref[...] = (a * _rms(a) * g_ref[...]).astype(ya_ref.dtype)

    cur = lambda w: BS((BLK, w), lambda n: (n, 0))
    prev = lambda w: BS((BLK, w), lambda n: (jnp.maximum(n - 1, 0), 0))
    return _pcall(body, name="swa_fwd", grid=(nb,),
                  in_specs=[cur(512), cur(256), prev(256), cur(256), prev(256),
                            BS((4, 2 * BLK, 1), lambda n: (0, 0, 0)), BS((1, 512), lambda n: (0, 0))],
                  out_specs=[cur(512), cur(512)],
                  out_shape=[SDS((S, 512), F32), SDS((S, 512), MXU_DTYPE)])(qr, kr, kr, vb, vb, sinkcol, gao)


def gmlp_fwd(gvn, gu, ya, w2, bsl, ggo):
    S = gvn.shape[0]

    def body(gvn_ref, gu_ref, ya_ref, w2_ref, bsl_ref, g_ref, gm_ref, y_ref):
        lo = _lane((BLK, 128)) < 64
        for j in range(4):
            sl = slice(j * 128, (j + 1) * 128)
            m2 = _dot(w2_ref[j], gvn_ref[:, sl])
            mixed = jnp.where(lo, m2[:BLK], m2[BLK:]) + bsl_ref[j]
            gm_ref[:, sl] = gu_ref[:, sl] * mixed
        gm = gm_ref[...]
        y_ref[:, :512] = ya_ref[...]
        y_ref[:, 512:] = (gm * _rms(gm) * g_ref[...]).astype(y_ref.dtype)

    row = lambda w: BS((BLK, w), lambda n: (n, 0))
    return _pcall(body, name="gmlp_fwd", grid=(S // BLK,),
                  in_specs=[row(512), row(512), row(512), BS((4, 2 * BLK, BLK), lambda n: (0, 0, 0)),
                            BS((4, BLK, 128), lambda n: (0, 0, 0)), BS((1, 512), lambda n: (0, 0))],
                  out_specs=[row(512), row(1024)],
                  out_shape=[SDS((S, 512), F32), SDS((S, 1024), MXU_DTYPE)])(gvn, gu, ya, w2, bsl, ggo)


def mem_pre(kv, gxk):
    def body(kv_ref, g_ref, kn_ref, vb_ref):
        for h in range(XA_HEADS):
            sl = slice(h * XA_DH, (h + 1) * XA_DH)
            k = kv_ref[:, sl]
            kn_ref[:, sl] = (k * _rms(k) * g_ref[...]).astype(kn_ref.dtype)
        vb_ref[...] = kv_ref[:, 1024:2048].astype(vb_ref.dtype)

    full = lambda r, w: BS((r, w), lambda i: (0, 0))
    return _pcall(body, name="mem_pre", grid=(1,), in_specs=[full(MEM_LEN, 2048), full(1, XA_DH)],
                  out_specs=[full(MEM_LEN, 1024), full(MEM_LEN, 1024)],
                  out_shape=[SDS((MEM_LEN, 1024), MXU_DTYPE), SDS((MEM_LEN, 1024), MXU_DTYPE)])(kv, gxk)


def _xa_probs(qh, g, kn_h):
    r = _rms(qh)
    qn = qh * r * g
    s = _dot(qn, kn_h, NT) * (1.0 / math.sqrt(XA_DH))
    p = jnp.exp(s - jnp.max(s, axis=1, keepdims=True))
    return r, qn, p / jnp.sum(p, axis=1, keepdims=True)


def xattn_fwd(qx, kn, vb, gxq):
    S = qx.shape[0]
    tm = _tile(S, (256,))

    def body(q_ref, kn_ref, vb_ref, g_ref, o_ref):
        for h in range(XA_HEADS):
            sl = slice(h * XA_DH, (h + 1) * XA_DH)
            _, _, p = _xa_probs(q_ref[:, sl], g_ref[...], kn_ref[:, sl])
            o_ref[:, sl] = _dot(p, vb_ref[:, sl]).astype(o_ref.dtype)

    full = lambda r, w: BS((r, w), lambda i: (0, 0))
    return _pcall(body, name="xattn_fwd", grid=(S // tm,),
                  in_specs=[BS((tm, 1024), lambda i: (i, 0)), full(MEM_LEN, 1024), full(MEM_LEN, 1024), full(1, XA_DH)],
                  out_specs=BS((tm, 1024), lambda i: (i, 0)), out_shape=SDS((S, 1024), MXU_DTYPE))(qx, kn, vb, gxq)


def _causal_taps(a, halo_ref, first_tile, row):
    h6 = jnp.where(first_tile, 0.0, halo_ref[6:7, :])
    h7 = jnp.where(first_tile, 0.0, halo_ref[7:8, :])
    a1 = jnp.where(row == 0, h7, pltpu.roll(a, 1, 0))
    a2 = jnp.where(row == 0, h6, jnp.where(row == 1, h7, pltpu.roll(a, 2, 0)))
    return a1, a2


def _conv(a, a1, a2, w_ref, b_ref):
    return w_ref[2:3, :] * a + w_ref[1:2, :] * a1 + w_ref[0:1, :] * a2 + b_ref[...]


def _conv_specs(tm):
    halo_blocks = tm // 8
    return [BS((tm, D_FF), lambda i: (i, 0)), BS((tm, D_FF), lambda i: (i, 1)),
            BS((8, D_FF), lambda i: (jnp.maximum(i * halo_blocks - 1, 0), 0)),
            BS((8, D_FF), lambda i: (jnp.maximum(i * halo_blocks - 1, 0), 1)),
            BS((3, D_FF), lambda i: (0, 0)), BS((3, D_FF), lambda i: (0, 1)),
            BS((1, D_FF), lambda i: (0, 0)), BS((1, D_FF), lambda i: (0, 1))]


def convgate_fwd(a, cw, cb):
    S = a.shape[0]
    tm = _tile(S, (256,))

    def body(ag_ref, au_ref, hg_ref, hu_ref, wg_ref, wu_ref, bg_ref, bu_ref, f_ref):
        first_tile = pl.program_id(0) == 0
        row = lax.broadcasted_iota(jnp.int32, (tm, D_FF), 0)
        ag, au = ag_ref[...], au_ref[...]
        cg = _conv(ag, *_causal_taps(ag, hg_ref, first_tile, row), wg_ref, bg_ref)
        cu = _conv(au, *_causal_taps(au, hu_ref, first_tile, row), wu_ref, bu_ref)
        f_ref[...] = (_gelu(cg) * cu).astype(f_ref.dtype)

    return _pcall(body, name="convgate_fwd", grid=(S // tm,), in_specs=_conv_specs(tm),
                  out_specs=BS((tm, D_FF), lambda i: (i, 0)),
                  out_shape=SDS((S, D_FF), MXU_DTYPE))(a, a, a, a, cw, cw, cb, cb)


def convgate_bwd(a, df, cw, cb, after=None):
    S = a.shape[0]
    tm = _tile(S, (128,))

    def body(ag_ref, au_ref, hg_ref, hu_ref, wg_ref, wu_ref, bg_ref, bu_ref, df_ref, dc_ref, gw_ref):
        first_tile = pl.program_id(0) == 0

        @pl.when(first_tile)
        def _():
            gw_ref[...] = jnp.zeros_like(gw_ref)

        row = lax.broadcasted_iota(jnp.int32, (tm, D_FF), 0)
        ag, au, df_v = ag_ref[...], au_ref[...], df_ref[...]
        ag1, ag2 = _causal_taps(ag, hg_ref, first_tile, row)
        au1, au2 = _causal_taps(au, hu_ref, first_tile, row)
        cg = _conv(ag, ag1, ag2, wg_ref, bg_ref)
        cu = _conv(au, au1, au2, wu_ref, bu_ref)
        dcg = df_v * cu * _gelu_grad(cg)
        dcu = df_v * _gelu(cg)
        dc_ref[:, :D_FF] = dcg
        dc_ref[:, D_FF:] = dcu
        for col, dcv, taps in ((slice(0, D_FF), dcg, (ag2, ag1, ag)), (slice(D_FF, 2 * D_FF), dcu, (au2, au1, au))):
            for j in range(3):
                gw_ref[j:j + 1, col] += jnp.sum(dcv * taps[j], axis=0, keepdims=True)
            gw_ref[3:4, col] += jnp.sum(dcv, axis=0, keepdims=True)

    return _pcall(body, name="convgate_bwd", grid=(S // tm,), after=after,
                  in_specs=_conv_specs(tm) + [BS((tm, D_FF), lambda i: (i, 0))],
                  out_specs=[BS((tm, 2 * D_FF), lambda i: (i, 0)), BS((8, 2 * D_FF), lambda i: (0, 0))],
                  out_shape=[SDS((S, 2 * D_FF), F32), SDS((8, 2 * D_FF), F32)])(a, a, a, a, cw, cw, cb, cb, df)


def conv_transpose(dc, cw):
    S, C = dc.shape
    tm = _tile(S, (128,))
    nt = S // tm
    halo_blocks = tm // 8

    def body(dc_ref, halo_ref, w_ref, da_ref):
        last_tile = pl.program_id(0) == nt - 1
        row = lax.broadcasted_iota(jnp.int32, (tm, C), 0)
        h0 = jnp.where(last_tile, 0.0, halo_ref[0:1, :])
        h1 = jnp.where(last_tile, 0.0, halo_ref[1:2, :])
        dc_v = dc_ref[...]
        n1 = jnp.where(row == tm - 1, h0, pltpu.roll(dc_v, tm - 1, 0))
        n2 = jnp.where(row == tm - 1, h1, jnp.where(row == tm - 2, h0, pltpu.roll(dc_v, tm - 2, 0)))
        da_ref[...] = (w_ref[2:3, :] * dc_v + w_ref[1:2, :] * n1 + w_ref[0:1, :] * n2).astype(da_ref.dtype)

    return _pcall(body, name="conv_transpose", grid=(nt,),
                  in_specs=[BS((tm, C), lambda i: (i, 0)),
                            BS((8, C), lambda i: (jnp.minimum((i + 1) * halo_blocks, S // 8 - 1), 0)),
                            BS((3, C), lambda i: (0, 0))],
                  out_specs=BS((tm, C), lambda i: (i, 0)), out_shape=SDS((S, C), MXU_DTYPE))(dc, dc, cw)


def xattn_bwd(qx, dxo, kn, vb, gxq, after=None):
    S = qx.shape[0]
    tm = _tile(S, (256,))

    def body(q_ref, do_ref, kn_ref, vb_ref, g_ref, dq_ref, dkn_ref, dv_ref, dg_ref):
        @pl.when(pl.program_id(0) == 0)
        def _():
            dkn_ref[...] = jnp.zeros_like(dkn_ref)
            dv_ref[...] = jnp.zeros_like(dv_ref)
            dg_ref[...] = jnp.zeros_like(dg_ref)

        g = g_ref[...]
        for h in range(XA_HEADS):
            sl = slice(h * XA_DH, (h + 1) * XA_DH)
            qh, do = q_ref[:, sl], do_ref[:, sl]
            r, qn, p = _xa_probs(qh, g, kn_ref[:, sl])
            dp = _dot(do, vb_ref[:, sl], NT)
            ds = p * (dp - jnp.sum(dp * p, axis=1, keepdims=True)) * (1.0 / math.sqrt(XA_DH))
            dqn = _dot(ds, kn_ref[:, sl])
            dkn_ref[:, sl] += _dot(ds, qn, TN)
            dv_ref[:, sl] += _dot(p, do, TN)
            dqh, dgc = _rms_bwd(dqn, qh, g, r)
            dq_ref[:, sl] = dqh.astype(dq_ref.dtype)
            _acc_rows(dg_ref, 0, dgc)

    row = BS((tm, 1024), lambda i: (i, 0))
    full = lambda r, w: BS((r, w), lambda i: (0, 0))
    return _pcall(body, name="xattn_bwd", grid=(S // tm,), after=after,
                  in_specs=[row, row, full(MEM_LEN, 1024), full(MEM_LEN, 1024), full(1, XA_DH)],
                  out_specs=[row, full(MEM_LEN, 1024), full(MEM_LEN, 1024), full(8, XA_DH)],
                  out_shape=[SDS((S, 1024), MXU_DTYPE), SDS((MEM_LEN, 1024), F32), SDS((MEM_LEN, 1024), F32),
                             SDS((8, XA_DH), F32)])(qx, dxo, kn, vb, gxq)


def mem_bwd(kv, dkn, dvb, gxk, after=None):
    def body(kv_ref, dkn_ref, dv_ref, g_ref, dkv_ref, dg_ref):
        dg_ref[...] = jnp.zeros_like(dg_ref)
        for h in range(XA_HEADS):
            sl = slice(h * XA_DH, (h + 1) * XA_DH)
            k = kv_ref[:, sl]
            dk, dgc = _rms_bwd(dkn_ref[:, sl], k, g_ref[...], _rms(k))
            dkv_ref[:, sl] = dk.astype(dkv_ref.dtype)
            _acc_rows(dg_ref, 0, dgc)
        dkv_ref[:, 1024:2048] = dv_ref[...].astype(dkv_ref.dtype)

    full = lambda r, w: BS((r, w), lambda i: (0, 0))
    return _pcall(body, name="mem_bwd", grid=(1,), after=after,
                  in_specs=[full(MEM_LEN, 2048), full(MEM_LEN, 1024), full(MEM_LEN, 1024), full(1, XA_DH)],
                  out_specs=[full(MEM_LEN, 2048), full(8, XA_DH)],
                  out_shape=[SDS((MEM_LEN, 2048), MXU_DTYPE), SDS((8, XA_DH), F32)])(kv, dkn, dvb, gxk)


def gmlp_bwd(dgm, gvn, gu, w2, w2t, bsl, after=None):
    S = dgm.shape[0]

    def body(dgm_ref, gvn_ref, gu_ref, w2_ref, w2t_ref, bsl_ref, dgu_ref, dgvn_ref, dws_ref, dbl_ref):
        @pl.when(pl.program_id(0) == 0)
        def _():
            dws_ref[...] = jnp.zeros_like(dws_ref)
            dbl_ref[...] = jnp.zeros_like(dbl_ref)

        lo = _lane((BLK, 128)) < 64
        for j in range(4):
            sl = slice(j * 128, (j + 1) * 128)
            gvn_s = gvn_ref[:, sl]
            m2 = _dot(w2_ref[j], gvn_s)
            mixed = jnp.where(lo, m2[:BLK], m2[BLK:]) + bsl_ref[j]
            dgm_s = dgm_ref[:, sl]
            dgu_ref[:, sl] = dgm_s * mixed
            dmx = dgm_s * gu_ref[:, sl]
            d2 = _dot(w2t_ref[j], dmx)
            dgvn_ref[:, sl] = jnp.where(lo, d2[:BLK], d2[BLK:])
            z = jnp.zeros_like(dmx)
            dws_ref[2 * j] += _dot(jnp.where(lo, dmx, z), gvn_s, NT)
            dws_ref[2 * j + 1] += _dot(jnp.where(lo, z, dmx), gvn_s, NT)
            dbl_ref[j] += dmx

    row = lambda w: BS((BLK, w), lambda n: (n, 0))
    const3 = lambda a, b, c: BS((a, b, c), lambda n: (0, 0, 0))
    return _pcall(body, name="gmlp_bwd", grid=(S // BLK,), after=after,
                  in_specs=[row(512), row(512), row(512), const3(4, 2 * BLK, BLK), const3(4, 2 * BLK, BLK),
                            const3(4, BLK, 128)],
                  out_specs=[row(512), row(512), const3(8, BLK, BLK), const3(4, BLK, 128)],
                  out_shape=[SDS((S, 512), F32), SDS((S, 512), F32), SDS((8, BLK, BLK), F32),
                             SDS((4, BLK, 128), F32)])(dgm, gvn, gu, w2, w2t, bsl)


def swa_bwd(qr, kr, vb, sinkcol, dattn):
    S = qr.shape[0]
    nb = S // BLK

    def body(q_ref, kc_ref, kp_ref, vc_ref, vp_ref, sk_ref, do_ref, dq_ref, dk_ref, dv_ref, dsk_ref,
             carry_k, carry_v, prev_k, prev_v):
        n = pl.program_id(0)

        @pl.when(n == 0)
        def _():
            dsk_ref[...] = jnp.zeros_like(dsk_ref)
            carry_k[...] = jnp.zeros_like(carry_k)
            carry_v[...] = jnp.zeros_like(carry_v)

        @pl.when(n < nb)
        def _():
            lo = _lane((BLK, 128)) < 64
            for h in range(2):
                hs = slice(h * 128, (h + 1) * 128)
                kd = jnp.concatenate([kp_ref[:, hs], kc_ref[:, hs]], axis=0)
                vd = jnp.concatenate([vp_ref[:, hs], vc_ref[:, hs]], axis=0)
                dkd = jnp.zeros((2 * BLK, 128), F32)
                dvd = jnp.zeros((2 * BLK, 128), F32)
                for s in (2 * h, 2 * h + 1):
                    sl = slice(s * 128, (s + 1) * 128)
                    qp, p, psink = _swa_probs(q_ref[:, sl], kd, sk_ref[s], n, lo)
                    do = do_ref[:, sl]
                    z = jnp.zeros_like(do)
                    dop = jnp.concatenate([jnp.where(lo, do, z), jnp.where(lo, z, do)], axis=0)
                    dp = _dot(dop, vd, NT)
                    delta = jnp.sum(dp * p, axis=1, keepdims=True)
                    ds = p * (dp - delta) * (1.0 / math.sqrt(HEAD_DIM))
                    dsk_ref[s] += -psink * delta
                    dq2 = _dot(ds, kd)
                    dq_ref[:, sl] = jnp.where(lo, dq2[:BLK], dq2[BLK:])
                    dkd = dkd + _dot(ds, qp, TN)
                    dvd = dvd + _dot(p, dop, TN)
                prev_k[:, hs] = carry_k[:, hs] + dkd[:BLK]
                prev_v[:, hs] = carry_v[:, hs] + dvd[:BLK]
                carry_k[:, hs] = dkd[BLK:]
                carry_v[:, hs] = dvd[BLK:]

        @pl.when(n == nb)
        def _():
            prev_k[...] = carry_k[...]
            prev_v[...] = carry_v[...]

        dk_ref[...] = prev_k[...]
        dv_ref[...] = prev_v[...]

    last = nb - 1
    cur = lambda w: BS((BLK, w), lambda n: (jnp.minimum(n, last), 0))
    prev = lambda w: BS((BLK, w), lambda n: (jnp.clip(n - 1, 0, last), 0))
    done = lambda w: BS((BLK, w), lambda n: (jnp.maximum(n - 1, 0), 0))
    return _pcall(body, name="swa_bwd", grid=(nb + 1,),
                  in_specs=[cur(512), cur(256), prev(256), cur(256), prev(256),
                            BS((4, 2 * BLK, 1), lambda n: (0, 0, 0)), cur(512)],
                  out_specs=[cur(512), done(256), done(256), BS((4, 2 * BLK, 1), lambda n: (0, 0, 0))],
                  out_shape=[SDS((S, 512), F32), SDS((S, 256), F32), SDS((S, 256), F32), SDS((4, 2 * BLK, 1), F32)],
                  scratch=[pltpu.VMEM((BLK, 256), F32)] * 4)(qr, kr, kr, vb, vb, sinkcol, dattn)


def mixer_pre_bwd(proj, cos, sin, gq, gk, gvn, bmat, dqr, dkr, dvb, dgu, dgvn):
    S = proj.shape[0]
    tm = _tile(S, (256,))

    def body(p_ref, c_ref, s_ref, gq_ref, gk_ref, gvn_ref, b_ref, dqr_ref, dkr_ref, dvb_ref, dgu_ref, dgvn_ref,
             dp_ref, dgq_ref, dgk_ref, dgv_ref):
        @pl.when(pl.program_id(0) == 0)
        def _():
            dgq_ref[...] = jnp.zeros_like(dgq_ref)
            dgk_ref[...] = jnp.zeros_like(dgk_ref)
            dgv_ref[...] = jnp.zeros_like(dgv_ref)

        cos_v, sin_v, bm = c_ref[...], s_ref[...], b_ref[...]
        first = (_lane((tm, 128)) & 63) < 32

        def slab_bwd(slab, dout, g, dg_ref):
            r = lax.rsqrt(_segsum(slab * slab, bm) * (1.0 / HEAD_DIM) + EPS)
            ds = dout * sin_v
            dqn = dout * cos_v + jnp.where(first, pltpu.roll(ds, 96, 1), pltpu.roll(ds, 32, 1))
            dyg = dqn * g
            dx = r * dyg - slab * (r * r * r) * (_segsum(dyg * slab, bm) * (1.0 / HEAD_DIM))
            _acc_rows(dg_ref, 0, dqn * slab * r)
            return dx

        for s in range(4):
            sl = slice(s * 128, (s + 1) * 128)
            dp_ref[:, sl] = slab_bwd(p_ref[:, sl], dqr_ref[:, sl], gq_ref[...], dgq_ref).astype(dp_ref.dtype)
        for s in range(2):
            sl = slice(512 + s * 128, 640 + s * 128)
            dp_ref[:, sl] = slab_bwd(p_ref[:, sl], dkr_ref[:, s * 128:(s + 1) * 128], gk_ref[...],
                                     dgk_ref).astype(dp_ref.dtype)
        dp_ref[:, 768:1024] = dvb_ref[...].astype(dp_ref.dtype)
        dp_ref[:, 1024:1536] = (dgu_ref[...] * _gelu_grad(p_ref[:, 1024:1536])).astype(dp_ref.dtype)
        gvp = p_ref[:, 1536:2048]
        gv = _gelu(gvp)
        dgv, dgc = _rms_bwd(dgvn_ref[...], gv, gvn_ref[...], _rms(gv))
        dp_ref[:, 1536:2048] = (dgv * _gelu_grad(gvp)).astype(dp_ref.dtype)
        _acc_rows(dgv_ref, 0, dgc)

    row = lambda w: BS((tm, w), lambda i: (i, 0))
    const = lambda r, w: BS((r, w), lambda i: (0, 0))
    return _pcall(body, name="mixer_pre_bwd", grid=(S // tm,),
                  in_specs=[row(IN_COLS_DUP), row(128), row(128), const(1, 128), const(1, 128), const(1, 512),
                            const(128, 128), row(512), row(256), row(256), row(512), row(512)],
                  out_specs=[row(IN_COLS_DUP), const(8, 128), const(8, 128), const(8, 512)],
                  out_shape=[SDS((S, IN_COLS_DUP), MXU_DTYPE), SDS((8, 128), F32), SDS((8, 128), F32),
                             SDS((8, 512), F32)])(proj, cos, sin, gq, gk, gvn, bmat, dqr, dkr, dvb, dgu, dgvn)


BIG = (("w_in", (1024, 448), True), ("w_out", (256, 1024), False), ("xa_wq", (256, 1024), False),
       ("xa_wkv", (1024, 512), True), ("xa_wo", (256, 1024), False), ("ffn_up", (1024, 1408), True),
       ("ffn_down", (704, 1024), False))
BIG_NAMES = tuple(n for n, _, _ in BIG)
SMALL_VECS = (("mix_norm", 1024), ("q_norm", 64), ("k_norm", 64), ("attn_sinks", 8), ("gmlp_v_norm", 512),
              ("attn_out_norm", 512), ("gmlp_out_norm", 512), ("xa_norm", 1024), ("mem_norm", 1024),
              ("xa_q_norm", 256), ("xa_k_norm", 256), ("ffn_norm", 1024), ("ffn_conv_b", 5632))
SMALL = tuple(n for n, _ in SMALL_VECS) + ("gmlp_bs", "gmlp_ws", "ffn_conv")
WEIGHTS = ("mix_norm", "w_in", "q_norm", "k_norm", "attn_sinks", "gmlp_v_norm", "gmlp_ws", "gmlp_bs",
           "attn_out_norm", "gmlp_out_norm", "w_out", "xa_norm", "mem_norm", "xa_wq", "xa_wkv", "xa_q_norm",
           "xa_k_norm", "xa_wo", "ffn_norm", "ffn_up", "ffn_conv", "ffn_conv_b", "ffn_down")
CONV_SHARD = (3, 1408)
CONV_LANE_ROWS = CONV_SHARD[1] // 128
CONV_CHIP_ROWS = 40


def _small_rows():
    rows, r = {}, 0
    for n, length in SMALL_VECS:
        rows[n] = r
        r += -(-length // 128)
    r += -r % 8
    rows["gmlp_bs"] = r
    r += 8
    rows["gmlp_ws"] = r
    r += 8 * BLK
    rows["ffn_conv"] = r
    r += N_CHIPS * CONV_CHIP_ROWS
    return rows, r


SMALL_ROW, SMALL_ROWS = _small_rows()


def pack_small(dg_mix, dgq, dgk, dsk, dg_gvn, dg_y, dg_xa, dg_mem, dg_xq, dg_xk, dg_ffn, gcw, dbl, dws):
    def body(mix_ref, q_ref, k_ref, sk_ref, gvn_ref, y_ref, xa_ref, mem_ref, xq_ref, xk_ref, ffn_ref, cw_ref,
             dbl_ref, dws_ref, o_ref):
        o_ref[...] = jnp.zeros_like(o_ref)
        lane = _lane((1, 128))

        def put(name, src_ref, row, lane0, length):
            for k in range(length // 128):
                o_ref[SMALL_ROW[name] + k:SMALL_ROW[name] + k + 1, :] = src_ref[row:row + 1, lane0 + k * 128:lane0 + (k + 1) * 128]

        put("mix_norm", mix_ref, 0, 0, 1024)
        for name, ref in (("q_norm", q_ref), ("k_norm", k_ref)):
            v = ref[0:1, :]
            o_ref[SMALL_ROW[name]:SMALL_ROW[name] + 1, :] = jnp.where(lane < HEAD_DIM, v + pltpu.roll(v, 64, 1), 0.0)
        sinks = jnp.zeros((1, 128), F32)
        for s in range(4):
            col = sk_ref[s]
            sinks = sinks + jnp.where(lane == 2 * s, jnp.sum(col[:BLK]), 0.0) + jnp.where(lane == 2 * s + 1, jnp.sum(col[BLK:]), 0.0)
        o_ref[SMALL_ROW["attn_sinks"]:SMALL_ROW["attn_sinks"] + 1, :] = sinks
        put("gmlp_v_norm", gvn_ref, 0, 0, 512)
        put("attn_out_norm", y_ref, 0, 0, 512)
        put("gmlp_out_norm", y_ref, 0, 512, 512)
        put("xa_norm", xa_ref, 0, 0, 1024)
        put("mem_norm", mem_ref, 0, 0, 1024)
        put("xa_q_norm", xq_ref, 0, 0, 256)
        put("xa_k_norm", xk_ref, 0, 0, 256)
        put("ffn_norm", ffn_ref, 0, 0, 1024)
        put("ffn_conv_b", cw_ref, 3, 0, 2 * D_FF)
        r8 = lax.broadcasted_iota(jnp.int32, (8, 128), 0)
        l8 = _lane((8, 128))
        bs = jnp.zeros((8, BLK), F32)
        for j in range(4):
            sel = (((r8 == 2 * j) & (l8 < 64)) | ((r8 == 2 * j + 1) & (l8 >= 64))).astype(F32).astype(BF16)
            xj = dbl_ref[j]
            hi = xj.astype(BF16)
            lo = (xj - hi.astype(F32)).astype(BF16)
            bs = bs + lax.dot_general(sel, hi, NT, preferred_element_type=F32) + lax.dot_general(sel, lo, NT, preferred_element_type=F32)
        o_ref[SMALL_ROW["gmlp_bs"]:SMALL_ROW["gmlp_bs"] + 8, :] = bs
        causal = lax.broadcasted_iota(jnp.int32, (BLK, BLK), 0) >= lax.broadcasted_iota(jnp.int32, (BLK, BLK), 1)
        for h in range(8):
            r0 = SMALL_ROW["gmlp_ws"] + h * BLK
            o_ref[r0:r0 + BLK, :] = jnp.where(causal, dws_ref[h], 0.0)
        for q in range(N_CHIPS):
            for j in range(3):
                for k in range(CONV_LANE_ROWS):
                    r0 = SMALL_ROW["ffn_conv"] + q * CONV_CHIP_ROWS + j * CONV_LANE_ROWS + k
                    l0 = (q * CONV_LANE_ROWS + k) * 128
                    o_ref[r0:r0 + 1, :] = cw_ref[j:j + 1, l0:l0 + 128]

    args = (dg_mix, dgq, dgk, dsk, dg_gvn, dg_y, dg_xa, dg_mem, dg_xq, dg_xk, dg_ffn, gcw, dbl, dws)
    full = lambda a: BS(a.shape, lambda i, nd=a.ndim: (0,) * nd)
    return _pcall(body, name="pack_small", grid=(1,), in_specs=[full(a) for a in args],
                  out_specs=BS((SMALL_ROWS, 128), lambda i: (0, 0)), out_shape=SDS((SMALL_ROWS, 128), F32))(*args)


def _adam(w, g, m, v):
    mn = ADAM_B1 * m + (1.0 - ADAM_B1) * g
    vn = ADAM_B2 * v + (1.0 - ADAM_B2) * (g * g)
    m_hat = mn / (1.0 - ADAM_B1 ** ADAM_STEP)
    v_hat = vn / (1.0 - ADAM_B2 ** ADAM_STEP)
    return -ADAM_LR * (m_hat / (jnp.sqrt(v_hat) + ADAM_EPS) + ADAM_WD * w), mn, vn


def adamw_small(gsum, w, m, v, chipvec):
    n = len(SMALL)

    def body(chip_ref, g_ref, *refs):
        w_refs, m_refs, v_refs = refs[:n], refs[n:2 * n], refs[2 * n:3 * n]
        outs = refs[3 * n:]
        go, do, mo, vo = outs[:n], outs[n:2 * n], outs[2 * n:3 * n], outs[3 * n:]

        def update(i, idx, g):
            d, mn, vn = _adam(w_refs[i][idx], g, m_refs[i][idx], v_refs[i][idx])
            go[i][idx] = g
            do[i][idx] = d
            mo[i][idx] = mn
            vo[i][idx] = vn

        for i, (name, length) in enumerate(SMALL_VECS):
            for k in range(-(-length // 128)):
                wd = min(128, length - k * 128)
                r = SMALL_ROW[name] + k
                update(i, (slice(0, 1), slice(k * 128, k * 128 + wd)), g_ref[r:r + 1, 0:wd])
        i_bs, i_ws, i_cv = len(SMALL_VECS), len(SMALL_VECS) + 1, len(SMALL_VECS) + 2
        update(i_bs, (0,), g_ref[SMALL_ROW["gmlp_bs"]:SMALL_ROW["gmlp_bs"] + 8, :])
        for h in range(8):
            r0 = SMALL_ROW["gmlp_ws"] + h * BLK
            update(i_ws, (0, h), g_ref[r0:r0 + BLK, :])
        mine = g_ref[pl.ds(pl.multiple_of(SMALL_ROW["ffn_conv"] + chip_ref[0] * CONV_CHIP_ROWS, 8), CONV_CHIP_ROWS), :]
        for j in range(3):
            for k in range(CONV_LANE_ROWS):
                r = j * CONV_LANE_ROWS + k
                update(i_cv, (0, slice(j, j + 1), slice(k * 128, (k + 1) * 128)), mine[r:r + 1, :])

    nat = [w[nm] for nm in SMALL]
    full = lambda a: BS(a.shape, lambda i, c, nd=a.ndim: (0,) * nd)
    outs = _pcall(body, name="adamw_small", grid=(1,), prefetch=1,
                  in_specs=[BS((SMALL_ROWS, 128), lambda i, c: (0, 0))] + [full(a) for a in nat] * 3,
                  out_specs=[full(a) for a in nat] * 4, out_shape=[SDS(a.shape, F32) for a in nat] * 4)(
        chipvec, gsum, *nat, *[m[nm] for nm in SMALL], *[v[nm] for nm in SMALL])
    return outs[:n], outs[n:2 * n], outs[2 * n:3 * n], outs[3 * n:]


def adamw_matrix(w, m, v, g_own, g_other, cvec, *, name):
    _, r, c = w.shape
    half = r // 2
    tr = _tile(half, (128, 176))
    T = half // tr

    def body(c_ref, w_ref, m_ref, v_ref, own_ref, oth_ref, g_ref, d_ref, mo_ref, vo_ref):
        g = jnp.where(pl.program_id(0) == c_ref[0], own_ref[...], oth_ref[...])
        d, mn, vn = _adam(w_ref[...], g, m_ref[...], v_ref[...])
        g_ref[...] = g
        d_ref[...] = d
        mo_ref[...] = mn
        vo_ref[...] = vn

    nat = BS((None, tr, c), lambda hf, t, cr: (0, hf * T + t, 0))
    hlf = BS((tr, c), lambda hf, t, cr: (t, 0))
    return _pcall(body, name=name, grid=(2, T), prefetch=1, in_specs=[nat, nat, nat, hlf, hlf], out_specs=[nat] * 4,
                  out_shape=[SDS(w.shape, F32)] * 4)(cvec, w, m, v, g_own, g_other)


def _place():
    return lax.axis_index("x"), lax.axis_index("y"), lax.axis_index("c")


def _other_chips(x, y):
    return [(1 - x, y), (x, 1 - y), (1 - x, 1 - y)]


def _rows_of_core(c, half):
    return pl.ds(pl.multiple_of(c * half, 16), half)


def _rcopy(src, dst, sems, k, to):
    return pltpu.make_async_remote_copy(src_ref=src, dst_ref=dst, send_sem=sems[0].at[k], recv_sem=sems[1].at[k],
                                        device_id=to, device_id_type=MESH)


def _comm_call(body, *, name, out_shape, n_in, n_sems, aliases=None):
    return pl.pallas_call(body, name=name, out_shape=out_shape, in_specs=[ANY] * n_in, out_specs=[ANY] * len(out_shape),
                          scratch_shapes=[pltpu.SemaphoreType.DMA((n_sems,)), pltpu.SemaphoreType.DMA((n_sems,))],
                          input_output_aliases=aliases or {},
                          compiler_params=pltpu.CompilerParams(has_side_effects=True))


def cast_shards(shards, conv, chipvec):
    n = len(shards)

    def body(chip_ref, *refs):
        for i_ref, o_ref in zip(refs[:n + 1], refs[n + 1:]):
            o_ref[...] = i_ref[...].astype(o_ref.dtype)

    in_specs = [BS((s.shape[0] // 4, s.shape[1]), lambda i, p: (i, 0)) for s in shards]
    in_specs.append(BS(conv.shape, lambda i, p: (0, 0)))
    out_specs = [BS((None, s.shape[0] // 4, s.shape[1]), lambda i, p: (p[0], i, 0)) for s in shards]
    out_specs.append(BS((None,) + conv.shape, lambda i, p: (p[0], 0, 0)))
    out_shape = [SDS((N_CHIPS,) + s.shape, MXU_DTYPE) for s in shards] + [SDS((N_CHIPS,) + conv.shape, F32)]
    return _pcall(body, name="cast_shards", grid=(4,), prefetch=1, in_specs=in_specs, out_specs=out_specs,
                  out_shape=out_shape)(chipvec, *shards, conv)


HBM = pl.BlockSpec(memory_space=pltpu.HBM)
SEM = pl.BlockSpec(memory_space=pltpu.SEMAPHORE)
DATAFLOW = pltpu.SideEffectType.DATAFLOW_SIDE_EFFECTING
VMEM_WHOLE = pl.BlockSpec(memory_space=pltpu.VMEM)
TOKEN = jax.ShapeDtypeStruct((8, 128), jnp.float32)


def _gather_copies(bufs, send_sems, recv_sems, outgoing):
    x, y, c = _place()
    p = 2 * x + y
    cps = []
    for i, o in enumerate(bufs):
        for j, (cx, cy) in enumerate(_other_chips(x, y)):
            slot = o.at[p] if outgoing else o.at[2 * cx + cy]
            cps.append(_rcopy(slot, slot, (send_sems, recv_sems), 3 * i + j, (cx, cy, c)))
    return cps


def gather_start(slots):
    n = len(slots)

    def body(*refs):
        send_sems, recv_sems, thru, token = refs[n], refs[n + 1], refs[n + 2:2 * n + 2], refs[2 * n + 2]
        for cp in _gather_copies(thru, send_sems, recv_sems, True):
            cp.start()
        token[...] = jnp.zeros_like(token)

    hbm = [pltpu.with_memory_space_constraint(s, pltpu.HBM) for s in slots]
    outs = pl.pallas_call(
        body, name="gather_start_%d" % n,
        out_shape=[pltpu.SemaphoreType.DMA((3 * n,)), pltpu.SemaphoreType.DMA((3 * n,))]
        + [pltpu.HBM(s.shape, s.dtype) for s in slots] + [TOKEN],
        in_specs=[HBM] * n, out_specs=[SEM, SEM] + [HBM] * n + [VMEM_WHOLE],
        input_output_aliases={i: 2 + i for i in range(n)},
        compiler_params=pltpu.CompilerParams(has_side_effects=DATAFLOW))(*hbm)
    return outs[0], outs[1], outs[2:2 + n], outs[2 + n]


def gather_wait(send_sems, recv_sems, bufs, after):
    n = len(bufs)

    def body(*refs):
        ins, send_ref, recv_ref = refs[:n], refs[n], refs[n + 1]
        for cp in _gather_copies(ins, send_ref, recv_ref, False):
            cp.wait_send()
            cp.wait_recv()

    return pl.pallas_call(
        body, name="gather_wait_%d" % n, out_shape=[pltpu.HBM(s.shape, s.dtype) for s in bufs],
        in_specs=[HBM] * n + [SEM, SEM, ANY], out_specs=[HBM] * n, input_output_aliases={i: i for i in range(n)},
        compiler_params=pltpu.CompilerParams(has_side_effects=DATAFLOW))(*bufs, send_sems, recv_sems, after)


def _peers(x, y, c):
    return [(1 - x if k & 4 else x, 1 - y if k & 2 else y, 1 - c if k & 1 else c) for k in range(1, N_DEV)]


def _partial_copies(g_ref, land_ref, send_sems, recv_sems, outgoing):
    x, y, c = _place()
    half = g_ref.shape[1] // 2
    cps = []
    for k, (px, py, pc) in enumerate(_peers(x, y, c)):
        src = g_ref.at[2 * px + py, _rows_of_core(pc, half)]
        dst = land_ref.at[4 * x + 2 * y + c] if outgoing else land_ref.at[4 * px + 2 * py + pc]
        cps.append(_rcopy(src, dst, (send_sems, recv_sems), k, (px, py, pc)))
    return cps


def partials_start(g, *, name):
    land = lax.empty((N_DEV, g.shape[1] // 2, g.shape[2]), g.dtype)

    def body(g_ref, land_ref, send_sems, recv_sems, g_thru, land_thru, token):
        for cp in _partial_copies(g_thru, land_thru, send_sems, recv_sems, True):
            cp.start()
        token[...] = jnp.zeros_like(token)

    return pl.pallas_call(
        body, name=name,
        out_shape=[pltpu.SemaphoreType.DMA((N_DEV - 1,)), pltpu.SemaphoreType.DMA((N_DEV - 1,)),
                   pltpu.HBM(g.shape, g.dtype), pltpu.HBM(land.shape, land.dtype), TOKEN],
        in_specs=[HBM, HBM], out_specs=[SEM, SEM, HBM, HBM, VMEM_WHOLE], input_output_aliases={0: 2, 1: 3},
        compiler_params=pltpu.CompilerParams(has_side_effects=DATAFLOW))(
        pltpu.with_memory_space_constraint(g, pltpu.HBM), pltpu.with_memory_space_constraint(land, pltpu.HBM))


def partials_wait(started, after):
    n = len(started)

    def body(*refs):
        for i in range(n):
            send_ref, recv_ref, g_ref, land_ref = refs[4 * i:4 * i + 4]
            for cp in _partial_copies(g_ref, land_ref, send_ref, recv_ref, False):
                cp.wait_send()
                cp.wait_recv()

    flat = [a for s in started for a in s]
    bufs = [a for s in started for a in s[2:]]
    outs = pl.pallas_call(
        body, name="partials_wait", out_shape=[pltpu.HBM(b.shape, b.dtype) for b in bufs],
        in_specs=[SEM, SEM, HBM, HBM] * n + [ANY], out_specs=[HBM] * (2 * n),
        input_output_aliases={4 * i + 2 + j: 2 * i + j for i in range(n) for j in range(2)},
        compiler_params=pltpu.CompilerParams(has_side_effects=DATAFLOW))(*flat, after)
    return [(outs[2 * i], outs[2 * i + 1]) for i in range(n)]


def sum_partials(pairs, order):
    n = len(pairs)

    def body(o_ref, *refs):
        j = pl.program_id(0)
        for g_ref, l_ref, f_ref in zip(refs[:n], refs[n:2 * n], refs[2 * n:]):
            @pl.when(j == 0)
            def _():
                f_ref[...] = g_ref[...].astype(F32)

            @pl.when(j > 0)
            def _():
                f_ref[...] += l_ref[...].astype(F32)

    g4 = [g.reshape(g.shape[0], 2, g.shape[1] // 2, g.shape[2]) for g, _ in pairs]
    lands = [l for _, l in pairs]
    return _pcall(body, name="sum_partials", grid=(N_DEV,), prefetch=1,
                  in_specs=[BS((None, None) + g.shape[2:], lambda j, o: (o[0], o[1], 0, 0)) for g in g4]
                  + [BS((None,) + l.shape[1:], lambda j, o: (o[jnp.maximum(j, 1) + 1], 0, 0)) for l in lands],
                  out_specs=[BS(l.shape[1:], lambda j, o: (0, 0)) for l in lands],
                  out_shape=[SDS(l.shape[1:], F32) for l in lands])(order, *g4, *lands)


def pair_share(fs):
    n = len(fs)

    def body(*refs):
        f_refs, o_refs, sems = refs[:n], refs[n:2 * n], refs[2 * n:]
        x, y, c = _place()
        cps = [_rcopy(f, o, sems, i, (x, y, 1 - c)) for i, (f, o) in enumerate(zip(f_refs, o_refs))]
        for cp in cps:
            cp.start()
        for cp in cps:
            cp.wait()

    return _comm_call(body, name="pair_share", n_in=n, n_sems=n, out_shape=[SDS(f.shape, f.dtype) for f in fs])(*fs)


def _small_copies(s_ref, land_ref, send_sems, recv_sems, outgoing):
    x, y, c = _place()
    cps = []
    for k, (px, py, pc) in enumerate(_peers(x, y, c)):
        dst = land_ref.at[4 * x + 2 * y + c] if outgoing else land_ref.at[4 * px + 2 * py + pc]
        cps.append(_rcopy(s_ref, dst, (send_sems, recv_sems), k, (px, py, pc)))
    return cps


def small_start(sm):
    land = lax.empty((N_DEV,) + sm.shape, sm.dtype)

    def body(s_ref, land_ref, send_sems, recv_sems, s_thru, land_thru):
        for cp in _small_copies(s_thru, land_thru, send_sems, recv_sems, True):
            cp.start()

    return pl.pallas_call(
        body, name="small_start",
        out_shape=[pltpu.SemaphoreType.DMA((N_DEV - 1,)), pltpu.SemaphoreType.DMA((N_DEV - 1,)),
                   pltpu.HBM(sm.shape, sm.dtype), pltpu.HBM(land.shape, land.dtype)],
        in_specs=[HBM, HBM], out_specs=[SEM, SEM, HBM, HBM], input_output_aliases={0: 2, 1: 3},
        compiler_params=pltpu.CompilerParams(has_side_effects=DATAFLOW))(
        pltpu.with_memory_space_constraint(sm, pltpu.HBM), pltpu.with_memory_space_constraint(land, pltpu.HBM))


def small_wait(send_sems, recv_sems, sm, land, after):
    def body(send_ref, recv_ref, s_ref, land_ref, after_ref, s_out, land_out):
        for cp in _small_copies(s_ref, land_ref, send_ref, recv_ref, False):
            cp.wait_send()
            cp.wait_recv()

    return pl.pallas_call(
        body, name="small_wait", out_shape=[pltpu.HBM(sm.shape, sm.dtype), pltpu.HBM(land.shape, land.dtype)],
        in_specs=[SEM, SEM, HBM, HBM, ANY], out_specs=[HBM, HBM], input_output_aliases={2: 0, 3: 1},
        compiler_params=pltpu.CompilerParams(has_side_effects=DATAFLOW))(send_sems, recv_sems, sm, land, after)


def sum_small(own, land, mevec):
    n, rows, width = land.shape
    tr = _tile(rows, (184, 8))

    def body(me_ref, own_ref, land_ref, o_ref):
        acc = jnp.zeros((tr, width), F32)
        for s in range(n):
            acc = acc + jnp.where(me_ref[0] == s, own_ref[...], land_ref[s])
        o_ref[...] = acc

    return _pcall(body, name="sum_small", grid=(rows // tr,), prefetch=1,
                  in_specs=[BS((tr, width), lambda i, me: (i, 0)), BS((n, tr, width), lambda i, me: (0, i, 0))],
                  out_specs=BS((tr, width), lambda i, me: (i, 0)), out_shape=SDS((rows, width), F32))(mevec, own, land)


def _to_full(blk, col):
    n, r, c = blk.shape
    return blk.transpose(1, 0, 2).reshape(r, n * c) if col else blk.reshape(n * r, c)


def _dup_cols(w):
    dup = lambda t: jnp.concatenate([t[:, :64], t[:, :64], t[:, 64:], t[:, 64:]], axis=1)
    return jnp.concatenate([w[:, :512], dup(w[:, 512:640]), dup(w[:, 640:768]), w[:, 768:]], axis=1)


def _fold_cols(d):
    fold = lambda t: jnp.concatenate([t[:, 0:64] + t[:, 64:128], t[:, 128:192] + t[:, 192:256]], axis=1)
    return jnp.concatenate([d[:, :512], fold(d[:, 512:768]), fold(d[:, 768:1024]), d[:, 1024:]], axis=1)


def _local_step(x, mem, positions, target, w_in, later, sp, emit):
    gain = lambda n: sp[n].reshape(1, -1)
    half = HEAD_DIM // 2
    inv_freq = 1.0 / (10000.0 ** (jnp.arange(half, dtype=F32) * (2.0 / HEAD_DIM)))
    ang = positions.astype(F32)[:, None] * inv_freq
    cos, sin = jnp.cos(ang), jnp.sin(ang)
    cos128 = jnp.tile(cos, (1, 4))
    sin128 = jnp.concatenate([-sin, sin, -sin, sin], axis=1)
    seg = jnp.arange(128) // HEAD_DIM
    bmat = (seg[:, None] == seg[None, :]).astype(BF16)
    gq128, gk128 = jnp.tile(gain("q_norm"), (1, 2)), jnp.tile(gain("k_norm"), (1, 2))
    sinkcol = jnp.repeat(sp["attn_sinks"].reshape(4, 2), BLK, axis=1).reshape(4, 2 * BLK, 1)
    wsc = sp["gmlp_ws"] * jnp.tril(jnp.ones((BLK, BLK), F32))[None]
    w2 = wsc.reshape(4, 2 * BLK, BLK).astype(MXU_DTYPE)
    w2t = wsc.swapaxes(1, 2).reshape(4, 2 * BLK, BLK).astype(MXU_DTYPE)
    bsl = jnp.repeat(sp["gmlp_bs"].reshape(4, 2, BLK).transpose(0, 2, 1), HEAD_DIM, axis=2)
    cb = sp["ffn_conv_b"].reshape(1, -1)
    w_in_d = _dup_cols(_to_full(w_in, True))[None]

    h1, proj = rms_mm(x, gain("mix_norm"), w_in_d, name="mix_in")
    qr, kr, vb, gu, gvn = mixer_pre(proj, cos128, sin128, gq128, gk128, gain("gmlp_v_norm"), bmat)
    attn, ya = swa_fwd(qr, kr, vb, sinkcol, gain("attn_out_norm"))
    gm, y = gmlp_fwd(gvn, gu, ya, w2, bsl, gain("gmlp_out_norm"))
    wf, cw = later(y)
    w_out, xa_wq, xa_wo, ffn_down = (_to_full(wf[n], False) for n in ("w_out", "xa_wq", "xa_wo", "ffn_down"))
    x1 = mm(y, w_out, res=x, name="mix_out")
    h2, qx = rms_mm(x1, gain("xa_norm"), xa_wq[None], name="xa_q")
    mn, kv = rms_mm(mem, gain("mem_norm"), wf["xa_wkv"], name="xa_kv")
    kn, vbx = mem_pre(kv, gain("xa_k_norm"))
    xo = xattn_fwd(qx, kn, vbx, gain("xa_q_norm"))
    x2 = mm(xo, xa_wo, res=x1, name="xa_out")
    h3, a = rms_mm(x2, gain("ffn_norm"), wf["ffn_up"], name="ffn_up", tm=1024)
    f = convgate_fwd(a, cw, cb)
    dx3, loss_acc = mm_loss(f, ffn_down, x2, target, name="ffn_down_loss")

    by_rows = lambda g: g.reshape(N_CHIPS, g.shape[1] // N_CHIPS, g.shape[2])
    df = mm_nt(dx3, ffn_down[None], name="d_f")
    sent = emit("ffn_down", by_rows(mm_tn(f, dx3, name="g_ffn_down", out_dtype=WIRE_DTYPE)))
    dc, gcw = convgate_bwd(a, df, cw, cb, after=sent)
    da = conv_transpose(dc, cw)
    dx2, dg_ffn = mm_nt_rms_bwd(da, wf["ffn_up"], x2, gain("ffn_norm"), dx3, name="d_x2", tm=256)
    sent = emit("ffn_up", mm_tn(h3, da, name="g_ffn_up", out_dtype=WIRE_DTYPE, chunks=N_CHIPS))
    dxo = mm_nt(dx2, xa_wo[None], name="d_xo", after=sent)
    sent = emit("xa_wo", by_rows(mm_tn(xo, dx2, name="g_xa_wo", out_dtype=WIRE_DTYPE)))
    dqx, dkn, dvx, dg_xq = xattn_bwd(qx, dxo, kn, vbx, gain("xa_q_norm"), after=sent)
    dx1, dg_xa = mm_nt_rms_bwd(dqx, xa_wq[None], x1, gain("xa_norm"), dx2, name="d_x1")
    sent = emit("xa_wq", by_rows(mm_tn(h2, dqx, name="g_xa_wq", out_dtype=WIRE_DTYPE)))
    dkv, dg_xk = mem_bwd(kv, dkn, dvx, gain("xa_k_norm"), after=sent)
    _, dg_mem = mm_nt_rms_bwd(dkv, wf["xa_wkv"], mem, gain("mem_norm"), jnp.zeros_like(mem), name="d_mem")
    sent = emit("xa_wkv", mm_tn(mn, dkv, name="g_xa_wkv", out_dtype=WIRE_DTYPE, chunks=N_CHIPS))
    dattn, dgm, dg_y = mm_nt_post_bwd(dx1, w_out[None], attn, gm, gain("attn_out_norm"), gain("gmlp_out_norm"),
                                      name="d_mix_out", after=sent)
    sent = emit("w_out", by_rows(mm_tn(y, dx1, name="g_w_out", out_dtype=WIRE_DTYPE)))
    dgu, dgvn, dws, dbl = gmlp_bwd(dgm, gvn, gu, w2, w2t, bsl, after=sent)
    dqr, dkr, dvb, dsk = swa_bwd(qr, kr, vb, sinkcol, dattn)
    dproj, dgq, dgk, dg_gvn = mixer_pre_bwd(proj, cos128, sin128, gq128, gk128, gain("gmlp_v_norm"), bmat,
                                            dqr, dkr, dvb, dgu, dgvn)
    g_in = _fold_cols(mm_tn(h1, dproj, name="g_w_in", out_dtype=F32)[0])
    sent = emit("w_in", g_in.reshape(1024, N_CHIPS, 448).transpose(1, 0, 2).astype(WIRE_DTYPE))
    grad_x, dg_mix = mm_nt_rms_bwd(dproj, w_in_d, x, gain("mix_norm"), dx1, name="d_x", after=sent)
    packed = pack_small(dg_mix, dgq, dgk, dsk, dg_gvn, dg_y, dg_xa, dg_mem, dg_xq, dg_xk, dg_ffn, gcw, dbl, dws)
    return loss_acc, grad_x, packed


def _gather_step(w, chipvec):
    slots = cast_shards([w[n][0] for n in BIG_NAMES], w["ffn_conv"][0], chipvec)
    send_a, recv_a, first, _ = gather_start(slots[:1])
    send_b, recv_b, rest, rest_started = gather_start(slots[1:])
    w_in, = gather_wait(send_a, recv_a, first, rest_started)

    def later(after):
        got = gather_wait(send_b, recv_b, rest, after)
        return dict(zip(BIG_NAMES[1:], got[:-1])), _to_full(got[-1], True)

    return w_in, later


def _reduce_update(started, packed, w, m, v, chipvec, cvec, order):
    small_sent = small_start(packed)
    own = sum_partials(partials_wait([started[n] for n in BIG_NAMES], small_sent[2]), order)
    other = pair_share(own)
    res = [{}, {}, {}, {}]
    for n, g_own, g_other in zip(BIG_NAMES, own, other):
        for d, o in zip(res, adamw_matrix(w[n], m[n], v[n], g_own, g_other, cvec, name="adamw_" + n)):
            d[n] = o
    mevec = (2 * order[0:1] + order[1:2]).astype(jnp.int32)
    small_sum = sum_small(*small_wait(*small_sent, res[3][BIG_NAMES[-1]]), mevec)
    for d, outs in zip(res, adamw_small(small_sum, w, m, v, chipvec)):
        d.update(zip(SMALL, outs))
    return res


def kernel(x, mem, positions, mix_norm, w_in, q_norm, k_norm, attn_sinks, gmlp_v_norm, gmlp_ws, gmlp_bs, attn_out_norm, gmlp_out_norm, w_out, xa_norm, mem_norm, xa_wq, xa_wkv, xa_q_norm, xa_k_norm, xa_wo, ffn_norm, ffn_up, ffn_conv, ffn_conv_b, ffn_down, loss_target, m_mix_norm, m_w_in, m_q_norm, m_k_norm, m_attn_sinks, m_gmlp_v_norm, m_gmlp_ws, m_gmlp_bs, m_attn_out_norm, m_gmlp_out_norm, m_w_out, m_xa_norm, m_mem_norm, m_xa_wq, m_xa_wkv, m_xa_q_norm, m_xa_k_norm, m_xa_wo, m_ffn_norm, m_ffn_up, m_ffn_conv, m_ffn_conv_b, m_ffn_down, v_mix_norm, v_w_in, v_q_norm, v_k_norm, v_attn_sinks, v_gmlp_v_norm, v_gmlp_ws, v_gmlp_bs, v_attn_out_norm, v_gmlp_out_norm, v_w_out, v_xa_norm, v_mem_norm, v_xa_wq, v_xa_wkv, v_xa_q_norm, v_xa_k_norm, v_xa_wo, v_ffn_norm, v_ffn_up, v_ffn_conv, v_ffn_conv_b, v_ffn_down):
    w = dict(mix_norm=mix_norm, w_in=w_in, q_norm=q_norm, k_norm=k_norm, attn_sinks=attn_sinks, gmlp_v_norm=gmlp_v_norm, gmlp_ws=gmlp_ws, gmlp_bs=gmlp_bs, attn_out_norm=attn_out_norm, gmlp_out_norm=gmlp_out_norm, w_out=w_out, xa_norm=xa_norm, mem_norm=mem_norm, xa_wq=xa_wq, xa_wkv=xa_wkv, xa_q_norm=xa_q_norm, xa_k_norm=xa_k_norm, xa_wo=xa_wo, ffn_norm=ffn_norm, ffn_up=ffn_up, ffn_conv=ffn_conv, ffn_conv_b=ffn_conv_b, ffn_down=ffn_down)
    m = dict(mix_norm=m_mix_norm, w_in=m_w_in, q_norm=m_q_norm, k_norm=m_k_norm, attn_sinks=m_attn_sinks, gmlp_v_norm=m_gmlp_v_norm, gmlp_ws=m_gmlp_ws, gmlp_bs=m_gmlp_bs, attn_out_norm=m_attn_out_norm, gmlp_out_norm=m_gmlp_out_norm, w_out=m_w_out, xa_norm=m_xa_norm, mem_norm=m_mem_norm, xa_wq=m_xa_wq, xa_wkv=m_xa_wkv, xa_q_norm=m_xa_q_norm, xa_k_norm=m_xa_k_norm, xa_wo=m_xa_wo, ffn_norm=m_ffn_norm, ffn_up=m_ffn_up, ffn_conv=m_ffn_conv, ffn_conv_b=m_ffn_conv_b, ffn_down=m_ffn_down)
    v = dict(mix_norm=v_mix_norm, w_in=v_w_in, q_norm=v_q_norm, k_norm=v_k_norm, attn_sinks=v_attn_sinks, gmlp_v_norm=v_gmlp_v_norm, gmlp_ws=v_gmlp_ws, gmlp_bs=v_gmlp_bs, attn_out_norm=v_attn_out_norm, gmlp_out_norm=v_gmlp_out_norm, w_out=v_w_out, xa_norm=v_xa_norm, mem_norm=v_mem_norm, xa_wq=v_xa_wq, xa_wkv=v_xa_wkv, xa_q_norm=v_xa_q_norm, xa_k_norm=v_xa_k_norm, xa_wo=v_xa_wo, ffn_norm=v_ffn_norm, ffn_up=v_ffn_up, ffn_conv=v_ffn_conv, ffn_conv_b=v_ffn_conv_b, ffn_down=v_ffn_down)
    ix, iy, ic = lax.axis_index("x"), lax.axis_index("y"), lax.axis_index("c")
    chip = 2 * ix + iy
    chipvec = chip.astype(jnp.int32).reshape(1)
    cvec = ic.astype(jnp.int32).reshape(1)
    order = jnp.stack([chip, ic] + [4 * px + 2 * py + pc for px, py, pc in _peers(ix, iy, ic)]).astype(jnp.int32)

    w_in_all, later = _gather_step(w, chipvec)
    sp = {n: w[n][0] for n in SMALL if n != "ffn_conv"}
    started = {}

    def emit(name, g):
        *started[name], token = partials_start(g, name="partials_start_" + name)
        return token

    loss_acc, grad_x, packed = _local_step(x[0], mem[0], positions[0], loss_target[0], w_in_all, later, sp, emit)
    grads, delta, new_m, new_v = _reduce_update(started, packed, w, m, v, chipvec, cvec, order)
    loss = lax.psum(loss_acc[0, 0], ("x", "y", "c"))
    ordered = lambda d: [d[n] for n in WEIGHTS]
    return (loss, grad_x[None], *ordered(grads), *ordered(delta), *ordered(new_m), *ordered(new_v))
```

```python
import math

import jax
import jax.numpy as jnp
from jax import lax
from jax.experimental import pallas as pl
from jax.experimental.pallas import tpu as pltpu

F32 = jnp.float32
BF16 = jnp.bfloat16
MXU_DTYPE = jnp.bfloat16
WIRE_DTYPE = jnp.bfloat16
EPS = 1e-6
VMEM_LIMIT_V7X = 56 * 1024 * 1024

D_MODEL = 1024
HEAD_DIM = 64
BLK = 128
XA_HEADS = 4
XA_DH = 256
MEM_LEN = 256
D_FF = 2816
IN_COLS_DUP = 2048
N_CHIPS = 4
N_DEV = 8

ADAM_LR = 0.001
ADAM_B1 = 0.9
ADAM_B2 = 0.999
ADAM_EPS = 1e-08
ADAM_WD = 0.01
ADAM_STEP = 10

NT = (((1,), (1,)), ((), ()))
TN = (((0,), (0,)), ((), ()))
NN = (((1,), (0,)), ((), ()))
MINF = float(jnp.finfo(jnp.float32).min)
GELU_K0 = math.sqrt(2.0 / math.pi)
GELU_K1 = 0.044715

BS = pl.BlockSpec
SDS = jax.ShapeDtypeStruct
ANY = pl.BlockSpec(memory_space=pl.ANY)
MESH = pl.DeviceIdType.MESH


def _dot(a, b, dims=NN):
    return lax.dot_general(a.astype(MXU_DTYPE), b.astype(MXU_DTYPE), dims, preferred_element_type=F32)


def _segsum(x, bmat):
    hi = x.astype(BF16)
    lo = (x - hi.astype(F32)).astype(BF16)
    return (jnp.dot(hi, bmat, preferred_element_type=F32) + jnp.dot(lo, bmat, preferred_element_type=F32))


def _gelu(x):
    return 0.5 * x * (1.0 + jnp.tanh(GELU_K0 * (x + GELU_K1 * x * x * x)))


def _gelu_grad(x):
    t = jnp.tanh(GELU_K0 * (x + GELU_K1 * x * x * x))
    return 0.5 * (1.0 + t) + 0.5 * x * (1.0 - t * t) * GELU_K0 * (1.0 + 3.0 * GELU_K1 * x * x)


def _rms(x):
    return lax.rsqrt(jnp.mean(x * x, axis=-1, keepdims=True) + EPS)


def _rms_bwd(dy, x, g, r):
    dyg = dy * g
    dx = r * dyg - x * (r * r * r) * jnp.mean(dyg * x, axis=-1, keepdims=True)
    return dx, dy * x * r


def _pcall(body, *, name, grid, in_specs, out_specs, out_shape, scratch=(), prefetch=0, after=None):
    params = pltpu.CompilerParams(dimension_semantics=("arbitrary",) * len(grid), vmem_limit_bytes=VMEM_LIMIT_V7X)
    in_specs = list(in_specs)
    kernel_fn = body
    if after is not None:
        n_in = prefetch + len(in_specs)
        in_specs.append(ANY)

        def kernel_fn(*refs):
            return body(*refs[:n_in], *refs[n_in + 1:])

    if prefetch:
        spec = pltpu.PrefetchScalarGridSpec(num_scalar_prefetch=prefetch, grid=grid, in_specs=in_specs,
                                            out_specs=out_specs, scratch_shapes=list(scratch))
        call = pl.pallas_call(kernel_fn, name=name, grid_spec=spec, out_shape=out_shape, compiler_params=params)
    else:
        call = pl.pallas_call(kernel_fn, name=name, grid=grid, in_specs=in_specs, out_specs=out_specs,
                              out_shape=out_shape, scratch_shapes=list(scratch), compiler_params=params)
    return call if after is None else (lambda *args: call(*args, after))


def _tile(n, prefs):
    for p in prefs:
        if p <= n and n % p == 0:
            return p
    return n


def _resident(shape):
    return pl.BlockSpec(shape, lambda *_: (0,) * len(shape), pipeline_mode=pl.Buffered(1))


def _acc_rows(ref, row, val):
    ref[row:row + 1, :] += jnp.sum(val, axis=0, keepdims=True)


def rms_mm(x, g, w3, *, name, tm=1024):
    M, K = x.shape
    Q, _, C = w3.shape
    tm = _tile(M, (tm, 256))

    def body(x_ref, g_ref, w_ref, h_ref, o_ref):
        @pl.when(pl.program_id(1) == 0)
        def _():
            xv = x_ref[...]
            h_ref[...] = (xv * _rms(xv) * g_ref[...]).astype(h_ref.dtype)

        o_ref[...] = _dot(h_ref[...], w_ref[pl.program_id(1)])

    return _pcall(body, name=name, grid=(M // tm, Q),
                  in_specs=[BS((tm, K), lambda i, j: (i, 0)), BS((1, K), lambda i, j: (0, 0)),
                            _resident((Q, K, C))],
                  out_specs=[BS((tm, K), lambda i, j: (i, 0)), BS((tm, C), lambda i, j: (i, j))],
                  out_shape=[SDS((M, K), MXU_DTYPE), SDS((M, Q * C), F32)])(x, g, w3)


def mm(a, w, *, name, res):
    M, K = a.shape
    N = w.shape[1]
    tm = _tile(M, (1024, 256))

    def body(a_ref, w_ref, r_ref, o_ref):
        o_ref[...] = _dot(a_ref[...], w_ref[...]) + r_ref[...]

    return _pcall(body, name=name, grid=(M // tm,),
                  in_specs=[BS((tm, K), lambda i: (i, 0)), _resident((K, N)), BS((tm, N), lambda i: (i, 0))],
                  out_specs=BS((tm, N), lambda i: (i, 0)), out_shape=SDS((M, N), F32))(a, w, res)


def _nt_chunks(a_ref, w_ref):
    q_n, _, kc = w_ref.shape
    acc = _dot(a_ref[:, 0:kc], w_ref[0], NT)
    for q in range(1, q_n):
        acc = acc + _dot(a_ref[:, q * kc:(q + 1) * kc], w_ref[q], NT)
    return acc


def mm_nt(a, w3, *, name, after=None):
    M = a.shape[0]
    Q, N, Kc = w3.shape
    tm = _tile(M, (1024, 256))

    def body(a_ref, w_ref, o_ref):
        o_ref[...] = _nt_chunks(a_ref, w_ref)

    return _pcall(body, name=name, grid=(M // tm,), after=after,
                  in_specs=[BS((tm, Q * Kc), lambda i: (i, 0)), _resident((Q, N, Kc))],
                  out_specs=BS((tm, N), lambda i: (i, 0)), out_shape=SDS((M, N), F32))(a, w3)


def mm_nt_rms_bwd(a, w3, x, g, dres, *, name, tm=512, after=None):
    M = a.shape[0]
    Q, N, Kc = w3.shape
    tm = _tile(M, (tm, 256))

    def body(a_ref, w_ref, x_ref, g_ref, dr_ref, dx_ref, dg_ref):
        @pl.when(pl.program_id(0) == 0)
        def _():
            dg_ref[...] = jnp.zeros_like(dg_ref)

        xv = x_ref[...]
        dx, dgc = _rms_bwd(_nt_chunks(a_ref, w_ref), xv, g_ref[...], _rms(xv))
        dx_ref[...] = dr_ref[...] + dx
        _acc_rows(dg_ref, 0, dgc)

    row = BS((tm, N), lambda i: (i, 0))
    return _pcall(body, name=name, grid=(M // tm,), after=after,
                  in_specs=[BS((tm, Q * Kc), lambda i: (i, 0)), _resident((Q, N, Kc)), row,
                            BS((1, N), lambda i: (0, 0)), row],
                  out_specs=[row, BS((8, N), lambda i: (0, 0))],
                  out_shape=[SDS((M, N), F32), SDS((8, N), F32)])(a, w3, x, g, dres)


def mm_nt_post_bwd(a, w3, attn, gm, gao, ggo, *, name, after=None):
    M = a.shape[0]
    Q, N, Kc = w3.shape
    tm = _tile(M, (512, 256))
    hw = N // 2

    def body(a_ref, w_ref, at_ref, gm_ref, gao_ref, ggo_ref, da_ref, dgm_ref, dg_ref):
        @pl.when(pl.program_id(0) == 0)
        def _():
            dg_ref[...] = jnp.zeros_like(dg_ref)

        dy = _nt_chunks(a_ref, w_ref)
        av, gmv = at_ref[...], gm_ref[...]
        da, dga = _rms_bwd(dy[:, :hw], av, gao_ref[...], _rms(av))
        dgm, dgg = _rms_bwd(dy[:, hw:], gmv, ggo_ref[...], _rms(gmv))
        da_ref[...] = da
        dgm_ref[...] = dgm
        dg_ref[0:1, :hw] += jnp.sum(dga, axis=0, keepdims=True)
        dg_ref[0:1, hw:] += jnp.sum(dgg, axis=0, keepdims=True)

    half = BS((tm, hw), lambda i: (i, 0))
    const = lambda r, w: BS((r, w), lambda i: (0, 0))
    return _pcall(body, name=name, grid=(M // tm,), after=after,
                  in_specs=[BS((tm, Q * Kc), lambda i: (i, 0)), _resident((Q, N, Kc)), half, half,
                            const(1, hw), const(1, hw)],
                  out_specs=[half, half, const(8, N)],
                  out_shape=[SDS((M, hw), F32), SDS((M, hw), F32), SDS((8, N), F32)])(a, w3, attn, gm, gao, ggo)


def mm_loss(a, w, res, target, *, name):
    M, K = a.shape
    N = w.shape[1]
    tm = _tile(M, (512, 256))

    def body(a_ref, w_ref, r_ref, t_ref, d_ref, l_ref):
        @pl.when(pl.program_id(0) == 0)
        def _():
            l_ref[...] = jnp.zeros_like(l_ref)

        e = _dot(a_ref[...], w_ref[...]) + r_ref[...] - t_ref[...]
        d_ref[...] = e * (1.0 / N)
        l_ref[...] += jnp.sum(e * e) * (0.5 / N)

    row = BS((tm, N), lambda i: (i, 0))
    return _pcall(body, name=name, grid=(M // tm,),
                  in_specs=[BS((tm, K), lambda i: (i, 0)), _resident((K, N)), row, row],
                  out_specs=[row, BS((8, 128), lambda i: (0, 0))],
                  out_shape=[SDS((M, N), F32), SDS((8, 128), F32)])(a, w, res, target)


def mm_tn(a, b, *, name, out_dtype, chunks=1):
    M, K = a.shape
    N = b.shape[1]
    C = N // chunks
    tm = _tile(M, (1024, 256))
    tk = _tile(K, (1408, 1024, 512))
    tn = _tile(C, (1408, 1024, 512))
    per = C // tn
    nm = M // tm

    def body(a_ref, b_ref, o_ref, acc):
        m = pl.program_id(2)

        @pl.when(m == 0)
        def _():
            acc[...] = jnp.zeros_like(acc)

        acc[...] += _dot(a_ref[...], b_ref[...], TN)

        @pl.when(m == nm - 1)
        def _():
            o_ref[...] = acc[...].astype(o_ref.dtype)

    return _pcall(body, name=name, grid=(K // tk, N // tn, nm),
                  in_specs=[BS((tm, tk), lambda k, n, m: (m, k)), BS((tm, tn), lambda k, n, m: (m, n))],
                  out_specs=BS((None, tk, tn), lambda k, n, m: (n // per, k, n % per)),
                  out_shape=SDS((chunks, K, C), out_dtype), scratch=[pltpu.VMEM((tk, tn), F32)])(a, b)


def _lane(shape):
    return lax.broadcasted_iota(jnp.int32, shape, 1)


def _head_means(slabs, bmat):
    tm = slabs[0].shape[0]
    means = _segsum(jnp.concatenate(slabs, axis=0), bmat) * (1.0 / HEAD_DIM)
    return [means[i * tm:(i + 1) * tm] for i in range(len(slabs))]


def _half_swap(x, first):
    return jnp.where(first, pltpu.roll(x, 96, 1), pltpu.roll(x, 32, 1))


def mixer_pre(proj, cos, sin, gq, gk, gvn, bmat):
    S = proj.shape[0]
    tm = _tile(S, (256,))

    def body(p_ref, c_ref, s_ref, gq_ref, gk_ref, gvn_ref, b_ref, qr_ref, kr_ref, vb_ref, gu_ref, gvo_ref):
        cos_v, sin_v, bm = c_ref[...], s_ref[...], b_ref[...]
        first = (_lane((tm, 128)) & 63) < 32
        slabs = [p_ref[:, s * 128:(s + 1) * 128] for s in range(6)]
        for s, (slab, ms) in enumerate(zip(slabs, _head_means([x * x for x in slabs], bm))):
            qn = slab * lax.rsqrt(ms + EPS) * (gq_ref[...] if s < 4 else gk_ref[...])
            out = qn * cos_v + _half_swap(qn, first) * sin_v
            if s < 4:
                qr_ref[:, s * 128:(s + 1) * 128] = out.astype(qr_ref.dtype)
            else:
                kr_ref[:, (s - 4) * 128:(s - 3) * 128] = out.astype(kr_ref.dtype)
        vb_ref[...] = p_ref[:, 768:1024].astype(vb_ref.dtype)
        gu_ref[...] = _gelu(p_ref[:, 1024:1536])
        gv = _gelu(p_ref[:, 1536:2048])
        gvo_ref[...] = (gv * _rms(gv) * gvn_ref[...]).astype(gvo_ref.dtype)

    row = lambda w: BS((tm, w), lambda i: (i, 0))
    const = lambda r, w: BS((r, w), lambda i: (0, 0))
    return _pcall(body, name="mixer_pre", grid=(S // tm,),
                  in_specs=[row(IN_COLS_DUP), row(128), row(128), const(1, 128), const(1, 128), const(1, 512),
                            const(128, 128)],
                  out_specs=[row(512), row(256), row(256), row(512), row(512)],
                  out_shape=[SDS((S, 512), MXU_DTYPE), SDS((S, 256), MXU_DTYPE), SDS((S, 256), MXU_DTYPE),
                             SDS((S, 512), F32), SDS((S, 512), MXU_DTYPE)])(proj, cos, sin, gq, gk, gvn, bmat)


def _by_head(x2, lo):
    z = jnp.zeros((BLK, 128), x2.dtype)
    parts = []
    for s in range(2):
        xs = x2[:, s * 128:(s + 1) * 128]
        parts += [jnp.where(lo, xs, z), jnp.where(lo, z, xs)]
    return jnp.concatenate(parts, axis=0)


def _from_heads(o4, lo):
    return jnp.concatenate([jnp.where(lo, o4[0:BLK], o4[BLK:2 * BLK]),
                            jnp.where(lo, o4[2 * BLK:3 * BLK], o4[3 * BLK:])], axis=1)


def _swa_probs(q2, kd, sink, n, lo):
    qp = _by_head(q2, lo)
    sc = _dot(qp, kd, NT) * (1.0 / math.sqrt(HEAD_DIM))
    r_i = lax.broadcasted_iota(jnp.int32, (4 * BLK, 2 * BLK), 0)
    k_j = lax.broadcasted_iota(jnp.int32, (4 * BLK, 2 * BLK), 1)
    diff = (r_i & (BLK - 1)) + BLK - k_j
    mask = (diff >= 0) & (diff < BLK) & ((k_j >= BLK) | (n > 0))
    sc = jnp.where(mask, sc, MINF)
    m = jnp.maximum(jnp.max(sc, axis=1, keepdims=True), sink)
    p = jnp.exp(sc - m)
    es = jnp.exp(sink - m)
    inv = 1.0 / (jnp.sum(p, axis=1, keepdims=True) + es)
    return qp, p * inv, es * inv


def swa_fwd(qr, kr, vb, sinkcol, gao):
    S = qr.shape[0]
    nb = S // BLK

    def body(q_ref, kc_ref, kp_ref, vc_ref, vp_ref, sk_ref, g_ref, o_ref, ya_ref):
        n = pl.program_id(0)
        lo = _lane((BLK, 128)) < 64
        for h in range(2):
            hs, qs = slice(h * 128, (h + 1) * 128), slice(h * 256, (h + 1) * 256)
            kd = jnp.concatenate([kp_ref[:, hs], kc_ref[:, hs]], axis=0)
            vd = jnp.concatenate([vp_ref[:, hs], vc_ref[:, hs]], axis=0)
            sink = jnp.concatenate([sk_ref[2 * h], sk_ref[2 * h + 1]], axis=0)
            _, p, _ = _swa_probs(q_ref[:, qs], kd, sink, n, lo)
            o_ref[:, qs] = _from_heads(_dot(p, vd), lo)
        a = o_ref[...]
        ya_ref[...] = (a * _rms(a) * g_ref[...]).astype(ya_ref.dtype)

    cur = lambda w: BS((BLK, w), lambda n: (n, 0))
    prev = lambda w: BS((BLK, w), lambda n: (jnp.maximum(n - 1, 0), 0))
    return _pcall(body, name="swa_fwd", grid=(nb,),
                  in_specs=[cur(512), cur(256), prev(256), cur(256), prev(256),
                            BS((4, 2 * BLK, 1), lambda n: (0, 0, 0)), BS((1, 512), lambda n: (0, 0))],
                  out_specs=[cur(512), cur(512)],
                  out_shape=[SDS((S, 512), F32), SDS((S, 512), MXU_DTYPE)])(qr, kr, kr, vb, vb, sinkcol, gao)


def gmlp_fwd(gvn, gu, ya, w2, bsl, ggo):
    S = gvn.shape[0]

    def body(gvn_ref, gu_ref, ya_ref, w2_ref, bsl_ref, g_ref, gm_ref, y_ref):
        lo = _lane((BLK, 128)) < 64
        for j in range(4):
            sl = slice(j * 128, (j + 1) * 128)
            m2 = _dot(w2_ref[j], gvn_ref[:, sl])
            mixed = jnp.where(lo, m2[:BLK], m2[BLK:]) + bsl_ref[j]
            gm_ref[:, sl] = gu_ref[:, sl] * mixed
        gm = gm_ref[...]
        y_ref[:, :512] = ya_ref[...]
        y_ref[:, 512:] = (gm * _rms(gm) * g_ref[...]).astype(y_ref.dtype)

    row = lambda w: BS((BLK, w), lambda n: (n, 0))
    return _pcall(body, name="gmlp_fwd", grid=(S // BLK,),
                  in_specs=[row(512), row(512), row(512), BS((4, 2 * BLK, BLK), lambda n: (0, 0, 0)),
                            BS((4, BLK, 128), lambda n: (0, 0, 0)), BS((1, 512), lambda n: (0, 0))],
                  out_specs=[row(512), row(1024)],
                  out_shape=[SDS((S, 512), F32), SDS((S, 1024), MXU_DTYPE)])(gvn, gu, ya, w2, bsl, ggo)


def mem_pre(kv, gxk):
    def body(kv_ref, g_ref, kn_ref, vb_ref):
        for h in range(XA_HEADS):
            sl = slice(h * XA_DH, (h + 1) * XA_DH)
            k = kv_ref[:, sl]
            kn_ref[:, sl] = (k * _rms(k) * g_ref[...]).astype(kn_ref.dtype)
        vb_ref[...] = kv_ref[:, 1024:2048].astype(vb_ref.dtype)

    full = lambda r, w: BS((r, w), lambda i: (0, 0))
    return _pcall(body, name="mem_pre", grid=(1,), in_specs=[full(MEM_LEN, 2048), full(1, XA_DH)],
                  out_specs=[full(MEM_LEN, 1024), full(MEM_LEN, 1024)],
                  out_shape=[SDS((MEM_LEN, 1024), MXU_DTYPE), SDS((MEM_LEN, 1024), MXU_DTYPE)])(kv, gxk)


def _xa_probs(qh, g, kn_h):
    r = _rms(qh)
    qn = qh * r * g
    s = _dot(qn, kn_h, NT) * (1.0 / math.sqrt(XA_DH))
    p = jnp.exp(s - jnp.max(s, axis=1, keepdims=True))
    return r, qn, p * (1.0 / jnp.sum(p, axis=1, keepdims=True))


def xattn_fwd(qx, kn, vb, gxq):
    S = qx.shape[0]
    tm = _tile(S, (512, 256))

    def body(q_ref, kn_ref, vb_ref, g_ref, o_ref):
        for h in range(XA_HEADS):
            sl = slice(h * XA_DH, (h + 1) * XA_DH)
            _, _, p = _xa_probs(q_ref[:, sl], g_ref[...], kn_ref[:, sl])
            o_ref[:, sl] = _dot(p, vb_ref[:, sl]).astype(o_ref.dtype)

    full = lambda r, w: BS((r, w), lambda i: (0, 0))
    return _pcall(body, name="xattn_fwd", grid=(S // tm,),
                  in_specs=[BS((tm, 1024), lambda i: (i, 0)), full(MEM_LEN, 1024), full(MEM_LEN, 1024), full(1, XA_DH)],
                  out_specs=BS((tm, 1024), lambda i: (i, 0)), out_shape=SDS((S, 1024), MXU_DTYPE))(qx, kn, vb, gxq)


def _causal_taps(a, halo_ref, first_tile, row):
    h6 = jnp.where(first_tile, 0.0, halo_ref[6:7, :])
    h7 = jnp.where(first_tile, 0.0, halo_ref[7:8, :])
    a1 = jnp.where(row == 0, h7, pltpu.roll(a, 1, 0))
    a2 = jnp.where(row == 0, h6, jnp.where(row == 1, h7, pltpu.roll(a, 2, 0)))
    return a1, a2


def _conv(a, a1, a2, w_ref, b_ref):
    return w_ref[2:3, :] * a + w_ref[1:2, :] * a1 + w_ref[0:1, :] * a2 + b_ref[...]


def _conv_specs(tm):
    halo_blocks = tm // 8
    return [BS((tm, D_FF), lambda i: (i, 0)), BS((tm, D_FF), lambda i: (i, 1)),
            BS((8, D_FF), lambda i: (jnp.maximum(i * halo_blocks - 1, 0), 0)),
            BS((8, D_FF), lambda i: (jnp.maximum(i * halo_blocks - 1, 0), 1)),
            BS((3, D_FF), lambda i: (0, 0)), BS((3, D_FF), lambda i: (0, 1)),
            BS((1, D_FF), lambda i: (0, 0)), BS((1, D_FF), lambda i: (0, 1))]


def convgate_fwd(a, cw, cb):
    S = a.shape[0]
    tm = _tile(S, (256,))

    def body(ag_ref, au_ref, hg_ref, hu_ref, wg_ref, wu_ref, bg_ref, bu_ref, f_ref):
        first_tile = pl.program_id(0) == 0
        row = lax.broadcasted_iota(jnp.int32, (tm, D_FF), 0)
        ag, au = ag_ref[...], au_ref[...]
        cg = _conv(ag, *_causal_taps(ag, hg_ref, first_tile, row), wg_ref, bg_ref)
        cu = _conv(au, *_causal_taps(au, hu_ref, first_tile, row), wu_ref, bu_ref)
        f_ref[...] = (_gelu(cg) * cu).astype(f_ref.dtype)

    return _pcall(body, name="convgate_fwd", grid=(S // tm,), in_specs=_conv_specs(tm),
                  out_specs=BS((tm, D_FF), lambda i: (i, 0)),
                  out_shape=SDS((S, D_FF), MXU_DTYPE))(a, a, a, a, cw, cw, cb, cb)


def convgate_bwd(a, df, cw, cb, after=None):
    S = a.shape[0]
    tm = _tile(S, (128,))

    def body(ag_ref, au_ref, hg_ref, hu_ref, wg_ref, wu_ref, bg_ref, bu_ref, df_ref, dc_ref, gw_ref):
        first_tile = pl.program_id(0) == 0

        @pl.when(first_tile)
        def _():
            gw_ref[...] = jnp.zeros_like(gw_ref)

        row = lax.broadcasted_iota(jnp.int32, (tm, D_FF), 0)
        ag, au, df_v = ag_ref[...], au_ref[...], df_ref[...]
        ag1, ag2 = _causal_taps(ag, hg_ref, first_tile, row)
        au1, au2 = _causal_taps(au, hu_ref, first_tile, row)
        cg = _conv(ag, ag1, ag2, wg_ref, bg_ref)
        cu = _conv(au, au1, au2, wu_ref, bu_ref)
        dcg = df_v * cu * _gelu_grad(cg)
        dcu = df_v * _gelu(cg)
        dc_ref[:, :D_FF] = dcg
        dc_ref[:, D_FF:] = dcu
        for col, dcv, taps in ((slice(0, D_FF), dcg, (ag2, ag1, ag)), (slice(D_FF, 2 * D_FF), dcu, (au2, au1, au))):
            for j in range(3):
                gw_ref[j:j + 1, col] += jnp.sum(dcv * taps[j], axis=0, keepdims=True)
            gw_ref[3:4, col] += jnp.sum(dcv, axis=0, keepdims=True)

    return _pcall(body, name="convgate_bwd", grid=(S // tm,), after=after,
                  in_specs=_conv_specs(tm) + [BS((tm, D_FF), lambda i: (i, 0))],
                  out_specs=[BS((tm, 2 * D_FF), lambda i: (i, 0)), BS((8, 2 * D_FF), lambda i: (0, 0))],
                  out_shape=[SDS((S, 2 * D_FF), F32), SDS((8, 2 * D_FF), F32)])(a, a, a, a, cw, cw, cb, cb, df)


def conv_transpose(dc, cw):
    S, C = dc.shape
    tm = _tile(S, (128,))
    nt = S // tm
    halo_blocks = tm // 8

    def body(dc_ref, halo_ref, w_ref, da_ref):
        last_tile = pl.program_id(0) == nt - 1
        row = lax.broadcasted_iota(jnp.int32, (tm, C), 0)
        h0 = jnp.where(last_tile, 0.0, halo_ref[0:1, :])
        h1 = jnp.where(last_tile, 0.0, halo_ref[1:2, :])
        dc_v = dc_ref[...]
        n1 = jnp.where(row == tm - 1, h0, pltpu.roll(dc_v, tm - 1, 0))
        n2 = jnp.where(row == tm - 1, h1, jnp.where(row == tm - 2, h0, pltpu.roll(dc_v, tm - 2, 0)))
        da_ref[...] = (w_ref[2:3, :] * dc_v + w_ref[1:2, :] * n1 + w_ref[0:1, :] * n2).astype(da_ref.dtype)

    return _pcall(body, name="conv_transpose", grid=(nt,),
                  in_specs=[BS((tm, C), lambda i: (i, 0)),
                            BS((8, C), lambda i: (jnp.minimum((i + 1) * halo_blocks, S // 8 - 1), 0)),
                            BS((3, C), lambda i: (0, 0))],
                  out_specs=BS((tm, C), lambda i: (i, 0)), out_shape=SDS((S, C), MXU_DTYPE))(dc, dc, cw)


def xattn_bwd(qx, dxo, kn, vb, gxq, after=None):
    S = qx.shape[0]
    tm = _tile(S, (512, 256))

    def body(q_ref, do_ref, kn_ref, vb_ref, g_ref, dq_ref, dkn_ref, dv_ref, dg_ref):
        @pl.when(pl.program_id(0) == 0)
        def _():
            dkn_ref[...] = jnp.zeros_like(dkn_ref)
            dv_ref[...] = jnp.zeros_like(dv_ref)
            dg_ref[...] = jnp.zeros_like(dg_ref)

        g = g_ref[...]
        for h in range(XA_HEADS):
            sl = slice(h * XA_DH, (h + 1) * XA_DH)
            qh, do = q_ref[:, sl], do_ref[:, sl]
            r, qn, p = _xa_probs(qh, g, kn_ref[:, sl])
            dp = _dot(do, vb_ref[:, sl], NT)
            ds = p * (dp - jnp.sum(dp * p, axis=1, keepdims=True)) * (1.0 / math.sqrt(XA_DH))
            dqn = _dot(ds, kn_ref[:, sl])
            dkn_ref[:, sl] += _dot(ds, qn, TN)
            dv_ref[:, sl] += _dot(p, do, TN)
            dqh, dgc = _rms_bwd(dqn, qh, g, r)
            dq_ref[:, sl] = dqh.astype(dq_ref.dtype)
            _acc_rows(dg_ref, 0, dgc)

    row = BS((tm, 1024), lambda i: (i, 0))
    full = lambda r, w: BS((r, w), lambda i: (0, 0))
    return _pcall(body, name="xattn_bwd", grid=(S // tm,), after=after,
                  in_specs=[row, row, full(MEM_LEN, 1024), full(MEM_LEN, 1024), full(1, XA_DH)],
                  out_specs=[row, full(MEM_LEN, 1024), full(MEM_LEN, 1024), full(8, XA_DH)],
                  out_shape=[SDS((S, 1024), MXU_DTYPE), SDS((MEM_LEN, 1024), F32), SDS((MEM_LEN, 1024), F32),
                             SDS((8, XA_DH), F32)])(qx, dxo, kn, vb, gxq)


def mem_bwd(kv, dkn, dvb, gxk, after=None):
    def body(kv_ref, dkn_ref, dv_ref, g_ref, dkv_ref, dg_ref):
        dg_ref[...] = jnp.zeros_like(dg_ref)
        for h in range(XA_HEADS):
            sl = slice(h * XA_DH, (h + 1) * XA_DH)
            k = kv_ref[:, sl]
            dk, dgc = _rms_bwd(dkn_ref[:, sl], k, g_ref[...], _rms(k))
            dkv_ref[:, sl] = dk.astype(dkv_ref.dtype)
            _acc_rows(dg_ref, 0, dgc)
        dkv_ref[:, 1024:2048] = dv_ref[...].astype(dkv_ref.dtype)

    full = lambda r, w: BS((r, w), lambda i: (0, 0))
    return _pcall(body, name="mem_bwd", grid=(1,), after=after,
                  in_specs=[full(MEM_LEN, 2048), full(MEM_LEN, 1024), full(MEM_LEN, 1024), full(1, XA_DH)],
                  out_specs=[full(MEM_LEN, 2048), full(8, XA_DH)],
                  out_shape=[SDS((MEM_LEN, 2048), MXU_DTYPE), SDS((8, XA_DH), F32)])(kv, dkn, dvb, gxk)


def gmlp_bwd(dgm, gvn, gu, w2, w2t, bsl, after=None):
    S = dgm.shape[0]

    def body(dgm_ref, gvn_ref, gu_ref, w2_ref, w2t_ref, bsl_ref, dgu_ref, dgvn_ref, dws_ref, dbl_ref):
        @pl.when(pl.program_id(0) == 0)
        def _():
            dws_ref[...] = jnp.zeros_like(dws_ref)
            dbl_ref[...] = jnp.zeros_like(dbl_ref)

        lo = _lane((BLK, 128)) < 64
        for j in range(4):
            sl = slice(j * 128, (j + 1) * 128)
            gvn_s = gvn_ref[:, sl]
            m2 = _dot(w2_ref[j], gvn_s)
            mixed = jnp.where(lo, m2[:BLK], m2[BLK:]) + bsl_ref[j]
            dgm_s = dgm_ref[:, sl]
            dgu_ref[:, sl] = dgm_s * mixed
            dmx = dgm_s * gu_ref[:, sl]
            d2 = _dot(w2t_ref[j], dmx)
            dgvn_ref[:, sl] = jnp.where(lo, d2[:BLK], d2[BLK:])
            z = jnp.zeros_like(dmx)
            dws_ref[2 * j] += _dot(jnp.where(lo, dmx, z), gvn_s, NT)
            dws_ref[2 * j + 1] += _dot(jnp.where(lo, z, dmx), gvn_s, NT)
            dbl_ref[j] += dmx

    row = lambda w: BS((BLK, w), lambda n: (n, 0))
    const3 = lambda a, b, c: BS((a, b, c), lambda n: (0, 0, 0))
    return _pcall(body, name="gmlp_bwd", grid=(S // BLK,), after=after,
                  in_specs=[row(512), row(512), row(512), const3(4, 2 * BLK, BLK), const3(4, 2 * BLK, BLK),
                            const3(4, BLK, 128)],
                  out_specs=[row(512), row(512), const3(8, BLK, BLK), const3(4, BLK, 128)],
                  out_shape=[SDS((S, 512), F32), SDS((S, 512), F32), SDS((8, BLK, BLK), F32),
                             SDS((4, BLK, 128), F32)])(dgm, gvn, gu, w2, w2t, bsl)


def swa_bwd(qr, kr, vb, sinkcol, dattn):
    S = qr.shape[0]
    nb = S // BLK

    def body(q_ref, kc_ref, kp_ref, vc_ref, vp_ref, sk_ref, do_ref, dq_ref, dk_ref, dv_ref, dsk_ref,
             carry_k, carry_v, prev_k, prev_v):
        n = pl.program_id(0)

        @pl.when(n == 0)
        def _():
            dsk_ref[...] = jnp.zeros_like(dsk_ref)
            carry_k[...] = jnp.zeros_like(carry_k)
            carry_v[...] = jnp.zeros_like(carry_v)

        @pl.when(n < nb)
        def _():
            lo = _lane((BLK, 128)) < 64
            for h in range(2):
                hs, qs = slice(h * 128, (h + 1) * 128), slice(h * 256, (h + 1) * 256)
                kd = jnp.concatenate([kp_ref[:, hs], kc_ref[:, hs]], axis=0)
                vd = jnp.concatenate([vp_ref[:, hs], vc_ref[:, hs]], axis=0)
                sink = jnp.concatenate([sk_ref[2 * h], sk_ref[2 * h + 1]], axis=0)
                qp, p, psink = _swa_probs(q_ref[:, qs], kd, sink, n, lo)
                dop = _by_head(do_ref[:, qs], lo)
                dp = _dot(dop, vd, NT)
                delta = jnp.sum(dp * p, axis=1, keepdims=True)
                ds = p * (dp - delta) * (1.0 / math.sqrt(HEAD_DIM))
                dsink = -psink * delta
                dsk_ref[2 * h] += dsink[:2 * BLK]
                dsk_ref[2 * h + 1] += dsink[2 * BLK:]
                dq_ref[:, qs] = _from_heads(_dot(ds, kd), lo)
                dkd = _dot(ds, qp, TN)
                dvd = _dot(p, dop, TN)
                prev_k[:, hs] = carry_k[:, hs] + dkd[:BLK]
                prev_v[:, hs] = carry_v[:, hs] + dvd[:BLK]
                carry_k[:, hs] = dkd[BLK:]
                carry_v[:, hs] = dvd[BLK:]

        @pl.when(n == nb)
        def _():
            prev_k[...] = carry_k[...]
            prev_v[...] = carry_v[...]

        dk_ref[...] = prev_k[...]
        dv_ref[...] = prev_v[...]

    last = nb - 1
    cur = lambda w: BS((BLK, w), lambda n: (jnp.minimum(n, last), 0))
    prev = lambda w: BS((BLK, w), lambda n: (jnp.clip(n - 1, 0, last), 0))
    done = lambda w: BS((BLK, w), lambda n: (jnp.maximum(n - 1, 0), 0))
    return _pcall(body, name="swa_bwd", grid=(nb + 1,),
                  in_specs=[cur(512), cur(256), prev(256), cur(256), prev(256),
                            BS((4, 2 * BLK, 1), lambda n: (0, 0, 0)), cur(512)],
                  out_specs=[cur(512), done(256), done(256), BS((4, 2 * BLK, 1), lambda n: (0, 0, 0))],
                  out_shape=[SDS((S, 512), F32), SDS((S, 256), F32), SDS((S, 256), F32), SDS((4, 2 * BLK, 1), F32)],
                  scratch=[pltpu.VMEM((BLK, 256), F32)] * 4)(qr, kr, kr, vb, vb, sinkcol, dattn)


def mixer_pre_bwd(proj, cos, sin, gq, gk, gvn, bmat, dqr, dkr, dvb, dgu, dgvn):
    S = proj.shape[0]
    tm = _tile(S, (256,))

    def body(p_ref, c_ref, s_ref, gq_ref, gk_ref, gvn_ref, b_ref, dqr_ref, dkr_ref, dvb_ref, dgu_ref, dgvn_ref,
             dp_ref, dgq_ref, dgk_ref, dgv_ref):
        @pl.when(pl.program_id(0) == 0)
        def _():
            dgq_ref[...] = jnp.zeros_like(dgq_ref)
            dgk_ref[...] = jnp.zeros_like(dgk_ref)
            dgv_ref[...] = jnp.zeros_like(dgv_ref)

        cos_v, sin_v, bm = c_ref[...], s_ref[...], b_ref[...]
        first = (_lane((tm, 128)) & 63) < 32

        slabs = [p_ref[:, s * 128:(s + 1) * 128] for s in range(6)]
        douts = [dqr_ref[:, s * 128:(s + 1) * 128] for s in range(4)] + [dkr_ref[:, s * 128:(s + 1) * 128] for s in range(2)]
        gains = [gq_ref[...]] * 4 + [gk_ref[...]] * 2
        dqns = [d * cos_v + _half_swap(d * sin_v, first) for d in douts]
        rs = [lax.rsqrt(ms + EPS) for ms in _head_means([x * x for x in slabs], bm)]
        projs = _head_means([dqn * g * x for dqn, g, x in zip(dqns, gains, slabs)], bm)
        for s, (slab, dqn, g, r, pr) in enumerate(zip(slabs, dqns, gains, rs, projs)):
            dx = r * (dqn * g) - slab * (r * r * r) * pr
            dp_ref[:, s * 128:(s + 1) * 128] = dx.astype(dp_ref.dtype)
            _acc_rows(dgq_ref if s < 4 else dgk_ref, 0, dqn * slab * r)
        dp_ref[:, 768:1024] = dvb_ref[...].astype(dp_ref.dtype)
        dp_ref[:, 1024:1536] = (dgu_ref[...] * _gelu_grad(p_ref[:, 1024:1536])).astype(dp_ref.dtype)
        gvp = p_ref[:, 1536:2048]
        gv = _gelu(gvp)
        dgv, dgc = _rms_bwd(dgvn_ref[...], gv, gvn_ref[...], _rms(gv))
        dp_ref[:, 1536:2048] = (dgv * _gelu_grad(gvp)).astype(dp_ref.dtype)
        _acc_rows(dgv_ref, 0, dgc)

    row = lambda w: BS((tm, w), lambda i: (i, 0))
    const = lambda r, w: BS((r, w), lambda i: (0, 0))
    return _pcall(body, name="mixer_pre_bwd", grid=(S // tm,),
                  in_specs=[row(IN_COLS_DUP), row(128), row(128), const(1, 128), const(1, 128), const(1, 512),
                            const(128, 128), row(512), row(256), row(256), row(512), row(512)],
                  out_specs=[row(IN_COLS_DUP), const(8, 128), const(8, 128), const(8, 512)],
                  out_shape=[SDS((S, IN_COLS_DUP), MXU_DTYPE), SDS((8, 128), F32), SDS((8, 128), F32),
                             SDS((8, 512), F32)])(proj, cos, sin, gq, gk, gvn, bmat, dqr, dkr, dvb, dgu, dgvn)


BIG = (("w_in", (1024, 448), True), ("w_out", (256, 1024), False), ("xa_wq", (256, 1024), False),
       ("xa_wkv", (1024, 512), True), ("xa_wo", (256, 1024), False), ("ffn_up", (1024, 1408), True),
       ("ffn_down", (704, 1024), False))
BIG_NAMES = tuple(n for n, _, _ in BIG)
SMALL_VECS = (("mix_norm", 1024), ("q_norm", 64), ("k_norm", 64), ("attn_sinks", 8), ("gmlp_v_norm", 512),
              ("attn_out_norm", 512), ("gmlp_out_norm", 512), ("xa_norm", 1024), ("mem_norm", 1024),
              ("xa_q_norm", 256), ("xa_k_norm", 256), ("ffn_norm", 1024), ("ffn_conv_b", 5632))
SMALL = tuple(n for n, _ in SMALL_VECS) + ("gmlp_bs", "gmlp_ws", "ffn_conv")
WEIGHTS = ("mix_norm", "w_in", "q_norm", "k_norm", "attn_sinks", "gmlp_v_norm", "gmlp_ws", "gmlp_bs",
           "attn_out_norm", "gmlp_out_norm", "w_out", "xa_norm", "mem_norm", "xa_wq", "xa_wkv", "xa_q_norm",
           "xa_k_norm", "xa_wo", "ffn_norm", "ffn_up", "ffn_conv", "ffn_conv_b", "ffn_down")
CONV_SHARD = (3, 1408)
CONV_LANE_ROWS = CONV_SHARD[1] // 128
CONV_CHIP_ROWS = 40


def _small_rows():
    rows, r = {}, 0
    for n, length in SMALL_VECS:
        rows[n] = r
        r += -(-length // 128)
    r += -r % 8
    rows["gmlp_bs"] = r
    r += 8
    rows["gmlp_ws"] = r
    r += 8 * BLK
    rows["ffn_conv"] = r
    r += N_CHIPS * CONV_CHIP_ROWS
    return rows, r


SMALL_ROW, SMALL_ROWS = _small_rows()


def pack_small(dg_mix, dgq, dgk, dsk, dg_gvn, dg_y, dg_xa, dg_mem, dg_xq, dg_xk, dg_ffn, gcw, dbl, dws):
    def body(mix_ref, q_ref, k_ref, sk_ref, gvn_ref, y_ref, xa_ref, mem_ref, xq_ref, xk_ref, ffn_ref, cw_ref,
             dbl_ref, dws_ref, o_ref):
        o_ref[...] = jnp.zeros_like(o_ref)
        lane = _lane((1, 128))

        def put(name, src_ref, row, lane0, length):
            for k in range(length // 128):
                o_ref[SMALL_ROW[name] + k:SMALL_ROW[name] + k + 1, :] = src_ref[row:row + 1, lane0 + k * 128:lane0 + (k + 1) * 128]

        put("mix_norm", mix_ref, 0, 0, 1024)
        for name, ref in (("q_norm", q_ref), ("k_norm", k_ref)):
            v = ref[0:1, :]
            o_ref[SMALL_ROW[name]:SMALL_ROW[name] + 1, :] = jnp.where(lane < HEAD_DIM, v + pltpu.roll(v, 64, 1), 0.0)
        sinks = jnp.zeros((1, 128), F32)
        for s in range(4):
            col = sk_ref[s]
            sinks = sinks + jnp.where(lane == 2 * s, jnp.sum(col[:BLK]), 0.0) + jnp.where(lane == 2 * s + 1, jnp.sum(col[BLK:]), 0.0)
        o_ref[SMALL_ROW["attn_sinks"]:SMALL_ROW["attn_sinks"] + 1, :] = sinks
        put("gmlp_v_norm", gvn_ref, 0, 0, 512)
        put("attn_out_norm", y_ref, 0, 0, 512)
        put("gmlp_out_norm", y_ref, 0, 512, 512)
        put("xa_norm", xa_ref, 0, 0, 1024)
        put("mem_norm", mem_ref, 0, 0, 1024)
        put("xa_q_norm", xq_ref, 0, 0, 256)
        put("xa_k_norm", xk_ref, 0, 0, 256)
        put("ffn_norm", ffn_ref, 0, 0, 1024)
        put("ffn_conv_b", cw_ref, 3, 0, 2 * D_FF)
        r8 = lax.broadcasted_iota(jnp.int32, (8, 128), 0)
        l8 = _lane((8, 128))
        bs = jnp.zeros((8, BLK), F32)
        for j in range(4):
            sel = (((r8 == 2 * j) & (l8 < 64)) | ((r8 == 2 * j + 1) & (l8 >= 64))).astype(F32).astype(BF16)
            xj = dbl_ref[j]
            hi = xj.astype(BF16)
            lo = (xj - hi.astype(F32)).astype(BF16)
            bs = bs + lax.dot_general(sel, hi, NT, preferred_element_type=F32) + lax.dot_general(sel, lo, NT, preferred_element_type=F32)
        o_ref[SMALL_ROW["gmlp_bs"]:SMALL_ROW["gmlp_bs"] + 8, :] = bs
        causal = lax.broadcasted_iota(jnp.int32, (BLK, BLK), 0) >= lax.broadcasted_iota(jnp.int32, (BLK, BLK), 1)
        for h in range(8):
            r0 = SMALL_ROW["gmlp_ws"] + h * BLK
            o_ref[r0:r0 + BLK, :] = jnp.where(causal, dws_ref[h], 0.0)
        for q in range(N_CHIPS):
            for j in range(3):
                for k in range(CONV_LANE_ROWS):
                    r0 = SMALL_ROW["ffn_conv"] + q * CONV_CHIP_ROWS + j * CONV_LANE_ROWS + k
                    l0 = (q * CONV_LANE_ROWS + k) * 128
                    o_ref[r0:r0 + 1, :] = cw_ref[j:j + 1, l0:l0 + 128]

    args = (dg_mix, dgq, dgk, dsk, dg_gvn, dg_y, dg_xa, dg_mem, dg_xq, dg_xk, dg_ffn, gcw, dbl, dws)
    full = lambda a: BS(a.shape, lambda i, nd=a.ndim: (0,) * nd)
    return _pcall(body, name="pack_small", grid=(1,), in_specs=[full(a) for a in args],
                  out_specs=BS((SMALL_ROWS, 128), lambda i: (0, 0)), out_shape=SDS((SMALL_ROWS, 128), F32))(*args)


def _adam(w, g, m, v):
    mn = ADAM_B1 * m + (1.0 - ADAM_B1) * g
    vn = ADAM_B2 * v + (1.0 - ADAM_B2) * (g * g)
    m_hat = mn / (1.0 - ADAM_B1 ** ADAM_STEP)
    v_hat = vn / (1.0 - ADAM_B2 ** ADAM_STEP)
    return -ADAM_LR * (m_hat / (jnp.sqrt(v_hat) + ADAM_EPS) + ADAM_WD * w), mn, vn


def adamw_small(gsum, w, m, v, chipvec):
    n = len(SMALL)

    def body(chip_ref, g_ref, *refs):
        w_refs, m_refs, v_refs = refs[:n], refs[n:2 * n], refs[2 * n:3 * n]
        outs = refs[3 * n:]
        go, do, mo, vo = outs[:n], outs[n:2 * n], outs[2 * n:3 * n], outs[3 * n:]

        def update(i, idx, g):
            d, mn, vn = _adam(w_refs[i][idx], g, m_refs[i][idx], v_refs[i][idx])
            go[i][idx] = g
            do[i][idx] = d
            mo[i][idx] = mn
            vo[i][idx] = vn

        for i, (name, length) in enumerate(SMALL_VECS):
            for k in range(-(-length // 128)):
                wd = min(128, length - k * 128)
                r = SMALL_ROW[name] + k
                update(i, (slice(0, 1), slice(k * 128, k * 128 + wd)), g_ref[r:r + 1, 0:wd])
        i_bs, i_ws, i_cv = len(SMALL_VECS), len(SMALL_VECS) + 1, len(SMALL_VECS) + 2
        update(i_bs, (0,), g_ref[SMALL_ROW["gmlp_bs"]:SMALL_ROW["gmlp_bs"] + 8, :])
        for h in range(8):
            r0 = SMALL_ROW["gmlp_ws"] + h * BLK
            update(i_ws, (0, h), g_ref[r0:r0 + BLK, :])
        mine = g_ref[pl.ds(pl.multiple_of(SMALL_ROW["ffn_conv"] + chip_ref[0] * CONV_CHIP_ROWS, 8), CONV_CHIP_ROWS), :]
        for j in range(3):
            for k in range(CONV_LANE_ROWS):
                r = j * CONV_LANE_ROWS + k
                update(i_cv, (0, slice(j, j + 1), slice(k * 128, (k + 1) * 128)), mine[r:r + 1, :])

    nat = [w[nm] for nm in SMALL]
    full = lambda a: BS(a.shape, lambda i, c, nd=a.ndim: (0,) * nd)
    outs = _pcall(body, name="adamw_small", grid=(1,), prefetch=1,
                  in_specs=[BS((SMALL_ROWS, 128), lambda i, c: (0, 0))] + [full(a) for a in nat] * 3,
                  out_specs=[full(a) for a in nat] * 4, out_shape=[SDS(a.shape, F32) for a in nat] * 4)(
        chipvec, gsum, *nat, *[m[nm] for nm in SMALL], *[v[nm] for nm in SMALL])
    return outs[:n], outs[n:2 * n], outs[2 * n:3 * n], outs[3 * n:]


def adamw_matrix(w, m, v, g_own, g_other, cvec, *, name):
    _, r, c = w.shape
    half = r // 2
    tr = _tile(half, (128, 176))
    T = half // tr

    def body(c_ref, w_ref, m_ref, v_ref, own_ref, oth_ref, g_ref, d_ref, mo_ref, vo_ref):
        g = jnp.where(pl.program_id(0) == c_ref[0], own_ref[...], oth_ref[...])
        d, mn, vn = _adam(w_ref[...], g, m_ref[...], v_ref[...])
        g_ref[...] = g
        d_ref[...] = d
        mo_ref[...] = mn
        vo_ref[...] = vn

    nat = BS((None, tr, c), lambda hf, t, cr: (0, hf * T + t, 0))
    hlf = BS((tr, c), lambda hf, t, cr: (t, 0))
    return _pcall(body, name=name, grid=(2, T), prefetch=1, in_specs=[nat, nat, nat, hlf, hlf], out_specs=[nat] * 4,
                  out_shape=[SDS(w.shape, F32)] * 4)(cvec, w, m, v, g_own, g_other)


def _place():
    return lax.axis_index("x"), lax.axis_index("y"), lax.axis_index("c")


def _other_chips(x, y):
    return [(1 - x, y), (x, 1 - y), (1 - x, 1 - y)]


def _rows_of_core(c, half):
    return pl.ds(pl.multiple_of(c * half, 16), half)


def _rcopy(src, dst, sems, k, to):
    return pltpu.make_async_remote_copy(src_ref=src, dst_ref=dst, send_sem=sems[0].at[k], recv_sem=sems[1].at[k],
                                        device_id=to, device_id_type=MESH)


def _comm_call(body, *, name, out_shape, n_in, n_sems, aliases=None):
    return pl.pallas_call(body, name=name, out_shape=out_shape, in_specs=[ANY] * n_in, out_specs=[ANY] * len(out_shape),
                          scratch_shapes=[pltpu.SemaphoreType.DMA((n_sems,)), pltpu.SemaphoreType.DMA((n_sems,))],
                          input_output_aliases=aliases or {},
                          compiler_params=pltpu.CompilerParams(has_side_effects=True))


def cast_shards(shards, conv, chipvec):
    n = len(shards)

    def body(chip_ref, *refs):
        for i_ref, o_ref in zip(refs[:n + 1], refs[n + 1:]):
            o_ref[...] = i_ref[...].astype(o_ref.dtype)

    in_specs = [BS((s.shape[0] // 4, s.shape[1]), lambda i, p: (i, 0)) for s in shards]
    in_specs.append(BS(conv.shape, lambda i, p: (0, 0)))
    out_specs = [BS((None, s.shape[0] // 4, s.shape[1]), lambda i, p: (p[0], i, 0)) for s in shards]
    out_specs.append(BS((None,) + conv.shape, lambda i, p: (p[0], 0, 0)))
    out_shape = [SDS((N_CHIPS,) + s.shape, MXU_DTYPE) for s in shards] + [SDS((N_CHIPS,) + conv.shape, F32)]
    return _pcall(body, name="cast_shards", grid=(4,), prefetch=1, in_specs=in_specs, out_specs=out_specs,
                  out_shape=out_shape)(chipvec, *shards, conv)


HBM = pl.BlockSpec(memory_space=pltpu.HBM)
SEM = pl.BlockSpec(memory_space=pltpu.SEMAPHORE)
DATAFLOW = pltpu.SideEffectType.DATAFLOW_SIDE_EFFECTING
VMEM_WHOLE = pl.BlockSpec(memory_space=pltpu.VMEM)
TOKEN = jax.ShapeDtypeStruct((8, 128), jnp.float32)


def _gather_copies(bufs, send_sems, recv_sems, outgoing):
    x, y, c = _place()
    p = 2 * x + y
    cps = []
    for i, o in enumerate(bufs):
        for j, (cx, cy) in enumerate(_other_chips(x, y)):
            slot = o.at[p] if outgoing else o.at[2 * cx + cy]
            cps.append(_rcopy(slot, slot, (send_sems, recv_sems), 3 * i + j, (cx, cy, c)))
    return cps


def gather_start(slots):
    n = len(slots)

    def body(*refs):
        send_sems, recv_sems, thru, token = refs[n], refs[n + 1], refs[n + 2:2 * n + 2], refs[2 * n + 2]
        for cp in _gather_copies(thru, send_sems, recv_sems, True):
            cp.start()
        token[...] = jnp.zeros_like(token)

    hbm = [pltpu.with_memory_space_constraint(s, pltpu.HBM) for s in slots]
    outs = pl.pallas_call(
        body, name="gather_start_%d" % n,
        out_shape=[pltpu.SemaphoreType.DMA((3 * n,)), pltpu.SemaphoreType.DMA((3 * n,))]
        + [pltpu.HBM(s.shape, s.dtype) for s in slots] + [TOKEN],
        in_specs=[HBM] * n, out_specs=[SEM, SEM] + [HBM] * n + [VMEM_WHOLE],
        input_output_aliases={i: 2 + i for i in range(n)},
        compiler_params=pltpu.CompilerParams(has_side_effects=DATAFLOW))(*hbm)
    return outs[0], outs[1], outs[2:2 + n], outs[2 + n]


def gather_wait(send_sems, recv_sems, bufs, after):
    n = len(bufs)

    def body(*refs):
        ins, send_ref, recv_ref = refs[:n], refs[n], refs[n + 1]
        for cp in _gather_copies(ins, send_ref, recv_ref, False):
            cp.wait_send()
            cp.wait_recv()

    return pl.pallas_call(
        body, name="gather_wait_%d" % n, out_shape=[pltpu.HBM(s.shape, s.dtype) for s in bufs],
        in_specs=[HBM] * n + [SEM, SEM, ANY], out_specs=[HBM] * n, input_output_aliases={i: i for i in range(n)},
        compiler_params=pltpu.CompilerParams(has_side_effects=DATAFLOW))(*bufs, send_sems, recv_sems, after)


def _peers(x, y, c):
    return [(1 - x if k & 4 else x, 1 - y if k & 2 else y, 1 - c if k & 1 else c) for k in range(1, N_DEV)]


def _partial_copies(g_ref, land_ref, send_sems, recv_sems, outgoing):
    x, y, c = _place()
    half = g_ref.shape[1] // 2
    cps = []
    for k, (px, py, pc) in enumerate(_peers(x, y, c)):
        src = g_ref.at[2 * px + py, _rows_of_core(pc, half)]
        dst = land_ref.at[4 * x + 2 * y + c] if outgoing else land_ref.at[4 * px + 2 * py + pc]
        cps.append(_rcopy(src, dst, (send_sems, recv_sems), k, (px, py, pc)))
    return cps


def partials_start(g, *, name):
    land = lax.empty((N_DEV, g.shape[1] // 2, g.shape[2]), g.dtype)

    def body(g_ref, land_ref, send_sems, recv_sems, g_thru, land_thru, token):
        for cp in _partial_copies(g_thru, land_thru, send_sems, recv_sems, True):
            cp.start()
        token[...] = jnp.zeros_like(token)

    return pl.pallas_call(
        body, name=name,
        out_shape=[pltpu.SemaphoreType.DMA((N_DEV - 1,)), pltpu.SemaphoreType.DMA((N_DEV - 1,)),
                   pltpu.HBM(g.shape, g.dtype), pltpu.HBM(land.shape, land.dtype), TOKEN],
        in_specs=[HBM, HBM], out_specs=[SEM, SEM, HBM, HBM, VMEM_WHOLE], input_output_aliases={0: 2, 1: 3},
        compiler_params=pltpu.CompilerParams(has_side_effects=DATAFLOW))(
        pltpu.with_memory_space_constraint(g, pltpu.HBM), pltpu.with_memory_space_constraint(land, pltpu.HBM))


def partials_wait(started, after):
    n = len(started)

    def body(*refs):
        for i in range(n):
            send_ref, recv_ref, g_ref, land_ref = refs[4 * i:4 * i + 4]
            for cp in _partial_copies(g_ref, land_ref, send_ref, recv_ref, False):
                cp.wait_send()
                cp.wait_recv()

    flat = [a for s in started for a in s]
    bufs = [a for s in started for a in s[2:]]
    outs = pl.pallas_call(
        body, name="partials_wait", out_shape=[pltpu.HBM(b.shape, b.dtype) for b in bufs],
        in_specs=[SEM, SEM, HBM, HBM] * n + [ANY], out_specs=[HBM] * (2 * n),
        input_output_aliases={4 * i + 2 + j: 2 * i + j for i in range(n) for j in range(2)},
        compiler_params=pltpu.CompilerParams(has_side_effects=DATAFLOW))(*flat, after)
    return [(outs[2 * i], outs[2 * i + 1]) for i in range(n)]


def sum_partials(pairs, order):
    n = len(pairs)

    def body(o_ref, *refs):
        j = pl.program_id(0)
        for g_ref, l_ref, f_ref in zip(refs[:n], refs[n:2 * n], refs[2 * n:]):
            @pl.when(j == 0)
            def _():
                f_ref[...] = g_ref[...].astype(F32)

            @pl.when(j > 0)
            def _():
                f_ref[...] += l_ref[...].astype(F32)

    g4 = [g.reshape(g.shape[0], 2, g.shape[1] // 2, g.shape[2]) for g, _ in pairs]
    lands = [l for _, l in pairs]
    return _pcall(body, name="sum_partials", grid=(N_DEV,), prefetch=1,
                  in_specs=[BS((None, None) + g.shape[2:], lambda j, o: (o[0], o[1], 0, 0)) for g in g4]
                  + [BS((None,) + l.shape[1:], lambda j, o: (o[jnp.maximum(j, 1) + 1], 0, 0)) for l in lands],
                  out_specs=[BS(l.shape[1:], lambda j, o: (0, 0)) for l in lands],
                  out_shape=[SDS(l.shape[1:], F32) for l in lands])(order, *g4, *lands)


def pair_share(fs):
    n = len(fs)

    def body(*refs):
        f_refs, o_refs, sems = refs[:n], refs[n:2 * n], refs[2 * n:]
        x, y, c = _place()
        cps = [_rcopy(f, o, sems, i, (x, y, 1 - c)) for i, (f, o) in enumerate(zip(f_refs, o_refs))]
        for cp in cps:
            cp.start()
        for cp in cps:
            cp.wait()

    return _comm_call(body, name="pair_share", n_in=n, n_sems=n, out_shape=[SDS(f.shape, f.dtype) for f in fs])(*fs)


def _small_copies(s_ref, land_ref, send_sems, recv_sems, outgoing):
    x, y, c = _place()
    cps = []
    for k, (px, py, pc) in enumerate(_peers(x, y, c)):
        dst = land_ref.at[4 * x + 2 * y + c] if outgoing else land_ref.at[4 * px + 2 * py + pc]
        cps.append(_rcopy(s_ref, dst, (send_sems, recv_sems), k, (px, py, pc)))
    return cps


def small_start(sm):
    land = lax.empty((N_DEV,) + sm.shape, sm.dtype)

    def body(s_ref, land_ref, send_sems, recv_sems, s_thru, land_thru):
        for cp in _small_copies(s_thru, land_thru, send_sems, recv_sems, True):
            cp.start()

    return pl.pallas_call(
        body, name="small_start",
        out_shape=[pltpu.SemaphoreType.DMA((N_DEV - 1,)), pltpu.SemaphoreType.DMA((N_DEV - 1,)),
                   pltpu.HBM(sm.shape, sm.dtype), pltpu.HBM(land.shape, land.dtype)],
        in_specs=[HBM, HBM], out_specs=[SEM, SEM, HBM, HBM], input_output_aliases={0: 2, 1: 3},
        compiler_params=pltpu.CompilerParams(has_side_effects=DATAFLOW))(
        pltpu.with_memory_space_constraint(sm, pltpu.HBM), pltpu.with_memory_space_constraint(land, pltpu.HBM))


def small_wait(send_sems, recv_sems, sm, land, after):
    def body(send_ref, recv_ref, s_ref, land_ref, after_ref, s_out, land_out):
        for cp in _small_copies(s_ref, land_ref, send_ref, recv_ref, False):
            cp.wait_send()
            cp.wait_recv()

    return pl.pallas_call(
        body, name="small_wait", out_shape=[pltpu.HBM(sm.shape, sm.dtype), pltpu.HBM(land.shape, land.dtype)],
        in_specs=[SEM, SEM, HBM, HBM, ANY], out_specs=[HBM, HBM], input_output_aliases={2: 0, 3: 1},
        compiler_params=pltpu.CompilerParams(has_side_effects=DATAFLOW))(send_sems, recv_sems, sm, land, after)


def sum_small(own, land, mevec):
    n, rows, width = land.shape
    tr = _tile(rows, (184, 8))

    def body(me_ref, own_ref, land_ref, o_ref):
        acc = jnp.zeros((tr, width), F32)
        for s in range(n):
            acc = acc + jnp.where(me_ref[0] == s, own_ref[...], land_ref[s])
        o_ref[...] = acc

    return _pcall(body, name="sum_small", grid=(rows // tr,), prefetch=1,
                  in_specs=[BS((tr, width), lambda i, me: (i, 0)), BS((n, tr, width), lambda i, me: (0, i, 0))],
                  out_specs=BS((tr, width), lambda i, me: (i, 0)), out_shape=SDS((rows, width), F32))(mevec, own, land)


def _to_full(blk, col):
    n, r, c = blk.shape
    return blk.transpose(1, 0, 2).reshape(r, n * c) if col else blk.reshape(n * r, c)


def _dup_cols(w):
    dup = lambda t: jnp.concatenate([t[:, :64], t[:, :64], t[:, 64:], t[:, 64:]], axis=1)
    return jnp.concatenate([w[:, :512], dup(w[:, 512:640]), dup(w[:, 640:768]), w[:, 768:]], axis=1)


def _fold_cols(d):
    fold = lambda t: jnp.concatenate([t[:, 0:64] + t[:, 64:128], t[:, 128:192] + t[:, 192:256]], axis=1)
    return jnp.concatenate([d[:, :512], fold(d[:, 512:768]), fold(d[:, 768:1024]), d[:, 1024:]], axis=1)


def _local_step(x, mem, positions, target, w_in, later, sp, emit):
    gain = lambda n: sp[n].reshape(1, -1)
    half = HEAD_DIM // 2
    inv_freq = 1.0 / (10000.0 ** (jnp.arange(half, dtype=F32) * (2.0 / HEAD_DIM)))
    ang = positions.astype(F32)[:, None] * inv_freq
    cos, sin = jnp.cos(ang), jnp.sin(ang)
    cos128 = jnp.tile(cos, (1, 4))
    sin128 = jnp.concatenate([-sin, sin, -sin, sin], axis=1)
    seg = jnp.arange(128) // HEAD_DIM
    bmat = (seg[:, None] == seg[None, :]).astype(BF16)
    gq128, gk128 = jnp.tile(gain("q_norm"), (1, 2)), jnp.tile(gain("k_norm"), (1, 2))
    sinkcol = jnp.repeat(sp["attn_sinks"].reshape(4, 2), BLK, axis=1).reshape(4, 2 * BLK, 1)
    wsc = sp["gmlp_ws"] * jnp.tril(jnp.ones((BLK, BLK), F32))[None]
    w2 = wsc.reshape(4, 2 * BLK, BLK).astype(MXU_DTYPE)
    w2t = wsc.swapaxes(1, 2).reshape(4, 2 * BLK, BLK).astype(MXU_DTYPE)
    bsl = jnp.repeat(sp["gmlp_bs"].reshape(4, 2, BLK).transpose(0, 2, 1), HEAD_DIM, axis=2)
    cb = sp["ffn_conv_b"].reshape(1, -1)
    w_in_d = _dup_cols(_to_full(w_in, True))[None]

    h1, proj = rms_mm(x, gain("mix_norm"), w_in_d, name="mix_in")
    qr, kr, vb, gu, gvn = mixer_pre(proj, cos128, sin128, gq128, gk128, gain("gmlp_v_norm"), bmat)
    attn, ya = swa_fwd(qr, kr, vb, sinkcol, gain("attn_out_norm"))
    gm, y = gmlp_fwd(gvn, gu, ya, w2, bsl, gain("gmlp_out_norm"))
    wf, cw = later(y)
    w_out, xa_wq, xa_wo, ffn_down = (_to_full(wf[n], False) for n in ("w_out", "xa_wq", "xa_wo", "ffn_down"))
    x1 = mm(y, w_out, res=x, name="mix_out")
    h2, qx = rms_mm(x1, gain("xa_norm"), xa_wq[None], name="xa_q")
    mn, kv = rms_mm(mem, gain("mem_norm"), wf["xa_wkv"], name="xa_kv")
    kn, vbx = mem_pre(kv, gain("xa_k_norm"))
    xo = xattn_fwd(qx, kn, vbx, gain("xa_q_norm"))
    x2 = mm(xo, xa_wo, res=x1, name="xa_out")
    h3, a = rms_mm(x2, gain("ffn_norm"), wf["ffn_up"], name="ffn_up")
    f = convgate_fwd(a, cw, cb)
    dx3, loss_acc = mm_loss(f, ffn_down, x2, target, name="ffn_down_loss")

    by_rows = lambda g: g.reshape(N_CHIPS, g.shape[1] // N_CHIPS, g.shape[2])
    df = mm_nt(dx3, ffn_down[None], name="d_f")
    sent = emit("ffn_down", by_rows(mm_tn(f, dx3, name="g_ffn_down", out_dtype=WIRE_DTYPE)))
    dc, gcw = convgate_bwd(a, df, cw, cb, after=sent)
    da = conv_transpose(dc, cw)
    dx2, dg_ffn = mm_nt_rms_bwd(da, wf["ffn_up"], x2, gain("ffn_norm"), dx3, name="d_x2")
    sent = emit("ffn_up", mm_tn(h3, da, name="g_ffn_up", out_dtype=WIRE_DTYPE, chunks=N_CHIPS))
    dxo = mm_nt(dx2, xa_wo[None], name="d_xo", after=sent)
    sent = emit("xa_wo", by_rows(mm_tn(xo, dx2, name="g_xa_wo", out_dtype=WIRE_DTYPE)))
    dqx, dkn, dvx, dg_xq = xattn_bwd(qx, dxo, kn, vbx, gain("xa_q_norm"), after=sent)
    dx1, dg_xa = mm_nt_rms_bwd(dqx, xa_wq[None], x1, gain("xa_norm"), dx2, name="d_x1")
    sent = emit("xa_wq", by_rows(mm_tn(h2, dqx, name="g_xa_wq", out_dtype=WIRE_DTYPE)))
    dkv, dg_xk = mem_bwd(kv, dkn, dvx, gain("xa_k_norm"), after=sent)
    _, dg_mem = mm_nt_rms_bwd(dkv, wf["xa_wkv"], mem, gain("mem_norm"), jnp.zeros_like(mem), name="d_mem")
    sent = emit("xa_wkv", mm_tn(mn, dkv, name="g_xa_wkv", out_dtype=WIRE_DTYPE, chunks=N_CHIPS))
    dattn, dgm, dg_y = mm_nt_post_bwd(dx1, w_out[None], attn, gm, gain("attn_out_norm"), gain("gmlp_out_norm"),
                                      name="d_mix_out", after=sent)
    sent = emit("w_out", by_rows(mm_tn(y, dx1, name="g_w_out", out_dtype=WIRE_DTYPE)))
    dgu, dgvn, dws, dbl = gmlp_bwd(dgm, gvn, gu, w2, w2t, bsl, after=sent)
    dqr, dkr, dvb, dsk = swa_bwd(qr, kr, vb, sinkcol, dattn)
    dproj, dgq, dgk, dg_gvn = mixer_pre_bwd(proj, cos128, sin128, gq128, gk128, gain("gmlp_v_norm"), bmat,
                                            dqr, dkr, dvb, dgu, dgvn)
    g_in = _fold_cols(mm_tn(h1, dproj, name="g_w_in", out_dtype=F32)[0])
    sent = emit("w_in", g_in.reshape(1024, N_CHIPS, 448).transpose(1, 0, 2).astype(WIRE_DTYPE))
    grad_x, dg_mix = mm_nt_rms_bwd(dproj, w_in_d, x, gain("mix_norm"), dx1, name="d_x", after=sent)
    packed = pack_small(dg_mix, dgq, dgk, dsk, dg_gvn, dg_y, dg_xa, dg_mem, dg_xq, dg_xk, dg_ffn, gcw, dbl, dws)
    return loss_acc, grad_x, packed


def _gather_step(w, chipvec):
    slots = cast_shards([w[n][0] for n in BIG_NAMES], w["ffn_conv"][0], chipvec)
    send_a, recv_a, first, _ = gather_start(slots[:1])
    send_b, recv_b, rest, rest_started = gather_start(slots[1:])
    w_in, = gather_wait(send_a, recv_a, first, rest_started)

    def later(after):
        got = gather_wait(send_b, recv_b, rest, after)
        return dict(zip(BIG_NAMES[1:], got[:-1])), _to_full(got[-1], True)

    return w_in, later


def _reduce_update(started, packed, w, m, v, chipvec, cvec, order):
    small_sent = small_start(packed)
    own = sum_partials(partials_wait([started[n] for n in BIG_NAMES], small_sent[2]), order)
    other = pair_share(own)
    res = [{}, {}, {}, {}]
    for n, g_own, g_other in zip(BIG_NAMES, own, other):
        for d, o in zip(res, adamw_matrix(w[n], m[n], v[n], g_own, g_other, cvec, name="adamw_" + n)):
            d[n] = o
    mevec = (2 * order[0:1] + order[1:2]).astype(jnp.int32)
    small_sum = sum_small(*small_wait(*small_sent, res[3][BIG_NAMES[-1]]), mevec)
    for d, outs in zip(res, adamw_small(small_sum, w, m, v, chipvec)):
        d.update(zip(SMALL, outs))
    return res


def kernel(x, mem, positions, mix_norm, w_in, q_norm, k_norm, attn_sinks, gmlp_v_norm, gmlp_ws, gmlp_bs, attn_out_norm, gmlp_out_norm, w_out, xa_norm, mem_norm, xa_wq, xa_wkv, xa_q_norm, xa_k_norm, xa_wo, ffn_norm, ffn_up, ffn_conv, ffn_conv_b, ffn_down, loss_target, m_mix_norm, m_w_in, m_q_norm, m_k_norm, m_attn_sinks, m_gmlp_v_norm, m_gmlp_ws, m_gmlp_bs, m_attn_out_norm, m_gmlp_out_norm, m_w_out, m_xa_norm, m_mem_norm, m_xa_wq, m_xa_wkv, m_xa_q_norm, m_xa_k_norm, m_xa_wo, m_ffn_norm, m_ffn_up, m_ffn_conv, m_ffn_conv_b, m_ffn_down, v_mix_norm, v_w_in, v_q_norm, v_k_norm, v_attn_sinks, v_gmlp_v_norm, v_gmlp_ws, v_gmlp_bs, v_attn_out_norm, v_gmlp_out_norm, v_w_out, v_xa_norm, v_mem_norm, v_xa_wq, v_xa_wkv, v_xa_q_norm, v_xa_k_norm, v_xa_wo, v_ffn_norm, v_ffn_up, v_ffn_conv, v_ffn_conv_b, v_ffn_down):
    w = dict(mix_norm=mix_norm, w_in=w_in, q_norm=q_norm, k_norm=k_norm, attn_sinks=attn_sinks, gmlp_v_norm=gmlp_v_norm, gmlp_ws=gmlp_ws, gmlp_bs=gmlp_bs, attn_out_norm=attn_out_norm, gmlp_out_norm=gmlp_out_norm, w_out=w_out, xa_norm=xa_norm, mem_norm=mem_norm, xa_wq=xa_wq, xa_wkv=xa_wkv, xa_q_norm=xa_q_norm, xa_k_norm=xa_k_norm, xa_wo=xa_wo, ffn_norm=ffn_norm, ffn_up=ffn_up, ffn_conv=ffn_conv, ffn_conv_b=ffn_conv_b, ffn_down=ffn_down)
    m = dict(mix_norm=m_mix_norm, w_in=m_w_in, q_norm=m_q_norm, k_norm=m_k_norm, attn_sinks=m_attn_sinks, gmlp_v_norm=m_gmlp_v_norm, gmlp_ws=m_gmlp_ws, gmlp_bs=m_gmlp_bs, attn_out_norm=m_attn_out_norm, gmlp_out_norm=m_gmlp_out_norm, w_out=m_w_out, xa_norm=m_xa_norm, mem_norm=m_mem_norm, xa_wq=m_xa_wq, xa_wkv=m_xa_wkv, xa_q_norm=m_xa_q_norm, xa_k_norm=m_xa_k_norm, xa_wo=m_xa_wo, ffn_norm=m_ffn_norm, ffn_up=m_ffn_up, ffn_conv=m_ffn_conv, ffn_conv_b=m_ffn_conv_b, ffn_down=m_ffn_down)
    v = dict(mix_norm=v_mix_norm, w_in=v_w_in, q_norm=v_q_norm, k_norm=v_k_norm, attn_sinks=v_attn_sinks, gmlp_v_norm=v_gmlp_v_norm, gmlp_ws=v_gmlp_ws, gmlp_bs=v_gmlp_bs, attn_out_norm=v_attn_out_norm, gmlp_out_norm=v_gmlp_out_norm, w_out=v_w_out, xa_norm=v_xa_norm, mem_norm=v_mem_norm, xa_wq=v_xa_wq, xa_wkv=v_xa_wkv, xa_q_norm=v_xa_q_norm, xa_k_norm=v_xa_k_norm, xa_wo=v_xa_wo, ffn_norm=v_ffn_norm, ffn_up=v_ffn_up, ffn_conv=v_ffn_conv, ffn_conv_b=v_ffn_conv_b, ffn_down=v_ffn_down)
    ix, iy, ic = lax.axis_index("x"), lax.axis_index("y"), lax.axis_index("c")
    chip = 2 * ix + iy
    chipvec = chip.astype(jnp.int32).reshape(1)
    cvec = ic.astype(jnp.int32).reshape(1)
    order = jnp.stack([chip, ic] + [4 * px + 2 * py + pc for px, py, pc in _peers(ix, iy, ic)]).astype(jnp.int32)

    w_in_all, later = _gather_step(w, chipvec)
    sp = {n: w[n][0] for n in SMALL if n != "ffn_conv"}
    started = {}

    def emit(name, g):
        *started[name], token = partials_start(g, name="partials_start_" + name)
        return token

    loss_acc, grad_x, packed = _local_step(x[0], mem[0], positions[0], loss_target[0], w_in_all, later, sp, emit)
    grads, delta, new_m, new_v = _reduce_update(started, packed, w, m, v, chipvec, cvec, order)
    loss = lax.psum(loss_acc[0, 0], ("x", "y", "c"))
    ordered = lambda d: [d[n] for n in WEIGHTS]
    return (loss, grad_x[None], *ordered(grads), *ordered(delta), *ordered(new_m), *ordered(new_v))
```

```python
import math

import jax
import jax.numpy as jnp
from jax import lax
from jax.experimental import pallas as pl
from jax.experimental.pallas import tpu as pltpu

F32 = jnp.float32
BF16 = jnp.bfloat16
MXU_DTYPE = jnp.bfloat16
WIRE_DTYPE = jnp.bfloat16
EPS = 1e-6
VMEM_LIMIT_V7X = 56 * 1024 * 1024

D_MODEL = 1024
HEAD_DIM = 64
BLK = 128
XA_HEADS = 4
XA_DH = 256
MEM_LEN = 256
D_FF = 2816
IN_COLS_DUP = 2048
N_CHIPS = 4
N_DEV = 8

ADAM_LR = 0.001
ADAM_B1 = 0.9
ADAM_B2 = 0.999
ADAM_EPS = 1e-08
ADAM_WD = 0.01
ADAM_STEP = 10

NT = (((1,), (1,)), ((), ()))
TN = (((0,), (0,)), ((), ()))
NN = (((1,), (0,)), ((), ()))
MINF = float(jnp.finfo(jnp.float32).min)
GELU_K0 = math.sqrt(2.0 / math.pi)
GELU_K1 = 0.044715

BS = pl.BlockSpec
SDS = jax.ShapeDtypeStruct
ANY = pl.BlockSpec(memory_space=pl.ANY)
MESH = pl.DeviceIdType.MESH


def _dot(a, b, dims=NN):
    return lax.dot_general(a.astype(MXU_DTYPE), b.astype(MXU_DTYPE), dims, preferred_element_type=F32)


def _segsum(x, bmat):
    hi = x.astype(BF16)
    lo = (x - hi.astype(F32)).astype(BF16)
    return (jnp.dot(hi, bmat, preferred_element_type=F32) + jnp.dot(lo, bmat, preferred_element_type=F32))


def _gelu(x):
    return 0.5 * x * (1.0 + jnp.tanh(GELU_K0 * (x + GELU_K1 * x * x * x)))


def _gelu_grad(x):
    t = jnp.tanh(GELU_K0 * (x + GELU_K1 * x * x * x))
    return 0.5 * (1.0 + t) + 0.5 * x * (1.0 - t * t) * GELU_K0 * (1.0 + 3.0 * GELU_K1 * x * x)


def _rms(x):
    return lax.rsqrt(jnp.mean(x * x, axis=-1, keepdims=True) + EPS)


def _rms_bwd(dy, x, g, r):
    dyg = dy * g
    dx = r * dyg - x * (r * r * r) * jnp.mean(dyg * x, axis=-1, keepdims=True)
    return dx, dy * x * r


def _pcall(body, *, name, grid, in_specs, out_specs, out_shape, scratch=(), prefetch=0, after=None):
    params = pltpu.CompilerParams(dimension_semantics=("arbitrary",) * len(grid), vmem_limit_bytes=VMEM_LIMIT_V7X)
    in_specs = list(in_specs)
    kernel_fn = body
    if after is not None:
        n_in = prefetch + len(in_specs)
        in_specs.append(ANY)

        def kernel_fn(*refs):
            return body(*refs[:n_in], *refs[n_in + 1:])

    if prefetch:
        spec = pltpu.PrefetchScalarGridSpec(num_scalar_prefetch=prefetch, grid=grid, in_specs=in_specs,
                                            out_specs=out_specs, scratch_shapes=list(scratch))
        call = pl.pallas_call(kernel_fn, name=name, grid_spec=spec, out_shape=out_shape, compiler_params=params)
    else:
        call = pl.pallas_call(kernel_fn, name=name, grid=grid, in_specs=in_specs, out_specs=out_specs,
                              out_shape=out_shape, scratch_shapes=list(scratch), compiler_params=params)
    return call if after is None else (lambda *args: call(*args, after))


def _tile(n, prefs):
    for p in prefs:
        if p <= n and n % p == 0:
            return p
    return n


def _resident(shape):
    return pl.BlockSpec(shape, lambda *_: (0,) * len(shape), pipeline_mode=pl.Buffered(1))


def _acc_rows(ref, row, val):
    ref[row:row + 1, :] += jnp.sum(val, axis=0, keepdims=True)


def rms_mm(x, g, w3, *, name, tm=1024):
    M, K = x.shape
    Q, _, C = w3.shape
    tm = _tile(M, (tm, 256))

    def body(x_ref, g_ref, w_ref, h_ref, o_ref):
        @pl.when(pl.program_id(1) == 0)
        def _():
            xv = x_ref[...]
            h_ref[...] = (xv * _rms(xv) * g_ref[...]).astype(h_ref.dtype)

        o_ref[...] = _dot(h_ref[...], w_ref[pl.program_id(1)])

    return _pcall(body, name=name, grid=(M // tm, Q),
                  in_specs=[BS((tm, K), lambda i, j: (i, 0)), BS((1, K), lambda i, j: (0, 0)),
                            _resident((Q, K, C))],
                  out_specs=[BS((tm, K), lambda i, j: (i, 0)), BS((tm, C), lambda i, j: (i, j))],
                  out_shape=[SDS((M, K), MXU_DTYPE), SDS((M, Q * C), F32)])(x, g, w3)


def mm(a, w, *, name, res):
    M, K = a.shape
    N = w.shape[1]
    tm = _tile(M, (1024, 256))

    def body(a_ref, w_ref, r_ref, o_ref):
        o_ref[...] = _dot(a_ref[...], w_ref[...]) + r_ref[...]

    return _pcall(body, name=name, grid=(M // tm,),
                  in_specs=[BS((tm, K), lambda i: (i, 0)), _resident((K, N)), BS((tm, N), lambda i: (i, 0))],
                  out_specs=BS((tm, N), lambda i: (i, 0)), out_shape=SDS((M, N), F32))(a, w, res)


def _nt_chunks(a_ref, w_ref):
    q_n, _, kc = w_ref.shape
    acc = _dot(a_ref[:, 0:kc], w_ref[0], NT)
    for q in range(1, q_n):
        acc = acc + _dot(a_ref[:, q * kc:(q + 1) * kc], w_ref[q], NT)
    return acc


def mm_nt(a, w3, *, name, after=None):
    M = a.shape[0]
    Q, N, Kc = w3.shape
    tm = _tile(M, (1024, 256))

    def body(a_ref, w_ref, o_ref):
        o_ref[...] = _nt_chunks(a_ref, w_ref)

    return _pcall(body, name=name, grid=(M // tm,), after=after,
                  in_specs=[BS((tm, Q * Kc), lambda i: (i, 0)), _resident((Q, N, Kc))],
                  out_specs=BS((tm, N), lambda i: (i, 0)), out_shape=SDS((M, N), F32))(a, w3)


def mm_nt_rms_bwd(a, w3, x, g, dres, *, name, tm=512, after=None):
    M = a.shape[0]
    Q, N, Kc = w3.shape
    tm = _tile(M, (tm, 256))

    def body(a_ref, w_ref, x_ref, g_ref, dr_ref, dx_ref, dg_ref):
        @pl.when(pl.program_id(0) == 0)
        def _():
            dg_ref[...] = jnp.zeros_like(dg_ref)

        xv = x_ref[...]
        dx, dgc = _rms_bwd(_nt_chunks(a_ref, w_ref), xv, g_ref[...], _rms(xv))
        dx_ref[...] = dr_ref[...] + dx
        _acc_rows(dg_ref, 0, dgc)

    row = BS((tm, N), lambda i: (i, 0))
    return _pcall(body, name=name, grid=(M // tm,), after=after,
                  in_specs=[BS((tm, Q * Kc), lambda i: (i, 0)), _resident((Q, N, Kc)), row,
                            BS((1, N), lambda i: (0, 0)), row],
                  out_specs=[row, BS((8, N), lambda i: (0, 0))],
                  out_shape=[SDS((M, N), F32), SDS((8, N), F32)])(a, w3, x, g, dres)


def mm_nt_post_bwd(a, w3, attn, gm, gao, ggo, *, name, after=None):
    M = a.shape[0]
    Q, N, Kc = w3.shape
    tm = _tile(M, (512, 256))
    hw = N // 2

    def body(a_ref, w_ref, at_ref, gm_ref, gao_ref, ggo_ref, da_ref, dgm_ref, dg_ref):
        @pl.when(pl.program_id(0) == 0)
        def _():
            dg_ref[...] = jnp.zeros_like(dg_ref)

        dy = _nt_chunks(a_ref, w_ref)
        av, gmv = at_ref[...], gm_ref[...]
        da, dga = _rms_bwd(dy[:, :hw], av, gao_ref[...], _rms(av))
        dgm, dgg = _rms_bwd(dy[:, hw:], gmv, ggo_ref[...], _rms(gmv))
        da_ref[...] = da
        dgm_ref[...] = dgm
        dg_ref[0:1, :hw] += jnp.sum(dga, axis=0, keepdims=True)
        dg_ref[0:1, hw:] += jnp.sum(dgg, axis=0, keepdims=True)

    half = BS((tm, hw), lambda i: (i, 0))
    const = lambda r, w: BS((r, w), lambda i: (0, 0))
    return _pcall(body, name=name, grid=(M // tm,), after=after,
                  in_specs=[BS((tm, Q * Kc), lambda i: (i, 0)), _resident((Q, N, Kc)), half, half,
                            const(1, hw), const(1, hw)],
                  out_specs=[half, half, const(8, N)],
                  out_shape=[SDS((M, hw), F32), SDS((M, hw), F32), SDS((8, N), F32)])(a, w3, attn, gm, gao, ggo)


def mm_tn(a, b, *, name, out_dtype, chunks=1):
    M, K = a.shape
    N = b.shape[1]
    C = N // chunks
    tm = _tile(M, (1024, 256))
    tk = _tile(K, (1408, 1024, 512))
    tn = _tile(C, (1408, 1024, 512))
    per = C // tn
    nm = M // tm

    def body(a_ref, b_ref, o_ref, acc):
        m = pl.program_id(2)

        @pl.when(m == 0)
        def _():
            acc[...] = jnp.zeros_like(acc)

        acc[...] += _dot(a_ref[...], b_ref[...], TN)

        @pl.when(m == nm - 1)
        def _():
            o_ref[...] = acc[...].astype(o_ref.dtype)

    return _pcall(body, name=name, grid=(K // tk, N // tn, nm),
                  in_specs=[BS((tm, tk), lambda k, n, m: (m, k)), BS((tm, tn), lambda k, n, m: (m, n))],
                  out_specs=BS((None, tk, tn), lambda k, n, m: (n // per, k, n % per)),
                  out_shape=SDS((chunks, K, C), out_dtype), scratch=[pltpu.VMEM((tk, tn), F32)])(a, b)


def _lane(shape):
    return lax.broadcasted_iota(jnp.int32, shape, 1)


def _head_means(slabs, bmat):
    tm = slabs[0].shape[0]
    means = _segsum(jnp.concatenate(slabs, axis=0), bmat) * (1.0 / HEAD_DIM)
    return [means[i * tm:(i + 1) * tm] for i in range(len(slabs))]


def _half_swap(x, first):
    return jnp.where(first, pltpu.roll(x, 96, 1), pltpu.roll(x, 32, 1))


def mixer_pre(proj, cos, sin, gq, gk, gvn, bmat):
    S = proj.shape[0]
    tm = _tile(S, (256,))

    def body(p_ref, c_ref, s_ref, gq_ref, gk_ref, gvn_ref, b_ref, qr_ref, kr_ref, vb_ref, gu_ref, gvo_ref):
        cos_v, sin_v, bm = c_ref[...], s_ref[...], b_ref[...]
        first = (_lane((tm, 128)) & 63) < 32
        slabs = [p_ref[:, s * 128:(s + 1) * 128] for s in range(6)]
        for s, (slab, ms) in enumerate(zip(slabs, _head_means([x * x for x in slabs], bm))):
            qn = slab * lax.rsqrt(ms + EPS) * (gq_ref[...] if s < 4 else gk_ref[...])
            out = qn * cos_v + _half_swap(qn, first) * sin_v
            if s < 4:
                qr_ref[:, s * 128:(s + 1) * 128] = out.astype(qr_ref.dtype)
            else:
                kr_ref[:, (s - 4) * 128:(s - 3) * 128] = out.astype(kr_ref.dtype)
        vb_ref[...] = p_ref[:, 768:1024].astype(vb_ref.dtype)
        gu_ref[...] = _gelu(p_ref[:, 1024:1536])
        gv = _gelu(p_ref[:, 1536:2048])
        gvo_ref[...] = (gv * _rms(gv) * gvn_ref[...]).astype(gvo_ref.dtype)

    row = lambda w: BS((tm, w), lambda i: (i, 0))
    const = lambda r, w: BS((r, w), lambda i: (0, 0))
    return _pcall(body, name="mixer_pre", grid=(S // tm,),
                  in_specs=[row(IN_COLS_DUP), row(128), row(128), const(1, 128), const(1, 128), const(1, 512),
                            const(128, 128)],
                  out_specs=[row(512), row(256), row(256), row(512), row(512)],
                  out_shape=[SDS((S, 512), MXU_DTYPE), SDS((S, 256), MXU_DTYPE), SDS((S, 256), MXU_DTYPE),
                             SDS((S, 512), F32), SDS((S, 512), MXU_DTYPE)])(proj, cos, sin, gq, gk, gvn, bmat)


def _by_head(x2, lo):
    z = jnp.zeros((BLK, 128), x2.dtype)
    parts = []
    for s in range(2):
        xs = x2[:, s * 128:(s + 1) * 128]
        parts += [jnp.where(lo, xs, z), jnp.where(lo, z, xs)]
    return jnp.concatenate(parts, axis=0)


def _from_heads(o4, lo):
    return jnp.concatenate([jnp.where(lo, o4[0:BLK], o4[BLK:2 * BLK]),
                            jnp.where(lo, o4[2 * BLK:3 * BLK], o4[3 * BLK:])], axis=1)


def _swa_probs(q2, kd, sink, n, lo):
    qp = _by_head(q2, lo)
    sc = _dot(qp, kd, NT) * (1.0 / math.sqrt(HEAD_DIM))
    r_i = lax.broadcasted_iota(jnp.int32, (4 * BLK, 2 * BLK), 0)
    k_j = lax.broadcasted_iota(jnp.int32, (4 * BLK, 2 * BLK), 1)
    diff = (r_i & (BLK - 1)) + BLK - k_j
    mask = (diff >= 0) & (diff < BLK) & ((k_j >= BLK) | (n > 0))
    sc = jnp.where(mask, sc, MINF)
    m = jnp.maximum(jnp.max(sc, axis=1, keepdims=True), sink)
    p = jnp.exp(sc - m)
    es = jnp.exp(sink - m)
    inv = 1.0 / (jnp.sum(p, axis=1, keepdims=True) + es)
    return qp, p * inv, es * inv


def swa_fwd(qr, kr, vb, sinkcol, gao):
    S = qr.shape[0]
    nb = S // BLK

    def body(q_ref, kc_ref, kp_ref, vc_ref, vp_ref, sk_ref, g_ref, o_ref, ya_ref):
        n = pl.program_id(0)
        lo = _lane((BLK, 128)) < 64
        for h in range(2):
            hs, qs = slice(h * 128, (h + 1) * 128), slice(h * 256, (h + 1) * 256)
            kd = jnp.concatenate([kp_ref[:, hs], kc_ref[:, hs]], axis=0)
            vd = jnp.concatenate([vp_ref[:, hs], vc_ref[:, hs]], axis=0)
            sink = jnp.concatenate([sk_ref[2 * h], sk_ref[2 * h + 1]], axis=0)
            _, p, _ = _swa_probs(q_ref[:, qs], kd, sink, n, lo)
            o_ref[:, qs] = _from_heads(_dot(p, vd), lo)
        a = o_ref[...]
        ya_ref[...] = (a * _rms(a) * g_ref[...]).astype(ya_ref.dtype)

    cur = lambda w: BS((BLK, w), lambda n: (n, 0))
    prev = lambda w: BS((BLK, w), lambda n: (jnp.maximum(n - 1, 0), 0))
    return _pcall(body, name="swa_fwd", grid=(nb,),
                  in_specs=[cur(512), cur(256), prev(256), cur(256), prev(256),
                            BS((4, 2 * BLK, 1), lambda n: (0, 0, 0)), BS((1, 512), lambda n: (0, 0))],
                  out_specs=[cur(512), cur(512)],
                  out_shape=[SDS((S, 512), F32), SDS((S, 512), MXU_DTYPE)])(qr, kr, kr, vb, vb, sinkcol, gao)


def gmlp_fwd(gvn, gu, ya, w2, bsl, ggo):
    S = gvn.shape[0]

    def body(gvn_ref, gu_ref, ya_ref, w2_ref, bsl_ref, g_ref, gm_ref, y_ref):
        lo = _lane((BLK, 128)) < 64
        for j in range(4):
            sl = slice(j * 128, (j + 1) * 128)
            m2 = _dot(w2_ref[j], gvn_ref[:, sl])
            mixed = jnp.where(lo, m2[:BLK], m2[BLK:]) + bsl_ref[j]
            gm_ref[:, sl] = gu_ref[:, sl] * mixed
        gm = gm_ref[...]
        y_ref[:, :512] = ya_ref[...]
        y_ref[:, 512:] = (gm * _rms(gm) * g_ref[...]).astype(y_ref.dtype)

    row = lambda w: BS((BLK, w), lambda n: (n, 0))
    return _pcall(body, name="gmlp_fwd", grid=(S // BLK,),
                  in_specs=[row(512), row(512), row(512), BS((4, 2 * BLK, BLK), lambda n: (0, 0, 0)),
                            BS((4, BLK, 128), lambda n: (0, 0, 0)), BS((1, 512), lambda n: (0, 0))],
                  out_specs=[row(512), row(1024)],
                  out_shape=[SDS((S, 512), F32), SDS((S, 1024), MXU_DTYPE)])(gvn, gu, ya, w2, bsl, ggo)


def mem_pre(kv, gxk):
    def body(kv_ref, g_ref, kn_ref, vb_ref):
        for h in range(XA_HEADS):
            sl = slice(h * XA_DH, (h + 1) * XA_DH)
            k = kv_ref[:, sl]
            kn_ref[:, sl] = (k * _rms(k) * g_ref[...]).astype(kn_ref.dtype)
        vb_ref[...] = kv_ref[:, 1024:2048].astype(vb_ref.dtype)

    full = lambda r, w: BS((r, w), lambda i: (0, 0))
    return _pcall(body, name="mem_pre", grid=(1,), in_specs=[full(MEM_LEN, 2048), full(1, XA_DH)],
                  out_specs=[full(MEM_LEN, 1024), full(MEM_LEN, 1024)],
                  out_shape=[SDS((MEM_LEN, 1024), MXU_DTYPE), SDS((MEM_LEN, 1024), MXU_DTYPE)])(kv, gxk)


def _xa_probs(qh, g, kn_h):
    r = _rms(qh)
    qn = qh * r * g
    s = _dot(qn, kn_h, NT) * (1.0 / math.sqrt(XA_DH))
    p = jnp.exp(s - jnp.max(s, axis=1, keepdims=True))
    return r, qn, p * (1.0 / jnp.sum(p, axis=1, keepdims=True))


def xattn_fwd(qx, kn, vb, gxq):
    S = qx.shape[0]
    tm = _tile(S, (512, 256))

    def body(q_ref, kn_ref, vb_ref, g_ref, o_ref):
        for h in range(XA_HEADS):
            sl = slice(h * XA_DH, (h + 1) * XA_DH)
            _, _, p = _xa_probs(q_ref[:, sl], g_ref[...], kn_ref[:, sl])
            o_ref[:, sl] = _dot(p, vb_ref[:, sl]).astype(o_ref.dtype)

    full = lambda r, w: BS((r, w), lambda i: (0, 0))
    return _pcall(body, name="xattn_fwd", grid=(S // tm,),
                  in_specs=[BS((tm, 1024), lambda i: (i, 0)), full(MEM_LEN, 1024), full(MEM_LEN, 1024), full(1, XA_DH)],
                  out_specs=BS((tm, 1024), lambda i: (i, 0)), out_shape=SDS((S, 1024), MXU_DTYPE))(qx, kn, vb, gxq)


CONV_COLS = 1408


def _conv_taps(a_ref, halo_ref, w_ref, b_ref, cols, first_tile):
    a = a_ref[:, cols]
    row = lax.broadcasted_iota(jnp.int32, a.shape, 0)
    h6 = jnp.where(first_tile, 0.0, halo_ref[6:7, cols])
    h7 = jnp.where(first_tile, 0.0, halo_ref[7:8, cols])
    a1 = jnp.where(row == 0, h7, pltpu.roll(a, 1, 0))
    a2 = jnp.where(row == 0, h6, jnp.where(row == 1, h7, pltpu.roll(a, 2, 0)))
    c = w_ref[2:3, cols] * a + w_ref[1:2, cols] * a1 + w_ref[0:1, cols] * a2 + b_ref[:, cols]
    return c, (a2, a1, a)


def _conv_specs(tm):
    halo_blocks = tm // 8
    return [BS((tm, D_FF), lambda i: (i, 0)), BS((tm, D_FF), lambda i: (i, 1)),
            BS((8, D_FF), lambda i: (jnp.maximum(i * halo_blocks - 1, 0), 0)),
            BS((8, D_FF), lambda i: (jnp.maximum(i * halo_blocks - 1, 0), 1)),
            BS((3, D_FF), lambda i: (0, 0)), BS((3, D_FF), lambda i: (0, 1)),
            BS((1, D_FF), lambda i: (0, 0)), BS((1, D_FF), lambda i: (0, 1))]


def convgate_down_loss(a, cw, cb, w, res, target):
    S = a.shape[0]
    N = w.shape[1]
    tm = _tile(S, (256,))

    def body(ag_ref, au_ref, hg_ref, hu_ref, wg_ref, wu_ref, bg_ref, bu_ref, w_ref, r_ref, t_ref, f_ref, d_ref,
             l_ref):
        first_tile = pl.program_id(0) == 0

        @pl.when(first_tile)
        def _():
            l_ref[...] = jnp.zeros_like(l_ref)

        for c0 in range(0, D_FF, CONV_COLS):
            cols = slice(c0, c0 + CONV_COLS)
            cg, _ = _conv_taps(ag_ref, hg_ref, wg_ref, bg_ref, cols, first_tile)
            cu, _ = _conv_taps(au_ref, hu_ref, wu_ref, bu_ref, cols, first_tile)
            f_ref[:, cols] = (_gelu(cg) * cu).astype(f_ref.dtype)
        e = _dot(f_ref[...], w_ref[...]) + r_ref[...] - t_ref[...]
        d_ref[...] = e * (1.0 / N)
        l_ref[...] += jnp.sum(e * e) * (0.5 / N)

    row_n = BS((tm, N), lambda i: (i, 0))
    return _pcall(body, name="convgate_down_loss", grid=(S // tm,),
                  in_specs=_conv_specs(tm) + [_resident((D_FF, N)), row_n, row_n],
                  out_specs=[BS((tm, D_FF), lambda i: (i, 0)), row_n, BS((8, 128), lambda i: (0, 0))],
                  out_shape=[SDS((S, D_FF), MXU_DTYPE), SDS((S, N), F32), SDS((8, 128), F32)])(
        a, a, a, a, cw, cw, cb, cb, w, res, target)


def convgate_bwd(a, dx3, w3, cw, cb, after=None):
    S = a.shape[0]
    tm = _tile(S, (256,))

    def body(ag_ref, au_ref, hg_ref, hu_ref, wg_ref, wu_ref, bg_ref, bu_ref, dx_ref, wd_ref, dc_ref, gw_ref, df_ref):
        first_tile = pl.program_id(0) == 0

        @pl.when(first_tile)
        def _():
            gw_ref[...] = jnp.zeros_like(gw_ref)

        df_ref[...] = _nt_chunks(dx_ref, wd_ref)
        for c0 in range(0, D_FF, CONV_COLS):
            cols, ucols = slice(c0, c0 + CONV_COLS), slice(D_FF + c0, D_FF + c0 + CONV_COLS)
            cg, g_taps = _conv_taps(ag_ref, hg_ref, wg_ref, bg_ref, cols, first_tile)
            cu, u_taps = _conv_taps(au_ref, hu_ref, wu_ref, bu_ref, cols, first_tile)
            df_v = df_ref[:, cols]
            dcg = df_v * cu * _gelu_grad(cg)
            dcu = df_v * _gelu(cg)
            dc_ref[:, cols] = dcg
            dc_ref[:, ucols] = dcu
            for col, dcv, taps in ((cols, dcg, g_taps), (ucols, dcu, u_taps)):
                for j in range(3):
                    gw_ref[j:j + 1, col] += jnp.sum(dcv * taps[j], axis=0, keepdims=True)
                gw_ref[3:4, col] += jnp.sum(dcv, axis=0, keepdims=True)

    return _pcall(body, name="convgate_bwd", grid=(S // tm,), after=after,
                  in_specs=_conv_specs(tm) + [BS((tm, dx3.shape[1]), lambda i: (i, 0)), _resident(w3.shape)],
                  out_specs=[BS((tm, 2 * D_FF), lambda i: (i, 0)), BS((8, 2 * D_FF), lambda i: (0, 0))],
                  out_shape=[SDS((S, 2 * D_FF), F32), SDS((8, 2 * D_FF), F32)],
                  scratch=[pltpu.VMEM((tm, D_FF), F32)])(a, a, a, a, cw, cw, cb, cb, dx3, w3)


def conv_transpose_rms_bwd(dc, cw, w3, x, g, dres):
    S, C = dc.shape
    Q, N, Kc = w3.shape
    tm = _tile(S, (256,))
    nt = S // tm
    halo_blocks = tm // 8

    def body(dc_ref, halo_ref, cw_ref, w_ref, x_ref, g_ref, dr_ref, da_ref, dx_ref, dg_ref):
        @pl.when(pl.program_id(0) == 0)
        def _():
            dg_ref[...] = jnp.zeros_like(dg_ref)

        last_tile = pl.program_id(0) == nt - 1
        row = lax.broadcasted_iota(jnp.int32, (tm, CONV_COLS), 0)
        for c0 in range(0, C, CONV_COLS):
            cols = slice(c0, c0 + CONV_COLS)
            h0 = jnp.where(last_tile, 0.0, halo_ref[0:1, cols])
            h1 = jnp.where(last_tile, 0.0, halo_ref[1:2, cols])
            dc_v = dc_ref[:, cols]
            n1 = jnp.where(row == tm - 1, h0, pltpu.roll(dc_v, tm - 1, 0))
            n2 = jnp.where(row == tm - 1, h1, jnp.where(row == tm - 2, h0, pltpu.roll(dc_v, tm - 2, 0)))
            da_ref[:, cols] = (cw_ref[2:3, cols] * dc_v + cw_ref[1:2, cols] * n1
                               + cw_ref[0:1, cols] * n2).astype(da_ref.dtype)
        xv = x_ref[...]
        dx, dgc = _rms_bwd(_nt_chunks(da_ref, w_ref), xv, g_ref[...], _rms(xv))
        dx_ref[...] = dr_ref[...] + dx
        _acc_rows(dg_ref, 0, dgc)

    row_n = BS((tm, N), lambda i: (i, 0))
    return _pcall(body, name="conv_transpose_rms_bwd", grid=(nt,),
                  in_specs=[BS((tm, C), lambda i: (i, 0)),
                            BS((8, C), lambda i: (jnp.minimum((i + 1) * halo_blocks, S // 8 - 1), 0)),
                            BS((3, C), lambda i: (0, 0)), _resident((Q, N, Kc)), row_n, BS((1, N), lambda i: (0, 0)),
                            row_n],
                  out_specs=[BS((tm, C), lambda i: (i, 0)), row_n, BS((8, N), lambda i: (0, 0))],
                  out_shape=[SDS((S, C), MXU_DTYPE), SDS((S, N), F32), SDS((8, N), F32)])(dc, dc, cw, w3, x, g, dres)


def xattn_bwd(qx, dxo, kn, vb, gxq, after=None):
    S = qx.shape[0]
    tm = _tile(S, (512, 256))

    def body(q_ref, do_ref, kn_ref, vb_ref, g_ref, dq_ref, dkn_ref, dv_ref, dg_ref):
        @pl.when(pl.program_id(0) == 0)
        def _():
            dkn_ref[...] = jnp.zeros_like(dkn_ref)
            dv_ref[...] = jnp.zeros_like(dv_ref)
            dg_ref[...] = jnp.zeros_like(dg_ref)

        g = g_ref[...]
        for h in range(XA_HEADS):
            sl = slice(h * XA_DH, (h + 1) * XA_DH)
            qh, do = q_ref[:, sl], do_ref[:, sl]
            r, qn, p = _xa_probs(qh, g, kn_ref[:, sl])
            dp = _dot(do, vb_ref[:, sl], NT)
            ds = p * (dp - jnp.sum(dp * p, axis=1, keepdims=True)) * (1.0 / math.sqrt(XA_DH))
            dqn = _dot(ds, kn_ref[:, sl])
            dkn_ref[:, sl] += _dot(ds, qn, TN)
            dv_ref[:, sl] += _dot(p, do, TN)
            dqh, dgc = _rms_bwd(dqn, qh, g, r)
            dq_ref[:, sl] = dqh.astype(dq_ref.dtype)
            _acc_rows(dg_ref, 0, dgc)

    row = BS((tm, 1024), lambda i: (i, 0))
    full = lambda r, w: BS((r, w), lambda i: (0, 0))
    return _pcall(body, name="xattn_bwd", grid=(S // tm,), after=after,
                  in_specs=[row, row, full(MEM_LEN, 1024), full(MEM_LEN, 1024), full(1, XA_DH)],
                  out_specs=[row, full(MEM_LEN, 1024), full(MEM_LEN, 1024), full(8, XA_DH)],
                  out_shape=[SDS((S, 1024), MXU_DTYPE), SDS((MEM_LEN, 1024), F32), SDS((MEM_LEN, 1024), F32),
                             SDS((8, XA_DH), F32)])(qx, dxo, kn, vb, gxq)


def mem_bwd(kv, dkn, dvb, gxk, after=None):
    def body(kv_ref, dkn_ref, dv_ref, g_ref, dkv_ref, dg_ref):
        dg_ref[...] = jnp.zeros_like(dg_ref)
        for h in range(XA_HEADS):
            sl = slice(h * XA_DH, (h + 1) * XA_DH)
            k = kv_ref[:, sl]
            dk, dgc = _rms_bwd(dkn_ref[:, sl], k, g_ref[...], _rms(k))
            dkv_ref[:, sl] = dk.astype(dkv_ref.dtype)
            _acc_rows(dg_ref, 0, dgc)
        dkv_ref[:, 1024:2048] = dv_ref[...].astype(dkv_ref.dtype)

    full = lambda r, w: BS((r, w), lambda i: (0, 0))
    return _pcall(body, name="mem_bwd", grid=(1,), after=after,
                  in_specs=[full(MEM_LEN, 2048), full(MEM_LEN, 1024), full(MEM_LEN, 1024), full(1, XA_DH)],
                  out_specs=[full(MEM_LEN, 2048), full(8, XA_DH)],
                  out_shape=[SDS((MEM_LEN, 2048), MXU_DTYPE), SDS((8, XA_DH), F32)])(kv, dkn, dvb, gxk)


def gmlp_bwd(dgm, gvn, gu, w2, w2t, bsl, after=None):
    S = dgm.shape[0]

    def body(dgm_ref, gvn_ref, gu_ref, w2_ref, w2t_ref, bsl_ref, dgu_ref, dgvn_ref, dws_ref, dbl_ref):
        @pl.when(pl.program_id(0) == 0)
        def _():
            dws_ref[...] = jnp.zeros_like(dws_ref)
            dbl_ref[...] = jnp.zeros_like(dbl_ref)

        lo = _lane((BLK, 128)) < 64
        for j in range(4):
            sl = slice(j * 128, (j + 1) * 128)
            gvn_s = gvn_ref[:, sl]
            m2 = _dot(w2_ref[j], gvn_s)
            mixed = jnp.where(lo, m2[:BLK], m2[BLK:]) + bsl_ref[j]
            dgm_s = dgm_ref[:, sl]
            dgu_ref[:, sl] = dgm_s * mixed
            dmx = dgm_s * gu_ref[:, sl]
            d2 = _dot(w2t_ref[j], dmx)
            dgvn_ref[:, sl] = jnp.where(lo, d2[:BLK], d2[BLK:])
            z = jnp.zeros_like(dmx)
            dws_ref[2 * j] += _dot(jnp.where(lo, dmx, z), gvn_s, NT)
            dws_ref[2 * j + 1] += _dot(jnp.where(lo, z, dmx), gvn_s, NT)
            dbl_ref[j] += dmx

    row = lambda w: BS((BLK, w), lambda n: (n, 0))
    const3 = lambda a, b, c: BS((a, b, c), lambda n: (0, 0, 0))
    return _pcall(body, name="gmlp_bwd", grid=(S // BLK,), after=after,
                  in_specs=[row(512), row(512), row(512), const3(4, 2 * BLK, BLK), const3(4, 2 * BLK, BLK),
                            const3(4, BLK, 128)],
                  out_specs=[row(512), row(512), const3(8, BLK, BLK), const3(4, BLK, 128)],
                  out_shape=[SDS((S, 512), F32), SDS((S, 512), F32), SDS((8, BLK, BLK), F32),
                             SDS((4, BLK, 128), F32)])(dgm, gvn, gu, w2, w2t, bsl)


def swa_bwd(qr, kr, vb, sinkcol, dattn):
    S = qr.shape[0]
    nb = S // BLK

    def body(q_ref, kc_ref, kp_ref, vc_ref, vp_ref, sk_ref, do_ref, dq_ref, dk_ref, dv_ref, dsk_ref,
             carry_k, carry_v, prev_k, prev_v):
        n = pl.program_id(0)

        @pl.when(n == 0)
        def _():
            dsk_ref[...] = jnp.zeros_like(dsk_ref)
            carry_k[...] = jnp.zeros_like(carry_k)
            carry_v[...] = jnp.zeros_like(carry_v)

        @pl.when(n < nb)
        def _():
            lo = _lane((BLK, 128)) < 64
            for h in range(2):
                hs, qs = slice(h * 128, (h + 1) * 128), slice(h * 256, (h + 1) * 256)
                kd = jnp.concatenate([kp_ref[:, hs], kc_ref[:, hs]], axis=0)
                vd = jnp.concatenate([vp_ref[:, hs], vc_ref[:, hs]], axis=0)
                sink = jnp.concatenate([sk_ref[2 * h], sk_ref[2 * h + 1]], axis=0)
                qp, p, psink = _swa_probs(q_ref[:, qs], kd, sink, n, lo)
                dop = _by_head(do_ref[:, qs], lo)
                dp = _dot(dop, vd, NT)
                delta = jnp.sum(dp * p, axis=1, keepdims=True)
                ds = p * (dp - delta) * (1.0 / math.sqrt(HEAD_DIM))
                dsink = -psink * delta
                dsk_ref[2 * h] += dsink[:2 * BLK]
                dsk_ref[2 * h + 1] += dsink[2 * BLK:]
                dq_ref[:, qs] = _from_heads(_dot(ds, kd), lo)
                dkd = _dot(ds, qp, TN)
                dvd = _dot(p, dop, TN)
                prev_k[:, hs] = carry_k[:, hs] + dkd[:BLK]
                prev_v[:, hs] = carry_v[:, hs] + dvd[:BLK]
                carry_k[:, hs] = dkd[BLK:]
                carry_v[:, hs] = dvd[BLK:]

        @pl.when(n == nb)
        def _():
            prev_k[...] = carry_k[...]
            prev_v[...] = carry_v[...]

        dk_ref[...] = prev_k[...]
        dv_ref[...] = prev_v[...]

    last = nb - 1
    cur = lambda w: BS((BLK, w), lambda n: (jnp.minimum(n, last), 0))
    prev = lambda w: BS((BLK, w), lambda n: (jnp.clip(n - 1, 0, last), 0))
    done = lambda w: BS((BLK, w), lambda n: (jnp.maximum(n - 1, 0), 0))
    return _pcall(body, name="swa_bwd", grid=(nb + 1,),
                  in_specs=[cur(512), cur(256), prev(256), cur(256), prev(256),
                            BS((4, 2 * BLK, 1), lambda n: (0, 0, 0)), cur(512)],
                  out_specs=[cur(512), done(256), done(256), BS((4, 2 * BLK, 1), lambda n: (0, 0, 0))],
                  out_shape=[SDS((S, 512), F32), SDS((S, 256), F32), SDS((S, 256), F32), SDS((4, 2 * BLK, 1), F32)],
                  scratch=[pltpu.VMEM((BLK, 256), F32)] * 4)(qr, kr, kr, vb, vb, sinkcol, dattn)


def mixer_pre_bwd(proj, cos, sin, gq, gk, gvn, bmat, dqr, dkr, dvb, dgu, dgvn):
    S = proj.shape[0]
    tm = _tile(S, (256,))

    def body(p_ref, c_ref, s_ref, gq_ref, gk_ref, gvn_ref, b_ref, dqr_ref, dkr_ref, dvb_ref, dgu_ref, dgvn_ref,
             dp_ref, dgq_ref, dgk_ref, dgv_ref):
        @pl.when(pl.program_id(0) == 0)
        def _():
            dgq_ref[...] = jnp.zeros_like(dgq_ref)
            dgk_ref[...] = jnp.zeros_like(dgk_ref)
            dgv_ref[...] = jnp.zeros_like(dgv_ref)

        cos_v, sin_v, bm = c_ref[...], s_ref[...], b_ref[...]
        first = (_lane((tm, 128)) & 63) < 32

        slabs = [p_ref[:, s * 128:(s + 1) * 128] for s in range(6)]
        douts = [dqr_ref[:, s * 128:(s + 1) * 128] for s in range(4)] + [dkr_ref[:, s * 128:(s + 1) * 128] for s in range(2)]
        gains = [gq_ref[...]] * 4 + [gk_ref[...]] * 2
        dqns = [d * cos_v + _half_swap(d * sin_v, first) for d in douts]
        rs = [lax.rsqrt(ms + EPS) for ms in _head_means([x * x for x in slabs], bm)]
        projs = _head_means([dqn * g * x for dqn, g, x in zip(dqns, gains, slabs)], bm)
        for s, (slab, dqn, g, r, pr) in enumerate(zip(slabs, dqns, gains, rs, projs)):
            dx = r * (dqn * g) - slab * (r * r * r) * pr
            dp_ref[:, s * 128:(s + 1) * 128] = dx.astype(dp_ref.dtype)
            _acc_rows(dgq_ref if s < 4 else dgk_ref, 0, dqn * slab * r)
        dp_ref[:, 768:1024] = dvb_ref[...].astype(dp_ref.dtype)
        dp_ref[:, 1024:1536] = (dgu_ref[...] * _gelu_grad(p_ref[:, 1024:1536])).astype(dp_ref.dtype)
        gvp = p_ref[:, 1536:2048]
        gv = _gelu(gvp)
        dgv, dgc = _rms_bwd(dgvn_ref[...], gv, gvn_ref[...], _rms(gv))
        dp_ref[:, 1536:2048] = (dgv * _gelu_grad(gvp)).astype(dp_ref.dtype)
        _acc_rows(dgv_ref, 0, dgc)

    row = lambda w: BS((tm, w), lambda i: (i, 0))
    const = lambda r, w: BS((r, w), lambda i: (0, 0))
    return _pcall(body, name="mixer_pre_bwd", grid=(S // tm,),
                  in_specs=[row(IN_COLS_DUP), row(128), row(128), const(1, 128), const(1, 128), const(1, 512),
                            const(128, 128), row(512), row(256), row(256), row(512), row(512)],
                  out_specs=[row(IN_COLS_DUP), const(8, 128), const(8, 128), const(8, 512)],
                  out_shape=[SDS((S, IN_COLS_DUP), MXU_DTYPE), SDS((8, 128), F32), SDS((8, 128), F32),
                             SDS((8, 512), F32)])(proj, cos, sin, gq, gk, gvn, bmat, dqr, dkr, dvb, dgu, dgvn)


BIG = (("w_in", (1024, 448), True), ("w_out", (256, 1024), False), ("xa_wq", (256, 1024), False),
       ("xa_wkv", (1024, 512), True), ("xa_wo", (256, 1024), False), ("ffn_up", (1024, 1408), True),
       ("ffn_down", (704, 1024), False))
BIG_NAMES = tuple(n for n, _, _ in BIG)
SMALL_VECS = (("mix_norm", 1024), ("q_norm", 64), ("k_norm", 64), ("attn_sinks", 8), ("gmlp_v_norm", 512),
              ("attn_out_norm", 512), ("gmlp_out_norm", 512), ("xa_norm", 1024), ("mem_norm", 1024),
              ("xa_q_norm", 256), ("xa_k_norm", 256), ("ffn_norm", 1024), ("ffn_conv_b", 5632))
SMALL = tuple(n for n, _ in SMALL_VECS) + ("gmlp_bs", "gmlp_ws", "ffn_conv")
WEIGHTS = ("mix_norm", "w_in", "q_norm", "k_norm", "attn_sinks", "gmlp_v_norm", "gmlp_ws", "gmlp_bs",
           "attn_out_norm", "gmlp_out_norm", "w_out", "xa_norm", "mem_norm", "xa_wq", "xa_wkv", "xa_q_norm",
           "xa_k_norm", "xa_wo", "ffn_norm", "ffn_up", "ffn_conv", "ffn_conv_b", "ffn_down")
CONV_SHARD = (3, 1408)
CONV_LANE_ROWS = CONV_SHARD[1] // 128
CONV_CHIP_ROWS = 40


def _small_rows():
    rows, r = {}, 0
    for n, length in SMALL_VECS:
        rows[n] = r
        r += -(-length // 128)
    r += -r % 8
    rows["gmlp_bs"] = r
    r += 8
    rows["gmlp_ws"] = r
    r += 8 * BLK
    rows["ffn_conv"] = r
    r += N_CHIPS * CONV_CHIP_ROWS
    return rows, r


SMALL_ROW, SMALL_ROWS = _small_rows()


def pack_small(dg_mix, dgq, dgk, dsk, dg_gvn, dg_y, dg_xa, dg_mem, dg_xq, dg_xk, dg_ffn, gcw, dbl, dws):
    def body(mix_ref, q_ref, k_ref, sk_ref, gvn_ref, y_ref, xa_ref, mem_ref, xq_ref, xk_ref, ffn_ref, cw_ref,
             dbl_ref, dws_ref, o_ref):
        o_ref[...] = jnp.zeros_like(o_ref)
        lane = _lane((1, 128))

        def put(name, src_ref, row, lane0, length):
            for k in range(length // 128):
                o_ref[SMALL_ROW[name] + k:SMALL_ROW[name] + k + 1, :] = src_ref[row:row + 1, lane0 + k * 128:lane0 + (k + 1) * 128]

        put("mix_norm", mix_ref, 0, 0, 1024)
        for name, ref in (("q_norm", q_ref), ("k_norm", k_ref)):
            v = ref[0:1, :]
            o_ref[SMALL_ROW[name]:SMALL_ROW[name] + 1, :] = jnp.where(lane < HEAD_DIM, v + pltpu.roll(v, 64, 1), 0.0)
        sinks = jnp.zeros((1, 128), F32)
        for s in range(4):
            col = sk_ref[s]
            sinks = sinks + jnp.where(lane == 2 * s, jnp.sum(col[:BLK]), 0.0) + jnp.where(lane == 2 * s + 1, jnp.sum(col[BLK:]), 0.0)
        o_ref[SMALL_ROW["attn_sinks"]:SMALL_ROW["attn_sinks"] + 1, :] = sinks
        put("gmlp_v_norm", gvn_ref, 0, 0, 512)
        put("attn_out_norm", y_ref, 0, 0, 512)
        put("gmlp_out_norm", y_ref, 0, 512, 512)
        put("xa_norm", xa_ref, 0, 0, 1024)
        put("mem_norm", mem_ref, 0, 0, 1024)
        put("xa_q_norm", xq_ref, 0, 0, 256)
        put("xa_k_norm", xk_ref, 0, 0, 256)
        put("ffn_norm", ffn_ref, 0, 0, 1024)
        put("ffn_conv_b", cw_ref, 3, 0, 2 * D_FF)
        r8 = lax.broadcasted_iota(jnp.int32, (8, 128), 0)
        l8 = _lane((8, 128))
        bs = jnp.zeros((8, BLK), F32)
        for j in range(4):
            sel = (((r8 == 2 * j) & (l8 < 64)) | ((r8 == 2 * j + 1) & (l8 >= 64))).astype(F32).astype(BF16)
            xj = dbl_ref[j]
            hi = xj.astype(BF16)
            lo = (xj - hi.astype(F32)).astype(BF16)
            bs = bs + lax.dot_general(sel, hi, NT, preferred_element_type=F32) + lax.dot_general(sel, lo, NT, preferred_element_type=F32)
        o_ref[SMALL_ROW["gmlp_bs"]:SMALL_ROW["gmlp_bs"] + 8, :] = bs
        causal = lax.broadcasted_iota(jnp.int32, (BLK, BLK), 0) >= lax.broadcasted_iota(jnp.int32, (BLK, BLK), 1)
        for h in range(8):
            r0 = SMALL_ROW["gmlp_ws"] + h * BLK
            o_ref[r0:r0 + BLK, :] = jnp.where(causal, dws_ref[h], 0.0)
        for q in range(N_CHIPS):
            for j in range(3):
                for k in range(CONV_LANE_ROWS):
                    r0 = SMALL_ROW["ffn_conv"] + q * CONV_CHIP_ROWS + j * CONV_LANE_ROWS + k
                    l0 = (q * CONV_LANE_ROWS + k) * 128
                    o_ref[r0:r0 + 1, :] = cw_ref[j:j + 1, l0:l0 + 128]

    args = (dg_mix, dgq, dgk, dsk, dg_gvn, dg_y, dg_xa, dg_mem, dg_xq, dg_xk, dg_ffn, gcw, dbl, dws)
    full = lambda a: BS(a.shape, lambda i, nd=a.ndim: (0,) * nd)
    return _pcall(body, name="pack_small", grid=(1,), in_specs=[full(a) for a in args],
                  out_specs=BS((SMALL_ROWS, 128), lambda i: (0, 0)), out_shape=SDS((SMALL_ROWS, 128), F32))(*args)


def _adam(w, g, m, v):
    mn = ADAM_B1 * m + (1.0 - ADAM_B1) * g
    vn = ADAM_B2 * v + (1.0 - ADAM_B2) * (g * g)
    m_hat = mn / (1.0 - ADAM_B1 ** ADAM_STEP)
    v_hat = vn / (1.0 - ADAM_B2 ** ADAM_STEP)
    return -ADAM_LR * (m_hat / (jnp.sqrt(v_hat) + ADAM_EPS) + ADAM_WD * w), mn, vn


def adamw_small(gsum, w, m, v, chipvec):
    n = len(SMALL)

    def body(chip_ref, g_ref, *refs):
        w_refs, m_refs, v_refs = refs[:n], refs[n:2 * n], refs[2 * n:3 * n]
        outs = refs[3 * n:]
        go, do, mo, vo = outs[:n], outs[n:2 * n], outs[2 * n:3 * n], outs[3 * n:]

        def update(i, idx, g):
            d, mn, vn = _adam(w_refs[i][idx], g, m_refs[i][idx], v_refs[i][idx])
            go[i][idx] = g
            do[i][idx] = d
            mo[i][idx] = mn
            vo[i][idx] = vn

        for i, (name, length) in enumerate(SMALL_VECS):
            for k in range(-(-length // 128)):
                wd = min(128, length - k * 128)
                r = SMALL_ROW[name] + k
                update(i, (slice(0, 1), slice(k * 128, k * 128 + wd)), g_ref[r:r + 1, 0:wd])
        i_bs, i_ws, i_cv = len(SMALL_VECS), len(SMALL_VECS) + 1, len(SMALL_VECS) + 2
        update(i_bs, (0,), g_ref[SMALL_ROW["gmlp_bs"]:SMALL_ROW["gmlp_bs"] + 8, :])
        for h in range(8):
            r0 = SMALL_ROW["gmlp_ws"] + h * BLK
            update(i_ws, (0, h), g_ref[r0:r0 + BLK, :])
        mine = g_ref[pl.ds(pl.multiple_of(SMALL_ROW["ffn_conv"] + chip_ref[0] * CONV_CHIP_ROWS, 8), CONV_CHIP_ROWS), :]
        for j in range(3):
            for k in range(CONV_LANE_ROWS):
                r = j * CONV_LANE_ROWS + k
                update(i_cv, (0, slice(j, j + 1), slice(k * 128, (k + 1) * 128)), mine[r:r + 1, :])

    nat = [w[nm] for nm in SMALL]
    full = lambda a: BS(a.shape, lambda i, c, nd=a.ndim: (0,) * nd)
    outs = _pcall(body, name="adamw_small", grid=(1,), prefetch=1,
                  in_specs=[BS((SMALL_ROWS, 128), lambda i, c: (0, 0))] + [full(a) for a in nat] * 3,
                  out_specs=[full(a) for a in nat] * 4, out_shape=[SDS(a.shape, F32) for a in nat] * 4)(
        chipvec, gsum, *nat, *[m[nm] for nm in SMALL], *[v[nm] for nm in SMALL])
    return outs[:n], outs[n:2 * n], outs[2 * n:3 * n], outs[3 * n:]


def adamw_matrix(w, m, v, g_own, g_other, cvec, *, name):
    _, r, c = w.shape
    half = r // 2
    tr = _tile(half, (128, 176))
    T = half // tr

    def body(c_ref, w_ref, m_ref, v_ref, own_ref, oth_ref, g_ref, d_ref, mo_ref, vo_ref):
        g = jnp.where(pl.program_id(0) == c_ref[0], own_ref[...], oth_ref[...])
        d, mn, vn = _adam(w_ref[...], g, m_ref[...], v_ref[...])
        g_ref[...] = g
        d_ref[...] = d
        mo_ref[...] = mn
        vo_ref[...] = vn

    nat = BS((None, tr, c), lambda hf, t, cr: (0, hf * T + t, 0))
    hlf = BS((tr, c), lambda hf, t, cr: (t, 0))
    return _pcall(body, name=name, grid=(2, T), prefetch=1, in_specs=[nat, nat, nat, hlf, hlf], out_specs=[nat] * 4,
                  out_shape=[SDS(w.shape, F32)] * 4)(cvec, w, m, v, g_own, g_other)


def _place():
    return lax.axis_index("x"), lax.axis_index("y"), lax.axis_index("c")


def _other_chips(x, y):
    return [(1 - x, y), (x, 1 - y), (1 - x, 1 - y)]


def _rows_of_core(c, half):
    return pl.ds(pl.multiple_of(c * half, 16), half)


def _rcopy(src, dst, sems, k, to):
    return pltpu.make_async_remote_copy(src_ref=src, dst_ref=dst, send_sem=sems[0].at[k], recv_sem=sems[1].at[k],
                                        device_id=to, device_id_type=MESH)


def _comm_call(body, *, name, out_shape, n_in, n_sems, aliases=None):
    return pl.pallas_call(body, name=name, out_shape=out_shape, in_specs=[ANY] * n_in, out_specs=[ANY] * len(out_shape),
                          scratch_shapes=[pltpu.SemaphoreType.DMA((n_sems,)), pltpu.SemaphoreType.DMA((n_sems,))],
                          input_output_aliases=aliases or {},
                          compiler_params=pltpu.CompilerParams(has_side_effects=True))


def cast_shards(shards, conv, chipvec):
    n = len(shards)

    def body(chip_ref, *refs):
        for i_ref, o_ref in zip(refs[:n + 1], refs[n + 1:]):
            o_ref[...] = i_ref[...].astype(o_ref.dtype)

    in_specs = [BS((s.shape[0] // 4, s.shape[1]), lambda i, p: (i, 0)) for s in shards]
    in_specs.append(BS(conv.shape, lambda i, p: (0, 0)))
    out_specs = [BS((None, s.shape[0] // 4, s.shape[1]), lambda i, p: (p[0], i, 0)) for s in shards]
    out_specs.append(BS((None,) + conv.shape, lambda i, p: (p[0], 0, 0)))
    out_shape = [SDS((N_CHIPS,) + s.shape, MXU_DTYPE) for s in shards] + [SDS((N_CHIPS,) + conv.shape, F32)]
    return _pcall(body, name="cast_shards", grid=(4,), prefetch=1, in_specs=in_specs, out_specs=out_specs,
                  out_shape=out_shape)(chipvec, *shards, conv)


HBM = pl.BlockSpec(memory_space=pltpu.HBM)
SEM = pl.BlockSpec(memory_space=pltpu.SEMAPHORE)
DATAFLOW = pltpu.SideEffectType.DATAFLOW_SIDE_EFFECTING
VMEM_WHOLE = pl.BlockSpec(memory_space=pltpu.VMEM)
TOKEN = jax.ShapeDtypeStruct((8, 128), jnp.float32)


def _gather_copies(bufs, send_sems, recv_sems, outgoing):
    x, y, c = _place()
    p = 2 * x + y
    cps = []
    for i, o in enumerate(bufs):
        for j, (cx, cy) in enumerate(_other_chips(x, y)):
            slot = o.at[p] if outgoing else o.at[2 * cx + cy]
            cps.append(_rcopy(slot, slot, (send_sems, recv_sems), 3 * i + j, (cx, cy, c)))
    return cps


def gather_start(slots):
    n = len(slots)

    def body(*refs):
        send_sems, recv_sems, thru, token = refs[n], refs[n + 1], refs[n + 2:2 * n + 2], refs[2 * n + 2]
        for cp in _gather_copies(thru, send_sems, recv_sems, True):
            cp.start()
        token[...] = jnp.zeros_like(token)

    hbm = [pltpu.with_memory_space_constraint(s, pltpu.HBM) for s in slots]
    outs = pl.pallas_call(
        body, name="gather_start_%d" % n,
        out_shape=[pltpu.SemaphoreType.DMA((3 * n,)), pltpu.SemaphoreType.DMA((3 * n,))]
        + [pltpu.HBM(s.shape, s.dtype) for s in slots] + [TOKEN],
        in_specs=[HBM] * n, out_specs=[SEM, SEM] + [HBM] * n + [VMEM_WHOLE],
        input_output_aliases={i: 2 + i for i in range(n)},
        compiler_params=pltpu.CompilerParams(has_side_effects=DATAFLOW))(*hbm)
    return outs[0], outs[1], outs[2:2 + n], outs[2 + n]


def gather_wait(send_sems, recv_sems, bufs, after):
    n = len(bufs)

    def body(*refs):
        ins, send_ref, recv_ref = refs[:n], refs[n], refs[n + 1]
        for cp in _gather_copies(ins, send_ref, recv_ref, False):
            cp.wait_send()
            cp.wait_recv()

    return pl.pallas_call(
        body, name="gather_wait_%d" % n, out_shape=[pltpu.HBM(s.shape, s.dtype) for s in bufs],
        in_specs=[HBM] * n + [SEM, SEM, ANY], out_specs=[HBM] * n, input_output_aliases={i: i for i in range(n)},
        compiler_params=pltpu.CompilerParams(has_side_effects=DATAFLOW))(*bufs, send_sems, recv_sems, after)


def _peers(x, y, c):
    return [(1 - x if k & 4 else x, 1 - y if k & 2 else y, 1 - c if k & 1 else c) for k in range(1, N_DEV)]


def _partial_copies(g_ref, land_ref, send_sems, recv_sems, outgoing):
    x, y, c = _place()
    half = g_ref.shape[1] // 2
    cps = []
    for k, (px, py, pc) in enumerate(_peers(x, y, c)):
        src = g_ref.at[2 * px + py, _rows_of_core(pc, half)]
        dst = land_ref.at[4 * x + 2 * y + c] if outgoing else land_ref.at[4 * px + 2 * py + pc]
        cps.append(_rcopy(src, dst, (send_sems, recv_sems), k, (px, py, pc)))
    return cps


def partials_start(g, *, name):
    land = lax.empty((N_DEV, g.shape[1] // 2, g.shape[2]), g.dtype)

    def body(g_ref, land_ref, send_sems, recv_sems, g_thru, land_thru, token):
        for cp in _partial_copies(g_thru, land_thru, send_sems, recv_sems, True):
            cp.start()
        token[...] = jnp.zeros_like(token)

    return pl.pallas_call(
        body, name=name,
        out_shape=[pltpu.SemaphoreType.DMA((N_DEV - 1,)), pltpu.SemaphoreType.DMA((N_DEV - 1,)),
                   pltpu.HBM(g.shape, g.dtype), pltpu.HBM(land.shape, land.dtype), TOKEN],
        in_specs=[HBM, HBM], out_specs=[SEM, SEM, HBM, HBM, VMEM_WHOLE], input_output_aliases={0: 2, 1: 3},
        compiler_params=pltpu.CompilerParams(has_side_effects=DATAFLOW))(
        pltpu.with_memory_space_constraint(g, pltpu.HBM), pltpu.with_memory_space_constraint(land, pltpu.HBM))


def partials_wait(started, after):
    n = len(started)

    def body(*refs):
        for i in range(n):
            send_ref, recv_ref, g_ref, land_ref = refs[4 * i:4 * i + 4]
            for cp in _partial_copies(g_ref, land_ref, send_ref, recv_ref, False):
                cp.wait_send()
                cp.wait_recv()

    flat = [a for s in started for a in s]
    bufs = [a for s in started for a in s[2:]]
    outs = pl.pallas_call(
        body, name="partials_wait", out_shape=[pltpu.HBM(b.shape, b.dtype) for b in bufs],
        in_specs=[SEM, SEM, HBM, HBM] * n + [ANY], out_specs=[HBM] * (2 * n),
        input_output_aliases={4 * i + 2 + j: 2 * i + j for i in range(n) for j in range(2)},
        compiler_params=pltpu.CompilerParams(has_side_effects=DATAFLOW))(*flat, after)
    return [(outs[2 * i], outs[2 * i + 1]) for i in range(n)]


def sum_partials(pairs, order):
    n = len(pairs)

    def body(o_ref, *refs):
        j = pl.program_id(0)
        for g_ref, l_ref, f_ref in zip(refs[:n], refs[n:2 * n], refs[2 * n:]):
            @pl.when(j == 0)
            def _():
                f_ref[...] = g_ref[...].astype(F32)

            @pl.when(j > 0)
            def _():
                f_ref[...] += l_ref[...].astype(F32)

    g4 = [g.reshape(g.shape[0], 2, g.shape[1] // 2, g.shape[2]) for g, _ in pairs]
    lands = [l for _, l in pairs]
    return _pcall(body, name="sum_partials", grid=(N_DEV,), prefetch=1,
                  in_specs=[BS((None, None) + g.shape[2:], lambda j, o: (o[0], o[1], 0, 0)) for g in g4]
                  + [BS((None,) + l.shape[1:], lambda j, o: (o[jnp.maximum(j, 1) + 1], 0, 0)) for l in lands],
                  out_specs=[BS(l.shape[1:], lambda j, o: (0, 0)) for l in lands],
                  out_shape=[SDS(l.shape[1:], F32) for l in lands])(order, *g4, *lands)


def pair_share(fs):
    n = len(fs)

    def body(*refs):
        f_refs, o_refs, sems = refs[:n], refs[n:2 * n], refs[2 * n:]
        x, y, c = _place()
        cps = [_rcopy(f, o, sems, i, (x, y, 1 - c)) for i, (f, o) in enumerate(zip(f_refs, o_refs))]
        for cp in cps:
            cp.start()
        for cp in cps:
            cp.wait()

    return _comm_call(body, name="pair_share", n_in=n, n_sems=n, out_shape=[SDS(f.shape, f.dtype) for f in fs])(*fs)


def _small_copies(s_ref, land_ref, send_sems, recv_sems, outgoing):
    x, y, c = _place()
    cps = []
    for k, (px, py, pc) in enumerate(_peers(x, y, c)):
        dst = land_ref.at[4 * x + 2 * y + c] if outgoing else land_ref.at[4 * px + 2 * py + pc]
        cps.append(_rcopy(s_ref, dst, (send_sems, recv_sems), k, (px, py, pc)))
    return cps


def small_start(sm):
    land = lax.empty((N_DEV,) + sm.shape, sm.dtype)

    def body(s_ref, land_ref, send_sems, recv_sems, s_thru, land_thru):
        for cp in _small_copies(s_thru, land_thru, send_sems, recv_sems, True):
            cp.start()

    return pl.pallas_call(
        body, name="small_start",
        out_shape=[pltpu.SemaphoreType.DMA((N_DEV - 1,)), pltpu.SemaphoreType.DMA((N_DEV - 1,)),
                   pltpu.HBM(sm.shape, sm.dtype), pltpu.HBM(land.shape, land.dtype)],
        in_specs=[HBM, HBM], out_specs=[SEM, SEM, HBM, HBM], input_output_aliases={0: 2, 1: 3},
        compiler_params=pltpu.CompilerParams(has_side_effects=DATAFLOW))(
        pltpu.with_memory_space_constraint(sm, pltpu.HBM), pltpu.with_memory_space_constraint(land, pltpu.HBM))


def small_wait(send_sems, recv_sems, sm, land, after):
    def body(send_ref, recv_ref, s_ref, land_ref, after_ref, s_out, land_out):
        for cp in _small_copies(s_ref, land_ref, send_ref, recv_ref, False):
            cp.wait_send()
            cp.wait_recv()

    return pl.pallas_call(
        body, name="small_wait", out_shape=[pltpu.HBM(sm.shape, sm.dtype), pltpu.HBM(land.shape, land.dtype)],
        in_specs=[SEM, SEM, HBM, HBM, ANY], out_specs=[HBM, HBM], input_output_aliases={2: 0, 3: 1},
        compiler_params=pltpu.CompilerParams(has_side_effects=DATAFLOW))(send_sems, recv_sems, sm, land, after)


def sum_small(own, land, mevec):
    n, rows, width = land.shape
    tr = _tile(rows, (184, 8))

    def body(me_ref, own_ref, land_ref, o_ref):
        acc = jnp.zeros((tr, width), F32)
        for s in range(n):
            acc = acc + jnp.where(me_ref[0] == s, own_ref[...], land_ref[s])
        o_ref[...] = acc

    return _pcall(body, name="sum_small", grid=(rows // tr,), prefetch=1,
                  in_specs=[BS((tr, width), lambda i, me: (i, 0)), BS((n, tr, width), lambda i, me: (0, i, 0))],
                  out_specs=BS((tr, width), lambda i, me: (i, 0)), out_shape=SDS((rows, width), F32))(mevec, own, land)


def _to_full(blk, col):
    n, r, c = blk.shape
    return blk.transpose(1, 0, 2).reshape(r, n * c) if col else blk.reshape(n * r, c)


def _dup_cols(w):
    dup = lambda t: jnp.concatenate([t[:, :64], t[:, :64], t[:, 64:], t[:, 64:]], axis=1)
    return jnp.concatenate([w[:, :512], dup(w[:, 512:640]), dup(w[:, 640:768]), w[:, 768:]], axis=1)


def _fold_cols(d):
    fold = lambda t: jnp.concatenate([t[:, 0:64] + t[:, 64:128], t[:, 128:192] + t[:, 192:256]], axis=1)
    return jnp.concatenate([d[:, :512], fold(d[:, 512:768]), fold(d[:, 768:1024]), d[:, 1024:]], axis=1)


def _local_step(x, mem, positions, target, w_in, later, sp, emit):
    gain = lambda n: sp[n].reshape(1, -1)
    half = HEAD_DIM // 2
    inv_freq = 1.0 / (10000.0 ** (jnp.arange(half, dtype=F32) * (2.0 / HEAD_DIM)))
    ang = positions.astype(F32)[:, None] * inv_freq
    cos, sin = jnp.cos(ang), jnp.sin(ang)
    cos128 = jnp.tile(cos, (1, 4))
    sin128 = jnp.concatenate([-sin, sin, -sin, sin], axis=1)
    seg = jnp.arange(128) // HEAD_DIM
    bmat = (seg[:, None] == seg[None, :]).astype(BF16)
    gq128, gk128 = jnp.tile(gain("q_norm"), (1, 2)), jnp.tile(gain("k_norm"), (1, 2))
    sinkcol = jnp.repeat(sp["attn_sinks"].reshape(4, 2), BLK, axis=1).reshape(4, 2 * BLK, 1)
    wsc = sp["gmlp_ws"] * jnp.tril(jnp.ones((BLK, BLK), F32))[None]
    w2 = wsc.reshape(4, 2 * BLK, BLK).astype(MXU_DTYPE)
    w2t = wsc.swapaxes(1, 2).reshape(4, 2 * BLK, BLK).astype(MXU_DTYPE)
    bsl = jnp.repeat(sp["gmlp_bs"].reshape(4, 2, BLK).transpose(0, 2, 1), HEAD_DIM, axis=2)
    cb = sp["ffn_conv_b"].reshape(1, -1)
    w_in_d = _dup_cols(_to_full(w_in, True))[None]

    h1, proj = rms_mm(x, gain("mix_norm"), w_in_d, name="mix_in")
    qr, kr, vb, gu, gvn = mixer_pre(proj, cos128, sin128, gq128, gk128, gain("gmlp_v_norm"), bmat)
    attn, ya = swa_fwd(qr, kr, vb, sinkcol, gain("attn_out_norm"))
    gm, y = gmlp_fwd(gvn, gu, ya, w2, bsl, gain("gmlp_out_norm"))
    wf, cw = later(y)
    w_out, xa_wq, xa_wo, ffn_down = (_to_full(wf[n], False) for n in ("w_out", "xa_wq", "xa_wo", "ffn_down"))
    x1 = mm(y, w_out, res=x, name="mix_out")
    h2, qx = rms_mm(x1, gain("xa_norm"), xa_wq[None], name="xa_q")
    mn, kv = rms_mm(mem, gain("mem_norm"), wf["xa_wkv"], name="xa_kv")
    kn, vbx = mem_pre(kv, gain("xa_k_norm"))
    xo = xattn_fwd(qx, kn, vbx, gain("xa_q_norm"))
    x2 = mm(xo, xa_wo, res=x1, name="xa_out")
    h3, a = rms_mm(x2, gain("ffn_norm"), wf["ffn_up"], name="ffn_up")
    f, dx3, loss_acc = convgate_down_loss(a, cw, cb, ffn_down, x2, target)

    by_rows = lambda g: g.reshape(N_CHIPS, g.shape[1] // N_CHIPS, g.shape[2])
    sent = emit("ffn_down", by_rows(mm_tn(f, dx3, name="g_ffn_down", out_dtype=WIRE_DTYPE)))
    dc, gcw = convgate_bwd(a, dx3, ffn_down[None], cw, cb, after=sent)
    da, dx2, dg_ffn = conv_transpose_rms_bwd(dc, cw, wf["ffn_up"], x2, gain("ffn_norm"), dx3)
    sent = emit("ffn_up", mm_tn(h3, da, name="g_ffn_up", out_dtype=WIRE_DTYPE, chunks=N_CHIPS))
    dxo = mm_nt(dx2, xa_wo[None], name="d_xo", after=sent)
    sent = emit("xa_wo", by_rows(mm_tn(xo, dx2, name="g_xa_wo", out_dtype=WIRE_DTYPE)))
    dqx, dkn, dvx, dg_xq = xattn_bwd(qx, dxo, kn, vbx, gain("xa_q_norm"), after=sent)
    dx1, dg_xa = mm_nt_rms_bwd(dqx, xa_wq[None], x1, gain("xa_norm"), dx2, name="d_x1")
    sent = emit("xa_wq", by_rows(mm_tn(h2, dqx, name="g_xa_wq", out_dtype=WIRE_DTYPE)))
    dkv, dg_xk = mem_bwd(kv, dkn, dvx, gain("xa_k_norm"), after=sent)
    _, dg_mem = mm_nt_rms_bwd(dkv, wf["xa_wkv"], mem, gain("mem_norm"), jnp.zeros_like(mem), name="d_mem")
    sent = emit("xa_wkv", mm_tn(mn, dkv, name="g_xa_wkv", out_dtype=WIRE_DTYPE, chunks=N_CHIPS))
    dattn, dgm, dg_y = mm_nt_post_bwd(dx1, w_out[None], attn, gm, gain("attn_out_norm"), gain("gmlp_out_norm"),
                                      name="d_mix_out", after=sent)
    sent = emit("w_out", by_rows(mm_tn(y, dx1, name="g_w_out", out_dtype=WIRE_DTYPE)))
    dgu, dgvn, dws, dbl = gmlp_bwd(dgm, gvn, gu, w2, w2t, bsl, after=sent)
    dqr, dkr, dvb, dsk = swa_bwd(qr, kr, vb, sinkcol, dattn)
    dproj, dgq, dgk, dg_gvn = mixer_pre_bwd(proj, cos128, sin128, gq128, gk128, gain("gmlp_v_norm"), bmat,
                                            dqr, dkr, dvb, dgu, dgvn)
    g_in = _fold_cols(mm_tn(h1, dproj, name="g_w_in", out_dtype=F32)[0])
    sent = emit("w_in", g_in.reshape(1024, N_CHIPS, 448).transpose(1, 0, 2).astype(WIRE_DTYPE))
    grad_x, dg_mix = mm_nt_rms_bwd(dproj, w_in_d, x, gain("mix_norm"), dx1, name="d_x", after=sent)
    packed = pack_small(dg_mix, dgq, dgk, dsk, dg_gvn, dg_y, dg_xa, dg_mem, dg_xq, dg_xk, dg_ffn, gcw, dbl, dws)
    return loss_acc, grad_x, packed


def _gather_step(w, chipvec):
    slots = cast_shards([w[n][0] for n in BIG_NAMES], w["ffn_conv"][0], chipvec)
    send_a, recv_a, first, _ = gather_start(slots[:1])
    send_b, recv_b, rest, rest_started = gather_start(slots[1:])
    w_in, = gather_wait(send_a, recv_a, first, rest_started)

    def later(after):
        got = gather_wait(send_b, recv_b, rest, after)
        return dict(zip(BIG_NAMES[1:], got[:-1])), _to_full(got[-1], True)

    return w_in, later


def _reduce_update(started, packed, w, m, v, chipvec, cvec, order):
    small_sent = small_start(packed)
    own = sum_partials(partials_wait([started[n] for n in BIG_NAMES], small_sent[2]), order)
    other = pair_share(own)
    res = [{}, {}, {}, {}]
    for n, g_own, g_other in zip(BIG_NAMES, own, other):
        for d, o in zip(res, adamw_matrix(w[n], m[n], v[n], g_own, g_other, cvec, name="adamw_" + n)):
            d[n] = o
    mevec = (2 * order[0:1] + order[1:2]).astype(jnp.int32)
    small_sum = sum_small(*small_wait(*small_sent, res[3][BIG_NAMES[-1]]), mevec)
    for d, outs in zip(res, adamw_small(small_sum, w, m, v, chipvec)):
        d.update(zip(SMALL, outs))
    return res


def kernel(x, mem, positions, mix_norm, w_in, q_norm, k_norm, attn_sinks, gmlp_v_norm, gmlp_ws, gmlp_bs, attn_out_norm, gmlp_out_norm, w_out, xa_norm, mem_norm, xa_wq, xa_wkv, xa_q_norm, xa_k_norm, xa_wo, ffn_norm, ffn_up, ffn_conv, ffn_conv_b, ffn_down, loss_target, m_mix_norm, m_w_in, m_q_norm, m_k_norm, m_attn_sinks, m_gmlp_v_norm, m_gmlp_ws, m_gmlp_bs, m_attn_out_norm, m_gmlp_out_norm, m_w_out, m_xa_norm, m_mem_norm, m_xa_wq, m_xa_wkv, m_xa_q_norm, m_xa_k_norm, m_xa_wo, m_ffn_norm, m_ffn_up, m_ffn_conv, m_ffn_conv_b, m_ffn_down, v_mix_norm, v_w_in, v_q_norm, v_k_norm, v_attn_sinks, v_gmlp_v_norm, v_gmlp_ws, v_gmlp_bs, v_attn_out_norm, v_gmlp_out_norm, v_w_out, v_xa_norm, v_mem_norm, v_xa_wq, v_xa_wkv, v_xa_q_norm, v_xa_k_norm, v_xa_wo, v_ffn_norm, v_ffn_up, v_ffn_conv, v_ffn_conv_b, v_ffn_down):
    w = dict(mix_norm=mix_norm, w_in=w_in, q_norm=q_norm, k_norm=k_norm, attn_sinks=attn_sinks, gmlp_v_norm=gmlp_v_norm, gmlp_ws=gmlp_ws, gmlp_bs=gmlp_bs, attn_out_norm=attn_out_norm, gmlp_out_norm=gmlp_out_norm, w_out=w_out, xa_norm=xa_norm, mem_norm=mem_norm, xa_wq=xa_wq, xa_wkv=xa_wkv, xa_q_norm=xa_q_norm, xa_k_norm=xa_k_norm, xa_wo=xa_wo, ffn_norm=ffn_norm, ffn_up=ffn_up, ffn_conv=ffn_conv, ffn_conv_b=ffn_conv_b, ffn_down=ffn_down)
    m = dict(mix_norm=m_mix_norm, w_in=m_w_in, q_norm=m_q_norm, k_norm=m_k_norm, attn_sinks=m_attn_sinks, gmlp_v_norm=m_gmlp_v_norm, gmlp_ws=m_gmlp_ws, gmlp_bs=m_gmlp_bs, attn_out_norm=m_attn_out_norm, gmlp_out_norm=m_gmlp_out_norm, w_out=m_w_out, xa_norm=m_xa_norm, mem_norm=m_mem_norm, xa_wq=m_xa_wq, xa_wkv=m_xa_wkv, xa_q_norm=m_xa_q_norm, xa_k_norm=m_xa_k_norm, xa_wo=m_xa_wo, ffn_norm=m_ffn_norm, ffn_up=m_ffn_up, ffn_conv=m_ffn_conv, ffn_conv_b=m_ffn_conv_b, ffn_down=m_ffn_down)
    v = dict(mix_norm=v_mix_norm, w_in=v_w_in, q_norm=v_q_norm, k_norm=v_k_norm, attn_sinks=v_attn_sinks, gmlp_v_norm=v_gmlp_v_norm, gmlp_ws=v_gmlp_ws, gmlp_bs=v_gmlp_bs, attn_out_norm=v_attn_out_norm, gmlp_out_norm=v_gmlp_out_norm, w_out=v_w_out, xa_norm=v_xa_norm, mem_norm=v_mem_norm, xa_wq=v_xa_wq, xa_wkv=v_xa_wkv, xa_q_norm=v_xa_q_norm, xa_k_norm=v_xa_k_norm, xa_wo=v_xa_wo, ffn_norm=v_ffn_norm, ffn_up=v_ffn_up, ffn_conv=v_ffn_conv, ffn_conv_b=v_ffn_conv_b, ffn_down=v_ffn_down)
    ix, iy, ic = lax.axis_index("x"), lax.axis_index("y"), lax.axis_index("c")
    chip = 2 * ix + iy
    chipvec = chip.astype(jnp.int32).reshape(1)
    cvec = ic.astype(jnp.int32).reshape(1)
    order = jnp.stack([chip, ic] + [4 * px + 2 * py + pc for px, py, pc in _peers(ix, iy, ic)]).astype(jnp.int32)

    w_in_all, later = _gather_step(w, chipvec)
    sp = {n: w[n][0] for n in SMALL if n != "ffn_conv"}
    started = {}

    def emit(name, g):
        *started[name], token = partials_start(g, name="partials_start_" + name)
        return token

    loss_acc, grad_x, packed = _local_step(x[0], mem[0], positions[0], loss_target[0], w_in_all, later, sp, emit)
    grads, delta, new_m, new_v = _reduce_update(started, packed, w, m, v, chipvec, cvec, order)
    loss = lax.psum(loss_acc[0, 0], ("x", "y", "c"))
    ordered = lambda d: [d[n] for n in WEIGHTS]
    return (loss, grad_x[None], *ordered(grads), *ordered(delta), *ordered(new_m), *ordered(new_v))
```

```python
import math

import jax
import jax.numpy as jnp
from jax import lax
from jax.experimental import pallas as pl
from jax.experimental.pallas import tpu as pltpu

F32 = jnp.float32
BF16 = jnp.bfloat16
MXU_DTYPE = jnp.bfloat16
WIRE_DTYPE = jnp.bfloat16
EPS = 1e-6
VMEM_LIMIT_V7X = 56 * 1024 * 1024

D_MODEL = 1024
HEAD_DIM = 64
BLK = 128
XA_HEADS = 4
XA_DH = 256
MEM_LEN = 256
D_FF = 2816
IN_COLS_DUP = 2048
N_CHIPS = 4
N_DEV = 8

ADAM_LR = 0.001
ADAM_B1 = 0.9
ADAM_B2 = 0.999
ADAM_EPS = 1e-08
ADAM_WD = 0.01
ADAM_STEP = 10

NT = (((1,), (1,)), ((), ()))
TN = (((0,), (0,)), ((), ()))
NN = (((1,), (0,)), ((), ()))
MINF = float(jnp.finfo(jnp.float32).min)
GELU_K0 = math.sqrt(2.0 / math.pi)
GELU_K1 = 0.044715

BS = pl.BlockSpec
SDS = jax.ShapeDtypeStruct
ANY = pl.BlockSpec(memory_space=pl.ANY)
MESH = pl.DeviceIdType.MESH


def _dot(a, b, dims=NN):
    return lax.dot_general(a.astype(MXU_DTYPE), b.astype(MXU_DTYPE), dims, preferred_element_type=F32)


def _segsum(x, bmat):
    hi = x.astype(BF16)
    lo = (x - hi.astype(F32)).astype(BF16)
    return (jnp.dot(hi, bmat, preferred_element_type=F32) + jnp.dot(lo, bmat, preferred_element_type=F32))


def _gelu(x):
    return 0.5 * x * (1.0 + jnp.tanh(GELU_K0 * (x + GELU_K1 * x * x * x)))


def _gelu_grad(x):
    t = jnp.tanh(GELU_K0 * (x + GELU_K1 * x * x * x))
    return 0.5 * (1.0 + t) + 0.5 * x * (1.0 - t * t) * GELU_K0 * (1.0 + 3.0 * GELU_K1 * x * x)


def _rms(x):
    return lax.rsqrt(jnp.mean(x * x, axis=-1, keepdims=True) + EPS)


def _rms_bwd(dy, x, g, r):
    dyg = dy * g
    dx = r * dyg - x * (r * r * r) * jnp.mean(dyg * x, axis=-1, keepdims=True)
    return dx, dy * x * r


def _pcall(body, *, name, grid, in_specs, out_specs, out_shape, scratch=(), prefetch=0, after=None):
    params = pltpu.CompilerParams(dimension_semantics=("arbitrary",) * len(grid), vmem_limit_bytes=VMEM_LIMIT_V7X)
    in_specs = list(in_specs)
    kernel_fn = body
    if after is not None:
        n_in = prefetch + len(in_specs)
        in_specs.append(ANY)

        def kernel_fn(*refs):
            return body(*refs[:n_in], *refs[n_in + 1:])

    if prefetch:
        spec = pltpu.PrefetchScalarGridSpec(num_scalar_prefetch=prefetch, grid=grid, in_specs=in_specs,
                                            out_specs=out_specs, scratch_shapes=list(scratch))
        call = pl.pallas_call(kernel_fn, name=name, grid_spec=spec, out_shape=out_shape, compiler_params=params)
    else:
        call = pl.pallas_call(kernel_fn, name=name, grid=grid, in_specs=in_specs, out_specs=out_specs,
                              out_shape=out_shape, scratch_shapes=list(scratch), compiler_params=params)
    return call if after is None else (lambda *args: call(*args, after))


def _tile(n, prefs):
    for p in prefs:
        if p <= n and n % p == 0:
            return p
    return n


def _resident(shape):
    return pl.BlockSpec(shape, lambda *_: (0,) * len(shape), pipeline_mode=pl.Buffered(1))


def _acc_rows(ref, row, val):
    ref[row:row + 1, :] += jnp.sum(val, axis=0, keepdims=True)


def rms_mm(x, g, w3, *, name, tm=1024):
    M, K = x.shape
    Q, _, C = w3.shape
    tm = _tile(M, (tm, 256))

    def body(x_ref, g_ref, w_ref, h_ref, o_ref):
        @pl.when(pl.program_id(1) == 0)
        def _():
            xv = x_ref[...]
            h_ref[...] = (xv * _rms(xv) * g_ref[...]).astype(h_ref.dtype)

        o_ref[...] = _dot(h_ref[...], w_ref[pl.program_id(1)])

    return _pcall(body, name=name, grid=(M // tm, Q),
                  in_specs=[BS((tm, K), lambda i, j: (i, 0)), BS((1, K), lambda i, j: (0, 0)),
                            _resident((Q, K, C))],
                  out_specs=[BS((tm, K), lambda i, j: (i, 0)), BS((tm, C), lambda i, j: (i, j))],
                  out_shape=[SDS((M, K), MXU_DTYPE), SDS((M, Q * C), F32)])(x, g, w3)


def mm(a, w, *, name, res):
    M, K = a.shape
    N = w.shape[1]
    tm = _tile(M, (1024, 256))

    def body(a_ref, w_ref, r_ref, o_ref):
        o_ref[...] = _dot(a_ref[...], w_ref[...]) + r_ref[...]

    return _pcall(body, name=name, grid=(M // tm,),
                  in_specs=[BS((tm, K), lambda i: (i, 0)), _resident((K, N)), BS((tm, N), lambda i: (i, 0))],
                  out_specs=BS((tm, N), lambda i: (i, 0)), out_shape=SDS((M, N), F32))(a, w, res)


def _nt_chunks(a_ref, w_ref):
    q_n, _, kc = w_ref.shape
    acc = _dot(a_ref[:, 0:kc], w_ref[0], NT)
    for q in range(1, q_n):
        acc = acc + _dot(a_ref[:, q * kc:(q + 1) * kc], w_ref[q], NT)
    return acc


def mm_nt(a, w3, *, name, after=None):
    M = a.shape[0]
    Q, N, Kc = w3.shape
    tm = _tile(M, (1024, 256))

    def body(a_ref, w_ref, o_ref):
        o_ref[...] = _nt_chunks(a_ref, w_ref)

    return _pcall(body, name=name, grid=(M // tm,), after=after,
                  in_specs=[BS((tm, Q * Kc), lambda i: (i, 0)), _resident((Q, N, Kc))],
                  out_specs=BS((tm, N), lambda i: (i, 0)), out_shape=SDS((M, N), F32))(a, w3)


def mm_nt_rms_bwd(a, w3, x, g, dres, *, name, tm=512, after=None):
    M = a.shape[0]
    Q, N, Kc = w3.shape
    tm = _tile(M, (tm, 256))

    def body(a_ref, w_ref, x_ref, g_ref, dr_ref, dx_ref, dg_ref):
        @pl.when(pl.program_id(0) == 0)
        def _():
            dg_ref[...] = jnp.zeros_like(dg_ref)

        xv = x_ref[...]
        dx, dgc = _rms_bwd(_nt_chunks(a_ref, w_ref), xv, g_ref[...], _rms(xv))
        dx_ref[...] = dr_ref[...] + dx
        _acc_rows(dg_ref, 0, dgc)

    row = BS((tm, N), lambda i: (i, 0))
    return _pcall(body, name=name, grid=(M // tm,), after=after,
                  in_specs=[BS((tm, Q * Kc), lambda i: (i, 0)), _resident((Q, N, Kc)), row,
                            BS((1, N), lambda i: (0, 0)), row],
                  out_specs=[row, BS((8, N), lambda i: (0, 0))],
                  out_shape=[SDS((M, N), F32), SDS((8, N), F32)])(a, w3, x, g, dres)


def mm_nt_post_bwd(a, w3, attn, gm, gao, ggo, *, name, after=None):
    M = a.shape[0]
    Q, N, Kc = w3.shape
    tm = _tile(M, (512, 256))
    hw = N // 2

    def body(a_ref, w_ref, at_ref, gm_ref, gao_ref, ggo_ref, da_ref, dgm_ref, dg_ref):
        @pl.when(pl.program_id(0) == 0)
        def _():
            dg_ref[...] = jnp.zeros_like(dg_ref)

        dy = _nt_chunks(a_ref, w_ref)
        av, gmv = at_ref[...], gm_ref[...]
        da, dga = _rms_bwd(dy[:, :hw], av, gao_ref[...], _rms(av))
        dgm, dgg = _rms_bwd(dy[:, hw:], gmv, ggo_ref[...], _rms(gmv))
        da_ref[...] = da
        dgm_ref[...] = dgm
        dg_ref[0:1, :hw] += jnp.sum(dga, axis=0, keepdims=True)
        dg_ref[0:1, hw:] += jnp.sum(dgg, axis=0, keepdims=True)

    half = BS((tm, hw), lambda i: (i, 0))
    const = lambda r, w: BS((r, w), lambda i: (0, 0))
    return _pcall(body, name=name, grid=(M // tm,), after=after,
                  in_specs=[BS((tm, Q * Kc), lambda i: (i, 0)), _resident((Q, N, Kc)), half, half,
                            const(1, hw), const(1, hw)],
                  out_specs=[half, half, const(8, N)],
                  out_shape=[SDS((M, hw), F32), SDS((M, hw), F32), SDS((8, N), F32)])(a, w3, attn, gm, gao, ggo)


def mm_tn(a, b, *, name, out_dtype, chunks=1):
    M, K = a.shape
    N = b.shape[1]
    C = N // chunks
    tm = _tile(M, (1024, 256))
    tk = _tile(K, (1408, 1024, 512))
    tn = _tile(C, (1408, 1024, 512))
    per = C // tn
    nm = M // tm

    def body(a_ref, b_ref, o_ref, acc):
        m = pl.program_id(2)

        @pl.when(m == 0)
        def _():
            acc[...] = jnp.zeros_like(acc)

        acc[...] += _dot(a_ref[...], b_ref[...], TN)

        @pl.when(m == nm - 1)
        def _():
            o_ref[...] = acc[...].astype(o_ref.dtype)

    return _pcall(body, name=name, grid=(K // tk, N // tn, nm),
                  in_specs=[BS((tm, tk), lambda k, n, m: (m, k)), BS((tm, tn), lambda k, n, m: (m, n))],
                  out_specs=BS((None, tk, tn), lambda k, n, m: (n // per, k, n % per)),
                  out_shape=SDS((chunks, K, C), out_dtype), scratch=[pltpu.VMEM((tk, tn), F32)])(a, b)


def _lane(shape):
    return lax.broadcasted_iota(jnp.int32, shape, 1)


def _head_means(slabs, bmat):
    tm = slabs[0].shape[0]
    means = _segsum(jnp.concatenate(slabs, axis=0), bmat) * (1.0 / HEAD_DIM)
    return [means[i * tm:(i + 1) * tm] for i in range(len(slabs))]


def _half_swap(x, first):
    return jnp.where(first, pltpu.roll(x, 96, 1), pltpu.roll(x, 32, 1))


def _by_head(x2, lo):
    z = jnp.zeros((BLK, 128), x2.dtype)
    parts = []
    for s in range(2):
        xs = x2[:, s * 128:(s + 1) * 128]
        parts += [jnp.where(lo, xs, z), jnp.where(lo, z, xs)]
    return jnp.concatenate(parts, axis=0)


def _from_heads(o4, lo):
    return jnp.concatenate([jnp.where(lo, o4[0:BLK], o4[BLK:2 * BLK]),
                            jnp.where(lo, o4[2 * BLK:3 * BLK], o4[3 * BLK:])], axis=1)


def _swa_probs(q2, kd, sink, n, lo):
    qp = _by_head(q2, lo)
    sc = _dot(qp, kd, NT) * (1.0 / math.sqrt(HEAD_DIM))
    r_i = lax.broadcasted_iota(jnp.int32, (4 * BLK, 2 * BLK), 0)
    k_j = lax.broadcasted_iota(jnp.int32, (4 * BLK, 2 * BLK), 1)
    diff = (r_i & (BLK - 1)) + BLK - k_j
    mask = (diff >= 0) & (diff < BLK) & ((k_j >= BLK) | (n > 0))
    sc = jnp.where(mask, sc, MINF)
    m = jnp.maximum(jnp.max(sc, axis=1, keepdims=True), sink)
    p = jnp.exp(sc - m)
    es = jnp.exp(sink - m)
    inv = 1.0 / (jnp.sum(p, axis=1, keepdims=True) + es)
    return qp, p * inv, es * inv


def mixer_core_fwd(proj, cos, sin, gq, gk, gvn, bmat, sinkcol, gao, w2, bsl, ggo):
    S = proj.shape[0]

    def body(p_ref, c_ref, s_ref, gq_ref, gk_ref, gvn_ref, b_ref, sk_ref, gao_ref, w2_ref, bsl_ref, ggo_ref,
             qr_ref, kr_ref, vb_ref, gu_ref, gvo_ref, at_ref, gm_ref, y_ref, k_prev, v_prev):
        n = pl.program_id(0)

        @pl.when(n == 0)
        def _():
            k_prev[...] = jnp.zeros_like(k_prev)
            v_prev[...] = jnp.zeros_like(v_prev)

        cos_v, sin_v, bm = c_ref[...], s_ref[...], b_ref[...]
        first = (_lane((BLK, 128)) & 63) < 32
        lo = _lane((BLK, 128)) < 64
        slabs = [p_ref[:, s * 128:(s + 1) * 128] for s in range(6)]
        for s, (slab, ms) in enumerate(zip(slabs, _head_means([x * x for x in slabs], bm))):
            qn = slab * lax.rsqrt(ms + EPS) * (gq_ref[...] if s < 4 else gk_ref[...])
            out = qn * cos_v + _half_swap(qn, first) * sin_v
            if s < 4:
                qr_ref[:, s * 128:(s + 1) * 128] = out.astype(qr_ref.dtype)
            else:
                kr_ref[:, (s - 4) * 128:(s - 3) * 128] = out.astype(kr_ref.dtype)
        vb_ref[...] = p_ref[:, 768:1024].astype(vb_ref.dtype)
        gu_ref[...] = _gelu(p_ref[:, 1024:1536])
        gv = _gelu(p_ref[:, 1536:2048])
        gvo_ref[...] = (gv * _rms(gv) * gvn_ref[...]).astype(gvo_ref.dtype)

        for h in range(2):
            hs, qs = slice(h * 128, (h + 1) * 128), slice(h * 256, (h + 1) * 256)
            kd = jnp.concatenate([k_prev[:, hs], kr_ref[:, hs]], axis=0)
            vd = jnp.concatenate([v_prev[:, hs], vb_ref[:, hs]], axis=0)
            sink = jnp.concatenate([sk_ref[2 * h], sk_ref[2 * h + 1]], axis=0)
            _, p, _ = _swa_probs(qr_ref[:, qs], kd, sink, n, lo)
            at_ref[:, qs] = _from_heads(_dot(p, vd), lo)
        k_prev[...] = kr_ref[...]
        v_prev[...] = vb_ref[...]

        for j in range(4):
            sl = slice(j * 128, (j + 1) * 128)
            m2 = _dot(w2_ref[j], gvo_ref[:, sl])
            mixed = jnp.where(lo, m2[:BLK], m2[BLK:]) + bsl_ref[j]
            gm_ref[:, sl] = gu_ref[:, sl] * mixed
        a, gm = at_ref[...], gm_ref[...]
        y_ref[:, :512] = (a * _rms(a) * gao_ref[...]).astype(y_ref.dtype)
        y_ref[:, 512:] = (gm * _rms(gm) * ggo_ref[...]).astype(y_ref.dtype)

    row = lambda w: BS((BLK, w), lambda n: (n, 0))
    const = lambda *shape: BS(shape, lambda n: (0,) * len(shape))
    return _pcall(body, name="mixer_core_fwd", grid=(S // BLK,),
                  in_specs=[row(IN_COLS_DUP), row(128), row(128), const(1, 128), const(1, 128), const(1, 512),
                            const(128, 128), const(4, 2 * BLK, 1), const(1, 512), const(4, 2 * BLK, BLK),
                            const(4, BLK, 128), const(1, 512)],
                  out_specs=[row(512), row(256), row(256), row(512), row(512), row(512), row(512), row(1024)],
                  out_shape=[SDS((S, 512), MXU_DTYPE), SDS((S, 256), MXU_DTYPE), SDS((S, 256), MXU_DTYPE),
                             SDS((S, 512), F32), SDS((S, 512), MXU_DTYPE), SDS((S, 512), F32), SDS((S, 512), F32),
                             SDS((S, 1024), MXU_DTYPE)],
                  scratch=[pltpu.VMEM((BLK, 256), MXU_DTYPE), pltpu.VMEM((BLK, 256), MXU_DTYPE)])(
        proj, cos, sin, gq, gk, gvn, bmat, sinkcol, gao, w2, bsl, ggo)


def mem_pre(kv, gxk):
    def body(kv_ref, g_ref, kn_ref, vb_ref):
        for h in range(XA_HEADS):
            sl = slice(h * XA_DH, (h + 1) * XA_DH)
            k = kv_ref[:, sl]
            kn_ref[:, sl] = (k * _rms(k) * g_ref[...]).astype(kn_ref.dtype)
        vb_ref[...] = kv_ref[:, 1024:2048].astype(vb_ref.dtype)

    full = lambda r, w: BS((r, w), lambda i: (0, 0))
    return _pcall(body, name="mem_pre", grid=(1,), in_specs=[full(MEM_LEN, 2048), full(1, XA_DH)],
                  out_specs=[full(MEM_LEN, 1024), full(MEM_LEN, 1024)],
                  out_shape=[SDS((MEM_LEN, 1024), MXU_DTYPE), SDS((MEM_LEN, 1024), MXU_DTYPE)])(kv, gxk)


def _xa_probs(qh, g, kn_h):
    r = _rms(qh)
    qn = qh * r * g
    s = _dot(qn, kn_h, NT) * (1.0 / math.sqrt(XA_DH))
    p = jnp.exp(s - jnp.max(s, axis=1, keepdims=True))
    return r, qn, p * (1.0 / jnp.sum(p, axis=1, keepdims=True))


def xattn_fwd(qx, kn, vb, gxq):
    S = qx.shape[0]
    tm = _tile(S, (512, 256))

    def body(q_ref, kn_ref, vb_ref, g_ref, o_ref):
        for h in range(XA_HEADS):
            sl = slice(h * XA_DH, (h + 1) * XA_DH)
            _, _, p = _xa_probs(q_ref[:, sl], g_ref[...], kn_ref[:, sl])
            o_ref[:, sl] = _dot(p, vb_ref[:, sl]).astype(o_ref.dtype)

    full = lambda r, w: BS((r, w), lambda i: (0, 0))
    return _pcall(body, name="xattn_fwd", grid=(S // tm,),
                  in_specs=[BS((tm, 1024), lambda i: (i, 0)), full(MEM_LEN, 1024), full(MEM_LEN, 1024), full(1, XA_DH)],
                  out_specs=BS((tm, 1024), lambda i: (i, 0)), out_shape=SDS((S, 1024), MXU_DTYPE))(qx, kn, vb, gxq)


CONV_COLS = 1408


def _conv_taps(a_ref, halo_ref, w_ref, b_ref, cols, first_tile):
    a = a_ref[:, cols]
    row = lax.broadcasted_iota(jnp.int32, a.shape, 0)
    h6 = jnp.where(first_tile, 0.0, halo_ref[6:7, cols])
    h7 = jnp.where(first_tile, 0.0, halo_ref[7:8, cols])
    a1 = jnp.where(row == 0, h7, pltpu.roll(a, 1, 0))
    a2 = jnp.where(row == 0, h6, jnp.where(row == 1, h7, pltpu.roll(a, 2, 0)))
    c = w_ref[2:3, cols] * a + w_ref[1:2, cols] * a1 + w_ref[0:1, cols] * a2 + b_ref[:, cols]
    return c, (a2, a1, a)


def _conv_specs(tm):
    halo_blocks = tm // 8
    return [BS((tm, D_FF), lambda i: (i, 0)), BS((tm, D_FF), lambda i: (i, 1)),
            BS((8, D_FF), lambda i: (jnp.maximum(i * halo_blocks - 1, 0), 0)),
            BS((8, D_FF), lambda i: (jnp.maximum(i * halo_blocks - 1, 0), 1)),
            BS((3, D_FF), lambda i: (0, 0)), BS((3, D_FF), lambda i: (0, 1)),
            BS((1, D_FF), lambda i: (0, 0)), BS((1, D_FF), lambda i: (0, 1))]


def convgate_down_loss(a, cw, cb, w, res, target):
    S = a.shape[0]
    N = w.shape[1]
    tm = _tile(S, (256,))

    def body(ag_ref, au_ref, hg_ref, hu_ref, wg_ref, wu_ref, bg_ref, bu_ref, w_ref, r_ref, t_ref, f_ref, d_ref,
             l_ref):
        first_tile = pl.program_id(0) == 0

        @pl.when(first_tile)
        def _():
            l_ref[...] = jnp.zeros_like(l_ref)

        for c0 in range(0, D_FF, CONV_COLS):
            cols = slice(c0, c0 + CONV_COLS)
            cg, _ = _conv_taps(ag_ref, hg_ref, wg_ref, bg_ref, cols, first_tile)
            cu, _ = _conv_taps(au_ref, hu_ref, wu_ref, bu_ref, cols, first_tile)
            f_ref[:, cols] = (_gelu(cg) * cu).astype(f_ref.dtype)
        e = _dot(f_ref[...], w_ref[...]) + r_ref[...] - t_ref[...]
        d_ref[...] = e * (1.0 / N)
        l_ref[...] += jnp.sum(e * e) * (0.5 / N)

    row_n = BS((tm, N), lambda i: (i, 0))
    return _pcall(body, name="convgate_down_loss", grid=(S // tm,),
                  in_specs=_conv_specs(tm) + [_resident((D_FF, N)), row_n, row_n],
                  out_specs=[BS((tm, D_FF), lambda i: (i, 0)), row_n, BS((8, 128), lambda i: (0, 0))],
                  out_shape=[SDS((S, D_FF), MXU_DTYPE), SDS((S, N), F32), SDS((8, 128), F32)])(
        a, a, a, a, cw, cw, cb, cb, w, res, target)


def convgate_bwd(a, dx3, w3, cw, cb, after=None):
    S = a.shape[0]
    tm = _tile(S, (256,))

    def body(ag_ref, au_ref, hg_ref, hu_ref, wg_ref, wu_ref, bg_ref, bu_ref, dx_ref, wd_ref, dc_ref, gw_ref, df_ref):
        first_tile = pl.program_id(0) == 0

        @pl.when(first_tile)
        def _():
            gw_ref[...] = jnp.zeros_like(gw_ref)

        df_ref[...] = _nt_chunks(dx_ref, wd_ref)
        for c0 in range(0, D_FF, CONV_COLS):
            cols, ucols = slice(c0, c0 + CONV_COLS), slice(D_FF + c0, D_FF + c0 + CONV_COLS)
            cg, g_taps = _conv_taps(ag_ref, hg_ref, wg_ref, bg_ref, cols, first_tile)
            cu, u_taps = _conv_taps(au_ref, hu_ref, wu_ref, bu_ref, cols, first_tile)
            df_v = df_ref[:, cols]
            dcg = df_v * cu * _gelu_grad(cg)
            dcu = df_v * _gelu(cg)
            dc_ref[:, cols] = dcg
            dc_ref[:, ucols] = dcu
            for col, dcv, taps in ((cols, dcg, g_taps), (ucols, dcu, u_taps)):
                for j in range(3):
                    gw_ref[j:j + 1, col] += jnp.sum(dcv * taps[j], axis=0, keepdims=True)
                gw_ref[3:4, col] += jnp.sum(dcv, axis=0, keepdims=True)

    return _pcall(body, name="convgate_bwd", grid=(S // tm,), after=after,
                  in_specs=_conv_specs(tm) + [BS((tm, dx3.shape[1]), lambda i: (i, 0)), _resident(w3.shape)],
                  out_specs=[BS((tm, 2 * D_FF), lambda i: (i, 0)), BS((8, 2 * D_FF), lambda i: (0, 0))],
                  out_shape=[SDS((S, 2 * D_FF), F32), SDS((8, 2 * D_FF), F32)],
                  scratch=[pltpu.VMEM((tm, D_FF), F32)])(a, a, a, a, cw, cw, cb, cb, dx3, w3)


def conv_transpose_rms_bwd(dc, cw, w3, x, g, dres):
    S, C = dc.shape
    Q, N, Kc = w3.shape
    tm = _tile(S, (256,))
    nt = S // tm
    halo_blocks = tm // 8

    def body(dc_ref, halo_ref, cw_ref, w_ref, x_ref, g_ref, dr_ref, da_ref, dx_ref, dg_ref):
        @pl.when(pl.program_id(0) == 0)
        def _():
            dg_ref[...] = jnp.zeros_like(dg_ref)

        last_tile = pl.program_id(0) == nt - 1
        row = lax.broadcasted_iota(jnp.int32, (tm, CONV_COLS), 0)
        for c0 in range(0, C, CONV_COLS):
            cols = slice(c0, c0 + CONV_COLS)
            h0 = jnp.where(last_tile, 0.0, halo_ref[0:1, cols])
            h1 = jnp.where(last_tile, 0.0, halo_ref[1:2, cols])
            dc_v = dc_ref[:, cols]
            n1 = jnp.where(row == tm - 1, h0, pltpu.roll(dc_v, tm - 1, 0))
            n2 = jnp.where(row == tm - 1, h1, jnp.where(row == tm - 2, h0, pltpu.roll(dc_v, tm - 2, 0)))
            da_ref[:, cols] = (cw_ref[2:3, cols] * dc_v + cw_ref[1:2, cols] * n1
                               + cw_ref[0:1, cols] * n2).astype(da_ref.dtype)
        xv = x_ref[...]
        dx, dgc = _rms_bwd(_nt_chunks(da_ref, w_ref), xv, g_ref[...], _rms(xv))
        dx_ref[...] = dr_ref[...] + dx
        _acc_rows(dg_ref, 0, dgc)

    row_n = BS((tm, N), lambda i: (i, 0))
    return _pcall(body, name="conv_transpose_rms_bwd", grid=(nt,),
                  in_specs=[BS((tm, C), lambda i: (i, 0)),
                            BS((8, C), lambda i: (jnp.minimum((i + 1) * halo_blocks, S // 8 - 1), 0)),
                            BS((3, C), lambda i: (0, 0)), _resident((Q, N, Kc)), row_n, BS((1, N), lambda i: (0, 0)),
                            row_n],
                  out_specs=[BS((tm, C), lambda i: (i, 0)), row_n, BS((8, N), lambda i: (0, 0))],
                  out_shape=[SDS((S, C), MXU_DTYPE), SDS((S, N), F32), SDS((8, N), F32)])(dc, dc, cw, w3, x, g, dres)


def xattn_bwd(qx, dxo, kn, vb, gxq, after=None):
    S = qx.shape[0]
    tm = _tile(S, (512, 256))

    def body(q_ref, do_ref, kn_ref, vb_ref, g_ref, dq_ref, dkn_ref, dv_ref, dg_ref):
        @pl.when(pl.program_id(0) == 0)
        def _():
            dkn_ref[...] = jnp.zeros_like(dkn_ref)
            dv_ref[...] = jnp.zeros_like(dv_ref)
            dg_ref[...] = jnp.zeros_like(dg_ref)

        g = g_ref[...]
        for h in range(XA_HEADS):
            sl = slice(h * XA_DH, (h + 1) * XA_DH)
            qh, do = q_ref[:, sl], do_ref[:, sl]
            r, qn, p = _xa_probs(qh, g, kn_ref[:, sl])
            dp = _dot(do, vb_ref[:, sl], NT)
            ds = p * (dp - jnp.sum(dp * p, axis=1, keepdims=True)) * (1.0 / math.sqrt(XA_DH))
            dqn = _dot(ds, kn_ref[:, sl])
            dkn_ref[:, sl] += _dot(ds, qn, TN)
            dv_ref[:, sl] += _dot(p, do, TN)
            dqh, dgc = _rms_bwd(dqn, qh, g, r)
            dq_ref[:, sl] = dqh.astype(dq_ref.dtype)
            _acc_rows(dg_ref, 0, dgc)

    row = BS((tm, 1024), lambda i: (i, 0))
    full = lambda r, w: BS((r, w), lambda i: (0, 0))
    return _pcall(body, name="xattn_bwd", grid=(S // tm,), after=after,
                  in_specs=[row, row, full(MEM_LEN, 1024), full(MEM_LEN, 1024), full(1, XA_DH)],
                  out_specs=[row, full(MEM_LEN, 1024), full(MEM_LEN, 1024), full(8, XA_DH)],
                  out_shape=[SDS((S, 1024), MXU_DTYPE), SDS((MEM_LEN, 1024), F32), SDS((MEM_LEN, 1024), F32),
                             SDS((8, XA_DH), F32)])(qx, dxo, kn, vb, gxq)


def mem_bwd(kv, dkn, dvb, gxk, after=None):
    def body(kv_ref, dkn_ref, dv_ref, g_ref, dkv_ref, dg_ref):
        dg_ref[...] = jnp.zeros_like(dg_ref)
        for h in range(XA_HEADS):
            sl = slice(h * XA_DH, (h + 1) * XA_DH)
            k = kv_ref[:, sl]
            dk, dgc = _rms_bwd(dkn_ref[:, sl], k, g_ref[...], _rms(k))
            dkv_ref[:, sl] = dk.astype(dkv_ref.dtype)
            _acc_rows(dg_ref, 0, dgc)
        dkv_ref[:, 1024:2048] = dv_ref[...].astype(dkv_ref.dtype)

    full = lambda r, w: BS((r, w), lambda i: (0, 0))
    return _pcall(body, name="mem_bwd", grid=(1,), after=after,
                  in_specs=[full(MEM_LEN, 2048), full(MEM_LEN, 1024), full(MEM_LEN, 1024), full(1, XA_DH)],
                  out_specs=[full(MEM_LEN, 2048), full(8, XA_DH)],
                  out_shape=[SDS((MEM_LEN, 2048), MXU_DTYPE), SDS((8, XA_DH), F32)])(kv, dkn, dvb, gxk)


def gmlp_bwd(dgm, gvn, gu, w2, w2t, bsl, after=None):
    S = dgm.shape[0]

    def body(dgm_ref, gvn_ref, gu_ref, w2_ref, w2t_ref, bsl_ref, dgu_ref, dgvn_ref, dws_ref, dbl_ref):
        @pl.when(pl.program_id(0) == 0)
        def _():
            dws_ref[...] = jnp.zeros_like(dws_ref)
            dbl_ref[...] = jnp.zeros_like(dbl_ref)

        lo = _lane((BLK, 128)) < 64
        for j in range(4):
            sl = slice(j * 128, (j + 1) * 128)
            gvn_s = gvn_ref[:, sl]
            m2 = _dot(w2_ref[j], gvn_s)
            mixed = jnp.where(lo, m2[:BLK], m2[BLK:]) + bsl_ref[j]
            dgm_s = dgm_ref[:, sl]
            dgu_ref[:, sl] = dgm_s * mixed
            dmx = dgm_s * gu_ref[:, sl]
            d2 = _dot(w2t_ref[j], dmx)
            dgvn_ref[:, sl] = jnp.where(lo, d2[:BLK], d2[BLK:])
            z = jnp.zeros_like(dmx)
            dws_ref[2 * j] += _dot(jnp.where(lo, dmx, z), gvn_s, NT)
            dws_ref[2 * j + 1] += _dot(jnp.where(lo, z, dmx), gvn_s, NT)
            dbl_ref[j] += dmx

    row = lambda w: BS((BLK, w), lambda n: (n, 0))
    const3 = lambda a, b, c: BS((a, b, c), lambda n: (0, 0, 0))
    return _pcall(body, name="gmlp_bwd", grid=(S // BLK,), after=after,
                  in_specs=[row(512), row(512), row(512), const3(4, 2 * BLK, BLK), const3(4, 2 * BLK, BLK),
                            const3(4, BLK, 128)],
                  out_specs=[row(512), row(512), const3(8, BLK, BLK), const3(4, BLK, 128)],
                  out_shape=[SDS((S, 512), F32), SDS((S, 512), F32), SDS((8, BLK, BLK), F32),
                             SDS((4, BLK, 128), F32)])(dgm, gvn, gu, w2, w2t, bsl)


def swa_bwd(qr, kr, vb, sinkcol, dattn):
    S = qr.shape[0]
    nb = S // BLK

    def body(q_ref, kc_ref, kp_ref, vc_ref, vp_ref, sk_ref, do_ref, dq_ref, dk_ref, dv_ref, dsk_ref,
             carry_k, carry_v, prev_k, prev_v):
        n = pl.program_id(0)

        @pl.when(n == 0)
        def _():
            dsk_ref[...] = jnp.zeros_like(dsk_ref)
            carry_k[...] = jnp.zeros_like(carry_k)
            carry_v[...] = jnp.zeros_like(carry_v)

        @pl.when(n < nb)
        def _():
            lo = _lane((BLK, 128)) < 64
            for h in range(2):
                hs, qs = slice(h * 128, (h + 1) * 128), slice(h * 256, (h + 1) * 256)
                kd = jnp.concatenate([kp_ref[:, hs], kc_ref[:, hs]], axis=0)
                vd = jnp.concatenate([vp_ref[:, hs], vc_ref[:, hs]], axis=0)
                sink = jnp.concatenate([sk_ref[2 * h], sk_ref[2 * h + 1]], axis=0)
                qp, p, psink = _swa_probs(q_ref[:, qs], kd, sink, n, lo)
                dop = _by_head(do_ref[:, qs], lo)
                dp = _dot(dop, vd, NT)
                delta = jnp.sum(dp * p, axis=1, keepdims=True)
                ds = p * (dp - delta) * (1.0 / math.sqrt(HEAD_DIM))
                dsink = -psink * delta
                dsk_ref[2 * h] += dsink[:2 * BLK]
                dsk_ref[2 * h + 1] += dsink[2 * BLK:]
                dq_ref[:, qs] = _from_heads(_dot(ds, kd), lo)
                dkd = _dot(ds, qp, TN)
                dvd = _dot(p, dop, TN)
                prev_k[:, hs] = carry_k[:, hs] + dkd[:BLK]
                prev_v[:, hs] = carry_v[:, hs] + dvd[:BLK]
                carry_k[:, hs] = dkd[BLK:]
                carry_v[:, hs] = dvd[BLK:]

        @pl.when(n == nb)
        def _():
            prev_k[...] = carry_k[...]
            prev_v[...] = carry_v[...]

        dk_ref[...] = prev_k[...]
        dv_ref[...] = prev_v[...]

    last = nb - 1
    cur = lambda w: BS((BLK, w), lambda n: (jnp.minimum(n, last), 0))
    prev = lambda w: BS((BLK, w), lambda n: (jnp.clip(n - 1, 0, last), 0))
    done = lambda w: BS((BLK, w), lambda n: (jnp.maximum(n - 1, 0), 0))
    return _pcall(body, name="swa_bwd", grid=(nb + 1,),
                  in_specs=[cur(512), cur(256), prev(256), cur(256), prev(256),
                            BS((4, 2 * BLK, 1), lambda n: (0, 0, 0)), cur(512)],
                  out_specs=[cur(512), done(256), done(256), BS((4, 2 * BLK, 1), lambda n: (0, 0, 0))],
                  out_shape=[SDS((S, 512), F32), SDS((S, 256), F32), SDS((S, 256), F32), SDS((4, 2 * BLK, 1), F32)],
                  scratch=[pltpu.VMEM((BLK, 256), F32)] * 4)(qr, kr, kr, vb, vb, sinkcol, dattn)


def mixer_pre_bwd(proj, cos, sin, gq, gk, gvn, bmat, dqr, dkr, dvb, dgu, dgvn):
    S = proj.shape[0]
    tm = _tile(S, (256,))

    def body(p_ref, c_ref, s_ref, gq_ref, gk_ref, gvn_ref, b_ref, dqr_ref, dkr_ref, dvb_ref, dgu_ref, dgvn_ref,
             dp_ref, dgq_ref, dgk_ref, dgv_ref):
        @pl.when(pl.program_id(0) == 0)
        def _():
            dgq_ref[...] = jnp.zeros_like(dgq_ref)
            dgk_ref[...] = jnp.zeros_like(dgk_ref)
            dgv_ref[...] = jnp.zeros_like(dgv_ref)

        cos_v, sin_v, bm = c_ref[...], s_ref[...], b_ref[...]
        first = (_lane((tm, 128)) & 63) < 32

        slabs = [p_ref[:, s * 128:(s + 1) * 128] for s in range(6)]
        douts = [dqr_ref[:, s * 128:(s + 1) * 128] for s in range(4)] + [dkr_ref[:, s * 128:(s + 1) * 128] for s in range(2)]
        gains = [gq_ref[...]] * 4 + [gk_ref[...]] * 2
        dqns = [d * cos_v + _half_swap(d * sin_v, first) for d in douts]
        rs = [lax.rsqrt(ms + EPS) for ms in _head_means([x * x for x in slabs], bm)]
        projs = _head_means([dqn * g * x for dqn, g, x in zip(dqns, gains, slabs)], bm)
        for s, (slab, dqn, g, r, pr) in enumerate(zip(slabs, dqns, gains, rs, projs)):
            dx = r * (dqn * g) - slab * (r * r * r) * pr
            dp_ref[:, s * 128:(s + 1) * 128] = dx.astype(dp_ref.dtype)
            _acc_rows(dgq_ref if s < 4 else dgk_ref, 0, dqn * slab * r)
        dp_ref[:, 768:1024] = dvb_ref[...].astype(dp_ref.dtype)
        dp_ref[:, 1024:1536] = (dgu_ref[...] * _gelu_grad(p_ref[:, 1024:1536])).astype(dp_ref.dtype)
        gvp = p_ref[:, 1536:2048]
        gv = _gelu(gvp)
        dgv, dgc = _rms_bwd(dgvn_ref[...], gv, gvn_ref[...], _rms(gv))
        dp_ref[:, 1536:2048] = (dgv * _gelu_grad(gvp)).astype(dp_ref.dtype)
        _acc_rows(dgv_ref, 0, dgc)

    row = lambda w: BS((tm, w), lambda i: (i, 0))
    const = lambda r, w: BS((r, w), lambda i: (0, 0))
    return _pcall(body, name="mixer_pre_bwd", grid=(S // tm,),
                  in_specs=[row(IN_COLS_DUP), row(128), row(128), const(1, 128), const(1, 128), const(1, 512),
                            const(128, 128), row(512), row(256), row(256), row(512), row(512)],
                  out_specs=[row(IN_COLS_DUP), const(8, 128), const(8, 128), const(8, 512)],
                  out_shape=[SDS((S, IN_COLS_DUP), MXU_DTYPE), SDS((8, 128), F32), SDS((8, 128), F32),
                             SDS((8, 512), F32)])(proj, cos, sin, gq, gk, gvn, bmat, dqr, dkr, dvb, dgu, dgvn)


BIG = (("w_in", (1024, 448), True), ("w_out", (256, 1024), False), ("xa_wq", (256, 1024), False),
       ("xa_wkv", (1024, 512), True), ("xa_wo", (256, 1024), False), ("ffn_up", (1024, 1408), True),
       ("ffn_down", (704, 1024), False))
BIG_NAMES = tuple(n for n, _, _ in BIG)
SMALL_VECS = (("mix_norm", 1024), ("q_norm", 64), ("k_norm", 64), ("attn_sinks", 8), ("gmlp_v_norm", 512),
              ("attn_out_norm", 512), ("gmlp_out_norm", 512), ("xa_norm", 1024), ("mem_norm", 1024),
              ("xa_q_norm", 256), ("xa_k_norm", 256), ("ffn_norm", 1024), ("ffn_conv_b", 5632))
SMALL = tuple(n for n, _ in SMALL_VECS) + ("gmlp_bs", "gmlp_ws", "ffn_conv")
WEIGHTS = ("mix_norm", "w_in", "q_norm", "k_norm", "attn_sinks", "gmlp_v_norm", "gmlp_ws", "gmlp_bs",
           "attn_out_norm", "gmlp_out_norm", "w_out", "xa_norm", "mem_norm", "xa_wq", "xa_wkv", "xa_q_norm",
           "xa_k_norm", "xa_wo", "ffn_norm", "ffn_up", "ffn_conv", "ffn_conv_b", "ffn_down")
CONV_SHARD = (3, 1408)
CONV_LANE_ROWS = CONV_SHARD[1] // 128
CONV_CHIP_ROWS = 40


def _small_rows():
    rows, r = {}, 0
    for n, length in SMALL_VECS:
        rows[n] = r
        r += -(-length // 128)
    r += -r % 8
    rows["gmlp_bs"] = r
    r += 8
    rows["gmlp_ws"] = r
    r += 8 * BLK
    rows["ffn_conv"] = r
    r += N_CHIPS * CONV_CHIP_ROWS
    return rows, r


SMALL_ROW, SMALL_ROWS = _small_rows()


def pack_small(dg_mix, dgq, dgk, dsk, dg_gvn, dg_y, dg_xa, dg_mem, dg_xq, dg_xk, dg_ffn, gcw, dbl, dws):
    def body(mix_ref, q_ref, k_ref, sk_ref, gvn_ref, y_ref, xa_ref, mem_ref, xq_ref, xk_ref, ffn_ref, cw_ref,
             dbl_ref, dws_ref, o_ref):
        o_ref[...] = jnp.zeros_like(o_ref)
        lane = _lane((1, 128))

        def put(name, src_ref, row, lane0, length):
            for k in range(length // 128):
                o_ref[SMALL_ROW[name] + k:SMALL_ROW[name] + k + 1, :] = src_ref[row:row + 1, lane0 + k * 128:lane0 + (k + 1) * 128]

        put("mix_norm", mix_ref, 0, 0, 1024)
        for name, ref in (("q_norm", q_ref), ("k_norm", k_ref)):
            v = ref[0:1, :]
            o_ref[SMALL_ROW[name]:SMALL_ROW[name] + 1, :] = jnp.where(lane < HEAD_DIM, v + pltpu.roll(v, 64, 1), 0.0)
        sinks = jnp.zeros((1, 128), F32)
        for s in range(4):
            col = sk_ref[s]
            sinks = sinks + jnp.where(lane == 2 * s, jnp.sum(col[:BLK]), 0.0) + jnp.where(lane == 2 * s + 1, jnp.sum(col[BLK:]), 0.0)
        o_ref[SMALL_ROW["attn_sinks"]:SMALL_ROW["attn_sinks"] + 1, :] = sinks
        put("gmlp_v_norm", gvn_ref, 0, 0, 512)
        put("attn_out_norm", y_ref, 0, 0, 512)
        put("gmlp_out_norm", y_ref, 0, 512, 512)
        put("xa_norm", xa_ref, 0, 0, 1024)
        put("mem_norm", mem_ref, 0, 0, 1024)
        put("xa_q_norm", xq_ref, 0, 0, 256)
        put("xa_k_norm", xk_ref, 0, 0, 256)
        put("ffn_norm", ffn_ref, 0, 0, 1024)
        put("ffn_conv_b", cw_ref, 3, 0, 2 * D_FF)
        r8 = lax.broadcasted_iota(jnp.int32, (8, 128), 0)
        l8 = _lane((8, 128))
        bs = jnp.zeros((8, BLK), F32)
        for j in range(4):
            sel = (((r8 == 2 * j) & (l8 < 64)) | ((r8 == 2 * j + 1) & (l8 >= 64))).astype(F32).astype(BF16)
            xj = dbl_ref[j]
            hi = xj.astype(BF16)
            lo = (xj - hi.astype(F32)).astype(BF16)
            bs = bs + lax.dot_general(sel, hi, NT, preferred_element_type=F32) + lax.dot_general(sel, lo, NT, preferred_element_type=F32)
        o_ref[SMALL_ROW["gmlp_bs"]:SMALL_ROW["gmlp_bs"] + 8, :] = bs
        causal = lax.broadcasted_iota(jnp.int32, (BLK, BLK), 0) >= lax.broadcasted_iota(jnp.int32, (BLK, BLK), 1)
        for h in range(8):
            r0 = SMALL_ROW["gmlp_ws"] + h * BLK
            o_ref[r0:r0 + BLK, :] = jnp.where(causal, dws_ref[h], 0.0)
        for q in range(N_CHIPS):
            for j in range(3):
                for k in range(CONV_LANE_ROWS):
                    r0 = SMALL_ROW["ffn_conv"] + q * CONV_CHIP_ROWS + j * CONV_LANE_ROWS + k
                    l0 = (q * CONV_LANE_ROWS + k) * 128
                    o_ref[r0:r0 + 1, :] = cw_ref[j:j + 1, l0:l0 + 128]

    args = (dg_mix, dgq, dgk, dsk, dg_gvn, dg_y, dg_xa, dg_mem, dg_xq, dg_xk, dg_ffn, gcw, dbl, dws)
    full = lambda a: BS(a.shape, lambda i, nd=a.ndim: (0,) * nd)
    return _pcall(body, name="pack_small", grid=(1,), in_specs=[full(a) for a in args],
                  out_specs=BS((SMALL_ROWS, 128), lambda i: (0, 0)), out_shape=SDS((SMALL_ROWS, 128), F32))(*args)


def _adam(w, g, m, v):
    mn = ADAM_B1 * m + (1.0 - ADAM_B1) * g
    vn = ADAM_B2 * v + (1.0 - ADAM_B2) * (g * g)
    m_hat = mn / (1.0 - ADAM_B1 ** ADAM_STEP)
    v_hat = vn / (1.0 - ADAM_B2 ** ADAM_STEP)
    return -ADAM_LR * (m_hat / (jnp.sqrt(v_hat) + ADAM_EPS) + ADAM_WD * w), mn, vn


def adamw_small(gsum, w, m, v, chipvec):
    n = len(SMALL)

    def body(chip_ref, g_ref, *refs):
        w_refs, m_refs, v_refs = refs[:n], refs[n:2 * n], refs[2 * n:3 * n]
        outs = refs[3 * n:]
        go, do, mo, vo = outs[:n], outs[n:2 * n], outs[2 * n:3 * n], outs[3 * n:]

        def update(i, idx, g):
            d, mn, vn = _adam(w_refs[i][idx], g, m_refs[i][idx], v_refs[i][idx])
            go[i][idx] = g
            do[i][idx] = d
            mo[i][idx] = mn
            vo[i][idx] = vn

        for i, (name, length) in enumerate(SMALL_VECS):
            for k in range(-(-length // 128)):
                wd = min(128, length - k * 128)
                r = SMALL_ROW[name] + k
                update(i, (slice(0, 1), slice(k * 128, k * 128 + wd)), g_ref[r:r + 1, 0:wd])
        i_bs, i_ws, i_cv = len(SMALL_VECS), len(SMALL_VECS) + 1, len(SMALL_VECS) + 2
        update(i_bs, (0,), g_ref[SMALL_ROW["gmlp_bs"]:SMALL_ROW["gmlp_bs"] + 8, :])
        for h in range(8):
            r0 = SMALL_ROW["gmlp_ws"] + h * BLK
            update(i_ws, (0, h), g_ref[r0:r0 + BLK, :])
        mine = g_ref[pl.ds(pl.multiple_of(SMALL_ROW["ffn_conv"] + chip_ref[0] * CONV_CHIP_ROWS, 8), CONV_CHIP_ROWS), :]
        for j in range(3):
            for k in range(CONV_LANE_ROWS):
                r = j * CONV_LANE_ROWS + k
                update(i_cv, (0, slice(j, j + 1), slice(k * 128, (k + 1) * 128)), mine[r:r + 1, :])

    nat = [w[nm] for nm in SMALL]
    full = lambda a: BS(a.shape, lambda i, c, nd=a.ndim: (0,) * nd)
    outs = _pcall(body, name="adamw_small", grid=(1,), prefetch=1,
                  in_specs=[BS((SMALL_ROWS, 128), lambda i, c: (0, 0))] + [full(a) for a in nat] * 3,
                  out_specs=[full(a) for a in nat] * 4, out_shape=[SDS(a.shape, F32) for a in nat] * 4)(
        chipvec, gsum, *nat, *[m[nm] for nm in SMALL], *[v[nm] for nm in SMALL])
    return outs[:n], outs[n:2 * n], outs[2 * n:3 * n], outs[3 * n:]


def adamw_matrix(w, m, v, g_own, g_other, cvec, *, name):
    _, r, c = w.shape
    half = r // 2
    tr = _tile(half, (128, 176))
    T = half // tr

    def body(c_ref, w_ref, m_ref, v_ref, own_ref, oth_ref, g_ref, d_ref, mo_ref, vo_ref):
        g = jnp.where(pl.program_id(0) == c_ref[0], own_ref[...], oth_ref[...])
        d, mn, vn = _adam(w_ref[...], g, m_ref[...], v_ref[...])
        g_ref[...] = g
        d_ref[...] = d
        mo_ref[...] = mn
        vo_ref[...] = vn

    nat = BS((None, tr, c), lambda hf, t, cr: (0, hf * T + t, 0))
    hlf = BS((tr, c), lambda hf, t, cr: (t, 0))
    return _pcall(body, name=name, grid=(2, T), prefetch=1, in_specs=[nat, nat, nat, hlf, hlf], out_specs=[nat] * 4,
                  out_shape=[SDS(w.shape, F32)] * 4)(cvec, w, m, v, g_own, g_other)


def _place():
    return lax.axis_index("x"), lax.axis_index("y"), lax.axis_index("c")


def _other_chips(x, y):
    return [(1 - x, y), (x, 1 - y), (1 - x, 1 - y)]


def _rows_of_core(c, half):
    return pl.ds(pl.multiple_of(c * half, 16), half)


def _rcopy(src, dst, sems, k, to):
    return pltpu.make_async_remote_copy(src_ref=src, dst_ref=dst, send_sem=sems[0].at[k], recv_sem=sems[1].at[k],
                                        device_id=to, device_id_type=MESH)


def _comm_call(body, *, name, out_shape, n_in, n_sems, aliases=None):
    return pl.pallas_call(body, name=name, out_shape=out_shape, in_specs=[ANY] * n_in, out_specs=[ANY] * len(out_shape),
                          scratch_shapes=[pltpu.SemaphoreType.DMA((n_sems,)), pltpu.SemaphoreType.DMA((n_sems,))],
                          input_output_aliases=aliases or {},
                          compiler_params=pltpu.CompilerParams(has_side_effects=True))


def cast_shards(shards, conv, chipvec):
    n = len(shards)

    def body(chip_ref, *refs):
        for i_ref, o_ref in zip(refs[:n + 1], refs[n + 1:]):
            o_ref[...] = i_ref[...].astype(o_ref.dtype)

    in_specs = [BS((s.shape[0] // 4, s.shape[1]), lambda i, p: (i, 0)) for s in shards]
    in_specs.append(BS(conv.shape, lambda i, p: (0, 0)))
    out_specs = [BS((None, s.shape[0] // 4, s.shape[1]), lambda i, p: (p[0], i, 0)) for s in shards]
    out_specs.append(BS((None,) + conv.shape, lambda i, p: (p[0], 0, 0)))
    out_shape = [SDS((N_CHIPS,) + s.shape, MXU_DTYPE) for s in shards] + [SDS((N_CHIPS,) + conv.shape, F32)]
    return _pcall(body, name="cast_shards", grid=(4,), prefetch=1, in_specs=in_specs, out_specs=out_specs,
                  out_shape=out_shape)(chipvec, *shards, conv)


HBM = pl.BlockSpec(memory_space=pltpu.HBM)
SEM = pl.BlockSpec(memory_space=pltpu.SEMAPHORE)
DATAFLOW = pltpu.SideEffectType.DATAFLOW_SIDE_EFFECTING
VMEM_WHOLE = pl.BlockSpec(memory_space=pltpu.VMEM)
TOKEN = jax.ShapeDtypeStruct((8, 128), jnp.float32)


def _gather_copies(bufs, send_sems, recv_sems, outgoing):
    x, y, c = _place()
    p = 2 * x + y
    cps = []
    for i, o in enumerate(bufs):
        for j, (cx, cy) in enumerate(_other_chips(x, y)):
            slot = o.at[p] if outgoing else o.at[2 * cx + cy]
            cps.append(_rcopy(slot, slot, (send_sems, recv_sems), 3 * i + j, (cx, cy, c)))
    return cps


def gather_start(slots):
    n = len(slots)

    def body(*refs):
        send_sems, recv_sems, thru, token = refs[n], refs[n + 1], refs[n + 2:2 * n + 2], refs[2 * n + 2]
        for cp in _gather_copies(thru, send_sems, recv_sems, True):
            cp.start()
        token[...] = jnp.zeros_like(token)

    hbm = [pltpu.with_memory_space_constraint(s, pltpu.HBM) for s in slots]
    outs = pl.pallas_call(
        body, name="gather_start_%d" % n,
        out_shape=[pltpu.SemaphoreType.DMA((3 * n,)), pltpu.SemaphoreType.DMA((3 * n,))]
        + [pltpu.HBM(s.shape, s.dtype) for s in slots] + [TOKEN],
        in_specs=[HBM] * n, out_specs=[SEM, SEM] + [HBM] * n + [VMEM_WHOLE],
        input_output_aliases={i: 2 + i for i in range(n)},
        compiler_params=pltpu.CompilerParams(has_side_effects=DATAFLOW))(*hbm)
    return outs[0], outs[1], outs[2:2 + n], outs[2 + n]


def gather_wait(send_sems, recv_sems, bufs, after):
    n = len(bufs)

    def body(*refs):
        ins, send_ref, recv_ref = refs[:n], refs[n], refs[n + 1]
        for cp in _gather_copies(ins, send_ref, recv_ref, False):
            cp.wait_send()
            cp.wait_recv()

    return pl.pallas_call(
        body, name="gather_wait_%d" % n, out_shape=[pltpu.HBM(s.shape, s.dtype) for s in bufs],
        in_specs=[HBM] * n + [SEM, SEM, ANY], out_specs=[HBM] * n, input_output_aliases={i: i for i in range(n)},
        compiler_params=pltpu.CompilerParams(has_side_effects=DATAFLOW))(*bufs, send_sems, recv_sems, after)


def _peers(x, y, c):
    return [(1 - x if k & 4 else x, 1 - y if k & 2 else y, 1 - c if k & 1 else c) for k in range(1, N_DEV)]


def _partial_copies(g_ref, land_ref, send_sems, recv_sems, outgoing):
    x, y, c = _place()
    half = g_ref.shape[1] // 2
    cps = []
    for k, (px, py, pc) in enumerate(_peers(x, y, c)):
        src = g_ref.at[2 * px + py, _rows_of_core(pc, half)]
        dst = land_ref.at[4 * x + 2 * y + c] if outgoing else land_ref.at[4 * px + 2 * py + pc]
        cps.append(_rcopy(src, dst, (send_sems, recv_sems), k, (px, py, pc)))
    return cps


def partials_start(g, *, name):
    land = lax.empty((N_DEV, g.shape[1] // 2, g.shape[2]), g.dtype)

    def body(g_ref, land_ref, send_sems, recv_sems, g_thru, land_thru, token):
        for cp in _partial_copies(g_thru, land_thru, send_sems, recv_sems, True):
            cp.start()
        token[...] = jnp.zeros_like(token)

    return pl.pallas_call(
        body, name=name,
        out_shape=[pltpu.SemaphoreType.DMA((N_DEV - 1,)), pltpu.SemaphoreType.DMA((N_DEV - 1,)),
                   pltpu.HBM(g.shape, g.dtype), pltpu.HBM(land.shape, land.dtype), TOKEN],
        in_specs=[HBM, HBM], out_specs=[SEM, SEM, HBM, HBM, VMEM_WHOLE], input_output_aliases={0: 2, 1: 3},
        compiler_params=pltpu.CompilerParams(has_side_effects=DATAFLOW))(
        pltpu.with_memory_space_constraint(g, pltpu.HBM), pltpu.with_memory_space_constraint(land, pltpu.HBM))


def partials_wait(started, after):
    n = len(started)

    def body(*refs):
        for i in range(n):
            send_ref, recv_ref, g_ref, land_ref = refs[4 * i:4 * i + 4]
            for cp in _partial_copies(g_ref, land_ref, send_ref, recv_ref, False):
                cp.wait_send()
                cp.wait_recv()

    flat = [a for s in started for a in s]
    bufs = [a for s in started for a in s[2:]]
    outs = pl.pallas_call(
        body, name="partials_wait", out_shape=[pltpu.HBM(b.shape, b.dtype) for b in bufs],
        in_specs=[SEM, SEM, HBM, HBM] * n + [ANY], out_specs=[HBM] * (2 * n),
        input_output_aliases={4 * i + 2 + j: 2 * i + j for i in range(n) for j in range(2)},
        compiler_params=pltpu.CompilerParams(has_side_effects=DATAFLOW))(*flat, after)
    return [(outs[2 * i], outs[2 * i + 1]) for i in range(n)]


def sum_partials(pairs, order):
    n = len(pairs)

    def body(o_ref, *refs):
        j = pl.program_id(0)
        for g_ref, l_ref, f_ref in zip(refs[:n], refs[n:2 * n], refs[2 * n:]):
            @pl.when(j == 0)
            def _():
                f_ref[...] = g_ref[...].astype(F32)

            @pl.when(j > 0)
            def _():
                f_ref[...] += l_ref[...].astype(F32)

    g4 = [g.reshape(g.shape[0], 2, g.shape[1] // 2, g.shape[2]) for g, _ in pairs]
    lands = [l for _, l in pairs]
    return _pcall(body, name="sum_partials", grid=(N_DEV,), prefetch=1,
                  in_specs=[BS((None, None) + g.shape[2:], lambda j, o: (o[0], o[1], 0, 0)) for g in g4]
                  + [BS((None,) + l.shape[1:], lambda j, o: (o[jnp.maximum(j, 1) + 1], 0, 0)) for l in lands],
                  out_specs=[BS(l.shape[1:], lambda j, o: (0, 0)) for l in lands],
                  out_shape=[SDS(l.shape[1:], F32) for l in lands])(order, *g4, *lands)


def pair_share(fs):
    n = len(fs)

    def body(*refs):
        f_refs, o_refs, sems = refs[:n], refs[n:2 * n], refs[2 * n:]
        x, y, c = _place()
        cps = [_rcopy(f, o, sems, i, (x, y, 1 - c)) for i, (f, o) in enumerate(zip(f_refs, o_refs))]
        for cp in cps:
            cp.start()
        for cp in cps:
            cp.wait()

    return _comm_call(body, name="pair_share", n_in=n, n_sems=n, out_shape=[SDS(f.shape, f.dtype) for f in fs])(*fs)


def _small_copies(s_ref, land_ref, send_sems, recv_sems, outgoing):
    x, y, c = _place()
    cps = []
    for k, (px, py, pc) in enumerate(_peers(x, y, c)):
        dst = land_ref.at[4 * x + 2 * y + c] if outgoing else land_ref.at[4 * px + 2 * py + pc]
        cps.append(_rcopy(s_ref, dst, (send_sems, recv_sems), k, (px, py, pc)))
    return cps


def small_start(sm):
    land = lax.empty((N_DEV,) + sm.shape, sm.dtype)

    def body(s_ref, land_ref, send_sems, recv_sems, s_thru, land_thru):
        for cp in _small_copies(s_thru, land_thru, send_sems, recv_sems, True):
            cp.start()

    return pl.pallas_call(
        body, name="small_start",
        out_shape=[pltpu.SemaphoreType.DMA((N_DEV - 1,)), pltpu.SemaphoreType.DMA((N_DEV - 1,)),
                   pltpu.HBM(sm.shape, sm.dtype), pltpu.HBM(land.shape, land.dtype)],
        in_specs=[HBM, HBM], out_specs=[SEM, SEM, HBM, HBM], input_output_aliases={0: 2, 1: 3},
        compiler_params=pltpu.CompilerParams(has_side_effects=DATAFLOW))(
        pltpu.with_memory_space_constraint(sm, pltpu.HBM), pltpu.with_memory_space_constraint(land, pltpu.HBM))


def small_wait(send_sems, recv_sems, sm, land, after):
    def body(send_ref, recv_ref, s_ref, land_ref, after_ref, s_out, land_out):
        for cp in _small_copies(s_ref, land_ref, send_ref, recv_ref, False):
            cp.wait_send()
            cp.wait_recv()

    return pl.pallas_call(
        body, name="small_wait", out_shape=[pltpu.HBM(sm.shape, sm.dtype), pltpu.HBM(land.shape, land.dtype)],
        in_specs=[SEM, SEM, HBM, HBM, ANY], out_specs=[HBM, HBM], input_output_aliases={2: 0, 3: 1},
        compiler_params=pltpu.CompilerParams(has_side_effects=DATAFLOW))(send_sems, recv_sems, sm, land, after)


def sum_small(own, land, mevec):
    n, rows, width = land.shape
    tr = _tile(rows, (184, 8))

    def body(me_ref, own_ref, land_ref, o_ref):
        acc = jnp.zeros((tr, width), F32)
        for s in range(n):
            acc = acc + jnp.where(me_ref[0] == s, own_ref[...], land_ref[s])
        o_ref[...] = acc

    return _pcall(body, name="sum_small", grid=(rows // tr,), prefetch=1,
                  in_specs=[BS((tr, width), lambda i, me: (i, 0)), BS((n, tr, width), lambda i, me: (0, i, 0))],
                  out_specs=BS((tr, width), lambda i, me: (i, 0)), out_shape=SDS((rows, width), F32))(mevec, own, land)


def _to_full(blk, col):
    n, r, c = blk.shape
    return blk.transpose(1, 0, 2).reshape(r, n * c) if col else blk.reshape(n * r, c)


def _dup_cols(w):
    dup = lambda t: jnp.concatenate([t[:, :64], t[:, :64], t[:, 64:], t[:, 64:]], axis=1)
    return jnp.concatenate([w[:, :512], dup(w[:, 512:640]), dup(w[:, 640:768]), w[:, 768:]], axis=1)


def _fold_cols(d):
    fold = lambda t: jnp.concatenate([t[:, 0:64] + t[:, 64:128], t[:, 128:192] + t[:, 192:256]], axis=1)
    return jnp.concatenate([d[:, :512], fold(d[:, 512:768]), fold(d[:, 768:1024]), d[:, 1024:]], axis=1)


def _local_step(x, mem, positions, target, w_in, later, sp, emit):
    gain = lambda n: sp[n].reshape(1, -1)
    half = HEAD_DIM // 2
    inv_freq = 1.0 / (10000.0 ** (jnp.arange(half, dtype=F32) * (2.0 / HEAD_DIM)))
    ang = positions.astype(F32)[:, None] * inv_freq
    cos, sin = jnp.cos(ang), jnp.sin(ang)
    cos128 = jnp.tile(cos, (1, 4))
    sin128 = jnp.concatenate([-sin, sin, -sin, sin], axis=1)
    seg = jnp.arange(128) // HEAD_DIM
    bmat = (seg[:, None] == seg[None, :]).astype(BF16)
    gq128, gk128 = jnp.tile(gain("q_norm"), (1, 2)), jnp.tile(gain("k_norm"), (1, 2))
    sinkcol = jnp.repeat(sp["attn_sinks"].reshape(4, 2), BLK, axis=1).reshape(4, 2 * BLK, 1)
    wsc = sp["gmlp_ws"] * jnp.tril(jnp.ones((BLK, BLK), F32))[None]
    w2 = wsc.reshape(4, 2 * BLK, BLK).astype(MXU_DTYPE)
    w2t = wsc.swapaxes(1, 2).reshape(4, 2 * BLK, BLK).astype(MXU_DTYPE)
    bsl = jnp.repeat(sp["gmlp_bs"].reshape(4, 2, BLK).transpose(0, 2, 1), HEAD_DIM, axis=2)
    cb = sp["ffn_conv_b"].reshape(1, -1)
    w_in_d = _dup_cols(_to_full(w_in, True))[None]

    h1, proj = rms_mm(x, gain("mix_norm"), w_in_d, name="mix_in")
    qr, kr, vb, gu, gvn, attn, gm, y = mixer_core_fwd(proj, cos128, sin128, gq128, gk128, gain("gmlp_v_norm"), bmat,
                                                      sinkcol, gain("attn_out_norm"), w2, bsl, gain("gmlp_out_norm"))
    wf, cw = later(y)
    w_out, xa_wq, xa_wo, ffn_down = (_to_full(wf[n], False) for n in ("w_out", "xa_wq", "xa_wo", "ffn_down"))
    x1 = mm(y, w_out, res=x, name="mix_out")
    h2, qx = rms_mm(x1, gain("xa_norm"), xa_wq[None], name="xa_q")
    mn, kv = rms_mm(mem, gain("mem_norm"), wf["xa_wkv"], name="xa_kv")
    kn, vbx = mem_pre(kv, gain("xa_k_norm"))
    xo = xattn_fwd(qx, kn, vbx, gain("xa_q_norm"))
    x2 = mm(xo, xa_wo, res=x1, name="xa_out")
    h3, a = rms_mm(x2, gain("ffn_norm"), wf["ffn_up"], name="ffn_up")
    f, dx3, loss_acc = convgate_down_loss(a, cw, cb, ffn_down, x2, target)

    by_rows = lambda g: g.reshape(N_CHIPS, g.shape[1] // N_CHIPS, g.shape[2])
    sent = emit("ffn_down", by_rows(mm_tn(f, dx3, name="g_ffn_down", out_dtype=WIRE_DTYPE)))
    dc, gcw = convgate_bwd(a, dx3, ffn_down[None], cw, cb, after=sent)
    da, dx2, dg_ffn = conv_transpose_rms_bwd(dc, cw, wf["ffn_up"], x2, gain("ffn_norm"), dx3)
    sent = emit("ffn_up", mm_tn(h3, da, name="g_ffn_up", out_dtype=WIRE_DTYPE, chunks=N_CHIPS))
    dxo = mm_nt(dx2, xa_wo[None], name="d_xo", after=sent)
    sent = emit("xa_wo", by_rows(mm_tn(xo, dx2, name="g_xa_wo", out_dtype=WIRE_DTYPE)))
    dqx, dkn, dvx, dg_xq = xattn_bwd(qx, dxo, kn, vbx, gain("xa_q_norm"), after=sent)
    dx1, dg_xa = mm_nt_rms_bwd(dqx, xa_wq[None], x1, gain("xa_norm"), dx2, name="d_x1")
    sent = emit("xa_wq", by_rows(mm_tn(h2, dqx, name="g_xa_wq", out_dtype=WIRE_DTYPE)))
    dkv, dg_xk = mem_bwd(kv, dkn, dvx, gain("xa_k_norm"), after=sent)
    _, dg_mem = mm_nt_rms_bwd(dkv, wf["xa_wkv"], mem, gain("mem_norm"), jnp.zeros_like(mem), name="d_mem")
    sent = emit("xa_wkv", mm_tn(mn, dkv, name="g_xa_wkv", out_dtype=WIRE_DTYPE, chunks=N_CHIPS))
    dattn, dgm, dg_y = mm_nt_post_bwd(dx1, w_out[None], attn, gm, gain("attn_out_norm"), gain("gmlp_out_norm"),
                                      name="d_mix_out", after=sent)
    sent = emit("w_out", by_rows(mm_tn(y, dx1, name="g_w_out", out_dtype=WIRE_DTYPE)))
    dgu, dgvn, dws, dbl = gmlp_bwd(dgm, gvn, gu, w2, w2t, bsl, after=sent)
    dqr, dkr, dvb, dsk = swa_bwd(qr, kr, vb, sinkcol, dattn)
    dproj, dgq, dgk, dg_gvn = mixer_pre_bwd(proj, cos128, sin128, gq128, gk128, gain("gmlp_v_norm"), bmat,
                                            dqr, dkr, dvb, dgu, dgvn)
    g_in = _fold_cols(mm_tn(h1, dproj, name="g_w_in", out_dtype=F32)[0])
    sent = emit("w_in", g_in.reshape(1024, N_CHIPS, 448).transpose(1, 0, 2).astype(WIRE_DTYPE))
    grad_x, dg_mix = mm_nt_rms_bwd(dproj, w_in_d, x, gain("mix_norm"), dx1, name="d_x", after=sent)
    packed = pack_small(dg_mix, dgq, dgk, dsk, dg_gvn, dg_y, dg_xa, dg_mem, dg_xq, dg_xk, dg_ffn, gcw, dbl, dws)
    return loss_acc, grad_x, packed


def _gather_step(w, chipvec):
    slots = cast_shards([w[n][0] for n in BIG_NAMES], w["ffn_conv"][0], chipvec)
    send_a, recv_a, first, _ = gather_start(slots[:1])
    send_b, recv_b, rest, rest_started = gather_start(slots[1:])
    w_in, = gather_wait(send_a, recv_a, first, rest_started)

    def later(after):
        got = gather_wait(send_b, recv_b, rest, after)
        return dict(zip(BIG_NAMES[1:], got[:-1])), _to_full(got[-1], True)

    return w_in, later


def _reduce_update(started, packed, w, m, v, chipvec, cvec, order):
    small_sent = small_start(packed)
    own = sum_partials(partials_wait([started[n] for n in BIG_NAMES], small_sent[2]), order)
    other = pair_share(own)
    res = [{}, {}, {}, {}]
    for n, g_own, g_other in zip(BIG_NAMES, own, other):
        for d, o in zip(res, adamw_matrix(w[n], m[n], v[n], g_own, g_other, cvec, name="adamw_" + n)):
            d[n] = o
    mevec = (2 * order[0:1] + order[1:2]).astype(jnp.int32)
    small_sum = sum_small(*small_wait(*small_sent, res[3][BIG_NAMES[-1]]), mevec)
    for d, outs in zip(res, adamw_small(small_sum, w, m, v, chipvec)):
        d.update(zip(SMALL, outs))
    return res


def kernel(x, mem, positions, mix_norm, w_in, q_norm, k_norm, attn_sinks, gmlp_v_norm, gmlp_ws, gmlp_bs, attn_out_norm, gmlp_out_norm, w_out, xa_norm, mem_norm, xa_wq, xa_wkv, xa_q_norm, xa_k_norm, xa_wo, ffn_norm, ffn_up, ffn_conv, ffn_conv_b, ffn_down, loss_target, m_mix_norm, m_w_in, m_q_norm, m_k_norm, m_attn_sinks, m_gmlp_v_norm, m_gmlp_ws, m_gmlp_bs, m_attn_out_norm, m_gmlp_out_norm, m_w_out, m_xa_norm, m_mem_norm, m_xa_wq, m_xa_wkv, m_xa_q_norm, m_xa_k_norm, m_xa_wo, m_ffn_norm, m_ffn_up, m_ffn_conv, m_ffn_conv_b, m_ffn_down, v_mix_norm, v_w_in, v_q_norm, v_k_norm, v_attn_sinks, v_gmlp_v_norm, v_gmlp_ws, v_gmlp_bs, v_attn_out_norm, v_gmlp_out_norm, v_w_out, v_xa_norm, v_mem_norm, v_xa_wq, v_xa_wkv, v_xa_q_norm, v_xa_k_norm, v_xa_wo, v_ffn_norm, v_ffn_up, v_ffn_conv, v_ffn_conv_b, v_ffn_down):
    w = dict(mix_norm=mix_norm, w_in=w_in, q_norm=q_norm, k_norm=k_norm, attn_sinks=attn_sinks, gmlp_v_norm=gmlp_v_norm, gmlp_ws=gmlp_ws, gmlp_bs=gmlp_bs, attn_out_norm=attn_out_norm, gmlp_out_norm=gmlp_out_norm, w_out=w_out, xa_norm=xa_norm, mem_norm=mem_norm, xa_wq=xa_wq, xa_wkv=xa_wkv, xa_q_norm=xa_q_norm, xa_k_norm=xa_k_norm, xa_wo=xa_wo, ffn_norm=ffn_norm, ffn_up=ffn_up, ffn_conv=ffn_conv, ffn_conv_b=ffn_conv_b, ffn_down=ffn_down)
    m = dict(mix_norm=m_mix_norm, w_in=m_w_in, q_norm=m_q_norm, k_norm=m_k_norm, attn_sinks=m_attn_sinks, gmlp_v_norm=m_gmlp_v_norm, gmlp_ws=m_gmlp_ws, gmlp_bs=m_gmlp_bs, attn_out_norm=m_attn_out_norm, gmlp_out_norm=m_gmlp_out_norm, w_out=m_w_out, xa_norm=m_xa_norm, mem_norm=m_mem_norm, xa_wq=m_xa_wq, xa_wkv=m_xa_wkv, xa_q_norm=m_xa_q_norm, xa_k_norm=m_xa_k_norm, xa_wo=m_xa_wo, ffn_norm=m_ffn_norm, ffn_up=m_ffn_up, ffn_conv=m_ffn_conv, ffn_conv_b=m_ffn_conv_b, ffn_down=m_ffn_down)
    v = dict(mix_norm=v_mix_norm, w_in=v_w_in, q_norm=v_q_norm, k_norm=v_k_norm, attn_sinks=v_attn_sinks, gmlp_v_norm=v_gmlp_v_norm, gmlp_ws=v_gmlp_ws, gmlp_bs=v_gmlp_bs, attn_out_norm=v_attn_out_norm, gmlp_out_norm=v_gmlp_out_norm, w_out=v_w_out, xa_norm=v_xa_norm, mem_norm=v_mem_norm, xa_wq=v_xa_wq, xa_wkv=v_xa_wkv, xa_q_norm=v_xa_q_norm, xa_k_norm=v_xa_k_norm, xa_wo=v_xa_wo, ffn_norm=v_ffn_norm, ffn_up=v_ffn_up, ffn_conv=v_ffn_conv, ffn_conv_b=v_ffn_conv_b, ffn_down=v_ffn_down)
    ix, iy, ic = lax.axis_index("x"), lax.axis_index("y"), lax.axis_index("c")
    chip = 2 * ix + iy
    chipvec = chip.astype(jnp.int32).reshape(1)
    cvec = ic.astype(jnp.int32).reshape(1)
    order = jnp.stack([chip, ic] + [4 * px + 2 * py + pc for px, py, pc in _peers(ix, iy, ic)]).astype(jnp.int32)

    w_in_all, later = _gather_step(w, chipvec)
    sp = {n: w[n][0] for n in SMALL if n != "ffn_conv"}
    started = {}

    def emit(name, g):
        *started[name], token = partials_start(g, name="partials_start_" + name)
        return token

    loss_acc, grad_x, packed = _local_step(x[0], mem[0], positions[0], loss_target[0], w_in_all, later, sp, emit)
    grads, delta, new_m, new_v = _reduce_update(started, packed, w, m, v, chipvec, cvec, order)
    loss = lax.psum(loss_acc[0, 0], ("x", "y", "c"))
    ordered = lambda d: [d[n] for n in WEIGHTS]
    return (loss, grad_x[None], *ordered(grads), *ordered(delta), *ordered(new_m), *ordered(new_v))
```

```python
import math

import jax
import jax.numpy as jnp
from jax import lax
from jax.experimental import pallas as pl
from jax.experimental.pallas import tpu as pltpu

F32 = jnp.float32
BF16 = jnp.bfloat16
MXU_DTYPE = jnp.bfloat16
WIRE_DTYPE = jnp.bfloat16
EPS = 1e-6
VMEM_LIMIT_V7X = 56 * 1024 * 1024

D_MODEL = 1024
HEAD_DIM = 64
BLK = 128
XA_HEADS = 4
XA_DH = 256
MEM_LEN = 256
D_FF = 2816
IN_COLS_DUP = 2048
N_CHIPS = 4
N_DEV = 8

ADAM_LR = 0.001
ADAM_B1 = 0.9
ADAM_B2 = 0.999
ADAM_EPS = 1e-08
ADAM_WD = 0.01
ADAM_STEP = 10

NT = (((1,), (1,)), ((), ()))
TN = (((0,), (0,)), ((), ()))
NN = (((1,), (0,)), ((), ()))
MINF = float(jnp.finfo(jnp.float32).min)
GELU_K0 = math.sqrt(2.0 / math.pi)
GELU_K1 = 0.044715

BS = pl.BlockSpec
SDS = jax.ShapeDtypeStruct
ANY = pl.BlockSpec(memory_space=pl.ANY)
MESH = pl.DeviceIdType.MESH


def _dot(a, b, dims=NN):
    return lax.dot_general(a.astype(MXU_DTYPE), b.astype(MXU_DTYPE), dims, preferred_element_type=F32)


def _segsum(x, bmat):
    hi = x.astype(BF16)
    lo = (x - hi.astype(F32)).astype(BF16)
    return (jnp.dot(hi, bmat, preferred_element_type=F32) + jnp.dot(lo, bmat, preferred_element_type=F32))


def _gelu(x):
    return 0.5 * x * (1.0 + jnp.tanh(GELU_K0 * (x + GELU_K1 * x * x * x)))


def _gelu_grad(x):
    t = jnp.tanh(GELU_K0 * (x + GELU_K1 * x * x * x))
    return 0.5 * (1.0 + t) + 0.5 * x * (1.0 - t * t) * GELU_K0 * (1.0 + 3.0 * GELU_K1 * x * x)


def _rms(x):
    return lax.rsqrt(jnp.mean(x * x, axis=-1, keepdims=True) + EPS)


def _rms_bwd(dy, x, g, r):
    dyg = dy * g
    dx = r * dyg - x * (r * r * r) * jnp.mean(dyg * x, axis=-1, keepdims=True)
    return dx, dy * x * r


def _pcall(body, *, name, grid, in_specs, out_specs, out_shape, scratch=(), prefetch=0, after=None):
    params = pltpu.CompilerParams(dimension_semantics=("arbitrary",) * len(grid), vmem_limit_bytes=VMEM_LIMIT_V7X)
    in_specs = list(in_specs)
    kernel_fn = body
    if after is not None:
        n_in = prefetch + len(in_specs)
        in_specs.append(ANY)

        def kernel_fn(*refs):
            return body(*refs[:n_in], *refs[n_in + 1:])

    if prefetch:
        spec = pltpu.PrefetchScalarGridSpec(num_scalar_prefetch=prefetch, grid=grid, in_specs=in_specs,
                                            out_specs=out_specs, scratch_shapes=list(scratch))
        call = pl.pallas_call(kernel_fn, name=name, grid_spec=spec, out_shape=out_shape, compiler_params=params)
    else:
        call = pl.pallas_call(kernel_fn, name=name, grid=grid, in_specs=in_specs, out_specs=out_specs,
                              out_shape=out_shape, scratch_shapes=list(scratch), compiler_params=params)
    return call if after is None else (lambda *args: call(*args, after))


def _tile(n, prefs):
    for p in prefs:
        if p <= n and n % p == 0:
            return p
    return n


def _resident(shape):
    return pl.BlockSpec(shape, lambda *_: (0,) * len(shape), pipeline_mode=pl.Buffered(1))


def _acc_rows(ref, row, val):
    ref[row:row + 1, :] += jnp.sum(val, axis=0, keepdims=True)


def rms_mm(x, g, w3, *, name, tm=1024):
    M, K = x.shape
    Q, _, C = w3.shape
    tm = _tile(M, (tm, 256))

    def body(x_ref, g_ref, w_ref, h_ref, o_ref):
        @pl.when(pl.program_id(1) == 0)
        def _():
            xv = x_ref[...]
            h_ref[...] = (xv * _rms(xv) * g_ref[...]).astype(h_ref.dtype)

        o_ref[...] = _dot(h_ref[...], w_ref[pl.program_id(1)])

    return _pcall(body, name=name, grid=(M // tm, Q),
                  in_specs=[BS((tm, K), lambda i, j: (i, 0)), BS((1, K), lambda i, j: (0, 0)),
                            _resident((Q, K, C))],
                  out_specs=[BS((tm, K), lambda i, j: (i, 0)), BS((tm, C), lambda i, j: (i, j))],
                  out_shape=[SDS((M, K), MXU_DTYPE), SDS((M, Q * C), F32)])(x, g, w3)


def mm(a, w, *, name, res):
    M, K = a.shape
    N = w.shape[1]
    tm = _tile(M, (1024, 256))

    def body(a_ref, w_ref, r_ref, o_ref):
        o_ref[...] = _dot(a_ref[...], w_ref[...]) + r_ref[...]

    return _pcall(body, name=name, grid=(M // tm,),
                  in_specs=[BS((tm, K), lambda i: (i, 0)), _resident((K, N)), BS((tm, N), lambda i: (i, 0))],
                  out_specs=BS((tm, N), lambda i: (i, 0)), out_shape=SDS((M, N), F32))(a, w, res)


def _nt_chunks(a_ref, w_ref):
    q_n, _, kc = w_ref.shape
    acc = _dot(a_ref[:, 0:kc], w_ref[0], NT)
    for q in range(1, q_n):
        acc = acc + _dot(a_ref[:, q * kc:(q + 1) * kc], w_ref[q], NT)
    return acc


def mm_nt(a, w3, *, name, after=None):
    M = a.shape[0]
    Q, N, Kc = w3.shape
    tm = _tile(M, (1024, 256))

    def body(a_ref, w_ref, o_ref):
        o_ref[...] = _nt_chunks(a_ref, w_ref)

    return _pcall(body, name=name, grid=(M // tm,), after=after,
                  in_specs=[BS((tm, Q * Kc), lambda i: (i, 0)), _resident((Q, N, Kc))],
                  out_specs=BS((tm, N), lambda i: (i, 0)), out_shape=SDS((M, N), F32))(a, w3)


def mm_nt_rms_bwd(a, w3, x, g, dres, *, name, tm=512, after=None):
    M = a.shape[0]
    Q, N, Kc = w3.shape
    tm = _tile(M, (tm, 256))

    def body(a_ref, w_ref, x_ref, g_ref, dr_ref, dx_ref, dg_ref):
        @pl.when(pl.program_id(0) == 0)
        def _():
            dg_ref[...] = jnp.zeros_like(dg_ref)

        xv = x_ref[...]
        dx, dgc = _rms_bwd(_nt_chunks(a_ref, w_ref), xv, g_ref[...], _rms(xv))
        dx_ref[...] = dr_ref[...] + dx
        _acc_rows(dg_ref, 0, dgc)

    row = BS((tm, N), lambda i: (i, 0))
    return _pcall(body, name=name, grid=(M // tm,), after=after,
                  in_specs=[BS((tm, Q * Kc), lambda i: (i, 0)), _resident((Q, N, Kc)), row,
                            BS((1, N), lambda i: (0, 0)), row],
                  out_specs=[row, BS((8, N), lambda i: (0, 0))],
                  out_shape=[SDS((M, N), F32), SDS((8, N), F32)])(a, w3, x, g, dres)


def mm_nt_post_bwd(a, w3, attn, gm, gao, ggo, *, name, after=None):
    M = a.shape[0]
    Q, N, Kc = w3.shape
    tm = _tile(M, (512, 256))
    hw = N // 2

    def body(a_ref, w_ref, at_ref, gm_ref, gao_ref, ggo_ref, da_ref, dgm_ref, dg_ref):
        @pl.when(pl.program_id(0) == 0)
        def _():
            dg_ref[...] = jnp.zeros_like(dg_ref)

        dy = _nt_chunks(a_ref, w_ref)
        av, gmv = at_ref[...], gm_ref[...]
        da, dga = _rms_bwd(dy[:, :hw], av, gao_ref[...], _rms(av))
        dgm, dgg = _rms_bwd(dy[:, hw:], gmv, ggo_ref[...], _rms(gmv))
        da_ref[...] = da
        dgm_ref[...] = dgm
        dg_ref[0:1, :hw] += jnp.sum(dga, axis=0, keepdims=True)
        dg_ref[0:1, hw:] += jnp.sum(dgg, axis=0, keepdims=True)

    half = BS((tm, hw), lambda i: (i, 0))
    const = lambda r, w: BS((r, w), lambda i: (0, 0))
    return _pcall(body, name=name, grid=(M // tm,), after=after,
                  in_specs=[BS((tm, Q * Kc), lambda i: (i, 0)), _resident((Q, N, Kc)), half, half,
                            const(1, hw), const(1, hw)],
                  out_specs=[half, half, const(8, N)],
                  out_shape=[SDS((M, hw), F32), SDS((M, hw), F32), SDS((8, N), F32)])(a, w3, attn, gm, gao, ggo)


def mm_tn(a, b, *, name, out_dtype, chunks=1):
    M, K = a.shape
    N = b.shape[1]
    C = N // chunks
    tm = _tile(M, (1024, 256))
    tk = _tile(K, (1408, 1024, 512))
    tn = _tile(C, (1408, 1024, 512))
    per = C // tn
    nm = M // tm

    def body(a_ref, b_ref, o_ref, acc):
        m = pl.program_id(2)

        @pl.when(m == 0)
        def _():
            acc[...] = jnp.zeros_like(acc)

        acc[...] += _dot(a_ref[...], b_ref[...], TN)

        @pl.when(m == nm - 1)
        def _():
            o_ref[...] = acc[...].astype(o_ref.dtype)

    return _pcall(body, name=name, grid=(K // tk, N // tn, nm),
                  in_specs=[BS((tm, tk), lambda k, n, m: (m, k)), BS((tm, tn), lambda k, n, m: (m, n))],
                  out_specs=BS((None, tk, tn), lambda k, n, m: (n // per, k, n % per)),
                  out_shape=SDS((chunks, K, C), out_dtype), scratch=[pltpu.VMEM((tk, tn), F32)])(a, b)


def _lane(shape):
    return lax.broadcasted_iota(jnp.int32, shape, 1)


def _head_means(slabs, bmat):
    tm = slabs[0].shape[0]
    means = _segsum(jnp.concatenate(slabs, axis=0), bmat) * (1.0 / HEAD_DIM)
    return [means[i * tm:(i + 1) * tm] for i in range(len(slabs))]


def _half_swap(x, first):
    return jnp.where(first, pltpu.roll(x, 96, 1), pltpu.roll(x, 32, 1))


def _by_head(x2, lo):
    z = jnp.zeros((BLK, 128), x2.dtype)
    parts = []
    for s in range(2):
        xs = x2[:, s * 128:(s + 1) * 128]
        parts += [jnp.where(lo, xs, z), jnp.where(lo, z, xs)]
    return jnp.concatenate(parts, axis=0)


def _from_heads(o4, lo):
    return jnp.concatenate([jnp.where(lo, o4[0:BLK], o4[BLK:2 * BLK]),
                            jnp.where(lo, o4[2 * BLK:3 * BLK], o4[3 * BLK:])], axis=1)


def _swa_probs(q2, kd, sink, n, lo):
    qp = _by_head(q2, lo)
    sc = _dot(qp, kd, NT) * (1.0 / math.sqrt(HEAD_DIM))
    r_i = lax.broadcasted_iota(jnp.int32, (4 * BLK, 2 * BLK), 0)
    k_j = lax.broadcasted_iota(jnp.int32, (4 * BLK, 2 * BLK), 1)
    diff = (r_i & (BLK - 1)) + BLK - k_j
    mask = (diff >= 0) & (diff < BLK) & ((k_j >= BLK) | (n > 0))
    sc = jnp.where(mask, sc, MINF)
    m = jnp.maximum(jnp.max(sc, axis=1, keepdims=True), sink)
    p = jnp.exp(sc - m)
    es = jnp.exp(sink - m)
    inv = 1.0 / (jnp.sum(p, axis=1, keepdims=True) + es)
    return qp, p * inv, es * inv


def mixer_core_fwd(proj, cos, sin, gq, gk, gvn, bmat, sinkcol, gao, w2, bsl, ggo):
    S = proj.shape[0]

    def body(p_ref, c_ref, s_ref, gq_ref, gk_ref, gvn_ref, b_ref, sk_ref, gao_ref, w2_ref, bsl_ref, ggo_ref,
             qr_ref, kr_ref, vb_ref, gu_ref, gvo_ref, at_ref, gm_ref, y_ref, k_prev, v_prev):
        n = pl.program_id(0)

        @pl.when(n == 0)
        def _():
            k_prev[...] = jnp.zeros_like(k_prev)
            v_prev[...] = jnp.zeros_like(v_prev)

        cos_v, sin_v, bm = c_ref[...], s_ref[...], b_ref[...]
        first = (_lane((BLK, 128)) & 63) < 32
        lo = _lane((BLK, 128)) < 64
        slabs = [p_ref[:, s * 128:(s + 1) * 128] for s in range(6)]
        for s, (slab, ms) in enumerate(zip(slabs, _head_means([x * x for x in slabs], bm))):
            qn = slab * lax.rsqrt(ms + EPS) * (gq_ref[...] if s < 4 else gk_ref[...])
            out = qn * cos_v + _half_swap(qn, first) * sin_v
            if s < 4:
                qr_ref[:, s * 128:(s + 1) * 128] = out.astype(qr_ref.dtype)
            else:
                kr_ref[:, (s - 4) * 128:(s - 3) * 128] = out.astype(kr_ref.dtype)
        vb_ref[...] = p_ref[:, 768:1024].astype(vb_ref.dtype)
        gu_ref[...] = _gelu(p_ref[:, 1024:1536])
        gv = _gelu(p_ref[:, 1536:2048])
        gvo_ref[...] = (gv * _rms(gv) * gvn_ref[...]).astype(gvo_ref.dtype)

        for h in range(2):
            hs, qs = slice(h * 128, (h + 1) * 128), slice(h * 256, (h + 1) * 256)
            kd = jnp.concatenate([k_prev[:, hs], kr_ref[:, hs]], axis=0)
            vd = jnp.concatenate([v_prev[:, hs], vb_ref[:, hs]], axis=0)
            sink = jnp.concatenate([sk_ref[2 * h], sk_ref[2 * h + 1]], axis=0)
            _, p, _ = _swa_probs(qr_ref[:, qs], kd, sink, n, lo)
            at_ref[:, qs] = _from_heads(_dot(p, vd), lo)
        k_prev[...] = kr_ref[...]
        v_prev[...] = vb_ref[...]

        for j in range(4):
            sl = slice(j * 128, (j + 1) * 128)
            m2 = _dot(w2_ref[j], gvo_ref[:, sl])
            mixed = jnp.where(lo, m2[:BLK], m2[BLK:]) + bsl_ref[j]
            gm_ref[:, sl] = gu_ref[:, sl] * mixed
        a, gm = at_ref[...], gm_ref[...]
        y_ref[:, :512] = (a * _rms(a) * gao_ref[...]).astype(y_ref.dtype)
        y_ref[:, 512:] = (gm * _rms(gm) * ggo_ref[...]).astype(y_ref.dtype)

    row = lambda w: BS((BLK, w), lambda n: (n, 0))
    const = lambda *shape: BS(shape, lambda n: (0,) * len(shape))
    return _pcall(body, name="mixer_core_fwd", grid=(S // BLK,),
                  in_specs=[row(IN_COLS_DUP), row(128), row(128), const(1, 128), const(1, 128), const(1, 512),
                            const(128, 128), const(4, 2 * BLK, 1), const(1, 512), const(4, 2 * BLK, BLK),
                            const(4, BLK, 128), const(1, 512)],
                  out_specs=[row(512), row(256), row(256), row(512), row(512), row(512), row(512), row(1024)],
                  out_shape=[SDS((S, 512), MXU_DTYPE), SDS((S, 256), MXU_DTYPE), SDS((S, 256), MXU_DTYPE),
                             SDS((S, 512), F32), SDS((S, 512), MXU_DTYPE), SDS((S, 512), F32), SDS((S, 512), F32),
                             SDS((S, 1024), MXU_DTYPE)],
                  scratch=[pltpu.VMEM((BLK, 256), MXU_DTYPE), pltpu.VMEM((BLK, 256), MXU_DTYPE)])(
        proj, cos, sin, gq, gk, gvn, bmat, sinkcol, gao, w2, bsl, ggo)


def mem_pre(kv, gxk):
    def body(kv_ref, g_ref, kn_ref, vb_ref):
        for h in range(XA_HEADS):
            sl = slice(h * XA_DH, (h + 1) * XA_DH)
            k = kv_ref[:, sl]
            kn_ref[:, sl] = (k * _rms(k) * g_ref[...]).astype(kn_ref.dtype)
        vb_ref[...] = kv_ref[:, 1024:2048].astype(vb_ref.dtype)

    full = lambda r, w: BS((r, w), lambda i: (0, 0))
    return _pcall(body, name="mem_pre", grid=(1,), in_specs=[full(MEM_LEN, 2048), full(1, XA_DH)],
                  out_specs=[full(MEM_LEN, 1024), full(MEM_LEN, 1024)],
                  out_shape=[SDS((MEM_LEN, 1024), MXU_DTYPE), SDS((MEM_LEN, 1024), MXU_DTYPE)])(kv, gxk)


def _xa_probs(qh, g, kn_h):
    r = _rms(qh)
    qn = qh * r * g
    s = _dot(qn, kn_h, NT) * (1.0 / math.sqrt(XA_DH))
    p = jnp.exp(s - jnp.max(s, axis=1, keepdims=True))
    return r, qn, p * (1.0 / jnp.sum(p, axis=1, keepdims=True))


def xattn_fwd(qx, kn, vb, gxq):
    S = qx.shape[0]
    tm = _tile(S, (512, 256))

    def body(q_ref, kn_ref, vb_ref, g_ref, o_ref):
        for h in range(XA_HEADS):
            sl = slice(h * XA_DH, (h + 1) * XA_DH)
            _, _, p = _xa_probs(q_ref[:, sl], g_ref[...], kn_ref[:, sl])
            o_ref[:, sl] = _dot(p, vb_ref[:, sl]).astype(o_ref.dtype)

    full = lambda r, w: BS((r, w), lambda i: (0, 0))
    return _pcall(body, name="xattn_fwd", grid=(S // tm,),
                  in_specs=[BS((tm, 1024), lambda i: (i, 0)), full(MEM_LEN, 1024), full(MEM_LEN, 1024), full(1, XA_DH)],
                  out_specs=BS((tm, 1024), lambda i: (i, 0)), out_shape=SDS((S, 1024), MXU_DTYPE))(qx, kn, vb, gxq)


CONV_COLS = 1408


def _conv_taps(a_ref, halo_ref, w_ref, b_ref, cols, first_tile):
    a = a_ref[:, cols]
    row = lax.broadcasted_iota(jnp.int32, a.shape, 0)
    h6 = jnp.where(first_tile, 0.0, halo_ref[6:7, cols])
    h7 = jnp.where(first_tile, 0.0, halo_ref[7:8, cols])
    a1 = jnp.where(row == 0, h7, pltpu.roll(a, 1, 0))
    a2 = jnp.where(row == 0, h6, jnp.where(row == 1, h7, pltpu.roll(a, 2, 0)))
    c = w_ref[2:3, cols] * a + w_ref[1:2, cols] * a1 + w_ref[0:1, cols] * a2 + b_ref[:, cols]
    return c, (a2, a1, a)


def _conv_specs(tm):
    halo_blocks = tm // 8
    return [BS((tm, D_FF), lambda i: (i, 0)), BS((tm, D_FF), lambda i: (i, 1)),
            BS((8, D_FF), lambda i: (jnp.maximum(i * halo_blocks - 1, 0), 0)),
            BS((8, D_FF), lambda i: (jnp.maximum(i * halo_blocks - 1, 0), 1)),
            BS((3, D_FF), lambda i: (0, 0)), BS((3, D_FF), lambda i: (0, 1)),
            BS((1, D_FF), lambda i: (0, 0)), BS((1, D_FF), lambda i: (0, 1))]


def convgate_down_loss(a, cw, cb, w, res, target):
    S = a.shape[0]
    N = w.shape[1]
    tm = _tile(S, (256,))

    def body(ag_ref, au_ref, hg_ref, hu_ref, wg_ref, wu_ref, bg_ref, bu_ref, w_ref, r_ref, t_ref, f_ref, d_ref,
             l_ref):
        first_tile = pl.program_id(0) == 0

        @pl.when(first_tile)
        def _():
            l_ref[...] = jnp.zeros_like(l_ref)

        for c0 in range(0, D_FF, CONV_COLS):
            cols = slice(c0, c0 + CONV_COLS)
            cg, _ = _conv_taps(ag_ref, hg_ref, wg_ref, bg_ref, cols, first_tile)
            cu, _ = _conv_taps(au_ref, hu_ref, wu_ref, bu_ref, cols, first_tile)
            f_ref[:, cols] = (_gelu(cg) * cu).astype(f_ref.dtype)
        e = _dot(f_ref[...], w_ref[...]) + r_ref[...] - t_ref[...]
        d_ref[...] = e * (1.0 / N)
        l_ref[...] += jnp.sum(e * e) * (0.5 / N)

    row_n = BS((tm, N), lambda i: (i, 0))
    return _pcall(body, name="convgate_down_loss", grid=(S // tm,),
                  in_specs=_conv_specs(tm) + [_resident((D_FF, N)), row_n, row_n],
                  out_specs=[BS((tm, D_FF), lambda i: (i, 0)), row_n, BS((8, 128), lambda i: (0, 0))],
                  out_shape=[SDS((S, D_FF), MXU_DTYPE), SDS((S, N), F32), SDS((8, 128), F32)])(
        a, a, a, a, cw, cw, cb, cb, w, res, target)


def convgate_bwd(a, dx3, w3, cw, cb, after=None):
    S = a.shape[0]
    tm = _tile(S, (256,))

    def body(ag_ref, au_ref, hg_ref, hu_ref, wg_ref, wu_ref, bg_ref, bu_ref, dx_ref, wd_ref, dc_ref, gw_ref, df_ref):
        first_tile = pl.program_id(0) == 0

        @pl.when(first_tile)
        def _():
            gw_ref[...] = jnp.zeros_like(gw_ref)

        df_ref[...] = _nt_chunks(dx_ref, wd_ref)
        for c0 in range(0, D_FF, CONV_COLS):
            cols, ucols = slice(c0, c0 + CONV_COLS), slice(D_FF + c0, D_FF + c0 + CONV_COLS)
            cg, g_taps = _conv_taps(ag_ref, hg_ref, wg_ref, bg_ref, cols, first_tile)
            cu, u_taps = _conv_taps(au_ref, hu_ref, wu_ref, bu_ref, cols, first_tile)
            df_v = df_ref[:, cols]
            dcg = df_v * cu * _gelu_grad(cg)
            dcu = df_v * _gelu(cg)
            dc_ref[:, cols] = dcg
            dc_ref[:, ucols] = dcu
            for col, dcv, taps in ((cols, dcg, g_taps), (ucols, dcu, u_taps)):
                for j in range(3):
                    gw_ref[j:j + 1, col] += jnp.sum(dcv * taps[j], axis=0, keepdims=True)
                gw_ref[3:4, col] += jnp.sum(dcv, axis=0, keepdims=True)

    return _pcall(body, name="convgate_bwd", grid=(S // tm,), after=after,
                  in_specs=_conv_specs(tm) + [BS((tm, dx3.shape[1]), lambda i: (i, 0)), _resident(w3.shape)],
                  out_specs=[BS((tm, 2 * D_FF), lambda i: (i, 0)), BS((8, 2 * D_FF), lambda i: (0, 0))],
                  out_shape=[SDS((S, 2 * D_FF), F32), SDS((8, 2 * D_FF), F32)],
                  scratch=[pltpu.VMEM((tm, D_FF), F32)])(a, a, a, a, cw, cw, cb, cb, dx3, w3)


def conv_transpose_rms_bwd(dc, cw, w3, x, g, dres):
    S, C = dc.shape
    Q, N, Kc = w3.shape
    tm = _tile(S, (256,))
    nt = S // tm
    halo_blocks = tm // 8

    def body(dc_ref, halo_ref, cw_ref, w_ref, x_ref, g_ref, dr_ref, da_ref, dx_ref, dg_ref):
        @pl.when(pl.program_id(0) == 0)
        def _():
            dg_ref[...] = jnp.zeros_like(dg_ref)

        last_tile = pl.program_id(0) == nt - 1
        row = lax.broadcasted_iota(jnp.int32, (tm, CONV_COLS), 0)
        for c0 in range(0, C, CONV_COLS):
            cols = slice(c0, c0 + CONV_COLS)
            h0 = jnp.where(last_tile, 0.0, halo_ref[0:1, cols])
            h1 = jnp.where(last_tile, 0.0, halo_ref[1:2, cols])
            dc_v = dc_ref[:, cols]
            n1 = jnp.where(row == tm - 1, h0, pltpu.roll(dc_v, tm - 1, 0))
            n2 = jnp.where(row == tm - 1, h1, jnp.where(row == tm - 2, h0, pltpu.roll(dc_v, tm - 2, 0)))
            da_ref[:, cols] = (cw_ref[2:3, cols] * dc_v + cw_ref[1:2, cols] * n1
                               + cw_ref[0:1, cols] * n2).astype(da_ref.dtype)
        xv = x_ref[...]
        dx, dgc = _rms_bwd(_nt_chunks(da_ref, w_ref), xv, g_ref[...], _rms(xv))
        dx_ref[...] = dr_ref[...] + dx
        _acc_rows(dg_ref, 0, dgc)

    row_n = BS((tm, N), lambda i: (i, 0))
    return _pcall(body, name="conv_transpose_rms_bwd", grid=(nt,),
                  in_specs=[BS((tm, C), lambda i: (i, 0)),
                            BS((8, C), lambda i: (jnp.minimum((i + 1) * halo_blocks, S // 8 - 1), 0)),
                            BS((3, C), lambda i: (0, 0)), _resident((Q, N, Kc)), row_n, BS((1, N), lambda i: (0, 0)),
                            row_n],
                  out_specs=[BS((tm, C), lambda i: (i, 0)), row_n, BS((8, N), lambda i: (0, 0))],
                  out_shape=[SDS((S, C), MXU_DTYPE), SDS((S, N), F32), SDS((8, N), F32)])(dc, dc, cw, w3, x, g, dres)


def xattn_bwd(qx, dxo, kn, vb, gxq, after=None):
    S = qx.shape[0]
    tm = _tile(S, (512, 256))

    def body(q_ref, do_ref, kn_ref, vb_ref, g_ref, dq_ref, dkn_ref, dv_ref, dg_ref):
        @pl.when(pl.program_id(0) == 0)
        def _():
            dkn_ref[...] = jnp.zeros_like(dkn_ref)
            dv_ref[...] = jnp.zeros_like(dv_ref)
            dg_ref[...] = jnp.zeros_like(dg_ref)

        g = g_ref[...]
        for h in range(XA_HEADS):
            sl = slice(h * XA_DH, (h + 1) * XA_DH)
            qh, do = q_ref[:, sl], do_ref[:, sl]
            r, qn, p = _xa_probs(qh, g, kn_ref[:, sl])
            dp = _dot(do, vb_ref[:, sl], NT)
            ds = p * (dp - jnp.sum(dp * p, axis=1, keepdims=True)) * (1.0 / math.sqrt(XA_DH))
            dqn = _dot(ds, kn_ref[:, sl])
            dkn_ref[:, sl] += _dot(ds, qn, TN)
            dv_ref[:, sl] += _dot(p, do, TN)
            dqh, dgc = _rms_bwd(dqn, qh, g, r)
            dq_ref[:, sl] = dqh.astype(dq_ref.dtype)
            _acc_rows(dg_ref, 0, dgc)

    row = BS((tm, 1024), lambda i: (i, 0))
    full = lambda r, w: BS((r, w), lambda i: (0, 0))
    return _pcall(body, name="xattn_bwd", grid=(S // tm,), after=after,
                  in_specs=[row, row, full(MEM_LEN, 1024), full(MEM_LEN, 1024), full(1, XA_DH)],
                  out_specs=[row, full(MEM_LEN, 1024), full(MEM_LEN, 1024), full(8, XA_DH)],
                  out_shape=[SDS((S, 1024), MXU_DTYPE), SDS((MEM_LEN, 1024), F32), SDS((MEM_LEN, 1024), F32),
                             SDS((8, XA_DH), F32)])(qx, dxo, kn, vb, gxq)


def mem_bwd(kv, dkn, dvb, gxk, after=None):
    def body(kv_ref, dkn_ref, dv_ref, g_ref, dkv_ref, dg_ref):
        dg_ref[...] = jnp.zeros_like(dg_ref)
        for h in range(XA_HEADS):
            sl = slice(h * XA_DH, (h + 1) * XA_DH)
            k = kv_ref[:, sl]
            dk, dgc = _rms_bwd(dkn_ref[:, sl], k, g_ref[...], _rms(k))
            dkv_ref[:, sl] = dk.astype(dkv_ref.dtype)
            _acc_rows(dg_ref, 0, dgc)
        dkv_ref[:, 1024:2048] = dv_ref[...].astype(dkv_ref.dtype)

    full = lambda r, w: BS((r, w), lambda i: (0, 0))
    return _pcall(body, name="mem_bwd", grid=(1,), after=after,
                  in_specs=[full(MEM_LEN, 2048), full(MEM_LEN, 1024), full(MEM_LEN, 1024), full(1, XA_DH)],
                  out_specs=[full(MEM_LEN, 2048), full(8, XA_DH)],
                  out_shape=[SDS((MEM_LEN, 2048), MXU_DTYPE), SDS((8, XA_DH), F32)])(kv, dkn, dvb, gxk)


def _norm_rope_bwd(slabs, douts, g, bm, cos_v, sin_v, first):
    dqns = [d * cos_v + _half_swap(d * sin_v, first) for d in douts]
    rs = [lax.rsqrt(ms + EPS) for ms in _head_means([x * x for x in slabs], bm)]
    projs = _head_means([dqn * g * x for dqn, x in zip(dqns, slabs)], bm)
    dxs = [r * (dqn * g) - x * (r * r * r) * pr for x, dqn, r, pr in zip(slabs, dqns, rs, projs)]
    return dxs, [dqn * x * r for x, dqn, r in zip(slabs, dqns, rs)]


def mixer_core_bwd(proj, cos, sin, gq, gk, gvg, bmat, qr, kr, vb, sinkcol, dattn, dgm, gvn, gu, w2, w2t, bsl,
                   after=None):
    S = qr.shape[0]
    nb = S // BLK

    def body(p_ref, c_ref, s_ref, gq_ref, gk_ref, gvg_ref, b_ref, q_ref, kc_ref, kp_ref, vc_ref, vp_ref, sk_ref,
             do_ref, dgm_ref, gvn_ref, gu_ref, w2_ref, w2t_ref, bsl_ref,
             dp_ref, dsk_ref, dws_ref, dbl_ref, dgq_ref, dgk_ref, dgv_ref,
             carry_k, carry_v, done_k, done_v, dq_keep, dgu_keep, dgvn_keep):
        n = pl.program_id(0)

        @pl.when(n == 0)
        def _():
            for ref in (dsk_ref, dws_ref, dbl_ref, dgq_ref, dgk_ref, dgv_ref, carry_k, carry_v, dq_keep, dgu_keep,
                        dgvn_keep):
                ref[...] = jnp.zeros_like(ref)

        live = (n < nb).astype(F32)
        cos_v, sin_v, bm = c_ref[...], s_ref[...], b_ref[...]
        first = (_lane((BLK, 128)) & 63) < 32
        lo = _lane((BLK, 128)) < 64

        dxs, dgs = _norm_rope_bwd([p_ref[:, s * 128:(s + 1) * 128] for s in range(4)],
                                  [dq_keep[:, s * 128:(s + 1) * 128] for s in range(4)], gq_ref[...], bm,
                                  cos_v, sin_v, first)
        for s, (dx, dg) in enumerate(zip(dxs, dgs)):
            dp_ref[:, s * 128:(s + 1) * 128] = dx.astype(dp_ref.dtype)
            _acc_rows(dgq_ref, 0, dg)
        dp_ref[:, 1024:1536] = (dgu_keep[...] * _gelu_grad(p_ref[:, 1024:1536])).astype(dp_ref.dtype)
        gvp = p_ref[:, 1536:2048]
        gv = _gelu(gvp)
        dgv, dgc = _rms_bwd(dgvn_keep[...], gv, gvg_ref[...], _rms(gv))
        dp_ref[:, 1536:2048] = (dgv * _gelu_grad(gvp)).astype(dp_ref.dtype)
        _acc_rows(dgv_ref, 0, dgc)

        for h in range(2):
            hs, qs = slice(h * 128, (h + 1) * 128), slice(h * 256, (h + 1) * 256)
            kd = jnp.concatenate([kp_ref[:, hs], kc_ref[:, hs]], axis=0)
            vd = jnp.concatenate([vp_ref[:, hs], vc_ref[:, hs]], axis=0)
            sink = jnp.concatenate([sk_ref[2 * h], sk_ref[2 * h + 1]], axis=0)
            qp, p, psink = _swa_probs(q_ref[:, qs], kd, sink, n, lo)
            dop = _by_head(do_ref[:, qs], lo)
            dp = _dot(dop, vd, NT)
            delta = jnp.sum(dp * p, axis=1, keepdims=True)
            ds = p * (dp - delta) * (1.0 / math.sqrt(HEAD_DIM))
            dsink = -psink * delta * live
            dsk_ref[2 * h] += dsink[:2 * BLK]
            dsk_ref[2 * h + 1] += dsink[2 * BLK:]
            dq_keep[:, qs] = _from_heads(_dot(ds, kd), lo)
            dkd = _dot(ds, qp, TN)
            dvd = _dot(p, dop, TN)
            done_k[:, hs] = carry_k[:, hs] + live * dkd[:BLK]
            done_v[:, hs] = carry_v[:, hs] + live * dvd[:BLK]
            carry_k[:, hs] = dkd[BLK:]
            carry_v[:, hs] = dvd[BLK:]
        for j in range(4):
            sl = slice(j * 128, (j + 1) * 128)
            gvn_s = gvn_ref[:, sl]
            m2 = _dot(w2_ref[j], gvn_s)
            mixed = jnp.where(lo, m2[:BLK], m2[BLK:]) + bsl_ref[j]
            dgm_s = dgm_ref[:, sl]
            dgu_keep[:, sl] = dgm_s * mixed
            dmx = dgm_s * gu_ref[:, sl] * live
            d2 = _dot(w2t_ref[j], dmx)
            dgvn_keep[:, sl] = jnp.where(lo, d2[:BLK], d2[BLK:])
            z = jnp.zeros_like(dmx)
            dws_ref[2 * j] += _dot(jnp.where(lo, dmx, z), gvn_s, NT)
            dws_ref[2 * j + 1] += _dot(jnp.where(lo, z, dmx), gvn_s, NT)
            dbl_ref[j] += dmx

        dxs, dgs = _norm_rope_bwd([p_ref[:, 512 + s * 128:640 + s * 128] for s in range(2)],
                                  [done_k[:, s * 128:(s + 1) * 128] for s in range(2)], gk_ref[...], bm,
                                  cos_v, sin_v, first)
        for s, (dx, dg) in enumerate(zip(dxs, dgs)):
            dp_ref[:, 512 + s * 128:640 + s * 128] = dx.astype(dp_ref.dtype)
            _acc_rows(dgk_ref, 0, dg)
        dp_ref[:, 768:1024] = done_v[...].astype(dp_ref.dtype)

    last = nb - 1
    cur = lambda w: BS((BLK, w), lambda n: (jnp.minimum(n, last), 0))
    prev = lambda w: BS((BLK, w), lambda n: (jnp.clip(n - 1, 0, last), 0))
    done = lambda w: BS((BLK, w), lambda n: (jnp.maximum(n - 1, 0), 0))
    const = lambda *shape: BS(shape, lambda n: (0,) * len(shape))
    return _pcall(body, name="mixer_core_bwd", grid=(nb + 1,), after=after,
                  in_specs=[done(IN_COLS_DUP), done(128), done(128), const(1, 128), const(1, 128), const(1, 512),
                            const(128, 128), cur(512), cur(256), prev(256), cur(256), prev(256),
                            const(4, 2 * BLK, 1), cur(512), cur(512), cur(512), cur(512), const(4, 2 * BLK, BLK),
                            const(4, 2 * BLK, BLK), const(4, BLK, 128)],
                  out_specs=[done(IN_COLS_DUP), const(4, 2 * BLK, 1), const(8, BLK, BLK), const(4, BLK, 128),
                             const(8, 128), const(8, 128), const(8, 512)],
                  out_shape=[SDS((S, IN_COLS_DUP), MXU_DTYPE), SDS((4, 2 * BLK, 1), F32), SDS((8, BLK, BLK), F32),
                             SDS((4, BLK, 128), F32), SDS((8, 128), F32), SDS((8, 128), F32), SDS((8, 512), F32)],
                  scratch=[pltpu.VMEM((BLK, 256), F32)] * 4 + [pltpu.VMEM((BLK, 512), F32)] * 3)(
        proj, cos, sin, gq, gk, gvg, bmat, qr, kr, kr, vb, vb, sinkcol, dattn, dgm, gvn, gu, w2, w2t, bsl)


BIG = (("w_in", (1024, 448), True), ("w_out", (256, 1024), False), ("xa_wq", (256, 1024), False),
       ("xa_wkv", (1024, 512), True), ("xa_wo", (256, 1024), False), ("ffn_up", (1024, 1408), True),
       ("ffn_down", (704, 1024), False))
BIG_NAMES = tuple(n for n, _, _ in BIG)
SMALL_VECS = (("mix_norm", 1024), ("q_norm", 64), ("k_norm", 64), ("attn_sinks", 8), ("gmlp_v_norm", 512),
              ("attn_out_norm", 512), ("gmlp_out_norm", 512), ("xa_norm", 1024), ("mem_norm", 1024),
              ("xa_q_norm", 256), ("xa_k_norm", 256), ("ffn_norm", 1024), ("ffn_conv_b", 5632))
SMALL = tuple(n for n, _ in SMALL_VECS) + ("gmlp_bs", "gmlp_ws", "ffn_conv")
WEIGHTS = ("mix_norm", "w_in", "q_norm", "k_norm", "attn_sinks", "gmlp_v_norm", "gmlp_ws", "gmlp_bs",
           "attn_out_norm", "gmlp_out_norm", "w_out", "xa_norm", "mem_norm", "xa_wq", "xa_wkv", "xa_q_norm",
           "xa_k_norm", "xa_wo", "ffn_norm", "ffn_up", "ffn_conv", "ffn_conv_b", "ffn_down")
CONV_SHARD = (3, 1408)
CONV_LANE_ROWS = CONV_SHARD[1] // 128
CONV_CHIP_ROWS = 40


def _small_rows():
    rows, r = {}, 0
    for n, length in SMALL_VECS:
        rows[n] = r
        r += -(-length // 128)
    r += -r % 8
    rows["gmlp_bs"] = r
    r += 8
    rows["gmlp_ws"] = r
    r += 8 * BLK
    rows["ffn_conv"] = r
    r += N_CHIPS * CONV_CHIP_ROWS
    return rows, r


SMALL_ROW, SMALL_ROWS = _small_rows()


def pack_small(dg_mix, dgq, dgk, dsk, dg_gvn, dg_y, dg_xa, dg_mem, dg_xq, dg_xk, dg_ffn, gcw, dbl, dws):
    def body(mix_ref, q_ref, k_ref, sk_ref, gvn_ref, y_ref, xa_ref, mem_ref, xq_ref, xk_ref, ffn_ref, cw_ref,
             dbl_ref, dws_ref, o_ref):
        o_ref[...] = jnp.zeros_like(o_ref)
        lane = _lane((1, 128))

        def put(name, src_ref, row, lane0, length):
            for k in range(length // 128):
                o_ref[SMALL_ROW[name] + k:SMALL_ROW[name] + k + 1, :] = src_ref[row:row + 1, lane0 + k * 128:lane0 + (k + 1) * 128]

        put("mix_norm", mix_ref, 0, 0, 1024)
        for name, ref in (("q_norm", q_ref), ("k_norm", k_ref)):
            v = ref[0:1, :]
            o_ref[SMALL_ROW[name]:SMALL_ROW[name] + 1, :] = jnp.where(lane < HEAD_DIM, v + pltpu.roll(v, 64, 1), 0.0)
        sinks = jnp.zeros((1, 128), F32)
        for s in range(4):
            col = sk_ref[s]
            sinks = sinks + jnp.where(lane == 2 * s, jnp.sum(col[:BLK]), 0.0) + jnp.where(lane == 2 * s + 1, jnp.sum(col[BLK:]), 0.0)
        o_ref[SMALL_ROW["attn_sinks"]:SMALL_ROW["attn_sinks"] + 1, :] = sinks
        put("gmlp_v_norm", gvn_ref, 0, 0, 512)
        put("attn_out_norm", y_ref, 0, 0, 512)
        put("gmlp_out_norm", y_ref, 0, 512, 512)
        put("xa_norm", xa_ref, 0, 0, 1024)
        put("mem_norm", mem_ref, 0, 0, 1024)
        put("xa_q_norm", xq_ref, 0, 0, 256)
        put("xa_k_norm", xk_ref, 0, 0, 256)
        put("ffn_norm", ffn_ref, 0, 0, 1024)
        put("ffn_conv_b", cw_ref, 3, 0, 2 * D_FF)
        r8 = lax.broadcasted_iota(jnp.int32, (8, 128), 0)
        l8 = _lane((8, 128))
        bs = jnp.zeros((8, BLK), F32)
        for j in range(4):
            sel = (((r8 == 2 * j) & (l8 < 64)) | ((r8 == 2 * j + 1) & (l8 >= 64))).astype(F32).astype(BF16)
            xj = dbl_ref[j]
            hi = xj.astype(BF16)
            lo = (xj - hi.astype(F32)).astype(BF16)
            bs = bs + lax.dot_general(sel, hi, NT, preferred_element_type=F32) + lax.dot_general(sel, lo, NT, preferred_element_type=F32)
        o_ref[SMALL_ROW["gmlp_bs"]:SMALL_ROW["gmlp_bs"] + 8, :] = bs
        causal = lax.broadcasted_iota(jnp.int32, (BLK, BLK), 0) >= lax.broadcasted_iota(jnp.int32, (BLK, BLK), 1)
        for h in range(8):
            r0 = SMALL_ROW["gmlp_ws"] + h * BLK
            o_ref[r0:r0 + BLK, :] = jnp.where(causal, dws_ref[h], 0.0)
        for q in range(N_CHIPS):
            for j in range(3):
                for k in range(CONV_LANE_ROWS):
                    r0 = SMALL_ROW["ffn_conv"] + q * CONV_CHIP_ROWS + j * CONV_LANE_ROWS + k
                    l0 = (q * CONV_LANE_ROWS + k) * 128
                    o_ref[r0:r0 + 1, :] = cw_ref[j:j + 1, l0:l0 + 128]

    args = (dg_mix, dgq, dgk, dsk, dg_gvn, dg_y, dg_xa, dg_mem, dg_xq, dg_xk, dg_ffn, gcw, dbl, dws)
    full = lambda a: BS(a.shape, lambda i, nd=a.ndim: (0,) * nd)
    return _pcall(body, name="pack_small", grid=(1,), in_specs=[full(a) for a in args],
                  out_specs=BS((SMALL_ROWS, 128), lambda i: (0, 0)), out_shape=SDS((SMALL_ROWS, 128), F32))(*args)


def _adam(w, g, m, v):
    mn = ADAM_B1 * m + (1.0 - ADAM_B1) * g
    vn = ADAM_B2 * v + (1.0 - ADAM_B2) * (g * g)
    m_hat = mn / (1.0 - ADAM_B1 ** ADAM_STEP)
    v_hat = vn / (1.0 - ADAM_B2 ** ADAM_STEP)
    return -ADAM_LR * (m_hat / (jnp.sqrt(v_hat) + ADAM_EPS) + ADAM_WD * w), mn, vn


def adamw_small(gsum, w, m, v, chipvec):
    n = len(SMALL)

    def body(chip_ref, g_ref, *refs):
        w_refs, m_refs, v_refs = refs[:n], refs[n:2 * n], refs[2 * n:3 * n]
        outs = refs[3 * n:]
        go, do, mo, vo = outs[:n], outs[n:2 * n], outs[2 * n:3 * n], outs[3 * n:]

        def update(i, idx, g):
            d, mn, vn = _adam(w_refs[i][idx], g, m_refs[i][idx], v_refs[i][idx])
            go[i][idx] = g
            do[i][idx] = d
            mo[i][idx] = mn
            vo[i][idx] = vn

        for i, (name, length) in enumerate(SMALL_VECS):
            for k in range(-(-length // 128)):
                wd = min(128, length - k * 128)
                r = SMALL_ROW[name] + k
                update(i, (slice(0, 1), slice(k * 128, k * 128 + wd)), g_ref[r:r + 1, 0:wd])
        i_bs, i_ws, i_cv = len(SMALL_VECS), len(SMALL_VECS) + 1, len(SMALL_VECS) + 2
        update(i_bs, (0,), g_ref[SMALL_ROW["gmlp_bs"]:SMALL_ROW["gmlp_bs"] + 8, :])
        for h in range(8):
            r0 = SMALL_ROW["gmlp_ws"] + h * BLK
            update(i_ws, (0, h), g_ref[r0:r0 + BLK, :])
        mine = g_ref[pl.ds(pl.multiple_of(SMALL_ROW["ffn_conv"] + chip_ref[0] * CONV_CHIP_ROWS, 8), CONV_CHIP_ROWS), :]
        for j in range(3):
            for k in range(CONV_LANE_ROWS):
                r = j * CONV_LANE_ROWS + k
                update(i_cv, (0, slice(j, j + 1), slice(k * 128, (k + 1) * 128)), mine[r:r + 1, :])

    nat = [w[nm] for nm in SMALL]
    full = lambda a: BS(a.shape, lambda i, c, nd=a.ndim: (0,) * nd)
    outs = _pcall(body, name="adamw_small", grid=(1,), prefetch=1,
                  in_specs=[BS((SMALL_ROWS, 128), lambda i, c: (0, 0))] + [full(a) for a in nat] * 3,
                  out_specs=[full(a) for a in nat] * 4, out_shape=[SDS(a.shape, F32) for a in nat] * 4)(
        chipvec, gsum, *nat, *[m[nm] for nm in SMALL], *[v[nm] for nm in SMALL])
    return outs[:n], outs[n:2 * n], outs[2 * n:3 * n], outs[3 * n:]


def adamw_matrix(w, m, v, g_own, g_other, cvec, *, name):
    _, r, c = w.shape
    half = r // 2
    tr = _tile(half, (128, 176))
    T = half // tr

    def body(c_ref, w_ref, m_ref, v_ref, own_ref, oth_ref, g_ref, d_ref, mo_ref, vo_ref):
        g = jnp.where(pl.program_id(0) == c_ref[0], own_ref[...], oth_ref[...])
        d, mn, vn = _adam(w_ref[...], g, m_ref[...], v_ref[...])
        g_ref[...] = g
        d_ref[...] = d
        mo_ref[...] = mn
        vo_ref[...] = vn

    nat = BS((None, tr, c), lambda hf, t, cr: (0, hf * T + t, 0))
    hlf = BS((tr, c), lambda hf, t, cr: (t, 0))
    return _pcall(body, name=name, grid=(2, T), prefetch=1, in_specs=[nat, nat, nat, hlf, hlf], out_specs=[nat] * 4,
                  out_shape=[SDS(w.shape, F32)] * 4)(cvec, w, m, v, g_own, g_other)


def _place():
    return lax.axis_index("x"), lax.axis_index("y"), lax.axis_index("c")


def _other_chips(x, y):
    return [(1 - x, y), (x, 1 - y), (1 - x, 1 - y)]


def _rows_of_core(c, half):
    return pl.ds(pl.multiple_of(c * half, 16), half)


def _rcopy(src, dst, sems, k, to):
    return pltpu.make_async_remote_copy(src_ref=src, dst_ref=dst, send_sem=sems[0].at[k], recv_sem=sems[1].at[k],
                                        device_id=to, device_id_type=MESH)


def _comm_call(body, *, name, out_shape, n_in, n_sems, aliases=None):
    return pl.pallas_call(body, name=name, out_shape=out_shape, in_specs=[ANY] * n_in, out_specs=[ANY] * len(out_shape),
                          scratch_shapes=[pltpu.SemaphoreType.DMA((n_sems,)), pltpu.SemaphoreType.DMA((n_sems,))],
                          input_output_aliases=aliases or {},
                          compiler_params=pltpu.CompilerParams(has_side_effects=True))


def cast_shards(shards, conv, chipvec):
    n = len(shards)

    def body(chip_ref, *refs):
        for i_ref, o_ref in zip(refs[:n + 1], refs[n + 1:]):
            o_ref[...] = i_ref[...].astype(o_ref.dtype)

    in_specs = [BS((s.shape[0] // 4, s.shape[1]), lambda i, p: (i, 0)) for s in shards]
    in_specs.append(BS(conv.shape, lambda i, p: (0, 0)))
    out_specs = [BS((None, s.shape[0] // 4, s.shape[1]), lambda i, p: (p[0], i, 0)) for s in shards]
    out_specs.append(BS((None,) + conv.shape, lambda i, p: (p[0], 0, 0)))
    out_shape = [SDS((N_CHIPS,) + s.shape, MXU_DTYPE) for s in shards] + [SDS((N_CHIPS,) + conv.shape, F32)]
    return _pcall(body, name="cast_shards", grid=(4,), prefetch=1, in_specs=in_specs, out_specs=out_specs,
                  out_shape=out_shape)(chipvec, *shards, conv)


HBM = pl.BlockSpec(memory_space=pltpu.HBM)
SEM = pl.BlockSpec(memory_space=pltpu.SEMAPHORE)
DATAFLOW = pltpu.SideEffectType.DATAFLOW_SIDE_EFFECTING
VMEM_WHOLE = pl.BlockSpec(memory_space=pltpu.VMEM)
TOKEN = jax.ShapeDtypeStruct((8, 128), jnp.float32)


def _gather_copies(bufs, send_sems, recv_sems, outgoing):
    x, y, c = _place()
    p = 2 * x + y
    cps = []
    for i, o in enumerate(bufs):
        for j, (cx, cy) in enumerate(_other_chips(x, y)):
            slot = o.at[p] if outgoing else o.at[2 * cx + cy]
            cps.append(_rcopy(slot, slot, (send_sems, recv_sems), 3 * i + j, (cx, cy, c)))
    return cps


def gather_start(slots):
    n = len(slots)

    def body(*refs):
        send_sems, recv_sems, thru, token = refs[n], refs[n + 1], refs[n + 2:2 * n + 2], refs[2 * n + 2]
        for cp in _gather_copies(thru, send_sems, recv_sems, True):
            cp.start()
        token[...] = jnp.zeros_like(token)

    hbm = [pltpu.with_memory_space_constraint(s, pltpu.HBM) for s in slots]
    outs = pl.pallas_call(
        body, name="gather_start_%d" % n,
        out_shape=[pltpu.SemaphoreType.DMA((3 * n,)), pltpu.SemaphoreType.DMA((3 * n,))]
        + [pltpu.HBM(s.shape, s.dtype) for s in slots] + [TOKEN],
        in_specs=[HBM] * n, out_specs=[SEM, SEM] + [HBM] * n + [VMEM_WHOLE],
        input_output_aliases={i: 2 + i for i in range(n)},
        compiler_params=pltpu.CompilerParams(has_side_effects=DATAFLOW))(*hbm)
    return outs[0], outs[1], outs[2:2 + n], outs[2 + n]


def gather_wait(send_sems, recv_sems, bufs, after):
    n = len(bufs)

    def body(*refs):
        ins, send_ref, recv_ref = refs[:n], refs[n], refs[n + 1]
        for cp in _gather_copies(ins, send_ref, recv_ref, False):
            cp.wait_send()
            cp.wait_recv()

    return pl.pallas_call(
        body, name="gather_wait_%d" % n, out_shape=[pltpu.HBM(s.shape, s.dtype) for s in bufs],
        in_specs=[HBM] * n + [SEM, SEM, ANY], out_specs=[HBM] * n, input_output_aliases={i: i for i in range(n)},
        compiler_params=pltpu.CompilerParams(has_side_effects=DATAFLOW))(*bufs, send_sems, recv_sems, after)


def _peers(x, y, c):
    return [(1 - x if k & 4 else x, 1 - y if k & 2 else y, 1 - c if k & 1 else c) for k in range(1, N_DEV)]


def _partial_copies(g_ref, land_ref, send_sems, recv_sems, outgoing):
    x, y, c = _place()
    half = g_ref.shape[1] // 2
    cps = []
    for k, (px, py, pc) in enumerate(_peers(x, y, c)):
        src = g_ref.at[2 * px + py, _rows_of_core(pc, half)]
        dst = land_ref.at[4 * x + 2 * y + c] if outgoing else land_ref.at[4 * px + 2 * py + pc]
        cps.append(_rcopy(src, dst, (send_sems, recv_sems), k, (px, py, pc)))
    return cps


def partials_start(g, *, name):
    land = lax.empty((N_DEV, g.shape[1] // 2, g.shape[2]), g.dtype)

    def body(g_ref, land_ref, send_sems, recv_sems, g_thru, land_thru, token):
        for cp in _partial_copies(g_thru, land_thru, send_sems, recv_sems, True):
            cp.start()
        token[...] = jnp.zeros_like(token)

    return pl.pallas_call(
        body, name=name,
        out_shape=[pltpu.SemaphoreType.DMA((N_DEV - 1,)), pltpu.SemaphoreType.DMA((N_DEV - 1,)),
                   pltpu.HBM(g.shape, g.dtype), pltpu.HBM(land.shape, land.dtype), TOKEN],
        in_specs=[HBM, HBM], out_specs=[SEM, SEM, HBM, HBM, VMEM_WHOLE], input_output_aliases={0: 2, 1: 3},
        compiler_params=pltpu.CompilerParams(has_side_effects=DATAFLOW))(
        pltpu.with_memory_space_constraint(g, pltpu.HBM), pltpu.with_memory_space_constraint(land, pltpu.HBM))


def partials_wait(started, after):
    n = len(started)

    def body(*refs):
        for i in range(n):
            send_ref, recv_ref, g_ref, land_ref = refs[4 * i:4 * i + 4]
            for cp in _partial_copies(g_ref, land_ref, send_ref, recv_ref, False):
                cp.wait_send()
                cp.wait_recv()

    flat = [a for s in started for a in s]
    bufs = [a for s in started for a in s[2:]]
    outs = pl.pallas_call(
        body, name="partials_wait", out_shape=[pltpu.HBM(b.shape, b.dtype) for b in bufs],
        in_specs=[SEM, SEM, HBM, HBM] * n + [ANY], out_specs=[HBM] * (2 * n),
        input_output_aliases={4 * i + 2 + j: 2 * i + j for i in range(n) for j in range(2)},
        compiler_params=pltpu.CompilerParams(has_side_effects=DATAFLOW))(*flat, after)
    return [(outs[2 * i], outs[2 * i + 1]) for i in range(n)]


def sum_partials(pairs, order):
    n = len(pairs)

    def body(o_ref, *refs):
        j = pl.program_id(0)
        for g_ref, l_ref, f_ref in zip(refs[:n], refs[n:2 * n], refs[2 * n:]):
            @pl.when(j == 0)
            def _():
                f_ref[...] = g_ref[...].astype(F32)

            @pl.when(j > 0)
            def _():
                f_ref[...] += l_ref[...].astype(F32)

    g4 = [g.reshape(g.shape[0], 2, g.shape[1] // 2, g.shape[2]) for g, _ in pairs]
    lands = [l for _, l in pairs]
    return _pcall(body, name="sum_partials", grid=(N_DEV,), prefetch=1,
                  in_specs=[BS((None, None) + g.shape[2:], lambda j, o: (o[0], o[1], 0, 0)) for g in g4]
                  + [BS((None,) + l.shape[1:], lambda j, o: (o[jnp.maximum(j, 1) + 1], 0, 0)) for l in lands],
                  out_specs=[BS(l.shape[1:], lambda j, o: (0, 0)) for l in lands],
                  out_shape=[SDS(l.shape[1:], F32) for l in lands])(order, *g4, *lands)


def pair_share(fs):
    n = len(fs)

    def body(*refs):
        f_refs, o_refs, sems = refs[:n], refs[n:2 * n], refs[2 * n:]
        x, y, c = _place()
        cps = [_rcopy(f, o, sems, i, (x, y, 1 - c)) for i, (f, o) in enumerate(zip(f_refs, o_refs))]
        for cp in cps:
            cp.start()
        for cp in cps:
            cp.wait()

    return _comm_call(body, name="pair_share", n_in=n, n_sems=n, out_shape=[SDS(f.shape, f.dtype) for f in fs])(*fs)


def _small_copies(s_ref, land_ref, send_sems, recv_sems, outgoing):
    x, y, c = _place()
    cps = []
    for k, (px, py, pc) in enumerate(_peers(x, y, c)):
        dst = land_ref.at[4 * x + 2 * y + c] if outgoing else land_ref.at[4 * px + 2 * py + pc]
        cps.append(_rcopy(s_ref, dst, (send_sems, recv_sems), k, (px, py, pc)))
    return cps


def small_start(sm):
    land = lax.empty((N_DEV,) + sm.shape, sm.dtype)

    def body(s_ref, land_ref, send_sems, recv_sems, s_thru, land_thru):
        for cp in _small_copies(s_thru, land_thru, send_sems, recv_sems, True):
            cp.start()

    return pl.pallas_call(
        body, name="small_start",
        out_shape=[pltpu.SemaphoreType.DMA((N_DEV - 1,)), pltpu.SemaphoreType.DMA((N_DEV - 1,)),
                   pltpu.HBM(sm.shape, sm.dtype), pltpu.HBM(land.shape, land.dtype)],
        in_specs=[HBM, HBM], out_specs=[SEM, SEM, HBM, HBM], input_output_aliases={0: 2, 1: 3},
        compiler_params=pltpu.CompilerParams(has_side_effects=DATAFLOW))(
        pltpu.with_memory_space_constraint(sm, pltpu.HBM), pltpu.with_memory_space_constraint(land, pltpu.HBM))


def small_wait(send_sems, recv_sems, sm, land, after):
    def body(send_ref, recv_ref, s_ref, land_ref, after_ref, s_out, land_out):
        for cp in _small_copies(s_ref, land_ref, send_ref, recv_ref, False):
            cp.wait_send()
            cp.wait_recv()

    return pl.pallas_call(
        body, name="small_wait", out_shape=[pltpu.HBM(sm.shape, sm.dtype), pltpu.HBM(land.shape, land.dtype)],
        in_specs=[SEM, SEM, HBM, HBM, ANY], out_specs=[HBM, HBM], input_output_aliases={2: 0, 3: 1},
        compiler_params=pltpu.CompilerParams(has_side_effects=DATAFLOW))(send_sems, recv_sems, sm, land, after)


def sum_small(own, land, mevec):
    n, rows, width = land.shape
    tr = _tile(rows, (184, 8))

    def body(me_ref, own_ref, land_ref, o_ref):
        acc = jnp.zeros((tr, width), F32)
        for s in range(n):
            acc = acc + jnp.where(me_ref[0] == s, own_ref[...], land_ref[s])
        o_ref[...] = acc

    return _pcall(body, name="sum_small", grid=(rows // tr,), prefetch=1,
                  in_specs=[BS((tr, width), lambda i, me: (i, 0)), BS((n, tr, width), lambda i, me: (0, i, 0))],
                  out_specs=BS((tr, width), lambda i, me: (i, 0)), out_shape=SDS((rows, width), F32))(mevec, own, land)


def _to_full(blk, col):
    n, r, c = blk.shape
    return blk.transpose(1, 0, 2).reshape(r, n * c) if col else blk.reshape(n * r, c)


def _dup_cols(w):
    dup = lambda t: jnp.concatenate([t[:, :64], t[:, :64], t[:, 64:], t[:, 64:]], axis=1)
    return jnp.concatenate([w[:, :512], dup(w[:, 512:640]), dup(w[:, 640:768]), w[:, 768:]], axis=1)


def _fold_cols(d):
    fold = lambda t: jnp.concatenate([t[:, 0:64] + t[:, 64:128], t[:, 128:192] + t[:, 192:256]], axis=1)
    return jnp.concatenate([d[:, :512], fold(d[:, 512:768]), fold(d[:, 768:1024]), d[:, 1024:]], axis=1)


def _local_step(x, mem, positions, target, w_in, later, sp, emit):
    gain = lambda n: sp[n].reshape(1, -1)
    half = HEAD_DIM // 2
    inv_freq = 1.0 / (10000.0 ** (jnp.arange(half, dtype=F32) * (2.0 / HEAD_DIM)))
    ang = positions.astype(F32)[:, None] * inv_freq
    cos, sin = jnp.cos(ang), jnp.sin(ang)
    cos128 = jnp.tile(cos, (1, 4))
    sin128 = jnp.concatenate([-sin, sin, -sin, sin], axis=1)
    seg = jnp.arange(128) // HEAD_DIM
    bmat = (seg[:, None] == seg[None, :]).astype(BF16)
    gq128, gk128 = jnp.tile(gain("q_norm"), (1, 2)), jnp.tile(gain("k_norm"), (1, 2))
    sinkcol = jnp.repeat(sp["attn_sinks"].reshape(4, 2), BLK, axis=1).reshape(4, 2 * BLK, 1)
    wsc = sp["gmlp_ws"] * jnp.tril(jnp.ones((BLK, BLK), F32))[None]
    w2 = wsc.reshape(4, 2 * BLK, BLK).astype(MXU_DTYPE)
    w2t = wsc.swapaxes(1, 2).reshape(4, 2 * BLK, BLK).astype(MXU_DTYPE)
    bsl = jnp.repeat(sp["gmlp_bs"].reshape(4, 2, BLK).transpose(0, 2, 1), HEAD_DIM, axis=2)
    cb = sp["ffn_conv_b"].reshape(1, -1)
    w_in_d = _dup_cols(_to_full(w_in, True))[None]

    h1, proj = rms_mm(x, gain("mix_norm"), w_in_d, name="mix_in")
    qr, kr, vb, gu, gvn, attn, gm, y = mixer_core_fwd(proj, cos128, sin128, gq128, gk128, gain("gmlp_v_norm"), bmat,
                                                      sinkcol, gain("attn_out_norm"), w2, bsl, gain("gmlp_out_norm"))
    wf, cw = later(y)
    w_out, xa_wq, xa_wo, ffn_down = (_to_full(wf[n], False) for n in ("w_out", "xa_wq", "xa_wo", "ffn_down"))
    x1 = mm(y, w_out, res=x, name="mix_out")
    h2, qx = rms_mm(x1, gain("xa_norm"), xa_wq[None], name="xa_q")
    mn, kv = rms_mm(mem, gain("mem_norm"), wf["xa_wkv"], name="xa_kv")
    kn, vbx = mem_pre(kv, gain("xa_k_norm"))
    xo = xattn_fwd(qx, kn, vbx, gain("xa_q_norm"))
    x2 = mm(xo, xa_wo, res=x1, name="xa_out")
    h3, a = rms_mm(x2, gain("ffn_norm"), wf["ffn_up"], name="ffn_up")
    f, dx3, loss_acc = convgate_down_loss(a, cw, cb, ffn_down, x2, target)

    by_rows = lambda g: g.reshape(N_CHIPS, g.shape[1] // N_CHIPS, g.shape[2])
    sent = emit("ffn_down", by_rows(mm_tn(f, dx3, name="g_ffn_down", out_dtype=WIRE_DTYPE)))
    dc, gcw = convgate_bwd(a, dx3, ffn_down[None], cw, cb, after=sent)
    da, dx2, dg_ffn = conv_transpose_rms_bwd(dc, cw, wf["ffn_up"], x2, gain("ffn_norm"), dx3)
    sent = emit("ffn_up", mm_tn(h3, da, name="g_ffn_up", out_dtype=WIRE_DTYPE, chunks=N_CHIPS))
    dxo = mm_nt(dx2, xa_wo[None], name="d_xo", after=sent)
    sent = emit("xa_wo", by_rows(mm_tn(xo, dx2, name="g_xa_wo", out_dtype=WIRE_DTYPE)))
    dqx, dkn, dvx, dg_xq = xattn_bwd(qx, dxo, kn, vbx, gain("xa_q_norm"), after=sent)
    dx1, dg_xa = mm_nt_rms_bwd(dqx, xa_wq[None], x1, gain("xa_norm"), dx2, name="d_x1")
    sent = emit("xa_wq", by_rows(mm_tn(h2, dqx, name="g_xa_wq", out_dtype=WIRE_DTYPE)))
    dkv, dg_xk = mem_bwd(kv, dkn, dvx, gain("xa_k_norm"), after=sent)
    _, dg_mem = mm_nt_rms_bwd(dkv, wf["xa_wkv"], mem, gain("mem_norm"), jnp.zeros_like(mem), name="d_mem")
    sent = emit("xa_wkv", mm_tn(mn, dkv, name="g_xa_wkv", out_dtype=WIRE_DTYPE, chunks=N_CHIPS))
    dattn, dgm, dg_y = mm_nt_post_bwd(dx1, w_out[None], attn, gm, gain("attn_out_norm"), gain("gmlp_out_norm"),
                                      name="d_mix_out", after=sent)
    sent = emit("w_out", by_rows(mm_tn(y, dx1, name="g_w_out", out_dtype=WIRE_DTYPE)))
    dproj, dsk, dws, dbl, dgq, dgk, dg_gvn = mixer_core_bwd(
        proj, cos128, sin128, gq128, gk128, gain("gmlp_v_norm"), bmat, qr, kr, vb, sinkcol, dattn, dgm, gvn, gu,
        w2, w2t, bsl, after=sent)
    g_in = _fold_cols(mm_tn(h1, dproj, name="g_w_in", out_dtype=F32)[0])
    sent = emit("w_in", g_in.reshape(1024, N_CHIPS, 448).transpose(1, 0, 2).astype(WIRE_DTYPE))
    grad_x, dg_mix = mm_nt_rms_bwd(dproj, w_in_d, x, gain("mix_norm"), dx1, name="d_x", after=sent)
    packed = pack_small(dg_mix, dgq, dgk, dsk, dg_gvn, dg_y, dg_xa, dg_mem, dg_xq, dg_xk, dg_ffn, gcw, dbl, dws)
    return loss_acc, grad_x, packed


def _gather_step(w, chipvec):
    slots = cast_shards([w[n][0] for n in BIG_NAMES], w["ffn_conv"][0], chipvec)
    send_a, recv_a, first, _ = gather_start(slots[:1])
    send_b, recv_b, rest, rest_started = gather_start(slots[1:])
    w_in, = gather_wait(send_a, recv_a, first, rest_started)

    def later(after):
        got = gather_wait(send_b, recv_b, rest, after)
        return dict(zip(BIG_NAMES[1:], got[:-1])), _to_full(got[-1], True)

    return w_in, later


def _reduce_update(started, packed, w, m, v, chipvec, cvec, order):
    small_sent = small_start(packed)
    own = sum_partials(partials_wait([started[n] for n in BIG_NAMES], small_sent[2]), order)
    other = pair_share(own)
    res = [{}, {}, {}, {}]
    for n, g_own, g_other in zip(BIG_NAMES, own, other):
        for d, o in zip(res, adamw_matrix(w[n], m[n], v[n], g_own, g_other, cvec, name="adamw_" + n)):
            d[n] = o
    mevec = (2 * order[0:1] + order[1:2]).astype(jnp.int32)
    small_sum = sum_small(*small_wait(*small_sent, res[3][BIG_NAMES[-1]]), mevec)
    for d, outs in zip(res, adamw_small(small_sum, w, m, v, chipvec)):
        d.update(zip(SMALL, outs))
    return res


def kernel(x, mem, positions, mix_norm, w_in, q_norm, k_norm, attn_sinks, gmlp_v_norm, gmlp_ws, gmlp_bs, attn_out_norm, gmlp_out_norm, w_out, xa_norm, mem_norm, xa_wq, xa_wkv, xa_q_norm, xa_k_norm, xa_wo, ffn_norm, ffn_up, ffn_conv, ffn_conv_b, ffn_down, loss_target, m_mix_norm, m_w_in, m_q_norm, m_k_norm, m_attn_sinks, m_gmlp_v_norm, m_gmlp_ws, m_gmlp_bs, m_attn_out_norm, m_gmlp_out_norm, m_w_out, m_xa_norm, m_mem_norm, m_xa_wq, m_xa_wkv, m_xa_q_norm, m_xa_k_norm, m_xa_wo, m_ffn_norm, m_ffn_up, m_ffn_conv, m_ffn_conv_b, m_ffn_down, v_mix_norm, v_w_in, v_q_norm, v_k_norm, v_attn_sinks, v_gmlp_v_norm, v_gmlp_ws, v_gmlp_bs, v_attn_out_norm, v_gmlp_out_norm, v_w_out, v_xa_norm, v_mem_norm, v_xa_wq, v_xa_wkv, v_xa_q_norm, v_xa_k_norm, v_xa_wo, v_ffn_norm, v_ffn_up, v_ffn_conv, v_ffn_conv_b, v_ffn_down):
    w = dict(mix_norm=mix_norm, w_in=w_in, q_norm=q_norm, k_norm=k_norm, attn_sinks=attn_sinks, gmlp_v_norm=gmlp_v_norm, gmlp_ws=gmlp_ws, gmlp_bs=gmlp_bs, attn_out_norm=attn_out_norm, gmlp_out_norm=gmlp_out_norm, w_out=w_out, xa_norm=xa_norm, mem_norm=mem_norm, xa_wq=xa_wq, xa_wkv=xa_wkv, xa_q_norm=xa_q_norm, xa_k_norm=xa_k_norm, xa_wo=xa_wo, ffn_norm=ffn_norm, ffn_up=ffn_up, ffn_conv=ffn_conv, ffn_conv_b=ffn_conv_b, ffn_down=ffn_down)
    m = dict(mix_norm=m_mix_norm, w_in=m_w_in, q_norm=m_q_norm, k_norm=m_k_norm, attn_sinks=m_attn_sinks, gmlp_v_norm=m_gmlp_v_norm, gmlp_ws=m_gmlp_ws, gmlp_bs=m_gmlp_bs, attn_out_norm=m_attn_out_norm, gmlp_out_norm=m_gmlp_out_norm, w_out=m_w_out, xa_norm=m_xa_norm, mem_norm=m_mem_norm, xa_wq=m_xa_wq, xa_wkv=m_xa_wkv, xa_q_norm=m_xa_q_norm, xa_k_norm=m_xa_k_norm, xa_wo=m_xa_wo, ffn_norm=m_ffn_norm, ffn_up=m_ffn_up, ffn_conv=m_ffn_conv, ffn_conv_b=m_ffn_conv_b, ffn_down=m_ffn_down)
    v = dict(mix_norm=v_mix_norm, w_in=v_w_in, q_norm=v_q_norm, k_norm=v_k_norm, attn_sinks=v_attn_sinks, gmlp_v_norm=v_gmlp_v_norm, gmlp_ws=v_gmlp_ws, gmlp_bs=v_gmlp_bs, attn_out_norm=v_attn_out_norm, gmlp_out_norm=v_gmlp_out_norm, w_out=v_w_out, xa_norm=v_xa_norm, mem_norm=v_mem_norm, xa_wq=v_xa_wq, xa_wkv=v_xa_wkv, xa_q_norm=v_xa_q_norm, xa_k_norm=v_xa_k_norm, xa_wo=v_xa_wo, ffn_norm=v_ffn_norm, ffn_up=v_ffn_up, ffn_conv=v_ffn_conv, ffn_conv_b=v_ffn_conv_b, ffn_down=v_ffn_down)
    ix, iy, ic = lax.axis_index("x"), lax.axis_index("y"), lax.axis_index("c")
    chip = 2 * ix + iy
    chipvec = chip.astype(jnp.int32).reshape(1)
    cvec = ic.astype(jnp.int32).reshape(1)
    order = jnp.stack([chip, ic] + [4 * px + 2 * py + pc for px, py, pc in _peers(ix, iy, ic)]).astype(jnp.int32)

    w_in_all, later = _gather_step(w, chipvec)
    sp = {n: w[n][0] for n in SMALL if n != "ffn_conv"}
    started = {}

    def emit(name, g):
        *started[name], token = partials_start(g, name="partials_start_" + name)
        return token

    loss_acc, grad_x, packed = _local_step(x[0], mem[0], positions[0], loss_target[0], w_in_all, later, sp, emit)
    grads, delta, new_m, new_v = _reduce_update(started, packed, w, m, v, chipvec, cvec, order)
    loss = lax.psum(loss_acc[0, 0], ("x", "y", "c"))
    ordered = lambda d: [d[n] for n in WEIGHTS]
    return (loss, grad_x[None], *ordered(grads), *ordered(delta), *ordered(new_m), *ordered(new_v))
```

```python
import math

import jax
import jax.numpy as jnp
from jax import lax
from jax.experimental import pallas as pl
from jax.experimental.pallas import tpu as pltpu

F32 = jnp.float32
BF16 = jnp.bfloat16
MXU_DTYPE = jnp.bfloat16
WIRE_DTYPE = jnp.bfloat16
EPS = 1e-6
VMEM_LIMIT_V7X = 56 * 1024 * 1024

D_MODEL = 1024
HEAD_DIM = 64
BLK = 128
XA_HEADS = 4
XA_DH = 256
MEM_LEN = 256
D_FF = 2816
IN_COLS_DUP = 2048
N_CHIPS = 4
N_DEV = 8

ADAM_LR = 0.001
ADAM_B1 = 0.9
ADAM_B2 = 0.999
ADAM_EPS = 1e-08
ADAM_WD = 0.01
ADAM_STEP = 10

NT = (((1,), (1,)), ((), ()))
TN = (((0,), (0,)), ((), ()))
NN = (((1,), (0,)), ((), ()))
MINF = float(jnp.finfo(jnp.float32).min)
GELU_K0 = math.sqrt(2.0 / math.pi)
GELU_K1 = 0.044715

BS = pl.BlockSpec
SDS = jax.ShapeDtypeStruct
ANY = pl.BlockSpec(memory_space=pl.ANY)
MESH = pl.DeviceIdType.MESH


def _dot(a, b, dims=NN):
    return lax.dot_general(a.astype(MXU_DTYPE), b.astype(MXU_DTYPE), dims, preferred_element_type=F32)


def _segsum(x, bmat):
    hi = x.astype(BF16)
    lo = (x - hi.astype(F32)).astype(BF16)
    return (jnp.dot(hi, bmat, preferred_element_type=F32) + jnp.dot(lo, bmat, preferred_element_type=F32))


def _gelu(x):
    return 0.5 * x * (1.0 + jnp.tanh(GELU_K0 * (x + GELU_K1 * x * x * x)))


def _gelu_grad(x):
    t = jnp.tanh(GELU_K0 * (x + GELU_K1 * x * x * x))
    return 0.5 * (1.0 + t) + 0.5 * x * (1.0 - t * t) * GELU_K0 * (1.0 + 3.0 * GELU_K1 * x * x)


def _rms(x):
    return lax.rsqrt(jnp.mean(x * x, axis=-1, keepdims=True) + EPS)


def _rms_bwd(dy, x, g, r):
    dyg = dy * g
    dx = r * dyg - x * (r * r * r) * jnp.mean(dyg * x, axis=-1, keepdims=True)
    return dx, dy * x * r


def _pcall(body, *, name, grid, in_specs, out_specs, out_shape, scratch=(), prefetch=0, after=None):
    params = pltpu.CompilerParams(dimension_semantics=("arbitrary",) * len(grid), vmem_limit_bytes=VMEM_LIMIT_V7X)
    in_specs = list(in_specs)
    kernel_fn = body
    if after is not None:
        n_in = prefetch + len(in_specs)
        in_specs.append(ANY)

        def kernel_fn(*refs):
            return body(*refs[:n_in], *refs[n_in + 1:])

    if prefetch:
        spec = pltpu.PrefetchScalarGridSpec(num_scalar_prefetch=prefetch, grid=grid, in_specs=in_specs,
                                            out_specs=out_specs, scratch_shapes=list(scratch))
        call = pl.pallas_call(kernel_fn, name=name, grid_spec=spec, out_shape=out_shape, compiler_params=params)
    else:
        call = pl.pallas_call(kernel_fn, name=name, grid=grid, in_specs=in_specs, out_specs=out_specs,
                              out_shape=out_shape, scratch_shapes=list(scratch), compiler_params=params)
    return call if after is None else (lambda *args: call(*args, after))


def _tile(n, prefs):
    for p in prefs:
        if p <= n and n % p == 0:
            return p
    return n


def _resident(shape):
    return pl.BlockSpec(shape, lambda *_: (0,) * len(shape), pipeline_mode=pl.Buffered(1))


def _acc_rows(ref, row, val):
    ref[row:row + 1, :] += jnp.sum(val, axis=0, keepdims=True)


def rms_mm(x, g, w3, *, name, tm=1024):
    M, K = x.shape
    Q, _, C = w3.shape
    tm = _tile(M, (tm, 256))

    def body(x_ref, g_ref, w_ref, h_ref, o_ref):
        @pl.when(pl.program_id(1) == 0)
        def _():
            xv = x_ref[...]
            h_ref[...] = (xv * _rms(xv) * g_ref[...]).astype(h_ref.dtype)

        o_ref[...] = _dot(h_ref[...], w_ref[pl.program_id(1)])

    return _pcall(body, name=name, grid=(M // tm, Q),
                  in_specs=[BS((tm, K), lambda i, j: (i, 0)), BS((1, K), lambda i, j: (0, 0)),
                            _resident((Q, K, C))],
                  out_specs=[BS((tm, K), lambda i, j: (i, 0)), BS((tm, C), lambda i, j: (i, j))],
                  out_shape=[SDS((M, K), MXU_DTYPE), SDS((M, Q * C), F32)])(x, g, w3)


def mm(a, w, *, name, res):
    M, K = a.shape
    N = w.shape[1]
    tm = _tile(M, (1024, 256))

    def body(a_ref, w_ref, r_ref, o_ref):
        o_ref[...] = _dot(a_ref[...], w_ref[...]) + r_ref[...]

    return _pcall(body, name=name, grid=(M // tm,),
                  in_specs=[BS((tm, K), lambda i: (i, 0)), _resident((K, N)), BS((tm, N), lambda i: (i, 0))],
                  out_specs=BS((tm, N), lambda i: (i, 0)), out_shape=SDS((M, N), F32))(a, w, res)


def _nt_chunks(a_ref, w_ref):
    q_n, _, kc = w_ref.shape
    acc = _dot(a_ref[:, 0:kc], w_ref[0], NT)
    for q in range(1, q_n):
        acc = acc + _dot(a_ref[:, q * kc:(q + 1) * kc], w_ref[q], NT)
    return acc


def mm_nt_rms_bwd(a, w3, x, g, dres, *, name, tm=512, after=None):
    M = a.shape[0]
    Q, N, Kc = w3.shape
    tm = _tile(M, (tm, 256))

    def body(a_ref, w_ref, x_ref, g_ref, dr_ref, dx_ref, dg_ref):
        @pl.when(pl.program_id(0) == 0)
        def _():
            dg_ref[...] = jnp.zeros_like(dg_ref)

        xv = x_ref[...]
        dx, dgc = _rms_bwd(_nt_chunks(a_ref, w_ref), xv, g_ref[...], _rms(xv))
        dx_ref[...] = dr_ref[...] + dx
        _acc_rows(dg_ref, 0, dgc)

    row = BS((tm, N), lambda i: (i, 0))
    return _pcall(body, name=name, grid=(M // tm,), after=after,
                  in_specs=[BS((tm, Q * Kc), lambda i: (i, 0)), _resident((Q, N, Kc)), row,
                            BS((1, N), lambda i: (0, 0)), row],
                  out_specs=[row, BS((8, N), lambda i: (0, 0))],
                  out_shape=[SDS((M, N), F32), SDS((8, N), F32)])(a, w3, x, g, dres)


def mm_nt_post_bwd(a, w3, attn, gm, gao, ggo, *, name, after=None):
    M = a.shape[0]
    Q, N, Kc = w3.shape
    tm = _tile(M, (512, 256))
    hw = N // 2

    def body(a_ref, w_ref, at_ref, gm_ref, gao_ref, ggo_ref, da_ref, dgm_ref, dg_ref):
        @pl.when(pl.program_id(0) == 0)
        def _():
            dg_ref[...] = jnp.zeros_like(dg_ref)

        dy = _nt_chunks(a_ref, w_ref)
        av, gmv = at_ref[...], gm_ref[...]
        da, dga = _rms_bwd(dy[:, :hw], av, gao_ref[...], _rms(av))
        dgm, dgg = _rms_bwd(dy[:, hw:], gmv, ggo_ref[...], _rms(gmv))
        da_ref[...] = da
        dgm_ref[...] = dgm
        dg_ref[0:1, :hw] += jnp.sum(dga, axis=0, keepdims=True)
        dg_ref[0:1, hw:] += jnp.sum(dgg, axis=0, keepdims=True)

    half = BS((tm, hw), lambda i: (i, 0))
    const = lambda r, w: BS((r, w), lambda i: (0, 0))
    return _pcall(body, name=name, grid=(M // tm,), after=after,
                  in_specs=[BS((tm, Q * Kc), lambda i: (i, 0)), _resident((Q, N, Kc)), half, half,
                            const(1, hw), const(1, hw)],
                  out_specs=[half, half, const(8, N)],
                  out_shape=[SDS((M, hw), F32), SDS((M, hw), F32), SDS((8, N), F32)])(a, w3, attn, gm, gao, ggo)


def mm_tn(a, b, *, name, out_dtype, chunks=1, after=None):
    M, K = a.shape
    N = b.shape[1]
    C = N // chunks
    tm = _tile(M, (1024, 256))
    tk = _tile(K, (1408, 1024, 512))
    tn = _tile(C, (1408, 1024, 512))
    per = C // tn
    nm = M // tm

    def body(a_ref, b_ref, o_ref, acc):
        m = pl.program_id(2)

        @pl.when(m == 0)
        def _():
            acc[...] = jnp.zeros_like(acc)

        acc[...] += _dot(a_ref[...], b_ref[...], TN)

        @pl.when(m == nm - 1)
        def _():
            o_ref[...] = acc[...].astype(o_ref.dtype)

    return _pcall(body, name=name, grid=(K // tk, N // tn, nm), after=after,
                  in_specs=[BS((tm, tk), lambda k, n, m: (m, k)), BS((tm, tn), lambda k, n, m: (m, n))],
                  out_specs=BS((None, tk, tn), lambda k, n, m: (n // per, k, n % per)),
                  out_shape=SDS((chunks, K, C), out_dtype), scratch=[pltpu.VMEM((tk, tn), F32)])(a, b)


def _lane(shape):
    return lax.broadcasted_iota(jnp.int32, shape, 1)


def _head_means(slabs, bmat):
    tm = slabs[0].shape[0]
    means = _segsum(jnp.concatenate(slabs, axis=0), bmat) * (1.0 / HEAD_DIM)
    return [means[i * tm:(i + 1) * tm] for i in range(len(slabs))]


def _half_swap(x, first):
    return jnp.where(first, pltpu.roll(x, 96, 1), pltpu.roll(x, 32, 1))


def _by_head(x2, lo):
    z = jnp.zeros((BLK, 128), x2.dtype)
    parts = []
    for s in range(2):
        xs = x2[:, s * 128:(s + 1) * 128]
        parts += [jnp.where(lo, xs, z), jnp.where(lo, z, xs)]
    return jnp.concatenate(parts, axis=0)


def _from_heads(o4, lo):
    return jnp.concatenate([jnp.where(lo, o4[0:BLK], o4[BLK:2 * BLK]),
                            jnp.where(lo, o4[2 * BLK:3 * BLK], o4[3 * BLK:])], axis=1)


def _swa_probs(q2, kd, sink, n, lo):
    qp = _by_head(q2, lo)
    sc = _dot(qp, kd, NT) * (1.0 / math.sqrt(HEAD_DIM))
    r_i = lax.broadcasted_iota(jnp.int32, (4 * BLK, 2 * BLK), 0)
    k_j = lax.broadcasted_iota(jnp.int32, (4 * BLK, 2 * BLK), 1)
    diff = (r_i & (BLK - 1)) + BLK - k_j
    mask = (diff >= 0) & (diff < BLK) & ((k_j >= BLK) | (n > 0))
    sc = jnp.where(mask, sc, MINF)
    m = jnp.maximum(jnp.max(sc, axis=1, keepdims=True), sink)
    p = jnp.exp(sc - m)
    es = jnp.exp(sink - m)
    inv = 1.0 / (jnp.sum(p, axis=1, keepdims=True) + es)
    return qp, p * inv, es * inv


def mixer_core_fwd(proj, cos, sin, gq, gk, gvn, bmat, sinkcol, gao, w2, bsl, ggo):
    S = proj.shape[0]

    def body(p_ref, c_ref, s_ref, gq_ref, gk_ref, gvn_ref, b_ref, sk_ref, gao_ref, w2_ref, bsl_ref, ggo_ref,
             qr_ref, kr_ref, vb_ref, gu_ref, gvo_ref, at_ref, gm_ref, y_ref, k_prev, v_prev):
        n = pl.program_id(0)

        @pl.when(n == 0)
        def _():
            k_prev[...] = jnp.zeros_like(k_prev)
            v_prev[...] = jnp.zeros_like(v_prev)

        cos_v, sin_v, bm = c_ref[...], s_ref[...], b_ref[...]
        first = (_lane((BLK, 128)) & 63) < 32
        lo = _lane((BLK, 128)) < 64
        slabs = [p_ref[:, s * 128:(s + 1) * 128] for s in range(6)]
        for s, (slab, ms) in enumerate(zip(slabs, _head_means([x * x for x in slabs], bm))):
            qn = slab * lax.rsqrt(ms + EPS) * (gq_ref[...] if s < 4 else gk_ref[...])
            out = qn * cos_v + _half_swap(qn, first) * sin_v
            if s < 4:
                qr_ref[:, s * 128:(s + 1) * 128] = out.astype(qr_ref.dtype)
            else:
                kr_ref[:, (s - 4) * 128:(s - 3) * 128] = out.astype(kr_ref.dtype)
        vb_ref[...] = p_ref[:, 768:1024].astype(vb_ref.dtype)
        gu_ref[...] = _gelu(p_ref[:, 1024:1536])
        gv = _gelu(p_ref[:, 1536:2048])
        gvo_ref[...] = (gv * _rms(gv) * gvn_ref[...]).astype(gvo_ref.dtype)

        for h in range(2):
            hs, qs = slice(h * 128, (h + 1) * 128), slice(h * 256, (h + 1) * 256)
            kd = jnp.concatenate([k_prev[:, hs], kr_ref[:, hs]], axis=0)
            vd = jnp.concatenate([v_prev[:, hs], vb_ref[:, hs]], axis=0)
            sink = jnp.concatenate([sk_ref[2 * h], sk_ref[2 * h + 1]], axis=0)
            _, p, _ = _swa_probs(qr_ref[:, qs], kd, sink, n, lo)
            at_ref[:, qs] = _from_heads(_dot(p, vd), lo)
        k_prev[...] = kr_ref[...]
        v_prev[...] = vb_ref[...]

        for j in range(4):
            sl = slice(j * 128, (j + 1) * 128)
            m2 = _dot(w2_ref[j], gvo_ref[:, sl])
            mixed = jnp.where(lo, m2[:BLK], m2[BLK:]) + bsl_ref[j]
            gm_ref[:, sl] = gu_ref[:, sl] * mixed
        a, gm = at_ref[...], gm_ref[...]
        y_ref[:, :512] = (a * _rms(a) * gao_ref[...]).astype(y_ref.dtype)
        y_ref[:, 512:] = (gm * _rms(gm) * ggo_ref[...]).astype(y_ref.dtype)

    row = lambda w: BS((BLK, w), lambda n: (n, 0))
    const = lambda *shape: BS(shape, lambda n: (0,) * len(shape))
    return _pcall(body, name="mixer_core_fwd", grid=(S // BLK,),
                  in_specs=[row(IN_COLS_DUP), row(128), row(128), const(1, 128), const(1, 128), const(1, 512),
                            const(128, 128), const(4, 2 * BLK, 1), const(1, 512), const(4, 2 * BLK, BLK),
                            const(4, BLK, 128), const(1, 512)],
                  out_specs=[row(512), row(256), row(256), row(512), row(512), row(512), row(512), row(1024)],
                  out_shape=[SDS((S, 512), MXU_DTYPE), SDS((S, 256), MXU_DTYPE), SDS((S, 256), MXU_DTYPE),
                             SDS((S, 512), F32), SDS((S, 512), MXU_DTYPE), SDS((S, 512), F32), SDS((S, 512), F32),
                             SDS((S, 1024), MXU_DTYPE)],
                  scratch=[pltpu.VMEM((BLK, 256), MXU_DTYPE), pltpu.VMEM((BLK, 256), MXU_DTYPE)])(
        proj, cos, sin, gq, gk, gvn, bmat, sinkcol, gao, w2, bsl, ggo)


def mem_pre(kv, gxk):
    def body(kv_ref, g_ref, kn_ref, vb_ref):
        for h in range(XA_HEADS):
            sl = slice(h * XA_DH, (h + 1) * XA_DH)
            k = kv_ref[:, sl]
            kn_ref[:, sl] = (k * _rms(k) * g_ref[...]).astype(kn_ref.dtype)
        vb_ref[...] = kv_ref[:, 1024:2048].astype(vb_ref.dtype)

    full = lambda r, w: BS((r, w), lambda i: (0, 0))
    return _pcall(body, name="mem_pre", grid=(1,), in_specs=[full(MEM_LEN, 2048), full(1, XA_DH)],
                  out_specs=[full(MEM_LEN, 1024), full(MEM_LEN, 1024)],
                  out_shape=[SDS((MEM_LEN, 1024), MXU_DTYPE), SDS((MEM_LEN, 1024), MXU_DTYPE)])(kv, gxk)


def _xa_probs(qh, g, kn_h):
    r = _rms(qh)
    qn = qh * r * g
    s = _dot(qn, kn_h, NT) * (1.0 / math.sqrt(XA_DH))
    p = jnp.exp(s - jnp.max(s, axis=1, keepdims=True))
    return r, qn, p * (1.0 / jnp.sum(p, axis=1, keepdims=True))


def xattn_block_fwd(x1, g, wq, kn, vb, gxq, wo):
    S, D = x1.shape
    tm = _tile(S, (512, 256))

    def body(x_ref, g_ref, wq_ref, kn_ref, vb_ref, gxq_ref, wo_ref, h_ref, q_ref, o_ref, x2_ref):
        xv = x_ref[...]
        h_ref[...] = (xv * _rms(xv) * g_ref[...]).astype(h_ref.dtype)
        q_ref[...] = _dot(h_ref[...], wq_ref[...])
        for h in range(XA_HEADS):
            sl = slice(h * XA_DH, (h + 1) * XA_DH)
            _, _, p = _xa_probs(q_ref[:, sl], gxq_ref[...], kn_ref[:, sl])
            o_ref[:, sl] = _dot(p, vb_ref[:, sl]).astype(o_ref.dtype)
        x2_ref[...] = _dot(o_ref[...], wo_ref[...]) + xv

    row = BS((tm, D), lambda i: (i, 0))
    full = lambda r, w: BS((r, w), lambda i: (0, 0))
    return _pcall(body, name="xattn_block_fwd", grid=(S // tm,),
                  in_specs=[row, full(1, D), _resident(wq.shape), full(MEM_LEN, D), full(MEM_LEN, D), full(1, XA_DH),
                            _resident(wo.shape)],
                  out_specs=[row, row, row, row],
                  out_shape=[SDS((S, D), MXU_DTYPE), SDS((S, D), F32), SDS((S, D), MXU_DTYPE), SDS((S, D), F32)])(
        x1, g, wq, kn, vb, gxq, wo)


CONV_COLS = 1408


def _conv_taps(a_ref, halo_ref, w_ref, b_ref, cols, first_tile):
    a = a_ref[:, cols]
    row = lax.broadcasted_iota(jnp.int32, a.shape, 0)
    h6 = jnp.where(first_tile, 0.0, halo_ref[6:7, cols])
    h7 = jnp.where(first_tile, 0.0, halo_ref[7:8, cols])
    a1 = jnp.where(row == 0, h7, pltpu.roll(a, 1, 0))
    a2 = jnp.where(row == 0, h6, jnp.where(row == 1, h7, pltpu.roll(a, 2, 0)))
    c = w_ref[2:3, cols] * a + w_ref[1:2, cols] * a1 + w_ref[0:1, cols] * a2 + b_ref[:, cols]
    return c, (a2, a1, a)


def _conv_specs(tm):
    halo_blocks = tm // 8
    return [BS((tm, D_FF), lambda i: (i, 0)), BS((tm, D_FF), lambda i: (i, 1)),
            BS((8, D_FF), lambda i: (jnp.maximum(i * halo_blocks - 1, 0), 0)),
            BS((8, D_FF), lambda i: (jnp.maximum(i * halo_blocks - 1, 0), 1)),
            BS((3, D_FF), lambda i: (0, 0)), BS((3, D_FF), lambda i: (0, 1)),
            BS((1, D_FF), lambda i: (0, 0)), BS((1, D_FF), lambda i: (0, 1))]


def convgate_down_loss(a, cw, cb, w, res, target):
    S = a.shape[0]
    N = w.shape[1]
    tm = _tile(S, (256,))

    def body(ag_ref, au_ref, hg_ref, hu_ref, wg_ref, wu_ref, bg_ref, bu_ref, w_ref, r_ref, t_ref, f_ref, d_ref,
             l_ref):
        first_tile = pl.program_id(0) == 0

        @pl.when(first_tile)
        def _():
            l_ref[...] = jnp.zeros_like(l_ref)

        for c0 in range(0, D_FF, CONV_COLS):
            cols = slice(c0, c0 + CONV_COLS)
            cg, _ = _conv_taps(ag_ref, hg_ref, wg_ref, bg_ref, cols, first_tile)
            cu, _ = _conv_taps(au_ref, hu_ref, wu_ref, bu_ref, cols, first_tile)
            f_ref[:, cols] = (_gelu(cg) * cu).astype(f_ref.dtype)
        e = _dot(f_ref[...], w_ref[...]) + r_ref[...] - t_ref[...]
        d_ref[...] = e * (1.0 / N)
        l_ref[...] += jnp.sum(e * e) * (0.5 / N)

    row_n = BS((tm, N), lambda i: (i, 0))
    return _pcall(body, name="convgate_down_loss", grid=(S // tm,),
                  in_specs=_conv_specs(tm) + [_resident((D_FF, N)), row_n, row_n],
                  out_specs=[BS((tm, D_FF), lambda i: (i, 0)), row_n, BS((8, 128), lambda i: (0, 0))],
                  out_shape=[SDS((S, D_FF), MXU_DTYPE), SDS((S, N), F32), SDS((8, 128), F32)])(
        a, a, a, a, cw, cw, cb, cb, w, res, target)


def convgate_bwd(a, dx3, w3, cw, cb, after=None):
    S = a.shape[0]
    tm = _tile(S, (256,))

    def body(ag_ref, au_ref, hg_ref, hu_ref, wg_ref, wu_ref, bg_ref, bu_ref, dx_ref, wd_ref, dc_ref, gw_ref, df_ref):
        first_tile = pl.program_id(0) == 0

        @pl.when(first_tile)
        def _():
            gw_ref[...] = jnp.zeros_like(gw_ref)

        df_ref[...] = _nt_chunks(dx_ref, wd_ref)
        for c0 in range(0, D_FF, CONV_COLS):
            cols, ucols = slice(c0, c0 + CONV_COLS), slice(D_FF + c0, D_FF + c0 + CONV_COLS)
            cg, g_taps = _conv_taps(ag_ref, hg_ref, wg_ref, bg_ref, cols, first_tile)
            cu, u_taps = _conv_taps(au_ref, hu_ref, wu_ref, bu_ref, cols, first_tile)
            df_v = df_ref[:, cols]
            dcg = df_v * cu * _gelu_grad(cg)
            dcu = df_v * _gelu(cg)
            dc_ref[:, cols] = dcg
            dc_ref[:, ucols] = dcu
            for col, dcv, taps in ((cols, dcg, g_taps), (ucols, dcu, u_taps)):
                for j in range(3):
                    gw_ref[j:j + 1, col] += jnp.sum(dcv * taps[j], axis=0, keepdims=True)
                gw_ref[3:4, col] += jnp.sum(dcv, axis=0, keepdims=True)

    return _pcall(body, name="convgate_bwd", grid=(S // tm,), after=after,
                  in_specs=_conv_specs(tm) + [BS((tm, dx3.shape[1]), lambda i: (i, 0)), _resident(w3.shape)],
                  out_specs=[BS((tm, 2 * D_FF), lambda i: (i, 0)), BS((8, 2 * D_FF), lambda i: (0, 0))],
                  out_shape=[SDS((S, 2 * D_FF), F32), SDS((8, 2 * D_FF), F32)],
                  scratch=[pltpu.VMEM((tm, D_FF), F32)])(a, a, a, a, cw, cw, cb, cb, dx3, w3)


def conv_transpose_rms_bwd(dc, cw, w3, x, g, dres):
    S, C = dc.shape
    Q, N, Kc = w3.shape
    tm = _tile(S, (256,))
    nt = S // tm
    halo_blocks = tm // 8

    def body(dc_ref, halo_ref, cw_ref, w_ref, x_ref, g_ref, dr_ref, da_ref, dx_ref, dg_ref):
        @pl.when(pl.program_id(0) == 0)
        def _():
            dg_ref[...] = jnp.zeros_like(dg_ref)

        last_tile = pl.program_id(0) == nt - 1
        row = lax.broadcasted_iota(jnp.int32, (tm, CONV_COLS), 0)
        for c0 in range(0, C, CONV_COLS):
            cols = slice(c0, c0 + CONV_COLS)
            h0 = jnp.where(last_tile, 0.0, halo_ref[0:1, cols])
            h1 = jnp.where(last_tile, 0.0, halo_ref[1:2, cols])
            dc_v = dc_ref[:, cols]
            n1 = jnp.where(row == tm - 1, h0, pltpu.roll(dc_v, tm - 1, 0))
            n2 = jnp.where(row == tm - 1, h1, jnp.where(row == tm - 2, h0, pltpu.roll(dc_v, tm - 2, 0)))
            da_ref[:, cols] = (cw_ref[2:3, cols] * dc_v + cw_ref[1:2, cols] * n1
                               + cw_ref[0:1, cols] * n2).astype(da_ref.dtype)
        xv = x_ref[...]
        dx, dgc = _rms_bwd(_nt_chunks(da_ref, w_ref), xv, g_ref[...], _rms(xv))
        dx_ref[...] = dr_ref[...] + dx
        _acc_rows(dg_ref, 0, dgc)

    row_n = BS((tm, N), lambda i: (i, 0))
    return _pcall(body, name="conv_transpose_rms_bwd", grid=(nt,),
                  in_specs=[BS((tm, C), lambda i: (i, 0)),
                            BS((8, C), lambda i: (jnp.minimum((i + 1) * halo_blocks, S // 8 - 1), 0)),
                            BS((3, C), lambda i: (0, 0)), _resident((Q, N, Kc)), row_n, BS((1, N), lambda i: (0, 0)),
                            row_n],
                  out_specs=[BS((tm, C), lambda i: (i, 0)), row_n, BS((8, N), lambda i: (0, 0))],
                  out_shape=[SDS((S, C), MXU_DTYPE), SDS((S, N), F32), SDS((8, N), F32)])(dc, dc, cw, w3, x, g, dres)


def xattn_block_bwd(dx2, wo3, qx, kn, vb, gxq, wq3, x1, g, after=None):
    S, D = qx.shape
    tm = _tile(S, (512, 256))

    def body(dx2_ref, wo_ref, q_ref, kn_ref, vb_ref, gxq_ref, wq_ref, x_ref, g_ref,
             dq_ref, dx_ref, dkn_ref, dv_ref, dgq_ref, dg_ref):
        @pl.when(pl.program_id(0) == 0)
        def _():
            for ref in (dkn_ref, dv_ref, dgq_ref, dg_ref):
                ref[...] = jnp.zeros_like(ref)

        gq = gxq_ref[...]
        do_all = _nt_chunks(dx2_ref, wo_ref)
        for h in range(XA_HEADS):
            sl = slice(h * XA_DH, (h + 1) * XA_DH)
            qh, do = q_ref[:, sl], do_all[:, sl]
            r, qn, p = _xa_probs(qh, gq, kn_ref[:, sl])
            dp = _dot(do, vb_ref[:, sl], NT)
            ds = p * (dp - jnp.sum(dp * p, axis=1, keepdims=True)) * (1.0 / math.sqrt(XA_DH))
            dqn = _dot(ds, kn_ref[:, sl])
            dkn_ref[:, sl] += _dot(ds, qn, TN)
            dv_ref[:, sl] += _dot(p, do, TN)
            dqh, dgc = _rms_bwd(dqn, qh, gq, r)
            dq_ref[:, sl] = dqh.astype(dq_ref.dtype)
            _acc_rows(dgq_ref, 0, dgc)
        xv = x_ref[...]
        dx, dgc = _rms_bwd(_nt_chunks(dq_ref, wq_ref), xv, g_ref[...], _rms(xv))
        dx_ref[...] = dx2_ref[...] + dx
        _acc_rows(dg_ref, 0, dgc)

    row = BS((tm, D), lambda i: (i, 0))
    full = lambda r, w: BS((r, w), lambda i: (0, 0))
    return _pcall(body, name="xattn_block_bwd", grid=(S // tm,), after=after,
                  in_specs=[row, _resident(wo3.shape), row, full(MEM_LEN, D), full(MEM_LEN, D), full(1, XA_DH),
                            _resident(wq3.shape), row, full(1, D)],
                  out_specs=[row, row, full(MEM_LEN, D), full(MEM_LEN, D), full(8, XA_DH), full(8, D)],
                  out_shape=[SDS((S, D), MXU_DTYPE), SDS((S, D), F32), SDS((MEM_LEN, D), F32), SDS((MEM_LEN, D), F32),
                             SDS((8, XA_DH), F32), SDS((8, D), F32)])(dx2, wo3, qx, kn, vb, gxq, wq3, x1, g)


def mem_bwd(kv, dkn, dvb, gxk, after=None):
    def body(kv_ref, dkn_ref, dv_ref, g_ref, dkv_ref, dg_ref):
        dg_ref[...] = jnp.zeros_like(dg_ref)
        for h in range(XA_HEADS):
            sl = slice(h * XA_DH, (h + 1) * XA_DH)
            k = kv_ref[:, sl]
            dk, dgc = _rms_bwd(dkn_ref[:, sl], k, g_ref[...], _rms(k))
            dkv_ref[:, sl] = dk.astype(dkv_ref.dtype)
            _acc_rows(dg_ref, 0, dgc)
        dkv_ref[:, 1024:2048] = dv_ref[...].astype(dkv_ref.dtype)

    full = lambda r, w: BS((r, w), lambda i: (0, 0))
    return _pcall(body, name="mem_bwd", grid=(1,), after=after,
                  in_specs=[full(MEM_LEN, 2048), full(MEM_LEN, 1024), full(MEM_LEN, 1024), full(1, XA_DH)],
                  out_specs=[full(MEM_LEN, 2048), full(8, XA_DH)],
                  out_shape=[SDS((MEM_LEN, 2048), MXU_DTYPE), SDS((8, XA_DH), F32)])(kv, dkn, dvb, gxk)


def _norm_rope_bwd(slabs, douts, g, bm, cos_v, sin_v, first):
    dqns = [d * cos_v + _half_swap(d * sin_v, first) for d in douts]
    rs = [lax.rsqrt(ms + EPS) for ms in _head_means([x * x for x in slabs], bm)]
    projs = _head_means([dqn * g * x for dqn, x in zip(dqns, slabs)], bm)
    dxs = [r * (dqn * g) - x * (r * r * r) * pr for x, dqn, r, pr in zip(slabs, dqns, rs, projs)]
    return dxs, [dqn * x * r for x, dqn, r in zip(slabs, dqns, rs)]


def mixer_core_bwd(proj, cos, sin, gq, gk, gvg, bmat, qr, kr, vb, sinkcol, dattn, dgm, gvn, gu, w2, w2t, bsl,
                   after=None):
    S = qr.shape[0]
    nb = S // BLK

    def body(p_ref, c_ref, s_ref, gq_ref, gk_ref, gvg_ref, b_ref, q_ref, kc_ref, kp_ref, vc_ref, vp_ref, sk_ref,
             do_ref, dgm_ref, gvn_ref, gu_ref, w2_ref, w2t_ref, bsl_ref,
             dp_ref, dsk_ref, dws_ref, dbl_ref, dgq_ref, dgk_ref, dgv_ref,
             carry_k, carry_v, done_k, done_v, dq_keep, dgu_keep, dgvn_keep):
        n = pl.program_id(0)

        @pl.when(n == 0)
        def _():
            for ref in (dsk_ref, dws_ref, dbl_ref, dgq_ref, dgk_ref, dgv_ref, carry_k, carry_v, dq_keep, dgu_keep,
                        dgvn_keep):
                ref[...] = jnp.zeros_like(ref)

        live = (n < nb).astype(F32)
        cos_v, sin_v, bm = c_ref[...], s_ref[...], b_ref[...]
        first = (_lane((BLK, 128)) & 63) < 32
        lo = _lane((BLK, 128)) < 64

        dxs, dgs = _norm_rope_bwd([p_ref[:, s * 128:(s + 1) * 128] for s in range(4)],
                                  [dq_keep[:, s * 128:(s + 1) * 128] for s in range(4)], gq_ref[...], bm,
                                  cos_v, sin_v, first)
        for s, (dx, dg) in enumerate(zip(dxs, dgs)):
            dp_ref[:, s * 128:(s + 1) * 128] = dx.astype(dp_ref.dtype)
            _acc_rows(dgq_ref, 0, dg)
        dp_ref[:, 1024:1536] = (dgu_keep[...] * _gelu_grad(p_ref[:, 1024:1536])).astype(dp_ref.dtype)
        gvp = p_ref[:, 1536:2048]
        gv = _gelu(gvp)
        dgv, dgc = _rms_bwd(dgvn_keep[...], gv, gvg_ref[...], _rms(gv))
        dp_ref[:, 1536:2048] = (dgv * _gelu_grad(gvp)).astype(dp_ref.dtype)
        _acc_rows(dgv_ref, 0, dgc)

        for h in range(2):
            hs, qs = slice(h * 128, (h + 1) * 128), slice(h * 256, (h + 1) * 256)
            kd = jnp.concatenate([kp_ref[:, hs], kc_ref[:, hs]], axis=0)
            vd = jnp.concatenate([vp_ref[:, hs], vc_ref[:, hs]], axis=0)
            sink = jnp.concatenate([sk_ref[2 * h], sk_ref[2 * h + 1]], axis=0)
            qp, p, psink = _swa_probs(q_ref[:, qs], kd, sink, n, lo)
            dop = _by_head(do_ref[:, qs], lo)
            dp = _dot(dop, vd, NT)
            delta = jnp.sum(dp * p, axis=1, keepdims=True)
            ds = p * (dp - delta) * (1.0 / math.sqrt(HEAD_DIM))
            dsink = -psink * delta * live
            dsk_ref[2 * h] += dsink[:2 * BLK]
            dsk_ref[2 * h + 1] += dsink[2 * BLK:]
            dq_keep[:, qs] = _from_heads(_dot(ds, kd), lo)
            dkd = _dot(ds, qp, TN)
            dvd = _dot(p, dop, TN)
            done_k[:, hs] = carry_k[:, hs] + live * dkd[:BLK]
            done_v[:, hs] = carry_v[:, hs] + live * dvd[:BLK]
            carry_k[:, hs] = dkd[BLK:]
            carry_v[:, hs] = dvd[BLK:]
        for j in range(4):
            sl = slice(j * 128, (j + 1) * 128)
            gvn_s = gvn_ref[:, sl]
            m2 = _dot(w2_ref[j], gvn_s)
            mixed = jnp.where(lo, m2[:BLK], m2[BLK:]) + bsl_ref[j]
            dgm_s = dgm_ref[:, sl]
            dgu_keep[:, sl] = dgm_s * mixed
            dmx = dgm_s * gu_ref[:, sl] * live
            d2 = _dot(w2t_ref[j], dmx)
            dgvn_keep[:, sl] = jnp.where(lo, d2[:BLK], d2[BLK:])
            z = jnp.zeros_like(dmx)
            dws_ref[2 * j] += _dot(jnp.where(lo, dmx, z), gvn_s, NT)
            dws_ref[2 * j + 1] += _dot(jnp.where(lo, z, dmx), gvn_s, NT)
            dbl_ref[j] += dmx

        dxs, dgs = _norm_rope_bwd([p_ref[:, 512 + s * 128:640 + s * 128] for s in range(2)],
                                  [done_k[:, s * 128:(s + 1) * 128] for s in range(2)], gk_ref[...], bm,
                                  cos_v, sin_v, first)
        for s, (dx, dg) in enumerate(zip(dxs, dgs)):
            dp_ref[:, 512 + s * 128:640 + s * 128] = dx.astype(dp_ref.dtype)
            _acc_rows(dgk_ref, 0, dg)
        dp_ref[:, 768:1024] = done_v[...].astype(dp_ref.dtype)

    last = nb - 1
    cur = lambda w: BS((BLK, w), lambda n: (jnp.minimum(n, last), 0))
    prev = lambda w: BS((BLK, w), lambda n: (jnp.clip(n - 1, 0, last), 0))
    done = lambda w: BS((BLK, w), lambda n: (jnp.maximum(n - 1, 0), 0))
    const = lambda *shape: BS(shape, lambda n: (0,) * len(shape))
    return _pcall(body, name="mixer_core_bwd", grid=(nb + 1,), after=after,
                  in_specs=[done(IN_COLS_DUP), done(128), done(128), const(1, 128), const(1, 128), const(1, 512),
                            const(128, 128), cur(512), cur(256), prev(256), cur(256), prev(256),
                            const(4, 2 * BLK, 1), cur(512), cur(512), cur(512), cur(512), const(4, 2 * BLK, BLK),
                            const(4, 2 * BLK, BLK), const(4, BLK, 128)],
                  out_specs=[done(IN_COLS_DUP), const(4, 2 * BLK, 1), const(8, BLK, BLK), const(4, BLK, 128),
                             const(8, 128), const(8, 128), const(8, 512)],
                  out_shape=[SDS((S, IN_COLS_DUP), MXU_DTYPE), SDS((4, 2 * BLK, 1), F32), SDS((8, BLK, BLK), F32),
                             SDS((4, BLK, 128), F32), SDS((8, 128), F32), SDS((8, 128), F32), SDS((8, 512), F32)],
                  scratch=[pltpu.VMEM((BLK, 256), F32)] * 4 + [pltpu.VMEM((BLK, 512), F32)] * 3)(
        proj, cos, sin, gq, gk, gvg, bmat, qr, kr, kr, vb, vb, sinkcol, dattn, dgm, gvn, gu, w2, w2t, bsl)


BIG = (("w_in", (1024, 448), True), ("w_out", (256, 1024), False), ("xa_wq", (256, 1024), False),
       ("xa_wkv", (1024, 512), True), ("xa_wo", (256, 1024), False), ("ffn_up", (1024, 1408), True),
       ("ffn_down", (704, 1024), False))
BIG_NAMES = tuple(n for n, _, _ in BIG)
SMALL_VECS = (("mix_norm", 1024), ("q_norm", 64), ("k_norm", 64), ("attn_sinks", 8), ("gmlp_v_norm", 512),
              ("attn_out_norm", 512), ("gmlp_out_norm", 512), ("xa_norm", 1024), ("mem_norm", 1024),
              ("xa_q_norm", 256), ("xa_k_norm", 256), ("ffn_norm", 1024), ("ffn_conv_b", 5632))
SMALL = tuple(n for n, _ in SMALL_VECS) + ("gmlp_bs", "gmlp_ws", "ffn_conv")
WEIGHTS = ("mix_norm", "w_in", "q_norm", "k_norm", "attn_sinks", "gmlp_v_norm", "gmlp_ws", "gmlp_bs",
           "attn_out_norm", "gmlp_out_norm", "w_out", "xa_norm", "mem_norm", "xa_wq", "xa_wkv", "xa_q_norm",
           "xa_k_norm", "xa_wo", "ffn_norm", "ffn_up", "ffn_conv", "ffn_conv_b", "ffn_down")
CONV_SHARD = (3, 1408)
CONV_LANE_ROWS = CONV_SHARD[1] // 128
CONV_CHIP_ROWS = 40


def _small_rows():
    rows, r = {}, 0
    for n, length in SMALL_VECS:
        rows[n] = r
        r += -(-length // 128)
    r += -r % 8
    rows["gmlp_bs"] = r
    r += 8
    rows["gmlp_ws"] = r
    r += 8 * BLK
    rows["ffn_conv"] = r
    r += N_CHIPS * CONV_CHIP_ROWS
    return rows, r


SMALL_ROW, SMALL_ROWS = _small_rows()


def pack_small(dg_mix, dgq, dgk, dsk, dg_gvn, dg_y, dg_xa, dg_mem, dg_xq, dg_xk, dg_ffn, gcw, dbl, dws):
    def body(mix_ref, q_ref, k_ref, sk_ref, gvn_ref, y_ref, xa_ref, mem_ref, xq_ref, xk_ref, ffn_ref, cw_ref,
             dbl_ref, dws_ref, o_ref):
        o_ref[...] = jnp.zeros_like(o_ref)
        lane = _lane((1, 128))

        def put(name, src_ref, row, lane0, length):
            for k in range(length // 128):
                o_ref[SMALL_ROW[name] + k:SMALL_ROW[name] + k + 1, :] = src_ref[row:row + 1, lane0 + k * 128:lane0 + (k + 1) * 128]

        put("mix_norm", mix_ref, 0, 0, 1024)
        for name, ref in (("q_norm", q_ref), ("k_norm", k_ref)):
            v = ref[0:1, :]
            o_ref[SMALL_ROW[name]:SMALL_ROW[name] + 1, :] = jnp.where(lane < HEAD_DIM, v + pltpu.roll(v, 64, 1), 0.0)
        sinks = jnp.zeros((1, 128), F32)
        for s in range(4):
            col = sk_ref[s]
            sinks = sinks + jnp.where(lane == 2 * s, jnp.sum(col[:BLK]), 0.0) + jnp.where(lane == 2 * s + 1, jnp.sum(col[BLK:]), 0.0)
        o_ref[SMALL_ROW["attn_sinks"]:SMALL_ROW["attn_sinks"] + 1, :] = sinks
        put("gmlp_v_norm", gvn_ref, 0, 0, 512)
        put("attn_out_norm", y_ref, 0, 0, 512)
        put("gmlp_out_norm", y_ref, 0, 512, 512)
        put("xa_norm", xa_ref, 0, 0, 1024)
        put("mem_norm", mem_ref, 0, 0, 1024)
        put("xa_q_norm", xq_ref, 0, 0, 256)
        put("xa_k_norm", xk_ref, 0, 0, 256)
        put("ffn_norm", ffn_ref, 0, 0, 1024)
        put("ffn_conv_b", cw_ref, 3, 0, 2 * D_FF)
        r8 = lax.broadcasted_iota(jnp.int32, (8, 128), 0)
        l8 = _lane((8, 128))
        bs = jnp.zeros((8, BLK), F32)
        for j in range(4):
            sel = (((r8 == 2 * j) & (l8 < 64)) | ((r8 == 2 * j + 1) & (l8 >= 64))).astype(F32).astype(BF16)
            xj = dbl_ref[j]
            hi = xj.astype(BF16)
            lo = (xj - hi.astype(F32)).astype(BF16)
            bs = bs + lax.dot_general(sel, hi, NT, preferred_element_type=F32) + lax.dot_general(sel, lo, NT, preferred_element_type=F32)
        o_ref[SMALL_ROW["gmlp_bs"]:SMALL_ROW["gmlp_bs"] + 8, :] = bs
        causal = lax.broadcasted_iota(jnp.int32, (BLK, BLK), 0) >= lax.broadcasted_iota(jnp.int32, (BLK, BLK), 1)
        for h in range(8):
            r0 = SMALL_ROW["gmlp_ws"] + h * BLK
            o_ref[r0:r0 + BLK, :] = jnp.where(causal, dws_ref[h], 0.0)
        for q in range(N_CHIPS):
            for j in range(3):
                for k in range(CONV_LANE_ROWS):
                    r0 = SMALL_ROW["ffn_conv"] + q * CONV_CHIP_ROWS + j * CONV_LANE_ROWS + k
                    l0 = (q * CONV_LANE_ROWS + k) * 128
                    o_ref[r0:r0 + 1, :] = cw_ref[j:j + 1, l0:l0 + 128]

    args = (dg_mix, dgq, dgk, dsk, dg_gvn, dg_y, dg_xa, dg_mem, dg_xq, dg_xk, dg_ffn, gcw, dbl, dws)
    full = lambda a: BS(a.shape, lambda i, nd=a.ndim: (0,) * nd)
    return _pcall(body, name="pack_small", grid=(1,), in_specs=[full(a) for a in args],
                  out_specs=BS((SMALL_ROWS, 128), lambda i: (0, 0)), out_shape=SDS((SMALL_ROWS, 128), F32))(*args)


def _adam(w, g, m, v):
    mn = ADAM_B1 * m + (1.0 - ADAM_B1) * g
    vn = ADAM_B2 * v + (1.0 - ADAM_B2) * (g * g)
    m_hat = mn / (1.0 - ADAM_B1 ** ADAM_STEP)
    v_hat = vn / (1.0 - ADAM_B2 ** ADAM_STEP)
    return -ADAM_LR * (m_hat / (jnp.sqrt(v_hat) + ADAM_EPS) + ADAM_WD * w), mn, vn


def adamw_small(gsum, w, m, v, chipvec):
    n = len(SMALL)

    def body(chip_ref, g_ref, *refs):
        w_refs, m_refs, v_refs = refs[:n], refs[n:2 * n], refs[2 * n:3 * n]
        outs = refs[3 * n:]
        go, do, mo, vo = outs[:n], outs[n:2 * n], outs[2 * n:3 * n], outs[3 * n:]

        def update(i, idx, g):
            d, mn, vn = _adam(w_refs[i][idx], g, m_refs[i][idx], v_refs[i][idx])
            go[i][idx] = g
            do[i][idx] = d
            mo[i][idx] = mn
            vo[i][idx] = vn

        for i, (name, length) in enumerate(SMALL_VECS):
            for k in range(-(-length // 128)):
                wd = min(128, length - k * 128)
                r = SMALL_ROW[name] + k
                update(i, (slice(0, 1), slice(k * 128, k * 128 + wd)), g_ref[r:r + 1, 0:wd])
        i_bs, i_ws, i_cv = len(SMALL_VECS), len(SMALL_VECS) + 1, len(SMALL_VECS) + 2
        update(i_bs, (0,), g_ref[SMALL_ROW["gmlp_bs"]:SMALL_ROW["gmlp_bs"] + 8, :])
        for h in range(8):
            r0 = SMALL_ROW["gmlp_ws"] + h * BLK
            update(i_ws, (0, h), g_ref[r0:r0 + BLK, :])
        mine = g_ref[pl.ds(pl.multiple_of(SMALL_ROW["ffn_conv"] + chip_ref[0] * CONV_CHIP_ROWS, 8), CONV_CHIP_ROWS), :]
        for j in range(3):
            for k in range(CONV_LANE_ROWS):
                r = j * CONV_LANE_ROWS + k
                update(i_cv, (0, slice(j, j + 1), slice(k * 128, (k + 1) * 128)), mine[r:r + 1, :])

    nat = [w[nm] for nm in SMALL]
    full = lambda a: BS(a.shape, lambda i, c, nd=a.ndim: (0,) * nd)
    outs = _pcall(body, name="adamw_small", grid=(1,), prefetch=1,
                  in_specs=[BS((SMALL_ROWS, 128), lambda i, c: (0, 0))] + [full(a) for a in nat] * 3,
                  out_specs=[full(a) for a in nat] * 4, out_shape=[SDS(a.shape, F32) for a in nat] * 4)(
        chipvec, gsum, *nat, *[m[nm] for nm in SMALL], *[v[nm] for nm in SMALL])
    return outs[:n], outs[n:2 * n], outs[2 * n:3 * n], outs[3 * n:]


def adamw_matrix(w, m, v, g_own, g_other, cvec, *, name):
    _, r, c = w.shape
    half = r // 2
    tr = _tile(half, (128, 176))
    T = half // tr

    def body(c_ref, w_ref, m_ref, v_ref, own_ref, oth_ref, g_ref, d_ref, mo_ref, vo_ref):
        g = jnp.where(pl.program_id(0) == c_ref[0], own_ref[...], oth_ref[...])
        d, mn, vn = _adam(w_ref[...], g, m_ref[...], v_ref[...])
        g_ref[...] = g
        d_ref[...] = d
        mo_ref[...] = mn
        vo_ref[...] = vn

    nat = BS((None, tr, c), lambda hf, t, cr: (0, hf * T + t, 0))
    hlf = BS((tr, c), lambda hf, t, cr: (t, 0))
    return _pcall(body, name=name, grid=(2, T), prefetch=1, in_specs=[nat, nat, nat, hlf, hlf], out_specs=[nat] * 4,
                  out_shape=[SDS(w.shape, F32)] * 4)(cvec, w, m, v, g_own, g_other)


def _place():
    return lax.axis_index("x"), lax.axis_index("y"), lax.axis_index("c")


def _other_chips(x, y):
    return [(1 - x, y), (x, 1 - y), (1 - x, 1 - y)]


def _rows_of_core(c, half):
    return pl.ds(pl.multiple_of(c * half, 16), half)


def _rcopy(src, dst, sems, k, to):
    return pltpu.make_async_remote_copy(src_ref=src, dst_ref=dst, send_sem=sems[0].at[k], recv_sem=sems[1].at[k],
                                        device_id=to, device_id_type=MESH)


def _comm_call(body, *, name, out_shape, n_in, n_sems, aliases=None):
    return pl.pallas_call(body, name=name, out_shape=out_shape, in_specs=[ANY] * n_in, out_specs=[ANY] * len(out_shape),
                          scratch_shapes=[pltpu.SemaphoreType.DMA((n_sems,)), pltpu.SemaphoreType.DMA((n_sems,))],
                          input_output_aliases=aliases or {},
                          compiler_params=pltpu.CompilerParams(has_side_effects=True))


def cast_shards(shards, conv, chipvec):
    n = len(shards)

    def body(chip_ref, *refs):
        for i_ref, o_ref in zip(refs[:n + 1], refs[n + 1:]):
            o_ref[...] = i_ref[...].astype(o_ref.dtype)

    in_specs = [BS((s.shape[0] // 4, s.shape[1]), lambda i, p: (i, 0)) for s in shards]
    in_specs.append(BS(conv.shape, lambda i, p: (0, 0)))
    out_specs = [BS((None, s.shape[0] // 4, s.shape[1]), lambda i, p: (p[0], i, 0)) for s in shards]
    out_specs.append(BS((None,) + conv.shape, lambda i, p: (p[0], 0, 0)))
    out_shape = [SDS((N_CHIPS,) + s.shape, MXU_DTYPE) for s in shards] + [SDS((N_CHIPS,) + conv.shape, F32)]
    return _pcall(body, name="cast_shards", grid=(4,), prefetch=1, in_specs=in_specs, out_specs=out_specs,
                  out_shape=out_shape)(chipvec, *shards, conv)


HBM = pl.BlockSpec(memory_space=pltpu.HBM)
SEM = pl.BlockSpec(memory_space=pltpu.SEMAPHORE)
DATAFLOW = pltpu.SideEffectType.DATAFLOW_SIDE_EFFECTING
VMEM_WHOLE = pl.BlockSpec(memory_space=pltpu.VMEM)
TOKEN = jax.ShapeDtypeStruct((8, 128), jnp.float32)


def _gather_copies(bufs, send_sems, recv_sems, outgoing):
    x, y, c = _place()
    p = 2 * x + y
    cps = []
    for i, o in enumerate(bufs):
        for j, (cx, cy) in enumerate(_other_chips(x, y)):
            slot = o.at[p] if outgoing else o.at[2 * cx + cy]
            cps.append(_rcopy(slot, slot, (send_sems, recv_sems), 3 * i + j, (cx, cy, c)))
    return cps


def gather_start(slots):
    n = len(slots)

    def body(*refs):
        send_sems, recv_sems, thru, token = refs[n], refs[n + 1], refs[n + 2:2 * n + 2], refs[2 * n + 2]
        for cp in _gather_copies(thru, send_sems, recv_sems, True):
            cp.start()
        token[...] = jnp.zeros_like(token)

    hbm = [pltpu.with_memory_space_constraint(s, pltpu.HBM) for s in slots]
    outs = pl.pallas_call(
        body, name="gather_start_%d" % n,
        out_shape=[pltpu.SemaphoreType.DMA((3 * n,)), pltpu.SemaphoreType.DMA((3 * n,))]
        + [pltpu.HBM(s.shape, s.dtype) for s in slots] + [TOKEN],
        in_specs=[HBM] * n, out_specs=[SEM, SEM] + [HBM] * n + [VMEM_WHOLE],
        input_output_aliases={i: 2 + i for i in range(n)},
        compiler_params=pltpu.CompilerParams(has_side_effects=DATAFLOW))(*hbm)
    return outs[0], outs[1], outs[2:2 + n], outs[2 + n]


def gather_wait(send_sems, recv_sems, bufs, after):
    n = len(bufs)

    def body(*refs):
        ins, send_ref, recv_ref = refs[:n], refs[n], refs[n + 1]
        for cp in _gather_copies(ins, send_ref, recv_ref, False):
            cp.wait_send()
            cp.wait_recv()

    return pl.pallas_call(
        body, name="gather_wait_%d" % n, out_shape=[pltpu.HBM(s.shape, s.dtype) for s in bufs],
        in_specs=[HBM] * n + [SEM, SEM, ANY], out_specs=[HBM] * n, input_output_aliases={i: i for i in range(n)},
        compiler_params=pltpu.CompilerParams(has_side_effects=DATAFLOW))(*bufs, send_sems, recv_sems, after)


def _peers(x, y, c):
    return [(1 - x if k & 4 else x, 1 - y if k & 2 else y, 1 - c if k & 1 else c) for k in range(1, N_DEV)]


def _partial_copies(g_ref, land_ref, send_sems, recv_sems, outgoing):
    x, y, c = _place()
    half = g_ref.shape[1] // 2
    cps = []
    for k, (px, py, pc) in enumerate(_peers(x, y, c)):
        src = g_ref.at[2 * px + py, _rows_of_core(pc, half)]
        dst = land_ref.at[4 * x + 2 * y + c] if outgoing else land_ref.at[4 * px + 2 * py + pc]
        cps.append(_rcopy(src, dst, (send_sems, recv_sems), k, (px, py, pc)))
    return cps


def partials_start(g, *, name):
    land = lax.empty((N_DEV, g.shape[1] // 2, g.shape[2]), g.dtype)

    def body(g_ref, land_ref, send_sems, recv_sems, g_thru, land_thru, token):
        for cp in _partial_copies(g_thru, land_thru, send_sems, recv_sems, True):
            cp.start()
        token[...] = jnp.zeros_like(token)

    return pl.pallas_call(
        body, name=name,
        out_shape=[pltpu.SemaphoreType.DMA((N_DEV - 1,)), pltpu.SemaphoreType.DMA((N_DEV - 1,)),
                   pltpu.HBM(g.shape, g.dtype), pltpu.HBM(land.shape, land.dtype), TOKEN],
        in_specs=[HBM, HBM], out_specs=[SEM, SEM, HBM, HBM, VMEM_WHOLE], input_output_aliases={0: 2, 1: 3},
        compiler_params=pltpu.CompilerParams(has_side_effects=DATAFLOW))(
        pltpu.with_memory_space_constraint(g, pltpu.HBM), pltpu.with_memory_space_constraint(land, pltpu.HBM))


def partials_wait(started, after):
    n = len(started)

    def body(*refs):
        for i in range(n):
            send_ref, recv_ref, g_ref, land_ref = refs[4 * i:4 * i + 4]
            for cp in _partial_copies(g_ref, land_ref, send_ref, recv_ref, False):
                cp.wait_send()
                cp.wait_recv()

    flat = [a for s in started for a in s]
    bufs = [a for s in started for a in s[2:]]
    outs = pl.pallas_call(
        body, name="partials_wait", out_shape=[pltpu.HBM(b.shape, b.dtype) for b in bufs],
        in_specs=[SEM, SEM, HBM, HBM] * n + [ANY], out_specs=[HBM] * (2 * n),
        input_output_aliases={4 * i + 2 + j: 2 * i + j for i in range(n) for j in range(2)},
        compiler_params=pltpu.CompilerParams(has_side_effects=DATAFLOW))(*flat, after)
    return [(outs[2 * i], outs[2 * i + 1]) for i in range(n)]


def sum_partials(pairs, order):
    n = len(pairs)

    def body(o_ref, *refs):
        j = pl.program_id(0)
        for g_ref, l_ref, f_ref in zip(refs[:n], refs[n:2 * n], refs[2 * n:]):
            @pl.when(j == 0)
            def _():
                f_ref[...] = g_ref[...].astype(F32)

            @pl.when(j > 0)
            def _():
                f_ref[...] += l_ref[...].astype(F32)

    g4 = [g.reshape(g.shape[0], 2, g.shape[1] // 2, g.shape[2]) for g, _ in pairs]
    lands = [l for _, l in pairs]
    return _pcall(body, name="sum_partials", grid=(N_DEV,), prefetch=1,
                  in_specs=[BS((None, None) + g.shape[2:], lambda j, o: (o[0], o[1], 0, 0)) for g in g4]
                  + [BS((None,) + l.shape[1:], lambda j, o: (o[jnp.maximum(j, 1) + 1], 0, 0)) for l in lands],
                  out_specs=[BS(l.shape[1:], lambda j, o: (0, 0)) for l in lands],
                  out_shape=[SDS(l.shape[1:], F32) for l in lands])(order, *g4, *lands)


def pair_share(fs):
    n = len(fs)

    def body(*refs):
        f_refs, o_refs, sems = refs[:n], refs[n:2 * n], refs[2 * n:]
        x, y, c = _place()
        cps = [_rcopy(f, o, sems, i, (x, y, 1 - c)) for i, (f, o) in enumerate(zip(f_refs, o_refs))]
        for cp in cps:
            cp.start()
        for cp in cps:
            cp.wait()

    return _comm_call(body, name="pair_share", n_in=n, n_sems=n, out_shape=[SDS(f.shape, f.dtype) for f in fs])(*fs)


def _small_copies(s_ref, land_ref, send_sems, recv_sems, outgoing):
    x, y, c = _place()
    cps = []
    for k, (px, py, pc) in enumerate(_peers(x, y, c)):
        dst = land_ref.at[4 * x + 2 * y + c] if outgoing else land_ref.at[4 * px + 2 * py + pc]
        cps.append(_rcopy(s_ref, dst, (send_sems, recv_sems), k, (px, py, pc)))
    return cps


def small_start(sm):
    land = lax.empty((N_DEV,) + sm.shape, sm.dtype)

    def body(s_ref, land_ref, send_sems, recv_sems, s_thru, land_thru):
        for cp in _small_copies(s_thru, land_thru, send_sems, recv_sems, True):
            cp.start()

    return pl.pallas_call(
        body, name="small_start",
        out_shape=[pltpu.SemaphoreType.DMA((N_DEV - 1,)), pltpu.SemaphoreType.DMA((N_DEV - 1,)),
                   pltpu.HBM(sm.shape, sm.dtype), pltpu.HBM(land.shape, land.dtype)],
        in_specs=[HBM, HBM], out_specs=[SEM, SEM, HBM, HBM], input_output_aliases={0: 2, 1: 3},
        compiler_params=pltpu.CompilerParams(has_side_effects=DATAFLOW))(
        pltpu.with_memory_space_constraint(sm, pltpu.HBM), pltpu.with_memory_space_constraint(land, pltpu.HBM))


def small_wait(send_sems, recv_sems, sm, land, after):
    def body(send_ref, recv_ref, s_ref, land_ref, after_ref, s_out, land_out):
        for cp in _small_copies(s_ref, land_ref, send_ref, recv_ref, False):
            cp.wait_send()
            cp.wait_recv()

    return pl.pallas_call(
        body, name="small_wait", out_shape=[pltpu.HBM(sm.shape, sm.dtype), pltpu.HBM(land.shape, land.dtype)],
        in_specs=[SEM, SEM, HBM, HBM, ANY], out_specs=[HBM, HBM], input_output_aliases={2: 0, 3: 1},
        compiler_params=pltpu.CompilerParams(has_side_effects=DATAFLOW))(send_sems, recv_sems, sm, land, after)


def sum_small(own, land, mevec):
    n, rows, width = land.shape
    tr = _tile(rows, (184, 8))

    def body(me_ref, own_ref, land_ref, o_ref):
        acc = jnp.zeros((tr, width), F32)
        for s in range(n):
            acc = acc + jnp.where(me_ref[0] == s, own_ref[...], land_ref[s])
        o_ref[...] = acc

    return _pcall(body, name="sum_small", grid=(rows // tr,), prefetch=1,
                  in_specs=[BS((tr, width), lambda i, me: (i, 0)), BS((n, tr, width), lambda i, me: (0, i, 0))],
                  out_specs=BS((tr, width), lambda i, me: (i, 0)), out_shape=SDS((rows, width), F32))(mevec, own, land)


def _to_full(blk, col):
    n, r, c = blk.shape
    return blk.transpose(1, 0, 2).reshape(r, n * c) if col else blk.reshape(n * r, c)


def _dup_cols(w):
    dup = lambda t: jnp.concatenate([t[:, :64], t[:, :64], t[:, 64:], t[:, 64:]], axis=1)
    return jnp.concatenate([w[:, :512], dup(w[:, 512:640]), dup(w[:, 640:768]), w[:, 768:]], axis=1)


def _fold_cols(d):
    fold = lambda t: jnp.concatenate([t[:, 0:64] + t[:, 64:128], t[:, 128:192] + t[:, 192:256]], axis=1)
    return jnp.concatenate([d[:, :512], fold(d[:, 512:768]), fold(d[:, 768:1024]), d[:, 1024:]], axis=1)


def _local_step(x, mem, positions, target, w_in, later, sp, emit):
    gain = lambda n: sp[n].reshape(1, -1)
    half = HEAD_DIM // 2
    inv_freq = 1.0 / (10000.0 ** (jnp.arange(half, dtype=F32) * (2.0 / HEAD_DIM)))
    ang = positions.astype(F32)[:, None] * inv_freq
    cos, sin = jnp.cos(ang), jnp.sin(ang)
    cos128 = jnp.tile(cos, (1, 4))
    sin128 = jnp.concatenate([-sin, sin, -sin, sin], axis=1)
    seg = jnp.arange(128) // HEAD_DIM
    bmat = (seg[:, None] == seg[None, :]).astype(BF16)
    gq128, gk128 = jnp.tile(gain("q_norm"), (1, 2)), jnp.tile(gain("k_norm"), (1, 2))
    sinkcol = jnp.repeat(sp["attn_sinks"].reshape(4, 2), BLK, axis=1).reshape(4, 2 * BLK, 1)
    wsc = sp["gmlp_ws"] * jnp.tril(jnp.ones((BLK, BLK), F32))[None]
    w2 = wsc.reshape(4, 2 * BLK, BLK).astype(MXU_DTYPE)
    w2t = wsc.swapaxes(1, 2).reshape(4, 2 * BLK, BLK).astype(MXU_DTYPE)
    bsl = jnp.repeat(sp["gmlp_bs"].reshape(4, 2, BLK).transpose(0, 2, 1), HEAD_DIM, axis=2)
    cb = sp["ffn_conv_b"].reshape(1, -1)
    w_in_d = _dup_cols(_to_full(w_in, True))[None]

    h1, proj = rms_mm(x, gain("mix_norm"), w_in_d, name="mix_in")
    qr, kr, vb, gu, gvn, attn, gm, y = mixer_core_fwd(proj, cos128, sin128, gq128, gk128, gain("gmlp_v_norm"), bmat,
                                                      sinkcol, gain("attn_out_norm"), w2, bsl, gain("gmlp_out_norm"))
    wf, cw = later(y)
    w_out, xa_wq, xa_wo, ffn_down = (_to_full(wf[n], False) for n in ("w_out", "xa_wq", "xa_wo", "ffn_down"))
    x1 = mm(y, w_out, res=x, name="mix_out")
    mn, kv = rms_mm(mem, gain("mem_norm"), wf["xa_wkv"], name="xa_kv")
    kn, vbx = mem_pre(kv, gain("xa_k_norm"))
    h2, qx, xo, x2 = xattn_block_fwd(x1, gain("xa_norm"), xa_wq, kn, vbx, gain("xa_q_norm"), xa_wo)
    h3, a = rms_mm(x2, gain("ffn_norm"), wf["ffn_up"], name="ffn_up")
    f, dx3, loss_acc = convgate_down_loss(a, cw, cb, ffn_down, x2, target)

    by_rows = lambda g: g.reshape(N_CHIPS, g.shape[1] // N_CHIPS, g.shape[2])
    sent = emit("ffn_down", by_rows(mm_tn(f, dx3, name="g_ffn_down", out_dtype=WIRE_DTYPE)))
    dc, gcw = convgate_bwd(a, dx3, ffn_down[None], cw, cb, after=sent)
    da, dx2, dg_ffn = conv_transpose_rms_bwd(dc, cw, wf["ffn_up"], x2, gain("ffn_norm"), dx3)
    sent = emit("ffn_up", mm_tn(h3, da, name="g_ffn_up", out_dtype=WIRE_DTYPE, chunks=N_CHIPS))
    sent = emit("xa_wo", by_rows(mm_tn(xo, dx2, name="g_xa_wo", out_dtype=WIRE_DTYPE, after=sent)))
    dqx, dx1, dkn, dvx, dg_xq, dg_xa = xattn_block_bwd(dx2, xa_wo[None], qx, kn, vbx, gain("xa_q_norm"), xa_wq[None],
                                                       x1, gain("xa_norm"), after=sent)
    sent = emit("xa_wq", by_rows(mm_tn(h2, dqx, name="g_xa_wq", out_dtype=WIRE_DTYPE)))
    dkv, dg_xk = mem_bwd(kv, dkn, dvx, gain("xa_k_norm"), after=sent)
    _, dg_mem = mm_nt_rms_bwd(dkv, wf["xa_wkv"], mem, gain("mem_norm"), jnp.zeros_like(mem), name="d_mem")
    sent = emit("xa_wkv", mm_tn(mn, dkv, name="g_xa_wkv", out_dtype=WIRE_DTYPE, chunks=N_CHIPS))
    dattn, dgm, dg_y = mm_nt_post_bwd(dx1, w_out[None], attn, gm, gain("attn_out_norm"), gain("gmlp_out_norm"),
                                      name="d_mix_out", after=sent)
    sent = emit("w_out", by_rows(mm_tn(y, dx1, name="g_w_out", out_dtype=WIRE_DTYPE)))
    dproj, dsk, dws, dbl, dgq, dgk, dg_gvn = mixer_core_bwd(
        proj, cos128, sin128, gq128, gk128, gain("gmlp_v_norm"), bmat, qr, kr, vb, sinkcol, dattn, dgm, gvn, gu,
        w2, w2t, bsl, after=sent)
    g_in = _fold_cols(mm_tn(h1, dproj, name="g_w_in", out_dtype=F32)[0])
    sent = emit("w_in", g_in.reshape(1024, N_CHIPS, 448).transpose(1, 0, 2).astype(WIRE_DTYPE))
    grad_x, dg_mix = mm_nt_rms_bwd(dproj, w_in_d, x, gain("mix_norm"), dx1, name="d_x", after=sent)
    packed = pack_small(dg_mix, dgq, dgk, dsk, dg_gvn, dg_y, dg_xa, dg_mem, dg_xq, dg_xk, dg_ffn, gcw, dbl, dws)
    return loss_acc, grad_x, packed


def _gather_step(w, chipvec):
    slots = cast_shards([w[n][0] for n in BIG_NAMES], w["ffn_conv"][0], chipvec)
    send_a, recv_a, first, _ = gather_start(slots[:1])
    send_b, recv_b, rest, rest_started = gather_start(slots[1:])
    w_in, = gather_wait(send_a, recv_a, first, rest_started)

    def later(after):
        got = gather_wait(send_b, recv_b, rest, after)
        return dict(zip(BIG_NAMES[1:], got[:-1])), _to_full(got[-1], True)

    return w_in, later, rest_started


def _reduce_update(started, packed, w, m, v, chipvec, cvec, order):
    small_sent = small_start(packed)
    own = sum_partials(partials_wait([started[n] for n in BIG_NAMES], small_sent[2]), order)
    other = pair_share(own)
    res = [{}, {}, {}, {}]
    for n, g_own, g_other in zip(BIG_NAMES, own, other):
        for d, o in zip(res, adamw_matrix(w[n], m[n], v[n], g_own, g_other, cvec, name="adamw_" + n)):
            d[n] = o
    mevec = (2 * order[0:1] + order[1:2]).astype(jnp.int32)
    small_sum = sum_small(*small_wait(*small_sent, res[3][BIG_NAMES[-1]]), mevec)
    for d, outs in zip(res, adamw_small(small_sum, w, m, v, chipvec)):
        d.update(zip(SMALL, outs))
    return res


def kernel(x, mem, positions, mix_norm, w_in, q_norm, k_norm, attn_sinks, gmlp_v_norm, gmlp_ws, gmlp_bs, attn_out_norm, gmlp_out_norm, w_out, xa_norm, mem_norm, xa_wq, xa_wkv, xa_q_norm, xa_k_norm, xa_wo, ffn_norm, ffn_up, ffn_conv, ffn_conv_b, ffn_down, loss_target, m_mix_norm, m_w_in, m_q_norm, m_k_norm, m_attn_sinks, m_gmlp_v_norm, m_gmlp_ws, m_gmlp_bs, m_attn_out_norm, m_gmlp_out_norm, m_w_out, m_xa_norm, m_mem_norm, m_xa_wq, m_xa_wkv, m_xa_q_norm, m_xa_k_norm, m_xa_wo, m_ffn_norm, m_ffn_up, m_ffn_conv, m_ffn_conv_b, m_ffn_down, v_mix_norm, v_w_in, v_q_norm, v_k_norm, v_attn_sinks, v_gmlp_v_norm, v_gmlp_ws, v_gmlp_bs, v_attn_out_norm, v_gmlp_out_norm, v_w_out, v_xa_norm, v_mem_norm, v_xa_wq, v_xa_wkv, v_xa_q_norm, v_xa_k_norm, v_xa_wo, v_ffn_norm, v_ffn_up, v_ffn_conv, v_ffn_conv_b, v_ffn_down):
    w = dict(mix_norm=mix_norm, w_in=w_in, q_norm=q_norm, k_norm=k_norm, attn_sinks=attn_sinks, gmlp_v_norm=gmlp_v_norm, gmlp_ws=gmlp_ws, gmlp_bs=gmlp_bs, attn_out_norm=attn_out_norm, gmlp_out_norm=gmlp_out_norm, w_out=w_out, xa_norm=xa_norm, mem_norm=mem_norm, xa_wq=xa_wq, xa_wkv=xa_wkv, xa_q_norm=xa_q_norm, xa_k_norm=xa_k_norm, xa_wo=xa_wo, ffn_norm=ffn_norm, ffn_up=ffn_up, ffn_conv=ffn_conv, ffn_conv_b=ffn_conv_b, ffn_down=ffn_down)
    m = dict(mix_norm=m_mix_norm, w_in=m_w_in, q_norm=m_q_norm, k_norm=m_k_norm, attn_sinks=m_attn_sinks, gmlp_v_norm=m_gmlp_v_norm, gmlp_ws=m_gmlp_ws, gmlp_bs=m_gmlp_bs, attn_out_norm=m_attn_out_norm, gmlp_out_norm=m_gmlp_out_norm, w_out=m_w_out, xa_norm=m_xa_norm, mem_norm=m_mem_norm, xa_wq=m_xa_wq, xa_wkv=m_xa_wkv, xa_q_norm=m_xa_q_norm, xa_k_norm=m_xa_k_norm, xa_wo=m_xa_wo, ffn_norm=m_ffn_norm, ffn_up=m_ffn_up, ffn_conv=m_ffn_conv, ffn_conv_b=m_ffn_conv_b, ffn_down=m_ffn_down)
    v = dict(mix_norm=v_mix_norm, w_in=v_w_in, q_norm=v_q_norm, k_norm=v_k_norm, attn_sinks=v_attn_sinks, gmlp_v_norm=v_gmlp_v_norm, gmlp_ws=v_gmlp_ws, gmlp_bs=v_gmlp_bs, attn_out_norm=v_attn_out_norm, gmlp_out_norm=v_gmlp_out_norm, w_out=v_w_out, xa_norm=v_xa_norm, mem_norm=v_mem_norm, xa_wq=v_xa_wq, xa_wkv=v_xa_wkv, xa_q_norm=v_xa_q_norm, xa_k_norm=v_xa_k_norm, xa_wo=v_xa_wo, ffn_norm=v_ffn_norm, ffn_up=v_ffn_up, ffn_conv=v_ffn_conv, ffn_conv_b=v_ffn_conv_b, ffn_down=v_ffn_down)
    ix, iy, ic = lax.axis_index("x"), lax.axis_index("y"), lax.axis_index("c")
    chip = 2 * ix + iy
    chipvec = chip.astype(jnp.int32).reshape(1)
    cvec = ic.astype(jnp.int32).reshape(1)
    order = jnp.stack([chip, ic] + [4 * px + 2 * py + pc for px, py, pc in _peers(ix, iy, ic)]).astype(jnp.int32)

    w_in_all, later, token = _gather_step(w, chipvec)
    zero = token[0, 0]
    sp = {n: w[n][0] + zero for n in SMALL if n != "ffn_conv"}
    positions = positions + zero.astype(jnp.int32)
    started = {}

    def emit(name, g):
        *started[name], token = partials_start(g, name="partials_start_" + name)
        return token

    loss_acc, grad_x, packed = _local_step(x[0], mem[0], positions[0], loss_target[0], w_in_all, later, sp, emit)
    grads, delta, new_m, new_v = _reduce_update(started, packed, w, m, v, chipvec, cvec, order)
    loss = lax.psum(loss_acc[0, 0], ("x", "y", "c"))
    ordered = lambda d: [d[n] for n in WEIGHTS]
    return (loss, grad_x[None], *ordered(grads), *ordered(delta), *ordered(new_m), *ordered(new_v))
```

```python
import math

import jax
import jax.numpy as jnp
from jax import lax
from jax.experimental import pallas as pl
from jax.experimental.pallas import tpu as pltpu

F32 = jnp.float32
BF16 = jnp.bfloat16
MXU_DTYPE = jnp.bfloat16
WIRE_DTYPE = jnp.bfloat16
EPS = 1e-6
VMEM_LIMIT_V7X = 56 * 1024 * 1024

D_MODEL = 1024
HEAD_DIM = 64
BLK = 128
XA_HEADS = 4
XA_DH = 256
MEM_LEN = 256
D_FF = 2816
IN_COLS_DUP = 2048
N_CHIPS = 4
N_DEV = 8

ADAM_LR = 0.001
ADAM_B1 = 0.9
ADAM_B2 = 0.999
ADAM_EPS = 1e-08
ADAM_WD = 0.01
ADAM_STEP = 10

NT = (((1,), (1,)), ((), ()))
TN = (((0,), (0,)), ((), ()))
NN = (((1,), (0,)), ((), ()))
MINF = float(jnp.finfo(jnp.float32).min)
GELU_K0 = math.sqrt(2.0 / math.pi)
GELU_K1 = 0.044715

BS = pl.BlockSpec
SDS = jax.ShapeDtypeStruct
ANY = pl.BlockSpec(memory_space=pl.ANY)
MESH = pl.DeviceIdType.MESH


def _dot(a, b, dims=NN):
    return lax.dot_general(a.astype(MXU_DTYPE), b.astype(MXU_DTYPE), dims, preferred_element_type=F32)


def _segsum(x, bmat):
    hi = x.astype(BF16)
    lo = (x - hi.astype(F32)).astype(BF16)
    return (jnp.dot(hi, bmat, preferred_element_type=F32) + jnp.dot(lo, bmat, preferred_element_type=F32))


def _gelu(x):
    return 0.5 * x * (1.0 + jnp.tanh(GELU_K0 * (x + GELU_K1 * x * x * x)))


def _gelu_grad(x):
    t = jnp.tanh(GELU_K0 * (x + GELU_K1 * x * x * x))
    return 0.5 * (1.0 + t) + 0.5 * x * (1.0 - t * t) * GELU_K0 * (1.0 + 3.0 * GELU_K1 * x * x)


def _rms(x):
    return lax.rsqrt(jnp.mean(x * x, axis=-1, keepdims=True) + EPS)


def _rms_bwd(dy, x, g, r):
    dyg = dy * g
    dx = r * dyg - x * (r * r * r) * jnp.mean(dyg * x, axis=-1, keepdims=True)
    return dx, dy * x * r


def _pcall(body, *, name, grid, in_specs, out_specs, out_shape, scratch=(), prefetch=0, after=None):
    params = pltpu.CompilerParams(dimension_semantics=("arbitrary",) * len(grid), vmem_limit_bytes=VMEM_LIMIT_V7X)
    in_specs = list(in_specs)
    kernel_fn = body
    if after is not None:
        n_in = prefetch + len(in_specs)
        in_specs.append(ANY)

        def kernel_fn(*refs):
            return body(*refs[:n_in], *refs[n_in + 1:])

    if prefetch:
        spec = pltpu.PrefetchScalarGridSpec(num_scalar_prefetch=prefetch, grid=grid, in_specs=in_specs,
                                            out_specs=out_specs, scratch_shapes=list(scratch))
        call = pl.pallas_call(kernel_fn, name=name, grid_spec=spec, out_shape=out_shape, compiler_params=params)
    else:
        call = pl.pallas_call(kernel_fn, name=name, grid=grid, in_specs=in_specs, out_specs=out_specs,
                              out_shape=out_shape, scratch_shapes=list(scratch), compiler_params=params)
    return call if after is None else (lambda *args: call(*args, after))


def _tile(n, prefs):
    for p in prefs:
        if p <= n and n % p == 0:
            return p
    return n


def _resident(shape):
    return pl.BlockSpec(shape, lambda *_: (0,) * len(shape), pipeline_mode=pl.Buffered(1))


def _acc_rows(ref, row, val):
    ref[row:row + 1, :] += jnp.sum(val, axis=0, keepdims=True)


def rms_mm(x, g, w3, *, name, tm=1024):
    M, K = x.shape
    Q, _, C = w3.shape
    tm = _tile(M, (tm, 256))

    def body(x_ref, g_ref, w_ref, h_ref, o_ref):
        @pl.when(pl.program_id(1) == 0)
        def _():
            xv = x_ref[...]
            h_ref[...] = (xv * _rms(xv) * g_ref[...]).astype(h_ref.dtype)

        o_ref[...] = _dot(h_ref[...], w_ref[pl.program_id(1)])

    return _pcall(body, name=name, grid=(M // tm, Q),
                  in_specs=[BS((tm, K), lambda i, j: (i, 0)), BS((1, K), lambda i, j: (0, 0)),
                            _resident((Q, K, C))],
                  out_specs=[BS((tm, K), lambda i, j: (i, 0)), BS((tm, C), lambda i, j: (i, j))],
                  out_shape=[SDS((M, K), MXU_DTYPE), SDS((M, Q * C), F32)])(x, g, w3)


def mm(a, w, *, name, res):
    M, K = a.shape
    N = w.shape[1]
    tm = _tile(M, (1024, 256))

    def body(a_ref, w_ref, r_ref, o_ref):
        o_ref[...] = _dot(a_ref[...], w_ref[...]) + r_ref[...]

    return _pcall(body, name=name, grid=(M // tm,),
                  in_specs=[BS((tm, K), lambda i: (i, 0)), _resident((K, N)), BS((tm, N), lambda i: (i, 0))],
                  out_specs=BS((tm, N), lambda i: (i, 0)), out_shape=SDS((M, N), F32))(a, w, res)


def _nt_chunks(a_ref, w_ref):
    q_n, _, kc = w_ref.shape
    acc = _dot(a_ref[:, 0:kc], w_ref[0], NT)
    for q in range(1, q_n):
        acc = acc + _dot(a_ref[:, q * kc:(q + 1) * kc], w_ref[q], NT)
    return acc


def mm_nt_rms_bwd(a, w3, x, g, dres, *, name, tm=512, after=None):
    M = a.shape[0]
    Q, N, Kc = w3.shape
    tm = _tile(M, (tm, 256))

    def body(a_ref, w_ref, x_ref, g_ref, dr_ref, dx_ref, dg_ref):
        @pl.when(pl.program_id(0) == 0)
        def _():
            dg_ref[...] = jnp.zeros_like(dg_ref)

        xv = x_ref[...]
        dx, dgc = _rms_bwd(_nt_chunks(a_ref, w_ref), xv, g_ref[...], _rms(xv))
        dx_ref[...] = dr_ref[...] + dx
        _acc_rows(dg_ref, 0, dgc)

    row = BS((tm, N), lambda i: (i, 0))
    return _pcall(body, name=name, grid=(M // tm,), after=after,
                  in_specs=[BS((tm, Q * Kc), lambda i: (i, 0)), _resident((Q, N, Kc)), row,
                            BS((1, N), lambda i: (0, 0)), row],
                  out_specs=[row, BS((8, N), lambda i: (0, 0))],
                  out_shape=[SDS((M, N), F32), SDS((8, N), F32)])(a, w3, x, g, dres)


def mm_nt_post_bwd(a, w3, attn, gm, gao, ggo, *, name, after=None):
    M = a.shape[0]
    Q, N, Kc = w3.shape
    tm = _tile(M, (512, 256))
    hw = N // 2

    def body(a_ref, w_ref, at_ref, gm_ref, gao_ref, ggo_ref, da_ref, dgm_ref, dg_ref):
        @pl.when(pl.program_id(0) == 0)
        def _():
            dg_ref[...] = jnp.zeros_like(dg_ref)

        dy = _nt_chunks(a_ref, w_ref)
        av, gmv = at_ref[...], gm_ref[...]
        da, dga = _rms_bwd(dy[:, :hw], av, gao_ref[...], _rms(av))
        dgm, dgg = _rms_bwd(dy[:, hw:], gmv, ggo_ref[...], _rms(gmv))
        da_ref[...] = da
        dgm_ref[...] = dgm
        dg_ref[0:1, :hw] += jnp.sum(dga, axis=0, keepdims=True)
        dg_ref[0:1, hw:] += jnp.sum(dgg, axis=0, keepdims=True)

    half = BS((tm, hw), lambda i: (i, 0))
    const = lambda r, w: BS((r, w), lambda i: (0, 0))
    return _pcall(body, name=name, grid=(M // tm,), after=after,
                  in_specs=[BS((tm, Q * Kc), lambda i: (i, 0)), _resident((Q, N, Kc)), half, half,
                            const(1, hw), const(1, hw)],
                  out_specs=[half, half, const(8, N)],
                  out_shape=[SDS((M, hw), F32), SDS((M, hw), F32), SDS((8, N), F32)])(a, w3, attn, gm, gao, ggo)


def mm_tn(a, b, *, name, out_dtype, chunks=1, after=None):
    M, K = a.shape
    N = b.shape[1]
    C = N // chunks
    tm = _tile(M, (1024, 256))
    tk = _tile(K, (1408, 1024, 512))
    tn = _tile(C, (1408, 1024, 512))
    per = C // tn
    nm = M // tm

    def body(a_ref, b_ref, o_ref, acc):
        m = pl.program_id(2)

        @pl.when(m == 0)
        def _():
            acc[...] = jnp.zeros_like(acc)

        acc[...] += _dot(a_ref[...], b_ref[...], TN)

        @pl.when(m == nm - 1)
        def _():
            o_ref[...] = acc[...].astype(o_ref.dtype)

    return _pcall(body, name=name, grid=(K // tk, N // tn, nm), after=after,
                  in_specs=[BS((tm, tk), lambda k, n, m: (m, k)), BS((tm, tn), lambda k, n, m: (m, n))],
                  out_specs=BS((None, tk, tn), lambda k, n, m: (n // per, k, n % per)),
                  out_shape=SDS((chunks, K, C), out_dtype), scratch=[pltpu.VMEM((tk, tn), F32)])(a, b)


def _lane(shape):
    return lax.broadcasted_iota(jnp.int32, shape, 1)


def _head_means(slabs, bmat):
    tm = slabs[0].shape[0]
    means = _segsum(jnp.concatenate(slabs, axis=0), bmat) * (1.0 / HEAD_DIM)
    return [means[i * tm:(i + 1) * tm] for i in range(len(slabs))]


def _half_swap(x, first):
    return jnp.where(first, pltpu.roll(x, 96, 1), pltpu.roll(x, 32, 1))


def _by_head(x2, lo):
    z = jnp.zeros((BLK, 128), x2.dtype)
    parts = []
    for s in range(2):
        xs = x2[:, s * 128:(s + 1) * 128]
        parts += [jnp.where(lo, xs, z), jnp.where(lo, z, xs)]
    return jnp.concatenate(parts, axis=0)


def _from_heads(o4, lo):
    return jnp.concatenate([jnp.where(lo, o4[0:BLK], o4[BLK:2 * BLK]),
                            jnp.where(lo, o4[2 * BLK:3 * BLK], o4[3 * BLK:])], axis=1)


def _swa_probs(q2, kd, sink, n, lo):
    qp = _by_head(q2, lo)
    sc = _dot(qp, kd, NT) * (1.0 / math.sqrt(HEAD_DIM))
    r_i = lax.broadcasted_iota(jnp.int32, (4 * BLK, 2 * BLK), 0)
    k_j = lax.broadcasted_iota(jnp.int32, (4 * BLK, 2 * BLK), 1)
    diff = (r_i & (BLK - 1)) + BLK - k_j
    mask = (diff >= 0) & (diff < BLK) & ((k_j >= BLK) | (n > 0))
    sc = jnp.where(mask, sc, MINF)
    m = jnp.maximum(jnp.max(sc, axis=1, keepdims=True), sink)
    p = jnp.exp(sc - m)
    es = jnp.exp(sink - m)
    inv = 1.0 / (jnp.sum(p, axis=1, keepdims=True) + es)
    return qp, p * inv, es * inv


def mixer_core_fwd(proj, cos, sin, gq, gk, gvn, bmat, sinkcol, gao, w2, bsl, ggo):
    S = proj.shape[0]

    def body(p_ref, c_ref, s_ref, gq_ref, gk_ref, gvn_ref, b_ref, sk_ref, gao_ref, w2_ref, bsl_ref, ggo_ref,
             qr_ref, kr_ref, vb_ref, gu_ref, gvo_ref, at_ref, gm_ref, y_ref, k_prev, v_prev):
        n = pl.program_id(0)

        @pl.when(n == 0)
        def _():
            k_prev[...] = jnp.zeros_like(k_prev)
            v_prev[...] = jnp.zeros_like(v_prev)

        cos_v, sin_v, bm = c_ref[...], s_ref[...], b_ref[...]
        first = (_lane((BLK, 128)) & 63) < 32
        lo = _lane((BLK, 128)) < 64
        slabs = [p_ref[:, s * 128:(s + 1) * 128] for s in range(6)]
        for s, (slab, ms) in enumerate(zip(slabs, _head_means([x * x for x in slabs], bm))):
            qn = slab * lax.rsqrt(ms + EPS) * (gq_ref[...] if s < 4 else gk_ref[...])
            out = qn * cos_v + _half_swap(qn, first) * sin_v
            if s < 4:
                qr_ref[:, s * 128:(s + 1) * 128] = out.astype(qr_ref.dtype)
            else:
                kr_ref[:, (s - 4) * 128:(s - 3) * 128] = out.astype(kr_ref.dtype)
        vb_ref[...] = p_ref[:, 768:1024].astype(vb_ref.dtype)
        gu_ref[...] = _gelu(p_ref[:, 1024:1536])
        gv = _gelu(p_ref[:, 1536:2048])
        gvo_ref[...] = (gv * _rms(gv) * gvn_ref[...]).astype(gvo_ref.dtype)

        for h in range(2):
            hs, qs = slice(h * 128, (h + 1) * 128), slice(h * 256, (h + 1) * 256)
            kd = jnp.concatenate([k_prev[:, hs], kr_ref[:, hs]], axis=0)
            vd = jnp.concatenate([v_prev[:, hs], vb_ref[:, hs]], axis=0)
            sink = jnp.concatenate([sk_ref[2 * h], sk_ref[2 * h + 1]], axis=0)
            _, p, _ = _swa_probs(qr_ref[:, qs], kd, sink, n, lo)
            at_ref[:, qs] = _from_heads(_dot(p, vd), lo)
        k_prev[...] = kr_ref[...]
        v_prev[...] = vb_ref[...]

        for j in range(4):
            sl = slice(j * 128, (j + 1) * 128)
            m2 = _dot(w2_ref[j], gvo_ref[:, sl])
            mixed = jnp.where(lo, m2[:BLK], m2[BLK:]) + bsl_ref[j]
            gm_ref[:, sl] = gu_ref[:, sl] * mixed
        a, gm = at_ref[...], gm_ref[...]
        y_ref[:, :512] = (a * _rms(a) * gao_ref[...]).astype(y_ref.dtype)
        y_ref[:, 512:] = (gm * _rms(gm) * ggo_ref[...]).astype(y_ref.dtype)

    row = lambda w: BS((BLK, w), lambda n: (n, 0))
    const = lambda *shape: BS(shape, lambda n: (0,) * len(shape))
    return _pcall(body, name="mixer_core_fwd", grid=(S // BLK,),
                  in_specs=[row(IN_COLS_DUP), row(128), row(128), const(1, 128), const(1, 128), const(1, 512),
                            const(128, 128), const(4, 2 * BLK, 1), const(1, 512), const(4, 2 * BLK, BLK),
                            const(4, BLK, 128), const(1, 512)],
                  out_specs=[row(512), row(256), row(256), row(512), row(512), row(512), row(512), row(1024)],
                  out_shape=[SDS((S, 512), MXU_DTYPE), SDS((S, 256), MXU_DTYPE), SDS((S, 256), MXU_DTYPE),
                             SDS((S, 512), F32), SDS((S, 512), MXU_DTYPE), SDS((S, 512), F32), SDS((S, 512), F32),
                             SDS((S, 1024), MXU_DTYPE)],
                  scratch=[pltpu.VMEM((BLK, 256), MXU_DTYPE), pltpu.VMEM((BLK, 256), MXU_DTYPE)])(
        proj, cos, sin, gq, gk, gvn, bmat, sinkcol, gao, w2, bsl, ggo)


def mem_pre(kv, gxk):
    def body(kv_ref, g_ref, kn_ref, vb_ref):
        for h in range(XA_HEADS):
            sl = slice(h * XA_DH, (h + 1) * XA_DH)
            k = kv_ref[:, sl]
            kn_ref[:, sl] = (k * _rms(k) * g_ref[...]).astype(kn_ref.dtype)
        vb_ref[...] = kv_ref[:, 1024:2048].astype(vb_ref.dtype)

    full = lambda r, w: BS((r, w), lambda i: (0, 0))
    return _pcall(body, name="mem_pre", grid=(1,), in_specs=[full(MEM_LEN, 2048), full(1, XA_DH)],
                  out_specs=[full(MEM_LEN, 1024), full(MEM_LEN, 1024)],
                  out_shape=[SDS((MEM_LEN, 1024), MXU_DTYPE), SDS((MEM_LEN, 1024), MXU_DTYPE)])(kv, gxk)


def _xa_probs(qh, g, kn_h):
    r = _rms(qh)
    qn = qh * r * g
    s = _dot(qn, kn_h, NT) * (1.0 / math.sqrt(XA_DH))
    p = jnp.exp(s - jnp.max(s, axis=1, keepdims=True))
    return r, qn, p * (1.0 / jnp.sum(p, axis=1, keepdims=True))


def xattn_block_fwd(x1, g, wq, kn, vb, gxq, wo):
    S, D = x1.shape
    tm = _tile(S, (512, 256))

    def body(x_ref, g_ref, wq_ref, kn_ref, vb_ref, gxq_ref, wo_ref, h_ref, q_ref, o_ref, x2_ref):
        xv = x_ref[...]
        h_ref[...] = (xv * _rms(xv) * g_ref[...]).astype(h_ref.dtype)
        q_ref[...] = _dot(h_ref[...], wq_ref[...])
        for h in range(XA_HEADS):
            sl = slice(h * XA_DH, (h + 1) * XA_DH)
            _, _, p = _xa_probs(q_ref[:, sl], gxq_ref[...], kn_ref[:, sl])
            o_ref[:, sl] = _dot(p, vb_ref[:, sl]).astype(o_ref.dtype)
        x2_ref[...] = _dot(o_ref[...], wo_ref[...]) + xv

    row = BS((tm, D), lambda i: (i, 0))
    full = lambda r, w: BS((r, w), lambda i: (0, 0))
    return _pcall(body, name="xattn_block_fwd", grid=(S // tm,),
                  in_specs=[row, full(1, D), _resident(wq.shape), full(MEM_LEN, D), full(MEM_LEN, D), full(1, XA_DH),
                            _resident(wo.shape)],
                  out_specs=[row, row, row, row],
                  out_shape=[SDS((S, D), MXU_DTYPE), SDS((S, D), F32), SDS((S, D), MXU_DTYPE), SDS((S, D), F32)])(
        x1, g, wq, kn, vb, gxq, wo)


CONV_COLS = 1408


def _conv_taps(a_ref, halo_ref, w_ref, b_ref, cols, first_tile):
    a = a_ref[:, cols]
    row = lax.broadcasted_iota(jnp.int32, a.shape, 0)
    h6 = jnp.where(first_tile, 0.0, halo_ref[6:7, cols])
    h7 = jnp.where(first_tile, 0.0, halo_ref[7:8, cols])
    a1 = jnp.where(row == 0, h7, pltpu.roll(a, 1, 0))
    a2 = jnp.where(row == 0, h6, jnp.where(row == 1, h7, pltpu.roll(a, 2, 0)))
    c = w_ref[2:3, cols] * a + w_ref[1:2, cols] * a1 + w_ref[0:1, cols] * a2 + b_ref[:, cols]
    return c, (a2, a1, a)


def _conv_specs(tm):
    halo_blocks = tm // 8
    return [BS((tm, D_FF), lambda i: (i, 0)), BS((tm, D_FF), lambda i: (i, 1)),
            BS((8, D_FF), lambda i: (jnp.maximum(i * halo_blocks - 1, 0), 0)),
            BS((8, D_FF), lambda i: (jnp.maximum(i * halo_blocks - 1, 0), 1)),
            BS((3, D_FF), lambda i: (0, 0)), BS((3, D_FF), lambda i: (0, 1)),
            BS((1, D_FF), lambda i: (0, 0)), BS((1, D_FF), lambda i: (0, 1))]


def convgate_down_loss(a, cw, cb, w, res, target):
    S = a.shape[0]
    N = w.shape[1]
    tm = _tile(S, (256,))

    def body(ag_ref, au_ref, hg_ref, hu_ref, wg_ref, wu_ref, bg_ref, bu_ref, w_ref, r_ref, t_ref, f_ref, d_ref,
             l_ref):
        first_tile = pl.program_id(0) == 0

        @pl.when(first_tile)
        def _():
            l_ref[...] = jnp.zeros_like(l_ref)

        for c0 in range(0, D_FF, CONV_COLS):
            cols = slice(c0, c0 + CONV_COLS)
            cg, _ = _conv_taps(ag_ref, hg_ref, wg_ref, bg_ref, cols, first_tile)
            cu, _ = _conv_taps(au_ref, hu_ref, wu_ref, bu_ref, cols, first_tile)
            f_ref[:, cols] = (_gelu(cg) * cu).astype(f_ref.dtype)
        e = _dot(f_ref[...], w_ref[...]) + r_ref[...] - t_ref[...]
        d_ref[...] = e * (1.0 / N)
        l_ref[...] += jnp.sum(e * e) * (0.5 / N)

    row_n = BS((tm, N), lambda i: (i, 0))
    return _pcall(body, name="convgate_down_loss", grid=(S // tm,),
                  in_specs=_conv_specs(tm) + [_resident((D_FF, N)), row_n, row_n],
                  out_specs=[BS((tm, D_FF), lambda i: (i, 0)), row_n, BS((8, 128), lambda i: (0, 0))],
                  out_shape=[SDS((S, D_FF), MXU_DTYPE), SDS((S, N), F32), SDS((8, 128), F32)])(
        a, a, a, a, cw, cw, cb, cb, w, res, target)


def convgate_bwd(a, dx3, w3, cw, cb, after=None):
    S = a.shape[0]
    tm = _tile(S, (256,))

    def body(ag_ref, au_ref, hg_ref, hu_ref, wg_ref, wu_ref, bg_ref, bu_ref, dx_ref, wd_ref, dc_ref, gw_ref, df_ref):
        first_tile = pl.program_id(0) == 0

        @pl.when(first_tile)
        def _():
            gw_ref[...] = jnp.zeros_like(gw_ref)

        df_ref[...] = _nt_chunks(dx_ref, wd_ref)
        for c0 in range(0, D_FF, CONV_COLS):
            cols, ucols = slice(c0, c0 + CONV_COLS), slice(D_FF + c0, D_FF + c0 + CONV_COLS)
            cg, g_taps = _conv_taps(ag_ref, hg_ref, wg_ref, bg_ref, cols, first_tile)
            cu, u_taps = _conv_taps(au_ref, hu_ref, wu_ref, bu_ref, cols, first_tile)
            df_v = df_ref[:, cols]
            dcg = df_v * cu * _gelu_grad(cg)
            dcu = df_v * _gelu(cg)
            dc_ref[:, cols] = dcg
            dc_ref[:, ucols] = dcu
            for col, dcv, taps in ((cols, dcg, g_taps), (ucols, dcu, u_taps)):
                for j in range(3):
                    gw_ref[j:j + 1, col] += jnp.sum(dcv * taps[j], axis=0, keepdims=True)
                gw_ref[3:4, col] += jnp.sum(dcv, axis=0, keepdims=True)

    return _pcall(body, name="convgate_bwd", grid=(S // tm,), after=after,
                  in_specs=_conv_specs(tm) + [BS((tm, dx3.shape[1]), lambda i: (i, 0)), _resident(w3.shape)],
                  out_specs=[BS((tm, 2 * D_FF), lambda i: (i, 0)), BS((8, 2 * D_FF), lambda i: (0, 0))],
                  out_shape=[SDS((S, 2 * D_FF), F32), SDS((8, 2 * D_FF), F32)],
                  scratch=[pltpu.VMEM((tm, D_FF), F32)])(a, a, a, a, cw, cw, cb, cb, dx3, w3)


def conv_transpose_rms_bwd(dc, cw, w3, x, g, dres):
    S, C = dc.shape
    Q, N, Kc = w3.shape
    tm = _tile(S, (256,))
    nt = S // tm
    halo_blocks = tm // 8

    def body(dc_ref, halo_ref, cw_ref, w_ref, x_ref, g_ref, dr_ref, da_ref, dx_ref, dg_ref):
        @pl.when(pl.program_id(0) == 0)
        def _():
            dg_ref[...] = jnp.zeros_like(dg_ref)

        last_tile = pl.program_id(0) == nt - 1
        row = lax.broadcasted_iota(jnp.int32, (tm, CONV_COLS), 0)
        for c0 in range(0, C, CONV_COLS):
            cols = slice(c0, c0 + CONV_COLS)
            h0 = jnp.where(last_tile, 0.0, halo_ref[0:1, cols])
            h1 = jnp.where(last_tile, 0.0, halo_ref[1:2, cols])
            dc_v = dc_ref[:, cols]
            n1 = jnp.where(row == tm - 1, h0, pltpu.roll(dc_v, tm - 1, 0))
            n2 = jnp.where(row == tm - 1, h1, jnp.where(row == tm - 2, h0, pltpu.roll(dc_v, tm - 2, 0)))
            da_ref[:, cols] = (cw_ref[2:3, cols] * dc_v + cw_ref[1:2, cols] * n1
                               + cw_ref[0:1, cols] * n2).astype(da_ref.dtype)
        xv = x_ref[...]
        dx, dgc = _rms_bwd(_nt_chunks(da_ref, w_ref), xv, g_ref[...], _rms(xv))
        dx_ref[...] = dr_ref[...] + dx
        _acc_rows(dg_ref, 0, dgc)

    row_n = BS((tm, N), lambda i: (i, 0))
    return _pcall(body, name="conv_transpose_rms_bwd", grid=(nt,),
                  in_specs=[BS((tm, C), lambda i: (i, 0)),
                            BS((8, C), lambda i: (jnp.minimum((i + 1) * halo_blocks, S // 8 - 1), 0)),
                            BS((3, C), lambda i: (0, 0)), _resident((Q, N, Kc)), row_n, BS((1, N), lambda i: (0, 0)),
                            row_n],
                  out_specs=[BS((tm, C), lambda i: (i, 0)), row_n, BS((8, N), lambda i: (0, 0))],
                  out_shape=[SDS((S, C), MXU_DTYPE), SDS((S, N), F32), SDS((8, N), F32)])(dc, dc, cw, w3, x, g, dres)


def xattn_block_bwd(dx2, wo3, qx, kn, vb, gxq, wq3, x1, g, after=None):
    S, D = qx.shape
    tm = _tile(S, (512, 256))

    def body(dx2_ref, wo_ref, q_ref, kn_ref, vb_ref, gxq_ref, wq_ref, x_ref, g_ref,
             dq_ref, dx_ref, dkn_ref, dv_ref, dgq_ref, dg_ref):
        @pl.when(pl.program_id(0) == 0)
        def _():
            for ref in (dkn_ref, dv_ref, dgq_ref, dg_ref):
                ref[...] = jnp.zeros_like(ref)

        gq = gxq_ref[...]
        do_all = _nt_chunks(dx2_ref, wo_ref)
        for h in range(XA_HEADS):
            sl = slice(h * XA_DH, (h + 1) * XA_DH)
            qh, do = q_ref[:, sl], do_all[:, sl]
            r, qn, p = _xa_probs(qh, gq, kn_ref[:, sl])
            dp = _dot(do, vb_ref[:, sl], NT)
            ds = p * (dp - jnp.sum(dp * p, axis=1, keepdims=True)) * (1.0 / math.sqrt(XA_DH))
            dqn = _dot(ds, kn_ref[:, sl])
            dkn_ref[:, sl] += _dot(ds, qn, TN)
            dv_ref[:, sl] += _dot(p, do, TN)
            dqh, dgc = _rms_bwd(dqn, qh, gq, r)
            dq_ref[:, sl] = dqh.astype(dq_ref.dtype)
            _acc_rows(dgq_ref, 0, dgc)
        xv = x_ref[...]
        dx, dgc = _rms_bwd(_nt_chunks(dq_ref, wq_ref), xv, g_ref[...], _rms(xv))
        dx_ref[...] = dx2_ref[...] + dx
        _acc_rows(dg_ref, 0, dgc)

    row = BS((tm, D), lambda i: (i, 0))
    full = lambda r, w: BS((r, w), lambda i: (0, 0))
    return _pcall(body, name="xattn_block_bwd", grid=(S // tm,), after=after,
                  in_specs=[row, _resident(wo3.shape), row, full(MEM_LEN, D), full(MEM_LEN, D), full(1, XA_DH),
                            _resident(wq3.shape), row, full(1, D)],
                  out_specs=[row, row, full(MEM_LEN, D), full(MEM_LEN, D), full(8, XA_DH), full(8, D)],
                  out_shape=[SDS((S, D), MXU_DTYPE), SDS((S, D), F32), SDS((MEM_LEN, D), F32), SDS((MEM_LEN, D), F32),
                             SDS((8, XA_DH), F32), SDS((8, D), F32)])(dx2, wo3, qx, kn, vb, gxq, wq3, x1, g)


def mem_bwd(kv, dkn, dvb, gxk, after=None):
    def body(kv_ref, dkn_ref, dv_ref, g_ref, dkv_ref, dg_ref):
        dg_ref[...] = jnp.zeros_like(dg_ref)
        for h in range(XA_HEADS):
            sl = slice(h * XA_DH, (h + 1) * XA_DH)
            k = kv_ref[:, sl]
            dk, dgc = _rms_bwd(dkn_ref[:, sl], k, g_ref[...], _rms(k))
            dkv_ref[:, sl] = dk.astype(dkv_ref.dtype)
            _acc_rows(dg_ref, 0, dgc)
        dkv_ref[:, 1024:2048] = dv_ref[...].astype(dkv_ref.dtype)

    full = lambda r, w: BS((r, w), lambda i: (0, 0))
    return _pcall(body, name="mem_bwd", grid=(1,), after=after,
                  in_specs=[full(MEM_LEN, 2048), full(MEM_LEN, 1024), full(MEM_LEN, 1024), full(1, XA_DH)],
                  out_specs=[full(MEM_LEN, 2048), full(8, XA_DH)],
                  out_shape=[SDS((MEM_LEN, 2048), MXU_DTYPE), SDS((8, XA_DH), F32)])(kv, dkn, dvb, gxk)


def _norm_rope_bwd(slabs, douts, g, bm, cos_v, sin_v, first):
    dqns = [d * cos_v + _half_swap(d * sin_v, first) for d in douts]
    rs = [lax.rsqrt(ms + EPS) for ms in _head_means([x * x for x in slabs], bm)]
    projs = _head_means([dqn * g * x for dqn, x in zip(dqns, slabs)], bm)
    dxs = [r * (dqn * g) - x * (r * r * r) * pr for x, dqn, r, pr in zip(slabs, dqns, rs, projs)]
    return dxs, [dqn * x * r for x, dqn, r in zip(slabs, dqns, rs)]


def mixer_core_bwd(proj, cos, sin, gq, gk, gvg, bmat, qr, kr, vb, sinkcol, dattn, dgm, gvn, gu, w2, w2t, bsl,
                   after=None):
    S = qr.shape[0]
    nb = S // BLK

    def body(p_ref, c_ref, s_ref, gq_ref, gk_ref, gvg_ref, b_ref, q_ref, kc_ref, kp_ref, vc_ref, vp_ref, sk_ref,
             do_ref, dgm_ref, gvn_ref, gu_ref, w2_ref, w2t_ref, bsl_ref,
             dp_ref, dsk_ref, dws_ref, dbl_ref, dgq_ref, dgk_ref, dgv_ref,
             carry_k, carry_v, done_k, done_v, dq_keep, dgu_keep, dgvn_keep):
        n = pl.program_id(0)

        @pl.when(n == 0)
        def _():
            for ref in (dsk_ref, dws_ref, dbl_ref, dgq_ref, dgk_ref, dgv_ref, carry_k, carry_v, dq_keep, dgu_keep,
                        dgvn_keep):
                ref[...] = jnp.zeros_like(ref)

        live = (n < nb).astype(F32)
        cos_v, sin_v, bm = c_ref[...], s_ref[...], b_ref[...]
        first = (_lane((BLK, 128)) & 63) < 32
        lo = _lane((BLK, 128)) < 64

        dxs, dgs = _norm_rope_bwd([p_ref[:, s * 128:(s + 1) * 128] for s in range(4)],
                                  [dq_keep[:, s * 128:(s + 1) * 128] for s in range(4)], gq_ref[...], bm,
                                  cos_v, sin_v, first)
        for s, (dx, dg) in enumerate(zip(dxs, dgs)):
            dp_ref[:, s * 128:(s + 1) * 128] = dx.astype(dp_ref.dtype)
            _acc_rows(dgq_ref, 0, dg)
        dp_ref[:, 1024:1536] = (dgu_keep[...] * _gelu_grad(p_ref[:, 1024:1536])).astype(dp_ref.dtype)
        gvp = p_ref[:, 1536:2048]
        gv = _gelu(gvp)
        dgv, dgc = _rms_bwd(dgvn_keep[...], gv, gvg_ref[...], _rms(gv))
        dp_ref[:, 1536:2048] = (dgv * _gelu_grad(gvp)).astype(dp_ref.dtype)
        _acc_rows(dgv_ref, 0, dgc)

        for h in range(2):
            hs, qs = slice(h * 128, (h + 1) * 128), slice(h * 256, (h + 1) * 256)
            kd = jnp.concatenate([kp_ref[:, hs], kc_ref[:, hs]], axis=0)
            vd = jnp.concatenate([vp_ref[:, hs], vc_ref[:, hs]], axis=0)
            sink = jnp.concatenate([sk_ref[2 * h], sk_ref[2 * h + 1]], axis=0)
            qp, p, psink = _swa_probs(q_ref[:, qs], kd, sink, n, lo)
            dop = _by_head(do_ref[:, qs], lo)
            dp = _dot(dop, vd, NT)
            delta = jnp.sum(dp * p, axis=1, keepdims=True)
            ds = p * (dp - delta) * (1.0 / math.sqrt(HEAD_DIM))
            dsink = -psink * delta * live
            dsk_ref[2 * h] += dsink[:2 * BLK]
            dsk_ref[2 * h + 1] += dsink[2 * BLK:]
            dq_keep[:, qs] = _from_heads(_dot(ds, kd), lo)
            dkd = _dot(ds, qp, TN)
            dvd = _dot(p, dop, TN)
            done_k[:, hs] = carry_k[:, hs] + live * dkd[:BLK]
            done_v[:, hs] = carry_v[:, hs] + live * dvd[:BLK]
            carry_k[:, hs] = dkd[BLK:]
            carry_v[:, hs] = dvd[BLK:]
        for j in range(4):
            sl = slice(j * 128, (j + 1) * 128)
            gvn_s = gvn_ref[:, sl]
            m2 = _dot(w2_ref[j], gvn_s)
            mixed = jnp.where(lo, m2[:BLK], m2[BLK:]) + bsl_ref[j]
            dgm_s = dgm_ref[:, sl]
            dgu_keep[:, sl] = dgm_s * mixed
            dmx = dgm_s * gu_ref[:, sl] * live
            d2 = _dot(w2t_ref[j], dmx)
            dgvn_keep[:, sl] = jnp.where(lo, d2[:BLK], d2[BLK:])
            z = jnp.zeros_like(dmx)
            dws_ref[2 * j] += _dot(jnp.where(lo, dmx, z), gvn_s, NT)
            dws_ref[2 * j + 1] += _dot(jnp.where(lo, z, dmx), gvn_s, NT)
            dbl_ref[j] += dmx

        dxs, dgs = _norm_rope_bwd([p_ref[:, 512 + s * 128:640 + s * 128] for s in range(2)],
                                  [done_k[:, s * 128:(s + 1) * 128] for s in range(2)], gk_ref[...], bm,
                                  cos_v, sin_v, first)
        for s, (dx, dg) in enumerate(zip(dxs, dgs)):
            dp_ref[:, 512 + s * 128:640 + s * 128] = dx.astype(dp_ref.dtype)
            _acc_rows(dgk_ref, 0, dg)
        dp_ref[:, 768:1024] = done_v[...].astype(dp_ref.dtype)

    last = nb - 1
    cur = lambda w: BS((BLK, w), lambda n: (jnp.minimum(n, last), 0))
    prev = lambda w: BS((BLK, w), lambda n: (jnp.clip(n - 1, 0, last), 0))
    done = lambda w: BS((BLK, w), lambda n: (jnp.maximum(n - 1, 0), 0))
    const = lambda *shape: BS(shape, lambda n: (0,) * len(shape))
    return _pcall(body, name="mixer_core_bwd", grid=(nb + 1,), after=after,
                  in_specs=[done(IN_COLS_DUP), done(128), done(128), const(1, 128), const(1, 128), const(1, 512),
                            const(128, 128), cur(512), cur(256), prev(256), cur(256), prev(256),
                            const(4, 2 * BLK, 1), cur(512), cur(512), cur(512), cur(512), const(4, 2 * BLK, BLK),
                            const(4, 2 * BLK, BLK), const(4, BLK, 128)],
                  out_specs=[done(IN_COLS_DUP), const(4, 2 * BLK, 1), const(8, BLK, BLK), const(4, BLK, 128),
                             const(8, 128), const(8, 128), const(8, 512)],
                  out_shape=[SDS((S, IN_COLS_DUP), MXU_DTYPE), SDS((4, 2 * BLK, 1), F32), SDS((8, BLK, BLK), F32),
                             SDS((4, BLK, 128), F32), SDS((8, 128), F32), SDS((8, 128), F32), SDS((8, 512), F32)],
                  scratch=[pltpu.VMEM((BLK, 256), F32)] * 4 + [pltpu.VMEM((BLK, 512), F32)] * 3)(
        proj, cos, sin, gq, gk, gvg, bmat, qr, kr, kr, vb, vb, sinkcol, dattn, dgm, gvn, gu, w2, w2t, bsl)


BIG = (("w_in", (1024, 448), True), ("w_out", (256, 1024), False), ("xa_wq", (256, 1024), False),
       ("xa_wkv", (1024, 512), True), ("xa_wo", (256, 1024), False), ("ffn_up", (1024, 1408), True),
       ("ffn_down", (704, 1024), False))
BIG_NAMES = tuple(n for n, _, _ in BIG)
SMALL_VECS = (("mix_norm", 1024), ("q_norm", 64), ("k_norm", 64), ("attn_sinks", 8), ("gmlp_v_norm", 512),
              ("attn_out_norm", 512), ("gmlp_out_norm", 512), ("xa_norm", 1024), ("mem_norm", 1024),
              ("xa_q_norm", 256), ("xa_k_norm", 256), ("ffn_norm", 1024), ("ffn_conv_b", 5632))
SMALL = tuple(n for n, _ in SMALL_VECS) + ("gmlp_bs", "gmlp_ws", "ffn_conv")
WEIGHTS = ("mix_norm", "w_in", "q_norm", "k_norm", "attn_sinks", "gmlp_v_norm", "gmlp_ws", "gmlp_bs",
           "attn_out_norm", "gmlp_out_norm", "w_out", "xa_norm", "mem_norm", "xa_wq", "xa_wkv", "xa_q_norm",
           "xa_k_norm", "xa_wo", "ffn_norm", "ffn_up", "ffn_conv", "ffn_conv_b", "ffn_down")
CONV_SHARD = (3, 1408)
CONV_LANE_ROWS = CONV_SHARD[1] // 128
CONV_CHIP_ROWS = 40


def _small_rows():
    rows, r = {}, 0
    for n, length in SMALL_VECS:
        rows[n] = r
        r += -(-length // 128)
    r += -r % 8
    rows["gmlp_bs"] = r
    r += 8
    rows["gmlp_ws"] = r
    r += 8 * BLK
    rows["ffn_conv"] = r
    r += N_CHIPS * CONV_CHIP_ROWS
    return rows, r


SMALL_ROW, SMALL_ROWS = _small_rows()


def pack_small(dg_mix, dgq, dgk, dsk, dg_gvn, dg_y, dg_xa, dg_mem, dg_xq, dg_xk, dg_ffn, gcw, dbl, dws):
    def body(mix_ref, q_ref, k_ref, sk_ref, gvn_ref, y_ref, xa_ref, mem_ref, xq_ref, xk_ref, ffn_ref, cw_ref,
             dbl_ref, dws_ref, o_ref):
        o_ref[...] = jnp.zeros_like(o_ref)
        lane = _lane((1, 128))

        def put(name, src_ref, row, lane0, length):
            for k in range(length // 128):
                o_ref[SMALL_ROW[name] + k:SMALL_ROW[name] + k + 1, :] = src_ref[row:row + 1, lane0 + k * 128:lane0 + (k + 1) * 128]

        put("mix_norm", mix_ref, 0, 0, 1024)
        for name, ref in (("q_norm", q_ref), ("k_norm", k_ref)):
            v = ref[0:1, :]
            o_ref[SMALL_ROW[name]:SMALL_ROW[name] + 1, :] = jnp.where(lane < HEAD_DIM, v + pltpu.roll(v, 64, 1), 0.0)
        sinks = jnp.zeros((1, 128), F32)
        for s in range(4):
            col = sk_ref[s]
            sinks = sinks + jnp.where(lane == 2 * s, jnp.sum(col[:BLK]), 0.0) + jnp.where(lane == 2 * s + 1, jnp.sum(col[BLK:]), 0.0)
        o_ref[SMALL_ROW["attn_sinks"]:SMALL_ROW["attn_sinks"] + 1, :] = sinks
        put("gmlp_v_norm", gvn_ref, 0, 0, 512)
        put("attn_out_norm", y_ref, 0, 0, 512)
        put("gmlp_out_norm", y_ref, 0, 512, 512)
        put("xa_norm", xa_ref, 0, 0, 1024)
        put("mem_norm", mem_ref, 0, 0, 1024)
        put("xa_q_norm", xq_ref, 0, 0, 256)
        put("xa_k_norm", xk_ref, 0, 0, 256)
        put("ffn_norm", ffn_ref, 0, 0, 1024)
        put("ffn_conv_b", cw_ref, 3, 0, 2 * D_FF)
        r8 = lax.broadcasted_iota(jnp.int32, (8, 128), 0)
        l8 = _lane((8, 128))
        bs = jnp.zeros((8, BLK), F32)
        for j in range(4):
            sel = (((r8 == 2 * j) & (l8 < 64)) | ((r8 == 2 * j + 1) & (l8 >= 64))).astype(F32).astype(BF16)
            xj = dbl_ref[j]
            hi = xj.astype(BF16)
            lo = (xj - hi.astype(F32)).astype(BF16)
            bs = bs + lax.dot_general(sel, hi, NT, preferred_element_type=F32) + lax.dot_general(sel, lo, NT, preferred_element_type=F32)
        o_ref[SMALL_ROW["gmlp_bs"]:SMALL_ROW["gmlp_bs"] + 8, :] = bs
        causal = lax.broadcasted_iota(jnp.int32, (BLK, BLK), 0) >= lax.broadcasted_iota(jnp.int32, (BLK, BLK), 1)
        for h in range(8):
            r0 = SMALL_ROW["gmlp_ws"] + h * BLK
            o_ref[r0:r0 + BLK, :] = jnp.where(causal, dws_ref[h], 0.0)
        for q in range(N_CHIPS):
            for j in range(3):
                for k in range(CONV_LANE_ROWS):
                    r0 = SMALL_ROW["ffn_conv"] + q * CONV_CHIP_ROWS + j * CONV_LANE_ROWS + k
                    l0 = (q * CONV_LANE_ROWS + k) * 128
                    o_ref[r0:r0 + 1, :] = cw_ref[j:j + 1, l0:l0 + 128]

    args = (dg_mix, dgq, dgk, dsk, dg_gvn, dg_y, dg_xa, dg_mem, dg_xq, dg_xk, dg_ffn, gcw, dbl, dws)
    full = lambda a: BS(a.shape, lambda i, nd=a.ndim: (0,) * nd)
    return _pcall(body, name="pack_small", grid=(1,), in_specs=[full(a) for a in args],
                  out_specs=BS((SMALL_ROWS, 128), lambda i: (0, 0)), out_shape=SDS((SMALL_ROWS, 128), F32))(*args)


def _adam(w, g, m, v):
    mn = ADAM_B1 * m + (1.0 - ADAM_B1) * g
    vn = ADAM_B2 * v + (1.0 - ADAM_B2) * (g * g)
    m_hat = mn / (1.0 - ADAM_B1 ** ADAM_STEP)
    v_hat = vn / (1.0 - ADAM_B2 ** ADAM_STEP)
    return -ADAM_LR * (m_hat / (jnp.sqrt(v_hat) + ADAM_EPS) + ADAM_WD * w), mn, vn


def adamw_small(gsum, w, m, v, chipvec):
    n = len(SMALL)

    def body(chip_ref, g_ref, *refs):
        w_refs, m_refs, v_refs = refs[:n], refs[n:2 * n], refs[2 * n:3 * n]
        outs = refs[3 * n:]
        go, do, mo, vo = outs[:n], outs[n:2 * n], outs[2 * n:3 * n], outs[3 * n:]

        def update(i, idx, g):
            d, mn, vn = _adam(w_refs[i][idx], g, m_refs[i][idx], v_refs[i][idx])
            go[i][idx] = g
            do[i][idx] = d
            mo[i][idx] = mn
            vo[i][idx] = vn

        for i, (name, length) in enumerate(SMALL_VECS):
            for k in range(-(-length // 128)):
                wd = min(128, length - k * 128)
                r = SMALL_ROW[name] + k
                update(i, (slice(0, 1), slice(k * 128, k * 128 + wd)), g_ref[r:r + 1, 0:wd])
        i_bs, i_ws, i_cv = len(SMALL_VECS), len(SMALL_VECS) + 1, len(SMALL_VECS) + 2
        update(i_bs, (0,), g_ref[SMALL_ROW["gmlp_bs"]:SMALL_ROW["gmlp_bs"] + 8, :])
        for h in range(8):
            r0 = SMALL_ROW["gmlp_ws"] + h * BLK
            update(i_ws, (0, h), g_ref[r0:r0 + BLK, :])
        mine = g_ref[pl.ds(pl.multiple_of(SMALL_ROW["ffn_conv"] + chip_ref[0] * CONV_CHIP_ROWS, 8), CONV_CHIP_ROWS), :]
        for j in range(3):
            for k in range(CONV_LANE_ROWS):
                r = j * CONV_LANE_ROWS + k
                update(i_cv, (0, slice(j, j + 1), slice(k * 128, (k + 1) * 128)), mine[r:r + 1, :])

    nat = [w[nm] for nm in SMALL]
    full = lambda a: BS(a.shape, lambda i, c, nd=a.ndim: (0,) * nd)
    outs = _pcall(body, name="adamw_small", grid=(1,), prefetch=1,
                  in_specs=[BS((SMALL_ROWS, 128), lambda i, c: (0, 0))] + [full(a) for a in nat] * 3,
                  out_specs=[full(a) for a in nat] * 4, out_shape=[SDS(a.shape, F32) for a in nat] * 4)(
        chipvec, gsum, *nat, *[m[nm] for nm in SMALL], *[v[nm] for nm in SMALL])
    return outs[:n], outs[n:2 * n], outs[2 * n:3 * n], outs[3 * n:]


def adamw_matrix(w, m, v, g_own, g_other, cvec, *, name):
    _, r, c = w.shape
    half = r // 2
    tr = _tile(half, (128, 176))
    T = half // tr

    def body(c_ref, w_ref, m_ref, v_ref, own_ref, oth_ref, g_ref, d_ref, mo_ref, vo_ref):
        g = jnp.where(pl.program_id(0) == c_ref[0], own_ref[...], oth_ref[...])
        d, mn, vn = _adam(w_ref[...], g, m_ref[...], v_ref[...])
        g_ref[...] = g
        d_ref[...] = d
        mo_ref[...] = mn
        vo_ref[...] = vn

    nat = BS((None, tr, c), lambda hf, t, cr: (0, hf * T + t, 0))
    hlf = BS((tr, c), lambda hf, t, cr: (t, 0))
    return _pcall(body, name=name, grid=(2, T), prefetch=1, in_specs=[nat, nat, nat, hlf, hlf], out_specs=[nat] * 4,
                  out_shape=[SDS(w.shape, F32)] * 4)(cvec, w, m, v, g_own, g_other)


def _place():
    return lax.axis_index("x"), lax.axis_index("y"), lax.axis_index("c")


def _other_chips(x, y):
    return [(1 - x, y), (x, 1 - y), (1 - x, 1 - y)]


def _rows_of_core(c, half):
    return pl.ds(pl.multiple_of(c * half, 16), half)


def _rcopy(src, dst, sems, k, to):
    return pltpu.make_async_remote_copy(src_ref=src, dst_ref=dst, send_sem=sems[0].at[k], recv_sem=sems[1].at[k],
                                        device_id=to, device_id_type=MESH)


def _comm_call(body, *, name, out_shape, n_in, n_sems, aliases=None):
    return pl.pallas_call(body, name=name, out_shape=out_shape, in_specs=[ANY] * n_in, out_specs=[ANY] * len(out_shape),
                          scratch_shapes=[pltpu.SemaphoreType.DMA((n_sems,)), pltpu.SemaphoreType.DMA((n_sems,))],
                          input_output_aliases=aliases or {},
                          compiler_params=pltpu.CompilerParams(has_side_effects=True))


def cast_shards(shards, conv, chipvec):
    n = len(shards)

    def body(chip_ref, *refs):
        for i_ref, o_ref in zip(refs[:n + 1], refs[n + 1:]):
            o_ref[...] = i_ref[...].astype(o_ref.dtype)

    in_specs = [BS((s.shape[0] // 4, s.shape[1]), lambda i, p: (i, 0)) for s in shards]
    in_specs.append(BS(conv.shape, lambda i, p: (0, 0)))
    out_specs = [BS((None, s.shape[0] // 4, s.shape[1]), lambda i, p: (p[0], i, 0)) for s in shards]
    out_specs.append(BS((None,) + conv.shape, lambda i, p: (p[0], 0, 0)))
    out_shape = [SDS((N_CHIPS,) + s.shape, MXU_DTYPE) for s in shards] + [SDS((N_CHIPS,) + conv.shape, F32)]
    return _pcall(body, name="cast_shards", grid=(4,), prefetch=1, in_specs=in_specs, out_specs=out_specs,
                  out_shape=out_shape)(chipvec, *shards, conv)


HBM = pl.BlockSpec(memory_space=pltpu.HBM)
SEM = pl.BlockSpec(memory_space=pltpu.SEMAPHORE)
DATAFLOW = pltpu.SideEffectType.DATAFLOW_SIDE_EFFECTING
VMEM_WHOLE = pl.BlockSpec(memory_space=pltpu.VMEM)
TOKEN = jax.ShapeDtypeStruct((8, 128), jnp.float32)


def _gather_copies(bufs, send_sems, recv_sems, outgoing):
    x, y, c = _place()
    p = 2 * x + y
    cps = []
    for i, o in enumerate(bufs):
        for j, (cx, cy) in enumerate(_other_chips(x, y)):
            slot = o.at[p] if outgoing else o.at[2 * cx + cy]
            cps.append(_rcopy(slot, slot, (send_sems, recv_sems), 3 * i + j, (cx, cy, c)))
    return cps


def gather_start(slots, after):
    n = len(slots)

    def body(*refs):
        send_sems, recv_sems, thru, token = refs[n + 1], refs[n + 2], refs[n + 3:2 * n + 3], refs[2 * n + 3]
        for cp in _gather_copies(thru, send_sems, recv_sems, True):
            cp.start()
        token[...] = jnp.zeros_like(token)

    hbm = [pltpu.with_memory_space_constraint(s, pltpu.HBM) for s in slots]
    outs = pl.pallas_call(
        body, name="gather_start_%d" % n,
        out_shape=[pltpu.SemaphoreType.DMA((3 * n,)), pltpu.SemaphoreType.DMA((3 * n,))]
        + [pltpu.HBM(s.shape, s.dtype) for s in slots] + [TOKEN],
        in_specs=[HBM] * n + [ANY], out_specs=[SEM, SEM] + [HBM] * n + [VMEM_WHOLE],
        input_output_aliases={i: 2 + i for i in range(n)},
        compiler_params=pltpu.CompilerParams(has_side_effects=DATAFLOW))(*hbm, after)
    return outs[0], outs[1], outs[2:2 + n], outs[2 + n]


def gather_wait(send_sems, recv_sems, bufs, *after):
    n = len(bufs)

    def body(*refs):
        ins, send_ref, recv_ref = refs[:n], refs[n], refs[n + 1]
        for cp in _gather_copies(ins, send_ref, recv_ref, False):
            cp.wait_send()
            cp.wait_recv()

    return pl.pallas_call(
        body, name="gather_wait_%d" % n, out_shape=[pltpu.HBM(s.shape, s.dtype) for s in bufs],
        in_specs=[HBM] * n + [SEM, SEM] + [ANY] * len(after), out_specs=[HBM] * n,
        input_output_aliases={i: i for i in range(n)},
        compiler_params=pltpu.CompilerParams(has_side_effects=DATAFLOW))(*bufs, send_sems, recv_sems, *after)


def _peers(x, y, c):
    return [(1 - x if k & 4 else x, 1 - y if k & 2 else y, 1 - c if k & 1 else c) for k in range(1, N_DEV)]


def _partial_copies(g_ref, land_ref, send_sems, recv_sems, outgoing):
    x, y, c = _place()
    half = g_ref.shape[1] // 2
    cps = []
    for k, (px, py, pc) in enumerate(_peers(x, y, c)):
        src = g_ref.at[2 * px + py, _rows_of_core(pc, half)]
        dst = land_ref.at[4 * x + 2 * y + c] if outgoing else land_ref.at[4 * px + 2 * py + pc]
        cps.append(_rcopy(src, dst, (send_sems, recv_sems), k, (px, py, pc)))
    return cps


def partials_start(g, *, name):
    land = lax.empty((N_DEV, g.shape[1] // 2, g.shape[2]), g.dtype)

    def body(g_ref, land_ref, send_sems, recv_sems, g_thru, land_thru, token):
        for cp in _partial_copies(g_thru, land_thru, send_sems, recv_sems, True):
            cp.start()
        token[...] = jnp.zeros_like(token)

    return pl.pallas_call(
        body, name=name,
        out_shape=[pltpu.SemaphoreType.DMA((N_DEV - 1,)), pltpu.SemaphoreType.DMA((N_DEV - 1,)),
                   pltpu.HBM(g.shape, g.dtype), pltpu.HBM(land.shape, land.dtype), TOKEN],
        in_specs=[HBM, HBM], out_specs=[SEM, SEM, HBM, HBM, VMEM_WHOLE], input_output_aliases={0: 2, 1: 3},
        compiler_params=pltpu.CompilerParams(has_side_effects=DATAFLOW))(
        pltpu.with_memory_space_constraint(g, pltpu.HBM), pltpu.with_memory_space_constraint(land, pltpu.HBM))


def partials_wait(started, after):
    n = len(started)

    def body(*refs):
        for i in range(n):
            send_ref, recv_ref, g_ref, land_ref = refs[4 * i:4 * i + 4]
            for cp in _partial_copies(g_ref, land_ref, send_ref, recv_ref, False):
                cp.wait_send()
                cp.wait_recv()

    flat = [a for s in started for a in s]
    bufs = [a for s in started for a in s[2:]]
    outs = pl.pallas_call(
        body, name="partials_wait", out_shape=[pltpu.HBM(b.shape, b.dtype) for b in bufs],
        in_specs=[SEM, SEM, HBM, HBM] * n + [ANY], out_specs=[HBM] * (2 * n),
        input_output_aliases={4 * i + 2 + j: 2 * i + j for i in range(n) for j in range(2)},
        compiler_params=pltpu.CompilerParams(has_side_effects=DATAFLOW))(*flat, after)
    return [(outs[2 * i], outs[2 * i + 1]) for i in range(n)]


def sum_partials(pairs, order):
    n = len(pairs)

    def body(o_ref, *refs):
        j = pl.program_id(0)
        for g_ref, l_ref, f_ref in zip(refs[:n], refs[n:2 * n], refs[2 * n:]):
            @pl.when(j == 0)
            def _():
                f_ref[...] = g_ref[...].astype(F32)

            @pl.when(j > 0)
            def _():
                f_ref[...] += l_ref[...].astype(F32)

    g4 = [g.reshape(g.shape[0], 2, g.shape[1] // 2, g.shape[2]) for g, _ in pairs]
    lands = [l for _, l in pairs]
    return _pcall(body, name="sum_partials", grid=(N_DEV,), prefetch=1,
                  in_specs=[BS((None, None) + g.shape[2:], lambda j, o: (o[0], o[1], 0, 0)) for g in g4]
                  + [BS((None,) + l.shape[1:], lambda j, o: (o[jnp.maximum(j, 1) + 1], 0, 0)) for l in lands],
                  out_specs=[BS(l.shape[1:], lambda j, o: (0, 0)) for l in lands],
                  out_shape=[SDS(l.shape[1:], F32) for l in lands])(order, *g4, *lands)


def pair_share(fs):
    n = len(fs)

    def body(*refs):
        f_refs, o_refs, sems = refs[:n], refs[n:2 * n], refs[2 * n:]
        x, y, c = _place()
        cps = [_rcopy(f, o, sems, i, (x, y, 1 - c)) for i, (f, o) in enumerate(zip(f_refs, o_refs))]
        for cp in cps:
            cp.start()
        for cp in cps:
            cp.wait()

    return _comm_call(body, name="pair_share", n_in=n, n_sems=n, out_shape=[SDS(f.shape, f.dtype) for f in fs])(*fs)


def _small_copies(s_ref, land_ref, send_sems, recv_sems, outgoing):
    x, y, c = _place()
    cps = []
    for k, (px, py, pc) in enumerate(_peers(x, y, c)):
        dst = land_ref.at[4 * x + 2 * y + c] if outgoing else land_ref.at[4 * px + 2 * py + pc]
        cps.append(_rcopy(s_ref, dst, (send_sems, recv_sems), k, (px, py, pc)))
    return cps


def small_start(sm):
    land = lax.empty((N_DEV,) + sm.shape, sm.dtype)

    def body(s_ref, land_ref, send_sems, recv_sems, s_thru, land_thru):
        for cp in _small_copies(s_thru, land_thru, send_sems, recv_sems, True):
            cp.start()

    return pl.pallas_call(
        body, name="small_start",
        out_shape=[pltpu.SemaphoreType.DMA((N_DEV - 1,)), pltpu.SemaphoreType.DMA((N_DEV - 1,)),
                   pltpu.HBM(sm.shape, sm.dtype), pltpu.HBM(land.shape, land.dtype)],
        in_specs=[HBM, HBM], out_specs=[SEM, SEM, HBM, HBM], input_output_aliases={0: 2, 1: 3},
        compiler_params=pltpu.CompilerParams(has_side_effects=DATAFLOW))(
        pltpu.with_memory_space_constraint(sm, pltpu.HBM), pltpu.with_memory_space_constraint(land, pltpu.HBM))


def small_wait(send_sems, recv_sems, sm, land, after):
    def body(send_ref, recv_ref, s_ref, land_ref, after_ref, s_out, land_out):
        for cp in _small_copies(s_ref, land_ref, send_ref, recv_ref, False):
            cp.wait_send()
            cp.wait_recv()

    return pl.pallas_call(
        body, name="small_wait", out_shape=[pltpu.HBM(sm.shape, sm.dtype), pltpu.HBM(land.shape, land.dtype)],
        in_specs=[SEM, SEM, HBM, HBM, ANY], out_specs=[HBM, HBM], input_output_aliases={2: 0, 3: 1},
        compiler_params=pltpu.CompilerParams(has_side_effects=DATAFLOW))(send_sems, recv_sems, sm, land, after)


def sum_small(own, land, mevec):
    n, rows, width = land.shape
    tr = _tile(rows, (184, 8))

    def body(me_ref, own_ref, land_ref, o_ref):
        acc = jnp.zeros((tr, width), F32)
        for s in range(n):
            acc = acc + jnp.where(me_ref[0] == s, own_ref[...], land_ref[s])
        o_ref[...] = acc

    return _pcall(body, name="sum_small", grid=(rows // tr,), prefetch=1,
                  in_specs=[BS((tr, width), lambda i, me: (i, 0)), BS((n, tr, width), lambda i, me: (0, i, 0))],
                  out_specs=BS((tr, width), lambda i, me: (i, 0)), out_shape=SDS((rows, width), F32))(mevec, own, land)


def _to_full(blk, col):
    n, r, c = blk.shape
    return blk.transpose(1, 0, 2).reshape(r, n * c) if col else blk.reshape(n * r, c)


def _dup_cols(w):
    dup = lambda t: jnp.concatenate([t[:, :64], t[:, :64], t[:, 64:], t[:, 64:]], axis=1)
    return jnp.concatenate([w[:, :512], dup(w[:, 512:640]), dup(w[:, 640:768]), w[:, 768:]], axis=1)


def _fold_cols(d):
    fold = lambda t: jnp.concatenate([t[:, 0:64] + t[:, 64:128], t[:, 128:192] + t[:, 192:256]], axis=1)
    return jnp.concatenate([d[:, :512], fold(d[:, 512:768]), fold(d[:, 768:1024]), d[:, 1024:]], axis=1)


def _local_step(x, mem, positions, target, w_in, later, sp, emit):
    gain = lambda n: sp[n].reshape(1, -1)
    half = HEAD_DIM // 2
    inv_freq = 1.0 / (10000.0 ** (jnp.arange(half, dtype=F32) * (2.0 / HEAD_DIM)))
    ang = positions.astype(F32)[:, None] * inv_freq
    cos, sin = jnp.cos(ang), jnp.sin(ang)
    cos128 = jnp.tile(cos, (1, 4))
    sin128 = jnp.concatenate([-sin, sin, -sin, sin], axis=1)
    seg = jnp.arange(128) // HEAD_DIM
    bmat = (seg[:, None] == seg[None, :]).astype(BF16)
    gq128, gk128 = jnp.tile(gain("q_norm"), (1, 2)), jnp.tile(gain("k_norm"), (1, 2))
    sinkcol = jnp.repeat(sp["attn_sinks"].reshape(4, 2), BLK, axis=1).reshape(4, 2 * BLK, 1)
    wsc = sp["gmlp_ws"] * jnp.tril(jnp.ones((BLK, BLK), F32))[None]
    w2 = wsc.reshape(4, 2 * BLK, BLK).astype(MXU_DTYPE)
    w2t = wsc.swapaxes(1, 2).reshape(4, 2 * BLK, BLK).astype(MXU_DTYPE)
    bsl = jnp.repeat(sp["gmlp_bs"].reshape(4, 2, BLK).transpose(0, 2, 1), HEAD_DIM, axis=2)
    cb = sp["ffn_conv_b"].reshape(1, -1)
    w_in_d = _dup_cols(_to_full(w_in(cos128, sin128, gq128, gk128, sinkcol, w2, w2t, bsl), True))[None]

    h1, proj = rms_mm(x, gain("mix_norm"), w_in_d, name="mix_in")
    qr, kr, vb, gu, gvn, attn, gm, y = mixer_core_fwd(proj, cos128, sin128, gq128, gk128, gain("gmlp_v_norm"), bmat,
                                                      sinkcol, gain("attn_out_norm"), w2, bsl, gain("gmlp_out_norm"))
    wf, cw = later(y)
    w_out, xa_wq, xa_wo, ffn_down = (_to_full(wf[n], False) for n in ("w_out", "xa_wq", "xa_wo", "ffn_down"))
    x1 = mm(y, w_out, res=x, name="mix_out")
    mn, kv = rms_mm(mem, gain("mem_norm"), wf["xa_wkv"], name="xa_kv")
    kn, vbx = mem_pre(kv, gain("xa_k_norm"))
    h2, qx, xo, x2 = xattn_block_fwd(x1, gain("xa_norm"), xa_wq, kn, vbx, gain("xa_q_norm"), xa_wo)
    h3, a = rms_mm(x2, gain("ffn_norm"), wf["ffn_up"], name="ffn_up")
    f, dx3, loss_acc = convgate_down_loss(a, cw, cb, ffn_down, x2, target)

    by_rows = lambda g: g.reshape(N_CHIPS, g.shape[1] // N_CHIPS, g.shape[2])
    sent = emit("ffn_down", by_rows(mm_tn(f, dx3, name="g_ffn_down", out_dtype=WIRE_DTYPE)))
    dc, gcw = convgate_bwd(a, dx3, ffn_down[None], cw, cb, after=sent)
    da, dx2, dg_ffn = conv_transpose_rms_bwd(dc, cw, wf["ffn_up"], x2, gain("ffn_norm"), dx3)
    sent = emit("ffn_up", mm_tn(h3, da, name="g_ffn_up", out_dtype=WIRE_DTYPE, chunks=N_CHIPS))
    sent = emit("xa_wo", by_rows(mm_tn(xo, dx2, name="g_xa_wo", out_dtype=WIRE_DTYPE, after=sent)))
    dqx, dx1, dkn, dvx, dg_xq, dg_xa = xattn_block_bwd(dx2, xa_wo[None], qx, kn, vbx, gain("xa_q_norm"), xa_wq[None],
                                                       x1, gain("xa_norm"), after=sent)
    sent = emit("xa_wq", by_rows(mm_tn(h2, dqx, name="g_xa_wq", out_dtype=WIRE_DTYPE)))
    dkv, dg_xk = mem_bwd(kv, dkn, dvx, gain("xa_k_norm"), after=sent)
    _, dg_mem = mm_nt_rms_bwd(dkv, wf["xa_wkv"], mem, gain("mem_norm"), jnp.zeros_like(mem), name="d_mem")
    sent = emit("xa_wkv", mm_tn(mn, dkv, name="g_xa_wkv", out_dtype=WIRE_DTYPE, chunks=N_CHIPS))
    dattn, dgm, dg_y = mm_nt_post_bwd(dx1, w_out[None], attn, gm, gain("attn_out_norm"), gain("gmlp_out_norm"),
                                      name="d_mix_out", after=sent)
    sent = emit("w_out", by_rows(mm_tn(y, dx1, name="g_w_out", out_dtype=WIRE_DTYPE)))
    dproj, dsk, dws, dbl, dgq, dgk, dg_gvn = mixer_core_bwd(
        proj, cos128, sin128, gq128, gk128, gain("gmlp_v_norm"), bmat, qr, kr, vb, sinkcol, dattn, dgm, gvn, gu,
        w2, w2t, bsl, after=sent)
    g_in = _fold_cols(mm_tn(h1, dproj, name="g_w_in", out_dtype=F32)[0])
    sent = emit("w_in", g_in.reshape(1024, N_CHIPS, 448).transpose(1, 0, 2).astype(WIRE_DTYPE))
    grad_x, dg_mix = mm_nt_rms_bwd(dproj, w_in_d, x, gain("mix_norm"), dx1, name="d_x", after=sent)
    packed = pack_small(dg_mix, dgq, dgk, dsk, dg_gvn, dg_y, dg_xa, dg_mem, dg_xq, dg_xk, dg_ffn, gcw, dbl, dws)
    return loss_acc, grad_x, packed


def _gather_step(w, chipvec):
    slots = cast_shards([w[n][0] for n in BIG_NAMES], w["ffn_conv"][0], chipvec)
    send_a, recv_a, first, first_started = gather_start(slots[:1], chipvec)
    send_b, recv_b, rest, rest_started = gather_start(slots[1:], first_started)

    def w_in(*after):
        return gather_wait(send_a, recv_a, first, rest_started, *after)[0]

    def later(after):
        got = gather_wait(send_b, recv_b, rest, after)
        return dict(zip(BIG_NAMES[1:], got[:-1])), _to_full(got[-1], True)

    return w_in, later, rest_started


def _reduce_update(started, packed, w, m, v, chipvec, cvec, order):
    small_sent = small_start(packed)
    own = sum_partials(partials_wait([started[n] for n in BIG_NAMES], small_sent[2]), order)
    other = pair_share(own)
    res = [{}, {}, {}, {}]
    for n, g_own, g_other in zip(BIG_NAMES, own, other):
        for d, o in zip(res, adamw_matrix(w[n], m[n], v[n], g_own, g_other, cvec, name="adamw_" + n)):
            d[n] = o
    mevec = (2 * order[0:1] + order[1:2]).astype(jnp.int32)
    small_sum = sum_small(*small_wait(*small_sent, res[3][BIG_NAMES[-1]]), mevec)
    for d, outs in zip(res, adamw_small(small_sum, w, m, v, chipvec)):
        d.update(zip(SMALL, outs))
    return res


def kernel(x, mem, positions, mix_norm, w_in, q_norm, k_norm, attn_sinks, gmlp_v_norm, gmlp_ws, gmlp_bs, attn_out_norm, gmlp_out_norm, w_out, xa_norm, mem_norm, xa_wq, xa_wkv, xa_q_norm, xa_k_norm, xa_wo, ffn_norm, ffn_up, ffn_conv, ffn_conv_b, ffn_down, loss_target, m_mix_norm, m_w_in, m_q_norm, m_k_norm, m_attn_sinks, m_gmlp_v_norm, m_gmlp_ws, m_gmlp_bs, m_attn_out_norm, m_gmlp_out_norm, m_w_out, m_xa_norm, m_mem_norm, m_xa_wq, m_xa_wkv, m_xa_q_norm, m_xa_k_norm, m_xa_wo, m_ffn_norm, m_ffn_up, m_ffn_conv, m_ffn_conv_b, m_ffn_down, v_mix_norm, v_w_in, v_q_norm, v_k_norm, v_attn_sinks, v_gmlp_v_norm, v_gmlp_ws, v_gmlp_bs, v_attn_out_norm, v_gmlp_out_norm, v_w_out, v_xa_norm, v_mem_norm, v_xa_wq, v_xa_wkv, v_xa_q_norm, v_xa_k_norm, v_xa_wo, v_ffn_norm, v_ffn_up, v_ffn_conv, v_ffn_conv_b, v_ffn_down):
    w = dict(mix_norm=mix_norm, w_in=w_in, q_norm=q_norm, k_norm=k_norm, attn_sinks=attn_sinks, gmlp_v_norm=gmlp_v_norm, gmlp_ws=gmlp_ws, gmlp_bs=gmlp_bs, attn_out_norm=attn_out_norm, gmlp_out_norm=gmlp_out_norm, w_out=w_out, xa_norm=xa_norm, mem_norm=mem_norm, xa_wq=xa_wq, xa_wkv=xa_wkv, xa_q_norm=xa_q_norm, xa_k_norm=xa_k_norm, xa_wo=xa_wo, ffn_norm=ffn_norm, ffn_up=ffn_up, ffn_conv=ffn_conv, ffn_conv_b=ffn_conv_b, ffn_down=ffn_down)
    m = dict(mix_norm=m_mix_norm, w_in=m_w_in, q_norm=m_q_norm, k_norm=m_k_norm, attn_sinks=m_attn_sinks, gmlp_v_norm=m_gmlp_v_norm, gmlp_ws=m_gmlp_ws, gmlp_bs=m_gmlp_bs, attn_out_norm=m_attn_out_norm, gmlp_out_norm=m_gmlp_out_norm, w_out=m_w_out, xa_norm=m_xa_norm, mem_norm=m_mem_norm, xa_wq=m_xa_wq, xa_wkv=m_xa_wkv, xa_q_norm=m_xa_q_norm, xa_k_norm=m_xa_k_norm, xa_wo=m_xa_wo, ffn_norm=m_ffn_norm, ffn_up=m_ffn_up, ffn_conv=m_ffn_conv, ffn_conv_b=m_ffn_conv_b, ffn_down=m_ffn_down)
    v = dict(mix_norm=v_mix_norm, w_in=v_w_in, q_norm=v_q_norm, k_norm=v_k_norm, attn_sinks=v_attn_sinks, gmlp_v_norm=v_gmlp_v_norm, gmlp_ws=v_gmlp_ws, gmlp_bs=v_gmlp_bs, attn_out_norm=v_attn_out_norm, gmlp_out_norm=v_gmlp_out_norm, w_out=v_w_out, xa_norm=v_xa_norm, mem_norm=v_mem_norm, xa_wq=v_xa_wq, xa_wkv=v_xa_wkv, xa_q_norm=v_xa_q_norm, xa_k_norm=v_xa_k_norm, xa_wo=v_xa_wo, ffn_norm=v_ffn_norm, ffn_up=v_ffn_up, ffn_conv=v_ffn_conv, ffn_conv_b=v_ffn_conv_b, ffn_down=v_ffn_down)
    ix, iy, ic = lax.axis_index("x"), lax.axis_index("y"), lax.axis_index("c")
    chip = 2 * ix + iy
    chipvec = chip.astype(jnp.int32).reshape(1)
    cvec = ic.astype(jnp.int32).reshape(1)
    order = jnp.stack([chip, ic] + [4 * px + 2 * py + pc for px, py, pc in _peers(ix, iy, ic)]).astype(jnp.int32)

    w_in_all, later, token = _gather_step(w, chipvec)
    zero = token[0, 0]
    sp = {n: w[n][0] + zero for n in SMALL if n != "ffn_conv"}
    positions = positions + zero.astype(jnp.int32)
    started = {}

    def emit(name, g):
        *started[name], token = partials_start(g, name="partials_start_" + name)
        return token

    loss_acc, grad_x, packed = _local_step(x[0], mem[0], positions[0], loss_target[0], w_in_all, later, sp, emit)
    grads, delta, new_m, new_v = _reduce_update(started, packed, w, m, v, chipvec, cvec, order)
    loss = lax.psum(loss_acc[0, 0], ("x", "y", "c"))
    ordered = lambda d: [d[n] for n in WEIGHTS]
    return (loss, grad_x[None], *ordered(grads), *ordered(delta), *ordered(new_m), *ordered(new_v))
```

```python
import math

import jax
import jax.numpy as jnp
from jax import lax
from jax.experimental import pallas as pl
from jax.experimental.pallas import tpu as pltpu

F32 = jnp.float32
BF16 = jnp.bfloat16
MXU_DTYPE = jnp.bfloat16
WIRE_DTYPE = jnp.bfloat16
EPS = 1e-6
VMEM_LIMIT_V7X = 56 * 1024 * 1024

D_MODEL = 1024
HEAD_DIM = 64
BLK = 128
XA_HEADS = 4
XA_DH = 256
MEM_LEN = 256
D_FF = 2816
IN_COLS_DUP = 2048
N_CHIPS = 4
N_DEV = 8

ADAM_LR = 0.001
ADAM_B1 = 0.9
ADAM_B2 = 0.999
ADAM_EPS = 1e-08
ADAM_WD = 0.01
ADAM_STEP = 10

NT = (((1,), (1,)), ((), ()))
TN = (((0,), (0,)), ((), ()))
NN = (((1,), (0,)), ((), ()))
MINF = float(jnp.finfo(jnp.float32).min)
GELU_K0 = math.sqrt(2.0 / math.pi)
GELU_K1 = 0.044715

BS = pl.BlockSpec
SDS = jax.ShapeDtypeStruct
ANY = pl.BlockSpec(memory_space=pl.ANY)
MESH = pl.DeviceIdType.MESH


def _dot(a, b, dims=NN):
    return lax.dot_general(a.astype(MXU_DTYPE), b.astype(MXU_DTYPE), dims, preferred_element_type=F32)


def _segsum(x, bmat):
    hi = x.astype(BF16)
    lo = (x - hi.astype(F32)).astype(BF16)
    return (jnp.dot(hi, bmat, preferred_element_type=F32) + jnp.dot(lo, bmat, preferred_element_type=F32))


def _gelu(x):
    return 0.5 * x * (1.0 + jnp.tanh(GELU_K0 * (x + GELU_K1 * x * x * x)))


def _gelu_grad(x):
    t = jnp.tanh(GELU_K0 * (x + GELU_K1 * x * x * x))
    return 0.5 * (1.0 + t) + 0.5 * x * (1.0 - t * t) * GELU_K0 * (1.0 + 3.0 * GELU_K1 * x * x)


def _rms(x):
    return lax.rsqrt(jnp.mean(x * x, axis=-1, keepdims=True) + EPS)


def _rms_bwd(dy, x, g, r):
    dyg = dy * g
    dx = r * dyg - x * (r * r * r) * jnp.mean(dyg * x, axis=-1, keepdims=True)
    return dx, dy * x * r


def _pcall(body, *, name, grid, in_specs, out_specs, out_shape, scratch=(), prefetch=0, after=None):
    params = pltpu.CompilerParams(dimension_semantics=("arbitrary",) * len(grid), vmem_limit_bytes=VMEM_LIMIT_V7X)
    in_specs = list(in_specs)
    kernel_fn = body
    if after is not None:
        n_in = prefetch + len(in_specs)
        in_specs.append(ANY)

        def kernel_fn(*refs):
            return body(*refs[:n_in], *refs[n_in + 1:])

    if prefetch:
        spec = pltpu.PrefetchScalarGridSpec(num_scalar_prefetch=prefetch, grid=grid, in_specs=in_specs,
                                            out_specs=out_specs, scratch_shapes=list(scratch))
        call = pl.pallas_call(kernel_fn, name=name, grid_spec=spec, out_shape=out_shape, compiler_params=params)
    else:
        call = pl.pallas_call(kernel_fn, name=name, grid=grid, in_specs=in_specs, out_specs=out_specs,
                              out_shape=out_shape, scratch_shapes=list(scratch), compiler_params=params)
    return call if after is None else (lambda *args: call(*args, after))


def _tile(n, prefs):
    for p in prefs:
        if p <= n and n % p == 0:
            return p
    return n


def _resident(shape):
    return pl.BlockSpec(shape, lambda *_: (0,) * len(shape), pipeline_mode=pl.Buffered(1))


def _acc_rows(ref, row, val):
    ref[row:row + 1, :] += jnp.sum(val, axis=0, keepdims=True)


def rms_mm(x, g, w3, *, name, tm=1024):
    M, K = x.shape
    Q, _, C = w3.shape
    tm = _tile(M, (tm, 256))

    def body(x_ref, g_ref, w_ref, h_ref, o_ref):
        def write_h():
            xv = x_ref[...]
            h_ref[...] = (xv * _rms(xv) * g_ref[...]).astype(h_ref.dtype)

        if Q == 1:
            write_h()
        else:
            pl.when(pl.program_id(1) == 0)(write_h)
        o_ref[...] = _dot(h_ref[...], w_ref[pl.program_id(1)])

    return _pcall(body, name=name, grid=(M // tm, Q),
                  in_specs=[BS((tm, K), lambda i, j: (i, 0)), BS((1, K), lambda i, j: (0, 0)),
                            _resident((Q, K, C))],
                  out_specs=[BS((tm, K), lambda i, j: (i, 0)), BS((tm, C), lambda i, j: (i, j))],
                  out_shape=[SDS((M, K), MXU_DTYPE), SDS((M, Q * C), F32)])(x, g, w3)


def mm(a, w, *, name, res):
    M, K = a.shape
    N = w.shape[1]
    tm = _tile(M, (1024, 256))

    def body(a_ref, w_ref, r_ref, o_ref):
        o_ref[...] = _dot(a_ref[...], w_ref[...]) + r_ref[...]

    return _pcall(body, name=name, grid=(M // tm,),
                  in_specs=[BS((tm, K), lambda i: (i, 0)), _resident((K, N)), BS((tm, N), lambda i: (i, 0))],
                  out_specs=BS((tm, N), lambda i: (i, 0)), out_shape=SDS((M, N), F32))(a, w, res)


def _nt_chunks(a_ref, w_ref):
    q_n, _, kc = w_ref.shape
    acc = _dot(a_ref[:, 0:kc], w_ref[0], NT)
    for q in range(1, q_n):
        acc = acc + _dot(a_ref[:, q * kc:(q + 1) * kc], w_ref[q], NT)
    return acc


def mm_nt_rms_bwd(a, w3, x, g, dres, *, name, tm=512, after=None):
    M = a.shape[0]
    Q, N, Kc = w3.shape
    tm = _tile(M, (tm, 256))

    def body(a_ref, w_ref, x_ref, g_ref, dr_ref, dx_ref, dg_ref):
        @pl.when(pl.program_id(0) == 0)
        def _():
            dg_ref[...] = jnp.zeros_like(dg_ref)

        xv = x_ref[...]
        dx, dgc = _rms_bwd(_nt_chunks(a_ref, w_ref), xv, g_ref[...], _rms(xv))
        dx_ref[...] = dr_ref[...] + dx
        _acc_rows(dg_ref, 0, dgc)

    row = BS((tm, N), lambda i: (i, 0))
    return _pcall(body, name=name, grid=(M // tm,), after=after,
                  in_specs=[BS((tm, Q * Kc), lambda i: (i, 0)), _resident((Q, N, Kc)), row,
                            BS((1, N), lambda i: (0, 0)), row],
                  out_specs=[row, BS((8, N), lambda i: (0, 0))],
                  out_shape=[SDS((M, N), F32), SDS((8, N), F32)])(a, w3, x, g, dres)


def mm_nt_post_bwd(a, w3, attn, gm, gao, ggo, *, name, after=None):
    M = a.shape[0]
    Q, N, Kc = w3.shape
    tm = _tile(M, (512, 256))
    hw = N // 2

    def body(a_ref, w_ref, at_ref, gm_ref, gao_ref, ggo_ref, da_ref, dgm_ref, dg_ref):
        @pl.when(pl.program_id(0) == 0)
        def _():
            dg_ref[...] = jnp.zeros_like(dg_ref)

        dy = _nt_chunks(a_ref, w_ref)
        av, gmv = at_ref[...], gm_ref[...]
        da, dga = _rms_bwd(dy[:, :hw], av, gao_ref[...], _rms(av))
        dgm, dgg = _rms_bwd(dy[:, hw:], gmv, ggo_ref[...], _rms(gmv))
        da_ref[...] = da
        dgm_ref[...] = dgm
        dg_ref[0:1, :hw] += jnp.sum(dga, axis=0, keepdims=True)
        dg_ref[0:1, hw:] += jnp.sum(dgg, axis=0, keepdims=True)

    half = BS((tm, hw), lambda i: (i, 0))
    const = lambda r, w: BS((r, w), lambda i: (0, 0))
    return _pcall(body, name=name, grid=(M // tm,), after=after,
                  in_specs=[BS((tm, Q * Kc), lambda i: (i, 0)), _resident((Q, N, Kc)), half, half,
                            const(1, hw), const(1, hw)],
                  out_specs=[half, half, const(8, N)],
                  out_shape=[SDS((M, hw), F32), SDS((M, hw), F32), SDS((8, N), F32)])(a, w3, attn, gm, gao, ggo)


def mm_tn(a, b, *, name, out_dtype, chunks=1, after=None):
    M, K = a.shape
    N = b.shape[1]
    C = N // chunks
    tm = _tile(M, (1024, 256))
    tk = _tile(K, (1408, 1024, 512))
    tn = _tile(C, (1408, 1024, 512))
    per = C // tn
    nm = M // tm

    def body(a_ref, b_ref, o_ref, acc):
        m = pl.program_id(2)

        @pl.when(m == 0)
        def _():
            acc[...] = jnp.zeros_like(acc)

        acc[...] += _dot(a_ref[...], b_ref[...], TN)

        @pl.when(m == nm - 1)
        def _():
            o_ref[...] = acc[...].astype(o_ref.dtype)

    return _pcall(body, name=name, grid=(K // tk, N // tn, nm), after=after,
                  in_specs=[BS((tm, tk), lambda k, n, m: (m, k)), BS((tm, tn), lambda k, n, m: (m, n))],
                  out_specs=BS((None, tk, tn), lambda k, n, m: (n // per, k, n % per)),
                  out_shape=SDS((chunks, K, C), out_dtype), scratch=[pltpu.VMEM((tk, tn), F32)])(a, b)


def _lane(shape):
    return lax.broadcasted_iota(jnp.int32, shape, 1)


def _head_means(slabs, bmat):
    tm = slabs[0].shape[0]
    means = _segsum(jnp.concatenate(slabs, axis=0), bmat) * (1.0 / HEAD_DIM)
    return [means[i * tm:(i + 1) * tm] for i in range(len(slabs))]


def _half_swap(x, first):
    return jnp.where(first, pltpu.roll(x, 96, 1), pltpu.roll(x, 32, 1))


def _by_head(x2, lo):
    z = jnp.zeros((BLK, 128), x2.dtype)
    parts = []
    for s in range(2):
        xs = x2[:, s * 128:(s + 1) * 128]
        parts += [jnp.where(lo, xs, z), jnp.where(lo, z, xs)]
    return jnp.concatenate(parts, axis=0)


def _from_heads(o4, lo):
    return jnp.concatenate([jnp.where(lo, o4[0:BLK], o4[BLK:2 * BLK]),
                            jnp.where(lo, o4[2 * BLK:3 * BLK], o4[3 * BLK:])], axis=1)


def _swa_probs(q2, kd, sink, n, lo):
    qp = _by_head(q2, lo)
    sc = _dot(qp, kd, NT) * (1.0 / math.sqrt(HEAD_DIM))
    r_i = lax.broadcasted_iota(jnp.int32, (4 * BLK, 2 * BLK), 0)
    k_j = lax.broadcasted_iota(jnp.int32, (4 * BLK, 2 * BLK), 1)
    diff = (r_i & (BLK - 1)) + BLK - k_j
    mask = (diff >= 0) & (diff < BLK) & ((k_j >= BLK) | (n > 0))
    sc = jnp.where(mask, sc, MINF)
    m = jnp.maximum(jnp.max(sc, axis=1, keepdims=True), sink)
    p = jnp.exp(sc - m)
    es = jnp.exp(sink - m)
    inv = 1.0 / (jnp.sum(p, axis=1, keepdims=True) + es)
    return qp, p * inv, es * inv


def mixer_core_fwd(proj, cos, sin, gq, gk, gvn, bmat, sinkcol, gao, w2, bsl, ggo):
    S = proj.shape[0]

    def body(p_ref, c_ref, s_ref, gq_ref, gk_ref, gvn_ref, b_ref, sk_ref, gao_ref, w2_ref, bsl_ref, ggo_ref,
             qr_ref, kr_ref, vb_ref, gu_ref, gvo_ref, at_ref, gm_ref, y_ref, k_prev, v_prev):
        n = pl.program_id(0)

        @pl.when(n == 0)
        def _():
            k_prev[...] = jnp.zeros_like(k_prev)
            v_prev[...] = jnp.zeros_like(v_prev)

        cos_v, sin_v, bm = c_ref[...], s_ref[...], b_ref[...]
        first = (_lane((BLK, 128)) & 63) < 32
        lo = _lane((BLK, 128)) < 64
        slabs = [p_ref[:, s * 128:(s + 1) * 128] for s in range(6)]
        for s, (slab, ms) in enumerate(zip(slabs, _head_means([x * x for x in slabs], bm))):
            qn = slab * lax.rsqrt(ms + EPS) * (gq_ref[...] if s < 4 else gk_ref[...])
            out = qn * cos_v + _half_swap(qn, first) * sin_v
            if s < 4:
                qr_ref[:, s * 128:(s + 1) * 128] = out.astype(qr_ref.dtype)
            else:
                kr_ref[:, (s - 4) * 128:(s - 3) * 128] = out.astype(kr_ref.dtype)
        vb_ref[...] = p_ref[:, 768:1024].astype(vb_ref.dtype)
        gu_ref[...] = _gelu(p_ref[:, 1024:1536])
        gv = _gelu(p_ref[:, 1536:2048])
        gvo_ref[...] = (gv * _rms(gv) * gvn_ref[...]).astype(gvo_ref.dtype)

        for h in range(2):
            hs, qs = slice(h * 128, (h + 1) * 128), slice(h * 256, (h + 1) * 256)
            kd = jnp.concatenate([k_prev[:, hs], kr_ref[:, hs]], axis=0)
            vd = jnp.concatenate([v_prev[:, hs], vb_ref[:, hs]], axis=0)
            sink = jnp.concatenate([sk_ref[2 * h], sk_ref[2 * h + 1]], axis=0)
            _, p, _ = _swa_probs(qr_ref[:, qs], kd, sink, n, lo)
            at_ref[:, qs] = _from_heads(_dot(p, vd), lo)
        k_prev[...] = kr_ref[...]
        v_prev[...] = vb_ref[...]

        for j in range(4):
            sl = slice(j * 128, (j + 1) * 128)
            m2 = _dot(w2_ref[j], gvo_ref[:, sl])
            mixed = jnp.where(lo, m2[:BLK], m2[BLK:]) + bsl_ref[j]
            gm_ref[:, sl] = gu_ref[:, sl] * mixed
        a, gm = at_ref[...], gm_ref[...]
        y_ref[:, :512] = (a * _rms(a) * gao_ref[...]).astype(y_ref.dtype)
        y_ref[:, 512:] = (gm * _rms(gm) * ggo_ref[...]).astype(y_ref.dtype)

    row = lambda w: BS((BLK, w), lambda n: (n, 0))
    const = lambda *shape: BS(shape, lambda n: (0,) * len(shape))
    return _pcall(body, name="mixer_core_fwd", grid=(S // BLK,),
                  in_specs=[row(IN_COLS_DUP), row(128), row(128), const(1, 128), const(1, 128), const(1, 512),
                            const(128, 128), const(4, 2 * BLK, 1), const(1, 512), const(4, 2 * BLK, BLK),
                            const(4, BLK, 128), const(1, 512)],
                  out_specs=[row(512), row(256), row(256), row(512), row(512), row(512), row(512), row(1024)],
                  out_shape=[SDS((S, 512), MXU_DTYPE), SDS((S, 256), MXU_DTYPE), SDS((S, 256), MXU_DTYPE),
                             SDS((S, 512), F32), SDS((S, 512), MXU_DTYPE), SDS((S, 512), F32), SDS((S, 512), F32),
                             SDS((S, 1024), MXU_DTYPE)],
                  scratch=[pltpu.VMEM((BLK, 256), MXU_DTYPE), pltpu.VMEM((BLK, 256), MXU_DTYPE)])(
        proj, cos, sin, gq, gk, gvn, bmat, sinkcol, gao, w2, bsl, ggo)


def mem_pre(kv, gxk):
    def body(kv_ref, g_ref, kn_ref, vb_ref):
        for h in range(XA_HEADS):
            sl = slice(h * XA_DH, (h + 1) * XA_DH)
            k = kv_ref[:, sl]
            kn_ref[:, sl] = (k * _rms(k) * g_ref[...]).astype(kn_ref.dtype)
        vb_ref[...] = kv_ref[:, 1024:2048].astype(vb_ref.dtype)

    full = lambda r, w: BS((r, w), lambda i: (0, 0))
    return _pcall(body, name="mem_pre", grid=(1,), in_specs=[full(MEM_LEN, 2048), full(1, XA_DH)],
                  out_specs=[full(MEM_LEN, 1024), full(MEM_LEN, 1024)],
                  out_shape=[SDS((MEM_LEN, 1024), MXU_DTYPE), SDS((MEM_LEN, 1024), MXU_DTYPE)])(kv, gxk)


def _xa_probs(qh, g, kn_h):
    r = _rms(qh)
    qn = qh * r * g
    s = _dot(qn, kn_h, NT) * (1.0 / math.sqrt(XA_DH))
    p = jnp.exp(s - jnp.max(s, axis=1, keepdims=True))
    return r, qn, p * (1.0 / jnp.sum(p, axis=1, keepdims=True))


def xattn_block_fwd(x1, g, wq, kn, vb, gxq, wo):
    S, D = x1.shape
    tm = _tile(S, (512, 256))

    def body(x_ref, g_ref, wq_ref, kn_ref, vb_ref, gxq_ref, wo_ref, h_ref, q_ref, o_ref, x2_ref):
        xv = x_ref[...]
        h_ref[...] = (xv * _rms(xv) * g_ref[...]).astype(h_ref.dtype)
        q_ref[...] = _dot(h_ref[...], wq_ref[...])
        for h in range(XA_HEADS):
            sl = slice(h * XA_DH, (h + 1) * XA_DH)
            _, _, p = _xa_probs(q_ref[:, sl], gxq_ref[...], kn_ref[:, sl])
            o_ref[:, sl] = _dot(p, vb_ref[:, sl]).astype(o_ref.dtype)
        x2_ref[...] = _dot(o_ref[...], wo_ref[...]) + xv

    row = BS((tm, D), lambda i: (i, 0))
    full = lambda r, w: BS((r, w), lambda i: (0, 0))
    return _pcall(body, name="xattn_block_fwd", grid=(S // tm,),
                  in_specs=[row, full(1, D), _resident(wq.shape), full(MEM_LEN, D), full(MEM_LEN, D), full(1, XA_DH),
                            _resident(wo.shape)],
                  out_specs=[row, row, row, row],
                  out_shape=[SDS((S, D), MXU_DTYPE), SDS((S, D), F32), SDS((S, D), MXU_DTYPE), SDS((S, D), F32)])(
        x1, g, wq, kn, vb, gxq, wo)


CONV_COLS = 1408


def _conv_taps(a_ref, halo_ref, w_ref, b_ref, cols, first_tile):
    a = a_ref[:, cols]
    row = lax.broadcasted_iota(jnp.int32, a.shape, 0)
    h6 = jnp.where(first_tile, 0.0, halo_ref[6:7, cols])
    h7 = jnp.where(first_tile, 0.0, halo_ref[7:8, cols])
    a1 = jnp.where(row == 0, h7, pltpu.roll(a, 1, 0))
    a2 = jnp.where(row == 0, h6, jnp.where(row == 1, h7, pltpu.roll(a, 2, 0)))
    c = w_ref[2:3, cols] * a + w_ref[1:2, cols] * a1 + w_ref[0:1, cols] * a2 + b_ref[:, cols]
    return c, (a2, a1, a)


def _conv_specs(tm):
    halo_blocks = tm // 8
    return [BS((tm, D_FF), lambda i: (i, 0)), BS((tm, D_FF), lambda i: (i, 1)),
            BS((8, D_FF), lambda i: (jnp.maximum(i * halo_blocks - 1, 0), 0)),
            BS((8, D_FF), lambda i: (jnp.maximum(i * halo_blocks - 1, 0), 1)),
            BS((3, D_FF), lambda i: (0, 0)), BS((3, D_FF), lambda i: (0, 1)),
            BS((1, D_FF), lambda i: (0, 0)), BS((1, D_FF), lambda i: (0, 1))]


def convgate_down_loss(a, cw, cb, w, res, target):
    S = a.shape[0]
    N = w.shape[1]
    tm = _tile(S, (256,))

    def body(ag_ref, au_ref, hg_ref, hu_ref, wg_ref, wu_ref, bg_ref, bu_ref, w_ref, r_ref, t_ref, f_ref, d_ref,
             l_ref):
        first_tile = pl.program_id(0) == 0

        @pl.when(first_tile)
        def _():
            l_ref[...] = jnp.zeros_like(l_ref)

        for c0 in range(0, D_FF, CONV_COLS):
            cols = slice(c0, c0 + CONV_COLS)
            cg, _ = _conv_taps(ag_ref, hg_ref, wg_ref, bg_ref, cols, first_tile)
            cu, _ = _conv_taps(au_ref, hu_ref, wu_ref, bu_ref, cols, first_tile)
            f_ref[:, cols] = (_gelu(cg) * cu).astype(f_ref.dtype)
        e = _dot(f_ref[...], w_ref[...]) + r_ref[...] - t_ref[...]
        d_ref[...] = e * (1.0 / N)
        l_ref[...] += jnp.sum(e * e) * (0.5 / N)

    row_n = BS((tm, N), lambda i: (i, 0))
    return _pcall(body, name="convgate_down_loss", grid=(S // tm,),
                  in_specs=_conv_specs(tm) + [_resident((D_FF, N)), row_n, row_n],
                  out_specs=[BS((tm, D_FF), lambda i: (i, 0)), row_n, BS((8, 128), lambda i: (0, 0))],
                  out_shape=[SDS((S, D_FF), MXU_DTYPE), SDS((S, N), F32), SDS((8, 128), F32)])(
        a, a, a, a, cw, cw, cb, cb, w, res, target)


def convgate_bwd(a, dx3, w3, cw, cb, after=None):
    S = a.shape[0]
    tm = _tile(S, (256,))

    def body(ag_ref, au_ref, hg_ref, hu_ref, wg_ref, wu_ref, bg_ref, bu_ref, dx_ref, wd_ref, dc_ref, gw_ref, df_ref):
        first_tile = pl.program_id(0) == 0

        @pl.when(first_tile)
        def _():
            gw_ref[...] = jnp.zeros_like(gw_ref)

        df_ref[...] = _nt_chunks(dx_ref, wd_ref)
        for c0 in range(0, D_FF, CONV_COLS):
            cols, ucols = slice(c0, c0 + CONV_COLS), slice(D_FF + c0, D_FF + c0 + CONV_COLS)
            cg, g_taps = _conv_taps(ag_ref, hg_ref, wg_ref, bg_ref, cols, first_tile)
            cu, u_taps = _conv_taps(au_ref, hu_ref, wu_ref, bu_ref, cols, first_tile)
            df_v = df_ref[:, cols]
            dcg = df_v * cu * _gelu_grad(cg)
            dcu = df_v * _gelu(cg)
            dc_ref[:, cols] = dcg
            dc_ref[:, ucols] = dcu
            for col, dcv, taps in ((cols, dcg, g_taps), (ucols, dcu, u_taps)):
                for j in range(3):
                    gw_ref[j:j + 1, col] += jnp.sum(dcv * taps[j], axis=0, keepdims=True)
                gw_ref[3:4, col] += jnp.sum(dcv, axis=0, keepdims=True)

    return _pcall(body, name="convgate_bwd", grid=(S // tm,), after=after,
                  in_specs=_conv_specs(tm) + [BS((tm, dx3.shape[1]), lambda i: (i, 0)), _resident(w3.shape)],
                  out_specs=[BS((tm, 2 * D_FF), lambda i: (i, 0)), BS((8, 2 * D_FF), lambda i: (0, 0))],
                  out_shape=[SDS((S, 2 * D_FF), F32), SDS((8, 2 * D_FF), F32)],
                  scratch=[pltpu.VMEM((tm, D_FF), F32)])(a, a, a, a, cw, cw, cb, cb, dx3, w3)


def conv_transpose_rms_bwd(dc, cw, w3, x, g, dres):
    S, C = dc.shape
    Q, N, Kc = w3.shape
    tm = _tile(S, (256,))
    nt = S // tm
    halo_blocks = tm // 8

    def body(dc_ref, halo_ref, cw_ref, w_ref, x_ref, g_ref, dr_ref, da_ref, dx_ref, dg_ref):
        @pl.when(pl.program_id(0) == 0)
        def _():
            dg_ref[...] = jnp.zeros_like(dg_ref)

        last_tile = pl.program_id(0) == nt - 1
        row = lax.broadcasted_iota(jnp.int32, (tm, CONV_COLS), 0)
        for c0 in range(0, C, CONV_COLS):
            cols = slice(c0, c0 + CONV_COLS)
            h0 = jnp.where(last_tile, 0.0, halo_ref[0:1, cols])
            h1 = jnp.where(last_tile, 0.0, halo_ref[1:2, cols])
            dc_v = dc_ref[:, cols]
            n1 = jnp.where(row == tm - 1, h0, pltpu.roll(dc_v, tm - 1, 0))
            n2 = jnp.where(row == tm - 1, h1, jnp.where(row == tm - 2, h0, pltpu.roll(dc_v, tm - 2, 0)))
            da_ref[:, cols] = (cw_ref[2:3, cols] * dc_v + cw_ref[1:2, cols] * n1
                               + cw_ref[0:1, cols] * n2).astype(da_ref.dtype)
        xv = x_ref[...]
        dx, dgc = _rms_bwd(_nt_chunks(da_ref, w_ref), xv, g_ref[...], _rms(xv))
        dx_ref[...] = dr_ref[...] + dx
        _acc_rows(dg_ref, 0, dgc)

    row_n = BS((tm, N), lambda i: (i, 0))
    return _pcall(body, name="conv_transpose_rms_bwd", grid=(nt,),
                  in_specs=[BS((tm, C), lambda i: (i, 0)),
                            BS((8, C), lambda i: (jnp.minimum((i + 1) * halo_blocks, S // 8 - 1), 0)),
                            BS((3, C), lambda i: (0, 0)), _resident((Q, N, Kc)), row_n, BS((1, N), lambda i: (0, 0)),
                            row_n],
                  out_specs=[BS((tm, C), lambda i: (i, 0)), row_n, BS((8, N), lambda i: (0, 0))],
                  out_shape=[SDS((S, C), MXU_DTYPE), SDS((S, N), F32), SDS((8, N), F32)])(dc, dc, cw, w3, x, g, dres)


def xattn_block_bwd(dx2, wo3, qx, kn, vb, gxq, wq3, x1, g, after=None):
    S, D = qx.shape
    tm = _tile(S, (512, 256))

    def body(dx2_ref, wo_ref, q_ref, kn_ref, vb_ref, gxq_ref, wq_ref, x_ref, g_ref,
             dq_ref, dx_ref, dkn_ref, dv_ref, dgq_ref, dg_ref):
        @pl.when(pl.program_id(0) == 0)
        def _():
            for ref in (dkn_ref, dv_ref, dgq_ref, dg_ref):
                ref[...] = jnp.zeros_like(ref)

        gq = gxq_ref[...]
        do_all = _nt_chunks(dx2_ref, wo_ref)
        for h in range(XA_HEADS):
            sl = slice(h * XA_DH, (h + 1) * XA_DH)
            qh, do = q_ref[:, sl], do_all[:, sl]
            r, qn, p = _xa_probs(qh, gq, kn_ref[:, sl])
            dp = _dot(do, vb_ref[:, sl], NT)
            ds = p * (dp - jnp.sum(dp * p, axis=1, keepdims=True)) * (1.0 / math.sqrt(XA_DH))
            dqn = _dot(ds, kn_ref[:, sl])
            dkn_ref[:, sl] += _dot(ds, qn, TN)
            dv_ref[:, sl] += _dot(p, do, TN)
            dqh, dgc = _rms_bwd(dqn, qh, gq, r)
            dq_ref[:, sl] = dqh.astype(dq_ref.dtype)
            _acc_rows(dgq_ref, 0, dgc)
        xv = x_ref[...]
        dx, dgc = _rms_bwd(_nt_chunks(dq_ref, wq_ref), xv, g_ref[...], _rms(xv))
        dx_ref[...] = dx2_ref[...] + dx
        _acc_rows(dg_ref, 0, dgc)

    row = BS((tm, D), lambda i: (i, 0))
    full = lambda r, w: BS((r, w), lambda i: (0, 0))
    return _pcall(body, name="xattn_block_bwd", grid=(S // tm,), after=after,
                  in_specs=[row, _resident(wo3.shape), row, full(MEM_LEN, D), full(MEM_LEN, D), full(1, XA_DH),
                            _resident(wq3.shape), row, full(1, D)],
                  out_specs=[row, row, full(MEM_LEN, D), full(MEM_LEN, D), full(8, XA_DH), full(8, D)],
                  out_shape=[SDS((S, D), MXU_DTYPE), SDS((S, D), F32), SDS((MEM_LEN, D), F32), SDS((MEM_LEN, D), F32),
                             SDS((8, XA_DH), F32), SDS((8, D), F32)])(dx2, wo3, qx, kn, vb, gxq, wq3, x1, g)


def mem_bwd(kv, dkn, dvb, gxk, after=None):
    def body(kv_ref, dkn_ref, dv_ref, g_ref, dkv_ref, dg_ref):
        dg_ref[...] = jnp.zeros_like(dg_ref)
        for h in range(XA_HEADS):
            sl = slice(h * XA_DH, (h + 1) * XA_DH)
            k = kv_ref[:, sl]
            dk, dgc = _rms_bwd(dkn_ref[:, sl], k, g_ref[...], _rms(k))
            dkv_ref[:, sl] = dk.astype(dkv_ref.dtype)
            _acc_rows(dg_ref, 0, dgc)
        dkv_ref[:, 1024:2048] = dv_ref[...].astype(dkv_ref.dtype)

    full = lambda r, w: BS((r, w), lambda i: (0, 0))
    return _pcall(body, name="mem_bwd", grid=(1,), after=after,
                  in_specs=[full(MEM_LEN, 2048), full(MEM_LEN, 1024), full(MEM_LEN, 1024), full(1, XA_DH)],
                  out_specs=[full(MEM_LEN, 2048), full(8, XA_DH)],
                  out_shape=[SDS((MEM_LEN, 2048), MXU_DTYPE), SDS((8, XA_DH), F32)])(kv, dkn, dvb, gxk)


def _norm_rope_bwd(slabs, douts, g, bm, cos_v, sin_v, first):
    dqns = [d * cos_v + _half_swap(d * sin_v, first) for d in douts]
    rs = [lax.rsqrt(ms + EPS) for ms in _head_means([x * x for x in slabs], bm)]
    projs = _head_means([dqn * g * x for dqn, x in zip(dqns, slabs)], bm)
    dxs = [r * (dqn * g) - x * (r * r * r) * pr for x, dqn, r, pr in zip(slabs, dqns, rs, projs)]
    return dxs, [dqn * x * r for x, dqn, r in zip(slabs, dqns, rs)]


def mixer_core_bwd(proj, cos, sin, gq, gk, gvg, bmat, qr, kr, vb, sinkcol, dattn, dgm, gvn, gu, w2, w2t, bsl,
                   after=None):
    S = qr.shape[0]
    nb = S // BLK

    def body(p_ref, c_ref, s_ref, gq_ref, gk_ref, gvg_ref, b_ref, q_ref, kc_ref, kp_ref, vc_ref, vp_ref, sk_ref,
             do_ref, dgm_ref, gvn_ref, gu_ref, w2_ref, w2t_ref, bsl_ref,
             dp_ref, dsk_ref, dws_ref, dbl_ref, dgq_ref, dgk_ref, dgv_ref,
             carry_k, carry_v, done_k, done_v, dq_keep, dgu_keep, dgvn_keep):
        n = pl.program_id(0)

        @pl.when(n == 0)
        def _():
            for ref in (dsk_ref, dws_ref, dbl_ref, dgq_ref, dgk_ref, dgv_ref, carry_k, carry_v, dq_keep, dgu_keep,
                        dgvn_keep):
                ref[...] = jnp.zeros_like(ref)

        live = (n < nb).astype(F32)
        cos_v, sin_v, bm = c_ref[...], s_ref[...], b_ref[...]
        first = (_lane((BLK, 128)) & 63) < 32
        lo = _lane((BLK, 128)) < 64

        dxs, dgs = _norm_rope_bwd([p_ref[:, s * 128:(s + 1) * 128] for s in range(4)],
                                  [dq_keep[:, s * 128:(s + 1) * 128] for s in range(4)], gq_ref[...], bm,
                                  cos_v, sin_v, first)
        for s, (dx, dg) in enumerate(zip(dxs, dgs)):
            dp_ref[:, s * 128:(s + 1) * 128] = dx.astype(dp_ref.dtype)
            _acc_rows(dgq_ref, 0, dg)
        dp_ref[:, 1024:1536] = (dgu_keep[...] * _gelu_grad(p_ref[:, 1024:1536])).astype(dp_ref.dtype)
        gvp = p_ref[:, 1536:2048]
        gv = _gelu(gvp)
        dgv, dgc = _rms_bwd(dgvn_keep[...], gv, gvg_ref[...], _rms(gv))
        dp_ref[:, 1536:2048] = (dgv * _gelu_grad(gvp)).astype(dp_ref.dtype)
        _acc_rows(dgv_ref, 0, dgc)

        for h in range(2):
            hs, qs = slice(h * 128, (h + 1) * 128), slice(h * 256, (h + 1) * 256)
            kd = jnp.concatenate([kp_ref[:, hs], kc_ref[:, hs]], axis=0)
            vd = jnp.concatenate([vp_ref[:, hs], vc_ref[:, hs]], axis=0)
            sink = jnp.concatenate([sk_ref[2 * h], sk_ref[2 * h + 1]], axis=0)
            qp, p, psink = _swa_probs(q_ref[:, qs], kd, sink, n, lo)
            dop = _by_head(do_ref[:, qs], lo)
            dp = _dot(dop, vd, NT)
            delta = jnp.sum(dp * p, axis=1, keepdims=True)
            ds = p * (dp - delta) * (1.0 / math.sqrt(HEAD_DIM))
            dsink = -psink * delta * live
            dsk_ref[2 * h] += dsink[:2 * BLK]
            dsk_ref[2 * h + 1] += dsink[2 * BLK:]
            dq_keep[:, qs] = _from_heads(_dot(ds, kd), lo)
            dkd = _dot(ds, qp, TN)
            dvd = _dot(p, dop, TN)
            done_k[:, hs] = carry_k[:, hs] + live * dkd[:BLK]
            done_v[:, hs] = carry_v[:, hs] + live * dvd[:BLK]
            carry_k[:, hs] = dkd[BLK:]
            carry_v[:, hs] = dvd[BLK:]
        for j in range(4):
            sl = slice(j * 128, (j + 1) * 128)
            gvn_s = gvn_ref[:, sl]
            m2 = _dot(w2_ref[j], gvn_s)
            mixed = jnp.where(lo, m2[:BLK], m2[BLK:]) + bsl_ref[j]
            dgm_s = dgm_ref[:, sl]
            dgu_keep[:, sl] = dgm_s * mixed
            dmx = dgm_s * gu_ref[:, sl] * live
            d2 = _dot(w2t_ref[j], dmx)
            dgvn_keep[:, sl] = jnp.where(lo, d2[:BLK], d2[BLK:])
            z = jnp.zeros_like(dmx)
            dws_ref[2 * j] += _dot(jnp.where(lo, dmx, z), gvn_s, NT)
            dws_ref[2 * j + 1] += _dot(jnp.where(lo, z, dmx), gvn_s, NT)
            dbl_ref[j] += dmx

        dxs, dgs = _norm_rope_bwd([p_ref[:, 512 + s * 128:640 + s * 128] for s in range(2)],
                                  [done_k[:, s * 128:(s + 1) * 128] for s in range(2)], gk_ref[...], bm,
                                  cos_v, sin_v, first)
        for s, (dx, dg) in enumerate(zip(dxs, dgs)):
            dp_ref[:, 512 + s * 128:640 + s * 128] = dx.astype(dp_ref.dtype)
            _acc_rows(dgk_ref, 0, dg)
        dp_ref[:, 768:1024] = done_v[...].astype(dp_ref.dtype)

    last = nb - 1
    cur = lambda w: BS((BLK, w), lambda n: (jnp.minimum(n, last), 0))
    prev = lambda w: BS((BLK, w), lambda n: (jnp.clip(n - 1, 0, last), 0))
    done = lambda w: BS((BLK, w), lambda n: (jnp.maximum(n - 1, 0), 0))
    const = lambda *shape: BS(shape, lambda n: (0,) * len(shape))
    return _pcall(body, name="mixer_core_bwd", grid=(nb + 1,), after=after,
                  in_specs=[done(IN_COLS_DUP), done(128), done(128), const(1, 128), const(1, 128), const(1, 512),
                            const(128, 128), cur(512), cur(256), prev(256), cur(256), prev(256),
                            const(4, 2 * BLK, 1), cur(512), cur(512), cur(512), cur(512), const(4, 2 * BLK, BLK),
                            const(4, 2 * BLK, BLK), const(4, BLK, 128)],
                  out_specs=[done(IN_COLS_DUP), const(4, 2 * BLK, 1), const(8, BLK, BLK), const(4, BLK, 128),
                             const(8, 128), const(8, 128), const(8, 512)],
                  out_shape=[SDS((S, IN_COLS_DUP), MXU_DTYPE), SDS((4, 2 * BLK, 1), F32), SDS((8, BLK, BLK), F32),
                             SDS((4, BLK, 128), F32), SDS((8, 128), F32), SDS((8, 128), F32), SDS((8, 512), F32)],
                  scratch=[pltpu.VMEM((BLK, 256), F32)] * 4 + [pltpu.VMEM((BLK, 512), F32)] * 3)(
        proj, cos, sin, gq, gk, gvg, bmat, qr, kr, kr, vb, vb, sinkcol, dattn, dgm, gvn, gu, w2, w2t, bsl)


BIG = (("w_in", (1024, 448), True), ("w_out", (256, 1024), False), ("xa_wq", (256, 1024), False),
       ("xa_wkv", (1024, 512), True), ("xa_wo", (256, 1024), False), ("ffn_up", (1024, 1408), True),
       ("ffn_down", (704, 1024), False))
BIG_NAMES = tuple(n for n, _, _ in BIG)
SMALL_VECS = (("mix_norm", 1024), ("q_norm", 64), ("k_norm", 64), ("attn_sinks", 8), ("gmlp_v_norm", 512),
              ("attn_out_norm", 512), ("gmlp_out_norm", 512), ("xa_norm", 1024), ("mem_norm", 1024),
              ("xa_q_norm", 256), ("xa_k_norm", 256), ("ffn_norm", 1024), ("ffn_conv_b", 5632))
SMALL = tuple(n for n, _ in SMALL_VECS) + ("gmlp_bs", "gmlp_ws", "ffn_conv")
WEIGHTS = ("mix_norm", "w_in", "q_norm", "k_norm", "attn_sinks", "gmlp_v_norm", "gmlp_ws", "gmlp_bs",
           "attn_out_norm", "gmlp_out_norm", "w_out", "xa_norm", "mem_norm", "xa_wq", "xa_wkv", "xa_q_norm",
           "xa_k_norm", "xa_wo", "ffn_norm", "ffn_up", "ffn_conv", "ffn_conv_b", "ffn_down")
CONV_SHARD = (3, 1408)
CONV_LANE_ROWS = CONV_SHARD[1] // 128
CONV_CHIP_ROWS = 40


def _small_rows():
    rows, r = {}, 0
    for n, length in SMALL_VECS:
        rows[n] = r
        r += -(-length // 128)
    r += -r % 8
    rows["gmlp_bs"] = r
    r += 8
    rows["gmlp_ws"] = r
    r += 8 * BLK
    rows["ffn_conv"] = r
    r += N_CHIPS * CONV_CHIP_ROWS
    return rows, r


SMALL_ROW, SMALL_ROWS = _small_rows()


def pack_small(dg_mix, dgq, dgk, dsk, dg_gvn, dg_y, dg_xa, dg_mem, dg_xq, dg_xk, dg_ffn, gcw, dbl, dws):
    def body(mix_ref, q_ref, k_ref, sk_ref, gvn_ref, y_ref, xa_ref, mem_ref, xq_ref, xk_ref, ffn_ref, cw_ref,
             dbl_ref, dws_ref, o_ref):
        o_ref[...] = jnp.zeros_like(o_ref)
        lane = _lane((1, 128))

        def put(name, src_ref, row, lane0, length):
            for k in range(length // 128):
                o_ref[SMALL_ROW[name] + k:SMALL_ROW[name] + k + 1, :] = src_ref[row:row + 1, lane0 + k * 128:lane0 + (k + 1) * 128]

        put("mix_norm", mix_ref, 0, 0, 1024)
        for name, ref in (("q_norm", q_ref), ("k_norm", k_ref)):
            v = ref[0:1, :]
            o_ref[SMALL_ROW[name]:SMALL_ROW[name] + 1, :] = jnp.where(lane < HEAD_DIM, v + pltpu.roll(v, 64, 1), 0.0)
        sinks = jnp.zeros((1, 128), F32)
        for s in range(4):
            col = sk_ref[s]
            sinks = sinks + jnp.where(lane == 2 * s, jnp.sum(col[:BLK]), 0.0) + jnp.where(lane == 2 * s + 1, jnp.sum(col[BLK:]), 0.0)
        o_ref[SMALL_ROW["attn_sinks"]:SMALL_ROW["attn_sinks"] + 1, :] = sinks
        put("gmlp_v_norm", gvn_ref, 0, 0, 512)
        put("attn_out_norm", y_ref, 0, 0, 512)
        put("gmlp_out_norm", y_ref, 0, 512, 512)
        put("xa_norm", xa_ref, 0, 0, 1024)
        put("mem_norm", mem_ref, 0, 0, 1024)
        put("xa_q_norm", xq_ref, 0, 0, 256)
        put("xa_k_norm", xk_ref, 0, 0, 256)
        put("ffn_norm", ffn_ref, 0, 0, 1024)
        put("ffn_conv_b", cw_ref, 3, 0, 2 * D_FF)
        r8 = lax.broadcasted_iota(jnp.int32, (8, 128), 0)
        l8 = _lane((8, 128))
        bs = jnp.zeros((8, BLK), F32)
        for j in range(4):
            sel = (((r8 == 2 * j) & (l8 < 64)) | ((r8 == 2 * j + 1) & (l8 >= 64))).astype(F32).astype(BF16)
            xj = dbl_ref[j]
            hi = xj.astype(BF16)
            lo = (xj - hi.astype(F32)).astype(BF16)
            bs = bs + lax.dot_general(sel, hi, NT, preferred_element_type=F32) + lax.dot_general(sel, lo, NT, preferred_element_type=F32)
        o_ref[SMALL_ROW["gmlp_bs"]:SMALL_ROW["gmlp_bs"] + 8, :] = bs
        causal = lax.broadcasted_iota(jnp.int32, (BLK, BLK), 0) >= lax.broadcasted_iota(jnp.int32, (BLK, BLK), 1)
        for h in range(8):
            r0 = SMALL_ROW["gmlp_ws"] + h * BLK
            o_ref[r0:r0 + BLK, :] = jnp.where(causal, dws_ref[h], 0.0)
        for q in range(N_CHIPS):
            for j in range(3):
                for k in range(CONV_LANE_ROWS):
                    r0 = SMALL_ROW["ffn_conv"] + q * CONV_CHIP_ROWS + j * CONV_LANE_ROWS + k
                    l0 = (q * CONV_LANE_ROWS + k) * 128
                    o_ref[r0:r0 + 1, :] = cw_ref[j:j + 1, l0:l0 + 128]

    args = (dg_mix, dgq, dgk, dsk, dg_gvn, dg_y, dg_xa, dg_mem, dg_xq, dg_xk, dg_ffn, gcw, dbl, dws)
    full = lambda a: BS(a.shape, lambda i, nd=a.ndim: (0,) * nd)
    return _pcall(body, name="pack_small", grid=(1,), in_specs=[full(a) for a in args],
                  out_specs=BS((SMALL_ROWS, 128), lambda i: (0, 0)), out_shape=SDS((SMALL_ROWS, 128), F32))(*args)


def _adam(w, g, m, v):
    mn = ADAM_B1 * m + (1.0 - ADAM_B1) * g
    vn = ADAM_B2 * v + (1.0 - ADAM_B2) * (g * g)
    m_hat = mn / (1.0 - ADAM_B1 ** ADAM_STEP)
    v_hat = vn / (1.0 - ADAM_B2 ** ADAM_STEP)
    return -ADAM_LR * (m_hat / (jnp.sqrt(v_hat) + ADAM_EPS) + ADAM_WD * w), mn, vn


def adamw_small(gsum, w, m, v, chipvec):
    n = len(SMALL)

    def body(chip_ref, g_ref, *refs):
        w_refs, m_refs, v_refs = refs[:n], refs[n:2 * n], refs[2 * n:3 * n]
        outs = refs[3 * n:]
        go, do, mo, vo = outs[:n], outs[n:2 * n], outs[2 * n:3 * n], outs[3 * n:]

        def update(i, idx, g):
            d, mn, vn = _adam(w_refs[i][idx], g, m_refs[i][idx], v_refs[i][idx])
            go[i][idx] = g
            do[i][idx] = d
            mo[i][idx] = mn
            vo[i][idx] = vn

        for i, (name, length) in enumerate(SMALL_VECS):
            for k in range(-(-length // 128)):
                wd = min(128, length - k * 128)
                r = SMALL_ROW[name] + k
                update(i, (slice(0, 1), slice(k * 128, k * 128 + wd)), g_ref[r:r + 1, 0:wd])
        i_bs, i_ws, i_cv = len(SMALL_VECS), len(SMALL_VECS) + 1, len(SMALL_VECS) + 2
        update(i_bs, (0,), g_ref[SMALL_ROW["gmlp_bs"]:SMALL_ROW["gmlp_bs"] + 8, :])
        for h in range(8):
            r0 = SMALL_ROW["gmlp_ws"] + h * BLK
            update(i_ws, (0, h), g_ref[r0:r0 + BLK, :])
        mine = g_ref[pl.ds(pl.multiple_of(SMALL_ROW["ffn_conv"] + chip_ref[0] * CONV_CHIP_ROWS, 8), CONV_CHIP_ROWS), :]
        for j in range(3):
            for k in range(CONV_LANE_ROWS):
                r = j * CONV_LANE_ROWS + k
                update(i_cv, (0, slice(j, j + 1), slice(k * 128, (k + 1) * 128)), mine[r:r + 1, :])

    nat = [w[nm] for nm in SMALL]
    full = lambda a: BS(a.shape, lambda i, c, nd=a.ndim: (0,) * nd)
    outs = _pcall(body, name="adamw_small", grid=(1,), prefetch=1,
                  in_specs=[BS((SMALL_ROWS, 128), lambda i, c: (0, 0))] + [full(a) for a in nat] * 3,
                  out_specs=[full(a) for a in nat] * 4, out_shape=[SDS(a.shape, F32) for a in nat] * 4)(
        chipvec, gsum, *nat, *[m[nm] for nm in SMALL], *[v[nm] for nm in SMALL])
    return outs[:n], outs[n:2 * n], outs[2 * n:3 * n], outs[3 * n:]


def adamw_matrix(w, m, v, g_own, g_other, cvec, *, name):
    _, r, c = w.shape
    half = r // 2
    tr = _tile(half, (128, 176))
    T = half // tr

    def body(c_ref, w_ref, m_ref, v_ref, own_ref, oth_ref, g_ref, d_ref, mo_ref, vo_ref):
        g = jnp.where(pl.program_id(0) == c_ref[0], own_ref[...], oth_ref[...])
        d, mn, vn = _adam(w_ref[...], g, m_ref[...], v_ref[...])
        g_ref[...] = g
        d_ref[...] = d
        mo_ref[...] = mn
        vo_ref[...] = vn

    nat = BS((None, tr, c), lambda hf, t, cr: (0, hf * T + t, 0))
    hlf = BS((tr, c), lambda hf, t, cr: (t, 0))
    return _pcall(body, name=name, grid=(2, T), prefetch=1, in_specs=[nat, nat, nat, hlf, hlf], out_specs=[nat] * 4,
                  out_shape=[SDS(w.shape, F32)] * 4)(cvec, w, m, v, g_own, g_other)


def _place():
    return lax.axis_index("x"), lax.axis_index("y"), lax.axis_index("c")


def _other_chips(x, y):
    return [(1 - x, y), (x, 1 - y), (1 - x, 1 - y)]


def _rows_of_core(c, half):
    return pl.ds(pl.multiple_of(c * half, 16), half)


def _rcopy(src, dst, sems, k, to):
    return pltpu.make_async_remote_copy(src_ref=src, dst_ref=dst, send_sem=sems[0].at[k], recv_sem=sems[1].at[k],
                                        device_id=to, device_id_type=MESH)


def _comm_call(body, *, name, out_shape, n_in, n_sems, aliases=None):
    return pl.pallas_call(body, name=name, out_shape=out_shape, in_specs=[ANY] * n_in, out_specs=[ANY] * len(out_shape),
                          scratch_shapes=[pltpu.SemaphoreType.DMA((n_sems,)), pltpu.SemaphoreType.DMA((n_sems,))],
                          input_output_aliases=aliases or {},
                          compiler_params=pltpu.CompilerParams(has_side_effects=True))


def cast_shards(shards, conv, chipvec):
    n = len(shards)

    def body(chip_ref, *refs):
        for i_ref, o_ref in zip(refs[:n + 1], refs[n + 1:]):
            o_ref[...] = i_ref[...].astype(o_ref.dtype)

    in_specs = [BS((s.shape[0] // 4, s.shape[1]), lambda i, p: (i, 0)) for s in shards]
    in_specs.append(BS(conv.shape, lambda i, p: (0, 0)))
    out_specs = [BS((None, s.shape[0] // 4, s.shape[1]), lambda i, p: (p[0], i, 0)) for s in shards]
    out_specs.append(BS((None,) + conv.shape, lambda i, p: (p[0], 0, 0)))
    out_shape = [SDS((N_CHIPS,) + s.shape, MXU_DTYPE) for s in shards] + [SDS((N_CHIPS,) + conv.shape, F32)]
    return _pcall(body, name="cast_shards", grid=(4,), prefetch=1, in_specs=in_specs, out_specs=out_specs,
                  out_shape=out_shape)(chipvec, *shards, conv)


HBM = pl.BlockSpec(memory_space=pltpu.HBM)
SEM = pl.BlockSpec(memory_space=pltpu.SEMAPHORE)
DATAFLOW = pltpu.SideEffectType.DATAFLOW_SIDE_EFFECTING
VMEM_WHOLE = pl.BlockSpec(memory_space=pltpu.VMEM)
TOKEN = jax.ShapeDtypeStruct((8, 128), jnp.float32)


def _gather_copies(bufs, send_sems, recv_sems, outgoing):
    x, y, c = _place()
    p = 2 * x + y
    cps = []
    for i, o in enumerate(bufs):
        for j, (cx, cy) in enumerate(_other_chips(x, y)):
            slot = o.at[p] if outgoing else o.at[2 * cx + cy]
            cps.append(_rcopy(slot, slot, (send_sems, recv_sems), 3 * i + j, (cx, cy, c)))
    return cps


def gather_start(slots, after):
    n = len(slots)

    def body(*refs):
        send_sems, recv_sems, thru, token = refs[n + 1], refs[n + 2], refs[n + 3:2 * n + 3], refs[2 * n + 3]
        for cp in _gather_copies(thru, send_sems, recv_sems, True):
            cp.start()
        token[...] = jnp.zeros_like(token)

    hbm = [pltpu.with_memory_space_constraint(s, pltpu.HBM) for s in slots]
    outs = pl.pallas_call(
        body, name="gather_start_%d" % n,
        out_shape=[pltpu.SemaphoreType.DMA((3 * n,)), pltpu.SemaphoreType.DMA((3 * n,))]
        + [pltpu.HBM(s.shape, s.dtype) for s in slots] + [TOKEN],
        in_specs=[HBM] * n + [ANY], out_specs=[SEM, SEM] + [HBM] * n + [VMEM_WHOLE],
        input_output_aliases={i: 2 + i for i in range(n)},
        compiler_params=pltpu.CompilerParams(has_side_effects=DATAFLOW))(*hbm, after)
    return outs[0], outs[1], outs[2:2 + n], outs[2 + n]


def gather_wait(send_sems, recv_sems, bufs, *after):
    n = len(bufs)

    def body(*refs):
        ins, send_ref, recv_ref = refs[:n], refs[n], refs[n + 1]
        for cp in _gather_copies(ins, send_ref, recv_ref, False):
            cp.wait_send()
            cp.wait_recv()

    return pl.pallas_call(
        body, name="gather_wait_%d" % n, out_shape=[pltpu.HBM(s.shape, s.dtype) for s in bufs],
        in_specs=[HBM] * n + [SEM, SEM] + [ANY] * len(after), out_specs=[HBM] * n,
        input_output_aliases={i: i for i in range(n)},
        compiler_params=pltpu.CompilerParams(has_side_effects=DATAFLOW))(*bufs, send_sems, recv_sems, *after)


def _peers(x, y, c):
    return [(1 - x if k & 4 else x, 1 - y if k & 2 else y, 1 - c if k & 1 else c) for k in range(1, N_DEV)]


def _partial_copies(g_ref, land_ref, send_sems, recv_sems, outgoing):
    x, y, c = _place()
    half = g_ref.shape[1] // 2
    cps = []
    for k, (px, py, pc) in enumerate(_peers(x, y, c)):
        src = g_ref.at[2 * px + py, _rows_of_core(pc, half)]
        dst = land_ref.at[4 * x + 2 * y + c] if outgoing else land_ref.at[4 * px + 2 * py + pc]
        cps.append(_rcopy(src, dst, (send_sems, recv_sems), k, (px, py, pc)))
    return cps


def partials_start(g, *, name):
    land = lax.empty((N_DEV, g.shape[1] // 2, g.shape[2]), g.dtype)

    def body(g_ref, land_ref, send_sems, recv_sems, g_thru, land_thru, token):
        for cp in _partial_copies(g_thru, land_thru, send_sems, recv_sems, True):
            cp.start()
        token[...] = jnp.zeros_like(token)

    return pl.pallas_call(
        body, name=name,
        out_shape=[pltpu.SemaphoreType.DMA((N_DEV - 1,)), pltpu.SemaphoreType.DMA((N_DEV - 1,)),
                   pltpu.HBM(g.shape, g.dtype), pltpu.HBM(land.shape, land.dtype), TOKEN],
        in_specs=[HBM, HBM], out_specs=[SEM, SEM, HBM, HBM, VMEM_WHOLE], input_output_aliases={0: 2, 1: 3},
        compiler_params=pltpu.CompilerParams(has_side_effects=DATAFLOW))(
        pltpu.with_memory_space_constraint(g, pltpu.HBM), pltpu.with_memory_space_constraint(land, pltpu.HBM))


def partials_wait(started, after):
    n = len(started)

    def body(*refs):
        for i in range(n):
            send_ref, recv_ref, g_ref, land_ref = refs[4 * i:4 * i + 4]
            for cp in _partial_copies(g_ref, land_ref, send_ref, recv_ref, False):
                cp.wait_send()
                cp.wait_recv()

    flat = [a for s in started for a in s]
    bufs = [a for s in started for a in s[2:]]
    outs = pl.pallas_call(
        body, name="partials_wait", out_shape=[pltpu.HBM(b.shape, b.dtype) for b in bufs],
        in_specs=[SEM, SEM, HBM, HBM] * n + [ANY], out_specs=[HBM] * (2 * n),
        input_output_aliases={4 * i + 2 + j: 2 * i + j for i in range(n) for j in range(2)},
        compiler_params=pltpu.CompilerParams(has_side_effects=DATAFLOW))(*flat, after)
    return [(outs[2 * i], outs[2 * i + 1]) for i in range(n)]


def sum_partials(pairs, order):
    n = len(pairs)

    def body(o_ref, *refs):
        j = pl.program_id(0)
        for g_ref, l_ref, f_ref in zip(refs[:n], refs[n:2 * n], refs[2 * n:]):
            @pl.when(j == 0)
            def _():
                f_ref[...] = g_ref[...].astype(F32)

            @pl.when(j > 0)
            def _():
                f_ref[...] += l_ref[...].astype(F32)

    g4 = [g.reshape(g.shape[0], 2, g.shape[1] // 2, g.shape[2]) for g, _ in pairs]
    lands = [l for _, l in pairs]
    return _pcall(body, name="sum_partials", grid=(N_DEV,), prefetch=1,
                  in_specs=[BS((None, None) + g.shape[2:], lambda j, o: (o[0], o[1], 0, 0)) for g in g4]
                  + [BS((None,) + l.shape[1:], lambda j, o: (o[jnp.maximum(j, 1) + 1], 0, 0)) for l in lands],
                  out_specs=[BS(l.shape[1:], lambda j, o: (0, 0)) for l in lands],
                  out_shape=[SDS(l.shape[1:], F32) for l in lands])(order, *g4, *lands)


def pair_share(fs):
    n = len(fs)

    def body(*refs):
        f_refs, o_refs, sems = refs[:n], refs[n:2 * n], refs[2 * n:]
        x, y, c = _place()
        cps = [_rcopy(f, o, sems, i, (x, y, 1 - c)) for i, (f, o) in enumerate(zip(f_refs, o_refs))]
        for cp in cps:
            cp.start()
        for cp in cps:
            cp.wait()

    return _comm_call(body, name="pair_share", n_in=n, n_sems=n, out_shape=[SDS(f.shape, f.dtype) for f in fs])(*fs)


def _small_copies(s_ref, land_ref, send_sems, recv_sems, outgoing):
    x, y, c = _place()
    cps = []
    for k, (px, py, pc) in enumerate(_peers(x, y, c)):
        dst = land_ref.at[4 * x + 2 * y + c] if outgoing else land_ref.at[4 * px + 2 * py + pc]
        cps.append(_rcopy(s_ref, dst, (send_sems, recv_sems), k, (px, py, pc)))
    return cps


def small_start(sm):
    land = lax.empty((N_DEV,) + sm.shape, sm.dtype)

    def body(s_ref, land_ref, send_sems, recv_sems, s_thru, land_thru):
        for cp in _small_copies(s_thru, land_thru, send_sems, recv_sems, True):
            cp.start()

    return pl.pallas_call(
        body, name="small_start",
        out_shape=[pltpu.SemaphoreType.DMA((N_DEV - 1,)), pltpu.SemaphoreType.DMA((N_DEV - 1,)),
                   pltpu.HBM(sm.shape, sm.dtype), pltpu.HBM(land.shape, land.dtype)],
        in_specs=[HBM, HBM], out_specs=[SEM, SEM, HBM, HBM], input_output_aliases={0: 2, 1: 3},
        compiler_params=pltpu.CompilerParams(has_side_effects=DATAFLOW))(
        pltpu.with_memory_space_constraint(sm, pltpu.HBM), pltpu.with_memory_space_constraint(land, pltpu.HBM))


def small_wait(send_sems, recv_sems, sm, land, after):
    def body(send_ref, recv_ref, s_ref, land_ref, after_ref, s_out, land_out):
        for cp in _small_copies(s_ref, land_ref, send_ref, recv_ref, False):
            cp.wait_send()
            cp.wait_recv()

    return pl.pallas_call(
        body, name="small_wait", out_shape=[pltpu.HBM(sm.shape, sm.dtype), pltpu.HBM(land.shape, land.dtype)],
        in_specs=[SEM, SEM, HBM, HBM, ANY], out_specs=[HBM, HBM], input_output_aliases={2: 0, 3: 1},
        compiler_params=pltpu.CompilerParams(has_side_effects=DATAFLOW))(send_sems, recv_sems, sm, land, after)


def sum_small(own, land, mevec):
    n, rows, width = land.shape
    tr = _tile(rows, (184, 8))

    def body(me_ref, own_ref, land_ref, o_ref):
        acc = jnp.zeros((tr, width), F32)
        for s in range(n):
            acc = acc + jnp.where(me_ref[0] == s, own_ref[...], land_ref[s])
        o_ref[...] = acc

    return _pcall(body, name="sum_small", grid=(rows // tr,), prefetch=1,
                  in_specs=[BS((tr, width), lambda i, me: (i, 0)), BS((n, tr, width), lambda i, me: (0, i, 0))],
                  out_specs=BS((tr, width), lambda i, me: (i, 0)), out_shape=SDS((rows, width), F32))(mevec, own, land)


def _to_full(blk, col):
    n, r, c = blk.shape
    return blk.transpose(1, 0, 2).reshape(r, n * c) if col else blk.reshape(n * r, c)


def _dup_cols(w):
    dup = lambda t: jnp.concatenate([t[:, :64], t[:, :64], t[:, 64:], t[:, 64:]], axis=1)
    return jnp.concatenate([w[:, :512], dup(w[:, 512:640]), dup(w[:, 640:768]), w[:, 768:]], axis=1)


def _fold_cols(d):
    fold = lambda t: jnp.concatenate([t[:, 0:64] + t[:, 64:128], t[:, 128:192] + t[:, 192:256]], axis=1)
    return jnp.concatenate([d[:, :512], fold(d[:, 512:768]), fold(d[:, 768:1024]), d[:, 1024:]], axis=1)


def _local_step(x, mem, positions, target, w_in, later, sp, emit):
    gain = lambda n: sp[n].reshape(1, -1)
    half = HEAD_DIM // 2
    inv_freq = 1.0 / (10000.0 ** (jnp.arange(half, dtype=F32) * (2.0 / HEAD_DIM)))
    ang = positions.astype(F32)[:, None] * inv_freq
    cos, sin = jnp.cos(ang), jnp.sin(ang)
    cos128 = jnp.tile(cos, (1, 4))
    sin128 = jnp.concatenate([-sin, sin, -sin, sin], axis=1)
    seg = jnp.arange(128) // HEAD_DIM
    bmat = (seg[:, None] == seg[None, :]).astype(BF16)
    gq128, gk128 = jnp.tile(gain("q_norm"), (1, 2)), jnp.tile(gain("k_norm"), (1, 2))
    sinkcol = jnp.repeat(sp["attn_sinks"].reshape(4, 2), BLK, axis=1).reshape(4, 2 * BLK, 1)
    wsc = sp["gmlp_ws"] * jnp.tril(jnp.ones((BLK, BLK), F32))[None]
    w2 = wsc.reshape(4, 2 * BLK, BLK).astype(MXU_DTYPE)
    w2t = wsc.swapaxes(1, 2).reshape(4, 2 * BLK, BLK).astype(MXU_DTYPE)
    bsl = jnp.repeat(sp["gmlp_bs"].reshape(4, 2, BLK).transpose(0, 2, 1), HEAD_DIM, axis=2)
    cb = sp["ffn_conv_b"].reshape(1, -1)
    w_in_d = _dup_cols(_to_full(w_in(cos128, sin128, gq128, gk128, sinkcol, w2, w2t, bsl), True))[None]

    h1, proj = rms_mm(x, gain("mix_norm"), w_in_d, name="mix_in")
    qr, kr, vb, gu, gvn, attn, gm, y = mixer_core_fwd(proj, cos128, sin128, gq128, gk128, gain("gmlp_v_norm"), bmat,
                                                      sinkcol, gain("attn_out_norm"), w2, bsl, gain("gmlp_out_norm"))
    wf, last = later(y)
    w_out, xa_wq, xa_wo = (_to_full(wf[n], False) for n in ("w_out", "xa_wq", "xa_wo"))
    x1 = mm(y, w_out, res=x, name="mix_out")
    mn, kv = rms_mm(mem, gain("mem_norm"), wf["xa_wkv"], name="xa_kv")
    kn, vbx = mem_pre(kv, gain("xa_k_norm"))
    h2, qx, xo, x2 = xattn_block_fwd(x1, gain("xa_norm"), xa_wq, kn, vbx, gain("xa_q_norm"), xa_wo)
    ffn_w, cw = last(x2)
    wf = {**wf, **ffn_w}
    ffn_down = _to_full(wf["ffn_down"], False)
    h3, a = rms_mm(x2, gain("ffn_norm"), wf["ffn_up"], name="ffn_up")
    f, dx3, loss_acc = convgate_down_loss(a, cw, cb, ffn_down, x2, target)

    by_rows = lambda g: g.reshape(N_CHIPS, g.shape[1] // N_CHIPS, g.shape[2])
    sent = emit("ffn_down", by_rows(mm_tn(f, dx3, name="g_ffn_down", out_dtype=WIRE_DTYPE)))
    dc, gcw = convgate_bwd(a, dx3, ffn_down[None], cw, cb, after=sent)
    da, dx2, dg_ffn = conv_transpose_rms_bwd(dc, cw, wf["ffn_up"], x2, gain("ffn_norm"), dx3)
    sent = emit("ffn_up", mm_tn(h3, da, name="g_ffn_up", out_dtype=WIRE_DTYPE, chunks=N_CHIPS))
    sent = emit("xa_wo", by_rows(mm_tn(xo, dx2, name="g_xa_wo", out_dtype=WIRE_DTYPE, after=sent)))
    dqx, dx1, dkn, dvx, dg_xq, dg_xa = xattn_block_bwd(dx2, xa_wo[None], qx, kn, vbx, gain("xa_q_norm"), xa_wq[None],
                                                       x1, gain("xa_norm"), after=sent)
    sent = emit("xa_wq", by_rows(mm_tn(h2, dqx, name="g_xa_wq", out_dtype=WIRE_DTYPE)))
    dkv, dg_xk = mem_bwd(kv, dkn, dvx, gain("xa_k_norm"), after=sent)
    _, dg_mem = mm_nt_rms_bwd(dkv, wf["xa_wkv"], mem, gain("mem_norm"), jnp.zeros_like(mem), name="d_mem")
    sent = emit("xa_wkv", mm_tn(mn, dkv, name="g_xa_wkv", out_dtype=WIRE_DTYPE, chunks=N_CHIPS))
    dattn, dgm, dg_y = mm_nt_post_bwd(dx1, w_out[None], attn, gm, gain("attn_out_norm"), gain("gmlp_out_norm"),
                                      name="d_mix_out", after=sent)
    sent = emit("w_out", by_rows(mm_tn(y, dx1, name="g_w_out", out_dtype=WIRE_DTYPE)))
    dproj, dsk, dws, dbl, dgq, dgk, dg_gvn = mixer_core_bwd(
        proj, cos128, sin128, gq128, gk128, gain("gmlp_v_norm"), bmat, qr, kr, vb, sinkcol, dattn, dgm, gvn, gu,
        w2, w2t, bsl, after=sent)
    g_in = _fold_cols(mm_tn(h1, dproj, name="g_w_in", out_dtype=F32)[0])
    sent = emit("w_in", g_in.reshape(1024, N_CHIPS, 448).transpose(1, 0, 2).astype(WIRE_DTYPE))
    grad_x, dg_mix = mm_nt_rms_bwd(dproj, w_in_d, x, gain("mix_norm"), dx1, name="d_x", after=sent)
    packed = pack_small(dg_mix, dgq, dgk, dsk, dg_gvn, dg_y, dg_xa, dg_mem, dg_xq, dg_xk, dg_ffn, gcw, dbl, dws)
    return loss_acc, grad_x, packed


def _gather_step(w, chipvec):
    slots = cast_shards([w[n][0] for n in BIG_NAMES], w["ffn_conv"][0], chipvec)
    send_a, recv_a, first, token = gather_start(slots[:1], chipvec)
    send_b, recv_b, mid, token = gather_start(slots[1:5], token)
    send_c, recv_c, rest, token = gather_start(slots[5:], token)

    def w_in(*after):
        return gather_wait(send_a, recv_a, first, token, *after)[0]

    def last(after):
        got = gather_wait(send_c, recv_c, rest, after)
        return dict(zip(BIG_NAMES[5:], got[:-1])), _to_full(got[-1], True)

    def later(after):
        return dict(zip(BIG_NAMES[1:5], gather_wait(send_b, recv_b, mid, after))), last

    return w_in, later, token


def _reduce_update(started, packed, w, m, v, chipvec, cvec, order):
    small_sent = small_start(packed)
    own = sum_partials(partials_wait([started[n] for n in BIG_NAMES], small_sent[2]), order)
    other = pair_share(own)
    res = [{}, {}, {}, {}]
    for n, g_own, g_other in zip(BIG_NAMES, own, other):
        for d, o in zip(res, adamw_matrix(w[n], m[n], v[n], g_own, g_other, cvec, name="adamw_" + n)):
            d[n] = o
    mevec = (2 * order[0:1] + order[1:2]).astype(jnp.int32)
    small_sum = sum_small(*small_wait(*small_sent, res[3][BIG_NAMES[-1]]), mevec)
    for d, outs in zip(res, adamw_small(small_sum, w, m, v, chipvec)):
        d.update(zip(SMALL, outs))
    return res


def kernel(x, mem, positions, mix_norm, w_in, q_norm, k_norm, attn_sinks, gmlp_v_norm, gmlp_ws, gmlp_bs, attn_out_norm, gmlp_out_norm, w_out, xa_norm, mem_norm, xa_wq, xa_wkv, xa_q_norm, xa_k_norm, xa_wo, ffn_norm, ffn_up, ffn_conv, ffn_conv_b, ffn_down, loss_target, m_mix_norm, m_w_in, m_q_norm, m_k_norm, m_attn_sinks, m_gmlp_v_norm, m_gmlp_ws, m_gmlp_bs, m_attn_out_norm, m_gmlp_out_norm, m_w_out, m_xa_norm, m_mem_norm, m_xa_wq, m_xa_wkv, m_xa_q_norm, m_xa_k_norm, m_xa_wo, m_ffn_norm, m_ffn_up, m_ffn_conv, m_ffn_conv_b, m_ffn_down, v_mix_norm, v_w_in, v_q_norm, v_k_norm, v_attn_sinks, v_gmlp_v_norm, v_gmlp_ws, v_gmlp_bs, v_attn_out_norm, v_gmlp_out_norm, v_w_out, v_xa_norm, v_mem_norm, v_xa_wq, v_xa_wkv, v_xa_q_norm, v_xa_k_norm, v_xa_wo, v_ffn_norm, v_ffn_up, v_ffn_conv, v_ffn_conv_b, v_ffn_down):
    w = dict(mix_norm=mix_norm, w_in=w_in, q_norm=q_norm, k_norm=k_norm, attn_sinks=attn_sinks, gmlp_v_norm=gmlp_v_norm, gmlp_ws=gmlp_ws, gmlp_bs=gmlp_bs, attn_out_norm=attn_out_norm, gmlp_out_norm=gmlp_out_norm, w_out=w_out, xa_norm=xa_norm, mem_norm=mem_norm, xa_wq=xa_wq, xa_wkv=xa_wkv, xa_q_norm=xa_q_norm, xa_k_norm=xa_k_norm, xa_wo=xa_wo, ffn_norm=ffn_norm, ffn_up=ffn_up, ffn_conv=ffn_conv, ffn_conv_b=ffn_conv_b, ffn_down=ffn_down)
    m = dict(mix_norm=m_mix_norm, w_in=m_w_in, q_norm=m_q_norm, k_norm=m_k_norm, attn_sinks=m_attn_sinks, gmlp_v_norm=m_gmlp_v_norm, gmlp_ws=m_gmlp_ws, gmlp_bs=m_gmlp_bs, attn_out_norm=m_attn_out_norm, gmlp_out_norm=m_gmlp_out_norm, w_out=m_w_out, xa_norm=m_xa_norm, mem_norm=m_mem_norm, xa_wq=m_xa_wq, xa_wkv=m_xa_wkv, xa_q_norm=m_xa_q_norm, xa_k_norm=m_xa_k_norm, xa_wo=m_xa_wo, ffn_norm=m_ffn_norm, ffn_up=m_ffn_up, ffn_conv=m_ffn_conv, ffn_conv_b=m_ffn_conv_b, ffn_down=m_ffn_down)
    v = dict(mix_norm=v_mix_norm, w_in=v_w_in, q_norm=v_q_norm, k_norm=v_k_norm, attn_sinks=v_attn_sinks, gmlp_v_norm=v_gmlp_v_norm, gmlp_ws=v_gmlp_ws, gmlp_bs=v_gmlp_bs, attn_out_norm=v_attn_out_norm, gmlp_out_norm=v_gmlp_out_norm, w_out=v_w_out, xa_norm=v_xa_norm, mem_norm=v_mem_norm, xa_wq=v_xa_wq, xa_wkv=v_xa_wkv, xa_q_norm=v_xa_q_norm, xa_k_norm=v_xa_k_norm, xa_wo=v_xa_wo, ffn_norm=v_ffn_norm, ffn_up=v_ffn_up, ffn_conv=v_ffn_conv, ffn_conv_b=v_ffn_conv_b, ffn_down=v_ffn_down)
    ix, iy, ic = lax.axis_index("x"), lax.axis_index("y"), lax.axis_index("c")
    chip = 2 * ix + iy
    chipvec = chip.astype(jnp.int32).reshape(1)
    cvec = ic.astype(jnp.int32).reshape(1)
    order = jnp.stack([chip, ic] + [4 * px + 2 * py + pc for px, py, pc in _peers(ix, iy, ic)]).astype(jnp.int32)

    w_in_all, later, token = _gather_step(w, chipvec)
    zero = token[0, 0]
    sp = {n: w[n][0] + zero for n in SMALL if n != "ffn_conv"}
    positions = positions + zero.astype(jnp.int32)
    started = {}

    def emit(name, g):
        *started[name], token = partials_start(g, name="partials_start_" + name)
        return token

    loss_acc, grad_x, packed = _local_step(x[0], mem[0], positions[0], loss_target[0], w_in_all, later, sp, emit)
    grads, delta, new_m, new_v = _reduce_update(started, packed, w, m, v, chipvec, cvec, order)
    loss = lax.psum(loss_acc[0, 0], ("x", "y", "c"))
    ordered = lambda d: [d[n] for n in WEIGHTS]
    return (loss, grad_x[None], *ordered(grads), *ordered(delta), *ordered(new_m), *ordered(new_v))
```

```python
import math

import jax
import jax.numpy as jnp
from jax import lax
from jax.experimental import pallas as pl
from jax.experimental.pallas import tpu as pltpu

F32 = jnp.float32
BF16 = jnp.bfloat16
MXU_DTYPE = jnp.bfloat16
WIRE_DTYPE = jnp.bfloat16
EPS = 1e-6
VMEM_LIMIT_V7X = 56 * 1024 * 1024

D_MODEL = 1024
HEAD_DIM = 64
BLK = 128
XA_HEADS = 4
XA_DH = 256
MEM_LEN = 256
D_FF = 2816
IN_COLS_DUP = 2048
N_CHIPS = 4
N_DEV = 8

ADAM_LR = 0.001
ADAM_B1 = 0.9
ADAM_B2 = 0.999
ADAM_EPS = 1e-08
ADAM_WD = 0.01
ADAM_STEP = 10

NT = (((1,), (1,)), ((), ()))
TN = (((0,), (0,)), ((), ()))
NN = (((1,), (0,)), ((), ()))
MINF = float(jnp.finfo(jnp.float32).min)
GELU_K0 = math.sqrt(2.0 / math.pi)
GELU_K1 = 0.044715

BS = pl.BlockSpec
SDS = jax.ShapeDtypeStruct
ANY = pl.BlockSpec(memory_space=pl.ANY)
MESH = pl.DeviceIdType.MESH


def _dot(a, b, dims=NN):
    return lax.dot_general(a.astype(MXU_DTYPE), b.astype(MXU_DTYPE), dims, preferred_element_type=F32)


def _segsum(x, bmat):
    hi = x.astype(BF16)
    lo = (x - hi.astype(F32)).astype(BF16)
    return (jnp.dot(hi, bmat, preferred_element_type=F32) + jnp.dot(lo, bmat, preferred_element_type=F32))


def _gelu(x):
    return 0.5 * x * (1.0 + jnp.tanh(GELU_K0 * (x + GELU_K1 * x * x * x)))


def _gelu_grad(x):
    t = jnp.tanh(GELU_K0 * (x + GELU_K1 * x * x * x))
    return 0.5 * (1.0 + t) + 0.5 * x * (1.0 - t * t) * GELU_K0 * (1.0 + 3.0 * GELU_K1 * x * x)


def _rms(x):
    return lax.rsqrt(jnp.mean(x * x, axis=-1, keepdims=True) + EPS)


def _rms_bwd(dy, x, g, r):
    dyg = dy * g
    dx = r * dyg - x * (r * r * r) * jnp.mean(dyg * x, axis=-1, keepdims=True)
    return dx, dy * x * r


def _pcall(body, *, name, grid, in_specs, out_specs, out_shape, scratch=(), prefetch=0, after=None):
    params = pltpu.CompilerParams(dimension_semantics=("arbitrary",) * len(grid), vmem_limit_bytes=VMEM_LIMIT_V7X)
    in_specs = list(in_specs)
    kernel_fn = body
    if after is not None:
        n_in = prefetch + len(in_specs)
        in_specs.append(ANY)

        def kernel_fn(*refs):
            return body(*refs[:n_in], *refs[n_in + 1:])

    if prefetch:
        spec = pltpu.PrefetchScalarGridSpec(num_scalar_prefetch=prefetch, grid=grid, in_specs=in_specs,
                                            out_specs=out_specs, scratch_shapes=list(scratch))
        call = pl.pallas_call(kernel_fn, name=name, grid_spec=spec, out_shape=out_shape, compiler_params=params)
    else:
        call = pl.pallas_call(kernel_fn, name=name, grid=grid, in_specs=in_specs, out_specs=out_specs,
                              out_shape=out_shape, scratch_shapes=list(scratch), compiler_params=params)
    return call if after is None else (lambda *args: call(*args, after))


def _tile(n, prefs):
    for p in prefs:
        if p <= n and n % p == 0:
            return p
    return n


def _resident(shape):
    return pl.BlockSpec(shape, lambda *_: (0,) * len(shape), pipeline_mode=pl.Buffered(1))


def _acc_rows(ref, row, val):
    ref[row:row + 1, :] += jnp.sum(val, axis=0, keepdims=True)


def rms_mm(x, g, w3, *, name, tm=1024):
    M, K = x.shape
    Q, _, C = w3.shape
    tm = _tile(M, (tm, 256))

    def body(x_ref, g_ref, w_ref, h_ref, o_ref):
        def write_h():
            xv = x_ref[...]
            h_ref[...] = (xv * _rms(xv) * g_ref[...]).astype(h_ref.dtype)

        if Q == 1:
            write_h()
        else:
            pl.when(pl.program_id(1) == 0)(write_h)
        o_ref[...] = _dot(h_ref[...], w_ref[pl.program_id(1)])

    return _pcall(body, name=name, grid=(M // tm, Q),
                  in_specs=[BS((tm, K), lambda i, j: (i, 0)), BS((1, K), lambda i, j: (0, 0)),
                            _resident((Q, K, C))],
                  out_specs=[BS((tm, K), lambda i, j: (i, 0)), BS((tm, C), lambda i, j: (i, j))],
                  out_shape=[SDS((M, K), MXU_DTYPE), SDS((M, Q * C), F32)])(x, g, w3)


def mm(a, w, *, name, res):
    M, K = a.shape
    N = w.shape[1]
    tm = _tile(M, (1024, 256))

    def body(a_ref, w_ref, r_ref, o_ref):
        o_ref[...] = _dot(a_ref[...], w_ref[...]) + r_ref[...]

    return _pcall(body, name=name, grid=(M // tm,),
                  in_specs=[BS((tm, K), lambda i: (i, 0)), _resident((K, N)), BS((tm, N), lambda i: (i, 0))],
                  out_specs=BS((tm, N), lambda i: (i, 0)), out_shape=SDS((M, N), F32))(a, w, res)


def _nt_chunks(a_ref, w_ref):
    q_n, _, kc = w_ref.shape
    acc = _dot(a_ref[:, 0:kc], w_ref[0], NT)
    for q in range(1, q_n):
        acc = acc + _dot(a_ref[:, q * kc:(q + 1) * kc], w_ref[q], NT)
    return acc


def mm_nt_rms_bwd(a, w3, x, g, dres, *, name, tm=512, after=None):
    M = a.shape[0]
    Q, N, Kc = w3.shape
    tm = _tile(M, (tm, 256))

    def body(a_ref, w_ref, x_ref, g_ref, dr_ref, dx_ref, dg_ref):
        @pl.when(pl.program_id(0) == 0)
        def _():
            dg_ref[...] = jnp.zeros_like(dg_ref)

        xv = x_ref[...]
        dx, dgc = _rms_bwd(_nt_chunks(a_ref, w_ref), xv, g_ref[...], _rms(xv))
        dx_ref[...] = dr_ref[...] + dx
        _acc_rows(dg_ref, 0, dgc)

    row = BS((tm, N), lambda i: (i, 0))
    return _pcall(body, name=name, grid=(M // tm,), after=after,
                  in_specs=[BS((tm, Q * Kc), lambda i: (i, 0)), _resident((Q, N, Kc)), row,
                            BS((1, N), lambda i: (0, 0)), row],
                  out_specs=[row, BS((8, N), lambda i: (0, 0))],
                  out_shape=[SDS((M, N), F32), SDS((8, N), F32)])(a, w3, x, g, dres)


def mm_nt_post_bwd(a, w3, attn, gm, gao, ggo, *, name, after=None):
    M = a.shape[0]
    Q, N, Kc = w3.shape
    tm = _tile(M, (512, 256))
    hw = N // 2

    def body(a_ref, w_ref, at_ref, gm_ref, gao_ref, ggo_ref, da_ref, dgm_ref, dg_ref):
        @pl.when(pl.program_id(0) == 0)
        def _():
            dg_ref[...] = jnp.zeros_like(dg_ref)

        dy = _nt_chunks(a_ref, w_ref)
        av, gmv = at_ref[...], gm_ref[...]
        da, dga = _rms_bwd(dy[:, :hw], av, gao_ref[...], _rms(av))
        dgm, dgg = _rms_bwd(dy[:, hw:], gmv, ggo_ref[...], _rms(gmv))
        da_ref[...] = da
        dgm_ref[...] = dgm
        dg_ref[0:1, :hw] += jnp.sum(dga, axis=0, keepdims=True)
        dg_ref[0:1, hw:] += jnp.sum(dgg, axis=0, keepdims=True)

    half = BS((tm, hw), lambda i: (i, 0))
    const = lambda r, w: BS((r, w), lambda i: (0, 0))
    return _pcall(body, name=name, grid=(M // tm,), after=after,
                  in_specs=[BS((tm, Q * Kc), lambda i: (i, 0)), _resident((Q, N, Kc)), half, half,
                            const(1, hw), const(1, hw)],
                  out_specs=[half, half, const(8, N)],
                  out_shape=[SDS((M, hw), F32), SDS((M, hw), F32), SDS((8, N), F32)])(a, w3, attn, gm, gao, ggo)


def mm_tn(a, b, *, name, out_dtype, chunks=1, after=None):
    M, K = a.shape
    N = b.shape[1]
    C = N // chunks
    tm = _tile(M, (1024, 256))
    tk = _tile(K, (1408, 1024, 512))
    tn = _tile(C, (1408, 1024, 512))
    per = C // tn
    nm = M // tm

    def body(a_ref, b_ref, o_ref, acc):
        m = pl.program_id(2)

        @pl.when(m == 0)
        def _():
            acc[...] = jnp.zeros_like(acc)

        acc[...] += _dot(a_ref[...], b_ref[...], TN)

        @pl.when(m == nm - 1)
        def _():
            o_ref[...] = acc[...].astype(o_ref.dtype)

    return _pcall(body, name=name, grid=(K // tk, N // tn, nm), after=after,
                  in_specs=[BS((tm, tk), lambda k, n, m: (m, k)), BS((tm, tn), lambda k, n, m: (m, n))],
                  out_specs=BS((None, tk, tn), lambda k, n, m: (n // per, k, n % per)),
                  out_shape=SDS((chunks, K, C), out_dtype), scratch=[pltpu.VMEM((tk, tn), F32)])(a, b)


def _lane(shape):
    return lax.broadcasted_iota(jnp.int32, shape, 1)


def _head_means(slabs, bmat):
    tm = slabs[0].shape[0]
    means = _segsum(jnp.concatenate(slabs, axis=0), bmat) * (1.0 / HEAD_DIM)
    return [means[i * tm:(i + 1) * tm] for i in range(len(slabs))]


def _half_swap(x, first):
    return jnp.where(first, pltpu.roll(x, 96, 1), pltpu.roll(x, 32, 1))


def _by_head(x2, lo):
    z = jnp.zeros((BLK, 128), x2.dtype)
    parts = []
    for s in range(2):
        xs = x2[:, s * 128:(s + 1) * 128]
        parts += [jnp.where(lo, xs, z), jnp.where(lo, z, xs)]
    return jnp.concatenate(parts, axis=0)


def _from_heads(o4, lo):
    return jnp.concatenate([jnp.where(lo, o4[0:BLK], o4[BLK:2 * BLK]),
                            jnp.where(lo, o4[2 * BLK:3 * BLK], o4[3 * BLK:])], axis=1)


def _swa_probs(q2, kd, sink, n, lo):
    qp = _by_head(q2, lo)
    sc = _dot(qp, kd, NT) * (1.0 / math.sqrt(HEAD_DIM))
    r_i = lax.broadcasted_iota(jnp.int32, (4 * BLK, 2 * BLK), 0)
    k_j = lax.broadcasted_iota(jnp.int32, (4 * BLK, 2 * BLK), 1)
    diff = (r_i & (BLK - 1)) + BLK - k_j
    mask = (diff >= 0) & (diff < BLK) & ((k_j >= BLK) | (n > 0))
    sc = jnp.where(mask, sc, MINF)
    m = jnp.maximum(jnp.max(sc, axis=1, keepdims=True), sink)
    p = jnp.exp(sc - m)
    es = jnp.exp(sink - m)
    inv = 1.0 / (jnp.sum(p, axis=1, keepdims=True) + es)
    return qp, p * inv, es * inv


def mixer_core_fwd(proj, cos, sin, gq, gk, gvn, bmat, sinkcol, gao, w2, bsl, ggo):
    S = proj.shape[0]

    def body(p_ref, c_ref, s_ref, gq_ref, gk_ref, gvn_ref, b_ref, sk_ref, gao_ref, w2_ref, bsl_ref, ggo_ref,
             qr_ref, kr_ref, vb_ref, gu_ref, gvo_ref, at_ref, gm_ref, y_ref, k_prev, v_prev):
        n = pl.program_id(0)

        @pl.when(n == 0)
        def _():
            k_prev[...] = jnp.zeros_like(k_prev)
            v_prev[...] = jnp.zeros_like(v_prev)

        cos_v, sin_v, bm = c_ref[...], s_ref[...], b_ref[...]
        first = (_lane((BLK, 128)) & 63) < 32
        lo = _lane((BLK, 128)) < 64
        slabs = [p_ref[:, s * 128:(s + 1) * 128] for s in range(6)]
        for s, (slab, ms) in enumerate(zip(slabs, _head_means([x * x for x in slabs], bm))):
            qn = slab * lax.rsqrt(ms + EPS) * (gq_ref[...] if s < 4 else gk_ref[...])
            out = qn * cos_v + _half_swap(qn, first) * sin_v
            if s < 4:
                qr_ref[:, s * 128:(s + 1) * 128] = out.astype(qr_ref.dtype)
            else:
                kr_ref[:, (s - 4) * 128:(s - 3) * 128] = out.astype(kr_ref.dtype)
        vb_ref[...] = p_ref[:, 768:1024].astype(vb_ref.dtype)
        gu_ref[...] = _gelu(p_ref[:, 1024:1536])
        gv = _gelu(p_ref[:, 1536:2048])
        gvo_ref[...] = (gv * _rms(gv) * gvn_ref[...]).astype(gvo_ref.dtype)

        for h in range(2):
            hs, qs = slice(h * 128, (h + 1) * 128), slice(h * 256, (h + 1) * 256)
            kd = jnp.concatenate([k_prev[:, hs], kr_ref[:, hs]], axis=0)
            vd = jnp.concatenate([v_prev[:, hs], vb_ref[:, hs]], axis=0)
            sink = jnp.concatenate([sk_ref[2 * h], sk_ref[2 * h + 1]], axis=0)
            _, p, _ = _swa_probs(qr_ref[:, qs], kd, sink, n, lo)
            at_ref[:, qs] = _from_heads(_dot(p, vd), lo)
        k_prev[...] = kr_ref[...]
        v_prev[...] = vb_ref[...]

        for j in range(4):
            sl = slice(j * 128, (j + 1) * 128)
            m2 = _dot(w2_ref[j], gvo_ref[:, sl])
            mixed = jnp.where(lo, m2[:BLK], m2[BLK:]) + bsl_ref[j]
            gm_ref[:, sl] = gu_ref[:, sl] * mixed
        a, gm = at_ref[...], gm_ref[...]
        y_ref[:, :512] = (a * _rms(a) * gao_ref[...]).astype(y_ref.dtype)
        y_ref[:, 512:] = (gm * _rms(gm) * ggo_ref[...]).astype(y_ref.dtype)

    row = lambda w: BS((BLK, w), lambda n: (n, 0))
    const = lambda *shape: BS(shape, lambda n: (0,) * len(shape))
    return _pcall(body, name="mixer_core_fwd", grid=(S // BLK,),
                  in_specs=[row(IN_COLS_DUP), row(128), row(128), const(1, 128), const(1, 128), const(1, 512),
                            const(128, 128), const(4, 2 * BLK, 1), const(1, 512), const(4, 2 * BLK, BLK),
                            const(4, BLK, 128), const(1, 512)],
                  out_specs=[row(512), row(256), row(256), row(512), row(512), row(512), row(512), row(1024)],
                  out_shape=[SDS((S, 512), MXU_DTYPE), SDS((S, 256), MXU_DTYPE), SDS((S, 256), MXU_DTYPE),
                             SDS((S, 512), F32), SDS((S, 512), MXU_DTYPE), SDS((S, 512), F32), SDS((S, 512), F32),
                             SDS((S, 1024), MXU_DTYPE)],
                  scratch=[pltpu.VMEM((BLK, 256), MXU_DTYPE), pltpu.VMEM((BLK, 256), MXU_DTYPE)])(
        proj, cos, sin, gq, gk, gvn, bmat, sinkcol, gao, w2, bsl, ggo)


def mem_pre(kv, gxk):
    def body(kv_ref, g_ref, kn_ref, vb_ref):
        for h in range(XA_HEADS):
            sl = slice(h * XA_DH, (h + 1) * XA_DH)
            k = kv_ref[:, sl]
            kn_ref[:, sl] = (k * _rms(k) * g_ref[...]).astype(kn_ref.dtype)
        vb_ref[...] = kv_ref[:, 1024:2048].astype(vb_ref.dtype)

    full = lambda r, w: BS((r, w), lambda i: (0, 0))
    return _pcall(body, name="mem_pre", grid=(1,), in_specs=[full(MEM_LEN, 2048), full(1, XA_DH)],
                  out_specs=[full(MEM_LEN, 1024), full(MEM_LEN, 1024)],
                  out_shape=[SDS((MEM_LEN, 1024), MXU_DTYPE), SDS((MEM_LEN, 1024), MXU_DTYPE)])(kv, gxk)


def _xa_probs(qh, g, kn_h):
    r = _rms(qh)
    qn = qh * r * g
    s = _dot(qn, kn_h, NT) * (1.0 / math.sqrt(XA_DH))
    p = jnp.exp(s - jnp.max(s, axis=1, keepdims=True))
    return r, qn, p * (1.0 / jnp.sum(p, axis=1, keepdims=True))


def xattn_block_fwd(x1, g, wq, kn, vb, gxq, wo):
    S, D = x1.shape
    tm = _tile(S, (512, 256))

    def body(x_ref, g_ref, wq_ref, kn_ref, vb_ref, gxq_ref, wo_ref, h_ref, q_ref, o_ref, x2_ref):
        xv = x_ref[...]
        h_ref[...] = (xv * _rms(xv) * g_ref[...]).astype(h_ref.dtype)
        q_ref[...] = _dot(h_ref[...], wq_ref[...])
        for h in range(XA_HEADS):
            sl = slice(h * XA_DH, (h + 1) * XA_DH)
            _, _, p = _xa_probs(q_ref[:, sl], gxq_ref[...], kn_ref[:, sl])
            o_ref[:, sl] = _dot(p, vb_ref[:, sl]).astype(o_ref.dtype)
        x2_ref[...] = _dot(o_ref[...], wo_ref[...]) + xv

    row = BS((tm, D), lambda i: (i, 0))
    full = lambda r, w: BS((r, w), lambda i: (0, 0))
    return _pcall(body, name="xattn_block_fwd", grid=(S // tm,),
                  in_specs=[row, full(1, D), _resident(wq.shape), full(MEM_LEN, D), full(MEM_LEN, D), full(1, XA_DH),
                            _resident(wo.shape)],
                  out_specs=[row, row, row, row],
                  out_shape=[SDS((S, D), MXU_DTYPE), SDS((S, D), F32), SDS((S, D), MXU_DTYPE), SDS((S, D), F32)])(
        x1, g, wq, kn, vb, gxq, wo)


CONV_COLS = 1408


def _conv_taps(a_ref, halo_ref, w_ref, b_ref, cols, first_tile):
    a = a_ref[:, cols]
    row = lax.broadcasted_iota(jnp.int32, a.shape, 0)
    h6 = jnp.where(first_tile, 0.0, halo_ref[6:7, cols])
    h7 = jnp.where(first_tile, 0.0, halo_ref[7:8, cols])
    a1 = jnp.where(row == 0, h7, pltpu.roll(a, 1, 0))
    a2 = jnp.where(row == 0, h6, jnp.where(row == 1, h7, pltpu.roll(a, 2, 0)))
    c = w_ref[2:3, cols] * a + w_ref[1:2, cols] * a1 + w_ref[0:1, cols] * a2 + b_ref[:, cols]
    return c, (a2, a1, a)


def _conv_specs(tm):
    halo_blocks = tm // 8
    return [BS((tm, D_FF), lambda i: (i, 0)), BS((tm, D_FF), lambda i: (i, 1)),
            BS((8, D_FF), lambda i: (jnp.maximum(i * halo_blocks - 1, 0), 0)),
            BS((8, D_FF), lambda i: (jnp.maximum(i * halo_blocks - 1, 0), 1)),
            BS((3, D_FF), lambda i: (0, 0)), BS((3, D_FF), lambda i: (0, 1)),
            BS((1, D_FF), lambda i: (0, 0)), BS((1, D_FF), lambda i: (0, 1))]


def ffn_fwd_loss(x2, g, w_up3, cw, cb, w_down, target):
    S, D = x2.shape
    Q, _, C = w_up3.shape
    tm = _tile(S, (256,))

    def body(x_ref, g_ref, wu_ref, cw_ref, cb_ref, wd_ref, t_ref, h_ref, a_ref, f_ref, d_ref, l_ref, tail):
        first_tile = pl.program_id(0) == 0

        @pl.when(first_tile)
        def _():
            l_ref[...] = jnp.zeros_like(l_ref)
            tail[...] = jnp.zeros_like(tail)

        xv = x_ref[...]
        h_ref[...] = (xv * _rms(xv) * g_ref[...]).astype(h_ref.dtype)
        for q in range(Q):
            a_ref[:, q * C:(q + 1) * C] = _dot(h_ref[...], wu_ref[q])
        for c0 in range(0, D_FF, CONV_COLS):
            cols, ucols = slice(c0, c0 + CONV_COLS), slice(D_FF + c0, D_FF + c0 + CONV_COLS)
            cg, _ = _conv_taps(a_ref, tail, cw_ref, cb_ref, cols, first_tile)
            cu, _ = _conv_taps(a_ref, tail, cw_ref, cb_ref, ucols, first_tile)
            f_ref[:, cols] = (_gelu(cg) * cu).astype(f_ref.dtype)
        tail[...] = a_ref[tm - 8:tm, :]
        e = _dot(f_ref[...], wd_ref[...]) + xv - t_ref[...]
        d_ref[...] = e * (1.0 / D)
        l_ref[...] += jnp.sum(e * e) * (0.5 / D)

    row = lambda w: BS((tm, w), lambda i: (i, 0))
    const = lambda r, w: BS((r, w), lambda i: (0, 0))
    return _pcall(body, name="ffn_fwd_loss", grid=(S // tm,),
                  in_specs=[row(D), const(1, D), _resident(w_up3.shape), const(3, 2 * D_FF), const(1, 2 * D_FF),
                            _resident(w_down.shape), row(D)],
                  out_specs=[row(D), row(2 * D_FF), row(D_FF), row(D), const(8, 128)],
                  out_shape=[SDS((S, D), MXU_DTYPE), SDS((S, 2 * D_FF), F32), SDS((S, D_FF), MXU_DTYPE),
                             SDS((S, D), F32), SDS((8, 128), F32)],
                  scratch=[pltpu.VMEM((8, 2 * D_FF), F32)])(x2, g, w_up3, cw, cb, w_down, target)


def convgate_bwd(a, dx3, w3, cw, cb, after=None):
    S = a.shape[0]
    tm = _tile(S, (256,))

    def body(ag_ref, au_ref, hg_ref, hu_ref, wg_ref, wu_ref, bg_ref, bu_ref, dx_ref, wd_ref, dc_ref, gw_ref, df_ref):
        first_tile = pl.program_id(0) == 0

        @pl.when(first_tile)
        def _():
            gw_ref[...] = jnp.zeros_like(gw_ref)

        df_ref[...] = _nt_chunks(dx_ref, wd_ref)
        for c0 in range(0, D_FF, CONV_COLS):
            cols, ucols = slice(c0, c0 + CONV_COLS), slice(D_FF + c0, D_FF + c0 + CONV_COLS)
            cg, g_taps = _conv_taps(ag_ref, hg_ref, wg_ref, bg_ref, cols, first_tile)
            cu, u_taps = _conv_taps(au_ref, hu_ref, wu_ref, bu_ref, cols, first_tile)
            df_v = df_ref[:, cols]
            dcg = df_v * cu * _gelu_grad(cg)
            dcu = df_v * _gelu(cg)
            dc_ref[:, cols] = dcg
            dc_ref[:, ucols] = dcu
            for col, dcv, taps in ((cols, dcg, g_taps), (ucols, dcu, u_taps)):
                for j in range(3):
                    gw_ref[j:j + 1, col] += jnp.sum(dcv * taps[j], axis=0, keepdims=True)
                gw_ref[3:4, col] += jnp.sum(dcv, axis=0, keepdims=True)

    return _pcall(body, name="convgate_bwd", grid=(S // tm,), after=after,
                  in_specs=_conv_specs(tm) + [BS((tm, dx3.shape[1]), lambda i: (i, 0)), _resident(w3.shape)],
                  out_specs=[BS((tm, 2 * D_FF), lambda i: (i, 0)), BS((8, 2 * D_FF), lambda i: (0, 0))],
                  out_shape=[SDS((S, 2 * D_FF), F32), SDS((8, 2 * D_FF), F32)],
                  scratch=[pltpu.VMEM((tm, D_FF), F32)])(a, a, a, a, cw, cw, cb, cb, dx3, w3)


def conv_transpose_rms_bwd(dc, cw, w3, x, g, dres):
    S, C = dc.shape
    Q, N, Kc = w3.shape
    tm = _tile(S, (256,))
    nt = S // tm
    halo_blocks = tm // 8

    def body(dc_ref, halo_ref, cw_ref, w_ref, x_ref, g_ref, dr_ref, da_ref, dx_ref, dg_ref):
        @pl.when(pl.program_id(0) == 0)
        def _():
            dg_ref[...] = jnp.zeros_like(dg_ref)

        last_tile = pl.program_id(0) == nt - 1
        row = lax.broadcasted_iota(jnp.int32, (tm, CONV_COLS), 0)
        for c0 in range(0, C, CONV_COLS):
            cols = slice(c0, c0 + CONV_COLS)
            h0 = jnp.where(last_tile, 0.0, halo_ref[0:1, cols])
            h1 = jnp.where(last_tile, 0.0, halo_ref[1:2, cols])
            dc_v = dc_ref[:, cols]
            n1 = jnp.where(row == tm - 1, h0, pltpu.roll(dc_v, tm - 1, 0))
            n2 = jnp.where(row == tm - 1, h1, jnp.where(row == tm - 2, h0, pltpu.roll(dc_v, tm - 2, 0)))
            da_ref[:, cols] = (cw_ref[2:3, cols] * dc_v + cw_ref[1:2, cols] * n1
                               + cw_ref[0:1, cols] * n2).astype(da_ref.dtype)
        xv = x_ref[...]
        dx, dgc = _rms_bwd(_nt_chunks(da_ref, w_ref), xv, g_ref[...], _rms(xv))
        dx_ref[...] = dr_ref[...] + dx
        _acc_rows(dg_ref, 0, dgc)

    row_n = BS((tm, N), lambda i: (i, 0))
    return _pcall(body, name="conv_transpose_rms_bwd", grid=(nt,),
                  in_specs=[BS((tm, C), lambda i: (i, 0)),
                            BS((8, C), lambda i: (jnp.minimum((i + 1) * halo_blocks, S // 8 - 1), 0)),
                            BS((3, C), lambda i: (0, 0)), _resident((Q, N, Kc)), row_n, BS((1, N), lambda i: (0, 0)),
                            row_n],
                  out_specs=[BS((tm, C), lambda i: (i, 0)), row_n, BS((8, N), lambda i: (0, 0))],
                  out_shape=[SDS((S, C), MXU_DTYPE), SDS((S, N), F32), SDS((8, N), F32)])(dc, dc, cw, w3, x, g, dres)


def xattn_block_bwd(dx2, wo3, qx, kn, vb, gxq, wq3, x1, g, after=None):
    S, D = qx.shape
    tm = _tile(S, (512, 256))

    def body(dx2_ref, wo_ref, q_ref, kn_ref, vb_ref, gxq_ref, wq_ref, x_ref, g_ref,
             dq_ref, dx_ref, dkn_ref, dv_ref, dgq_ref, dg_ref):
        @pl.when(pl.program_id(0) == 0)
        def _():
            for ref in (dkn_ref, dv_ref, dgq_ref, dg_ref):
                ref[...] = jnp.zeros_like(ref)

        gq = gxq_ref[...]
        do_all = _nt_chunks(dx2_ref, wo_ref)
        for h in range(XA_HEADS):
            sl = slice(h * XA_DH, (h + 1) * XA_DH)
            qh, do = q_ref[:, sl], do_all[:, sl]
            r, qn, p = _xa_probs(qh, gq, kn_ref[:, sl])
            dp = _dot(do, vb_ref[:, sl], NT)
            ds = p * (dp - jnp.sum(dp * p, axis=1, keepdims=True)) * (1.0 / math.sqrt(XA_DH))
            dqn = _dot(ds, kn_ref[:, sl])
            dkn_ref[:, sl] += _dot(ds, qn, TN)
            dv_ref[:, sl] += _dot(p, do, TN)
            dqh, dgc = _rms_bwd(dqn, qh, gq, r)
            dq_ref[:, sl] = dqh.astype(dq_ref.dtype)
            _acc_rows(dgq_ref, 0, dgc)
        xv = x_ref[...]
        dx, dgc = _rms_bwd(_nt_chunks(dq_ref, wq_ref), xv, g_ref[...], _rms(xv))
        dx_ref[...] = dx2_ref[...] + dx
        _acc_rows(dg_ref, 0, dgc)

    row = BS((tm, D), lambda i: (i, 0))
    full = lambda r, w: BS((r, w), lambda i: (0, 0))
    return _pcall(body, name="xattn_block_bwd", grid=(S // tm,), after=after,
                  in_specs=[row, _resident(wo3.shape), row, full(MEM_LEN, D), full(MEM_LEN, D), full(1, XA_DH),
                            _resident(wq3.shape), row, full(1, D)],
                  out_specs=[row, row, full(MEM_LEN, D), full(MEM_LEN, D), full(8, XA_DH), full(8, D)],
                  out_shape=[SDS((S, D), MXU_DTYPE), SDS((S, D), F32), SDS((MEM_LEN, D), F32), SDS((MEM_LEN, D), F32),
                             SDS((8, XA_DH), F32), SDS((8, D), F32)])(dx2, wo3, qx, kn, vb, gxq, wq3, x1, g)


def mem_bwd(kv, dkn, dvb, gxk, after=None):
    def body(kv_ref, dkn_ref, dv_ref, g_ref, dkv_ref, dg_ref):
        dg_ref[...] = jnp.zeros_like(dg_ref)
        for h in range(XA_HEADS):
            sl = slice(h * XA_DH, (h + 1) * XA_DH)
            k = kv_ref[:, sl]
            dk, dgc = _rms_bwd(dkn_ref[:, sl], k, g_ref[...], _rms(k))
            dkv_ref[:, sl] = dk.astype(dkv_ref.dtype)
            _acc_rows(dg_ref, 0, dgc)
        dkv_ref[:, 1024:2048] = dv_ref[...].astype(dkv_ref.dtype)

    full = lambda r, w: BS((r, w), lambda i: (0, 0))
    return _pcall(body, name="mem_bwd", grid=(1,), after=after,
                  in_specs=[full(MEM_LEN, 2048), full(MEM_LEN, 1024), full(MEM_LEN, 1024), full(1, XA_DH)],
                  out_specs=[full(MEM_LEN, 2048), full(8, XA_DH)],
                  out_shape=[SDS((MEM_LEN, 2048), MXU_DTYPE), SDS((8, XA_DH), F32)])(kv, dkn, dvb, gxk)


def _norm_rope_bwd(slabs, douts, g, bm, cos_v, sin_v, first):
    dqns = [d * cos_v + _half_swap(d * sin_v, first) for d in douts]
    rs = [lax.rsqrt(ms + EPS) for ms in _head_means([x * x for x in slabs], bm)]
    projs = _head_means([dqn * g * x for dqn, x in zip(dqns, slabs)], bm)
    dxs = [r * (dqn * g) - x * (r * r * r) * pr for x, dqn, r, pr in zip(slabs, dqns, rs, projs)]
    return dxs, [dqn * x * r for x, dqn, r in zip(slabs, dqns, rs)]


def mixer_core_bwd(proj, cos, sin, gq, gk, gvg, bmat, qr, kr, vb, sinkcol, dattn, dgm, gvn, gu, w2, w2t, bsl,
                   after=None):
    S = qr.shape[0]
    nb = S // BLK

    def body(p_ref, c_ref, s_ref, gq_ref, gk_ref, gvg_ref, b_ref, q_ref, kc_ref, kp_ref, vc_ref, vp_ref, sk_ref,
             do_ref, dgm_ref, gvn_ref, gu_ref, w2_ref, w2t_ref, bsl_ref,
             dp_ref, dsk_ref, dws_ref, dbl_ref, dgq_ref, dgk_ref, dgv_ref,
             carry_k, carry_v, done_k, done_v, dq_keep, dgu_keep, dgvn_keep):
        n = pl.program_id(0)

        @pl.when(n == 0)
        def _():
            for ref in (dsk_ref, dws_ref, dbl_ref, dgq_ref, dgk_ref, dgv_ref, carry_k, carry_v, dq_keep, dgu_keep,
                        dgvn_keep):
                ref[...] = jnp.zeros_like(ref)

        live = (n < nb).astype(F32)
        cos_v, sin_v, bm = c_ref[...], s_ref[...], b_ref[...]
        first = (_lane((BLK, 128)) & 63) < 32
        lo = _lane((BLK, 128)) < 64

        dxs, dgs = _norm_rope_bwd([p_ref[:, s * 128:(s + 1) * 128] for s in range(4)],
                                  [dq_keep[:, s * 128:(s + 1) * 128] for s in range(4)], gq_ref[...], bm,
                                  cos_v, sin_v, first)
        for s, (dx, dg) in enumerate(zip(dxs, dgs)):
            dp_ref[:, s * 128:(s + 1) * 128] = dx.astype(dp_ref.dtype)
            _acc_rows(dgq_ref, 0, dg)
        dp_ref[:, 1024:1536] = (dgu_keep[...] * _gelu_grad(p_ref[:, 1024:1536])).astype(dp_ref.dtype)
        gvp = p_ref[:, 1536:2048]
        gv = _gelu(gvp)
        dgv, dgc = _rms_bwd(dgvn_keep[...], gv, gvg_ref[...], _rms(gv))
        dp_ref[:, 1536:2048] = (dgv * _gelu_grad(gvp)).astype(dp_ref.dtype)
        _acc_rows(dgv_ref, 0, dgc)

        for h in range(2):
            hs, qs = slice(h * 128, (h + 1) * 128), slice(h * 256, (h + 1) * 256)
            kd = jnp.concatenate([kp_ref[:, hs], kc_ref[:, hs]], axis=0)
            vd = jnp.concatenate([vp_ref[:, hs], vc_ref[:, hs]], axis=0)
            sink = jnp.concatenate([sk_ref[2 * h], sk_ref[2 * h + 1]], axis=0)
            qp, p, psink = _swa_probs(q_ref[:, qs], kd, sink, n, lo)
            dop = _by_head(do_ref[:, qs], lo)
            dp = _dot(dop, vd, NT)
            delta = jnp.sum(dp * p, axis=1, keepdims=True)
            ds = p * (dp - delta) * (1.0 / math.sqrt(HEAD_DIM))
            dsink = -psink * delta * live
            dsk_ref[2 * h] += dsink[:2 * BLK]
            dsk_ref[2 * h + 1] += dsink[2 * BLK:]
            dq_keep[:, qs] = _from_heads(_dot(ds, kd), lo)
            dkd = _dot(ds, qp, TN)
            dvd = _dot(p, dop, TN)
            done_k[:, hs] = carry_k[:, hs] + live * dkd[:BLK]
            done_v[:, hs] = carry_v[:, hs] + live * dvd[:BLK]
            carry_k[:, hs] = dkd[BLK:]
            carry_v[:, hs] = dvd[BLK:]
        for j in range(4):
            sl = slice(j * 128, (j + 1) * 128)
            gvn_s = gvn_ref[:, sl]
            m2 = _dot(w2_ref[j], gvn_s)
            mixed = jnp.where(lo, m2[:BLK], m2[BLK:]) + bsl_ref[j]
            dgm_s = dgm_ref[:, sl]
            dgu_keep[:, sl] = dgm_s * mixed
            dmx = dgm_s * gu_ref[:, sl] * live
            d2 = _dot(w2t_ref[j], dmx)
            dgvn_keep[:, sl] = jnp.where(lo, d2[:BLK], d2[BLK:])
            z = jnp.zeros_like(dmx)
            dws_ref[2 * j] += _dot(jnp.where(lo, dmx, z), gvn_s, NT)
            dws_ref[2 * j + 1] += _dot(jnp.where(lo, z, dmx), gvn_s, NT)
            dbl_ref[j] += dmx

        dxs, dgs = _norm_rope_bwd([p_ref[:, 512 + s * 128:640 + s * 128] for s in range(2)],
                                  [done_k[:, s * 128:(s + 1) * 128] for s in range(2)], gk_ref[...], bm,
                                  cos_v, sin_v, first)
        for s, (dx, dg) in enumerate(zip(dxs, dgs)):
            dp_ref[:, 512 + s * 128:640 + s * 128] = dx.astype(dp_ref.dtype)
            _acc_rows(dgk_ref, 0, dg)
        dp_ref[:, 768:1024] = done_v[...].astype(dp_ref.dtype)

    last = nb - 1
    cur = lambda w: BS((BLK, w), lambda n: (jnp.minimum(n, last), 0))
    prev = lambda w: BS((BLK, w), lambda n: (jnp.clip(n - 1, 0, last), 0))
    done = lambda w: BS((BLK, w), lambda n: (jnp.maximum(n - 1, 0), 0))
    const = lambda *shape: BS(shape, lambda n: (0,) * len(shape))
    return _pcall(body, name="mixer_core_bwd", grid=(nb + 1,), after=after,
                  in_specs=[done(IN_COLS_DUP), done(128), done(128), const(1, 128), const(1, 128), const(1, 512),
                            const(128, 128), cur(512), cur(256), prev(256), cur(256), prev(256),
                            const(4, 2 * BLK, 1), cur(512), cur(512), cur(512), cur(512), const(4, 2 * BLK, BLK),
                            const(4, 2 * BLK, BLK), const(4, BLK, 128)],
                  out_specs=[done(IN_COLS_DUP), const(4, 2 * BLK, 1), const(8, BLK, BLK), const(4, BLK, 128),
                             const(8, 128), const(8, 128), const(8, 512)],
                  out_shape=[SDS((S, IN_COLS_DUP), MXU_DTYPE), SDS((4, 2 * BLK, 1), F32), SDS((8, BLK, BLK), F32),
                             SDS((4, BLK, 128), F32), SDS((8, 128), F32), SDS((8, 128), F32), SDS((8, 512), F32)],
                  scratch=[pltpu.VMEM((BLK, 256), F32)] * 4 + [pltpu.VMEM((BLK, 512), F32)] * 3)(
        proj, cos, sin, gq, gk, gvg, bmat, qr, kr, kr, vb, vb, sinkcol, dattn, dgm, gvn, gu, w2, w2t, bsl)


BIG = (("w_in", (1024, 448), True), ("w_out", (256, 1024), False), ("xa_wq", (256, 1024), False),
       ("xa_wkv", (1024, 512), True), ("xa_wo", (256, 1024), False), ("ffn_up", (1024, 1408), True),
       ("ffn_down", (704, 1024), False))
BIG_NAMES = tuple(n for n, _, _ in BIG)
SMALL_VECS = (("mix_norm", 1024), ("q_norm", 64), ("k_norm", 64), ("attn_sinks", 8), ("gmlp_v_norm", 512),
              ("attn_out_norm", 512), ("gmlp_out_norm", 512), ("xa_norm", 1024), ("mem_norm", 1024),
              ("xa_q_norm", 256), ("xa_k_norm", 256), ("ffn_norm", 1024), ("ffn_conv_b", 5632))
SMALL = tuple(n for n, _ in SMALL_VECS) + ("gmlp_bs", "gmlp_ws", "ffn_conv")
WEIGHTS = ("mix_norm", "w_in", "q_norm", "k_norm", "attn_sinks", "gmlp_v_norm", "gmlp_ws", "gmlp_bs",
           "attn_out_norm", "gmlp_out_norm", "w_out", "xa_norm", "mem_norm", "xa_wq", "xa_wkv", "xa_q_norm",
           "xa_k_norm", "xa_wo", "ffn_norm", "ffn_up", "ffn_conv", "ffn_conv_b", "ffn_down")
CONV_SHARD = (3, 1408)
CONV_LANE_ROWS = CONV_SHARD[1] // 128
CONV_CHIP_ROWS = 40


def _small_rows():
    rows, r = {}, 0
    for n, length in SMALL_VECS:
        rows[n] = r
        r += -(-length // 128)
    r += -r % 8
    rows["gmlp_bs"] = r
    r += 8
    rows["gmlp_ws"] = r
    r += 8 * BLK
    rows["ffn_conv"] = r
    r += N_CHIPS * CONV_CHIP_ROWS
    return rows, r


SMALL_ROW, SMALL_ROWS = _small_rows()


def pack_small(dg_mix, dgq, dgk, dsk, dg_gvn, dg_y, dg_xa, dg_mem, dg_xq, dg_xk, dg_ffn, gcw, dbl, dws):
    def body(mix_ref, q_ref, k_ref, sk_ref, gvn_ref, y_ref, xa_ref, mem_ref, xq_ref, xk_ref, ffn_ref, cw_ref,
             dbl_ref, dws_ref, o_ref):
        o_ref[...] = jnp.zeros_like(o_ref)
        lane = _lane((1, 128))

        def put(name, src_ref, row, lane0, length):
            for k in range(length // 128):
                o_ref[SMALL_ROW[name] + k:SMALL_ROW[name] + k + 1, :] = src_ref[row:row + 1, lane0 + k * 128:lane0 + (k + 1) * 128]

        put("mix_norm", mix_ref, 0, 0, 1024)
        for name, ref in (("q_norm", q_ref), ("k_norm", k_ref)):
            v = ref[0:1, :]
            o_ref[SMALL_ROW[name]:SMALL_ROW[name] + 1, :] = jnp.where(lane < HEAD_DIM, v + pltpu.roll(v, 64, 1), 0.0)
        sinks = jnp.zeros((1, 128), F32)
        for s in range(4):
            col = sk_ref[s]
            sinks = sinks + jnp.where(lane == 2 * s, jnp.sum(col[:BLK]), 0.0) + jnp.where(lane == 2 * s + 1, jnp.sum(col[BLK:]), 0.0)
        o_ref[SMALL_ROW["attn_sinks"]:SMALL_ROW["attn_sinks"] + 1, :] = sinks
        put("gmlp_v_norm", gvn_ref, 0, 0, 512)
        put("attn_out_norm", y_ref, 0, 0, 512)
        put("gmlp_out_norm", y_ref, 0, 512, 512)
        put("xa_norm", xa_ref, 0, 0, 1024)
        put("mem_norm", mem_ref, 0, 0, 1024)
        put("xa_q_norm", xq_ref, 0, 0, 256)
        put("xa_k_norm", xk_ref, 0, 0, 256)
        put("ffn_norm", ffn_ref, 0, 0, 1024)
        put("ffn_conv_b", cw_ref, 3, 0, 2 * D_FF)
        r8 = lax.broadcasted_iota(jnp.int32, (8, 128), 0)
        l8 = _lane((8, 128))
        bs = jnp.zeros((8, BLK), F32)
        for j in range(4):
            sel = (((r8 == 2 * j) & (l8 < 64)) | ((r8 == 2 * j + 1) & (l8 >= 64))).astype(F32).astype(BF16)
            xj = dbl_ref[j]
            hi = xj.astype(BF16)
            lo = (xj - hi.astype(F32)).astype(BF16)
            bs = bs + lax.dot_general(sel, hi, NT, preferred_element_type=F32) + lax.dot_general(sel, lo, NT, preferred_element_type=F32)
        o_ref[SMALL_ROW["gmlp_bs"]:SMALL_ROW["gmlp_bs"] + 8, :] = bs
        causal = lax.broadcasted_iota(jnp.int32, (BLK, BLK), 0) >= lax.broadcasted_iota(jnp.int32, (BLK, BLK), 1)
        for h in range(8):
            r0 = SMALL_ROW["gmlp_ws"] + h * BLK
            o_ref[r0:r0 + BLK, :] = jnp.where(causal, dws_ref[h], 0.0)
        for q in range(N_CHIPS):
            for j in range(3):
                for k in range(CONV_LANE_ROWS):
                    r0 = SMALL_ROW["ffn_conv"] + q * CONV_CHIP_ROWS + j * CONV_LANE_ROWS + k
                    l0 = (q * CONV_LANE_ROWS + k) * 128
                    o_ref[r0:r0 + 1, :] = cw_ref[j:j + 1, l0:l0 + 128]

    args = (dg_mix, dgq, dgk, dsk, dg_gvn, dg_y, dg_xa, dg_mem, dg_xq, dg_xk, dg_ffn, gcw, dbl, dws)
    full = lambda a: BS(a.shape, lambda i, nd=a.ndim: (0,) * nd)
    return _pcall(body, name="pack_small", grid=(1,), in_specs=[full(a) for a in args],
                  out_specs=BS((SMALL_ROWS, 128), lambda i: (0, 0)), out_shape=SDS((SMALL_ROWS, 128), F32))(*args)


def _adam(w, g, m, v):
    mn = ADAM_B1 * m + (1.0 - ADAM_B1) * g
    vn = ADAM_B2 * v + (1.0 - ADAM_B2) * (g * g)
    m_hat = mn / (1.0 - ADAM_B1 ** ADAM_STEP)
    v_hat = vn / (1.0 - ADAM_B2 ** ADAM_STEP)
    return -ADAM_LR * (m_hat / (jnp.sqrt(v_hat) + ADAM_EPS) + ADAM_WD * w), mn, vn


def adamw_small(gsum, w, m, v, chipvec):
    n = len(SMALL)

    def body(chip_ref, g_ref, *refs):
        w_refs, m_refs, v_refs = refs[:n], refs[n:2 * n], refs[2 * n:3 * n]
        outs = refs[3 * n:]
        go, do, mo, vo = outs[:n], outs[n:2 * n], outs[2 * n:3 * n], outs[3 * n:]

        def update(i, idx, g):
            d, mn, vn = _adam(w_refs[i][idx], g, m_refs[i][idx], v_refs[i][idx])
            go[i][idx] = g
            do[i][idx] = d
            mo[i][idx] = mn
            vo[i][idx] = vn

        for i, (name, length) in enumerate(SMALL_VECS):
            for k in range(-(-length // 128)):
                wd = min(128, length - k * 128)
                r = SMALL_ROW[name] + k
                update(i, (slice(0, 1), slice(k * 128, k * 128 + wd)), g_ref[r:r + 1, 0:wd])
        i_bs, i_ws, i_cv = len(SMALL_VECS), len(SMALL_VECS) + 1, len(SMALL_VECS) + 2
        update(i_bs, (0,), g_ref[SMALL_ROW["gmlp_bs"]:SMALL_ROW["gmlp_bs"] + 8, :])
        for h in range(8):
            r0 = SMALL_ROW["gmlp_ws"] + h * BLK
            update(i_ws, (0, h), g_ref[r0:r0 + BLK, :])
        mine = g_ref[pl.ds(pl.multiple_of(SMALL_ROW["ffn_conv"] + chip_ref[0] * CONV_CHIP_ROWS, 8), CONV_CHIP_ROWS), :]
        for j in range(3):
            for k in range(CONV_LANE_ROWS):
                r = j * CONV_LANE_ROWS + k
                update(i_cv, (0, slice(j, j + 1), slice(k * 128, (k + 1) * 128)), mine[r:r + 1, :])

    nat = [w[nm] for nm in SMALL]
    full = lambda a: BS(a.shape, lambda i, c, nd=a.ndim: (0,) * nd)
    outs = _pcall(body, name="adamw_small", grid=(1,), prefetch=1,
                  in_specs=[BS((SMALL_ROWS, 128), lambda i, c: (0, 0))] + [full(a) for a in nat] * 3,
                  out_specs=[full(a) for a in nat] * 4, out_shape=[SDS(a.shape, F32) for a in nat] * 4)(
        chipvec, gsum, *nat, *[m[nm] for nm in SMALL], *[v[nm] for nm in SMALL])
    return outs[:n], outs[n:2 * n], outs[2 * n:3 * n], outs[3 * n:]


def adamw_matrix(w, m, v, g_own, g_other, cvec, *, name):
    _, r, c = w.shape
    half = r // 2
    tr = _tile(half, (128, 176))
    T = half // tr

    def body(c_ref, w_ref, m_ref, v_ref, own_ref, oth_ref, g_ref, d_ref, mo_ref, vo_ref):
        g = jnp.where(pl.program_id(0) == c_ref[0], own_ref[...], oth_ref[...])
        d, mn, vn = _adam(w_ref[...], g, m_ref[...], v_ref[...])
        g_ref[...] = g
        d_ref[...] = d
        mo_ref[...] = mn
        vo_ref[...] = vn

    nat = BS((None, tr, c), lambda hf, t, cr: (0, hf * T + t, 0))
    hlf = BS((tr, c), lambda hf, t, cr: (t, 0))
    return _pcall(body, name=name, grid=(2, T), prefetch=1, in_specs=[nat, nat, nat, hlf, hlf], out_specs=[nat] * 4,
                  out_shape=[SDS(w.shape, F32)] * 4)(cvec, w, m, v, g_own, g_other)


def _place():
    return lax.axis_index("x"), lax.axis_index("y"), lax.axis_index("c")


def _other_chips(x, y):
    return [(1 - x, y), (x, 1 - y), (1 - x, 1 - y)]


def _rows_of_core(c, half):
    return pl.ds(pl.multiple_of(c * half, 16), half)


def _rcopy(src, dst, sems, k, to):
    return pltpu.make_async_remote_copy(src_ref=src, dst_ref=dst, send_sem=sems[0].at[k], recv_sem=sems[1].at[k],
                                        device_id=to, device_id_type=MESH)


def _comm_call(body, *, name, out_shape, n_in, n_sems, aliases=None):
    return pl.pallas_call(body, name=name, out_shape=out_shape, in_specs=[ANY] * n_in, out_specs=[ANY] * len(out_shape),
                          scratch_shapes=[pltpu.SemaphoreType.DMA((n_sems,)), pltpu.SemaphoreType.DMA((n_sems,))],
                          input_output_aliases=aliases or {},
                          compiler_params=pltpu.CompilerParams(has_side_effects=True))


def cast_shards(shards, conv, chipvec):
    n = len(shards)

    def body(chip_ref, *refs):
        for i_ref, o_ref in zip(refs[:n + 1], refs[n + 1:]):
            o_ref[...] = i_ref[...].astype(o_ref.dtype)

    in_specs = [BS((s.shape[0] // 4, s.shape[1]), lambda i, p: (i, 0)) for s in shards]
    in_specs.append(BS(conv.shape, lambda i, p: (0, 0)))
    out_specs = [BS((None, s.shape[0] // 4, s.shape[1]), lambda i, p: (p[0], i, 0)) for s in shards]
    out_specs.append(BS((None,) + conv.shape, lambda i, p: (p[0], 0, 0)))
    out_shape = [SDS((N_CHIPS,) + s.shape, MXU_DTYPE) for s in shards] + [SDS((N_CHIPS,) + conv.shape, F32)]
    return _pcall(body, name="cast_shards", grid=(4,), prefetch=1, in_specs=in_specs, out_specs=out_specs,
                  out_shape=out_shape)(chipvec, *shards, conv)


HBM = pl.BlockSpec(memory_space=pltpu.HBM)
SEM = pl.BlockSpec(memory_space=pltpu.SEMAPHORE)
DATAFLOW = pltpu.SideEffectType.DATAFLOW_SIDE_EFFECTING
VMEM_WHOLE = pl.BlockSpec(memory_space=pltpu.VMEM)
TOKEN = jax.ShapeDtypeStruct((8, 128), jnp.float32)


def _gather_copies(bufs, send_sems, recv_sems, outgoing):
    x, y, c = _place()
    p = 2 * x + y
    cps = []
    for i, o in enumerate(bufs):
        for j, (cx, cy) in enumerate(_other_chips(x, y)):
            slot = o.at[p] if outgoing else o.at[2 * cx + cy]
            cps.append(_rcopy(slot, slot, (send_sems, recv_sems), 3 * i + j, (cx, cy, c)))
    return cps


def gather_start(slots, after):
    n = len(slots)

    def body(*refs):
        send_sems, recv_sems, thru, token = refs[n + 1], refs[n + 2], refs[n + 3:2 * n + 3], refs[2 * n + 3]
        for cp in _gather_copies(thru, send_sems, recv_sems, True):
            cp.start()
        token[...] = jnp.zeros_like(token)

    hbm = [pltpu.with_memory_space_constraint(s, pltpu.HBM) for s in slots]
    outs = pl.pallas_call(
        body, name="gather_start_%d" % n,
        out_shape=[pltpu.SemaphoreType.DMA((3 * n,)), pltpu.SemaphoreType.DMA((3 * n,))]
        + [pltpu.HBM(s.shape, s.dtype) for s in slots] + [TOKEN],
        in_specs=[HBM] * n + [ANY], out_specs=[SEM, SEM] + [HBM] * n + [VMEM_WHOLE],
        input_output_aliases={i: 2 + i for i in range(n)},
        compiler_params=pltpu.CompilerParams(has_side_effects=DATAFLOW))(*hbm, after)
    return outs[0], outs[1], outs[2:2 + n], outs[2 + n]


def gather_wait(send_sems, recv_sems, bufs, *after):
    n = len(bufs)

    def body(*refs):
        ins, send_ref, recv_ref = refs[:n], refs[n], refs[n + 1]
        for cp in _gather_copies(ins, send_ref, recv_ref, False):
            cp.wait_send()
            cp.wait_recv()

    return pl.pallas_call(
        body, name="gather_wait_%d" % n, out_shape=[pltpu.HBM(s.shape, s.dtype) for s in bufs],
        in_specs=[HBM] * n + [SEM, SEM] + [ANY] * len(after), out_specs=[HBM] * n,
        input_output_aliases={i: i for i in range(n)},
        compiler_params=pltpu.CompilerParams(has_side_effects=DATAFLOW))(*bufs, send_sems, recv_sems, *after)


def _peers(x, y, c):
    return [(1 - x if k & 4 else x, 1 - y if k & 2 else y, 1 - c if k & 1 else c) for k in range(1, N_DEV)]


def _partial_copies(g_ref, land_ref, send_sems, recv_sems, outgoing):
    x, y, c = _place()
    half = g_ref.shape[1] // 2
    cps = []
    for k, (px, py, pc) in enumerate(_peers(x, y, c)):
        src = g_ref.at[2 * px + py, _rows_of_core(pc, half)]
        dst = land_ref.at[4 * x + 2 * y + c] if outgoing else land_ref.at[4 * px + 2 * py + pc]
        cps.append(_rcopy(src, dst, (send_sems, recv_sems), k, (px, py, pc)))
    return cps


def partials_start(g, *, name):
    land = lax.empty((N_DEV, g.shape[1] // 2, g.shape[2]), g.dtype)

    def body(g_ref, land_ref, send_sems, recv_sems, g_thru, land_thru, token):
        for cp in _partial_copies(g_thru, land_thru, send_sems, recv_sems, True):
            cp.start()
        token[...] = jnp.zeros_like(token)

    return pl.pallas_call(
        body, name=name,
        out_shape=[pltpu.SemaphoreType.DMA((N_DEV - 1,)), pltpu.SemaphoreType.DMA((N_DEV - 1,)),
                   pltpu.HBM(g.shape, g.dtype), pltpu.HBM(land.shape, land.dtype), TOKEN],
        in_specs=[HBM, HBM], out_specs=[SEM, SEM, HBM, HBM, VMEM_WHOLE], input_output_aliases={0: 2, 1: 3},
        compiler_params=pltpu.CompilerParams(has_side_effects=DATAFLOW))(
        pltpu.with_memory_space_constraint(g, pltpu.HBM), pltpu.with_memory_space_constraint(land, pltpu.HBM))


def partials_wait(started, after):
    n = len(started)

    def body(*refs):
        for i in range(n):
            send_ref, recv_ref, g_ref, land_ref = refs[4 * i:4 * i + 4]
            for cp in _partial_copies(g_ref, land_ref, send_ref, recv_ref, False):
                cp.wait_send()
                cp.wait_recv()

    flat = [a for s in started for a in s]
    bufs = [a for s in started for a in s[2:]]
    outs = pl.pallas_call(
        body, name="partials_wait", out_shape=[pltpu.HBM(b.shape, b.dtype) for b in bufs],
        in_specs=[SEM, SEM, HBM, HBM] * n + [ANY], out_specs=[HBM] * (2 * n),
        input_output_aliases={4 * i + 2 + j: 2 * i + j for i in range(n) for j in range(2)},
        compiler_params=pltpu.CompilerParams(has_side_effects=DATAFLOW))(*flat, after)
    return [(outs[2 * i], outs[2 * i + 1]) for i in range(n)]


def sum_partials(pairs, order):
    n = len(pairs)

    def body(o_ref, *refs):
        j = pl.program_id(0)
        for g_ref, l_ref, f_ref in zip(refs[:n], refs[n:2 * n], refs[2 * n:]):
            @pl.when(j == 0)
            def _():
                f_ref[...] = g_ref[...].astype(F32)

            @pl.when(j > 0)
            def _():
                f_ref[...] += l_ref[...].astype(F32)

    g4 = [g.reshape(g.shape[0], 2, g.shape[1] // 2, g.shape[2]) for g, _ in pairs]
    lands = [l for _, l in pairs]
    return _pcall(body, name="sum_partials", grid=(N_DEV,), prefetch=1,
                  in_specs=[BS((None, None) + g.shape[2:], lambda j, o: (o[0], o[1], 0, 0)) for g in g4]
                  + [BS((None,) + l.shape[1:], lambda j, o: (o[jnp.maximum(j, 1) + 1], 0, 0)) for l in lands],
                  out_specs=[BS(l.shape[1:], lambda j, o: (0, 0)) for l in lands],
                  out_shape=[SDS(l.shape[1:], F32) for l in lands])(order, *g4, *lands)


def pair_share(fs):
    n = len(fs)

    def body(*refs):
        f_refs, o_refs, sems = refs[:n], refs[n:2 * n], refs[2 * n:]
        x, y, c = _place()
        cps = [_rcopy(f, o, sems, i, (x, y, 1 - c)) for i, (f, o) in enumerate(zip(f_refs, o_refs))]
        for cp in cps:
            cp.start()
        for cp in cps:
            cp.wait()

    return _comm_call(body, name="pair_share", n_in=n, n_sems=n, out_shape=[SDS(f.shape, f.dtype) for f in fs])(*fs)


def _small_copies(s_ref, land_ref, send_sems, recv_sems, outgoing):
    x, y, c = _place()
    cps = []
    for k, (px, py, pc) in enumerate(_peers(x, y, c)):
        dst = land_ref.at[4 * x + 2 * y + c] if outgoing else land_ref.at[4 * px + 2 * py + pc]
        cps.append(_rcopy(s_ref, dst, (send_sems, recv_sems), k, (px, py, pc)))
    return cps


def small_start(sm):
    land = lax.empty((N_DEV,) + sm.shape, sm.dtype)

    def body(s_ref, land_ref, send_sems, recv_sems, s_thru, land_thru):
        for cp in _small_copies(s_thru, land_thru, send_sems, recv_sems, True):
            cp.start()

    return pl.pallas_call(
        body, name="small_start",
        out_shape=[pltpu.SemaphoreType.DMA((N_DEV - 1,)), pltpu.SemaphoreType.DMA((N_DEV - 1,)),
                   pltpu.HBM(sm.shape, sm.dtype), pltpu.HBM(land.shape, land.dtype)],
        in_specs=[HBM, HBM], out_specs=[SEM, SEM, HBM, HBM], input_output_aliases={0: 2, 1: 3},
        compiler_params=pltpu.CompilerParams(has_side_effects=DATAFLOW))(
        pltpu.with_memory_space_constraint(sm, pltpu.HBM), pltpu.with_memory_space_constraint(land, pltpu.HBM))


def small_wait(send_sems, recv_sems, sm, land, after):
    def body(send_ref, recv_ref, s_ref, land_ref, after_ref, s_out, land_out):
        for cp in _small_copies(s_ref, land_ref, send_ref, recv_ref, False):
            cp.wait_send()
            cp.wait_recv()

    return pl.pallas_call(
        body, name="small_wait", out_shape=[pltpu.HBM(sm.shape, sm.dtype), pltpu.HBM(land.shape, land.dtype)],
        in_specs=[SEM, SEM, HBM, HBM, ANY], out_specs=[HBM, HBM], input_output_aliases={2: 0, 3: 1},
        compiler_params=pltpu.CompilerParams(has_side_effects=DATAFLOW))(send_sems, recv_sems, sm, land, after)


def sum_small(own, land, mevec):
    n, rows, width = land.shape
    tr = _tile(rows, (184, 8))

    def body(me_ref, own_ref, land_ref, o_ref):
        acc = jnp.zeros((tr, width), F32)
        for s in range(n):
            acc = acc + jnp.where(me_ref[0] == s, own_ref[...], land_ref[s])
        o_ref[...] = acc

    return _pcall(body, name="sum_small", grid=(rows // tr,), prefetch=1,
                  in_specs=[BS((tr, width), lambda i, me: (i, 0)), BS((n, tr, width), lambda i, me: (0, i, 0))],
                  out_specs=BS((tr, width), lambda i, me: (i, 0)), out_shape=SDS((rows, width), F32))(mevec, own, land)


def _to_full(blk, col):
    n, r, c = blk.shape
    return blk.transpose(1, 0, 2).reshape(r, n * c) if col else blk.reshape(n * r, c)


def _dup_cols(w):
    dup = lambda t: jnp.concatenate([t[:, :64], t[:, :64], t[:, 64:], t[:, 64:]], axis=1)
    return jnp.concatenate([w[:, :512], dup(w[:, 512:640]), dup(w[:, 640:768]), w[:, 768:]], axis=1)


def _fold_cols(d):
    fold = lambda t: jnp.concatenate([t[:, 0:64] + t[:, 64:128], t[:, 128:192] + t[:, 192:256]], axis=1)
    return jnp.concatenate([d[:, :512], fold(d[:, 512:768]), fold(d[:, 768:1024]), d[:, 1024:]], axis=1)


def _local_step(x, mem, positions, target, w_in, later, sp, emit):
    gain = lambda n: sp[n].reshape(1, -1)
    half = HEAD_DIM // 2
    inv_freq = 1.0 / (10000.0 ** (jnp.arange(half, dtype=F32) * (2.0 / HEAD_DIM)))
    ang = positions.astype(F32)[:, None] * inv_freq
    cos, sin = jnp.cos(ang), jnp.sin(ang)
    cos128 = jnp.tile(cos, (1, 4))
    sin128 = jnp.concatenate([-sin, sin, -sin, sin], axis=1)
    seg = jnp.arange(128) // HEAD_DIM
    bmat = (seg[:, None] == seg[None, :]).astype(BF16)
    gq128, gk128 = jnp.tile(gain("q_norm"), (1, 2)), jnp.tile(gain("k_norm"), (1, 2))
    sinkcol = jnp.repeat(sp["attn_sinks"].reshape(4, 2), BLK, axis=1).reshape(4, 2 * BLK, 1)
    wsc = sp["gmlp_ws"] * jnp.tril(jnp.ones((BLK, BLK), F32))[None]
    w2 = wsc.reshape(4, 2 * BLK, BLK).astype(MXU_DTYPE)
    w2t = wsc.swapaxes(1, 2).reshape(4, 2 * BLK, BLK).astype(MXU_DTYPE)
    bsl = jnp.repeat(sp["gmlp_bs"].reshape(4, 2, BLK).transpose(0, 2, 1), HEAD_DIM, axis=2)
    cb = sp["ffn_conv_b"].reshape(1, -1)
    w_in_d = _dup_cols(_to_full(w_in(cos128, sin128, gq128, gk128, sinkcol, w2, w2t, bsl), True))[None]

    h1, proj = rms_mm(x, gain("mix_norm"), w_in_d, name="mix_in")
    qr, kr, vb, gu, gvn, attn, gm, y = mixer_core_fwd(proj, cos128, sin128, gq128, gk128, gain("gmlp_v_norm"), bmat,
                                                      sinkcol, gain("attn_out_norm"), w2, bsl, gain("gmlp_out_norm"))
    wf, last = later(y)
    w_out, xa_wq, xa_wo = (_to_full(wf[n], False) for n in ("w_out", "xa_wq", "xa_wo"))
    x1 = mm(y, w_out, res=x, name="mix_out")
    mn, kv = rms_mm(mem, gain("mem_norm"), wf["xa_wkv"], name="xa_kv")
    kn, vbx = mem_pre(kv, gain("xa_k_norm"))
    h2, qx, xo, x2 = xattn_block_fwd(x1, gain("xa_norm"), xa_wq, kn, vbx, gain("xa_q_norm"), xa_wo)
    ffn_w, cw = last(x2)
    wf = {**wf, **ffn_w}
    ffn_down = _to_full(wf["ffn_down"], False)
    h3, a, f, dx3, loss_acc = ffn_fwd_loss(x2, gain("ffn_norm"), wf["ffn_up"], cw, cb, ffn_down, target)

    by_rows = lambda g: g.reshape(N_CHIPS, g.shape[1] // N_CHIPS, g.shape[2])
    sent = emit("ffn_down", by_rows(mm_tn(f, dx3, name="g_ffn_down", out_dtype=WIRE_DTYPE)))
    dc, gcw = convgate_bwd(a, dx3, ffn_down[None], cw, cb, after=sent)
    da, dx2, dg_ffn = conv_transpose_rms_bwd(dc, cw, wf["ffn_up"], x2, gain("ffn_norm"), dx3)
    sent = emit("ffn_up", mm_tn(h3, da, name="g_ffn_up", out_dtype=WIRE_DTYPE, chunks=N_CHIPS))
    sent = emit("xa_wo", by_rows(mm_tn(xo, dx2, name="g_xa_wo", out_dtype=WIRE_DTYPE, after=sent)))
    dqx, dx1, dkn, dvx, dg_xq, dg_xa = xattn_block_bwd(dx2, xa_wo[None], qx, kn, vbx, gain("xa_q_norm"), xa_wq[None],
                                                       x1, gain("xa_norm"), after=sent)
    sent = emit("xa_wq", by_rows(mm_tn(h2, dqx, name="g_xa_wq", out_dtype=WIRE_DTYPE)))
    dkv, dg_xk = mem_bwd(kv, dkn, dvx, gain("xa_k_norm"), after=sent)
    _, dg_mem = mm_nt_rms_bwd(dkv, wf["xa_wkv"], mem, gain("mem_norm"), jnp.zeros_like(mem), name="d_mem")
    sent = emit("xa_wkv", mm_tn(mn, dkv, name="g_xa_wkv", out_dtype=WIRE_DTYPE, chunks=N_CHIPS))
    dattn, dgm, dg_y = mm_nt_post_bwd(dx1, w_out[None], attn, gm, gain("attn_out_norm"), gain("gmlp_out_norm"),
                                      name="d_mix_out", after=sent)
    sent = emit("w_out", by_rows(mm_tn(y, dx1, name="g_w_out", out_dtype=WIRE_DTYPE)))
    dproj, dsk, dws, dbl, dgq, dgk, dg_gvn = mixer_core_bwd(
        proj, cos128, sin128, gq128, gk128, gain("gmlp_v_norm"), bmat, qr, kr, vb, sinkcol, dattn, dgm, gvn, gu,
        w2, w2t, bsl, after=sent)
    g_in = _fold_cols(mm_tn(h1, dproj, name="g_w_in", out_dtype=F32)[0])
    sent = emit("w_in", g_in.reshape(1024, N_CHIPS, 448).transpose(1, 0, 2).astype(WIRE_DTYPE))
    grad_x, dg_mix = mm_nt_rms_bwd(dproj, w_in_d, x, gain("mix_norm"), dx1, name="d_x", after=sent)
    packed = pack_small(dg_mix, dgq, dgk, dsk, dg_gvn, dg_y, dg_xa, dg_mem, dg_xq, dg_xk, dg_ffn, gcw, dbl, dws)
    return loss_acc, grad_x, packed


def _gather_step(w, chipvec):
    slots = cast_shards([w[n][0] for n in BIG_NAMES], w["ffn_conv"][0], chipvec)
    send_a, recv_a, first, token = gather_start(slots[:1], chipvec)
    send_b, recv_b, mid, token = gather_start(slots[1:5], token)
    send_c, recv_c, rest, token = gather_start(slots[5:], token)

    def w_in(*after):
        return gather_wait(send_a, recv_a, first, token, *after)[0]

    def last(after):
        got = gather_wait(send_c, recv_c, rest, after)
        return dict(zip(BIG_NAMES[5:], got[:-1])), _to_full(got[-1], True)

    def later(after):
        return dict(zip(BIG_NAMES[1:5], gather_wait(send_b, recv_b, mid, after))), last

    return w_in, later, token


def _reduce_update(started, packed, w, m, v, chipvec, cvec, order):
    small_sent = small_start(packed)
    own = sum_partials(partials_wait([started[n] for n in BIG_NAMES], small_sent[2]), order)
    other = pair_share(own)
    res = [{}, {}, {}, {}]
    for n, g_own, g_other in zip(BIG_NAMES, own, other):
        for d, o in zip(res, adamw_matrix(w[n], m[n], v[n], g_own, g_other, cvec, name="adamw_" + n)):
            d[n] = o
    mevec = (2 * order[0:1] + order[1:2]).astype(jnp.int32)
    small_sum = sum_small(*small_wait(*small_sent, res[3][BIG_NAMES[-1]]), mevec)
    for d, outs in zip(res, adamw_small(small_sum, w, m, v, chipvec)):
        d.update(zip(SMALL, outs))
    return res


def kernel(x, mem, positions, mix_norm, w_in, q_norm, k_norm, attn_sinks, gmlp_v_norm, gmlp_ws, gmlp_bs, attn_out_norm, gmlp_out_norm, w_out, xa_norm, mem_norm, xa_wq, xa_wkv, xa_q_norm, xa_k_norm, xa_wo, ffn_norm, ffn_up, ffn_conv, ffn_conv_b, ffn_down, loss_target, m_mix_norm, m_w_in, m_q_norm, m_k_norm, m_attn_sinks, m_gmlp_v_norm, m_gmlp_ws, m_gmlp_bs, m_attn_out_norm, m_gmlp_out_norm, m_w_out, m_xa_norm, m_mem_norm, m_xa_wq, m_xa_wkv, m_xa_q_norm, m_xa_k_norm, m_xa_wo, m_ffn_norm, m_ffn_up, m_ffn_conv, m_ffn_conv_b, m_ffn_down, v_mix_norm, v_w_in, v_q_norm, v_k_norm, v_attn_sinks, v_gmlp_v_norm, v_gmlp_ws, v_gmlp_bs, v_attn_out_norm, v_gmlp_out_norm, v_w_out, v_xa_norm, v_mem_norm, v_xa_wq, v_xa_wkv, v_xa_q_norm, v_xa_k_norm, v_xa_wo, v_ffn_norm, v_ffn_up, v_ffn_conv, v_ffn_conv_b, v_ffn_down):
    w = dict(mix_norm=mix_norm, w_in=w_in, q_norm=q_norm, k_norm=k_norm, attn_sinks=attn_sinks, gmlp_v_norm=gmlp_v_norm, gmlp_ws=gmlp_ws, gmlp_bs=gmlp_bs, attn_out_norm=attn_out_norm, gmlp_out_norm=gmlp_out_norm, w_out=w_out, xa_norm=xa_norm, mem_norm=mem_norm, xa_wq=xa_wq, xa_wkv=xa_wkv, xa_q_norm=xa_q_norm, xa_k_norm=xa_k_norm, xa_wo=xa_wo, ffn_norm=ffn_norm, ffn_up=ffn_up, ffn_conv=ffn_conv, ffn_conv_b=ffn_conv_b, ffn_down=ffn_down)
    m = dict(mix_norm=m_mix_norm, w_in=m_w_in, q_norm=m_q_norm, k_norm=m_k_norm, attn_sinks=m_attn_sinks, gmlp_v_norm=m_gmlp_v_norm, gmlp_ws=m_gmlp_ws, gmlp_bs=m_gmlp_bs, attn_out_norm=m_attn_out_norm, gmlp_out_norm=m_gmlp_out_norm, w_out=m_w_out, xa_norm=m_xa_norm, mem_norm=m_mem_norm, xa_wq=m_xa_wq, xa_wkv=m_xa_wkv, xa_q_norm=m_xa_q_norm, xa_k_norm=m_xa_k_norm, xa_wo=m_xa_wo, ffn_norm=m_ffn_norm, ffn_up=m_ffn_up, ffn_conv=m_ffn_conv, ffn_conv_b=m_ffn_conv_b, ffn_down=m_ffn_down)
    v = dict(mix_norm=v_mix_norm, w_in=v_w_in, q_norm=v_q_norm, k_norm=v_k_norm, attn_sinks=v_attn_sinks, gmlp_v_norm=v_gmlp_v_norm, gmlp_ws=v_gmlp_ws, gmlp_bs=v_gmlp_bs, attn_out_norm=v_attn_out_norm, gmlp_out_norm=v_gmlp_out_norm, w_out=v_w_out, xa_norm=v_xa_norm, mem_norm=v_mem_norm, xa_wq=v_xa_wq, xa_wkv=v_xa_wkv, xa_q_norm=v_xa_q_norm, xa_k_norm=v_xa_k_norm, xa_wo=v_xa_wo, ffn_norm=v_ffn_norm, ffn_up=v_ffn_up, ffn_conv=v_ffn_conv, ffn_conv_b=v_ffn_conv_b, ffn_down=v_ffn_down)
    ix, iy, ic = lax.axis_index("x"), lax.axis_index("y"), lax.axis_index("c")
    chip = 2 * ix + iy
    chipvec = chip.astype(jnp.int32).reshape(1)
    cvec = ic.astype(jnp.int32).reshape(1)
    order = jnp.stack([chip, ic] + [4 * px + 2 * py + pc for px, py, pc in _peers(ix, iy, ic)]).astype(jnp.int32)

    w_in_all, later, token = _gather_step(w, chipvec)
    zero = token[0, 0]
    sp = {n: w[n][0] + zero for n in SMALL if n != "ffn_conv"}
    positions = positions + zero.astype(jnp.int32)
    started = {}

    def emit(name, g):
        *started[name], token = partials_start(g, name="partials_start_" + name)
        return token

    loss_acc, grad_x, packed = _local_step(x[0], mem[0], positions[0], loss_target[0], w_in_all, later, sp, emit)
    grads, delta, new_m, new_v = _reduce_update(started, packed, w, m, v, chipvec, cvec, order)
    loss = lax.psum(loss_acc[0, 0], ("x", "y", "c"))
    ordered = lambda d: [d[n] for n in WEIGHTS]
    return (loss, grad_x[None], *ordered(grads), *ordered(delta), *ordered(new_m), *ordered(new_v))
```

```python
import math

import jax
import jax.numpy as jnp
from jax import lax
from jax.experimental import pallas as pl
from jax.experimental.pallas import tpu as pltpu

F32 = jnp.float32
BF16 = jnp.bfloat16
MXU_DTYPE = jnp.bfloat16
WIRE_DTYPE = jnp.bfloat16
EPS = 1e-6
VMEM_LIMIT_V7X = 56 * 1024 * 1024

D_MODEL = 1024
HEAD_DIM = 64
BLK = 128
XA_HEADS = 4
XA_DH = 256
MEM_LEN = 256
D_FF = 2816
IN_COLS_DUP = 2048
N_CHIPS = 4
N_DEV = 8

ADAM_LR = 0.001
ADAM_B1 = 0.9
ADAM_B2 = 0.999
ADAM_EPS = 1e-08
ADAM_WD = 0.01
ADAM_STEP = 10

NT = (((1,), (1,)), ((), ()))
TN = (((0,), (0,)), ((), ()))
NN = (((1,), (0,)), ((), ()))
MINF = float(jnp.finfo(jnp.float32).min)
GELU_K0 = math.sqrt(2.0 / math.pi)
GELU_K1 = 0.044715

BS = pl.BlockSpec
SDS = jax.ShapeDtypeStruct
ANY = pl.BlockSpec(memory_space=pl.ANY)
MESH = pl.DeviceIdType.MESH


def _dot(a, b, dims=NN):
    return lax.dot_general(a.astype(MXU_DTYPE), b.astype(MXU_DTYPE), dims, preferred_element_type=F32)


def _segsum(x, bmat):
    hi = x.astype(BF16)
    lo = (x - hi.astype(F32)).astype(BF16)
    return (jnp.dot(hi, bmat, preferred_element_type=F32) + jnp.dot(lo, bmat, preferred_element_type=F32))


def _gelu(x):
    return 0.5 * x * (1.0 + jnp.tanh(GELU_K0 * (x + GELU_K1 * x * x * x)))


def _gelu_grad(x):
    t = jnp.tanh(GELU_K0 * (x + GELU_K1 * x * x * x))
    return 0.5 * (1.0 + t) + 0.5 * x * (1.0 - t * t) * GELU_K0 * (1.0 + 3.0 * GELU_K1 * x * x)


def _rms(x):
    return lax.rsqrt(jnp.mean(x * x, axis=-1, keepdims=True) + EPS)


def _rms_bwd(dy, x, g, r):
    dyg = dy * g
    dx = r * dyg - x * (r * r * r) * jnp.mean(dyg * x, axis=-1, keepdims=True)
    return dx, dy * x * r


def _pcall(body, *, name, grid, in_specs, out_specs, out_shape, scratch=(), prefetch=0, after=None):
    params = pltpu.CompilerParams(dimension_semantics=("arbitrary",) * len(grid), vmem_limit_bytes=VMEM_LIMIT_V7X)
    in_specs = list(in_specs)
    kernel_fn = body
    if after is not None:
        n_in = prefetch + len(in_specs)
        in_specs.append(ANY)

        def kernel_fn(*refs):
            return body(*refs[:n_in], *refs[n_in + 1:])

    if prefetch:
        spec = pltpu.PrefetchScalarGridSpec(num_scalar_prefetch=prefetch, grid=grid, in_specs=in_specs,
                                            out_specs=out_specs, scratch_shapes=list(scratch))
        call = pl.pallas_call(kernel_fn, name=name, grid_spec=spec, out_shape=out_shape, compiler_params=params)
    else:
        call = pl.pallas_call(kernel_fn, name=name, grid=grid, in_specs=in_specs, out_specs=out_specs,
                              out_shape=out_shape, scratch_shapes=list(scratch), compiler_params=params)
    return call if after is None else (lambda *args: call(*args, after))


def _tile(n, prefs):
    for p in prefs:
        if p <= n and n % p == 0:
            return p
    return n


def _resident(shape):
    return pl.BlockSpec(shape, lambda *_: (0,) * len(shape), pipeline_mode=pl.Buffered(1))


def _acc_rows(ref, row, val):
    ref[row:row + 1, :] += jnp.sum(val, axis=0, keepdims=True)


def rms_mm(x, g, w3, *, name, tm=1024):
    M, K = x.shape
    Q, _, C = w3.shape
    tm = _tile(M, (tm, 256))

    def body(x_ref, g_ref, w_ref, h_ref, o_ref):
        def write_h():
            xv = x_ref[...]
            h_ref[...] = (xv * _rms(xv) * g_ref[...]).astype(h_ref.dtype)

        if Q == 1:
            write_h()
        else:
            pl.when(pl.program_id(1) == 0)(write_h)
        o_ref[...] = _dot(h_ref[...], w_ref[pl.program_id(1)])

    return _pcall(body, name=name, grid=(M // tm, Q),
                  in_specs=[BS((tm, K), lambda i, j: (i, 0)), BS((1, K), lambda i, j: (0, 0)),
                            _resident((Q, K, C))],
                  out_specs=[BS((tm, K), lambda i, j: (i, 0)), BS((tm, C), lambda i, j: (i, j))],
                  out_shape=[SDS((M, K), MXU_DTYPE), SDS((M, Q * C), F32)])(x, g, w3)


def mm(a, w, *, name, res):
    M, K = a.shape
    N = w.shape[1]
    tm = _tile(M, (1024, 256))

    def body(a_ref, w_ref, r_ref, o_ref):
        o_ref[...] = _dot(a_ref[...], w_ref[...]) + r_ref[...]

    return _pcall(body, name=name, grid=(M // tm,),
                  in_specs=[BS((tm, K), lambda i: (i, 0)), _resident((K, N)), BS((tm, N), lambda i: (i, 0))],
                  out_specs=BS((tm, N), lambda i: (i, 0)), out_shape=SDS((M, N), F32))(a, w, res)


def _nt_chunks(a_ref, w_ref):
    q_n, _, kc = w_ref.shape
    acc = _dot(a_ref[:, 0:kc], w_ref[0], NT)
    for q in range(1, q_n):
        acc = acc + _dot(a_ref[:, q * kc:(q + 1) * kc], w_ref[q], NT)
    return acc


def mm_nt_rms_bwd(a, w3, x, g, dres, *, name, tm=512, after=None):
    M = a.shape[0]
    Q, N, Kc = w3.shape
    tm = _tile(M, (tm, 256))

    def body(a_ref, w_ref, x_ref, g_ref, dr_ref, dx_ref, dg_ref):
        @pl.when(pl.program_id(0) == 0)
        def _():
            dg_ref[...] = jnp.zeros_like(dg_ref)

        xv = x_ref[...]
        dx, dgc = _rms_bwd(_nt_chunks(a_ref, w_ref), xv, g_ref[...], _rms(xv))
        dx_ref[...] = dr_ref[...] + dx
        _acc_rows(dg_ref, 0, dgc)

    row = BS((tm, N), lambda i: (i, 0))
    return _pcall(body, name=name, grid=(M // tm,), after=after,
                  in_specs=[BS((tm, Q * Kc), lambda i: (i, 0)), _resident((Q, N, Kc)), row,
                            BS((1, N), lambda i: (0, 0)), row],
                  out_specs=[row, BS((8, N), lambda i: (0, 0))],
                  out_shape=[SDS((M, N), F32), SDS((8, N), F32)])(a, w3, x, g, dres)


def mm_nt_post_bwd(a, w3, attn, gm, gao, ggo, *, name, after=None):
    M = a.shape[0]
    Q, N, Kc = w3.shape
    tm = _tile(M, (512, 256))
    hw = N // 2

    def body(a_ref, w_ref, at_ref, gm_ref, gao_ref, ggo_ref, da_ref, dgm_ref, dg_ref):
        @pl.when(pl.program_id(0) == 0)
        def _():
            dg_ref[...] = jnp.zeros_like(dg_ref)

        dy = _nt_chunks(a_ref, w_ref)
        av, gmv = at_ref[...], gm_ref[...]
        da, dga = _rms_bwd(dy[:, :hw], av, gao_ref[...], _rms(av))
        dgm, dgg = _rms_bwd(dy[:, hw:], gmv, ggo_ref[...], _rms(gmv))
        da_ref[...] = da
        dgm_ref[...] = dgm
        dg_ref[0:1, :hw] += jnp.sum(dga, axis=0, keepdims=True)
        dg_ref[0:1, hw:] += jnp.sum(dgg, axis=0, keepdims=True)

    half = BS((tm, hw), lambda i: (i, 0))
    const = lambda r, w: BS((r, w), lambda i: (0, 0))
    return _pcall(body, name=name, grid=(M // tm,), after=after,
                  in_specs=[BS((tm, Q * Kc), lambda i: (i, 0)), _resident((Q, N, Kc)), half, half,
                            const(1, hw), const(1, hw)],
                  out_specs=[half, half, const(8, N)],
                  out_shape=[SDS((M, hw), F32), SDS((M, hw), F32), SDS((8, N), F32)])(a, w3, attn, gm, gao, ggo)


def mm_tn(a, b, *, name, out_dtype, chunks=1, after=None):
    M, K = a.shape
    N = b.shape[1]
    C = N // chunks
    tm = _tile(M, (1024, 256))
    tk = _tile(K, (1408, 1024, 512))
    tn = _tile(C, (1408, 1024, 512))
    per = C // tn
    nm = M // tm

    def body(a_ref, b_ref, o_ref, acc):
        m = pl.program_id(2)

        @pl.when(m == 0)
        def _():
            acc[...] = jnp.zeros_like(acc)

        acc[...] += _dot(a_ref[...], b_ref[...], TN)

        @pl.when(m == nm - 1)
        def _():
            o_ref[...] = acc[...].astype(o_ref.dtype)

    return _pcall(body, name=name, grid=(K // tk, N // tn, nm), after=after,
                  in_specs=[BS((tm, tk), lambda k, n, m: (m, k)), BS((tm, tn), lambda k, n, m: (m, n))],
                  out_specs=BS((None, tk, tn), lambda k, n, m: (n // per, k, n % per)),
                  out_shape=SDS((chunks, K, C), out_dtype), scratch=[pltpu.VMEM((tk, tn), F32)])(a, b)


def _lane(shape):
    return lax.broadcasted_iota(jnp.int32, shape, 1)


def _head_means(slabs, bmat):
    tm = slabs[0].shape[0]
    means = _segsum(jnp.concatenate(slabs, axis=0), bmat) * (1.0 / HEAD_DIM)
    return [means[i * tm:(i + 1) * tm] for i in range(len(slabs))]


def _half_swap(x, first):
    return jnp.where(first, pltpu.roll(x, 96, 1), pltpu.roll(x, 32, 1))


def _by_head(x2, lo):
    z = jnp.zeros((BLK, 128), x2.dtype)
    parts = []
    for s in range(2):
        xs = x2[:, s * 128:(s + 1) * 128]
        parts += [jnp.where(lo, xs, z), jnp.where(lo, z, xs)]
    return jnp.concatenate(parts, axis=0)


def _from_heads(o4, lo):
    return jnp.concatenate([jnp.where(lo, o4[0:BLK], o4[BLK:2 * BLK]),
                            jnp.where(lo, o4[2 * BLK:3 * BLK], o4[3 * BLK:])], axis=1)


def _swa_probs(q2, kd, sink, n, lo):
    qp = _by_head(q2, lo)
    sc = _dot(qp, kd, NT) * (1.0 / math.sqrt(HEAD_DIM))
    r_i = lax.broadcasted_iota(jnp.int32, (4 * BLK, 2 * BLK), 0)
    k_j = lax.broadcasted_iota(jnp.int32, (4 * BLK, 2 * BLK), 1)
    diff = (r_i & (BLK - 1)) + BLK - k_j
    mask = (diff >= 0) & (diff < BLK) & ((k_j >= BLK) | (n > 0))
    sc = jnp.where(mask, sc, MINF)
    m = jnp.maximum(jnp.max(sc, axis=1, keepdims=True), sink)
    p = jnp.exp(sc - m)
    es = jnp.exp(sink - m)
    inv = 1.0 / (jnp.sum(p, axis=1, keepdims=True) + es)
    return qp, p * inv, es * inv


def mixer_core_fwd(proj, cos, sin, gq, gk, gvn, bmat, sinkcol, gao, w2, bsl, ggo):
    S = proj.shape[0]
    sub = 4 if S % (4 * BLK) == 0 else 1

    def body(p_ref, c_ref, s_ref, gq_ref, gk_ref, gvn_ref, b_ref, sk_ref, gao_ref, w2_ref, bsl_ref, ggo_ref,
             qr_ref, kr_ref, vb_ref, gu_ref, gvo_ref, at_ref, gm_ref, y_ref, k_prev, v_prev):
        n = pl.program_id(0)

        @pl.when(n == 0)
        def _():
            k_prev[...] = jnp.zeros_like(k_prev)
            v_prev[...] = jnp.zeros_like(v_prev)

        bm = b_ref[...]
        first = (_lane((BLK, 128)) & 63) < 32
        lo = _lane((BLK, 128)) < 64
        for sb in range(sub):
            rs = slice(sb * BLK, (sb + 1) * BLK)
            cos_v, sin_v = c_ref[rs, :], s_ref[rs, :]
            slabs = [p_ref[rs, s * 128:(s + 1) * 128] for s in range(6)]
            for s, (slab, ms) in enumerate(zip(slabs, _head_means([x * x for x in slabs], bm))):
                qn = slab * lax.rsqrt(ms + EPS) * (gq_ref[...] if s < 4 else gk_ref[...])
                out = qn * cos_v + _half_swap(qn, first) * sin_v
                if s < 4:
                    qr_ref[rs, s * 128:(s + 1) * 128] = out.astype(qr_ref.dtype)
                else:
                    kr_ref[rs, (s - 4) * 128:(s - 3) * 128] = out.astype(kr_ref.dtype)
            vb_ref[rs, :] = p_ref[rs, 768:1024].astype(vb_ref.dtype)
            gu_ref[rs, :] = _gelu(p_ref[rs, 1024:1536])
            gv = _gelu(p_ref[rs, 1536:2048])
            gvo_ref[rs, :] = (gv * _rms(gv) * gvn_ref[...]).astype(gvo_ref.dtype)

            before = slice((sb - 1) * BLK, sb * BLK)
            for h in range(2):
                hs, qs = slice(h * 128, (h + 1) * 128), slice(h * 256, (h + 1) * 256)
                k_before = k_prev[:, hs] if sb == 0 else kr_ref[before, hs]
                v_before = v_prev[:, hs] if sb == 0 else vb_ref[before, hs]
                kd = jnp.concatenate([k_before, kr_ref[rs, hs]], axis=0)
                vd = jnp.concatenate([v_before, vb_ref[rs, hs]], axis=0)
                sink = jnp.concatenate([sk_ref[2 * h], sk_ref[2 * h + 1]], axis=0)
                _, p, _ = _swa_probs(qr_ref[rs, qs], kd, sink, n * sub + sb, lo)
                at_ref[rs, qs] = _from_heads(_dot(p, vd), lo)

            for j in range(4):
                sl = slice(j * 128, (j + 1) * 128)
                m2 = _dot(w2_ref[j], gvo_ref[rs, sl])
                mixed = jnp.where(lo, m2[:BLK], m2[BLK:]) + bsl_ref[j]
                gm_ref[rs, sl] = gu_ref[rs, sl] * mixed
            a, gm = at_ref[rs, :], gm_ref[rs, :]
            y_ref[rs, :512] = (a * _rms(a) * gao_ref[...]).astype(y_ref.dtype)
            y_ref[rs, 512:] = (gm * _rms(gm) * ggo_ref[...]).astype(y_ref.dtype)
        k_prev[...] = kr_ref[(sub - 1) * BLK:, :]
        v_prev[...] = vb_ref[(sub - 1) * BLK:, :]

    row = lambda w: BS((sub * BLK, w), lambda n: (n, 0))
    const = lambda *shape: BS(shape, lambda n: (0,) * len(shape))
    return _pcall(body, name="mixer_core_fwd", grid=(S // (sub * BLK),),
                  in_specs=[row(IN_COLS_DUP), row(128), row(128), const(1, 128), const(1, 128), const(1, 512),
                            const(128, 128), const(4, 2 * BLK, 1), const(1, 512), const(4, 2 * BLK, BLK),
                            const(4, BLK, 128), const(1, 512)],
                  out_specs=[row(512), row(256), row(256), row(512), row(512), row(512), row(512), row(1024)],
                  out_shape=[SDS((S, 512), MXU_DTYPE), SDS((S, 256), MXU_DTYPE), SDS((S, 256), MXU_DTYPE),
                             SDS((S, 512), F32), SDS((S, 512), MXU_DTYPE), SDS((S, 512), F32), SDS((S, 512), F32),
                             SDS((S, 1024), MXU_DTYPE)],
                  scratch=[pltpu.VMEM((BLK, 256), MXU_DTYPE), pltpu.VMEM((BLK, 256), MXU_DTYPE)])(
        proj, cos, sin, gq, gk, gvn, bmat, sinkcol, gao, w2, bsl, ggo)


def mem_pre(kv, gxk):
    def body(kv_ref, g_ref, kn_ref, vb_ref):
        for h in range(XA_HEADS):
            sl = slice(h * XA_DH, (h + 1) * XA_DH)
            k = kv_ref[:, sl]
            kn_ref[:, sl] = (k * _rms(k) * g_ref[...]).astype(kn_ref.dtype)
        vb_ref[...] = kv_ref[:, 1024:2048].astype(vb_ref.dtype)

    full = lambda r, w: BS((r, w), lambda i: (0, 0))
    return _pcall(body, name="mem_pre", grid=(1,), in_specs=[full(MEM_LEN, 2048), full(1, XA_DH)],
                  out_specs=[full(MEM_LEN, 1024), full(MEM_LEN, 1024)],
                  out_shape=[SDS((MEM_LEN, 1024), MXU_DTYPE), SDS((MEM_LEN, 1024), MXU_DTYPE)])(kv, gxk)


def _xa_probs(qh, g, kn_h):
    r = _rms(qh)
    qn = qh * r * g
    s = _dot(qn, kn_h, NT) * (1.0 / math.sqrt(XA_DH))
    p = jnp.exp(s - jnp.max(s, axis=1, keepdims=True))
    return r, qn, p * (1.0 / jnp.sum(p, axis=1, keepdims=True))


def xattn_block_fwd(x1, g, wq, kn, vb, gxq, wo):
    S, D = x1.shape
    tm = _tile(S, (512, 256))

    def body(x_ref, g_ref, wq_ref, kn_ref, vb_ref, gxq_ref, wo_ref, h_ref, q_ref, o_ref, x2_ref):
        xv = x_ref[...]
        h_ref[...] = (xv * _rms(xv) * g_ref[...]).astype(h_ref.dtype)
        q_ref[...] = _dot(h_ref[...], wq_ref[...])
        for h in range(XA_HEADS):
            sl = slice(h * XA_DH, (h + 1) * XA_DH)
            _, _, p = _xa_probs(q_ref[:, sl], gxq_ref[...], kn_ref[:, sl])
            o_ref[:, sl] = _dot(p, vb_ref[:, sl]).astype(o_ref.dtype)
        x2_ref[...] = _dot(o_ref[...], wo_ref[...]) + xv

    row = BS((tm, D), lambda i: (i, 0))
    full = lambda r, w: BS((r, w), lambda i: (0, 0))
    return _pcall(body, name="xattn_block_fwd", grid=(S // tm,),
                  in_specs=[row, full(1, D), _resident(wq.shape), full(MEM_LEN, D), full(MEM_LEN, D), full(1, XA_DH),
                            _resident(wo.shape)],
                  out_specs=[row, row, row, row],
                  out_shape=[SDS((S, D), MXU_DTYPE), SDS((S, D), F32), SDS((S, D), MXU_DTYPE), SDS((S, D), F32)])(
        x1, g, wq, kn, vb, gxq, wo)


CONV_COLS = 1408


def _conv_taps(a_ref, halo_ref, w_ref, b_ref, cols, first_tile):
    a = a_ref[:, cols]
    row = lax.broadcasted_iota(jnp.int32, a.shape, 0)
    h6 = jnp.where(first_tile, 0.0, halo_ref[6:7, cols])
    h7 = jnp.where(first_tile, 0.0, halo_ref[7:8, cols])
    a1 = jnp.where(row == 0, h7, pltpu.roll(a, 1, 0))
    a2 = jnp.where(row == 0, h6, jnp.where(row == 1, h7, pltpu.roll(a, 2, 0)))
    c = w_ref[2:3, cols] * a + w_ref[1:2, cols] * a1 + w_ref[0:1, cols] * a2 + b_ref[:, cols]
    return c, (a2, a1, a)


def _conv_specs(tm):
    halo_blocks = tm // 8
    return [BS((tm, D_FF), lambda i: (i, 0)), BS((tm, D_FF), lambda i: (i, 1)),
            BS((8, D_FF), lambda i: (jnp.maximum(i * halo_blocks - 1, 0), 0)),
            BS((8, D_FF), lambda i: (jnp.maximum(i * halo_blocks - 1, 0), 1)),
            BS((3, D_FF), lambda i: (0, 0)), BS((3, D_FF), lambda i: (0, 1)),
            BS((1, D_FF), lambda i: (0, 0)), BS((1, D_FF), lambda i: (0, 1))]


def ffn_fwd_loss(x2, g, w_up3, cw, cb, w_down, target):
    S, D = x2.shape
    Q, _, C = w_up3.shape
    tm = _tile(S, (256,))

    def body(x_ref, g_ref, wu_ref, cw_ref, cb_ref, wd_ref, t_ref, h_ref, a_ref, f_ref, d_ref, l_ref, tail):
        first_tile = pl.program_id(0) == 0

        @pl.when(first_tile)
        def _():
            l_ref[...] = jnp.zeros_like(l_ref)
            tail[...] = jnp.zeros_like(tail)

        xv = x_ref[...]
        h_ref[...] = (xv * _rms(xv) * g_ref[...]).astype(h_ref.dtype)
        for q in range(Q):
            a_ref[:, q * C:(q + 1) * C] = _dot(h_ref[...], wu_ref[q])
        for c0 in range(0, D_FF, CONV_COLS):
            cols, ucols = slice(c0, c0 + CONV_COLS), slice(D_FF + c0, D_FF + c0 + CONV_COLS)
            cg, _ = _conv_taps(a_ref, tail, cw_ref, cb_ref, cols, first_tile)
            cu, _ = _conv_taps(a_ref, tail, cw_ref, cb_ref, ucols, first_tile)
            f_ref[:, cols] = (_gelu(cg) * cu).astype(f_ref.dtype)
        tail[...] = a_ref[tm - 8:tm, :]
        e = _dot(f_ref[...], wd_ref[...]) + xv - t_ref[...]
        d_ref[...] = e * (1.0 / D)
        l_ref[...] += jnp.sum(e * e) * (0.5 / D)

    row = lambda w: BS((tm, w), lambda i: (i, 0))
    const = lambda r, w: BS((r, w), lambda i: (0, 0))
    return _pcall(body, name="ffn_fwd_loss", grid=(S // tm,),
                  in_specs=[row(D), const(1, D), _resident(w_up3.shape), const(3, 2 * D_FF), const(1, 2 * D_FF),
                            _resident(w_down.shape), row(D)],
                  out_specs=[row(D), row(2 * D_FF), row(D_FF), row(D), const(8, 128)],
                  out_shape=[SDS((S, D), MXU_DTYPE), SDS((S, 2 * D_FF), F32), SDS((S, D_FF), MXU_DTYPE),
                             SDS((S, D), F32), SDS((8, 128), F32)],
                  scratch=[pltpu.VMEM((8, 2 * D_FF), F32)])(x2, g, w_up3, cw, cb, w_down, target)


def convgate_bwd(a, dx3, w3, cw, cb, after=None):
    S = a.shape[0]
    tm = _tile(S, (256,))

    def body(ag_ref, au_ref, hg_ref, hu_ref, wg_ref, wu_ref, bg_ref, bu_ref, dx_ref, wd_ref, dc_ref, gw_ref, df_ref):
        first_tile = pl.program_id(0) == 0

        @pl.when(first_tile)
        def _():
            gw_ref[...] = jnp.zeros_like(gw_ref)

        df_ref[...] = _nt_chunks(dx_ref, wd_ref)
        for c0 in range(0, D_FF, CONV_COLS):
            cols, ucols = slice(c0, c0 + CONV_COLS), slice(D_FF + c0, D_FF + c0 + CONV_COLS)
            cg, g_taps = _conv_taps(ag_ref, hg_ref, wg_ref, bg_ref, cols, first_tile)
            cu, u_taps = _conv_taps(au_ref, hu_ref, wu_ref, bu_ref, cols, first_tile)
            df_v = df_ref[:, cols]
            dcg = df_v * cu * _gelu_grad(cg)
            dcu = df_v * _gelu(cg)
            dc_ref[:, cols] = dcg
            dc_ref[:, ucols] = dcu
            for col, dcv, taps in ((cols, dcg, g_taps), (ucols, dcu, u_taps)):
                for j in range(3):
                    gw_ref[j:j + 1, col] += jnp.sum(dcv * taps[j], axis=0, keepdims=True)
                gw_ref[3:4, col] += jnp.sum(dcv, axis=0, keepdims=True)

    return _pcall(body, name="convgate_bwd", grid=(S // tm,), after=after,
                  in_specs=_conv_specs(tm) + [BS((tm, dx3.shape[1]), lambda i: (i, 0)), _resident(w3.shape)],
                  out_specs=[BS((tm, 2 * D_FF), lambda i: (i, 0)), BS((8, 2 * D_FF), lambda i: (0, 0))],
                  out_shape=[SDS((S, 2 * D_FF), F32), SDS((8, 2 * D_FF), F32)],
                  scratch=[pltpu.VMEM((tm, D_FF), F32)])(a, a, a, a, cw, cw, cb, cb, dx3, w3)


def conv_transpose_rms_bwd(dc, cw, w3, x, g, dres):
    S, C = dc.shape
    Q, N, Kc = w3.shape
    tm = _tile(S, (256,))
    nt = S // tm
    halo_blocks = tm // 8

    def body(dc_ref, halo_ref, cw_ref, w_ref, x_ref, g_ref, dr_ref, da_ref, dx_ref, dg_ref):
        @pl.when(pl.program_id(0) == 0)
        def _():
            dg_ref[...] = jnp.zeros_like(dg_ref)

        last_tile = pl.program_id(0) == nt - 1
        row = lax.broadcasted_iota(jnp.int32, (tm, CONV_COLS), 0)
        for c0 in range(0, C, CONV_COLS):
            cols = slice(c0, c0 + CONV_COLS)
            h0 = jnp.where(last_tile, 0.0, halo_ref[0:1, cols])
            h1 = jnp.where(last_tile, 0.0, halo_ref[1:2, cols])
            dc_v = dc_ref[:, cols]
            n1 = jnp.where(row == tm - 1, h0, pltpu.roll(dc_v, tm - 1, 0))
            n2 = jnp.where(row == tm - 1, h1, jnp.where(row == tm - 2, h0, pltpu.roll(dc_v, tm - 2, 0)))
            da_ref[:, cols] = (cw_ref[2:3, cols] * dc_v + cw_ref[1:2, cols] * n1
                               + cw_ref[0:1, cols] * n2).astype(da_ref.dtype)
        xv = x_ref[...]
        dx, dgc = _rms_bwd(_nt_chunks(da_ref, w_ref), xv, g_ref[...], _rms(xv))
        dx_ref[...] = dr_ref[...] + dx
        _acc_rows(dg_ref, 0, dgc)

    row_n = BS((tm, N), lambda i: (i, 0))
    return _pcall(body, name="conv_transpose_rms_bwd", grid=(nt,),
                  in_specs=[BS((tm, C), lambda i: (i, 0)),
                            BS((8, C), lambda i: (jnp.minimum((i + 1) * halo_blocks, S // 8 - 1), 0)),
                            BS((3, C), lambda i: (0, 0)), _resident((Q, N, Kc)), row_n, BS((1, N), lambda i: (0, 0)),
                            row_n],
                  out_specs=[BS((tm, C), lambda i: (i, 0)), row_n, BS((8, N), lambda i: (0, 0))],
                  out_shape=[SDS((S, C), MXU_DTYPE), SDS((S, N), F32), SDS((8, N), F32)])(dc, dc, cw, w3, x, g, dres)


def xattn_block_bwd(dx2, wo3, qx, kn, vb, gxq, wq3, x1, g, after=None):
    S, D = qx.shape
    tm = _tile(S, (512, 256))

    def body(dx2_ref, wo_ref, q_ref, kn_ref, vb_ref, gxq_ref, wq_ref, x_ref, g_ref,
             dq_ref, dx_ref, dkn_ref, dv_ref, dgq_ref, dg_ref):
        @pl.when(pl.program_id(0) == 0)
        def _():
            for ref in (dkn_ref, dv_ref, dgq_ref, dg_ref):
                ref[...] = jnp.zeros_like(ref)

        gq = gxq_ref[...]
        do_all = _nt_chunks(dx2_ref, wo_ref)
        for h in range(XA_HEADS):
            sl = slice(h * XA_DH, (h + 1) * XA_DH)
            qh, do = q_ref[:, sl], do_all[:, sl]
            r, qn, p = _xa_probs(qh, gq, kn_ref[:, sl])
            dp = _dot(do, vb_ref[:, sl], NT)
            ds = p * (dp - jnp.sum(dp * p, axis=1, keepdims=True)) * (1.0 / math.sqrt(XA_DH))
            dqn = _dot(ds, kn_ref[:, sl])
            dkn_ref[:, sl] += _dot(ds, qn, TN)
            dv_ref[:, sl] += _dot(p, do, TN)
            dqh, dgc = _rms_bwd(dqn, qh, gq, r)
            dq_ref[:, sl] = dqh.astype(dq_ref.dtype)
            _acc_rows(dgq_ref, 0, dgc)
        xv = x_ref[...]
        dx, dgc = _rms_bwd(_nt_chunks(dq_ref, wq_ref), xv, g_ref[...], _rms(xv))
        dx_ref[...] = dx2_ref[...] + dx
        _acc_rows(dg_ref, 0, dgc)

    row = BS((tm, D), lambda i: (i, 0))
    full = lambda r, w: BS((r, w), lambda i: (0, 0))
    return _pcall(body, name="xattn_block_bwd", grid=(S // tm,), after=after,
                  in_specs=[row, _resident(wo3.shape), row, full(MEM_LEN, D), full(MEM_LEN, D), full(1, XA_DH),
                            _resident(wq3.shape), row, full(1, D)],
                  out_specs=[row, row, full(MEM_LEN, D), full(MEM_LEN, D), full(8, XA_DH), full(8, D)],
                  out_shape=[SDS((S, D), MXU_DTYPE), SDS((S, D), F32), SDS((MEM_LEN, D), F32), SDS((MEM_LEN, D), F32),
                             SDS((8, XA_DH), F32), SDS((8, D), F32)])(dx2, wo3, qx, kn, vb, gxq, wq3, x1, g)


def mem_bwd(kv, dkn, dvb, gxk, after=None):
    def body(kv_ref, dkn_ref, dv_ref, g_ref, dkv_ref, dg_ref):
        dg_ref[...] = jnp.zeros_like(dg_ref)
        for h in range(XA_HEADS):
            sl = slice(h * XA_DH, (h + 1) * XA_DH)
            k = kv_ref[:, sl]
            dk, dgc = _rms_bwd(dkn_ref[:, sl], k, g_ref[...], _rms(k))
            dkv_ref[:, sl] = dk.astype(dkv_ref.dtype)
            _acc_rows(dg_ref, 0, dgc)
        dkv_ref[:, 1024:2048] = dv_ref[...].astype(dkv_ref.dtype)

    full = lambda r, w: BS((r, w), lambda i: (0, 0))
    return _pcall(body, name="mem_bwd", grid=(1,), after=after,
                  in_specs=[full(MEM_LEN, 2048), full(MEM_LEN, 1024), full(MEM_LEN, 1024), full(1, XA_DH)],
                  out_specs=[full(MEM_LEN, 2048), full(8, XA_DH)],
                  out_shape=[SDS((MEM_LEN, 2048), MXU_DTYPE), SDS((8, XA_DH), F32)])(kv, dkn, dvb, gxk)


def _norm_rope_bwd(slabs, douts, g, bm, cos_v, sin_v, first):
    dqns = [d * cos_v + _half_swap(d * sin_v, first) for d in douts]
    rs = [lax.rsqrt(ms + EPS) for ms in _head_means([x * x for x in slabs], bm)]
    projs = _head_means([dqn * g * x for dqn, x in zip(dqns, slabs)], bm)
    dxs = [r * (dqn * g) - x * (r * r * r) * pr for x, dqn, r, pr in zip(slabs, dqns, rs, projs)]
    return dxs, [dqn * x * r for x, dqn, r in zip(slabs, dqns, rs)]


def mixer_core_bwd(proj, cos, sin, gq, gk, gvg, bmat, qr, kr, vb, sinkcol, dattn, dgm, gvn, gu, w2, w2t, bsl,
                   after=None):
    S = qr.shape[0]
    nb = S // BLK

    def body(p_ref, c_ref, s_ref, gq_ref, gk_ref, gvg_ref, b_ref, q_ref, kc_ref, kp_ref, vc_ref, vp_ref, sk_ref,
             do_ref, dgm_ref, gvn_ref, gu_ref, w2_ref, w2t_ref, bsl_ref,
             dp_ref, dsk_ref, dws_ref, dbl_ref, dgq_ref, dgk_ref, dgv_ref,
             carry_k, carry_v, done_k, done_v, dq_keep, dgu_keep, dgvn_keep):
        n = pl.program_id(0)

        @pl.when(n == 0)
        def _():
            for ref in (dsk_ref, dws_ref, dbl_ref, dgq_ref, dgk_ref, dgv_ref, carry_k, carry_v, dq_keep, dgu_keep,
                        dgvn_keep):
                ref[...] = jnp.zeros_like(ref)

        live = (n < nb).astype(F32)
        cos_v, sin_v, bm = c_ref[...], s_ref[...], b_ref[...]
        first = (_lane((BLK, 128)) & 63) < 32
        lo = _lane((BLK, 128)) < 64

        dxs, dgs = _norm_rope_bwd([p_ref[:, s * 128:(s + 1) * 128] for s in range(4)],
                                  [dq_keep[:, s * 128:(s + 1) * 128] for s in range(4)], gq_ref[...], bm,
                                  cos_v, sin_v, first)
        for s, (dx, dg) in enumerate(zip(dxs, dgs)):
            dp_ref[:, s * 128:(s + 1) * 128] = dx.astype(dp_ref.dtype)
            _acc_rows(dgq_ref, 0, dg)
        dp_ref[:, 1024:1536] = (dgu_keep[...] * _gelu_grad(p_ref[:, 1024:1536])).astype(dp_ref.dtype)
        gvp = p_ref[:, 1536:2048]
        gv = _gelu(gvp)
        dgv, dgc = _rms_bwd(dgvn_keep[...], gv, gvg_ref[...], _rms(gv))
        dp_ref[:, 1536:2048] = (dgv * _gelu_grad(gvp)).astype(dp_ref.dtype)
        _acc_rows(dgv_ref, 0, dgc)

        for h in range(2):
            hs, qs = slice(h * 128, (h + 1) * 128), slice(h * 256, (h + 1) * 256)
            kd = jnp.concatenate([kp_ref[:, hs], kc_ref[:, hs]], axis=0)
            vd = jnp.concatenate([vp_ref[:, hs], vc_ref[:, hs]], axis=0)
            sink = jnp.concatenate([sk_ref[2 * h], sk_ref[2 * h + 1]], axis=0)
            qp, p, psink = _swa_probs(q_ref[:, qs], kd, sink, n, lo)
            dop = _by_head(do_ref[:, qs], lo)
            dp = _dot(dop, vd, NT)
            delta = jnp.sum(dp * p, axis=1, keepdims=True)
            ds = p * (dp - delta) * (1.0 / math.sqrt(HEAD_DIM))
            dsink = -psink * delta * live
            dsk_ref[2 * h] += dsink[:2 * BLK]
            dsk_ref[2 * h + 1] += dsink[2 * BLK:]
            dq_keep[:, qs] = _from_heads(_dot(ds, kd), lo)
            dkd = _dot(ds, qp, TN)
            dvd = _dot(p, dop, TN)
            done_k[:, hs] = carry_k[:, hs] + live * dkd[:BLK]
            done_v[:, hs] = carry_v[:, hs] + live * dvd[:BLK]
            carry_k[:, hs] = dkd[BLK:]
            carry_v[:, hs] = dvd[BLK:]
        for j in range(4):
            sl = slice(j * 128, (j + 1) * 128)
            gvn_s = gvn_ref[:, sl]
            m2 = _dot(w2_ref[j], gvn_s)
            mixed = jnp.where(lo, m2[:BLK], m2[BLK:]) + bsl_ref[j]
            dgm_s = dgm_ref[:, sl]
            dgu_keep[:, sl] = dgm_s * mixed
            dmx = dgm_s * gu_ref[:, sl] * live
            d2 = _dot(w2t_ref[j], dmx)
            dgvn_keep[:, sl] = jnp.where(lo, d2[:BLK], d2[BLK:])
            z = jnp.zeros_like(dmx)
            dws_ref[2 * j] += _dot(jnp.where(lo, dmx, z), gvn_s, NT)
            dws_ref[2 * j + 1] += _dot(jnp.where(lo, z, dmx), gvn_s, NT)
            dbl_ref[j] += dmx

        dxs, dgs = _norm_rope_bwd([p_ref[:, 512 + s * 128:640 + s * 128] for s in range(2)],
                                  [done_k[:, s * 128:(s + 1) * 128] for s in range(2)], gk_ref[...], bm,
                                  cos_v, sin_v, first)
        for s, (dx, dg) in enumerate(zip(dxs, dgs)):
            dp_ref[:, 512 + s * 128:640 + s * 128] = dx.astype(dp_ref.dtype)
            _acc_rows(dgk_ref, 0, dg)
        dp_ref[:, 768:1024] = done_v[...].astype(dp_ref.dtype)

    last = nb - 1
    cur = lambda w: BS((BLK, w), lambda n: (jnp.minimum(n, last), 0))
    prev = lambda w: BS((BLK, w), lambda n: (jnp.clip(n - 1, 0, last), 0))
    done = lambda w: BS((BLK, w), lambda n: (jnp.maximum(n - 1, 0), 0))
    const = lambda *shape: BS(shape, lambda n: (0,) * len(shape))
    return _pcall(body, name="mixer_core_bwd", grid=(nb + 1,), after=after,
                  in_specs=[done(IN_COLS_DUP), done(128), done(128), const(1, 128), const(1, 128), const(1, 512),
                            const(128, 128), cur(512), cur(256), prev(256), cur(256), prev(256),
                            const(4, 2 * BLK, 1), cur(512), cur(512), cur(512), cur(512), const(4, 2 * BLK, BLK),
                            const(4, 2 * BLK, BLK), const(4, BLK, 128)],
                  out_specs=[done(IN_COLS_DUP), const(4, 2 * BLK, 1), const(8, BLK, BLK), const(4, BLK, 128),
                             const(8, 128), const(8, 128), const(8, 512)],
                  out_shape=[SDS((S, IN_COLS_DUP), MXU_DTYPE), SDS((4, 2 * BLK, 1), F32), SDS((8, BLK, BLK), F32),
                             SDS((4, BLK, 128), F32), SDS((8, 128), F32), SDS((8, 128), F32), SDS((8, 512), F32)],
                  scratch=[pltpu.VMEM((BLK, 256), F32)] * 4 + [pltpu.VMEM((BLK, 512), F32)] * 3)(
        proj, cos, sin, gq, gk, gvg, bmat, qr, kr, kr, vb, vb, sinkcol, dattn, dgm, gvn, gu, w2, w2t, bsl)


BIG = (("w_in", (1024, 448), True), ("w_out", (256, 1024), False), ("xa_wq", (256, 1024), False),
       ("xa_wkv", (1024, 512), True), ("xa_wo", (256, 1024), False), ("ffn_up", (1024, 1408), True),
       ("ffn_down", (704, 1024), False))
BIG_NAMES = tuple(n for n, _, _ in BIG)
SMALL_VECS = (("mix_norm", 1024), ("q_norm", 64), ("k_norm", 64), ("attn_sinks", 8), ("gmlp_v_norm", 512),
              ("attn_out_norm", 512), ("gmlp_out_norm", 512), ("xa_norm", 1024), ("mem_norm", 1024),
              ("xa_q_norm", 256), ("xa_k_norm", 256), ("ffn_norm", 1024), ("ffn_conv_b", 5632))
SMALL = tuple(n for n, _ in SMALL_VECS) + ("gmlp_bs", "gmlp_ws", "ffn_conv")
WEIGHTS = ("mix_norm", "w_in", "q_norm", "k_norm", "attn_sinks", "gmlp_v_norm", "gmlp_ws", "gmlp_bs",
           "attn_out_norm", "gmlp_out_norm", "w_out", "xa_norm", "mem_norm", "xa_wq", "xa_wkv", "xa_q_norm",
           "xa_k_norm", "xa_wo", "ffn_norm", "ffn_up", "ffn_conv", "ffn_conv_b", "ffn_down")
CONV_SHARD = (3, 1408)
CONV_LANE_ROWS = CONV_SHARD[1] // 128
CONV_CHIP_ROWS = 40


def _small_rows():
    rows, r = {}, 0
    for n, length in SMALL_VECS:
        rows[n] = r
        r += -(-length // 128)
    r += -r % 8
    rows["gmlp_bs"] = r
    r += 8
    rows["gmlp_ws"] = r
    r += 8 * BLK
    rows["ffn_conv"] = r
    r += N_CHIPS * CONV_CHIP_ROWS
    return rows, r


SMALL_ROW, SMALL_ROWS = _small_rows()


def pack_small(dg_mix, dgq, dgk, dsk, dg_gvn, dg_y, dg_xa, dg_mem, dg_xq, dg_xk, dg_ffn, gcw, dbl, dws):
    def body(mix_ref, q_ref, k_ref, sk_ref, gvn_ref, y_ref, xa_ref, mem_ref, xq_ref, xk_ref, ffn_ref, cw_ref,
             dbl_ref, dws_ref, o_ref):
        o_ref[...] = jnp.zeros_like(o_ref)
        lane = _lane((1, 128))

        def put(name, src_ref, row, lane0, length):
            for k in range(length // 128):
                o_ref[SMALL_ROW[name] + k:SMALL_ROW[name] + k + 1, :] = src_ref[row:row + 1, lane0 + k * 128:lane0 + (k + 1) * 128]

        put("mix_norm", mix_ref, 0, 0, 1024)
        for name, ref in (("q_norm", q_ref), ("k_norm", k_ref)):
            v = ref[0:1, :]
            o_ref[SMALL_ROW[name]:SMALL_ROW[name] + 1, :] = jnp.where(lane < HEAD_DIM, v + pltpu.roll(v, 64, 1), 0.0)
        sinks = jnp.zeros((1, 128), F32)
        for s in range(4):
            col = sk_ref[s]
            sinks = sinks + jnp.where(lane == 2 * s, jnp.sum(col[:BLK]), 0.0) + jnp.where(lane == 2 * s + 1, jnp.sum(col[BLK:]), 0.0)
        o_ref[SMALL_ROW["attn_sinks"]:SMALL_ROW["attn_sinks"] + 1, :] = sinks
        put("gmlp_v_norm", gvn_ref, 0, 0, 512)
        put("attn_out_norm", y_ref, 0, 0, 512)
        put("gmlp_out_norm", y_ref, 0, 512, 512)
        put("xa_norm", xa_ref, 0, 0, 1024)
        put("mem_norm", mem_ref, 0, 0, 1024)
        put("xa_q_norm", xq_ref, 0, 0, 256)
        put("xa_k_norm", xk_ref, 0, 0, 256)
        put("ffn_norm", ffn_ref, 0, 0, 1024)
        put("ffn_conv_b", cw_ref, 3, 0, 2 * D_FF)
        r8 = lax.broadcasted_iota(jnp.int32, (8, 128), 0)
        l8 = _lane((8, 128))
        bs = jnp.zeros((8, BLK), F32)
        for j in range(4):
            sel = (((r8 == 2 * j) & (l8 < 64)) | ((r8 == 2 * j + 1) & (l8 >= 64))).astype(F32).astype(BF16)
            xj = dbl_ref[j]
            hi = xj.astype(BF16)
            lo = (xj - hi.astype(F32)).astype(BF16)
            bs = bs + lax.dot_general(sel, hi, NT, preferred_element_type=F32) + lax.dot_general(sel, lo, NT, preferred_element_type=F32)
        o_ref[SMALL_ROW["gmlp_bs"]:SMALL_ROW["gmlp_bs"] + 8, :] = bs
        causal = lax.broadcasted_iota(jnp.int32, (BLK, BLK), 0) >= lax.broadcasted_iota(jnp.int32, (BLK, BLK), 1)
        for h in range(8):
            r0 = SMALL_ROW["gmlp_ws"] + h * BLK
            o_ref[r0:r0 + BLK, :] = jnp.where(causal, dws_ref[h], 0.0)
        for q in range(N_CHIPS):
            for j in range(3):
                for k in range(CONV_LANE_ROWS):
                    r0 = SMALL_ROW["ffn_conv"] + q * CONV_CHIP_ROWS + j * CONV_LANE_ROWS + k
                    l0 = (q * CONV_LANE_ROWS + k) * 128
                    o_ref[r0:r0 + 1, :] = cw_ref[j:j + 1, l0:l0 + 128]

    args = (dg_mix, dgq, dgk, dsk, dg_gvn, dg_y, dg_xa, dg_mem, dg_xq, dg_xk, dg_ffn, gcw, dbl, dws)
    full = lambda a: BS(a.shape, lambda i, nd=a.ndim: (0,) * nd)
    return _pcall(body, name="pack_small", grid=(1,), in_specs=[full(a) for a in args],
                  out_specs=BS((SMALL_ROWS, 128), lambda i: (0, 0)), out_shape=SDS((SMALL_ROWS, 128), F32))(*args)


def _adam(w, g, m, v):
    mn = ADAM_B1 * m + (1.0 - ADAM_B1) * g
    vn = ADAM_B2 * v + (1.0 - ADAM_B2) * (g * g)
    m_hat = mn / (1.0 - ADAM_B1 ** ADAM_STEP)
    v_hat = vn / (1.0 - ADAM_B2 ** ADAM_STEP)
    return -ADAM_LR * (m_hat / (jnp.sqrt(v_hat) + ADAM_EPS) + ADAM_WD * w), mn, vn


def adamw_small(gsum, w, m, v, chipvec):
    n = len(SMALL)

    def body(chip_ref, g_ref, *refs):
        w_refs, m_refs, v_refs = refs[:n], refs[n:2 * n], refs[2 * n:3 * n]
        outs = refs[3 * n:]
        go, do, mo, vo = outs[:n], outs[n:2 * n], outs[2 * n:3 * n], outs[3 * n:]

        def update(i, idx, g):
            d, mn, vn = _adam(w_refs[i][idx], g, m_refs[i][idx], v_refs[i][idx])
            go[i][idx] = g
            do[i][idx] = d
            mo[i][idx] = mn
            vo[i][idx] = vn

        for i, (name, length) in enumerate(SMALL_VECS):
            for k in range(-(-length // 128)):
                wd = min(128, length - k * 128)
                r = SMALL_ROW[name] + k
                update(i, (slice(0, 1), slice(k * 128, k * 128 + wd)), g_ref[r:r + 1, 0:wd])
        i_bs, i_ws, i_cv = len(SMALL_VECS), len(SMALL_VECS) + 1, len(SMALL_VECS) + 2
        update(i_bs, (0,), g_ref[SMALL_ROW["gmlp_bs"]:SMALL_ROW["gmlp_bs"] + 8, :])
        for h in range(8):
            r0 = SMALL_ROW["gmlp_ws"] + h * BLK
            update(i_ws, (0, h), g_ref[r0:r0 + BLK, :])
        mine = g_ref[pl.ds(pl.multiple_of(SMALL_ROW["ffn_conv"] + chip_ref[0] * CONV_CHIP_ROWS, 8), CONV_CHIP_ROWS), :]
        for j in range(3):
            for k in range(CONV_LANE_ROWS):
                r = j * CONV_LANE_ROWS + k
                update(i_cv, (0, slice(j, j + 1), slice(k * 128, (k + 1) * 128)), mine[r:r + 1, :])

    nat = [w[nm] for nm in SMALL]
    full = lambda a: BS(a.shape, lambda i, c, nd=a.ndim: (0,) * nd)
    outs = _pcall(body, name="adamw_small", grid=(1,), prefetch=1,
                  in_specs=[BS((SMALL_ROWS, 128), lambda i, c: (0, 0))] + [full(a) for a in nat] * 3,
                  out_specs=[full(a) for a in nat] * 4, out_shape=[SDS(a.shape, F32) for a in nat] * 4)(
        chipvec, gsum, *nat, *[m[nm] for nm in SMALL], *[v[nm] for nm in SMALL])
    return outs[:n], outs[n:2 * n], outs[2 * n:3 * n], outs[3 * n:]


def adamw_matrix(w, m, v, g_own, g_other, cvec, *, name):
    _, r, c = w.shape
    half = r // 2
    tr = _tile(half, (128, 176))
    T = half // tr

    def body(c_ref, w_ref, m_ref, v_ref, own_ref, oth_ref, g_ref, d_ref, mo_ref, vo_ref):
        g = jnp.where(pl.program_id(0) == c_ref[0], own_ref[...], oth_ref[...])
        d, mn, vn = _adam(w_ref[...], g, m_ref[...], v_ref[...])
        g_ref[...] = g
        d_ref[...] = d
        mo_ref[...] = mn
        vo_ref[...] = vn

    nat = BS((None, tr, c), lambda hf, t, cr: (0, hf * T + t, 0))
    hlf = BS((tr, c), lambda hf, t, cr: (t, 0))
    return _pcall(body, name=name, grid=(2, T), prefetch=1, in_specs=[nat, nat, nat, hlf, hlf], out_specs=[nat] * 4,
                  out_shape=[SDS(w.shape, F32)] * 4)(cvec, w, m, v, g_own, g_other)


def _place():
    return lax.axis_index("x"), lax.axis_index("y"), lax.axis_index("c")


def _other_chips(x, y):
    return [(1 - x, y), (x, 1 - y), (1 - x, 1 - y)]


def _rows_of_core(c, half):
    return pl.ds(pl.multiple_of(c * half, 16), half)


def _rcopy(src, dst, sems, k, to):
    return pltpu.make_async_remote_copy(src_ref=src, dst_ref=dst, send_sem=sems[0].at[k], recv_sem=sems[1].at[k],
                                        device_id=to, device_id_type=MESH)


def _comm_call(body, *, name, out_shape, n_in, n_sems, aliases=None):
    return pl.pallas_call(body, name=name, out_shape=out_shape, in_specs=[ANY] * n_in, out_specs=[ANY] * len(out_shape),
                          scratch_shapes=[pltpu.SemaphoreType.DMA((n_sems,)), pltpu.SemaphoreType.DMA((n_sems,))],
                          input_output_aliases=aliases or {},
                          compiler_params=pltpu.CompilerParams(has_side_effects=True))


def cast_shards(shards, conv, chipvec):
    n = len(shards)

    def body(chip_ref, *refs):
        for i_ref, o_ref in zip(refs[:n + 1], refs[n + 1:]):
            o_ref[...] = i_ref[...].astype(o_ref.dtype)

    in_specs = [BS((s.shape[0] // 4, s.shape[1]), lambda i, p: (i, 0)) for s in shards]
    in_specs.append(BS(conv.shape, lambda i, p: (0, 0)))
    out_specs = [BS((None, s.shape[0] // 4, s.shape[1]), lambda i, p: (p[0], i, 0)) for s in shards]
    out_specs.append(BS((None,) + conv.shape, lambda i, p: (p[0], 0, 0)))
    out_shape = [SDS((N_CHIPS,) + s.shape, MXU_DTYPE) for s in shards] + [SDS((N_CHIPS,) + conv.shape, F32)]
    return _pcall(body, name="cast_shards", grid=(4,), prefetch=1, in_specs=in_specs, out_specs=out_specs,
                  out_shape=out_shape)(chipvec, *shards, conv)


HBM = pl.BlockSpec(memory_space=pltpu.HBM)
SEM = pl.BlockSpec(memory_space=pltpu.SEMAPHORE)
DATAFLOW = pltpu.SideEffectType.DATAFLOW_SIDE_EFFECTING
VMEM_WHOLE = pl.BlockSpec(memory_space=pltpu.VMEM)
TOKEN = jax.ShapeDtypeStruct((8, 128), jnp.float32)


def _gather_copies(bufs, send_sems, recv_sems, outgoing):
    x, y, c = _place()
    p = 2 * x + y
    cps = []
    for i, o in enumerate(bufs):
        for j, (cx, cy) in enumerate(_other_chips(x, y)):
            slot = o.at[p] if outgoing else o.at[2 * cx + cy]
            cps.append(_rcopy(slot, slot, (send_sems, recv_sems), 3 * i + j, (cx, cy, c)))
    return cps


def gather_start(slots, after):
    n = len(slots)

    def body(*refs):
        send_sems, recv_sems, thru, token = refs[n + 1], refs[n + 2], refs[n + 3:2 * n + 3], refs[2 * n + 3]
        for cp in _gather_copies(thru, send_sems, recv_sems, True):
            cp.start()
        token[...] = jnp.zeros_like(token)

    hbm = [pltpu.with_memory_space_constraint(s, pltpu.HBM) for s in slots]
    outs = pl.pallas_call(
        body, name="gather_start_%d" % n,
        out_shape=[pltpu.SemaphoreType.DMA((3 * n,)), pltpu.SemaphoreType.DMA((3 * n,))]
        + [pltpu.HBM(s.shape, s.dtype) for s in slots] + [TOKEN],
        in_specs=[HBM] * n + [ANY], out_specs=[SEM, SEM] + [HBM] * n + [VMEM_WHOLE],
        input_output_aliases={i: 2 + i for i in range(n)},
        compiler_params=pltpu.CompilerParams(has_side_effects=DATAFLOW))(*hbm, after)
    return outs[0], outs[1], outs[2:2 + n], outs[2 + n]


def gather_wait(send_sems, recv_sems, bufs, *after):
    n = len(bufs)

    def body(*refs):
        ins, send_ref, recv_ref = refs[:n], refs[n], refs[n + 1]
        for cp in _gather_copies(ins, send_ref, recv_ref, False):
            cp.wait_send()
            cp.wait_recv()

    return pl.pallas_call(
        body, name="gather_wait_%d" % n, out_shape=[pltpu.HBM(s.shape, s.dtype) for s in bufs],
        in_specs=[HBM] * n + [SEM, SEM] + [ANY] * len(after), out_specs=[HBM] * n,
        input_output_aliases={i: i for i in range(n)},
        compiler_params=pltpu.CompilerParams(has_side_effects=DATAFLOW))(*bufs, send_sems, recv_sems, *after)


def _peers(x, y, c):
    return [(1 - x if k & 4 else x, 1 - y if k & 2 else y, 1 - c if k & 1 else c) for k in range(1, N_DEV)]


def _partial_copies(g_ref, land_ref, send_sems, recv_sems, outgoing):
    x, y, c = _place()
    half = g_ref.shape[1] // 2
    cps = []
    for k, (px, py, pc) in enumerate(_peers(x, y, c)):
        src = g_ref.at[2 * px + py, _rows_of_core(pc, half)]
        dst = land_ref.at[4 * x + 2 * y + c] if outgoing else land_ref.at[4 * px + 2 * py + pc]
        cps.append(_rcopy(src, dst, (send_sems, recv_sems), k, (px, py, pc)))
    return cps


def partials_start(g, *, name):
    land = lax.empty((N_DEV, g.shape[1] // 2, g.shape[2]), g.dtype)

    def body(g_ref, land_ref, send_sems, recv_sems, g_thru, land_thru, token):
        for cp in _partial_copies(g_thru, land_thru, send_sems, recv_sems, True):
            cp.start()
        token[...] = jnp.zeros_like(token)

    return pl.pallas_call(
        body, name=name,
        out_shape=[pltpu.SemaphoreType.DMA((N_DEV - 1,)), pltpu.SemaphoreType.DMA((N_DEV - 1,)),
                   pltpu.HBM(g.shape, g.dtype), pltpu.HBM(land.shape, land.dtype), TOKEN],
        in_specs=[HBM, HBM], out_specs=[SEM, SEM, HBM, HBM, VMEM_WHOLE], input_output_aliases={0: 2, 1: 3},
        compiler_params=pltpu.CompilerParams(has_side_effects=DATAFLOW))(
        pltpu.with_memory_space_constraint(g, pltpu.HBM), pltpu.with_memory_space_constraint(land, pltpu.HBM))


def partials_wait(started, after):
    n = len(started)

    def body(*refs):
        for i in range(n):
            send_ref, recv_ref, g_ref, land_ref = refs[4 * i:4 * i + 4]
            for cp in _partial_copies(g_ref, land_ref, send_ref, recv_ref, False):
                cp.wait_send()
                cp.wait_recv()

    flat = [a for s in started for a in s]
    bufs = [a for s in started for a in s[2:]]
    outs = pl.pallas_call(
        body, name="partials_wait", out_shape=[pltpu.HBM(b.shape, b.dtype) for b in bufs],
        in_specs=[SEM, SEM, HBM, HBM] * n + [ANY], out_specs=[HBM] * (2 * n),
        input_output_aliases={4 * i + 2 + j: 2 * i + j for i in range(n) for j in range(2)},
        compiler_params=pltpu.CompilerParams(has_side_effects=DATAFLOW))(*flat, after)
    return [(outs[2 * i], outs[2 * i + 1]) for i in range(n)]


def sum_partials(pairs, order):
    n = len(pairs)

    def body(o_ref, *refs):
        j = pl.program_id(0)
        for g_ref, l_ref, f_ref in zip(refs[:n], refs[n:2 * n], refs[2 * n:]):
            @pl.when(j == 0)
            def _():
                f_ref[...] = g_ref[...].astype(F32)

            @pl.when(j > 0)
            def _():
                f_ref[...] += l_ref[...].astype(F32)

    g4 = [g.reshape(g.shape[0], 2, g.shape[1] // 2, g.shape[2]) for g, _ in pairs]
    lands = [l for _, l in pairs]
    return _pcall(body, name="sum_partials", grid=(N_DEV,), prefetch=1,
                  in_specs=[BS((None, None) + g.shape[2:], lambda j, o: (o[0], o[1], 0, 0)) for g in g4]
                  + [BS((None,) + l.shape[1:], lambda j, o: (o[jnp.maximum(j, 1) + 1], 0, 0)) for l in lands],
                  out_specs=[BS(l.shape[1:], lambda j, o: (0, 0)) for l in lands],
                  out_shape=[SDS(l.shape[1:], F32) for l in lands])(order, *g4, *lands)


def pair_share(fs):
    n = len(fs)

    def body(*refs):
        f_refs, o_refs, sems = refs[:n], refs[n:2 * n], refs[2 * n:]
        x, y, c = _place()
        cps = [_rcopy(f, o, sems, i, (x, y, 1 - c)) for i, (f, o) in enumerate(zip(f_refs, o_refs))]
        for cp in cps:
            cp.start()
        for cp in cps:
            cp.wait()

    return _comm_call(body, name="pair_share", n_in=n, n_sems=n, out_shape=[SDS(f.shape, f.dtype) for f in fs])(*fs)


def _small_copies(s_ref, land_ref, send_sems, recv_sems, outgoing):
    x, y, c = _place()
    cps = []
    for k, (px, py, pc) in enumerate(_peers(x, y, c)):
        dst = land_ref.at[4 * x + 2 * y + c] if outgoing else land_ref.at[4 * px + 2 * py + pc]
        cps.append(_rcopy(s_ref, dst, (send_sems, recv_sems), k, (px, py, pc)))
    return cps


def small_start(sm):
    land = lax.empty((N_DEV,) + sm.shape, sm.dtype)

    def body(s_ref, land_ref, send_sems, recv_sems, s_thru, land_thru):
        for cp in _small_copies(s_thru, land_thru, send_sems, recv_sems, True):
            cp.start()

    return pl.pallas_call(
        body, name="small_start",
        out_shape=[pltpu.SemaphoreType.DMA((N_DEV - 1,)), pltpu.SemaphoreType.DMA((N_DEV - 1,)),
                   pltpu.HBM(sm.shape, sm.dtype), pltpu.HBM(land.shape, land.dtype)],
        in_specs=[HBM, HBM], out_specs=[SEM, SEM, HBM, HBM], input_output_aliases={0: 2, 1: 3},
        compiler_params=pltpu.CompilerParams(has_side_effects=DATAFLOW))(
        pltpu.with_memory_space_constraint(sm, pltpu.HBM), pltpu.with_memory_space_constraint(land, pltpu.HBM))


def small_wait(send_sems, recv_sems, sm, land, after):
    def body(send_ref, recv_ref, s_ref, land_ref, after_ref, s_out, land_out):
        for cp in _small_copies(s_ref, land_ref, send_ref, recv_ref, False):
            cp.wait_send()
            cp.wait_recv()

    return pl.pallas_call(
        body, name="small_wait", out_shape=[pltpu.HBM(sm.shape, sm.dtype), pltpu.HBM(land.shape, land.dtype)],
        in_specs=[SEM, SEM, HBM, HBM, ANY], out_specs=[HBM, HBM], input_output_aliases={2: 0, 3: 1},
        compiler_params=pltpu.CompilerParams(has_side_effects=DATAFLOW))(send_sems, recv_sems, sm, land, after)


def sum_small(own, land, mevec):
    n, rows, width = land.shape
    tr = _tile(rows, (184, 8))

    def body(me_ref, own_ref, land_ref, o_ref):
        acc = jnp.zeros((tr, width), F32)
        for s in range(n):
            acc = acc + jnp.where(me_ref[0] == s, own_ref[...], land_ref[s])
        o_ref[...] = acc

    return _pcall(body, name="sum_small", grid=(rows // tr,), prefetch=1,
                  in_specs=[BS((tr, width), lambda i, me: (i, 0)), BS((n, tr, width), lambda i, me: (0, i, 0))],
                  out_specs=BS((tr, width), lambda i, me: (i, 0)), out_shape=SDS((rows, width), F32))(mevec, own, land)


def _to_full(blk, col):
    n, r, c = blk.shape
    return blk.transpose(1, 0, 2).reshape(r, n * c) if col else blk.reshape(n * r, c)


def _dup_cols(w):
    dup = lambda t: jnp.concatenate([t[:, :64], t[:, :64], t[:, 64:], t[:, 64:]], axis=1)
    return jnp.concatenate([w[:, :512], dup(w[:, 512:640]), dup(w[:, 640:768]), w[:, 768:]], axis=1)


def _fold_cols(d):
    fold = lambda t: jnp.concatenate([t[:, 0:64] + t[:, 64:128], t[:, 128:192] + t[:, 192:256]], axis=1)
    return jnp.concatenate([d[:, :512], fold(d[:, 512:768]), fold(d[:, 768:1024]), d[:, 1024:]], axis=1)


def _local_step(x, mem, positions, target, w_in, later, sp, emit):
    gain = lambda n: sp[n].reshape(1, -1)
    half = HEAD_DIM // 2
    inv_freq = 1.0 / (10000.0 ** (jnp.arange(half, dtype=F32) * (2.0 / HEAD_DIM)))
    ang = positions.astype(F32)[:, None] * inv_freq
    cos, sin = jnp.cos(ang), jnp.sin(ang)
    cos128 = jnp.tile(cos, (1, 4))
    sin128 = jnp.concatenate([-sin, sin, -sin, sin], axis=1)
    seg = jnp.arange(128) // HEAD_DIM
    bmat = (seg[:, None] == seg[None, :]).astype(BF16)
    gq128, gk128 = jnp.tile(gain("q_norm"), (1, 2)), jnp.tile(gain("k_norm"), (1, 2))
    sinkcol = jnp.repeat(sp["attn_sinks"].reshape(4, 2), BLK, axis=1).reshape(4, 2 * BLK, 1)
    wsc = sp["gmlp_ws"] * jnp.tril(jnp.ones((BLK, BLK), F32))[None]
    w2 = wsc.reshape(4, 2 * BLK, BLK).astype(MXU_DTYPE)
    w2t = wsc.swapaxes(1, 2).reshape(4, 2 * BLK, BLK).astype(MXU_DTYPE)
    bsl = jnp.repeat(sp["gmlp_bs"].reshape(4, 2, BLK).transpose(0, 2, 1), HEAD_DIM, axis=2)
    cb = sp["ffn_conv_b"].reshape(1, -1)
    w_in_d = _dup_cols(_to_full(w_in(cos128, sin128, gq128, gk128, sinkcol, w2, w2t, bsl), True))[None]

    h1, proj = rms_mm(x, gain("mix_norm"), w_in_d, name="mix_in")
    qr, kr, vb, gu, gvn, attn, gm, y = mixer_core_fwd(proj, cos128, sin128, gq128, gk128, gain("gmlp_v_norm"), bmat,
                                                      sinkcol, gain("attn_out_norm"), w2, bsl, gain("gmlp_out_norm"))
    wf, last = later(y)
    w_out, xa_wq, xa_wo = (_to_full(wf[n], False) for n in ("w_out", "xa_wq", "xa_wo"))
    x1 = mm(y, w_out, res=x, name="mix_out")
    mn, kv = rms_mm(mem, gain("mem_norm"), wf["xa_wkv"], name="xa_kv")
    kn, vbx = mem_pre(kv, gain("xa_k_norm"))
    h2, qx, xo, x2 = xattn_block_fwd(x1, gain("xa_norm"), xa_wq, kn, vbx, gain("xa_q_norm"), xa_wo)
    ffn_w, cw = last(x2)
    wf = {**wf, **ffn_w}
    ffn_down = _to_full(wf["ffn_down"], False)
    h3, a, f, dx3, loss_acc = ffn_fwd_loss(x2, gain("ffn_norm"), wf["ffn_up"], cw, cb, ffn_down, target)

    by_rows = lambda g: g.reshape(N_CHIPS, g.shape[1] // N_CHIPS, g.shape[2])
    sent = emit("ffn_down", by_rows(mm_tn(f, dx3, name="g_ffn_down", out_dtype=WIRE_DTYPE)))
    dc, gcw = convgate_bwd(a, dx3, ffn_down[None], cw, cb, after=sent)
    da, dx2, dg_ffn = conv_transpose_rms_bwd(dc, cw, wf["ffn_up"], x2, gain("ffn_norm"), dx3)
    sent = emit("ffn_up", mm_tn(h3, da, name="g_ffn_up", out_dtype=WIRE_DTYPE, chunks=N_CHIPS))
    sent = emit("xa_wo", by_rows(mm_tn(xo, dx2, name="g_xa_wo", out_dtype=WIRE_DTYPE, after=sent)))
    dqx, dx1, dkn, dvx, dg_xq, dg_xa = xattn_block_bwd(dx2, xa_wo[None], qx, kn, vbx, gain("xa_q_norm"), xa_wq[None],
                                                       x1, gain("xa_norm"), after=sent)
    sent = emit("xa_wq", by_rows(mm_tn(h2, dqx, name="g_xa_wq", out_dtype=WIRE_DTYPE)))
    dkv, dg_xk = mem_bwd(kv, dkn, dvx, gain("xa_k_norm"), after=sent)
    _, dg_mem = mm_nt_rms_bwd(dkv, wf["xa_wkv"], mem, gain("mem_norm"), jnp.zeros_like(mem), name="d_mem")
    sent = emit("xa_wkv", mm_tn(mn, dkv, name="g_xa_wkv", out_dtype=WIRE_DTYPE, chunks=N_CHIPS))
    dattn, dgm, dg_y = mm_nt_post_bwd(dx1, w_out[None], attn, gm, gain("attn_out_norm"), gain("gmlp_out_norm"),
                                      name="d_mix_out", after=sent)
    sent = emit("w_out", by_rows(mm_tn(y, dx1, name="g_w_out", out_dtype=WIRE_DTYPE)))
    dproj, dsk, dws, dbl, dgq, dgk, dg_gvn = mixer_core_bwd(
        proj, cos128, sin128, gq128, gk128, gain("gmlp_v_norm"), bmat, qr, kr, vb, sinkcol, dattn, dgm, gvn, gu,
        w2, w2t, bsl, after=sent)
    g_in = _fold_cols(mm_tn(h1, dproj, name="g_w_in", out_dtype=F32)[0])
    sent = emit("w_in", g_in.reshape(1024, N_CHIPS, 448).transpose(1, 0, 2).astype(WIRE_DTYPE))
    grad_x, dg_mix = mm_nt_rms_bwd(dproj, w_in_d, x, gain("mix_norm"), dx1, name="d_x", after=sent)
    packed = pack_small(dg_mix, dgq, dgk, dsk, dg_gvn, dg_y, dg_xa, dg_mem, dg_xq, dg_xk, dg_ffn, gcw, dbl, dws)
    return loss_acc, grad_x, packed


def _gather_step(w, chipvec):
    slots = cast_shards([w[n][0] for n in BIG_NAMES], w["ffn_conv"][0], chipvec)
    send_a, recv_a, first, token = gather_start(slots[:1], chipvec)
    send_b, recv_b, mid, token = gather_start(slots[1:5], token)
    send_c, recv_c, rest, token = gather_start(slots[5:], token)

    def w_in(*after):
        return gather_wait(send_a, recv_a, first, token, *after)[0]

    def last(after):
        got = gather_wait(send_c, recv_c, rest, after)
        return dict(zip(BIG_NAMES[5:], got[:-1])), _to_full(got[-1], True)

    def later(after):
        return dict(zip(BIG_NAMES[1:5], gather_wait(send_b, recv_b, mid, after))), last

    return w_in, later, token


def _reduce_update(started, packed, w, m, v, chipvec, cvec, order):
    small_sent = small_start(packed)
    own = sum_partials(partials_wait([started[n] for n in BIG_NAMES], small_sent[2]), order)
    other = pair_share(own)
    res = [{}, {}, {}, {}]
    for n, g_own, g_other in zip(BIG_NAMES, own, other):
        for d, o in zip(res, adamw_matrix(w[n], m[n], v[n], g_own, g_other, cvec, name="adamw_" + n)):
            d[n] = o
    mevec = (2 * order[0:1] + order[1:2]).astype(jnp.int32)
    small_sum = sum_small(*small_wait(*small_sent, res[3][BIG_NAMES[-1]]), mevec)
    for d, outs in zip(res, adamw_small(small_sum, w, m, v, chipvec)):
        d.update(zip(SMALL, outs))
    return res


def kernel(x, mem, positions, mix_norm, w_in, q_norm, k_norm, attn_sinks, gmlp_v_norm, gmlp_ws, gmlp_bs, attn_out_norm, gmlp_out_norm, w_out, xa_norm, mem_norm, xa_wq, xa_wkv, xa_q_norm, xa_k_norm, xa_wo, ffn_norm, ffn_up, ffn_conv, ffn_conv_b, ffn_down, loss_target, m_mix_norm, m_w_in, m_q_norm, m_k_norm, m_attn_sinks, m_gmlp_v_norm, m_gmlp_ws, m_gmlp_bs, m_attn_out_norm, m_gmlp_out_norm, m_w_out, m_xa_norm, m_mem_norm, m_xa_wq, m_xa_wkv, m_xa_q_norm, m_xa_k_norm, m_xa_wo, m_ffn_norm, m_ffn_up, m_ffn_conv, m_ffn_conv_b, m_ffn_down, v_mix_norm, v_w_in, v_q_norm, v_k_norm, v_attn_sinks, v_gmlp_v_norm, v_gmlp_ws, v_gmlp_bs, v_attn_out_norm, v_gmlp_out_norm, v_w_out, v_xa_norm, v_mem_norm, v_xa_wq, v_xa_wkv, v_xa_q_norm, v_xa_k_norm, v_xa_wo, v_ffn_norm, v_ffn_up, v_ffn_conv, v_ffn_conv_b, v_ffn_down):
    w = dict(mix_norm=mix_norm, w_in=w_in, q_norm=q_norm, k_norm=k_norm, attn_sinks=attn_sinks, gmlp_v_norm=gmlp_v_norm, gmlp_ws=gmlp_ws, gmlp_bs=gmlp_bs, attn_out_norm=attn_out_norm, gmlp_out_norm=gmlp_out_norm, w_out=w_out, xa_norm=xa_norm, mem_norm=mem_norm, xa_wq=xa_wq, xa_wkv=xa_wkv, xa_q_norm=xa_q_norm, xa_k_norm=xa_k_norm, xa_wo=xa_wo, ffn_norm=ffn_norm, ffn_up=ffn_up, ffn_conv=ffn_conv, ffn_conv_b=ffn_conv_b, ffn_down=ffn_down)
    m = dict(mix_norm=m_mix_norm, w_in=m_w_in, q_norm=m_q_norm, k_norm=m_k_norm, attn_sinks=m_attn_sinks, gmlp_v_norm=m_gmlp_v_norm, gmlp_ws=m_gmlp_ws, gmlp_bs=m_gmlp_bs, attn_out_norm=m_attn_out_norm, gmlp_out_norm=m_gmlp_out_norm, w_out=m_w_out, xa_norm=m_xa_norm, mem_norm=m_mem_norm, xa_wq=m_xa_wq, xa_wkv=m_xa_wkv, xa_q_norm=m_xa_q_norm, xa_k_norm=m_xa_k_norm, xa_wo=m_xa_wo, ffn_norm=m_ffn_norm, ffn_up=m_ffn_up, ffn_conv=m_ffn_conv, ffn_conv_b=m_ffn_conv_b, ffn_down=m_ffn_down)
    v = dict(mix_norm=v_mix_norm, w_in=v_w_in, q_norm=v_q_norm, k_norm=v_k_norm, attn_sinks=v_attn_sinks, gmlp_v_norm=v_gmlp_v_norm, gmlp_ws=v_gmlp_ws, gmlp_bs=v_gmlp_bs, attn_out_norm=v_attn_out_norm, gmlp_out_norm=v_gmlp_out_norm, w_out=v_w_out, xa_norm=v_xa_norm, mem_norm=v_mem_norm, xa_wq=v_xa_wq, xa_wkv=v_xa_wkv, xa_q_norm=v_xa_q_norm, xa_k_norm=v_xa_k_norm, xa_wo=v_xa_wo, ffn_norm=v_ffn_norm, ffn_up=v_ffn_up, ffn_conv=v_ffn_conv, ffn_conv_b=v_ffn_conv_b, ffn_down=v_ffn_down)
    ix, iy, ic = lax.axis_index("x"), lax.axis_index("y"), lax.axis_index("c")
    chip = 2 * ix + iy
    chipvec = chip.astype(jnp.int32).reshape(1)
    cvec = ic.astype(jnp.int32).reshape(1)
    order = jnp.stack([chip, ic] + [4 * px + 2 * py + pc for px, py, pc in _peers(ix, iy, ic)]).astype(jnp.int32)

    w_in_all, later, token = _gather_step(w, chipvec)
    zero = token[0, 0]
    sp = {n: w[n][0] + zero for n in SMALL if n != "ffn_conv"}
    positions = positions + zero.astype(jnp.int32)
    started = {}

    def emit(name, g):
        *started[name], token = partials_start(g, name="partials_start_" + name)
        return token

    loss_acc, grad_x, packed = _local_step(x[0], mem[0], positions[0], loss_target[0], w_in_all, later, sp, emit)
    grads, delta, new_m, new_v = _reduce_update(started, packed, w, m, v, chipvec, cvec, order)
    loss = lax.psum(loss_acc[0, 0], ("x", "y", "c"))
    ordered = lambda d: [d[n] for n in WEIGHTS]
    return (loss, grad_x[None], *ordered(grads), *ordered(delta), *ordered(new_m), *ordered(new_v))
```

```python
import math

import jax
import jax.numpy as jnp
from jax import lax
from jax.experimental import pallas as pl
from jax.experimental.pallas import tpu as pltpu

F32 = jnp.float32
BF16 = jnp.bfloat16
MXU_DTYPE = jnp.bfloat16
WIRE_DTYPE = jnp.bfloat16
EPS = 1e-6
VMEM_LIMIT_V7X = 56 * 1024 * 1024

D_MODEL = 1024
HEAD_DIM = 64
BLK = 128
XA_HEADS = 4
XA_DH = 256
MEM_LEN = 256
D_FF = 2816
IN_COLS_DUP = 2048
N_CHIPS = 4
N_DEV = 8

ADAM_LR = 0.001
ADAM_B1 = 0.9
ADAM_B2 = 0.999
ADAM_EPS = 1e-08
ADAM_WD = 0.01
ADAM_STEP = 10

NT = (((1,), (1,)), ((), ()))
TN = (((0,), (0,)), ((), ()))
NN = (((1,), (0,)), ((), ()))
MINF = float(jnp.finfo(jnp.float32).min)
GELU_K0 = math.sqrt(2.0 / math.pi)
GELU_K1 = 0.044715

BS = pl.BlockSpec
SDS = jax.ShapeDtypeStruct
ANY = pl.BlockSpec(memory_space=pl.ANY)
MESH = pl.DeviceIdType.MESH


def _dot(a, b, dims=NN):
    return lax.dot_general(a.astype(MXU_DTYPE), b.astype(MXU_DTYPE), dims, preferred_element_type=F32)


def _segsum(x, bmat):
    hi = x.astype(BF16)
    lo = (x - hi.astype(F32)).astype(BF16)
    return (jnp.dot(hi, bmat, preferred_element_type=F32) + jnp.dot(lo, bmat, preferred_element_type=F32))


def _gelu(x):
    return 0.5 * x * (1.0 + jnp.tanh(GELU_K0 * (x + GELU_K1 * x * x * x)))


def _gelu_grad(x):
    t = jnp.tanh(GELU_K0 * (x + GELU_K1 * x * x * x))
    return 0.5 * (1.0 + t) + 0.5 * x * (1.0 - t * t) * GELU_K0 * (1.0 + 3.0 * GELU_K1 * x * x)


def _rms(x):
    return lax.rsqrt(jnp.mean(x * x, axis=-1, keepdims=True) + EPS)


def _rms_bwd(dy, x, g, r):
    dyg = dy * g
    dx = r * dyg - x * (r * r * r) * jnp.mean(dyg * x, axis=-1, keepdims=True)
    return dx, dy * x * r


def _pcall(body, *, name, grid, in_specs, out_specs, out_shape, scratch=(), prefetch=0, after=None):
    params = pltpu.CompilerParams(dimension_semantics=("arbitrary",) * len(grid), vmem_limit_bytes=VMEM_LIMIT_V7X)
    in_specs = list(in_specs)
    kernel_fn = body
    if after is not None:
        n_in = prefetch + len(in_specs)
        in_specs.append(ANY)

        def kernel_fn(*refs):
            return body(*refs[:n_in], *refs[n_in + 1:])

    if prefetch:
        spec = pltpu.PrefetchScalarGridSpec(num_scalar_prefetch=prefetch, grid=grid, in_specs=in_specs,
                                            out_specs=out_specs, scratch_shapes=list(scratch))
        call = pl.pallas_call(kernel_fn, name=name, grid_spec=spec, out_shape=out_shape, compiler_params=params)
    else:
        call = pl.pallas_call(kernel_fn, name=name, grid=grid, in_specs=in_specs, out_specs=out_specs,
                              out_shape=out_shape, scratch_shapes=list(scratch), compiler_params=params)
    return call if after is None else (lambda *args: call(*args, after))


def _tile(n, prefs):
    for p in prefs:
        if p <= n and n % p == 0:
            return p
    return n


def _resident(shape):
    return pl.BlockSpec(shape, lambda *_: (0,) * len(shape), pipeline_mode=pl.Buffered(1))


def _acc_rows(ref, row, val):
    ref[row:row + 1, :] += jnp.sum(val, axis=0, keepdims=True)


def rms_mm(x, g, w3, *, name, tm=1024):
    M, K = x.shape
    Q, _, C = w3.shape
    tm = _tile(M, (tm, 256))

    def body(x_ref, g_ref, w_ref, h_ref, o_ref):
        def write_h():
            xv = x_ref[...]
            h_ref[...] = (xv * _rms(xv) * g_ref[...]).astype(h_ref.dtype)

        if Q == 1:
            write_h()
        else:
            pl.when(pl.program_id(1) == 0)(write_h)
        o_ref[...] = _dot(h_ref[...], w_ref[pl.program_id(1)])

    return _pcall(body, name=name, grid=(M // tm, Q),
                  in_specs=[BS((tm, K), lambda i, j: (i, 0)), BS((1, K), lambda i, j: (0, 0)),
                            _resident((Q, K, C))],
                  out_specs=[BS((tm, K), lambda i, j: (i, 0)), BS((tm, C), lambda i, j: (i, j))],
                  out_shape=[SDS((M, K), MXU_DTYPE), SDS((M, Q * C), F32)])(x, g, w3)


def mm(a, w, *, name, res):
    M, K = a.shape
    N = w.shape[1]
    tm = _tile(M, (1024, 256))

    def body(a_ref, w_ref, r_ref, o_ref):
        o_ref[...] = _dot(a_ref[...], w_ref[...]) + r_ref[...]

    return _pcall(body, name=name, grid=(M // tm,),
                  in_specs=[BS((tm, K), lambda i: (i, 0)), _resident((K, N)), BS((tm, N), lambda i: (i, 0))],
                  out_specs=BS((tm, N), lambda i: (i, 0)), out_shape=SDS((M, N), F32))(a, w, res)


def _nt_chunks(a_ref, w_ref):
    q_n, _, kc = w_ref.shape
    acc = _dot(a_ref[:, 0:kc], w_ref[0], NT)
    for q in range(1, q_n):
        acc = acc + _dot(a_ref[:, q * kc:(q + 1) * kc], w_ref[q], NT)
    return acc


def mm_nt_rms_bwd(a, w3, x, g, dres, *, name, tm=512, after=None):
    M = a.shape[0]
    Q, N, Kc = w3.shape
    tm = _tile(M, (tm, 256))

    def body(a_ref, w_ref, x_ref, g_ref, dr_ref, dx_ref, dg_ref):
        @pl.when(pl.program_id(0) == 0)
        def _():
            dg_ref[...] = jnp.zeros_like(dg_ref)

        xv = x_ref[...]
        dx, dgc = _rms_bwd(_nt_chunks(a_ref, w_ref), xv, g_ref[...], _rms(xv))
        dx_ref[...] = dr_ref[...] + dx
        _acc_rows(dg_ref, 0, dgc)

    row = BS((tm, N), lambda i: (i, 0))
    return _pcall(body, name=name, grid=(M // tm,), after=after,
                  in_specs=[BS((tm, Q * Kc), lambda i: (i, 0)), _resident((Q, N, Kc)), row,
                            BS((1, N), lambda i: (0, 0)), row],
                  out_specs=[row, BS((8, N), lambda i: (0, 0))],
                  out_shape=[SDS((M, N), F32), SDS((8, N), F32)])(a, w3, x, g, dres)


def mm_nt_post_bwd(a, w3, attn, gm, gao, ggo, *, name, after=None):
    M = a.shape[0]
    Q, N, Kc = w3.shape
    tm = _tile(M, (512, 256))
    hw = N // 2

    def body(a_ref, w_ref, at_ref, gm_ref, gao_ref, ggo_ref, da_ref, dgm_ref, dg_ref):
        @pl.when(pl.program_id(0) == 0)
        def _():
            dg_ref[...] = jnp.zeros_like(dg_ref)

        dy = _nt_chunks(a_ref, w_ref)
        av, gmv = at_ref[...], gm_ref[...]
        da, dga = _rms_bwd(dy[:, :hw], av, gao_ref[...], _rms(av))
        dgm, dgg = _rms_bwd(dy[:, hw:], gmv, ggo_ref[...], _rms(gmv))
        da_ref[...] = da
        dgm_ref[...] = dgm
        dg_ref[0:1, :hw] += jnp.sum(dga, axis=0, keepdims=True)
        dg_ref[0:1, hw:] += jnp.sum(dgg, axis=0, keepdims=True)

    half = BS((tm, hw), lambda i: (i, 0))
    const = lambda r, w: BS((r, w), lambda i: (0, 0))
    return _pcall(body, name=name, grid=(M // tm,), after=after,
                  in_specs=[BS((tm, Q * Kc), lambda i: (i, 0)), _resident((Q, N, Kc)), half, half,
                            const(1, hw), const(1, hw)],
                  out_specs=[half, half, const(8, N)],
                  out_shape=[SDS((M, hw), F32), SDS((M, hw), F32), SDS((8, N), F32)])(a, w3, attn, gm, gao, ggo)


def mm_tn(a, b, *, name, out_dtype, chunks=1, after=None):
    M, K = a.shape
    N = b.shape[1]
    C = N // chunks
    tm = _tile(M, (1024, 256))
    tk = _tile(K, (1408, 1024, 512))
    tn = _tile(C, (1408, 1024, 512))
    per = C // tn
    nm = M // tm

    def body(a_ref, b_ref, o_ref, acc):
        m = pl.program_id(2)

        @pl.when(m == 0)
        def _():
            acc[...] = jnp.zeros_like(acc)

        acc[...] += _dot(a_ref[...], b_ref[...], TN)

        @pl.when(m == nm - 1)
        def _():
            o_ref[...] = acc[...].astype(o_ref.dtype)

    return _pcall(body, name=name, grid=(K // tk, N // tn, nm), after=after,
                  in_specs=[BS((tm, tk), lambda k, n, m: (m, k)), BS((tm, tn), lambda k, n, m: (m, n))],
                  out_specs=BS((None, tk, tn), lambda k, n, m: (n // per, k, n % per)),
                  out_shape=SDS((chunks, K, C), out_dtype), scratch=[pltpu.VMEM((tk, tn), F32)])(a, b)


def _lane(shape):
    return lax.broadcasted_iota(jnp.int32, shape, 1)


def _head_means(slabs, bmat):
    tm = slabs[0].shape[0]
    means = _segsum(jnp.concatenate(slabs, axis=0), bmat) * (1.0 / HEAD_DIM)
    return [means[i * tm:(i + 1) * tm] for i in range(len(slabs))]


def _half_swap(x, first):
    return jnp.where(first, pltpu.roll(x, 96, 1), pltpu.roll(x, 32, 1))


def _by_head(x2, lo):
    z = jnp.zeros((BLK, 128), x2.dtype)
    parts = []
    for s in range(2):
        xs = x2[:, s * 128:(s + 1) * 128]
        parts += [jnp.where(lo, xs, z), jnp.where(lo, z, xs)]
    return jnp.concatenate(parts, axis=0)


def _from_heads(o4, lo):
    return jnp.concatenate([jnp.where(lo, o4[0:BLK], o4[BLK:2 * BLK]),
                            jnp.where(lo, o4[2 * BLK:3 * BLK], o4[3 * BLK:])], axis=1)


def _swa_probs(q2, kd, sink, n, lo):
    qp = _by_head(q2, lo)
    sc = _dot(qp, kd, NT) * (1.0 / math.sqrt(HEAD_DIM))
    r_i = lax.broadcasted_iota(jnp.int32, (4 * BLK, 2 * BLK), 0)
    k_j = lax.broadcasted_iota(jnp.int32, (4 * BLK, 2 * BLK), 1)
    diff = (r_i & (BLK - 1)) + BLK - k_j
    mask = (diff >= 0) & (diff < BLK) & ((k_j >= BLK) | (n > 0))
    sc = jnp.where(mask, sc, MINF)
    m = jnp.maximum(jnp.max(sc, axis=1, keepdims=True), sink)
    p = jnp.exp(sc - m)
    es = jnp.exp(sink - m)
    inv = 1.0 / (jnp.sum(p, axis=1, keepdims=True) + es)
    return qp, p * inv, es * inv


def mixer_core_fwd(proj, cos, sin, gq, gk, gvn, bmat, sinkcol, gao, w2, bsl, ggo):
    S = proj.shape[0]
    sub = 4 if S % (4 * BLK) == 0 else 1

    def body(p_ref, c_ref, s_ref, gq_ref, gk_ref, gvn_ref, b_ref, sk_ref, gao_ref, w2_ref, bsl_ref, ggo_ref,
             qr_ref, kr_ref, vb_ref, gu_ref, gvo_ref, at_ref, gm_ref, y_ref, k_prev, v_prev):
        n = pl.program_id(0)

        @pl.when(n == 0)
        def _():
            k_prev[...] = jnp.zeros_like(k_prev)
            v_prev[...] = jnp.zeros_like(v_prev)

        bm = b_ref[...]
        first = (_lane((BLK, 128)) & 63) < 32
        lo = _lane((BLK, 128)) < 64
        for sb in range(sub):
            rs = slice(sb * BLK, (sb + 1) * BLK)
            cos_v, sin_v = c_ref[rs, :], s_ref[rs, :]
            slabs = [p_ref[rs, s * 128:(s + 1) * 128] for s in range(6)]
            for s, (slab, ms) in enumerate(zip(slabs, _head_means([x * x for x in slabs], bm))):
                qn = slab * lax.rsqrt(ms + EPS) * (gq_ref[...] if s < 4 else gk_ref[...])
                out = qn * cos_v + _half_swap(qn, first) * sin_v
                if s < 4:
                    qr_ref[rs, s * 128:(s + 1) * 128] = out.astype(qr_ref.dtype)
                else:
                    kr_ref[rs, (s - 4) * 128:(s - 3) * 128] = out.astype(kr_ref.dtype)
            vb_ref[rs, :] = p_ref[rs, 768:1024].astype(vb_ref.dtype)
            gu_ref[rs, :] = _gelu(p_ref[rs, 1024:1536])
            gv = _gelu(p_ref[rs, 1536:2048])
            gvo_ref[rs, :] = (gv * _rms(gv) * gvn_ref[...]).astype(gvo_ref.dtype)

            before = slice((sb - 1) * BLK, sb * BLK)
            for h in range(2):
                hs, qs = slice(h * 128, (h + 1) * 128), slice(h * 256, (h + 1) * 256)
                k_before = k_prev[:, hs] if sb == 0 else kr_ref[before, hs]
                v_before = v_prev[:, hs] if sb == 0 else vb_ref[before, hs]
                kd = jnp.concatenate([k_before, kr_ref[rs, hs]], axis=0)
                vd = jnp.concatenate([v_before, vb_ref[rs, hs]], axis=0)
                sink = jnp.concatenate([sk_ref[2 * h], sk_ref[2 * h + 1]], axis=0)
                _, p, _ = _swa_probs(qr_ref[rs, qs], kd, sink, n * sub + sb, lo)
                at_ref[rs, qs] = _from_heads(_dot(p, vd), lo)

            for j in range(4):
                sl = slice(j * 128, (j + 1) * 128)
                m2 = _dot(w2_ref[j], gvo_ref[rs, sl])
                mixed = jnp.where(lo, m2[:BLK], m2[BLK:]) + bsl_ref[j]
                gm_ref[rs, sl] = gu_ref[rs, sl] * mixed
            a, gm = at_ref[rs, :], gm_ref[rs, :]
            y_ref[rs, :512] = (a * _rms(a) * gao_ref[...]).astype(y_ref.dtype)
            y_ref[rs, 512:] = (gm * _rms(gm) * ggo_ref[...]).astype(y_ref.dtype)
        k_prev[...] = kr_ref[(sub - 1) * BLK:, :]
        v_prev[...] = vb_ref[(sub - 1) * BLK:, :]

    row = lambda w: BS((sub * BLK, w), lambda n: (n, 0))
    const = lambda *shape: BS(shape, lambda n: (0,) * len(shape))
    return _pcall(body, name="mixer_core_fwd", grid=(S // (sub * BLK),),
                  in_specs=[row(IN_COLS_DUP), row(128), row(128), const(1, 128), const(1, 128), const(1, 512),
                            const(128, 128), const(4, 2 * BLK, 1), const(1, 512), const(4, 2 * BLK, BLK),
                            const(4, BLK, 128), const(1, 512)],
                  out_specs=[row(512), row(256), row(256), row(512), row(512), row(512), row(512), row(1024)],
                  out_shape=[SDS((S, 512), MXU_DTYPE), SDS((S, 256), MXU_DTYPE), SDS((S, 256), MXU_DTYPE),
                             SDS((S, 512), F32), SDS((S, 512), MXU_DTYPE), SDS((S, 512), F32), SDS((S, 512), F32),
                             SDS((S, 1024), MXU_DTYPE)],
                  scratch=[pltpu.VMEM((BLK, 256), MXU_DTYPE), pltpu.VMEM((BLK, 256), MXU_DTYPE)])(
        proj, cos, sin, gq, gk, gvn, bmat, sinkcol, gao, w2, bsl, ggo)


def mem_pre(kv, gxk):
    def body(kv_ref, g_ref, kn_ref, vb_ref):
        for h in range(XA_HEADS):
            sl = slice(h * XA_DH, (h + 1) * XA_DH)
            k = kv_ref[:, sl]
            kn_ref[:, sl] = (k * _rms(k) * g_ref[...]).astype(kn_ref.dtype)
        vb_ref[...] = kv_ref[:, 1024:2048].astype(vb_ref.dtype)

    full = lambda r, w: BS((r, w), lambda i: (0, 0))
    return _pcall(body, name="mem_pre", grid=(1,), in_specs=[full(MEM_LEN, 2048), full(1, XA_DH)],
                  out_specs=[full(MEM_LEN, 1024), full(MEM_LEN, 1024)],
                  out_shape=[SDS((MEM_LEN, 1024), MXU_DTYPE), SDS((MEM_LEN, 1024), MXU_DTYPE)])(kv, gxk)


def _xa_probs(qh, g, kn_h):
    r = _rms(qh)
    qn = qh * r * g
    s = _dot(qn, kn_h, NT) * (1.0 / math.sqrt(XA_DH))
    p = jnp.exp(s - jnp.max(s, axis=1, keepdims=True))
    return r, qn, p * (1.0 / jnp.sum(p, axis=1, keepdims=True))


def xattn_block_fwd(x1, g, wq, kn, vb, gxq, wo):
    S, D = x1.shape
    tm = _tile(S, (512, 256))

    def body(x_ref, g_ref, wq_ref, kn_ref, vb_ref, gxq_ref, wo_ref, h_ref, q_ref, o_ref, x2_ref):
        xv = x_ref[...]
        h_ref[...] = (xv * _rms(xv) * g_ref[...]).astype(h_ref.dtype)
        q_ref[...] = _dot(h_ref[...], wq_ref[...])
        for h in range(XA_HEADS):
            sl = slice(h * XA_DH, (h + 1) * XA_DH)
            _, _, p = _xa_probs(q_ref[:, sl], gxq_ref[...], kn_ref[:, sl])
            o_ref[:, sl] = _dot(p, vb_ref[:, sl]).astype(o_ref.dtype)
        x2_ref[...] = _dot(o_ref[...], wo_ref[...]) + xv

    row = BS((tm, D), lambda i: (i, 0))
    full = lambda r, w: BS((r, w), lambda i: (0, 0))
    return _pcall(body, name="xattn_block_fwd", grid=(S // tm,),
                  in_specs=[row, full(1, D), _resident(wq.shape), full(MEM_LEN, D), full(MEM_LEN, D), full(1, XA_DH),
                            _resident(wo.shape)],
                  out_specs=[row, row, row, row],
                  out_shape=[SDS((S, D), MXU_DTYPE), SDS((S, D), F32), SDS((S, D), MXU_DTYPE), SDS((S, D), F32)])(
        x1, g, wq, kn, vb, gxq, wo)


CONV_COLS = 1408


def _conv_taps(a_ref, halo_ref, w_ref, b_ref, cols, first_tile):
    a = a_ref[:, cols]
    row = lax.broadcasted_iota(jnp.int32, (8, a.shape[1]), 0)
    h6 = jnp.where(first_tile, 0.0, halo_ref[6:7, cols])
    h7 = jnp.where(first_tile, 0.0, halo_ref[7:8, cols])
    r1, r2 = pltpu.roll(a, 1, 0), pltpu.roll(a, 2, 0)
    a1 = jnp.concatenate([jnp.where(row == 0, h7, r1[0:8]), r1[8:]], axis=0)
    a2 = jnp.concatenate([jnp.where(row == 0, h6, jnp.where(row == 1, h7, r2[0:8])), r2[8:]], axis=0)
    c = w_ref[2:3, cols] * a + w_ref[1:2, cols] * a1 + w_ref[0:1, cols] * a2 + b_ref[:, cols]
    return c, (a2, a1, a)


def _conv_specs(tm):
    halo_blocks = tm // 8
    return [BS((tm, D_FF), lambda i: (i, 0)), BS((tm, D_FF), lambda i: (i, 1)),
            BS((8, D_FF), lambda i: (jnp.maximum(i * halo_blocks - 1, 0), 0)),
            BS((8, D_FF), lambda i: (jnp.maximum(i * halo_blocks - 1, 0), 1)),
            BS((3, D_FF), lambda i: (0, 0)), BS((3, D_FF), lambda i: (0, 1)),
            BS((1, D_FF), lambda i: (0, 0)), BS((1, D_FF), lambda i: (0, 1))]


def ffn_fwd_loss(x2, g, w_up3, cw, cb, w_down, target):
    S, D = x2.shape
    Q, _, C = w_up3.shape
    tm = _tile(S, (256,))

    def body(x_ref, g_ref, wu_ref, cw_ref, cb_ref, wd_ref, t_ref, h_ref, a_ref, f_ref, d_ref, l_ref, tail):
        first_tile = pl.program_id(0) == 0

        @pl.when(first_tile)
        def _():
            l_ref[...] = jnp.zeros_like(l_ref)
            tail[...] = jnp.zeros_like(tail)

        xv = x_ref[...]
        h_ref[...] = (xv * _rms(xv) * g_ref[...]).astype(h_ref.dtype)
        for q in range(Q):
            a_ref[:, q * C:(q + 1) * C] = _dot(h_ref[...], wu_ref[q])
        for c0 in range(0, D_FF, CONV_COLS):
            cols, ucols = slice(c0, c0 + CONV_COLS), slice(D_FF + c0, D_FF + c0 + CONV_COLS)
            cg, _ = _conv_taps(a_ref, tail, cw_ref, cb_ref, cols, first_tile)
            cu, _ = _conv_taps(a_ref, tail, cw_ref, cb_ref, ucols, first_tile)
            f_ref[:, cols] = (_gelu(cg) * cu).astype(f_ref.dtype)
        tail[...] = a_ref[tm - 8:tm, :]
        e = _dot(f_ref[...], wd_ref[...]) + xv - t_ref[...]
        d_ref[...] = e * (1.0 / D)
        l_ref[...] += jnp.sum(e * e) * (0.5 / D)

    row = lambda w: BS((tm, w), lambda i: (i, 0))
    const = lambda r, w: BS((r, w), lambda i: (0, 0))
    return _pcall(body, name="ffn_fwd_loss", grid=(S // tm,),
                  in_specs=[row(D), const(1, D), _resident(w_up3.shape), const(3, 2 * D_FF), const(1, 2 * D_FF),
                            _resident(w_down.shape), row(D)],
                  out_specs=[row(D), row(2 * D_FF), row(D_FF), row(D), const(8, 128)],
                  out_shape=[SDS((S, D), MXU_DTYPE), SDS((S, 2 * D_FF), F32), SDS((S, D_FF), MXU_DTYPE),
                             SDS((S, D), F32), SDS((8, 128), F32)],
                  scratch=[pltpu.VMEM((8, 2 * D_FF), F32)])(x2, g, w_up3, cw, cb, w_down, target)


def convgate_bwd(a, dx3, w3, cw, cb, after=None):
    S = a.shape[0]
    tm = _tile(S, (256,))

    def body(ag_ref, au_ref, hg_ref, hu_ref, wg_ref, wu_ref, bg_ref, bu_ref, dx_ref, wd_ref, dc_ref, gw_ref, df_ref):
        first_tile = pl.program_id(0) == 0

        @pl.when(first_tile)
        def _():
            gw_ref[...] = jnp.zeros_like(gw_ref)

        df_ref[...] = _nt_chunks(dx_ref, wd_ref)
        for c0 in range(0, D_FF, CONV_COLS):
            cols, ucols = slice(c0, c0 + CONV_COLS), slice(D_FF + c0, D_FF + c0 + CONV_COLS)
            cg, g_taps = _conv_taps(ag_ref, hg_ref, wg_ref, bg_ref, cols, first_tile)
            cu, u_taps = _conv_taps(au_ref, hu_ref, wu_ref, bu_ref, cols, first_tile)
            df_v = df_ref[:, cols]
            dcg = df_v * cu * _gelu_grad(cg)
            dcu = df_v * _gelu(cg)
            dc_ref[:, cols] = dcg
            dc_ref[:, ucols] = dcu
            for col, dcv, taps in ((cols, dcg, g_taps), (ucols, dcu, u_taps)):
                for j in range(3):
                    gw_ref[j:j + 1, col] += jnp.sum(dcv * taps[j], axis=0, keepdims=True)
                gw_ref[3:4, col] += jnp.sum(dcv, axis=0, keepdims=True)

    return _pcall(body, name="convgate_bwd", grid=(S // tm,), after=after,
                  in_specs=_conv_specs(tm) + [BS((tm, dx3.shape[1]), lambda i: (i, 0)), _resident(w3.shape)],
                  out_specs=[BS((tm, 2 * D_FF), lambda i: (i, 0)), BS((8, 2 * D_FF), lambda i: (0, 0))],
                  out_shape=[SDS((S, 2 * D_FF), F32), SDS((8, 2 * D_FF), F32)],
                  scratch=[pltpu.VMEM((tm, D_FF), F32)])(a, a, a, a, cw, cw, cb, cb, dx3, w3)


def conv_transpose_rms_bwd(dc, cw, w3, x, g, dres):
    S, C = dc.shape
    Q, N, Kc = w3.shape
    tm = _tile(S, (256,))
    nt = S // tm
    halo_blocks = tm // 8

    def body(dc_ref, halo_ref, cw_ref, w_ref, x_ref, g_ref, dr_ref, da_ref, dx_ref, dg_ref):
        @pl.when(pl.program_id(0) == 0)
        def _():
            dg_ref[...] = jnp.zeros_like(dg_ref)

        last_tile = pl.program_id(0) == nt - 1
        row = lax.broadcasted_iota(jnp.int32, (8, CONV_COLS), 0)
        for c0 in range(0, C, CONV_COLS):
            cols = slice(c0, c0 + CONV_COLS)
            h0 = jnp.where(last_tile, 0.0, halo_ref[0:1, cols])
            h1 = jnp.where(last_tile, 0.0, halo_ref[1:2, cols])
            dc_v = dc_ref[:, cols]
            r1, r2 = pltpu.roll(dc_v, tm - 1, 0), pltpu.roll(dc_v, tm - 2, 0)
            n1 = jnp.concatenate([r1[:tm - 8], jnp.where(row == 7, h0, r1[tm - 8:])], axis=0)
            n2 = jnp.concatenate([r2[:tm - 8], jnp.where(row == 7, h1, jnp.where(row == 6, h0, r2[tm - 8:]))], axis=0)
            da_ref[:, cols] = (cw_ref[2:3, cols] * dc_v + cw_ref[1:2, cols] * n1
                               + cw_ref[0:1, cols] * n2).astype(da_ref.dtype)
        xv = x_ref[...]
        dx, dgc = _rms_bwd(_nt_chunks(da_ref, w_ref), xv, g_ref[...], _rms(xv))
        dx_ref[...] = dr_ref[...] + dx
        _acc_rows(dg_ref, 0, dgc)

    row_n = BS((tm, N), lambda i: (i, 0))
    return _pcall(body, name="conv_transpose_rms_bwd", grid=(nt,),
                  in_specs=[BS((tm, C), lambda i: (i, 0)),
                            BS((8, C), lambda i: (jnp.minimum((i + 1) * halo_blocks, S // 8 - 1), 0)),
                            BS((3, C), lambda i: (0, 0)), _resident((Q, N, Kc)), row_n, BS((1, N), lambda i: (0, 0)),
                            row_n],
                  out_specs=[BS((tm, C), lambda i: (i, 0)), row_n, BS((8, N), lambda i: (0, 0))],
                  out_shape=[SDS((S, C), MXU_DTYPE), SDS((S, N), F32), SDS((8, N), F32)])(dc, dc, cw, w3, x, g, dres)


def xattn_block_bwd(dx2, wo3, qx, kn, vb, gxq, wq3, x1, g, after=None):
    S, D = qx.shape
    tm = _tile(S, (512, 256))

    def body(dx2_ref, wo_ref, q_ref, kn_ref, vb_ref, gxq_ref, wq_ref, x_ref, g_ref,
             dq_ref, dx_ref, dkn_ref, dv_ref, dgq_ref, dg_ref):
        @pl.when(pl.program_id(0) == 0)
        def _():
            for ref in (dkn_ref, dv_ref, dgq_ref, dg_ref):
                ref[...] = jnp.zeros_like(ref)

        gq = gxq_ref[...]
        do_all = _nt_chunks(dx2_ref, wo_ref)
        for h in range(XA_HEADS):
            sl = slice(h * XA_DH, (h + 1) * XA_DH)
            qh, do = q_ref[:, sl], do_all[:, sl]
            r, qn, p = _xa_probs(qh, gq, kn_ref[:, sl])
            dp = _dot(do, vb_ref[:, sl], NT)
            ds = p * (dp - jnp.sum(dp * p, axis=1, keepdims=True)) * (1.0 / math.sqrt(XA_DH))
            dqn = _dot(ds, kn_ref[:, sl])
            dkn_ref[:, sl] += _dot(ds, qn, TN)
            dv_ref[:, sl] += _dot(p, do, TN)
            dqh, dgc = _rms_bwd(dqn, qh, gq, r)
            dq_ref[:, sl] = dqh.astype(dq_ref.dtype)
            _acc_rows(dgq_ref, 0, dgc)
        xv = x_ref[...]
        dx, dgc = _rms_bwd(_nt_chunks(dq_ref, wq_ref), xv, g_ref[...], _rms(xv))
        dx_ref[...] = dx2_ref[...] + dx
        _acc_rows(dg_ref, 0, dgc)

    row = BS((tm, D), lambda i: (i, 0))
    full = lambda r, w: BS((r, w), lambda i: (0, 0))
    return _pcall(body, name="xattn_block_bwd", grid=(S // tm,), after=after,
                  in_specs=[row, _resident(wo3.shape), row, full(MEM_LEN, D), full(MEM_LEN, D), full(1, XA_DH),
                            _resident(wq3.shape), row, full(1, D)],
                  out_specs=[row, row, full(MEM_LEN, D), full(MEM_LEN, D), full(8, XA_DH), full(8, D)],
                  out_shape=[SDS((S, D), MXU_DTYPE), SDS((S, D), F32), SDS((MEM_LEN, D), F32), SDS((MEM_LEN, D), F32),
                             SDS((8, XA_DH), F32), SDS((8, D), F32)])(dx2, wo3, qx, kn, vb, gxq, wq3, x1, g)


def mem_bwd(kv, dkn, dvb, gxk, after=None):
    def body(kv_ref, dkn_ref, dv_ref, g_ref, dkv_ref, dg_ref):
        dg_ref[...] = jnp.zeros_like(dg_ref)
        for h in range(XA_HEADS):
            sl = slice(h * XA_DH, (h + 1) * XA_DH)
            k = kv_ref[:, sl]
            dk, dgc = _rms_bwd(dkn_ref[:, sl], k, g_ref[...], _rms(k))
            dkv_ref[:, sl] = dk.astype(dkv_ref.dtype)
            _acc_rows(dg_ref, 0, dgc)
        dkv_ref[:, 1024:2048] = dv_ref[...].astype(dkv_ref.dtype)

    full = lambda r, w: BS((r, w), lambda i: (0, 0))
    return _pcall(body, name="mem_bwd", grid=(1,), after=after,
                  in_specs=[full(MEM_LEN, 2048), full(MEM_LEN, 1024), full(MEM_LEN, 1024), full(1, XA_DH)],
                  out_specs=[full(MEM_LEN, 2048), full(8, XA_DH)],
                  out_shape=[SDS((MEM_LEN, 2048), MXU_DTYPE), SDS((8, XA_DH), F32)])(kv, dkn, dvb, gxk)


def _norm_rope_bwd(slabs, douts, g, bm, cos_v, sin_v, first):
    dqns = [d * cos_v + _half_swap(d * sin_v, first) for d in douts]
    rs = [lax.rsqrt(ms + EPS) for ms in _head_means([x * x for x in slabs], bm)]
    projs = _head_means([dqn * g * x for dqn, x in zip(dqns, slabs)], bm)
    dxs = [r * (dqn * g) - x * (r * r * r) * pr for x, dqn, r, pr in zip(slabs, dqns, rs, projs)]
    return dxs, [dqn * x * r for x, dqn, r in zip(slabs, dqns, rs)]


def mixer_core_bwd(proj, cos, sin, gq, gk, gvg, bmat, qr, kr, vb, sinkcol, dattn, dgm, gvn, gu, w2, w2t, bsl,
                   after=None):
    S = qr.shape[0]
    nb = S // BLK

    def body(p_ref, c_ref, s_ref, gq_ref, gk_ref, gvg_ref, b_ref, q_ref, kc_ref, kp_ref, vc_ref, vp_ref, sk_ref,
             do_ref, dgm_ref, gvn_ref, gu_ref, w2_ref, w2t_ref, bsl_ref,
             dp_ref, dsk_ref, dws_ref, dbl_ref, dgq_ref, dgk_ref, dgv_ref,
             carry_k, carry_v, done_k, done_v, dq_keep, dgu_keep, dgvn_keep):
        n = pl.program_id(0)

        @pl.when(n == 0)
        def _():
            for ref in (dsk_ref, dws_ref, dbl_ref, dgq_ref, dgk_ref, dgv_ref, carry_k, carry_v, dq_keep, dgu_keep,
                        dgvn_keep):
                ref[...] = jnp.zeros_like(ref)

        live = (n < nb).astype(F32)
        cos_v, sin_v, bm = c_ref[...], s_ref[...], b_ref[...]
        first = (_lane((BLK, 128)) & 63) < 32
        lo = _lane((BLK, 128)) < 64

        dxs, dgs = _norm_rope_bwd([p_ref[:, s * 128:(s + 1) * 128] for s in range(4)],
                                  [dq_keep[:, s * 128:(s + 1) * 128] for s in range(4)], gq_ref[...], bm,
                                  cos_v, sin_v, first)
        for s, (dx, dg) in enumerate(zip(dxs, dgs)):
            dp_ref[:, s * 128:(s + 1) * 128] = dx.astype(dp_ref.dtype)
            _acc_rows(dgq_ref, 0, dg)
        dp_ref[:, 1024:1536] = (dgu_keep[...] * _gelu_grad(p_ref[:, 1024:1536])).astype(dp_ref.dtype)
        gvp = p_ref[:, 1536:2048]
        gv = _gelu(gvp)
        dgv, dgc = _rms_bwd(dgvn_keep[...], gv, gvg_ref[...], _rms(gv))
        dp_ref[:, 1536:2048] = (dgv * _gelu_grad(gvp)).astype(dp_ref.dtype)
        _acc_rows(dgv_ref, 0, dgc)

        for h in range(2):
            hs, qs = slice(h * 128, (h + 1) * 128), slice(h * 256, (h + 1) * 256)
            kd = jnp.concatenate([kp_ref[:, hs], kc_ref[:, hs]], axis=0)
            vd = jnp.concatenate([vp_ref[:, hs], vc_ref[:, hs]], axis=0)
            sink = jnp.concatenate([sk_ref[2 * h], sk_ref[2 * h + 1]], axis=0)
            qp, p, psink = _swa_probs(q_ref[:, qs], kd, sink, n, lo)
            dop = _by_head(do_ref[:, qs], lo)
            dp = _dot(dop, vd, NT)
            delta = jnp.sum(dp * p, axis=1, keepdims=True)
            ds = p * (dp - delta) * (1.0 / math.sqrt(HEAD_DIM))
            dsink = -psink * delta * live
            dsk_ref[2 * h] += dsink[:2 * BLK]
            dsk_ref[2 * h + 1] += dsink[2 * BLK:]
            dq_keep[:, qs] = _from_heads(_dot(ds, kd), lo)
            dkd = _dot(ds, qp, TN)
            dvd = _dot(p, dop, TN)
            done_k[:, hs] = carry_k[:, hs] + live * dkd[:BLK]
            done_v[:, hs] = carry_v[:, hs] + live * dvd[:BLK]
            carry_k[:, hs] = dkd[BLK:]
            carry_v[:, hs] = dvd[BLK:]
        for j in range(4):
            sl = slice(j * 128, (j + 1) * 128)
            gvn_s = gvn_ref[:, sl]
            m2 = _dot(w2_ref[j], gvn_s)
            mixed = jnp.where(lo, m2[:BLK], m2[BLK:]) + bsl_ref[j]
            dgm_s = dgm_ref[:, sl]
            dgu_keep[:, sl] = dgm_s * mixed
            dmx = dgm_s * gu_ref[:, sl] * live
            d2 = _dot(w2t_ref[j], dmx)
            dgvn_keep[:, sl] = jnp.where(lo, d2[:BLK], d2[BLK:])
            z = jnp.zeros_like(dmx)
            dws_ref[2 * j] += _dot(jnp.where(lo, dmx, z), gvn_s, NT)
            dws_ref[2 * j + 1] += _dot(jnp.where(lo, z, dmx), gvn_s, NT)
            dbl_ref[j] += dmx

        dxs, dgs = _norm_rope_bwd([p_ref[:, 512 + s * 128:640 + s * 128] for s in range(2)],
                                  [done_k[:, s * 128:(s + 1) * 128] for s in range(2)], gk_ref[...], bm,
                                  cos_v, sin_v, first)
        for s, (dx, dg) in enumerate(zip(dxs, dgs)):
            dp_ref[:, 512 + s * 128:640 + s * 128] = dx.astype(dp_ref.dtype)
            _acc_rows(dgk_ref, 0, dg)
        dp_ref[:, 768:1024] = done_v[...].astype(dp_ref.dtype)

    last = nb - 1
    cur = lambda w: BS((BLK, w), lambda n: (jnp.minimum(n, last), 0))
    prev = lambda w: BS((BLK, w), lambda n: (jnp.clip(n - 1, 0, last), 0))
    done = lambda w: BS((BLK, w), lambda n: (jnp.maximum(n - 1, 0), 0))
    const = lambda *shape: BS(shape, lambda n: (0,) * len(shape))
    return _pcall(body, name="mixer_core_bwd", grid=(nb + 1,), after=after,
                  in_specs=[done(IN_COLS_DUP), done(128), done(128), const(1, 128), const(1, 128), const(1, 512),
                            const(128, 128), cur(512), cur(256), prev(256), cur(256), prev(256),
                            const(4, 2 * BLK, 1), cur(512), cur(512), cur(512), cur(512), const(4, 2 * BLK, BLK),
                            const(4, 2 * BLK, BLK), const(4, BLK, 128)],
                  out_specs=[done(IN_COLS_DUP), const(4, 2 * BLK, 1), const(8, BLK, BLK), const(4, BLK, 128),
                             const(8, 128), const(8, 128), const(8, 512)],
                  out_shape=[SDS((S, IN_COLS_DUP), MXU_DTYPE), SDS((4, 2 * BLK, 1), F32), SDS((8, BLK, BLK), F32),
                             SDS((4, BLK, 128), F32), SDS((8, 128), F32), SDS((8, 128), F32), SDS((8, 512), F32)],
                  scratch=[pltpu.VMEM((BLK, 256), F32)] * 4 + [pltpu.VMEM((BLK, 512), F32)] * 3)(
        proj, cos, sin, gq, gk, gvg, bmat, qr, kr, kr, vb, vb, sinkcol, dattn, dgm, gvn, gu, w2, w2t, bsl)


BIG = (("w_in", (1024, 448), True), ("w_out", (256, 1024), False), ("xa_wq", (256, 1024), False),
       ("xa_wkv", (1024, 512), True), ("xa_wo", (256, 1024), False), ("ffn_up", (1024, 1408), True),
       ("ffn_down", (704, 1024), False))
BIG_NAMES = tuple(n for n, _, _ in BIG)
SMALL_VECS = (("mix_norm", 1024), ("q_norm", 64), ("k_norm", 64), ("attn_sinks", 8), ("gmlp_v_norm", 512),
              ("attn_out_norm", 512), ("gmlp_out_norm", 512), ("xa_norm", 1024), ("mem_norm", 1024),
              ("xa_q_norm", 256), ("xa_k_norm", 256), ("ffn_norm", 1024), ("ffn_conv_b", 5632))
SMALL = tuple(n for n, _ in SMALL_VECS) + ("gmlp_bs", "gmlp_ws", "ffn_conv")
WEIGHTS = ("mix_norm", "w_in", "q_norm", "k_norm", "attn_sinks", "gmlp_v_norm", "gmlp_ws", "gmlp_bs",
           "attn_out_norm", "gmlp_out_norm", "w_out", "xa_norm", "mem_norm", "xa_wq", "xa_wkv", "xa_q_norm",
           "xa_k_norm", "xa_wo", "ffn_norm", "ffn_up", "ffn_conv", "ffn_conv_b", "ffn_down")
CONV_SHARD = (3, 1408)
CONV_LANE_ROWS = CONV_SHARD[1] // 128
CONV_CHIP_ROWS = 40


def _small_rows():
    rows, r = {}, 0
    for n, length in SMALL_VECS:
        rows[n] = r
        r += -(-length // 128)
    r += -r % 8
    rows["gmlp_bs"] = r
    r += 8
    rows["gmlp_ws"] = r
    r += 8 * BLK
    rows["ffn_conv"] = r
    r += N_CHIPS * CONV_CHIP_ROWS
    return rows, r


SMALL_ROW, SMALL_ROWS = _small_rows()


def pack_small(dg_mix, dgq, dgk, dsk, dg_gvn, dg_y, dg_xa, dg_mem, dg_xq, dg_xk, dg_ffn, gcw, dbl, dws):
    def body(mix_ref, q_ref, k_ref, sk_ref, gvn_ref, y_ref, xa_ref, mem_ref, xq_ref, xk_ref, ffn_ref, cw_ref,
             dbl_ref, dws_ref, o_ref):
        o_ref[...] = jnp.zeros_like(o_ref)
        lane = _lane((1, 128))

        def put(name, src_ref, row, lane0, length):
            for k in range(length // 128):
                o_ref[SMALL_ROW[name] + k:SMALL_ROW[name] + k + 1, :] = src_ref[row:row + 1, lane0 + k * 128:lane0 + (k + 1) * 128]

        put("mix_norm", mix_ref, 0, 0, 1024)
        for name, ref in (("q_norm", q_ref), ("k_norm", k_ref)):
            v = ref[0:1, :]
            o_ref[SMALL_ROW[name]:SMALL_ROW[name] + 1, :] = jnp.where(lane < HEAD_DIM, v + pltpu.roll(v, 64, 1), 0.0)
        sinks = jnp.zeros((1, 128), F32)
        for s in range(4):
            col = sk_ref[s]
            sinks = sinks + jnp.where(lane == 2 * s, jnp.sum(col[:BLK]), 0.0) + jnp.where(lane == 2 * s + 1, jnp.sum(col[BLK:]), 0.0)
        o_ref[SMALL_ROW["attn_sinks"]:SMALL_ROW["attn_sinks"] + 1, :] = sinks
        put("gmlp_v_norm", gvn_ref, 0, 0, 512)
        put("attn_out_norm", y_ref, 0, 0, 512)
        put("gmlp_out_norm", y_ref, 0, 512, 512)
        put("xa_norm", xa_ref, 0, 0, 1024)
        put("mem_norm", mem_ref, 0, 0, 1024)
        put("xa_q_norm", xq_ref, 0, 0, 256)
        put("xa_k_norm", xk_ref, 0, 0, 256)
        put("ffn_norm", ffn_ref, 0, 0, 1024)
        put("ffn_conv_b", cw_ref, 3, 0, 2 * D_FF)
        r8 = lax.broadcasted_iota(jnp.int32, (8, 128), 0)
        l8 = _lane((8, 128))
        bs = jnp.zeros((8, BLK), F32)
        for j in range(4):
            sel = (((r8 == 2 * j) & (l8 < 64)) | ((r8 == 2 * j + 1) & (l8 >= 64))).astype(F32).astype(BF16)
            xj = dbl_ref[j]
            hi = xj.astype(BF16)
            lo = (xj - hi.astype(F32)).astype(BF16)
            bs = bs + lax.dot_general(sel, hi, NT, preferred_element_type=F32) + lax.dot_general(sel, lo, NT, preferred_element_type=F32)
        o_ref[SMALL_ROW["gmlp_bs"]:SMALL_ROW["gmlp_bs"] + 8, :] = bs
        causal = lax.broadcasted_iota(jnp.int32, (BLK, BLK), 0) >= lax.broadcasted_iota(jnp.int32, (BLK, BLK), 1)
        for h in range(8):
            r0 = SMALL_ROW["gmlp_ws"] + h * BLK
            o_ref[r0:r0 + BLK, :] = jnp.where(causal, dws_ref[h], 0.0)
        for q in range(N_CHIPS):
            for j in range(3):
                for k in range(CONV_LANE_ROWS):
                    r0 = SMALL_ROW["ffn_conv"] + q * CONV_CHIP_ROWS + j * CONV_LANE_ROWS + k
                    l0 = (q * CONV_LANE_ROWS + k) * 128
                    o_ref[r0:r0 + 1, :] = cw_ref[j:j + 1, l0:l0 + 128]

    args = (dg_mix, dgq, dgk, dsk, dg_gvn, dg_y, dg_xa, dg_mem, dg_xq, dg_xk, dg_ffn, gcw, dbl, dws)
    full = lambda a: BS(a.shape, lambda i, nd=a.ndim: (0,) * nd)
    return _pcall(body, name="pack_small", grid=(1,), in_specs=[full(a) for a in args],
                  out_specs=BS((SMALL_ROWS, 128), lambda i: (0, 0)), out_shape=SDS((SMALL_ROWS, 128), F32))(*args)


def _adam(w, g, m, v):
    mn = ADAM_B1 * m + (1.0 - ADAM_B1) * g
    vn = ADAM_B2 * v + (1.0 - ADAM_B2) * (g * g)
    m_hat = mn / (1.0 - ADAM_B1 ** ADAM_STEP)
    v_hat = vn / (1.0 - ADAM_B2 ** ADAM_STEP)
    return -ADAM_LR * (m_hat / (jnp.sqrt(v_hat) + ADAM_EPS) + ADAM_WD * w), mn, vn


def adamw_small(gsum, w, m, v, chipvec):
    n = len(SMALL)

    def body(chip_ref, g_ref, *refs):
        w_refs, m_refs, v_refs = refs[:n], refs[n:2 * n], refs[2 * n:3 * n]
        outs = refs[3 * n:]
        go, do, mo, vo = outs[:n], outs[n:2 * n], outs[2 * n:3 * n], outs[3 * n:]

        def update(i, idx, g):
            d, mn, vn = _adam(w_refs[i][idx], g, m_refs[i][idx], v_refs[i][idx])
            go[i][idx] = g
            do[i][idx] = d
            mo[i][idx] = mn
            vo[i][idx] = vn

        for i, (name, length) in enumerate(SMALL_VECS):
            for k in range(-(-length // 128)):
                wd = min(128, length - k * 128)
                r = SMALL_ROW[name] + k
                update(i, (slice(0, 1), slice(k * 128, k * 128 + wd)), g_ref[r:r + 1, 0:wd])
        i_bs, i_ws, i_cv = len(SMALL_VECS), len(SMALL_VECS) + 1, len(SMALL_VECS) + 2
        update(i_bs, (0,), g_ref[SMALL_ROW["gmlp_bs"]:SMALL_ROW["gmlp_bs"] + 8, :])
        for h in range(8):
            r0 = SMALL_ROW["gmlp_ws"] + h * BLK
            update(i_ws, (0, h), g_ref[r0:r0 + BLK, :])
        mine = g_ref[pl.ds(pl.multiple_of(SMALL_ROW["ffn_conv"] + chip_ref[0] * CONV_CHIP_ROWS, 8), CONV_CHIP_ROWS), :]
        for j in range(3):
            for k in range(CONV_LANE_ROWS):
                r = j * CONV_LANE_ROWS + k
                update(i_cv, (0, slice(j, j + 1), slice(k * 128, (k + 1) * 128)), mine[r:r + 1, :])

    nat = [w[nm] for nm in SMALL]
    full = lambda a: BS(a.shape, lambda i, c, nd=a.ndim: (0,) * nd)
    outs = _pcall(body, name="adamw_small", grid=(1,), prefetch=1,
                  in_specs=[BS((SMALL_ROWS, 128), lambda i, c: (0, 0))] + [full(a) for a in nat] * 3,
                  out_specs=[full(a) for a in nat] * 4, out_shape=[SDS(a.shape, F32) for a in nat] * 4)(
        chipvec, gsum, *nat, *[m[nm] for nm in SMALL], *[v[nm] for nm in SMALL])
    return outs[:n], outs[n:2 * n], outs[2 * n:3 * n], outs[3 * n:]


def adamw_matrix(w, m, v, g_own, g_other, cvec, *, name):
    _, r, c = w.shape
    half = r // 2
    tr = _tile(half, (256, 176, 128))
    T = half // tr

    def body(c_ref, w_ref, m_ref, v_ref, own_ref, oth_ref, g_ref, d_ref, mo_ref, vo_ref):
        g = jnp.where(pl.program_id(0) == c_ref[0], own_ref[...], oth_ref[...])
        d, mn, vn = _adam(w_ref[...], g, m_ref[...], v_ref[...])
        g_ref[...] = g
        d_ref[...] = d
        mo_ref[...] = mn
        vo_ref[...] = vn

    nat = BS((None, tr, c), lambda hf, t, cr: (0, hf * T + t, 0))
    hlf = BS((tr, c), lambda hf, t, cr: (t, 0))
    return _pcall(body, name=name, grid=(2, T), prefetch=1, in_specs=[nat, nat, nat, hlf, hlf], out_specs=[nat] * 4,
                  out_shape=[SDS(w.shape, F32)] * 4)(cvec, w, m, v, g_own, g_other)


def _place():
    return lax.axis_index("x"), lax.axis_index("y"), lax.axis_index("c")


def _other_chips(x, y):
    return [(1 - x, y), (x, 1 - y), (1 - x, 1 - y)]


def _rows_of_core(c, half):
    return pl.ds(pl.multiple_of(c * half, 16), half)


def _rcopy(src, dst, sems, k, to):
    return pltpu.make_async_remote_copy(src_ref=src, dst_ref=dst, send_sem=sems[0].at[k], recv_sem=sems[1].at[k],
                                        device_id=to, device_id_type=MESH)


def cast_shards(shards, conv, chipvec):
    n = len(shards)

    def body(chip_ref, *refs):
        for i_ref, o_ref in zip(refs[:n + 1], refs[n + 1:]):
            o_ref[...] = i_ref[...].astype(o_ref.dtype)

    in_specs = [BS((s.shape[0] // 4, s.shape[1]), lambda i, p: (i, 0)) for s in shards]
    in_specs.append(BS(conv.shape, lambda i, p: (0, 0)))
    out_specs = [BS((None, s.shape[0] // 4, s.shape[1]), lambda i, p: (p[0], i, 0)) for s in shards]
    out_specs.append(BS((None,) + conv.shape, lambda i, p: (p[0], 0, 0)))
    out_shape = [SDS((N_CHIPS,) + s.shape, MXU_DTYPE) for s in shards] + [SDS((N_CHIPS,) + conv.shape, F32)]
    return _pcall(body, name="cast_shards", grid=(4,), prefetch=1, in_specs=in_specs, out_specs=out_specs,
                  out_shape=out_shape)(chipvec, *shards, conv)


HBM = pl.BlockSpec(memory_space=pltpu.HBM)
SEM = pl.BlockSpec(memory_space=pltpu.SEMAPHORE)
DATAFLOW = pltpu.SideEffectType.DATAFLOW_SIDE_EFFECTING
VMEM_WHOLE = pl.BlockSpec(memory_space=pltpu.VMEM)
TOKEN = jax.ShapeDtypeStruct((8, 128), jnp.float32)


def _gather_copies(bufs, send_sems, recv_sems, outgoing):
    x, y, c = _place()
    p = 2 * x + y
    cps = []
    for i, o in enumerate(bufs):
        for j, (cx, cy) in enumerate(_other_chips(x, y)):
            slot = o.at[p] if outgoing else o.at[2 * cx + cy]
            cps.append(_rcopy(slot, slot, (send_sems, recv_sems), 3 * i + j, (cx, cy, c)))
    return cps


def gather_start(slots, after):
    n = len(slots)

    def body(*refs):
        send_sems, recv_sems, thru, token = refs[n + 1], refs[n + 2], refs[n + 3:2 * n + 3], refs[2 * n + 3]
        for cp in _gather_copies(thru, send_sems, recv_sems, True):
            cp.start()
        token[...] = jnp.zeros_like(token)

    hbm = [pltpu.with_memory_space_constraint(s, pltpu.HBM) for s in slots]
    outs = pl.pallas_call(
        body, name="gather_start_%d" % n,
        out_shape=[pltpu.SemaphoreType.DMA((3 * n,)), pltpu.SemaphoreType.DMA((3 * n,))]
        + [pltpu.HBM(s.shape, s.dtype) for s in slots] + [TOKEN],
        in_specs=[HBM] * n + [ANY], out_specs=[SEM, SEM] + [HBM] * n + [VMEM_WHOLE],
        input_output_aliases={i: 2 + i for i in range(n)},
        compiler_params=pltpu.CompilerParams(has_side_effects=DATAFLOW))(*hbm, after)
    return outs[0], outs[1], outs[2:2 + n], outs[2 + n]


def gather_wait(send_sems, recv_sems, bufs, *after):
    n = len(bufs)

    def body(*refs):
        ins, send_ref, recv_ref = refs[:n], refs[n], refs[n + 1]
        for cp in _gather_copies(ins, send_ref, recv_ref, False):
            cp.wait_send()
            cp.wait_recv()

    return pl.pallas_call(
        body, name="gather_wait_%d" % n, out_shape=[pltpu.HBM(s.shape, s.dtype) for s in bufs],
        in_specs=[HBM] * n + [SEM, SEM] + [ANY] * len(after), out_specs=[HBM] * n,
        input_output_aliases={i: i for i in range(n)},
        compiler_params=pltpu.CompilerParams(has_side_effects=DATAFLOW))(*bufs, send_sems, recv_sems, *after)


def _peers(x, y, c):
    return [(1 - x if k & 4 else x, 1 - y if k & 2 else y, 1 - c if k & 1 else c) for k in range(1, N_DEV)]


def _partial_copies(g_ref, land_ref, send_sems, recv_sems, outgoing):
    x, y, c = _place()
    half = g_ref.shape[1] // 2
    cps = []
    for k, (px, py, pc) in enumerate(_peers(x, y, c)):
        src = g_ref.at[2 * px + py, _rows_of_core(pc, half)]
        dst = land_ref.at[4 * x + 2 * y + c] if outgoing else land_ref.at[4 * px + 2 * py + pc]
        cps.append(_rcopy(src, dst, (send_sems, recv_sems), k, (px, py, pc)))
    return cps


def partials_start(g, *, name):
    land = lax.empty((N_DEV, g.shape[1] // 2, g.shape[2]), g.dtype)

    def body(g_ref, land_ref, send_sems, recv_sems, g_thru, land_thru, token):
        for cp in _partial_copies(g_thru, land_thru, send_sems, recv_sems, True):
            cp.start()
        token[...] = jnp.zeros_like(token)

    return pl.pallas_call(
        body, name=name,
        out_shape=[pltpu.SemaphoreType.DMA((N_DEV - 1,)), pltpu.SemaphoreType.DMA((N_DEV - 1,)),
                   pltpu.HBM(g.shape, g.dtype), pltpu.HBM(land.shape, land.dtype), TOKEN],
        in_specs=[HBM, HBM], out_specs=[SEM, SEM, HBM, HBM, VMEM_WHOLE], input_output_aliases={0: 2, 1: 3},
        compiler_params=pltpu.CompilerParams(has_side_effects=DATAFLOW))(
        pltpu.with_memory_space_constraint(g, pltpu.HBM), pltpu.with_memory_space_constraint(land, pltpu.HBM))


def partials_wait(started, after):
    n = len(started)

    def body(*refs):
        for i in range(n):
            send_ref, recv_ref, g_ref, land_ref = refs[4 * i:4 * i + 4]
            for cp in _partial_copies(g_ref, land_ref, send_ref, recv_ref, False):
                cp.wait_send()
                cp.wait_recv()

    flat = [a for s in started for a in s]
    bufs = [a for s in started for a in s[2:]]
    outs = pl.pallas_call(
        body, name="partials_wait", out_shape=[pltpu.HBM(b.shape, b.dtype) for b in bufs],
        in_specs=[SEM, SEM, HBM, HBM] * n + [ANY], out_specs=[HBM] * (2 * n),
        input_output_aliases={4 * i + 2 + j: 2 * i + j for i in range(n) for j in range(2)},
        compiler_params=pltpu.CompilerParams(has_side_effects=DATAFLOW))(*flat, after)
    return [(outs[2 * i], outs[2 * i + 1]) for i in range(n)]


def sum_partials(pairs, order):
    n = len(pairs)

    def body(o_ref, *refs):
        j = pl.program_id(0)
        for g_ref, l_ref, f_ref in zip(refs[:n], refs[n:2 * n], refs[2 * n:]):
            @pl.when(j == 0)
            def _():
                f_ref[...] = g_ref[...].astype(F32)

            @pl.when(j > 0)
            def _():
                f_ref[...] += l_ref[...].astype(F32)

    g4 = [g.reshape(g.shape[0], 2, g.shape[1] // 2, g.shape[2]) for g, _ in pairs]
    lands = [l for _, l in pairs]
    return _pcall(body, name="sum_partials", grid=(N_DEV,), prefetch=1,
                  in_specs=[BS((None, None) + g.shape[2:], lambda j, o: (o[0], o[1], 0, 0)) for g in g4]
                  + [BS((None,) + l.shape[1:], lambda j, o: (o[jnp.maximum(j, 1) + 1], 0, 0)) for l in lands],
                  out_specs=[BS(l.shape[1:], lambda j, o: (0, 0)) for l in lands],
                  out_shape=[SDS(l.shape[1:], F32) for l in lands])(order, *g4, *lands)


def _pair_copies(f_refs, land_refs, send_sems, recv_sems):
    x, y, c = _place()
    return [_rcopy(f, o, (send_sems, recv_sems), i, (x, y, 1 - c)) for i, (f, o) in enumerate(zip(f_refs, land_refs))]


def pair_start(fs):
    n = len(fs)
    lands = [lax.empty(f.shape, f.dtype) for f in fs]

    def body(*refs):
        send_sems, recv_sems = refs[2 * n], refs[2 * n + 1]
        thru, land_thru, token = refs[2 * n + 2:3 * n + 2], refs[3 * n + 2:4 * n + 2], refs[4 * n + 2]
        for cp in _pair_copies(thru, land_thru, send_sems, recv_sems):
            cp.start()
        token[...] = jnp.zeros_like(token)

    hbm = [pltpu.with_memory_space_constraint(a, pltpu.HBM) for a in list(fs) + lands]
    outs = pl.pallas_call(
        body, name="pair_start",
        out_shape=[pltpu.SemaphoreType.DMA((n,)), pltpu.SemaphoreType.DMA((n,))]
        + [pltpu.HBM(a.shape, a.dtype) for a in list(fs) + lands] + [TOKEN],
        in_specs=[HBM] * (2 * n), out_specs=[SEM, SEM] + [HBM] * (2 * n) + [VMEM_WHOLE],
        input_output_aliases={i: 2 + i for i in range(2 * n)},
        compiler_params=pltpu.CompilerParams(has_side_effects=DATAFLOW))(*hbm)
    return outs[0], outs[1], outs[2:2 + n], outs[2 + n:2 + 2 * n], outs[2 + 2 * n]


def pair_wait(send_sems, recv_sems, fs, lands, after):
    n = len(fs)

    def body(*refs):
        for cp in _pair_copies(refs[:n], refs[n:2 * n], refs[2 * n], refs[2 * n + 1]):
            cp.wait_send()
            cp.wait_recv()

    outs = pl.pallas_call(
        body, name="pair_wait", out_shape=[pltpu.HBM(a.shape, a.dtype) for a in list(fs) + list(lands)],
        in_specs=[HBM] * (2 * n) + [SEM, SEM, ANY], out_specs=[HBM] * (2 * n),
        input_output_aliases={i: i for i in range(2 * n)},
        compiler_params=pltpu.CompilerParams(has_side_effects=DATAFLOW))(*fs, *lands, send_sems, recv_sems, after)
    return outs[:n], outs[n:]


def _small_copies(s_ref, land_ref, send_sems, recv_sems, outgoing):
    x, y, c = _place()
    cps = []
    for k, (px, py, pc) in enumerate(_peers(x, y, c)):
        dst = land_ref.at[4 * x + 2 * y + c] if outgoing else land_ref.at[4 * px + 2 * py + pc]
        cps.append(_rcopy(s_ref, dst, (send_sems, recv_sems), k, (px, py, pc)))
    return cps


def small_start(sm):
    land = lax.empty((N_DEV,) + sm.shape, sm.dtype)

    def body(s_ref, land_ref, send_sems, recv_sems, s_thru, land_thru):
        for cp in _small_copies(s_thru, land_thru, send_sems, recv_sems, True):
            cp.start()

    return pl.pallas_call(
        body, name="small_start",
        out_shape=[pltpu.SemaphoreType.DMA((N_DEV - 1,)), pltpu.SemaphoreType.DMA((N_DEV - 1,)),
                   pltpu.HBM(sm.shape, sm.dtype), pltpu.HBM(land.shape, land.dtype)],
        in_specs=[HBM, HBM], out_specs=[SEM, SEM, HBM, HBM], input_output_aliases={0: 2, 1: 3},
        compiler_params=pltpu.CompilerParams(has_side_effects=DATAFLOW))(
        pltpu.with_memory_space_constraint(sm, pltpu.HBM), pltpu.with_memory_space_constraint(land, pltpu.HBM))


def small_wait(send_sems, recv_sems, sm, land, after):
    def body(send_ref, recv_ref, s_ref, land_ref, after_ref, s_out, land_out):
        for cp in _small_copies(s_ref, land_ref, send_ref, recv_ref, False):
            cp.wait_send()
            cp.wait_recv()

    return pl.pallas_call(
        body, name="small_wait", out_shape=[pltpu.HBM(sm.shape, sm.dtype), pltpu.HBM(land.shape, land.dtype)],
        in_specs=[SEM, SEM, HBM, HBM, ANY], out_specs=[HBM, HBM], input_output_aliases={2: 0, 3: 1},
        compiler_params=pltpu.CompilerParams(has_side_effects=DATAFLOW))(send_sems, recv_sems, sm, land, after)


def sum_small(own, land, mevec):
    n, rows, width = land.shape
    tr = _tile(rows, (184, 8))

    def body(me_ref, own_ref, land_ref, o_ref):
        acc = jnp.zeros((tr, width), F32)
        for s in range(n):
            acc = acc + jnp.where(me_ref[0] == s, own_ref[...], land_ref[s])
        o_ref[...] = acc

    return _pcall(body, name="sum_small", grid=(rows // tr,), prefetch=1,
                  in_specs=[BS((tr, width), lambda i, me: (i, 0)), BS((n, tr, width), lambda i, me: (0, i, 0))],
                  out_specs=BS((tr, width), lambda i, me: (i, 0)), out_shape=SDS((rows, width), F32))(mevec, own, land)


def _to_full(blk, col):
    n, r, c = blk.shape
    return blk.transpose(1, 0, 2).reshape(r, n * c) if col else blk.reshape(n * r, c)


def _dup_cols(w):
    dup = lambda t: jnp.concatenate([t[:, :64], t[:, :64], t[:, 64:], t[:, 64:]], axis=1)
    return jnp.concatenate([w[:, :512], dup(w[:, 512:640]), dup(w[:, 640:768]), w[:, 768:]], axis=1)


def _fold_cols(d):
    fold = lambda t: jnp.concatenate([t[:, 0:64] + t[:, 64:128], t[:, 128:192] + t[:, 192:256]], axis=1)
    return jnp.concatenate([d[:, :512], fold(d[:, 512:768]), fold(d[:, 768:1024]), d[:, 1024:]], axis=1)


def _local_step(x, mem, positions, target, w_in, later, sp, emit):
    gain = lambda n: sp[n].reshape(1, -1)
    half = HEAD_DIM // 2
    inv_freq = 1.0 / (10000.0 ** (jnp.arange(half, dtype=F32) * (2.0 / HEAD_DIM)))
    ang = positions.astype(F32)[:, None] * inv_freq
    cos, sin = jnp.cos(ang), jnp.sin(ang)
    cos128 = jnp.tile(cos, (1, 4))
    sin128 = jnp.concatenate([-sin, sin, -sin, sin], axis=1)
    seg = jnp.arange(128) // HEAD_DIM
    bmat = (seg[:, None] == seg[None, :]).astype(BF16)
    gq128, gk128 = jnp.tile(gain("q_norm"), (1, 2)), jnp.tile(gain("k_norm"), (1, 2))
    sinkcol = jnp.repeat(sp["attn_sinks"].reshape(4, 2), BLK, axis=1).reshape(4, 2 * BLK, 1)
    wsc = sp["gmlp_ws"] * jnp.tril(jnp.ones((BLK, BLK), F32))[None]
    w2 = wsc.reshape(4, 2 * BLK, BLK).astype(MXU_DTYPE)
    w2t = wsc.swapaxes(1, 2).reshape(4, 2 * BLK, BLK).astype(MXU_DTYPE)
    bsl = jnp.repeat(sp["gmlp_bs"].reshape(4, 2, BLK).transpose(0, 2, 1), HEAD_DIM, axis=2)
    cb = sp["ffn_conv_b"].reshape(1, -1)
    w_in_d = _dup_cols(_to_full(w_in(cos128, sin128, gq128, gk128, sinkcol, w2, w2t, bsl), True))[None]

    h1, proj = rms_mm(x, gain("mix_norm"), w_in_d, name="mix_in")
    qr, kr, vb, gu, gvn, attn, gm, y = mixer_core_fwd(proj, cos128, sin128, gq128, gk128, gain("gmlp_v_norm"), bmat,
                                                      sinkcol, gain("attn_out_norm"), w2, bsl, gain("gmlp_out_norm"))
    wf, last = later(y)
    w_out, xa_wq, xa_wo = (_to_full(wf[n], False) for n in ("w_out", "xa_wq", "xa_wo"))
    x1 = mm(y, w_out, res=x, name="mix_out")
    mn, kv = rms_mm(mem, gain("mem_norm"), wf["xa_wkv"], name="xa_kv")
    kn, vbx = mem_pre(kv, gain("xa_k_norm"))
    h2, qx, xo, x2 = xattn_block_fwd(x1, gain("xa_norm"), xa_wq, kn, vbx, gain("xa_q_norm"), xa_wo)
    ffn_w, cw = last(x2)
    wf = {**wf, **ffn_w}
    ffn_down = _to_full(wf["ffn_down"], False)
    h3, a, f, dx3, loss_acc = ffn_fwd_loss(x2, gain("ffn_norm"), wf["ffn_up"], cw, cb, ffn_down, target)

    by_rows = lambda g: g.reshape(N_CHIPS, g.shape[1] // N_CHIPS, g.shape[2])
    sent = emit("ffn_down", by_rows(mm_tn(f, dx3, name="g_ffn_down", out_dtype=WIRE_DTYPE)))
    dc, gcw = convgate_bwd(a, dx3, ffn_down[None], cw, cb, after=sent)
    da, dx2, dg_ffn = conv_transpose_rms_bwd(dc, cw, wf["ffn_up"], x2, gain("ffn_norm"), dx3)
    sent = emit("ffn_up", mm_tn(h3, da, name="g_ffn_up", out_dtype=WIRE_DTYPE, chunks=N_CHIPS))
    sent = emit("xa_wo", by_rows(mm_tn(xo, dx2, name="g_xa_wo", out_dtype=WIRE_DTYPE, after=sent)))
    dqx, dx1, dkn, dvx, dg_xq, dg_xa = xattn_block_bwd(dx2, xa_wo[None], qx, kn, vbx, gain("xa_q_norm"), xa_wq[None],
                                                       x1, gain("xa_norm"), after=sent)
    sent = emit("xa_wq", by_rows(mm_tn(h2, dqx, name="g_xa_wq", out_dtype=WIRE_DTYPE)))
    dkv, dg_xk = mem_bwd(kv, dkn, dvx, gain("xa_k_norm"), after=sent)
    _, dg_mem = mm_nt_rms_bwd(dkv, wf["xa_wkv"], mem, gain("mem_norm"), jnp.zeros_like(mem), name="d_mem")
    sent = emit("xa_wkv", mm_tn(mn, dkv, name="g_xa_wkv", out_dtype=WIRE_DTYPE, chunks=N_CHIPS))
    dattn, dgm, dg_y = mm_nt_post_bwd(dx1, w_out[None], attn, gm, gain("attn_out_norm"), gain("gmlp_out_norm"),
                                      name="d_mix_out", after=sent)
    sent = emit("w_out", by_rows(mm_tn(y, dx1, name="g_w_out", out_dtype=WIRE_DTYPE)))
    dproj, dsk, dws, dbl, dgq, dgk, dg_gvn = mixer_core_bwd(
        proj, cos128, sin128, gq128, gk128, gain("gmlp_v_norm"), bmat, qr, kr, vb, sinkcol, dattn, dgm, gvn, gu,
        w2, w2t, bsl, after=sent)
    g_in = _fold_cols(mm_tn(h1, dproj, name="g_w_in", out_dtype=F32)[0])
    sent = emit("w_in", g_in.reshape(1024, N_CHIPS, 448).transpose(1, 0, 2).astype(WIRE_DTYPE))
    grad_x, dg_mix = mm_nt_rms_bwd(dproj, w_in_d, x, gain("mix_norm"), dx1, name="d_x", after=sent)
    packed = pack_small(dg_mix, dgq, dgk, dsk, dg_gvn, dg_y, dg_xa, dg_mem, dg_xq, dg_xk, dg_ffn, gcw, dbl, dws)
    return loss_acc, grad_x, packed


def _gather_step(w, chipvec):
    slots = cast_shards([w[n][0] for n in BIG_NAMES], w["ffn_conv"][0], chipvec)
    send_a, recv_a, first, token = gather_start(slots[:1], chipvec)
    send_b, recv_b, mid, token = gather_start(slots[1:5], token)
    send_c, recv_c, rest, token = gather_start(slots[5:], token)

    def w_in(*after):
        return gather_wait(send_a, recv_a, first, token, *after)[0]

    def last(after):
        got = gather_wait(send_c, recv_c, rest, after)
        return dict(zip(BIG_NAMES[5:], got[:-1])), _to_full(got[-1], True)

    def later(after):
        return dict(zip(BIG_NAMES[1:5], gather_wait(send_b, recv_b, mid, after))), last

    return w_in, later, token


def _reduce_update(started, packed, w, m, v, chipvec, cvec, order):
    small_sent = small_start(packed)
    own = sum_partials(partials_wait([started[n] for n in BIG_NAMES], small_sent[2]), order)
    pair_send, pair_recv, own, lands, pair_started = pair_start(own)
    res = [{}, {}, {}, {}]
    mevec = (2 * order[0:1] + order[1:2]).astype(jnp.int32)
    small_sum = sum_small(*small_wait(*small_sent, pair_started), mevec)
    small_res = adamw_small(small_sum, w, m, v, chipvec)
    for d, outs in zip(res, small_res):
        d.update(zip(SMALL, outs))
    own, other = pair_wait(pair_send, pair_recv, own, lands, small_res[0][0])
    for n, g_own, g_other in zip(BIG_NAMES, own, other):
        for d, o in zip(res, adamw_matrix(w[n], m[n], v[n], g_own, g_other, cvec, name="adamw_" + n)):
            d[n] = o
    return res


def kernel(x, mem, positions, mix_norm, w_in, q_norm, k_norm, attn_sinks, gmlp_v_norm, gmlp_ws, gmlp_bs, attn_out_norm, gmlp_out_norm, w_out, xa_norm, mem_norm, xa_wq, xa_wkv, xa_q_norm, xa_k_norm, xa_wo, ffn_norm, ffn_up, ffn_conv, ffn_conv_b, ffn_down, loss_target, m_mix_norm, m_w_in, m_q_norm, m_k_norm, m_attn_sinks, m_gmlp_v_norm, m_gmlp_ws, m_gmlp_bs, m_attn_out_norm, m_gmlp_out_norm, m_w_out, m_xa_norm, m_mem_norm, m_xa_wq, m_xa_wkv, m_xa_q_norm, m_xa_k_norm, m_xa_wo, m_ffn_norm, m_ffn_up, m_ffn_conv, m_ffn_conv_b, m_ffn_down, v_mix_norm, v_w_in, v_q_norm, v_k_norm, v_attn_sinks, v_gmlp_v_norm, v_gmlp_ws, v_gmlp_bs, v_attn_out_norm, v_gmlp_out_norm, v_w_out, v_xa_norm, v_mem_norm, v_xa_wq, v_xa_wkv, v_xa_q_norm, v_xa_k_norm, v_xa_wo, v_ffn_norm, v_ffn_up, v_ffn_conv, v_ffn_conv_b, v_ffn_down):
    w = dict(mix_norm=mix_norm, w_in=w_in, q_norm=q_norm, k_norm=k_norm, attn_sinks=attn_sinks, gmlp_v_norm=gmlp_v_norm, gmlp_ws=gmlp_ws, gmlp_bs=gmlp_bs, attn_out_norm=attn_out_norm, gmlp_out_norm=gmlp_out_norm, w_out=w_out, xa_norm=xa_norm, mem_norm=mem_norm, xa_wq=xa_wq, xa_wkv=xa_wkv, xa_q_norm=xa_q_norm, xa_k_norm=xa_k_norm, xa_wo=xa_wo, ffn_norm=ffn_norm, ffn_up=ffn_up, ffn_conv=ffn_conv, ffn_conv_b=ffn_conv_b, ffn_down=ffn_down)
    m = dict(mix_norm=m_mix_norm, w_in=m_w_in, q_norm=m_q_norm, k_norm=m_k_norm, attn_sinks=m_attn_sinks, gmlp_v_norm=m_gmlp_v_norm, gmlp_ws=m_gmlp_ws, gmlp_bs=m_gmlp_bs, attn_out_norm=m_attn_out_norm, gmlp_out_norm=m_gmlp_out_norm, w_out=m_w_out, xa_norm=m_xa_norm, mem_norm=m_mem_norm, xa_wq=m_xa_wq, xa_wkv=m_xa_wkv, xa_q_norm=m_xa_q_norm, xa_k_norm=m_xa_k_norm, xa_wo=m_xa_wo, ffn_norm=m_ffn_norm, ffn_up=m_ffn_up, ffn_conv=m_ffn_conv, ffn_conv_b=m_ffn_conv_b, ffn_down=m_ffn_down)
    v = dict(mix_norm=v_mix_norm, w_in=v_w_in, q_norm=v_q_norm, k_norm=v_k_norm, attn_sinks=v_attn_sinks, gmlp_v_norm=v_gmlp_v_norm, gmlp_ws=v_gmlp_ws, gmlp_bs=v_gmlp_bs, attn_out_norm=v_attn_out_norm, gmlp_out_norm=v_gmlp_out_norm, w_out=v_w_out, xa_norm=v_xa_norm, mem_norm=v_mem_norm, xa_wq=v_xa_wq, xa_wkv=v_xa_wkv, xa_q_norm=v_xa_q_norm, xa_k_norm=v_xa_k_norm, xa_wo=v_xa_wo, ffn_norm=v_ffn_norm, ffn_up=v_ffn_up, ffn_conv=v_ffn_conv, ffn_conv_b=v_ffn_conv_b, ffn_down=v_ffn_down)
    ix, iy, ic = lax.axis_index("x"), lax.axis_index("y"), lax.axis_index("c")
    chip = 2 * ix + iy
    chipvec = chip.astype(jnp.int32).reshape(1)
    cvec = ic.astype(jnp.int32).reshape(1)
    order = jnp.stack([chip, ic] + [4 * px + 2 * py + pc for px, py, pc in _peers(ix, iy, ic)]).astype(jnp.int32)

    w_in_all, later, token = _gather_step(w, chipvec)
    zero = token[0, 0]
    sp = {n: w[n][0] + zero for n in SMALL if n != "ffn_conv"}
    positions = positions + zero.astype(jnp.int32)
    started = {}

    def emit(name, g):
        *started[name], token = partials_start(g, name="partials_start_" + name)
        return token

    loss_acc, grad_x, packed = _local_step(x[0], mem[0], positions[0], loss_target[0], w_in_all, later, sp, emit)
    grads, delta, new_m, new_v = _reduce_update(started, packed, w, m, v, chipvec, cvec, order)
    loss = lax.psum(loss_acc[0, 0], ("x", "y", "c"))
    ordered = lambda d: [d[n] for n in WEIGHTS]
    return (loss, grad_x[None], *ordered(grads), *ordered(delta), *ordered(new_m), *ordered(new_v))
```

```python
import math

import jax
import jax.numpy as jnp
from jax import lax
from jax.experimental import pallas as pl
from jax.experimental.pallas import tpu as pltpu

F32 = jnp.float32
BF16 = jnp.bfloat16
MXU_DTYPE = jnp.bfloat16
WIRE_DTYPE = jnp.bfloat16
EPS = 1e-6
VMEM_LIMIT_V7X = 56 * 1024 * 1024

D_MODEL = 1024
HEAD_DIM = 64
BLK = 128
XA_HEADS = 4
XA_DH = 256
MEM_LEN = 256
D_FF = 2816
IN_COLS_DUP = 2048
N_CHIPS = 4
N_DEV = 8

ADAM_LR = 0.001
ADAM_B1 = 0.9
ADAM_B2 = 0.999
ADAM_EPS = 1e-08
ADAM_WD = 0.01
ADAM_STEP = 10

NT = (((1,), (1,)), ((), ()))
TN = (((0,), (0,)), ((), ()))
NN = (((1,), (0,)), ((), ()))
MINF = float(jnp.finfo(jnp.float32).min)
GELU_K0 = math.sqrt(2.0 / math.pi)
GELU_K1 = 0.044715

BS = pl.BlockSpec
SDS = jax.ShapeDtypeStruct
ANY = pl.BlockSpec(memory_space=pl.ANY)
MESH = pl.DeviceIdType.MESH


def _dot(a, b, dims=NN):
    return lax.dot_general(a.astype(MXU_DTYPE), b.astype(MXU_DTYPE), dims, preferred_element_type=F32)


def _segsum(x, bmat):
    hi = x.astype(BF16)
    lo = (x - hi.astype(F32)).astype(BF16)
    return (jnp.dot(hi, bmat, preferred_element_type=F32) + jnp.dot(lo, bmat, preferred_element_type=F32))


def _gelu(x):
    return 0.5 * x * (1.0 + jnp.tanh(GELU_K0 * (x + GELU_K1 * x * x * x)))


def _gelu_grad(x):
    t = jnp.tanh(GELU_K0 * (x + GELU_K1 * x * x * x))
    return 0.5 * (1.0 + t) + 0.5 * x * (1.0 - t * t) * GELU_K0 * (1.0 + 3.0 * GELU_K1 * x * x)


def _gelu_and_grad(x):
    x2 = x * x
    t = jnp.tanh(x * (GELU_K0 * GELU_K1 * x2 + GELU_K0))
    hx = 0.5 * x
    return hx * t + hx, 0.5 * t + 0.5 + hx * (1.0 - t * t) * (3.0 * GELU_K0 * GELU_K1 * x2 + GELU_K0)


def _rms(x):
    return lax.rsqrt(jnp.mean(x * x, axis=-1, keepdims=True) + EPS)


def _rms_bwd(dy, x, g, r):
    dyg = dy * g
    dx = r * dyg - x * (r * r * r) * jnp.mean(dyg * x, axis=-1, keepdims=True)
    return dx, dy * x * r


def _pcall(body, *, name, grid, in_specs, out_specs, out_shape, scratch=(), prefetch=0, after=None):
    params = pltpu.CompilerParams(dimension_semantics=("arbitrary",) * len(grid), vmem_limit_bytes=VMEM_LIMIT_V7X)
    in_specs = list(in_specs)
    kernel_fn = body
    if after is not None:
        n_in = prefetch + len(in_specs)
        in_specs.append(ANY)

        def kernel_fn(*refs):
            return body(*refs[:n_in], *refs[n_in + 1:])

    if prefetch:
        spec = pltpu.PrefetchScalarGridSpec(num_scalar_prefetch=prefetch, grid=grid, in_specs=in_specs,
                                            out_specs=out_specs, scratch_shapes=list(scratch))
        call = pl.pallas_call(kernel_fn, name=name, grid_spec=spec, out_shape=out_shape, compiler_params=params)
    else:
        call = pl.pallas_call(kernel_fn, name=name, grid=grid, in_specs=in_specs, out_specs=out_specs,
                              out_shape=out_shape, scratch_shapes=list(scratch), compiler_params=params)
    return call if after is None else (lambda *args: call(*args, after))


def _tile(n, prefs):
    for p in prefs:
        if p <= n and n % p == 0:
            return p
    return n


def _resident(shape):
    return pl.BlockSpec(shape, lambda *_: (0,) * len(shape), pipeline_mode=pl.Buffered(1))


def _acc_rows(ref, row, val):
    ref[row:row + 1, :] += jnp.sum(val, axis=0, keepdims=True)


def rms_mm(x, g, w3, *, name, tm=1024):
    M, K = x.shape
    Q, _, C = w3.shape
    tm = _tile(M, (tm, 256))

    def body(x_ref, g_ref, w_ref, h_ref, o_ref):
        def write_h():
            xv = x_ref[...]
            h_ref[...] = (xv * _rms(xv) * g_ref[...]).astype(h_ref.dtype)

        if Q == 1:
            write_h()
        else:
            pl.when(pl.program_id(1) == 0)(write_h)
        o_ref[...] = _dot(h_ref[...], w_ref[pl.program_id(1)])

    return _pcall(body, name=name, grid=(M // tm, Q),
                  in_specs=[BS((tm, K), lambda i, j: (i, 0)), BS((1, K), lambda i, j: (0, 0)),
                            _resident((Q, K, C))],
                  out_specs=[BS((tm, K), lambda i, j: (i, 0)), BS((tm, C), lambda i, j: (i, j))],
                  out_shape=[SDS((M, K), MXU_DTYPE), SDS((M, Q * C), F32)])(x, g, w3)


def mm(a, w, *, name, res):
    M, K = a.shape
    N = w.shape[1]
    tm = _tile(M, (1024, 256))

    def body(a_ref, w_ref, r_ref, o_ref):
        o_ref[...] = _dot(a_ref[...], w_ref[...]) + r_ref[...]

    return _pcall(body, name=name, grid=(M // tm,),
                  in_specs=[BS((tm, K), lambda i: (i, 0)), _resident((K, N)), BS((tm, N), lambda i: (i, 0))],
                  out_specs=BS((tm, N), lambda i: (i, 0)), out_shape=SDS((M, N), F32))(a, w, res)


def _nt_chunks(a_ref, w_ref):
    q_n, _, kc = w_ref.shape
    acc = _dot(a_ref[:, 0:kc], w_ref[0], NT)
    for q in range(1, q_n):
        acc = acc + _dot(a_ref[:, q * kc:(q + 1) * kc], w_ref[q], NT)
    return acc


def mm_nt_rms_bwd(a, w3, x, g, dres, *, name, tm=512, after=None):
    M = a.shape[0]
    Q, N, Kc = w3.shape
    tm = _tile(M, (tm, 256))

    def body(a_ref, w_ref, x_ref, g_ref, dr_ref, dx_ref, dg_ref):
        @pl.when(pl.program_id(0) == 0)
        def _():
            dg_ref[...] = jnp.zeros_like(dg_ref)

        xv = x_ref[...]
        dx, dgc = _rms_bwd(_nt_chunks(a_ref, w_ref), xv, g_ref[...], _rms(xv))
        dx_ref[...] = dr_ref[...] + dx
        _acc_rows(dg_ref, 0, dgc)

    row = BS((tm, N), lambda i: (i, 0))
    return _pcall(body, name=name, grid=(M // tm,), after=after,
                  in_specs=[BS((tm, Q * Kc), lambda i: (i, 0)), _resident((Q, N, Kc)), row,
                            BS((1, N), lambda i: (0, 0)), row],
                  out_specs=[row, BS((8, N), lambda i: (0, 0))],
                  out_shape=[SDS((M, N), F32), SDS((8, N), F32)])(a, w3, x, g, dres)


def mm_nt_post_bwd(a, w3, attn, gm, gao, ggo, *, name, after=None):
    M = a.shape[0]
    Q, N, Kc = w3.shape
    tm = _tile(M, (512, 256))
    hw = N // 2

    def body(a_ref, w_ref, at_ref, gm_ref, gao_ref, ggo_ref, da_ref, dgm_ref, dg_ref):
        @pl.when(pl.program_id(0) == 0)
        def _():
            dg_ref[...] = jnp.zeros_like(dg_ref)

        dy = _nt_chunks(a_ref, w_ref)
        av, gmv = at_ref[...], gm_ref[...]
        da, dga = _rms_bwd(dy[:, :hw], av, gao_ref[...], _rms(av))
        dgm, dgg = _rms_bwd(dy[:, hw:], gmv, ggo_ref[...], _rms(gmv))
        da_ref[...] = da
        dgm_ref[...] = dgm
        dg_ref[0:1, :hw] += jnp.sum(dga, axis=0, keepdims=True)
        dg_ref[0:1, hw:] += jnp.sum(dgg, axis=0, keepdims=True)

    half = BS((tm, hw), lambda i: (i, 0))
    const = lambda r, w: BS((r, w), lambda i: (0, 0))
    return _pcall(body, name=name, grid=(M // tm,), after=after,
                  in_specs=[BS((tm, Q * Kc), lambda i: (i, 0)), _resident((Q, N, Kc)), half, half,
                            const(1, hw), const(1, hw)],
                  out_specs=[half, half, const(8, N)],
                  out_shape=[SDS((M, hw), F32), SDS((M, hw), F32), SDS((8, N), F32)])(a, w3, attn, gm, gao, ggo)


def mm_tn(a, b, *, name, out_dtype, chunks=1, after=None):
    M, K = a.shape
    N = b.shape[1]
    C = N // chunks
    tm = _tile(M, (1024, 256))
    tk = _tile(K, (1408, 1024, 512))
    tn = _tile(C, (1408, 1024, 512))
    per = C // tn
    nm = M // tm

    def body(a_ref, b_ref, o_ref, acc):
        m = pl.program_id(2)

        @pl.when(m == 0)
        def _():
            acc[...] = jnp.zeros_like(acc)

        acc[...] += _dot(a_ref[...], b_ref[...], TN)

        @pl.when(m == nm - 1)
        def _():
            o_ref[...] = acc[...].astype(o_ref.dtype)

    return _pcall(body, name=name, grid=(K // tk, N // tn, nm), after=after,
                  in_specs=[BS((tm, tk), lambda k, n, m: (m, k)), BS((tm, tn), lambda k, n, m: (m, n))],
                  out_specs=BS((None, tk, tn), lambda k, n, m: (n // per, k, n % per)),
                  out_shape=SDS((chunks, K, C), out_dtype), scratch=[pltpu.VMEM((tk, tn), F32)])(a, b)


def _lane(shape):
    return lax.broadcasted_iota(jnp.int32, shape, 1)


def _head_means(slabs, bmat):
    tm = slabs[0].shape[0]
    means = _segsum(jnp.concatenate(slabs, axis=0), bmat) * (1.0 / HEAD_DIM)
    return [means[i * tm:(i + 1) * tm] for i in range(len(slabs))]


def _half_swap(x, first):
    return jnp.where(first, pltpu.roll(x, 96, 1), pltpu.roll(x, 32, 1))


def _by_head(x2, lo):
    z = jnp.zeros((BLK, 128), x2.dtype)
    parts = []
    for s in range(2):
        xs = x2[:, s * 128:(s + 1) * 128]
        parts += [jnp.where(lo, xs, z), jnp.where(lo, z, xs)]
    return jnp.concatenate(parts, axis=0)


def _from_heads(o4, lo):
    return jnp.concatenate([jnp.where(lo, o4[0:BLK], o4[BLK:2 * BLK]),
                            jnp.where(lo, o4[2 * BLK:3 * BLK], o4[3 * BLK:])], axis=1)


def _swa_probs(q2, kd, sink, n, lo):
    qp = _by_head(q2, lo)
    sc = _dot(qp, kd, NT) * (1.0 / math.sqrt(HEAD_DIM))
    r_i = lax.broadcasted_iota(jnp.int32, (4 * BLK, 2 * BLK), 0)
    k_j = lax.broadcasted_iota(jnp.int32, (4 * BLK, 2 * BLK), 1)
    diff = (r_i & (BLK - 1)) + BLK - k_j
    mask = (diff >= 0) & (diff < BLK) & ((k_j >= BLK) | (n > 0))
    sc = jnp.where(mask, sc, MINF)
    m = jnp.maximum(jnp.max(sc, axis=1, keepdims=True), sink)
    p = jnp.exp(sc - m)
    es = jnp.exp(sink - m)
    inv = 1.0 / (jnp.sum(p, axis=1, keepdims=True) + es)
    return qp, p * inv, es * inv


def mixer_core_fwd(proj, cos, sin, gq, gk, gvn, bmat, sinkcol, gao, w2, bsl, ggo):
    S = proj.shape[0]
    sub = 4 if S % (4 * BLK) == 0 else 1

    def body(p_ref, c_ref, s_ref, gq_ref, gk_ref, gvn_ref, b_ref, sk_ref, gao_ref, w2_ref, bsl_ref, ggo_ref,
             qr_ref, kr_ref, vb_ref, gu_ref, gvo_ref, at_ref, gm_ref, y_ref, k_prev, v_prev):
        n = pl.program_id(0)

        @pl.when(n == 0)
        def _():
            k_prev[...] = jnp.zeros_like(k_prev)
            v_prev[...] = jnp.zeros_like(v_prev)

        bm = b_ref[...]
        first = (_lane((BLK, 128)) & 63) < 32
        lo = _lane((BLK, 128)) < 64
        for sb in range(sub):
            rs = slice(sb * BLK, (sb + 1) * BLK)
            cos_v, sin_v = c_ref[rs, :], s_ref[rs, :]
            slabs = [p_ref[rs, s * 128:(s + 1) * 128] for s in range(6)]
            for s, (slab, ms) in enumerate(zip(slabs, _head_means([x * x for x in slabs], bm))):
                qn = slab * lax.rsqrt(ms + EPS) * (gq_ref[...] if s < 4 else gk_ref[...])
                out = qn * cos_v + _half_swap(qn, first) * sin_v
                if s < 4:
                    qr_ref[rs, s * 128:(s + 1) * 128] = out.astype(qr_ref.dtype)
                else:
                    kr_ref[rs, (s - 4) * 128:(s - 3) * 128] = out.astype(kr_ref.dtype)
            vb_ref[rs, :] = p_ref[rs, 768:1024].astype(vb_ref.dtype)
            gu_ref[rs, :] = _gelu(p_ref[rs, 1024:1536])
            gv = _gelu(p_ref[rs, 1536:2048])
            gvo_ref[rs, :] = (gv * _rms(gv) * gvn_ref[...]).astype(gvo_ref.dtype)

            before = slice((sb - 1) * BLK, sb * BLK)
            for h in range(2):
                hs, qs = slice(h * 128, (h + 1) * 128), slice(h * 256, (h + 1) * 256)
                k_before = k_prev[:, hs] if sb == 0 else kr_ref[before, hs]
                v_before = v_prev[:, hs] if sb == 0 else vb_ref[before, hs]
                kd = jnp.concatenate([k_before, kr_ref[rs, hs]], axis=0)
                vd = jnp.concatenate([v_before, vb_ref[rs, hs]], axis=0)
                sink = jnp.concatenate([sk_ref[2 * h], sk_ref[2 * h + 1]], axis=0)
                _, p, _ = _swa_probs(qr_ref[rs, qs], kd, sink, n * sub + sb, lo)
                at_ref[rs, qs] = _from_heads(_dot(p, vd), lo)

            for j in range(4):
                sl = slice(j * 128, (j + 1) * 128)
                m2 = _dot(w2_ref[j], gvo_ref[rs, sl])
                mixed = jnp.where(lo, m2[:BLK], m2[BLK:]) + bsl_ref[j]
                gm_ref[rs, sl] = gu_ref[rs, sl] * mixed
            a, gm = at_ref[rs, :], gm_ref[rs, :]
            y_ref[rs, :512] = (a * _rms(a) * gao_ref[...]).astype(y_ref.dtype)
            y_ref[rs, 512:] = (gm * _rms(gm) * ggo_ref[...]).astype(y_ref.dtype)
        k_prev[...] = kr_ref[(sub - 1) * BLK:, :]
        v_prev[...] = vb_ref[(sub - 1) * BLK:, :]

    row = lambda w: BS((sub * BLK, w), lambda n: (n, 0))
    const = lambda *shape: BS(shape, lambda n: (0,) * len(shape))
    return _pcall(body, name="mixer_core_fwd", grid=(S // (sub * BLK),),
                  in_specs=[row(IN_COLS_DUP), row(128), row(128), const(1, 128), const(1, 128), const(1, 512),
                            const(128, 128), const(4, 2 * BLK, 1), const(1, 512), const(4, 2 * BLK, BLK),
                            const(4, BLK, 128), const(1, 512)],
                  out_specs=[row(512), row(256), row(256), row(512), row(512), row(512), row(512), row(1024)],
                  out_shape=[SDS((S, 512), MXU_DTYPE), SDS((S, 256), MXU_DTYPE), SDS((S, 256), MXU_DTYPE),
                             SDS((S, 512), F32), SDS((S, 512), MXU_DTYPE), SDS((S, 512), F32), SDS((S, 512), F32),
                             SDS((S, 1024), MXU_DTYPE)],
                  scratch=[pltpu.VMEM((BLK, 256), MXU_DTYPE), pltpu.VMEM((BLK, 256), MXU_DTYPE)])(
        proj, cos, sin, gq, gk, gvn, bmat, sinkcol, gao, w2, bsl, ggo)


def mem_pre(kv, gxk):
    def body(kv_ref, g_ref, kn_ref, vb_ref):
        for h in range(XA_HEADS):
            sl = slice(h * XA_DH, (h + 1) * XA_DH)
            k = kv_ref[:, sl]
            kn_ref[:, sl] = (k * _rms(k) * g_ref[...]).astype(kn_ref.dtype)
        vb_ref[...] = kv_ref[:, 1024:2048].astype(vb_ref.dtype)

    full = lambda r, w: BS((r, w), lambda i: (0, 0))
    return _pcall(body, name="mem_pre", grid=(1,), in_specs=[full(MEM_LEN, 2048), full(1, XA_DH)],
                  out_specs=[full(MEM_LEN, 1024), full(MEM_LEN, 1024)],
                  out_shape=[SDS((MEM_LEN, 1024), MXU_DTYPE), SDS((MEM_LEN, 1024), MXU_DTYPE)])(kv, gxk)


def _xa_probs(qh, g, kn_h):
    r = _rms(qh)
    qn = qh * r * g
    s = _dot(qn, kn_h, NT) * (1.0 / math.sqrt(XA_DH))
    p = jnp.exp(s - jnp.max(s, axis=1, keepdims=True))
    return r, qn, p * (1.0 / jnp.sum(p, axis=1, keepdims=True))


def xattn_block_fwd(x1, g, wq, kn, vb, gxq, wo):
    S, D = x1.shape
    tm = _tile(S, (512, 256))

    def body(x_ref, g_ref, wq_ref, kn_ref, vb_ref, gxq_ref, wo_ref, h_ref, q_ref, o_ref, x2_ref):
        xv = x_ref[...]
        h_ref[...] = (xv * _rms(xv) * g_ref[...]).astype(h_ref.dtype)
        q_ref[...] = _dot(h_ref[...], wq_ref[...])
        for h in range(XA_HEADS):
            sl = slice(h * XA_DH, (h + 1) * XA_DH)
            _, _, p = _xa_probs(q_ref[:, sl], gxq_ref[...], kn_ref[:, sl])
            o_ref[:, sl] = _dot(p, vb_ref[:, sl]).astype(o_ref.dtype)
        x2_ref[...] = _dot(o_ref[...], wo_ref[...]) + xv

    row = BS((tm, D), lambda i: (i, 0))
    full = lambda r, w: BS((r, w), lambda i: (0, 0))
    return _pcall(body, name="xattn_block_fwd", grid=(S // tm,),
                  in_specs=[row, full(1, D), _resident(wq.shape), full(MEM_LEN, D), full(MEM_LEN, D), full(1, XA_DH),
                            _resident(wo.shape)],
                  out_specs=[row, row, row, row],
                  out_shape=[SDS((S, D), MXU_DTYPE), SDS((S, D), F32), SDS((S, D), MXU_DTYPE), SDS((S, D), F32)])(
        x1, g, wq, kn, vb, gxq, wo)


CONV_COLS = 1408


def _conv_taps(a_ref, halo_ref, w_ref, b_ref, cols, first_tile):
    a = a_ref[:, cols]
    row = lax.broadcasted_iota(jnp.int32, (8, a.shape[1]), 0)
    h6 = jnp.where(first_tile, 0.0, halo_ref[6:7, cols])
    h7 = jnp.where(first_tile, 0.0, halo_ref[7:8, cols])
    r1, r2 = pltpu.roll(a, 1, 0), pltpu.roll(a, 2, 0)
    a1 = jnp.concatenate([jnp.where(row == 0, h7, r1[0:8]), r1[8:]], axis=0)
    a2 = jnp.concatenate([jnp.where(row == 0, h6, jnp.where(row == 1, h7, r2[0:8])), r2[8:]], axis=0)
    c = w_ref[2:3, cols] * a + w_ref[1:2, cols] * a1 + w_ref[0:1, cols] * a2 + b_ref[:, cols]
    return c, (a2, a1, a)


def _conv_specs(tm):
    halo_blocks = tm // 8
    return [BS((tm, D_FF), lambda i: (i, 0)), BS((tm, D_FF), lambda i: (i, 1)),
            BS((8, D_FF), lambda i: (jnp.maximum(i * halo_blocks - 1, 0), 0)),
            BS((8, D_FF), lambda i: (jnp.maximum(i * halo_blocks - 1, 0), 1)),
            BS((3, D_FF), lambda i: (0, 0)), BS((3, D_FF), lambda i: (0, 1)),
            BS((1, D_FF), lambda i: (0, 0)), BS((1, D_FF), lambda i: (0, 1))]


def ffn_fwd_loss(x2, g, w_up3, cw, cb, w_down, target):
    S, D = x2.shape
    Q, _, C = w_up3.shape
    tm = _tile(S, (256,))

    def body(x_ref, g_ref, wu_ref, cw_ref, cb_ref, wd_ref, t_ref, h_ref, a_ref, f_ref, d_ref, l_ref, tail):
        first_tile = pl.program_id(0) == 0

        @pl.when(first_tile)
        def _():
            l_ref[...] = jnp.zeros_like(l_ref)
            tail[...] = jnp.zeros_like(tail)

        xv = x_ref[...]
        h_ref[...] = (xv * _rms(xv) * g_ref[...]).astype(h_ref.dtype)
        for q in range(Q):
            a_ref[:, q * C:(q + 1) * C] = _dot(h_ref[...], wu_ref[q])
        for c0 in range(0, D_FF, CONV_COLS):
            cols, ucols = slice(c0, c0 + CONV_COLS), slice(D_FF + c0, D_FF + c0 + CONV_COLS)
            cg, _ = _conv_taps(a_ref, tail, cw_ref, cb_ref, cols, first_tile)
            cu, _ = _conv_taps(a_ref, tail, cw_ref, cb_ref, ucols, first_tile)
            f_ref[:, cols] = (_gelu(cg) * cu).astype(f_ref.dtype)
        tail[...] = a_ref[tm - 8:tm, :]
        e = _dot(f_ref[...], wd_ref[...]) + xv - t_ref[...]
        d_ref[...] = e * (1.0 / D)
        l_ref[...] += jnp.sum(e * e) * (0.5 / D)

    row = lambda w: BS((tm, w), lambda i: (i, 0))
    const = lambda r, w: BS((r, w), lambda i: (0, 0))
    return _pcall(body, name="ffn_fwd_loss", grid=(S // tm,),
                  in_specs=[row(D), const(1, D), _resident(w_up3.shape), const(3, 2 * D_FF), const(1, 2 * D_FF),
                            _resident(w_down.shape), row(D)],
                  out_specs=[row(D), row(2 * D_FF), row(D_FF), row(D), const(8, 128)],
                  out_shape=[SDS((S, D), MXU_DTYPE), SDS((S, 2 * D_FF), F32), SDS((S, D_FF), MXU_DTYPE),
                             SDS((S, D), F32), SDS((8, 128), F32)],
                  scratch=[pltpu.VMEM((8, 2 * D_FF), F32)])(x2, g, w_up3, cw, cb, w_down, target)


def convgate_bwd(a, dx3, w3, cw, cb, after=None):
    S = a.shape[0]
    tm = _tile(S, (256,))

    def body(ag_ref, au_ref, hg_ref, hu_ref, wg_ref, wu_ref, bg_ref, bu_ref, dx_ref, wd_ref, dc_ref, gw_ref, df_ref):
        first_tile = pl.program_id(0) == 0

        @pl.when(first_tile)
        def _():
            gw_ref[...] = jnp.zeros_like(gw_ref)

        df_ref[...] = _nt_chunks(dx_ref, wd_ref)
        for c0 in range(0, D_FF, CONV_COLS):
            cols, ucols = slice(c0, c0 + CONV_COLS), slice(D_FF + c0, D_FF + c0 + CONV_COLS)
            cg, g_taps = _conv_taps(ag_ref, hg_ref, wg_ref, bg_ref, cols, first_tile)
            cu, u_taps = _conv_taps(au_ref, hu_ref, wu_ref, bu_ref, cols, first_tile)
            df_v = df_ref[:, cols]
            gate, gate_grad = _gelu_and_grad(cg)
            dcg = df_v * cu * gate_grad
            dcu = df_v * gate
            dc_ref[:, cols] = dcg
            dc_ref[:, ucols] = dcu
            for col, dcv, taps in ((cols, dcg, g_taps), (ucols, dcu, u_taps)):
                for j in range(3):
                    gw_ref[j:j + 1, col] += jnp.sum(dcv * taps[j], axis=0, keepdims=True)
                gw_ref[3:4, col] += jnp.sum(dcv, axis=0, keepdims=True)

    return _pcall(body, name="convgate_bwd", grid=(S // tm,), after=after,
                  in_specs=_conv_specs(tm) + [BS((tm, dx3.shape[1]), lambda i: (i, 0)), _resident(w3.shape)],
                  out_specs=[BS((tm, 2 * D_FF), lambda i: (i, 0)), BS((8, 2 * D_FF), lambda i: (0, 0))],
                  out_shape=[SDS((S, 2 * D_FF), F32), SDS((8, 2 * D_FF), F32)],
                  scratch=[pltpu.VMEM((tm, D_FF), F32)])(a, a, a, a, cw, cw, cb, cb, dx3, w3)


def conv_transpose_rms_bwd(dc, cw, w3, x, g, dres):
    S, C = dc.shape
    Q, N, Kc = w3.shape
    tm = _tile(S, (256,))
    nt = S // tm
    halo_blocks = tm // 8

    def body(dc_ref, halo_ref, cw_ref, w_ref, x_ref, g_ref, dr_ref, da_ref, dx_ref, dg_ref):
        @pl.when(pl.program_id(0) == 0)
        def _():
            dg_ref[...] = jnp.zeros_like(dg_ref)

        last_tile = pl.program_id(0) == nt - 1
        row = lax.broadcasted_iota(jnp.int32, (8, CONV_COLS), 0)
        for c0 in range(0, C, CONV_COLS):
            cols = slice(c0, c0 + CONV_COLS)
            h0 = jnp.where(last_tile, 0.0, halo_ref[0:1, cols])
            h1 = jnp.where(last_tile, 0.0, halo_ref[1:2, cols])
            dc_v = dc_ref[:, cols]
            r1, r2 = pltpu.roll(dc_v, tm - 1, 0), pltpu.roll(dc_v, tm - 2, 0)
            n1 = jnp.concatenate([r1[:tm - 8], jnp.where(row == 7, h0, r1[tm - 8:])], axis=0)
            n2 = jnp.concatenate([r2[:tm - 8], jnp.where(row == 7, h1, jnp.where(row == 6, h0, r2[tm - 8:]))], axis=0)
            da_ref[:, cols] = (cw_ref[2:3, cols] * dc_v + cw_ref[1:2, cols] * n1
                               + cw_ref[0:1, cols] * n2).astype(da_ref.dtype)
        xv = x_ref[...]
        dx, dgc = _rms_bwd(_nt_chunks(da_ref, w_ref), xv, g_ref[...], _rms(xv))
        dx_ref[...] = dr_ref[...] + dx
        _acc_rows(dg_ref, 0, dgc)

    row_n = BS((tm, N), lambda i: (i, 0))
    return _pcall(body, name="conv_transpose_rms_bwd", grid=(nt,),
                  in_specs=[BS((tm, C), lambda i: (i, 0)),
                            BS((8, C), lambda i: (jnp.minimum((i + 1) * halo_blocks, S // 8 - 1), 0)),
                            BS((3, C), lambda i: (0, 0)), _resident((Q, N, Kc)), row_n, BS((1, N), lambda i: (0, 0)),
                            row_n],
                  out_specs=[BS((tm, C), lambda i: (i, 0)), row_n, BS((8, N), lambda i: (0, 0))],
                  out_shape=[SDS((S, C), MXU_DTYPE), SDS((S, N), F32), SDS((8, N), F32)])(dc, dc, cw, w3, x, g, dres)


def xattn_block_bwd(dx2, wo3, qx, kn, vb, gxq, wq3, x1, g, after=None):
    S, D = qx.shape
    tm = _tile(S, (512, 256))

    def body(dx2_ref, wo_ref, q_ref, kn_ref, vb_ref, gxq_ref, wq_ref, x_ref, g_ref,
             dq_ref, dx_ref, dkn_ref, dv_ref, dgq_ref, dg_ref):
        @pl.when(pl.program_id(0) == 0)
        def _():
            for ref in (dkn_ref, dv_ref, dgq_ref, dg_ref):
                ref[...] = jnp.zeros_like(ref)

        gq = gxq_ref[...]
        do_all = _nt_chunks(dx2_ref, wo_ref)
        for h in range(XA_HEADS):
            sl = slice(h * XA_DH, (h + 1) * XA_DH)
            qh, do = q_ref[:, sl], do_all[:, sl]
            r, qn, p = _xa_probs(qh, gq, kn_ref[:, sl])
            dp = _dot(do, vb_ref[:, sl], NT)
            ds = p * (dp - jnp.sum(dp * p, axis=1, keepdims=True)) * (1.0 / math.sqrt(XA_DH))
            dqn = _dot(ds, kn_ref[:, sl])
            dkn_ref[:, sl] += _dot(ds, qn, TN)
            dv_ref[:, sl] += _dot(p, do, TN)
            dqh, dgc = _rms_bwd(dqn, qh, gq, r)
            dq_ref[:, sl] = dqh.astype(dq_ref.dtype)
            _acc_rows(dgq_ref, 0, dgc)
        xv = x_ref[...]
        dx, dgc = _rms_bwd(_nt_chunks(dq_ref, wq_ref), xv, g_ref[...], _rms(xv))
        dx_ref[...] = dx2_ref[...] + dx
        _acc_rows(dg_ref, 0, dgc)

    row = BS((tm, D), lambda i: (i, 0))
    full = lambda r, w: BS((r, w), lambda i: (0, 0))
    return _pcall(body, name="xattn_block_bwd", grid=(S // tm,), after=after,
                  in_specs=[row, _resident(wo3.shape), row, full(MEM_LEN, D), full(MEM_LEN, D), full(1, XA_DH),
                            _resident(wq3.shape), row, full(1, D)],
                  out_specs=[row, row, full(MEM_LEN, D), full(MEM_LEN, D), full(8, XA_DH), full(8, D)],
                  out_shape=[SDS((S, D), MXU_DTYPE), SDS((S, D), F32), SDS((MEM_LEN, D), F32), SDS((MEM_LEN, D), F32),
                             SDS((8, XA_DH), F32), SDS((8, D), F32)])(dx2, wo3, qx, kn, vb, gxq, wq3, x1, g)


def mem_bwd(kv, dkn, dvb, gxk, after=None):
    def body(kv_ref, dkn_ref, dv_ref, g_ref, dkv_ref, dg_ref):
        dg_ref[...] = jnp.zeros_like(dg_ref)
        for h in range(XA_HEADS):
            sl = slice(h * XA_DH, (h + 1) * XA_DH)
            k = kv_ref[:, sl]
            dk, dgc = _rms_bwd(dkn_ref[:, sl], k, g_ref[...], _rms(k))
            dkv_ref[:, sl] = dk.astype(dkv_ref.dtype)
            _acc_rows(dg_ref, 0, dgc)
        dkv_ref[:, 1024:2048] = dv_ref[...].astype(dkv_ref.dtype)

    full = lambda r, w: BS((r, w), lambda i: (0, 0))
    return _pcall(body, name="mem_bwd", grid=(1,), after=after,
                  in_specs=[full(MEM_LEN, 2048), full(MEM_LEN, 1024), full(MEM_LEN, 1024), full(1, XA_DH)],
                  out_specs=[full(MEM_LEN, 2048), full(8, XA_DH)],
                  out_shape=[SDS((MEM_LEN, 2048), MXU_DTYPE), SDS((8, XA_DH), F32)])(kv, dkn, dvb, gxk)


def _norm_rope_bwd(slabs, douts, g, bm, cos_v, sin_v, first):
    dqns = [d * cos_v + _half_swap(d * sin_v, first) for d in douts]
    rs = [lax.rsqrt(ms + EPS) for ms in _head_means([x * x for x in slabs], bm)]
    projs = _head_means([dqn * g * x for dqn, x in zip(dqns, slabs)], bm)
    dxs = [r * (dqn * g) - x * (r * r * r) * pr for x, dqn, r, pr in zip(slabs, dqns, rs, projs)]
    return dxs, [dqn * x * r for x, dqn, r in zip(slabs, dqns, rs)]


def mixer_core_bwd(proj, cos, sin, gq, gk, gvg, bmat, qr, kr, vb, sinkcol, dattn, dgm, gvn, gu, w2, w2t, bsl,
                   after=None):
    S = qr.shape[0]
    nb = S // BLK

    def body(p_ref, c_ref, s_ref, gq_ref, gk_ref, gvg_ref, b_ref, q_ref, kc_ref, kp_ref, vc_ref, vp_ref, sk_ref,
             do_ref, dgm_ref, gvn_ref, gu_ref, w2_ref, w2t_ref, bsl_ref,
             dp_ref, dsk_ref, dws_ref, dbl_ref, dgq_ref, dgk_ref, dgv_ref,
             carry_k, carry_v, done_k, done_v, dq_keep, dgu_keep, dgvn_keep):
        n = pl.program_id(0)

        @pl.when(n == 0)
        def _():
            for ref in (dsk_ref, dws_ref, dbl_ref, dgq_ref, dgk_ref, dgv_ref, carry_k, carry_v, dq_keep, dgu_keep,
                        dgvn_keep):
                ref[...] = jnp.zeros_like(ref)

        live = (n < nb).astype(F32)
        cos_v, sin_v, bm = c_ref[...], s_ref[...], b_ref[...]
        first = (_lane((BLK, 128)) & 63) < 32
        lo = _lane((BLK, 128)) < 64

        dxs, dgs = _norm_rope_bwd([p_ref[:, s * 128:(s + 1) * 128] for s in range(4)],
                                  [dq_keep[:, s * 128:(s + 1) * 128] for s in range(4)], gq_ref[...], bm,
                                  cos_v, sin_v, first)
        for s, (dx, dg) in enumerate(zip(dxs, dgs)):
            dp_ref[:, s * 128:(s + 1) * 128] = dx.astype(dp_ref.dtype)
            _acc_rows(dgq_ref, 0, dg)
        dp_ref[:, 1024:1536] = (dgu_keep[...] * _gelu_grad(p_ref[:, 1024:1536])).astype(dp_ref.dtype)
        gvp = p_ref[:, 1536:2048]
        gv = _gelu(gvp)
        dgv, dgc = _rms_bwd(dgvn_keep[...], gv, gvg_ref[...], _rms(gv))
        dp_ref[:, 1536:2048] = (dgv * _gelu_grad(gvp)).astype(dp_ref.dtype)
        _acc_rows(dgv_ref, 0, dgc)

        for h in range(2):
            hs, qs = slice(h * 128, (h + 1) * 128), slice(h * 256, (h + 1) * 256)
            kd = jnp.concatenate([kp_ref[:, hs], kc_ref[:, hs]], axis=0)
            vd = jnp.concatenate([vp_ref[:, hs], vc_ref[:, hs]], axis=0)
            sink = jnp.concatenate([sk_ref[2 * h], sk_ref[2 * h + 1]], axis=0)
            qp, p, psink = _swa_probs(q_ref[:, qs], kd, sink, n, lo)
            dop = _by_head(do_ref[:, qs], lo)
            dp = _dot(dop, vd, NT)
            delta = jnp.sum(dp * p, axis=1, keepdims=True)
            ds = p * (dp - delta) * (1.0 / math.sqrt(HEAD_DIM))
            dsink = -psink * delta * live
            dsk_ref[2 * h] += dsink[:2 * BLK]
            dsk_ref[2 * h + 1] += dsink[2 * BLK:]
            dq_keep[:, qs] = _from_heads(_dot(ds, kd), lo)
            dkd = _dot(ds, qp, TN)
            dvd = _dot(p, dop, TN)
            done_k[:, hs] = carry_k[:, hs] + live * dkd[:BLK]
            done_v[:, hs] = carry_v[:, hs] + live * dvd[:BLK]
            carry_k[:, hs] = dkd[BLK:]
            carry_v[:, hs] = dvd[BLK:]
        for j in range(4):
            sl = slice(j * 128, (j + 1) * 128)
            gvn_s = gvn_ref[:, sl]
            m2 = _dot(w2_ref[j], gvn_s)
            mixed = jnp.where(lo, m2[:BLK], m2[BLK:]) + bsl_ref[j]
            dgm_s = dgm_ref[:, sl]
            dgu_keep[:, sl] = dgm_s * mixed
            dmx = dgm_s * gu_ref[:, sl] * live
            d2 = _dot(w2t_ref[j], dmx)
            dgvn_keep[:, sl] = jnp.where(lo, d2[:BLK], d2[BLK:])
            z = jnp.zeros_like(dmx)
            dws_ref[2 * j] += _dot(jnp.where(lo, dmx, z), gvn_s, NT)
            dws_ref[2 * j + 1] += _dot(jnp.where(lo, z, dmx), gvn_s, NT)
            dbl_ref[j] += dmx

        dxs, dgs = _norm_rope_bwd([p_ref[:, 512 + s * 128:640 + s * 128] for s in range(2)],
                                  [done_k[:, s * 128:(s + 1) * 128] for s in range(2)], gk_ref[...], bm,
                                  cos_v, sin_v, first)
        for s, (dx, dg) in enumerate(zip(dxs, dgs)):
            dp_ref[:, 512 + s * 128:640 + s * 128] = dx.astype(dp_ref.dtype)
            _acc_rows(dgk_ref, 0, dg)
        dp_ref[:, 768:1024] = done_v[...].astype(dp_ref.dtype)

    last = nb - 1
    cur = lambda w: BS((BLK, w), lambda n: (jnp.minimum(n, last), 0))
    prev = lambda w: BS((BLK, w), lambda n: (jnp.clip(n - 1, 0, last), 0))
    done = lambda w: BS((BLK, w), lambda n: (jnp.maximum(n - 1, 0), 0))
    const = lambda *shape: BS(shape, lambda n: (0,) * len(shape))
    return _pcall(body, name="mixer_core_bwd", grid=(nb + 1,), after=after,
                  in_specs=[done(IN_COLS_DUP), done(128), done(128), const(1, 128), const(1, 128), const(1, 512),
                            const(128, 128), cur(512), cur(256), prev(256), cur(256), prev(256),
                            const(4, 2 * BLK, 1), cur(512), cur(512), cur(512), cur(512), const(4, 2 * BLK, BLK),
                            const(4, 2 * BLK, BLK), const(4, BLK, 128)],
                  out_specs=[done(IN_COLS_DUP), const(4, 2 * BLK, 1), const(8, BLK, BLK), const(4, BLK, 128),
                             const(8, 128), const(8, 128), const(8, 512)],
                  out_shape=[SDS((S, IN_COLS_DUP), MXU_DTYPE), SDS((4, 2 * BLK, 1), F32), SDS((8, BLK, BLK), F32),
                             SDS((4, BLK, 128), F32), SDS((8, 128), F32), SDS((8, 128), F32), SDS((8, 512), F32)],
                  scratch=[pltpu.VMEM((BLK, 256), F32)] * 4 + [pltpu.VMEM((BLK, 512), F32)] * 3)(
        proj, cos, sin, gq, gk, gvg, bmat, qr, kr, kr, vb, vb, sinkcol, dattn, dgm, gvn, gu, w2, w2t, bsl)


BIG = (("w_in", (1024, 448), True), ("w_out", (256, 1024), False), ("xa_wq", (256, 1024), False),
       ("xa_wkv", (1024, 512), True), ("xa_wo", (256, 1024), False), ("ffn_up", (1024, 1408), True),
       ("ffn_down", (704, 1024), False))
BIG_NAMES = tuple(n for n, _, _ in BIG)
SMALL_VECS = (("mix_norm", 1024), ("q_norm", 64), ("k_norm", 64), ("attn_sinks", 8), ("gmlp_v_norm", 512),
              ("attn_out_norm", 512), ("gmlp_out_norm", 512), ("xa_norm", 1024), ("mem_norm", 1024),
              ("xa_q_norm", 256), ("xa_k_norm", 256), ("ffn_norm", 1024), ("ffn_conv_b", 5632))
SMALL = tuple(n for n, _ in SMALL_VECS) + ("gmlp_bs", "gmlp_ws", "ffn_conv")
WEIGHTS = ("mix_norm", "w_in", "q_norm", "k_norm", "attn_sinks", "gmlp_v_norm", "gmlp_ws", "gmlp_bs",
           "attn_out_norm", "gmlp_out_norm", "w_out", "xa_norm", "mem_norm", "xa_wq", "xa_wkv", "xa_q_norm",
           "xa_k_norm", "xa_wo", "ffn_norm", "ffn_up", "ffn_conv", "ffn_conv_b", "ffn_down")
CONV_SHARD = (3, 1408)
CONV_LANE_ROWS = CONV_SHARD[1] // 128
CONV_CHIP_ROWS = 40


def _small_rows():
    rows, r = {}, 0
    for n, length in SMALL_VECS:
        rows[n] = r
        r += -(-length // 128)
    r += -r % 8
    rows["gmlp_bs"] = r
    r += 8
    rows["gmlp_ws"] = r
    r += 8 * BLK
    rows["ffn_conv"] = r
    r += N_CHIPS * CONV_CHIP_ROWS
    return rows, r


SMALL_ROW, SMALL_ROWS = _small_rows()


def pack_small(dg_mix, dgq, dgk, dsk, dg_gvn, dg_y, dg_xa, dg_mem, dg_xq, dg_xk, dg_ffn, gcw, dbl, dws):
    def body(mix_ref, q_ref, k_ref, sk_ref, gvn_ref, y_ref, xa_ref, mem_ref, xq_ref, xk_ref, ffn_ref, cw_ref,
             dbl_ref, dws_ref, o_ref):
        o_ref[...] = jnp.zeros_like(o_ref)
        lane = _lane((1, 128))

        def put(name, src_ref, row, lane0, length):
            for k in range(length // 128):
                o_ref[SMALL_ROW[name] + k:SMALL_ROW[name] + k + 1, :] = src_ref[row:row + 1, lane0 + k * 128:lane0 + (k + 1) * 128]

        put("mix_norm", mix_ref, 0, 0, 1024)
        for name, ref in (("q_norm", q_ref), ("k_norm", k_ref)):
            v = ref[0:1, :]
            o_ref[SMALL_ROW[name]:SMALL_ROW[name] + 1, :] = jnp.where(lane < HEAD_DIM, v + pltpu.roll(v, 64, 1), 0.0)
        sinks = jnp.zeros((1, 128), F32)
        for s in range(4):
            col = sk_ref[s]
            sinks = sinks + jnp.where(lane == 2 * s, jnp.sum(col[:BLK]), 0.0) + jnp.where(lane == 2 * s + 1, jnp.sum(col[BLK:]), 0.0)
        o_ref[SMALL_ROW["attn_sinks"]:SMALL_ROW["attn_sinks"] + 1, :] = sinks
        put("gmlp_v_norm", gvn_ref, 0, 0, 512)
        put("attn_out_norm", y_ref, 0, 0, 512)
        put("gmlp_out_norm", y_ref, 0, 512, 512)
        put("xa_norm", xa_ref, 0, 0, 1024)
        put("mem_norm", mem_ref, 0, 0, 1024)
        put("xa_q_norm", xq_ref, 0, 0, 256)
        put("xa_k_norm", xk_ref, 0, 0, 256)
        put("ffn_norm", ffn_ref, 0, 0, 1024)
        put("ffn_conv_b", cw_ref, 3, 0, 2 * D_FF)
        r8 = lax.broadcasted_iota(jnp.int32, (8, 128), 0)
        l8 = _lane((8, 128))
        bs = jnp.zeros((8, BLK), F32)
        for j in range(4):
            sel = (((r8 == 2 * j) & (l8 < 64)) | ((r8 == 2 * j + 1) & (l8 >= 64))).astype(F32).astype(BF16)
            xj = dbl_ref[j]
            hi = xj.astype(BF16)
            lo = (xj - hi.astype(F32)).astype(BF16)
            bs = bs + lax.dot_general(sel, hi, NT, preferred_element_type=F32) + lax.dot_general(sel, lo, NT, preferred_element_type=F32)
        o_ref[SMALL_ROW["gmlp_bs"]:SMALL_ROW["gmlp_bs"] + 8, :] = bs
        causal = lax.broadcasted_iota(jnp.int32, (BLK, BLK), 0) >= lax.broadcasted_iota(jnp.int32, (BLK, BLK), 1)
        for h in range(8):
            r0 = SMALL_ROW["gmlp_ws"] + h * BLK
            o_ref[r0:r0 + BLK, :] = jnp.where(causal, dws_ref[h], 0.0)
        for q in range(N_CHIPS):
            for j in range(3):
                for k in range(CONV_LANE_ROWS):
                    r0 = SMALL_ROW["ffn_conv"] + q * CONV_CHIP_ROWS + j * CONV_LANE_ROWS + k
                    l0 = (q * CONV_LANE_ROWS + k) * 128
                    o_ref[r0:r0 + 1, :] = cw_ref[j:j + 1, l0:l0 + 128]

    args = (dg_mix, dgq, dgk, dsk, dg_gvn, dg_y, dg_xa, dg_mem, dg_xq, dg_xk, dg_ffn, gcw, dbl, dws)
    full = lambda a: BS(a.shape, lambda i, nd=a.ndim: (0,) * nd)
    return _pcall(body, name="pack_small", grid=(1,), in_specs=[full(a) for a in args],
                  out_specs=BS((SMALL_ROWS, 128), lambda i: (0, 0)), out_shape=SDS((SMALL_ROWS, 128), F32))(*args)


def _adam(w, g, m, v):
    mn = ADAM_B1 * m + (1.0 - ADAM_B1) * g
    vn = ADAM_B2 * v + (1.0 - ADAM_B2) * (g * g)
    m_hat = mn / (1.0 - ADAM_B1 ** ADAM_STEP)
    v_hat = vn / (1.0 - ADAM_B2 ** ADAM_STEP)
    return -ADAM_LR * (m_hat / (jnp.sqrt(v_hat) + ADAM_EPS) + ADAM_WD * w), mn, vn


def adamw_small(gsum, w, m, v, chipvec):
    n = len(SMALL)

    def body(chip_ref, g_ref, *refs):
        w_refs, m_refs, v_refs = refs[:n], refs[n:2 * n], refs[2 * n:3 * n]
        outs = refs[3 * n:]
        go, do, mo, vo = outs[:n], outs[n:2 * n], outs[2 * n:3 * n], outs[3 * n:]

        def update(i, idx, g):
            d, mn, vn = _adam(w_refs[i][idx], g, m_refs[i][idx], v_refs[i][idx])
            go[i][idx] = g
            do[i][idx] = d
            mo[i][idx] = mn
            vo[i][idx] = vn

        for i, (name, length) in enumerate(SMALL_VECS):
            for k in range(-(-length // 128)):
                wd = min(128, length - k * 128)
                r = SMALL_ROW[name] + k
                update(i, (slice(0, 1), slice(k * 128, k * 128 + wd)), g_ref[r:r + 1, 0:wd])
        i_bs, i_ws, i_cv = len(SMALL_VECS), len(SMALL_VECS) + 1, len(SMALL_VECS) + 2
        update(i_bs, (0,), g_ref[SMALL_ROW["gmlp_bs"]:SMALL_ROW["gmlp_bs"] + 8, :])
        for h in range(8):
            r0 = SMALL_ROW["gmlp_ws"] + h * BLK
            update(i_ws, (0, h), g_ref[r0:r0 + BLK, :])
        mine = g_ref[pl.ds(pl.multiple_of(SMALL_ROW["ffn_conv"] + chip_ref[0] * CONV_CHIP_ROWS, 8), CONV_CHIP_ROWS), :]
        for j in range(3):
            for k in range(CONV_LANE_ROWS):
                r = j * CONV_LANE_ROWS + k
                update(i_cv, (0, slice(j, j + 1), slice(k * 128, (k + 1) * 128)), mine[r:r + 1, :])

    nat = [w[nm] for nm in SMALL]
    full = lambda a: BS(a.shape, lambda i, c, nd=a.ndim: (0,) * nd)
    outs = _pcall(body, name="adamw_small", grid=(1,), prefetch=1,
                  in_specs=[BS((SMALL_ROWS, 128), lambda i, c: (0, 0))] + [full(a) for a in nat] * 3,
                  out_specs=[full(a) for a in nat] * 4, out_shape=[SDS(a.shape, F32) for a in nat] * 4)(
        chipvec, gsum, *nat, *[m[nm] for nm in SMALL], *[v[nm] for nm in SMALL])
    return outs[:n], outs[n:2 * n], outs[2 * n:3 * n], outs[3 * n:]


def adamw_matrix(w, m, v, g_own, g_other, cvec, *, name):
    _, r, c = w.shape
    half = r // 2
    tr = _tile(half, (256, 176, 128))
    T = half // tr

    def body(c_ref, w_ref, m_ref, v_ref, own_ref, oth_ref, g_ref, d_ref, mo_ref, vo_ref):
        g = jnp.where(pl.program_id(0) == c_ref[0], own_ref[...], oth_ref[...])
        d, mn, vn = _adam(w_ref[...], g, m_ref[...], v_ref[...])
        g_ref[...] = g
        d_ref[...] = d
        mo_ref[...] = mn
        vo_ref[...] = vn

    nat = BS((None, tr, c), lambda hf, t, cr: (0, hf * T + t, 0))
    hlf = BS((tr, c), lambda hf, t, cr: (t, 0))
    return _pcall(body, name=name, grid=(2, T), prefetch=1, in_specs=[nat, nat, nat, hlf, hlf], out_specs=[nat] * 4,
                  out_shape=[SDS(w.shape, F32)] * 4)(cvec, w, m, v, g_own, g_other)


def _place():
    return lax.axis_index("x"), lax.axis_index("y"), lax.axis_index("c")


def _other_chips(x, y):
    return [(1 - x, y), (x, 1 - y), (1 - x, 1 - y)]


def _rows_of_core(c, half):
    return pl.ds(pl.multiple_of(c * half, 16), half)


def _rcopy(src, dst, sems, k, to):
    return pltpu.make_async_remote_copy(src_ref=src, dst_ref=dst, send_sem=sems[0].at[k], recv_sem=sems[1].at[k],
                                        device_id=to, device_id_type=MESH)


def cast_shards(shards, conv, chipvec):
    n = len(shards)

    def body(chip_ref, *refs):
        for i_ref, o_ref in zip(refs[:n + 1], refs[n + 1:]):
            o_ref[...] = i_ref[...].astype(o_ref.dtype)

    in_specs = [BS((s.shape[0] // 4, s.shape[1]), lambda i, p: (i, 0)) for s in shards]
    in_specs.append(BS(conv.shape, lambda i, p: (0, 0)))
    out_specs = [BS((None, s.shape[0] // 4, s.shape[1]), lambda i, p: (p[0], i, 0)) for s in shards]
    out_specs.append(BS((None,) + conv.shape, lambda i, p: (p[0], 0, 0)))
    out_shape = [SDS((N_CHIPS,) + s.shape, MXU_DTYPE) for s in shards] + [SDS((N_CHIPS,) + conv.shape, F32)]
    return _pcall(body, name="cast_shards", grid=(4,), prefetch=1, in_specs=in_specs, out_specs=out_specs,
                  out_shape=out_shape)(chipvec, *shards, conv)


HBM = pl.BlockSpec(memory_space=pltpu.HBM)
SEM = pl.BlockSpec(memory_space=pltpu.SEMAPHORE)
DATAFLOW = pltpu.SideEffectType.DATAFLOW_SIDE_EFFECTING
VMEM_WHOLE = pl.BlockSpec(memory_space=pltpu.VMEM)
TOKEN = jax.ShapeDtypeStruct((8, 128), jnp.float32)


def _gather_copies(bufs, send_sems, recv_sems, outgoing):
    x, y, c = _place()
    p = 2 * x + y
    cps = []
    for i, o in enumerate(bufs):
        for j, (cx, cy) in enumerate(_other_chips(x, y)):
            slot = o.at[p] if outgoing else o.at[2 * cx + cy]
            cps.append(_rcopy(slot, slot, (send_sems, recv_sems), 3 * i + j, (cx, cy, c)))
    return cps


def gather_start(slots, after):
    n = len(slots)

    def body(*refs):
        send_sems, recv_sems, thru, token = refs[n + 1], refs[n + 2], refs[n + 3:2 * n + 3], refs[2 * n + 3]
        for cp in _gather_copies(thru, send_sems, recv_sems, True):
            cp.start()
        token[...] = jnp.zeros_like(token)

    hbm = [pltpu.with_memory_space_constraint(s, pltpu.HBM) for s in slots]
    outs = pl.pallas_call(
        body, name="gather_start_%d" % n,
        out_shape=[pltpu.SemaphoreType.DMA((3 * n,)), pltpu.SemaphoreType.DMA((3 * n,))]
        + [pltpu.HBM(s.shape, s.dtype) for s in slots] + [TOKEN],
        in_specs=[HBM] * n + [ANY], out_specs=[SEM, SEM] + [HBM] * n + [VMEM_WHOLE],
        input_output_aliases={i: 2 + i for i in range(n)},
        compiler_params=pltpu.CompilerParams(has_side_effects=DATAFLOW))(*hbm, after)
    return outs[0], outs[1], outs[2:2 + n], outs[2 + n]


def gather_wait(send_sems, recv_sems, bufs, *after):
    n = len(bufs)

    def body(*refs):
        ins, send_ref, recv_ref = refs[:n], refs[n], refs[n + 1]
        for cp in _gather_copies(ins, send_ref, recv_ref, False):
            cp.wait_send()
            cp.wait_recv()

    return pl.pallas_call(
        body, name="gather_wait_%d" % n, out_shape=[pltpu.HBM(s.shape, s.dtype) for s in bufs],
        in_specs=[HBM] * n + [SEM, SEM] + [ANY] * len(after), out_specs=[HBM] * n,
        input_output_aliases={i: i for i in range(n)},
        compiler_params=pltpu.CompilerParams(has_side_effects=DATAFLOW))(*bufs, send_sems, recv_sems, *after)


def _peers(x, y, c):
    return [(1 - x if k & 4 else x, 1 - y if k & 2 else y, 1 - c if k & 1 else c) for k in range(1, N_DEV)]


def _partial_copies(g_ref, land_ref, send_sems, recv_sems, outgoing):
    x, y, c = _place()
    half = g_ref.shape[1] // 2
    cps = []
    for k, (px, py, pc) in enumerate(_peers(x, y, c)):
        src = g_ref.at[2 * px + py, _rows_of_core(pc, half)]
        dst = land_ref.at[4 * x + 2 * y + c] if outgoing else land_ref.at[4 * px + 2 * py + pc]
        cps.append(_rcopy(src, dst, (send_sems, recv_sems), k, (px, py, pc)))
    return cps


def partials_start(g, *, name):
    land = lax.empty((N_DEV, g.shape[1] // 2, g.shape[2]), g.dtype)

    def body(g_ref, land_ref, send_sems, recv_sems, g_thru, land_thru, token):
        for cp in _partial_copies(g_thru, land_thru, send_sems, recv_sems, True):
            cp.start()
        token[...] = jnp.zeros_like(token)

    return pl.pallas_call(
        body, name=name,
        out_shape=[pltpu.SemaphoreType.DMA((N_DEV - 1,)), pltpu.SemaphoreType.DMA((N_DEV - 1,)),
                   pltpu.HBM(g.shape, g.dtype), pltpu.HBM(land.shape, land.dtype), TOKEN],
        in_specs=[HBM, HBM], out_specs=[SEM, SEM, HBM, HBM, VMEM_WHOLE], input_output_aliases={0: 2, 1: 3},
        compiler_params=pltpu.CompilerParams(has_side_effects=DATAFLOW))(
        pltpu.with_memory_space_constraint(g, pltpu.HBM), pltpu.with_memory_space_constraint(land, pltpu.HBM))


def partials_wait(started, after):
    n = len(started)

    def body(*refs):
        for i in range(n):
            send_ref, recv_ref, g_ref, land_ref = refs[4 * i:4 * i + 4]
            for cp in _partial_copies(g_ref, land_ref, send_ref, recv_ref, False):
                cp.wait_send()
                cp.wait_recv()

    flat = [a for s in started for a in s]
    bufs = [a for s in started for a in s[2:]]
    outs = pl.pallas_call(
        body, name="partials_wait", out_shape=[pltpu.HBM(b.shape, b.dtype) for b in bufs],
        in_specs=[SEM, SEM, HBM, HBM] * n + [ANY], out_specs=[HBM] * (2 * n),
        input_output_aliases={4 * i + 2 + j: 2 * i + j for i in range(n) for j in range(2)},
        compiler_params=pltpu.CompilerParams(has_side_effects=DATAFLOW))(*flat, after)
    return [(outs[2 * i], outs[2 * i + 1]) for i in range(n)]


def sum_partials(pairs, order):
    n = len(pairs)

    def body(o_ref, *refs):
        j = pl.program_id(0)
        for g_ref, l_ref, f_ref in zip(refs[:n], refs[n:2 * n], refs[2 * n:]):
            @pl.when(j == 0)
            def _():
                f_ref[...] = g_ref[...].astype(F32)

            @pl.when(j > 0)
            def _():
                f_ref[...] += l_ref[...].astype(F32)

    g4 = [g.reshape(g.shape[0], 2, g.shape[1] // 2, g.shape[2]) for g, _ in pairs]
    lands = [l for _, l in pairs]
    return _pcall(body, name="sum_partials", grid=(N_DEV,), prefetch=1,
                  in_specs=[BS((None, None) + g.shape[2:], lambda j, o: (o[0], o[1], 0, 0)) for g in g4]
                  + [BS((None,) + l.shape[1:], lambda j, o: (o[jnp.maximum(j, 1) + 1], 0, 0)) for l in lands],
                  out_specs=[BS(l.shape[1:], lambda j, o: (0, 0)) for l in lands],
                  out_shape=[SDS(l.shape[1:], F32) for l in lands])(order, *g4, *lands)


def _pair_copies(f_refs, land_refs, send_sems, recv_sems):
    x, y, c = _place()
    return [_rcopy(f, o, (send_sems, recv_sems), i, (x, y, 1 - c)) for i, (f, o) in enumerate(zip(f_refs, land_refs))]


def pair_start(fs):
    n = len(fs)
    lands = [lax.empty(f.shape, f.dtype) for f in fs]

    def body(*refs):
        send_sems, recv_sems = refs[2 * n], refs[2 * n + 1]
        thru, land_thru, token = refs[2 * n + 2:3 * n + 2], refs[3 * n + 2:4 * n + 2], refs[4 * n + 2]
        for cp in _pair_copies(thru, land_thru, send_sems, recv_sems):
            cp.start()
        token[...] = jnp.zeros_like(token)

    hbm = [pltpu.with_memory_space_constraint(a, pltpu.HBM) for a in list(fs) + lands]
    outs = pl.pallas_call(
        body, name="pair_start",
        out_shape=[pltpu.SemaphoreType.DMA((n,)), pltpu.SemaphoreType.DMA((n,))]
        + [pltpu.HBM(a.shape, a.dtype) for a in list(fs) + lands] + [TOKEN],
        in_specs=[HBM] * (2 * n), out_specs=[SEM, SEM] + [HBM] * (2 * n) + [VMEM_WHOLE],
        input_output_aliases={i: 2 + i for i in range(2 * n)},
        compiler_params=pltpu.CompilerParams(has_side_effects=DATAFLOW))(*hbm)
    return outs[0], outs[1], outs[2:2 + n], outs[2 + n:2 + 2 * n], outs[2 + 2 * n]


def pair_wait(send_sems, recv_sems, fs, lands, after):
    n = len(fs)

    def body(*refs):
        for cp in _pair_copies(refs[:n], refs[n:2 * n], refs[2 * n], refs[2 * n + 1]):
            cp.wait_send()
            cp.wait_recv()

    outs = pl.pallas_call(
        body, name="pair_wait", out_shape=[pltpu.HBM(a.shape, a.dtype) for a in list(fs) + list(lands)],
        in_specs=[HBM] * (2 * n) + [SEM, SEM, ANY], out_specs=[HBM] * (2 * n),
        input_output_aliases={i: i for i in range(2 * n)},
        compiler_params=pltpu.CompilerParams(has_side_effects=DATAFLOW))(*fs, *lands, send_sems, recv_sems, after)
    return outs[:n], outs[n:]


def _small_copies(s_ref, land_ref, send_sems, recv_sems, outgoing):
    x, y, c = _place()
    cps = []
    for k, (px, py, pc) in enumerate(_peers(x, y, c)):
        dst = land_ref.at[4 * x + 2 * y + c] if outgoing else land_ref.at[4 * px + 2 * py + pc]
        cps.append(_rcopy(s_ref, dst, (send_sems, recv_sems), k, (px, py, pc)))
    return cps


def small_start(sm):
    land = lax.empty((N_DEV,) + sm.shape, sm.dtype)

    def body(s_ref, land_ref, send_sems, recv_sems, s_thru, land_thru):
        for cp in _small_copies(s_thru, land_thru, send_sems, recv_sems, True):
            cp.start()

    return pl.pallas_call(
        body, name="small_start",
        out_shape=[pltpu.SemaphoreType.DMA((N_DEV - 1,)), pltpu.SemaphoreType.DMA((N_DEV - 1,)),
                   pltpu.HBM(sm.shape, sm.dtype), pltpu.HBM(land.shape, land.dtype)],
        in_specs=[HBM, HBM], out_specs=[SEM, SEM, HBM, HBM], input_output_aliases={0: 2, 1: 3},
        compiler_params=pltpu.CompilerParams(has_side_effects=DATAFLOW))(
        pltpu.with_memory_space_constraint(sm, pltpu.HBM), pltpu.with_memory_space_constraint(land, pltpu.HBM))


def small_wait(send_sems, recv_sems, sm, land, *after):
    def body(send_ref, recv_ref, s_ref, land_ref, *rest):
        for cp in _small_copies(s_ref, land_ref, send_ref, recv_ref, False):
            cp.wait_send()
            cp.wait_recv()

    return pl.pallas_call(
        body, name="small_wait", out_shape=[pltpu.HBM(sm.shape, sm.dtype), pltpu.HBM(land.shape, land.dtype)],
        in_specs=[SEM, SEM, HBM, HBM] + [ANY] * len(after), out_specs=[HBM, HBM], input_output_aliases={2: 0, 3: 1},
        compiler_params=pltpu.CompilerParams(has_side_effects=DATAFLOW))(send_sems, recv_sems, sm, land, *after)


def sum_small(own, land, mevec):
    n, rows, width = land.shape
    tr = _tile(rows, (184, 8))

    def body(me_ref, own_ref, land_ref, o_ref):
        acc = jnp.zeros((tr, width), F32)
        for s in range(n):
            acc = acc + jnp.where(me_ref[0] == s, own_ref[...], land_ref[s])
        o_ref[...] = acc

    return _pcall(body, name="sum_small", grid=(rows // tr,), prefetch=1,
                  in_specs=[BS((tr, width), lambda i, me: (i, 0)), BS((n, tr, width), lambda i, me: (0, i, 0))],
                  out_specs=BS((tr, width), lambda i, me: (i, 0)), out_shape=SDS((rows, width), F32))(mevec, own, land)


def _to_full(blk, col):
    n, r, c = blk.shape
    return blk.transpose(1, 0, 2).reshape(r, n * c) if col else blk.reshape(n * r, c)


def _dup_cols(w):
    dup = lambda t: jnp.concatenate([t[:, :64], t[:, :64], t[:, 64:], t[:, 64:]], axis=1)
    return jnp.concatenate([w[:, :512], dup(w[:, 512:640]), dup(w[:, 640:768]), w[:, 768:]], axis=1)


def _fold_cols(d):
    fold = lambda t: jnp.concatenate([t[:, 0:64] + t[:, 64:128], t[:, 128:192] + t[:, 192:256]], axis=1)
    return jnp.concatenate([d[:, :512], fold(d[:, 512:768]), fold(d[:, 768:1024]), d[:, 1024:]], axis=1)


def _local_step(x, mem, positions, target, w_in, later, sp, emit):
    gain = lambda n: sp[n].reshape(1, -1)
    half = HEAD_DIM // 2
    inv_freq = 1.0 / (10000.0 ** (jnp.arange(half, dtype=F32) * (2.0 / HEAD_DIM)))
    ang = positions.astype(F32)[:, None] * inv_freq
    cos, sin = jnp.cos(ang), jnp.sin(ang)
    cos128 = jnp.tile(cos, (1, 4))
    sin128 = jnp.concatenate([-sin, sin, -sin, sin], axis=1)
    seg = jnp.arange(128) // HEAD_DIM
    bmat = (seg[:, None] == seg[None, :]).astype(BF16)
    gq128, gk128 = jnp.tile(gain("q_norm"), (1, 2)), jnp.tile(gain("k_norm"), (1, 2))
    sinkcol = jnp.repeat(sp["attn_sinks"].reshape(4, 2), BLK, axis=1).reshape(4, 2 * BLK, 1)
    wsc = sp["gmlp_ws"] * jnp.tril(jnp.ones((BLK, BLK), F32))[None]
    w2 = wsc.reshape(4, 2 * BLK, BLK).astype(MXU_DTYPE)
    w2t = wsc.swapaxes(1, 2).reshape(4, 2 * BLK, BLK).astype(MXU_DTYPE)
    bsl = jnp.repeat(sp["gmlp_bs"].reshape(4, 2, BLK).transpose(0, 2, 1), HEAD_DIM, axis=2)
    cb = sp["ffn_conv_b"].reshape(1, -1)
    w_in_d = _dup_cols(_to_full(w_in(cos128, sin128, gq128, gk128, sinkcol, w2, w2t, bsl), True))[None]

    h1, proj = rms_mm(x, gain("mix_norm"), w_in_d, name="mix_in")
    qr, kr, vb, gu, gvn, attn, gm, y = mixer_core_fwd(proj, cos128, sin128, gq128, gk128, gain("gmlp_v_norm"), bmat,
                                                      sinkcol, gain("attn_out_norm"), w2, bsl, gain("gmlp_out_norm"))
    wf, last = later(y)
    w_out, xa_wq, xa_wo = (_to_full(wf[n], False) for n in ("w_out", "xa_wq", "xa_wo"))
    x1 = mm(y, w_out, res=x, name="mix_out")
    mn, kv = rms_mm(mem, gain("mem_norm"), wf["xa_wkv"], name="xa_kv")
    kn, vbx = mem_pre(kv, gain("xa_k_norm"))
    h2, qx, xo, x2 = xattn_block_fwd(x1, gain("xa_norm"), xa_wq, kn, vbx, gain("xa_q_norm"), xa_wo)
    ffn_w, cw = last(x2)
    wf = {**wf, **ffn_w}
    ffn_down = _to_full(wf["ffn_down"], False)
    h3, a, f, dx3, loss_acc = ffn_fwd_loss(x2, gain("ffn_norm"), wf["ffn_up"], cw, cb, ffn_down, target)

    by_rows = lambda g: g.reshape(N_CHIPS, g.shape[1] // N_CHIPS, g.shape[2])
    sent = emit("ffn_down", by_rows(mm_tn(f, dx3, name="g_ffn_down", out_dtype=WIRE_DTYPE)))
    dc, gcw = convgate_bwd(a, dx3, ffn_down[None], cw, cb, after=sent)
    da, dx2, dg_ffn = conv_transpose_rms_bwd(dc, cw, wf["ffn_up"], x2, gain("ffn_norm"), dx3)
    sent = emit("ffn_up", mm_tn(h3, da, name="g_ffn_up", out_dtype=WIRE_DTYPE, chunks=N_CHIPS))
    sent = emit("xa_wo", by_rows(mm_tn(xo, dx2, name="g_xa_wo", out_dtype=WIRE_DTYPE, after=sent)))
    dqx, dx1, dkn, dvx, dg_xq, dg_xa = xattn_block_bwd(dx2, xa_wo[None], qx, kn, vbx, gain("xa_q_norm"), xa_wq[None],
                                                       x1, gain("xa_norm"), after=sent)
    sent = emit("xa_wq", by_rows(mm_tn(h2, dqx, name="g_xa_wq", out_dtype=WIRE_DTYPE)))
    dkv, dg_xk = mem_bwd(kv, dkn, dvx, gain("xa_k_norm"), after=sent)
    _, dg_mem = mm_nt_rms_bwd(dkv, wf["xa_wkv"], mem, gain("mem_norm"), jnp.zeros_like(mem), name="d_mem")
    sent = emit("xa_wkv", mm_tn(mn, dkv, name="g_xa_wkv", out_dtype=WIRE_DTYPE, chunks=N_CHIPS))
    dattn, dgm, dg_y = mm_nt_post_bwd(dx1, w_out[None], attn, gm, gain("attn_out_norm"), gain("gmlp_out_norm"),
                                      name="d_mix_out", after=sent)
    sent = emit("w_out", by_rows(mm_tn(y, dx1, name="g_w_out", out_dtype=WIRE_DTYPE)))
    dproj, dsk, dws, dbl, dgq, dgk, dg_gvn = mixer_core_bwd(
        proj, cos128, sin128, gq128, gk128, gain("gmlp_v_norm"), bmat, qr, kr, vb, sinkcol, dattn, dgm, gvn, gu,
        w2, w2t, bsl, after=sent)
    g_in = _fold_cols(mm_tn(h1, dproj, name="g_w_in", out_dtype=F32)[0])
    sent = emit("w_in", g_in.reshape(1024, N_CHIPS, 448).transpose(1, 0, 2).astype(WIRE_DTYPE))
    grad_x, dg_mix = mm_nt_rms_bwd(dproj, w_in_d, x, gain("mix_norm"), dx1, name="d_x", after=sent)
    packed = pack_small(dg_mix, dgq, dgk, dsk, dg_gvn, dg_y, dg_xa, dg_mem, dg_xq, dg_xk, dg_ffn, gcw, dbl, dws)
    return loss_acc, grad_x, packed


def _gather_step(w, chipvec):
    slots = cast_shards([w[n][0] for n in BIG_NAMES], w["ffn_conv"][0], chipvec)
    send_a, recv_a, first, token = gather_start(slots[:1], chipvec)
    send_b, recv_b, mid, token = gather_start(slots[1:5], token)
    send_c, recv_c, rest, token = gather_start(slots[5:], token)

    def w_in(*after):
        return gather_wait(send_a, recv_a, first, token, *after)[0]

    def last(after):
        got = gather_wait(send_c, recv_c, rest, after)
        return dict(zip(BIG_NAMES[5:], got[:-1])), _to_full(got[-1], True)

    def later(after):
        return dict(zip(BIG_NAMES[1:5], gather_wait(send_b, recv_b, mid, after))), last

    return w_in, later, token


def _reduce_update(started, packed, w, m, v, chipvec, cvec, order):
    small_sent = small_start(packed)
    own = sum_partials(partials_wait([started[n] for n in BIG_NAMES], small_sent[2]), order)
    pair_send, pair_recv, own, lands, pair_started = pair_start(own)
    own, other = pair_wait(pair_send, pair_recv, own, lands, pair_started)
    res = [{}, {}, {}, {}]
    for n, g_own, g_other in zip(BIG_NAMES, own, other):
        for d, o in zip(res, adamw_matrix(w[n], m[n], v[n], g_own, g_other, cvec, name="adamw_" + n)):
            d[n] = o
    mevec = (2 * order[0:1] + order[1:2]).astype(jnp.int32)
    small_sum = sum_small(*small_wait(*small_sent, *[res[3][n] for n in BIG_NAMES]), mevec)
    for d, outs in zip(res, adamw_small(small_sum, w, m, v, chipvec)):
        d.update(zip(SMALL, outs))
    return res


def kernel(x, mem, positions, mix_norm, w_in, q_norm, k_norm, attn_sinks, gmlp_v_norm, gmlp_ws, gmlp_bs, attn_out_norm, gmlp_out_norm, w_out, xa_norm, mem_norm, xa_wq, xa_wkv, xa_q_norm, xa_k_norm, xa_wo, ffn_norm, ffn_up, ffn_conv, ffn_conv_b, ffn_down, loss_target, m_mix_norm, m_w_in, m_q_norm, m_k_norm, m_attn_sinks, m_gmlp_v_norm, m_gmlp_ws, m_gmlp_bs, m_attn_out_norm, m_gmlp_out_norm, m_w_out, m_xa_norm, m_mem_norm, m_xa_wq, m_xa_wkv, m_xa_q_norm, m_xa_k_norm, m_xa_wo, m_ffn_norm, m_ffn_up, m_ffn_conv, m_ffn_conv_b, m_ffn_down, v_mix_norm, v_w_in, v_q_norm, v_k_norm, v_attn_sinks, v_gmlp_v_norm, v_gmlp_ws, v_gmlp_bs, v_attn_out_norm, v_gmlp_out_norm, v_w_out, v_xa_norm, v_mem_norm, v_xa_wq, v_xa_wkv, v_xa_q_norm, v_xa_k_norm, v_xa_wo, v_ffn_norm, v_ffn_up, v_ffn_conv, v_ffn_conv_b, v_ffn_down):
    w = dict(mix_norm=mix_norm, w_in=w_in, q_norm=q_norm, k_norm=k_norm, attn_sinks=attn_sinks, gmlp_v_norm=gmlp_v_norm, gmlp_ws=gmlp_ws, gmlp_bs=gmlp_bs, attn_out_norm=attn_out_norm, gmlp_out_norm=gmlp_out_norm, w_out=w_out, xa_norm=xa_norm, mem_norm=mem_norm, xa_wq=xa_wq, xa_wkv=xa_wkv, xa_q_norm=xa_q_norm, xa_k_norm=xa_k_norm, xa_wo=xa_wo, ffn_norm=ffn_norm, ffn_up=ffn_up, ffn_conv=ffn_conv, ffn_conv_b=ffn_conv_b, ffn_down=ffn_down)
    m = dict(mix_norm=m_mix_norm, w_in=m_w_in, q_norm=m_q_norm, k_norm=m_k_norm, attn_sinks=m_attn_sinks, gmlp_v_norm=m_gmlp_v_norm, gmlp_ws=m_gmlp_ws, gmlp_bs=m_gmlp_bs, attn_out_norm=m_attn_out_norm, gmlp_out_norm=m_gmlp_out_norm, w_out=m_w_out, xa_norm=m_xa_norm, mem_norm=m_mem_norm, xa_wq=m_xa_wq, xa_wkv=m_xa_wkv, xa_q_norm=m_xa_q_norm, xa_k_norm=m_xa_k_norm, xa_wo=m_xa_wo, ffn_norm=m_ffn_norm, ffn_up=m_ffn_up, ffn_conv=m_ffn_conv, ffn_conv_b=m_ffn_conv_b, ffn_down=m_ffn_down)
    v = dict(mix_norm=v_mix_norm, w_in=v_w_in, q_norm=v_q_norm, k_norm=v_k_norm, attn_sinks=v_attn_sinks, gmlp_v_norm=v_gmlp_v_norm, gmlp_ws=v_gmlp_ws, gmlp_bs=v_gmlp_bs, attn_out_norm=v_attn_out_norm, gmlp_out_norm=v_gmlp_out_norm, w_out=v_w_out, xa_norm=v_xa_norm, mem_norm=v_mem_norm, xa_wq=v_xa_wq, xa_wkv=v_xa_wkv, xa_q_norm=v_xa_q_norm, xa_k_norm=v_xa_k_norm, xa_wo=v_xa_wo, ffn_norm=v_ffn_norm, ffn_up=v_ffn_up, ffn_conv=v_ffn_conv, ffn_conv_b=v_ffn_conv_b, ffn_down=v_ffn_down)
    ix, iy, ic = lax.axis_index("x"), lax.axis_index("y"), lax.axis_index("c")
    chip = 2 * ix + iy
    chipvec = chip.astype(jnp.int32).reshape(1)
    cvec = ic.astype(jnp.int32).reshape(1)
    order = jnp.stack([chip, ic] + [4 * px + 2 * py + pc for px, py, pc in _peers(ix, iy, ic)]).astype(jnp.int32)

    w_in_all, later, token = _gather_step(w, chipvec)
    zero = token[0, 0]
    sp = {n: w[n][0] + zero for n in SMALL if n != "ffn_conv"}
    positions = positions + zero.astype(jnp.int32)
    started = {}

    def emit(name, g):
        *started[name], token = partials_start(g, name="partials_start_" + name)
        return token

    loss_acc, grad_x, packed = _local_step(x[0], mem[0], positions[0], loss_target[0], w_in_all, later, sp, emit)
    grads, delta, new_m, new_v = _reduce_update(started, packed, w, m, v, chipvec, cvec, order)
    loss = lax.psum(loss_acc[0, 0], ("x", "y", "c"))
    ordered = lambda d: [d[n] for n in WEIGHTS]
    return (loss, grad_x[None], *ordered(grads), *ordered(delta), *ordered(new_m), *ordered(new_v))
```

```python
import math

import jax
import jax.numpy as jnp
from jax import lax
from jax.experimental import pallas as pl
from jax.experimental.pallas import tpu as pltpu

F32 = jnp.float32
BF16 = jnp.bfloat16
MXU_DTYPE = jnp.bfloat16
WIRE_DTYPE = jnp.bfloat16
EPS = 1e-6
VMEM_LIMIT_V7X = 56 * 1024 * 1024

D_MODEL = 1024
HEAD_DIM = 64
BLK = 128
XA_HEADS = 4
XA_DH = 256
MEM_LEN = 256
D_FF = 2816
IN_COLS_DUP = 2048
N_CHIPS = 4
N_DEV = 8

ADAM_LR = 0.001
ADAM_B1 = 0.9
ADAM_B2 = 0.999
ADAM_EPS = 1e-08
ADAM_WD = 0.01
ADAM_STEP = 10

NT = (((1,), (1,)), ((), ()))
TN = (((0,), (0,)), ((), ()))
NN = (((1,), (0,)), ((), ()))
MINF = float(jnp.finfo(jnp.float32).min)
GELU_K0 = math.sqrt(2.0 / math.pi)
GELU_K1 = 0.044715

BS = pl.BlockSpec
SDS = jax.ShapeDtypeStruct
ANY = pl.BlockSpec(memory_space=pl.ANY)
MESH = pl.DeviceIdType.MESH


def _dot(a, b, dims=NN):
    return lax.dot_general(a.astype(MXU_DTYPE), b.astype(MXU_DTYPE), dims, preferred_element_type=F32)


def _segsum(x, bmat):
    hi = x.astype(BF16)
    lo = (x - hi.astype(F32)).astype(BF16)
    return (jnp.dot(hi, bmat, preferred_element_type=F32) + jnp.dot(lo, bmat, preferred_element_type=F32))


def _gelu(x):
    return 0.5 * x * (1.0 + jnp.tanh(GELU_K0 * (x + GELU_K1 * x * x * x)))


def _gelu_grad(x):
    t = jnp.tanh(GELU_K0 * (x + GELU_K1 * x * x * x))
    return 0.5 * (1.0 + t) + 0.5 * x * (1.0 - t * t) * GELU_K0 * (1.0 + 3.0 * GELU_K1 * x * x)


def _gelu_and_grad(x):
    x2 = x * x
    t = jnp.tanh(x * (GELU_K0 * GELU_K1 * x2 + GELU_K0))
    hx = 0.5 * x
    return hx * t + hx, 0.5 * t + 0.5 + hx * (1.0 - t * t) * (3.0 * GELU_K0 * GELU_K1 * x2 + GELU_K0)


def _rms(x):
    return lax.rsqrt(jnp.mean(x * x, axis=-1, keepdims=True) + EPS)


def _rms_bwd(dy, x, g, r):
    dyg = dy * g
    dx = r * dyg - x * (r * r * r) * jnp.mean(dyg * x, axis=-1, keepdims=True)
    return dx, dy * x * r


def _pcall(body, *, name, grid, in_specs, out_specs, out_shape, scratch=(), prefetch=0, after=None):
    params = pltpu.CompilerParams(dimension_semantics=("arbitrary",) * len(grid), vmem_limit_bytes=VMEM_LIMIT_V7X)
    in_specs = list(in_specs)
    kernel_fn = body
    if after is not None:
        n_in = prefetch + len(in_specs)
        in_specs.append(ANY)

        def kernel_fn(*refs):
            return body(*refs[:n_in], *refs[n_in + 1:])

    if prefetch:
        spec = pltpu.PrefetchScalarGridSpec(num_scalar_prefetch=prefetch, grid=grid, in_specs=in_specs,
                                            out_specs=out_specs, scratch_shapes=list(scratch))
        call = pl.pallas_call(kernel_fn, name=name, grid_spec=spec, out_shape=out_shape, compiler_params=params)
    else:
        call = pl.pallas_call(kernel_fn, name=name, grid=grid, in_specs=in_specs, out_specs=out_specs,
                              out_shape=out_shape, scratch_shapes=list(scratch), compiler_params=params)
    return call if after is None else (lambda *args: call(*args, after))


def _tile(n, prefs):
    for p in prefs:
        if p <= n and n % p == 0:
            return p
    return n


def _resident(shape):
    return pl.BlockSpec(shape, lambda *_: (0,) * len(shape), pipeline_mode=pl.Buffered(1))


def _acc_rows(ref, row, val):
    ref[row:row + 1, :] += jnp.sum(val, axis=0, keepdims=True)


def rms_mm(x, g, w3, *, name, tm=1024):
    M, K = x.shape
    Q, _, C = w3.shape
    tm = _tile(M, (tm, 256))

    def body(x_ref, g_ref, w_ref, h_ref, o_ref):
        def write_h():
            xv = x_ref[...]
            h_ref[...] = (xv * _rms(xv) * g_ref[...]).astype(h_ref.dtype)

        if Q == 1:
            write_h()
        else:
            pl.when(pl.program_id(1) == 0)(write_h)
        o_ref[...] = _dot(h_ref[...], w_ref[pl.program_id(1)])

    return _pcall(body, name=name, grid=(M // tm, Q),
                  in_specs=[BS((tm, K), lambda i, j: (i, 0)), BS((1, K), lambda i, j: (0, 0)),
                            _resident((Q, K, C))],
                  out_specs=[BS((tm, K), lambda i, j: (i, 0)), BS((tm, C), lambda i, j: (i, j))],
                  out_shape=[SDS((M, K), MXU_DTYPE), SDS((M, Q * C), F32)])(x, g, w3)


def mm(a, w, *, name, res):
    M, K = a.shape
    N = w.shape[1]
    tm = _tile(M, (1024, 256))

    def body(a_ref, w_ref, r_ref, o_ref):
        o_ref[...] = _dot(a_ref[...], w_ref[...]) + r_ref[...]

    return _pcall(body, name=name, grid=(M // tm,),
                  in_specs=[BS((tm, K), lambda i: (i, 0)), _resident((K, N)), BS((tm, N), lambda i: (i, 0))],
                  out_specs=BS((tm, N), lambda i: (i, 0)), out_shape=SDS((M, N), F32))(a, w, res)


def _nt_chunks(a_ref, w_ref):
    q_n, _, kc = w_ref.shape
    acc = _dot(a_ref[:, 0:kc], w_ref[0], NT)
    for q in range(1, q_n):
        acc = acc + _dot(a_ref[:, q * kc:(q + 1) * kc], w_ref[q], NT)
    return acc


def mm_nt_rms_bwd(a, w3, x, g, dres, *, name, tm=512, after=None):
    M = a.shape[0]
    Q, N, Kc = w3.shape
    tm = _tile(M, (tm, 256))

    def body(a_ref, w_ref, x_ref, g_ref, dr_ref, dx_ref, dg_ref):
        @pl.when(pl.program_id(0) == 0)
        def _():
            dg_ref[...] = jnp.zeros_like(dg_ref)

        xv = x_ref[...]
        dx, dgc = _rms_bwd(_nt_chunks(a_ref, w_ref), xv, g_ref[...], _rms(xv))
        dx_ref[...] = dr_ref[...] + dx
        _acc_rows(dg_ref, 0, dgc)

    row = BS((tm, N), lambda i: (i, 0))
    return _pcall(body, name=name, grid=(M // tm,), after=after,
                  in_specs=[BS((tm, Q * Kc), lambda i: (i, 0)), _resident((Q, N, Kc)), row,
                            BS((1, N), lambda i: (0, 0)), row],
                  out_specs=[row, BS((8, N), lambda i: (0, 0))],
                  out_shape=[SDS((M, N), F32), SDS((8, N), F32)])(a, w3, x, g, dres)


def mm_nt_post_bwd(a, w3, attn, gm, gao, ggo, *, name, after=None):
    M = a.shape[0]
    Q, N, Kc = w3.shape
    tm = _tile(M, (512, 256))
    hw = N // 2

    def body(a_ref, w_ref, at_ref, gm_ref, gao_ref, ggo_ref, da_ref, dgm_ref, dg_ref):
        @pl.when(pl.program_id(0) == 0)
        def _():
            dg_ref[...] = jnp.zeros_like(dg_ref)

        dy = _nt_chunks(a_ref, w_ref)
        av, gmv = at_ref[...], gm_ref[...]
        da, dga = _rms_bwd(dy[:, :hw], av, gao_ref[...], _rms(av))
        dgm, dgg = _rms_bwd(dy[:, hw:], gmv, ggo_ref[...], _rms(gmv))
        da_ref[...] = da
        dgm_ref[...] = dgm
        dg_ref[0:1, :hw] += jnp.sum(dga, axis=0, keepdims=True)
        dg_ref[0:1, hw:] += jnp.sum(dgg, axis=0, keepdims=True)

    half = BS((tm, hw), lambda i: (i, 0))
    const = lambda r, w: BS((r, w), lambda i: (0, 0))
    return _pcall(body, name=name, grid=(M // tm,), after=after,
                  in_specs=[BS((tm, Q * Kc), lambda i: (i, 0)), _resident((Q, N, Kc)), half, half,
                            const(1, hw), const(1, hw)],
                  out_specs=[half, half, const(8, N)],
                  out_shape=[SDS((M, hw), F32), SDS((M, hw), F32), SDS((8, N), F32)])(a, w3, attn, gm, gao, ggo)


def mm_tn(a, b, *, name, out_dtype, chunks=1, after=None):
    M, K = a.shape
    N = b.shape[1]
    C = N // chunks
    tm = _tile(M, (1024, 256))
    tk = _tile(K, (1408, 1024, 512))
    tn = _tile(C, (1408, 1024, 512))
    per = C // tn
    nm = M // tm

    def body(a_ref, b_ref, o_ref, acc):
        m = pl.program_id(2)

        @pl.when(m == 0)
        def _():
            acc[...] = jnp.zeros_like(acc)

        acc[...] += _dot(a_ref[...], b_ref[...], TN)

        @pl.when(m == nm - 1)
        def _():
            o_ref[...] = acc[...].astype(o_ref.dtype)

    return _pcall(body, name=name, grid=(K // tk, N // tn, nm), after=after,
                  in_specs=[BS((tm, tk), lambda k, n, m: (m, k)), BS((tm, tn), lambda k, n, m: (m, n))],
                  out_specs=BS((None, tk, tn), lambda k, n, m: (n // per, k, n % per)),
                  out_shape=SDS((chunks, K, C), out_dtype), scratch=[pltpu.VMEM((tk, tn), F32)])(a, b)


def _lane(shape):
    return lax.broadcasted_iota(jnp.int32, shape, 1)


def _head_means(slabs, bmat):
    tm = slabs[0].shape[0]
    means = _segsum(jnp.concatenate(slabs, axis=0), bmat) * (1.0 / HEAD_DIM)
    return [means[i * tm:(i + 1) * tm] for i in range(len(slabs))]


def _half_swap(x, first):
    return jnp.where(first, pltpu.roll(x, 96, 1), pltpu.roll(x, 32, 1))


def _by_head(x2, lo):
    z = jnp.zeros((BLK, 128), x2.dtype)
    parts = []
    for s in range(2):
        xs = x2[:, s * 128:(s + 1) * 128]
        parts += [jnp.where(lo, xs, z), jnp.where(lo, z, xs)]
    return jnp.concatenate(parts, axis=0)


def _from_heads(o4, lo):
    return jnp.concatenate([jnp.where(lo, o4[0:BLK], o4[BLK:2 * BLK]),
                            jnp.where(lo, o4[2 * BLK:3 * BLK], o4[3 * BLK:])], axis=1)


def _swa_probs(q2, kd, sink, n, lo):
    qp = _by_head(q2, lo)
    sc = _dot(qp, kd, NT) * (1.0 / math.sqrt(HEAD_DIM))
    r_i = lax.broadcasted_iota(jnp.int32, (4 * BLK, 2 * BLK), 0)
    k_j = lax.broadcasted_iota(jnp.int32, (4 * BLK, 2 * BLK), 1)
    diff = (r_i & (BLK - 1)) + BLK - k_j
    mask = (diff >= 0) & (diff < BLK) & ((k_j >= BLK) | (n > 0))
    sc = jnp.where(mask, sc, MINF)
    m = jnp.maximum(jnp.max(sc, axis=1, keepdims=True), sink)
    p = jnp.exp(sc - m)
    es = jnp.exp(sink - m)
    inv = 1.0 / (jnp.sum(p, axis=1, keepdims=True) + es)
    return qp, p * inv, es * inv


def mixer_core_fwd(proj, cos, sin, gq, gk, gvn, bmat, sinkcol, gao, w2, bsl, ggo):
    S = proj.shape[0]
    sub = 4 if S % (4 * BLK) == 0 else 1

    def body(p_ref, c_ref, s_ref, gq_ref, gk_ref, gvn_ref, b_ref, sk_ref, gao_ref, w2_ref, bsl_ref, ggo_ref,
             qr_ref, kr_ref, vb_ref, gu_ref, gvo_ref, at_ref, gm_ref, y_ref, k_prev, v_prev):
        n = pl.program_id(0)

        @pl.when(n == 0)
        def _():
            k_prev[...] = jnp.zeros_like(k_prev)
            v_prev[...] = jnp.zeros_like(v_prev)

        bm = b_ref[...]
        first = (_lane((BLK, 128)) & 63) < 32
        lo = _lane((BLK, 128)) < 64
        for sb in range(sub):
            rs = slice(sb * BLK, (sb + 1) * BLK)
            cos_v, sin_v = c_ref[rs, :], s_ref[rs, :]
            slabs = [p_ref[rs, s * 128:(s + 1) * 128] for s in range(6)]
            for s, (slab, ms) in enumerate(zip(slabs, _head_means([x * x for x in slabs], bm))):
                qn = slab * lax.rsqrt(ms + EPS) * (gq_ref[...] if s < 4 else gk_ref[...])
                out = qn * cos_v + _half_swap(qn, first) * sin_v
                if s < 4:
                    qr_ref[rs, s * 128:(s + 1) * 128] = out.astype(qr_ref.dtype)
                else:
                    kr_ref[rs, (s - 4) * 128:(s - 3) * 128] = out.astype(kr_ref.dtype)
            vb_ref[rs, :] = p_ref[rs, 768:1024].astype(vb_ref.dtype)
            gu_ref[rs, :] = _gelu(p_ref[rs, 1024:1536])
            gv = _gelu(p_ref[rs, 1536:2048])
            gvo_ref[rs, :] = (gv * _rms(gv) * gvn_ref[...]).astype(gvo_ref.dtype)

            before = slice((sb - 1) * BLK, sb * BLK)
            for h in range(2):
                hs, qs = slice(h * 128, (h + 1) * 128), slice(h * 256, (h + 1) * 256)
                k_before = k_prev[:, hs] if sb == 0 else kr_ref[before, hs]
                v_before = v_prev[:, hs] if sb == 0 else vb_ref[before, hs]
                kd = jnp.concatenate([k_before, kr_ref[rs, hs]], axis=0)
                vd = jnp.concatenate([v_before, vb_ref[rs, hs]], axis=0)
                sink = jnp.concatenate([sk_ref[2 * h], sk_ref[2 * h + 1]], axis=0)
                _, p, _ = _swa_probs(qr_ref[rs, qs], kd, sink, n * sub + sb, lo)
                at_ref[rs, qs] = _from_heads(_dot(p, vd), lo)

            for j in range(4):
                sl = slice(j * 128, (j + 1) * 128)
                m2 = _dot(w2_ref[j], gvo_ref[rs, sl])
                mixed = jnp.where(lo, m2[:BLK], m2[BLK:]) + bsl_ref[j]
                gm_ref[rs, sl] = gu_ref[rs, sl] * mixed
            a, gm = at_ref[rs, :], gm_ref[rs, :]
            y_ref[rs, :512] = (a * _rms(a) * gao_ref[...]).astype(y_ref.dtype)
            y_ref[rs, 512:] = (gm * _rms(gm) * ggo_ref[...]).astype(y_ref.dtype)
        k_prev[...] = kr_ref[(sub - 1) * BLK:, :]
        v_prev[...] = vb_ref[(sub - 1) * BLK:, :]

    row = lambda w: BS((sub * BLK, w), lambda n: (n, 0))
    const = lambda *shape: BS(shape, lambda n: (0,) * len(shape))
    return _pcall(body, name="mixer_core_fwd", grid=(S // (sub * BLK),),
                  in_specs=[row(IN_COLS_DUP), row(128), row(128), const(1, 128), const(1, 128), const(1, 512),
                            const(128, 128), const(4, 2 * BLK, 1), const(1, 512), const(4, 2 * BLK, BLK),
                            const(4, BLK, 128), const(1, 512)],
                  out_specs=[row(512), row(256), row(256), row(512), row(512), row(512), row(512), row(1024)],
                  out_shape=[SDS((S, 512), MXU_DTYPE), SDS((S, 256), MXU_DTYPE), SDS((S, 256), MXU_DTYPE),
                             SDS((S, 512), F32), SDS((S, 512), MXU_DTYPE), SDS((S, 512), F32), SDS((S, 512), F32),
                             SDS((S, 1024), MXU_DTYPE)],
                  scratch=[pltpu.VMEM((BLK, 256), MXU_DTYPE), pltpu.VMEM((BLK, 256), MXU_DTYPE)])(
        proj, cos, sin, gq, gk, gvn, bmat, sinkcol, gao, w2, bsl, ggo)


def mem_pre(kv, gxk):
    def body(kv_ref, g_ref, kn_ref, vb_ref):
        for h in range(XA_HEADS):
            sl = slice(h * XA_DH, (h + 1) * XA_DH)
            k = kv_ref[:, sl]
            kn_ref[:, sl] = (k * _rms(k) * g_ref[...]).astype(kn_ref.dtype)
        vb_ref[...] = kv_ref[:, 1024:2048].astype(vb_ref.dtype)

    full = lambda r, w: BS((r, w), lambda i: (0, 0))
    return _pcall(body, name="mem_pre", grid=(1,), in_specs=[full(MEM_LEN, 2048), full(1, XA_DH)],
                  out_specs=[full(MEM_LEN, 1024), full(MEM_LEN, 1024)],
                  out_shape=[SDS((MEM_LEN, 1024), MXU_DTYPE), SDS((MEM_LEN, 1024), MXU_DTYPE)])(kv, gxk)


def _xa_probs(qh, g, kn_h):
    r = _rms(qh)
    qn = qh * r * g
    s = _dot(qn, kn_h, NT) * (1.0 / math.sqrt(XA_DH))
    p = jnp.exp(s - jnp.max(s, axis=1, keepdims=True))
    return r, qn, p * (1.0 / jnp.sum(p, axis=1, keepdims=True))


def xattn_block_fwd(x1, g, wq, kn, vb, gxq, wo):
    S, D = x1.shape
    tm = _tile(S, (512, 256))

    def body(x_ref, g_ref, wq_ref, kn_ref, vb_ref, gxq_ref, wo_ref, h_ref, q_ref, o_ref, x2_ref):
        xv = x_ref[...]
        h_ref[...] = (xv * _rms(xv) * g_ref[...]).astype(h_ref.dtype)
        q_ref[...] = _dot(h_ref[...], wq_ref[...])
        for h in range(XA_HEADS):
            sl = slice(h * XA_DH, (h + 1) * XA_DH)
            _, _, p = _xa_probs(q_ref[:, sl], gxq_ref[...], kn_ref[:, sl])
            o_ref[:, sl] = _dot(p, vb_ref[:, sl]).astype(o_ref.dtype)
        x2_ref[...] = _dot(o_ref[...], wo_ref[...]) + xv

    row = BS((tm, D), lambda i: (i, 0))
    full = lambda r, w: BS((r, w), lambda i: (0, 0))
    return _pcall(body, name="xattn_block_fwd", grid=(S // tm,),
                  in_specs=[row, full(1, D), _resident(wq.shape), full(MEM_LEN, D), full(MEM_LEN, D), full(1, XA_DH),
                            _resident(wo.shape)],
                  out_specs=[row, row, row, row],
                  out_shape=[SDS((S, D), MXU_DTYPE), SDS((S, D), F32), SDS((S, D), MXU_DTYPE), SDS((S, D), F32)])(
        x1, g, wq, kn, vb, gxq, wo)


CONV_COLS = 1408


def _conv_taps(a_ref, halo_ref, w_ref, b_ref, cols, first_tile):
    a = a_ref[:, cols]
    row = lax.broadcasted_iota(jnp.int32, (8, a.shape[1]), 0)
    h6 = jnp.where(first_tile, 0.0, halo_ref[6:7, cols])
    h7 = jnp.where(first_tile, 0.0, halo_ref[7:8, cols])
    r1, r2 = pltpu.roll(a, 1, 0), pltpu.roll(a, 2, 0)
    a1 = jnp.concatenate([jnp.where(row == 0, h7, r1[0:8]), r1[8:]], axis=0)
    a2 = jnp.concatenate([jnp.where(row == 0, h6, jnp.where(row == 1, h7, r2[0:8])), r2[8:]], axis=0)
    c = w_ref[2:3, cols] * a + w_ref[1:2, cols] * a1 + w_ref[0:1, cols] * a2 + b_ref[:, cols]
    return c, (a2, a1, a)


def _conv_specs(tm):
    halo_blocks = tm // 8
    return [BS((tm, D_FF), lambda i: (i, 0)), BS((tm, D_FF), lambda i: (i, 1)),
            BS((8, D_FF), lambda i: (jnp.maximum(i * halo_blocks - 1, 0), 0)),
            BS((8, D_FF), lambda i: (jnp.maximum(i * halo_blocks - 1, 0), 1)),
            BS((3, D_FF), lambda i: (0, 0)), BS((3, D_FF), lambda i: (0, 1)),
            BS((1, D_FF), lambda i: (0, 0)), BS((1, D_FF), lambda i: (0, 1))]


def ffn_fwd_loss(x2, g, w_up3, cw, cb, w_down, target):
    S, D = x2.shape
    Q, _, C = w_up3.shape
    tm = _tile(S, (256,))

    def body(x_ref, g_ref, wu_ref, cw_ref, cb_ref, wd_ref, t_ref, h_ref, a_ref, f_ref, d_ref, l_ref, tail):
        first_tile = pl.program_id(0) == 0

        @pl.when(first_tile)
        def _():
            l_ref[...] = jnp.zeros_like(l_ref)
            tail[...] = jnp.zeros_like(tail)

        xv = x_ref[...]
        h_ref[...] = (xv * _rms(xv) * g_ref[...]).astype(h_ref.dtype)
        for q in range(Q):
            a_ref[:, q * C:(q + 1) * C] = _dot(h_ref[...], wu_ref[q])
        for c0 in range(0, D_FF, CONV_COLS):
            cols, ucols = slice(c0, c0 + CONV_COLS), slice(D_FF + c0, D_FF + c0 + CONV_COLS)
            cg, _ = _conv_taps(a_ref, tail, cw_ref, cb_ref, cols, first_tile)
            cu, _ = _conv_taps(a_ref, tail, cw_ref, cb_ref, ucols, first_tile)
            f_ref[:, cols] = (_gelu(cg) * cu).astype(f_ref.dtype)
        tail[...] = a_ref[tm - 8:tm, :]
        e = _dot(f_ref[...], wd_ref[...]) + xv - t_ref[...]
        d_ref[...] = e * (1.0 / D)
        l_ref[...] += jnp.sum(e * e) * (0.5 / D)

    row = lambda w: BS((tm, w), lambda i: (i, 0))
    const = lambda r, w: BS((r, w), lambda i: (0, 0))
    return _pcall(body, name="ffn_fwd_loss", grid=(S // tm,),
                  in_specs=[row(D), const(1, D), _resident(w_up3.shape), const(3, 2 * D_FF), const(1, 2 * D_FF),
                            _resident(w_down.shape), row(D)],
                  out_specs=[row(D), row(2 * D_FF), row(D_FF), row(D), const(8, 128)],
                  out_shape=[SDS((S, D), MXU_DTYPE), SDS((S, 2 * D_FF), F32), SDS((S, D_FF), MXU_DTYPE),
                             SDS((S, D), F32), SDS((8, 128), F32)],
                  scratch=[pltpu.VMEM((8, 2 * D_FF), F32)])(x2, g, w_up3, cw, cb, w_down, target)


def convgate_bwd(a, dx3, w3, cw, cb, after=None):
    S = a.shape[0]
    tm = _tile(S, (256,))

    def body(ag_ref, au_ref, hg_ref, hu_ref, wg_ref, wu_ref, bg_ref, bu_ref, dx_ref, wd_ref, dc_ref, gw_ref, df_ref):
        first_tile = pl.program_id(0) == 0

        @pl.when(first_tile)
        def _():
            gw_ref[...] = jnp.zeros_like(gw_ref)

        df_ref[...] = _nt_chunks(dx_ref, wd_ref)
        for c0 in range(0, D_FF, CONV_COLS):
            cols, ucols = slice(c0, c0 + CONV_COLS), slice(D_FF + c0, D_FF + c0 + CONV_COLS)
            cg, g_taps = _conv_taps(ag_ref, hg_ref, wg_ref, bg_ref, cols, first_tile)
            cu, u_taps = _conv_taps(au_ref, hu_ref, wu_ref, bu_ref, cols, first_tile)
            df_v = df_ref[:, cols]
            gate, gate_grad = _gelu_and_grad(cg)
            dcg = df_v * cu * gate_grad
            dcu = df_v * gate
            dc_ref[:, cols] = dcg
            dc_ref[:, ucols] = dcu
            for col, dcv, taps in ((cols, dcg, g_taps), (ucols, dcu, u_taps)):
                for j in range(3):
                    gw_ref[j:j + 1, col] += jnp.sum(dcv * taps[j], axis=0, keepdims=True)
                gw_ref[3:4, col] += jnp.sum(dcv, axis=0, keepdims=True)

    return _pcall(body, name="convgate_bwd", grid=(S // tm,), after=after,
                  in_specs=_conv_specs(tm) + [BS((tm, dx3.shape[1]), lambda i: (i, 0)), _resident(w3.shape)],
                  out_specs=[BS((tm, 2 * D_FF), lambda i: (i, 0)), BS((8, 2 * D_FF), lambda i: (0, 0))],
                  out_shape=[SDS((S, 2 * D_FF), F32), SDS((8, 2 * D_FF), F32)],
                  scratch=[pltpu.VMEM((tm, D_FF), F32)])(a, a, a, a, cw, cw, cb, cb, dx3, w3)


def conv_transpose_rms_bwd(dc, cw, w3, x, g, dres):
    S, C = dc.shape
    Q, N, Kc = w3.shape
    tm = _tile(S, (256,))
    nt = S // tm
    halo_blocks = tm // 8

    def body(dc_ref, halo_ref, cw_ref, w_ref, x_ref, g_ref, dr_ref, da_ref, dx_ref, dg_ref):
        @pl.when(pl.program_id(0) == 0)
        def _():
            dg_ref[...] = jnp.zeros_like(dg_ref)

        last_tile = pl.program_id(0) == nt - 1
        row = lax.broadcasted_iota(jnp.int32, (8, CONV_COLS), 0)
        for c0 in range(0, C, CONV_COLS):
            cols = slice(c0, c0 + CONV_COLS)
            h0 = jnp.where(last_tile, 0.0, halo_ref[0:1, cols])
            h1 = jnp.where(last_tile, 0.0, halo_ref[1:2, cols])
            dc_v = dc_ref[:, cols]
            r1, r2 = pltpu.roll(dc_v, tm - 1, 0), pltpu.roll(dc_v, tm - 2, 0)
            n1 = jnp.concatenate([r1[:tm - 8], jnp.where(row == 7, h0, r1[tm - 8:])], axis=0)
            n2 = jnp.concatenate([r2[:tm - 8], jnp.where(row == 7, h1, jnp.where(row == 6, h0, r2[tm - 8:]))], axis=0)
            da_ref[:, cols] = (cw_ref[2:3, cols] * dc_v + cw_ref[1:2, cols] * n1
                               + cw_ref[0:1, cols] * n2).astype(da_ref.dtype)
        xv = x_ref[...]
        dx, dgc = _rms_bwd(_nt_chunks(da_ref, w_ref), xv, g_ref[...], _rms(xv))
        dx_ref[...] = dr_ref[...] + dx
        _acc_rows(dg_ref, 0, dgc)

    row_n = BS((tm, N), lambda i: (i, 0))
    return _pcall(body, name="conv_transpose_rms_bwd", grid=(nt,),
                  in_specs=[BS((tm, C), lambda i: (i, 0)),
                            BS((8, C), lambda i: (jnp.minimum((i + 1) * halo_blocks, S // 8 - 1), 0)),
                            BS((3, C), lambda i: (0, 0)), _resident((Q, N, Kc)), row_n, BS((1, N), lambda i: (0, 0)),
                            row_n],
                  out_specs=[BS((tm, C), lambda i: (i, 0)), row_n, BS((8, N), lambda i: (0, 0))],
                  out_shape=[SDS((S, C), MXU_DTYPE), SDS((S, N), F32), SDS((8, N), F32)])(dc, dc, cw, w3, x, g, dres)


def xattn_block_bwd(dx2, wo3, qx, kn, vb, gxq, wq3, x1, g, after=None):
    S, D = qx.shape
    tm = _tile(S, (512, 256))

    def body(dx2_ref, wo_ref, q_ref, kn_ref, vb_ref, gxq_ref, wq_ref, x_ref, g_ref,
             dq_ref, dx_ref, dkn_ref, dv_ref, dgq_ref, dg_ref):
        @pl.when(pl.program_id(0) == 0)
        def _():
            for ref in (dkn_ref, dv_ref, dgq_ref, dg_ref):
                ref[...] = jnp.zeros_like(ref)

        gq = gxq_ref[...]
        do_all = _nt_chunks(dx2_ref, wo_ref)
        for h in range(XA_HEADS):
            sl = slice(h * XA_DH, (h + 1) * XA_DH)
            qh, do = q_ref[:, sl], do_all[:, sl]
            r, qn, p = _xa_probs(qh, gq, kn_ref[:, sl])
            dp = _dot(do, vb_ref[:, sl], NT)
            ds = p * (dp - jnp.sum(dp * p, axis=1, keepdims=True)) * (1.0 / math.sqrt(XA_DH))
            dqn = _dot(ds, kn_ref[:, sl])
            dkn_ref[:, sl] += _dot(ds, qn, TN)
            dv_ref[:, sl] += _dot(p, do, TN)
            dqh, dgc = _rms_bwd(dqn, qh, gq, r)
            dq_ref[:, sl] = dqh.astype(dq_ref.dtype)
            _acc_rows(dgq_ref, 0, dgc)
        xv = x_ref[...]
        dx, dgc = _rms_bwd(_nt_chunks(dq_ref, wq_ref), xv, g_ref[...], _rms(xv))
        dx_ref[...] = dx2_ref[...] + dx
        _acc_rows(dg_ref, 0, dgc)

    row = BS((tm, D), lambda i: (i, 0))
    full = lambda r, w: BS((r, w), lambda i: (0, 0))
    return _pcall(body, name="xattn_block_bwd", grid=(S // tm,), after=after,
                  in_specs=[row, _resident(wo3.shape), row, full(MEM_LEN, D), full(MEM_LEN, D), full(1, XA_DH),
                            _resident(wq3.shape), row, full(1, D)],
                  out_specs=[row, row, full(MEM_LEN, D), full(MEM_LEN, D), full(8, XA_DH), full(8, D)],
                  out_shape=[SDS((S, D), MXU_DTYPE), SDS((S, D), F32), SDS((MEM_LEN, D), F32), SDS((MEM_LEN, D), F32),
                             SDS((8, XA_DH), F32), SDS((8, D), F32)])(dx2, wo3, qx, kn, vb, gxq, wq3, x1, g)


def mem_bwd(kv, dkn, dvb, gxk, after=None):
    def body(kv_ref, dkn_ref, dv_ref, g_ref, dkv_ref, dg_ref):
        dg_ref[...] = jnp.zeros_like(dg_ref)
        for h in range(XA_HEADS):
            sl = slice(h * XA_DH, (h + 1) * XA_DH)
            k = kv_ref[:, sl]
            dk, dgc = _rms_bwd(dkn_ref[:, sl], k, g_ref[...], _rms(k))
            dkv_ref[:, sl] = dk.astype(dkv_ref.dtype)
            _acc_rows(dg_ref, 0, dgc)
        dkv_ref[:, 1024:2048] = dv_ref[...].astype(dkv_ref.dtype)

    full = lambda r, w: BS((r, w), lambda i: (0, 0))
    return _pcall(body, name="mem_bwd", grid=(1,), after=after,
                  in_specs=[full(MEM_LEN, 2048), full(MEM_LEN, 1024), full(MEM_LEN, 1024), full(1, XA_DH)],
                  out_specs=[full(MEM_LEN, 2048), full(8, XA_DH)],
                  out_shape=[SDS((MEM_LEN, 2048), MXU_DTYPE), SDS((8, XA_DH), F32)])(kv, dkn, dvb, gxk)


def _norm_rope_bwd(slabs, douts, g, bm, cos_v, sin_v, first):
    dqns = [d * cos_v + _half_swap(d * sin_v, first) for d in douts]
    rs = [lax.rsqrt(ms + EPS) for ms in _head_means([x * x for x in slabs], bm)]
    projs = _head_means([dqn * g * x for dqn, x in zip(dqns, slabs)], bm)
    dxs = [r * (dqn * g) - x * (r * r * r) * pr for x, dqn, r, pr in zip(slabs, dqns, rs, projs)]
    return dxs, [dqn * x * r for x, dqn, r in zip(slabs, dqns, rs)]


def mixer_core_bwd(proj, cos, sin, gq, gk, gvg, bmat, qr, kr, vb, sinkcol, dattn, dgm, gvn, gu, w2, w2t, bsl,
                   after=None):
    S = qr.shape[0]
    nb = S // BLK

    def body(p_ref, c_ref, s_ref, gq_ref, gk_ref, gvg_ref, b_ref, q_ref, kc_ref, kp_ref, vc_ref, vp_ref, sk_ref,
             do_ref, dgm_ref, gvn_ref, gu_ref, w2_ref, w2t_ref, bsl_ref,
             dp_ref, dsk_ref, dws_ref, dbl_ref, dgq_ref, dgk_ref, dgv_ref,
             carry_k, carry_v, done_k, done_v, dq_keep, dgu_keep, dgvn_keep):
        n = pl.program_id(0)

        @pl.when(n == 0)
        def _():
            for ref in (dsk_ref, dws_ref, dbl_ref, dgq_ref, dgk_ref, dgv_ref, carry_k, carry_v, dq_keep, dgu_keep,
                        dgvn_keep):
                ref[...] = jnp.zeros_like(ref)

        live = (n < nb).astype(F32)
        cos_v, sin_v, bm = c_ref[...], s_ref[...], b_ref[...]
        first = (_lane((BLK, 128)) & 63) < 32
        lo = _lane((BLK, 128)) < 64

        dxs, dgs = _norm_rope_bwd([p_ref[:, s * 128:(s + 1) * 128] for s in range(4)],
                                  [dq_keep[:, s * 128:(s + 1) * 128] for s in range(4)], gq_ref[...], bm,
                                  cos_v, sin_v, first)
        for s, (dx, dg) in enumerate(zip(dxs, dgs)):
            dp_ref[:, s * 128:(s + 1) * 128] = dx.astype(dp_ref.dtype)
            _acc_rows(dgq_ref, 0, dg)
        dp_ref[:, 1024:1536] = (dgu_keep[...] * _gelu_grad(p_ref[:, 1024:1536])).astype(dp_ref.dtype)
        gv, gv_grad = _gelu_and_grad(p_ref[:, 1536:2048])
        dgv, dgc = _rms_bwd(dgvn_keep[...], gv, gvg_ref[...], _rms(gv))
        dp_ref[:, 1536:2048] = (dgv * gv_grad).astype(dp_ref.dtype)
        _acc_rows(dgv_ref, 0, dgc)

        for h in range(2):
            hs, qs = slice(h * 128, (h + 1) * 128), slice(h * 256, (h + 1) * 256)
            kd = jnp.concatenate([kp_ref[:, hs], kc_ref[:, hs]], axis=0)
            vd = jnp.concatenate([vp_ref[:, hs], vc_ref[:, hs]], axis=0)
            sink = jnp.concatenate([sk_ref[2 * h], sk_ref[2 * h + 1]], axis=0)
            qp, p, psink = _swa_probs(q_ref[:, qs], kd, sink, n, lo)
            dop = _by_head(do_ref[:, qs], lo)
            dp = _dot(dop, vd, NT)
            delta = jnp.sum(dp * p, axis=1, keepdims=True)
            ds = p * (dp - delta) * (1.0 / math.sqrt(HEAD_DIM))
            dsink = -psink * delta * live
            dsk_ref[2 * h] += dsink[:2 * BLK]
            dsk_ref[2 * h + 1] += dsink[2 * BLK:]
            dq_keep[:, qs] = _from_heads(_dot(ds, kd), lo)
            dkd = _dot(ds, qp, TN)
            dvd = _dot(p, dop, TN)
            done_k[:, hs] = carry_k[:, hs] + live * dkd[:BLK]
            done_v[:, hs] = carry_v[:, hs] + live * dvd[:BLK]
            carry_k[:, hs] = dkd[BLK:]
            carry_v[:, hs] = dvd[BLK:]
        for j in range(4):
            sl = slice(j * 128, (j + 1) * 128)
            gvn_s = gvn_ref[:, sl]
            m2 = _dot(w2_ref[j], gvn_s)
            mixed = jnp.where(lo, m2[:BLK], m2[BLK:]) + bsl_ref[j]
            dgm_s = dgm_ref[:, sl]
            dgu_keep[:, sl] = dgm_s * mixed
            dmx = dgm_s * gu_ref[:, sl] * live
            d2 = _dot(w2t_ref[j], dmx)
            dgvn_keep[:, sl] = jnp.where(lo, d2[:BLK], d2[BLK:])
            z = jnp.zeros_like(dmx)
            dws_ref[2 * j] += _dot(jnp.where(lo, dmx, z), gvn_s, NT)
            dws_ref[2 * j + 1] += _dot(jnp.where(lo, z, dmx), gvn_s, NT)
            dbl_ref[j] += dmx

        dxs, dgs = _norm_rope_bwd([p_ref[:, 512 + s * 128:640 + s * 128] for s in range(2)],
                                  [done_k[:, s * 128:(s + 1) * 128] for s in range(2)], gk_ref[...], bm,
                                  cos_v, sin_v, first)
        for s, (dx, dg) in enumerate(zip(dxs, dgs)):
            dp_ref[:, 512 + s * 128:640 + s * 128] = dx.astype(dp_ref.dtype)
            _acc_rows(dgk_ref, 0, dg)
        dp_ref[:, 768:1024] = done_v[...].astype(dp_ref.dtype)

    last = nb - 1
    cur = lambda w: BS((BLK, w), lambda n: (jnp.minimum(n, last), 0))
    prev = lambda w: BS((BLK, w), lambda n: (jnp.clip(n - 1, 0, last), 0))
    done = lambda w: BS((BLK, w), lambda n: (jnp.maximum(n - 1, 0), 0))
    const = lambda *shape: BS(shape, lambda n: (0,) * len(shape))
    return _pcall(body, name="mixer_core_bwd", grid=(nb + 1,), after=after,
                  in_specs=[done(IN_COLS_DUP), done(128), done(128), const(1, 128), const(1, 128), const(1, 512),
                            const(128, 128), cur(512), cur(256), prev(256), cur(256), prev(256),
                            const(4, 2 * BLK, 1), cur(512), cur(512), cur(512), cur(512), const(4, 2 * BLK, BLK),
                            const(4, 2 * BLK, BLK), const(4, BLK, 128)],
                  out_specs=[done(IN_COLS_DUP), const(4, 2 * BLK, 1), const(8, BLK, BLK), const(4, BLK, 128),
                             const(8, 128), const(8, 128), const(8, 512)],
                  out_shape=[SDS((S, IN_COLS_DUP), MXU_DTYPE), SDS((4, 2 * BLK, 1), F32), SDS((8, BLK, BLK), F32),
                             SDS((4, BLK, 128), F32), SDS((8, 128), F32), SDS((8, 128), F32), SDS((8, 512), F32)],
                  scratch=[pltpu.VMEM((BLK, 256), F32)] * 4 + [pltpu.VMEM((BLK, 512), F32)] * 3)(
        proj, cos, sin, gq, gk, gvg, bmat, qr, kr, kr, vb, vb, sinkcol, dattn, dgm, gvn, gu, w2, w2t, bsl)


BIG = (("w_in", (1024, 448), True), ("w_out", (256, 1024), False), ("xa_wq", (256, 1024), False),
       ("xa_wkv", (1024, 512), True), ("xa_wo", (256, 1024), False), ("ffn_up", (1024, 1408), True),
       ("ffn_down", (704, 1024), False))
BIG_NAMES = tuple(n for n, _, _ in BIG)
SMALL_VECS = (("mix_norm", 1024), ("q_norm", 64), ("k_norm", 64), ("attn_sinks", 8), ("gmlp_v_norm", 512),
              ("attn_out_norm", 512), ("gmlp_out_norm", 512), ("xa_norm", 1024), ("mem_norm", 1024),
              ("xa_q_norm", 256), ("xa_k_norm", 256), ("ffn_norm", 1024), ("ffn_conv_b", 5632))
SMALL = tuple(n for n, _ in SMALL_VECS) + ("gmlp_bs", "gmlp_ws", "ffn_conv")
WEIGHTS = ("mix_norm", "w_in", "q_norm", "k_norm", "attn_sinks", "gmlp_v_norm", "gmlp_ws", "gmlp_bs",
           "attn_out_norm", "gmlp_out_norm", "w_out", "xa_norm", "mem_norm", "xa_wq", "xa_wkv", "xa_q_norm",
           "xa_k_norm", "xa_wo", "ffn_norm", "ffn_up", "ffn_conv", "ffn_conv_b", "ffn_down")
CONV_SHARD = (3, 1408)
CONV_LANE_ROWS = CONV_SHARD[1] // 128
CONV_CHIP_ROWS = 40


def _small_rows():
    rows, r = {}, 0
    for n, length in SMALL_VECS:
        rows[n] = r
        r += -(-length // 128)
    r += -r % 8
    rows["gmlp_bs"] = r
    r += 8
    rows["gmlp_ws"] = r
    r += 8 * BLK
    rows["ffn_conv"] = r
    r += N_CHIPS * CONV_CHIP_ROWS
    return rows, r


SMALL_ROW, SMALL_ROWS = _small_rows()


def pack_small(dg_mix, dgq, dgk, dsk, dg_gvn, dg_y, dg_xa, dg_mem, dg_xq, dg_xk, dg_ffn, gcw, dbl, dws):
    def body(mix_ref, q_ref, k_ref, sk_ref, gvn_ref, y_ref, xa_ref, mem_ref, xq_ref, xk_ref, ffn_ref, cw_ref,
             dbl_ref, dws_ref, o_ref):
        o_ref[...] = jnp.zeros_like(o_ref)
        lane = _lane((1, 128))

        def put(name, src_ref, row, lane0, length):
            for k in range(length // 128):
                o_ref[SMALL_ROW[name] + k:SMALL_ROW[name] + k + 1, :] = src_ref[row:row + 1, lane0 + k * 128:lane0 + (k + 1) * 128]

        put("mix_norm", mix_ref, 0, 0, 1024)
        for name, ref in (("q_norm", q_ref), ("k_norm", k_ref)):
            v = ref[0:1, :]
            o_ref[SMALL_ROW[name]:SMALL_ROW[name] + 1, :] = jnp.where(lane < HEAD_DIM, v + pltpu.roll(v, 64, 1), 0.0)
        sinks = jnp.zeros((1, 128), F32)
        for s in range(4):
            col = sk_ref[s]
            sinks = sinks + jnp.where(lane == 2 * s, jnp.sum(col[:BLK]), 0.0) + jnp.where(lane == 2 * s + 1, jnp.sum(col[BLK:]), 0.0)
        o_ref[SMALL_ROW["attn_sinks"]:SMALL_ROW["attn_sinks"] + 1, :] = sinks
        put("gmlp_v_norm", gvn_ref, 0, 0, 512)
        put("attn_out_norm", y_ref, 0, 0, 512)
        put("gmlp_out_norm", y_ref, 0, 512, 512)
        put("xa_norm", xa_ref, 0, 0, 1024)
        put("mem_norm", mem_ref, 0, 0, 1024)
        put("xa_q_norm", xq_ref, 0, 0, 256)
        put("xa_k_norm", xk_ref, 0, 0, 256)
        put("ffn_norm", ffn_ref, 0, 0, 1024)
        put("ffn_conv_b", cw_ref, 3, 0, 2 * D_FF)
        r8 = lax.broadcasted_iota(jnp.int32, (8, 128), 0)
        l8 = _lane((8, 128))
        bs = jnp.zeros((8, BLK), F32)
        for j in range(4):
            sel = (((r8 == 2 * j) & (l8 < 64)) | ((r8 == 2 * j + 1) & (l8 >= 64))).astype(F32).astype(BF16)
            xj = dbl_ref[j]
            hi = xj.astype(BF16)
            lo = (xj - hi.astype(F32)).astype(BF16)
            bs = bs + lax.dot_general(sel, hi, NT, preferred_element_type=F32) + lax.dot_general(sel, lo, NT, preferred_element_type=F32)
        o_ref[SMALL_ROW["gmlp_bs"]:SMALL_ROW["gmlp_bs"] + 8, :] = bs
        causal = lax.broadcasted_iota(jnp.int32, (BLK, BLK), 0) >= lax.broadcasted_iota(jnp.int32, (BLK, BLK), 1)
        for h in range(8):
            r0 = SMALL_ROW["gmlp_ws"] + h * BLK
            o_ref[r0:r0 + BLK, :] = jnp.where(causal, dws_ref[h], 0.0)
        for q in range(N_CHIPS):
            for j in range(3):
                for k in range(CONV_LANE_ROWS):
                    r0 = SMALL_ROW["ffn_conv"] + q * CONV_CHIP_ROWS + j * CONV_LANE_ROWS + k
                    l0 = (q * CONV_LANE_ROWS + k) * 128
                    o_ref[r0:r0 + 1, :] = cw_ref[j:j + 1, l0:l0 + 128]

    args = (dg_mix, dgq, dgk, dsk, dg_gvn, dg_y, dg_xa, dg_mem, dg_xq, dg_xk, dg_ffn, gcw, dbl, dws)
    full = lambda a: BS(a.shape, lambda i, nd=a.ndim: (0,) * nd)
    return _pcall(body, name="pack_small", grid=(1,), in_specs=[full(a) for a in args],
                  out_specs=BS((SMALL_ROWS, 128), lambda i: (0, 0)), out_shape=SDS((SMALL_ROWS, 128), F32))(*args)


def _adam(w, g, m, v):
    mn = ADAM_B1 * m + (1.0 - ADAM_B1) * g
    vn = ADAM_B2 * v + (1.0 - ADAM_B2) * (g * g)
    m_hat = mn / (1.0 - ADAM_B1 ** ADAM_STEP)
    v_hat = vn / (1.0 - ADAM_B2 ** ADAM_STEP)
    return -ADAM_LR * (m_hat / (jnp.sqrt(v_hat) + ADAM_EPS) + ADAM_WD * w), mn, vn


def adamw_small(gsum, w, m, v, chipvec):
    n = len(SMALL)

    def body(chip_ref, g_ref, *refs):
        w_refs, m_refs, v_refs = refs[:n], refs[n:2 * n], refs[2 * n:3 * n]
        outs = refs[3 * n:]
        go, do, mo, vo = outs[:n], outs[n:2 * n], outs[2 * n:3 * n], outs[3 * n:]

        def update(i, idx, g):
            d, mn, vn = _adam(w_refs[i][idx], g, m_refs[i][idx], v_refs[i][idx])
            go[i][idx] = g
            do[i][idx] = d
            mo[i][idx] = mn
            vo[i][idx] = vn

        for i, (name, length) in enumerate(SMALL_VECS):
            for k in range(-(-length // 128)):
                wd = min(128, length - k * 128)
                r = SMALL_ROW[name] + k
                update(i, (slice(0, 1), slice(k * 128, k * 128 + wd)), g_ref[r:r + 1, 0:wd])
        i_bs, i_ws, i_cv = len(SMALL_VECS), len(SMALL_VECS) + 1, len(SMALL_VECS) + 2
        update(i_bs, (0,), g_ref[SMALL_ROW["gmlp_bs"]:SMALL_ROW["gmlp_bs"] + 8, :])
        for h in range(8):
            r0 = SMALL_ROW["gmlp_ws"] + h * BLK
            update(i_ws, (0, h), g_ref[r0:r0 + BLK, :])
        mine = g_ref[pl.ds(pl.multiple_of(SMALL_ROW["ffn_conv"] + chip_ref[0] * CONV_CHIP_ROWS, 8), CONV_CHIP_ROWS), :]
        for j in range(3):
            for k in range(CONV_LANE_ROWS):
                r = j * CONV_LANE_ROWS + k
                update(i_cv, (0, slice(j, j + 1), slice(k * 128, (k + 1) * 128)), mine[r:r + 1, :])

    nat = [w[nm] for nm in SMALL]
    full = lambda a: BS(a.shape, lambda i, c, nd=a.ndim: (0,) * nd)
    outs = _pcall(body, name="adamw_small", grid=(1,), prefetch=1,
                  in_specs=[BS((SMALL_ROWS, 128), lambda i, c: (0, 0))] + [full(a) for a in nat] * 3,
                  out_specs=[full(a) for a in nat] * 4, out_shape=[SDS(a.shape, F32) for a in nat] * 4)(
        chipvec, gsum, *nat, *[m[nm] for nm in SMALL], *[v[nm] for nm in SMALL])
    return outs[:n], outs[n:2 * n], outs[2 * n:3 * n], outs[3 * n:]


def adamw_matrix(w, m, v, g_own, g_other, cvec, *, name):
    _, r, c = w.shape
    half = r // 2
    tr = _tile(half, (256, 176, 128))
    T = half // tr

    def body(c_ref, w_ref, m_ref, v_ref, own_ref, oth_ref, g_ref, d_ref, mo_ref, vo_ref):
        g = jnp.where(pl.program_id(0) == c_ref[0], own_ref[...], oth_ref[...])
        d, mn, vn = _adam(w_ref[...], g, m_ref[...], v_ref[...])
        g_ref[...] = g
        d_ref[...] = d
        mo_ref[...] = mn
        vo_ref[...] = vn

    nat = BS((None, tr, c), lambda hf, t, cr: (0, hf * T + t, 0))
    hlf = BS((tr, c), lambda hf, t, cr: (t, 0))
    return _pcall(body, name=name, grid=(2, T), prefetch=1, in_specs=[nat, nat, nat, hlf, hlf], out_specs=[nat] * 4,
                  out_shape=[SDS(w.shape, F32)] * 4)(cvec, w, m, v, g_own, g_other)


def _place():
    return lax.axis_index("x"), lax.axis_index("y"), lax.axis_index("c")


def _other_chips(x, y):
    return [(1 - x, y), (x, 1 - y), (1 - x, 1 - y)]


def _rows_of_core(c, half):
    return pl.ds(pl.multiple_of(c * half, 16), half)


def _rcopy(src, dst, sems, k, to):
    return pltpu.make_async_remote_copy(src_ref=src, dst_ref=dst, send_sem=sems[0].at[k], recv_sem=sems[1].at[k],
                                        device_id=to, device_id_type=MESH)


def cast_shards(shards, conv, chipvec):
    n = len(shards)

    def body(chip_ref, *refs):
        for i_ref, o_ref in zip(refs[:n + 1], refs[n + 1:]):
            o_ref[...] = i_ref[...].astype(o_ref.dtype)

    in_specs = [BS((s.shape[0] // 4, s.shape[1]), lambda i, p: (i, 0)) for s in shards]
    in_specs.append(BS(conv.shape, lambda i, p: (0, 0)))
    out_specs = [BS((None, s.shape[0] // 4, s.shape[1]), lambda i, p: (p[0], i, 0)) for s in shards]
    out_specs.append(BS((None,) + conv.shape, lambda i, p: (p[0], 0, 0)))
    out_shape = [SDS((N_CHIPS,) + s.shape, MXU_DTYPE) for s in shards] + [SDS((N_CHIPS,) + conv.shape, F32)]
    return _pcall(body, name="cast_shards", grid=(4,), prefetch=1, in_specs=in_specs, out_specs=out_specs,
                  out_shape=out_shape)(chipvec, *shards, conv)


HBM = pl.BlockSpec(memory_space=pltpu.HBM)
SEM = pl.BlockSpec(memory_space=pltpu.SEMAPHORE)
DATAFLOW = pltpu.SideEffectType.DATAFLOW_SIDE_EFFECTING
VMEM_WHOLE = pl.BlockSpec(memory_space=pltpu.VMEM)
TOKEN = jax.ShapeDtypeStruct((8, 128), jnp.float32)


def _gather_copies(bufs, send_sems, recv_sems, outgoing):
    x, y, c = _place()
    p = 2 * x + y
    cps = []
    for i, o in enumerate(bufs):
        for j, (cx, cy) in enumerate(_other_chips(x, y)):
            slot = o.at[p] if outgoing else o.at[2 * cx + cy]
            cps.append(_rcopy(slot, slot, (send_sems, recv_sems), 3 * i + j, (cx, cy, c)))
    return cps


def gather_start(slots, after):
    n = len(slots)

    def body(*refs):
        send_sems, recv_sems, thru, token = refs[n + 1], refs[n + 2], refs[n + 3:2 * n + 3], refs[2 * n + 3]
        for cp in _gather_copies(thru, send_sems, recv_sems, True):
            cp.start()
        token[...] = jnp.zeros_like(token)

    hbm = [pltpu.with_memory_space_constraint(s, pltpu.HBM) for s in slots]
    outs = pl.pallas_call(
        body, name="gather_start_%d" % n,
        out_shape=[pltpu.SemaphoreType.DMA((3 * n,)), pltpu.SemaphoreType.DMA((3 * n,))]
        + [pltpu.HBM(s.shape, s.dtype) for s in slots] + [TOKEN],
        in_specs=[HBM] * n + [ANY], out_specs=[SEM, SEM] + [HBM] * n + [VMEM_WHOLE],
        input_output_aliases={i: 2 + i for i in range(n)},
        compiler_params=pltpu.CompilerParams(has_side_effects=DATAFLOW))(*hbm, after)
    return outs[0], outs[1], outs[2:2 + n], outs[2 + n]


def gather_wait(send_sems, recv_sems, bufs, *after):
    n = len(bufs)

    def body(*refs):
        ins, send_ref, recv_ref = refs[:n], refs[n], refs[n + 1]
        for cp in _gather_copies(ins, send_ref, recv_ref, False):
            cp.wait_send()
            cp.wait_recv()

    return pl.pallas_call(
        body, name="gather_wait_%d" % n, out_shape=[pltpu.HBM(s.shape, s.dtype) for s in bufs],
        in_specs=[HBM] * n + [SEM, SEM] + [ANY] * len(after), out_specs=[HBM] * n,
        input_output_aliases={i: i for i in range(n)},
        compiler_params=pltpu.CompilerParams(has_side_effects=DATAFLOW))(*bufs, send_sems, recv_sems, *after)


def _peers(x, y, c):
    return [(1 - x if k & 4 else x, 1 - y if k & 2 else y, 1 - c if k & 1 else c) for k in range(1, N_DEV)]


def _partial_copies(g_ref, land_ref, send_sems, recv_sems, outgoing):
    x, y, c = _place()
    half = g_ref.shape[1] // 2
    cps = []
    for k, (px, py, pc) in enumerate(_peers(x, y, c)):
        src = g_ref.at[2 * px + py, _rows_of_core(pc, half)]
        dst = land_ref.at[4 * x + 2 * y + c] if outgoing else land_ref.at[4 * px + 2 * py + pc]
        cps.append(_rcopy(src, dst, (send_sems, recv_sems), k, (px, py, pc)))
    return cps


def partials_start(g, *, name):
    land = lax.empty((N_DEV, g.shape[1] // 2, g.shape[2]), g.dtype)

    def body(g_ref, land_ref, send_sems, recv_sems, g_thru, land_thru, token):
        for cp in _partial_copies(g_thru, land_thru, send_sems, recv_sems, True):
            cp.start()
        token[...] = jnp.zeros_like(token)

    return pl.pallas_call(
        body, name=name,
        out_shape=[pltpu.SemaphoreType.DMA((N_DEV - 1,)), pltpu.SemaphoreType.DMA((N_DEV - 1,)),
                   pltpu.HBM(g.shape, g.dtype), pltpu.HBM(land.shape, land.dtype), TOKEN],
        in_specs=[HBM, HBM], out_specs=[SEM, SEM, HBM, HBM, VMEM_WHOLE], input_output_aliases={0: 2, 1: 3},
        compiler_params=pltpu.CompilerParams(has_side_effects=DATAFLOW))(
        pltpu.with_memory_space_constraint(g, pltpu.HBM), pltpu.with_memory_space_constraint(land, pltpu.HBM))


def partials_wait(started, after):
    n = len(started)

    def body(*refs):
        for i in range(n):
            send_ref, recv_ref, g_ref, land_ref = refs[4 * i:4 * i + 4]
            for cp in _partial_copies(g_ref, land_ref, send_ref, recv_ref, False):
                cp.wait_send()
                cp.wait_recv()

    flat = [a for s in started for a in s]
    bufs = [a for s in started for a in s[2:]]
    outs = pl.pallas_call(
        body, name="partials_wait", out_shape=[pltpu.HBM(b.shape, b.dtype) for b in bufs],
        in_specs=[SEM, SEM, HBM, HBM] * n + [ANY], out_specs=[HBM] * (2 * n),
        input_output_aliases={4 * i + 2 + j: 2 * i + j for i in range(n) for j in range(2)},
        compiler_params=pltpu.CompilerParams(has_side_effects=DATAFLOW))(*flat, after)
    return [(outs[2 * i], outs[2 * i + 1]) for i in range(n)]


def sum_partials(pairs, order):
    n = len(pairs)

    def body(o_ref, *refs):
        j = pl.program_id(0)
        for g_ref, l_ref, f_ref in zip(refs[:n], refs[n:2 * n], refs[2 * n:]):
            @pl.when(j == 0)
            def _():
                f_ref[...] = g_ref[...].astype(F32)

            @pl.when(j > 0)
            def _():
                f_ref[...] += l_ref[...].astype(F32)

    g4 = [g.reshape(g.shape[0], 2, g.shape[1] // 2, g.shape[2]) for g, _ in pairs]
    lands = [l for _, l in pairs]
    return _pcall(body, name="sum_partials", grid=(N_DEV,), prefetch=1,
                  in_specs=[BS((None, None) + g.shape[2:], lambda j, o: (o[0], o[1], 0, 0)) for g in g4]
                  + [BS((None,) + l.shape[1:], lambda j, o: (o[jnp.maximum(j, 1) + 1], 0, 0)) for l in lands],
                  out_specs=[BS(l.shape[1:], lambda j, o: (0, 0)) for l in lands],
                  out_shape=[SDS(l.shape[1:], F32) for l in lands])(order, *g4, *lands)


def _pair_copies(f_refs, land_refs, send_sems, recv_sems):
    x, y, c = _place()
    return [_rcopy(f, o, (send_sems, recv_sems), i, (x, y, 1 - c)) for i, (f, o) in enumerate(zip(f_refs, land_refs))]


def pair_start(fs):
    n = len(fs)
    lands = [lax.empty(f.shape, f.dtype) for f in fs]

    def body(*refs):
        send_sems, recv_sems = refs[2 * n], refs[2 * n + 1]
        thru, land_thru, token = refs[2 * n + 2:3 * n + 2], refs[3 * n + 2:4 * n + 2], refs[4 * n + 2]
        for cp in _pair_copies(thru, land_thru, send_sems, recv_sems):
            cp.start()
        token[...] = jnp.zeros_like(token)

    hbm = [pltpu.with_memory_space_constraint(a, pltpu.HBM) for a in list(fs) + lands]
    outs = pl.pallas_call(
        body, name="pair_start",
        out_shape=[pltpu.SemaphoreType.DMA((n,)), pltpu.SemaphoreType.DMA((n,))]
        + [pltpu.HBM(a.shape, a.dtype) for a in list(fs) + lands] + [TOKEN],
        in_specs=[HBM] * (2 * n), out_specs=[SEM, SEM] + [HBM] * (2 * n) + [VMEM_WHOLE],
        input_output_aliases={i: 2 + i for i in range(2 * n)},
        compiler_params=pltpu.CompilerParams(has_side_effects=DATAFLOW))(*hbm)
    return outs[0], outs[1], outs[2:2 + n], outs[2 + n:2 + 2 * n], outs[2 + 2 * n]


def pair_wait(send_sems, recv_sems, fs, lands, after):
    n = len(fs)

    def body(*refs):
        for cp in _pair_copies(refs[:n], refs[n:2 * n], refs[2 * n], refs[2 * n + 1]):
            cp.wait_send()
            cp.wait_recv()

    outs = pl.pallas_call(
        body, name="pair_wait", out_shape=[pltpu.HBM(a.shape, a.dtype) for a in list(fs) + list(lands)],
        in_specs=[HBM] * (2 * n) + [SEM, SEM, ANY], out_specs=[HBM] * (2 * n),
        input_output_aliases={i: i for i in range(2 * n)},
        compiler_params=pltpu.CompilerParams(has_side_effects=DATAFLOW))(*fs, *lands, send_sems, recv_sems, after)
    return outs[:n], outs[n:]


def _small_copies(s_ref, land_ref, send_sems, recv_sems, outgoing):
    x, y, c = _place()
    cps = []
    for k, (px, py, pc) in enumerate(_peers(x, y, c)):
        dst = land_ref.at[4 * x + 2 * y + c] if outgoing else land_ref.at[4 * px + 2 * py + pc]
        cps.append(_rcopy(s_ref, dst, (send_sems, recv_sems), k, (px, py, pc)))
    return cps


def small_start(sm):
    land = lax.empty((N_DEV,) + sm.shape, sm.dtype)

    def body(s_ref, land_ref, send_sems, recv_sems, s_thru, land_thru):
        for cp in _small_copies(s_thru, land_thru, send_sems, recv_sems, True):
            cp.start()

    return pl.pallas_call(
        body, name="small_start",
        out_shape=[pltpu.SemaphoreType.DMA((N_DEV - 1,)), pltpu.SemaphoreType.DMA((N_DEV - 1,)),
                   pltpu.HBM(sm.shape, sm.dtype), pltpu.HBM(land.shape, land.dtype)],
        in_specs=[HBM, HBM], out_specs=[SEM, SEM, HBM, HBM], input_output_aliases={0: 2, 1: 3},
        compiler_params=pltpu.CompilerParams(has_side_effects=DATAFLOW))(
        pltpu.with_memory_space_constraint(sm, pltpu.HBM), pltpu.with_memory_space_constraint(land, pltpu.HBM))


def small_wait(send_sems, recv_sems, sm, land, *after):
    def body(send_ref, recv_ref, s_ref, land_ref, *rest):
        for cp in _small_copies(s_ref, land_ref, send_ref, recv_ref, False):
            cp.wait_send()
            cp.wait_recv()

    return pl.pallas_call(
        body, name="small_wait", out_shape=[pltpu.HBM(sm.shape, sm.dtype), pltpu.HBM(land.shape, land.dtype)],
        in_specs=[SEM, SEM, HBM, HBM] + [ANY] * len(after), out_specs=[HBM, HBM], input_output_aliases={2: 0, 3: 1},
        compiler_params=pltpu.CompilerParams(has_side_effects=DATAFLOW))(send_sems, recv_sems, sm, land, *after)


def sum_small(own, land, mevec):
    n, rows, width = land.shape
    tr = _tile(rows, (184, 8))

    def body(me_ref, own_ref, land_ref, o_ref):
        acc = jnp.zeros((tr, width), F32)
        for s in range(n):
            acc = acc + jnp.where(me_ref[0] == s, own_ref[...], land_ref[s])
        o_ref[...] = acc

    return _pcall(body, name="sum_small", grid=(rows // tr,), prefetch=1,
                  in_specs=[BS((tr, width), lambda i, me: (i, 0)), BS((n, tr, width), lambda i, me: (0, i, 0))],
                  out_specs=BS((tr, width), lambda i, me: (i, 0)), out_shape=SDS((rows, width), F32))(mevec, own, land)


def _to_full(blk, col):
    n, r, c = blk.shape
    return blk.transpose(1, 0, 2).reshape(r, n * c) if col else blk.reshape(n * r, c)


def _dup_cols(w):
    dup = lambda t: jnp.concatenate([t[:, :64], t[:, :64], t[:, 64:], t[:, 64:]], axis=1)
    return jnp.concatenate([w[:, :512], dup(w[:, 512:640]), dup(w[:, 640:768]), w[:, 768:]], axis=1)


def _fold_cols(d):
    fold = lambda t: jnp.concatenate([t[:, 0:64] + t[:, 64:128], t[:, 128:192] + t[:, 192:256]], axis=1)
    return jnp.concatenate([d[:, :512], fold(d[:, 512:768]), fold(d[:, 768:1024]), d[:, 1024:]], axis=1)


def _local_step(x, mem, positions, target, w_in, later, sp, emit):
    gain = lambda n: sp[n].reshape(1, -1)
    half = HEAD_DIM // 2
    inv_freq = 1.0 / (10000.0 ** (jnp.arange(half, dtype=F32) * (2.0 / HEAD_DIM)))
    ang = positions.astype(F32)[:, None] * inv_freq
    cos, sin = jnp.cos(ang), jnp.sin(ang)
    cos128 = jnp.tile(cos, (1, 4))
    sin128 = jnp.concatenate([-sin, sin, -sin, sin], axis=1)
    seg = jnp.arange(128) // HEAD_DIM
    bmat = (seg[:, None] == seg[None, :]).astype(BF16)
    gq128, gk128 = jnp.tile(gain("q_norm"), (1, 2)), jnp.tile(gain("k_norm"), (1, 2))
    sinkcol = jnp.repeat(sp["attn_sinks"].reshape(4, 2), BLK, axis=1).reshape(4, 2 * BLK, 1)
    wsc = sp["gmlp_ws"] * jnp.tril(jnp.ones((BLK, BLK), F32))[None]
    w2 = wsc.reshape(4, 2 * BLK, BLK).astype(MXU_DTYPE)
    w2t = wsc.swapaxes(1, 2).reshape(4, 2 * BLK, BLK).astype(MXU_DTYPE)
    bsl = jnp.repeat(sp["gmlp_bs"].reshape(4, 2, BLK).transpose(0, 2, 1), HEAD_DIM, axis=2)
    cb = sp["ffn_conv_b"].reshape(1, -1)
    w_in_d = _dup_cols(_to_full(w_in(cos128, sin128, gq128, gk128, sinkcol, w2, w2t, bsl), True))[None]

    h1, proj = rms_mm(x, gain("mix_norm"), w_in_d, name="mix_in")
    qr, kr, vb, gu, gvn, attn, gm, y = mixer_core_fwd(proj, cos128, sin128, gq128, gk128, gain("gmlp_v_norm"), bmat,
                                                      sinkcol, gain("attn_out_norm"), w2, bsl, gain("gmlp_out_norm"))
    wf, last = later(y)
    w_out, xa_wq, xa_wo = (_to_full(wf[n], False) for n in ("w_out", "xa_wq", "xa_wo"))
    x1 = mm(y, w_out, res=x, name="mix_out")
    mn, kv = rms_mm(mem, gain("mem_norm"), wf["xa_wkv"], name="xa_kv")
    kn, vbx = mem_pre(kv, gain("xa_k_norm"))
    h2, qx, xo, x2 = xattn_block_fwd(x1, gain("xa_norm"), xa_wq, kn, vbx, gain("xa_q_norm"), xa_wo)
    ffn_w, cw = last(x2)
    wf = {**wf, **ffn_w}
    ffn_down = _to_full(wf["ffn_down"], False)
    h3, a, f, dx3, loss_acc = ffn_fwd_loss(x2, gain("ffn_norm"), wf["ffn_up"], cw, cb, ffn_down, target)

    by_rows = lambda g: g.reshape(N_CHIPS, g.shape[1] // N_CHIPS, g.shape[2])
    sent = emit("ffn_down", by_rows(mm_tn(f, dx3, name="g_ffn_down", out_dtype=WIRE_DTYPE)))
    dc, gcw = convgate_bwd(a, dx3, ffn_down[None], cw, cb, after=sent)
    da, dx2, dg_ffn = conv_transpose_rms_bwd(dc, cw, wf["ffn_up"], x2, gain("ffn_norm"), dx3)
    sent = emit("ffn_up", mm_tn(h3, da, name="g_ffn_up", out_dtype=WIRE_DTYPE, chunks=N_CHIPS))
    sent = emit("xa_wo", by_rows(mm_tn(xo, dx2, name="g_xa_wo", out_dtype=WIRE_DTYPE, after=sent)))
    dqx, dx1, dkn, dvx, dg_xq, dg_xa = xattn_block_bwd(dx2, xa_wo[None], qx, kn, vbx, gain("xa_q_norm"), xa_wq[None],
                                                       x1, gain("xa_norm"), after=sent)
    sent = emit("xa_wq", by_rows(mm_tn(h2, dqx, name="g_xa_wq", out_dtype=WIRE_DTYPE)))
    dkv, dg_xk = mem_bwd(kv, dkn, dvx, gain("xa_k_norm"), after=sent)
    _, dg_mem = mm_nt_rms_bwd(dkv, wf["xa_wkv"], mem, gain("mem_norm"), jnp.zeros_like(mem), name="d_mem")
    sent = emit("xa_wkv", mm_tn(mn, dkv, name="g_xa_wkv", out_dtype=WIRE_DTYPE, chunks=N_CHIPS))
    dattn, dgm, dg_y = mm_nt_post_bwd(dx1, w_out[None], attn, gm, gain("attn_out_norm"), gain("gmlp_out_norm"),
                                      name="d_mix_out", after=sent)
    sent = emit("w_out", by_rows(mm_tn(y, dx1, name="g_w_out", out_dtype=WIRE_DTYPE)))
    dproj, dsk, dws, dbl, dgq, dgk, dg_gvn = mixer_core_bwd(
        proj, cos128, sin128, gq128, gk128, gain("gmlp_v_norm"), bmat, qr, kr, vb, sinkcol, dattn, dgm, gvn, gu,
        w2, w2t, bsl, after=sent)
    g_in = _fold_cols(mm_tn(h1, dproj, name="g_w_in", out_dtype=F32)[0])
    sent = emit("w_in", g_in.reshape(1024, N_CHIPS, 448).transpose(1, 0, 2).astype(WIRE_DTYPE))
    grad_x, dg_mix = mm_nt_rms_bwd(dproj, w_in_d, x, gain("mix_norm"), dx1, name="d_x", tm=1024, after=sent)
    packed = pack_small(dg_mix, dgq, dgk, dsk, dg_gvn, dg_y, dg_xa, dg_mem, dg_xq, dg_xk, dg_ffn, gcw, dbl, dws)
    return loss_acc, grad_x, packed


def _gather_step(w, chipvec):
    slots = cast_shards([w[n][0] for n in BIG_NAMES], w["ffn_conv"][0], chipvec)
    send_a, recv_a, first, token = gather_start(slots[:1], chipvec)
    send_b, recv_b, mid, token = gather_start(slots[1:5], token)
    send_c, recv_c, rest, token = gather_start(slots[5:], token)

    def w_in(*after):
        return gather_wait(send_a, recv_a, first, token, *after)[0]

    def last(after):
        got = gather_wait(send_c, recv_c, rest, after)
        return dict(zip(BIG_NAMES[5:], got[:-1])), _to_full(got[-1], True)

    def later(after):
        return dict(zip(BIG_NAMES[1:5], gather_wait(send_b, recv_b, mid, after))), last

    return w_in, later, token


def _reduce_update(started, packed, w, m, v, chipvec, cvec, order):
    small_sent = small_start(packed)
    own = sum_partials(partials_wait([started[n] for n in BIG_NAMES], small_sent[2]), order)
    pair_send, pair_recv, own, lands, pair_started = pair_start(own)
    own, other = pair_wait(pair_send, pair_recv, own, lands, pair_started)
    res = [{}, {}, {}, {}]
    for n, g_own, g_other in zip(BIG_NAMES, own, other):
        for d, o in zip(res, adamw_matrix(w[n], m[n], v[n], g_own, g_other, cvec, name="adamw_" + n)):
            d[n] = o
    mevec = (2 * order[0:1] + order[1:2]).astype(jnp.int32)
    small_sum = sum_small(*small_wait(*small_sent, *[res[3][n] for n in BIG_NAMES]), mevec)
    for d, outs in zip(res, adamw_small(small_sum, w, m, v, chipvec)):
        d.update(zip(SMALL, outs))
    return res


def kernel(x, mem, positions, mix_norm, w_in, q_norm, k_norm, attn_sinks, gmlp_v_norm, gmlp_ws, gmlp_bs, attn_out_norm, gmlp_out_norm, w_out, xa_norm, mem_norm, xa_wq, xa_wkv, xa_q_norm, xa_k_norm, xa_wo, ffn_norm, ffn_up, ffn_conv, ffn_conv_b, ffn_down, loss_target, m_mix_norm, m_w_in, m_q_norm, m_k_norm, m_attn_sinks, m_gmlp_v_norm, m_gmlp_ws, m_gmlp_bs, m_attn_out_norm, m_gmlp_out_norm, m_w_out, m_xa_norm, m_mem_norm, m_xa_wq, m_xa_wkv, m_xa_q_norm, m_xa_k_norm, m_xa_wo, m_ffn_norm, m_ffn_up, m_ffn_conv, m_ffn_conv_b, m_ffn_down, v_mix_norm, v_w_in, v_q_norm, v_k_norm, v_attn_sinks, v_gmlp_v_norm, v_gmlp_ws, v_gmlp_bs, v_attn_out_norm, v_gmlp_out_norm, v_w_out, v_xa_norm, v_mem_norm, v_xa_wq, v_xa_wkv, v_xa_q_norm, v_xa_k_norm, v_xa_wo, v_ffn_norm, v_ffn_up, v_ffn_conv, v_ffn_conv_b, v_ffn_down):
    w = dict(mix_norm=mix_norm, w_in=w_in, q_norm=q_norm, k_norm=k_norm, attn_sinks=attn_sinks, gmlp_v_norm=gmlp_v_norm, gmlp_ws=gmlp_ws, gmlp_bs=gmlp_bs, attn_out_norm=attn_out_norm, gmlp_out_norm=gmlp_out_norm, w_out=w_out, xa_norm=xa_norm, mem_norm=mem_norm, xa_wq=xa_wq, xa_wkv=xa_wkv, xa_q_norm=xa_q_norm, xa_k_norm=xa_k_norm, xa_wo=xa_wo, ffn_norm=ffn_norm, ffn_up=ffn_up, ffn_conv=ffn_conv, ffn_conv_b=ffn_conv_b, ffn_down=ffn_down)
    m = dict(mix_norm=m_mix_norm, w_in=m_w_in, q_norm=m_q_norm, k_norm=m_k_norm, attn_sinks=m_attn_sinks, gmlp_v_norm=m_gmlp_v_norm, gmlp_ws=m_gmlp_ws, gmlp_bs=m_gmlp_bs, attn_out_norm=m_attn_out_norm, gmlp_out_norm=m_gmlp_out_norm, w_out=m_w_out, xa_norm=m_xa_norm, mem_norm=m_mem_norm, xa_wq=m_xa_wq, xa_wkv=m_xa_wkv, xa_q_norm=m_xa_q_norm, xa_k_norm=m_xa_k_norm, xa_wo=m_xa_wo, ffn_norm=m_ffn_norm, ffn_up=m_ffn_up, ffn_conv=m_ffn_conv, ffn_conv_b=m_ffn_conv_b, ffn_down=m_ffn_down)
    v = dict(mix_norm=v_mix_norm, w_in=v_w_in, q_norm=v_q_norm, k_norm=v_k_norm, attn_sinks=v_attn_sinks, gmlp_v_norm=v_gmlp_v_norm, gmlp_ws=v_gmlp_ws, gmlp_bs=v_gmlp_bs, attn_out_norm=v_attn_out_norm, gmlp_out_norm=v_gmlp_out_norm, w_out=v_w_out, xa_norm=v_xa_norm, mem_norm=v_mem_norm, xa_wq=v_xa_wq, xa_wkv=v_xa_wkv, xa_q_norm=v_xa_q_norm, xa_k_norm=v_xa_k_norm, xa_wo=v_xa_wo, ffn_norm=v_ffn_norm, ffn_up=v_ffn_up, ffn_conv=v_ffn_conv, ffn_conv_b=v_ffn_conv_b, ffn_down=v_ffn_down)
    ix, iy, ic = lax.axis_index("x"), lax.axis_index("y"), lax.axis_index("c")
    chip = 2 * ix + iy
    chipvec = chip.astype(jnp.int32).reshape(1)
    cvec = ic.astype(jnp.int32).reshape(1)
    order = jnp.stack([chip, ic] + [4 * px + 2 * py + pc for px, py, pc in _peers(ix, iy, ic)]).astype(jnp.int32)

    w_in_all, later, token = _gather_step(w, chipvec)
    zero = token[0, 0]
    sp = {n: w[n][0] + zero for n in SMALL if n != "ffn_conv"}
    positions = positions + zero.astype(jnp.int32)
    started = {}

    def emit(name, g):
        *started[name], token = partials_start(g, name="partials_start_" + name)
        return token

    loss_acc, grad_x, packed = _local_step(x[0], mem[0], positions[0], loss_target[0], w_in_all, later, sp, emit)
    grads, delta, new_m, new_v = _reduce_update(started, packed, w, m, v, chipvec, cvec, order)
    loss = lax.psum(loss_acc[0, 0], ("x", "y", "c"))
    ordered = lambda d: [d[n] for n in WEIGHTS]
    return (loss, grad_x[None], *ordered(grads), *ordered(delta), *ordered(new_m), *ordered(new_v))
```

```python
import math

import jax
import jax.numpy as jnp
from jax import lax
from jax.experimental import pallas as pl
from jax.experimental.pallas import tpu as pltpu

F32 = jnp.float32
BF16 = jnp.bfloat16
MXU_DTYPE = jnp.bfloat16
WIRE_DTYPE = jnp.bfloat16
EPS = 1e-6
VMEM_LIMIT_V7X = 56 * 1024 * 1024

D_MODEL = 1024
HEAD_DIM = 64
BLK = 128
XA_HEADS = 4
XA_DH = 256
MEM_LEN = 256
D_FF = 2816
IN_COLS_DUP = 2048
N_CHIPS = 4
N_DEV = 8

ADAM_LR = 0.001
ADAM_B1 = 0.9
ADAM_B2 = 0.999
ADAM_EPS = 1e-08
ADAM_WD = 0.01
ADAM_STEP = 10

NT = (((1,), (1,)), ((), ()))
TN = (((0,), (0,)), ((), ()))
NN = (((1,), (0,)), ((), ()))
MINF = float(jnp.finfo(jnp.float32).min)
GELU_K0 = math.sqrt(2.0 / math.pi)
GELU_K1 = 0.044715

BS = pl.BlockSpec
SDS = jax.ShapeDtypeStruct
ANY = pl.BlockSpec(memory_space=pl.ANY)
MESH = pl.DeviceIdType.MESH


def _dot(a, b, dims=NN):
    return lax.dot_general(a.astype(MXU_DTYPE), b.astype(MXU_DTYPE), dims, preferred_element_type=F32)


def _segsum(x, bmat):
    hi = x.astype(BF16)
    lo = (x - hi.astype(F32)).astype(BF16)
    return (jnp.dot(hi, bmat, preferred_element_type=F32) + jnp.dot(lo, bmat, preferred_element_type=F32))


def _gelu(x):
    return 0.5 * x * (1.0 + jnp.tanh(GELU_K0 * (x + GELU_K1 * x * x * x)))


def _gelu_grad(x):
    t = jnp.tanh(GELU_K0 * (x + GELU_K1 * x * x * x))
    return 0.5 * (1.0 + t) + 0.5 * x * (1.0 - t * t) * GELU_K0 * (1.0 + 3.0 * GELU_K1 * x * x)


def _gelu_and_grad(x):
    x2 = x * x
    t = jnp.tanh(x * (GELU_K0 * GELU_K1 * x2 + GELU_K0))
    hx = 0.5 * x
    return hx * t + hx, 0.5 * t + 0.5 + hx * (1.0 - t * t) * (3.0 * GELU_K0 * GELU_K1 * x2 + GELU_K0)


def _rms(x):
    return lax.rsqrt(jnp.mean(x * x, axis=-1, keepdims=True) + EPS)


def _rms_bwd(dy, x, g, r):
    dyg = dy * g
    dx = r * dyg - x * (r * r * r) * jnp.mean(dyg * x, axis=-1, keepdims=True)
    return dx, dy * x * r


def _pcall(body, *, name, grid, in_specs, out_specs, out_shape, scratch=(), prefetch=0, after=None):
    params = pltpu.CompilerParams(dimension_semantics=("arbitrary",) * len(grid), vmem_limit_bytes=VMEM_LIMIT_V7X)
    in_specs = list(in_specs)
    kernel_fn = body
    if after is not None:
        n_in = prefetch + len(in_specs)
        in_specs.append(ANY)

        def kernel_fn(*refs):
            return body(*refs[:n_in], *refs[n_in + 1:])

    if prefetch:
        spec = pltpu.PrefetchScalarGridSpec(num_scalar_prefetch=prefetch, grid=grid, in_specs=in_specs,
                                            out_specs=out_specs, scratch_shapes=list(scratch))
        call = pl.pallas_call(kernel_fn, name=name, grid_spec=spec, out_shape=out_shape, compiler_params=params)
    else:
        call = pl.pallas_call(kernel_fn, name=name, grid=grid, in_specs=in_specs, out_specs=out_specs,
                              out_shape=out_shape, scratch_shapes=list(scratch), compiler_params=params)
    return call if after is None else (lambda *args: call(*args, after))


def _tile(n, prefs):
    for p in prefs:
        if p <= n and n % p == 0:
            return p
    return n


def _resident(shape):
    return pl.BlockSpec(shape, lambda *_: (0,) * len(shape), pipeline_mode=pl.Buffered(1))


def _acc_rows(ref, row, val):
    ref[row:row + 1, :] += jnp.sum(val, axis=0, keepdims=True)


def rms_mm(x, g, w3, *, name, tm=1024):
    M, K = x.shape
    Q, _, C = w3.shape
    tm = _tile(M, (tm, 256))

    def body(x_ref, g_ref, w_ref, h_ref, o_ref):
        def write_h():
            xv = x_ref[...]
            h_ref[...] = (xv * _rms(xv) * g_ref[...]).astype(h_ref.dtype)

        if Q == 1:
            write_h()
        else:
            pl.when(pl.program_id(1) == 0)(write_h)
        o_ref[...] = _dot(h_ref[...], w_ref[pl.program_id(1)])

    return _pcall(body, name=name, grid=(M // tm, Q),
                  in_specs=[BS((tm, K), lambda i, j: (i, 0)), BS((1, K), lambda i, j: (0, 0)),
                            _resident((Q, K, C))],
                  out_specs=[BS((tm, K), lambda i, j: (i, 0)), BS((tm, C), lambda i, j: (i, j))],
                  out_shape=[SDS((M, K), MXU_DTYPE), SDS((M, Q * C), F32)])(x, g, w3)


def mm(a, w, *, name, res):
    M, K = a.shape
    N = w.shape[1]
    tm = _tile(M, (1024, 256))

    def body(a_ref, w_ref, r_ref, o_ref):
        o_ref[...] = _dot(a_ref[...], w_ref[...]) + r_ref[...]

    return _pcall(body, name=name, grid=(M // tm,),
                  in_specs=[BS((tm, K), lambda i: (i, 0)), _resident((K, N)), BS((tm, N), lambda i: (i, 0))],
                  out_specs=BS((tm, N), lambda i: (i, 0)), out_shape=SDS((M, N), F32))(a, w, res)


def _nt_chunks(a_ref, w_ref):
    q_n, _, kc = w_ref.shape
    acc = _dot(a_ref[:, 0:kc], w_ref[0], NT)
    for q in range(1, q_n):
        acc = acc + _dot(a_ref[:, q * kc:(q + 1) * kc], w_ref[q], NT)
    return acc


def mm_nt_rms_bwd(a, w3, x, g, dres, *, name, tm=512, after=None):
    M = a.shape[0]
    Q, N, Kc = w3.shape
    tm = _tile(M, (tm, 256))

    def body(a_ref, w_ref, x_ref, g_ref, dr_ref, dx_ref, dg_ref):
        @pl.when(pl.program_id(0) == 0)
        def _():
            dg_ref[...] = jnp.zeros_like(dg_ref)

        xv = x_ref[...]
        dx, dgc = _rms_bwd(_nt_chunks(a_ref, w_ref), xv, g_ref[...], _rms(xv))
        dx_ref[...] = dr_ref[...] + dx
        _acc_rows(dg_ref, 0, dgc)

    row = BS((tm, N), lambda i: (i, 0))
    return _pcall(body, name=name, grid=(M // tm,), after=after,
                  in_specs=[BS((tm, Q * Kc), lambda i: (i, 0)), _resident((Q, N, Kc)), row,
                            BS((1, N), lambda i: (0, 0)), row],
                  out_specs=[row, BS((8, N), lambda i: (0, 0))],
                  out_shape=[SDS((M, N), F32), SDS((8, N), F32)])(a, w3, x, g, dres)


def mm_tn(a, b, *, name, out_dtype, chunks=1, after=None):
    M, K = a.shape
    N = b.shape[1]
    C = N // chunks
    tm = _tile(M, (1024, 256))
    tk = _tile(K, (1408, 1024, 512))
    tn = _tile(C, (1408, 1024, 512))
    per = C // tn
    nm = M // tm

    def body(a_ref, b_ref, o_ref, acc):
        m = pl.program_id(2)

        @pl.when(m == 0)
        def _():
            acc[...] = jnp.zeros_like(acc)

        acc[...] += _dot(a_ref[...], b_ref[...], TN)

        @pl.when(m == nm - 1)
        def _():
            o_ref[...] = acc[...].astype(o_ref.dtype)

    return _pcall(body, name=name, grid=(K // tk, N // tn, nm), after=after,
                  in_specs=[BS((tm, tk), lambda k, n, m: (m, k)), BS((tm, tn), lambda k, n, m: (m, n))],
                  out_specs=BS((None, tk, tn), lambda k, n, m: (n // per, k, n % per)),
                  out_shape=SDS((chunks, K, C), out_dtype), scratch=[pltpu.VMEM((tk, tn), F32)])(a, b)


def _lane(shape):
    return lax.broadcasted_iota(jnp.int32, shape, 1)


def _head_means(slabs, bmat):
    tm = slabs[0].shape[0]
    means = _segsum(jnp.concatenate(slabs, axis=0), bmat) * (1.0 / HEAD_DIM)
    return [means[i * tm:(i + 1) * tm] for i in range(len(slabs))]


def _half_swap(x, first):
    return jnp.where(first, pltpu.roll(x, 96, 1), pltpu.roll(x, 32, 1))


def _by_head(x2, lo):
    z = jnp.zeros((BLK, 128), x2.dtype)
    parts = []
    for s in range(2):
        xs = x2[:, s * 128:(s + 1) * 128]
        parts += [jnp.where(lo, xs, z), jnp.where(lo, z, xs)]
    return jnp.concatenate(parts, axis=0)


def _from_heads(o4, lo):
    return jnp.concatenate([jnp.where(lo, o4[0:BLK], o4[BLK:2 * BLK]),
                            jnp.where(lo, o4[2 * BLK:3 * BLK], o4[3 * BLK:])], axis=1)


def _swa_probs(q2, kd, sink, n, lo):
    qp = _by_head(q2, lo)
    sc = _dot(qp, kd, NT) * (1.0 / math.sqrt(HEAD_DIM))
    r_i = lax.broadcasted_iota(jnp.int32, (4 * BLK, 2 * BLK), 0)
    k_j = lax.broadcasted_iota(jnp.int32, (4 * BLK, 2 * BLK), 1)
    diff = (r_i & (BLK - 1)) + BLK - k_j
    mask = (diff >= 0) & (diff < BLK) & ((k_j >= BLK) | (n > 0))
    sc = jnp.where(mask, sc, MINF)
    m = jnp.maximum(jnp.max(sc, axis=1, keepdims=True), sink)
    p = jnp.exp(sc - m)
    es = jnp.exp(sink - m)
    inv = 1.0 / (jnp.sum(p, axis=1, keepdims=True) + es)
    return qp, p * inv, es * inv


def mixer_core_fwd(proj, cos, sin, gq, gk, gvn, bmat, sinkcol, gao, w2, bsl, ggo):
    S = proj.shape[0]
    sub = 4 if S % (4 * BLK) == 0 else 1

    def body(p_ref, c_ref, s_ref, gq_ref, gk_ref, gvn_ref, b_ref, sk_ref, gao_ref, w2_ref, bsl_ref, ggo_ref,
             qr_ref, kr_ref, vb_ref, gu_ref, gvo_ref, at_ref, gm_ref, y_ref, k_prev, v_prev):
        n = pl.program_id(0)

        @pl.when(n == 0)
        def _():
            k_prev[...] = jnp.zeros_like(k_prev)
            v_prev[...] = jnp.zeros_like(v_prev)

        bm = b_ref[...]
        first = (_lane((BLK, 128)) & 63) < 32
        lo = _lane((BLK, 128)) < 64
        for sb in range(sub):
            rs = slice(sb * BLK, (sb + 1) * BLK)
            cos_v, sin_v = c_ref[rs, :], s_ref[rs, :]
            slabs = [p_ref[rs, s * 128:(s + 1) * 128] for s in range(6)]
            for s, (slab, ms) in enumerate(zip(slabs, _head_means([x * x for x in slabs], bm))):
                qn = slab * lax.rsqrt(ms + EPS) * (gq_ref[...] if s < 4 else gk_ref[...])
                out = qn * cos_v + _half_swap(qn, first) * sin_v
                if s < 4:
                    qr_ref[rs, s * 128:(s + 1) * 128] = out.astype(qr_ref.dtype)
                else:
                    kr_ref[rs, (s - 4) * 128:(s - 3) * 128] = out.astype(kr_ref.dtype)
            vb_ref[rs, :] = p_ref[rs, 768:1024].astype(vb_ref.dtype)
            gu_ref[rs, :] = _gelu(p_ref[rs, 1024:1536])
            gv = _gelu(p_ref[rs, 1536:2048])
            gvo_ref[rs, :] = (gv * _rms(gv) * gvn_ref[...]).astype(gvo_ref.dtype)

            before = slice((sb - 1) * BLK, sb * BLK)
            for h in range(2):
                hs, qs = slice(h * 128, (h + 1) * 128), slice(h * 256, (h + 1) * 256)
                k_before = k_prev[:, hs] if sb == 0 else kr_ref[before, hs]
                v_before = v_prev[:, hs] if sb == 0 else vb_ref[before, hs]
                kd = jnp.concatenate([k_before, kr_ref[rs, hs]], axis=0)
                vd = jnp.concatenate([v_before, vb_ref[rs, hs]], axis=0)
                sink = jnp.concatenate([sk_ref[2 * h], sk_ref[2 * h + 1]], axis=0)
                _, p, _ = _swa_probs(qr_ref[rs, qs], kd, sink, n * sub + sb, lo)
                at_ref[rs, qs] = _from_heads(_dot(p, vd), lo)

            for j in range(4):
                sl = slice(j * 128, (j + 1) * 128)
                m2 = _dot(w2_ref[j], gvo_ref[rs, sl])
                mixed = jnp.where(lo, m2[:BLK], m2[BLK:]) + bsl_ref[j]
                gm_ref[rs, sl] = gu_ref[rs, sl] * mixed
            a, gm = at_ref[rs, :], gm_ref[rs, :]
            y_ref[rs, :512] = (a * _rms(a) * gao_ref[...]).astype(y_ref.dtype)
            y_ref[rs, 512:] = (gm * _rms(gm) * ggo_ref[...]).astype(y_ref.dtype)
        k_prev[...] = kr_ref[(sub - 1) * BLK:, :]
        v_prev[...] = vb_ref[(sub - 1) * BLK:, :]

    row = lambda w: BS((sub * BLK, w), lambda n: (n, 0))
    const = lambda *shape: BS(shape, lambda n: (0,) * len(shape))
    return _pcall(body, name="mixer_core_fwd", grid=(S // (sub * BLK),),
                  in_specs=[row(IN_COLS_DUP), row(128), row(128), const(1, 128), const(1, 128), const(1, 512),
                            const(128, 128), const(4, 2 * BLK, 1), const(1, 512), const(4, 2 * BLK, BLK),
                            const(4, BLK, 128), const(1, 512)],
                  out_specs=[row(512), row(256), row(256), row(512), row(512), row(512), row(512), row(1024)],
                  out_shape=[SDS((S, 512), MXU_DTYPE), SDS((S, 256), MXU_DTYPE), SDS((S, 256), MXU_DTYPE),
                             SDS((S, 512), F32), SDS((S, 512), MXU_DTYPE), SDS((S, 512), F32), SDS((S, 512), F32),
                             SDS((S, 1024), MXU_DTYPE)],
                  scratch=[pltpu.VMEM((BLK, 256), MXU_DTYPE), pltpu.VMEM((BLK, 256), MXU_DTYPE)])(
        proj, cos, sin, gq, gk, gvn, bmat, sinkcol, gao, w2, bsl, ggo)


def mem_pre(kv, gxk):
    def body(kv_ref, g_ref, kn_ref, vb_ref):
        for h in range(XA_HEADS):
            sl = slice(h * XA_DH, (h + 1) * XA_DH)
            k = kv_ref[:, sl]
            kn_ref[:, sl] = (k * _rms(k) * g_ref[...]).astype(kn_ref.dtype)
        vb_ref[...] = kv_ref[:, 1024:2048].astype(vb_ref.dtype)

    full = lambda r, w: BS((r, w), lambda i: (0, 0))
    return _pcall(body, name="mem_pre", grid=(1,), in_specs=[full(MEM_LEN, 2048), full(1, XA_DH)],
                  out_specs=[full(MEM_LEN, 1024), full(MEM_LEN, 1024)],
                  out_shape=[SDS((MEM_LEN, 1024), MXU_DTYPE), SDS((MEM_LEN, 1024), MXU_DTYPE)])(kv, gxk)


def _xa_probs(qh, g, kn_h):
    r = _rms(qh)
    qn = qh * r * g
    s = _dot(qn, kn_h, NT) * (1.0 / math.sqrt(XA_DH))
    p = jnp.exp(s - jnp.max(s, axis=1, keepdims=True))
    return r, qn, p * (1.0 / jnp.sum(p, axis=1, keepdims=True))


def xattn_block_fwd(x1, g, wq, kn, vb, gxq, wo):
    S, D = x1.shape
    tm = _tile(S, (512, 256))

    def body(x_ref, g_ref, wq_ref, kn_ref, vb_ref, gxq_ref, wo_ref, h_ref, q_ref, o_ref, x2_ref):
        xv = x_ref[...]
        h_ref[...] = (xv * _rms(xv) * g_ref[...]).astype(h_ref.dtype)
        q_ref[...] = _dot(h_ref[...], wq_ref[...])
        for h in range(XA_HEADS):
            sl = slice(h * XA_DH, (h + 1) * XA_DH)
            _, _, p = _xa_probs(q_ref[:, sl], gxq_ref[...], kn_ref[:, sl])
            o_ref[:, sl] = _dot(p, vb_ref[:, sl]).astype(o_ref.dtype)
        x2_ref[...] = _dot(o_ref[...], wo_ref[...]) + xv

    row = BS((tm, D), lambda i: (i, 0))
    full = lambda r, w: BS((r, w), lambda i: (0, 0))
    return _pcall(body, name="xattn_block_fwd", grid=(S // tm,),
                  in_specs=[row, full(1, D), _resident(wq.shape), full(MEM_LEN, D), full(MEM_LEN, D), full(1, XA_DH),
                            _resident(wo.shape)],
                  out_specs=[row, row, row, row],
                  out_shape=[SDS((S, D), MXU_DTYPE), SDS((S, D), F32), SDS((S, D), MXU_DTYPE), SDS((S, D), F32)])(
        x1, g, wq, kn, vb, gxq, wo)


CONV_COLS = 1408


def _conv_taps(a_ref, halo_ref, w_ref, b_ref, cols, first_tile):
    a = a_ref[:, cols]
    row = lax.broadcasted_iota(jnp.int32, (8, a.shape[1]), 0)
    h6 = jnp.where(first_tile, 0.0, halo_ref[6:7, cols])
    h7 = jnp.where(first_tile, 0.0, halo_ref[7:8, cols])
    r1, r2 = pltpu.roll(a, 1, 0), pltpu.roll(a, 2, 0)
    a1 = jnp.concatenate([jnp.where(row == 0, h7, r1[0:8]), r1[8:]], axis=0)
    a2 = jnp.concatenate([jnp.where(row == 0, h6, jnp.where(row == 1, h7, r2[0:8])), r2[8:]], axis=0)
    c = w_ref[2:3, cols] * a + w_ref[1:2, cols] * a1 + w_ref[0:1, cols] * a2 + b_ref[:, cols]
    return c, (a2, a1, a)


def _conv_specs(tm):
    halo_blocks = tm // 8
    return [BS((tm, D_FF), lambda i: (i, 0)), BS((tm, D_FF), lambda i: (i, 1)),
            BS((8, D_FF), lambda i: (jnp.maximum(i * halo_blocks - 1, 0), 0)),
            BS((8, D_FF), lambda i: (jnp.maximum(i * halo_blocks - 1, 0), 1)),
            BS((3, D_FF), lambda i: (0, 0)), BS((3, D_FF), lambda i: (0, 1)),
            BS((1, D_FF), lambda i: (0, 0)), BS((1, D_FF), lambda i: (0, 1))]


def ffn_fwd_loss(x2, g, w_up3, cw, cb, w_down, target):
    S, D = x2.shape
    Q, _, C = w_up3.shape
    tm = _tile(S, (256,))

    def body(x_ref, g_ref, wu_ref, cw_ref, cb_ref, wd_ref, t_ref, h_ref, a_ref, f_ref, d_ref, l_ref, tail):
        first_tile = pl.program_id(0) == 0

        @pl.when(first_tile)
        def _():
            l_ref[...] = jnp.zeros_like(l_ref)
            tail[...] = jnp.zeros_like(tail)

        xv = x_ref[...]
        h_ref[...] = (xv * _rms(xv) * g_ref[...]).astype(h_ref.dtype)
        for q in range(Q):
            a_ref[:, q * C:(q + 1) * C] = _dot(h_ref[...], wu_ref[q])
        for c0 in range(0, D_FF, CONV_COLS):
            cols, ucols = slice(c0, c0 + CONV_COLS), slice(D_FF + c0, D_FF + c0 + CONV_COLS)
            cg, _ = _conv_taps(a_ref, tail, cw_ref, cb_ref, cols, first_tile)
            cu, _ = _conv_taps(a_ref, tail, cw_ref, cb_ref, ucols, first_tile)
            f_ref[:, cols] = (_gelu(cg) * cu).astype(f_ref.dtype)
        tail[...] = a_ref[tm - 8:tm, :]
        e = _dot(f_ref[...], wd_ref[...]) + xv - t_ref[...]
        d_ref[...] = e * (1.0 / D)
        l_ref[...] += jnp.sum(e * e) * (0.5 / D)

    row = lambda w: BS((tm, w), lambda i: (i, 0))
    const = lambda r, w: BS((r, w), lambda i: (0, 0))
    return _pcall(body, name="ffn_fwd_loss", grid=(S // tm,),
                  in_specs=[row(D), const(1, D), _resident(w_up3.shape), const(3, 2 * D_FF), const(1, 2 * D_FF),
                            _resident(w_down.shape), row(D)],
                  out_specs=[row(D), row(2 * D_FF), row(D_FF), row(D), const(8, 128)],
                  out_shape=[SDS((S, D), MXU_DTYPE), SDS((S, 2 * D_FF), F32), SDS((S, D_FF), MXU_DTYPE),
                             SDS((S, D), F32), SDS((8, 128), F32)],
                  scratch=[pltpu.VMEM((8, 2 * D_FF), F32)])(x2, g, w_up3, cw, cb, w_down, target)


def convgate_bwd(a, dx3, w3, cw, cb, after=None):
    S = a.shape[0]
    tm = _tile(S, (256,))

    def body(ag_ref, au_ref, hg_ref, hu_ref, wg_ref, wu_ref, bg_ref, bu_ref, dx_ref, wd_ref, dc_ref, gw_ref, df_ref):
        first_tile = pl.program_id(0) == 0

        @pl.when(first_tile)
        def _():
            gw_ref[...] = jnp.zeros_like(gw_ref)

        df_ref[...] = _nt_chunks(dx_ref, wd_ref)
        for c0 in range(0, D_FF, CONV_COLS):
            cols, ucols = slice(c0, c0 + CONV_COLS), slice(D_FF + c0, D_FF + c0 + CONV_COLS)
            cg, g_taps = _conv_taps(ag_ref, hg_ref, wg_ref, bg_ref, cols, first_tile)
            cu, u_taps = _conv_taps(au_ref, hu_ref, wu_ref, bu_ref, cols, first_tile)
            df_v = df_ref[:, cols]
            gate, gate_grad = _gelu_and_grad(cg)
            dcg = df_v * cu * gate_grad
            dcu = df_v * gate
            dc_ref[:, cols] = dcg
            dc_ref[:, ucols] = dcu
            for col, dcv, taps in ((cols, dcg, g_taps), (ucols, dcu, u_taps)):
                for j in range(3):
                    gw_ref[j:j + 1, col] += jnp.sum(dcv * taps[j], axis=0, keepdims=True)
                gw_ref[3:4, col] += jnp.sum(dcv, axis=0, keepdims=True)

    return _pcall(body, name="convgate_bwd", grid=(S // tm,), after=after,
                  in_specs=_conv_specs(tm) + [BS((tm, dx3.shape[1]), lambda i: (i, 0)), _resident(w3.shape)],
                  out_specs=[BS((tm, 2 * D_FF), lambda i: (i, 0)), BS((8, 2 * D_FF), lambda i: (0, 0))],
                  out_shape=[SDS((S, 2 * D_FF), F32), SDS((8, 2 * D_FF), F32)],
                  scratch=[pltpu.VMEM((tm, D_FF), F32)])(a, a, a, a, cw, cw, cb, cb, dx3, w3)


def conv_transpose_rms_bwd(dc, cw, w3, x, g, dres):
    S, C = dc.shape
    Q, N, Kc = w3.shape
    tm = _tile(S, (256,))
    nt = S // tm
    halo_blocks = tm // 8

    def body(dc_ref, halo_ref, cw_ref, w_ref, x_ref, g_ref, dr_ref, da_ref, dx_ref, dg_ref):
        @pl.when(pl.program_id(0) == 0)
        def _():
            dg_ref[...] = jnp.zeros_like(dg_ref)

        last_tile = pl.program_id(0) == nt - 1
        row = lax.broadcasted_iota(jnp.int32, (8, CONV_COLS), 0)
        for c0 in range(0, C, CONV_COLS):
            cols = slice(c0, c0 + CONV_COLS)
            h0 = jnp.where(last_tile, 0.0, halo_ref[0:1, cols])
            h1 = jnp.where(last_tile, 0.0, halo_ref[1:2, cols])
            dc_v = dc_ref[:, cols]
            r1, r2 = pltpu.roll(dc_v, tm - 1, 0), pltpu.roll(dc_v, tm - 2, 0)
            n1 = jnp.concatenate([r1[:tm - 8], jnp.where(row == 7, h0, r1[tm - 8:])], axis=0)
            n2 = jnp.concatenate([r2[:tm - 8], jnp.where(row == 7, h1, jnp.where(row == 6, h0, r2[tm - 8:]))], axis=0)
            da_ref[:, cols] = (cw_ref[2:3, cols] * dc_v + cw_ref[1:2, cols] * n1
                               + cw_ref[0:1, cols] * n2).astype(da_ref.dtype)
        xv = x_ref[...]
        dx, dgc = _rms_bwd(_nt_chunks(da_ref, w_ref), xv, g_ref[...], _rms(xv))
        dx_ref[...] = dr_ref[...] + dx
        _acc_rows(dg_ref, 0, dgc)

    row_n = BS((tm, N), lambda i: (i, 0))
    return _pcall(body, name="conv_transpose_rms_bwd", grid=(nt,),
                  in_specs=[BS((tm, C), lambda i: (i, 0)),
                            BS((8, C), lambda i: (jnp.minimum((i + 1) * halo_blocks, S // 8 - 1), 0)),
                            BS((3, C), lambda i: (0, 0)), _resident((Q, N, Kc)), row_n, BS((1, N), lambda i: (0, 0)),
                            row_n],
                  out_specs=[BS((tm, C), lambda i: (i, 0)), row_n, BS((8, N), lambda i: (0, 0))],
                  out_shape=[SDS((S, C), MXU_DTYPE), SDS((S, N), F32), SDS((8, N), F32)])(dc, dc, cw, w3, x, g, dres)


def xattn_block_bwd(dx2, wo3, qx, kn, vb, gxq, wq3, x1, g, wout3, attn, gm, gao, ggo, after=None):
    S, D = qx.shape
    tm = _tile(S, (512, 256))
    hw = D // 2

    def body(dx2_ref, wo_ref, q_ref, kn_ref, vb_ref, gxq_ref, wq_ref, x_ref, g_ref, wout_ref, at_ref, gm_ref,
             gao_ref, ggo_ref, dq_ref, dx_ref, dkn_ref, dv_ref, dgq_ref, dg_ref, da_ref, dgm_ref, dgy_ref):
        @pl.when(pl.program_id(0) == 0)
        def _():
            for ref in (dkn_ref, dv_ref, dgq_ref, dg_ref, dgy_ref):
                ref[...] = jnp.zeros_like(ref)

        gq = gxq_ref[...]
        do_all = _nt_chunks(dx2_ref, wo_ref)
        for h in range(XA_HEADS):
            sl = slice(h * XA_DH, (h + 1) * XA_DH)
            qh, do = q_ref[:, sl], do_all[:, sl]
            r, qn, p = _xa_probs(qh, gq, kn_ref[:, sl])
            dp = _dot(do, vb_ref[:, sl], NT)
            ds = p * (dp - jnp.sum(dp * p, axis=1, keepdims=True)) * (1.0 / math.sqrt(XA_DH))
            dqn = _dot(ds, kn_ref[:, sl])
            dkn_ref[:, sl] += _dot(ds, qn, TN)
            dv_ref[:, sl] += _dot(p, do, TN)
            dqh, dgc = _rms_bwd(dqn, qh, gq, r)
            dq_ref[:, sl] = dqh.astype(dq_ref.dtype)
            _acc_rows(dgq_ref, 0, dgc)
        xv = x_ref[...]
        dx, dgc = _rms_bwd(_nt_chunks(dq_ref, wq_ref), xv, g_ref[...], _rms(xv))
        dx1 = dx2_ref[...] + dx
        dx_ref[...] = dx1
        _acc_rows(dg_ref, 0, dgc)
        dy = _dot(dx1, wout_ref[0], NT)
        av, gmv = at_ref[...], gm_ref[...]
        da, dga = _rms_bwd(dy[:, :hw], av, gao_ref[...], _rms(av))
        dgm, dgg = _rms_bwd(dy[:, hw:], gmv, ggo_ref[...], _rms(gmv))
        da_ref[...] = da
        dgm_ref[...] = dgm
        dgy_ref[0:1, :hw] += jnp.sum(dga, axis=0, keepdims=True)
        dgy_ref[0:1, hw:] += jnp.sum(dgg, axis=0, keepdims=True)

    row = BS((tm, D), lambda i: (i, 0))
    half = BS((tm, hw), lambda i: (i, 0))
    full = lambda r, w: BS((r, w), lambda i: (0, 0))
    return _pcall(body, name="xattn_block_bwd", grid=(S // tm,), after=after,
                  in_specs=[row, _resident(wo3.shape), row, full(MEM_LEN, D), full(MEM_LEN, D), full(1, XA_DH),
                            _resident(wq3.shape), row, full(1, D), _resident(wout3.shape), half, half, full(1, hw),
                            full(1, hw)],
                  out_specs=[row, row, full(MEM_LEN, D), full(MEM_LEN, D), full(8, XA_DH), full(8, D), half, half,
                             full(8, D)],
                  out_shape=[SDS((S, D), MXU_DTYPE), SDS((S, D), F32), SDS((MEM_LEN, D), F32), SDS((MEM_LEN, D), F32),
                             SDS((8, XA_DH), F32), SDS((8, D), F32), SDS((S, hw), F32), SDS((S, hw), F32),
                             SDS((8, D), F32)])(dx2, wo3, qx, kn, vb, gxq, wq3, x1, g, wout3, attn, gm, gao, ggo)


def mem_bwd(kv, dkn, dvb, gxk, after=None):
    def body(kv_ref, dkn_ref, dv_ref, g_ref, dkv_ref, dg_ref):
        dg_ref[...] = jnp.zeros_like(dg_ref)
        for h in range(XA_HEADS):
            sl = slice(h * XA_DH, (h + 1) * XA_DH)
            k = kv_ref[:, sl]
            dk, dgc = _rms_bwd(dkn_ref[:, sl], k, g_ref[...], _rms(k))
            dkv_ref[:, sl] = dk.astype(dkv_ref.dtype)
            _acc_rows(dg_ref, 0, dgc)
        dkv_ref[:, 1024:2048] = dv_ref[...].astype(dkv_ref.dtype)

    full = lambda r, w: BS((r, w), lambda i: (0, 0))
    return _pcall(body, name="mem_bwd", grid=(1,), after=after,
                  in_specs=[full(MEM_LEN, 2048), full(MEM_LEN, 1024), full(MEM_LEN, 1024), full(1, XA_DH)],
                  out_specs=[full(MEM_LEN, 2048), full(8, XA_DH)],
                  out_shape=[SDS((MEM_LEN, 2048), MXU_DTYPE), SDS((8, XA_DH), F32)])(kv, dkn, dvb, gxk)


def _norm_rope_bwd(slabs, douts, g, bm, cos_v, sin_v, first):
    dqns = [d * cos_v + _half_swap(d * sin_v, first) for d in douts]
    rs = [lax.rsqrt(ms + EPS) for ms in _head_means([x * x for x in slabs], bm)]
    projs = _head_means([dqn * g * x for dqn, x in zip(dqns, slabs)], bm)
    dxs = [r * (dqn * g) - x * (r * r * r) * pr for x, dqn, r, pr in zip(slabs, dqns, rs, projs)]
    return dxs, [dqn * x * r for x, dqn, r in zip(slabs, dqns, rs)]


def mixer_core_bwd(proj, cos, sin, gq, gk, gvg, bmat, qr, kr, vb, sinkcol, dattn, dgm, gvn, gu, w2, w2t, bsl,
                   after=None):
    S = qr.shape[0]
    nb = S // BLK

    def body(p_ref, c_ref, s_ref, gq_ref, gk_ref, gvg_ref, b_ref, q_ref, kc_ref, kp_ref, vc_ref, vp_ref, sk_ref,
             do_ref, dgm_ref, gvn_ref, gu_ref, w2_ref, w2t_ref, bsl_ref,
             dp_ref, dsk_ref, dws_ref, dbl_ref, dgq_ref, dgk_ref, dgv_ref,
             carry_k, carry_v, done_k, done_v, dq_keep, dgu_keep, dgvn_keep):
        n = pl.program_id(0)

        @pl.when(n == 0)
        def _():
            for ref in (dsk_ref, dws_ref, dbl_ref, dgq_ref, dgk_ref, dgv_ref, carry_k, carry_v, dq_keep, dgu_keep,
                        dgvn_keep):
                ref[...] = jnp.zeros_like(ref)

        live = (n < nb).astype(F32)
        cos_v, sin_v, bm = c_ref[...], s_ref[...], b_ref[...]
        first = (_lane((BLK, 128)) & 63) < 32
        lo = _lane((BLK, 128)) < 64

        dxs, dgs = _norm_rope_bwd([p_ref[:, s * 128:(s + 1) * 128] for s in range(4)],
                                  [dq_keep[:, s * 128:(s + 1) * 128] for s in range(4)], gq_ref[...], bm,
                                  cos_v, sin_v, first)
        for s, (dx, dg) in enumerate(zip(dxs, dgs)):
            dp_ref[:, s * 128:(s + 1) * 128] = dx.astype(dp_ref.dtype)
            _acc_rows(dgq_ref, 0, dg)
        dp_ref[:, 1024:1536] = (dgu_keep[...] * _gelu_grad(p_ref[:, 1024:1536])).astype(dp_ref.dtype)
        gv, gv_grad = _gelu_and_grad(p_ref[:, 1536:2048])
        dgv, dgc = _rms_bwd(dgvn_keep[...], gv, gvg_ref[...], _rms(gv))
        dp_ref[:, 1536:2048] = (dgv * gv_grad).astype(dp_ref.dtype)
        _acc_rows(dgv_ref, 0, dgc)

        for h in range(2):
            hs, qs = slice(h * 128, (h + 1) * 128), slice(h * 256, (h + 1) * 256)
            kd = jnp.concatenate([kp_ref[:, hs], kc_ref[:, hs]], axis=0)
            vd = jnp.concatenate([vp_ref[:, hs], vc_ref[:, hs]], axis=0)
            sink = jnp.concatenate([sk_ref[2 * h], sk_ref[2 * h + 1]], axis=0)
            qp, p, psink = _swa_probs(q_ref[:, qs], kd, sink, n, lo)
            dop = _by_head(do_ref[:, qs], lo)
            dp = _dot(dop, vd, NT)
            delta = jnp.sum(dp * p, axis=1, keepdims=True)
            ds = p * (dp - delta) * (1.0 / math.sqrt(HEAD_DIM))
            dsink = -psink * delta * live
            dsk_ref[2 * h] += dsink[:2 * BLK]
            dsk_ref[2 * h + 1] += dsink[2 * BLK:]
            dq_keep[:, qs] = _from_heads(_dot(ds, kd), lo)
            dkd = _dot(ds, qp, TN)
            dvd = _dot(p, dop, TN)
            done_k[:, hs] = carry_k[:, hs] + live * dkd[:BLK]
            done_v[:, hs] = carry_v[:, hs] + live * dvd[:BLK]
            carry_k[:, hs] = dkd[BLK:]
            carry_v[:, hs] = dvd[BLK:]
        for j in range(4):
            sl = slice(j * 128, (j + 1) * 128)
            gvn_s = gvn_ref[:, sl]
            m2 = _dot(w2_ref[j], gvn_s)
            mixed = jnp.where(lo, m2[:BLK], m2[BLK:]) + bsl_ref[j]
            dgm_s = dgm_ref[:, sl]
            dgu_keep[:, sl] = dgm_s * mixed
            dmx = dgm_s * gu_ref[:, sl] * live
            d2 = _dot(w2t_ref[j], dmx)
            dgvn_keep[:, sl] = jnp.where(lo, d2[:BLK], d2[BLK:])
            z = jnp.zeros_like(dmx)
            dws_ref[2 * j] += _dot(jnp.where(lo, dmx, z), gvn_s, NT)
            dws_ref[2 * j + 1] += _dot(jnp.where(lo, z, dmx), gvn_s, NT)
            dbl_ref[j] += dmx

        dxs, dgs = _norm_rope_bwd([p_ref[:, 512 + s * 128:640 + s * 128] for s in range(2)],
                                  [done_k[:, s * 128:(s + 1) * 128] for s in range(2)], gk_ref[...], bm,
                                  cos_v, sin_v, first)
        for s, (dx, dg) in enumerate(zip(dxs, dgs)):
            dp_ref[:, 512 + s * 128:640 + s * 128] = dx.astype(dp_ref.dtype)
            _acc_rows(dgk_ref, 0, dg)
        dp_ref[:, 768:1024] = done_v[...].astype(dp_ref.dtype)

    last = nb - 1
    cur = lambda w: BS((BLK, w), lambda n: (jnp.minimum(n, last), 0))
    prev = lambda w: BS((BLK, w), lambda n: (jnp.clip(n - 1, 0, last), 0))
    done = lambda w: BS((BLK, w), lambda n: (jnp.maximum(n - 1, 0), 0))
    const = lambda *shape: BS(shape, lambda n: (0,) * len(shape))
    return _pcall(body, name="mixer_core_bwd", grid=(nb + 1,), after=after,
                  in_specs=[done(IN_COLS_DUP), done(128), done(128), const(1, 128), const(1, 128), const(1, 512),
                            const(128, 128), cur(512), cur(256), prev(256), cur(256), prev(256),
                            const(4, 2 * BLK, 1), cur(512), cur(512), cur(512), cur(512), const(4, 2 * BLK, BLK),
                            const(4, 2 * BLK, BLK), const(4, BLK, 128)],
                  out_specs=[done(IN_COLS_DUP), const(4, 2 * BLK, 1), const(8, BLK, BLK), const(4, BLK, 128),
                             const(8, 128), const(8, 128), const(8, 512)],
                  out_shape=[SDS((S, IN_COLS_DUP), MXU_DTYPE), SDS((4, 2 * BLK, 1), F32), SDS((8, BLK, BLK), F32),
                             SDS((4, BLK, 128), F32), SDS((8, 128), F32), SDS((8, 128), F32), SDS((8, 512), F32)],
                  scratch=[pltpu.VMEM((BLK, 256), F32)] * 4 + [pltpu.VMEM((BLK, 512), F32)] * 3)(
        proj, cos, sin, gq, gk, gvg, bmat, qr, kr, kr, vb, vb, sinkcol, dattn, dgm, gvn, gu, w2, w2t, bsl)


BIG = (("w_in", (1024, 448), True), ("w_out", (256, 1024), False), ("xa_wq", (256, 1024), False),
       ("xa_wkv", (1024, 512), True), ("xa_wo", (256, 1024), False), ("ffn_up", (1024, 1408), True),
       ("ffn_down", (704, 1024), False))
BIG_NAMES = tuple(n for n, _, _ in BIG)
SMALL_VECS = (("mix_norm", 1024), ("q_norm", 64), ("k_norm", 64), ("attn_sinks", 8), ("gmlp_v_norm", 512),
              ("attn_out_norm", 512), ("gmlp_out_norm", 512), ("xa_norm", 1024), ("mem_norm", 1024),
              ("xa_q_norm", 256), ("xa_k_norm", 256), ("ffn_norm", 1024), ("ffn_conv_b", 5632))
SMALL = tuple(n for n, _ in SMALL_VECS) + ("gmlp_bs", "gmlp_ws", "ffn_conv")
WEIGHTS = ("mix_norm", "w_in", "q_norm", "k_norm", "attn_sinks", "gmlp_v_norm", "gmlp_ws", "gmlp_bs",
           "attn_out_norm", "gmlp_out_norm", "w_out", "xa_norm", "mem_norm", "xa_wq", "xa_wkv", "xa_q_norm",
           "xa_k_norm", "xa_wo", "ffn_norm", "ffn_up", "ffn_conv", "ffn_conv_b", "ffn_down")
CONV_SHARD = (3, 1408)
CONV_LANE_ROWS = CONV_SHARD[1] // 128
CONV_CHIP_ROWS = 40


def _small_rows():
    rows, r = {}, 0
    for n, length in SMALL_VECS:
        rows[n] = r
        r += -(-length // 128)
    r += -r % 8
    rows["gmlp_bs"] = r
    r += 8
    rows["gmlp_ws"] = r
    r += 8 * BLK
    rows["ffn_conv"] = r
    r += N_CHIPS * CONV_CHIP_ROWS
    return rows, r


SMALL_ROW, SMALL_ROWS = _small_rows()


def pack_small(dg_mix, dgq, dgk, dsk, dg_gvn, dg_y, dg_xa, dg_mem, dg_xq, dg_xk, dg_ffn, gcw, dbl, dws):
    def body(mix_ref, q_ref, k_ref, sk_ref, gvn_ref, y_ref, xa_ref, mem_ref, xq_ref, xk_ref, ffn_ref, cw_ref,
             dbl_ref, dws_ref, o_ref):
        o_ref[...] = jnp.zeros_like(o_ref)
        lane = _lane((1, 128))

        def put(name, src_ref, row, lane0, length):
            for k in range(length // 128):
                o_ref[SMALL_ROW[name] + k:SMALL_ROW[name] + k + 1, :] = src_ref[row:row + 1, lane0 + k * 128:lane0 + (k + 1) * 128]

        put("mix_norm", mix_ref, 0, 0, 1024)
        for name, ref in (("q_norm", q_ref), ("k_norm", k_ref)):
            v = ref[0:1, :]
            o_ref[SMALL_ROW[name]:SMALL_ROW[name] + 1, :] = jnp.where(lane < HEAD_DIM, v + pltpu.roll(v, 64, 1), 0.0)
        sinks = jnp.zeros((1, 128), F32)
        for s in range(4):
            col = sk_ref[s]
            sinks = sinks + jnp.where(lane == 2 * s, jnp.sum(col[:BLK]), 0.0) + jnp.where(lane == 2 * s + 1, jnp.sum(col[BLK:]), 0.0)
        o_ref[SMALL_ROW["attn_sinks"]:SMALL_ROW["attn_sinks"] + 1, :] = sinks
        put("gmlp_v_norm", gvn_ref, 0, 0, 512)
        put("attn_out_norm", y_ref, 0, 0, 512)
        put("gmlp_out_norm", y_ref, 0, 512, 512)
        put("xa_norm", xa_ref, 0, 0, 1024)
        put("mem_norm", mem_ref, 0, 0, 1024)
        put("xa_q_norm", xq_ref, 0, 0, 256)
        put("xa_k_norm", xk_ref, 0, 0, 256)
        put("ffn_norm", ffn_ref, 0, 0, 1024)
        put("ffn_conv_b", cw_ref, 3, 0, 2 * D_FF)
        r8 = lax.broadcasted_iota(jnp.int32, (8, 128), 0)
        l8 = _lane((8, 128))
        bs = jnp.zeros((8, BLK), F32)
        for j in range(4):
            sel = (((r8 == 2 * j) & (l8 < 64)) | ((r8 == 2 * j + 1) & (l8 >= 64))).astype(F32).astype(BF16)
            xj = dbl_ref[j]
            hi = xj.astype(BF16)
            lo = (xj - hi.astype(F32)).astype(BF16)
            bs = bs + lax.dot_general(sel, hi, NT, preferred_element_type=F32) + lax.dot_general(sel, lo, NT, preferred_element_type=F32)
        o_ref[SMALL_ROW["gmlp_bs"]:SMALL_ROW["gmlp_bs"] + 8, :] = bs
        causal = lax.broadcasted_iota(jnp.int32, (BLK, BLK), 0) >= lax.broadcasted_iota(jnp.int32, (BLK, BLK), 1)
        for h in range(8):
            r0 = SMALL_ROW["gmlp_ws"] + h * BLK
            o_ref[r0:r0 + BLK, :] = jnp.where(causal, dws_ref[h], 0.0)
        for q in range(N_CHIPS):
            for j in range(3):
                for k in range(CONV_LANE_ROWS):
                    r0 = SMALL_ROW["ffn_conv"] + q * CONV_CHIP_ROWS + j * CONV_LANE_ROWS + k
                    l0 = (q * CONV_LANE_ROWS + k) * 128
                    o_ref[r0:r0 + 1, :] = cw_ref[j:j + 1, l0:l0 + 128]

    args = (dg_mix, dgq, dgk, dsk, dg_gvn, dg_y, dg_xa, dg_mem, dg_xq, dg_xk, dg_ffn, gcw, dbl, dws)
    full = lambda a: BS(a.shape, lambda i, nd=a.ndim: (0,) * nd)
    return _pcall(body, name="pack_small", grid=(1,), in_specs=[full(a) for a in args],
                  out_specs=BS((SMALL_ROWS, 128), lambda i: (0, 0)), out_shape=SDS((SMALL_ROWS, 128), F32))(*args)


def _adam(w, g, m, v):
    mn = ADAM_B1 * m + (1.0 - ADAM_B1) * g
    vn = ADAM_B2 * v + (1.0 - ADAM_B2) * (g * g)
    m_hat = mn / (1.0 - ADAM_B1 ** ADAM_STEP)
    v_hat = vn / (1.0 - ADAM_B2 ** ADAM_STEP)
    return -ADAM_LR * (m_hat / (jnp.sqrt(v_hat) + ADAM_EPS) + ADAM_WD * w), mn, vn


def adamw_small(gsum, w, m, v, chipvec):
    n = len(SMALL)

    def body(chip_ref, g_ref, *refs):
        w_refs, m_refs, v_refs = refs[:n], refs[n:2 * n], refs[2 * n:3 * n]
        outs = refs[3 * n:]
        go, do, mo, vo = outs[:n], outs[n:2 * n], outs[2 * n:3 * n], outs[3 * n:]

        def update(i, idx, g):
            d, mn, vn = _adam(w_refs[i][idx], g, m_refs[i][idx], v_refs[i][idx])
            go[i][idx] = g
            do[i][idx] = d
            mo[i][idx] = mn
            vo[i][idx] = vn

        for i, (name, length) in enumerate(SMALL_VECS):
            for k in range(-(-length // 128)):
                wd = min(128, length - k * 128)
                r = SMALL_ROW[name] + k
                update(i, (slice(0, 1), slice(k * 128, k * 128 + wd)), g_ref[r:r + 1, 0:wd])
        i_bs, i_ws, i_cv = len(SMALL_VECS), len(SMALL_VECS) + 1, len(SMALL_VECS) + 2
        update(i_bs, (0,), g_ref[SMALL_ROW["gmlp_bs"]:SMALL_ROW["gmlp_bs"] + 8, :])
        for h in range(8):
            r0 = SMALL_ROW["gmlp_ws"] + h * BLK
            update(i_ws, (0, h), g_ref[r0:r0 + BLK, :])
        mine = g_ref[pl.ds(pl.multiple_of(SMALL_ROW["ffn_conv"] + chip_ref[0] * CONV_CHIP_ROWS, 8), CONV_CHIP_ROWS), :]
        for j in range(3):
            for k in range(CONV_LANE_ROWS):
                r = j * CONV_LANE_ROWS + k
                update(i_cv, (0, slice(j, j + 1), slice(k * 128, (k + 1) * 128)), mine[r:r + 1, :])

    nat = [w[nm] for nm in SMALL]
    full = lambda a: BS(a.shape, lambda i, c, nd=a.ndim: (0,) * nd)
    outs = _pcall(body, name="adamw_small", grid=(1,), prefetch=1,
                  in_specs=[BS((SMALL_ROWS, 128), lambda i, c: (0, 0))] + [full(a) for a in nat] * 3,
                  out_specs=[full(a) for a in nat] * 4, out_shape=[SDS(a.shape, F32) for a in nat] * 4)(
        chipvec, gsum, *nat, *[m[nm] for nm in SMALL], *[v[nm] for nm in SMALL])
    return outs[:n], outs[n:2 * n], outs[2 * n:3 * n], outs[3 * n:]


def adamw_matrix(w, m, v, g_own, g_other, cvec, *, name):
    _, r, c = w.shape
    half = r // 2
    tr = _tile(half, (256, 176, 128))
    T = half // tr

    def body(c_ref, w_ref, m_ref, v_ref, own_ref, oth_ref, g_ref, d_ref, mo_ref, vo_ref):
        g = jnp.where(pl.program_id(0) == c_ref[0], own_ref[...], oth_ref[...])
        d, mn, vn = _adam(w_ref[...], g, m_ref[...], v_ref[...])
        g_ref[...] = g
        d_ref[...] = d
        mo_ref[...] = mn
        vo_ref[...] = vn

    nat = BS((None, tr, c), lambda hf, t, cr: (0, hf * T + t, 0))
    hlf = BS((tr, c), lambda hf, t, cr: (t, 0))
    return _pcall(body, name=name, grid=(2, T), prefetch=1, in_specs=[nat, nat, nat, hlf, hlf], out_specs=[nat] * 4,
                  out_shape=[SDS(w.shape, F32)] * 4)(cvec, w, m, v, g_own, g_other)


def _place():
    return lax.axis_index("x"), lax.axis_index("y"), lax.axis_index("c")


def _other_chips(x, y):
    return [(1 - x, y), (x, 1 - y), (1 - x, 1 - y)]


def _rows_of_core(c, half):
    return pl.ds(pl.multiple_of(c * half, 16), half)


def _rcopy(src, dst, sems, k, to):
    return pltpu.make_async_remote_copy(src_ref=src, dst_ref=dst, send_sem=sems[0].at[k], recv_sem=sems[1].at[k],
                                        device_id=to, device_id_type=MESH)


def cast_shards(shards, conv, chipvec):
    n = len(shards)

    def body(chip_ref, *refs):
        for i_ref, o_ref in zip(refs[:n + 1], refs[n + 1:]):
            o_ref[...] = i_ref[...].astype(o_ref.dtype)

    in_specs = [BS((s.shape[0] // 4, s.shape[1]), lambda i, p: (i, 0)) for s in shards]
    in_specs.append(BS(conv.shape, lambda i, p: (0, 0)))
    out_specs = [BS((None, s.shape[0] // 4, s.shape[1]), lambda i, p: (p[0], i, 0)) for s in shards]
    out_specs.append(BS((None,) + conv.shape, lambda i, p: (p[0], 0, 0)))
    out_shape = [SDS((N_CHIPS,) + s.shape, MXU_DTYPE) for s in shards] + [SDS((N_CHIPS,) + conv.shape, F32)]
    return _pcall(body, name="cast_shards", grid=(4,), prefetch=1, in_specs=in_specs, out_specs=out_specs,
                  out_shape=out_shape)(chipvec, *shards, conv)


HBM = pl.BlockSpec(memory_space=pltpu.HBM)
SEM = pl.BlockSpec(memory_space=pltpu.SEMAPHORE)
DATAFLOW = pltpu.SideEffectType.DATAFLOW_SIDE_EFFECTING
VMEM_WHOLE = pl.BlockSpec(memory_space=pltpu.VMEM)
TOKEN = jax.ShapeDtypeStruct((8, 128), jnp.float32)


def _gather_copies(bufs, send_sems, recv_sems, outgoing):
    x, y, c = _place()
    p = 2 * x + y
    cps = []
    for i, o in enumerate(bufs):
        for j, (cx, cy) in enumerate(_other_chips(x, y)):
            slot = o.at[p] if outgoing else o.at[2 * cx + cy]
            cps.append(_rcopy(slot, slot, (send_sems, recv_sems), 3 * i + j, (cx, cy, c)))
    return cps


def gather_start(slots, after):
    n = len(slots)

    def body(*refs):
        send_sems, recv_sems, thru, token = refs[n + 1], refs[n + 2], refs[n + 3:2 * n + 3], refs[2 * n + 3]
        for cp in _gather_copies(thru, send_sems, recv_sems, True):
            cp.start()
        token[...] = jnp.zeros_like(token)

    hbm = [pltpu.with_memory_space_constraint(s, pltpu.HBM) for s in slots]
    outs = pl.pallas_call(
        body, name="gather_start_%d" % n,
        out_shape=[pltpu.SemaphoreType.DMA((3 * n,)), pltpu.SemaphoreType.DMA((3 * n,))]
        + [pltpu.HBM(s.shape, s.dtype) for s in slots] + [TOKEN],
        in_specs=[HBM] * n + [ANY], out_specs=[SEM, SEM] + [HBM] * n + [VMEM_WHOLE],
        input_output_aliases={i: 2 + i for i in range(n)},
        compiler_params=pltpu.CompilerParams(has_side_effects=DATAFLOW))(*hbm, after)
    return outs[0], outs[1], outs[2:2 + n], outs[2 + n]


def gather_wait(send_sems, recv_sems, bufs, *after):
    n = len(bufs)

    def body(*refs):
        ins, send_ref, recv_ref = refs[:n], refs[n], refs[n + 1]
        for cp in _gather_copies(ins, send_ref, recv_ref, False):
            cp.wait_send()
            cp.wait_recv()

    return pl.pallas_call(
        body, name="gather_wait_%d" % n, out_shape=[pltpu.HBM(s.shape, s.dtype) for s in bufs],
        in_specs=[HBM] * n + [SEM, SEM] + [ANY] * len(after), out_specs=[HBM] * n,
        input_output_aliases={i: i for i in range(n)},
        compiler_params=pltpu.CompilerParams(has_side_effects=DATAFLOW))(*bufs, send_sems, recv_sems, *after)


def _peers(x, y, c):
    return [(1 - x if k & 4 else x, 1 - y if k & 2 else y, 1 - c if k & 1 else c) for k in range(1, N_DEV)]


def _partial_copies(g_ref, land_ref, send_sems, recv_sems, outgoing):
    x, y, c = _place()
    half = g_ref.shape[1] // 2
    cps = []
    for k, (px, py, pc) in enumerate(_peers(x, y, c)):
        src = g_ref.at[2 * px + py, _rows_of_core(pc, half)]
        dst = land_ref.at[4 * x + 2 * y + c] if outgoing else land_ref.at[4 * px + 2 * py + pc]
        cps.append(_rcopy(src, dst, (send_sems, recv_sems), k, (px, py, pc)))
    return cps


def partials_start(g, *, name):
    land = lax.empty((N_DEV, g.shape[1] // 2, g.shape[2]), g.dtype)

    def body(g_ref, land_ref, send_sems, recv_sems, g_thru, land_thru, token):
        for cp in _partial_copies(g_thru, land_thru, send_sems, recv_sems, True):
            cp.start()
        token[...] = jnp.zeros_like(token)

    return pl.pallas_call(
        body, name=name,
        out_shape=[pltpu.SemaphoreType.DMA((N_DEV - 1,)), pltpu.SemaphoreType.DMA((N_DEV - 1,)),
                   pltpu.HBM(g.shape, g.dtype), pltpu.HBM(land.shape, land.dtype), TOKEN],
        in_specs=[HBM, HBM], out_specs=[SEM, SEM, HBM, HBM, VMEM_WHOLE], input_output_aliases={0: 2, 1: 3},
        compiler_params=pltpu.CompilerParams(has_side_effects=DATAFLOW))(
        pltpu.with_memory_space_constraint(g, pltpu.HBM), pltpu.with_memory_space_constraint(land, pltpu.HBM))


def partials_wait(started, after):
    n = len(started)

    def body(*refs):
        for i in range(n):
            send_ref, recv_ref, g_ref, land_ref = refs[4 * i:4 * i + 4]
            for cp in _partial_copies(g_ref, land_ref, send_ref, recv_ref, False):
                cp.wait_send()
                cp.wait_recv()

    flat = [a for s in started for a in s]
    bufs = [a for s in started for a in s[2:]]
    outs = pl.pallas_call(
        body, name="partials_wait", out_shape=[pltpu.HBM(b.shape, b.dtype) for b in bufs],
        in_specs=[SEM, SEM, HBM, HBM] * n + [ANY], out_specs=[HBM] * (2 * n),
        input_output_aliases={4 * i + 2 + j: 2 * i + j for i in range(n) for j in range(2)},
        compiler_params=pltpu.CompilerParams(has_side_effects=DATAFLOW))(*flat, after)
    return [(outs[2 * i], outs[2 * i + 1]) for i in range(n)]


def sum_partials(pairs, order):
    n = len(pairs)

    def body(o_ref, *refs):
        j = pl.program_id(0)
        for g_ref, l_ref, f_ref in zip(refs[:n], refs[n:2 * n], refs[2 * n:]):
            @pl.when(j == 0)
            def _():
                f_ref[...] = g_ref[...].astype(F32)

            @pl.when(j > 0)
            def _():
                f_ref[...] += l_ref[...].astype(F32)

    g4 = [g.reshape(g.shape[0], 2, g.shape[1] // 2, g.shape[2]) for g, _ in pairs]
    lands = [l for _, l in pairs]
    return _pcall(body, name="sum_partials", grid=(N_DEV,), prefetch=1,
                  in_specs=[BS((None, None) + g.shape[2:], lambda j, o: (o[0], o[1], 0, 0)) for g in g4]
                  + [BS((None,) + l.shape[1:], lambda j, o: (o[jnp.maximum(j, 1) + 1], 0, 0)) for l in lands],
                  out_specs=[BS(l.shape[1:], lambda j, o: (0, 0)) for l in lands],
                  out_shape=[SDS(l.shape[1:], F32) for l in lands])(order, *g4, *lands)


def _pair_copies(f_refs, land_refs, send_sems, recv_sems):
    x, y, c = _place()
    return [_rcopy(f, o, (send_sems, recv_sems), i, (x, y, 1 - c)) for i, (f, o) in enumerate(zip(f_refs, land_refs))]


def pair_start(fs):
    n = len(fs)
    lands = [lax.empty(f.shape, f.dtype) for f in fs]

    def body(*refs):
        send_sems, recv_sems = refs[2 * n], refs[2 * n + 1]
        thru, land_thru, token = refs[2 * n + 2:3 * n + 2], refs[3 * n + 2:4 * n + 2], refs[4 * n + 2]
        for cp in _pair_copies(thru, land_thru, send_sems, recv_sems):
            cp.start()
        token[...] = jnp.zeros_like(token)

    hbm = [pltpu.with_memory_space_constraint(a, pltpu.HBM) for a in list(fs) + lands]
    outs = pl.pallas_call(
        body, name="pair_start",
        out_shape=[pltpu.SemaphoreType.DMA((n,)), pltpu.SemaphoreType.DMA((n,))]
        + [pltpu.HBM(a.shape, a.dtype) for a in list(fs) + lands] + [TOKEN],
        in_specs=[HBM] * (2 * n), out_specs=[SEM, SEM] + [HBM] * (2 * n) + [VMEM_WHOLE],
        input_output_aliases={i: 2 + i for i in range(2 * n)},
        compiler_params=pltpu.CompilerParams(has_side_effects=DATAFLOW))(*hbm)
    return outs[0], outs[1], outs[2:2 + n], outs[2 + n:2 + 2 * n], outs[2 + 2 * n]


def pair_wait(send_sems, recv_sems, fs, lands, after):
    n = len(fs)

    def body(*refs):
        for cp in _pair_copies(refs[:n], refs[n:2 * n], refs[2 * n], refs[2 * n + 1]):
            cp.wait_send()
            cp.wait_recv()

    outs = pl.pallas_call(
        body, name="pair_wait", out_shape=[pltpu.HBM(a.shape, a.dtype) for a in list(fs) + list(lands)],
        in_specs=[HBM] * (2 * n) + [SEM, SEM, ANY], out_specs=[HBM] * (2 * n),
        input_output_aliases={i: i for i in range(2 * n)},
        compiler_params=pltpu.CompilerParams(has_side_effects=DATAFLOW))(*fs, *lands, send_sems, recv_sems, after)
    return outs[:n], outs[n:]


def _small_copies(s_ref, land_ref, send_sems, recv_sems, outgoing):
    x, y, c = _place()
    cps = []
    for k, (px, py, pc) in enumerate(_peers(x, y, c)):
        dst = land_ref.at[4 * x + 2 * y + c] if outgoing else land_ref.at[4 * px + 2 * py + pc]
        cps.append(_rcopy(s_ref, dst, (send_sems, recv_sems), k, (px, py, pc)))
    return cps


def small_start(sm):
    land = lax.empty((N_DEV,) + sm.shape, sm.dtype)

    def body(s_ref, land_ref, send_sems, recv_sems, s_thru, land_thru):
        for cp in _small_copies(s_thru, land_thru, send_sems, recv_sems, True):
            cp.start()

    return pl.pallas_call(
        body, name="small_start",
        out_shape=[pltpu.SemaphoreType.DMA((N_DEV - 1,)), pltpu.SemaphoreType.DMA((N_DEV - 1,)),
                   pltpu.HBM(sm.shape, sm.dtype), pltpu.HBM(land.shape, land.dtype)],
        in_specs=[HBM, HBM], out_specs=[SEM, SEM, HBM, HBM], input_output_aliases={0: 2, 1: 3},
        compiler_params=pltpu.CompilerParams(has_side_effects=DATAFLOW))(
        pltpu.with_memory_space_constraint(sm, pltpu.HBM), pltpu.with_memory_space_constraint(land, pltpu.HBM))


def small_wait(send_sems, recv_sems, sm, land, *after):
    def body(send_ref, recv_ref, s_ref, land_ref, *rest):
        for cp in _small_copies(s_ref, land_ref, send_ref, recv_ref, False):
            cp.wait_send()
            cp.wait_recv()

    return pl.pallas_call(
        body, name="small_wait", out_shape=[pltpu.HBM(sm.shape, sm.dtype), pltpu.HBM(land.shape, land.dtype)],
        in_specs=[SEM, SEM, HBM, HBM] + [ANY] * len(after), out_specs=[HBM, HBM], input_output_aliases={2: 0, 3: 1},
        compiler_params=pltpu.CompilerParams(has_side_effects=DATAFLOW))(send_sems, recv_sems, sm, land, *after)


def sum_small(own, land, mevec):
    n, rows, width = land.shape
    tr = _tile(rows, (184, 8))

    def body(me_ref, own_ref, land_ref, o_ref):
        acc = jnp.zeros((tr, width), F32)
        for s in range(n):
            acc = acc + jnp.where(me_ref[0] == s, own_ref[...], land_ref[s])
        o_ref[...] = acc

    return _pcall(body, name="sum_small", grid=(rows // tr,), prefetch=1,
                  in_specs=[BS((tr, width), lambda i, me: (i, 0)), BS((n, tr, width), lambda i, me: (0, i, 0))],
                  out_specs=BS((tr, width), lambda i, me: (i, 0)), out_shape=SDS((rows, width), F32))(mevec, own, land)


def _to_full(blk, col):
    n, r, c = blk.shape
    return blk.transpose(1, 0, 2).reshape(r, n * c) if col else blk.reshape(n * r, c)


def _dup_cols(w):
    dup = lambda t: jnp.concatenate([t[:, :64], t[:, :64], t[:, 64:], t[:, 64:]], axis=1)
    return jnp.concatenate([w[:, :512], dup(w[:, 512:640]), dup(w[:, 640:768]), w[:, 768:]], axis=1)


def _fold_cols(d):
    fold = lambda t: jnp.concatenate([t[:, 0:64] + t[:, 64:128], t[:, 128:192] + t[:, 192:256]], axis=1)
    return jnp.concatenate([d[:, :512], fold(d[:, 512:768]), fold(d[:, 768:1024]), d[:, 1024:]], axis=1)


def _local_step(x, mem, positions, target, w_in, later, sp, emit):
    gain = lambda n: sp[n].reshape(1, -1)
    half = HEAD_DIM // 2
    inv_freq = 1.0 / (10000.0 ** (jnp.arange(half, dtype=F32) * (2.0 / HEAD_DIM)))
    ang = positions.astype(F32)[:, None] * inv_freq
    cos, sin = jnp.cos(ang), jnp.sin(ang)
    cos128 = jnp.tile(cos, (1, 4))
    sin128 = jnp.concatenate([-sin, sin, -sin, sin], axis=1)
    seg = jnp.arange(128) // HEAD_DIM
    bmat = (seg[:, None] == seg[None, :]).astype(BF16)
    gq128, gk128 = jnp.tile(gain("q_norm"), (1, 2)), jnp.tile(gain("k_norm"), (1, 2))
    sinkcol = jnp.repeat(sp["attn_sinks"].reshape(4, 2), BLK, axis=1).reshape(4, 2 * BLK, 1)
    wsc = sp["gmlp_ws"] * jnp.tril(jnp.ones((BLK, BLK), F32))[None]
    w2 = wsc.reshape(4, 2 * BLK, BLK).astype(MXU_DTYPE)
    w2t = wsc.swapaxes(1, 2).reshape(4, 2 * BLK, BLK).astype(MXU_DTYPE)
    bsl = jnp.repeat(sp["gmlp_bs"].reshape(4, 2, BLK).transpose(0, 2, 1), HEAD_DIM, axis=2)
    cb = sp["ffn_conv_b"].reshape(1, -1)
    w_in_d = _dup_cols(_to_full(w_in(cos128, sin128, gq128, gk128, sinkcol, w2, w2t, bsl), True))[None]

    h1, proj = rms_mm(x, gain("mix_norm"), w_in_d, name="mix_in")
    qr, kr, vb, gu, gvn, attn, gm, y = mixer_core_fwd(proj, cos128, sin128, gq128, gk128, gain("gmlp_v_norm"), bmat,
                                                      sinkcol, gain("attn_out_norm"), w2, bsl, gain("gmlp_out_norm"))
    wf, last = later(y)
    w_out, xa_wq, xa_wo = (_to_full(wf[n], False) for n in ("w_out", "xa_wq", "xa_wo"))
    x1 = mm(y, w_out, res=x, name="mix_out")
    mn, kv = rms_mm(mem, gain("mem_norm"), wf["xa_wkv"], name="xa_kv")
    kn, vbx = mem_pre(kv, gain("xa_k_norm"))
    h2, qx, xo, x2 = xattn_block_fwd(x1, gain("xa_norm"), xa_wq, kn, vbx, gain("xa_q_norm"), xa_wo)
    ffn_w, cw = last(x2)
    wf = {**wf, **ffn_w}
    ffn_down = _to_full(wf["ffn_down"], False)
    h3, a, f, dx3, loss_acc = ffn_fwd_loss(x2, gain("ffn_norm"), wf["ffn_up"], cw, cb, ffn_down, target)

    by_rows = lambda g: g.reshape(N_CHIPS, g.shape[1] // N_CHIPS, g.shape[2])
    sent = emit("ffn_down", by_rows(mm_tn(f, dx3, name="g_ffn_down", out_dtype=WIRE_DTYPE)))
    dc, gcw = convgate_bwd(a, dx3, ffn_down[None], cw, cb, after=sent)
    da, dx2, dg_ffn = conv_transpose_rms_bwd(dc, cw, wf["ffn_up"], x2, gain("ffn_norm"), dx3)
    sent = emit("ffn_up", mm_tn(h3, da, name="g_ffn_up", out_dtype=WIRE_DTYPE, chunks=N_CHIPS))
    sent = emit("xa_wo", by_rows(mm_tn(xo, dx2, name="g_xa_wo", out_dtype=WIRE_DTYPE, after=sent)))
    dqx, dx1, dkn, dvx, dg_xq, dg_xa, dattn, dgm, dg_y = xattn_block_bwd(
        dx2, xa_wo[None], qx, kn, vbx, gain("xa_q_norm"), xa_wq[None], x1, gain("xa_norm"), w_out[None], attn, gm,
        gain("attn_out_norm"), gain("gmlp_out_norm"), after=sent)
    sent = emit("xa_wq", by_rows(mm_tn(h2, dqx, name="g_xa_wq", out_dtype=WIRE_DTYPE)))
    dkv, dg_xk = mem_bwd(kv, dkn, dvx, gain("xa_k_norm"), after=sent)
    _, dg_mem = mm_nt_rms_bwd(dkv, wf["xa_wkv"], mem, gain("mem_norm"), jnp.zeros_like(mem), name="d_mem")
    sent = emit("xa_wkv", mm_tn(mn, dkv, name="g_xa_wkv", out_dtype=WIRE_DTYPE, chunks=N_CHIPS))
    sent = emit("w_out", by_rows(mm_tn(y, dx1, name="g_w_out", out_dtype=WIRE_DTYPE, after=sent)))
    dproj, dsk, dws, dbl, dgq, dgk, dg_gvn = mixer_core_bwd(
        proj, cos128, sin128, gq128, gk128, gain("gmlp_v_norm"), bmat, qr, kr, vb, sinkcol, dattn, dgm, gvn, gu,
        w2, w2t, bsl, after=sent)
    g_in = _fold_cols(mm_tn(h1, dproj, name="g_w_in", out_dtype=F32)[0])
    sent = emit("w_in", g_in.reshape(1024, N_CHIPS, 448).transpose(1, 0, 2).astype(WIRE_DTYPE))
    grad_x, dg_mix = mm_nt_rms_bwd(dproj, w_in_d, x, gain("mix_norm"), dx1, name="d_x", tm=1024, after=sent)
    packed = pack_small(dg_mix, dgq, dgk, dsk, dg_gvn, dg_y, dg_xa, dg_mem, dg_xq, dg_xk, dg_ffn, gcw, dbl, dws)
    return loss_acc, grad_x, packed


def _gather_step(w, chipvec):
    slots = cast_shards([w[n][0] for n in BIG_NAMES], w["ffn_conv"][0], chipvec)
    send_a, recv_a, first, token = gather_start(slots[:1], chipvec)
    send_b, recv_b, mid, token = gather_start(slots[1:5], token)
    send_c, recv_c, rest, token = gather_start(slots[5:], token)

    def w_in(*after):
        return gather_wait(send_a, recv_a, first, token, *after)[0]

    def last(after):
        got = gather_wait(send_c, recv_c, rest, after)
        return dict(zip(BIG_NAMES[5:], got[:-1])), _to_full(got[-1], True)

    def later(after):
        return dict(zip(BIG_NAMES[1:5], gather_wait(send_b, recv_b, mid, after))), last

    return w_in, later, token


def _reduce_update(started, packed, w, m, v, chipvec, cvec, order):
    small_sent = small_start(packed)
    own = sum_partials(partials_wait([started[n] for n in BIG_NAMES], small_sent[2]), order)
    pair_send, pair_recv, own, lands, pair_started = pair_start(own)
    own, other = pair_wait(pair_send, pair_recv, own, lands, pair_started)
    res = [{}, {}, {}, {}]
    for n, g_own, g_other in zip(BIG_NAMES, own, other):
        for d, o in zip(res, adamw_matrix(w[n], m[n], v[n], g_own, g_other, cvec, name="adamw_" + n)):
            d[n] = o
    mevec = (2 * order[0:1] + order[1:2]).astype(jnp.int32)
    small_sum = sum_small(*small_wait(*small_sent, *[res[3][n] for n in BIG_NAMES]), mevec)
    for d, outs in zip(res, adamw_small(small_sum, w, m, v, chipvec)):
        d.update(zip(SMALL, outs))
    return res


def kernel(x, mem, positions, mix_norm, w_in, q_norm, k_norm, attn_sinks, gmlp_v_norm, gmlp_ws, gmlp_bs, attn_out_norm, gmlp_out_norm, w_out, xa_norm, mem_norm, xa_wq, xa_wkv, xa_q_norm, xa_k_norm, xa_wo, ffn_norm, ffn_up, ffn_conv, ffn_conv_b, ffn_down, loss_target, m_mix_norm, m_w_in, m_q_norm, m_k_norm, m_attn_sinks, m_gmlp_v_norm, m_gmlp_ws, m_gmlp_bs, m_attn_out_norm, m_gmlp_out_norm, m_w_out, m_xa_norm, m_mem_norm, m_xa_wq, m_xa_wkv, m_xa_q_norm, m_xa_k_norm, m_xa_wo, m_ffn_norm, m_ffn_up, m_ffn_conv, m_ffn_conv_b, m_ffn_down, v_mix_norm, v_w_in, v_q_norm, v_k_norm, v_attn_sinks, v_gmlp_v_norm, v_gmlp_ws, v_gmlp_bs, v_attn_out_norm, v_gmlp_out_norm, v_w_out, v_xa_norm, v_mem_norm, v_xa_wq, v_xa_wkv, v_xa_q_norm, v_xa_k_norm, v_xa_wo, v_ffn_norm, v_ffn_up, v_ffn_conv, v_ffn_conv_b, v_ffn_down):
    w = dict(mix_norm=mix_norm, w_in=w_in, q_norm=q_norm, k_norm=k_norm, attn_sinks=attn_sinks, gmlp_v_norm=gmlp_v_norm, gmlp_ws=gmlp_ws, gmlp_bs=gmlp_bs, attn_out_norm=attn_out_norm, gmlp_out_norm=gmlp_out_norm, w_out=w_out, xa_norm=xa_norm, mem_norm=mem_norm, xa_wq=xa_wq, xa_wkv=xa_wkv, xa_q_norm=xa_q_norm, xa_k_norm=xa_k_norm, xa_wo=xa_wo, ffn_norm=ffn_norm, ffn_up=ffn_up, ffn_conv=ffn_conv, ffn_conv_b=ffn_conv_b, ffn_down=ffn_down)
    m = dict(mix_norm=m_mix_norm, w_in=m_w_in, q_norm=m_q_norm, k_norm=m_k_norm, attn_sinks=m_attn_sinks, gmlp_v_norm=m_gmlp_v_norm, gmlp_ws=m_gmlp_ws, gmlp_bs=m_gmlp_bs, attn_out_norm=m_attn_out_norm, gmlp_out_norm=m_gmlp_out_norm, w_out=m_w_out, xa_norm=m_xa_norm, mem_norm=m_mem_norm, xa_wq=m_xa_wq, xa_wkv=m_xa_wkv, xa_q_norm=m_xa_q_norm, xa_k_norm=m_xa_k_norm, xa_wo=m_xa_wo, ffn_norm=m_ffn_norm, ffn_up=m_ffn_up, ffn_conv=m_ffn_conv, ffn_conv_b=m_ffn_conv_b, ffn_down=m_ffn_down)
    v = dict(mix_norm=v_mix_norm, w_in=v_w_in, q_norm=v_q_norm, k_norm=v_k_norm, attn_sinks=v_attn_sinks, gmlp_v_norm=v_gmlp_v_norm, gmlp_ws=v_gmlp_ws, gmlp_bs=v_gmlp_bs, attn_out_norm=v_attn_out_norm, gmlp_out_norm=v_gmlp_out_norm, w_out=v_w_out, xa_norm=v_xa_norm, mem_norm=v_mem_norm, xa_wq=v_xa_wq, xa_wkv=v_xa_wkv, xa_q_norm=v_xa_q_norm, xa_k_norm=v_xa_k_norm, xa_wo=v_xa_wo, ffn_norm=v_ffn_norm, ffn_up=v_ffn_up, ffn_conv=v_ffn_conv, ffn_conv_b=v_ffn_conv_b, ffn_down=v_ffn_down)
    ix, iy, ic = lax.axis_index("x"), lax.axis_index("y"), lax.axis_index("c")
    chip = 2 * ix + iy
    chipvec = chip.astype(jnp.int32).reshape(1)
    cvec = ic.astype(jnp.int32).reshape(1)
    order = jnp.stack([chip, ic] + [4 * px + 2 * py + pc for px, py, pc in _peers(ix, iy, ic)]).astype(jnp.int32)

    w_in_all, later, token = _gather_step(w, chipvec)
    zero = token[0, 0]
    sp = {n: w[n][0] + zero for n in SMALL if n != "ffn_conv"}
    positions = positions + zero.astype(jnp.int32)
    started = {}

    def emit(name, g):
        *started[name], token = partials_start(g, name="partials_start_" + name)
        return token

    loss_acc, grad_x, packed = _local_step(x[0], mem[0], positions[0], loss_target[0], w_in_all, later, sp, emit)
    grads, delta, new_m, new_v = _reduce_update(started, packed, w, m, v, chipvec, cvec, order)
    loss = lax.psum(loss_acc[0, 0], ("x", "y", "c"))
    ordered = lambda d: [d[n] for n in WEIGHTS]
    return (loss, grad_x[None], *ordered(grads), *ordered(delta), *ordered(new_m), *ordered(new_v))
```

```python
import math

import jax
import jax.numpy as jnp
from jax import lax
from jax.experimental import pallas as pl
from jax.experimental.pallas import tpu as pltpu

F32 = jnp.float32
BF16 = jnp.bfloat16
MXU_DTYPE = jnp.bfloat16
WIRE_DTYPE = jnp.bfloat16
EPS = 1e-6
VMEM_LIMIT_V7X = 56 * 1024 * 1024

D_MODEL = 1024
HEAD_DIM = 64
BLK = 128
XA_HEADS = 4
XA_DH = 256
MEM_LEN = 256
D_FF = 2816
IN_COLS_DUP = 2048
N_CHIPS = 4
N_DEV = 8

ADAM_LR = 0.001
ADAM_B1 = 0.9
ADAM_B2 = 0.999
ADAM_EPS = 1e-08
ADAM_WD = 0.01
ADAM_STEP = 10

NT = (((1,), (1,)), ((), ()))
TN = (((0,), (0,)), ((), ()))
NN = (((1,), (0,)), ((), ()))
MINF = float(jnp.finfo(jnp.float32).min)
GELU_K0 = math.sqrt(2.0 / math.pi)
GELU_K1 = 0.044715

BS = pl.BlockSpec
SDS = jax.ShapeDtypeStruct
ANY = pl.BlockSpec(memory_space=pl.ANY)
MESH = pl.DeviceIdType.MESH


def _dot(a, b, dims=NN):
    return lax.dot_general(a.astype(MXU_DTYPE), b.astype(MXU_DTYPE), dims, preferred_element_type=F32)


def _segsum(x, bmat):
    hi = x.astype(BF16)
    lo = (x - hi.astype(F32)).astype(BF16)
    return (jnp.dot(hi, bmat, preferred_element_type=F32) + jnp.dot(lo, bmat, preferred_element_type=F32))


def _gelu(x):
    return 0.5 * x * (1.0 + jnp.tanh(GELU_K0 * (x + GELU_K1 * x * x * x)))


def _gelu_grad(x):
    t = jnp.tanh(GELU_K0 * (x + GELU_K1 * x * x * x))
    return 0.5 * (1.0 + t) + 0.5 * x * (1.0 - t * t) * GELU_K0 * (1.0 + 3.0 * GELU_K1 * x * x)


def _gelu_and_grad(x):
    x2 = x * x
    t = jnp.tanh(x * (GELU_K0 * GELU_K1 * x2 + GELU_K0))
    hx = 0.5 * x
    return hx * t + hx, 0.5 * t + 0.5 + hx * (1.0 - t * t) * (3.0 * GELU_K0 * GELU_K1 * x2 + GELU_K0)


def _rms(x):
    return lax.rsqrt(jnp.mean(x * x, axis=-1, keepdims=True) + EPS)


def _rms_bwd(dy, x, g, r):
    dyg = dy * g
    dx = r * dyg - x * (r * r * r) * jnp.mean(dyg * x, axis=-1, keepdims=True)
    return dx, dy * x * r


def _pcall(body, *, name, grid, in_specs, out_specs, out_shape, scratch=(), prefetch=0, after=None):
    params = pltpu.CompilerParams(dimension_semantics=("arbitrary",) * len(grid), vmem_limit_bytes=VMEM_LIMIT_V7X)
    in_specs = list(in_specs)
    kernel_fn = body
    if after is not None:
        n_in = prefetch + len(in_specs)
        in_specs.append(ANY)

        def kernel_fn(*refs):
            return body(*refs[:n_in], *refs[n_in + 1:])

    if prefetch:
        spec = pltpu.PrefetchScalarGridSpec(num_scalar_prefetch=prefetch, grid=grid, in_specs=in_specs,
                                            out_specs=out_specs, scratch_shapes=list(scratch))
        call = pl.pallas_call(kernel_fn, name=name, grid_spec=spec, out_shape=out_shape, compiler_params=params)
    else:
        call = pl.pallas_call(kernel_fn, name=name, grid=grid, in_specs=in_specs, out_specs=out_specs,
                              out_shape=out_shape, scratch_shapes=list(scratch), compiler_params=params)
    return call if after is None else (lambda *args: call(*args, after))


def _tile(n, prefs):
    for p in prefs:
        if p <= n and n % p == 0:
            return p
    return n


def _resident(shape):
    return pl.BlockSpec(shape, lambda *_: (0,) * len(shape), pipeline_mode=pl.Buffered(1))


def _acc_rows(ref, row, val):
    ref[row:row + 1, :] += jnp.sum(val, axis=0, keepdims=True)


def rms_mm(x, g, w3, *, name, tm=1024):
    M, K = x.shape
    Q, _, C = w3.shape
    tm = _tile(M, (tm, 256))

    def body(x_ref, g_ref, w_ref, h_ref, o_ref):
        def write_h():
            xv = x_ref[...]
            h_ref[...] = (xv * _rms(xv) * g_ref[...]).astype(h_ref.dtype)

        if Q == 1:
            write_h()
        else:
            pl.when(pl.program_id(1) == 0)(write_h)
        o_ref[...] = _dot(h_ref[...], w_ref[pl.program_id(1)])

    return _pcall(body, name=name, grid=(M // tm, Q),
                  in_specs=[BS((tm, K), lambda i, j: (i, 0)), BS((1, K), lambda i, j: (0, 0)),
                            _resident((Q, K, C))],
                  out_specs=[BS((tm, K), lambda i, j: (i, 0)), BS((tm, C), lambda i, j: (i, j))],
                  out_shape=[SDS((M, K), MXU_DTYPE), SDS((M, Q * C), F32)])(x, g, w3)


def _nt_chunks(a_ref, w_ref):
    q_n, _, kc = w_ref.shape
    acc = _dot(a_ref[:, 0:kc], w_ref[0], NT)
    for q in range(1, q_n):
        acc = acc + _dot(a_ref[:, q * kc:(q + 1) * kc], w_ref[q], NT)
    return acc


def mm_nt_rms_bwd(a, w3, x, g, dres, *, name, tm=512, after=None):
    M = a.shape[0]
    Q, N, Kc = w3.shape
    tm = _tile(M, (tm, 256))

    def body(a_ref, w_ref, x_ref, g_ref, dr_ref, dx_ref, dg_ref):
        @pl.when(pl.program_id(0) == 0)
        def _():
            dg_ref[...] = jnp.zeros_like(dg_ref)

        xv = x_ref[...]
        dx, dgc = _rms_bwd(_nt_chunks(a_ref, w_ref), xv, g_ref[...], _rms(xv))
        dx_ref[...] = dr_ref[...] + dx
        _acc_rows(dg_ref, 0, dgc)

    row = BS((tm, N), lambda i: (i, 0))
    return _pcall(body, name=name, grid=(M // tm,), after=after,
                  in_specs=[BS((tm, Q * Kc), lambda i: (i, 0)), _resident((Q, N, Kc)), row,
                            BS((1, N), lambda i: (0, 0)), row],
                  out_specs=[row, BS((8, N), lambda i: (0, 0))],
                  out_shape=[SDS((M, N), F32), SDS((8, N), F32)])(a, w3, x, g, dres)


def mm_tn(a, b, *, name, out_dtype, chunks=1, after=None):
    M, K = a.shape
    N = b.shape[1]
    C = N // chunks
    tm = _tile(M, (1024, 256))
    tk = _tile(K, (1408, 1024, 512))
    tn = _tile(C, (1408, 1024, 512))
    per = C // tn
    nm = M // tm

    def body(a_ref, b_ref, o_ref, acc):
        m = pl.program_id(2)

        @pl.when(m == 0)
        def _():
            acc[...] = jnp.zeros_like(acc)

        acc[...] += _dot(a_ref[...], b_ref[...], TN)

        @pl.when(m == nm - 1)
        def _():
            o_ref[...] = acc[...].astype(o_ref.dtype)

    return _pcall(body, name=name, grid=(K // tk, N // tn, nm), after=after,
                  in_specs=[BS((tm, tk), lambda k, n, m: (m, k)), BS((tm, tn), lambda k, n, m: (m, n))],
                  out_specs=BS((None, tk, tn), lambda k, n, m: (n // per, k, n % per)),
                  out_shape=SDS((chunks, K, C), out_dtype), scratch=[pltpu.VMEM((tk, tn), F32)])(a, b)


def _lane(shape):
    return lax.broadcasted_iota(jnp.int32, shape, 1)


def _head_means(slabs, bmat):
    tm = slabs[0].shape[0]
    means = _segsum(jnp.concatenate(slabs, axis=0), bmat) * (1.0 / HEAD_DIM)
    return [means[i * tm:(i + 1) * tm] for i in range(len(slabs))]


def _half_swap(x, first):
    return jnp.where(first, pltpu.roll(x, 96, 1), pltpu.roll(x, 32, 1))


def _by_head(x2, lo):
    z = jnp.zeros((BLK, 128), x2.dtype)
    parts = []
    for s in range(2):
        xs = x2[:, s * 128:(s + 1) * 128]
        parts += [jnp.where(lo, xs, z), jnp.where(lo, z, xs)]
    return jnp.concatenate(parts, axis=0)


def _from_heads(o4, lo):
    return jnp.concatenate([jnp.where(lo, o4[0:BLK], o4[BLK:2 * BLK]),
                            jnp.where(lo, o4[2 * BLK:3 * BLK], o4[3 * BLK:])], axis=1)


def _swa_probs(q2, kd, sink, n, lo):
    qp = _by_head(q2, lo)
    sc = _dot(qp, kd, NT) * (1.0 / math.sqrt(HEAD_DIM))
    r_i = lax.broadcasted_iota(jnp.int32, (4 * BLK, 2 * BLK), 0)
    k_j = lax.broadcasted_iota(jnp.int32, (4 * BLK, 2 * BLK), 1)
    diff = (r_i & (BLK - 1)) + BLK - k_j
    mask = (diff >= 0) & (diff < BLK) & ((k_j >= BLK) | (n > 0))
    sc = jnp.where(mask, sc, MINF)
    m = jnp.maximum(jnp.max(sc, axis=1, keepdims=True), sink)
    p = jnp.exp(sc - m)
    es = jnp.exp(sink - m)
    inv = 1.0 / (jnp.sum(p, axis=1, keepdims=True) + es)
    return qp, p * inv, es * inv


def mixer_core_fwd(proj, cos, sin, gq, gk, gvn, bmat, sinkcol, gao, w2, bsl, ggo):
    S = proj.shape[0]
    sub = 4 if S % (4 * BLK) == 0 else 1

    def body(p_ref, c_ref, s_ref, gq_ref, gk_ref, gvn_ref, b_ref, sk_ref, gao_ref, w2_ref, bsl_ref, ggo_ref,
             qr_ref, kr_ref, vb_ref, gu_ref, gvo_ref, at_ref, gm_ref, y_ref, k_prev, v_prev):
        n = pl.program_id(0)

        @pl.when(n == 0)
        def _():
            k_prev[...] = jnp.zeros_like(k_prev)
            v_prev[...] = jnp.zeros_like(v_prev)

        bm = b_ref[...]
        first = (_lane((BLK, 128)) & 63) < 32
        lo = _lane((BLK, 128)) < 64
        for sb in range(sub):
            rs = slice(sb * BLK, (sb + 1) * BLK)
            cos_v, sin_v = c_ref[rs, :], s_ref[rs, :]
            slabs = [p_ref[rs, s * 128:(s + 1) * 128] for s in range(6)]
            for s, (slab, ms) in enumerate(zip(slabs, _head_means([x * x for x in slabs], bm))):
                qn = slab * lax.rsqrt(ms + EPS) * (gq_ref[...] if s < 4 else gk_ref[...])
                out = qn * cos_v + _half_swap(qn, first) * sin_v
                if s < 4:
                    qr_ref[rs, s * 128:(s + 1) * 128] = out.astype(qr_ref.dtype)
                else:
                    kr_ref[rs, (s - 4) * 128:(s - 3) * 128] = out.astype(kr_ref.dtype)
            vb_ref[rs, :] = p_ref[rs, 768:1024].astype(vb_ref.dtype)
            gu_ref[rs, :] = _gelu(p_ref[rs, 1024:1536])
            gv = _gelu(p_ref[rs, 1536:2048])
            gvo_ref[rs, :] = (gv * _rms(gv) * gvn_ref[...]).astype(gvo_ref.dtype)

            before = slice((sb - 1) * BLK, sb * BLK)
            for h in range(2):
                hs, qs = slice(h * 128, (h + 1) * 128), slice(h * 256, (h + 1) * 256)
                k_before = k_prev[:, hs] if sb == 0 else kr_ref[before, hs]
                v_before = v_prev[:, hs] if sb == 0 else vb_ref[before, hs]
                kd = jnp.concatenate([k_before, kr_ref[rs, hs]], axis=0)
                vd = jnp.concatenate([v_before, vb_ref[rs, hs]], axis=0)
                sink = jnp.concatenate([sk_ref[2 * h], sk_ref[2 * h + 1]], axis=0)
                _, p, _ = _swa_probs(qr_ref[rs, qs], kd, sink, n * sub + sb, lo)
                at_ref[rs, qs] = _from_heads(_dot(p, vd), lo)

            for j in range(4):
                sl = slice(j * 128, (j + 1) * 128)
                m2 = _dot(w2_ref[j], gvo_ref[rs, sl])
                mixed = jnp.where(lo, m2[:BLK], m2[BLK:]) + bsl_ref[j]
                gm_ref[rs, sl] = gu_ref[rs, sl] * mixed
            a, gm = at_ref[rs, :], gm_ref[rs, :]
            y_ref[rs, :512] = (a * _rms(a) * gao_ref[...]).astype(y_ref.dtype)
            y_ref[rs, 512:] = (gm * _rms(gm) * ggo_ref[...]).astype(y_ref.dtype)
        k_prev[...] = kr_ref[(sub - 1) * BLK:, :]
        v_prev[...] = vb_ref[(sub - 1) * BLK:, :]

    row = lambda w: BS((sub * BLK, w), lambda n: (n, 0))
    const = lambda *shape: BS(shape, lambda n: (0,) * len(shape))
    return _pcall(body, name="mixer_core_fwd", grid=(S // (sub * BLK),),
                  in_specs=[row(IN_COLS_DUP), row(128), row(128), const(1, 128), const(1, 128), const(1, 512),
                            const(128, 128), const(4, 2 * BLK, 1), const(1, 512), const(4, 2 * BLK, BLK),
                            const(4, BLK, 128), const(1, 512)],
                  out_specs=[row(512), row(256), row(256), row(512), row(512), row(512), row(512), row(1024)],
                  out_shape=[SDS((S, 512), MXU_DTYPE), SDS((S, 256), MXU_DTYPE), SDS((S, 256), MXU_DTYPE),
                             SDS((S, 512), F32), SDS((S, 512), MXU_DTYPE), SDS((S, 512), F32), SDS((S, 512), F32),
                             SDS((S, 1024), MXU_DTYPE)],
                  scratch=[pltpu.VMEM((BLK, 256), MXU_DTYPE), pltpu.VMEM((BLK, 256), MXU_DTYPE)])(
        proj, cos, sin, gq, gk, gvn, bmat, sinkcol, gao, w2, bsl, ggo)


def mem_pre(kv, gxk):
    def body(kv_ref, g_ref, kn_ref, vb_ref):
        for h in range(XA_HEADS):
            sl = slice(h * XA_DH, (h + 1) * XA_DH)
            k = kv_ref[:, sl]
            kn_ref[:, sl] = (k * _rms(k) * g_ref[...]).astype(kn_ref.dtype)
        vb_ref[...] = kv_ref[:, 1024:2048].astype(vb_ref.dtype)

    full = lambda r, w: BS((r, w), lambda i: (0, 0))
    return _pcall(body, name="mem_pre", grid=(1,), in_specs=[full(MEM_LEN, 2048), full(1, XA_DH)],
                  out_specs=[full(MEM_LEN, 1024), full(MEM_LEN, 1024)],
                  out_shape=[SDS((MEM_LEN, 1024), MXU_DTYPE), SDS((MEM_LEN, 1024), MXU_DTYPE)])(kv, gxk)


def _xa_probs(qh, g, kn_h):
    r = _rms(qh)
    qn = qh * r * g
    s = _dot(qn, kn_h, NT) * (1.0 / math.sqrt(XA_DH))
    p = jnp.exp(s - jnp.max(s, axis=1, keepdims=True))
    return r, qn, p * (1.0 / jnp.sum(p, axis=1, keepdims=True))


def xattn_block_fwd(y, w_out, x, g, wq, kn, vb, gxq, wo):
    S, D = x.shape
    tm = _tile(S, (512, 256))

    def body(y_ref, wout_ref, x_ref, g_ref, wq_ref, kn_ref, vb_ref, gxq_ref, wo_ref, x1_ref, h_ref, q_ref, o_ref,
             x2_ref):
        x1_ref[...] = _dot(y_ref[...], wout_ref[...]) + x_ref[...]
        xv = x1_ref[...]
        h_ref[...] = (xv * _rms(xv) * g_ref[...]).astype(h_ref.dtype)
        q_ref[...] = _dot(h_ref[...], wq_ref[...])
        for h in range(XA_HEADS):
            sl = slice(h * XA_DH, (h + 1) * XA_DH)
            _, _, p = _xa_probs(q_ref[:, sl], gxq_ref[...], kn_ref[:, sl])
            o_ref[:, sl] = _dot(p, vb_ref[:, sl]).astype(o_ref.dtype)
        x2_ref[...] = _dot(o_ref[...], wo_ref[...]) + x1_ref[...]

    row = BS((tm, D), lambda i: (i, 0))
    full = lambda r, w: BS((r, w), lambda i: (0, 0))
    return _pcall(body, name="xattn_block_fwd", grid=(S // tm,),
                  in_specs=[BS((tm, y.shape[1]), lambda i: (i, 0)), _resident(w_out.shape), row, full(1, D),
                            _resident(wq.shape), full(MEM_LEN, D), full(MEM_LEN, D), full(1, XA_DH),
                            _resident(wo.shape)],
                  out_specs=[row, row, row, row, row],
                  out_shape=[SDS((S, D), F32), SDS((S, D), MXU_DTYPE), SDS((S, D), F32), SDS((S, D), MXU_DTYPE),
                             SDS((S, D), F32)])(y, w_out, x, g, wq, kn, vb, gxq, wo)


CONV_COLS = 1408


def _conv_taps(a_ref, halo_ref, w_ref, b_ref, cols, first_tile):
    a = a_ref[:, cols]
    row = lax.broadcasted_iota(jnp.int32, (8, a.shape[1]), 0)
    h6 = jnp.where(first_tile, 0.0, halo_ref[6:7, cols])
    h7 = jnp.where(first_tile, 0.0, halo_ref[7:8, cols])
    r1, r2 = pltpu.roll(a, 1, 0), pltpu.roll(a, 2, 0)
    a1 = jnp.concatenate([jnp.where(row == 0, h7, r1[0:8]), r1[8:]], axis=0)
    a2 = jnp.concatenate([jnp.where(row == 0, h6, jnp.where(row == 1, h7, r2[0:8])), r2[8:]], axis=0)
    c = w_ref[2:3, cols] * a + w_ref[1:2, cols] * a1 + w_ref[0:1, cols] * a2 + b_ref[:, cols]
    return c, (a2, a1, a)


def _conv_specs(tm):
    halo_blocks = tm // 8
    return [BS((tm, D_FF), lambda i: (i, 0)), BS((tm, D_FF), lambda i: (i, 1)),
            BS((8, D_FF), lambda i: (jnp.maximum(i * halo_blocks - 1, 0), 0)),
            BS((8, D_FF), lambda i: (jnp.maximum(i * halo_blocks - 1, 0), 1)),
            BS((3, D_FF), lambda i: (0, 0)), BS((3, D_FF), lambda i: (0, 1)),
            BS((1, D_FF), lambda i: (0, 0)), BS((1, D_FF), lambda i: (0, 1))]


def ffn_fwd_loss(x2, g, w_up3, cw, cb, w_down, target):
    S, D = x2.shape
    Q, _, C = w_up3.shape
    tm = _tile(S, (256,))

    def body(x_ref, g_ref, wu_ref, cw_ref, cb_ref, wd_ref, t_ref, h_ref, a_ref, f_ref, d_ref, l_ref, tail):
        first_tile = pl.program_id(0) == 0

        @pl.when(first_tile)
        def _():
            l_ref[...] = jnp.zeros_like(l_ref)
            tail[...] = jnp.zeros_like(tail)

        xv = x_ref[...]
        h_ref[...] = (xv * _rms(xv) * g_ref[...]).astype(h_ref.dtype)
        for q in range(Q):
            a_ref[:, q * C:(q + 1) * C] = _dot(h_ref[...], wu_ref[q])
        for c0 in range(0, D_FF, CONV_COLS):
            cols, ucols = slice(c0, c0 + CONV_COLS), slice(D_FF + c0, D_FF + c0 + CONV_COLS)
            cg, _ = _conv_taps(a_ref, tail, cw_ref, cb_ref, cols, first_tile)
            cu, _ = _conv_taps(a_ref, tail, cw_ref, cb_ref, ucols, first_tile)
            f_ref[:, cols] = (_gelu(cg) * cu).astype(f_ref.dtype)
        tail[...] = a_ref[tm - 8:tm, :]
        e = _dot(f_ref[...], wd_ref[...]) + xv - t_ref[...]
        d_ref[...] = e * (1.0 / D)
        l_ref[...] += jnp.sum(e * e) * (0.5 / D)

    row = lambda w: BS((tm, w), lambda i: (i, 0))
    const = lambda r, w: BS((r, w), lambda i: (0, 0))
    return _pcall(body, name="ffn_fwd_loss", grid=(S // tm,),
                  in_specs=[row(D), const(1, D), _resident(w_up3.shape), const(3, 2 * D_FF), const(1, 2 * D_FF),
                            _resident(w_down.shape), row(D)],
                  out_specs=[row(D), row(2 * D_FF), row(D_FF), row(D), const(8, 128)],
                  out_shape=[SDS((S, D), MXU_DTYPE), SDS((S, 2 * D_FF), F32), SDS((S, D_FF), MXU_DTYPE),
                             SDS((S, D), F32), SDS((8, 128), F32)],
                  scratch=[pltpu.VMEM((8, 2 * D_FF), F32)])(x2, g, w_up3, cw, cb, w_down, target)


def convgate_bwd(a, dx3, w3, cw, cb, after=None):
    S = a.shape[0]
    tm = _tile(S, (256,))

    def body(ag_ref, au_ref, hg_ref, hu_ref, wg_ref, wu_ref, bg_ref, bu_ref, dx_ref, wd_ref, dc_ref, gw_ref, df_ref):
        first_tile = pl.program_id(0) == 0

        @pl.when(first_tile)
        def _():
            gw_ref[...] = jnp.zeros_like(gw_ref)

        df_ref[...] = _nt_chunks(dx_ref, wd_ref)
        for c0 in range(0, D_FF, CONV_COLS):
            cols, ucols = slice(c0, c0 + CONV_COLS), slice(D_FF + c0, D_FF + c0 + CONV_COLS)
            cg, g_taps = _conv_taps(ag_ref, hg_ref, wg_ref, bg_ref, cols, first_tile)
            cu, u_taps = _conv_taps(au_ref, hu_ref, wu_ref, bu_ref, cols, first_tile)
            df_v = df_ref[:, cols]
            gate, gate_grad = _gelu_and_grad(cg)
            dcg = df_v * cu * gate_grad
            dcu = df_v * gate
            dc_ref[:, cols] = dcg
            dc_ref[:, ucols] = dcu
            for col, dcv, taps in ((cols, dcg, g_taps), (ucols, dcu, u_taps)):
                for j in range(3):
                    gw_ref[j:j + 1, col] += jnp.sum(dcv * taps[j], axis=0, keepdims=True)
                gw_ref[3:4, col] += jnp.sum(dcv, axis=0, keepdims=True)

    return _pcall(body, name="convgate_bwd", grid=(S // tm,), after=after,
                  in_specs=_conv_specs(tm) + [BS((tm, dx3.shape[1]), lambda i: (i, 0)), _resident(w3.shape)],
                  out_specs=[BS((tm, 2 * D_FF), lambda i: (i, 0)), BS((8, 2 * D_FF), lambda i: (0, 0))],
                  out_shape=[SDS((S, 2 * D_FF), F32), SDS((8, 2 * D_FF), F32)],
                  scratch=[pltpu.VMEM((tm, D_FF), F32)])(a, a, a, a, cw, cw, cb, cb, dx3, w3)


def conv_transpose_rms_bwd(dc, cw, w3, x, g, dres):
    S, C = dc.shape
    Q, N, Kc = w3.shape
    tm = _tile(S, (256,))
    nt = S // tm
    halo_blocks = tm // 8

    def body(dc_ref, halo_ref, cw_ref, w_ref, x_ref, g_ref, dr_ref, da_ref, dx_ref, dg_ref):
        @pl.when(pl.program_id(0) == 0)
        def _():
            dg_ref[...] = jnp.zeros_like(dg_ref)

        last_tile = pl.program_id(0) == nt - 1
        row = lax.broadcasted_iota(jnp.int32, (8, CONV_COLS), 0)
        for c0 in range(0, C, CONV_COLS):
            cols = slice(c0, c0 + CONV_COLS)
            h0 = jnp.where(last_tile, 0.0, halo_ref[0:1, cols])
            h1 = jnp.where(last_tile, 0.0, halo_ref[1:2, cols])
            dc_v = dc_ref[:, cols]
            r1, r2 = pltpu.roll(dc_v, tm - 1, 0), pltpu.roll(dc_v, tm - 2, 0)
            n1 = jnp.concatenate([r1[:tm - 8], jnp.where(row == 7, h0, r1[tm - 8:])], axis=0)
            n2 = jnp.concatenate([r2[:tm - 8], jnp.where(row == 7, h1, jnp.where(row == 6, h0, r2[tm - 8:]))], axis=0)
            da_ref[:, cols] = (cw_ref[2:3, cols] * dc_v + cw_ref[1:2, cols] * n1
                               + cw_ref[0:1, cols] * n2).astype(da_ref.dtype)
        xv = x_ref[...]
        dx, dgc = _rms_bwd(_nt_chunks(da_ref, w_ref), xv, g_ref[...], _rms(xv))
        dx_ref[...] = dr_ref[...] + dx
        _acc_rows(dg_ref, 0, dgc)

    row_n = BS((tm, N), lambda i: (i, 0))
    return _pcall(body, name="conv_transpose_rms_bwd", grid=(nt,),
                  in_specs=[BS((tm, C), lambda i: (i, 0)),
                            BS((8, C), lambda i: (jnp.minimum((i + 1) * halo_blocks, S // 8 - 1), 0)),
                            BS((3, C), lambda i: (0, 0)), _resident((Q, N, Kc)), row_n, BS((1, N), lambda i: (0, 0)),
                            row_n],
                  out_specs=[BS((tm, C), lambda i: (i, 0)), row_n, BS((8, N), lambda i: (0, 0))],
                  out_shape=[SDS((S, C), MXU_DTYPE), SDS((S, N), F32), SDS((8, N), F32)])(dc, dc, cw, w3, x, g, dres)


def xattn_block_bwd(dx2, wo3, qx, kn, vb, gxq, wq3, x1, g, wout3, attn, gm, gao, ggo, after=None):
    S, D = qx.shape
    tm = _tile(S, (512, 256))
    hw = D // 2

    def body(dx2_ref, wo_ref, q_ref, kn_ref, vb_ref, gxq_ref, wq_ref, x_ref, g_ref, wout_ref, at_ref, gm_ref,
             gao_ref, ggo_ref, dq_ref, dx_ref, dkn_ref, dv_ref, dgq_ref, dg_ref, da_ref, dgm_ref, dgy_ref):
        @pl.when(pl.program_id(0) == 0)
        def _():
            for ref in (dkn_ref, dv_ref, dgq_ref, dg_ref, dgy_ref):
                ref[...] = jnp.zeros_like(ref)

        gq = gxq_ref[...]
        do_all = _nt_chunks(dx2_ref, wo_ref)
        for h in range(XA_HEADS):
            sl = slice(h * XA_DH, (h + 1) * XA_DH)
            qh, do = q_ref[:, sl], do_all[:, sl]
            r, qn, p = _xa_probs(qh, gq, kn_ref[:, sl])
            dp = _dot(do, vb_ref[:, sl], NT)
            ds = p * (dp - jnp.sum(dp * p, axis=1, keepdims=True)) * (1.0 / math.sqrt(XA_DH))
            dqn = _dot(ds, kn_ref[:, sl])
            dkn_ref[:, sl] += _dot(ds, qn, TN)
            dv_ref[:, sl] += _dot(p, do, TN)
            dqh, dgc = _rms_bwd(dqn, qh, gq, r)
            dq_ref[:, sl] = dqh.astype(dq_ref.dtype)
            _acc_rows(dgq_ref, 0, dgc)
        xv = x_ref[...]
        dx, dgc = _rms_bwd(_nt_chunks(dq_ref, wq_ref), xv, g_ref[...], _rms(xv))
        dx1 = dx2_ref[...] + dx
        dx_ref[...] = dx1
        _acc_rows(dg_ref, 0, dgc)
        dy = _dot(dx1, wout_ref[0], NT)
        av, gmv = at_ref[...], gm_ref[...]
        da, dga = _rms_bwd(dy[:, :hw], av, gao_ref[...], _rms(av))
        dgm, dgg = _rms_bwd(dy[:, hw:], gmv, ggo_ref[...], _rms(gmv))
        da_ref[...] = da
        dgm_ref[...] = dgm
        dgy_ref[0:1, :hw] += jnp.sum(dga, axis=0, keepdims=True)
        dgy_ref[0:1, hw:] += jnp.sum(dgg, axis=0, keepdims=True)

    row = BS((tm, D), lambda i: (i, 0))
    half = BS((tm, hw), lambda i: (i, 0))
    full = lambda r, w: BS((r, w), lambda i: (0, 0))
    return _pcall(body, name="xattn_block_bwd", grid=(S // tm,), after=after,
                  in_specs=[row, _resident(wo3.shape), row, full(MEM_LEN, D), full(MEM_LEN, D), full(1, XA_DH),
                            _resident(wq3.shape), row, full(1, D), _resident(wout3.shape), half, half, full(1, hw),
                            full(1, hw)],
                  out_specs=[row, row, full(MEM_LEN, D), full(MEM_LEN, D), full(8, XA_DH), full(8, D), half, half,
                             full(8, D)],
                  out_shape=[SDS((S, D), MXU_DTYPE), SDS((S, D), F32), SDS((MEM_LEN, D), F32), SDS((MEM_LEN, D), F32),
                             SDS((8, XA_DH), F32), SDS((8, D), F32), SDS((S, hw), F32), SDS((S, hw), F32),
                             SDS((8, D), F32)])(dx2, wo3, qx, kn, vb, gxq, wq3, x1, g, wout3, attn, gm, gao, ggo)


def mem_bwd(kv, dkn, dvb, gxk, after=None):
    def body(kv_ref, dkn_ref, dv_ref, g_ref, dkv_ref, dg_ref):
        dg_ref[...] = jnp.zeros_like(dg_ref)
        for h in range(XA_HEADS):
            sl = slice(h * XA_DH, (h + 1) * XA_DH)
            k = kv_ref[:, sl]
            dk, dgc = _rms_bwd(dkn_ref[:, sl], k, g_ref[...], _rms(k))
            dkv_ref[:, sl] = dk.astype(dkv_ref.dtype)
            _acc_rows(dg_ref, 0, dgc)
        dkv_ref[:, 1024:2048] = dv_ref[...].astype(dkv_ref.dtype)

    full = lambda r, w: BS((r, w), lambda i: (0, 0))
    return _pcall(body, name="mem_bwd", grid=(1,), after=after,
                  in_specs=[full(MEM_LEN, 2048), full(MEM_LEN, 1024), full(MEM_LEN, 1024), full(1, XA_DH)],
                  out_specs=[full(MEM_LEN, 2048), full(8, XA_DH)],
                  out_shape=[SDS((MEM_LEN, 2048), MXU_DTYPE), SDS((8, XA_DH), F32)])(kv, dkn, dvb, gxk)


def _norm_rope_bwd(slabs, douts, g, bm, cos_v, sin_v, first):
    dqns = [d * cos_v + _half_swap(d * sin_v, first) for d in douts]
    rs = [lax.rsqrt(ms + EPS) for ms in _head_means([x * x for x in slabs], bm)]
    projs = _head_means([dqn * g * x for dqn, x in zip(dqns, slabs)], bm)
    dxs = [r * (dqn * g) - x * (r * r * r) * pr for x, dqn, r, pr in zip(slabs, dqns, rs, projs)]
    return dxs, [dqn * x * r for x, dqn, r in zip(slabs, dqns, rs)]


def mixer_core_bwd(proj, cos, sin, gq, gk, gvg, bmat, qr, kr, vb, sinkcol, dattn, dgm, gvn, gu, w2, w2t, bsl,
                   after=None):
    S = qr.shape[0]
    nb = S // BLK

    def body(p_ref, c_ref, s_ref, gq_ref, gk_ref, gvg_ref, b_ref, q_ref, kc_ref, kp_ref, vc_ref, vp_ref, sk_ref,
             do_ref, dgm_ref, gvn_ref, gu_ref, w2_ref, w2t_ref, bsl_ref,
             dp_ref, dsk_ref, dws_ref, dbl_ref, dgq_ref, dgk_ref, dgv_ref,
             carry_k, carry_v, done_k, done_v, dq_keep, dgu_keep, dgvn_keep):
        n = pl.program_id(0)

        @pl.when(n == 0)
        def _():
            for ref in (dsk_ref, dws_ref, dbl_ref, dgq_ref, dgk_ref, dgv_ref, carry_k, carry_v, dq_keep, dgu_keep,
                        dgvn_keep):
                ref[...] = jnp.zeros_like(ref)

        live = (n < nb).astype(F32)
        cos_v, sin_v, bm = c_ref[...], s_ref[...], b_ref[...]
        first = (_lane((BLK, 128)) & 63) < 32
        lo = _lane((BLK, 128)) < 64

        dxs, dgs = _norm_rope_bwd([p_ref[:, s * 128:(s + 1) * 128] for s in range(4)],
                                  [dq_keep[:, s * 128:(s + 1) * 128] for s in range(4)], gq_ref[...], bm,
                                  cos_v, sin_v, first)
        for s, (dx, dg) in enumerate(zip(dxs, dgs)):
            dp_ref[:, s * 128:(s + 1) * 128] = dx.astype(dp_ref.dtype)
            _acc_rows(dgq_ref, 0, dg)
        dp_ref[:, 1024:1536] = (dgu_keep[...] * _gelu_grad(p_ref[:, 1024:1536])).astype(dp_ref.dtype)
        gv, gv_grad = _gelu_and_grad(p_ref[:, 1536:2048])
        dgv, dgc = _rms_bwd(dgvn_keep[...], gv, gvg_ref[...], _rms(gv))
        dp_ref[:, 1536:2048] = (dgv * gv_grad).astype(dp_ref.dtype)
        _acc_rows(dgv_ref, 0, dgc)

        for h in range(2):
            hs, qs = slice(h * 128, (h + 1) * 128), slice(h * 256, (h + 1) * 256)
            kd = jnp.concatenate([kp_ref[:, hs], kc_ref[:, hs]], axis=0)
            vd = jnp.concatenate([vp_ref[:, hs], vc_ref[:, hs]], axis=0)
            sink = jnp.concatenate([sk_ref[2 * h], sk_ref[2 * h + 1]], axis=0)
            qp, p, psink = _swa_probs(q_ref[:, qs], kd, sink, n, lo)
            dop = _by_head(do_ref[:, qs], lo)
            dp = _dot(dop, vd, NT)
            delta = jnp.sum(dp * p, axis=1, keepdims=True)
            ds = p * (dp - delta) * (1.0 / math.sqrt(HEAD_DIM))
            dsink = -psink * delta * live
            dsk_ref[2 * h] += dsink[:2 * BLK]
            dsk_ref[2 * h + 1] += dsink[2 * BLK:]
            dq_keep[:, qs] = _from_heads(_dot(ds, kd), lo)
            dkd = _dot(ds, qp, TN)
            dvd = _dot(p, dop, TN)
            done_k[:, hs] = carry_k[:, hs] + live * dkd[:BLK]
            done_v[:, hs] = carry_v[:, hs] + live * dvd[:BLK]
            carry_k[:, hs] = dkd[BLK:]
            carry_v[:, hs] = dvd[BLK:]
        for j in range(4):
            sl = slice(j * 128, (j + 1) * 128)
            gvn_s = gvn_ref[:, sl]
            m2 = _dot(w2_ref[j], gvn_s)
            mixed = jnp.where(lo, m2[:BLK], m2[BLK:]) + bsl_ref[j]
            dgm_s = dgm_ref[:, sl]
            dgu_keep[:, sl] = dgm_s * mixed
            dmx = dgm_s * gu_ref[:, sl] * live
            d2 = _dot(w2t_ref[j], dmx)
            dgvn_keep[:, sl] = jnp.where(lo, d2[:BLK], d2[BLK:])
            z = jnp.zeros_like(dmx)
            dws_ref[2 * j] += _dot(jnp.where(lo, dmx, z), gvn_s, NT)
            dws_ref[2 * j + 1] += _dot(jnp.where(lo, z, dmx), gvn_s, NT)
            dbl_ref[j] += dmx

        dxs, dgs = _norm_rope_bwd([p_ref[:, 512 + s * 128:640 + s * 128] for s in range(2)],
                                  [done_k[:, s * 128:(s + 1) * 128] for s in range(2)], gk_ref[...], bm,
                                  cos_v, sin_v, first)
        for s, (dx, dg) in enumerate(zip(dxs, dgs)):
            dp_ref[:, 512 + s * 128:640 + s * 128] = dx.astype(dp_ref.dtype)
            _acc_rows(dgk_ref, 0, dg)
        dp_ref[:, 768:1024] = done_v[...].astype(dp_ref.dtype)

    last = nb - 1
    cur = lambda w: BS((BLK, w), lambda n: (jnp.minimum(n, last), 0))
    prev = lambda w: BS((BLK, w), lambda n: (jnp.clip(n - 1, 0, last), 0))
    done = lambda w: BS((BLK, w), lambda n: (jnp.maximum(n - 1, 0), 0))
    const = lambda *shape: BS(shape, lambda n: (0,) * len(shape))
    return _pcall(body, name="mixer_core_bwd", grid=(nb + 1,), after=after,
                  in_specs=[done(IN_COLS_DUP), done(128), done(128), const(1, 128), const(1, 128), const(1, 512),
                            const(128, 128), cur(512), cur(256), prev(256), cur(256), prev(256),
                            const(4, 2 * BLK, 1), cur(512), cur(512), cur(512), cur(512), const(4, 2 * BLK, BLK),
                            const(4, 2 * BLK, BLK), const(4, BLK, 128)],
                  out_specs=[done(IN_COLS_DUP), const(4, 2 * BLK, 1), const(8, BLK, BLK), const(4, BLK, 128),
                             const(8, 128), const(8, 128), const(8, 512)],
                  out_shape=[SDS((S, IN_COLS_DUP), MXU_DTYPE), SDS((4, 2 * BLK, 1), F32), SDS((8, BLK, BLK), F32),
                             SDS((4, BLK, 128), F32), SDS((8, 128), F32), SDS((8, 128), F32), SDS((8, 512), F32)],
                  scratch=[pltpu.VMEM((BLK, 256), F32)] * 4 + [pltpu.VMEM((BLK, 512), F32)] * 3)(
        proj, cos, sin, gq, gk, gvg, bmat, qr, kr, kr, vb, vb, sinkcol, dattn, dgm, gvn, gu, w2, w2t, bsl)


BIG = (("w_in", (1024, 448), True), ("w_out", (256, 1024), False), ("xa_wq", (256, 1024), False),
       ("xa_wkv", (1024, 512), True), ("xa_wo", (256, 1024), False), ("ffn_up", (1024, 1408), True),
       ("ffn_down", (704, 1024), False))
BIG_NAMES = tuple(n for n, _, _ in BIG)
SMALL_VECS = (("mix_norm", 1024), ("q_norm", 64), ("k_norm", 64), ("attn_sinks", 8), ("gmlp_v_norm", 512),
              ("attn_out_norm", 512), ("gmlp_out_norm", 512), ("xa_norm", 1024), ("mem_norm", 1024),
              ("xa_q_norm", 256), ("xa_k_norm", 256), ("ffn_norm", 1024), ("ffn_conv_b", 5632))
SMALL = tuple(n for n, _ in SMALL_VECS) + ("gmlp_bs", "gmlp_ws", "ffn_conv")
WEIGHTS = ("mix_norm", "w_in", "q_norm", "k_norm", "attn_sinks", "gmlp_v_norm", "gmlp_ws", "gmlp_bs",
           "attn_out_norm", "gmlp_out_norm", "w_out", "xa_norm", "mem_norm", "xa_wq", "xa_wkv", "xa_q_norm",
           "xa_k_norm", "xa_wo", "ffn_norm", "ffn_up", "ffn_conv", "ffn_conv_b", "ffn_down")
CONV_SHARD = (3, 1408)
CONV_LANE_ROWS = CONV_SHARD[1] // 128
CONV_CHIP_ROWS = 40


def _small_rows():
    rows, r = {}, 0
    for n, length in SMALL_VECS:
        rows[n] = r
        r += -(-length // 128)
    r += -r % 8
    rows["gmlp_bs"] = r
    r += 8
    rows["gmlp_ws"] = r
    r += 8 * BLK
    rows["ffn_conv"] = r
    r += N_CHIPS * CONV_CHIP_ROWS
    return rows, r


SMALL_ROW, SMALL_ROWS = _small_rows()


def pack_small(dg_mix, dgq, dgk, dsk, dg_gvn, dg_y, dg_xa, dg_mem, dg_xq, dg_xk, dg_ffn, gcw, dbl, dws):
    def body(mix_ref, q_ref, k_ref, sk_ref, gvn_ref, y_ref, xa_ref, mem_ref, xq_ref, xk_ref, ffn_ref, cw_ref,
             dbl_ref, dws_ref, o_ref):
        o_ref[...] = jnp.zeros_like(o_ref)
        lane = _lane((1, 128))

        def put(name, src_ref, row, lane0, length):
            for k in range(length // 128):
                o_ref[SMALL_ROW[name] + k:SMALL_ROW[name] + k + 1, :] = src_ref[row:row + 1, lane0 + k * 128:lane0 + (k + 1) * 128]

        put("mix_norm", mix_ref, 0, 0, 1024)
        for name, ref in (("q_norm", q_ref), ("k_norm", k_ref)):
            v = ref[0:1, :]
            o_ref[SMALL_ROW[name]:SMALL_ROW[name] + 1, :] = jnp.where(lane < HEAD_DIM, v + pltpu.roll(v, 64, 1), 0.0)
        sinks = jnp.zeros((1, 128), F32)
        for s in range(4):
            col = sk_ref[s]
            sinks = sinks + jnp.where(lane == 2 * s, jnp.sum(col[:BLK]), 0.0) + jnp.where(lane == 2 * s + 1, jnp.sum(col[BLK:]), 0.0)
        o_ref[SMALL_ROW["attn_sinks"]:SMALL_ROW["attn_sinks"] + 1, :] = sinks
        put("gmlp_v_norm", gvn_ref, 0, 0, 512)
        put("attn_out_norm", y_ref, 0, 0, 512)
        put("gmlp_out_norm", y_ref, 0, 512, 512)
        put("xa_norm", xa_ref, 0, 0, 1024)
        put("mem_norm", mem_ref, 0, 0, 1024)
        put("xa_q_norm", xq_ref, 0, 0, 256)
        put("xa_k_norm", xk_ref, 0, 0, 256)
        put("ffn_norm", ffn_ref, 0, 0, 1024)
        put("ffn_conv_b", cw_ref, 3, 0, 2 * D_FF)
        r8 = lax.broadcasted_iota(jnp.int32, (8, 128), 0)
        l8 = _lane((8, 128))
        bs = jnp.zeros((8, BLK), F32)
        for j in range(4):
            sel = (((r8 == 2 * j) & (l8 < 64)) | ((r8 == 2 * j + 1) & (l8 >= 64))).astype(F32).astype(BF16)
            xj = dbl_ref[j]
            hi = xj.astype(BF16)
            lo = (xj - hi.astype(F32)).astype(BF16)
            bs = bs + lax.dot_general(sel, hi, NT, preferred_element_type=F32) + lax.dot_general(sel, lo, NT, preferred_element_type=F32)
        o_ref[SMALL_ROW["gmlp_bs"]:SMALL_ROW["gmlp_bs"] + 8, :] = bs
        causal = lax.broadcasted_iota(jnp.int32, (BLK, BLK), 0) >= lax.broadcasted_iota(jnp.int32, (BLK, BLK), 1)
        for h in range(8):
            r0 = SMALL_ROW["gmlp_ws"] + h * BLK
            o_ref[r0:r0 + BLK, :] = jnp.where(causal, dws_ref[h], 0.0)
        for q in range(N_CHIPS):
            for j in range(3):
                for k in range(CONV_LANE_ROWS):
                    r0 = SMALL_ROW["ffn_conv"] + q * CONV_CHIP_ROWS + j * CONV_LANE_ROWS + k
                    l0 = (q * CONV_LANE_ROWS + k) * 128
                    o_ref[r0:r0 + 1, :] = cw_ref[j:j + 1, l0:l0 + 128]

    args = (dg_mix, dgq, dgk, dsk, dg_gvn, dg_y, dg_xa, dg_mem, dg_xq, dg_xk, dg_ffn, gcw, dbl, dws)
    full = lambda a: BS(a.shape, lambda i, nd=a.ndim: (0,) * nd)
    return _pcall(body, name="pack_small", grid=(1,), in_specs=[full(a) for a in args],
                  out_specs=BS((SMALL_ROWS, 128), lambda i: (0, 0)), out_shape=SDS((SMALL_ROWS, 128), F32))(*args)


def _adam(w, g, m, v):
    mn = ADAM_B1 * m + (1.0 - ADAM_B1) * g
    vn = ADAM_B2 * v + (1.0 - ADAM_B2) * (g * g)
    m_hat = mn / (1.0 - ADAM_B1 ** ADAM_STEP)
    v_hat = vn / (1.0 - ADAM_B2 ** ADAM_STEP)
    return -ADAM_LR * (m_hat / (jnp.sqrt(v_hat) + ADAM_EPS) + ADAM_WD * w), mn, vn


def adamw_small(gsum, w, m, v, chipvec):
    n = len(SMALL)

    def body(chip_ref, g_ref, *refs):
        w_refs, m_refs, v_refs = refs[:n], refs[n:2 * n], refs[2 * n:3 * n]
        outs = refs[3 * n:]
        go, do, mo, vo = outs[:n], outs[n:2 * n], outs[2 * n:3 * n], outs[3 * n:]

        def update(i, idx, g):
            d, mn, vn = _adam(w_refs[i][idx], g, m_refs[i][idx], v_refs[i][idx])
            go[i][idx] = g
            do[i][idx] = d
            mo[i][idx] = mn
            vo[i][idx] = vn

        for i, (name, length) in enumerate(SMALL_VECS):
            for k in range(-(-length // 128)):
                wd = min(128, length - k * 128)
                r = SMALL_ROW[name] + k
                update(i, (slice(0, 1), slice(k * 128, k * 128 + wd)), g_ref[r:r + 1, 0:wd])
        i_bs, i_ws, i_cv = len(SMALL_VECS), len(SMALL_VECS) + 1, len(SMALL_VECS) + 2
        update(i_bs, (0,), g_ref[SMALL_ROW["gmlp_bs"]:SMALL_ROW["gmlp_bs"] + 8, :])
        for h in range(8):
            r0 = SMALL_ROW["gmlp_ws"] + h * BLK
            update(i_ws, (0, h), g_ref[r0:r0 + BLK, :])
        mine = g_ref[pl.ds(pl.multiple_of(SMALL_ROW["ffn_conv"] + chip_ref[0] * CONV_CHIP_ROWS, 8), CONV_CHIP_ROWS), :]
        for j in range(3):
            for k in range(CONV_LANE_ROWS):
                r = j * CONV_LANE_ROWS + k
                update(i_cv, (0, slice(j, j + 1), slice(k * 128, (k + 1) * 128)), mine[r:r + 1, :])

    nat = [w[nm] for nm in SMALL]
    full = lambda a: BS(a.shape, lambda i, c, nd=a.ndim: (0,) * nd)
    outs = _pcall(body, name="adamw_small", grid=(1,), prefetch=1,
                  in_specs=[BS((SMALL_ROWS, 128), lambda i, c: (0, 0))] + [full(a) for a in nat] * 3,
                  out_specs=[full(a) for a in nat] * 4, out_shape=[SDS(a.shape, F32) for a in nat] * 4)(
        chipvec, gsum, *nat, *[m[nm] for nm in SMALL], *[v[nm] for nm in SMALL])
    return outs[:n], outs[n:2 * n], outs[2 * n:3 * n], outs[3 * n:]


def adamw_matrix(w, m, v, g_own, g_other, cvec, *, name):
    _, r, c = w.shape
    half = r // 2
    tr = _tile(half, (256, 176, 128))
    T = half // tr

    def body(c_ref, w_ref, m_ref, v_ref, own_ref, oth_ref, g_ref, d_ref, mo_ref, vo_ref):
        g = jnp.where(pl.program_id(0) == c_ref[0], own_ref[...], oth_ref[...])
        d, mn, vn = _adam(w_ref[...], g, m_ref[...], v_ref[...])
        g_ref[...] = g
        d_ref[...] = d
        mo_ref[...] = mn
        vo_ref[...] = vn

    nat = BS((None, tr, c), lambda hf, t, cr: (0, hf * T + t, 0))
    hlf = BS((tr, c), lambda hf, t, cr: (t, 0))
    return _pcall(body, name=name, grid=(2, T), prefetch=1, in_specs=[nat, nat, nat, hlf, hlf], out_specs=[nat] * 4,
                  out_shape=[SDS(w.shape, F32)] * 4)(cvec, w, m, v, g_own, g_other)


def _place():
    return lax.axis_index("x"), lax.axis_index("y"), lax.axis_index("c")


def _other_chips(x, y):
    return [(1 - x, y), (x, 1 - y), (1 - x, 1 - y)]


def _rows_of_core(c, half):
    return pl.ds(pl.multiple_of(c * half, 16), half)


def _rcopy(src, dst, sems, k, to):
    return pltpu.make_async_remote_copy(src_ref=src, dst_ref=dst, send_sem=sems[0].at[k], recv_sem=sems[1].at[k],
                                        device_id=to, device_id_type=MESH)


def cast_shards(shards, conv, chipvec):
    n = len(shards)

    def body(chip_ref, *refs):
        for i_ref, o_ref in zip(refs[:n + 1], refs[n + 1:]):
            o_ref[...] = i_ref[...].astype(o_ref.dtype)

    in_specs = [BS((s.shape[0] // 4, s.shape[1]), lambda i, p: (i, 0)) for s in shards]
    in_specs.append(BS(conv.shape, lambda i, p: (0, 0)))
    out_specs = [BS((None, s.shape[0] // 4, s.shape[1]), lambda i, p: (p[0], i, 0)) for s in shards]
    out_specs.append(BS((None,) + conv.shape, lambda i, p: (p[0], 0, 0)))
    out_shape = [SDS((N_CHIPS,) + s.shape, MXU_DTYPE) for s in shards] + [SDS((N_CHIPS,) + conv.shape, F32)]
    return _pcall(body, name="cast_shards", grid=(4,), prefetch=1, in_specs=in_specs, out_specs=out_specs,
                  out_shape=out_shape)(chipvec, *shards, conv)


HBM = pl.BlockSpec(memory_space=pltpu.HBM)
SEM = pl.BlockSpec(memory_space=pltpu.SEMAPHORE)
DATAFLOW = pltpu.SideEffectType.DATAFLOW_SIDE_EFFECTING
VMEM_WHOLE = pl.BlockSpec(memory_space=pltpu.VMEM)
TOKEN = jax.ShapeDtypeStruct((8, 128), jnp.float32)


def _gather_copies(bufs, send_sems, recv_sems, outgoing):
    x, y, c = _place()
    p = 2 * x + y
    cps = []
    for i, o in enumerate(bufs):
        for j, (cx, cy) in enumerate(_other_chips(x, y)):
            slot = o.at[p] if outgoing else o.at[2 * cx + cy]
            cps.append(_rcopy(slot, slot, (send_sems, recv_sems), 3 * i + j, (cx, cy, c)))
    return cps


def gather_start(slots, after):
    n = len(slots)

    def body(*refs):
        send_sems, recv_sems, thru, token = refs[n + 1], refs[n + 2], refs[n + 3:2 * n + 3], refs[2 * n + 3]
        for cp in _gather_copies(thru, send_sems, recv_sems, True):
            cp.start()
        token[...] = jnp.zeros_like(token)

    hbm = [pltpu.with_memory_space_constraint(s, pltpu.HBM) for s in slots]
    outs = pl.pallas_call(
        body, name="gather_start_%d" % n,
        out_shape=[pltpu.SemaphoreType.DMA((3 * n,)), pltpu.SemaphoreType.DMA((3 * n,))]
        + [pltpu.HBM(s.shape, s.dtype) for s in slots] + [TOKEN],
        in_specs=[HBM] * n + [ANY], out_specs=[SEM, SEM] + [HBM] * n + [VMEM_WHOLE],
        input_output_aliases={i: 2 + i for i in range(n)},
        compiler_params=pltpu.CompilerParams(has_side_effects=DATAFLOW))(*hbm, after)
    return outs[0], outs[1], outs[2:2 + n], outs[2 + n]


def gather_wait(send_sems, recv_sems, bufs, *after):
    n = len(bufs)

    def body(*refs):
        ins, send_ref, recv_ref = refs[:n], refs[n], refs[n + 1]
        for cp in _gather_copies(ins, send_ref, recv_ref, False):
            cp.wait_send()
            cp.wait_recv()

    return pl.pallas_call(
        body, name="gather_wait_%d" % n, out_shape=[pltpu.HBM(s.shape, s.dtype) for s in bufs],
        in_specs=[HBM] * n + [SEM, SEM] + [ANY] * len(after), out_specs=[HBM] * n,
        input_output_aliases={i: i for i in range(n)},
        compiler_params=pltpu.CompilerParams(has_side_effects=DATAFLOW))(*bufs, send_sems, recv_sems, *after)


def _peers(x, y, c):
    return [(1 - x if k & 4 else x, 1 - y if k & 2 else y, 1 - c if k & 1 else c) for k in range(1, N_DEV)]


def _partial_copies(g_ref, land_ref, send_sems, recv_sems, outgoing):
    x, y, c = _place()
    half = g_ref.shape[1] // 2
    cps = []
    for k, (px, py, pc) in enumerate(_peers(x, y, c)):
        src = g_ref.at[2 * px + py, _rows_of_core(pc, half)]
        dst = land_ref.at[4 * x + 2 * y + c] if outgoing else land_ref.at[4 * px + 2 * py + pc]
        cps.append(_rcopy(src, dst, (send_sems, recv_sems), k, (px, py, pc)))
    return cps


def partials_start(g, *, name):
    land = lax.empty((N_DEV, g.shape[1] // 2, g.shape[2]), g.dtype)

    def body(g_ref, land_ref, send_sems, recv_sems, g_thru, land_thru, token):
        for cp in _partial_copies(g_thru, land_thru, send_sems, recv_sems, True):
            cp.start()
        token[...] = jnp.zeros_like(token)

    return pl.pallas_call(
        body, name=name,
        out_shape=[pltpu.SemaphoreType.DMA((N_DEV - 1,)), pltpu.SemaphoreType.DMA((N_DEV - 1,)),
                   pltpu.HBM(g.shape, g.dtype), pltpu.HBM(land.shape, land.dtype), TOKEN],
        in_specs=[HBM, HBM], out_specs=[SEM, SEM, HBM, HBM, VMEM_WHOLE], input_output_aliases={0: 2, 1: 3},
        compiler_params=pltpu.CompilerParams(has_side_effects=DATAFLOW))(
        pltpu.with_memory_space_constraint(g, pltpu.HBM), pltpu.with_memory_space_constraint(land, pltpu.HBM))


def partials_wait(started, after):
    n = len(started)

    def body(*refs):
        for i in range(n):
            send_ref, recv_ref, g_ref, land_ref = refs[4 * i:4 * i + 4]
            for cp in _partial_copies(g_ref, land_ref, send_ref, recv_ref, False):
                cp.wait_send()
                cp.wait_recv()

    flat = [a for s in started for a in s]
    bufs = [a for s in started for a in s[2:]]
    outs = pl.pallas_call(
        body, name="partials_wait", out_shape=[pltpu.HBM(b.shape, b.dtype) for b in bufs],
        in_specs=[SEM, SEM, HBM, HBM] * n + [ANY], out_specs=[HBM] * (2 * n),
        input_output_aliases={4 * i + 2 + j: 2 * i + j for i in range(n) for j in range(2)},
        compiler_params=pltpu.CompilerParams(has_side_effects=DATAFLOW))(*flat, after)
    return [(outs[2 * i], outs[2 * i + 1]) for i in range(n)]


def sum_partials(pairs, order):
    n = len(pairs)

    def body(o_ref, *refs):
        j = pl.program_id(0)
        for g_ref, l_ref, f_ref in zip(refs[:n], refs[n:2 * n], refs[2 * n:]):
            @pl.when(j == 0)
            def _():
                f_ref[...] = g_ref[...].astype(F32)

            @pl.when(j > 0)
            def _():
                f_ref[...] += l_ref[...].astype(F32)

    g4 = [g.reshape(g.shape[0], 2, g.shape[1] // 2, g.shape[2]) for g, _ in pairs]
    lands = [l for _, l in pairs]
    return _pcall(body, name="sum_partials", grid=(N_DEV,), prefetch=1,
                  in_specs=[BS((None, None) + g.shape[2:], lambda j, o: (o[0], o[1], 0, 0)) for g in g4]
                  + [BS((None,) + l.shape[1:], lambda j, o: (o[jnp.maximum(j, 1) + 1], 0, 0)) for l in lands],
                  out_specs=[BS(l.shape[1:], lambda j, o: (0, 0)) for l in lands],
                  out_shape=[SDS(l.shape[1:], F32) for l in lands])(order, *g4, *lands)


def _pair_copies(f_refs, land_refs, send_sems, recv_sems):
    x, y, c = _place()
    return [_rcopy(f, o, (send_sems, recv_sems), i, (x, y, 1 - c)) for i, (f, o) in enumerate(zip(f_refs, land_refs))]


def pair_start(fs):
    n = len(fs)
    lands = [lax.empty(f.shape, f.dtype) for f in fs]

    def body(*refs):
        send_sems, recv_sems = refs[2 * n], refs[2 * n + 1]
        thru, land_thru, token = refs[2 * n + 2:3 * n + 2], refs[3 * n + 2:4 * n + 2], refs[4 * n + 2]
        for cp in _pair_copies(thru, land_thru, send_sems, recv_sems):
            cp.start()
        token[...] = jnp.zeros_like(token)

    hbm = [pltpu.with_memory_space_constraint(a, pltpu.HBM) for a in list(fs) + lands]
    outs = pl.pallas_call(
        body, name="pair_start",
        out_shape=[pltpu.SemaphoreType.DMA((n,)), pltpu.SemaphoreType.DMA((n,))]
        + [pltpu.HBM(a.shape, a.dtype) for a in list(fs) + lands] + [TOKEN],
        in_specs=[HBM] * (2 * n), out_specs=[SEM, SEM] + [HBM] * (2 * n) + [VMEM_WHOLE],
        input_output_aliases={i: 2 + i for i in range(2 * n)},
        compiler_params=pltpu.CompilerParams(has_side_effects=DATAFLOW))(*hbm)
    return outs[0], outs[1], outs[2:2 + n], outs[2 + n:2 + 2 * n], outs[2 + 2 * n]


def pair_wait(send_sems, recv_sems, fs, lands, after):
    n = len(fs)

    def body(*refs):
        for cp in _pair_copies(refs[:n], refs[n:2 * n], refs[2 * n], refs[2 * n + 1]):
            cp.wait_send()
            cp.wait_recv()

    outs = pl.pallas_call(
        body, name="pair_wait", out_shape=[pltpu.HBM(a.shape, a.dtype) for a in list(fs) + list(lands)],
        in_specs=[HBM] * (2 * n) + [SEM, SEM, ANY], out_specs=[HBM] * (2 * n),
        input_output_aliases={i: i for i in range(2 * n)},
        compiler_params=pltpu.CompilerParams(has_side_effects=DATAFLOW))(*fs, *lands, send_sems, recv_sems, after)
    return outs[:n], outs[n:]


def _small_copies(s_ref, land_ref, send_sems, recv_sems, outgoing):
    x, y, c = _place()
    cps = []
    for k, (px, py, pc) in enumerate(_peers(x, y, c)):
        dst = land_ref.at[4 * x + 2 * y + c] if outgoing else land_ref.at[4 * px + 2 * py + pc]
        cps.append(_rcopy(s_ref, dst, (send_sems, recv_sems), k, (px, py, pc)))
    return cps


def small_start(sm):
    land = lax.empty((N_DEV,) + sm.shape, sm.dtype)

    def body(s_ref, land_ref, send_sems, recv_sems, s_thru, land_thru):
        for cp in _small_copies(s_thru, land_thru, send_sems, recv_sems, True):
            cp.start()

    return pl.pallas_call(
        body, name="small_start",
        out_shape=[pltpu.SemaphoreType.DMA((N_DEV - 1,)), pltpu.SemaphoreType.DMA((N_DEV - 1,)),
                   pltpu.HBM(sm.shape, sm.dtype), pltpu.HBM(land.shape, land.dtype)],
        in_specs=[HBM, HBM], out_specs=[SEM, SEM, HBM, HBM], input_output_aliases={0: 2, 1: 3},
        compiler_params=pltpu.CompilerParams(has_side_effects=DATAFLOW))(
        pltpu.with_memory_space_constraint(sm, pltpu.HBM), pltpu.with_memory_space_constraint(land, pltpu.HBM))


def small_wait(send_sems, recv_sems, sm, land, *after):
    def body(send_ref, recv_ref, s_ref, land_ref, *rest):
        for cp in _small_copies(s_ref, land_ref, send_ref, recv_ref, False):
            cp.wait_send()
            cp.wait_recv()

    return pl.pallas_call(
        body, name="small_wait", out_shape=[pltpu.HBM(sm.shape, sm.dtype), pltpu.HBM(land.shape, land.dtype)],
        in_specs=[SEM, SEM, HBM, HBM] + [ANY] * len(after), out_specs=[HBM, HBM], input_output_aliases={2: 0, 3: 1},
        compiler_params=pltpu.CompilerParams(has_side_effects=DATAFLOW))(send_sems, recv_sems, sm, land, *after)


def sum_small(own, land, mevec):
    n, rows, width = land.shape
    tr = _tile(rows, (184, 8))

    def body(me_ref, own_ref, land_ref, o_ref):
        acc = jnp.zeros((tr, width), F32)
        for s in range(n):
            acc = acc + jnp.where(me_ref[0] == s, own_ref[...], land_ref[s])
        o_ref[...] = acc

    return _pcall(body, name="sum_small", grid=(rows // tr,), prefetch=1,
                  in_specs=[BS((tr, width), lambda i, me: (i, 0)), BS((n, tr, width), lambda i, me: (0, i, 0))],
                  out_specs=BS((tr, width), lambda i, me: (i, 0)), out_shape=SDS((rows, width), F32))(mevec, own, land)


def _to_full(blk, col):
    n, r, c = blk.shape
    return blk.transpose(1, 0, 2).reshape(r, n * c) if col else blk.reshape(n * r, c)


def _dup_cols(w):
    dup = lambda t: jnp.concatenate([t[:, :64], t[:, :64], t[:, 64:], t[:, 64:]], axis=1)
    return jnp.concatenate([w[:, :512], dup(w[:, 512:640]), dup(w[:, 640:768]), w[:, 768:]], axis=1)


def _fold_cols(d):
    fold = lambda t: jnp.concatenate([t[:, 0:64] + t[:, 64:128], t[:, 128:192] + t[:, 192:256]], axis=1)
    return jnp.concatenate([d[:, :512], fold(d[:, 512:768]), fold(d[:, 768:1024]), d[:, 1024:]], axis=1)


def _local_step(x, mem, positions, target, w_in, later, sp, emit):
    gain = lambda n: sp[n].reshape(1, -1)
    half = HEAD_DIM // 2
    inv_freq = 1.0 / (10000.0 ** (jnp.arange(half, dtype=F32) * (2.0 / HEAD_DIM)))
    ang = positions.astype(F32)[:, None] * inv_freq
    cos, sin = jnp.cos(ang), jnp.sin(ang)
    cos128 = jnp.tile(cos, (1, 4))
    sin128 = jnp.concatenate([-sin, sin, -sin, sin], axis=1)
    seg = jnp.arange(128) // HEAD_DIM
    bmat = (seg[:, None] == seg[None, :]).astype(BF16)
    gq128, gk128 = jnp.tile(gain("q_norm"), (1, 2)), jnp.tile(gain("k_norm"), (1, 2))
    sinkcol = jnp.repeat(sp["attn_sinks"].reshape(4, 2), BLK, axis=1).reshape(4, 2 * BLK, 1)
    wsc = sp["gmlp_ws"] * jnp.tril(jnp.ones((BLK, BLK), F32))[None]
    w2 = wsc.reshape(4, 2 * BLK, BLK).astype(MXU_DTYPE)
    w2t = wsc.swapaxes(1, 2).reshape(4, 2 * BLK, BLK).astype(MXU_DTYPE)
    bsl = jnp.repeat(sp["gmlp_bs"].reshape(4, 2, BLK).transpose(0, 2, 1), HEAD_DIM, axis=2)
    cb = sp["ffn_conv_b"].reshape(1, -1)
    w_in_d = _dup_cols(_to_full(w_in(cos128, sin128, gq128, gk128, sinkcol, w2, w2t, bsl), True))[None]

    h1, proj = rms_mm(x, gain("mix_norm"), w_in_d, name="mix_in")
    qr, kr, vb, gu, gvn, attn, gm, y = mixer_core_fwd(proj, cos128, sin128, gq128, gk128, gain("gmlp_v_norm"), bmat,
                                                      sinkcol, gain("attn_out_norm"), w2, bsl, gain("gmlp_out_norm"))
    wf, last = later(y)
    w_out, xa_wq, xa_wo = (_to_full(wf[n], False) for n in ("w_out", "xa_wq", "xa_wo"))
    mn, kv = rms_mm(mem, gain("mem_norm"), wf["xa_wkv"], name="xa_kv")
    kn, vbx = mem_pre(kv, gain("xa_k_norm"))
    x1, h2, qx, xo, x2 = xattn_block_fwd(y, w_out, x, gain("xa_norm"), xa_wq, kn, vbx, gain("xa_q_norm"), xa_wo)
    ffn_w, cw = last(x2)
    wf = {**wf, **ffn_w}
    ffn_down = _to_full(wf["ffn_down"], False)
    h3, a, f, dx3, loss_acc = ffn_fwd_loss(x2, gain("ffn_norm"), wf["ffn_up"], cw, cb, ffn_down, target)

    by_rows = lambda g: g.reshape(N_CHIPS, g.shape[1] // N_CHIPS, g.shape[2])
    sent = emit("ffn_down", by_rows(mm_tn(f, dx3, name="g_ffn_down", out_dtype=WIRE_DTYPE)))
    dc, gcw = convgate_bwd(a, dx3, ffn_down[None], cw, cb, after=sent)
    da, dx2, dg_ffn = conv_transpose_rms_bwd(dc, cw, wf["ffn_up"], x2, gain("ffn_norm"), dx3)
    sent = emit("ffn_up", mm_tn(h3, da, name="g_ffn_up", out_dtype=WIRE_DTYPE, chunks=N_CHIPS))
    sent = emit("xa_wo", by_rows(mm_tn(xo, dx2, name="g_xa_wo", out_dtype=WIRE_DTYPE, after=sent)))
    dqx, dx1, dkn, dvx, dg_xq, dg_xa, dattn, dgm, dg_y = xattn_block_bwd(
        dx2, xa_wo[None], qx, kn, vbx, gain("xa_q_norm"), xa_wq[None], x1, gain("xa_norm"), w_out[None], attn, gm,
        gain("attn_out_norm"), gain("gmlp_out_norm"), after=sent)
    sent = emit("xa_wq", by_rows(mm_tn(h2, dqx, name="g_xa_wq", out_dtype=WIRE_DTYPE)))
    dkv, dg_xk = mem_bwd(kv, dkn, dvx, gain("xa_k_norm"), after=sent)
    _, dg_mem = mm_nt_rms_bwd(dkv, wf["xa_wkv"], mem, gain("mem_norm"), jnp.zeros_like(mem), name="d_mem")
    sent = emit("xa_wkv", mm_tn(mn, dkv, name="g_xa_wkv", out_dtype=WIRE_DTYPE, chunks=N_CHIPS))
    sent = emit("w_out", by_rows(mm_tn(y, dx1, name="g_w_out", out_dtype=WIRE_DTYPE, after=sent)))
    dproj, dsk, dws, dbl, dgq, dgk, dg_gvn = mixer_core_bwd(
        proj, cos128, sin128, gq128, gk128, gain("gmlp_v_norm"), bmat, qr, kr, vb, sinkcol, dattn, dgm, gvn, gu,
        w2, w2t, bsl, after=sent)
    g_in = _fold_cols(mm_tn(h1, dproj, name="g_w_in", out_dtype=F32)[0])
    sent = emit("w_in", g_in.reshape(1024, N_CHIPS, 448).transpose(1, 0, 2).astype(WIRE_DTYPE))
    grad_x, dg_mix = mm_nt_rms_bwd(dproj, w_in_d, x, gain("mix_norm"), dx1, name="d_x", tm=1024, after=sent)
    packed = pack_small(dg_mix, dgq, dgk, dsk, dg_gvn, dg_y, dg_xa, dg_mem, dg_xq, dg_xk, dg_ffn, gcw, dbl, dws)
    return loss_acc, grad_x, packed


def _gather_step(w, chipvec):
    slots = cast_shards([w[n][0] for n in BIG_NAMES], w["ffn_conv"][0], chipvec)
    send_a, recv_a, first, token = gather_start(slots[:1], chipvec)
    send_b, recv_b, mid, token = gather_start(slots[1:5], token)
    send_c, recv_c, rest, token = gather_start(slots[5:], token)

    def w_in(*after):
        return gather_wait(send_a, recv_a, first, token, *after)[0]

    def last(after):
        got = gather_wait(send_c, recv_c, rest, after)
        return dict(zip(BIG_NAMES[5:], got[:-1])), _to_full(got[-1], True)

    def later(after):
        return dict(zip(BIG_NAMES[1:5], gather_wait(send_b, recv_b, mid, after))), last

    return w_in, later, token


def _reduce_update(started, packed, w, m, v, chipvec, cvec, order):
    small_sent = small_start(packed)
    own = sum_partials(partials_wait([started[n] for n in BIG_NAMES], small_sent[2]), order)
    pair_send, pair_recv, own, lands, pair_started = pair_start(own)
    own, other = pair_wait(pair_send, pair_recv, own, lands, pair_started)
    res = [{}, {}, {}, {}]
    for n, g_own, g_other in zip(BIG_NAMES, own, other):
        for d, o in zip(res, adamw_matrix(w[n], m[n], v[n], g_own, g_other, cvec, name="adamw_" + n)):
            d[n] = o
    mevec = (2 * order[0:1] + order[1:2]).astype(jnp.int32)
    small_sum = sum_small(*small_wait(*small_sent, *[res[3][n] for n in BIG_NAMES]), mevec)
    for d, outs in zip(res, adamw_small(small_sum, w, m, v, chipvec)):
        d.update(zip(SMALL, outs))
    return res


def kernel(x, mem, positions, mix_norm, w_in, q_norm, k_norm, attn_sinks, gmlp_v_norm, gmlp_ws, gmlp_bs, attn_out_norm, gmlp_out_norm, w_out, xa_norm, mem_norm, xa_wq, xa_wkv, xa_q_norm, xa_k_norm, xa_wo, ffn_norm, ffn_up, ffn_conv, ffn_conv_b, ffn_down, loss_target, m_mix_norm, m_w_in, m_q_norm, m_k_norm, m_attn_sinks, m_gmlp_v_norm, m_gmlp_ws, m_gmlp_bs, m_attn_out_norm, m_gmlp_out_norm, m_w_out, m_xa_norm, m_mem_norm, m_xa_wq, m_xa_wkv, m_xa_q_norm, m_xa_k_norm, m_xa_wo, m_ffn_norm, m_ffn_up, m_ffn_conv, m_ffn_conv_b, m_ffn_down, v_mix_norm, v_w_in, v_q_norm, v_k_norm, v_attn_sinks, v_gmlp_v_norm, v_gmlp_ws, v_gmlp_bs, v_attn_out_norm, v_gmlp_out_norm, v_w_out, v_xa_norm, v_mem_norm, v_xa_wq, v_xa_wkv, v_xa_q_norm, v_xa_k_norm, v_xa_wo, v_ffn_norm, v_ffn_up, v_ffn_conv, v_ffn_conv_b, v_ffn_down):
    w = dict(mix_norm=mix_norm, w_in=w_in, q_norm=q_norm, k_norm=k_norm, attn_sinks=attn_sinks, gmlp_v_norm=gmlp_v_norm, gmlp_ws=gmlp_ws, gmlp_bs=gmlp_bs, attn_out_norm=attn_out_norm, gmlp_out_norm=gmlp_out_norm, w_out=w_out, xa_norm=xa_norm, mem_norm=mem_norm, xa_wq=xa_wq, xa_wkv=xa_wkv, xa_q_norm=xa_q_norm, xa_k_norm=xa_k_norm, xa_wo=xa_wo, ffn_norm=ffn_norm, ffn_up=ffn_up, ffn_conv=ffn_conv, ffn_conv_b=ffn_conv_b, ffn_down=ffn_down)
    m = dict(mix_norm=m_mix_norm, w_in=m_w_in, q_norm=m_q_norm, k_norm=m_k_norm, attn_sinks=m_attn_sinks, gmlp_v_norm=m_gmlp_v_norm, gmlp_ws=m_gmlp_ws, gmlp_bs=m_gmlp_bs, attn_out_norm=m_attn_out_norm, gmlp_out_norm=m_gmlp_out_norm, w_out=m_w_out, xa_norm=m_xa_norm, mem_norm=m_mem_norm, xa_wq=m_xa_wq, xa_wkv=m_xa_wkv, xa_q_norm=m_xa_q_norm, xa_k_norm=m_xa_k_norm, xa_wo=m_xa_wo, ffn_norm=m_ffn_norm, ffn_up=m_ffn_up, ffn_conv=m_ffn_conv, ffn_conv_b=m_ffn_conv_b, ffn_down=m_ffn_down)
    v = dict(mix_norm=v_mix_norm, w_in=v_w_in, q_norm=v_q_norm, k_norm=v_k_norm, attn_sinks=v_attn_sinks, gmlp_v_norm=v_gmlp_v_norm, gmlp_ws=v_gmlp_ws, gmlp_bs=v_gmlp_bs, attn_out_norm=v_attn_out_norm, gmlp_out_norm=v_gmlp_out_norm, w_out=v_w_out, xa_norm=v_xa_norm, mem_norm=v_mem_norm, xa_wq=v_xa_wq, xa_wkv=v_xa_wkv, xa_q_norm=v_xa_q_norm, xa_k_norm=v_xa_k_norm, xa_wo=v_xa_wo, ffn_norm=v_ffn_norm, ffn_up=v_ffn_up, ffn_conv=v_ffn_conv, ffn_conv_b=v_ffn_conv_b, ffn_down=v_ffn_down)
    ix, iy, ic = lax.axis_index("x"), lax.axis_index("y"), lax.axis_index("c")
    chip = 2 * ix + iy
    chipvec = chip.astype(jnp.int32).reshape(1)
    cvec = ic.astype(jnp.int32).reshape(1)
    order = jnp.stack([chip, ic] + [4 * px + 2 * py + pc for px, py, pc in _peers(ix, iy, ic)]).astype(jnp.int32)

    w_in_all, later, token = _gather_step(w, chipvec)
    zero = token[0, 0]
    sp = {n: w[n][0] + zero for n in SMALL if n != "ffn_conv"}
    positions = positions + zero.astype(jnp.int32)
    started = {}

    def emit(name, g):
        *started[name], token = partials_start(g, name="partials_start_" + name)
        return token

    loss_acc, grad_x, packed = _local_step(x[0], mem[0], positions[0], loss_target[0], w_in_all, later, sp, emit)
    grads, delta, new_m, new_v = _reduce_update(started, packed, w, m, v, chipvec, cvec, order)
    loss = lax.psum(loss_acc[0, 0], ("x", "y", "c"))
    ordered = lambda d: [d[n] for n in WEIGHTS]
    return (loss, grad_x[None], *ordered(grads), *ordered(delta), *ordered(new_m), *ordered(new_v))
```

```python
import math

import jax
import jax.numpy as jnp
from jax import lax
from jax.experimental import pallas as pl
from jax.experimental.pallas import tpu as pltpu

F32 = jnp.float32
BF16 = jnp.bfloat16
MXU_DTYPE = jnp.bfloat16
WIRE_DTYPE = jnp.bfloat16
EPS = 1e-6
VMEM_LIMIT_V7X = 56 * 1024 * 1024

D_MODEL = 1024
HEAD_DIM = 64
BLK = 128
XA_HEADS = 4
XA_DH = 256
MEM_LEN = 256
D_FF = 2816
IN_COLS_DUP = 2048
N_CHIPS = 4
N_DEV = 8

ADAM_LR = 0.001
ADAM_B1 = 0.9
ADAM_B2 = 0.999
ADAM_EPS = 1e-08
ADAM_WD = 0.01
ADAM_STEP = 10

NT = (((1,), (1,)), ((), ()))
TN = (((0,), (0,)), ((), ()))
NN = (((1,), (0,)), ((), ()))
MINF = float(jnp.finfo(jnp.float32).min)
GELU_K0 = math.sqrt(2.0 / math.pi)
GELU_K1 = 0.044715

BS = pl.BlockSpec
SDS = jax.ShapeDtypeStruct
ANY = pl.BlockSpec(memory_space=pl.ANY)
MESH = pl.DeviceIdType.MESH


def _dot(a, b, dims=NN):
    return lax.dot_general(a.astype(MXU_DTYPE), b.astype(MXU_DTYPE), dims, preferred_element_type=F32)


def _segsum(x, bmat):
    hi = x.astype(BF16)
    lo = (x - hi.astype(F32)).astype(BF16)
    return (jnp.dot(hi, bmat, preferred_element_type=F32) + jnp.dot(lo, bmat, preferred_element_type=F32))


def _gelu(x):
    return 0.5 * x * (1.0 + jnp.tanh(GELU_K0 * (x + GELU_K1 * x * x * x)))


def _gelu_grad(x):
    t = jnp.tanh(GELU_K0 * (x + GELU_K1 * x * x * x))
    return 0.5 * (1.0 + t) + 0.5 * x * (1.0 - t * t) * GELU_K0 * (1.0 + 3.0 * GELU_K1 * x * x)


def _gelu_and_grad(x):
    x2 = x * x
    t = jnp.tanh(x * (GELU_K0 * GELU_K1 * x2 + GELU_K0))
    hx = 0.5 * x
    return hx * t + hx, 0.5 * t + 0.5 + hx * (1.0 - t * t) * (3.0 * GELU_K0 * GELU_K1 * x2 + GELU_K0)


def _rms(x):
    return lax.rsqrt(jnp.mean(x * x, axis=-1, keepdims=True) + EPS)


def _rms_bwd(dy, x, g, r):
    dyg = dy * g
    dx = r * dyg - x * (r * r * r) * jnp.mean(dyg * x, axis=-1, keepdims=True)
    return dx, dy * x * r


def _pcall(body, *, name, grid, in_specs, out_specs, out_shape, scratch=(), prefetch=0, after=None):
    params = pltpu.CompilerParams(dimension_semantics=("arbitrary",) * len(grid), vmem_limit_bytes=VMEM_LIMIT_V7X)
    in_specs = list(in_specs)
    kernel_fn = body
    if after is not None:
        n_in = prefetch + len(in_specs)
        in_specs.append(ANY)

        def kernel_fn(*refs):
            return body(*refs[:n_in], *refs[n_in + 1:])

    if prefetch:
        spec = pltpu.PrefetchScalarGridSpec(num_scalar_prefetch=prefetch, grid=grid, in_specs=in_specs,
                                            out_specs=out_specs, scratch_shapes=list(scratch))
        call = pl.pallas_call(kernel_fn, name=name, grid_spec=spec, out_shape=out_shape, compiler_params=params)
    else:
        call = pl.pallas_call(kernel_fn, name=name, grid=grid, in_specs=in_specs, out_specs=out_specs,
                              out_shape=out_shape, scratch_shapes=list(scratch), compiler_params=params)
    return call if after is None else (lambda *args: call(*args, after))


def _tile(n, prefs):
    for p in prefs:
        if p <= n and n % p == 0:
            return p
    return n


def _resident(shape):
    return pl.BlockSpec(shape, lambda *_: (0,) * len(shape), pipeline_mode=pl.Buffered(1))


def _acc_rows(ref, row, val):
    ref[row:row + 1, :] += jnp.sum(val, axis=0, keepdims=True)


def rms_mm(x, g, w3, *, name, tm=1024):
    M, K = x.shape
    Q, _, C = w3.shape
    tm = _tile(M, (tm, 256))

    def body(x_ref, g_ref, w_ref, h_ref, o_ref):
        def write_h():
            xv = x_ref[...]
            h_ref[...] = (xv * _rms(xv) * g_ref[...]).astype(h_ref.dtype)

        if Q == 1:
            write_h()
        else:
            pl.when(pl.program_id(1) == 0)(write_h)
        o_ref[...] = _dot(h_ref[...], w_ref[pl.program_id(1)])

    return _pcall(body, name=name, grid=(M // tm, Q),
                  in_specs=[BS((tm, K), lambda i, j: (i, 0)), BS((1, K), lambda i, j: (0, 0)),
                            _resident((Q, K, C))],
                  out_specs=[BS((tm, K), lambda i, j: (i, 0)), BS((tm, C), lambda i, j: (i, j))],
                  out_shape=[SDS((M, K), MXU_DTYPE), SDS((M, Q * C), F32)])(x, g, w3)


def _nt_chunks(a_ref, w_ref):
    q_n, _, kc = w_ref.shape
    acc = _dot(a_ref[:, 0:kc], w_ref[0], NT)
    for q in range(1, q_n):
        acc = acc + _dot(a_ref[:, q * kc:(q + 1) * kc], w_ref[q], NT)
    return acc


def mm_nt_rms_bwd(a, w3, x, g, dres, *, name, tm=512, after=None):
    M = a.shape[0]
    Q, N, Kc = w3.shape
    tm = _tile(M, (tm, 256))

    def body(a_ref, w_ref, x_ref, g_ref, dr_ref, dx_ref, dg_ref):
        @pl.when(pl.program_id(0) == 0)
        def _():
            dg_ref[...] = jnp.zeros_like(dg_ref)

        xv = x_ref[...]
        dx, dgc = _rms_bwd(_nt_chunks(a_ref, w_ref), xv, g_ref[...], _rms(xv))
        dx_ref[...] = dr_ref[...] + dx
        _acc_rows(dg_ref, 0, dgc)

    row = BS((tm, N), lambda i: (i, 0))
    return _pcall(body, name=name, grid=(M // tm,), after=after,
                  in_specs=[BS((tm, Q * Kc), lambda i: (i, 0)), _resident((Q, N, Kc)), row,
                            BS((1, N), lambda i: (0, 0)), row],
                  out_specs=[row, BS((8, N), lambda i: (0, 0))],
                  out_shape=[SDS((M, N), F32), SDS((8, N), F32)])(a, w3, x, g, dres)


def mm_tn(a, b, *, name, out_dtype, chunks=1, after=None):
    M, K = a.shape
    N = b.shape[1]
    C = N // chunks
    tm = _tile(M, (1024, 256))
    tk = _tile(K, (1408, 1024, 512))
    tn = _tile(C, (1408, 1024, 512))
    per = C // tn
    nm = M // tm

    def body(a_ref, b_ref, o_ref, acc):
        m = pl.program_id(2)

        @pl.when(m == 0)
        def _():
            acc[...] = jnp.zeros_like(acc)

        acc[...] += _dot(a_ref[...], b_ref[...], TN)

        @pl.when(m == nm - 1)
        def _():
            o_ref[...] = acc[...].astype(o_ref.dtype)

    return _pcall(body, name=name, grid=(K // tk, N // tn, nm), after=after,
                  in_specs=[BS((tm, tk), lambda k, n, m: (m, k)), BS((tm, tn), lambda k, n, m: (m, n))],
                  out_specs=BS((None, tk, tn), lambda k, n, m: (n // per, k, n % per)),
                  out_shape=SDS((chunks, K, C), out_dtype), scratch=[pltpu.VMEM((tk, tn), F32)])(a, b)


def _lane(shape):
    return lax.broadcasted_iota(jnp.int32, shape, 1)


def _head_means(slabs, bmat):
    tm = slabs[0].shape[0]
    means = _segsum(jnp.concatenate(slabs, axis=0), bmat) * (1.0 / HEAD_DIM)
    return [means[i * tm:(i + 1) * tm] for i in range(len(slabs))]


def _half_swap(x, first):
    return jnp.where(first, pltpu.roll(x, 96, 1), pltpu.roll(x, 32, 1))


def _by_head(x2, lo):
    z = jnp.zeros((BLK, 128), x2.dtype)
    parts = []
    for s in range(2):
        xs = x2[:, s * 128:(s + 1) * 128]
        parts += [jnp.where(lo, xs, z), jnp.where(lo, z, xs)]
    return jnp.concatenate(parts, axis=0)


def _from_heads(o4, lo):
    return jnp.concatenate([jnp.where(lo, o4[0:BLK], o4[BLK:2 * BLK]),
                            jnp.where(lo, o4[2 * BLK:3 * BLK], o4[3 * BLK:])], axis=1)


def _swa_probs(q2, kd, sink, n, lo):
    qp = _by_head(q2, lo)
    sc = _dot(qp, kd, NT) * (1.0 / math.sqrt(HEAD_DIM))
    r_i = lax.broadcasted_iota(jnp.int32, (4 * BLK, 2 * BLK), 0)
    k_j = lax.broadcasted_iota(jnp.int32, (4 * BLK, 2 * BLK), 1)
    diff = (r_i & (BLK - 1)) + BLK - k_j
    mask = (diff >= 0) & (diff < BLK) & ((k_j >= BLK) | (n > 0))
    sc = jnp.where(mask, sc, MINF)
    m = jnp.maximum(jnp.max(sc, axis=1, keepdims=True), sink)
    p = jnp.exp(sc - m)
    es = jnp.exp(sink - m)
    inv = 1.0 / (_segsum(p, jnp.ones((2 * BLK, BLK), BF16)) + es)
    return qp, p * jnp.concatenate([inv, inv], axis=1), es * inv[:, :1]


def mixer_core_fwd(proj, cos, sin, gq, gk, gvn, bmat, sinkcol, gao, w2, bsl, ggo):
    S = proj.shape[0]
    sub = 4 if S % (4 * BLK) == 0 else 1

    def body(p_ref, c_ref, s_ref, gq_ref, gk_ref, gvn_ref, b_ref, sk_ref, gao_ref, w2_ref, bsl_ref, ggo_ref,
             qr_ref, kr_ref, vb_ref, gu_ref, gvo_ref, at_ref, gm_ref, y_ref, k_prev, v_prev):
        n = pl.program_id(0)

        @pl.when(n == 0)
        def _():
            k_prev[...] = jnp.zeros_like(k_prev)
            v_prev[...] = jnp.zeros_like(v_prev)

        bm = b_ref[...]
        first = (_lane((BLK, 128)) & 63) < 32
        lo = _lane((BLK, 128)) < 64
        for sb in range(sub):
            rs = slice(sb * BLK, (sb + 1) * BLK)
            cos_v, sin_v = c_ref[rs, :], s_ref[rs, :]
            slabs = [p_ref[rs, s * 128:(s + 1) * 128] for s in range(6)]
            for s, (slab, ms) in enumerate(zip(slabs, _head_means([x * x for x in slabs], bm))):
                qn = slab * lax.rsqrt(ms + EPS) * (gq_ref[...] if s < 4 else gk_ref[...])
                out = qn * cos_v + _half_swap(qn, first) * sin_v
                if s < 4:
                    qr_ref[rs, s * 128:(s + 1) * 128] = out.astype(qr_ref.dtype)
                else:
                    kr_ref[rs, (s - 4) * 128:(s - 3) * 128] = out.astype(kr_ref.dtype)
            vb_ref[rs, :] = p_ref[rs, 768:1024].astype(vb_ref.dtype)
            gu_ref[rs, :] = _gelu(p_ref[rs, 1024:1536])
            gv = _gelu(p_ref[rs, 1536:2048])
            gvo_ref[rs, :] = (gv * _rms(gv) * gvn_ref[...]).astype(gvo_ref.dtype)

            before = slice((sb - 1) * BLK, sb * BLK)
            for h in range(2):
                hs, qs = slice(h * 128, (h + 1) * 128), slice(h * 256, (h + 1) * 256)
                k_before = k_prev[:, hs] if sb == 0 else kr_ref[before, hs]
                v_before = v_prev[:, hs] if sb == 0 else vb_ref[before, hs]
                kd = jnp.concatenate([k_before, kr_ref[rs, hs]], axis=0)
                vd = jnp.concatenate([v_before, vb_ref[rs, hs]], axis=0)
                sink = jnp.concatenate([sk_ref[2 * h], sk_ref[2 * h + 1]], axis=0)
                _, p, _ = _swa_probs(qr_ref[rs, qs], kd, sink, n * sub + sb, lo)
                at_ref[rs, qs] = _from_heads(_dot(p, vd), lo)

            for j in range(4):
                sl = slice(j * 128, (j + 1) * 128)
                m2 = _dot(w2_ref[j], gvo_ref[rs, sl])
                mixed = jnp.where(lo, m2[:BLK], m2[BLK:]) + bsl_ref[j]
                gm_ref[rs, sl] = gu_ref[rs, sl] * mixed
            a, gm = at_ref[rs, :], gm_ref[rs, :]
            y_ref[rs, :512] = (a * _rms(a) * gao_ref[...]).astype(y_ref.dtype)
            y_ref[rs, 512:] = (gm * _rms(gm) * ggo_ref[...]).astype(y_ref.dtype)
        k_prev[...] = kr_ref[(sub - 1) * BLK:, :]
        v_prev[...] = vb_ref[(sub - 1) * BLK:, :]

    row = lambda w: BS((sub * BLK, w), lambda n: (n, 0))
    const = lambda *shape: BS(shape, lambda n: (0,) * len(shape))
    return _pcall(body, name="mixer_core_fwd", grid=(S // (sub * BLK),),
                  in_specs=[row(IN_COLS_DUP), row(128), row(128), const(1, 128), const(1, 128), const(1, 512),
                            const(128, 128), const(4, 2 * BLK, 1), const(1, 512), const(4, 2 * BLK, BLK),
                            const(4, BLK, 128), const(1, 512)],
                  out_specs=[row(512), row(256), row(256), row(512), row(512), row(512), row(512), row(1024)],
                  out_shape=[SDS((S, 512), MXU_DTYPE), SDS((S, 256), MXU_DTYPE), SDS((S, 256), MXU_DTYPE),
                             SDS((S, 512), F32), SDS((S, 512), MXU_DTYPE), SDS((S, 512), F32), SDS((S, 512), F32),
                             SDS((S, 1024), MXU_DTYPE)],
                  scratch=[pltpu.VMEM((BLK, 256), MXU_DTYPE), pltpu.VMEM((BLK, 256), MXU_DTYPE)])(
        proj, cos, sin, gq, gk, gvn, bmat, sinkcol, gao, w2, bsl, ggo)


def mem_pre(kv, gxk):
    def body(kv_ref, g_ref, kn_ref, vb_ref):
        for h in range(XA_HEADS):
            sl = slice(h * XA_DH, (h + 1) * XA_DH)
            k = kv_ref[:, sl]
            kn_ref[:, sl] = (k * _rms(k) * g_ref[...]).astype(kn_ref.dtype)
        vb_ref[...] = kv_ref[:, 1024:2048].astype(vb_ref.dtype)

    full = lambda r, w: BS((r, w), lambda i: (0, 0))
    return _pcall(body, name="mem_pre", grid=(1,), in_specs=[full(MEM_LEN, 2048), full(1, XA_DH)],
                  out_specs=[full(MEM_LEN, 1024), full(MEM_LEN, 1024)],
                  out_shape=[SDS((MEM_LEN, 1024), MXU_DTYPE), SDS((MEM_LEN, 1024), MXU_DTYPE)])(kv, gxk)


def _xa_probs(qh, g, kn_h):
    r = _rms(qh)
    qn = qh * r * g
    s = _dot(qn, kn_h, NT) * (1.0 / math.sqrt(XA_DH))
    p = jnp.exp(s - jnp.max(s, axis=1, keepdims=True))
    return r, qn, p * (1.0 / jnp.sum(p, axis=1, keepdims=True))


def xattn_block_fwd(y, w_out, x, g, wq, kn, vb, gxq, wo):
    S, D = x.shape
    tm = _tile(S, (512, 256))

    def body(y_ref, wout_ref, x_ref, g_ref, wq_ref, kn_ref, vb_ref, gxq_ref, wo_ref, x1_ref, h_ref, q_ref, o_ref,
             x2_ref):
        x1_ref[...] = _dot(y_ref[...], wout_ref[...]) + x_ref[...]
        xv = x1_ref[...]
        h_ref[...] = (xv * _rms(xv) * g_ref[...]).astype(h_ref.dtype)
        q_ref[...] = _dot(h_ref[...], wq_ref[...])
        for h in range(XA_HEADS):
            sl = slice(h * XA_DH, (h + 1) * XA_DH)
            _, _, p = _xa_probs(q_ref[:, sl], gxq_ref[...], kn_ref[:, sl])
            o_ref[:, sl] = _dot(p, vb_ref[:, sl]).astype(o_ref.dtype)
        x2_ref[...] = _dot(o_ref[...], wo_ref[...]) + x1_ref[...]

    row = BS((tm, D), lambda i: (i, 0))
    full = lambda r, w: BS((r, w), lambda i: (0, 0))
    return _pcall(body, name="xattn_block_fwd", grid=(S // tm,),
                  in_specs=[BS((tm, y.shape[1]), lambda i: (i, 0)), _resident(w_out.shape), row, full(1, D),
                            _resident(wq.shape), full(MEM_LEN, D), full(MEM_LEN, D), full(1, XA_DH),
                            _resident(wo.shape)],
                  out_specs=[row, row, row, row, row],
                  out_shape=[SDS((S, D), F32), SDS((S, D), MXU_DTYPE), SDS((S, D), F32), SDS((S, D), MXU_DTYPE),
                             SDS((S, D), F32)])(y, w_out, x, g, wq, kn, vb, gxq, wo)


CONV_COLS = 1408


def _conv_taps(a_ref, halo_ref, w_ref, b_ref, cols, first_tile):
    a = a_ref[:, cols]
    row = lax.broadcasted_iota(jnp.int32, (8, a.shape[1]), 0)
    h6 = jnp.where(first_tile, 0.0, halo_ref[6:7, cols])
    h7 = jnp.where(first_tile, 0.0, halo_ref[7:8, cols])
    r1, r2 = pltpu.roll(a, 1, 0), pltpu.roll(a, 2, 0)
    a1 = jnp.concatenate([jnp.where(row == 0, h7, r1[0:8]), r1[8:]], axis=0)
    a2 = jnp.concatenate([jnp.where(row == 0, h6, jnp.where(row == 1, h7, r2[0:8])), r2[8:]], axis=0)
    c = w_ref[2:3, cols] * a + w_ref[1:2, cols] * a1 + w_ref[0:1, cols] * a2 + b_ref[:, cols]
    return c, (a2, a1, a)


def _conv_specs(tm):
    halo_blocks = tm // 8
    return [BS((tm, D_FF), lambda i: (i, 0)), BS((tm, D_FF), lambda i: (i, 1)),
            BS((8, D_FF), lambda i: (jnp.maximum(i * halo_blocks - 1, 0), 0)),
            BS((8, D_FF), lambda i: (jnp.maximum(i * halo_blocks - 1, 0), 1)),
            BS((3, D_FF), lambda i: (0, 0)), BS((3, D_FF), lambda i: (0, 1)),
            BS((1, D_FF), lambda i: (0, 0)), BS((1, D_FF), lambda i: (0, 1))]


def ffn_fwd_loss(x2, g, w_up3, cw, cb, w_down, target):
    S, D = x2.shape
    Q, _, C = w_up3.shape
    tm = _tile(S, (256,))

    def body(x_ref, g_ref, wu_ref, cw_ref, cb_ref, wd_ref, t_ref, h_ref, a_ref, f_ref, d_ref, l_ref, tail):
        first_tile = pl.program_id(0) == 0

        @pl.when(first_tile)
        def _():
            l_ref[...] = jnp.zeros_like(l_ref)
            tail[...] = jnp.zeros_like(tail)

        xv = x_ref[...]
        h_ref[...] = (xv * _rms(xv) * g_ref[...]).astype(h_ref.dtype)
        for q in range(Q):
            a_ref[:, q * C:(q + 1) * C] = _dot(h_ref[...], wu_ref[q])
        for c0 in range(0, D_FF, CONV_COLS):
            cols, ucols = slice(c0, c0 + CONV_COLS), slice(D_FF + c0, D_FF + c0 + CONV_COLS)
            cg, _ = _conv_taps(a_ref, tail, cw_ref, cb_ref, cols, first_tile)
            cu, _ = _conv_taps(a_ref, tail, cw_ref, cb_ref, ucols, first_tile)
            f_ref[:, cols] = (_gelu(cg) * cu).astype(f_ref.dtype)
        tail[...] = a_ref[tm - 8:tm, :]
        e = _dot(f_ref[...], wd_ref[...]) + xv - t_ref[...]
        d_ref[...] = e * (1.0 / D)
        l_ref[...] += jnp.sum(e * e) * (0.5 / D)

    row = lambda w: BS((tm, w), lambda i: (i, 0))
    const = lambda r, w: BS((r, w), lambda i: (0, 0))
    return _pcall(body, name="ffn_fwd_loss", grid=(S // tm,),
                  in_specs=[row(D), const(1, D), _resident(w_up3.shape), const(3, 2 * D_FF), const(1, 2 * D_FF),
                            _resident(w_down.shape), row(D)],
                  out_specs=[row(D), row(2 * D_FF), row(D_FF), row(D), const(8, 128)],
                  out_shape=[SDS((S, D), MXU_DTYPE), SDS((S, 2 * D_FF), F32), SDS((S, D_FF), MXU_DTYPE),
                             SDS((S, D), F32), SDS((8, 128), F32)],
                  scratch=[pltpu.VMEM((8, 2 * D_FF), F32)])(x2, g, w_up3, cw, cb, w_down, target)


def convgate_bwd(a, dx3, w3, cw, cb, after=None):
    S = a.shape[0]
    tm = _tile(S, (256,))

    def body(ag_ref, au_ref, hg_ref, hu_ref, wg_ref, wu_ref, bg_ref, bu_ref, dx_ref, wd_ref, dc_ref, gw_ref, df_ref):
        first_tile = pl.program_id(0) == 0

        @pl.when(first_tile)
        def _():
            gw_ref[...] = jnp.zeros_like(gw_ref)

        df_ref[...] = _nt_chunks(dx_ref, wd_ref)
        for c0 in range(0, D_FF, CONV_COLS):
            cols, ucols = slice(c0, c0 + CONV_COLS), slice(D_FF + c0, D_FF + c0 + CONV_COLS)
            cg, g_taps = _conv_taps(ag_ref, hg_ref, wg_ref, bg_ref, cols, first_tile)
            cu, u_taps = _conv_taps(au_ref, hu_ref, wu_ref, bu_ref, cols, first_tile)
            df_v = df_ref[:, cols]
            gate, gate_grad = _gelu_and_grad(cg)
            dcg = df_v * cu * gate_grad
            dcu = df_v * gate
            dc_ref[:, cols] = dcg
            dc_ref[:, ucols] = dcu
            for col, dcv, taps in ((cols, dcg, g_taps), (ucols, dcu, u_taps)):
                for j in range(3):
                    gw_ref[j:j + 1, col] += jnp.sum(dcv * taps[j], axis=0, keepdims=True)
                gw_ref[3:4, col] += jnp.sum(dcv, axis=0, keepdims=True)

    return _pcall(body, name="convgate_bwd", grid=(S // tm,), after=after,
                  in_specs=_conv_specs(tm) + [BS((tm, dx3.shape[1]), lambda i: (i, 0)), _resident(w3.shape)],
                  out_specs=[BS((tm, 2 * D_FF), lambda i: (i, 0)), BS((8, 2 * D_FF), lambda i: (0, 0))],
                  out_shape=[SDS((S, 2 * D_FF), F32), SDS((8, 2 * D_FF), F32)],
                  scratch=[pltpu.VMEM((tm, D_FF), F32)])(a, a, a, a, cw, cw, cb, cb, dx3, w3)


def conv_transpose_rms_bwd(dc, cw, w3, x, g, dres):
    S, C = dc.shape
    Q, N, Kc = w3.shape
    tm = _tile(S, (256,))
    nt = S // tm
    halo_blocks = tm // 8

    def body(dc_ref, halo_ref, cw_ref, w_ref, x_ref, g_ref, dr_ref, da_ref, dx_ref, dg_ref):
        @pl.when(pl.program_id(0) == 0)
        def _():
            dg_ref[...] = jnp.zeros_like(dg_ref)

        last_tile = pl.program_id(0) == nt - 1
        row = lax.broadcasted_iota(jnp.int32, (8, CONV_COLS), 0)
        for c0 in range(0, C, CONV_COLS):
            cols = slice(c0, c0 + CONV_COLS)
            h0 = jnp.where(last_tile, 0.0, halo_ref[0:1, cols])
            h1 = jnp.where(last_tile, 0.0, halo_ref[1:2, cols])
            dc_v = dc_ref[:, cols]
            r1, r2 = pltpu.roll(dc_v, tm - 1, 0), pltpu.roll(dc_v, tm - 2, 0)
            n1 = jnp.concatenate([r1[:tm - 8], jnp.where(row == 7, h0, r1[tm - 8:])], axis=0)
            n2 = jnp.concatenate([r2[:tm - 8], jnp.where(row == 7, h1, jnp.where(row == 6, h0, r2[tm - 8:]))], axis=0)
            da_ref[:, cols] = (cw_ref[2:3, cols] * dc_v + cw_ref[1:2, cols] * n1
                               + cw_ref[0:1, cols] * n2).astype(da_ref.dtype)
        xv = x_ref[...]
        dx, dgc = _rms_bwd(_nt_chunks(da_ref, w_ref), xv, g_ref[...], _rms(xv))
        dx_ref[...] = dr_ref[...] + dx
        _acc_rows(dg_ref, 0, dgc)

    row_n = BS((tm, N), lambda i: (i, 0))
    return _pcall(body, name="conv_transpose_rms_bwd", grid=(nt,),
                  in_specs=[BS((tm, C), lambda i: (i, 0)),
                            BS((8, C), lambda i: (jnp.minimum((i + 1) * halo_blocks, S // 8 - 1), 0)),
                            BS((3, C), lambda i: (0, 0)), _resident((Q, N, Kc)), row_n, BS((1, N), lambda i: (0, 0)),
                            row_n],
                  out_specs=[BS((tm, C), lambda i: (i, 0)), row_n, BS((8, N), lambda i: (0, 0))],
                  out_shape=[SDS((S, C), MXU_DTYPE), SDS((S, N), F32), SDS((8, N), F32)])(dc, dc, cw, w3, x, g, dres)


def xattn_block_bwd(dx2, wo3, qx, kn, vb, gxq, wq3, x1, g, wout3, attn, gm, gao, ggo, after=None):
    S, D = qx.shape
    tm = _tile(S, (512, 256))
    hw = D // 2

    def body(dx2_ref, wo_ref, q_ref, kn_ref, vb_ref, gxq_ref, wq_ref, x_ref, g_ref, wout_ref, at_ref, gm_ref,
             gao_ref, ggo_ref, dq_ref, dx_ref, dkn_ref, dv_ref, dgq_ref, dg_ref, da_ref, dgm_ref, dgy_ref):
        @pl.when(pl.program_id(0) == 0)
        def _():
            for ref in (dkn_ref, dv_ref, dgq_ref, dg_ref, dgy_ref):
                ref[...] = jnp.zeros_like(ref)

        gq = gxq_ref[...]
        do_all = _nt_chunks(dx2_ref, wo_ref)
        for h in range(XA_HEADS):
            sl = slice(h * XA_DH, (h + 1) * XA_DH)
            qh, do = q_ref[:, sl], do_all[:, sl]
            r, qn, p = _xa_probs(qh, gq, kn_ref[:, sl])
            dp = _dot(do, vb_ref[:, sl], NT)
            ds = p * (dp - jnp.sum(dp * p, axis=1, keepdims=True)) * (1.0 / math.sqrt(XA_DH))
            dqn = _dot(ds, kn_ref[:, sl])
            dkn_ref[:, sl] += _dot(ds, qn, TN)
            dv_ref[:, sl] += _dot(p, do, TN)
            dqh, dgc = _rms_bwd(dqn, qh, gq, r)
            dq_ref[:, sl] = dqh.astype(dq_ref.dtype)
            _acc_rows(dgq_ref, 0, dgc)
        xv = x_ref[...]
        dx, dgc = _rms_bwd(_nt_chunks(dq_ref, wq_ref), xv, g_ref[...], _rms(xv))
        dx1 = dx2_ref[...] + dx
        dx_ref[...] = dx1
        _acc_rows(dg_ref, 0, dgc)
        dy = _dot(dx1, wout_ref[0], NT)
        av, gmv = at_ref[...], gm_ref[...]
        da, dga = _rms_bwd(dy[:, :hw], av, gao_ref[...], _rms(av))
        dgm, dgg = _rms_bwd(dy[:, hw:], gmv, ggo_ref[...], _rms(gmv))
        da_ref[...] = da
        dgm_ref[...] = dgm
        dgy_ref[0:1, :hw] += jnp.sum(dga, axis=0, keepdims=True)
        dgy_ref[0:1, hw:] += jnp.sum(dgg, axis=0, keepdims=True)

    row = BS((tm, D), lambda i: (i, 0))
    half = BS((tm, hw), lambda i: (i, 0))
    full = lambda r, w: BS((r, w), lambda i: (0, 0))
    return _pcall(body, name="xattn_block_bwd", grid=(S // tm,), after=after,
                  in_specs=[row, _resident(wo3.shape), row, full(MEM_LEN, D), full(MEM_LEN, D), full(1, XA_DH),
                            _resident(wq3.shape), row, full(1, D), _resident(wout3.shape), half, half, full(1, hw),
                            full(1, hw)],
                  out_specs=[row, row, full(MEM_LEN, D), full(MEM_LEN, D), full(8, XA_DH), full(8, D), half, half,
                             full(8, D)],
                  out_shape=[SDS((S, D), MXU_DTYPE), SDS((S, D), F32), SDS((MEM_LEN, D), F32), SDS((MEM_LEN, D), F32),
                             SDS((8, XA_DH), F32), SDS((8, D), F32), SDS((S, hw), F32), SDS((S, hw), F32),
                             SDS((8, D), F32)])(dx2, wo3, qx, kn, vb, gxq, wq3, x1, g, wout3, attn, gm, gao, ggo)


def mem_bwd(kv, dkn, dvb, gxk, after=None):
    def body(kv_ref, dkn_ref, dv_ref, g_ref, dkv_ref, dg_ref):
        dg_ref[...] = jnp.zeros_like(dg_ref)
        for h in range(XA_HEADS):
            sl = slice(h * XA_DH, (h + 1) * XA_DH)
            k = kv_ref[:, sl]
            dk, dgc = _rms_bwd(dkn_ref[:, sl], k, g_ref[...], _rms(k))
            dkv_ref[:, sl] = dk.astype(dkv_ref.dtype)
            _acc_rows(dg_ref, 0, dgc)
        dkv_ref[:, 1024:2048] = dv_ref[...].astype(dkv_ref.dtype)

    full = lambda r, w: BS((r, w), lambda i: (0, 0))
    return _pcall(body, name="mem_bwd", grid=(1,), after=after,
                  in_specs=[full(MEM_LEN, 2048), full(MEM_LEN, 1024), full(MEM_LEN, 1024), full(1, XA_DH)],
                  out_specs=[full(MEM_LEN, 2048), full(8, XA_DH)],
                  out_shape=[SDS((MEM_LEN, 2048), MXU_DTYPE), SDS((8, XA_DH), F32)])(kv, dkn, dvb, gxk)


def _norm_rope_bwd(slabs, douts, g, bm, cos_v, sin_v, first):
    dqns = [d * cos_v + _half_swap(d * sin_v, first) for d in douts]
    rs = [lax.rsqrt(ms + EPS) for ms in _head_means([x * x for x in slabs], bm)]
    projs = _head_means([dqn * g * x for dqn, x in zip(dqns, slabs)], bm)
    dxs = [r * (dqn * g) - x * (r * r * r) * pr for x, dqn, r, pr in zip(slabs, dqns, rs, projs)]
    return dxs, [dqn * x * r for x, dqn, r in zip(slabs, dqns, rs)]


def mixer_core_bwd(proj, cos, sin, gq, gk, gvg, bmat, qr, kr, vb, sinkcol, dattn, dgm, gvn, gu, w2, w2t, bsl,
                   after=None):
    S = qr.shape[0]
    nb = S // BLK

    def body(p_ref, c_ref, s_ref, gq_ref, gk_ref, gvg_ref, b_ref, q_ref, kc_ref, kp_ref, vc_ref, vp_ref, sk_ref,
             do_ref, dgm_ref, gvn_ref, gu_ref, w2_ref, w2t_ref, bsl_ref,
             dp_ref, dsk_ref, dws_ref, dbl_ref, dgq_ref, dgk_ref, dgv_ref,
             carry_k, carry_v, done_k, done_v, dq_keep, dgu_keep, dgvn_keep):
        n = pl.program_id(0)

        @pl.when(n == 0)
        def _():
            for ref in (dsk_ref, dws_ref, dbl_ref, dgq_ref, dgk_ref, dgv_ref, carry_k, carry_v, dq_keep, dgu_keep,
                        dgvn_keep):
                ref[...] = jnp.zeros_like(ref)

        live = (n < nb).astype(F32)
        cos_v, sin_v, bm = c_ref[...], s_ref[...], b_ref[...]
        first = (_lane((BLK, 128)) & 63) < 32
        lo = _lane((BLK, 128)) < 64

        dxs, dgs = _norm_rope_bwd([p_ref[:, s * 128:(s + 1) * 128] for s in range(4)],
                                  [dq_keep[:, s * 128:(s + 1) * 128] for s in range(4)], gq_ref[...], bm,
                                  cos_v, sin_v, first)
        for s, (dx, dg) in enumerate(zip(dxs, dgs)):
            dp_ref[:, s * 128:(s + 1) * 128] = dx.astype(dp_ref.dtype)
            _acc_rows(dgq_ref, 0, dg)
        dp_ref[:, 1024:1536] = (dgu_keep[...] * _gelu_grad(p_ref[:, 1024:1536])).astype(dp_ref.dtype)
        gv, gv_grad = _gelu_and_grad(p_ref[:, 1536:2048])
        dgv, dgc = _rms_bwd(dgvn_keep[...], gv, gvg_ref[...], _rms(gv))
        dp_ref[:, 1536:2048] = (dgv * gv_grad).astype(dp_ref.dtype)
        _acc_rows(dgv_ref, 0, dgc)

        for h in range(2):
            hs, qs = slice(h * 128, (h + 1) * 128), slice(h * 256, (h + 1) * 256)
            kd = jnp.concatenate([kp_ref[:, hs], kc_ref[:, hs]], axis=0)
            vd = jnp.concatenate([vp_ref[:, hs], vc_ref[:, hs]], axis=0)
            sink = jnp.concatenate([sk_ref[2 * h], sk_ref[2 * h + 1]], axis=0)
            qp, p, psink = _swa_probs(q_ref[:, qs], kd, sink, n, lo)
            dop = _by_head(do_ref[:, qs], lo)
            dp = _dot(dop, vd, NT)
            delta = jnp.sum(dp * p, axis=1, keepdims=True)
            ds = p * (dp - delta) * (1.0 / math.sqrt(HEAD_DIM))
            dsink = -psink * delta * live
            dsk_ref[2 * h] += dsink[:2 * BLK]
            dsk_ref[2 * h + 1] += dsink[2 * BLK:]
            dq_keep[:, qs] = _from_heads(_dot(ds, kd), lo)
            dkd = _dot(ds, qp, TN)
            dvd = _dot(p, dop, TN)
            done_k[:, hs] = carry_k[:, hs] + live * dkd[:BLK]
            done_v[:, hs] = carry_v[:, hs] + live * dvd[:BLK]
            carry_k[:, hs] = dkd[BLK:]
            carry_v[:, hs] = dvd[BLK:]
        for j in range(4):
            sl = slice(j * 128, (j + 1) * 128)
            gvn_s = gvn_ref[:, sl]
            m2 = _dot(w2_ref[j], gvn_s)
            mixed = jnp.where(lo, m2[:BLK], m2[BLK:]) + bsl_ref[j]
            dgm_s = dgm_ref[:, sl]
            dgu_keep[:, sl] = dgm_s * mixed
            dmx = dgm_s * gu_ref[:, sl] * live
            d2 = _dot(w2t_ref[j], dmx)
            dgvn_keep[:, sl] = jnp.where(lo, d2[:BLK], d2[BLK:])
            z = jnp.zeros_like(dmx)
            dws_ref[2 * j] += _dot(jnp.where(lo, dmx, z), gvn_s, NT)
            dws_ref[2 * j + 1] += _dot(jnp.where(lo, z, dmx), gvn_s, NT)
            dbl_ref[j] += dmx

        dxs, dgs = _norm_rope_bwd([p_ref[:, 512 + s * 128:640 + s * 128] for s in range(2)],
                                  [done_k[:, s * 128:(s + 1) * 128] for s in range(2)], gk_ref[...], bm,
                                  cos_v, sin_v, first)
        for s, (dx, dg) in enumerate(zip(dxs, dgs)):
            dp_ref[:, 512 + s * 128:640 + s * 128] = dx.astype(dp_ref.dtype)
            _acc_rows(dgk_ref, 0, dg)
        dp_ref[:, 768:1024] = done_v[...].astype(dp_ref.dtype)

    last = nb - 1
    cur = lambda w: BS((BLK, w), lambda n: (jnp.minimum(n, last), 0))
    prev = lambda w: BS((BLK, w), lambda n: (jnp.clip(n - 1, 0, last), 0))
    done = lambda w: BS((BLK, w), lambda n: (jnp.maximum(n - 1, 0), 0))
    const = lambda *shape: BS(shape, lambda n: (0,) * len(shape))
    return _pcall(body, name="mixer_core_bwd", grid=(nb + 1,), after=after,
                  in_specs=[done(IN_COLS_DUP), done(128), done(128), const(1, 128), const(1, 128), const(1, 512),
                            const(128, 128), cur(512), cur(256), prev(256), cur(256), prev(256),
                            const(4, 2 * BLK, 1), cur(512), cur(512), cur(512), cur(512), const(4, 2 * BLK, BLK),
                            const(4, 2 * BLK, BLK), const(4, BLK, 128)],
                  out_specs=[done(IN_COLS_DUP), const(4, 2 * BLK, 1), const(8, BLK, BLK), const(4, BLK, 128),
                             const(8, 128), const(8, 128), const(8, 512)],
                  out_shape=[SDS((S, IN_COLS_DUP), MXU_DTYPE), SDS((4, 2 * BLK, 1), F32), SDS((8, BLK, BLK), F32),
                             SDS((4, BLK, 128), F32), SDS((8, 128), F32), SDS((8, 128), F32), SDS((8, 512), F32)],
                  scratch=[pltpu.VMEM((BLK, 256), F32)] * 4 + [pltpu.VMEM((BLK, 512), F32)] * 3)(
        proj, cos, sin, gq, gk, gvg, bmat, qr, kr, kr, vb, vb, sinkcol, dattn, dgm, gvn, gu, w2, w2t, bsl)


BIG = (("w_in", (1024, 448), True), ("w_out", (256, 1024), False), ("xa_wq", (256, 1024), False),
       ("xa_wkv", (1024, 512), True), ("xa_wo", (256, 1024), False), ("ffn_up", (1024, 1408), True),
       ("ffn_down", (704, 1024), False))
BIG_NAMES = tuple(n for n, _, _ in BIG)
SMALL_VECS = (("mix_norm", 1024), ("q_norm", 64), ("k_norm", 64), ("attn_sinks", 8), ("gmlp_v_norm", 512),
              ("attn_out_norm", 512), ("gmlp_out_norm", 512), ("xa_norm", 1024), ("mem_norm", 1024),
              ("xa_q_norm", 256), ("xa_k_norm", 256), ("ffn_norm", 1024), ("ffn_conv_b", 5632))
SMALL = tuple(n for n, _ in SMALL_VECS) + ("gmlp_bs", "gmlp_ws", "ffn_conv")
WEIGHTS = ("mix_norm", "w_in", "q_norm", "k_norm", "attn_sinks", "gmlp_v_norm", "gmlp_ws", "gmlp_bs",
           "attn_out_norm", "gmlp_out_norm", "w_out", "xa_norm", "mem_norm", "xa_wq", "xa_wkv", "xa_q_norm",
           "xa_k_norm", "xa_wo", "ffn_norm", "ffn_up", "ffn_conv", "ffn_conv_b", "ffn_down")
CONV_SHARD = (3, 1408)
CONV_LANE_ROWS = CONV_SHARD[1] // 128
CONV_CHIP_ROWS = 40


def _small_rows():
    rows, r = {}, 0
    for n, length in SMALL_VECS:
        rows[n] = r
        r += -(-length // 128)
    r += -r % 8
    rows["gmlp_bs"] = r
    r += 8
    rows["gmlp_ws"] = r
    r += 8 * BLK
    rows["ffn_conv"] = r
    r += N_CHIPS * CONV_CHIP_ROWS
    return rows, r


SMALL_ROW, SMALL_ROWS = _small_rows()


def pack_small(dg_mix, dgq, dgk, dsk, dg_gvn, dg_y, dg_xa, dg_mem, dg_xq, dg_xk, dg_ffn, gcw, dbl, dws):
    def body(mix_ref, q_ref, k_ref, sk_ref, gvn_ref, y_ref, xa_ref, mem_ref, xq_ref, xk_ref, ffn_ref, cw_ref,
             dbl_ref, dws_ref, o_ref):
        o_ref[...] = jnp.zeros_like(o_ref)
        lane = _lane((1, 128))

        def put(name, src_ref, row, lane0, length):
            for k in range(length // 128):
                o_ref[SMALL_ROW[name] + k:SMALL_ROW[name] + k + 1, :] = src_ref[row:row + 1, lane0 + k * 128:lane0 + (k + 1) * 128]

        put("mix_norm", mix_ref, 0, 0, 1024)
        for name, ref in (("q_norm", q_ref), ("k_norm", k_ref)):
            v = ref[0:1, :]
            o_ref[SMALL_ROW[name]:SMALL_ROW[name] + 1, :] = jnp.where(lane < HEAD_DIM, v + pltpu.roll(v, 64, 1), 0.0)
        sinks = jnp.zeros((1, 128), F32)
        for s in range(4):
            col = sk_ref[s]
            sinks = sinks + jnp.where(lane == 2 * s, jnp.sum(col[:BLK]), 0.0) + jnp.where(lane == 2 * s + 1, jnp.sum(col[BLK:]), 0.0)
        o_ref[SMALL_ROW["attn_sinks"]:SMALL_ROW["attn_sinks"] + 1, :] = sinks
        put("gmlp_v_norm", gvn_ref, 0, 0, 512)
        put("attn_out_norm", y_ref, 0, 0, 512)
        put("gmlp_out_norm", y_ref, 0, 512, 512)
        put("xa_norm", xa_ref, 0, 0, 1024)
        put("mem_norm", mem_ref, 0, 0, 1024)
        put("xa_q_norm", xq_ref, 0, 0, 256)
        put("xa_k_norm", xk_ref, 0, 0, 256)
        put("ffn_norm", ffn_ref, 0, 0, 1024)
        put("ffn_conv_b", cw_ref, 3, 0, 2 * D_FF)
        r8 = lax.broadcasted_iota(jnp.int32, (8, 128), 0)
        l8 = _lane((8, 128))
        bs = jnp.zeros((8, BLK), F32)
        for j in range(4):
            sel = (((r8 == 2 * j) & (l8 < 64)) | ((r8 == 2 * j + 1) & (l8 >= 64))).astype(F32).astype(BF16)
            xj = dbl_ref[j]
            hi = xj.astype(BF16)
            lo = (xj - hi.astype(F32)).astype(BF16)
            bs = bs + lax.dot_general(sel, hi, NT, preferred_element_type=F32) + lax.dot_general(sel, lo, NT, preferred_element_type=F32)
        o_ref[SMALL_ROW["gmlp_bs"]:SMALL_ROW["gmlp_bs"] + 8, :] = bs
        causal = lax.broadcasted_iota(jnp.int32, (BLK, BLK), 0) >= lax.broadcasted_iota(jnp.int32, (BLK, BLK), 1)
        for h in range(8):
            r0 = SMALL_ROW["gmlp_ws"] + h * BLK
            o_ref[r0:r0 + BLK, :] = jnp.where(causal, dws_ref[h], 0.0)
        for q in range(N_CHIPS):
            for j in range(3):
                for k in range(CONV_LANE_ROWS):
                    r0 = SMALL_ROW["ffn_conv"] + q * CONV_CHIP_ROWS + j * CONV_LANE_ROWS + k
                    l0 = (q * CONV_LANE_ROWS + k) * 128
                    o_ref[r0:r0 + 1, :] = cw_ref[j:j + 1, l0:l0 + 128]

    args = (dg_mix, dgq, dgk, dsk, dg_gvn, dg_y, dg_xa, dg_mem, dg_xq, dg_xk, dg_ffn, gcw, dbl, dws)
    full = lambda a: BS(a.shape, lambda i, nd=a.ndim: (0,) * nd)
    return _pcall(body, name="pack_small", grid=(1,), in_specs=[full(a) for a in args],
                  out_specs=BS((SMALL_ROWS, 128), lambda i: (0, 0)), out_shape=SDS((SMALL_ROWS, 128), F32))(*args)


def _adam(w, g, m, v):
    mn = ADAM_B1 * m + (1.0 - ADAM_B1) * g
    vn = ADAM_B2 * v + (1.0 - ADAM_B2) * (g * g)
    m_hat = mn / (1.0 - ADAM_B1 ** ADAM_STEP)
    v_hat = vn / (1.0 - ADAM_B2 ** ADAM_STEP)
    return -ADAM_LR * (m_hat / (jnp.sqrt(v_hat) + ADAM_EPS) + ADAM_WD * w), mn, vn


def adamw_small(gsum, w, m, v, chipvec):
    n = len(SMALL)

    def body(chip_ref, g_ref, *refs):
        w_refs, m_refs, v_refs = refs[:n], refs[n:2 * n], refs[2 * n:3 * n]
        outs = refs[3 * n:]
        go, do, mo, vo = outs[:n], outs[n:2 * n], outs[2 * n:3 * n], outs[3 * n:]

        def update(i, idx, g):
            d, mn, vn = _adam(w_refs[i][idx], g, m_refs[i][idx], v_refs[i][idx])
            go[i][idx] = g
            do[i][idx] = d
            mo[i][idx] = mn
            vo[i][idx] = vn

        for i, (name, length) in enumerate(SMALL_VECS):
            for k in range(-(-length // 128)):
                wd = min(128, length - k * 128)
                r = SMALL_ROW[name] + k
                update(i, (slice(0, 1), slice(k * 128, k * 128 + wd)), g_ref[r:r + 1, 0:wd])
        i_bs, i_ws, i_cv = len(SMALL_VECS), len(SMALL_VECS) + 1, len(SMALL_VECS) + 2
        update(i_bs, (0,), g_ref[SMALL_ROW["gmlp_bs"]:SMALL_ROW["gmlp_bs"] + 8, :])
        for h in range(8):
            r0 = SMALL_ROW["gmlp_ws"] + h * BLK
            update(i_ws, (0, h), g_ref[r0:r0 + BLK, :])
        mine = g_ref[pl.ds(pl.multiple_of(SMALL_ROW["ffn_conv"] + chip_ref[0] * CONV_CHIP_ROWS, 8), CONV_CHIP_ROWS), :]
        for j in range(3):
            for k in range(CONV_LANE_ROWS):
                r = j * CONV_LANE_ROWS + k
                update(i_cv, (0, slice(j, j + 1), slice(k * 128, (k + 1) * 128)), mine[r:r + 1, :])

    nat = [w[nm] for nm in SMALL]
    full = lambda a: BS(a.shape, lambda i, c, nd=a.ndim: (0,) * nd)
    outs = _pcall(body, name="adamw_small", grid=(1,), prefetch=1,
                  in_specs=[BS((SMALL_ROWS, 128), lambda i, c: (0, 0))] + [full(a) for a in nat] * 3,
                  out_specs=[full(a) for a in nat] * 4, out_shape=[SDS(a.shape, F32) for a in nat] * 4)(
        chipvec, gsum, *nat, *[m[nm] for nm in SMALL], *[v[nm] for nm in SMALL])
    return outs[:n], outs[n:2 * n], outs[2 * n:3 * n], outs[3 * n:]


def adamw_matrix(w, m, v, g_own, g_other, cvec, *, name):
    _, r, c = w.shape
    half = r // 2
    tr = _tile(half, (256, 176, 128))
    T = half // tr

    def body(c_ref, w_ref, m_ref, v_ref, own_ref, oth_ref, g_ref, d_ref, mo_ref, vo_ref):
        g = jnp.where(pl.program_id(0) == c_ref[0], own_ref[...], oth_ref[...])
        d, mn, vn = _adam(w_ref[...], g, m_ref[...], v_ref[...])
        g_ref[...] = g
        d_ref[...] = d
        mo_ref[...] = mn
        vo_ref[...] = vn

    nat = BS((None, tr, c), lambda hf, t, cr: (0, hf * T + t, 0))
    hlf = BS((tr, c), lambda hf, t, cr: (t, 0))
    return _pcall(body, name=name, grid=(2, T), prefetch=1, in_specs=[nat, nat, nat, hlf, hlf], out_specs=[nat] * 4,
                  out_shape=[SDS(w.shape, F32)] * 4)(cvec, w, m, v, g_own, g_other)


def _place():
    return lax.axis_index("x"), lax.axis_index("y"), lax.axis_index("c")


def _other_chips(x, y):
    return [(1 - x, y), (x, 1 - y), (1 - x, 1 - y)]


def _rows_of_core(c, half):
    return pl.ds(pl.multiple_of(c * half, 16), half)


def _rcopy(src, dst, sems, k, to):
    return pltpu.make_async_remote_copy(src_ref=src, dst_ref=dst, send_sem=sems[0].at[k], recv_sem=sems[1].at[k],
                                        device_id=to, device_id_type=MESH)


def cast_shards(shards, conv, chipvec):
    n = len(shards)

    def body(chip_ref, *refs):
        for i_ref, o_ref in zip(refs[:n + 1], refs[n + 1:]):
            o_ref[...] = i_ref[...].astype(o_ref.dtype)

    in_specs = [BS((s.shape[0] // 4, s.shape[1]), lambda i, p: (i, 0)) for s in shards]
    in_specs.append(BS(conv.shape, lambda i, p: (0, 0)))
    out_specs = [BS((None, s.shape[0] // 4, s.shape[1]), lambda i, p: (p[0], i, 0)) for s in shards]
    out_specs.append(BS((None,) + conv.shape, lambda i, p: (p[0], 0, 0)))
    out_shape = [SDS((N_CHIPS,) + s.shape, MXU_DTYPE) for s in shards] + [SDS((N_CHIPS,) + conv.shape, F32)]
    return _pcall(body, name="cast_shards", grid=(4,), prefetch=1, in_specs=in_specs, out_specs=out_specs,
                  out_shape=out_shape)(chipvec, *shards, conv)


HBM = pl.BlockSpec(memory_space=pltpu.HBM)
SEM = pl.BlockSpec(memory_space=pltpu.SEMAPHORE)
DATAFLOW = pltpu.SideEffectType.DATAFLOW_SIDE_EFFECTING
VMEM_WHOLE = pl.BlockSpec(memory_space=pltpu.VMEM)
TOKEN = jax.ShapeDtypeStruct((8, 128), jnp.float32)


def _gather_copies(bufs, send_sems, recv_sems, outgoing):
    x, y, c = _place()
    p = 2 * x + y
    cps = []
    for i, o in enumerate(bufs):
        for j, (cx, cy) in enumerate(_other_chips(x, y)):
            slot = o.at[p] if outgoing else o.at[2 * cx + cy]
            cps.append(_rcopy(slot, slot, (send_sems, recv_sems), 3 * i + j, (cx, cy, c)))
    return cps


def gather_start(slots, after):
    n = len(slots)

    def body(*refs):
        send_sems, recv_sems, thru, token = refs[n + 1], refs[n + 2], refs[n + 3:2 * n + 3], refs[2 * n + 3]
        for cp in _gather_copies(thru, send_sems, recv_sems, True):
            cp.start()
        token[...] = jnp.zeros_like(token)

    hbm = [pltpu.with_memory_space_constraint(s, pltpu.HBM) for s in slots]
    outs = pl.pallas_call(
        body, name="gather_start_%d" % n,
        out_shape=[pltpu.SemaphoreType.DMA((3 * n,)), pltpu.SemaphoreType.DMA((3 * n,))]
        + [pltpu.HBM(s.shape, s.dtype) for s in slots] + [TOKEN],
        in_specs=[HBM] * n + [ANY], out_specs=[SEM, SEM] + [HBM] * n + [VMEM_WHOLE],
        input_output_aliases={i: 2 + i for i in range(n)},
        compiler_params=pltpu.CompilerParams(has_side_effects=DATAFLOW))(*hbm, after)
    return outs[0], outs[1], outs[2:2 + n], outs[2 + n]


def gather_wait(send_sems, recv_sems, bufs, *after):
    n = len(bufs)

    def body(*refs):
        ins, send_ref, recv_ref = refs[:n], refs[n], refs[n + 1]
        for cp in _gather_copies(ins, send_ref, recv_ref, False):
            cp.wait_send()
            cp.wait_recv()

    return pl.pallas_call(
        body, name="gather_wait_%d" % n, out_shape=[pltpu.HBM(s.shape, s.dtype) for s in bufs],
        in_specs=[HBM] * n + [SEM, SEM] + [ANY] * len(after), out_specs=[HBM] * n,
        input_output_aliases={i: i for i in range(n)},
        compiler_params=pltpu.CompilerParams(has_side_effects=DATAFLOW))(*bufs, send_sems, recv_sems, *after)


def _peers(x, y, c):
    return [(1 - x if k & 4 else x, 1 - y if k & 2 else y, 1 - c if k & 1 else c) for k in range(1, N_DEV)]


def _partial_copies(g_ref, land_ref, send_sems, recv_sems, outgoing):
    x, y, c = _place()
    half = g_ref.shape[1] // 2
    cps = []
    for k, (px, py, pc) in enumerate(_peers(x, y, c)):
        src = g_ref.at[2 * px + py, _rows_of_core(pc, half)]
        dst = land_ref.at[4 * x + 2 * y + c] if outgoing else land_ref.at[4 * px + 2 * py + pc]
        cps.append(_rcopy(src, dst, (send_sems, recv_sems), k, (px, py, pc)))
    return cps


def partials_start(g, *, name):
    land = lax.empty((N_DEV, g.shape[1] // 2, g.shape[2]), g.dtype)

    def body(g_ref, land_ref, send_sems, recv_sems, g_thru, land_thru, token):
        for cp in _partial_copies(g_thru, land_thru, send_sems, recv_sems, True):
            cp.start()
        token[...] = jnp.zeros_like(token)

    return pl.pallas_call(
        body, name=name,
        out_shape=[pltpu.SemaphoreType.DMA((N_DEV - 1,)), pltpu.SemaphoreType.DMA((N_DEV - 1,)),
                   pltpu.HBM(g.shape, g.dtype), pltpu.HBM(land.shape, land.dtype), TOKEN],
        in_specs=[HBM, HBM], out_specs=[SEM, SEM, HBM, HBM, VMEM_WHOLE], input_output_aliases={0: 2, 1: 3},
        compiler_params=pltpu.CompilerParams(has_side_effects=DATAFLOW))(
        pltpu.with_memory_space_constraint(g, pltpu.HBM), pltpu.with_memory_space_constraint(land, pltpu.HBM))


def partials_wait(started, after):
    n = len(started)

    def body(*refs):
        for i in range(n):
            send_ref, recv_ref, g_ref, land_ref = refs[4 * i:4 * i + 4]
            for cp in _partial_copies(g_ref, land_ref, send_ref, recv_ref, False):
                cp.wait_send()
                cp.wait_recv()

    flat = [a for s in started for a in s]
    bufs = [a for s in started for a in s[2:]]
    outs = pl.pallas_call(
        body, name="partials_wait", out_shape=[pltpu.HBM(b.shape, b.dtype) for b in bufs],
        in_specs=[SEM, SEM, HBM, HBM] * n + [ANY], out_specs=[HBM] * (2 * n),
        input_output_aliases={4 * i + 2 + j: 2 * i + j for i in range(n) for j in range(2)},
        compiler_params=pltpu.CompilerParams(has_side_effects=DATAFLOW))(*flat, after)
    return [(outs[2 * i], outs[2 * i + 1]) for i in range(n)]


def sum_partials(pairs, order):
    n = len(pairs)

    def body(o_ref, *refs):
        j = pl.program_id(0)
        for g_ref, l_ref, f_ref in zip(refs[:n], refs[n:2 * n], refs[2 * n:]):
            @pl.when(j == 0)
            def _():
                f_ref[...] = g_ref[...].astype(F32)

            @pl.when(j > 0)
            def _():
                f_ref[...] += l_ref[...].astype(F32)

    g4 = [g.reshape(g.shape[0], 2, g.shape[1] // 2, g.shape[2]) for g, _ in pairs]
    lands = [l for _, l in pairs]
    return _pcall(body, name="sum_partials", grid=(N_DEV,), prefetch=1,
                  in_specs=[BS((None, None) + g.shape[2:], lambda j, o: (o[0], o[1], 0, 0)) for g in g4]
                  + [BS((None,) + l.shape[1:], lambda j, o: (o[jnp.maximum(j, 1) + 1], 0, 0)) for l in lands],
                  out_specs=[BS(l.shape[1:], lambda j, o: (0, 0)) for l in lands],
                  out_shape=[SDS(l.shape[1:], F32) for l in lands])(order, *g4, *lands)


def _pair_copies(f_refs, land_refs, send_sems, recv_sems):
    x, y, c = _place()
    return [_rcopy(f, o, (send_sems, recv_sems), i, (x, y, 1 - c)) for i, (f, o) in enumerate(zip(f_refs, land_refs))]


def pair_start(fs):
    n = len(fs)
    lands = [lax.empty(f.shape, f.dtype) for f in fs]

    def body(*refs):
        send_sems, recv_sems = refs[2 * n], refs[2 * n + 1]
        thru, land_thru, token = refs[2 * n + 2:3 * n + 2], refs[3 * n + 2:4 * n + 2], refs[4 * n + 2]
        for cp in _pair_copies(thru, land_thru, send_sems, recv_sems):
            cp.start()
        token[...] = jnp.zeros_like(token)

    hbm = [pltpu.with_memory_space_constraint(a, pltpu.HBM) for a in list(fs) + lands]
    outs = pl.pallas_call(
        body, name="pair_start",
        out_shape=[pltpu.SemaphoreType.DMA((n,)), pltpu.SemaphoreType.DMA((n,))]
        + [pltpu.HBM(a.shape, a.dtype) for a in list(fs) + lands] + [TOKEN],
        in_specs=[HBM] * (2 * n), out_specs=[SEM, SEM] + [HBM] * (2 * n) + [VMEM_WHOLE],
        input_output_aliases={i: 2 + i for i in range(2 * n)},
        compiler_params=pltpu.CompilerParams(has_side_effects=DATAFLOW))(*hbm)
    return outs[0], outs[1], outs[2:2 + n], outs[2 + n:2 + 2 * n], outs[2 + 2 * n]


def pair_wait(send_sems, recv_sems, fs, lands, after):
    n = len(fs)

    def body(*refs):
        for cp in _pair_copies(refs[:n], refs[n:2 * n], refs[2 * n], refs[2 * n + 1]):
            cp.wait_send()
            cp.wait_recv()

    outs = pl.pallas_call(
        body, name="pair_wait", out_shape=[pltpu.HBM(a.shape, a.dtype) for a in list(fs) + list(lands)],
        in_specs=[HBM] * (2 * n) + [SEM, SEM, ANY], out_specs=[HBM] * (2 * n),
        input_output_aliases={i: i for i in range(2 * n)},
        compiler_params=pltpu.CompilerParams(has_side_effects=DATAFLOW))(*fs, *lands, send_sems, recv_sems, after)
    return outs[:n], outs[n:]


def _small_copies(s_ref, land_ref, send_sems, recv_sems, outgoing):
    x, y, c = _place()
    cps = []
    for k, (px, py, pc) in enumerate(_peers(x, y, c)):
        dst = land_ref.at[4 * x + 2 * y + c] if outgoing else land_ref.at[4 * px + 2 * py + pc]
        cps.append(_rcopy(s_ref, dst, (send_sems, recv_sems), k, (px, py, pc)))
    return cps


def small_start(sm):
    land = lax.empty((N_DEV,) + sm.shape, sm.dtype)

    def body(s_ref, land_ref, send_sems, recv_sems, s_thru, land_thru):
        for cp in _small_copies(s_thru, land_thru, send_sems, recv_sems, True):
            cp.start()

    return pl.pallas_call(
        body, name="small_start",
        out_shape=[pltpu.SemaphoreType.DMA((N_DEV - 1,)), pltpu.SemaphoreType.DMA((N_DEV - 1,)),
                   pltpu.HBM(sm.shape, sm.dtype), pltpu.HBM(land.shape, land.dtype)],
        in_specs=[HBM, HBM], out_specs=[SEM, SEM, HBM, HBM], input_output_aliases={0: 2, 1: 3},
        compiler_params=pltpu.CompilerParams(has_side_effects=DATAFLOW))(
        pltpu.with_memory_space_constraint(sm, pltpu.HBM), pltpu.with_memory_space_constraint(land, pltpu.HBM))


def small_wait(send_sems, recv_sems, sm, land, *after):
    def body(send_ref, recv_ref, s_ref, land_ref, *rest):
        for cp in _small_copies(s_ref, land_ref, send_ref, recv_ref, False):
            cp.wait_send()
            cp.wait_recv()

    return pl.pallas_call(
        body, name="small_wait", out_shape=[pltpu.HBM(sm.shape, sm.dtype), pltpu.HBM(land.shape, land.dtype)],
        in_specs=[SEM, SEM, HBM, HBM] + [ANY] * len(after), out_specs=[HBM, HBM], input_output_aliases={2: 0, 3: 1},
        compiler_params=pltpu.CompilerParams(has_side_effects=DATAFLOW))(send_sems, recv_sems, sm, land, *after)


def sum_small(own, land, mevec):
    n, rows, width = land.shape
    tr = _tile(rows, (184, 8))

    def body(me_ref, own_ref, land_ref, o_ref):
        acc = jnp.zeros((tr, width), F32)
        for s in range(n):
            acc = acc + jnp.where(me_ref[0] == s, own_ref[...], land_ref[s])
        o_ref[...] = acc

    return _pcall(body, name="sum_small", grid=(rows // tr,), prefetch=1,
                  in_specs=[BS((tr, width), lambda i, me: (i, 0)), BS((n, tr, width), lambda i, me: (0, i, 0))],
                  out_specs=BS((tr, width), lambda i, me: (i, 0)), out_shape=SDS((rows, width), F32))(mevec, own, land)


def _to_full(blk, col):
    n, r, c = blk.shape
    return blk.transpose(1, 0, 2).reshape(r, n * c) if col else blk.reshape(n * r, c)


def _dup_cols(w):
    dup = lambda t: jnp.concatenate([t[:, :64], t[:, :64], t[:, 64:], t[:, 64:]], axis=1)
    return jnp.concatenate([w[:, :512], dup(w[:, 512:640]), dup(w[:, 640:768]), w[:, 768:]], axis=1)


def _fold_cols(d):
    fold = lambda t: jnp.concatenate([t[:, 0:64] + t[:, 64:128], t[:, 128:192] + t[:, 192:256]], axis=1)
    return jnp.concatenate([d[:, :512], fold(d[:, 512:768]), fold(d[:, 768:1024]), d[:, 1024:]], axis=1)


def _local_step(x, mem, positions, target, w_in, later, sp, emit):
    gain = lambda n: sp[n].reshape(1, -1)
    half = HEAD_DIM // 2
    inv_freq = 1.0 / (10000.0 ** (jnp.arange(half, dtype=F32) * (2.0 / HEAD_DIM)))
    ang = positions.astype(F32)[:, None] * inv_freq
    cos, sin = jnp.cos(ang), jnp.sin(ang)
    cos128 = jnp.tile(cos, (1, 4))
    sin128 = jnp.concatenate([-sin, sin, -sin, sin], axis=1)
    seg = jnp.arange(128) // HEAD_DIM
    bmat = (seg[:, None] == seg[None, :]).astype(BF16)
    gq128, gk128 = jnp.tile(gain("q_norm"), (1, 2)), jnp.tile(gain("k_norm"), (1, 2))
    sinkcol = jnp.repeat(sp["attn_sinks"].reshape(4, 2), BLK, axis=1).reshape(4, 2 * BLK, 1)
    wsc = sp["gmlp_ws"] * jnp.tril(jnp.ones((BLK, BLK), F32))[None]
    w2 = wsc.reshape(4, 2 * BLK, BLK).astype(MXU_DTYPE)
    w2t = wsc.swapaxes(1, 2).reshape(4, 2 * BLK, BLK).astype(MXU_DTYPE)
    bsl = jnp.repeat(sp["gmlp_bs"].reshape(4, 2, BLK).transpose(0, 2, 1), HEAD_DIM, axis=2)
    cb = sp["ffn_conv_b"].reshape(1, -1)
    w_in_d = _dup_cols(_to_full(w_in(cos128, sin128, gq128, gk128, sinkcol, w2, w2t, bsl), True))[None]

    h1, proj = rms_mm(x, gain("mix_norm"), w_in_d, name="mix_in")
    qr, kr, vb, gu, gvn, attn, gm, y = mixer_core_fwd(proj, cos128, sin128, gq128, gk128, gain("gmlp_v_norm"), bmat,
                                                      sinkcol, gain("attn_out_norm"), w2, bsl, gain("gmlp_out_norm"))
    wf, last = later(y)
    w_out, xa_wq, xa_wo = (_to_full(wf[n], False) for n in ("w_out", "xa_wq", "xa_wo"))
    mn, kv = rms_mm(mem, gain("mem_norm"), wf["xa_wkv"], name="xa_kv")
    kn, vbx = mem_pre(kv, gain("xa_k_norm"))
    x1, h2, qx, xo, x2 = xattn_block_fwd(y, w_out, x, gain("xa_norm"), xa_wq, kn, vbx, gain("xa_q_norm"), xa_wo)
    ffn_w, cw = last(x2)
    wf = {**wf, **ffn_w}
    ffn_down = _to_full(wf["ffn_down"], False)
    h3, a, f, dx3, loss_acc = ffn_fwd_loss(x2, gain("ffn_norm"), wf["ffn_up"], cw, cb, ffn_down, target)

    by_rows = lambda g: g.reshape(N_CHIPS, g.shape[1] // N_CHIPS, g.shape[2])
    sent = emit("ffn_down", by_rows(mm_tn(f, dx3, name="g_ffn_down", out_dtype=WIRE_DTYPE)))
    dc, gcw = convgate_bwd(a, dx3, ffn_down[None], cw, cb, after=sent)
    da, dx2, dg_ffn = conv_transpose_rms_bwd(dc, cw, wf["ffn_up"], x2, gain("ffn_norm"), dx3)
    sent = emit("ffn_up", mm_tn(h3, da, name="g_ffn_up", out_dtype=WIRE_DTYPE, chunks=N_CHIPS))
    sent = emit("xa_wo", by_rows(mm_tn(xo, dx2, name="g_xa_wo", out_dtype=WIRE_DTYPE, after=sent)))
    dqx, dx1, dkn, dvx, dg_xq, dg_xa, dattn, dgm, dg_y = xattn_block_bwd(
        dx2, xa_wo[None], qx, kn, vbx, gain("xa_q_norm"), xa_wq[None], x1, gain("xa_norm"), w_out[None], attn, gm,
        gain("attn_out_norm"), gain("gmlp_out_norm"), after=sent)
    sent = emit("xa_wq", by_rows(mm_tn(h2, dqx, name="g_xa_wq", out_dtype=WIRE_DTYPE)))
    dkv, dg_xk = mem_bwd(kv, dkn, dvx, gain("xa_k_norm"), after=sent)
    _, dg_mem = mm_nt_rms_bwd(dkv, wf["xa_wkv"], mem, gain("mem_norm"), jnp.zeros_like(mem), name="d_mem")
    sent = emit("xa_wkv", mm_tn(mn, dkv, name="g_xa_wkv", out_dtype=WIRE_DTYPE, chunks=N_CHIPS))
    sent = emit("w_out", by_rows(mm_tn(y, dx1, name="g_w_out", out_dtype=WIRE_DTYPE, after=sent)))
    dproj, dsk, dws, dbl, dgq, dgk, dg_gvn = mixer_core_bwd(
        proj, cos128, sin128, gq128, gk128, gain("gmlp_v_norm"), bmat, qr, kr, vb, sinkcol, dattn, dgm, gvn, gu,
        w2, w2t, bsl, after=sent)
    g_in = _fold_cols(mm_tn(h1, dproj, name="g_w_in", out_dtype=F32)[0])
    sent = emit("w_in", g_in.reshape(1024, N_CHIPS, 448).transpose(1, 0, 2).astype(WIRE_DTYPE))
    grad_x, dg_mix = mm_nt_rms_bwd(dproj, w_in_d, x, gain("mix_norm"), dx1, name="d_x", tm=1024, after=sent)
    packed = pack_small(dg_mix, dgq, dgk, dsk, dg_gvn, dg_y, dg_xa, dg_mem, dg_xq, dg_xk, dg_ffn, gcw, dbl, dws)
    return loss_acc, grad_x, packed


def _gather_step(w, chipvec):
    slots = cast_shards([w[n][0] for n in BIG_NAMES], w["ffn_conv"][0], chipvec)
    send_a, recv_a, first, token = gather_start(slots[:1], chipvec)
    send_b, recv_b, mid, token = gather_start(slots[1:5], token)
    send_c, recv_c, rest, token = gather_start(slots[5:], token)

    def w_in(*after):
        return gather_wait(send_a, recv_a, first, token, *after)[0]

    def last(after):
        got = gather_wait(send_c, recv_c, rest, after)
        return dict(zip(BIG_NAMES[5:], got[:-1])), _to_full(got[-1], True)

    def later(after):
        return dict(zip(BIG_NAMES[1:5], gather_wait(send_b, recv_b, mid, after))), last

    return w_in, later, token


def _reduce_update(started, packed, w, m, v, chipvec, cvec, order):
    small_sent = small_start(packed)
    own = sum_partials(partials_wait([started[n] for n in BIG_NAMES], small_sent[2]), order)
    pair_send, pair_recv, own, lands, pair_started = pair_start(own)
    own, other = pair_wait(pair_send, pair_recv, own, lands, pair_started)
    res = [{}, {}, {}, {}]
    for n, g_own, g_other in zip(BIG_NAMES, own, other):
        for d, o in zip(res, adamw_matrix(w[n], m[n], v[n], g_own, g_other, cvec, name="adamw_" + n)):
            d[n] = o
    mevec = (2 * order[0:1] + order[1:2]).astype(jnp.int32)
    small_sum = sum_small(*small_wait(*small_sent, *[res[3][n] for n in BIG_NAMES]), mevec)
    for d, outs in zip(res, adamw_small(small_sum, w, m, v, chipvec)):
        d.update(zip(SMALL, outs))
    return res


def kernel(x, mem, positions, mix_norm, w_in, q_norm, k_norm, attn_sinks, gmlp_v_norm, gmlp_ws, gmlp_bs, attn_out_norm, gmlp_out_norm, w_out, xa_norm, mem_norm, xa_wq, xa_wkv, xa_q_norm, xa_k_norm, xa_wo, ffn_norm, ffn_up, ffn_conv, ffn_conv_b, ffn_down, loss_target, m_mix_norm, m_w_in, m_q_norm, m_k_norm, m_attn_sinks, m_gmlp_v_norm, m_gmlp_ws, m_gmlp_bs, m_attn_out_norm, m_gmlp_out_norm, m_w_out, m_xa_norm, m_mem_norm, m_xa_wq, m_xa_wkv, m_xa_q_norm, m_xa_k_norm, m_xa_wo, m_ffn_norm, m_ffn_up, m_ffn_conv, m_ffn_conv_b, m_ffn_down, v_mix_norm, v_w_in, v_q_norm, v_k_norm, v_attn_sinks, v_gmlp_v_norm, v_gmlp_ws, v_gmlp_bs, v_attn_out_norm, v_gmlp_out_norm, v_w_out, v_xa_norm, v_mem_norm, v_xa_wq, v_xa_wkv, v_xa_q_norm, v_xa_k_norm, v_xa_wo, v_ffn_norm, v_ffn_up, v_ffn_conv, v_ffn_conv_b, v_ffn_down):
    w = dict(mix_norm=mix_norm, w_in=w_in, q_norm=q_norm, k_norm=k_norm, attn_sinks=attn_sinks, gmlp_v_norm=gmlp_v_norm, gmlp_ws=gmlp_ws, gmlp_bs=gmlp_bs, attn_out_norm=attn_out_norm, gmlp_out_norm=gmlp_out_norm, w_out=w_out, xa_norm=xa_norm, mem_norm=mem_norm, xa_wq=xa_wq, xa_wkv=xa_wkv, xa_q_norm=xa_q_norm, xa_k_norm=xa_k_norm, xa_wo=xa_wo, ffn_norm=ffn_norm, ffn_up=ffn_up, ffn_conv=ffn_conv, ffn_conv_b=ffn_conv_b, ffn_down=ffn_down)
    m = dict(mix_norm=m_mix_norm, w_in=m_w_in, q_norm=m_q_norm, k_norm=m_k_norm, attn_sinks=m_attn_sinks, gmlp_v_norm=m_gmlp_v_norm, gmlp_ws=m_gmlp_ws, gmlp_bs=m_gmlp_bs, attn_out_norm=m_attn_out_norm, gmlp_out_norm=m_gmlp_out_norm, w_out=m_w_out, xa_norm=m_xa_norm, mem_norm=m_mem_norm, xa_wq=m_xa_wq, xa_wkv=m_xa_wkv, xa_q_norm=m_xa_q_norm, xa_k_norm=m_xa_k_norm, xa_wo=m_xa_wo, ffn_norm=m_ffn_norm, ffn_up=m_ffn_up, ffn_conv=m_ffn_conv, ffn_conv_b=m_ffn_conv_b, ffn_down=m_ffn_down)
    v = dict(mix_norm=v_mix_norm, w_in=v_w_in, q_norm=v_q_norm, k_norm=v_k_norm, attn_sinks=v_attn_sinks, gmlp_v_norm=v_gmlp_v_norm, gmlp_ws=v_gmlp_ws, gmlp_bs=v_gmlp_bs, attn_out_norm=v_attn_out_norm, gmlp_out_norm=v_gmlp_out_norm, w_out=v_w_out, xa_norm=v_xa_norm, mem_norm=v_mem_norm, xa_wq=v_xa_wq, xa_wkv=v_xa_wkv, xa_q_norm=v_xa_q_norm, xa_k_norm=v_xa_k_norm, xa_wo=v_xa_wo, ffn_norm=v_ffn_norm, ffn_up=v_ffn_up, ffn_conv=v_ffn_conv, ffn_conv_b=v_ffn_conv_b, ffn_down=v_ffn_down)
    ix, iy, ic = lax.axis_index("x"), lax.axis_index("y"), lax.axis_index("c")
    chip = 2 * ix + iy
    chipvec = chip.astype(jnp.int32).reshape(1)
    cvec = ic.astype(jnp.int32).reshape(1)
    order = jnp.stack([chip, ic] + [4 * px + 2 * py + pc for px, py, pc in _peers(ix, iy, ic)]).astype(jnp.int32)

    w_in_all, later, token = _gather_step(w, chipvec)
    zero = token[0, 0]
    sp = {n: w[n][0] + zero for n in SMALL if n != "ffn_conv"}
    positions = positions + zero.astype(jnp.int32)
    started = {}

    def emit(name, g):
        *started[name], token = partials_start(g, name="partials_start_" + name)
        return token

    loss_acc, grad_x, packed = _local_step(x[0], mem[0], positions[0], loss_target[0], w_in_all, later, sp, emit)
    grads, delta, new_m, new_v = _reduce_update(started, packed, w, m, v, chipvec, cvec, order)
    loss = lax.psum(loss_acc[0, 0], ("x", "y", "c"))
    ordered = lambda d: [d[n] for n in WEIGHTS]
    return (loss, grad_x[None], *ordered(grads), *ordered(delta), *ordered(new_m), *ordered(new_v))
```

```python
import math

import jax
import jax.numpy as jnp
from jax import lax
from jax.experimental import pallas as pl
from jax.experimental.pallas import tpu as pltpu

F32 = jnp.float32
BF16 = jnp.bfloat16
MXU_DTYPE = jnp.bfloat16
WIRE_DTYPE = jnp.bfloat16
EPS = 1e-6
VMEM_LIMIT_V7X = 56 * 1024 * 1024

D_MODEL = 1024
HEAD_DIM = 64
BLK = 128
XA_HEADS = 4
XA_DH = 256
MEM_LEN = 256
D_FF = 2816
IN_COLS_DUP = 2048
N_CHIPS = 4
N_DEV = 8

ADAM_LR = 0.001
ADAM_B1 = 0.9
ADAM_B2 = 0.999
ADAM_EPS = 1e-08
ADAM_WD = 0.01
ADAM_STEP = 10

NT = (((1,), (1,)), ((), ()))
TN = (((0,), (0,)), ((), ()))
NN = (((1,), (0,)), ((), ()))
MINF = float(jnp.finfo(jnp.float32).min)
GELU_K0 = math.sqrt(2.0 / math.pi)
GELU_K1 = 0.044715

BS = pl.BlockSpec
SDS = jax.ShapeDtypeStruct
ANY = pl.BlockSpec(memory_space=pl.ANY)
MESH = pl.DeviceIdType.MESH


def _dot(a, b, dims=NN):
    return lax.dot_general(a.astype(MXU_DTYPE), b.astype(MXU_DTYPE), dims, preferred_element_type=F32)


def _segsum(x, bmat):
    hi = x.astype(BF16)
    lo = (x - hi.astype(F32)).astype(BF16)
    return (jnp.dot(hi, bmat, preferred_element_type=F32) + jnp.dot(lo, bmat, preferred_element_type=F32))


def _gelu(x):
    return 0.5 * x * (1.0 + jnp.tanh(GELU_K0 * (x + GELU_K1 * x * x * x)))


def _gelu_grad(x):
    t = jnp.tanh(GELU_K0 * (x + GELU_K1 * x * x * x))
    return 0.5 * (1.0 + t) + 0.5 * x * (1.0 - t * t) * GELU_K0 * (1.0 + 3.0 * GELU_K1 * x * x)


def _gelu_and_grad(x):
    x2 = x * x
    t = jnp.tanh(x * (GELU_K0 * GELU_K1 * x2 + GELU_K0))
    hx = 0.5 * x
    return hx * t + hx, 0.5 * t + 0.5 + hx * (1.0 - t * t) * (3.0 * GELU_K0 * GELU_K1 * x2 + GELU_K0)


def _rms(x):
    return lax.rsqrt(jnp.mean(x * x, axis=-1, keepdims=True) + EPS)


def _rms_bwd(dy, x, g, r):
    dyg = dy * g
    dx = r * dyg - x * (r * r * r) * jnp.mean(dyg * x, axis=-1, keepdims=True)
    return dx, dy * x * r


def _pcall(body, *, name, grid, in_specs, out_specs, out_shape, scratch=(), prefetch=0, after=None):
    params = pltpu.CompilerParams(dimension_semantics=("arbitrary",) * len(grid), vmem_limit_bytes=VMEM_LIMIT_V7X)
    in_specs = list(in_specs)
    kernel_fn = body
    if after is not None:
        n_in = prefetch + len(in_specs)
        in_specs.append(ANY)

        def kernel_fn(*refs):
            return body(*refs[:n_in], *refs[n_in + 1:])

    if prefetch:
        spec = pltpu.PrefetchScalarGridSpec(num_scalar_prefetch=prefetch, grid=grid, in_specs=in_specs,
                                            out_specs=out_specs, scratch_shapes=list(scratch))
        call = pl.pallas_call(kernel_fn, name=name, grid_spec=spec, out_shape=out_shape, compiler_params=params)
    else:
        call = pl.pallas_call(kernel_fn, name=name, grid=grid, in_specs=in_specs, out_specs=out_specs,
                              out_shape=out_shape, scratch_shapes=list(scratch), compiler_params=params)
    return call if after is None else (lambda *args: call(*args, after))


def _tile(n, prefs):
    for p in prefs:
        if p <= n and n % p == 0:
            return p
    return n


def _resident(shape):
    return pl.BlockSpec(shape, lambda *_: (0,) * len(shape), pipeline_mode=pl.Buffered(1))


def _acc_rows(ref, row, val):
    ref[row:row + 1, :] += jnp.sum(val, axis=0, keepdims=True)


def rms_mm(x, g, w3, *, name, tm=1024):
    M, K = x.shape
    Q, _, C = w3.shape
    tm = _tile(M, (tm, 256))

    def body(x_ref, g_ref, w_ref, h_ref, o_ref):
        def write_h():
            xv = x_ref[...]
            h_ref[...] = (xv * _rms(xv) * g_ref[...]).astype(h_ref.dtype)

        if Q == 1:
            write_h()
        else:
            pl.when(pl.program_id(1) == 0)(write_h)
        o_ref[...] = _dot(h_ref[...], w_ref[pl.program_id(1)])

    return _pcall(body, name=name, grid=(M // tm, Q),
                  in_specs=[BS((tm, K), lambda i, j: (i, 0)), BS((1, K), lambda i, j: (0, 0)),
                            _resident((Q, K, C))],
                  out_specs=[BS((tm, K), lambda i, j: (i, 0)), BS((tm, C), lambda i, j: (i, j))],
                  out_shape=[SDS((M, K), MXU_DTYPE), SDS((M, Q * C), F32)])(x, g, w3)


def _nt_chunks(a_ref, w_ref):
    q_n, _, kc = w_ref.shape
    acc = _dot(a_ref[:, 0:kc], w_ref[0], NT)
    for q in range(1, q_n):
        acc = acc + _dot(a_ref[:, q * kc:(q + 1) * kc], w_ref[q], NT)
    return acc


def mm_nt_rms_bwd(a, w3, x, g, dres, *, name, tm=512, after=None):
    M = a.shape[0]
    Q, N, Kc = w3.shape
    tm = _tile(M, (tm, 256))

    def body(a_ref, w_ref, x_ref, g_ref, dr_ref, dx_ref, dg_ref):
        @pl.when(pl.program_id(0) == 0)
        def _():
            dg_ref[...] = jnp.zeros_like(dg_ref)

        xv = x_ref[...]
        dx, dgc = _rms_bwd(_nt_chunks(a_ref, w_ref), xv, g_ref[...], _rms(xv))
        dx_ref[...] = dr_ref[...] + dx
        _acc_rows(dg_ref, 0, dgc)

    row = BS((tm, N), lambda i: (i, 0))
    return _pcall(body, name=name, grid=(M // tm,), after=after,
                  in_specs=[BS((tm, Q * Kc), lambda i: (i, 0)), _resident((Q, N, Kc)), row,
                            BS((1, N), lambda i: (0, 0)), row],
                  out_specs=[row, BS((8, N), lambda i: (0, 0))],
                  out_shape=[SDS((M, N), F32), SDS((8, N), F32)])(a, w3, x, g, dres)


def mm_tn(a, b, *, name, out_dtype, chunks=1, after=None):
    M, K = a.shape
    N = b.shape[1]
    C = N // chunks
    tm = _tile(M, (1024, 256))
    tk = _tile(K, (1408, 1024, 512))
    tn = _tile(C, (1408, 1024, 512))
    per = C // tn
    nm = M // tm

    def body(a_ref, b_ref, o_ref, acc):
        m = pl.program_id(2)

        @pl.when(m == 0)
        def _():
            acc[...] = jnp.zeros_like(acc)

        acc[...] += _dot(a_ref[...], b_ref[...], TN)

        @pl.when(m == nm - 1)
        def _():
            o_ref[...] = acc[...].astype(o_ref.dtype)

    return _pcall(body, name=name, grid=(K // tk, N // tn, nm), after=after,
                  in_specs=[BS((tm, tk), lambda k, n, m: (m, k)), BS((tm, tn), lambda k, n, m: (m, n))],
                  out_specs=BS((None, tk, tn), lambda k, n, m: (n // per, k, n % per)),
                  out_shape=SDS((chunks, K, C), out_dtype), scratch=[pltpu.VMEM((tk, tn), F32)])(a, b)


def _lane(shape):
    return lax.broadcasted_iota(jnp.int32, shape, 1)


def _head_means(slabs, bmat):
    tm = slabs[0].shape[0]
    means = _segsum(jnp.concatenate(slabs, axis=0), bmat) * (1.0 / HEAD_DIM)
    return [means[i * tm:(i + 1) * tm] for i in range(len(slabs))]


def _half_swap(x, first):
    return jnp.where(first, pltpu.roll(x, 96, 1), pltpu.roll(x, 32, 1))


def _by_head(x2, lo):
    z = jnp.zeros((BLK, 128), x2.dtype)
    parts = []
    for s in range(2):
        xs = x2[:, s * 128:(s + 1) * 128]
        parts += [jnp.where(lo, xs, z), jnp.where(lo, z, xs)]
    return jnp.concatenate(parts, axis=0)


def _from_heads(o4, lo):
    return jnp.concatenate([jnp.where(lo, o4[0:BLK], o4[BLK:2 * BLK]),
                            jnp.where(lo, o4[2 * BLK:3 * BLK], o4[3 * BLK:])], axis=1)


def _swa_probs(q2, kd, sink, n, lo):
    qp = _by_head(q2, lo)
    sc = _dot(qp, kd, NT) * (1.0 / math.sqrt(HEAD_DIM))
    r_i = lax.broadcasted_iota(jnp.int32, (4 * BLK, 2 * BLK), 0)
    k_j = lax.broadcasted_iota(jnp.int32, (4 * BLK, 2 * BLK), 1)
    diff = (r_i & (BLK - 1)) + BLK - k_j
    mask = (diff >= 0) & (diff < BLK) & ((k_j >= BLK) | (n > 0))
    sc = jnp.where(mask, sc, MINF)
    m = jnp.maximum(jnp.max(sc, axis=1, keepdims=True), sink)
    p = jnp.exp(sc - m)
    es = jnp.exp(sink - m)
    inv = 1.0 / (_segsum(p, jnp.ones((2 * BLK, BLK), BF16)) + es)
    return qp, p * jnp.concatenate([inv, inv], axis=1), es * inv[:, :1]


def mixer_core_fwd(proj, cos, sin, gq, gk, gvn, bmat, sinkcol, gao, w2, bsl, ggo):
    S = proj.shape[0]
    sub = 4 if S % (4 * BLK) == 0 else 1

    def body(p_ref, c_ref, s_ref, gq_ref, gk_ref, gvn_ref, b_ref, sk_ref, gao_ref, w2_ref, bsl_ref, ggo_ref,
             qr_ref, kr_ref, vb_ref, gu_ref, gvo_ref, at_ref, gm_ref, y_ref, k_prev, v_prev):
        n = pl.program_id(0)

        @pl.when(n == 0)
        def _():
            k_prev[...] = jnp.zeros_like(k_prev)
            v_prev[...] = jnp.zeros_like(v_prev)

        bm = b_ref[...]
        first = (_lane((BLK, 128)) & 63) < 32
        lo = _lane((BLK, 128)) < 64
        for sb in range(sub):
            rs = slice(sb * BLK, (sb + 1) * BLK)
            cos_v, sin_v = c_ref[rs, :], s_ref[rs, :]
            slabs = [p_ref[rs, s * 128:(s + 1) * 128] for s in range(6)]
            for s, (slab, ms) in enumerate(zip(slabs, _head_means([x * x for x in slabs], bm))):
                qn = slab * lax.rsqrt(ms + EPS) * (gq_ref[...] if s < 4 else gk_ref[...])
                out = qn * cos_v + _half_swap(qn, first) * sin_v
                if s < 4:
                    qr_ref[rs, s * 128:(s + 1) * 128] = out.astype(qr_ref.dtype)
                else:
                    kr_ref[rs, (s - 4) * 128:(s - 3) * 128] = out.astype(kr_ref.dtype)
            vb_ref[rs, :] = p_ref[rs, 768:1024].astype(vb_ref.dtype)
            gu_ref[rs, :] = _gelu(p_ref[rs, 1024:1536])
            gv = _gelu(p_ref[rs, 1536:2048])
            gvo_ref[rs, :] = (gv * _rms(gv) * gvn_ref[...]).astype(gvo_ref.dtype)

            before = slice((sb - 1) * BLK, sb * BLK)
            for h in range(2):
                hs, qs = slice(h * 128, (h + 1) * 128), slice(h * 256, (h + 1) * 256)
                k_before = k_prev[:, hs] if sb == 0 else kr_ref[before, hs]
                v_before = v_prev[:, hs] if sb == 0 else vb_ref[before, hs]
                kd = jnp.concatenate([k_before, kr_ref[rs, hs]], axis=0)
                vd = jnp.concatenate([v_before, vb_ref[rs, hs]], axis=0)
                sink = jnp.concatenate([sk_ref[2 * h], sk_ref[2 * h + 1]], axis=0)
                _, p, _ = _swa_probs(qr_ref[rs, qs], kd, sink, n * sub + sb, lo)
                at_ref[rs, qs] = _from_heads(_dot(p, vd), lo)

            for j in range(4):
                sl = slice(j * 128, (j + 1) * 128)
                m2 = _dot(w2_ref[j], gvo_ref[rs, sl])
                mixed = jnp.where(lo, m2[:BLK], m2[BLK:]) + bsl_ref[j]
                gm_ref[rs, sl] = gu_ref[rs, sl] * mixed
            a, gm = at_ref[rs, :], gm_ref[rs, :]
            y_ref[rs, :512] = (a * _rms(a) * gao_ref[...]).astype(y_ref.dtype)
            y_ref[rs, 512:] = (gm * _rms(gm) * ggo_ref[...]).astype(y_ref.dtype)
        k_prev[...] = kr_ref[(sub - 1) * BLK:, :]
        v_prev[...] = vb_ref[(sub - 1) * BLK:, :]

    row = lambda w: BS((sub * BLK, w), lambda n: (n, 0))
    const = lambda *shape: BS(shape, lambda n: (0,) * len(shape))
    return _pcall(body, name="mixer_core_fwd", grid=(S // (sub * BLK),),
                  in_specs=[row(IN_COLS_DUP), row(128), row(128), const(1, 128), const(1, 128), const(1, 512),
                            const(128, 128), const(4, 2 * BLK, 1), const(1, 512), const(4, 2 * BLK, BLK),
                            const(4, BLK, 128), const(1, 512)],
                  out_specs=[row(512), row(256), row(256), row(512), row(512), row(512), row(512), row(1024)],
                  out_shape=[SDS((S, 512), MXU_DTYPE), SDS((S, 256), MXU_DTYPE), SDS((S, 256), MXU_DTYPE),
                             SDS((S, 512), F32), SDS((S, 512), MXU_DTYPE), SDS((S, 512), F32), SDS((S, 512), F32),
                             SDS((S, 1024), MXU_DTYPE)],
                  scratch=[pltpu.VMEM((BLK, 256), MXU_DTYPE), pltpu.VMEM((BLK, 256), MXU_DTYPE)])(
        proj, cos, sin, gq, gk, gvn, bmat, sinkcol, gao, w2, bsl, ggo)


def mem_pre(kv, gxk):
    def body(kv_ref, g_ref, kn_ref, vb_ref):
        for h in range(XA_HEADS):
            sl = slice(h * XA_DH, (h + 1) * XA_DH)
            k = kv_ref[:, sl]
            kn_ref[:, sl] = (k * _rms(k) * g_ref[...]).astype(kn_ref.dtype)
        vb_ref[...] = kv_ref[:, 1024:2048].astype(vb_ref.dtype)

    full = lambda r, w: BS((r, w), lambda i: (0, 0))
    return _pcall(body, name="mem_pre", grid=(1,), in_specs=[full(MEM_LEN, 2048), full(1, XA_DH)],
                  out_specs=[full(MEM_LEN, 1024), full(MEM_LEN, 1024)],
                  out_shape=[SDS((MEM_LEN, 1024), MXU_DTYPE), SDS((MEM_LEN, 1024), MXU_DTYPE)])(kv, gxk)


def _xa_probs(qh, g, kn_h):
    r = _rms(qh)
    qn = qh * r * g
    s = _dot(qn, kn_h, NT) * (1.0 / math.sqrt(XA_DH))
    p = jnp.exp(s - jnp.max(s, axis=1, keepdims=True))
    inv = 1.0 / _segsum(p, jnp.ones((p.shape[1], 128), BF16))
    return r, qn, p * jnp.concatenate([inv] * (p.shape[1] // 128), axis=1)


def xattn_block_fwd(y, w_out, x, g, wq, kn, vb, gxq, wo):
    S, D = x.shape
    tm = _tile(S, (512, 256))

    def body(y_ref, wout_ref, x_ref, g_ref, wq_ref, kn_ref, vb_ref, gxq_ref, wo_ref, x1_ref, h_ref, q_ref, o_ref,
             x2_ref):
        x1_ref[...] = _dot(y_ref[...], wout_ref[...]) + x_ref[...]
        xv = x1_ref[...]
        h_ref[...] = (xv * _rms(xv) * g_ref[...]).astype(h_ref.dtype)
        q_ref[...] = _dot(h_ref[...], wq_ref[...])
        for h in range(XA_HEADS):
            sl = slice(h * XA_DH, (h + 1) * XA_DH)
            _, _, p = _xa_probs(q_ref[:, sl], gxq_ref[...], kn_ref[:, sl])
            o_ref[:, sl] = _dot(p, vb_ref[:, sl]).astype(o_ref.dtype)
        x2_ref[...] = _dot(o_ref[...], wo_ref[...]) + x1_ref[...]

    row = BS((tm, D), lambda i: (i, 0))
    full = lambda r, w: BS((r, w), lambda i: (0, 0))
    return _pcall(body, name="xattn_block_fwd", grid=(S // tm,),
                  in_specs=[BS((tm, y.shape[1]), lambda i: (i, 0)), _resident(w_out.shape), row, full(1, D),
                            _resident(wq.shape), full(MEM_LEN, D), full(MEM_LEN, D), full(1, XA_DH),
                            _resident(wo.shape)],
                  out_specs=[row, row, row, row, row],
                  out_shape=[SDS((S, D), F32), SDS((S, D), MXU_DTYPE), SDS((S, D), F32), SDS((S, D), MXU_DTYPE),
                             SDS((S, D), F32)])(y, w_out, x, g, wq, kn, vb, gxq, wo)


CONV_COLS = 1408


def _conv_taps(a_ref, halo_ref, w_ref, b_ref, cols, first_tile):
    a = a_ref[:, cols]
    row = lax.broadcasted_iota(jnp.int32, (8, a.shape[1]), 0)
    h6 = jnp.where(first_tile, 0.0, halo_ref[6:7, cols])
    h7 = jnp.where(first_tile, 0.0, halo_ref[7:8, cols])
    r1, r2 = pltpu.roll(a, 1, 0), pltpu.roll(a, 2, 0)
    a1 = jnp.concatenate([jnp.where(row == 0, h7, r1[0:8]), r1[8:]], axis=0)
    a2 = jnp.concatenate([jnp.where(row == 0, h6, jnp.where(row == 1, h7, r2[0:8])), r2[8:]], axis=0)
    c = w_ref[2:3, cols] * a + w_ref[1:2, cols] * a1 + w_ref[0:1, cols] * a2 + b_ref[:, cols]
    return c, (a2, a1, a)


def _conv_specs(tm):
    halo_blocks = tm // 8
    return [BS((tm, D_FF), lambda i: (i, 0)), BS((tm, D_FF), lambda i: (i, 1)),
            BS((8, D_FF), lambda i: (jnp.maximum(i * halo_blocks - 1, 0), 0)),
            BS((8, D_FF), lambda i: (jnp.maximum(i * halo_blocks - 1, 0), 1)),
            BS((3, D_FF), lambda i: (0, 0)), BS((3, D_FF), lambda i: (0, 1)),
            BS((1, D_FF), lambda i: (0, 0)), BS((1, D_FF), lambda i: (0, 1))]


def ffn_fwd_loss(x2, g, w_up3, cw, cb, w_down, target):
    S, D = x2.shape
    Q, _, C = w_up3.shape
    tm = _tile(S, (256,))

    def body(x_ref, g_ref, wu_ref, cw_ref, cb_ref, wd_ref, t_ref, h_ref, a_ref, f_ref, d_ref, l_ref, tail):
        first_tile = pl.program_id(0) == 0

        @pl.when(first_tile)
        def _():
            l_ref[...] = jnp.zeros_like(l_ref)
            tail[...] = jnp.zeros_like(tail)

        xv = x_ref[...]
        h_ref[...] = (xv * _rms(xv) * g_ref[...]).astype(h_ref.dtype)
        for q in range(Q):
            a_ref[:, q * C:(q + 1) * C] = _dot(h_ref[...], wu_ref[q])
        for c0 in range(0, D_FF, CONV_COLS):
            cols, ucols = slice(c0, c0 + CONV_COLS), slice(D_FF + c0, D_FF + c0 + CONV_COLS)
            cg, _ = _conv_taps(a_ref, tail, cw_ref, cb_ref, cols, first_tile)
            cu, _ = _conv_taps(a_ref, tail, cw_ref, cb_ref, ucols, first_tile)
            f_ref[:, cols] = (_gelu(cg) * cu).astype(f_ref.dtype)
        tail[...] = a_ref[tm - 8:tm, :]
        e = _dot(f_ref[...], wd_ref[...]) + xv - t_ref[...]
        d_ref[...] = e * (1.0 / D)
        l_ref[...] += jnp.sum(e * e) * (0.5 / D)

    row = lambda w: BS((tm, w), lambda i: (i, 0))
    const = lambda r, w: BS((r, w), lambda i: (0, 0))
    return _pcall(body, name="ffn_fwd_loss", grid=(S // tm,),
                  in_specs=[row(D), const(1, D), _resident(w_up3.shape), const(3, 2 * D_FF), const(1, 2 * D_FF),
                            _resident(w_down.shape), row(D)],
                  out_specs=[row(D), row(2 * D_FF), row(D_FF), row(D), const(8, 128)],
                  out_shape=[SDS((S, D), MXU_DTYPE), SDS((S, 2 * D_FF), F32), SDS((S, D_FF), MXU_DTYPE),
                             SDS((S, D), F32), SDS((8, 128), F32)],
                  scratch=[pltpu.VMEM((8, 2 * D_FF), F32)])(x2, g, w_up3, cw, cb, w_down, target)


def convgate_bwd(a, dx3, w3, cw, cb, after=None):
    S = a.shape[0]
    tm = _tile(S, (256,))

    def body(ag_ref, au_ref, hg_ref, hu_ref, wg_ref, wu_ref, bg_ref, bu_ref, dx_ref, wd_ref, dc_ref, gw_ref, df_ref):
        first_tile = pl.program_id(0) == 0

        @pl.when(first_tile)
        def _():
            gw_ref[...] = jnp.zeros_like(gw_ref)

        df_ref[...] = _nt_chunks(dx_ref, wd_ref)
        for c0 in range(0, D_FF, CONV_COLS):
            cols, ucols = slice(c0, c0 + CONV_COLS), slice(D_FF + c0, D_FF + c0 + CONV_COLS)
            cg, g_taps = _conv_taps(ag_ref, hg_ref, wg_ref, bg_ref, cols, first_tile)
            cu, u_taps = _conv_taps(au_ref, hu_ref, wu_ref, bu_ref, cols, first_tile)
            df_v = df_ref[:, cols]
            gate, gate_grad = _gelu_and_grad(cg)
            dcg = df_v * cu * gate_grad
            dcu = df_v * gate
            dc_ref[:, cols] = dcg
            dc_ref[:, ucols] = dcu
            for col, dcv, taps in ((cols, dcg, g_taps), (ucols, dcu, u_taps)):
                for j in range(3):
                    gw_ref[j:j + 1, col] += jnp.sum(dcv * taps[j], axis=0, keepdims=True)
                gw_ref[3:4, col] += jnp.sum(dcv, axis=0, keepdims=True)

    return _pcall(body, name="convgate_bwd", grid=(S // tm,), after=after,
                  in_specs=_conv_specs(tm) + [BS((tm, dx3.shape[1]), lambda i: (i, 0)), _resident(w3.shape)],
                  out_specs=[BS((tm, 2 * D_FF), lambda i: (i, 0)), BS((8, 2 * D_FF), lambda i: (0, 0))],
                  out_shape=[SDS((S, 2 * D_FF), F32), SDS((8, 2 * D_FF), F32)],
                  scratch=[pltpu.VMEM((tm, D_FF), F32)])(a, a, a, a, cw, cw, cb, cb, dx3, w3)


def conv_transpose_rms_bwd(dc, cw, w3, x, g, dres):
    S, C = dc.shape
    Q, N, Kc = w3.shape
    tm = _tile(S, (256,))
    nt = S // tm
    halo_blocks = tm // 8

    def body(dc_ref, halo_ref, cw_ref, w_ref, x_ref, g_ref, dr_ref, da_ref, dx_ref, dg_ref):
        @pl.when(pl.program_id(0) == 0)
        def _():
            dg_ref[...] = jnp.zeros_like(dg_ref)

        last_tile = pl.program_id(0) == nt - 1
        row = lax.broadcasted_iota(jnp.int32, (8, CONV_COLS), 0)
        for c0 in range(0, C, CONV_COLS):
            cols = slice(c0, c0 + CONV_COLS)
            h0 = jnp.where(last_tile, 0.0, halo_ref[0:1, cols])
            h1 = jnp.where(last_tile, 0.0, halo_ref[1:2, cols])
            dc_v = dc_ref[:, cols]
            r1, r2 = pltpu.roll(dc_v, tm - 1, 0), pltpu.roll(dc_v, tm - 2, 0)
            n1 = jnp.concatenate([r1[:tm - 8], jnp.where(row == 7, h0, r1[tm - 8:])], axis=0)
            n2 = jnp.concatenate([r2[:tm - 8], jnp.where(row == 7, h1, jnp.where(row == 6, h0, r2[tm - 8:]))], axis=0)
            da_ref[:, cols] = (cw_ref[2:3, cols] * dc_v + cw_ref[1:2, cols] * n1
                               + cw_ref[0:1, cols] * n2).astype(da_ref.dtype)
        xv = x_ref[...]
        dx, dgc = _rms_bwd(_nt_chunks(da_ref, w_ref), xv, g_ref[...], _rms(xv))
        dx_ref[...] = dr_ref[...] + dx
        _acc_rows(dg_ref, 0, dgc)

    row_n = BS((tm, N), lambda i: (i, 0))
    return _pcall(body, name="conv_transpose_rms_bwd", grid=(nt,),
                  in_specs=[BS((tm, C), lambda i: (i, 0)),
                            BS((8, C), lambda i: (jnp.minimum((i + 1) * halo_blocks, S // 8 - 1), 0)),
                            BS((3, C), lambda i: (0, 0)), _resident((Q, N, Kc)), row_n, BS((1, N), lambda i: (0, 0)),
                            row_n],
                  out_specs=[BS((tm, C), lambda i: (i, 0)), row_n, BS((8, N), lambda i: (0, 0))],
                  out_shape=[SDS((S, C), MXU_DTYPE), SDS((S, N), F32), SDS((8, N), F32)])(dc, dc, cw, w3, x, g, dres)


def xattn_block_bwd(dx2, wo3, qx, kn, vb, gxq, wq3, x1, g, wout3, attn, gm, gao, ggo, after=None):
    S, D = qx.shape
    tm = _tile(S, (512, 256))
    hw = D // 2

    def body(dx2_ref, wo_ref, q_ref, kn_ref, vb_ref, gxq_ref, wq_ref, x_ref, g_ref, wout_ref, at_ref, gm_ref,
             gao_ref, ggo_ref, dq_ref, dx_ref, dkn_ref, dv_ref, dgq_ref, dg_ref, da_ref, dgm_ref, dgy_ref):
        @pl.when(pl.program_id(0) == 0)
        def _():
            for ref in (dkn_ref, dv_ref, dgq_ref, dg_ref, dgy_ref):
                ref[...] = jnp.zeros_like(ref)

        gq = gxq_ref[...]
        do_all = _nt_chunks(dx2_ref, wo_ref)
        for h in range(XA_HEADS):
            sl = slice(h * XA_DH, (h + 1) * XA_DH)
            qh, do = q_ref[:, sl], do_all[:, sl]
            r, qn, p = _xa_probs(qh, gq, kn_ref[:, sl])
            dp = _dot(do, vb_ref[:, sl], NT)
            ds = p * (dp - jnp.sum(dp * p, axis=1, keepdims=True)) * (1.0 / math.sqrt(XA_DH))
            dqn = _dot(ds, kn_ref[:, sl])
            dkn_ref[:, sl] += _dot(ds, qn, TN)
            dv_ref[:, sl] += _dot(p, do, TN)
            dqh, dgc = _rms_bwd(dqn, qh, gq, r)
            dq_ref[:, sl] = dqh.astype(dq_ref.dtype)
            _acc_rows(dgq_ref, 0, dgc)
        xv = x_ref[...]
        dx, dgc = _rms_bwd(_nt_chunks(dq_ref, wq_ref), xv, g_ref[...], _rms(xv))
        dx1 = dx2_ref[...] + dx
        dx_ref[...] = dx1
        _acc_rows(dg_ref, 0, dgc)
        dy = _dot(dx1, wout_ref[0], NT)
        av, gmv = at_ref[...], gm_ref[...]
        da, dga = _rms_bwd(dy[:, :hw], av, gao_ref[...], _rms(av))
        dgm, dgg = _rms_bwd(dy[:, hw:], gmv, ggo_ref[...], _rms(gmv))
        da_ref[...] = da
        dgm_ref[...] = dgm
        dgy_ref[0:1, :hw] += jnp.sum(dga, axis=0, keepdims=True)
        dgy_ref[0:1, hw:] += jnp.sum(dgg, axis=0, keepdims=True)

    row = BS((tm, D), lambda i: (i, 0))
    half = BS((tm, hw), lambda i: (i, 0))
    full = lambda r, w: BS((r, w), lambda i: (0, 0))
    return _pcall(body, name="xattn_block_bwd", grid=(S // tm,), after=after,
                  in_specs=[row, _resident(wo3.shape), row, full(MEM_LEN, D), full(MEM_LEN, D), full(1, XA_DH),
                            _resident(wq3.shape), row, full(1, D), _resident(wout3.shape), half, half, full(1, hw),
                            full(1, hw)],
                  out_specs=[row, row, full(MEM_LEN, D), full(MEM_LEN, D), full(8, XA_DH), full(8, D), half, half,
                             full(8, D)],
                  out_shape=[SDS((S, D), MXU_DTYPE), SDS((S, D), F32), SDS((MEM_LEN, D), F32), SDS((MEM_LEN, D), F32),
                             SDS((8, XA_DH), F32), SDS((8, D), F32), SDS((S, hw), F32), SDS((S, hw), F32),
                             SDS((8, D), F32)])(dx2, wo3, qx, kn, vb, gxq, wq3, x1, g, wout3, attn, gm, gao, ggo)


def mem_bwd(kv, dkn, dvb, gxk, after=None):
    def body(kv_ref, dkn_ref, dv_ref, g_ref, dkv_ref, dg_ref):
        dg_ref[...] = jnp.zeros_like(dg_ref)
        for h in range(XA_HEADS):
            sl = slice(h * XA_DH, (h + 1) * XA_DH)
            k = kv_ref[:, sl]
            dk, dgc = _rms_bwd(dkn_ref[:, sl], k, g_ref[...], _rms(k))
            dkv_ref[:, sl] = dk.astype(dkv_ref.dtype)
            _acc_rows(dg_ref, 0, dgc)
        dkv_ref[:, 1024:2048] = dv_ref[...].astype(dkv_ref.dtype)

    full = lambda r, w: BS((r, w), lambda i: (0, 0))
    return _pcall(body, name="mem_bwd", grid=(1,), after=after,
                  in_specs=[full(MEM_LEN, 2048), full(MEM_LEN, 1024), full(MEM_LEN, 1024), full(1, XA_DH)],
                  out_specs=[full(MEM_LEN, 2048), full(8, XA_DH)],
                  out_shape=[SDS((MEM_LEN, 2048), MXU_DTYPE), SDS((8, XA_DH), F32)])(kv, dkn, dvb, gxk)


def _norm_rope_bwd(slabs, douts, g, bm, cos_v, sin_v, first):
    dqns = [d * cos_v + _half_swap(d * sin_v, first) for d in douts]
    rs = [lax.rsqrt(ms + EPS) for ms in _head_means([x * x for x in slabs], bm)]
    projs = _head_means([dqn * g * x for dqn, x in zip(dqns, slabs)], bm)
    dxs = [r * (dqn * g) - x * (r * r * r) * pr for x, dqn, r, pr in zip(slabs, dqns, rs, projs)]
    return dxs, [dqn * x * r for x, dqn, r in zip(slabs, dqns, rs)]


def mixer_core_bwd(proj, cos, sin, gq, gk, gvg, bmat, qr, kr, vb, sinkcol, dattn, dgm, gvn, gu, w2, w2t, bsl,
                   after=None):
    S = qr.shape[0]
    nb = S // BLK

    def body(p_ref, c_ref, s_ref, gq_ref, gk_ref, gvg_ref, b_ref, q_ref, kc_ref, kp_ref, vc_ref, vp_ref, sk_ref,
             do_ref, dgm_ref, gvn_ref, gu_ref, w2_ref, w2t_ref, bsl_ref,
             dp_ref, dsk_ref, dws_ref, dbl_ref, dgq_ref, dgk_ref, dgv_ref,
             carry_k, carry_v, done_k, done_v, dq_keep, dgu_keep, dgvn_keep):
        n = pl.program_id(0)

        @pl.when(n == 0)
        def _():
            for ref in (dsk_ref, dws_ref, dbl_ref, dgq_ref, dgk_ref, dgv_ref, carry_k, carry_v, dq_keep, dgu_keep,
                        dgvn_keep):
                ref[...] = jnp.zeros_like(ref)

        live = (n < nb).astype(F32)
        cos_v, sin_v, bm = c_ref[...], s_ref[...], b_ref[...]
        first = (_lane((BLK, 128)) & 63) < 32
        lo = _lane((BLK, 128)) < 64

        dxs, dgs = _norm_rope_bwd([p_ref[:, s * 128:(s + 1) * 128] for s in range(4)],
                                  [dq_keep[:, s * 128:(s + 1) * 128] for s in range(4)], gq_ref[...], bm,
                                  cos_v, sin_v, first)
        for s, (dx, dg) in enumerate(zip(dxs, dgs)):
            dp_ref[:, s * 128:(s + 1) * 128] = dx.astype(dp_ref.dtype)
            _acc_rows(dgq_ref, 0, dg)
        dp_ref[:, 1024:1536] = (dgu_keep[...] * _gelu_grad(p_ref[:, 1024:1536])).astype(dp_ref.dtype)
        gv, gv_grad = _gelu_and_grad(p_ref[:, 1536:2048])
        dgv, dgc = _rms_bwd(dgvn_keep[...], gv, gvg_ref[...], _rms(gv))
        dp_ref[:, 1536:2048] = (dgv * gv_grad).astype(dp_ref.dtype)
        _acc_rows(dgv_ref, 0, dgc)

        for h in range(2):
            hs, qs = slice(h * 128, (h + 1) * 128), slice(h * 256, (h + 1) * 256)
            kd = jnp.concatenate([kp_ref[:, hs], kc_ref[:, hs]], axis=0)
            vd = jnp.concatenate([vp_ref[:, hs], vc_ref[:, hs]], axis=0)
            sink = jnp.concatenate([sk_ref[2 * h], sk_ref[2 * h + 1]], axis=0)
            qp, p, psink = _swa_probs(q_ref[:, qs], kd, sink, n, lo)
            dop = _by_head(do_ref[:, qs], lo)
            dp = _dot(dop, vd, NT)
            delta = jnp.sum(dp * p, axis=1, keepdims=True)
            ds = p * (dp - delta) * (1.0 / math.sqrt(HEAD_DIM))
            dsink = -psink * delta * live
            dsk_ref[2 * h] += dsink[:2 * BLK]
            dsk_ref[2 * h + 1] += dsink[2 * BLK:]
            dq_keep[:, qs] = _from_heads(_dot(ds, kd), lo)
            dkd = _dot(ds, qp, TN)
            dvd = _dot(p, dop, TN)
            done_k[:, hs] = carry_k[:, hs] + live * dkd[:BLK]
            done_v[:, hs] = carry_v[:, hs] + live * dvd[:BLK]
            carry_k[:, hs] = dkd[BLK:]
            carry_v[:, hs] = dvd[BLK:]
        for j in range(4):
            sl = slice(j * 128, (j + 1) * 128)
            gvn_s = gvn_ref[:, sl]
            m2 = _dot(w2_ref[j], gvn_s)
            mixed = jnp.where(lo, m2[:BLK], m2[BLK:]) + bsl_ref[j]
            dgm_s = dgm_ref[:, sl]
            dgu_keep[:, sl] = dgm_s * mixed
            dmx = dgm_s * gu_ref[:, sl] * live
            d2 = _dot(w2t_ref[j], dmx)
            dgvn_keep[:, sl] = jnp.where(lo, d2[:BLK], d2[BLK:])
            z = jnp.zeros_like(dmx)
            dws_ref[2 * j] += _dot(jnp.where(lo, dmx, z), gvn_s, NT)
            dws_ref[2 * j + 1] += _dot(jnp.where(lo, z, dmx), gvn_s, NT)
            dbl_ref[j] += dmx

        dxs, dgs = _norm_rope_bwd([p_ref[:, 512 + s * 128:640 + s * 128] for s in range(2)],
                                  [done_k[:, s * 128:(s + 1) * 128] for s in range(2)], gk_ref[...], bm,
                                  cos_v, sin_v, first)
        for s, (dx, dg) in enumerate(zip(dxs, dgs)):
            dp_ref[:, 512 + s * 128:640 + s * 128] = dx.astype(dp_ref.dtype)
            _acc_rows(dgk_ref, 0, dg)
        dp_ref[:, 768:1024] = done_v[...].astype(dp_ref.dtype)

    last = nb - 1
    cur = lambda w: BS((BLK, w), lambda n: (jnp.minimum(n, last), 0))
    prev = lambda w: BS((BLK, w), lambda n: (jnp.clip(n - 1, 0, last), 0))
    done = lambda w: BS((BLK, w), lambda n: (jnp.maximum(n - 1, 0), 0))
    const = lambda *shape: BS(shape, lambda n: (0,) * len(shape))
    return _pcall(body, name="mixer_core_bwd", grid=(nb + 1,), after=after,
                  in_specs=[done(IN_COLS_DUP), done(128), done(128), const(1, 128), const(1, 128), const(1, 512),
                            const(128, 128), cur(512), cur(256), prev(256), cur(256), prev(256),
                            const(4, 2 * BLK, 1), cur(512), cur(512), cur(512), cur(512), const(4, 2 * BLK, BLK),
                            const(4, 2 * BLK, BLK), const(4, BLK, 128)],
                  out_specs=[done(IN_COLS_DUP), const(4, 2 * BLK, 1), const(8, BLK, BLK), const(4, BLK, 128),
                             const(8, 128), const(8, 128), const(8, 512)],
                  out_shape=[SDS((S, IN_COLS_DUP), MXU_DTYPE), SDS((4, 2 * BLK, 1), F32), SDS((8, BLK, BLK), F32),
                             SDS((4, BLK, 128), F32), SDS((8, 128), F32), SDS((8, 128), F32), SDS((8, 512), F32)],
                  scratch=[pltpu.VMEM((BLK, 256), F32)] * 4 + [pltpu.VMEM((BLK, 512), F32)] * 3)(
        proj, cos, sin, gq, gk, gvg, bmat, qr, kr, kr, vb, vb, sinkcol, dattn, dgm, gvn, gu, w2, w2t, bsl)


BIG = (("w_in", (1024, 448), True), ("w_out", (256, 1024), False), ("xa_wq", (256, 1024), False),
       ("xa_wkv", (1024, 512), True), ("xa_wo", (256, 1024), False), ("ffn_up", (1024, 1408), True),
       ("ffn_down", (704, 1024), False))
BIG_NAMES = tuple(n for n, _, _ in BIG)
SMALL_VECS = (("mix_norm", 1024), ("q_norm", 64), ("k_norm", 64), ("attn_sinks", 8), ("gmlp_v_norm", 512),
              ("attn_out_norm", 512), ("gmlp_out_norm", 512), ("xa_norm", 1024), ("mem_norm", 1024),
              ("xa_q_norm", 256), ("xa_k_norm", 256), ("ffn_norm", 1024), ("ffn_conv_b", 5632))
SMALL = tuple(n for n, _ in SMALL_VECS) + ("gmlp_bs", "gmlp_ws", "ffn_conv")
WEIGHTS = ("mix_norm", "w_in", "q_norm", "k_norm", "attn_sinks", "gmlp_v_norm", "gmlp_ws", "gmlp_bs",
           "attn_out_norm", "gmlp_out_norm", "w_out", "xa_norm", "mem_norm", "xa_wq", "xa_wkv", "xa_q_norm",
           "xa_k_norm", "xa_wo", "ffn_norm", "ffn_up", "ffn_conv", "ffn_conv_b", "ffn_down")
CONV_SHARD = (3, 1408)
CONV_LANE_ROWS = CONV_SHARD[1] // 128
CONV_CHIP_ROWS = 40


def _small_rows():
    rows, r = {}, 0
    for n, length in SMALL_VECS:
        rows[n] = r
        r += -(-length // 128)
    r += -r % 8
    rows["gmlp_bs"] = r
    r += 8
    rows["gmlp_ws"] = r
    r += 8 * BLK
    rows["ffn_conv"] = r
    r += N_CHIPS * CONV_CHIP_ROWS
    return rows, r


SMALL_ROW, SMALL_ROWS = _small_rows()


def pack_small(dg_mix, dgq, dgk, dsk, dg_gvn, dg_y, dg_xa, dg_mem, dg_xq, dg_xk, dg_ffn, gcw, dbl, dws):
    def body(mix_ref, q_ref, k_ref, sk_ref, gvn_ref, y_ref, xa_ref, mem_ref, xq_ref, xk_ref, ffn_ref, cw_ref,
             dbl_ref, dws_ref, o_ref):
        o_ref[...] = jnp.zeros_like(o_ref)
        lane = _lane((1, 128))

        def put(name, src_ref, row, lane0, length):
            for k in range(length // 128):
                o_ref[SMALL_ROW[name] + k:SMALL_ROW[name] + k + 1, :] = src_ref[row:row + 1, lane0 + k * 128:lane0 + (k + 1) * 128]

        put("mix_norm", mix_ref, 0, 0, 1024)
        for name, ref in (("q_norm", q_ref), ("k_norm", k_ref)):
            v = ref[0:1, :]
            o_ref[SMALL_ROW[name]:SMALL_ROW[name] + 1, :] = jnp.where(lane < HEAD_DIM, v + pltpu.roll(v, 64, 1), 0.0)
        sinks = jnp.zeros((1, 128), F32)
        for s in range(4):
            col = sk_ref[s]
            sinks = sinks + jnp.where(lane == 2 * s, jnp.sum(col[:BLK]), 0.0) + jnp.where(lane == 2 * s + 1, jnp.sum(col[BLK:]), 0.0)
        o_ref[SMALL_ROW["attn_sinks"]:SMALL_ROW["attn_sinks"] + 1, :] = sinks
        put("gmlp_v_norm", gvn_ref, 0, 0, 512)
        put("attn_out_norm", y_ref, 0, 0, 512)
        put("gmlp_out_norm", y_ref, 0, 512, 512)
        put("xa_norm", xa_ref, 0, 0, 1024)
        put("mem_norm", mem_ref, 0, 0, 1024)
        put("xa_q_norm", xq_ref, 0, 0, 256)
        put("xa_k_norm", xk_ref, 0, 0, 256)
        put("ffn_norm", ffn_ref, 0, 0, 1024)
        put("ffn_conv_b", cw_ref, 3, 0, 2 * D_FF)
        r8 = lax.broadcasted_iota(jnp.int32, (8, 128), 0)
        l8 = _lane((8, 128))
        bs = jnp.zeros((8, BLK), F32)
        for j in range(4):
            sel = (((r8 == 2 * j) & (l8 < 64)) | ((r8 == 2 * j + 1) & (l8 >= 64))).astype(F32).astype(BF16)
            xj = dbl_ref[j]
            hi = xj.astype(BF16)
            lo = (xj - hi.astype(F32)).astype(BF16)
            bs = bs + lax.dot_general(sel, hi, NT, preferred_element_type=F32) + lax.dot_general(sel, lo, NT, preferred_element_type=F32)
        o_ref[SMALL_ROW["gmlp_bs"]:SMALL_ROW["gmlp_bs"] + 8, :] = bs
        causal = lax.broadcasted_iota(jnp.int32, (BLK, BLK), 0) >= lax.broadcasted_iota(jnp.int32, (BLK, BLK), 1)
        for h in range(8):
            r0 = SMALL_ROW["gmlp_ws"] + h * BLK
            o_ref[r0:r0 + BLK, :] = jnp.where(causal, dws_ref[h], 0.0)
        for q in range(N_CHIPS):
            for j in range(3):
                for k in range(CONV_LANE_ROWS):
                    r0 = SMALL_ROW["ffn_conv"] + q * CONV_CHIP_ROWS + j * CONV_LANE_ROWS + k
                    l0 = (q * CONV_LANE_ROWS + k) * 128
                    o_ref[r0:r0 + 1, :] = cw_ref[j:j + 1, l0:l0 + 128]

    args = (dg_mix, dgq, dgk, dsk, dg_gvn, dg_y, dg_xa, dg_mem, dg_xq, dg_xk, dg_ffn, gcw, dbl, dws)
    full = lambda a: BS(a.shape, lambda i, nd=a.ndim: (0,) * nd)
    return _pcall(body, name="pack_small", grid=(1,), in_specs=[full(a) for a in args],
                  out_specs=BS((SMALL_ROWS, 128), lambda i: (0, 0)), out_shape=SDS((SMALL_ROWS, 128), F32))(*args)


def _adam(w, g, m, v):
    mn = ADAM_B1 * m + (1.0 - ADAM_B1) * g
    vn = ADAM_B2 * v + (1.0 - ADAM_B2) * (g * g)
    m_hat = mn / (1.0 - ADAM_B1 ** ADAM_STEP)
    v_hat = vn / (1.0 - ADAM_B2 ** ADAM_STEP)
    return -ADAM_LR * (m_hat / (jnp.sqrt(v_hat) + ADAM_EPS) + ADAM_WD * w), mn, vn


def adamw_small(gsum, w, m, v, chipvec):
    n = len(SMALL)

    def body(chip_ref, g_ref, *refs):
        w_refs, m_refs, v_refs = refs[:n], refs[n:2 * n], refs[2 * n:3 * n]
        outs = refs[3 * n:]
        go, do, mo, vo = outs[:n], outs[n:2 * n], outs[2 * n:3 * n], outs[3 * n:]

        def update(i, idx, g):
            d, mn, vn = _adam(w_refs[i][idx], g, m_refs[i][idx], v_refs[i][idx])
            go[i][idx] = g
            do[i][idx] = d
            mo[i][idx] = mn
            vo[i][idx] = vn

        for i, (name, length) in enumerate(SMALL_VECS):
            for k in range(-(-length // 128)):
                wd = min(128, length - k * 128)
                r = SMALL_ROW[name] + k
                update(i, (slice(0, 1), slice(k * 128, k * 128 + wd)), g_ref[r:r + 1, 0:wd])
        i_bs, i_ws, i_cv = len(SMALL_VECS), len(SMALL_VECS) + 1, len(SMALL_VECS) + 2
        update(i_bs, (0,), g_ref[SMALL_ROW["gmlp_bs"]:SMALL_ROW["gmlp_bs"] + 8, :])
        for h in range(8):
            r0 = SMALL_ROW["gmlp_ws"] + h * BLK
            update(i_ws, (0, h), g_ref[r0:r0 + BLK, :])
        mine = g_ref[pl.ds(pl.multiple_of(SMALL_ROW["ffn_conv"] + chip_ref[0] * CONV_CHIP_ROWS, 8), CONV_CHIP_ROWS), :]
        for j in range(3):
            for k in range(CONV_LANE_ROWS):
                r = j * CONV_LANE_ROWS + k
                update(i_cv, (0, slice(j, j + 1), slice(k * 128, (k + 1) * 128)), mine[r:r + 1, :])

    nat = [w[nm] for nm in SMALL]
    full = lambda a: BS(a.shape, lambda i, c, nd=a.ndim: (0,) * nd)
    outs = _pcall(body, name="adamw_small", grid=(1,), prefetch=1,
                  in_specs=[BS((SMALL_ROWS, 128), lambda i, c: (0, 0))] + [full(a) for a in nat] * 3,
                  out_specs=[full(a) for a in nat] * 4, out_shape=[SDS(a.shape, F32) for a in nat] * 4)(
        chipvec, gsum, *nat, *[m[nm] for nm in SMALL], *[v[nm] for nm in SMALL])
    return outs[:n], outs[n:2 * n], outs[2 * n:3 * n], outs[3 * n:]


def adamw_matrix(w, m, v, g_own, g_other, cvec, *, name):
    _, r, c = w.shape
    half = r // 2
    tr = _tile(half, (256, 176, 128))
    T = half // tr

    def body(c_ref, w_ref, m_ref, v_ref, own_ref, oth_ref, g_ref, d_ref, mo_ref, vo_ref):
        g = jnp.where(pl.program_id(0) == c_ref[0], own_ref[...], oth_ref[...])
        d, mn, vn = _adam(w_ref[...], g, m_ref[...], v_ref[...])
        g_ref[...] = g
        d_ref[...] = d
        mo_ref[...] = mn
        vo_ref[...] = vn

    nat = BS((None, tr, c), lambda hf, t, cr: (0, hf * T + t, 0))
    hlf = BS((tr, c), lambda hf, t, cr: (t, 0))
    return _pcall(body, name=name, grid=(2, T), prefetch=1, in_specs=[nat, nat, nat, hlf, hlf], out_specs=[nat] * 4,
                  out_shape=[SDS(w.shape, F32)] * 4)(cvec, w, m, v, g_own, g_other)


def _place():
    return lax.axis_index("x"), lax.axis_index("y"), lax.axis_index("c")


def _other_chips(x, y):
    return [(1 - x, y), (x, 1 - y), (1 - x, 1 - y)]


def _rows_of_core(c, half):
    return pl.ds(pl.multiple_of(c * half, 16), half)


def _rcopy(src, dst, sems, k, to):
    return pltpu.make_async_remote_copy(src_ref=src, dst_ref=dst, send_sem=sems[0].at[k], recv_sem=sems[1].at[k],
                                        device_id=to, device_id_type=MESH)


def cast_shards(shards, conv, chipvec):
    n = len(shards)

    def body(chip_ref, *refs):
        for i_ref, o_ref in zip(refs[:n + 1], refs[n + 1:]):
            o_ref[...] = i_ref[...].astype(o_ref.dtype)

    in_specs = [BS((s.shape[0] // 4, s.shape[1]), lambda i, p: (i, 0)) for s in shards]
    in_specs.append(BS(conv.shape, lambda i, p: (0, 0)))
    out_specs = [BS((None, s.shape[0] // 4, s.shape[1]), lambda i, p: (p[0], i, 0)) for s in shards]
    out_specs.append(BS((None,) + conv.shape, lambda i, p: (p[0], 0, 0)))
    out_shape = [SDS((N_CHIPS,) + s.shape, MXU_DTYPE) for s in shards] + [SDS((N_CHIPS,) + conv.shape, F32)]
    return _pcall(body, name="cast_shards", grid=(4,), prefetch=1, in_specs=in_specs, out_specs=out_specs,
                  out_shape=out_shape)(chipvec, *shards, conv)


HBM = pl.BlockSpec(memory_space=pltpu.HBM)
SEM = pl.BlockSpec(memory_space=pltpu.SEMAPHORE)
DATAFLOW = pltpu.SideEffectType.DATAFLOW_SIDE_EFFECTING
VMEM_WHOLE = pl.BlockSpec(memory_space=pltpu.VMEM)
TOKEN = jax.ShapeDtypeStruct((8, 128), jnp.float32)


def _gather_copies(bufs, send_sems, recv_sems, outgoing):
    x, y, c = _place()
    p = 2 * x + y
    cps = []
    for i, o in enumerate(bufs):
        for j, (cx, cy) in enumerate(_other_chips(x, y)):
            slot = o.at[p] if outgoing else o.at[2 * cx + cy]
            cps.append(_rcopy(slot, slot, (send_sems, recv_sems), 3 * i + j, (cx, cy, c)))
    return cps


def gather_start(slots, after):
    n = len(slots)

    def body(*refs):
        send_sems, recv_sems, thru, token = refs[n + 1], refs[n + 2], refs[n + 3:2 * n + 3], refs[2 * n + 3]
        for cp in _gather_copies(thru, send_sems, recv_sems, True):
            cp.start()
        token[...] = jnp.zeros_like(token)

    hbm = [pltpu.with_memory_space_constraint(s, pltpu.HBM) for s in slots]
    outs = pl.pallas_call(
        body, name="gather_start_%d" % n,
        out_shape=[pltpu.SemaphoreType.DMA((3 * n,)), pltpu.SemaphoreType.DMA((3 * n,))]
        + [pltpu.HBM(s.shape, s.dtype) for s in slots] + [TOKEN],
        in_specs=[HBM] * n + [ANY], out_specs=[SEM, SEM] + [HBM] * n + [VMEM_WHOLE],
        input_output_aliases={i: 2 + i for i in range(n)},
        compiler_params=pltpu.CompilerParams(has_side_effects=DATAFLOW))(*hbm, after)
    return outs[0], outs[1], outs[2:2 + n], outs[2 + n]


def gather_wait(send_sems, recv_sems, bufs, *after):
    n = len(bufs)

    def body(*refs):
        ins, send_ref, recv_ref = refs[:n], refs[n], refs[n + 1]
        for cp in _gather_copies(ins, send_ref, recv_ref, False):
            cp.wait_send()
            cp.wait_recv()

    return pl.pallas_call(
        body, name="gather_wait_%d" % n, out_shape=[pltpu.HBM(s.shape, s.dtype) for s in bufs],
        in_specs=[HBM] * n + [SEM, SEM] + [ANY] * len(after), out_specs=[HBM] * n,
        input_output_aliases={i: i for i in range(n)},
        compiler_params=pltpu.CompilerParams(has_side_effects=DATAFLOW))(*bufs, send_sems, recv_sems, *after)


def _peers(x, y, c):
    return [(1 - x if k & 4 else x, 1 - y if k & 2 else y, 1 - c if k & 1 else c) for k in range(1, N_DEV)]


def _partial_copies(g_ref, land_ref, send_sems, recv_sems, outgoing):
    x, y, c = _place()
    half = g_ref.shape[1] // 2
    cps = []
    for k, (px, py, pc) in enumerate(_peers(x, y, c)):
        src = g_ref.at[2 * px + py, _rows_of_core(pc, half)]
        dst = land_ref.at[4 * x + 2 * y + c] if outgoing else land_ref.at[4 * px + 2 * py + pc]
        cps.append(_rcopy(src, dst, (send_sems, recv_sems), k, (px, py, pc)))
    return cps


def partials_start(g, *, name):
    land = lax.empty((N_DEV, g.shape[1] // 2, g.shape[2]), g.dtype)

    def body(g_ref, land_ref, send_sems, recv_sems, g_thru, land_thru, token):
        for cp in _partial_copies(g_thru, land_thru, send_sems, recv_sems, True):
            cp.start()
        token[...] = jnp.zeros_like(token)

    return pl.pallas_call(
        body, name=name,
        out_shape=[pltpu.SemaphoreType.DMA((N_DEV - 1,)), pltpu.SemaphoreType.DMA((N_DEV - 1,)),
                   pltpu.HBM(g.shape, g.dtype), pltpu.HBM(land.shape, land.dtype), TOKEN],
        in_specs=[HBM, HBM], out_specs=[SEM, SEM, HBM, HBM, VMEM_WHOLE], input_output_aliases={0: 2, 1: 3},
        compiler_params=pltpu.CompilerParams(has_side_effects=DATAFLOW))(
        pltpu.with_memory_space_constraint(g, pltpu.HBM), pltpu.with_memory_space_constraint(land, pltpu.HBM))


def partials_wait(started, after):
    n = len(started)

    def body(*refs):
        for i in range(n):
            send_ref, recv_ref, g_ref, land_ref = refs[4 * i:4 * i + 4]
            for cp in _partial_copies(g_ref, land_ref, send_ref, recv_ref, False):
                cp.wait_send()
                cp.wait_recv()

    flat = [a for s in started for a in s]
    bufs = [a for s in started for a in s[2:]]
    outs = pl.pallas_call(
        body, name="partials_wait", out_shape=[pltpu.HBM(b.shape, b.dtype) for b in bufs],
        in_specs=[SEM, SEM, HBM, HBM] * n + [ANY], out_specs=[HBM] * (2 * n),
        input_output_aliases={4 * i + 2 + j: 2 * i + j for i in range(n) for j in range(2)},
        compiler_params=pltpu.CompilerParams(has_side_effects=DATAFLOW))(*flat, after)
    return [(outs[2 * i], outs[2 * i + 1]) for i in range(n)]


def sum_partials(pairs, order):
    n = len(pairs)

    def body(o_ref, *refs):
        j = pl.program_id(0)
        for g_ref, l_ref, f_ref in zip(refs[:n], refs[n:2 * n], refs[2 * n:]):
            @pl.when(j == 0)
            def _():
                f_ref[...] = g_ref[...].astype(F32)

            @pl.when(j > 0)
            def _():
                f_ref[...] += l_ref[...].astype(F32)

    g4 = [g.reshape(g.shape[0], 2, g.shape[1] // 2, g.shape[2]) for g, _ in pairs]
    lands = [l for _, l in pairs]
    return _pcall(body, name="sum_partials", grid=(N_DEV,), prefetch=1,
                  in_specs=[BS((None, None) + g.shape[2:], lambda j, o: (o[0], o[1], 0, 0)) for g in g4]
                  + [BS((None,) + l.shape[1:], lambda j, o: (o[jnp.maximum(j, 1) + 1], 0, 0)) for l in lands],
                  out_specs=[BS(l.shape[1:], lambda j, o: (0, 0)) for l in lands],
                  out_shape=[SDS(l.shape[1:], F32) for l in lands])(order, *g4, *lands)


def _pair_copies(f_refs, land_refs, send_sems, recv_sems):
    x, y, c = _place()
    return [_rcopy(f, o, (send_sems, recv_sems), i, (x, y, 1 - c)) for i, (f, o) in enumerate(zip(f_refs, land_refs))]


def pair_start(fs):
    n = len(fs)
    lands = [lax.empty(f.shape, f.dtype) for f in fs]

    def body(*refs):
        send_sems, recv_sems = refs[2 * n], refs[2 * n + 1]
        thru, land_thru, token = refs[2 * n + 2:3 * n + 2], refs[3 * n + 2:4 * n + 2], refs[4 * n + 2]
        for cp in _pair_copies(thru, land_thru, send_sems, recv_sems):
            cp.start()
        token[...] = jnp.zeros_like(token)

    hbm = [pltpu.with_memory_space_constraint(a, pltpu.HBM) for a in list(fs) + lands]
    outs = pl.pallas_call(
        body, name="pair_start",
        out_shape=[pltpu.SemaphoreType.DMA((n,)), pltpu.SemaphoreType.DMA((n,))]
        + [pltpu.HBM(a.shape, a.dtype) for a in list(fs) + lands] + [TOKEN],
        in_specs=[HBM] * (2 * n), out_specs=[SEM, SEM] + [HBM] * (2 * n) + [VMEM_WHOLE],
        input_output_aliases={i: 2 + i for i in range(2 * n)},
        compiler_params=pltpu.CompilerParams(has_side_effects=DATAFLOW))(*hbm)
    return outs[0], outs[1], outs[2:2 + n], outs[2 + n:2 + 2 * n], outs[2 + 2 * n]


def pair_wait(send_sems, recv_sems, fs, lands, after):
    n = len(fs)

    def body(*refs):
        for cp in _pair_copies(refs[:n], refs[n:2 * n], refs[2 * n], refs[2 * n + 1]):
            cp.wait_send()
            cp.wait_recv()

    outs = pl.pallas_call(
        body, name="pair_wait", out_shape=[pltpu.HBM(a.shape, a.dtype) for a in list(fs) + list(lands)],
        in_specs=[HBM] * (2 * n) + [SEM, SEM, ANY], out_specs=[HBM] * (2 * n),
        input_output_aliases={i: i for i in range(2 * n)},
        compiler_params=pltpu.CompilerParams(has_side_effects=DATAFLOW))(*fs, *lands, send_sems, recv_sems, after)
    return outs[:n], outs[n:]


def _small_copies(s_ref, land_ref, send_sems, recv_sems, outgoing):
    x, y, c = _place()
    cps = []
    for k, (px, py, pc) in enumerate(_peers(x, y, c)):
        dst = land_ref.at[4 * x + 2 * y + c] if outgoing else land_ref.at[4 * px + 2 * py + pc]
        cps.append(_rcopy(s_ref, dst, (send_sems, recv_sems), k, (px, py, pc)))
    return cps


def small_start(sm):
    land = lax.empty((N_DEV,) + sm.shape, sm.dtype)

    def body(s_ref, land_ref, send_sems, recv_sems, s_thru, land_thru):
        for cp in _small_copies(s_thru, land_thru, send_sems, recv_sems, True):
            cp.start()

    return pl.pallas_call(
        body, name="small_start",
        out_shape=[pltpu.SemaphoreType.DMA((N_DEV - 1,)), pltpu.SemaphoreType.DMA((N_DEV - 1,)),
                   pltpu.HBM(sm.shape, sm.dtype), pltpu.HBM(land.shape, land.dtype)],
        in_specs=[HBM, HBM], out_specs=[SEM, SEM, HBM, HBM], input_output_aliases={0: 2, 1: 3},
        compiler_params=pltpu.CompilerParams(has_side_effects=DATAFLOW))(
        pltpu.with_memory_space_constraint(sm, pltpu.HBM), pltpu.with_memory_space_constraint(land, pltpu.HBM))


def small_wait(send_sems, recv_sems, sm, land, *after):
    def body(send_ref, recv_ref, s_ref, land_ref, *rest):
        for cp in _small_copies(s_ref, land_ref, send_ref, recv_ref, False):
            cp.wait_send()
            cp.wait_recv()

    return pl.pallas_call(
        body, name="small_wait", out_shape=[pltpu.HBM(sm.shape, sm.dtype), pltpu.HBM(land.shape, land.dtype)],
        in_specs=[SEM, SEM, HBM, HBM] + [ANY] * len(after), out_specs=[HBM, HBM], input_output_aliases={2: 0, 3: 1},
        compiler_params=pltpu.CompilerParams(has_side_effects=DATAFLOW))(send_sems, recv_sems, sm, land, *after)


def sum_small(own, land, mevec):
    n, rows, width = land.shape
    tr = _tile(rows, (184, 8))

    def body(me_ref, own_ref, land_ref, o_ref):
        acc = jnp.zeros((tr, width), F32)
        for s in range(n):
            acc = acc + jnp.where(me_ref[0] == s, own_ref[...], land_ref[s])
        o_ref[...] = acc

    return _pcall(body, name="sum_small", grid=(rows // tr,), prefetch=1,
                  in_specs=[BS((tr, width), lambda i, me: (i, 0)), BS((n, tr, width), lambda i, me: (0, i, 0))],
                  out_specs=BS((tr, width), lambda i, me: (i, 0)), out_shape=SDS((rows, width), F32))(mevec, own, land)


def _to_full(blk, col):
    n, r, c = blk.shape
    return blk.transpose(1, 0, 2).reshape(r, n * c) if col else blk.reshape(n * r, c)


def _dup_cols(w):
    dup = lambda t: jnp.concatenate([t[:, :64], t[:, :64], t[:, 64:], t[:, 64:]], axis=1)
    return jnp.concatenate([w[:, :512], dup(w[:, 512:640]), dup(w[:, 640:768]), w[:, 768:]], axis=1)


def _fold_cols(d):
    fold = lambda t: jnp.concatenate([t[:, 0:64] + t[:, 64:128], t[:, 128:192] + t[:, 192:256]], axis=1)
    return jnp.concatenate([d[:, :512], fold(d[:, 512:768]), fold(d[:, 768:1024]), d[:, 1024:]], axis=1)


def _local_step(x, mem, positions, target, w_in, later, sp, emit):
    gain = lambda n: sp[n].reshape(1, -1)
    half = HEAD_DIM // 2
    inv_freq = 1.0 / (10000.0 ** (jnp.arange(half, dtype=F32) * (2.0 / HEAD_DIM)))
    ang = positions.astype(F32)[:, None] * inv_freq
    cos, sin = jnp.cos(ang), jnp.sin(ang)
    cos128 = jnp.tile(cos, (1, 4))
    sin128 = jnp.concatenate([-sin, sin, -sin, sin], axis=1)
    seg = jnp.arange(128) // HEAD_DIM
    bmat = (seg[:, None] == seg[None, :]).astype(BF16)
    gq128, gk128 = jnp.tile(gain("q_norm"), (1, 2)), jnp.tile(gain("k_norm"), (1, 2))
    sinkcol = jnp.repeat(sp["attn_sinks"].reshape(4, 2), BLK, axis=1).reshape(4, 2 * BLK, 1)
    wsc = sp["gmlp_ws"] * jnp.tril(jnp.ones((BLK, BLK), F32))[None]
    w2 = wsc.reshape(4, 2 * BLK, BLK).astype(MXU_DTYPE)
    w2t = wsc.swapaxes(1, 2).reshape(4, 2 * BLK, BLK).astype(MXU_DTYPE)
    bsl = jnp.repeat(sp["gmlp_bs"].reshape(4, 2, BLK).transpose(0, 2, 1), HEAD_DIM, axis=2)
    cb = sp["ffn_conv_b"].reshape(1, -1)
    w_in_d = _dup_cols(_to_full(w_in(cos128, sin128, gq128, gk128, sinkcol, w2, w2t, bsl), True))[None]

    h1, proj = rms_mm(x, gain("mix_norm"), w_in_d, name="mix_in")
    qr, kr, vb, gu, gvn, attn, gm, y = mixer_core_fwd(proj, cos128, sin128, gq128, gk128, gain("gmlp_v_norm"), bmat,
                                                      sinkcol, gain("attn_out_norm"), w2, bsl, gain("gmlp_out_norm"))
    wf, last = later(y)
    w_out, xa_wq, xa_wo = (_to_full(wf[n], False) for n in ("w_out", "xa_wq", "xa_wo"))
    mn, kv = rms_mm(mem, gain("mem_norm"), wf["xa_wkv"], name="xa_kv")
    kn, vbx = mem_pre(kv, gain("xa_k_norm"))
    x1, h2, qx, xo, x2 = xattn_block_fwd(y, w_out, x, gain("xa_norm"), xa_wq, kn, vbx, gain("xa_q_norm"), xa_wo)
    ffn_w, cw = last(x2)
    wf = {**wf, **ffn_w}
    ffn_down = _to_full(wf["ffn_down"], False)
    h3, a, f, dx3, loss_acc = ffn_fwd_loss(x2, gain("ffn_norm"), wf["ffn_up"], cw, cb, ffn_down, target)

    by_rows = lambda g: g.reshape(N_CHIPS, g.shape[1] // N_CHIPS, g.shape[2])
    sent = emit("ffn_down", by_rows(mm_tn(f, dx3, name="g_ffn_down", out_dtype=WIRE_DTYPE)))
    dc, gcw = convgate_bwd(a, dx3, ffn_down[None], cw, cb, after=sent)
    da, dx2, dg_ffn = conv_transpose_rms_bwd(dc, cw, wf["ffn_up"], x2, gain("ffn_norm"), dx3)
    sent = emit("ffn_up", mm_tn(h3, da, name="g_ffn_up", out_dtype=WIRE_DTYPE, chunks=N_CHIPS))
    sent = emit("xa_wo", by_rows(mm_tn(xo, dx2, name="g_xa_wo", out_dtype=WIRE_DTYPE, after=sent)))
    dqx, dx1, dkn, dvx, dg_xq, dg_xa, dattn, dgm, dg_y = xattn_block_bwd(
        dx2, xa_wo[None], qx, kn, vbx, gain("xa_q_norm"), xa_wq[None], x1, gain("xa_norm"), w_out[None], attn, gm,
        gain("attn_out_norm"), gain("gmlp_out_norm"), after=sent)
    sent = emit("xa_wq", by_rows(mm_tn(h2, dqx, name="g_xa_wq", out_dtype=WIRE_DTYPE)))
    dkv, dg_xk = mem_bwd(kv, dkn, dvx, gain("xa_k_norm"), after=sent)
    _, dg_mem = mm_nt_rms_bwd(dkv, wf["xa_wkv"], mem, gain("mem_norm"), jnp.zeros_like(mem), name="d_mem")
    sent = emit("xa_wkv", mm_tn(mn, dkv, name="g_xa_wkv", out_dtype=WIRE_DTYPE, chunks=N_CHIPS))
    sent = emit("w_out", by_rows(mm_tn(y, dx1, name="g_w_out", out_dtype=WIRE_DTYPE, after=sent)))
    dproj, dsk, dws, dbl, dgq, dgk, dg_gvn = mixer_core_bwd(
        proj, cos128, sin128, gq128, gk128, gain("gmlp_v_norm"), bmat, qr, kr, vb, sinkcol, dattn, dgm, gvn, gu,
        w2, w2t, bsl, after=sent)
    g_in = _fold_cols(mm_tn(h1, dproj, name="g_w_in", out_dtype=F32)[0])
    sent = emit("w_in", g_in.reshape(1024, N_CHIPS, 448).transpose(1, 0, 2).astype(WIRE_DTYPE))
    grad_x, dg_mix = mm_nt_rms_bwd(dproj, w_in_d, x, gain("mix_norm"), dx1, name="d_x", tm=1024, after=sent)
    packed = pack_small(dg_mix, dgq, dgk, dsk, dg_gvn, dg_y, dg_xa, dg_mem, dg_xq, dg_xk, dg_ffn, gcw, dbl, dws)
    return loss_acc, grad_x, packed


def _gather_step(w, chipvec):
    slots = cast_shards([w[n][0] for n in BIG_NAMES], w["ffn_conv"][0], chipvec)
    send_a, recv_a, first, token = gather_start(slots[:1], chipvec)
    send_b, recv_b, mid, token = gather_start(slots[1:5], token)
    send_c, recv_c, rest, token = gather_start(slots[5:], token)

    def w_in(*after):
        return gather_wait(send_a, recv_a, first, token, *after)[0]

    def last(after):
        got = gather_wait(send_c, recv_c, rest, after)
        return dict(zip(BIG_NAMES[5:], got[:-1])), _to_full(got[-1], True)

    def later(after):
        return dict(zip(BIG_NAMES[1:5], gather_wait(send_b, recv_b, mid, after))), last

    return w_in, later, token


def _reduce_update(started, packed, w, m, v, chipvec, cvec, order):
    small_sent = small_start(packed)
    own = sum_partials(partials_wait([started[n] for n in BIG_NAMES], small_sent[2]), order)
    pair_send, pair_recv, own, lands, pair_started = pair_start(own)
    own, other = pair_wait(pair_send, pair_recv, own, lands, pair_started)
    res = [{}, {}, {}, {}]
    for n, g_own, g_other in zip(BIG_NAMES, own, other):
        for d, o in zip(res, adamw_matrix(w[n], m[n], v[n], g_own, g_other, cvec, name="adamw_" + n)):
            d[n] = o
    mevec = (2 * order[0:1] + order[1:2]).astype(jnp.int32)
    small_sum = sum_small(*small_wait(*small_sent, *[res[3][n] for n in BIG_NAMES]), mevec)
    for d, outs in zip(res, adamw_small(small_sum, w, m, v, chipvec)):
        d.update(zip(SMALL, outs))
    return res


def kernel(x, mem, positions, mix_norm, w_in, q_norm, k_norm, attn_sinks, gmlp_v_norm, gmlp_ws, gmlp_bs, attn_out_norm, gmlp_out_norm, w_out, xa_norm, mem_norm, xa_wq, xa_wkv, xa_q_norm, xa_k_norm, xa_wo, ffn_norm, ffn_up, ffn_conv, ffn_conv_b, ffn_down, loss_target, m_mix_norm, m_w_in, m_q_norm, m_k_norm, m_attn_sinks, m_gmlp_v_norm, m_gmlp_ws, m_gmlp_bs, m_attn_out_norm, m_gmlp_out_norm, m_w_out, m_xa_norm, m_mem_norm, m_xa_wq, m_xa_wkv, m_xa_q_norm, m_xa_k_norm, m_xa_wo, m_ffn_norm, m_ffn_up, m_ffn_conv, m_ffn_conv_b, m_ffn_down, v_mix_norm, v_w_in, v_q_norm, v_k_norm, v_attn_sinks, v_gmlp_v_norm, v_gmlp_ws, v_gmlp_bs, v_attn_out_norm, v_gmlp_out_norm, v_w_out, v_xa_norm, v_mem_norm, v_xa_wq, v_xa_wkv, v_xa_q_norm, v_xa_k_norm, v_xa_wo, v_ffn_norm, v_ffn_up, v_ffn_conv, v_ffn_conv_b, v_ffn_down):
    w = dict(mix_norm=mix_norm, w_in=w_in, q_norm=q_norm, k_norm=k_norm, attn_sinks=attn_sinks, gmlp_v_norm=gmlp_v_norm, gmlp_ws=gmlp_ws, gmlp_bs=gmlp_bs, attn_out_norm=attn_out_norm, gmlp_out_norm=gmlp_out_norm, w_out=w_out, xa_norm=xa_norm, mem_norm=mem_norm, xa_wq=xa_wq, xa_wkv=xa_wkv, xa_q_norm=xa_q_norm, xa_k_norm=xa_k_norm, xa_wo=xa_wo, ffn_norm=ffn_norm, ffn_up=ffn_up, ffn_conv=ffn_conv, ffn_conv_b=ffn_conv_b, ffn_down=ffn_down)
    m = dict(mix_norm=m_mix_norm, w_in=m_w_in, q_norm=m_q_norm, k_norm=m_k_norm, attn_sinks=m_attn_sinks, gmlp_v_norm=m_gmlp_v_norm, gmlp_ws=m_gmlp_ws, gmlp_bs=m_gmlp_bs, attn_out_norm=m_attn_out_norm, gmlp_out_norm=m_gmlp_out_norm, w_out=m_w_out, xa_norm=m_xa_norm, mem_norm=m_mem_norm, xa_wq=m_xa_wq, xa_wkv=m_xa_wkv, xa_q_norm=m_xa_q_norm, xa_k_norm=m_xa_k_norm, xa_wo=m_xa_wo, ffn_norm=m_ffn_norm, ffn_up=m_ffn_up, ffn_conv=m_ffn_conv, ffn_conv_b=m_ffn_conv_b, ffn_down=m_ffn_down)
    v = dict(mix_norm=v_mix_norm, w_in=v_w_in, q_norm=v_q_norm, k_norm=v_k_norm, attn_sinks=v_attn_sinks, gmlp_v_norm=v_gmlp_v_norm, gmlp_ws=v_gmlp_ws, gmlp_bs=v_gmlp_bs, attn_out_norm=v_attn_out_norm, gmlp_out_norm=v_gmlp_out_norm, w_out=v_w_out, xa_norm=v_xa_norm, mem_norm=v_mem_norm, xa_wq=v_xa_wq, xa_wkv=v_xa_wkv, xa_q_norm=v_xa_q_norm, xa_k_norm=v_xa_k_norm, xa_wo=v_xa_wo, ffn_norm=v_ffn_norm, ffn_up=v_ffn_up, ffn_conv=v_ffn_conv, ffn_conv_b=v_ffn_conv_b, ffn_down=v_ffn_down)
    ix, iy, ic = lax.axis_index("x"), lax.axis_index("y"), lax.axis_index("c")
    chip = 2 * ix + iy
    chipvec = chip.astype(jnp.int32).reshape(1)
    cvec = ic.astype(jnp.int32).reshape(1)
    order = jnp.stack([chip, ic] + [4 * px + 2 * py + pc for px, py, pc in _peers(ix, iy, ic)]).astype(jnp.int32)

    w_in_all, later, token = _gather_step(w, chipvec)
    zero = token[0, 0]
    sp = {n: w[n][0] + zero for n in SMALL if n != "ffn_conv"}
    positions = positions + zero.astype(jnp.int32)
    started = {}

    def emit(name, g):
        *started[name], token = partials_start(g, name="partials_start_" + name)
        return token

    loss_acc, grad_x, packed = _local_step(x[0], mem[0], positions[0], loss_target[0], w_in_all, later, sp, emit)
    grads, delta, new_m, new_v = _reduce_update(started, packed, w, m, v, chipvec, cvec, order)
    loss = lax.psum(loss_acc[0, 0], ("x", "y", "c"))
    ordered = lambda d: [d[n] for n in WEIGHTS]
    return (loss, grad_x[None], *ordered(grads), *ordered(delta), *ordered(new_m), *ordered(new_v))
```

```python
import math

import jax
import jax.numpy as jnp
from jax import lax
from jax.experimental import pallas as pl
from jax.experimental.pallas import tpu as pltpu

F32 = jnp.float32
BF16 = jnp.bfloat16
MXU_DTYPE = jnp.bfloat16
WIRE_DTYPE = jnp.bfloat16
EPS = 1e-6
VMEM_LIMIT_V7X = 56 * 1024 * 1024

D_MODEL = 1024
HEAD_DIM = 64
BLK = 128
XA_HEADS = 4
XA_DH = 256
MEM_LEN = 256
D_FF = 2816
IN_COLS_DUP = 2048
N_CHIPS = 4
N_DEV = 8

ADAM_LR = 0.001
ADAM_B1 = 0.9
ADAM_B2 = 0.999
ADAM_EPS = 1e-08
ADAM_WD = 0.01
ADAM_STEP = 10

NT = (((1,), (1,)), ((), ()))
TN = (((0,), (0,)), ((), ()))
NN = (((1,), (0,)), ((), ()))
MINF = float(jnp.finfo(jnp.float32).min)
GELU_K0 = math.sqrt(2.0 / math.pi)
GELU_K1 = 0.044715

BS = pl.BlockSpec
SDS = jax.ShapeDtypeStruct
ANY = pl.BlockSpec(memory_space=pl.ANY)
MESH = pl.DeviceIdType.MESH


def _dot(a, b, dims=NN):
    return lax.dot_general(a.astype(MXU_DTYPE), b.astype(MXU_DTYPE), dims, preferred_element_type=F32)


def _segsum(x, bmat):
    hi = x.astype(BF16)
    lo = (x - hi.astype(F32)).astype(BF16)
    return (jnp.dot(hi, bmat, preferred_element_type=F32) + jnp.dot(lo, bmat, preferred_element_type=F32))


def _gelu(x):
    return 0.5 * x * (1.0 + jnp.tanh(GELU_K0 * (x + GELU_K1 * x * x * x)))


def _gelu_grad(x):
    t = jnp.tanh(GELU_K0 * (x + GELU_K1 * x * x * x))
    return 0.5 * (1.0 + t) + 0.5 * x * (1.0 - t * t) * GELU_K0 * (1.0 + 3.0 * GELU_K1 * x * x)


def _gelu_and_grad(x):
    x2 = x * x
    t = jnp.tanh(x * (GELU_K0 * GELU_K1 * x2 + GELU_K0))
    hx = 0.5 * x
    return hx * t + hx, 0.5 * t + 0.5 + hx * (1.0 - t * t) * (3.0 * GELU_K0 * GELU_K1 * x2 + GELU_K0)


def _rms(x):
    return lax.rsqrt(jnp.mean(x * x, axis=-1, keepdims=True) + EPS)


def _rms_bwd(dy, x, g, r):
    dyg = dy * g
    dx = r * dyg - x * (r * r * r) * jnp.mean(dyg * x, axis=-1, keepdims=True)
    return dx, dy * x * r


def _pcall(body, *, name, grid, in_specs, out_specs, out_shape, scratch=(), prefetch=0, after=None):
    params = pltpu.CompilerParams(dimension_semantics=("arbitrary",) * len(grid), vmem_limit_bytes=VMEM_LIMIT_V7X)
    in_specs = list(in_specs)
    kernel_fn = body
    if after is not None:
        n_in = prefetch + len(in_specs)
        in_specs.append(ANY)

        def kernel_fn(*refs):
            return body(*refs[:n_in], *refs[n_in + 1:])

    if prefetch:
        spec = pltpu.PrefetchScalarGridSpec(num_scalar_prefetch=prefetch, grid=grid, in_specs=in_specs,
                                            out_specs=out_specs, scratch_shapes=list(scratch))
        call = pl.pallas_call(kernel_fn, name=name, grid_spec=spec, out_shape=out_shape, compiler_params=params)
    else:
        call = pl.pallas_call(kernel_fn, name=name, grid=grid, in_specs=in_specs, out_specs=out_specs,
                              out_shape=out_shape, scratch_shapes=list(scratch), compiler_params=params)
    return call if after is None else (lambda *args: call(*args, after))


def _tile(n, prefs):
    for p in prefs:
        if p <= n and n % p == 0:
            return p
    return n


def _resident(shape):
    return pl.BlockSpec(shape, lambda *_: (0,) * len(shape), pipeline_mode=pl.Buffered(1))


def _acc_rows(ref, row, val):
    ref[row:row + 1, :] += jnp.sum(val, axis=0, keepdims=True)


def rms_mm(x, g, w3, *, name, tm=1024):
    M, K = x.shape
    Q, _, C = w3.shape
    tm = _tile(M, (tm, 256))

    def body(x_ref, g_ref, w_ref, h_ref, o_ref):
        def write_h():
            xv = x_ref[...]
            h_ref[...] = (xv * _rms(xv) * g_ref[...]).astype(h_ref.dtype)

        if Q == 1:
            write_h()
        else:
            pl.when(pl.program_id(1) == 0)(write_h)
        o_ref[...] = _dot(h_ref[...], w_ref[pl.program_id(1)])

    return _pcall(body, name=name, grid=(M // tm, Q),
                  in_specs=[BS((tm, K), lambda i, j: (i, 0)), BS((1, K), lambda i, j: (0, 0)),
                            _resident((Q, K, C))],
                  out_specs=[BS((tm, K), lambda i, j: (i, 0)), BS((tm, C), lambda i, j: (i, j))],
                  out_shape=[SDS((M, K), MXU_DTYPE), SDS((M, Q * C), F32)])(x, g, w3)


def _nt_chunks(a_ref, w_ref):
    q_n, _, kc = w_ref.shape
    acc = _dot(a_ref[:, 0:kc], w_ref[0], NT)
    for q in range(1, q_n):
        acc = acc + _dot(a_ref[:, q * kc:(q + 1) * kc], w_ref[q], NT)
    return acc


def mm_nt_rms_bwd(a, w3, x, g, dres, *, name, tm=512, after=None):
    M = a.shape[0]
    Q, N, Kc = w3.shape
    tm = _tile(M, (tm, 256))

    def body(a_ref, w_ref, x_ref, g_ref, dr_ref, dx_ref, dg_ref):
        @pl.when(pl.program_id(0) == 0)
        def _():
            dg_ref[...] = jnp.zeros_like(dg_ref)

        xv = x_ref[...]
        dx, dgc = _rms_bwd(_nt_chunks(a_ref, w_ref), xv, g_ref[...], _rms(xv))
        dx_ref[...] = dr_ref[...] + dx
        _acc_rows(dg_ref, 0, dgc)

    row = BS((tm, N), lambda i: (i, 0))
    return _pcall(body, name=name, grid=(M // tm,), after=after,
                  in_specs=[BS((tm, Q * Kc), lambda i: (i, 0)), _resident((Q, N, Kc)), row,
                            BS((1, N), lambda i: (0, 0)), row],
                  out_specs=[row, BS((8, N), lambda i: (0, 0))],
                  out_shape=[SDS((M, N), F32), SDS((8, N), F32)])(a, w3, x, g, dres)


def mm_tn(a, b, *, name, out_dtype, chunks=1, after=None):
    M, K = a.shape
    N = b.shape[1]
    C = N // chunks
    tm = _tile(M, (1024, 256))
    tk = _tile(K, (1408, 1024, 512))
    tn = _tile(C, (1408, 1024, 512))
    per = C // tn
    nm = M // tm

    def body(a_ref, b_ref, o_ref, acc):
        m = pl.program_id(2)

        @pl.when(m == 0)
        def _():
            acc[...] = jnp.zeros_like(acc)

        acc[...] += _dot(a_ref[...], b_ref[...], TN)

        @pl.when(m == nm - 1)
        def _():
            o_ref[...] = acc[...].astype(o_ref.dtype)

    return _pcall(body, name=name, grid=(K // tk, N // tn, nm), after=after,
                  in_specs=[BS((tm, tk), lambda k, n, m: (m, k)), BS((tm, tn), lambda k, n, m: (m, n))],
                  out_specs=BS((None, tk, tn), lambda k, n, m: (n // per, k, n % per)),
                  out_shape=SDS((chunks, K, C), out_dtype), scratch=[pltpu.VMEM((tk, tn), F32)])(a, b)


def _lane(shape):
    return lax.broadcasted_iota(jnp.int32, shape, 1)


def _head_means(slabs, bmat):
    tm = slabs[0].shape[0]
    means = _segsum(jnp.concatenate(slabs, axis=0), bmat) * (1.0 / HEAD_DIM)
    return [means[i * tm:(i + 1) * tm] for i in range(len(slabs))]


def _half_swap(x, first):
    return jnp.where(first, pltpu.roll(x, 96, 1), pltpu.roll(x, 32, 1))


def _by_head(x2, lo):
    z = jnp.zeros((BLK, 128), x2.dtype)
    parts = []
    for s in range(2):
        xs = x2[:, s * 128:(s + 1) * 128]
        parts += [jnp.where(lo, xs, z), jnp.where(lo, z, xs)]
    return jnp.concatenate(parts, axis=0)


def _from_heads(o4, lo):
    return jnp.concatenate([jnp.where(lo, o4[0:BLK], o4[BLK:2 * BLK]),
                            jnp.where(lo, o4[2 * BLK:3 * BLK], o4[3 * BLK:])], axis=1)


def _swa_probs(q2, kd, sink, n, lo):
    qp = _by_head(q2, lo)
    sc = _dot(qp, kd, NT) * (1.0 / math.sqrt(HEAD_DIM))
    r_i = lax.broadcasted_iota(jnp.int32, (4 * BLK, 2 * BLK), 0)
    k_j = lax.broadcasted_iota(jnp.int32, (4 * BLK, 2 * BLK), 1)
    diff = (r_i & (BLK - 1)) + BLK - k_j
    mask = (diff >= 0) & (diff < BLK) & ((k_j >= BLK) | (n > 0))
    sc = jnp.where(mask, sc, MINF)
    m = jnp.maximum(jnp.max(sc, axis=1, keepdims=True), sink)
    p = jnp.exp(sc - m)
    es = jnp.exp(sink - m)
    inv = 1.0 / (_segsum(p, jnp.ones((2 * BLK, BLK), BF16)) + es)
    return qp, p * jnp.concatenate([inv, inv], axis=1), es * inv[:, :1]


def mixer_core_fwd(proj, cos, sin, gq, gk, gvn, bmat, sinkcol, gao, w2, bsl, ggo):
    S = proj.shape[0]
    sub = 4 if S % (4 * BLK) == 0 else 1

    def body(p_ref, c_ref, s_ref, gq_ref, gk_ref, gvn_ref, b_ref, sk_ref, gao_ref, w2_ref, bsl_ref, ggo_ref,
             qr_ref, kr_ref, vb_ref, gu_ref, gvo_ref, at_ref, gm_ref, y_ref, k_prev, v_prev):
        n = pl.program_id(0)

        @pl.when(n == 0)
        def _():
            k_prev[...] = jnp.zeros_like(k_prev)
            v_prev[...] = jnp.zeros_like(v_prev)

        bm = b_ref[...]
        first = (_lane((BLK, 128)) & 63) < 32
        lo = _lane((BLK, 128)) < 64
        for sb in range(sub):
            rs = slice(sb * BLK, (sb + 1) * BLK)
            cos_v, sin_v = c_ref[rs, :], s_ref[rs, :]
            slabs = [p_ref[rs, s * 128:(s + 1) * 128] for s in range(6)]
            for s, (slab, ms) in enumerate(zip(slabs, _head_means([x * x for x in slabs], bm))):
                qn = slab * lax.rsqrt(ms + EPS) * (gq_ref[...] if s < 4 else gk_ref[...])
                out = qn * cos_v + _half_swap(qn, first) * sin_v
                if s < 4:
                    qr_ref[rs, s * 128:(s + 1) * 128] = out.astype(qr_ref.dtype)
                else:
                    kr_ref[rs, (s - 4) * 128:(s - 3) * 128] = out.astype(kr_ref.dtype)
            vb_ref[rs, :] = p_ref[rs, 768:1024].astype(vb_ref.dtype)
            gu_ref[rs, :] = _gelu(p_ref[rs, 1024:1536])
            gv = _gelu(p_ref[rs, 1536:2048])
            gvo_ref[rs, :] = (gv * _rms(gv) * gvn_ref[...]).astype(gvo_ref.dtype)

            before = slice((sb - 1) * BLK, sb * BLK)
            for h in range(2):
                hs, qs = slice(h * 128, (h + 1) * 128), slice(h * 256, (h + 1) * 256)
                k_before = k_prev[:, hs] if sb == 0 else kr_ref[before, hs]
                v_before = v_prev[:, hs] if sb == 0 else vb_ref[before, hs]
                kd = jnp.concatenate([k_before, kr_ref[rs, hs]], axis=0)
                vd = jnp.concatenate([v_before, vb_ref[rs, hs]], axis=0)
                sink = jnp.concatenate([sk_ref[2 * h], sk_ref[2 * h + 1]], axis=0)
                _, p, _ = _swa_probs(qr_ref[rs, qs], kd, sink, n * sub + sb, lo)
                at_ref[rs, qs] = _from_heads(_dot(p, vd), lo)

            for j in range(4):
                sl = slice(j * 128, (j + 1) * 128)
                m2 = _dot(w2_ref[j], gvo_ref[rs, sl])
                mixed = jnp.where(lo, m2[:BLK], m2[BLK:]) + bsl_ref[j]
                gm_ref[rs, sl] = gu_ref[rs, sl] * mixed
            a, gm = at_ref[rs, :], gm_ref[rs, :]
            y_ref[rs, :512] = (a * _rms(a) * gao_ref[...]).astype(y_ref.dtype)
            y_ref[rs, 512:] = (gm * _rms(gm) * ggo_ref[...]).astype(y_ref.dtype)
        k_prev[...] = kr_ref[(sub - 1) * BLK:, :]
        v_prev[...] = vb_ref[(sub - 1) * BLK:, :]

    row = lambda w: BS((sub * BLK, w), lambda n: (n, 0))
    const = lambda *shape: BS(shape, lambda n: (0,) * len(shape))
    return _pcall(body, name="mixer_core_fwd", grid=(S // (sub * BLK),),
                  in_specs=[row(IN_COLS_DUP), row(128), row(128), const(1, 128), const(1, 128), const(1, 512),
                            const(128, 128), const(4, 2 * BLK, 1), const(1, 512), const(4, 2 * BLK, BLK),
                            const(4, BLK, 128), const(1, 512)],
                  out_specs=[row(512), row(256), row(256), row(512), row(512), row(512), row(512), row(1024)],
                  out_shape=[SDS((S, 512), MXU_DTYPE), SDS((S, 256), MXU_DTYPE), SDS((S, 256), MXU_DTYPE),
                             SDS((S, 512), F32), SDS((S, 512), MXU_DTYPE), SDS((S, 512), F32), SDS((S, 512), F32),
                             SDS((S, 1024), MXU_DTYPE)],
                  scratch=[pltpu.VMEM((BLK, 256), MXU_DTYPE), pltpu.VMEM((BLK, 256), MXU_DTYPE)])(
        proj, cos, sin, gq, gk, gvn, bmat, sinkcol, gao, w2, bsl, ggo)


def mem_pre(kv, gxk):
    def body(kv_ref, g_ref, kn_ref, vb_ref):
        for h in range(XA_HEADS):
            sl = slice(h * XA_DH, (h + 1) * XA_DH)
            k = kv_ref[:, sl]
            kn_ref[:, sl] = (k * _rms(k) * g_ref[...]).astype(kn_ref.dtype)
        vb_ref[...] = kv_ref[:, 1024:2048].astype(vb_ref.dtype)

    full = lambda r, w: BS((r, w), lambda i: (0, 0))
    return _pcall(body, name="mem_pre", grid=(1,), in_specs=[full(MEM_LEN, 2048), full(1, XA_DH)],
                  out_specs=[full(MEM_LEN, 1024), full(MEM_LEN, 1024)],
                  out_shape=[SDS((MEM_LEN, 1024), MXU_DTYPE), SDS((MEM_LEN, 1024), MXU_DTYPE)])(kv, gxk)


def _xa_probs(qh, g, kn_h):
    r = _rms(qh)
    qn = qh * r * g
    s = _dot(qn, kn_h, NT) * (1.0 / math.sqrt(XA_DH))
    p = jnp.exp(s - jnp.max(s, axis=1, keepdims=True))
    return r, qn, p * (1.0 / jnp.sum(p, axis=1, keepdims=True))


def xattn_block_fwd(y, w_out, x, g, wq, kn, vb, gxq, wo):
    S, D = x.shape
    tm = _tile(S, (512, 256))

    def body(y_ref, wout_ref, x_ref, g_ref, wq_ref, kn_ref, vb_ref, gxq_ref, wo_ref, x1_ref, h_ref, q_ref, o_ref,
             x2_ref):
        x1_ref[...] = _dot(y_ref[...], wout_ref[...]) + x_ref[...]
        xv = x1_ref[...]
        h_ref[...] = (xv * _rms(xv) * g_ref[...]).astype(h_ref.dtype)
        q_ref[...] = _dot(h_ref[...], wq_ref[...])
        for h in range(XA_HEADS):
            sl = slice(h * XA_DH, (h + 1) * XA_DH)
            _, _, p = _xa_probs(q_ref[:, sl], gxq_ref[...], kn_ref[:, sl])
            o_ref[:, sl] = _dot(p, vb_ref[:, sl]).astype(o_ref.dtype)
        x2_ref[...] = _dot(o_ref[...], wo_ref[...]) + x1_ref[...]

    row = BS((tm, D), lambda i: (i, 0))
    full = lambda r, w: BS((r, w), lambda i: (0, 0))
    return _pcall(body, name="xattn_block_fwd", grid=(S // tm,),
                  in_specs=[BS((tm, y.shape[1]), lambda i: (i, 0)), _resident(w_out.shape), row, full(1, D),
                            _resident(wq.shape), full(MEM_LEN, D), full(MEM_LEN, D), full(1, XA_DH),
                            _resident(wo.shape)],
                  out_specs=[row, row, row, row, row],
                  out_shape=[SDS((S, D), F32), SDS((S, D), MXU_DTYPE), SDS((S, D), F32), SDS((S, D), MXU_DTYPE),
                             SDS((S, D), F32)])(y, w_out, x, g, wq, kn, vb, gxq, wo)


CONV_COLS = 1408


def _conv_taps(a_ref, halo_ref, w_ref, b_ref, cols, first_tile):
    a = a_ref[:, cols]
    row = lax.broadcasted_iota(jnp.int32, (8, a.shape[1]), 0)
    h6 = jnp.where(first_tile, 0.0, halo_ref[6:7, cols])
    h7 = jnp.where(first_tile, 0.0, halo_ref[7:8, cols])
    r1, r2 = pltpu.roll(a, 1, 0), pltpu.roll(a, 2, 0)
    a1 = jnp.concatenate([jnp.where(row == 0, h7, r1[0:8]), r1[8:]], axis=0)
    a2 = jnp.concatenate([jnp.where(row == 0, h6, jnp.where(row == 1, h7, r2[0:8])), r2[8:]], axis=0)
    c = w_ref[2:3, cols] * a + w_ref[1:2, cols] * a1 + w_ref[0:1, cols] * a2 + b_ref[:, cols]
    return c, (a2, a1, a)


def _conv_specs(tm):
    halo_blocks = tm // 8
    return [BS((tm, D_FF), lambda i: (i, 0)), BS((tm, D_FF), lambda i: (i, 1)),
            BS((8, D_FF), lambda i: (jnp.maximum(i * halo_blocks - 1, 0), 0)),
            BS((8, D_FF), lambda i: (jnp.maximum(i * halo_blocks - 1, 0), 1)),
            BS((3, D_FF), lambda i: (0, 0)), BS((3, D_FF), lambda i: (0, 1)),
            BS((1, D_FF), lambda i: (0, 0)), BS((1, D_FF), lambda i: (0, 1))]


def ffn_fwd_loss(x2, g, w_up3, cw, cb, w_down, target):
    S, D = x2.shape
    Q, _, C = w_up3.shape
    tm = _tile(S, (256,))

    def body(x_ref, g_ref, wu_ref, cw_ref, cb_ref, wd_ref, t_ref, h_ref, a_ref, f_ref, d_ref, l_ref, tail):
        first_tile = pl.program_id(0) == 0

        @pl.when(first_tile)
        def _():
            l_ref[...] = jnp.zeros_like(l_ref)
            tail[...] = jnp.zeros_like(tail)

        xv = x_ref[...]
        h_ref[...] = (xv * _rms(xv) * g_ref[...]).astype(h_ref.dtype)
        for q in range(Q):
            a_ref[:, q * C:(q + 1) * C] = _dot(h_ref[...], wu_ref[q])
        for c0 in range(0, D_FF, CONV_COLS):
            cols, ucols = slice(c0, c0 + CONV_COLS), slice(D_FF + c0, D_FF + c0 + CONV_COLS)
            cg, _ = _conv_taps(a_ref, tail, cw_ref, cb_ref, cols, first_tile)
            cu, _ = _conv_taps(a_ref, tail, cw_ref, cb_ref, ucols, first_tile)
            f_ref[:, cols] = (_gelu(cg) * cu).astype(f_ref.dtype)
        tail[...] = a_ref[tm - 8:tm, :]
        e = _dot(f_ref[...], wd_ref[...]) + xv - t_ref[...]
        d_ref[...] = e * (1.0 / D)
        l_ref[...] += jnp.sum(e * e) * (0.5 / D)

    row = lambda w: BS((tm, w), lambda i: (i, 0))
    const = lambda r, w: BS((r, w), lambda i: (0, 0))
    return _pcall(body, name="ffn_fwd_loss", grid=(S // tm,),
                  in_specs=[row(D), const(1, D), _resident(w_up3.shape), const(3, 2 * D_FF), const(1, 2 * D_FF),
                            _resident(w_down.shape), row(D)],
                  out_specs=[row(D), row(2 * D_FF), row(D_FF), row(D), const(8, 128)],
                  out_shape=[SDS((S, D), MXU_DTYPE), SDS((S, 2 * D_FF), F32), SDS((S, D_FF), MXU_DTYPE),
                             SDS((S, D), F32), SDS((8, 128), F32)],
                  scratch=[pltpu.VMEM((8, 2 * D_FF), F32)])(x2, g, w_up3, cw, cb, w_down, target)


def convgate_bwd(a, dx3, w3, cw, cb, after=None):
    S = a.shape[0]
    tm = _tile(S, (256,))

    def body(ag_ref, au_ref, hg_ref, hu_ref, wg_ref, wu_ref, bg_ref, bu_ref, dx_ref, wd_ref, dc_ref, gw_ref, df_ref):
        first_tile = pl.program_id(0) == 0

        @pl.when(first_tile)
        def _():
            gw_ref[...] = jnp.zeros_like(gw_ref)

        df_ref[...] = _nt_chunks(dx_ref, wd_ref)
        for c0 in range(0, D_FF, CONV_COLS):
            cols, ucols = slice(c0, c0 + CONV_COLS), slice(D_FF + c0, D_FF + c0 + CONV_COLS)
            cg, g_taps = _conv_taps(ag_ref, hg_ref, wg_ref, bg_ref, cols, first_tile)
            cu, u_taps = _conv_taps(au_ref, hu_ref, wu_ref, bu_ref, cols, first_tile)
            df_v = df_ref[:, cols]
            gate, gate_grad = _gelu_and_grad(cg)
            dcg = df_v * cu * gate_grad
            dcu = df_v * gate
            dc_ref[:, cols] = dcg
            dc_ref[:, ucols] = dcu
            for col, dcv, taps in ((cols, dcg, g_taps), (ucols, dcu, u_taps)):
                for j in range(3):
                    gw_ref[j:j + 1, col] += jnp.sum(dcv * taps[j], axis=0, keepdims=True)
                gw_ref[3:4, col] += jnp.sum(dcv, axis=0, keepdims=True)

    return _pcall(body, name="convgate_bwd", grid=(S // tm,), after=after,
                  in_specs=_conv_specs(tm) + [BS((tm, dx3.shape[1]), lambda i: (i, 0)), _resident(w3.shape)],
                  out_specs=[BS((tm, 2 * D_FF), lambda i: (i, 0)), BS((8, 2 * D_FF), lambda i: (0, 0))],
                  out_shape=[SDS((S, 2 * D_FF), F32), SDS((8, 2 * D_FF), F32)],
                  scratch=[pltpu.VMEM((tm, D_FF), F32)])(a, a, a, a, cw, cw, cb, cb, dx3, w3)


def conv_transpose_rms_bwd(dc, cw, w3, x, g, dres):
    S, C = dc.shape
    Q, N, Kc = w3.shape
    tm = _tile(S, (256,))
    nt = S // tm
    halo_blocks = tm // 8

    def body(dc_ref, halo_ref, cw_ref, w_ref, x_ref, g_ref, dr_ref, da_ref, dx_ref, dg_ref):
        @pl.when(pl.program_id(0) == 0)
        def _():
            dg_ref[...] = jnp.zeros_like(dg_ref)

        last_tile = pl.program_id(0) == nt - 1
        row = lax.broadcasted_iota(jnp.int32, (8, CONV_COLS), 0)
        for c0 in range(0, C, CONV_COLS):
            cols = slice(c0, c0 + CONV_COLS)
            h0 = jnp.where(last_tile, 0.0, halo_ref[0:1, cols])
            h1 = jnp.where(last_tile, 0.0, halo_ref[1:2, cols])
            dc_v = dc_ref[:, cols]
            r1, r2 = pltpu.roll(dc_v, tm - 1, 0), pltpu.roll(dc_v, tm - 2, 0)
            n1 = jnp.concatenate([r1[:tm - 8], jnp.where(row == 7, h0, r1[tm - 8:])], axis=0)
            n2 = jnp.concatenate([r2[:tm - 8], jnp.where(row == 7, h1, jnp.where(row == 6, h0, r2[tm - 8:]))], axis=0)
            da_ref[:, cols] = (cw_ref[2:3, cols] * dc_v + cw_ref[1:2, cols] * n1
                               + cw_ref[0:1, cols] * n2).astype(da_ref.dtype)
        xv = x_ref[...]
        dx, dgc = _rms_bwd(_nt_chunks(da_ref, w_ref), xv, g_ref[...], _rms(xv))
        dx_ref[...] = dr_ref[...] + dx
        _acc_rows(dg_ref, 0, dgc)

    row_n = BS((tm, N), lambda i: (i, 0))
    return _pcall(body, name="conv_transpose_rms_bwd", grid=(nt,),
                  in_specs=[BS((tm, C), lambda i: (i, 0)),
                            BS((8, C), lambda i: (jnp.minimum((i + 1) * halo_blocks, S // 8 - 1), 0)),
                            BS((3, C), lambda i: (0, 0)), _resident((Q, N, Kc)), row_n, BS((1, N), lambda i: (0, 0)),
                            row_n],
                  out_specs=[BS((tm, C), lambda i: (i, 0)), row_n, BS((8, N), lambda i: (0, 0))],
                  out_shape=[SDS((S, C), MXU_DTYPE), SDS((S, N), F32), SDS((8, N), F32)])(dc, dc, cw, w3, x, g, dres)


def xattn_block_bwd(dx2, wo3, qx, kn, vb, gxq, wq3, x1, g, wout3, attn, gm, gao, ggo, after=None):
    S, D = qx.shape
    tm = _tile(S, (512, 256))
    hw = D // 2

    def body(dx2_ref, wo_ref, q_ref, kn_ref, vb_ref, gxq_ref, wq_ref, x_ref, g_ref, wout_ref, at_ref, gm_ref,
             gao_ref, ggo_ref, dq_ref, dx_ref, dkn_ref, dv_ref, dgq_ref, dg_ref, da_ref, dgm_ref, dgy_ref):
        @pl.when(pl.program_id(0) == 0)
        def _():
            for ref in (dkn_ref, dv_ref, dgq_ref, dg_ref, dgy_ref):
                ref[...] = jnp.zeros_like(ref)

        gq = gxq_ref[...]
        do_all = _nt_chunks(dx2_ref, wo_ref)
        for h in range(XA_HEADS):
            sl = slice(h * XA_DH, (h + 1) * XA_DH)
            qh, do = q_ref[:, sl], do_all[:, sl]
            r, qn, p = _xa_probs(qh, gq, kn_ref[:, sl])
            dp = _dot(do, vb_ref[:, sl], NT)
            ds = p * (dp - jnp.sum(dp * p, axis=1, keepdims=True)) * (1.0 / math.sqrt(XA_DH))
            dqn = _dot(ds, kn_ref[:, sl])
            dkn_ref[:, sl] += _dot(ds, qn, TN)
            dv_ref[:, sl] += _dot(p, do, TN)
            dqh, dgc = _rms_bwd(dqn, qh, gq, r)
            dq_ref[:, sl] = dqh.astype(dq_ref.dtype)
            _acc_rows(dgq_ref, 0, dgc)
        xv = x_ref[...]
        dx, dgc = _rms_bwd(_nt_chunks(dq_ref, wq_ref), xv, g_ref[...], _rms(xv))
        dx1 = dx2_ref[...] + dx
        dx_ref[...] = dx1
        _acc_rows(dg_ref, 0, dgc)
        dy = _dot(dx1, wout_ref[0], NT)
        av, gmv = at_ref[...], gm_ref[...]
        da, dga = _rms_bwd(dy[:, :hw], av, gao_ref[...], _rms(av))
        dgm, dgg = _rms_bwd(dy[:, hw:], gmv, ggo_ref[...], _rms(gmv))
        da_ref[...] = da
        dgm_ref[...] = dgm
        dgy_ref[0:1, :hw] += jnp.sum(dga, axis=0, keepdims=True)
        dgy_ref[0:1, hw:] += jnp.sum(dgg, axis=0, keepdims=True)

    row = BS((tm, D), lambda i: (i, 0))
    half = BS((tm, hw), lambda i: (i, 0))
    full = lambda r, w: BS((r, w), lambda i: (0, 0))
    return _pcall(body, name="xattn_block_bwd", grid=(S // tm,), after=after,
                  in_specs=[row, _resident(wo3.shape), row, full(MEM_LEN, D), full(MEM_LEN, D), full(1, XA_DH),
                            _resident(wq3.shape), row, full(1, D), _resident(wout3.shape), half, half, full(1, hw),
                            full(1, hw)],
                  out_specs=[row, row, full(MEM_LEN, D), full(MEM_LEN, D), full(8, XA_DH), full(8, D), half, half,
                             full(8, D)],
                  out_shape=[SDS((S, D), MXU_DTYPE), SDS((S, D), F32), SDS((MEM_LEN, D), F32), SDS((MEM_LEN, D), F32),
                             SDS((8, XA_DH), F32), SDS((8, D), F32), SDS((S, hw), F32), SDS((S, hw), F32),
                             SDS((8, D), F32)])(dx2, wo3, qx, kn, vb, gxq, wq3, x1, g, wout3, attn, gm, gao, ggo)


def mem_bwd(kv, dkn, dvb, gxk, after=None):
    def body(kv_ref, dkn_ref, dv_ref, g_ref, dkv_ref, dg_ref):
        dg_ref[...] = jnp.zeros_like(dg_ref)
        for h in range(XA_HEADS):
            sl = slice(h * XA_DH, (h + 1) * XA_DH)
            k = kv_ref[:, sl]
            dk, dgc = _rms_bwd(dkn_ref[:, sl], k, g_ref[...], _rms(k))
            dkv_ref[:, sl] = dk.astype(dkv_ref.dtype)
            _acc_rows(dg_ref, 0, dgc)
        dkv_ref[:, 1024:2048] = dv_ref[...].astype(dkv_ref.dtype)

    full = lambda r, w: BS((r, w), lambda i: (0, 0))
    return _pcall(body, name="mem_bwd", grid=(1,), after=after,
                  in_specs=[full(MEM_LEN, 2048), full(MEM_LEN, 1024), full(MEM_LEN, 1024), full(1, XA_DH)],
                  out_specs=[full(MEM_LEN, 2048), full(8, XA_DH)],
                  out_shape=[SDS((MEM_LEN, 2048), MXU_DTYPE), SDS((8, XA_DH), F32)])(kv, dkn, dvb, gxk)


def _norm_rope_bwd(slabs, douts, g, bm, cos_v, sin_v, first):
    dqns = [d * cos_v + _half_swap(d * sin_v, first) for d in douts]
    rs = [lax.rsqrt(ms + EPS) for ms in _head_means([x * x for x in slabs], bm)]
    projs = _head_means([dqn * g * x for dqn, x in zip(dqns, slabs)], bm)
    dxs = [r * (dqn * g) - x * (r * r * r) * pr for x, dqn, r, pr in zip(slabs, dqns, rs, projs)]
    return dxs, [dqn * x * r for x, dqn, r in zip(slabs, dqns, rs)]


def mixer_core_bwd(proj, cos, sin, gq, gk, gvg, bmat, qr, kr, vb, sinkcol, dattn, dgm, gvn, gu, w2, w2t, bsl,
                   after=None):
    S = qr.shape[0]
    nb = S // BLK

    def body(p_ref, c_ref, s_ref, gq_ref, gk_ref, gvg_ref, b_ref, q_ref, kc_ref, kp_ref, vc_ref, vp_ref, sk_ref,
             do_ref, dgm_ref, gvn_ref, gu_ref, w2_ref, w2t_ref, bsl_ref,
             dp_ref, dsk_ref, dws_ref, dbl_ref, dgq_ref, dgk_ref, dgv_ref,
             carry_k, carry_v, done_k, done_v, dq_keep, dgu_keep, dgvn_keep):
        n = pl.program_id(0)

        @pl.when(n == 0)
        def _():
            for ref in (dsk_ref, dws_ref, dbl_ref, dgq_ref, dgk_ref, dgv_ref, carry_k, carry_v, dq_keep, dgu_keep,
                        dgvn_keep):
                ref[...] = jnp.zeros_like(ref)

        live = (n < nb).astype(F32)
        cos_v, sin_v, bm = c_ref[...], s_ref[...], b_ref[...]
        first = (_lane((BLK, 128)) & 63) < 32
        lo = _lane((BLK, 128)) < 64

        dxs, dgs = _norm_rope_bwd([p_ref[:, s * 128:(s + 1) * 128] for s in range(4)],
                                  [dq_keep[:, s * 128:(s + 1) * 128] for s in range(4)], gq_ref[...], bm,
                                  cos_v, sin_v, first)
        for s, (dx, dg) in enumerate(zip(dxs, dgs)):
            dp_ref[:, s * 128:(s + 1) * 128] = dx.astype(dp_ref.dtype)
            _acc_rows(dgq_ref, 0, dg)
        dp_ref[:, 1024:1536] = (dgu_keep[...] * _gelu_grad(p_ref[:, 1024:1536])).astype(dp_ref.dtype)
        gv, gv_grad = _gelu_and_grad(p_ref[:, 1536:2048])
        dgv, dgc = _rms_bwd(dgvn_keep[...], gv, gvg_ref[...], _rms(gv))
        dp_ref[:, 1536:2048] = (dgv * gv_grad).astype(dp_ref.dtype)
        _acc_rows(dgv_ref, 0, dgc)

        for h in range(2):
            hs, qs = slice(h * 128, (h + 1) * 128), slice(h * 256, (h + 1) * 256)
            kd = jnp.concatenate([kp_ref[:, hs], kc_ref[:, hs]], axis=0)
            vd = jnp.concatenate([vp_ref[:, hs], vc_ref[:, hs]], axis=0)
            sink = jnp.concatenate([sk_ref[2 * h], sk_ref[2 * h + 1]], axis=0)
            qp, p, psink = _swa_probs(q_ref[:, qs], kd, sink, n, lo)
            dop = _by_head(do_ref[:, qs], lo)
            dp = _dot(dop, vd, NT)
            delta = _segsum(dp * p, jnp.ones((2 * BLK, BLK), BF16))
            ds = p * (dp - jnp.concatenate([delta, delta], axis=1)) * (1.0 / math.sqrt(HEAD_DIM))
            dsink = -psink * delta[:, :1] * live
            dsk_ref[2 * h] += dsink[:2 * BLK]
            dsk_ref[2 * h + 1] += dsink[2 * BLK:]
            dq_keep[:, qs] = _from_heads(_dot(ds, kd), lo)
            dkd = _dot(ds, qp, TN)
            dvd = _dot(p, dop, TN)
            done_k[:, hs] = carry_k[:, hs] + live * dkd[:BLK]
            done_v[:, hs] = carry_v[:, hs] + live * dvd[:BLK]
            carry_k[:, hs] = dkd[BLK:]
            carry_v[:, hs] = dvd[BLK:]
        for j in range(4):
            sl = slice(j * 128, (j + 1) * 128)
            gvn_s = gvn_ref[:, sl]
            m2 = _dot(w2_ref[j], gvn_s)
            mixed = jnp.where(lo, m2[:BLK], m2[BLK:]) + bsl_ref[j]
            dgm_s = dgm_ref[:, sl]
            dgu_keep[:, sl] = dgm_s * mixed
            dmx = dgm_s * gu_ref[:, sl] * live
            d2 = _dot(w2t_ref[j], dmx)
            dgvn_keep[:, sl] = jnp.where(lo, d2[:BLK], d2[BLK:])
            z = jnp.zeros_like(dmx)
            dws_ref[2 * j] += _dot(jnp.where(lo, dmx, z), gvn_s, NT)
            dws_ref[2 * j + 1] += _dot(jnp.where(lo, z, dmx), gvn_s, NT)
            dbl_ref[j] += dmx

        dxs, dgs = _norm_rope_bwd([p_ref[:, 512 + s * 128:640 + s * 128] for s in range(2)],
                                  [done_k[:, s * 128:(s + 1) * 128] for s in range(2)], gk_ref[...], bm,
                                  cos_v, sin_v, first)
        for s, (dx, dg) in enumerate(zip(dxs, dgs)):
            dp_ref[:, 512 + s * 128:640 + s * 128] = dx.astype(dp_ref.dtype)
            _acc_rows(dgk_ref, 0, dg)
        dp_ref[:, 768:1024] = done_v[...].astype(dp_ref.dtype)

    last = nb - 1
    cur = lambda w: BS((BLK, w), lambda n: (jnp.minimum(n, last), 0))
    prev = lambda w: BS((BLK, w), lambda n: (jnp.clip(n - 1, 0, last), 0))
    done = lambda w: BS((BLK, w), lambda n: (jnp.maximum(n - 1, 0), 0))
    const = lambda *shape: BS(shape, lambda n: (0,) * len(shape))
    return _pcall(body, name="mixer_core_bwd", grid=(nb + 1,), after=after,
                  in_specs=[done(IN_COLS_DUP), done(128), done(128), const(1, 128), const(1, 128), const(1, 512),
                            const(128, 128), cur(512), cur(256), prev(256), cur(256), prev(256),
                            const(4, 2 * BLK, 1), cur(512), cur(512), cur(512), cur(512), const(4, 2 * BLK, BLK),
                            const(4, 2 * BLK, BLK), const(4, BLK, 128)],
                  out_specs=[done(IN_COLS_DUP), const(4, 2 * BLK, 1), const(8, BLK, BLK), const(4, BLK, 128),
                             const(8, 128), const(8, 128), const(8, 512)],
                  out_shape=[SDS((S, IN_COLS_DUP), MXU_DTYPE), SDS((4, 2 * BLK, 1), F32), SDS((8, BLK, BLK), F32),
                             SDS((4, BLK, 128), F32), SDS((8, 128), F32), SDS((8, 128), F32), SDS((8, 512), F32)],
                  scratch=[pltpu.VMEM((BLK, 256), F32)] * 4 + [pltpu.VMEM((BLK, 512), F32)] * 3)(
        proj, cos, sin, gq, gk, gvg, bmat, qr, kr, kr, vb, vb, sinkcol, dattn, dgm, gvn, gu, w2, w2t, bsl)


BIG = (("w_in", (1024, 448), True), ("w_out", (256, 1024), False), ("xa_wq", (256, 1024), False),
       ("xa_wkv", (1024, 512), True), ("xa_wo", (256, 1024), False), ("ffn_up", (1024, 1408), True),
       ("ffn_down", (704, 1024), False))
BIG_NAMES = tuple(n for n, _, _ in BIG)
SMALL_VECS = (("mix_norm", 1024), ("q_norm", 64), ("k_norm", 64), ("attn_sinks", 8), ("gmlp_v_norm", 512),
              ("attn_out_norm", 512), ("gmlp_out_norm", 512), ("xa_norm", 1024), ("mem_norm", 1024),
              ("xa_q_norm", 256), ("xa_k_norm", 256), ("ffn_norm", 1024), ("ffn_conv_b", 5632))
SMALL = tuple(n for n, _ in SMALL_VECS) + ("gmlp_bs", "gmlp_ws", "ffn_conv")
WEIGHTS = ("mix_norm", "w_in", "q_norm", "k_norm", "attn_sinks", "gmlp_v_norm", "gmlp_ws", "gmlp_bs",
           "attn_out_norm", "gmlp_out_norm", "w_out", "xa_norm", "mem_norm", "xa_wq", "xa_wkv", "xa_q_norm",
           "xa_k_norm", "xa_wo", "ffn_norm", "ffn_up", "ffn_conv", "ffn_conv_b", "ffn_down")
CONV_SHARD = (3, 1408)
CONV_LANE_ROWS = CONV_SHARD[1] // 128
CONV_CHIP_ROWS = 40


def _small_rows():
    rows, r = {}, 0
    for n, length in SMALL_VECS:
        rows[n] = r
        r += -(-length // 128)
    r += -r % 8
    rows["gmlp_bs"] = r
    r += 8
    rows["gmlp_ws"] = r
    r += 8 * BLK
    rows["ffn_conv"] = r
    r += N_CHIPS * CONV_CHIP_ROWS
    return rows, r


SMALL_ROW, SMALL_ROWS = _small_rows()


def pack_small(dg_mix, dgq, dgk, dsk, dg_gvn, dg_y, dg_xa, dg_mem, dg_xq, dg_xk, dg_ffn, gcw, dbl, dws):
    def body(mix_ref, q_ref, k_ref, sk_ref, gvn_ref, y_ref, xa_ref, mem_ref, xq_ref, xk_ref, ffn_ref, cw_ref,
             dbl_ref, dws_ref, o_ref):
        o_ref[...] = jnp.zeros_like(o_ref)
        lane = _lane((1, 128))

        def put(name, src_ref, row, lane0, length):
            for k in range(length // 128):
                o_ref[SMALL_ROW[name] + k:SMALL_ROW[name] + k + 1, :] = src_ref[row:row + 1, lane0 + k * 128:lane0 + (k + 1) * 128]

        put("mix_norm", mix_ref, 0, 0, 1024)
        for name, ref in (("q_norm", q_ref), ("k_norm", k_ref)):
            v = ref[0:1, :]
            o_ref[SMALL_ROW[name]:SMALL_ROW[name] + 1, :] = jnp.where(lane < HEAD_DIM, v + pltpu.roll(v, 64, 1), 0.0)
        sinks = jnp.zeros((1, 128), F32)
        for s in range(4):
            col = sk_ref[s]
            sinks = sinks + jnp.where(lane == 2 * s, jnp.sum(col[:BLK]), 0.0) + jnp.where(lane == 2 * s + 1, jnp.sum(col[BLK:]), 0.0)
        o_ref[SMALL_ROW["attn_sinks"]:SMALL_ROW["attn_sinks"] + 1, :] = sinks
        put("gmlp_v_norm", gvn_ref, 0, 0, 512)
        put("attn_out_norm", y_ref, 0, 0, 512)
        put("gmlp_out_norm", y_ref, 0, 512, 512)
        put("xa_norm", xa_ref, 0, 0, 1024)
        put("mem_norm", mem_ref, 0, 0, 1024)
        put("xa_q_norm", xq_ref, 0, 0, 256)
        put("xa_k_norm", xk_ref, 0, 0, 256)
        put("ffn_norm", ffn_ref, 0, 0, 1024)
        put("ffn_conv_b", cw_ref, 3, 0, 2 * D_FF)
        r8 = lax.broadcasted_iota(jnp.int32, (8, 128), 0)
        l8 = _lane((8, 128))
        bs = jnp.zeros((8, BLK), F32)
        for j in range(4):
            sel = (((r8 == 2 * j) & (l8 < 64)) | ((r8 == 2 * j + 1) & (l8 >= 64))).astype(F32).astype(BF16)
            xj = dbl_ref[j]
            hi = xj.astype(BF16)
            lo = (xj - hi.astype(F32)).astype(BF16)
            bs = bs + lax.dot_general(sel, hi, NT, preferred_element_type=F32) + lax.dot_general(sel, lo, NT, preferred_element_type=F32)
        o_ref[SMALL_ROW["gmlp_bs"]:SMALL_ROW["gmlp_bs"] + 8, :] = bs
        causal = lax.broadcasted_iota(jnp.int32, (BLK, BLK), 0) >= lax.broadcasted_iota(jnp.int32, (BLK, BLK), 1)
        for h in range(8):
            r0 = SMALL_ROW["gmlp_ws"] + h * BLK
            o_ref[r0:r0 + BLK, :] = jnp.where(causal, dws_ref[h], 0.0)
        for q in range(N_CHIPS):
            for j in range(3):
                for k in range(CONV_LANE_ROWS):
                    r0 = SMALL_ROW["ffn_conv"] + q * CONV_CHIP_ROWS + j * CONV_LANE_ROWS + k
                    l0 = (q * CONV_LANE_ROWS + k) * 128
                    o_ref[r0:r0 + 1, :] = cw_ref[j:j + 1, l0:l0 + 128]

    args = (dg_mix, dgq, dgk, dsk, dg_gvn, dg_y, dg_xa, dg_mem, dg_xq, dg_xk, dg_ffn, gcw, dbl, dws)
    full = lambda a: BS(a.shape, lambda i, nd=a.ndim: (0,) * nd)
    return _pcall(body, name="pack_small", grid=(1,), in_specs=[full(a) for a in args],
                  out_specs=BS((SMALL_ROWS, 128), lambda i: (0, 0)), out_shape=SDS((SMALL_ROWS, 128), F32))(*args)


def _adam(w, g, m, v):
    mn = ADAM_B1 * m + (1.0 - ADAM_B1) * g
    vn = ADAM_B2 * v + (1.0 - ADAM_B2) * (g * g)
    m_hat = mn / (1.0 - ADAM_B1 ** ADAM_STEP)
    v_hat = vn / (1.0 - ADAM_B2 ** ADAM_STEP)
    return -ADAM_LR * (m_hat / (jnp.sqrt(v_hat) + ADAM_EPS) + ADAM_WD * w), mn, vn


def adamw_small(gsum, w, m, v, chipvec):
    n = len(SMALL)

    def body(chip_ref, g_ref, *refs):
        w_refs, m_refs, v_refs = refs[:n], refs[n:2 * n], refs[2 * n:3 * n]
        outs = refs[3 * n:]
        go, do, mo, vo = outs[:n], outs[n:2 * n], outs[2 * n:3 * n], outs[3 * n:]

        def update(i, idx, g):
            d, mn, vn = _adam(w_refs[i][idx], g, m_refs[i][idx], v_refs[i][idx])
            go[i][idx] = g
            do[i][idx] = d
            mo[i][idx] = mn
            vo[i][idx] = vn

        for i, (name, length) in enumerate(SMALL_VECS):
            for k in range(-(-length // 128)):
                wd = min(128, length - k * 128)
                r = SMALL_ROW[name] + k
                update(i, (slice(0, 1), slice(k * 128, k * 128 + wd)), g_ref[r:r + 1, 0:wd])
        i_bs, i_ws, i_cv = len(SMALL_VECS), len(SMALL_VECS) + 1, len(SMALL_VECS) + 2
        update(i_bs, (0,), g_ref[SMALL_ROW["gmlp_bs"]:SMALL_ROW["gmlp_bs"] + 8, :])
        for h in range(8):
            r0 = SMALL_ROW["gmlp_ws"] + h * BLK
            update(i_ws, (0, h), g_ref[r0:r0 + BLK, :])
        mine = g_ref[pl.ds(pl.multiple_of(SMALL_ROW["ffn_conv"] + chip_ref[0] * CONV_CHIP_ROWS, 8), CONV_CHIP_ROWS), :]
        for j in range(3):
            for k in range(CONV_LANE_ROWS):
                r = j * CONV_LANE_ROWS + k
                update(i_cv, (0, slice(j, j + 1), slice(k * 128, (k + 1) * 128)), mine[r:r + 1, :])

    nat = [w[nm] for nm in SMALL]
    full = lambda a: BS(a.shape, lambda i, c, nd=a.ndim: (0,) * nd)
    outs = _pcall(body, name="adamw_small", grid=(1,), prefetch=1,
                  in_specs=[BS((SMALL_ROWS, 128), lambda i, c: (0, 0))] + [full(a) for a in nat] * 3,
                  out_specs=[full(a) for a in nat] * 4, out_shape=[SDS(a.shape, F32) for a in nat] * 4)(
        chipvec, gsum, *nat, *[m[nm] for nm in SMALL], *[v[nm] for nm in SMALL])
    return outs[:n], outs[n:2 * n], outs[2 * n:3 * n], outs[3 * n:]


def adamw_matrix(w, m, v, g_own, g_other, cvec, *, name):
    _, r, c = w.shape
    half = r // 2
    tr = _tile(half, (256, 176, 128))
    T = half // tr

    def body(c_ref, w_ref, m_ref, v_ref, own_ref, oth_ref, g_ref, d_ref, mo_ref, vo_ref):
        g = jnp.where(pl.program_id(0) == c_ref[0], own_ref[...], oth_ref[...])
        d, mn, vn = _adam(w_ref[...], g, m_ref[...], v_ref[...])
        g_ref[...] = g
        d_ref[...] = d
        mo_ref[...] = mn
        vo_ref[...] = vn

    nat = BS((None, tr, c), lambda hf, t, cr: (0, hf * T + t, 0))
    hlf = BS((tr, c), lambda hf, t, cr: (t, 0))
    return _pcall(body, name=name, grid=(2, T), prefetch=1, in_specs=[nat, nat, nat, hlf, hlf], out_specs=[nat] * 4,
                  out_shape=[SDS(w.shape, F32)] * 4)(cvec, w, m, v, g_own, g_other)


def _place():
    return lax.axis_index("x"), lax.axis_index("y"), lax.axis_index("c")


def _other_chips(x, y):
    return [(1 - x, y), (x, 1 - y), (1 - x, 1 - y)]


def _rows_of_core(c, half):
    return pl.ds(pl.multiple_of(c * half, 16), half)


def _rcopy(src, dst, sems, k, to):
    return pltpu.make_async_remote_copy(src_ref=src, dst_ref=dst, send_sem=sems[0].at[k], recv_sem=sems[1].at[k],
                                        device_id=to, device_id_type=MESH)


def cast_shards(shards, conv, chipvec):
    n = len(shards)

    def body(chip_ref, *refs):
        for i_ref, o_ref in zip(refs[:n + 1], refs[n + 1:]):
            o_ref[...] = i_ref[...].astype(o_ref.dtype)

    in_specs = [BS((s.shape[0] // 4, s.shape[1]), lambda i, p: (i, 0)) for s in shards]
    in_specs.append(BS(conv.shape, lambda i, p: (0, 0)))
    out_specs = [BS((None, s.shape[0] // 4, s.shape[1]), lambda i, p: (p[0], i, 0)) for s in shards]
    out_specs.append(BS((None,) + conv.shape, lambda i, p: (p[0], 0, 0)))
    out_shape = [SDS((N_CHIPS,) + s.shape, MXU_DTYPE) for s in shards] + [SDS((N_CHIPS,) + conv.shape, F32)]
    return _pcall(body, name="cast_shards", grid=(4,), prefetch=1, in_specs=in_specs, out_specs=out_specs,
                  out_shape=out_shape)(chipvec, *shards, conv)


HBM = pl.BlockSpec(memory_space=pltpu.HBM)
SEM = pl.BlockSpec(memory_space=pltpu.SEMAPHORE)
DATAFLOW = pltpu.SideEffectType.DATAFLOW_SIDE_EFFECTING
VMEM_WHOLE = pl.BlockSpec(memory_space=pltpu.VMEM)
TOKEN = jax.ShapeDtypeStruct((8, 128), jnp.float32)


def _gather_copies(bufs, send_sems, recv_sems, outgoing):
    x, y, c = _place()
    p = 2 * x + y
    cps = []
    for i, o in enumerate(bufs):
        for j, (cx, cy) in enumerate(_other_chips(x, y)):
            slot = o.at[p] if outgoing else o.at[2 * cx + cy]
            cps.append(_rcopy(slot, slot, (send_sems, recv_sems), 3 * i + j, (cx, cy, c)))
    return cps


def gather_start(slots, after):
    n = len(slots)

    def body(*refs):
        send_sems, recv_sems, thru, token = refs[n + 1], refs[n + 2], refs[n + 3:2 * n + 3], refs[2 * n + 3]
        for cp in _gather_copies(thru, send_sems, recv_sems, True):
            cp.start()
        token[...] = jnp.zeros_like(token)

    hbm = [pltpu.with_memory_space_constraint(s, pltpu.HBM) for s in slots]
    outs = pl.pallas_call(
        body, name="gather_start_%d" % n,
        out_shape=[pltpu.SemaphoreType.DMA((3 * n,)), pltpu.SemaphoreType.DMA((3 * n,))]
        + [pltpu.HBM(s.shape, s.dtype) for s in slots] + [TOKEN],
        in_specs=[HBM] * n + [ANY], out_specs=[SEM, SEM] + [HBM] * n + [VMEM_WHOLE],
        input_output_aliases={i: 2 + i for i in range(n)},
        compiler_params=pltpu.CompilerParams(has_side_effects=DATAFLOW))(*hbm, after)
    return outs[0], outs[1], outs[2:2 + n], outs[2 + n]


def gather_wait(send_sems, recv_sems, bufs, *after):
    n = len(bufs)

    def body(*refs):
        ins, send_ref, recv_ref = refs[:n], refs[n], refs[n + 1]
        for cp in _gather_copies(ins, send_ref, recv_ref, False):
            cp.wait_send()
            cp.wait_recv()

    return pl.pallas_call(
        body, name="gather_wait_%d" % n, out_shape=[pltpu.HBM(s.shape, s.dtype) for s in bufs],
        in_specs=[HBM] * n + [SEM, SEM] + [ANY] * len(after), out_specs=[HBM] * n,
        input_output_aliases={i: i for i in range(n)},
        compiler_params=pltpu.CompilerParams(has_side_effects=DATAFLOW))(*bufs, send_sems, recv_sems, *after)


def _peers(x, y, c):
    return [(1 - x if k & 4 else x, 1 - y if k & 2 else y, 1 - c if k & 1 else c) for k in range(1, N_DEV)]


def _partial_copies(g_ref, land_ref, send_sems, recv_sems, outgoing):
    x, y, c = _place()
    half = g_ref.shape[1] // 2
    cps = []
    for k, (px, py, pc) in enumerate(_peers(x, y, c)):
        src = g_ref.at[2 * px + py, _rows_of_core(pc, half)]
        dst = land_ref.at[4 * x + 2 * y + c] if outgoing else land_ref.at[4 * px + 2 * py + pc]
        cps.append(_rcopy(src, dst, (send_sems, recv_sems), k, (px, py, pc)))
    return cps


def partials_start(g, *, name):
    land = lax.empty((N_DEV, g.shape[1] // 2, g.shape[2]), g.dtype)

    def body(g_ref, land_ref, send_sems, recv_sems, g_thru, land_thru, token):
        for cp in _partial_copies(g_thru, land_thru, send_sems, recv_sems, True):
            cp.start()
        token[...] = jnp.zeros_like(token)

    return pl.pallas_call(
        body, name=name,
        out_shape=[pltpu.SemaphoreType.DMA((N_DEV - 1,)), pltpu.SemaphoreType.DMA((N_DEV - 1,)),
                   pltpu.HBM(g.shape, g.dtype), pltpu.HBM(land.shape, land.dtype), TOKEN],
        in_specs=[HBM, HBM], out_specs=[SEM, SEM, HBM, HBM, VMEM_WHOLE], input_output_aliases={0: 2, 1: 3},
        compiler_params=pltpu.CompilerParams(has_side_effects=DATAFLOW))(
        pltpu.with_memory_space_constraint(g, pltpu.HBM), pltpu.with_memory_space_constraint(land, pltpu.HBM))


def partials_wait(started, after):
    n = len(started)

    def body(*refs):
        for i in range(n):
            send_ref, recv_ref, g_ref, land_ref = refs[4 * i:4 * i + 4]
            for cp in _partial_copies(g_ref, land_ref, send_ref, recv_ref, False):
                cp.wait_send()
                cp.wait_recv()

    flat = [a for s in started for a in s]
    bufs = [a for s in started for a in s[2:]]
    outs = pl.pallas_call(
        body, name="partials_wait", out_shape=[pltpu.HBM(b.shape, b.dtype) for b in bufs],
        in_specs=[SEM, SEM, HBM, HBM] * n + [ANY], out_specs=[HBM] * (2 * n),
        input_output_aliases={4 * i + 2 + j: 2 * i + j for i in range(n) for j in range(2)},
        compiler_params=pltpu.CompilerParams(has_side_effects=DATAFLOW))(*flat, after)
    return [(outs[2 * i], outs[2 * i + 1]) for i in range(n)]


def sum_partials(pairs, order):
    n = len(pairs)

    def body(o_ref, *refs):
        j = pl.program_id(0)
        for g_ref, l_ref, f_ref in zip(refs[:n], refs[n:2 * n], refs[2 * n:]):
            @pl.when(j == 0)
            def _():
                f_ref[...] = g_ref[...].astype(F32)

            @pl.when(j > 0)
            def _():
                f_ref[...] += l_ref[...].astype(F32)

    g4 = [g.reshape(g.shape[0], 2, g.shape[1] // 2, g.shape[2]) for g, _ in pairs]
    lands = [l for _, l in pairs]
    return _pcall(body, name="sum_partials", grid=(N_DEV,), prefetch=1,
                  in_specs=[BS((None, None) + g.shape[2:], lambda j, o: (o[0], o[1], 0, 0)) for g in g4]
                  + [BS((None,) + l.shape[1:], lambda j, o: (o[jnp.maximum(j, 1) + 1], 0, 0)) for l in lands],
                  out_specs=[BS(l.shape[1:], lambda j, o: (0, 0)) for l in lands],
                  out_shape=[SDS(l.shape[1:], F32) for l in lands])(order, *g4, *lands)


def _pair_copies(f_refs, land_refs, send_sems, recv_sems):
    x, y, c = _place()
    return [_rcopy(f, o, (send_sems, recv_sems), i, (x, y, 1 - c)) for i, (f, o) in enumerate(zip(f_refs, land_refs))]


def pair_start(fs):
    n = len(fs)
    lands = [lax.empty(f.shape, f.dtype) for f in fs]

    def body(*refs):
        send_sems, recv_sems = refs[2 * n], refs[2 * n + 1]
        thru, land_thru, token = refs[2 * n + 2:3 * n + 2], refs[3 * n + 2:4 * n + 2], refs[4 * n + 2]
        for cp in _pair_copies(thru, land_thru, send_sems, recv_sems):
            cp.start()
        token[...] = jnp.zeros_like(token)

    hbm = [pltpu.with_memory_space_constraint(a, pltpu.HBM) for a in list(fs) + lands]
    outs = pl.pallas_call(
        body, name="pair_start",
        out_shape=[pltpu.SemaphoreType.DMA((n,)), pltpu.SemaphoreType.DMA((n,))]
        + [pltpu.HBM(a.shape, a.dtype) for a in list(fs) + lands] + [TOKEN],
        in_specs=[HBM] * (2 * n), out_specs=[SEM, SEM] + [HBM] * (2 * n) + [VMEM_WHOLE],
        input_output_aliases={i: 2 + i for i in range(2 * n)},
        compiler_params=pltpu.CompilerParams(has_side_effects=DATAFLOW))(*hbm)
    return outs[0], outs[1], outs[2:2 + n], outs[2 + n:2 + 2 * n], outs[2 + 2 * n]


def pair_wait(send_sems, recv_sems, fs, lands, after):
    n = len(fs)

    def body(*refs):
        for cp in _pair_copies(refs[:n], refs[n:2 * n], refs[2 * n], refs[2 * n + 1]):
            cp.wait_send()
            cp.wait_recv()

    outs = pl.pallas_call(
        body, name="pair_wait", out_shape=[pltpu.HBM(a.shape, a.dtype) for a in list(fs) + list(lands)],
        in_specs=[HBM] * (2 * n) + [SEM, SEM, ANY], out_specs=[HBM] * (2 * n),
        input_output_aliases={i: i for i in range(2 * n)},
        compiler_params=pltpu.CompilerParams(has_side_effects=DATAFLOW))(*fs, *lands, send_sems, recv_sems, after)
    return outs[:n], outs[n:]


def _small_copies(s_ref, land_ref, send_sems, recv_sems, outgoing):
    x, y, c = _place()
    cps = []
    for k, (px, py, pc) in enumerate(_peers(x, y, c)):
        dst = land_ref.at[4 * x + 2 * y + c] if outgoing else land_ref.at[4 * px + 2 * py + pc]
        cps.append(_rcopy(s_ref, dst, (send_sems, recv_sems), k, (px, py, pc)))
    return cps


def small_start(sm):
    land = lax.empty((N_DEV,) + sm.shape, sm.dtype)

    def body(s_ref, land_ref, send_sems, recv_sems, s_thru, land_thru):
        for cp in _small_copies(s_thru, land_thru, send_sems, recv_sems, True):
            cp.start()

    return pl.pallas_call(
        body, name="small_start",
        out_shape=[pltpu.SemaphoreType.DMA((N_DEV - 1,)), pltpu.SemaphoreType.DMA((N_DEV - 1,)),
                   pltpu.HBM(sm.shape, sm.dtype), pltpu.HBM(land.shape, land.dtype)],
        in_specs=[HBM, HBM], out_specs=[SEM, SEM, HBM, HBM], input_output_aliases={0: 2, 1: 3},
        compiler_params=pltpu.CompilerParams(has_side_effects=DATAFLOW))(
        pltpu.with_memory_space_constraint(sm, pltpu.HBM), pltpu.with_memory_space_constraint(land, pltpu.HBM))


def small_wait(send_sems, recv_sems, sm, land, *after):
    def body(send_ref, recv_ref, s_ref, land_ref, *rest):
        for cp in _small_copies(s_ref, land_ref, send_ref, recv_ref, False):
            cp.wait_send()
            cp.wait_recv()

    return pl.pallas_call(
        body, name="small_wait", out_shape=[pltpu.HBM(sm.shape, sm.dtype), pltpu.HBM(land.shape, land.dtype)],
        in_specs=[SEM, SEM, HBM, HBM] + [ANY] * len(after), out_specs=[HBM, HBM], input_output_aliases={2: 0, 3: 1},
        compiler_params=pltpu.CompilerParams(has_side_effects=DATAFLOW))(send_sems, recv_sems, sm, land, *after)


def sum_small(own, land, mevec):
    n, rows, width = land.shape
    tr = _tile(rows, (184, 8))

    def body(me_ref, own_ref, land_ref, o_ref):
        acc = jnp.zeros((tr, width), F32)
        for s in range(n):
            acc = acc + jnp.where(me_ref[0] == s, own_ref[...], land_ref[s])
        o_ref[...] = acc

    return _pcall(body, name="sum_small", grid=(rows // tr,), prefetch=1,
                  in_specs=[BS((tr, width), lambda i, me: (i, 0)), BS((n, tr, width), lambda i, me: (0, i, 0))],
                  out_specs=BS((tr, width), lambda i, me: (i, 0)), out_shape=SDS((rows, width), F32))(mevec, own, land)


def _to_full(blk, col):
    n, r, c = blk.shape
    return blk.transpose(1, 0, 2).reshape(r, n * c) if col else blk.reshape(n * r, c)


def _dup_cols(w):
    dup = lambda t: jnp.concatenate([t[:, :64], t[:, :64], t[:, 64:], t[:, 64:]], axis=1)
    return jnp.concatenate([w[:, :512], dup(w[:, 512:640]), dup(w[:, 640:768]), w[:, 768:]], axis=1)


def _fold_cols(d):
    fold = lambda t: jnp.concatenate([t[:, 0:64] + t[:, 64:128], t[:, 128:192] + t[:, 192:256]], axis=1)
    return jnp.concatenate([d[:, :512], fold(d[:, 512:768]), fold(d[:, 768:1024]), d[:, 1024:]], axis=1)


def _local_step(x, mem, positions, target, w_in, later, sp, emit):
    gain = lambda n: sp[n].reshape(1, -1)
    half = HEAD_DIM // 2
    inv_freq = 1.0 / (10000.0 ** (jnp.arange(half, dtype=F32) * (2.0 / HEAD_DIM)))
    ang = positions.astype(F32)[:, None] * inv_freq
    cos, sin = jnp.cos(ang), jnp.sin(ang)
    cos128 = jnp.tile(cos, (1, 4))
    sin128 = jnp.concatenate([-sin, sin, -sin, sin], axis=1)
    seg = jnp.arange(128) // HEAD_DIM
    bmat = (seg[:, None] == seg[None, :]).astype(BF16)
    gq128, gk128 = jnp.tile(gain("q_norm"), (1, 2)), jnp.tile(gain("k_norm"), (1, 2))
    sinkcol = jnp.repeat(sp["attn_sinks"].reshape(4, 2), BLK, axis=1).reshape(4, 2 * BLK, 1)
    wsc = sp["gmlp_ws"] * jnp.tril(jnp.ones((BLK, BLK), F32))[None]
    w2 = wsc.reshape(4, 2 * BLK, BLK).astype(MXU_DTYPE)
    w2t = wsc.swapaxes(1, 2).reshape(4, 2 * BLK, BLK).astype(MXU_DTYPE)
    bsl = jnp.repeat(sp["gmlp_bs"].reshape(4, 2, BLK).transpose(0, 2, 1), HEAD_DIM, axis=2)
    cb = sp["ffn_conv_b"].reshape(1, -1)
    w_in_d = _dup_cols(_to_full(w_in(cos128, sin128, gq128, gk128, sinkcol, w2, w2t, bsl), True))[None]

    h1, proj = rms_mm(x, gain("mix_norm"), w_in_d, name="mix_in")
    qr, kr, vb, gu, gvn, attn, gm, y = mixer_core_fwd(proj, cos128, sin128, gq128, gk128, gain("gmlp_v_norm"), bmat,
                                                      sinkcol, gain("attn_out_norm"), w2, bsl, gain("gmlp_out_norm"))
    wf, last = later(y)
    w_out, xa_wq, xa_wo = (_to_full(wf[n], False) for n in ("w_out", "xa_wq", "xa_wo"))
    mn, kv = rms_mm(mem, gain("mem_norm"), wf["xa_wkv"], name="xa_kv")
    kn, vbx = mem_pre(kv, gain("xa_k_norm"))
    x1, h2, qx, xo, x2 = xattn_block_fwd(y, w_out, x, gain("xa_norm"), xa_wq, kn, vbx, gain("xa_q_norm"), xa_wo)
    ffn_w, cw = last(x2)
    wf = {**wf, **ffn_w}
    ffn_down = _to_full(wf["ffn_down"], False)
    h3, a, f, dx3, loss_acc = ffn_fwd_loss(x2, gain("ffn_norm"), wf["ffn_up"], cw, cb, ffn_down, target)

    by_rows = lambda g: g.reshape(N_CHIPS, g.shape[1] // N_CHIPS, g.shape[2])
    sent = emit("ffn_down", by_rows(mm_tn(f, dx3, name="g_ffn_down", out_dtype=WIRE_DTYPE)))
    dc, gcw = convgate_bwd(a, dx3, ffn_down[None], cw, cb, after=sent)
    da, dx2, dg_ffn = conv_transpose_rms_bwd(dc, cw, wf["ffn_up"], x2, gain("ffn_norm"), dx3)
    sent = emit("ffn_up", mm_tn(h3, da, name="g_ffn_up", out_dtype=WIRE_DTYPE, chunks=N_CHIPS))
    sent = emit("xa_wo", by_rows(mm_tn(xo, dx2, name="g_xa_wo", out_dtype=WIRE_DTYPE, after=sent)))
    dqx, dx1, dkn, dvx, dg_xq, dg_xa, dattn, dgm, dg_y = xattn_block_bwd(
        dx2, xa_wo[None], qx, kn, vbx, gain("xa_q_norm"), xa_wq[None], x1, gain("xa_norm"), w_out[None], attn, gm,
        gain("attn_out_norm"), gain("gmlp_out_norm"), after=sent)
    sent = emit("xa_wq", by_rows(mm_tn(h2, dqx, name="g_xa_wq", out_dtype=WIRE_DTYPE)))
    dkv, dg_xk = mem_bwd(kv, dkn, dvx, gain("xa_k_norm"), after=sent)
    _, dg_mem = mm_nt_rms_bwd(dkv, wf["xa_wkv"], mem, gain("mem_norm"), jnp.zeros_like(mem), name="d_mem")
    sent = emit("xa_wkv", mm_tn(mn, dkv, name="g_xa_wkv", out_dtype=WIRE_DTYPE, chunks=N_CHIPS))
    sent = emit("w_out", by_rows(mm_tn(y, dx1, name="g_w_out", out_dtype=WIRE_DTYPE, after=sent)))
    dproj, dsk, dws, dbl, dgq, dgk, dg_gvn = mixer_core_bwd(
        proj, cos128, sin128, gq128, gk128, gain("gmlp_v_norm"), bmat, qr, kr, vb, sinkcol, dattn, dgm, gvn, gu,
        w2, w2t, bsl, after=sent)
    g_in = _fold_cols(mm_tn(h1, dproj, name="g_w_in", out_dtype=F32)[0])
    sent = emit("w_in", g_in.reshape(1024, N_CHIPS, 448).transpose(1, 0, 2).astype(WIRE_DTYPE))
    grad_x, dg_mix = mm_nt_rms_bwd(dproj, w_in_d, x, gain("mix_norm"), dx1, name="d_x", tm=1024, after=sent)
    packed = pack_small(dg_mix, dgq, dgk, dsk, dg_gvn, dg_y, dg_xa, dg_mem, dg_xq, dg_xk, dg_ffn, gcw, dbl, dws)
    return loss_acc, grad_x, packed


def _gather_step(w, chipvec):
    slots = cast_shards([w[n][0] for n in BIG_NAMES], w["ffn_conv"][0], chipvec)
    send_a, recv_a, first, token = gather_start(slots[:1], chipvec)
    send_b, recv_b, mid, token = gather_start(slots[1:5], token)
    send_c, recv_c, rest, token = gather_start(slots[5:], token)

    def w_in(*after):
        return gather_wait(send_a, recv_a, first, token, *after)[0]

    def last(after):
        got = gather_wait(send_c, recv_c, rest, after)
        return dict(zip(BIG_NAMES[5:], got[:-1])), _to_full(got[-1], True)

    def later(after):
        return dict(zip(BIG_NAMES[1:5], gather_wait(send_b, recv_b, mid, after))), last

    return w_in, later, token


def _reduce_update(started, packed, w, m, v, chipvec, cvec, order):
    small_sent = small_start(packed)
    own = sum_partials(partials_wait([started[n] for n in BIG_NAMES], small_sent[2]), order)
    pair_send, pair_recv, own, lands, pair_started = pair_start(own)
    own, other = pair_wait(pair_send, pair_recv, own, lands, pair_started)
    res = [{}, {}, {}, {}]
    for n, g_own, g_other in zip(BIG_NAMES, own, other):
        for d, o in zip(res, adamw_matrix(w[n], m[n], v[n], g_own, g_other, cvec, name="adamw_" + n)):
            d[n] = o
    mevec = (2 * order[0:1] + order[1:2]).astype(jnp.int32)
    small_sum = sum_small(*small_wait(*small_sent, *[res[3][n] for n in BIG_NAMES]), mevec)
    for d, outs in zip(res, adamw_small(small_sum, w, m, v, chipvec)):
        d.update(zip(SMALL, outs))
    return res


def kernel(x, mem, positions, mix_norm, w_in, q_norm, k_norm, attn_sinks, gmlp_v_norm, gmlp_ws, gmlp_bs, attn_out_norm, gmlp_out_norm, w_out, xa_norm, mem_norm, xa_wq, xa_wkv, xa_q_norm, xa_k_norm, xa_wo, ffn_norm, ffn_up, ffn_conv, ffn_conv_b, ffn_down, loss_target, m_mix_norm, m_w_in, m_q_norm, m_k_norm, m_attn_sinks, m_gmlp_v_norm, m_gmlp_ws, m_gmlp_bs, m_attn_out_norm, m_gmlp_out_norm, m_w_out, m_xa_norm, m_mem_norm, m_xa_wq, m_xa_wkv, m_xa_q_norm, m_xa_k_norm, m_xa_wo, m_ffn_norm, m_ffn_up, m_ffn_conv, m_ffn_conv_b, m_ffn_down, v_mix_norm, v_w_in, v_q_norm, v_k_norm, v_attn_sinks, v_gmlp_v_norm, v_gmlp_ws, v_gmlp_bs, v_attn_out_norm, v_gmlp_out_norm, v_w_out, v_xa_norm, v_mem_norm, v_xa_wq, v_xa_wkv, v_xa_q_norm, v_xa_k_norm, v_xa_wo, v_ffn_norm, v_ffn_up, v_ffn_conv, v_ffn_conv_b, v_ffn_down):
    w = dict(mix_norm=mix_norm, w_in=w_in, q_norm=q_norm, k_norm=k_norm, attn_sinks=attn_sinks, gmlp_v_norm=gmlp_v_norm, gmlp_ws=gmlp_ws, gmlp_bs=gmlp_bs, attn_out_norm=attn_out_norm, gmlp_out_norm=gmlp_out_norm, w_out=w_out, xa_norm=xa_norm, mem_norm=mem_norm, xa_wq=xa_wq, xa_wkv=xa_wkv, xa_q_norm=xa_q_norm, xa_k_norm=xa_k_norm, xa_wo=xa_wo, ffn_norm=ffn_norm, ffn_up=ffn_up, ffn_conv=ffn_conv, ffn_conv_b=ffn_conv_b, ffn_down=ffn_down)
    m = dict(mix_norm=m_mix_norm, w_in=m_w_in, q_norm=m_q_norm, k_norm=m_k_norm, attn_sinks=m_attn_sinks, gmlp_v_norm=m_gmlp_v_norm, gmlp_ws=m_gmlp_ws, gmlp_bs=m_gmlp_bs, attn_out_norm=m_attn_out_norm, gmlp_out_norm=m_gmlp_out_norm, w_out=m_w_out, xa_norm=m_xa_norm, mem_norm=m_mem_norm, xa_wq=m_xa_wq, xa_wkv=m_xa_wkv, xa_q_norm=m_xa_q_norm, xa_k_norm=m_xa_k_norm, xa_wo=m_xa_wo, ffn_norm=m_ffn_norm, ffn_up=m_ffn_up, ffn_conv=m_ffn_conv, ffn_conv_b=m_ffn_conv_b, ffn_down=m_ffn_down)
    v = dict(mix_norm=v_mix_norm, w_in=v_w_in, q_norm=v_q_norm, k_norm=v_k_norm, attn_sinks=v_attn_sinks, gmlp_v_norm=v_gmlp_v_norm, gmlp_ws=v_gmlp_ws, gmlp_bs=v_gmlp_bs, attn_out_norm=v_attn_out_norm, gmlp_out_norm=v_gmlp_out_norm, w_out=v_w_out, xa_norm=v_xa_norm, mem_norm=v_mem_norm, xa_wq=v_xa_wq, xa_wkv=v_xa_wkv, xa_q_norm=v_xa_q_norm, xa_k_norm=v_xa_k_norm, xa_wo=v_xa_wo, ffn_norm=v_ffn_norm, ffn_up=v_ffn_up, ffn_conv=v_ffn_conv, ffn_conv_b=v_ffn_conv_b, ffn_down=v_ffn_down)
    ix, iy, ic = lax.axis_index("x"), lax.axis_index("y"), lax.axis_index("c")
    chip = 2 * ix + iy
    chipvec = chip.astype(jnp.int32).reshape(1)
    cvec = ic.astype(jnp.int32).reshape(1)
    order = jnp.stack([chip, ic] + [4 * px + 2 * py + pc for px, py, pc in _peers(ix, iy, ic)]).astype(jnp.int32)

    w_in_all, later, token = _gather_step(w, chipvec)
    zero = token[0, 0]
    sp = {n: w[n][0] + zero for n in SMALL if n != "ffn_conv"}
    positions = positions + zero.astype(jnp.int32)
    started = {}

    def emit(name, g):
        *started[name], token = partials_start(g, name="partials_start_" + name)
        return token

    loss_acc, grad_x, packed = _local_step(x[0], mem[0], positions[0], loss_target[0], w_in_all, later, sp, emit)
    grads, delta, new_m, new_v = _reduce_update(started, packed, w, m, v, chipvec, cvec, order)
    loss = lax.psum(loss_acc[0, 0], ("x", "y", "c"))
    ordered = lambda d: [d[n] for n in WEIGHTS]
    return (loss, grad_x[None], *ordered(grads), *ordered(delta), *ordered(new_m), *ordered(new_v))
```
